```python
import math
import jax, jax.numpy as jnp
from jax import lax
import numpy as np

D_MODEL = 1024
BATCH = 8
SEQ = 2048
DEPTH = 1

N_META = 16
GRID_W = 64
NA_WIDTH = D_MODEL // 2
S5_WIDTH = D_MODEL - NA_WIDTH
MIX_WIDTH = NA_WIDTH + S5_WIDTH
NA_HEAD_DIM = 64
NA_HEADS = NA_WIDTH // NA_HEAD_DIM
NA_KH_MAX = 8
NA_KW = 16
S5_GROUP = 16
S5_GROUPS = S5_WIDTH // S5_GROUP
S5_STATE = 64
D_FF = ((8 * D_MODEL // 3 + 127) // 128) * 128
RMS_EPS = 1e-6
DT_MIN = 1e-3
DT_MAX = 1e-1
NEG_INF = -1e30

kernel_name = "hybrid_natten_s5_macaron_block"


def rms_norm(x, g):
    xf = x.astype(jnp.float32)
    y = xf * lax.rsqrt(jnp.mean(xf * xf, axis=-1, keepdims=True) + RMS_EPS)
    return (y * g.astype(jnp.float32)).astype(x.dtype)


def swiglu(x, w_gate, w_up, w_down):
    return (jax.nn.silu(x @ w_gate) * (x @ w_up)) @ w_down


def _ssm_combine(e1, e2):
    a1, b1 = e1
    a2, b2 = e2
    return a1 * a2, a2 * b1 + b2


def s5_mixer(u, lam_re, lam_im, log_dt, b_re, b_im, c_re, c_im, d_skip, w_glu, b_glu):
    f32 = jnp.float32
    bsz, length, _ = u.shape
    uf = u.astype(f32).reshape(bsz, length, S5_GROUPS, S5_GROUP)
    uc = uf.astype(jnp.complex64)
    y = uf * d_skip.astype(f32).reshape(S5_GROUPS, S5_GROUP)
    for direction in range(2):
        lam = lax.complex(lam_re[direction].astype(f32), lam_im[direction].astype(f32))
        dt = jnp.exp(log_dt[direction].astype(f32))[:, None]
        lam_bar = jnp.exp(lam * dt)
        b = lax.complex(b_re[direction].astype(f32), b_im[direction].astype(f32))
        b_bar = ((lam_bar - 1.0) / lam)[..., None] * b
        bu = jnp.einsum('blgh,gph->blgp', uc, b_bar)
        a = jnp.broadcast_to(lam_bar, bu.shape)
        _, states = lax.associative_scan(_ssm_combine, (a, bu), axis=1, reverse=(direction == 1))
        y = y + jnp.einsum('blgp,ghp->blgh', jnp.real(states), c_re[direction].astype(f32)) \
              - jnp.einsum('blgp,ghp->blgh', jnp.imag(states), c_im[direction].astype(f32))
    y = jax.nn.gelu(y.reshape(bsz, length, S5_WIDTH))
    y = y * jax.nn.sigmoid(y @ w_glu.astype(f32) + b_glu.astype(f32))
    return y.astype(u.dtype)


def neighbourhood_attention(q, k, v, rpb):
    f32 = jnp.float32
    bsz, length = q.shape[0], q.shape[1]
    n_tok = length - N_META
    rows = n_tok // GRID_W
    kh = min(NA_KH_MAX, rows)
    kw = NA_KW
    scale = NA_HEAD_DIM ** -0.5
    qm, qt = q[:, :N_META], q[:, N_META:]
    km, kt = k[:, :N_META], k[:, N_META:]
    vm, vt = v[:, :N_META], v[:, N_META:]

    r = np.arange(rows)
    row_start = np.clip(r - kh // 2, 0, rows - kh)
    row_idx = row_start[:, None] + np.arange(kh)[None, :]
    c = np.arange(GRID_W)
    col_start = np.clip(c - kw // 2, 0, GRID_W - kw)
    col_in = (c[None, :] >= col_start[:, None]) & (c[None, :] < col_start[:, None] + kw)
    dr = row_idx - r[:, None] + NA_KH_MAX - 1
    dc = np.clip(c[None, :] - c[:, None] + kw - 1, 0, 2 * kw - 2)
    bias = rpb.astype(f32)[:, dr[:, None, :, None], dc[None, :, None, :]]
    bias = jnp.where(jnp.asarray(col_in)[None, None, :, None, :], bias, NEG_INF)
    bias = bias.reshape(NA_HEADS, rows, GRID_W, kh * GRID_W)

    qg = qt.reshape(bsz, rows, GRID_W, NA_HEADS, NA_HEAD_DIM)
    kg = kt.reshape(bsz, rows, GRID_W, NA_HEADS, NA_HEAD_DIM)
    vg = vt.reshape(bsz, rows, GRID_W, NA_HEADS, NA_HEAD_DIM)
    kb = jnp.take(kg, row_idx, axis=1).reshape(bsz, rows, kh * GRID_W, NA_HEADS, NA_HEAD_DIM)
    vb = jnp.take(vg, row_idx, axis=1).reshape(bsz, rows, kh * GRID_W, NA_HEADS, NA_HEAD_DIM)

    s_loc = jnp.einsum('brqhd,brkhd->bhrqk', qg, kb, preferred_element_type=f32) * scale + bias
    s_meta = jnp.einsum('brqhd,bmhd->bhrqm', qg, km, preferred_element_type=f32) * scale
    p = jax.nn.softmax(jnp.concatenate([s_loc, s_meta], axis=-1), axis=-1)
    p_loc, p_meta = p[..., :kh * GRID_W], p[..., kh * GRID_W:]
    o_tok = jnp.einsum('bhrqk,brkhd->brqhd', p_loc.astype(v.dtype), vb) \
          + jnp.einsum('bhrqm,bmhd->brqhd', p_meta.astype(v.dtype), vm)
    o_tok = o_tok.reshape(bsz, n_tok, NA_HEADS, NA_HEAD_DIM)

    s_mm = jnp.einsum('bqhd,bmhd->bhqm', qm, km, preferred_element_type=f32) * scale
    o_meta = jnp.einsum('bhqm,bmhd->bqhd', jax.nn.softmax(s_mm, axis=-1).astype(v.dtype), vm)
    return jnp.concatenate([o_meta, o_tok], axis=1).reshape(bsz, length, NA_WIDTH)


def _fwd_setup_inputs(seed: int = 0) -> dict:
    key = jax.random.key(seed)
    ks = iter(jax.random.split(key, 40))
    f32 = jnp.float32

    def nrm(shape, scale):
        return jax.random.normal(next(ks), shape, f32) * scale

    def gain(shape):
        return 1.0 + 0.02 * jax.random.normal(next(ks), shape, f32)

    L = DEPTH
    n_idx = jnp.arange(S5_STATE, dtype=f32)
    inp = {}
    inp['x'] = nrm((BATCH, SEQ, D_MODEL), 1.0)
    inp['meta_tokens'] = nrm((N_META, D_MODEL), 1.0)
    inp['ffn1_pre_g'] = gain((L, D_MODEL))
    inp['ffn1_post_g'] = gain((L, D_MODEL))
    inp['ffn1_w_gate'] = nrm((L, D_MODEL, D_FF), D_MODEL ** -0.5)
    inp['ffn1_w_up'] = nrm((L, D_MODEL, D_FF), D_MODEL ** -0.5)
    inp['ffn1_w_down'] = nrm((L, D_FF, D_MODEL), D_FF ** -0.5)
    inp['mix_pre_g'] = gain((L, D_MODEL))
    inp['w_in'] = nrm((L, D_MODEL, 3 * NA_WIDTH + S5_WIDTH), D_MODEL ** -0.5)
    inp['na_rpb'] = nrm((L, NA_HEADS, 2 * NA_KH_MAX - 1, 2 * NA_KW - 1), 0.05)
    inp['s5_lam_re'] = -0.5 + nrm((L, 2, S5_GROUPS, S5_STATE), 0.01)
    inp['s5_lam_im'] = math.pi * n_idx + nrm((L, 2, S5_GROUPS, S5_STATE), 0.01)
    inp['s5_log_dt'] = jax.random.uniform(next(ks), (L, 2, S5_GROUPS), f32, math.log(DT_MIN), math.log(DT_MAX))
    inp['s5_b_re'] = nrm((L, 2, S5_GROUPS, S5_STATE, S5_GROUP), (0.5 / S5_GROUP) ** 0.5)
    inp['s5_b_im'] = nrm((L, 2, S5_GROUPS, S5_STATE, S5_GROUP), (0.5 / S5_GROUP) ** 0.5)
    inp['s5_c_re'] = nrm((L, 2, S5_GROUPS, S5_GROUP, S5_STATE), (0.5 / S5_STATE) ** 0.5)
    inp['s5_c_im'] = nrm((L, 2, S5_GROUPS, S5_GROUP, S5_STATE), (0.5 / S5_STATE) ** 0.5)
    inp['s5_d'] = nrm((L, S5_WIDTH), 1.0)
    inp['s5_w_glu'] = nrm((L, S5_WIDTH, S5_WIDTH), S5_WIDTH ** -0.5)
    inp['s5_b_glu'] = nrm((L, S5_WIDTH), 0.02)
    inp['na_out_g'] = gain((L, NA_WIDTH))
    inp['s5_out_g'] = gain((L, S5_WIDTH))
    inp['w_out'] = nrm((L, MIX_WIDTH, D_MODEL), MIX_WIDTH ** -0.5)
    inp['mix_post_g'] = gain((L, D_MODEL))
    inp['ffn2_pre_g'] = gain((L, D_MODEL))
    inp['ffn2_post_g'] = gain((L, D_MODEL))
    inp['ffn2_w_gate'] = nrm((L, D_MODEL, D_FF), D_MODEL ** -0.5)
    inp['ffn2_w_up'] = nrm((L, D_MODEL, D_FF), D_MODEL ** -0.5)
    inp['ffn2_w_down'] = nrm((L, D_FF, D_MODEL), D_FF ** -0.5)
    inp['final_g'] = gain((L, D_MODEL))
    return inp


def _fwd_reference(x, meta_tokens, ffn1_pre_g, ffn1_post_g, ffn1_w_gate, ffn1_w_up, ffn1_w_down,
              mix_pre_g, w_in, na_rpb, s5_lam_re, s5_lam_im, s5_log_dt, s5_b_re, s5_b_im,
              s5_c_re, s5_c_im, s5_d, s5_w_glu, s5_b_glu, na_out_g, s5_out_g, w_out, mix_post_g,
              ffn2_pre_g, ffn2_post_g, ffn2_w_gate, ffn2_w_up, ffn2_w_down, final_g):
    bsz = x.shape[0]
    meta = jnp.broadcast_to(meta_tokens.astype(x.dtype)[None], (bsz, N_META, D_MODEL))
    h = jnp.concatenate([meta, x], axis=1)
    length = h.shape[1]
    for i in range(DEPTH):
        f = swiglu(rms_norm(h, ffn1_pre_g[i]), ffn1_w_gate[i], ffn1_w_up[i], ffn1_w_down[i])
        h = h + 0.5 * rms_norm(f, ffn1_post_g[i])
        a = rms_norm(h, mix_pre_g[i])
        proj = a @ w_in[i]
        q = proj[..., :NA_WIDTH].reshape(bsz, length, NA_HEADS, NA_HEAD_DIM)
        k = proj[..., NA_WIDTH:2 * NA_WIDTH].reshape(bsz, length, NA_HEADS, NA_HEAD_DIM)
        v = proj[..., 2 * NA_WIDTH:3 * NA_WIDTH].reshape(bsz, length, NA_HEADS, NA_HEAD_DIM)
        u = proj[..., 3 * NA_WIDTH:]
        o_na = neighbourhood_attention(q, k, v, na_rpb[i])
        o_s5 = s5_mixer(u, s5_lam_re[i], s5_lam_im[i], s5_log_dt[i], s5_b_re[i], s5_b_im[i],
                        s5_c_re[i], s5_c_im[i], s5_d[i], s5_w_glu[i], s5_b_glu[i])
        mix = jnp.concatenate([rms_norm(o_na, na_out_g[i]), rms_norm(o_s5, s5_out_g[i])], axis=-1) @ w_out[i]
        h = h + rms_norm(mix, mix_post_g[i])
        f = swiglu(rms_norm(h, ffn2_pre_g[i]), ffn2_w_gate[i], ffn2_w_up[i], ffn2_w_down[i])
        h = h + 0.5 * rms_norm(f, ffn2_post_g[i])
        h = rms_norm(h, final_g[i])
    return h[:, N_META:]


import jax as _jax
import jax.numpy as _jnp

TWIN_FORMAT = 'train_step'
FWD_PARAMS = ['x', 'meta_tokens', 'ffn1_pre_g', 'ffn1_post_g', 'ffn1_w_gate', 'ffn1_w_up', 'ffn1_w_down', 'mix_pre_g', 'w_in', 'na_rpb', 's5_lam_re', 's5_lam_im', 's5_log_dt', 's5_b_re', 's5_b_im', 's5_c_re', 's5_c_im', 's5_d', 's5_w_glu', 's5_b_glu', 'na_out_g', 's5_out_g', 'w_out', 'mix_post_g', 'ffn2_pre_g', 'ffn2_post_g', 'ffn2_w_gate', 'ffn2_w_up', 'ffn2_w_down', 'final_g']
TWIN_WEIGHTS = ['meta_tokens', 'ffn1_pre_g', 'ffn1_post_g', 'ffn1_w_gate', 'ffn1_w_up', 'ffn1_w_down', 'mix_pre_g', 'w_in', 'na_rpb', 's5_lam_re', 's5_lam_im', 's5_log_dt', 's5_b_re', 's5_b_im', 's5_c_re', 's5_c_im', 's5_d', 's5_w_glu', 's5_b_glu', 'na_out_g', 's5_out_g', 'w_out', 'mix_post_g', 'ffn2_pre_g', 'ffn2_post_g', 'ffn2_w_gate', 'ffn2_w_up', 'ffn2_w_down', 'final_g']
TWIN_DIFF_INPUT = 'x'
TWIN_INPUTS = ['x', 'meta_tokens', 'ffn1_pre_g', 'ffn1_post_g', 'ffn1_w_gate', 'ffn1_w_up', 'ffn1_w_down', 'mix_pre_g', 'w_in', 'na_rpb', 's5_lam_re', 's5_lam_im', 's5_log_dt', 's5_b_re', 's5_b_im', 's5_c_re', 's5_c_im', 's5_d', 's5_w_glu', 's5_b_glu', 'na_out_g', 's5_out_g', 'w_out', 'mix_post_g', 'ffn2_pre_g', 'ffn2_post_g', 'ffn2_w_gate', 'ffn2_w_up', 'ffn2_w_down', 'final_g', 'loss_target', 'm_meta_tokens', 'm_ffn1_pre_g', 'm_ffn1_post_g', 'm_ffn1_w_gate', 'm_ffn1_w_up', 'm_ffn1_w_down', 'm_mix_pre_g', 'm_w_in', 'm_na_rpb', 'm_s5_lam_re', 'm_s5_lam_im', 'm_s5_log_dt', 'm_s5_b_re', 'm_s5_b_im', 'm_s5_c_re', 'm_s5_c_im', 'm_s5_d', 'm_s5_w_glu', 'm_s5_b_glu', 'm_na_out_g', 'm_s5_out_g', 'm_w_out', 'm_mix_post_g', 'm_ffn2_pre_g', 'm_ffn2_post_g', 'm_ffn2_w_gate', 'm_ffn2_w_up', 'm_ffn2_w_down', 'm_final_g', 'v_meta_tokens', 'v_ffn1_pre_g', 'v_ffn1_post_g', 'v_ffn1_w_gate', 'v_ffn1_w_up', 'v_ffn1_w_down', 'v_mix_pre_g', 'v_w_in', 'v_na_rpb', 'v_s5_lam_re', 'v_s5_lam_im', 'v_s5_log_dt', 'v_s5_b_re', 'v_s5_b_im', 'v_s5_c_re', 'v_s5_c_im', 'v_s5_d', 'v_s5_w_glu', 'v_s5_b_glu', 'v_na_out_g', 'v_s5_out_g', 'v_w_out', 'v_mix_post_g', 'v_ffn2_pre_g', 'v_ffn2_post_g', 'v_ffn2_w_gate', 'v_ffn2_w_up', 'v_ffn2_w_down', 'v_final_g']
TWIN_OUTPUTS = ['loss', 'grad_x', 'grad_meta_tokens', 'grad_ffn1_pre_g', 'grad_ffn1_post_g', 'grad_ffn1_w_gate', 'grad_ffn1_w_up', 'grad_ffn1_w_down', 'grad_mix_pre_g', 'grad_w_in', 'grad_na_rpb', 'grad_s5_lam_re', 'grad_s5_lam_im', 'grad_s5_log_dt', 'grad_s5_b_re', 'grad_s5_b_im', 'grad_s5_c_re', 'grad_s5_c_im', 'grad_s5_d', 'grad_s5_w_glu', 'grad_s5_b_glu', 'grad_na_out_g', 'grad_s5_out_g', 'grad_w_out', 'grad_mix_post_g', 'grad_ffn2_pre_g', 'grad_ffn2_post_g', 'grad_ffn2_w_gate', 'grad_ffn2_w_up', 'grad_ffn2_w_down', 'grad_final_g', 'delta_meta_tokens', 'delta_ffn1_pre_g', 'delta_ffn1_post_g', 'delta_ffn1_w_gate', 'delta_ffn1_w_up', 'delta_ffn1_w_down', 'delta_mix_pre_g', 'delta_w_in', 'delta_na_rpb', 'delta_s5_lam_re', 'delta_s5_lam_im', 'delta_s5_log_dt', 'delta_s5_b_re', 'delta_s5_b_im', 'delta_s5_c_re', 'delta_s5_c_im', 'delta_s5_d', 'delta_s5_w_glu', 'delta_s5_b_glu', 'delta_na_out_g', 'delta_s5_out_g', 'delta_w_out', 'delta_mix_post_g', 'delta_ffn2_pre_g', 'delta_ffn2_post_g', 'delta_ffn2_w_gate', 'delta_ffn2_w_up', 'delta_ffn2_w_down', 'delta_final_g', 'new_m_meta_tokens', 'new_m_ffn1_pre_g', 'new_m_ffn1_post_g', 'new_m_ffn1_w_gate', 'new_m_ffn1_w_up', 'new_m_ffn1_w_down', 'new_m_mix_pre_g', 'new_m_w_in', 'new_m_na_rpb', 'new_m_s5_lam_re', 'new_m_s5_lam_im', 'new_m_s5_log_dt', 'new_m_s5_b_re', 'new_m_s5_b_im', 'new_m_s5_c_re', 'new_m_s5_c_im', 'new_m_s5_d', 'new_m_s5_w_glu', 'new_m_s5_b_glu', 'new_m_na_out_g', 'new_m_s5_out_g', 'new_m_w_out', 'new_m_mix_post_g', 'new_m_ffn2_pre_g', 'new_m_ffn2_post_g', 'new_m_ffn2_w_gate', 'new_m_ffn2_w_up', 'new_m_ffn2_w_down', 'new_m_final_g', 'new_v_meta_tokens', 'new_v_ffn1_pre_g', 'new_v_ffn1_post_g', 'new_v_ffn1_w_gate', 'new_v_ffn1_w_up', 'new_v_ffn1_w_down', 'new_v_mix_pre_g', 'new_v_w_in', 'new_v_na_rpb', 'new_v_s5_lam_re', 'new_v_s5_lam_im', 'new_v_s5_log_dt', 'new_v_s5_b_re', 'new_v_s5_b_im', 'new_v_s5_c_re', 'new_v_s5_c_im', 'new_v_s5_d', 'new_v_s5_w_glu', 'new_v_s5_b_glu', 'new_v_na_out_g', 'new_v_s5_out_g', 'new_v_w_out', 'new_v_mix_post_g', 'new_v_ffn2_pre_g', 'new_v_ffn2_post_g', 'new_v_ffn2_w_gate', 'new_v_ffn2_w_up', 'new_v_ffn2_w_down', 'new_v_final_g']
TWIN_LEAF_KINDS = {'loss': 'loss', 'grad_x': 'grad_x', 'grad_meta_tokens': 'grad_w', 'grad_ffn1_pre_g': 'grad_w', 'grad_ffn1_post_g': 'grad_w', 'grad_ffn1_w_gate': 'grad_w', 'grad_ffn1_w_up': 'grad_w', 'grad_ffn1_w_down': 'grad_w', 'grad_mix_pre_g': 'grad_w', 'grad_w_in': 'grad_w', 'grad_na_rpb': 'grad_w', 'grad_s5_lam_re': 'grad_w', 'grad_s5_lam_im': 'grad_w', 'grad_s5_log_dt': 'grad_w', 'grad_s5_b_re': 'grad_w', 'grad_s5_b_im': 'grad_w', 'grad_s5_c_re': 'grad_w', 'grad_s5_c_im': 'grad_w', 'grad_s5_d': 'grad_w', 'grad_s5_w_glu': 'grad_w', 'grad_s5_b_glu': 'grad_w', 'grad_na_out_g': 'grad_w', 'grad_s5_out_g': 'grad_w', 'grad_w_out': 'grad_w', 'grad_mix_post_g': 'grad_w', 'grad_ffn2_pre_g': 'grad_w', 'grad_ffn2_post_g': 'grad_w', 'grad_ffn2_w_gate': 'grad_w', 'grad_ffn2_w_up': 'grad_w', 'grad_ffn2_w_down': 'grad_w', 'grad_final_g': 'grad_w', 'delta_meta_tokens': 'delta_w', 'delta_ffn1_pre_g': 'delta_w', 'delta_ffn1_post_g': 'delta_w', 'delta_ffn1_w_gate': 'delta_w', 'delta_ffn1_w_up': 'delta_w', 'delta_ffn1_w_down': 'delta_w', 'delta_mix_pre_g': 'delta_w', 'delta_w_in': 'delta_w', 'delta_na_rpb': 'delta_w', 'delta_s5_lam_re': 'delta_w', 'delta_s5_lam_im': 'delta_w', 'delta_s5_log_dt': 'delta_w', 'delta_s5_b_re': 'delta_w', 'delta_s5_b_im': 'delta_w', 'delta_s5_c_re': 'delta_w', 'delta_s5_c_im': 'delta_w', 'delta_s5_d': 'delta_w', 'delta_s5_w_glu': 'delta_w', 'delta_s5_b_glu': 'delta_w', 'delta_na_out_g': 'delta_w', 'delta_s5_out_g': 'delta_w', 'delta_w_out': 'delta_w', 'delta_mix_post_g': 'delta_w', 'delta_ffn2_pre_g': 'delta_w', 'delta_ffn2_post_g': 'delta_w', 'delta_ffn2_w_gate': 'delta_w', 'delta_ffn2_w_up': 'delta_w', 'delta_ffn2_w_down': 'delta_w', 'delta_final_g': 'delta_w', 'new_m_meta_tokens': 'new_m', 'new_m_ffn1_pre_g': 'new_m', 'new_m_ffn1_post_g': 'new_m', 'new_m_ffn1_w_gate': 'new_m', 'new_m_ffn1_w_up': 'new_m', 'new_m_ffn1_w_down': 'new_m', 'new_m_mix_pre_g': 'new_m', 'new_m_w_in': 'new_m', 'new_m_na_rpb': 'new_m', 'new_m_s5_lam_re': 'new_m', 'new_m_s5_lam_im': 'new_m', 'new_m_s5_log_dt': 'new_m', 'new_m_s5_b_re': 'new_m', 'new_m_s5_b_im': 'new_m', 'new_m_s5_c_re': 'new_m', 'new_m_s5_c_im': 'new_m', 'new_m_s5_d': 'new_m', 'new_m_s5_w_glu': 'new_m', 'new_m_s5_b_glu': 'new_m', 'new_m_na_out_g': 'new_m', 'new_m_s5_out_g': 'new_m', 'new_m_w_out': 'new_m', 'new_m_mix_post_g': 'new_m', 'new_m_ffn2_pre_g': 'new_m', 'new_m_ffn2_post_g': 'new_m', 'new_m_ffn2_w_gate': 'new_m', 'new_m_ffn2_w_up': 'new_m', 'new_m_ffn2_w_down': 'new_m', 'new_m_final_g': 'new_m', 'new_v_meta_tokens': 'new_v', 'new_v_ffn1_pre_g': 'new_v', 'new_v_ffn1_post_g': 'new_v', 'new_v_ffn1_w_gate': 'new_v', 'new_v_ffn1_w_up': 'new_v', 'new_v_ffn1_w_down': 'new_v', 'new_v_mix_pre_g': 'new_v', 'new_v_w_in': 'new_v', 'new_v_na_rpb': 'new_v', 'new_v_s5_lam_re': 'new_v', 'new_v_s5_lam_im': 'new_v', 'new_v_s5_log_dt': 'new_v', 'new_v_s5_b_re': 'new_v', 'new_v_s5_b_im': 'new_v', 'new_v_s5_c_re': 'new_v', 'new_v_s5_c_im': 'new_v', 'new_v_s5_d': 'new_v', 'new_v_s5_w_glu': 'new_v', 'new_v_s5_b_glu': 'new_v', 'new_v_na_out_g': 'new_v', 'new_v_s5_out_g': 'new_v', 'new_v_w_out': 'new_v', 'new_v_mix_post_g': 'new_v', 'new_v_ffn2_pre_g': 'new_v', 'new_v_ffn2_post_g': 'new_v', 'new_v_ffn2_w_gate': 'new_v', 'new_v_ffn2_w_up': 'new_v', 'new_v_ffn2_w_down': 'new_v', 'new_v_final_g': 'new_v'}


def _forward(args):
    return _fwd_reference(*[args[k] for k in FWD_PARAMS])


def _output_shape():
    out = _jax.eval_shape(lambda: _forward(_fwd_setup_inputs(0)))
    return out.shape, out.dtype

N_MICROBATCH = 1
ADAM_LR = 0.001
ADAM_B1 = 0.9
ADAM_B2 = 0.999
ADAM_EPS = 1e-08
ADAM_WD = 0.01
ADAM_STEP = 10
PER_EXAMPLE_BATCH_AXIS = {'x': 0, 'loss_target': 0}
SHARED_INPUTS = []
_WEIGHT_DTYPES = {'meta_tokens': _jnp.float32, 'ffn1_pre_g': _jnp.float32, 'ffn1_post_g': _jnp.float32, 'ffn1_w_gate': _jnp.float32, 'ffn1_w_up': _jnp.float32, 'ffn1_w_down': _jnp.float32, 'mix_pre_g': _jnp.float32, 'w_in': _jnp.float32, 'na_rpb': _jnp.float32, 's5_lam_re': _jnp.float32, 's5_lam_im': _jnp.float32, 's5_log_dt': _jnp.float32, 's5_b_re': _jnp.float32, 's5_b_im': _jnp.float32, 's5_c_re': _jnp.float32, 's5_c_im': _jnp.float32, 's5_d': _jnp.float32, 's5_w_glu': _jnp.float32, 's5_b_glu': _jnp.float32, 'na_out_g': _jnp.float32, 's5_out_g': _jnp.float32, 'w_out': _jnp.float32, 'mix_post_g': _jnp.float32, 'ffn2_pre_g': _jnp.float32, 'ffn2_post_g': _jnp.float32, 'ffn2_w_gate': _jnp.float32, 'ffn2_w_up': _jnp.float32, 'ffn2_w_down': _jnp.float32, 'final_g': _jnp.float32}
MOMENT_SCALE = {'meta_tokens': 1.059649e-02, 'ffn1_pre_g': 1.042096e-01, 'ffn1_post_g': 8.141180e-02, 'ffn1_w_gate': 4.405728e-02, 'ffn1_w_up': 4.273118e-02, 'ffn1_w_down': 7.075376e-02, 'mix_pre_g': 1.263191e-01, 'w_in': 8.860705e-02, 'na_rpb': 2.970597e-02, 's5_lam_re': 4.785314e-03, 's5_lam_im': 4.896972e-03, 's5_log_dt': 4.467129e+00, 's5_b_re': 3.248140e-03, 's5_b_im': 3.180093e-03, 's5_c_re': 6.418746e-03, 's5_c_im': 6.434554e-03, 's5_d': 1.123603e-01, 's5_w_glu': 2.425824e-02, 's5_b_glu': 3.889281e-02, 'na_out_g': 8.456553e-02, 's5_out_g': 8.843141e-02, 'w_out': 9.034685e-02, 'mix_post_g': 2.280822e-01, 'ffn2_pre_g': 6.043003e-02, 'ffn2_post_g': 5.976434e-02, 'ffn2_w_gate': 2.467801e-02, 'ffn2_w_up': 2.397372e-02, 'ffn2_w_down': 3.968244e-02, 'final_g': 1.604506e+01}


def _to_microbatches(a, axis):
    t = _jnp.moveaxis(a, axis, 0)
    t = t.reshape((N_MICROBATCH, t.shape[0] // N_MICROBATCH) + t.shape[1:])
    return _jnp.moveaxis(t, 1, axis + 1)


def setup_inputs(seed: int = 0) -> dict:
    inp = _fwd_setup_inputs(seed)
    key = _jax.random.fold_in(_jax.random.key(seed), 7919)
    shape, _ = _output_shape()
    out = dict(inp)
    out["loss_target"] = _jax.random.normal(_jax.random.fold_in(key, 0), shape, _jnp.float32)
    for i, name in enumerate(TWIN_WEIGHTS):
        w = inp[name].astype(_jnp.float32)
        if MOMENT_SCALE is None:
            s = _jnp.sqrt(_jnp.mean(_jnp.square(w)) + 1e-30)
        else:
            s = MOMENT_SCALE[name]
        km, kv = _jax.random.split(_jax.random.fold_in(key, i + 1))
        out[name] = w
        out["m_" + name] = s * _jax.random.normal(km, w.shape, _jnp.float32)
        out["v_" + name] = (s * s) * _jax.random.uniform(kv, w.shape, _jnp.float32, 0.5, 1.5)
    if N_MICROBATCH > 1:
        for name, axis in PER_EXAMPLE_BATCH_AXIS.items():
            out[name] = _to_microbatches(out[name], axis)
    return {'x': out['x'], 'meta_tokens': out['meta_tokens'], 'ffn1_pre_g': out['ffn1_pre_g'], 'ffn1_post_g': out['ffn1_post_g'], 'ffn1_w_gate': out['ffn1_w_gate'], 'ffn1_w_up': out['ffn1_w_up'], 'ffn1_w_down': out['ffn1_w_down'], 'mix_pre_g': out['mix_pre_g'], 'w_in': out['w_in'], 'na_rpb': out['na_rpb'], 's5_lam_re': out['s5_lam_re'], 's5_lam_im': out['s5_lam_im'], 's5_log_dt': out['s5_log_dt'], 's5_b_re': out['s5_b_re'], 's5_b_im': out['s5_b_im'], 's5_c_re': out['s5_c_re'], 's5_c_im': out['s5_c_im'], 's5_d': out['s5_d'], 's5_w_glu': out['s5_w_glu'], 's5_b_glu': out['s5_b_glu'], 'na_out_g': out['na_out_g'], 's5_out_g': out['s5_out_g'], 'w_out': out['w_out'], 'mix_post_g': out['mix_post_g'], 'ffn2_pre_g': out['ffn2_pre_g'], 'ffn2_post_g': out['ffn2_post_g'], 'ffn2_w_gate': out['ffn2_w_gate'], 'ffn2_w_up': out['ffn2_w_up'], 'ffn2_w_down': out['ffn2_w_down'], 'final_g': out['final_g'], 'loss_target': out['loss_target'], 'm_meta_tokens': out['m_meta_tokens'], 'm_ffn1_pre_g': out['m_ffn1_pre_g'], 'm_ffn1_post_g': out['m_ffn1_post_g'], 'm_ffn1_w_gate': out['m_ffn1_w_gate'], 'm_ffn1_w_up': out['m_ffn1_w_up'], 'm_ffn1_w_down': out['m_ffn1_w_down'], 'm_mix_pre_g': out['m_mix_pre_g'], 'm_w_in': out['m_w_in'], 'm_na_rpb': out['m_na_rpb'], 'm_s5_lam_re': out['m_s5_lam_re'], 'm_s5_lam_im': out['m_s5_lam_im'], 'm_s5_log_dt': out['m_s5_log_dt'], 'm_s5_b_re': out['m_s5_b_re'], 'm_s5_b_im': out['m_s5_b_im'], 'm_s5_c_re': out['m_s5_c_re'], 'm_s5_c_im': out['m_s5_c_im'], 'm_s5_d': out['m_s5_d'], 'm_s5_w_glu': out['m_s5_w_glu'], 'm_s5_b_glu': out['m_s5_b_glu'], 'm_na_out_g': out['m_na_out_g'], 'm_s5_out_g': out['m_s5_out_g'], 'm_w_out': out['m_w_out'], 'm_mix_post_g': out['m_mix_post_g'], 'm_ffn2_pre_g': out['m_ffn2_pre_g'], 'm_ffn2_post_g': out['m_ffn2_post_g'], 'm_ffn2_w_gate': out['m_ffn2_w_gate'], 'm_ffn2_w_up': out['m_ffn2_w_up'], 'm_ffn2_w_down': out['m_ffn2_w_down'], 'm_final_g': out['m_final_g'], 'v_meta_tokens': out['v_meta_tokens'], 'v_ffn1_pre_g': out['v_ffn1_pre_g'], 'v_ffn1_post_g': out['v_ffn1_post_g'], 'v_ffn1_w_gate': out['v_ffn1_w_gate'], 'v_ffn1_w_up': out['v_ffn1_w_up'], 'v_ffn1_w_down': out['v_ffn1_w_down'], 'v_mix_pre_g': out['v_mix_pre_g'], 'v_w_in': out['v_w_in'], 'v_na_rpb': out['v_na_rpb'], 'v_s5_lam_re': out['v_s5_lam_re'], 'v_s5_lam_im': out['v_s5_lam_im'], 'v_s5_log_dt': out['v_s5_log_dt'], 'v_s5_b_re': out['v_s5_b_re'], 'v_s5_b_im': out['v_s5_b_im'], 'v_s5_c_re': out['v_s5_c_re'], 'v_s5_c_im': out['v_s5_c_im'], 'v_s5_d': out['v_s5_d'], 'v_s5_w_glu': out['v_s5_w_glu'], 'v_s5_b_glu': out['v_s5_b_glu'], 'v_na_out_g': out['v_na_out_g'], 'v_s5_out_g': out['v_s5_out_g'], 'v_w_out': out['v_w_out'], 'v_mix_post_g': out['v_mix_post_g'], 'v_ffn2_pre_g': out['v_ffn2_pre_g'], 'v_ffn2_post_g': out['v_ffn2_post_g'], 'v_ffn2_w_gate': out['v_ffn2_w_gate'], 'v_ffn2_w_up': out['v_ffn2_w_up'], 'v_ffn2_w_down': out['v_ffn2_w_down'], 'v_final_g': out['v_final_g']}


def _loss(weights, diff, rest, loss_target):
    with _jax.named_scope("forward"):
        args = {**rest, TWIN_DIFF_INPUT: diff, **{k: w.astype(_WEIGHT_DTYPES[k]) for k, w in weights.items()}}
        y = _forward(args)
    with _jax.named_scope("loss_head"):
        err = _jnp.square(y.astype(_jnp.float32) - loss_target)
        return 0.5 * _jnp.sum(_jnp.mean(err, axis=-1)) if err.ndim else 0.5 * err


def _adamw(w, g, m, v):
    m = ADAM_B1 * m + (1.0 - ADAM_B1) * g
    v = ADAM_B2 * v + (1.0 - ADAM_B2) * _jnp.square(g)
    m_hat = m / (1.0 - ADAM_B1 ** ADAM_STEP)
    v_hat = v / (1.0 - ADAM_B2 ** ADAM_STEP)
    delta = -ADAM_LR * (m_hat / (_jnp.sqrt(v_hat) + ADAM_EPS) + ADAM_WD * w)
    return delta, m, v


def reference(x, meta_tokens, ffn1_pre_g, ffn1_post_g, ffn1_w_gate, ffn1_w_up, ffn1_w_down, mix_pre_g, w_in, na_rpb, s5_lam_re, s5_lam_im, s5_log_dt, s5_b_re, s5_b_im, s5_c_re, s5_c_im, s5_d, s5_w_glu, s5_b_glu, na_out_g, s5_out_g, w_out, mix_post_g, ffn2_pre_g, ffn2_post_g, ffn2_w_gate, ffn2_w_up, ffn2_w_down, final_g, loss_target, m_meta_tokens, m_ffn1_pre_g, m_ffn1_post_g, m_ffn1_w_gate, m_ffn1_w_up, m_ffn1_w_down, m_mix_pre_g, m_w_in, m_na_rpb, m_s5_lam_re, m_s5_lam_im, m_s5_log_dt, m_s5_b_re, m_s5_b_im, m_s5_c_re, m_s5_c_im, m_s5_d, m_s5_w_glu, m_s5_b_glu, m_na_out_g, m_s5_out_g, m_w_out, m_mix_post_g, m_ffn2_pre_g, m_ffn2_post_g, m_ffn2_w_gate, m_ffn2_w_up, m_ffn2_w_down, m_final_g, v_meta_tokens, v_ffn1_pre_g, v_ffn1_post_g, v_ffn1_w_gate, v_ffn1_w_up, v_ffn1_w_down, v_mix_pre_g, v_w_in, v_na_rpb, v_s5_lam_re, v_s5_lam_im, v_s5_log_dt, v_s5_b_re, v_s5_b_im, v_s5_c_re, v_s5_c_im, v_s5_d, v_s5_w_glu, v_s5_b_glu, v_na_out_g, v_s5_out_g, v_w_out, v_mix_post_g, v_ffn2_pre_g, v_ffn2_post_g, v_ffn2_w_gate, v_ffn2_w_up, v_ffn2_w_down, v_final_g):
    given = dict(x=x, meta_tokens=meta_tokens, ffn1_pre_g=ffn1_pre_g, ffn1_post_g=ffn1_post_g, ffn1_w_gate=ffn1_w_gate, ffn1_w_up=ffn1_w_up, ffn1_w_down=ffn1_w_down, mix_pre_g=mix_pre_g, w_in=w_in, na_rpb=na_rpb, s5_lam_re=s5_lam_re, s5_lam_im=s5_lam_im, s5_log_dt=s5_log_dt, s5_b_re=s5_b_re, s5_b_im=s5_b_im, s5_c_re=s5_c_re, s5_c_im=s5_c_im, s5_d=s5_d, s5_w_glu=s5_w_glu, s5_b_glu=s5_b_glu, na_out_g=na_out_g, s5_out_g=s5_out_g, w_out=w_out, mix_post_g=mix_post_g, ffn2_pre_g=ffn2_pre_g, ffn2_post_g=ffn2_post_g, ffn2_w_gate=ffn2_w_gate, ffn2_w_up=ffn2_w_up, ffn2_w_down=ffn2_w_down, final_g=final_g, loss_target=loss_target, m_meta_tokens=m_meta_tokens, m_ffn1_pre_g=m_ffn1_pre_g, m_ffn1_post_g=m_ffn1_post_g, m_ffn1_w_gate=m_ffn1_w_gate, m_ffn1_w_up=m_ffn1_w_up, m_ffn1_w_down=m_ffn1_w_down, m_mix_pre_g=m_mix_pre_g, m_w_in=m_w_in, m_na_rpb=m_na_rpb, m_s5_lam_re=m_s5_lam_re, m_s5_lam_im=m_s5_lam_im, m_s5_log_dt=m_s5_log_dt, m_s5_b_re=m_s5_b_re, m_s5_b_im=m_s5_b_im, m_s5_c_re=m_s5_c_re, m_s5_c_im=m_s5_c_im, m_s5_d=m_s5_d, m_s5_w_glu=m_s5_w_glu, m_s5_b_glu=m_s5_b_glu, m_na_out_g=m_na_out_g, m_s5_out_g=m_s5_out_g, m_w_out=m_w_out, m_mix_post_g=m_mix_post_g, m_ffn2_pre_g=m_ffn2_pre_g, m_ffn2_post_g=m_ffn2_post_g, m_ffn2_w_gate=m_ffn2_w_gate, m_ffn2_w_up=m_ffn2_w_up, m_ffn2_w_down=m_ffn2_w_down, m_final_g=m_final_g, v_meta_tokens=v_meta_tokens, v_ffn1_pre_g=v_ffn1_pre_g, v_ffn1_post_g=v_ffn1_post_g, v_ffn1_w_gate=v_ffn1_w_gate, v_ffn1_w_up=v_ffn1_w_up, v_ffn1_w_down=v_ffn1_w_down, v_mix_pre_g=v_mix_pre_g, v_w_in=v_w_in, v_na_rpb=v_na_rpb, v_s5_lam_re=v_s5_lam_re, v_s5_lam_im=v_s5_lam_im, v_s5_log_dt=v_s5_log_dt, v_s5_b_re=v_s5_b_re, v_s5_b_im=v_s5_b_im, v_s5_c_re=v_s5_c_re, v_s5_c_im=v_s5_c_im, v_s5_d=v_s5_d, v_s5_w_glu=v_s5_w_glu, v_s5_b_glu=v_s5_b_glu, v_na_out_g=v_na_out_g, v_s5_out_g=v_s5_out_g, v_w_out=v_w_out, v_mix_post_g=v_mix_post_g, v_ffn2_pre_g=v_ffn2_pre_g, v_ffn2_post_g=v_ffn2_post_g, v_ffn2_w_gate=v_ffn2_w_gate, v_ffn2_w_up=v_ffn2_w_up, v_ffn2_w_down=v_ffn2_w_down, v_final_g=v_final_g)
    weights = {n: given[n] for n in TWIN_WEIGHTS}
    shared = {n: given[n] for n in SHARED_INPUTS}
    per_example = {n: given[n] for n in ['x']}
    grad_fn = _jax.value_and_grad(_loss, argnums=(0, 1))

    def one_microbatch(ex, loss_target):
        ex = dict(ex)
        diff = ex.pop(TWIN_DIFF_INPUT)
        return grad_fn(weights, diff, {**shared, **ex}, loss_target)

    if N_MICROBATCH == 1:
        loss, (grad_w, grad_x) = one_microbatch(per_example, given["loss_target"])
    else:
        def body(carry, xs):
            loss_sum, grad_sum = carry
            l_k, (gw_k, gx_k) = one_microbatch(xs[0], xs[1])
            with _jax.named_scope("update"):
                return (loss_sum + l_k, _jax.tree.map(_jnp.add, grad_sum, gw_k)), gx_k

        init = (_jnp.zeros((), _jnp.float32), _jax.tree.map(_jnp.zeros_like, weights))
        (loss, grad_w), grad_x = _jax.lax.scan(body, init, (per_example, given["loss_target"]))
    with _jax.named_scope("update"):
        delta_w, new_m, new_v = {}, {}, {}
        for n in TWIN_WEIGHTS:
            delta_w[n], new_m[n], new_v[n] = _adamw(weights[n], grad_w[n], given["m_" + n], given["v_" + n])
    return (loss, grad_x, *[grad_w[n] for n in TWIN_WEIGHTS], *[delta_w[n] for n in TWIN_WEIGHTS],
            *[new_m[n] for n in TWIN_WEIGHTS], *[new_v[n] for n in TWIN_WEIGHTS])
```

```python
import functools
import math

import numpy as np
import jax
import jax.numpy as jnp
from jax import lax
from jax.experimental import pallas as pl
from jax.experimental.pallas import tpu as pltpu

F32 = jnp.float32
BF16 = jnp.bfloat16
SDS = jax.ShapeDtypeStruct

D_MODEL = 1024
N_TOK = 2048
N_META = 16
SEQ = N_TOK + N_META
ROW_TILE = 688
N_ROW_TILES = SEQ // ROW_TILE
N_DEV = 8
D_FF = 2816
FF_SHARD = D_FF // N_DEV
IN_SHARD = 256
NA_WIDTH = 512
S5_WIDTH = 512
HEADS = 8
HEAD_DIM = 64
GRID_W = 64
GRID_ROWS = N_TOK // GRID_W
KH = 8
KW = 16
KEYS = KH * GRID_W
S5_GROUPS = 32
S5_GROUP = 16
S5_STATE = 64
S5_CHUNKS = 4
CH_W = S5_WIDTH // S5_CHUNKS
ST_W = S5_GROUPS * S5_STATE // S5_CHUNKS
SCAN_BLOCKS = 8
SCAN_T = SEQ // SCAN_BLOCKS
RMS_EPS = 1e-6
NEG_INF = -1e30
ATT_SCALE = HEAD_DIM ** -0.5
ADAM_LR, ADAM_B1, ADAM_B2, ADAM_EPS, ADAM_WD, ADAM_STEP = 0.001, 0.9, 0.999, 1e-08, 0.01, 10
VMEM_LIMIT = 56 * 1024 * 1024
MESH = pl.DeviceIdType.MESH
AXES = ("x", "y", "c")


def _params(sem=None):
    return pltpu.CompilerParams(dimension_semantics=sem, vmem_limit_bytes=VMEM_LIMIT)


def _dot(a, b):
    return jnp.dot(a, b, preferred_element_type=F32)


def _dot_nt(a, b):
    return lax.dot_general(a, b, (((1,), (1,)), ((), ())), preferred_element_type=F32)


def _dot_tn(a, b):
    return lax.dot_general(a, b, (((0,), (0,)), ((), ())), preferred_element_type=F32)


def _rstd(x):
    return lax.rsqrt(jnp.mean(x * x, axis=-1, keepdims=True) + RMS_EPS)


def _rms_bwd(x, r, g, dy):
    dyg = dy * g
    xr = x * r
    dx = r * (dyg - xr * jnp.mean(dyg * xr, axis=-1, keepdims=True))
    return dx, dy * xr


def _rows(i, size=ROW_TILE):
    return pl.ds(pl.multiple_of(i * size, 16), size)


def _row_spec(width):
    return pl.BlockSpec((ROW_TILE, width), lambda i: (i, 0))


def _fix_spec(shape):
    return pl.BlockSpec(shape, lambda i: (0,) * len(shape))


def _split3(x):
    hi = x.astype(BF16)
    r1 = x - hi.astype(F32)
    mid = r1.astype(BF16)
    lo = (r1 - mid.astype(F32)).astype(BF16)
    return hi, mid, lo


def _prenorm(x, g):
    def body(x_ref, g_ref, a_ref):
        xv = x_ref[...]
        a_ref[...] = (xv * _rstd(xv) * g_ref[...]).astype(BF16)

    return pl.pallas_call(
        body, grid=(N_ROW_TILES,), in_specs=[_row_spec(D_MODEL), _fix_spec((1, D_MODEL))],
        out_specs=_row_spec(D_MODEL), out_shape=SDS((SEQ, D_MODEL), BF16), name="prenorm",
        compiler_params=_params(("parallel",)))(x, g)


def _post_pre(f, hres, g_post, g_next, scale, name):
    def body(f_ref, h_ref, gp_ref, gn_ref, ho_ref, a_ref):
        fv = f_ref[...]
        h = h_ref[...] + scale * (fv * _rstd(fv) * gp_ref[...])
        ho_ref[...] = h
        a_ref[...] = (h * _rstd(h) * gn_ref[...]).astype(BF16)

    return pl.pallas_call(
        body, grid=(N_ROW_TILES,),
        in_specs=[_row_spec(D_MODEL), _row_spec(D_MODEL), _fix_spec((1, D_MODEL)), _fix_spec((1, D_MODEL))],
        out_specs=[_row_spec(D_MODEL), _row_spec(D_MODEL)],
        out_shape=[SDS((SEQ, D_MODEL), F32), SDS((SEQ, D_MODEL), BF16)], name=name,
        compiler_params=_params(("parallel",)))(f, hres, g_post, g_next)


def _final_loss(f2, h2, g_post, g_final, target):
    def body(f_ref, h_ref, gp_ref, gf_ref, t_ref, loss_ref, dh_ref, df_ref, dgf_ref, dgp_ref):
        i = pl.program_id(0)
        fv = f_ref[...]
        r1 = _rstd(fv)
        gp = gp_ref[...]
        h3 = h_ref[...] + 0.5 * (fv * r1 * gp)
        r2 = _rstd(h3)
        gf = gf_ref[...]
        y = h3 * r2 * gf
        row = lax.broadcasted_iota(jnp.int32, (ROW_TILE, 1), 0) + i * ROW_TILE
        err = jnp.where(row >= N_META, y - t_ref[...], 0.0)
        part = 0.5 * jnp.sum(jnp.mean(err * err, axis=-1, keepdims=True))
        dy = err * (1.0 / D_MODEL)
        dh3, dgf = _rms_bwd(h3, r2, gf, dy)
        dh_ref[...] = dh3
        df, dgp = _rms_bwd(fv, r1, gp, 0.5 * dh3)
        df_ref[...] = df.astype(BF16)

        @pl.when(i == 0)
        def _():
            loss_ref[...] = jnp.zeros_like(loss_ref)
            dgf_ref[...] = jnp.zeros_like(dgf_ref)
            dgp_ref[...] = jnp.zeros_like(dgp_ref)

        loss_ref[...] += part
        dgf_ref[...] += jnp.sum(dgf, axis=0, keepdims=True)
        dgp_ref[...] += jnp.sum(dgp, axis=0, keepdims=True)

    gain = _fix_spec((1, D_MODEL))
    return pl.pallas_call(
        body, grid=(N_ROW_TILES,),
        in_specs=[_row_spec(D_MODEL), _row_spec(D_MODEL), gain, gain, _row_spec(D_MODEL)],
        out_specs=[_fix_spec((8, 128)), _row_spec(D_MODEL), _row_spec(D_MODEL), gain, gain],
        out_shape=[SDS((8, 128), F32), SDS((SEQ, D_MODEL), F32), SDS((SEQ, D_MODEL), BF16),
                   SDS((1, D_MODEL), F32), SDS((1, D_MODEL), F32)],
        name="final_loss", compiler_params=_params(("arbitrary",)))(f2, h2, g_post, g_final, target)


def _bwd_pre_post(da, h, g_pre, dh_res, fprev, g_post, scale, name):
    def body(da_ref, h_ref, gpre_ref, dhr_ref, f_ref, gpost_ref, dh_ref, df_ref, dgpre_ref, dgpost_ref):
        i = pl.program_id(0)
        hv = h_ref[...]
        dxa, dgpre = _rms_bwd(hv, _rstd(hv), gpre_ref[...], da_ref[...])
        dh = dhr_ref[...] + dxa
        dh_ref[...] = dh
        fv = f_ref[...]
        df, dgpost = _rms_bwd(fv, _rstd(fv), gpost_ref[...], scale * dh)
        df_ref[...] = df.astype(BF16)

        @pl.when(i == 0)
        def _():
            dgpre_ref[...] = jnp.zeros_like(dgpre_ref)
            dgpost_ref[...] = jnp.zeros_like(dgpost_ref)

        dgpre_ref[...] += jnp.sum(dgpre, axis=0, keepdims=True)
        dgpost_ref[...] += jnp.sum(dgpost, axis=0, keepdims=True)

    gain = _fix_spec((1, D_MODEL))
    row = _row_spec(D_MODEL)
    return pl.pallas_call(
        body, grid=(N_ROW_TILES,), in_specs=[row, row, gain, row, row, gain],
        out_specs=[row, row, gain, gain],
        out_shape=[SDS((SEQ, D_MODEL), F32), SDS((SEQ, D_MODEL), BF16), SDS((1, D_MODEL), F32), SDS((1, D_MODEL), F32)],
        name=name, compiler_params=_params(("arbitrary",)))(da, h, g_pre, dh_res, fprev, g_post)


def _bwd_pre_only(da, h, g_pre, dh_res):
    def body(da_ref, h_ref, gpre_ref, dhr_ref, dh_ref, dgpre_ref):
        i = pl.program_id(0)
        hv = h_ref[...]
        dxa, dgpre = _rms_bwd(hv, _rstd(hv), gpre_ref[...], da_ref[...])
        dh_ref[...] = dhr_ref[...] + dxa

        @pl.when(i == 0)
        def _():
            dgpre_ref[...] = jnp.zeros_like(dgpre_ref)

        dgpre_ref[...] += jnp.sum(dgpre, axis=0, keepdims=True)

    gain = _fix_spec((1, D_MODEL))
    row = _row_spec(D_MODEL)
    return pl.pallas_call(
        body, grid=(N_ROW_TILES,), in_specs=[row, row, gain, row], out_specs=[row, gain],
        out_shape=[SDS((SEQ, D_MODEL), F32), SDS((1, D_MODEL), F32)],
        name="bwd_pre_only", compiler_params=_params(("arbitrary",)))(da, h, g_pre, dh_res)


def _ffn_fwd(a, wg, wu, wd, name):
    def body(a_ref, wg_ref, wu_ref, wd_ref, gate_ref, up_ref, f_ref):
        j = pl.program_id(0)

        def tile(i, carry):
            rows = _rows(i)
            at = a_ref[rows, :]
            gate = _dot(at, wg_ref[...])
            up = _dot(at, wu_ref[...])
            gate_ref[rows, :] = gate
            up_ref[rows, :] = up
            act = (gate * jax.nn.sigmoid(gate) * up).astype(BF16)
            contrib = _dot(act, wd_ref[...])

            @pl.when(j == 0)
            def _():
                f_ref[rows, :] = contrib

            @pl.when(j != 0)
            def _():
                f_ref[rows, :] += contrib

            return carry

        lax.fori_loop(0, N_ROW_TILES, tile, 0)

    shard_cols = pl.BlockSpec((None, D_MODEL, FF_SHARD), lambda j: (j, 0, 0))
    shard_rows = pl.BlockSpec((None, FF_SHARD, D_MODEL), lambda j: (j, 0, 0))
    hid = pl.BlockSpec((None, SEQ, FF_SHARD), lambda j: (j, 0, 0))
    full = pl.BlockSpec((SEQ, D_MODEL), lambda j: (0, 0))
    return pl.pallas_call(
        body, grid=(N_DEV,), in_specs=[full, shard_cols, shard_cols, shard_rows], out_specs=[hid, hid, full],
        out_shape=[SDS((N_DEV, SEQ, FF_SHARD), F32), SDS((N_DEV, SEQ, FF_SHARD), F32), SDS((SEQ, D_MODEL), F32)],
        name=name, compiler_params=_params(("arbitrary",)))(a, wg, wu, wd)


def _ffn_bwd(df, a, gate, up, wg, wu, wd, name):
    def body(df_ref, a_ref, gate_ref, up_ref, wg_ref, wu_ref, wd_ref, da_ref, dwg_ref, dwu_ref, dwd_ref,
             acc_g, acc_u, acc_d):
        j = pl.program_id(0)

        def tile(i, carry):
            rows = _rows(i)
            dft = df_ref[rows, :]
            at = a_ref[rows, :]
            gate = gate_ref[rows, :]
            up = up_ref[rows, :]
            dact = _dot_nt(dft, wd_ref[...])
            sig = jax.nn.sigmoid(gate)
            silu = gate * sig
            dgate = (dact * up * (sig * (1.0 + gate * (1.0 - sig)))).astype(BF16)
            dup = (dact * silu).astype(BF16)
            act = (silu * up).astype(BF16)
            dwd = _dot_tn(act, dft)
            dwg = _dot_tn(at, dgate)
            dwu = _dot_tn(at, dup)
            dat = _dot_nt(dgate, wg_ref[...]) + _dot_nt(dup, wu_ref[...])

            @pl.when(i == 0)
            def _():
                acc_d[...] = dwd
                acc_g[...] = dwg
                acc_u[...] = dwu

            @pl.when(i != 0)
            def _():
                acc_d[...] += dwd
                acc_g[...] += dwg
                acc_u[...] += dwu

            @pl.when(j == 0)
            def _():
                da_ref[rows, :] = dat

            @pl.when(j != 0)
            def _():
                da_ref[rows, :] += dat

            return carry

        lax.fori_loop(0, N_ROW_TILES, tile, 0)
        dwg_ref[...] = acc_g[...].astype(BF16)
        dwu_ref[...] = acc_u[...].astype(BF16)
        dwd_ref[...] = acc_d[...].astype(BF16)

    shard_cols = pl.BlockSpec((None, D_MODEL, FF_SHARD), lambda j: (j, 0, 0))
    shard_rows = pl.BlockSpec((None, FF_SHARD, D_MODEL), lambda j: (j, 0, 0))
    hid = pl.BlockSpec((None, SEQ, FF_SHARD), lambda j: (j, 0, 0))
    full = pl.BlockSpec((SEQ, D_MODEL), lambda j: (0, 0))
    return pl.pallas_call(
        body, grid=(N_DEV,), in_specs=[full, full, hid, hid, shard_cols, shard_cols, shard_rows],
        out_specs=[full, shard_cols, shard_cols, shard_rows],
        out_shape=[SDS((SEQ, D_MODEL), F32), SDS((N_DEV, D_MODEL, FF_SHARD), BF16),
                   SDS((N_DEV, D_MODEL, FF_SHARD), BF16), SDS((N_DEV, FF_SHARD, D_MODEL), BF16)],
        scratch_shapes=[pltpu.VMEM((D_MODEL, FF_SHARD), F32), pltpu.VMEM((D_MODEL, FF_SHARD), F32),
                        pltpu.VMEM((FF_SHARD, D_MODEL), F32)],
        name=name, compiler_params=_params(("arbitrary",)))(df, a, gate, up, wg, wu, wd)


def _proj_fwd(a, w):
    def body(a_ref, w_ref, o_ref):
        def tile(i, carry):
            rows = _rows(i)
            o_ref[rows, :] = _dot(a_ref[rows, :], w_ref[...])
            return carry

        lax.fori_loop(0, N_ROW_TILES, tile, 0)

    return pl.pallas_call(
        body, grid=(N_DEV,),
        in_specs=[pl.BlockSpec((SEQ, D_MODEL), lambda j: (0, 0)), pl.BlockSpec((None, D_MODEL, IN_SHARD), lambda j: (j, 0, 0))],
        out_specs=pl.BlockSpec((None, SEQ, IN_SHARD), lambda j: (j, 0, 0)),
        out_shape=SDS((N_DEV, SEQ, IN_SHARD), F32), name="proj_fwd",
        compiler_params=_params(("parallel",)))(a, w)


def _proj_bwd(dproj, a, w):
    def body(dp_ref, a_ref, w_ref, da_ref, dw_ref, acc):
        j = pl.program_id(0)

        def tile(i, carry):
            rows = _rows(i)
            dpt = dp_ref[rows, :]
            dw = _dot_tn(a_ref[rows, :], dpt)
            dat = _dot_nt(dpt, w_ref[...])

            @pl.when(i == 0)
            def _():
                acc[...] = dw

            @pl.when(i != 0)
            def _():
                acc[...] += dw

            @pl.when(j == 0)
            def _():
                da_ref[rows, :] = dat

            @pl.when(j != 0)
            def _():
                da_ref[rows, :] += dat

            return carry

        lax.fori_loop(0, N_ROW_TILES, tile, 0)
        dw_ref[...] = acc[...].astype(BF16)

    full = pl.BlockSpec((SEQ, D_MODEL), lambda j: (0, 0))
    wspec = pl.BlockSpec((None, D_MODEL, IN_SHARD), lambda j: (j, 0, 0))
    return pl.pallas_call(
        body, grid=(N_DEV,),
        in_specs=[pl.BlockSpec((None, SEQ, IN_SHARD), lambda j: (j, 0, 0)), full, wspec],
        out_specs=[full, wspec],
        out_shape=[SDS((SEQ, D_MODEL), F32), SDS((N_DEV, D_MODEL, IN_SHARD), BF16)],
        scratch_shapes=[pltpu.VMEM((D_MODEL, IN_SHARD), F32)],
        name="proj_bwd", compiler_params=_params(("arbitrary",)))(dproj, a, w)


def _na_consts():
    c = np.arange(GRID_W)
    col_start = np.clip(c - KW // 2, 0, GRID_W - KW)
    col_in = (c[None, :] >= col_start[:, None]) & (c[None, :] < col_start[:, None] + KW)
    dc = np.clip(c[None, :] - c[:, None] + KW - 1, 0, 2 * KW - 2)
    onehot = np.zeros((128, GRID_W * GRID_W), np.float32)
    qq, kk = np.meshgrid(c, c, indexing="ij")
    onehot[dc[col_in], (qq * GRID_W + kk)[col_in]] = 1.0
    negmask = np.where(col_in, 0.0, NEG_INF).astype(np.float32).reshape(1, -1)
    sel = np.zeros((16, 64), np.float32)
    for off in range(8):
        for kh in range(KH):
            sel[off + kh, off * 8 + kh] = 1.0
    return onehot, negmask, sel


def _rpb_expand(rpb):
    onehot, negmask, _ = _na_consts()
    rows = HEADS * (2 * KH - 1)
    rpb_pad = jnp.pad(rpb.reshape(rows, 2 * KW - 1), ((0, 128 - rows), (0, 128 - (2 * KW - 1))))

    def body(r_ref, oh_ref, m_ref, t_ref):
        hi, mid, lo = _split3(r_ref[...])
        oh = oh_ref[...]
        t_ref[...] = _dot(hi, oh) + _dot(mid, oh) + _dot(lo, oh) + m_ref[...]

    table = pl.pallas_call(body, out_shape=SDS((128, GRID_W * GRID_W), F32), name="rpb_expand",
                           compiler_params=_params())(rpb_pad, jnp.asarray(onehot, BF16), jnp.asarray(negmask))
    t4 = table[:rows].reshape(HEADS, 2 * KH - 1, GRID_W, GRID_W)
    per_off = jnp.stack([t4[:, o:o + KH] for o in range(8)], axis=1)
    return per_off.transpose(0, 1, 3, 2, 4).reshape(HEADS, 8, GRID_W, KEYS)


def _rpb_reduce(dbias):
    onehot, _, sel = _na_consts()
    x = dbias.reshape(HEADS, 8, GRID_W, KH, GRID_W).transpose(0, 1, 3, 2, 4).reshape(HEADS, 64, GRID_W * GRID_W)

    def body(x_ref, oht_ref, sel_ref, o_ref):
        hi, mid, lo = _split3(x_ref[...])
        oht = oht_ref[...]
        y = _dot(hi, oht) + _dot(mid, oht) + _dot(lo, oht)
        hi, mid, lo = _split3(y)
        s = sel_ref[...]
        o_ref[...] = _dot(s, hi) + _dot(s, mid) + _dot(s, lo)

    out = pl.pallas_call(
        body, grid=(HEADS,),
        in_specs=[pl.BlockSpec((None, 64, GRID_W * GRID_W), lambda h: (h, 0, 0)),
                  pl.BlockSpec((GRID_W * GRID_W, 128), lambda h: (0, 0)), pl.BlockSpec((16, 64), lambda h: (0, 0))],
        out_specs=pl.BlockSpec((None, 16, 128), lambda h: (h, 0, 0)),
        out_shape=SDS((HEADS, 16, 128), F32), name="rpb_reduce",
        compiler_params=_params(("parallel",)))(x, jnp.asarray(onehot.T, BF16), jnp.asarray(sel, BF16))
    return out[:, :2 * KH - 1, :2 * KW - 1]


def _block_geometry(r):
    row_start = jnp.clip(r - KH // 2, 0, GRID_ROWS - KH)
    off = row_start - r + (KH - 1)
    q0 = pl.multiple_of(N_META + r * GRID_W, 16)
    k0 = pl.multiple_of(N_META + row_start * GRID_W, 16)
    return off, q0, k0


def _na_probs(q, kk, km, bias):
    s = _dot_nt(q, kk) * ATT_SCALE + bias
    sm = _dot_nt(q, km) * ATT_SCALE
    m = jnp.maximum(jnp.max(s, axis=-1, keepdims=True), jnp.max(sm, axis=-1, keepdims=True))
    p = jnp.exp(s - m)
    pm = jnp.exp(sm - m)
    inv = 1.0 / (jnp.sum(p, axis=-1, keepdims=True) + jnp.sum(pm, axis=-1, keepdims=True))
    return p * inv, pm * inv


def _meta_probs(qm, km):
    s = _dot_nt(qm, km) * ATT_SCALE
    p = jnp.exp(s - jnp.max(s, axis=-1, keepdims=True))
    return p / jnp.sum(p, axis=-1, keepdims=True)


def _na_fwd(q, k, v, bias):
    def body(q_ref, k_ref, v_ref, b_ref, o_ref):
        km = k_ref[0:N_META, :].astype(BF16)
        vm = v_ref[0:N_META, :].astype(BF16)
        pmm = _meta_probs(q_ref[0:N_META, :].astype(BF16), km)
        o_ref[0:N_META, :] = _dot(pmm.astype(BF16), vm)

        def block(r, carry):
            off, q0, k0 = _block_geometry(r)
            qb = q_ref[pl.ds(q0, GRID_W), :].astype(BF16)
            kk = k_ref[pl.ds(k0, KEYS), :].astype(BF16)
            vv = v_ref[pl.ds(k0, KEYS), :].astype(BF16)
            p, pm = _na_probs(qb, kk, km, b_ref[off])
            o_ref[pl.ds(q0, GRID_W), :] = _dot(p.astype(BF16), vv) + _dot(pm.astype(BF16), vm)
            return carry

        lax.fori_loop(0, GRID_ROWS, block, 0)

    head = pl.BlockSpec((None, SEQ, HEAD_DIM), lambda h: (h, 0, 0))
    return pl.pallas_call(
        body, grid=(HEADS,), in_specs=[head, head, head, pl.BlockSpec((None, 8, GRID_W, KEYS), lambda h: (h, 0, 0, 0))],
        out_specs=head, out_shape=SDS((HEADS, SEQ, HEAD_DIM), F32), name="na_fwd",
        compiler_params=_params(("parallel",)))(q, k, v, bias)


def _na_bwd(q, k, v, bias, do):
    def body(q_ref, k_ref, v_ref, b_ref, do_ref, dq_ref, dk_ref, dv_ref, db_ref):
        km = k_ref[0:N_META, :].astype(BF16)
        vm = v_ref[0:N_META, :].astype(BF16)
        dk_ref[...] = jnp.zeros_like(dk_ref)
        dv_ref[...] = jnp.zeros_like(dv_ref)
        db_ref[...] = jnp.zeros_like(db_ref)

        qm = q_ref[0:N_META, :].astype(BF16)
        dom = do_ref[0:N_META, :].astype(BF16)
        pmm = _meta_probs(qm, km)
        dpm = _dot_nt(dom, vm)
        dsm = (pmm * (dpm - jnp.sum(pmm * dpm, axis=-1, keepdims=True)) * ATT_SCALE).astype(BF16)
        dq_ref[0:N_META, :] = _dot(dsm, km)
        dkm0 = _dot_tn(dsm, qm)
        dvm0 = _dot_tn(pmm.astype(BF16), dom)

        def block(r, carry):
            dkm, dvm = carry
            off, q0, k0 = _block_geometry(r)
            qb = q_ref[pl.ds(q0, GRID_W), :].astype(BF16)
            kk = k_ref[pl.ds(k0, KEYS), :].astype(BF16)
            vv = v_ref[pl.ds(k0, KEYS), :].astype(BF16)
            dob = do_ref[pl.ds(q0, GRID_W), :].astype(BF16)
            p, pm = _na_probs(qb, kk, km, b_ref[off])
            dp = _dot_nt(dob, vv)
            dpm_ = _dot_nt(dob, vm)
            delta = jnp.sum(p * dp, axis=-1, keepdims=True) + jnp.sum(pm * dpm_, axis=-1, keepdims=True)
            ds = p * (dp - delta)
            dsm_ = pm * (dpm_ - delta)
            db_ref[off] += ds
            dsb = (ds * ATT_SCALE).astype(BF16)
            dsmb = (dsm_ * ATT_SCALE).astype(BF16)
            dq_ref[pl.ds(q0, GRID_W), :] = _dot(dsb, kk) + _dot(dsmb, km)
            dk_ref[pl.ds(k0, KEYS), :] += _dot_tn(dsb, qb)
            dv_ref[pl.ds(k0, KEYS), :] += _dot_tn(p.astype(BF16), dob)
            return dkm + _dot_tn(dsmb, qb), dvm + _dot_tn(pm.astype(BF16), dob)

        dkm, dvm = lax.fori_loop(0, GRID_ROWS, block, (dkm0, dvm0))
        dk_ref[0:N_META, :] = dkm
        dv_ref[0:N_META, :] = dvm

    head = pl.BlockSpec((None, SEQ, HEAD_DIM), lambda h: (h, 0, 0))
    bspec = pl.BlockSpec((None, 8, GRID_W, KEYS), lambda h: (h, 0, 0, 0))
    return pl.pallas_call(
        body, grid=(HEADS,), in_specs=[head, head, head, bspec, head], out_specs=[head, head, head, bspec],
        out_shape=[SDS((HEADS, SEQ, HEAD_DIM), F32)] * 3 + [SDS((HEADS, 8, GRID_W, KEYS), F32)],
        name="na_bwd", compiler_params=_params(("parallel",)))(q, k, v, bias, do)


def _cmul(ar, ai, br, bi):
    return ar * br - ai * bi, ar * bi + ai * br


def _cpow(ar, ai, n):
    rr, ri = None, None
    br, bi = ar, ai
    while n:
        if n & 1:
            rr, ri = (br, bi) if rr is None else _cmul(rr, ri, br, bi)
        n >>= 1
        if n:
            br, bi = _cmul(br, bi, br, bi)
    return rr, ri


def _s5_prep(lr, li, logdt, bre, bim):
    def body(lr_ref, li_ref, dt_ref, br_ref, bi_ref, lbr_ref, lbi_ref, bbr_ref, bbi_ref):
        lr_, li_ = lr_ref[...], li_ref[...]
        dt = jnp.exp(dt_ref[...])
        mag = jnp.exp(lr_ * dt)
        lbr = mag * jnp.cos(li_ * dt)
        lbi = mag * jnp.sin(li_ * dt)
        lbr_ref[...] = lbr
        lbi_ref[...] = lbi
        den = lr_ * lr_ + li_ * li_
        xr = lbr - 1.0
        cr = (xr * lr_ + lbi * li_) / den
        ci = (lbi * lr_ - xr * li_) / den
        br, bi = br_ref[...], bi_ref[...]
        bbr_ref[...] = cr[:, None, :] * br - ci[:, None, :] * bi
        bbi_ref[...] = cr[:, None, :] * bi + ci[:, None, :] * br

    n = 2 * S5_GROUPS
    return pl.pallas_call(
        body, out_shape=[SDS((n, S5_STATE), F32)] * 2 + [SDS((n, S5_GROUP, S5_STATE), F32)] * 2,
        name="s5_prep", compiler_params=_params())(lr, li, logdt, bre, bim)


def _s5_prep_bwd(lr, li, logdt, bre, bim, dar, dai, dbbr, dbbi):
    def body(lr_ref, li_ref, dt_ref, br_ref, bi_ref, dar_ref, dai_ref, dbr_ref, dbi_ref,
             glr_ref, gli_ref, gdt_ref, gbr_ref, gbi_ref):
        lr_, li_ = lr_ref[...], li_ref[...]
        dt = jnp.exp(dt_ref[...])
        mag = jnp.exp(lr_ * dt)
        lbr = mag * jnp.cos(li_ * dt)
        lbi = mag * jnp.sin(li_ * dt)
        den = lr_ * lr_ + li_ * li_
        xr = lbr - 1.0
        cr = (xr * lr_ + lbi * li_) / den
        ci = (lbi * lr_ - xr * li_) / den
        br, bi = br_ref[...], bi_ref[...]
        dbr, dbi = dbr_ref[...], dbi_ref[...]
        gbr_ref[...] = cr[:, None, :] * dbr + ci[:, None, :] * dbi
        gbi_ref[...] = cr[:, None, :] * dbi - ci[:, None, :] * dbr
        gcr = jnp.sum(dbr * br + dbi * bi, axis=1)
        gci = jnp.sum(dbi * br - dbr * bi, axis=1)
        ilr, ili = lr_ / den, li_ / den
        tr, ti = _cmul(gcr, gci, ilr, ili)
        glbr = dar_ref[...] + tr
        glbi = dai_ref[...] + ti
        dr_, di_ = _cmul(tr, ti, cr, -ci)
        gwr, gwi = _cmul(glbr, glbi, lbr, -lbi)
        glr_ref[...] = gwr * dt - dr_
        gli_ref[...] = gwi * dt - di_
        gdt_ref[...] = jnp.sum(gwr * lr_ + gwi * li_, axis=-1, keepdims=True) * dt

    n = 2 * S5_GROUPS
    return pl.pallas_call(
        body, out_shape=[SDS((n, S5_STATE), F32)] * 2 + [SDS((n, 1), F32)] + [SDS((n, S5_GROUP, S5_STATE), F32)] * 2,
        name="s5_prep_bwd", compiler_params=_params())(lr, li, logdt, bre, bim, dar, dai, dbbr, dbbi)


def _scan_local(xr_ref, xi_ref, ar8, ai8, reverse):
    def step(i, carry):
        sr, si = carry
        idx = (SCAN_T - 1 - i) if reverse else i
        rows = pl.ds(pl.multiple_of(idx * SCAN_BLOCKS, SCAN_BLOCKS), SCAN_BLOCKS)
        nr = ar8 * sr - ai8 * si + xr_ref[rows, :]
        ni = ar8 * si + ai8 * sr + xi_ref[rows, :]
        xr_ref[rows, :] = nr
        xi_ref[rows, :] = ni
        return nr, ni

    z = jnp.zeros(ar8.shape, F32)
    return lax.fori_loop(0, SCAN_T, step, (z, z))


def _scan_carries(er, ei, atr, ati, reverse):
    row = lax.broadcasted_iota(jnp.int32, er.shape, 0)
    cr = jnp.zeros((1, er.shape[1]), F32)
    ci = cr
    outr = jnp.zeros(er.shape, F32)
    outi = outr
    order = range(SCAN_BLOCKS - 1, -1, -1) if reverse else range(SCAN_BLOCKS)
    for b in order:
        outr = jnp.where(row == b, cr, outr)
        outi = jnp.where(row == b, ci, outi)
        nr, ni = _cmul(atr, ati, cr, ci)
        cr, ci = nr + er[b:b + 1, :], ni + ei[b:b + 1, :]
    return outr, outi


def _scan_fixup(xr_ref, xi_ref, cr8, ci8, ar8, ai8, reverse):
    def step(i, carry):
        pr, pi = carry
        idx = (SCAN_T - 1 - i) if reverse else i
        rows = pl.ds(pl.multiple_of(idx * SCAN_BLOCKS, SCAN_BLOCKS), SCAN_BLOCKS)
        fr, fi = _cmul(pr, pi, cr8, ci8)
        xr_ref[rows, :] += fr
        xi_ref[rows, :] += fi
        return _cmul(pr, pi, ar8, ai8)

    lax.fori_loop(0, SCAN_T, step, (ar8, ai8))


def _scan(xr_ref, xi_ref, ar, ai, reverse):
    n = ar.shape[1]
    ar8 = jnp.broadcast_to(ar, (SCAN_BLOCKS, n))
    ai8 = jnp.broadcast_to(ai, (SCAN_BLOCKS, n))
    er, ei = _scan_local(xr_ref, xi_ref, ar8, ai8, reverse)
    atr, ati = _cpow(ar, ai, SCAN_T)
    cr8, ci8 = _scan_carries(er, ei, atr, ati, reverse)
    _scan_fixup(xr_ref, xi_ref, cr8, ci8, ar8, ai8, reverse)


def _s5_specs():
    chan = pl.BlockSpec((None, SEQ, CH_W), lambda d, c: (d, 0, c))
    state = pl.BlockSpec((None, SEQ, ST_W), lambda d, c: (d, 0, c))
    bmat = pl.BlockSpec((None, None, CH_W, ST_W), lambda d, c: (d, c, 0, 0))
    cmat = pl.BlockSpec((None, None, ST_W, CH_W), lambda d, c: (d, c, 0, 0))
    avec = pl.BlockSpec((None, None, 1, ST_W), lambda d, c: (d, c, 0, 0))
    return chan, state, bmat, cmat, avec


def _s5_scan_fwd(u2, bre, bim, are, aim, cre, cim):
    def body(u_ref, bre_ref, bim_ref, are_ref, aim_ref, cre_ref, cim_ref, sr_ref, si_ref, y_ref):
        ub = u_ref[...].astype(BF16)
        sr_ref[...] = _dot(ub, bre_ref[...])
        si_ref[...] = _dot(ub, bim_ref[...])
        _scan(sr_ref, si_ref, are_ref[...], aim_ref[...], reverse=False)
        y_ref[...] = _dot(sr_ref[...].astype(BF16), cre_ref[...]) - _dot(si_ref[...].astype(BF16), cim_ref[...])

    chan, state, bmat, cmat, avec = _s5_specs()
    return pl.pallas_call(
        body, grid=(2, S5_CHUNKS), in_specs=[chan, bmat, bmat, avec, avec, cmat, cmat], out_specs=[state, state, chan],
        out_shape=[SDS((2, SEQ, S5_GROUPS * S5_STATE), F32)] * 2 + [SDS((2, SEQ, S5_WIDTH), F32)],
        name="s5_scan_fwd", compiler_params=_params(("parallel", "parallel")))(u2, bre, bim, are, aim, cre, cim)


def _s5_scan_bwd(dy2, u2, sr, si, bre, bim, are, aim, cre, cim):
    def body(dy_ref, u_ref, sr_ref, si_ref, bre_ref, bim_ref, are_ref, aim_ref, cre_ref, cim_ref,
             du_ref, dbr_ref, dbi_ref, dcr_ref, dci_ref, dar_ref, dai_ref, gr_ref, gi_ref):
        dyb = dy_ref[...].astype(BF16)
        gr_ref[...] = _dot_nt(dyb, cre_ref[...])
        gi_ref[...] = -_dot_nt(dyb, cim_ref[...])
        dcr_ref[...] = _dot_tn(sr_ref[...].astype(BF16), dyb)
        dci_ref[...] = -_dot_tn(si_ref[...].astype(BF16), dyb)
        ar, ai = are_ref[...], aim_ref[...]
        _scan(gr_ref, gi_ref, ar, -ai, reverse=True)

        def step(i, carry):
            accr, acci = carry
            rows = pl.ds(pl.multiple_of(i * SCAN_BLOCKS, SCAN_BLOCKS), SCAN_BLOCKS)
            prev = pl.ds(pl.multiple_of((i - 1) * SCAN_BLOCKS, SCAN_BLOCKS), SCAN_BLOCKS)
            pr, pi = _cmul(gr_ref[rows, :], gi_ref[rows, :], sr_ref[prev, :], -si_ref[prev, :])
            return accr + pr, acci + pi

        last = pl.ds((SCAN_T - 1) * SCAN_BLOCKS, SCAN_BLOCKS)
        first = pl.ds(0, SCAN_BLOCKS)
        row = lax.broadcasted_iota(jnp.int32, (SCAN_BLOCKS, ST_W), 0)
        spr = jnp.where(row == 0, 0.0, pltpu.roll(sr_ref[last, :], 1, 0))
        spi = jnp.where(row == 0, 0.0, pltpu.roll(si_ref[last, :], 1, 0))
        acc0 = _cmul(gr_ref[first, :], gi_ref[first, :], spr, -spi)
        accr, acci = lax.fori_loop(1, SCAN_T, step, acc0)
        dar_ref[...] = jnp.sum(accr, axis=0, keepdims=True)
        dai_ref[...] = jnp.sum(acci, axis=0, keepdims=True)

        grb = gr_ref[...].astype(BF16)
        gib = gi_ref[...].astype(BF16)
        du_ref[...] = _dot_nt(grb, bre_ref[...]) + _dot_nt(gib, bim_ref[...])
        ub = u_ref[...].astype(BF16)
        dbr_ref[...] = _dot_tn(ub, grb)
        dbi_ref[...] = _dot_tn(ub, gib)

    chan, state, bmat, cmat, avec = _s5_specs()
    return pl.pallas_call(
        body, grid=(2, S5_CHUNKS), in_specs=[chan, chan, state, state, bmat, bmat, avec, avec, cmat, cmat],
        out_specs=[chan, bmat, bmat, cmat, cmat, avec, avec],
        out_shape=[SDS((2, SEQ, S5_WIDTH), F32)] + [SDS((2, S5_CHUNKS, CH_W, ST_W), F32)] * 2
                  + [SDS((2, S5_CHUNKS, ST_W, CH_W), F32)] * 2 + [SDS((2, S5_CHUNKS, 1, ST_W), F32)] * 2,
        scratch_shapes=[pltpu.VMEM((SEQ, ST_W), F32), pltpu.VMEM((SEQ, ST_W), F32)],
        name="s5_scan_bwd", compiler_params=_params(("parallel", "parallel")))(dy2, u2, sr, si, bre, bim, are, aim, cre, cim)


_GELU_K = math.sqrt(2.0 / math.pi)
_GELU_C = 0.044715


def _gelu(x):
    t = jnp.tanh(_GELU_K * (x + _GELU_C * x * x * x))
    return 0.5 * x * (1.0 + t), t


def _s5_glu_fwd(u, y0, y1, dskip, wglu, bglu):
    def body(u_ref, y0_ref, y1_ref, d_ref, w_ref, b_ref, o_ref, yp_ref):
        ypre = u_ref[...] * d_ref[...] + y0_ref[...] + y1_ref[...]
        yp_ref[...] = ypre
        y, _ = _gelu(ypre)
        z = _dot(y.astype(BF16), w_ref[...]) + b_ref[...]
        o_ref[...] = y * jax.nn.sigmoid(z)

    row = _row_spec(S5_WIDTH)
    vec = _fix_spec((1, S5_WIDTH))
    return pl.pallas_call(
        body, grid=(N_ROW_TILES,), in_specs=[row, row, row, vec, _fix_spec((S5_WIDTH, S5_WIDTH)), vec],
        out_specs=[row, row], out_shape=[SDS((SEQ, S5_WIDTH), F32)] * 2, name="s5_glu_fwd",
        compiler_params=_params(("parallel",)))(u, y0, y1, dskip, wglu, bglu)


def _s5_glu_bwd(do, ypre, u, dskip, wglu, bglu):
    def body(do_ref, yp_ref, u_ref, d_ref, w_ref, b_ref, dyp_ref, du_ref, dw_ref, db_ref, dd_ref):
        i = pl.program_id(0)
        ypre = yp_ref[...]
        y, t = _gelu(ypre)
        yb = y.astype(BF16)
        sg = jax.nn.sigmoid(_dot(yb, w_ref[...]) + b_ref[...])
        dov = do_ref[...]
        dz = dov * y * sg * (1.0 - sg)
        dzb = dz.astype(BF16)
        dy = dov * sg + _dot_nt(dzb, w_ref[...])
        dgelu = 0.5 * (1.0 + t) + 0.5 * ypre * (1.0 - t * t) * _GELU_K * (1.0 + 3.0 * _GELU_C * ypre * ypre)
        dyp = dy * dgelu
        dyp_ref[...] = dyp
        uv = u_ref[...]
        du_ref[...] = dyp * d_ref[...]

        @pl.when(i == 0)
        def _():
            dw_ref[...] = jnp.zeros_like(dw_ref)
            db_ref[...] = jnp.zeros_like(db_ref)
            dd_ref[...] = jnp.zeros_like(dd_ref)

        dw_ref[...] += _dot_tn(yb, dzb)
        db_ref[...] += jnp.sum(dz, axis=0, keepdims=True)
        dd_ref[...] += jnp.sum(dyp * uv, axis=0, keepdims=True)

    row = _row_spec(S5_WIDTH)
    vec = _fix_spec((1, S5_WIDTH))
    mat = _fix_spec((S5_WIDTH, S5_WIDTH))
    return pl.pallas_call(
        body, grid=(N_ROW_TILES,), in_specs=[row, row, row, vec, mat, vec], out_specs=[row, row, mat, vec, vec],
        out_shape=[SDS((SEQ, S5_WIDTH), F32)] * 2 + [SDS((S5_WIDTH, S5_WIDTH), F32), SDS((1, S5_WIDTH), F32), SDS((1, S5_WIDTH), F32)],
        name="s5_glu_bwd", compiler_params=_params(("arbitrary",)))(do, ypre, u, dskip, wglu, bglu)


def _mix_out_fwd(ona, os5, g_na, g_s5, wout):
    def body(a_ref, s_ref, ga_ref, gs_ref, w_ref, o_ref):
        av, sv = a_ref[...], s_ref[...]
        ca = (av * _rstd(av) * ga_ref[...]).astype(BF16)
        cs = (sv * _rstd(sv) * gs_ref[...]).astype(BF16)
        o_ref[...] = _dot(ca, w_ref[0:NA_WIDTH, :]) + _dot(cs, w_ref[NA_WIDTH:, :])

    row = _row_spec(NA_WIDTH)
    vec = _fix_spec((1, NA_WIDTH))
    return pl.pallas_call(
        body, grid=(N_ROW_TILES,), in_specs=[row, row, vec, vec, _fix_spec((D_MODEL, D_MODEL))],
        out_specs=_row_spec(D_MODEL), out_shape=SDS((SEQ, D_MODEL), F32), name="mix_out_fwd",
        compiler_params=_params(("parallel",)))(ona, os5, g_na, g_s5, wout)


def _mix_out_bwd(dmix, ona, os5, g_na, g_s5, wout):
    def body(dm_ref, a_ref, s_ref, ga_ref, gs_ref, w_ref, da_ref, ds_ref, dw_ref, dga_ref, dgs_ref):
        i = pl.program_id(0)
        dm = dm_ref[...]
        av, sv = a_ref[...], s_ref[...]
        ra, rs = _rstd(av), _rstd(sv)
        ga, gs = ga_ref[...], gs_ref[...]
        ca = (av * ra * ga).astype(BF16)
        cs = (sv * rs * gs).astype(BF16)
        dca = _dot_nt(dm, w_ref[0:NA_WIDTH, :])
        dcs = _dot_nt(dm, w_ref[NA_WIDTH:, :])
        da, dga = _rms_bwd(av, ra, ga, dca)
        ds, dgs = _rms_bwd(sv, rs, gs, dcs)
        da_ref[...] = da
        ds_ref[...] = ds

        @pl.when(i == 0)
        def _():
            dw_ref[...] = jnp.zeros_like(dw_ref)
            dga_ref[...] = jnp.zeros_like(dga_ref)
            dgs_ref[...] = jnp.zeros_like(dgs_ref)

        dw_ref[0:NA_WIDTH, :] += _dot_tn(ca, dm)
        dw_ref[NA_WIDTH:, :] += _dot_tn(cs, dm)
        dga_ref[...] += jnp.sum(dga, axis=0, keepdims=True)
        dgs_ref[...] += jnp.sum(dgs, axis=0, keepdims=True)

    row = _row_spec(NA_WIDTH)
    vec = _fix_spec((1, NA_WIDTH))
    mat = _fix_spec((D_MODEL, D_MODEL))
    return pl.pallas_call(
        body, grid=(N_ROW_TILES,), in_specs=[_row_spec(D_MODEL), row, row, vec, vec, mat],
        out_specs=[row, row, mat, vec, vec],
        out_shape=[SDS((SEQ, NA_WIDTH), F32)] * 2 + [SDS((D_MODEL, D_MODEL), F32), SDS((1, NA_WIDTH), F32), SDS((1, NA_WIDTH), F32)],
        name="mix_out_bwd", compiler_params=_params(("arbitrary",)))(dmix, ona, os5, g_na, g_s5, wout)


def _me():
    x, y, c = lax.axis_index("x"), lax.axis_index("y"), lax.axis_index("c")
    return x, y, c, 4 * x + 2 * y + c


def _peer(k):
    x, y, c, _ = _me()
    px = 1 - x if (k >> 2) & 1 else x
    py = 1 - y if (k >> 1) & 1 else y
    pc = 1 - c if k & 1 else c
    return (px, py, pc), 4 * px + 2 * py + pc


def _exchange(arrays, gather, name):
    n = len(arrays)

    def body(*refs):
        ins, outs = refs[:n], refs[n:2 * n]
        send_sems, recv_sems, local_sems = refs[2 * n:]
        _, _, _, me = _me()
        started = []
        for a in range(n):
            src_mine = ins[a] if gather else ins[a].at[me]
            local = pltpu.make_async_copy(src_mine, outs[a].at[me], local_sems.at[a])
            local.start()
            started.append(local)
        sends = []
        for k in range(1, N_DEV):
            peer, peer_idx = _peer(k)
            for a in range(n):
                src = ins[a] if gather else ins[a].at[peer_idx]
                cp = pltpu.make_async_remote_copy(src_ref=src, dst_ref=outs[a].at[me], send_sem=send_sems.at[a, k - 1],
                                                  recv_sem=recv_sems.at[a, k - 1], device_id=peer, device_id_type=MESH)
                cp.start()
                sends.append(cp)
        for k in range(1, N_DEV):
            peer, peer_idx = _peer(k)
            for a in range(n):
                src = ins[a] if gather else ins[a].at[peer_idx]
                pltpu.make_async_remote_copy(src_ref=src, dst_ref=outs[a].at[peer_idx], send_sem=send_sems.at[a, k - 1],
                                             recv_sem=recv_sems.at[a, k - 1], device_id=peer, device_id_type=MESH).wait_recv()
        for cp in sends:
            cp.wait_send()
        for local in started:
            local.wait()

    hbm = pl.BlockSpec(memory_space=pltpu.HBM)
    out_shape = [SDS((N_DEV,) + tuple(a.shape), a.dtype) if gather else SDS(a.shape, a.dtype) for a in arrays]
    return pl.pallas_call(
        body, in_specs=[hbm] * n, out_specs=[hbm] * n, out_shape=out_shape,
        scratch_shapes=[pltpu.SemaphoreType.DMA((n, N_DEV - 1)), pltpu.SemaphoreType.DMA((n, N_DEV - 1)),
                        pltpu.SemaphoreType.DMA((n,))],
        name=name)(*arrays)


def _adamw_math(w, g, m, v):
    m = ADAM_B1 * m + (1.0 - ADAM_B1) * g
    v = ADAM_B2 * v + (1.0 - ADAM_B2) * (g * g)
    m_hat = m / (1.0 - ADAM_B1 ** ADAM_STEP)
    v_hat = v / (1.0 - ADAM_B2 ** ADAM_STEP)
    delta = -ADAM_LR * (m_hat / (jnp.sqrt(v_hat) + ADAM_EPS) + ADAM_WD * w)
    return delta, m, v


def _adamw(w, m, v, pieces, name):
    rows, cols = w.shape
    tile = rows
    for cand in (256, 176, 128, 64, 16):
        if rows > cand and rows % cand == 0:
            tile = cand
            break

    def body(w_ref, m_ref, v_ref, p_ref, g_ref, d_ref, mo_ref, vo_ref):
        g = p_ref[0].astype(F32)
        for p in range(1, N_DEV):
            g = g + p_ref[p].astype(F32)
        g_ref[...] = g
        d_ref[...], mo_ref[...], vo_ref[...] = _adamw_math(w_ref[...], g, m_ref[...], v_ref[...])

    blk = pl.BlockSpec((tile, cols), lambda i: (i, 0))
    return pl.pallas_call(
        body, grid=(rows // tile,), in_specs=[blk, blk, blk, pl.BlockSpec((N_DEV, tile, cols), lambda i: (0, i, 0))],
        out_specs=[blk] * 4, out_shape=[SDS((rows, cols), F32)] * 4, name=name,
        compiler_params=_params(("parallel",)))(w, m, v, pieces)


def _perm_rows(x):
    return x.reshape(SCAN_BLOCKS, SCAN_T, x.shape[-1]).transpose(1, 0, 2).reshape(SEQ, x.shape[-1])


def _unperm_rows(x):
    return x.reshape(SCAN_T, SCAN_BLOCKS, x.shape[-1]).transpose(1, 0, 2).reshape(SEQ, x.shape[-1])


def _block_diag(x):
    eye = np.eye(8, dtype=bool)[None, None, :, None, :, None]
    full = jnp.where(eye, x[:, :, :, :, None, :], 0.0)
    return full.reshape(2, S5_CHUNKS, 8 * x.shape[3], 8 * x.shape[4])


def _diag_blocks(x, r, c):
    x6 = x.reshape(2, S5_CHUNKS, 8, r, 8, c)
    return jnp.stack([x6[:, :, g, :, g, :] for g in range(8)], axis=2)


def _local_step(x, target, meta, wts, small):
    h0 = jnp.concatenate([meta, x], axis=0)
    tgt = jnp.concatenate([jnp.zeros((N_META, D_MODEL), F32), target], axis=0)

    a1 = _prenorm(h0, small["ffn1_pre_g"])
    gate1, up1, f1 = _ffn_fwd(a1, wts["ffn1_w_gate"], wts["ffn1_w_up"], wts["ffn1_w_down"], "ffn1_fwd")
    h1, a2 = _post_pre(f1, h0, small["ffn1_post_g"], small["mix_pre_g"], 0.5, "post_pre1")
    proj = _proj_fwd(a2, wts["w_in"])
    qkv = proj[:6].reshape(3, 2, SEQ, 4, HEAD_DIM).transpose(0, 1, 3, 2, 4).reshape(3, HEADS, SEQ, HEAD_DIM)
    u = proj[6:].transpose(1, 0, 2).reshape(SEQ, S5_WIDTH)
    bias = _rpb_expand(small["na_rpb"][0])
    o3 = _na_fwd(qkv[0], qkv[1], qkv[2], bias)
    ona = o3.transpose(1, 0, 2).reshape(SEQ, NA_WIDTH)

    lr = small["s5_lam_re"].reshape(64, S5_STATE)
    li = small["s5_lam_im"].reshape(64, S5_STATE)
    logdt = small["s5_log_dt"].reshape(64, 1)
    b_t = [small[n].reshape(64, S5_STATE, S5_GROUP).transpose(0, 2, 1) for n in ("s5_b_re", "s5_b_im")]
    lbr, lbi, bbr, bbi = _s5_prep(lr, li, logdt, b_t[0], b_t[1])
    are = lbr.reshape(2, S5_CHUNKS, 1, ST_W)
    aim = lbi.reshape(2, S5_CHUNKS, 1, ST_W)
    bre = _block_diag(bbr.reshape(2, S5_CHUNKS, 8, S5_GROUP, S5_STATE)).astype(BF16)
    bim = _block_diag(bbi.reshape(2, S5_CHUNKS, 8, S5_GROUP, S5_STATE)).astype(BF16)
    c_t = [small[n].reshape(2, S5_CHUNKS, 8, S5_GROUP, S5_STATE).transpose(0, 1, 2, 4, 3) for n in ("s5_c_re", "s5_c_im")]
    cre = _block_diag(c_t[0]).astype(BF16)
    cim = _block_diag(c_t[1]).astype(BF16)
    u2 = jnp.stack([_perm_rows(u), _perm_rows(u[::-1])])
    sr, si, y2 = _s5_scan_fwd(u2, bre, bim, are, aim, cre, cim)
    y0 = _unperm_rows(y2[0])
    y1 = _unperm_rows(y2[1])[::-1]
    os5, ypre = _s5_glu_fwd(u, y0, y1, small["s5_d"], wts["s5_w_glu"], small["s5_b_glu"])

    mix = _mix_out_fwd(ona, os5, small["na_out_g"], small["s5_out_g"], wts["w_out"])
    h2, a3 = _post_pre(mix, h1, small["mix_post_g"], small["ffn2_pre_g"], 1.0, "post_pre2")
    gate2, up2, f2 = _ffn_fwd(a3, wts["ffn2_w_gate"], wts["ffn2_w_up"], wts["ffn2_w_down"], "ffn2_fwd")
    loss8, dh3, df2, g_final, g_ffn2_post = _final_loss(f2, h2, small["ffn2_post_g"], small["final_g"], tgt)

    da3, dwg2, dwu2, dwd2 = _ffn_bwd(df2, a3, gate2, up2, wts["ffn2_w_gate"], wts["ffn2_w_up"], wts["ffn2_w_down"], "ffn2_bwd")
    dh2, dmix, g_ffn2_pre, g_mix_post = _bwd_pre_post(da3, h2, small["ffn2_pre_g"], dh3, mix, small["mix_post_g"], 1.0, "bwd_pre_post2")
    dona, dos5, dwout, g_na_out, g_s5_out = _mix_out_bwd(dmix, ona, os5, small["na_out_g"], small["s5_out_g"], wts["w_out"])

    dypre, du_skip, dwglu, g_b_glu, g_s5_d = _s5_glu_bwd(dos5, ypre, u, small["s5_d"], wts["s5_w_glu"], small["s5_b_glu"])
    dy2 = jnp.stack([_perm_rows(dypre), _perm_rows(dypre[::-1])])
    du2, dbr, dbi, dcr, dci, dar, dai = _s5_scan_bwd(dy2, u2, sr, si, bre, bim, are, aim, cre, cim)
    du = du_skip + _unperm_rows(du2[0]) + _unperm_rows(du2[1])[::-1]
    dbbr = _diag_blocks(dbr, S5_GROUP, S5_STATE).reshape(64, S5_GROUP, S5_STATE)
    dbbi = _diag_blocks(dbi, S5_GROUP, S5_STATE).reshape(64, S5_GROUP, S5_STATE)
    g_lr, g_li, g_dt, g_br, g_bi = _s5_prep_bwd(lr, li, logdt, b_t[0], b_t[1], dar.reshape(64, S5_STATE),
                                                dai.reshape(64, S5_STATE), dbbr, dbbi)
    g_c = [_diag_blocks(d, S5_STATE, S5_GROUP).transpose(0, 1, 2, 4, 3).reshape(1, 2, S5_GROUPS, S5_GROUP, S5_STATE)
           for d in (dcr, dci)]

    do3 = dona.reshape(SEQ, HEADS, HEAD_DIM).transpose(1, 0, 2)
    dq, dk, dv, dbias = _na_bwd(qkv[0], qkv[1], qkv[2], bias, do3)
    g_rpb = _rpb_reduce(dbias)
    dqkv = jnp.stack([dq, dk, dv]).reshape(3, 2, 4, SEQ, HEAD_DIM).transpose(0, 1, 3, 2, 4).reshape(6, SEQ, IN_SHARD)
    dproj = jnp.concatenate([dqkv, du.reshape(SEQ, 2, IN_SHARD).transpose(1, 0, 2)], axis=0).astype(BF16)
    da2, dwin = _proj_bwd(dproj, a2, wts["w_in"])
    dh1, df1, g_mix_pre, g_ffn1_post = _bwd_pre_post(da2, h1, small["mix_pre_g"], dh2, f1, small["ffn1_post_g"], 0.5, "bwd_pre_post1")
    da1, dwg1, dwu1, dwd1 = _ffn_bwd(df1, a1, gate1, up1, wts["ffn1_w_gate"], wts["ffn1_w_up"], wts["ffn1_w_down"], "ffn1_bwd")
    dh0, g_ffn1_pre = _bwd_pre_only(da1, h0, small["ffn1_pre_g"], dh1)

    big = {
        "ffn1_w_gate": dwg1, "ffn1_w_up": dwu1, "ffn1_w_down": dwd1, "w_in": dwin,
        "s5_w_glu": dwglu.reshape(N_DEV, S5_WIDTH // N_DEV, S5_WIDTH).astype(BF16),
        "w_out": dwout.reshape(N_DEV, D_MODEL // N_DEV, D_MODEL).astype(BF16),
        "ffn2_w_gate": dwg2, "ffn2_w_up": dwu2, "ffn2_w_down": dwd2,
    }
    small_g = {
        "ffn1_pre_g": g_ffn1_pre, "ffn1_post_g": g_ffn1_post, "mix_pre_g": g_mix_pre,
        "na_rpb": g_rpb.reshape(1, HEADS, 2 * KH - 1, 2 * KW - 1),
        "s5_lam_re": g_lr.reshape(1, 2, S5_GROUPS, S5_STATE), "s5_lam_im": g_li.reshape(1, 2, S5_GROUPS, S5_STATE),
        "s5_log_dt": g_dt.reshape(1, 2, S5_GROUPS),
        "s5_b_re": g_br.transpose(0, 2, 1).reshape(1, 2, S5_GROUPS, S5_STATE, S5_GROUP),
        "s5_b_im": g_bi.transpose(0, 2, 1).reshape(1, 2, S5_GROUPS, S5_STATE, S5_GROUP),
        "s5_c_re": g_c[0], "s5_c_im": g_c[1], "s5_d": g_s5_d, "s5_b_glu": g_b_glu,
        "na_out_g": g_na_out, "s5_out_g": g_s5_out, "mix_post_g": g_mix_post,
        "ffn2_pre_g": g_ffn2_pre, "ffn2_post_g": g_ffn2_post, "final_g": g_final,
    }
    return loss8[0, 0], dh0[N_META:], dh0[:N_META], big, small_g


WEIGHT_NAMES = ['meta_tokens', 'ffn1_pre_g', 'ffn1_post_g', 'ffn1_w_gate', 'ffn1_w_up', 'ffn1_w_down', 'mix_pre_g', 'w_in',
                'na_rpb', 's5_lam_re', 's5_lam_im', 's5_log_dt', 's5_b_re', 's5_b_im', 's5_c_re', 's5_c_im', 's5_d',
                's5_w_glu', 's5_b_glu', 'na_out_g', 's5_out_g', 'w_out', 'mix_post_g', 'ffn2_pre_g', 'ffn2_post_g',
                'ffn2_w_gate', 'ffn2_w_up', 'ffn2_w_down', 'final_g']
BIG_NAMES = ['ffn1_w_gate', 'ffn1_w_up', 'ffn1_w_down', 'w_in', 's5_w_glu', 'w_out', 'ffn2_w_gate', 'ffn2_w_up', 'ffn2_w_down']
SMALL_NAMES = [n for n in WEIGHT_NAMES if n not in BIG_NAMES and n != 'meta_tokens']
PACK_LANES = 128
PACK_ALIGN = 256 * PACK_LANES


def _pack(parts):
    flat = jnp.concatenate([parts[n].reshape(-1).astype(F32) for n in SMALL_NAMES])
    pad = (-flat.shape[0]) % PACK_ALIGN
    return jnp.pad(flat, (0, pad)).reshape(-1, PACK_LANES)


def _unpack(packed, like):
    flat = packed.reshape(-1)
    out, pos = {}, 0
    for n in SMALL_NAMES:
        size = math.prod(like[n].shape)
        out[n] = flat[pos:pos + size].reshape(like[n].shape)
        pos += size
    return out


def kernel(x, meta_tokens, ffn1_pre_g, ffn1_post_g, ffn1_w_gate, ffn1_w_up, ffn1_w_down, mix_pre_g, w_in, na_rpb, s5_lam_re, s5_lam_im, s5_log_dt, s5_b_re, s5_b_im, s5_c_re, s5_c_im, s5_d, s5_w_glu, s5_b_glu, na_out_g, s5_out_g, w_out, mix_post_g, ffn2_pre_g, ffn2_post_g, ffn2_w_gate, ffn2_w_up, ffn2_w_down, final_g, loss_target, m_meta_tokens, m_ffn1_pre_g, m_ffn1_post_g, m_ffn1_w_gate, m_ffn1_w_up, m_ffn1_w_down, m_mix_pre_g, m_w_in, m_na_rpb, m_s5_lam_re, m_s5_lam_im, m_s5_log_dt, m_s5_b_re, m_s5_b_im, m_s5_c_re, m_s5_c_im, m_s5_d, m_s5_w_glu, m_s5_b_glu, m_na_out_g, m_s5_out_g, m_w_out, m_mix_post_g, m_ffn2_pre_g, m_ffn2_post_g, m_ffn2_w_gate, m_ffn2_w_up, m_ffn2_w_down, m_final_g, v_meta_tokens, v_ffn1_pre_g, v_ffn1_post_g, v_ffn1_w_gate, v_ffn1_w_up, v_ffn1_w_down, v_mix_pre_g, v_w_in, v_na_rpb, v_s5_lam_re, v_s5_lam_im, v_s5_log_dt, v_s5_b_re, v_s5_b_im, v_s5_c_re, v_s5_c_im, v_s5_d, v_s5_w_glu, v_s5_b_glu, v_na_out_g, v_s5_out_g, v_w_out, v_mix_post_g, v_ffn2_pre_g, v_ffn2_post_g, v_ffn2_w_gate, v_ffn2_w_up, v_ffn2_w_down, v_final_g):
    args = dict(locals())
    w = {n: args[n] for n in WEIGHT_NAMES}
    m = {n: args["m_" + n] for n in WEIGHT_NAMES}
    v = {n: args["v_" + n] for n in WEIGHT_NAMES}

    shards = [w[n][0].astype(BF16) for n in BIG_NAMES] + [w["meta_tokens"]]
    gathered = _exchange(shards, True, "gather_weights")
    wts = dict(zip(BIG_NAMES, gathered[:-1]))
    wts["s5_w_glu"] = wts["s5_w_glu"].reshape(S5_WIDTH, S5_WIDTH)
    wts["w_out"] = wts["w_out"].reshape(D_MODEL, D_MODEL)
    meta_full = gathered[-1].transpose(1, 0, 2).reshape(N_META, D_MODEL)
    small = {n: w[n] for n in SMALL_NAMES}

    loss_local, grad_x, gmeta, big, small_g = _local_step(x[0], loss_target[0], meta_full, wts, small)
    loss = lax.psum(loss_local, AXES)

    gmeta8 = gmeta.reshape(N_META, N_DEV, D_MODEL // N_DEV).transpose(1, 0, 2)
    pieces = _exchange([big[n] for n in BIG_NAMES] + [gmeta8], False, "scatter_grads")
    res = {}
    for n, p in zip(BIG_NAMES + ["meta_tokens"], pieces):
        shape = w[n].shape
        w2 = w[n].reshape(shape[-2], shape[-1])
        outs = _adamw(w2, m[n].reshape(w2.shape), v[n].reshape(w2.shape), p, "adamw_" + n)
        res[n] = [o.reshape(shape) for o in outs]

    packed_g = _exchange([_pack(small_g)], True, "gather_small_grads")[0]
    outs = _adamw(_pack(small), _pack({n: m[n] for n in SMALL_NAMES}), _pack({n: v[n] for n in SMALL_NAMES}), packed_g, "adamw_small")
    unpacked = [_unpack(o, small) for o in outs]
    for n in SMALL_NAMES:
        res[n] = [u_[n] for u_ in unpacked]

    out = [loss, grad_x[None]]
    for kind in range(4):
        out += [res[n][kind] for n in WEIGHT_NAMES]
    return tuple(out)
```

```python
import functools
import math

import numpy as np
import jax
import jax.numpy as jnp
from jax import lax
from jax.experimental import pallas as pl
from jax.experimental.pallas import tpu as pltpu

F32 = jnp.float32
BF16 = jnp.bfloat16
SDS = jax.ShapeDtypeStruct

D_MODEL = 1024
N_TOK = 2048
N_META = 16
SEQ = N_TOK + N_META
ROW_TILE = 688
N_ROW_TILES = SEQ // ROW_TILE
N_DEV = 8
D_FF = 2816
FF_SHARD = D_FF // N_DEV
IN_SHARD = 256
NA_WIDTH = 512
S5_WIDTH = 512
HEADS = 8
HEAD_DIM = 64
GRID_W = 64
GRID_ROWS = N_TOK // GRID_W
KH = 8
KW = 16
KEYS = KH * GRID_W
NA_UNROLL = 2
S5_GROUPS = 32
S5_GROUP = 16
S5_STATE = 64
S5_CHUNKS = 4
CH_W = S5_WIDTH // S5_CHUNKS
ST_W = S5_GROUPS * S5_STATE // S5_CHUNKS
SCAN_BLOCKS = 8
SCAN_T = SEQ // SCAN_BLOCKS
RMS_EPS = 1e-6
NEG_INF = -1e30
ATT_SCALE = HEAD_DIM ** -0.5
ADAM_LR, ADAM_B1, ADAM_B2, ADAM_EPS, ADAM_WD, ADAM_STEP = 0.001, 0.9, 0.999, 1e-08, 0.01, 10
VMEM_LIMIT = 56 * 1024 * 1024
MESH = pl.DeviceIdType.MESH
AXES = ("x", "y", "c")


def _params(sem=None):
    return pltpu.CompilerParams(dimension_semantics=sem, vmem_limit_bytes=VMEM_LIMIT)


def _dot(a, b):
    return jnp.dot(a, b, preferred_element_type=F32)


def _dot_nt(a, b):
    return lax.dot_general(a, b, (((1,), (1,)), ((), ())), preferred_element_type=F32)


def _dot_tn(a, b):
    return lax.dot_general(a, b, (((0,), (0,)), ((), ())), preferred_element_type=F32)


def _rstd(x):
    return lax.rsqrt(jnp.mean(x * x, axis=-1, keepdims=True) + RMS_EPS)


def _rms_bwd(x, r, g, dy):
    dyg = dy * g
    xr = x * r
    dx = r * (dyg - xr * jnp.mean(dyg * xr, axis=-1, keepdims=True))
    return dx, dy * xr


def _rows(i, size=ROW_TILE):
    return pl.ds(pl.multiple_of(i * size, 16), size)


def _row_spec(width):
    return pl.BlockSpec((ROW_TILE, width), lambda i: (i, 0))


def _fix_spec(shape):
    return pl.BlockSpec(shape, lambda i: (0,) * len(shape))


def _split3(x):
    hi = x.astype(BF16)
    r1 = x - hi.astype(F32)
    mid = r1.astype(BF16)
    lo = (r1 - mid.astype(F32)).astype(BF16)
    return hi, mid, lo


def _prenorm(x, g):
    def body(x_ref, g_ref, a_ref):
        xv = x_ref[...]
        a_ref[...] = (xv * _rstd(xv) * g_ref[...]).astype(BF16)

    return pl.pallas_call(
        body, grid=(N_ROW_TILES,), in_specs=[_row_spec(D_MODEL), _fix_spec((1, D_MODEL))],
        out_specs=_row_spec(D_MODEL), out_shape=SDS((SEQ, D_MODEL), BF16), name="prenorm",
        compiler_params=_params(("parallel",)))(x, g)


def _post_pre(f, hres, g_post, g_next, scale, name):
    def body(f_ref, h_ref, gp_ref, gn_ref, ho_ref, a_ref):
        fv = f_ref[...]
        h = h_ref[...] + scale * (fv * _rstd(fv) * gp_ref[...])
        ho_ref[...] = h
        a_ref[...] = (h * _rstd(h) * gn_ref[...]).astype(BF16)

    return pl.pallas_call(
        body, grid=(N_ROW_TILES,),
        in_specs=[_row_spec(D_MODEL), _row_spec(D_MODEL), _fix_spec((1, D_MODEL)), _fix_spec((1, D_MODEL))],
        out_specs=[_row_spec(D_MODEL), _row_spec(D_MODEL)],
        out_shape=[SDS((SEQ, D_MODEL), F32), SDS((SEQ, D_MODEL), BF16)], name=name,
        compiler_params=_params(("parallel",)))(f, hres, g_post, g_next)


def _final_loss(f2, h2, g_post, g_final, target):
    def body(f_ref, h_ref, gp_ref, gf_ref, t_ref, loss_ref, dh_ref, df_ref, dgf_ref, dgp_ref):
        i = pl.program_id(0)
        fv = f_ref[...]
        r1 = _rstd(fv)
        gp = gp_ref[...]
        h3 = h_ref[...] + 0.5 * (fv * r1 * gp)
        r2 = _rstd(h3)
        gf = gf_ref[...]
        y = h3 * r2 * gf
        row = lax.broadcasted_iota(jnp.int32, (ROW_TILE, 1), 0) + i * ROW_TILE
        err = jnp.where(row >= N_META, y - t_ref[...], 0.0)
        part = 0.5 * jnp.sum(jnp.mean(err * err, axis=-1, keepdims=True))
        dy = err * (1.0 / D_MODEL)
        dh3, dgf = _rms_bwd(h3, r2, gf, dy)
        dh_ref[...] = dh3
        df, dgp = _rms_bwd(fv, r1, gp, 0.5 * dh3)
        df_ref[...] = df.astype(BF16)

        @pl.when(i == 0)
        def _():
            loss_ref[...] = jnp.zeros_like(loss_ref)
            dgf_ref[...] = jnp.zeros_like(dgf_ref)
            dgp_ref[...] = jnp.zeros_like(dgp_ref)

        loss_ref[...] += part
        dgf_ref[...] += jnp.sum(dgf, axis=0, keepdims=True)
        dgp_ref[...] += jnp.sum(dgp, axis=0, keepdims=True)

    gain = _fix_spec((1, D_MODEL))
    return pl.pallas_call(
        body, grid=(N_ROW_TILES,),
        in_specs=[_row_spec(D_MODEL), _row_spec(D_MODEL), gain, gain, _row_spec(D_MODEL)],
        out_specs=[_fix_spec((8, 128)), _row_spec(D_MODEL), _row_spec(D_MODEL), gain, gain],
        out_shape=[SDS((8, 128), F32), SDS((SEQ, D_MODEL), F32), SDS((SEQ, D_MODEL), BF16),
                   SDS((1, D_MODEL), F32), SDS((1, D_MODEL), F32)],
        name="final_loss", compiler_params=_params(("arbitrary",)))(f2, h2, g_post, g_final, target)


def _bwd_pre_post(da, h, g_pre, dh_res, fprev, g_post, scale, name):
    def body(da_ref, h_ref, gpre_ref, dhr_ref, f_ref, gpost_ref, dh_ref, df_ref, dgpre_ref, dgpost_ref):
        i = pl.program_id(0)
        hv = h_ref[...]
        dxa, dgpre = _rms_bwd(hv, _rstd(hv), gpre_ref[...], da_ref[...])
        dh = dhr_ref[...] + dxa
        dh_ref[...] = dh
        fv = f_ref[...]
        df, dgpost = _rms_bwd(fv, _rstd(fv), gpost_ref[...], scale * dh)
        df_ref[...] = df.astype(BF16)

        @pl.when(i == 0)
        def _():
            dgpre_ref[...] = jnp.zeros_like(dgpre_ref)
            dgpost_ref[...] = jnp.zeros_like(dgpost_ref)

        dgpre_ref[...] += jnp.sum(dgpre, axis=0, keepdims=True)
        dgpost_ref[...] += jnp.sum(dgpost, axis=0, keepdims=True)

    gain = _fix_spec((1, D_MODEL))
    row = _row_spec(D_MODEL)
    return pl.pallas_call(
        body, grid=(N_ROW_TILES,), in_specs=[row, row, gain, row, row, gain],
        out_specs=[row, row, gain, gain],
        out_shape=[SDS((SEQ, D_MODEL), F32), SDS((SEQ, D_MODEL), BF16), SDS((1, D_MODEL), F32), SDS((1, D_MODEL), F32)],
        name=name, compiler_params=_params(("arbitrary",)))(da, h, g_pre, dh_res, fprev, g_post)


def _bwd_pre_only(da, h, g_pre, dh_res):
    def body(da_ref, h_ref, gpre_ref, dhr_ref, dh_ref, dgpre_ref):
        i = pl.program_id(0)
        hv = h_ref[...]
        dxa, dgpre = _rms_bwd(hv, _rstd(hv), gpre_ref[...], da_ref[...])
        dh_ref[...] = dhr_ref[...] + dxa

        @pl.when(i == 0)
        def _():
            dgpre_ref[...] = jnp.zeros_like(dgpre_ref)

        dgpre_ref[...] += jnp.sum(dgpre, axis=0, keepdims=True)

    gain = _fix_spec((1, D_MODEL))
    row = _row_spec(D_MODEL)
    return pl.pallas_call(
        body, grid=(N_ROW_TILES,), in_specs=[row, row, gain, row], out_specs=[row, gain],
        out_shape=[SDS((SEQ, D_MODEL), F32), SDS((1, D_MODEL), F32)],
        name="bwd_pre_only", compiler_params=_params(("arbitrary",)))(da, h, g_pre, dh_res)


def _ffn_fwd(a, wg, wu, wd, name):
    def body(a_ref, wg_ref, wu_ref, wd_ref, gate_ref, up_ref, f_ref):
        j = pl.program_id(0)

        def tile(i, carry):
            rows = _rows(i)
            at = a_ref[rows, :]
            gate = _dot(at, wg_ref[...])
            up = _dot(at, wu_ref[...])
            gate_ref[rows, :] = gate
            up_ref[rows, :] = up
            act = (gate * jax.nn.sigmoid(gate) * up).astype(BF16)
            contrib = _dot(act, wd_ref[...])

            @pl.when(j == 0)
            def _():
                f_ref[rows, :] = contrib

            @pl.when(j != 0)
            def _():
                f_ref[rows, :] += contrib

            return carry

        lax.fori_loop(0, N_ROW_TILES, tile, 0)

    shard_cols = pl.BlockSpec((None, D_MODEL, FF_SHARD), lambda j: (j, 0, 0))
    shard_rows = pl.BlockSpec((None, FF_SHARD, D_MODEL), lambda j: (j, 0, 0))
    hid = pl.BlockSpec((None, SEQ, FF_SHARD), lambda j: (j, 0, 0))
    full = pl.BlockSpec((SEQ, D_MODEL), lambda j: (0, 0))
    return pl.pallas_call(
        body, grid=(N_DEV,), in_specs=[full, shard_cols, shard_cols, shard_rows], out_specs=[hid, hid, full],
        out_shape=[SDS((N_DEV, SEQ, FF_SHARD), F32), SDS((N_DEV, SEQ, FF_SHARD), F32), SDS((SEQ, D_MODEL), F32)],
        name=name, compiler_params=_params(("arbitrary",)))(a, wg, wu, wd)


def _ffn_bwd(df, a, gate, up, wg, wu, wd, name):
    def body(df_ref, a_ref, gate_ref, up_ref, wg_ref, wu_ref, wd_ref, da_ref, dwg_ref, dwu_ref, dwd_ref,
             acc_g, acc_u, acc_d):
        j = pl.program_id(0)

        def tile(i, carry):
            rows = _rows(i)
            dft = df_ref[rows, :]
            at = a_ref[rows, :]
            gate = gate_ref[rows, :]
            up = up_ref[rows, :]
            dact = _dot_nt(dft, wd_ref[...])
            sig = jax.nn.sigmoid(gate)
            silu = gate * sig
            dgate = (dact * up * (sig * (1.0 + gate * (1.0 - sig)))).astype(BF16)
            dup = (dact * silu).astype(BF16)
            act = (silu * up).astype(BF16)
            dwd = _dot_tn(act, dft)
            dwg = _dot_tn(at, dgate)
            dwu = _dot_tn(at, dup)
            dat = _dot_nt(dgate, wg_ref[...]) + _dot_nt(dup, wu_ref[...])

            @pl.when(i == 0)
            def _():
                acc_d[...] = dwd
                acc_g[...] = dwg
                acc_u[...] = dwu

            @pl.when(i != 0)
            def _():
                acc_d[...] += dwd
                acc_g[...] += dwg
                acc_u[...] += dwu

            @pl.when(j == 0)
            def _():
                da_ref[rows, :] = dat

            @pl.when(j != 0)
            def _():
                da_ref[rows, :] += dat

            return carry

        lax.fori_loop(0, N_ROW_TILES, tile, 0)
        dwg_ref[...] = acc_g[...].astype(BF16)
        dwu_ref[...] = acc_u[...].astype(BF16)
        dwd_ref[...] = acc_d[...].astype(BF16)

    shard_cols = pl.BlockSpec((None, D_MODEL, FF_SHARD), lambda j: (j, 0, 0))
    shard_rows = pl.BlockSpec((None, FF_SHARD, D_MODEL), lambda j: (j, 0, 0))
    hid = pl.BlockSpec((None, SEQ, FF_SHARD), lambda j: (j, 0, 0))
    full = pl.BlockSpec((SEQ, D_MODEL), lambda j: (0, 0))
    return pl.pallas_call(
        body, grid=(N_DEV,), in_specs=[full, full, hid, hid, shard_cols, shard_cols, shard_rows],
        out_specs=[full, shard_cols, shard_cols, shard_rows],
        out_shape=[SDS((SEQ, D_MODEL), F32), SDS((N_DEV, D_MODEL, FF_SHARD), BF16),
                   SDS((N_DEV, D_MODEL, FF_SHARD), BF16), SDS((N_DEV, FF_SHARD, D_MODEL), BF16)],
        scratch_shapes=[pltpu.VMEM((D_MODEL, FF_SHARD), F32), pltpu.VMEM((D_MODEL, FF_SHARD), F32),
                        pltpu.VMEM((FF_SHARD, D_MODEL), F32)],
        name=name, compiler_params=_params(("arbitrary",)))(df, a, gate, up, wg, wu, wd)


def _proj_fwd(a, w):
    def body(a_ref, w_ref, o_ref):
        def tile(i, carry):
            rows = _rows(i)
            o_ref[rows, :] = _dot(a_ref[rows, :], w_ref[...])
            return carry

        lax.fori_loop(0, N_ROW_TILES, tile, 0)

    return pl.pallas_call(
        body, grid=(N_DEV,),
        in_specs=[pl.BlockSpec((SEQ, D_MODEL), lambda j: (0, 0)), pl.BlockSpec((None, D_MODEL, IN_SHARD), lambda j: (j, 0, 0))],
        out_specs=pl.BlockSpec((None, SEQ, IN_SHARD), lambda j: (j, 0, 0)),
        out_shape=SDS((N_DEV, SEQ, IN_SHARD), F32), name="proj_fwd",
        compiler_params=_params(("parallel",)))(a, w)


def _proj_bwd(dproj, a, w):
    def body(dp_ref, a_ref, w_ref, da_ref, dw_ref, acc):
        j = pl.program_id(0)

        def tile(i, carry):
            rows = _rows(i)
            dpt = dp_ref[rows, :]
            dw = _dot_tn(a_ref[rows, :], dpt)
            dat = _dot_nt(dpt, w_ref[...])

            @pl.when(i == 0)
            def _():
                acc[...] = dw

            @pl.when(i != 0)
            def _():
                acc[...] += dw

            @pl.when(j == 0)
            def _():
                da_ref[rows, :] = dat

            @pl.when(j != 0)
            def _():
                da_ref[rows, :] += dat

            return carry

        lax.fori_loop(0, N_ROW_TILES, tile, 0)
        dw_ref[...] = acc[...].astype(BF16)

    full = pl.BlockSpec((SEQ, D_MODEL), lambda j: (0, 0))
    wspec = pl.BlockSpec((None, D_MODEL, IN_SHARD), lambda j: (j, 0, 0))
    return pl.pallas_call(
        body, grid=(N_DEV,),
        in_specs=[pl.BlockSpec((None, SEQ, IN_SHARD), lambda j: (j, 0, 0)), full, wspec],
        out_specs=[full, wspec],
        out_shape=[SDS((SEQ, D_MODEL), F32), SDS((N_DEV, D_MODEL, IN_SHARD), BF16)],
        scratch_shapes=[pltpu.VMEM((D_MODEL, IN_SHARD), F32)],
        name="proj_bwd", compiler_params=_params(("arbitrary",)))(dproj, a, w)


def _na_consts():
    c = np.arange(GRID_W)
    col_start = np.clip(c - KW // 2, 0, GRID_W - KW)
    col_in = (c[None, :] >= col_start[:, None]) & (c[None, :] < col_start[:, None] + KW)
    dc = np.clip(c[None, :] - c[:, None] + KW - 1, 0, 2 * KW - 2)
    onehot = np.zeros((128, GRID_W * GRID_W), np.float32)
    qq, kk = np.meshgrid(c, c, indexing="ij")
    onehot[dc[col_in], (qq * GRID_W + kk)[col_in]] = 1.0
    negmask = np.where(col_in, 0.0, NEG_INF).astype(np.float32).reshape(1, -1)
    sel = np.zeros((16, 64), np.float32)
    for off in range(8):
        for kh in range(KH):
            sel[off + kh, off * 8 + kh] = 1.0
    return onehot, negmask, sel


def _rpb_expand(rpb):
    onehot, negmask, _ = _na_consts()
    rows = HEADS * (2 * KH - 1)
    rpb_pad = jnp.pad(rpb.reshape(rows, 2 * KW - 1), ((0, 128 - rows), (0, 128 - (2 * KW - 1))))

    def body(r_ref, oh_ref, m_ref, t_ref):
        hi, mid, lo = _split3(r_ref[...])
        oh = oh_ref[...]
        t_ref[...] = _dot(hi, oh) + _dot(mid, oh) + _dot(lo, oh) + m_ref[...]

    table = pl.pallas_call(body, out_shape=SDS((128, GRID_W * GRID_W), F32), name="rpb_expand",
                           compiler_params=_params())(rpb_pad, jnp.asarray(onehot, BF16), jnp.asarray(negmask))
    t4 = table[:rows].reshape(HEADS, 2 * KH - 1, GRID_W, GRID_W)
    per_off = jnp.stack([t4[:, o:o + KH] for o in range(8)], axis=1)
    return per_off.transpose(0, 1, 3, 2, 4).reshape(HEADS, 8, GRID_W, KEYS)


def _rpb_reduce(dbias):
    onehot, _, sel = _na_consts()
    x = dbias.reshape(HEADS, 8, GRID_W, KH, GRID_W).transpose(0, 1, 3, 2, 4).reshape(HEADS, 64, GRID_W * GRID_W)

    def body(x_ref, oht_ref, sel_ref, o_ref):
        hi, mid, lo = _split3(x_ref[...])
        oht = oht_ref[...]
        y = _dot(hi, oht) + _dot(mid, oht) + _dot(lo, oht)
        hi, mid, lo = _split3(y)
        s = sel_ref[...]
        o_ref[...] = _dot(s, hi) + _dot(s, mid) + _dot(s, lo)

    return pl.pallas_call(
        body, grid=(HEADS,),
        in_specs=[pl.BlockSpec((None, 64, GRID_W * GRID_W), lambda h: (h, 0, 0)),
                  pl.BlockSpec((GRID_W * GRID_W, 128), lambda h: (0, 0)), pl.BlockSpec((16, 64), lambda h: (0, 0))],
        out_specs=pl.BlockSpec((None, 16, 128), lambda h: (h, 0, 0)),
        out_shape=SDS((HEADS, 16, 128), F32), name="rpb_reduce",
        compiler_params=_params(("parallel",)))(x, jnp.asarray(onehot.T, BF16), jnp.asarray(sel, BF16))


def _block_geometry(r):
    row_start = jnp.clip(r - KH // 2, 0, GRID_ROWS - KH)
    off = row_start - r + (KH - 1)
    q0 = pl.multiple_of(N_META + r * GRID_W, 16)
    k0 = pl.multiple_of(N_META + row_start * GRID_W, 16)
    return off, q0, k0


def _na_probs(q, kk, km, bias):
    s = _dot_nt(q, kk) * ATT_SCALE + bias
    sm = _dot_nt(q, km) * ATT_SCALE
    m = jnp.maximum(jnp.max(s, axis=-1, keepdims=True), jnp.max(sm, axis=-1, keepdims=True))
    p = jnp.exp(s - m)
    pm = jnp.exp(sm - m)
    inv = 1.0 / (jnp.sum(p, axis=-1, keepdims=True) + jnp.sum(pm, axis=-1, keepdims=True))
    return p * inv, pm * inv


def _meta_probs(qm, km):
    s = _dot_nt(qm, km) * ATT_SCALE
    p = jnp.exp(s - jnp.max(s, axis=-1, keepdims=True))
    return p / jnp.sum(p, axis=-1, keepdims=True)


def _na_fwd(q, k, v, bias):
    def body(q_ref, k_ref, v_ref, b_ref, o_ref):
        km = k_ref[0:N_META, :].astype(BF16)
        vm = v_ref[0:N_META, :].astype(BF16)
        pmm = _meta_probs(q_ref[0:N_META, :].astype(BF16), km)
        o_ref[0:N_META, :] = _dot(pmm.astype(BF16), vm)

        def block(r, carry):
            off, q0, k0 = _block_geometry(r)
            qb = q_ref[pl.ds(q0, GRID_W), :].astype(BF16)
            kk = k_ref[pl.ds(k0, KEYS), :].astype(BF16)
            vv = v_ref[pl.ds(k0, KEYS), :].astype(BF16)
            p, pm = _na_probs(qb, kk, km, b_ref[off])
            o_ref[pl.ds(q0, GRID_W), :] = _dot(p.astype(BF16), vv) + _dot(pm.astype(BF16), vm)
            return carry

        lax.fori_loop(0, GRID_ROWS, block, 0, unroll=NA_UNROLL)

    head = pl.BlockSpec((None, SEQ, HEAD_DIM), lambda h: (h, 0, 0))
    return pl.pallas_call(
        body, grid=(HEADS,), in_specs=[head, head, head, pl.BlockSpec((None, 8, GRID_W, KEYS), lambda h: (h, 0, 0, 0))],
        out_specs=head, out_shape=SDS((HEADS, SEQ, HEAD_DIM), F32), name="na_fwd",
        compiler_params=_params(("parallel",)))(q, k, v, bias)


def _na_bwd(q, k, v, bias, do):
    def body(q_ref, k_ref, v_ref, b_ref, do_ref, dq_ref, dk_ref, dv_ref, db_ref):
        km = k_ref[0:N_META, :].astype(BF16)
        vm = v_ref[0:N_META, :].astype(BF16)
        dk_ref[...] = jnp.zeros_like(dk_ref)
        dv_ref[...] = jnp.zeros_like(dv_ref)
        db_ref[...] = jnp.zeros_like(db_ref)

        qm = q_ref[0:N_META, :].astype(BF16)
        dom = do_ref[0:N_META, :].astype(BF16)
        pmm = _meta_probs(qm, km)
        dpm = _dot_nt(dom, vm)
        dsm = (pmm * (dpm - jnp.sum(pmm * dpm, axis=-1, keepdims=True)) * ATT_SCALE).astype(BF16)
        dq_ref[0:N_META, :] = _dot(dsm, km)
        dkm0 = _dot_tn(dsm, qm)
        dvm0 = _dot_tn(pmm.astype(BF16), dom)

        def block(r, carry):
            dkm, dvm = carry
            off, q0, k0 = _block_geometry(r)
            qb = q_ref[pl.ds(q0, GRID_W), :].astype(BF16)
            kk = k_ref[pl.ds(k0, KEYS), :].astype(BF16)
            vv = v_ref[pl.ds(k0, KEYS), :].astype(BF16)
            dob = do_ref[pl.ds(q0, GRID_W), :].astype(BF16)
            p, pm = _na_probs(qb, kk, km, b_ref[off])
            dp = _dot_nt(dob, vv)
            dpm_ = _dot_nt(dob, vm)
            delta = jnp.sum(p * dp, axis=-1, keepdims=True) + jnp.sum(pm * dpm_, axis=-1, keepdims=True)
            ds = p * (dp - delta)
            dsm_ = pm * (dpm_ - delta)
            db_ref[off] += ds
            dsb = (ds * ATT_SCALE).astype(BF16)
            dsmb = (dsm_ * ATT_SCALE).astype(BF16)
            dq_ref[pl.ds(q0, GRID_W), :] = _dot(dsb, kk) + _dot(dsmb, km)
            dk_ref[pl.ds(k0, KEYS), :] += _dot_tn(dsb, qb)
            dv_ref[pl.ds(k0, KEYS), :] += _dot_tn(p.astype(BF16), dob)
            return dkm + _dot_tn(dsmb, qb), dvm + _dot_tn(pm.astype(BF16), dob)

        dkm, dvm = lax.fori_loop(0, GRID_ROWS, block, (dkm0, dvm0), unroll=NA_UNROLL)
        dk_ref[0:N_META, :] = dkm
        dv_ref[0:N_META, :] = dvm

    head = pl.BlockSpec((None, SEQ, HEAD_DIM), lambda h: (h, 0, 0))
    bspec = pl.BlockSpec((None, 8, GRID_W, KEYS), lambda h: (h, 0, 0, 0))
    return pl.pallas_call(
        body, grid=(HEADS,), in_specs=[head, head, head, bspec, head], out_specs=[head, head, head, bspec],
        out_shape=[SDS((HEADS, SEQ, HEAD_DIM), F32)] * 3 + [SDS((HEADS, 8, GRID_W, KEYS), F32)],
        name="na_bwd", compiler_params=_params(("parallel",)))(q, k, v, bias, do)


def _cmul(ar, ai, br, bi):
    return ar * br - ai * bi, ar * bi + ai * br


def _cpow(ar, ai, n):
    rr, ri = None, None
    br, bi = ar, ai
    while n:
        if n & 1:
            rr, ri = (br, bi) if rr is None else _cmul(rr, ri, br, bi)
        n >>= 1
        if n:
            br, bi = _cmul(br, bi, br, bi)
    return rr, ri


def _s5_prep(lr, li, logdt, bre, bim):
    def body(lr_ref, li_ref, dt_ref, br_ref, bi_ref, lbr_ref, lbi_ref, bbr_ref, bbi_ref):
        lr_, li_ = lr_ref[...], li_ref[...]
        dt = jnp.exp(dt_ref[...])
        mag = jnp.exp(lr_ * dt)
        lbr = mag * jnp.cos(li_ * dt)
        lbi = mag * jnp.sin(li_ * dt)
        lbr_ref[...] = lbr
        lbi_ref[...] = lbi
        den = lr_ * lr_ + li_ * li_
        xr = lbr - 1.0
        cr = (xr * lr_ + lbi * li_) / den
        ci = (lbi * lr_ - xr * li_) / den
        br, bi = br_ref[...], bi_ref[...]
        bbr_ref[...] = cr[:, None, :] * br - ci[:, None, :] * bi
        bbi_ref[...] = cr[:, None, :] * bi + ci[:, None, :] * br

    n = 2 * S5_GROUPS
    return pl.pallas_call(
        body, out_shape=[SDS((n, S5_STATE), F32)] * 2 + [SDS((n, S5_GROUP, S5_STATE), F32)] * 2,
        name="s5_prep", compiler_params=_params())(lr, li, logdt, bre, bim)


def _s5_prep_bwd(lr, li, logdt, bre, bim, dar, dai, dbbr, dbbi):
    def body(lr_ref, li_ref, dt_ref, br_ref, bi_ref, dar_ref, dai_ref, dbr_ref, dbi_ref,
             glr_ref, gli_ref, gdt_ref, gbr_ref, gbi_ref):
        lr_, li_ = lr_ref[...], li_ref[...]
        dt = jnp.exp(dt_ref[...])
        mag = jnp.exp(lr_ * dt)
        lbr = mag * jnp.cos(li_ * dt)
        lbi = mag * jnp.sin(li_ * dt)
        den = lr_ * lr_ + li_ * li_
        xr = lbr - 1.0
        cr = (xr * lr_ + lbi * li_) / den
        ci = (lbi * lr_ - xr * li_) / den
        br, bi = br_ref[...], bi_ref[...]
        dbr, dbi = dbr_ref[...], dbi_ref[...]
        gbr_ref[...] = cr[:, None, :] * dbr + ci[:, None, :] * dbi
        gbi_ref[...] = cr[:, None, :] * dbi - ci[:, None, :] * dbr
        gcr = jnp.sum(dbr * br + dbi * bi, axis=1)
        gci = jnp.sum(dbi * br - dbr * bi, axis=1)
        ilr, ili = lr_ / den, li_ / den
        tr, ti = _cmul(gcr, gci, ilr, ili)
        glbr = dar_ref[...] + tr
        glbi = dai_ref[...] + ti
        dr_, di_ = _cmul(tr, ti, cr, -ci)
        gwr, gwi = _cmul(glbr, glbi, lbr, -lbi)
        glr_ref[...] = gwr * dt - dr_
        gli_ref[...] = gwi * dt - di_
        gdt_ref[...] = jnp.sum(gwr * lr_ + gwi * li_, axis=-1, keepdims=True) * dt

    n = 2 * S5_GROUPS
    return pl.pallas_call(
        body, out_shape=[SDS((n, S5_STATE), F32)] * 2 + [SDS((n, 1), F32)] + [SDS((n, S5_GROUP, S5_STATE), F32)] * 2,
        name="s5_prep_bwd", compiler_params=_params())(lr, li, logdt, bre, bim, dar, dai, dbbr, dbbi)


def _scan_local(xr_ref, xi_ref, ar8, ai8, reverse):
    def step(i, carry):
        sr, si = carry
        idx = (SCAN_T - 1 - i) if reverse else i
        rows = pl.ds(pl.multiple_of(idx * SCAN_BLOCKS, SCAN_BLOCKS), SCAN_BLOCKS)
        nr = ar8 * sr - ai8 * si + xr_ref[rows, :]
        ni = ar8 * si + ai8 * sr + xi_ref[rows, :]
        xr_ref[rows, :] = nr
        xi_ref[rows, :] = ni
        return nr, ni

    z = jnp.zeros(ar8.shape, F32)
    return lax.fori_loop(0, SCAN_T, step, (z, z))


def _scan_carries(er, ei, atr, ati, reverse):
    row = lax.broadcasted_iota(jnp.int32, er.shape, 0)
    cr = jnp.zeros((1, er.shape[1]), F32)
    ci = cr
    outr = jnp.zeros(er.shape, F32)
    outi = outr
    order = range(SCAN_BLOCKS - 1, -1, -1) if reverse else range(SCAN_BLOCKS)
    for b in order:
        outr = jnp.where(row == b, cr, outr)
        outi = jnp.where(row == b, ci, outi)
        nr, ni = _cmul(atr, ati, cr, ci)
        cr, ci = nr + er[b:b + 1, :], ni + ei[b:b + 1, :]
    return outr, outi


def _scan_fixup(xr_ref, xi_ref, cr8, ci8, ar8, ai8, reverse):
    def step(i, carry):
        pr, pi = carry
        idx = (SCAN_T - 1 - i) if reverse else i
        rows = pl.ds(pl.multiple_of(idx * SCAN_BLOCKS, SCAN_BLOCKS), SCAN_BLOCKS)
        fr, fi = _cmul(pr, pi, cr8, ci8)
        xr_ref[rows, :] += fr
        xi_ref[rows, :] += fi
        return _cmul(pr, pi, ar8, ai8)

    lax.fori_loop(0, SCAN_T, step, (ar8, ai8))


def _scan(xr_ref, xi_ref, ar, ai, reverse):
    n = ar.shape[1]
    ar8 = jnp.broadcast_to(ar, (SCAN_BLOCKS, n))
    ai8 = jnp.broadcast_to(ai, (SCAN_BLOCKS, n))
    er, ei = _scan_local(xr_ref, xi_ref, ar8, ai8, reverse)
    atr, ati = _cpow(ar, ai, SCAN_T)
    cr8, ci8 = _scan_carries(er, ei, atr, ati, reverse)
    _scan_fixup(xr_ref, xi_ref, cr8, ci8, ar8, ai8, reverse)


def _s5_specs():
    chan = pl.BlockSpec((SEQ, CH_W), lambda c, d: (0, c))
    chan2 = pl.BlockSpec((None, SEQ, CH_W), lambda c, d: (d, 0, c))
    state = pl.BlockSpec((None, SEQ, ST_W), lambda c, d: (d, 0, c))
    bmat = pl.BlockSpec((None, None, CH_W, ST_W), lambda c, d: (d, c, 0, 0))
    cmat = pl.BlockSpec((None, None, ST_W, CH_W), lambda c, d: (d, c, 0, 0))
    avec = pl.BlockSpec((None, None, 1, ST_W), lambda c, d: (d, c, 0, 0))
    return chan, chan2, state, bmat, cmat, avec


def _scan_by_direction(xr_ref, xi_ref, ar, ai, d, adjoint):
    @pl.when(d == 0)
    def _():
        _scan(xr_ref, xi_ref, ar, ai, reverse=adjoint)

    @pl.when(d == 1)
    def _():
        _scan(xr_ref, xi_ref, ar, ai, reverse=not adjoint)


def _s5_scan_fwd(u, bre, bim, are, aim, cre, cim):
    def body(u_ref, bre_ref, bim_ref, are_ref, aim_ref, cre_ref, cim_ref, sr_ref, si_ref, y_ref):
        ub = u_ref[...].astype(BF16)
        sr_ref[...] = _dot(ub, bre_ref[...])
        si_ref[...] = _dot(ub, bim_ref[...])
        _scan_by_direction(sr_ref, si_ref, are_ref[...], aim_ref[...], pl.program_id(1), adjoint=False)
        y_ref[...] = _dot(sr_ref[...].astype(BF16), cre_ref[...]) - _dot(si_ref[...].astype(BF16), cim_ref[...])

    chan, chan2, state, bmat, cmat, avec = _s5_specs()
    return pl.pallas_call(
        body, grid=(S5_CHUNKS, 2), in_specs=[chan, bmat, bmat, avec, avec, cmat, cmat], out_specs=[state, state, chan2],
        out_shape=[SDS((2, SEQ, S5_GROUPS * S5_STATE), F32)] * 2 + [SDS((2, SEQ, S5_WIDTH), F32)],
        name="s5_scan_fwd", compiler_params=_params(("parallel", "parallel")))(u, bre, bim, are, aim, cre, cim)


def _dlam(gr_ref, gi_ref, sr_ref, si_ref, reverse):
    tile = lambda i: pl.ds(pl.multiple_of(i * SCAN_BLOCKS, SCAN_BLOCKS), SCAN_BLOCKS)
    row = lax.broadcasted_iota(jnp.int32, (SCAN_BLOCKS, ST_W), 0)
    if reverse:
        edge, src, shift, empty, lo, hi, dprev = SCAN_T - 1, 0, SCAN_BLOCKS - 1, SCAN_BLOCKS - 1, 0, SCAN_T - 1, 1
    else:
        edge, src, shift, empty, lo, hi, dprev = 0, SCAN_T - 1, 1, 0, 1, SCAN_T, -1
    spr = jnp.where(row == empty, 0.0, pltpu.roll(sr_ref[tile(src), :], shift, 0))
    spi = jnp.where(row == empty, 0.0, pltpu.roll(si_ref[tile(src), :], shift, 0))
    acc0 = _cmul(gr_ref[tile(edge), :], gi_ref[tile(edge), :], spr, -spi)

    def step(i, carry):
        accr, acci = carry
        pr, pi = _cmul(gr_ref[tile(i), :], gi_ref[tile(i), :], sr_ref[tile(i + dprev), :], -si_ref[tile(i + dprev), :])
        return accr + pr, acci + pi

    accr, acci = lax.fori_loop(lo, hi, step, acc0)
    return jnp.sum(accr, axis=0, keepdims=True), jnp.sum(acci, axis=0, keepdims=True)


def _s5_scan_bwd(dy, du_skip, u, sr, si, bre, bim, are, aim, cre, cim):
    def body(dy_ref, dus_ref, u_ref, sr_ref, si_ref, bre_ref, bim_ref, are_ref, aim_ref, cre_ref, cim_ref,
             du_ref, dbr_ref, dbi_ref, dcr_ref, dci_ref, dar_ref, dai_ref, gr_ref, gi_ref):
        d = pl.program_id(1)
        dyb = dy_ref[...].astype(BF16)
        gr_ref[...] = _dot_nt(dyb, cre_ref[...])
        gi_ref[...] = -_dot_nt(dyb, cim_ref[...])
        dcr_ref[...] = _dot_tn(sr_ref[...].astype(BF16), dyb)
        dci_ref[...] = -_dot_tn(si_ref[...].astype(BF16), dyb)
        _scan_by_direction(gr_ref, gi_ref, are_ref[...], -aim_ref[...], d, adjoint=True)

        @pl.when(d == 0)
        def _():
            dar_ref[...], dai_ref[...] = _dlam(gr_ref, gi_ref, sr_ref, si_ref, reverse=False)
            du_ref[...] = dus_ref[...]

        @pl.when(d == 1)
        def _():
            dar_ref[...], dai_ref[...] = _dlam(gr_ref, gi_ref, sr_ref, si_ref, reverse=True)

        grb = gr_ref[...].astype(BF16)
        gib = gi_ref[...].astype(BF16)
        du_ref[...] += _dot_nt(grb, bre_ref[...]) + _dot_nt(gib, bim_ref[...])
        ub = u_ref[...].astype(BF16)
        dbr_ref[...] = _dot_tn(ub, grb)
        dbi_ref[...] = _dot_tn(ub, gib)

    chan, _, state, bmat, cmat, avec = _s5_specs()
    return pl.pallas_call(
        body, grid=(S5_CHUNKS, 2), in_specs=[chan, chan, chan, state, state, bmat, bmat, avec, avec, cmat, cmat],
        out_specs=[chan, bmat, bmat, cmat, cmat, avec, avec],
        out_shape=[SDS((SEQ, S5_WIDTH), F32)] + [SDS((2, S5_CHUNKS, CH_W, ST_W), F32)] * 2
                  + [SDS((2, S5_CHUNKS, ST_W, CH_W), F32)] * 2 + [SDS((2, S5_CHUNKS, 1, ST_W), F32)] * 2,
        scratch_shapes=[pltpu.VMEM((SEQ, ST_W), F32), pltpu.VMEM((SEQ, ST_W), F32)],
        name="s5_scan_bwd", compiler_params=_params(("parallel", "arbitrary")))(dy, du_skip, u, sr, si, bre, bim, are, aim, cre, cim)


_GELU_K = math.sqrt(2.0 / math.pi)
_GELU_C = 0.044715


def _gelu(x):
    t = jnp.tanh(_GELU_K * (x + _GELU_C * x * x * x))
    return 0.5 * x * (1.0 + t), t


def _s5_glu_fwd(u, y2, dskip, wglu, bglu):
    def body(u_ref, y0_ref, y1_ref, d_ref, w_ref, b_ref, o_ref, yp_ref):
        ypre = u_ref[...] * d_ref[...] + y0_ref[...] + y1_ref[...]
        yp_ref[...] = ypre
        y, _ = _gelu(ypre)
        z = _dot(y.astype(BF16), w_ref[...]) + b_ref[...]
        o_ref[...] = y * jax.nn.sigmoid(z)

    row = _row_spec(S5_WIDTH)
    vec = _fix_spec((1, S5_WIDTH))
    dir0 = pl.BlockSpec((None, ROW_TILE, S5_WIDTH), lambda i: (0, i, 0))
    dir1 = pl.BlockSpec((None, ROW_TILE, S5_WIDTH), lambda i: (1, i, 0))
    return pl.pallas_call(
        body, grid=(N_ROW_TILES,), in_specs=[row, dir0, dir1, vec, _fix_spec((S5_WIDTH, S5_WIDTH)), vec],
        out_specs=[row, row], out_shape=[SDS((SEQ, S5_WIDTH), F32)] * 2, name="s5_glu_fwd",
        compiler_params=_params(("parallel",)))(u, y2, y2, dskip, wglu, bglu)


def _s5_glu_bwd(do, ypre, u, dskip, wglu, bglu):
    def body(do_ref, yp_ref, u_ref, d_ref, w_ref, b_ref, dyp_ref, du_ref, dw_ref, db_ref, dd_ref):
        i = pl.program_id(0)
        ypre = yp_ref[...]
        y, t = _gelu(ypre)
        yb = y.astype(BF16)
        sg = jax.nn.sigmoid(_dot(yb, w_ref[...]) + b_ref[...])
        dov = do_ref[...]
        dz = dov * y * sg * (1.0 - sg)
        dzb = dz.astype(BF16)
        dy = dov * sg + _dot_nt(dzb, w_ref[...])
        dgelu = 0.5 * (1.0 + t) + 0.5 * ypre * (1.0 - t * t) * _GELU_K * (1.0 + 3.0 * _GELU_C * ypre * ypre)
        dyp = dy * dgelu
        dyp_ref[...] = dyp
        uv = u_ref[...]
        du_ref[...] = dyp * d_ref[...]

        @pl.when(i == 0)
        def _():
            dw_ref[...] = jnp.zeros_like(dw_ref)
            db_ref[...] = jnp.zeros_like(db_ref)
            dd_ref[...] = jnp.zeros_like(dd_ref)

        dw_ref[...] += _dot_tn(yb, dzb)
        db_ref[...] += jnp.sum(dz, axis=0, keepdims=True)
        dd_ref[...] += jnp.sum(dyp * uv, axis=0, keepdims=True)

    row = _row_spec(S5_WIDTH)
    vec = _fix_spec((1, S5_WIDTH))
    mat = _fix_spec((S5_WIDTH, S5_WIDTH))
    return pl.pallas_call(
        body, grid=(N_ROW_TILES,), in_specs=[row, row, row, vec, mat, vec], out_specs=[row, row, mat, vec, vec],
        out_shape=[SDS((SEQ, S5_WIDTH), F32)] * 2 + [SDS((S5_WIDTH, S5_WIDTH), F32), SDS((1, S5_WIDTH), F32), SDS((1, S5_WIDTH), F32)],
        name="s5_glu_bwd", compiler_params=_params(("arbitrary",)))(do, ypre, u, dskip, wglu, bglu)


def _mix_out_fwd(ona, os5, g_na, g_s5, wout):
    def body(a_ref, s_ref, ga_ref, gs_ref, w_ref, o_ref):
        av, sv = a_ref[...], s_ref[...]
        ca = (av * _rstd(av) * ga_ref[...]).astype(BF16)
        cs = (sv * _rstd(sv) * gs_ref[...]).astype(BF16)
        o_ref[...] = _dot(ca, w_ref[0:NA_WIDTH, :]) + _dot(cs, w_ref[NA_WIDTH:, :])

    row = _row_spec(NA_WIDTH)
    vec = _fix_spec((1, NA_WIDTH))
    return pl.pallas_call(
        body, grid=(N_ROW_TILES,), in_specs=[row, row, vec, vec, _fix_spec((D_MODEL, D_MODEL))],
        out_specs=_row_spec(D_MODEL), out_shape=SDS((SEQ, D_MODEL), F32), name="mix_out_fwd",
        compiler_params=_params(("parallel",)))(ona, os5, g_na, g_s5, wout)


def _mix_out_bwd(dmix, ona, os5, g_na, g_s5, wout):
    def body(dm_ref, a_ref, s_ref, ga_ref, gs_ref, w_ref, da_ref, ds_ref, dw_ref, dga_ref, dgs_ref):
        i = pl.program_id(0)
        dm = dm_ref[...]
        av, sv = a_ref[...], s_ref[...]
        ra, rs = _rstd(av), _rstd(sv)
        ga, gs = ga_ref[...], gs_ref[...]
        ca = (av * ra * ga).astype(BF16)
        cs = (sv * rs * gs).astype(BF16)
        dca = _dot_nt(dm, w_ref[0:NA_WIDTH, :])
        dcs = _dot_nt(dm, w_ref[NA_WIDTH:, :])
        da, dga = _rms_bwd(av, ra, ga, dca)
        ds, dgs = _rms_bwd(sv, rs, gs, dcs)
        da_ref[...] = da
        ds_ref[...] = ds

        @pl.when(i == 0)
        def _():
            dw_ref[...] = jnp.zeros_like(dw_ref)
            dga_ref[...] = jnp.zeros_like(dga_ref)
            dgs_ref[...] = jnp.zeros_like(dgs_ref)

        dw_ref[0:NA_WIDTH, :] += _dot_tn(ca, dm)
        dw_ref[NA_WIDTH:, :] += _dot_tn(cs, dm)
        dga_ref[...] += jnp.sum(dga, axis=0, keepdims=True)
        dgs_ref[...] += jnp.sum(dgs, axis=0, keepdims=True)

    row = _row_spec(NA_WIDTH)
    vec = _fix_spec((1, NA_WIDTH))
    mat = _fix_spec((D_MODEL, D_MODEL))
    return pl.pallas_call(
        body, grid=(N_ROW_TILES,), in_specs=[_row_spec(D_MODEL), row, row, vec, vec, mat],
        out_specs=[row, row, mat, vec, vec],
        out_shape=[SDS((SEQ, NA_WIDTH), F32)] * 2 + [SDS((D_MODEL, D_MODEL), F32), SDS((1, NA_WIDTH), F32), SDS((1, NA_WIDTH), F32)],
        name="mix_out_bwd", compiler_params=_params(("arbitrary",)))(dmix, ona, os5, g_na, g_s5, wout)


def _me():
    x, y, c = lax.axis_index("x"), lax.axis_index("y"), lax.axis_index("c")
    return x, y, c, 4 * x + 2 * y + c


def _peer(k):
    x, y, c, _ = _me()
    px = 1 - x if (k >> 2) & 1 else x
    py = 1 - y if (k >> 1) & 1 else y
    pc = 1 - c if k & 1 else c
    return (px, py, pc), 4 * px + 2 * py + pc


def _exchange(arrays, gather, name):
    n = len(arrays)

    def body(*refs):
        ins, outs = refs[:n], refs[n:2 * n]
        send_sems, recv_sems, local_sems = refs[2 * n:]
        _, _, _, me = _me()
        started = []
        for a in range(n):
            src_mine = ins[a] if gather else ins[a].at[me]
            local = pltpu.make_async_copy(src_mine, outs[a].at[me], local_sems.at[a])
            local.start()
            started.append(local)
        sends = []
        for k in range(1, N_DEV):
            peer, peer_idx = _peer(k)
            for a in range(n):
                src = ins[a] if gather else ins[a].at[peer_idx]
                cp = pltpu.make_async_remote_copy(src_ref=src, dst_ref=outs[a].at[me], send_sem=send_sems.at[a, k - 1],
                                                  recv_sem=recv_sems.at[a, k - 1], device_id=peer, device_id_type=MESH)
                cp.start()
                sends.append(cp)
        for k in range(1, N_DEV):
            peer, peer_idx = _peer(k)
            for a in range(n):
                src = ins[a] if gather else ins[a].at[peer_idx]
                pltpu.make_async_remote_copy(src_ref=src, dst_ref=outs[a].at[peer_idx], send_sem=send_sems.at[a, k - 1],
                                             recv_sem=recv_sems.at[a, k - 1], device_id=peer, device_id_type=MESH).wait_recv()
        for cp in sends:
            cp.wait_send()
        for local in started:
            local.wait()

    hbm = pl.BlockSpec(memory_space=pltpu.HBM)
    out_shape = [SDS((N_DEV,) + tuple(a.shape), a.dtype) if gather else SDS(a.shape, a.dtype) for a in arrays]
    return pl.pallas_call(
        body, in_specs=[hbm] * n, out_specs=[hbm] * n, out_shape=out_shape,
        scratch_shapes=[pltpu.SemaphoreType.DMA((n, N_DEV - 1)), pltpu.SemaphoreType.DMA((n, N_DEV - 1)),
                        pltpu.SemaphoreType.DMA((n,))],
        name=name)(*arrays)


def _adamw_math(w, g, m, v):
    m = ADAM_B1 * m + (1.0 - ADAM_B1) * g
    v = ADAM_B2 * v + (1.0 - ADAM_B2) * (g * g)
    m_hat = m / (1.0 - ADAM_B1 ** ADAM_STEP)
    v_hat = v / (1.0 - ADAM_B2 ** ADAM_STEP)
    delta = -ADAM_LR * (m_hat / (jnp.sqrt(v_hat) + ADAM_EPS) + ADAM_WD * w)
    return delta, m, v


def _adamw(w, m, v, pieces, name):
    rows, cols = w.shape
    tile = rows
    for cand in (256, 176, 128, 64, 16):
        if rows > cand and rows % cand == 0:
            tile = cand
            break

    def body(w_ref, m_ref, v_ref, p_ref, g_ref, d_ref, mo_ref, vo_ref):
        g = p_ref[0].astype(F32)
        for p in range(1, N_DEV):
            g = g + p_ref[p].astype(F32)
        g_ref[...] = g
        d_ref[...], mo_ref[...], vo_ref[...] = _adamw_math(w_ref[...], g, m_ref[...], v_ref[...])

    blk = pl.BlockSpec((tile, cols), lambda i: (i, 0))
    return pl.pallas_call(
        body, grid=(rows // tile,), in_specs=[blk, blk, blk, pl.BlockSpec((N_DEV, tile, cols), lambda i: (0, i, 0))],
        out_specs=[blk] * 4, out_shape=[SDS((rows, cols), F32)] * 4, name=name,
        compiler_params=_params(("parallel",)))(w, m, v, pieces)


def _sum_pieces(p_ref):
    g = p_ref[0].astype(F32)
    for p in range(1, N_DEV):
        g = g + p_ref[p].astype(F32)
    return g


def _adamw_lead(w, m, v, pieces, tile, name):
    lead, b, c = w.shape

    def body(w_ref, m_ref, v_ref, p_ref, g_ref, d_ref, mo_ref, vo_ref):
        g = _sum_pieces(p_ref)
        g_ref[...] = g
        d_ref[...], mo_ref[...], vo_ref[...] = _adamw_math(w_ref[...], g, m_ref[...], v_ref[...])

    blk = pl.BlockSpec((tile, b, c), lambda i: (i, 0, 0))
    return pl.pallas_call(
        body, grid=(lead // tile,), in_specs=[blk, blk, blk, pl.BlockSpec((N_DEV, tile, b, c), lambda i: (0, i, 0, 0))],
        out_specs=[blk] * 4, out_shape=[SDS(w.shape, F32)] * 4, name=name,
        compiler_params=_params(("parallel",)))(w, m, v, pieces)


VEC_ROWS = ['ffn1_pre_g', 'ffn1_post_g', 'mix_pre_g', 'mix_post_g', 'ffn2_pre_g', 'ffn2_post_g', 'final_g',
            ('na_out_g', 's5_out_g'), ('s5_d', 's5_b_glu')]
VEC_NAMES = [n for row in VEC_ROWS for n in ((row,) if isinstance(row, str) else row)]
VEC_PACK_ROWS = 16


def _pack_vectors(grads):
    def body(*refs):
        o_ref = refs[-1]
        o_ref[...] = jnp.zeros_like(o_ref)
        k = 0
        for i, row in enumerate(VEC_ROWS):
            if isinstance(row, str):
                o_ref[i:i + 1, :] = refs[k][...]
                k += 1
            else:
                o_ref[i:i + 1, 0:NA_WIDTH] = refs[k][...]
                o_ref[i:i + 1, NA_WIDTH:] = refs[k + 1][...]
                k += 2

    return pl.pallas_call(body, out_shape=SDS((VEC_PACK_ROWS, D_MODEL), F32), name="pack_vectors",
                          compiler_params=_params())(*[grads[n] for n in VEC_NAMES])


def _adamw_small(packed8, vec_wmv, others):
    n_vec, n_oth = len(VEC_NAMES), len(others)

    def body(*refs):
        p_ref = refs[0]
        ins = refs[1:1 + 3 * n_vec + 4 * n_oth]
        outs = refs[1 + 3 * n_vec + 4 * n_oth:]
        gsum = _sum_pieces(p_ref)
        k = 0
        for i, row in enumerate(VEC_ROWS):
            parts = [(row, gsum[i:i + 1, :])] if isinstance(row, str) else \
                [(row[0], gsum[i:i + 1, 0:NA_WIDTH]), (row[1], gsum[i:i + 1, NA_WIDTH:])]
            for _, g in parts:
                w_ref, m_ref, v_ref = ins[3 * k:3 * k + 3]
                outs[4 * k][...] = g
                outs[4 * k + 1][...], outs[4 * k + 2][...], outs[4 * k + 3][...] = _adamw_math(w_ref[...], g, m_ref[...], v_ref[...])
                k += 1
        for j in range(n_oth):
            w_ref, m_ref, v_ref, g_ref = ins[3 * n_vec + 4 * j:3 * n_vec + 4 * j + 4]
            g = _sum_pieces(g_ref)
            g = g[tuple(slice(0, s) for s in w_ref.shape)]
            o = outs[4 * (n_vec + j):4 * (n_vec + j) + 4]
            o[0][...] = g
            o[1][...], o[2][...], o[3][...] = _adamw_math(w_ref[...], g, m_ref[...], v_ref[...])

    args, out_shape = [packed8], []
    for w, m, v in vec_wmv:
        args += [w, m, v]
        out_shape += [SDS(w.shape, F32)] * 4
    for w, m, v, g in others:
        args += [w, m, v, g]
        out_shape += [SDS(w.shape, F32)] * 4
    return pl.pallas_call(body, out_shape=out_shape, name="adamw_small", compiler_params=_params())(*args)


def _perm_rows(x):
    return x.reshape(SCAN_BLOCKS, SCAN_T, x.shape[-1]).transpose(1, 0, 2).reshape(SEQ, x.shape[-1])


def _unperm_rows(x):
    return x.reshape(SCAN_T, SCAN_BLOCKS, x.shape[-1]).transpose(1, 0, 2).reshape(SEQ, x.shape[-1])


def _block_diag(x):
    eye = np.eye(8, dtype=bool)[None, None, :, None, :, None]
    full = jnp.where(eye, x[:, :, :, :, None, :], 0.0)
    return full.reshape(2, S5_CHUNKS, 8 * x.shape[3], 8 * x.shape[4])


def _diag_blocks(x, r, c):
    x6 = x.reshape(2, S5_CHUNKS, 8, r, 8, c)
    return jnp.stack([x6[:, :, g, :, g, :] for g in range(8)], axis=2)


def _local_step(x, target, meta, wts, small):
    h0 = jnp.concatenate([meta, x], axis=0)
    tgt = jnp.concatenate([jnp.zeros((N_META, D_MODEL), F32), target], axis=0)

    a1 = _prenorm(h0, small["ffn1_pre_g"])
    gate1, up1, f1 = _ffn_fwd(a1, wts["ffn1_w_gate"], wts["ffn1_w_up"], wts["ffn1_w_down"], "ffn1_fwd")
    h1, a2 = _post_pre(f1, h0, small["ffn1_post_g"], small["mix_pre_g"], 0.5, "post_pre1")
    proj = _proj_fwd(a2, wts["w_in"])
    qkv = proj[:6].reshape(3, 2, SEQ, 4, HEAD_DIM).transpose(0, 1, 3, 2, 4).reshape(3, HEADS, SEQ, HEAD_DIM)
    u = proj[6:].transpose(1, 0, 2).reshape(SEQ, S5_WIDTH)
    bias = _rpb_expand(small["na_rpb"][0])
    o3 = _na_fwd(qkv[0], qkv[1], qkv[2], bias)
    ona = o3.transpose(1, 0, 2).reshape(SEQ, NA_WIDTH)

    lr = small["s5_lam_re"].reshape(64, S5_STATE)
    li = small["s5_lam_im"].reshape(64, S5_STATE)
    logdt = small["s5_log_dt"].reshape(64, 1)
    b_t = [small[n].reshape(64, S5_STATE, S5_GROUP).transpose(0, 2, 1) for n in ("s5_b_re", "s5_b_im")]
    lbr, lbi, bbr, bbi = _s5_prep(lr, li, logdt, b_t[0], b_t[1])
    are = lbr.reshape(2, S5_CHUNKS, 1, ST_W)
    aim = lbi.reshape(2, S5_CHUNKS, 1, ST_W)
    bre = _block_diag(bbr.reshape(2, S5_CHUNKS, 8, S5_GROUP, S5_STATE)).astype(BF16)
    bim = _block_diag(bbi.reshape(2, S5_CHUNKS, 8, S5_GROUP, S5_STATE)).astype(BF16)
    c_t = [small[n].reshape(2, S5_CHUNKS, 8, S5_GROUP, S5_STATE).transpose(0, 1, 2, 4, 3) for n in ("s5_c_re", "s5_c_im")]
    cre = _block_diag(c_t[0]).astype(BF16)
    cim = _block_diag(c_t[1]).astype(BF16)
    u_p = _perm_rows(u)
    sr, si, y2 = _s5_scan_fwd(u_p, bre, bim, are, aim, cre, cim)
    os5_p, ypre_p = _s5_glu_fwd(u_p, y2, small["s5_d"], wts["s5_w_glu"], small["s5_b_glu"])
    os5 = _unperm_rows(os5_p)

    mix = _mix_out_fwd(ona, os5, small["na_out_g"], small["s5_out_g"], wts["w_out"])
    h2, a3 = _post_pre(mix, h1, small["mix_post_g"], small["ffn2_pre_g"], 1.0, "post_pre2")
    gate2, up2, f2 = _ffn_fwd(a3, wts["ffn2_w_gate"], wts["ffn2_w_up"], wts["ffn2_w_down"], "ffn2_fwd")
    loss8, dh3, df2, g_final, g_ffn2_post = _final_loss(f2, h2, small["ffn2_post_g"], small["final_g"], tgt)

    da3, dwg2, dwu2, dwd2 = _ffn_bwd(df2, a3, gate2, up2, wts["ffn2_w_gate"], wts["ffn2_w_up"], wts["ffn2_w_down"], "ffn2_bwd")
    dh2, dmix, g_ffn2_pre, g_mix_post = _bwd_pre_post(da3, h2, small["ffn2_pre_g"], dh3, mix, small["mix_post_g"], 1.0, "bwd_pre_post2")
    dona, dos5, dwout, g_na_out, g_s5_out = _mix_out_bwd(dmix, ona, os5, small["na_out_g"], small["s5_out_g"], wts["w_out"])

    dypre_p, du_skip_p, dwglu, g_b_glu, g_s5_d = _s5_glu_bwd(_perm_rows(dos5), ypre_p, u_p, small["s5_d"], wts["s5_w_glu"],
                                                             small["s5_b_glu"])
    du_p, dbr, dbi, dcr, dci, dar, dai = _s5_scan_bwd(dypre_p, du_skip_p, u_p, sr, si, bre, bim, are, aim, cre, cim)
    du = _unperm_rows(du_p)
    dbbr = _diag_blocks(dbr, S5_GROUP, S5_STATE).reshape(64, S5_GROUP, S5_STATE)
    dbbi = _diag_blocks(dbi, S5_GROUP, S5_STATE).reshape(64, S5_GROUP, S5_STATE)
    g_lr, g_li, g_dt, g_br, g_bi = _s5_prep_bwd(lr, li, logdt, b_t[0], b_t[1], dar.reshape(64, S5_STATE),
                                                dai.reshape(64, S5_STATE), dbbr, dbbi)
    g_c = [_diag_blocks(d, S5_STATE, S5_GROUP).transpose(0, 1, 2, 4, 3).reshape(2 * S5_GROUPS, S5_GROUP, S5_STATE)
           for d in (dcr, dci)]

    do3 = dona.reshape(SEQ, HEADS, HEAD_DIM).transpose(1, 0, 2)
    dq, dk, dv, dbias = _na_bwd(qkv[0], qkv[1], qkv[2], bias, do3)
    g_rpb = _rpb_reduce(dbias)
    dqkv = jnp.stack([dq, dk, dv]).reshape(3, 2, 4, SEQ, HEAD_DIM).transpose(0, 1, 3, 2, 4).reshape(6, SEQ, IN_SHARD)
    dproj = jnp.concatenate([dqkv, du.reshape(SEQ, 2, IN_SHARD).transpose(1, 0, 2)], axis=0).astype(BF16)
    da2, dwin = _proj_bwd(dproj, a2, wts["w_in"])
    dh1, df1, g_mix_pre, g_ffn1_post = _bwd_pre_post(da2, h1, small["mix_pre_g"], dh2, f1, small["ffn1_post_g"], 0.5, "bwd_pre_post1")
    da1, dwg1, dwu1, dwd1 = _ffn_bwd(df1, a1, gate1, up1, wts["ffn1_w_gate"], wts["ffn1_w_up"], wts["ffn1_w_down"], "ffn1_bwd")
    dh0, g_ffn1_pre = _bwd_pre_only(da1, h0, small["ffn1_pre_g"], dh1)

    big = {
        "ffn1_w_gate": dwg1, "ffn1_w_up": dwu1, "ffn1_w_down": dwd1, "w_in": dwin,
        "s5_w_glu": dwglu.reshape(N_DEV, S5_WIDTH // N_DEV, S5_WIDTH).astype(BF16),
        "w_out": dwout.reshape(N_DEV, D_MODEL // N_DEV, D_MODEL).astype(BF16),
        "ffn2_w_gate": dwg2, "ffn2_w_up": dwu2, "ffn2_w_down": dwd2,
    }
    small_g = {
        "ffn1_pre_g": g_ffn1_pre, "ffn1_post_g": g_ffn1_post, "mix_pre_g": g_mix_pre,
        "na_rpb": g_rpb,
        "s5_lam_re": g_lr.reshape(2, S5_GROUPS, S5_STATE), "s5_lam_im": g_li.reshape(2, S5_GROUPS, S5_STATE),
        "s5_log_dt": g_dt.reshape(2, S5_GROUPS),
        "s5_b_re": g_br.transpose(0, 2, 1), "s5_b_im": g_bi.transpose(0, 2, 1),
        "s5_c_re": g_c[0], "s5_c_im": g_c[1], "s5_d": g_s5_d, "s5_b_glu": g_b_glu,
        "na_out_g": g_na_out, "s5_out_g": g_s5_out, "mix_post_g": g_mix_post,
        "ffn2_pre_g": g_ffn2_pre, "ffn2_post_g": g_ffn2_post, "final_g": g_final,
    }
    return loss8[0, 0], dh0[N_META:], dh0[:N_META], big, small_g


WEIGHT_NAMES = ['meta_tokens', 'ffn1_pre_g', 'ffn1_post_g', 'ffn1_w_gate', 'ffn1_w_up', 'ffn1_w_down', 'mix_pre_g', 'w_in',
                'na_rpb', 's5_lam_re', 's5_lam_im', 's5_log_dt', 's5_b_re', 's5_b_im', 's5_c_re', 's5_c_im', 's5_d',
                's5_w_glu', 's5_b_glu', 'na_out_g', 's5_out_g', 'w_out', 'mix_post_g', 'ffn2_pre_g', 'ffn2_post_g',
                'ffn2_w_gate', 'ffn2_w_up', 'ffn2_w_down', 'final_g']
BIG_NAMES = ['ffn1_w_gate', 'ffn1_w_up', 'ffn1_w_down', 'w_in', 's5_w_glu', 'w_out', 'ffn2_w_gate', 'ffn2_w_up', 'ffn2_w_down']
SMALL_NAMES = [n for n in WEIGHT_NAMES if n not in BIG_NAMES and n != 'meta_tokens']
WHOLE_NAMES = ['na_rpb', 's5_lam_re', 's5_lam_im', 's5_log_dt']
LEAD_NAMES = ['s5_b_re', 's5_b_im', 's5_c_re', 's5_c_im']


def kernel(x, meta_tokens, ffn1_pre_g, ffn1_post_g, ffn1_w_gate, ffn1_w_up, ffn1_w_down, mix_pre_g, w_in, na_rpb, s5_lam_re, s5_lam_im, s5_log_dt, s5_b_re, s5_b_im, s5_c_re, s5_c_im, s5_d, s5_w_glu, s5_b_glu, na_out_g, s5_out_g, w_out, mix_post_g, ffn2_pre_g, ffn2_post_g, ffn2_w_gate, ffn2_w_up, ffn2_w_down, final_g, loss_target, m_meta_tokens, m_ffn1_pre_g, m_ffn1_post_g, m_ffn1_w_gate, m_ffn1_w_up, m_ffn1_w_down, m_mix_pre_g, m_w_in, m_na_rpb, m_s5_lam_re, m_s5_lam_im, m_s5_log_dt, m_s5_b_re, m_s5_b_im, m_s5_c_re, m_s5_c_im, m_s5_d, m_s5_w_glu, m_s5_b_glu, m_na_out_g, m_s5_out_g, m_w_out, m_mix_post_g, m_ffn2_pre_g, m_ffn2_post_g, m_ffn2_w_gate, m_ffn2_w_up, m_ffn2_w_down, m_final_g, v_meta_tokens, v_ffn1_pre_g, v_ffn1_post_g, v_ffn1_w_gate, v_ffn1_w_up, v_ffn1_w_down, v_mix_pre_g, v_w_in, v_na_rpb, v_s5_lam_re, v_s5_lam_im, v_s5_log_dt, v_s5_b_re, v_s5_b_im, v_s5_c_re, v_s5_c_im, v_s5_d, v_s5_w_glu, v_s5_b_glu, v_na_out_g, v_s5_out_g, v_w_out, v_mix_post_g, v_ffn2_pre_g, v_ffn2_post_g, v_ffn2_w_gate, v_ffn2_w_up, v_ffn2_w_down, v_final_g):
    args = dict(locals())
    w = {n: args[n] for n in WEIGHT_NAMES}
    m = {n: args["m_" + n] for n in WEIGHT_NAMES}
    v = {n: args["v_" + n] for n in WEIGHT_NAMES}

    shards = [w[n][0].astype(BF16) for n in BIG_NAMES] + [w["meta_tokens"]]
    gathered = _exchange(shards, True, "gather_weights")
    wts = dict(zip(BIG_NAMES, gathered[:-1]))
    wts["s5_w_glu"] = wts["s5_w_glu"].reshape(S5_WIDTH, S5_WIDTH)
    wts["w_out"] = wts["w_out"].reshape(D_MODEL, D_MODEL)
    meta_full = gathered[-1].transpose(1, 0, 2).reshape(N_META, D_MODEL)
    small = {n: w[n] for n in SMALL_NAMES}

    loss_local, grad_x, gmeta, big, small_g = _local_step(x[0], loss_target[0], meta_full, wts, small)
    loss = lax.psum(loss_local, AXES)

    gmeta8 = gmeta.reshape(N_META, N_DEV, D_MODEL // N_DEV).transpose(1, 0, 2)
    pieces = _exchange([big[n] for n in BIG_NAMES] + [gmeta8], False, "scatter_grads")
    res = {}
    for n, p in zip(BIG_NAMES + ["meta_tokens"], pieces):
        shape = w[n].shape
        w2 = w[n].reshape(shape[-2], shape[-1])
        outs = _adamw(w2, m[n].reshape(w2.shape), v[n].reshape(w2.shape), p, "adamw_" + n)
        res[n] = [o.reshape(shape) for o in outs]

    gathered = _exchange([_pack_vectors(small_g)] + [small_g[n] for n in WHOLE_NAMES + LEAD_NAMES], True, "gather_small_grads")
    g8 = dict(zip(WHOLE_NAMES + LEAD_NAMES, gathered[1:]))
    outs = _adamw_small(gathered[0], [(w[n], m[n], v[n]) for n in VEC_NAMES], [(w[n][0], m[n][0], v[n][0], g8[n]) for n in WHOLE_NAMES])
    for i, n in enumerate(VEC_NAMES + WHOLE_NAMES):
        res[n] = [o.reshape(w[n].shape) for o in outs[4 * i:4 * i + 4]]
    for n in LEAD_NAMES:
        shape3 = (2 * S5_GROUPS,) + w[n].shape[-2:]
        outs = _adamw_lead(w[n].reshape(shape3), m[n].reshape(shape3), v[n].reshape(shape3), g8[n], 8, "adamw_" + n)
        res[n] = [o.reshape(w[n].shape) for o in outs]

    out = [loss, grad_x[None]]
    for kind in range(4):
        out += [res[n][kind] for n in WEIGHT_NAMES]
    return tuple(out)
```

```python
import functools
import math

import numpy as np
import jax
import jax.numpy as jnp
from jax import lax
from jax.experimental import pallas as pl
from jax.experimental.pallas import tpu as pltpu

F32 = jnp.float32
BF16 = jnp.bfloat16
SDS = jax.ShapeDtypeStruct

D_MODEL = 1024
N_TOK = 2048
N_META = 16
SEQ = N_TOK + N_META
ROW_TILE = 688
N_ROW_TILES = SEQ // ROW_TILE
N_DEV = 8
D_FF = 2816
FF_SHARD = D_FF // N_DEV
IN_SHARD = 256
NA_WIDTH = 512
S5_WIDTH = 512
HEADS = 8
HEAD_DIM = 64
GRID_W = 64
GRID_ROWS = N_TOK // GRID_W
KH = 8
KW = 16
KEYS = KH * GRID_W
NA_UNROLL = 2
S5_GROUPS = 32
S5_GROUP = 16
S5_STATE = 64
S5_CHUNKS = 4
CH_W = S5_WIDTH // S5_CHUNKS
ST_W = S5_GROUPS * S5_STATE // S5_CHUNKS
SCAN_BLOCKS = 8
SCAN_T = SEQ // SCAN_BLOCKS
RMS_EPS = 1e-6
NEG_INF = -1e30
ATT_SCALE = HEAD_DIM ** -0.5
ADAM_LR, ADAM_B1, ADAM_B2, ADAM_EPS, ADAM_WD, ADAM_STEP = 0.001, 0.9, 0.999, 1e-08, 0.01, 10
VMEM_LIMIT = 56 * 1024 * 1024
MESH = pl.DeviceIdType.MESH
AXES = ("x", "y", "c")


def _params(sem=None):
    return pltpu.CompilerParams(dimension_semantics=sem, vmem_limit_bytes=VMEM_LIMIT)


def _dot(a, b):
    return jnp.dot(a, b, preferred_element_type=F32)


def _dot_nt(a, b):
    return lax.dot_general(a, b, (((1,), (1,)), ((), ())), preferred_element_type=F32)


def _dot_tn(a, b):
    return lax.dot_general(a, b, (((0,), (0,)), ((), ())), preferred_element_type=F32)


def _rstd(x):
    return lax.rsqrt(jnp.mean(x * x, axis=-1, keepdims=True) + RMS_EPS)


def _rms_bwd(x, r, g, dy):
    dyg = dy * g
    xr = x * r
    dx = r * (dyg - xr * jnp.mean(dyg * xr, axis=-1, keepdims=True))
    return dx, dy * xr


def _rows(i, size=ROW_TILE):
    return pl.ds(pl.multiple_of(i * size, 16), size)


def _row_spec(width):
    return pl.BlockSpec((ROW_TILE, width), lambda i: (i, 0))


def _fix_spec(shape):
    return pl.BlockSpec(shape, lambda i: (0,) * len(shape))


def _split3(x):
    hi = x.astype(BF16)
    r1 = x - hi.astype(F32)
    mid = r1.astype(BF16)
    lo = (r1 - mid.astype(F32)).astype(BF16)
    return hi, mid, lo


def _prenorm(x, g):
    def body(x_ref, g_ref, a_ref):
        xv = x_ref[...]
        a_ref[...] = (xv * _rstd(xv) * g_ref[...]).astype(BF16)

    return pl.pallas_call(
        body, grid=(N_ROW_TILES,), in_specs=[_row_spec(D_MODEL), _fix_spec((1, D_MODEL))],
        out_specs=_row_spec(D_MODEL), out_shape=SDS((SEQ, D_MODEL), BF16), name="prenorm",
        compiler_params=_params(("parallel",)))(x, g)


def _post_pre(f, hres, g_post, g_next, scale, name):
    def body(f_ref, h_ref, gp_ref, gn_ref, ho_ref, a_ref):
        fv = f_ref[...]
        h = h_ref[...] + scale * (fv * _rstd(fv) * gp_ref[...])
        ho_ref[...] = h
        a_ref[...] = (h * _rstd(h) * gn_ref[...]).astype(BF16)

    return pl.pallas_call(
        body, grid=(N_ROW_TILES,),
        in_specs=[_row_spec(D_MODEL), _row_spec(D_MODEL), _fix_spec((1, D_MODEL)), _fix_spec((1, D_MODEL))],
        out_specs=[_row_spec(D_MODEL), _row_spec(D_MODEL)],
        out_shape=[SDS((SEQ, D_MODEL), F32), SDS((SEQ, D_MODEL), BF16)], name=name,
        compiler_params=_params(("parallel",)))(f, hres, g_post, g_next)


def _final_loss(f2, h2, g_post, g_final, target):
    def body(f_ref, h_ref, gp_ref, gf_ref, t_ref, loss_ref, dh_ref, df_ref, dgf_ref, dgp_ref):
        i = pl.program_id(0)
        fv = f_ref[...]
        r1 = _rstd(fv)
        gp = gp_ref[...]
        h3 = h_ref[...] + 0.5 * (fv * r1 * gp)
        r2 = _rstd(h3)
        gf = gf_ref[...]
        y = h3 * r2 * gf
        row = lax.broadcasted_iota(jnp.int32, (ROW_TILE, 1), 0) + i * ROW_TILE
        err = jnp.where(row >= N_META, y - t_ref[...], 0.0)
        part = 0.5 * jnp.sum(jnp.mean(err * err, axis=-1, keepdims=True))
        dy = err * (1.0 / D_MODEL)
        dh3, dgf = _rms_bwd(h3, r2, gf, dy)
        dh_ref[...] = dh3
        df, dgp = _rms_bwd(fv, r1, gp, 0.5 * dh3)
        df_ref[...] = df.astype(BF16)

        @pl.when(i == 0)
        def _():
            loss_ref[...] = jnp.zeros_like(loss_ref)
            dgf_ref[...] = jnp.zeros_like(dgf_ref)
            dgp_ref[...] = jnp.zeros_like(dgp_ref)

        loss_ref[...] += part
        dgf_ref[...] += jnp.sum(dgf, axis=0, keepdims=True)
        dgp_ref[...] += jnp.sum(dgp, axis=0, keepdims=True)

    gain = _fix_spec((1, D_MODEL))
    return pl.pallas_call(
        body, grid=(N_ROW_TILES,),
        in_specs=[_row_spec(D_MODEL), _row_spec(D_MODEL), gain, gain, _row_spec(D_MODEL)],
        out_specs=[_fix_spec((8, 128)), _row_spec(D_MODEL), _row_spec(D_MODEL), gain, gain],
        out_shape=[SDS((8, 128), F32), SDS((SEQ, D_MODEL), F32), SDS((SEQ, D_MODEL), BF16),
                   SDS((1, D_MODEL), F32), SDS((1, D_MODEL), F32)],
        name="final_loss", compiler_params=_params(("arbitrary",)))(f2, h2, g_post, g_final, target)


def _bwd_pre_post(da, h, g_pre, dh_res, fprev, g_post, scale, name):
    def body(da_ref, h_ref, gpre_ref, dhr_ref, f_ref, gpost_ref, dh_ref, df_ref, dgpre_ref, dgpost_ref):
        i = pl.program_id(0)
        hv = h_ref[...]
        dxa, dgpre = _rms_bwd(hv, _rstd(hv), gpre_ref[...], da_ref[...])
        dh = dhr_ref[...] + dxa
        dh_ref[...] = dh
        fv = f_ref[...]
        df, dgpost = _rms_bwd(fv, _rstd(fv), gpost_ref[...], scale * dh)
        df_ref[...] = df.astype(BF16)

        @pl.when(i == 0)
        def _():
            dgpre_ref[...] = jnp.zeros_like(dgpre_ref)
            dgpost_ref[...] = jnp.zeros_like(dgpost_ref)

        dgpre_ref[...] += jnp.sum(dgpre, axis=0, keepdims=True)
        dgpost_ref[...] += jnp.sum(dgpost, axis=0, keepdims=True)

    gain = _fix_spec((1, D_MODEL))
    row = _row_spec(D_MODEL)
    return pl.pallas_call(
        body, grid=(N_ROW_TILES,), in_specs=[row, row, gain, row, row, gain],
        out_specs=[row, row, gain, gain],
        out_shape=[SDS((SEQ, D_MODEL), F32), SDS((SEQ, D_MODEL), BF16), SDS((1, D_MODEL), F32), SDS((1, D_MODEL), F32)],
        name=name, compiler_params=_params(("arbitrary",)))(da, h, g_pre, dh_res, fprev, g_post)


def _bwd_pre_only(da, h, g_pre, dh_res):
    def body(da_ref, h_ref, gpre_ref, dhr_ref, dh_ref, dgpre_ref):
        i = pl.program_id(0)
        hv = h_ref[...]
        dxa, dgpre = _rms_bwd(hv, _rstd(hv), gpre_ref[...], da_ref[...])
        dh_ref[...] = dhr_ref[...] + dxa

        @pl.when(i == 0)
        def _():
            dgpre_ref[...] = jnp.zeros_like(dgpre_ref)

        dgpre_ref[...] += jnp.sum(dgpre, axis=0, keepdims=True)

    gain = _fix_spec((1, D_MODEL))
    row = _row_spec(D_MODEL)
    return pl.pallas_call(
        body, grid=(N_ROW_TILES,), in_specs=[row, row, gain, row], out_specs=[row, gain],
        out_shape=[SDS((SEQ, D_MODEL), F32), SDS((1, D_MODEL), F32)],
        name="bwd_pre_only", compiler_params=_params(("arbitrary",)))(da, h, g_pre, dh_res)


def _ffn_fwd(a, wg, wu, wd, name):
    def body(a_ref, wg_ref, wu_ref, wd_ref, gate_ref, up_ref, f_ref):
        j = pl.program_id(0)

        def tile(i, carry):
            rows = _rows(i)
            at = a_ref[rows, :]
            gate = _dot(at, wg_ref[...])
            up = _dot(at, wu_ref[...])
            gate_ref[rows, :] = gate
            up_ref[rows, :] = up
            act = (gate * jax.nn.sigmoid(gate) * up).astype(BF16)
            contrib = _dot(act, wd_ref[...])

            @pl.when(j == 0)
            def _():
                f_ref[rows, :] = contrib

            @pl.when(j != 0)
            def _():
                f_ref[rows, :] += contrib

            return carry

        lax.fori_loop(0, N_ROW_TILES, tile, 0)

    shard_cols = pl.BlockSpec((None, D_MODEL, FF_SHARD), lambda j: (j, 0, 0))
    shard_rows = pl.BlockSpec((None, FF_SHARD, D_MODEL), lambda j: (j, 0, 0))
    hid = pl.BlockSpec((None, SEQ, FF_SHARD), lambda j: (j, 0, 0))
    full = pl.BlockSpec((SEQ, D_MODEL), lambda j: (0, 0))
    return pl.pallas_call(
        body, grid=(N_DEV,), in_specs=[full, shard_cols, shard_cols, shard_rows], out_specs=[hid, hid, full],
        out_shape=[SDS((N_DEV, SEQ, FF_SHARD), F32), SDS((N_DEV, SEQ, FF_SHARD), F32), SDS((SEQ, D_MODEL), F32)],
        name=name, compiler_params=_params(("arbitrary",)))(a, wg, wu, wd)


def _ffn_bwd(df, a, gate, up, wg, wu, wd, name):
    def body(df_ref, a_ref, gate_ref, up_ref, wg_ref, wu_ref, wd_ref, da_ref, dwg_ref, dwu_ref, dwd_ref,
             acc_g, acc_u, acc_d):
        j = pl.program_id(0)

        def tile(i, carry):
            rows = _rows(i)
            dft = df_ref[rows, :]
            at = a_ref[rows, :]
            gate = gate_ref[rows, :]
            up = up_ref[rows, :]
            dact = _dot_nt(dft, wd_ref[...])
            sig = jax.nn.sigmoid(gate)
            silu = gate * sig
            dgate = (dact * up * (sig * (1.0 + gate * (1.0 - sig)))).astype(BF16)
            dup = (dact * silu).astype(BF16)
            act = (silu * up).astype(BF16)
            dwd = _dot_tn(act, dft)
            dwg = _dot_tn(at, dgate)
            dwu = _dot_tn(at, dup)
            dat = _dot_nt(dgate, wg_ref[...]) + _dot_nt(dup, wu_ref[...])

            @pl.when(i == 0)
            def _():
                acc_d[...] = dwd
                acc_g[...] = dwg
                acc_u[...] = dwu

            @pl.when(i != 0)
            def _():
                acc_d[...] += dwd
                acc_g[...] += dwg
                acc_u[...] += dwu

            @pl.when(j == 0)
            def _():
                da_ref[rows, :] = dat

            @pl.when(j != 0)
            def _():
                da_ref[rows, :] += dat

            return carry

        lax.fori_loop(0, N_ROW_TILES, tile, 0)
        dwg_ref[...] = acc_g[...].astype(BF16)
        dwu_ref[...] = acc_u[...].astype(BF16)
        dwd_ref[...] = acc_d[...].astype(BF16)

    shard_cols = pl.BlockSpec((None, D_MODEL, FF_SHARD), lambda j: (j, 0, 0))
    shard_rows = pl.BlockSpec((None, FF_SHARD, D_MODEL), lambda j: (j, 0, 0))
    hid = pl.BlockSpec((None, SEQ, FF_SHARD), lambda j: (j, 0, 0))
    full = pl.BlockSpec((SEQ, D_MODEL), lambda j: (0, 0))
    return pl.pallas_call(
        body, grid=(N_DEV,), in_specs=[full, full, hid, hid, shard_cols, shard_cols, shard_rows],
        out_specs=[full, shard_cols, shard_cols, shard_rows],
        out_shape=[SDS((SEQ, D_MODEL), F32), SDS((N_DEV, D_MODEL, FF_SHARD), BF16),
                   SDS((N_DEV, D_MODEL, FF_SHARD), BF16), SDS((N_DEV, FF_SHARD, D_MODEL), BF16)],
        scratch_shapes=[pltpu.VMEM((D_MODEL, FF_SHARD), F32), pltpu.VMEM((D_MODEL, FF_SHARD), F32),
                        pltpu.VMEM((FF_SHARD, D_MODEL), F32)],
        name=name, compiler_params=_params(("arbitrary",)))(df, a, gate, up, wg, wu, wd)


def _proj_fwd(a, w):
    def body(a_ref, w_ref, o_ref):
        def tile(i, carry):
            rows = _rows(i)
            o_ref[rows, :] = _dot(a_ref[rows, :], w_ref[...])
            return carry

        lax.fori_loop(0, N_ROW_TILES, tile, 0)

    return pl.pallas_call(
        body, grid=(N_DEV,),
        in_specs=[pl.BlockSpec((SEQ, D_MODEL), lambda j: (0, 0)), pl.BlockSpec((None, D_MODEL, IN_SHARD), lambda j: (j, 0, 0))],
        out_specs=pl.BlockSpec((None, SEQ, IN_SHARD), lambda j: (j, 0, 0)),
        out_shape=SDS((N_DEV, SEQ, IN_SHARD), F32), name="proj_fwd",
        compiler_params=_params(("parallel",)))(a, w)


def _proj_bwd(dproj, a, w):
    def body(dp_ref, a_ref, w_ref, da_ref, dw_ref, acc):
        j = pl.program_id(0)

        def tile(i, carry):
            rows = _rows(i)
            dpt = dp_ref[rows, :]
            dw = _dot_tn(a_ref[rows, :], dpt)
            dat = _dot_nt(dpt, w_ref[...])

            @pl.when(i == 0)
            def _():
                acc[...] = dw

            @pl.when(i != 0)
            def _():
                acc[...] += dw

            @pl.when(j == 0)
            def _():
                da_ref[rows, :] = dat

            @pl.when(j != 0)
            def _():
                da_ref[rows, :] += dat

            return carry

        lax.fori_loop(0, N_ROW_TILES, tile, 0)
        dw_ref[...] = acc[...].astype(BF16)

    full = pl.BlockSpec((SEQ, D_MODEL), lambda j: (0, 0))
    wspec = pl.BlockSpec((None, D_MODEL, IN_SHARD), lambda j: (j, 0, 0))
    return pl.pallas_call(
        body, grid=(N_DEV,),
        in_specs=[pl.BlockSpec((None, SEQ, IN_SHARD), lambda j: (j, 0, 0)), full, wspec],
        out_specs=[full, wspec],
        out_shape=[SDS((SEQ, D_MODEL), F32), SDS((N_DEV, D_MODEL, IN_SHARD), BF16)],
        scratch_shapes=[pltpu.VMEM((D_MODEL, IN_SHARD), F32)],
        name="proj_bwd", compiler_params=_params(("arbitrary",)))(dproj, a, w)


def _na_consts():
    c = np.arange(GRID_W)
    col_start = np.clip(c - KW // 2, 0, GRID_W - KW)
    col_in = (c[None, :] >= col_start[:, None]) & (c[None, :] < col_start[:, None] + KW)
    dc = np.clip(c[None, :] - c[:, None] + KW - 1, 0, 2 * KW - 2)
    onehot = np.zeros((128, GRID_W * GRID_W), np.float32)
    qq, kk = np.meshgrid(c, c, indexing="ij")
    onehot[dc[col_in], (qq * GRID_W + kk)[col_in]] = 1.0
    negmask = np.where(col_in, 0.0, NEG_INF).astype(np.float32).reshape(1, -1)
    sel = np.zeros((16, 64), np.float32)
    for off in range(8):
        for kh in range(KH):
            sel[off + kh, off * 8 + kh] = 1.0
    return onehot, negmask, sel


def _rpb_expand(rpb):
    onehot, negmask, _ = _na_consts()
    rows = HEADS * (2 * KH - 1)
    rpb_pad = jnp.pad(rpb.reshape(rows, 2 * KW - 1), ((0, 128 - rows), (0, 128 - (2 * KW - 1))))

    def body(r_ref, oh_ref, m_ref, t_ref):
        hi, mid, lo = _split3(r_ref[...])
        oh = oh_ref[...]
        t_ref[...] = _dot(hi, oh) + _dot(mid, oh) + _dot(lo, oh) + m_ref[...]

    table = pl.pallas_call(body, out_shape=SDS((128, GRID_W * GRID_W), F32), name="rpb_expand",
                           compiler_params=_params())(rpb_pad, jnp.asarray(onehot, BF16), jnp.asarray(negmask))
    t4 = table[:rows].reshape(HEADS, 2 * KH - 1, GRID_W, GRID_W)
    per_off = jnp.stack([t4[:, o:o + KH] for o in range(8)], axis=1)
    return per_off.transpose(0, 1, 3, 2, 4).reshape(HEADS, 8, GRID_W, KEYS)


def _rpb_reduce(dbias):
    onehot, _, sel = _na_consts()
    x = dbias.reshape(HEADS, 8, GRID_W, KH, GRID_W).transpose(0, 1, 3, 2, 4).reshape(HEADS, 64, GRID_W * GRID_W)

    def body(x_ref, oht_ref, sel_ref, o_ref):
        hi, mid, lo = _split3(x_ref[...])
        oht = oht_ref[...]
        y = _dot(hi, oht) + _dot(mid, oht) + _dot(lo, oht)
        hi, mid, lo = _split3(y)
        s = sel_ref[...]
        o_ref[...] = _dot(s, hi) + _dot(s, mid) + _dot(s, lo)

    return pl.pallas_call(
        body, grid=(HEADS,),
        in_specs=[pl.BlockSpec((None, 64, GRID_W * GRID_W), lambda h: (h, 0, 0)),
                  pl.BlockSpec((GRID_W * GRID_W, 128), lambda h: (0, 0)), pl.BlockSpec((16, 64), lambda h: (0, 0))],
        out_specs=pl.BlockSpec((None, 16, 128), lambda h: (h, 0, 0)),
        out_shape=SDS((HEADS, 16, 128), F32), name="rpb_reduce",
        compiler_params=_params(("parallel",)))(x, jnp.asarray(onehot.T, BF16), jnp.asarray(sel, BF16))


def _block_geometry(r):
    row_start = jnp.clip(r - KH // 2, 0, GRID_ROWS - KH)
    off = row_start - r + (KH - 1)
    q0 = pl.multiple_of(N_META + r * GRID_W, 16)
    k0 = pl.multiple_of(N_META + row_start * GRID_W, 16)
    return off, q0, k0


def _na_probs(q, kk, km, bias):
    s = _dot_nt(q, kk) * ATT_SCALE + bias
    sm = _dot_nt(q, km) * ATT_SCALE
    m = jnp.maximum(jnp.max(s, axis=-1, keepdims=True), jnp.max(sm, axis=-1, keepdims=True))
    p = jnp.exp(s - m)
    pm = jnp.exp(sm - m)
    inv = 1.0 / (jnp.sum(p, axis=-1, keepdims=True) + jnp.sum(pm, axis=-1, keepdims=True))
    return p * inv, pm * inv


def _meta_probs(qm, km):
    s = _dot_nt(qm, km) * ATT_SCALE
    p = jnp.exp(s - jnp.max(s, axis=-1, keepdims=True))
    return p / jnp.sum(p, axis=-1, keepdims=True)


def _na_fwd(q, k, v, bias):
    def body(q_ref, k_ref, v_ref, b_ref, o_ref):
        km = k_ref[0:N_META, :].astype(BF16)
        vm = v_ref[0:N_META, :].astype(BF16)
        pmm = _meta_probs(q_ref[0:N_META, :].astype(BF16), km)
        o_ref[0:N_META, :] = _dot(pmm.astype(BF16), vm)

        def block(r, carry):
            off, q0, k0 = _block_geometry(r)
            qb = q_ref[pl.ds(q0, GRID_W), :].astype(BF16)
            kk = k_ref[pl.ds(k0, KEYS), :].astype(BF16)
            vv = v_ref[pl.ds(k0, KEYS), :].astype(BF16)
            p, pm = _na_probs(qb, kk, km, b_ref[off])
            o_ref[pl.ds(q0, GRID_W), :] = _dot(p.astype(BF16), vv) + _dot(pm.astype(BF16), vm)
            return carry

        lax.fori_loop(0, GRID_ROWS, block, 0, unroll=NA_UNROLL)

    head = pl.BlockSpec((None, SEQ, HEAD_DIM), lambda h: (h, 0, 0))
    return pl.pallas_call(
        body, grid=(HEADS,), in_specs=[head, head, head, pl.BlockSpec((None, 8, GRID_W, KEYS), lambda h: (h, 0, 0, 0))],
        out_specs=head, out_shape=SDS((HEADS, SEQ, HEAD_DIM), F32), name="na_fwd",
        compiler_params=_params(("parallel",)))(q, k, v, bias)


def _na_bwd(q, k, v, bias, do):
    def body(q_ref, k_ref, v_ref, b_ref, do_ref, dq_ref, dk_ref, dv_ref, db_ref):
        km = k_ref[0:N_META, :].astype(BF16)
        vm = v_ref[0:N_META, :].astype(BF16)
        dk_ref[...] = jnp.zeros_like(dk_ref)
        dv_ref[...] = jnp.zeros_like(dv_ref)
        db_ref[...] = jnp.zeros_like(db_ref)

        qm = q_ref[0:N_META, :].astype(BF16)
        dom = do_ref[0:N_META, :].astype(BF16)
        pmm = _meta_probs(qm, km)
        dpm = _dot_nt(dom, vm)
        dsm = (pmm * (dpm - jnp.sum(pmm * dpm, axis=-1, keepdims=True)) * ATT_SCALE).astype(BF16)
        dq_ref[0:N_META, :] = _dot(dsm, km)
        dkm0 = _dot_tn(dsm, qm)
        dvm0 = _dot_tn(pmm.astype(BF16), dom)

        def block(r, carry):
            dkm, dvm = carry
            off, q0, k0 = _block_geometry(r)
            qb = q_ref[pl.ds(q0, GRID_W), :].astype(BF16)
            kk = k_ref[pl.ds(k0, KEYS), :].astype(BF16)
            vv = v_ref[pl.ds(k0, KEYS), :].astype(BF16)
            dob = do_ref[pl.ds(q0, GRID_W), :].astype(BF16)
            p, pm = _na_probs(qb, kk, km, b_ref[off])
            dp = _dot_nt(dob, vv)
            dpm_ = _dot_nt(dob, vm)
            delta = jnp.sum(p * dp, axis=-1, keepdims=True) + jnp.sum(pm * dpm_, axis=-1, keepdims=True)
            ds = p * (dp - delta)
            dsm_ = pm * (dpm_ - delta)
            db_ref[off] += ds
            dsb = (ds * ATT_SCALE).astype(BF16)
            dsmb = (dsm_ * ATT_SCALE).astype(BF16)
            dq_ref[pl.ds(q0, GRID_W), :] = _dot(dsb, kk) + _dot(dsmb, km)
            dk_ref[pl.ds(k0, KEYS), :] += _dot_tn(dsb, qb)
            dv_ref[pl.ds(k0, KEYS), :] += _dot_tn(p.astype(BF16), dob)
            return dkm + _dot_tn(dsmb, qb), dvm + _dot_tn(pm.astype(BF16), dob)

        dkm, dvm = lax.fori_loop(0, GRID_ROWS, block, (dkm0, dvm0), unroll=NA_UNROLL)
        dk_ref[0:N_META, :] = dkm
        dv_ref[0:N_META, :] = dvm

    head = pl.BlockSpec((None, SEQ, HEAD_DIM), lambda h: (h, 0, 0))
    bspec = pl.BlockSpec((None, 8, GRID_W, KEYS), lambda h: (h, 0, 0, 0))
    return pl.pallas_call(
        body, grid=(HEADS,), in_specs=[head, head, head, bspec, head], out_specs=[head, head, head, bspec],
        out_shape=[SDS((HEADS, SEQ, HEAD_DIM), F32)] * 3 + [SDS((HEADS, 8, GRID_W, KEYS), F32)],
        name="na_bwd", compiler_params=_params(("parallel",)))(q, k, v, bias, do)


def _cmul(ar, ai, br, bi):
    return ar * br - ai * bi, ar * bi + ai * br


def _cpow(ar, ai, n):
    rr, ri = None, None
    br, bi = ar, ai
    while n:
        if n & 1:
            rr, ri = (br, bi) if rr is None else _cmul(rr, ri, br, bi)
        n >>= 1
        if n:
            br, bi = _cmul(br, bi, br, bi)
    return rr, ri


def _s5_prep(lr, li, logdt, bre, bim):
    def body(lr_ref, li_ref, dt_ref, br_ref, bi_ref, lbr_ref, lbi_ref, bbr_ref, bbi_ref):
        lr_, li_ = lr_ref[...], li_ref[...]
        dt = jnp.exp(dt_ref[...])
        mag = jnp.exp(lr_ * dt)
        lbr = mag * jnp.cos(li_ * dt)
        lbi = mag * jnp.sin(li_ * dt)
        lbr_ref[...] = lbr
        lbi_ref[...] = lbi
        den = lr_ * lr_ + li_ * li_
        xr = lbr - 1.0
        cr = (xr * lr_ + lbi * li_) / den
        ci = (lbi * lr_ - xr * li_) / den
        br, bi = br_ref[...], bi_ref[...]
        bbr_ref[...] = cr[:, None, :] * br - ci[:, None, :] * bi
        bbi_ref[...] = cr[:, None, :] * bi + ci[:, None, :] * br

    n = 2 * S5_GROUPS
    return pl.pallas_call(
        body, out_shape=[SDS((n, S5_STATE), F32)] * 2 + [SDS((n, S5_GROUP, S5_STATE), F32)] * 2,
        name="s5_prep", compiler_params=_params())(lr, li, logdt, bre, bim)


def _s5_prep_bwd(lr, li, logdt, bre, bim, dar, dai, dbbr, dbbi):
    def body(lr_ref, li_ref, dt_ref, br_ref, bi_ref, dar_ref, dai_ref, dbr_ref, dbi_ref,
             glr_ref, gli_ref, gdt_ref, gbr_ref, gbi_ref):
        lr_, li_ = lr_ref[...], li_ref[...]
        dt = jnp.exp(dt_ref[...])
        mag = jnp.exp(lr_ * dt)
        lbr = mag * jnp.cos(li_ * dt)
        lbi = mag * jnp.sin(li_ * dt)
        den = lr_ * lr_ + li_ * li_
        xr = lbr - 1.0
        cr = (xr * lr_ + lbi * li_) / den
        ci = (lbi * lr_ - xr * li_) / den
        br, bi = br_ref[...], bi_ref[...]
        dbr, dbi = dbr_ref[...], dbi_ref[...]
        gbr_ref[...] = cr[:, None, :] * dbr + ci[:, None, :] * dbi
        gbi_ref[...] = cr[:, None, :] * dbi - ci[:, None, :] * dbr
        gcr = jnp.sum(dbr * br + dbi * bi, axis=1)
        gci = jnp.sum(dbi * br - dbr * bi, axis=1)
        ilr, ili = lr_ / den, li_ / den
        tr, ti = _cmul(gcr, gci, ilr, ili)
        glbr = dar_ref[...] + tr
        glbi = dai_ref[...] + ti
        dr_, di_ = _cmul(tr, ti, cr, -ci)
        gwr, gwi = _cmul(glbr, glbi, lbr, -lbi)
        glr_ref[...] = gwr * dt - dr_
        gli_ref[...] = gwi * dt - di_
        gdt_ref[...] = jnp.sum(gwr * lr_ + gwi * li_, axis=-1, keepdims=True) * dt

    n = 2 * S5_GROUPS
    return pl.pallas_call(
        body, out_shape=[SDS((n, S5_STATE), F32)] * 2 + [SDS((n, 1), F32)] + [SDS((n, S5_GROUP, S5_STATE), F32)] * 2,
        name="s5_prep_bwd", compiler_params=_params())(lr, li, logdt, bre, bim, dar, dai, dbbr, dbbi)


def _scan_local(xr_ref, xi_ref, ar8, ai8, reverse):
    def step(i, carry):
        sr, si = carry
        idx = (SCAN_T - 1 - i) if reverse else i
        rows = pl.ds(pl.multiple_of(idx * SCAN_BLOCKS, SCAN_BLOCKS), SCAN_BLOCKS)
        nr = ar8 * sr - ai8 * si + xr_ref[rows, :]
        ni = ar8 * si + ai8 * sr + xi_ref[rows, :]
        xr_ref[rows, :] = nr
        xi_ref[rows, :] = ni
        return nr, ni

    z = jnp.zeros(ar8.shape, F32)
    return lax.fori_loop(0, SCAN_T, step, (z, z))


def _scan_carries(er, ei, atr, ati, reverse):
    row = lax.broadcasted_iota(jnp.int32, er.shape, 0)
    cr = jnp.zeros((1, er.shape[1]), F32)
    ci = cr
    outr = jnp.zeros(er.shape, F32)
    outi = outr
    order = range(SCAN_BLOCKS - 1, -1, -1) if reverse else range(SCAN_BLOCKS)
    for b in order:
        outr = jnp.where(row == b, cr, outr)
        outi = jnp.where(row == b, ci, outi)
        nr, ni = _cmul(atr, ati, cr, ci)
        cr, ci = nr + er[b:b + 1, :], ni + ei[b:b + 1, :]
    return outr, outi


def _scan_fixup(xr_ref, xi_ref, cr8, ci8, ar8, ai8, reverse):
    def step(i, carry):
        pr, pi = carry
        idx = (SCAN_T - 1 - i) if reverse else i
        rows = pl.ds(pl.multiple_of(idx * SCAN_BLOCKS, SCAN_BLOCKS), SCAN_BLOCKS)
        fr, fi = _cmul(pr, pi, cr8, ci8)
        xr_ref[rows, :] += fr
        xi_ref[rows, :] += fi
        return _cmul(pr, pi, ar8, ai8)

    lax.fori_loop(0, SCAN_T, step, (ar8, ai8))


def _scan(xr_ref, xi_ref, ar, ai, reverse):
    n = ar.shape[1]
    ar8 = jnp.broadcast_to(ar, (SCAN_BLOCKS, n))
    ai8 = jnp.broadcast_to(ai, (SCAN_BLOCKS, n))
    er, ei = _scan_local(xr_ref, xi_ref, ar8, ai8, reverse)
    atr, ati = _cpow(ar, ai, SCAN_T)
    cr8, ci8 = _scan_carries(er, ei, atr, ati, reverse)
    _scan_fixup(xr_ref, xi_ref, cr8, ci8, ar8, ai8, reverse)


def _s5_specs():
    chan = pl.BlockSpec((SEQ, CH_W), lambda c, d: (0, c))
    chan2 = pl.BlockSpec((None, SEQ, CH_W), lambda c, d: (d, 0, c))
    state = pl.BlockSpec((None, SEQ, ST_W), lambda c, d: (d, 0, c))
    bmat = pl.BlockSpec((None, None, CH_W, ST_W), lambda c, d: (d, c, 0, 0))
    cmat = pl.BlockSpec((None, None, ST_W, CH_W), lambda c, d: (d, c, 0, 0))
    avec = pl.BlockSpec((None, None, 1, ST_W), lambda c, d: (d, c, 0, 0))
    return chan, chan2, state, bmat, cmat, avec


def _scan_by_direction(xr_ref, xi_ref, ar, ai, d, adjoint):
    @pl.when(d == 0)
    def _():
        _scan(xr_ref, xi_ref, ar, ai, reverse=adjoint)

    @pl.when(d == 1)
    def _():
        _scan(xr_ref, xi_ref, ar, ai, reverse=not adjoint)


def _s5_scan_fwd(u, bre, bim, are, aim, cre, cim):
    def body(u_ref, bre_ref, bim_ref, are_ref, aim_ref, cre_ref, cim_ref, sr_ref, si_ref, y_ref):
        ub = u_ref[...].astype(BF16)
        sr_ref[...] = _dot(ub, bre_ref[...])
        si_ref[...] = _dot(ub, bim_ref[...])
        _scan_by_direction(sr_ref, si_ref, are_ref[...], aim_ref[...], pl.program_id(1), adjoint=False)
        y_ref[...] = _dot(sr_ref[...].astype(BF16), cre_ref[...]) - _dot(si_ref[...].astype(BF16), cim_ref[...])

    chan, chan2, state, bmat, cmat, avec = _s5_specs()
    return pl.pallas_call(
        body, grid=(S5_CHUNKS, 2), in_specs=[chan, bmat, bmat, avec, avec, cmat, cmat], out_specs=[state, state, chan2],
        out_shape=[SDS((2, SEQ, S5_GROUPS * S5_STATE), F32)] * 2 + [SDS((2, SEQ, S5_WIDTH), F32)],
        name="s5_scan_fwd", compiler_params=_params(("parallel", "parallel")))(u, bre, bim, are, aim, cre, cim)


def _dlam(gr_ref, gi_ref, sr_ref, si_ref, reverse):
    tile = lambda i: pl.ds(pl.multiple_of(i * SCAN_BLOCKS, SCAN_BLOCKS), SCAN_BLOCKS)
    row = lax.broadcasted_iota(jnp.int32, (SCAN_BLOCKS, ST_W), 0)
    if reverse:
        edge, src, shift, empty, lo, hi, dprev = SCAN_T - 1, 0, SCAN_BLOCKS - 1, SCAN_BLOCKS - 1, 0, SCAN_T - 1, 1
    else:
        edge, src, shift, empty, lo, hi, dprev = 0, SCAN_T - 1, 1, 0, 1, SCAN_T, -1
    spr = jnp.where(row == empty, 0.0, pltpu.roll(sr_ref[tile(src), :], shift, 0))
    spi = jnp.where(row == empty, 0.0, pltpu.roll(si_ref[tile(src), :], shift, 0))
    acc0 = _cmul(gr_ref[tile(edge), :], gi_ref[tile(edge), :], spr, -spi)

    def step(i, carry):
        accr, acci = carry
        pr, pi = _cmul(gr_ref[tile(i), :], gi_ref[tile(i), :], sr_ref[tile(i + dprev), :], -si_ref[tile(i + dprev), :])
        return accr + pr, acci + pi

    accr, acci = lax.fori_loop(lo, hi, step, acc0)
    return jnp.sum(accr, axis=0, keepdims=True), jnp.sum(acci, axis=0, keepdims=True)


def _s5_scan_bwd(dy, du_skip, u, sr, si, bre, bim, are, aim, cre, cim):
    def body(dy_ref, dus_ref, u_ref, sr_ref, si_ref, bre_ref, bim_ref, are_ref, aim_ref, cre_ref, cim_ref,
             du_ref, dbr_ref, dbi_ref, dcr_ref, dci_ref, dar_ref, dai_ref, gr_ref, gi_ref):
        d = pl.program_id(1)
        dyb = dy_ref[...].astype(BF16)
        gr_ref[...] = _dot_nt(dyb, cre_ref[...])
        gi_ref[...] = -_dot_nt(dyb, cim_ref[...])
        dcr_ref[...] = _dot_tn(sr_ref[...].astype(BF16), dyb)
        dci_ref[...] = -_dot_tn(si_ref[...].astype(BF16), dyb)
        _scan_by_direction(gr_ref, gi_ref, are_ref[...], -aim_ref[...], d, adjoint=True)

        @pl.when(d == 0)
        def _():
            dar_ref[...], dai_ref[...] = _dlam(gr_ref, gi_ref, sr_ref, si_ref, reverse=False)
            du_ref[...] = dus_ref[...]

        @pl.when(d == 1)
        def _():
            dar_ref[...], dai_ref[...] = _dlam(gr_ref, gi_ref, sr_ref, si_ref, reverse=True)

        grb = gr_ref[...].astype(BF16)
        gib = gi_ref[...].astype(BF16)
        du_ref[...] += _dot_nt(grb, bre_ref[...]) + _dot_nt(gib, bim_ref[...])
        ub = u_ref[...].astype(BF16)
        dbr_ref[...] = _dot_tn(ub, grb)
        dbi_ref[...] = _dot_tn(ub, gib)

    chan, _, state, bmat, cmat, avec = _s5_specs()
    return pl.pallas_call(
        body, grid=(S5_CHUNKS, 2), in_specs=[chan, chan, chan, state, state, bmat, bmat, avec, avec, cmat, cmat],
        out_specs=[chan, bmat, bmat, cmat, cmat, avec, avec],
        out_shape=[SDS((SEQ, S5_WIDTH), F32)] + [SDS((2, S5_CHUNKS, CH_W, ST_W), F32)] * 2
                  + [SDS((2, S5_CHUNKS, ST_W, CH_W), F32)] * 2 + [SDS((2, S5_CHUNKS, 1, ST_W), F32)] * 2,
        scratch_shapes=[pltpu.VMEM((SEQ, ST_W), F32), pltpu.VMEM((SEQ, ST_W), F32)],
        name="s5_scan_bwd", compiler_params=_params(("parallel", "arbitrary")))(dy, du_skip, u, sr, si, bre, bim, are, aim, cre, cim)


_GELU_K = math.sqrt(2.0 / math.pi)
_GELU_C = 0.044715


def _gelu(x):
    t = jnp.tanh(_GELU_K * (x + _GELU_C * x * x * x))
    return 0.5 * x * (1.0 + t), t


def _s5_glu_fwd(u, y2, dskip, wglu, bglu):
    def body(u_ref, y0_ref, y1_ref, d_ref, w_ref, b_ref, o_ref, yp_ref):
        ypre = u_ref[...] * d_ref[...] + y0_ref[...] + y1_ref[...]
        yp_ref[...] = ypre
        y, _ = _gelu(ypre)
        z = _dot(y.astype(BF16), w_ref[...]) + b_ref[...]
        o_ref[...] = y * jax.nn.sigmoid(z)

    row = _row_spec(S5_WIDTH)
    vec = _fix_spec((1, S5_WIDTH))
    dir0 = pl.BlockSpec((None, ROW_TILE, S5_WIDTH), lambda i: (0, i, 0))
    dir1 = pl.BlockSpec((None, ROW_TILE, S5_WIDTH), lambda i: (1, i, 0))
    return pl.pallas_call(
        body, grid=(N_ROW_TILES,), in_specs=[row, dir0, dir1, vec, _fix_spec((S5_WIDTH, S5_WIDTH)), vec],
        out_specs=[row, row], out_shape=[SDS((SEQ, S5_WIDTH), F32)] * 2, name="s5_glu_fwd",
        compiler_params=_params(("parallel",)))(u, y2, y2, dskip, wglu, bglu)


def _s5_glu_bwd(do, ypre, u, dskip, wglu, bglu):
    def body(do_ref, yp_ref, u_ref, d_ref, w_ref, b_ref, dyp_ref, du_ref, dw_ref, db_ref, dd_ref):
        i = pl.program_id(0)
        ypre = yp_ref[...]
        y, t = _gelu(ypre)
        yb = y.astype(BF16)
        sg = jax.nn.sigmoid(_dot(yb, w_ref[...]) + b_ref[...])
        dov = do_ref[...]
        dz = dov * y * sg * (1.0 - sg)
        dzb = dz.astype(BF16)
        dy = dov * sg + _dot_nt(dzb, w_ref[...])
        dgelu = 0.5 * (1.0 + t) + 0.5 * ypre * (1.0 - t * t) * _GELU_K * (1.0 + 3.0 * _GELU_C * ypre * ypre)
        dyp = dy * dgelu
        dyp_ref[...] = dyp
        uv = u_ref[...]
        du_ref[...] = dyp * d_ref[...]

        @pl.when(i == 0)
        def _():
            dw_ref[...] = jnp.zeros_like(dw_ref)
            db_ref[...] = jnp.zeros_like(db_ref)
            dd_ref[...] = jnp.zeros_like(dd_ref)

        dw_ref[...] += _dot_tn(yb, dzb)
        db_ref[...] += jnp.sum(dz, axis=0, keepdims=True)
        dd_ref[...] += jnp.sum(dyp * uv, axis=0, keepdims=True)

    row = _row_spec(S5_WIDTH)
    vec = _fix_spec((1, S5_WIDTH))
    mat = _fix_spec((S5_WIDTH, S5_WIDTH))
    return pl.pallas_call(
        body, grid=(N_ROW_TILES,), in_specs=[row, row, row, vec, mat, vec], out_specs=[row, row, mat, vec, vec],
        out_shape=[SDS((SEQ, S5_WIDTH), F32)] * 2 + [SDS((S5_WIDTH, S5_WIDTH), F32), SDS((1, S5_WIDTH), F32), SDS((1, S5_WIDTH), F32)],
        name="s5_glu_bwd", compiler_params=_params(("arbitrary",)))(do, ypre, u, dskip, wglu, bglu)


def _mix_out_fwd(ona, os5, g_na, g_s5, wout):
    def body(a_ref, s_ref, ga_ref, gs_ref, w_ref, o_ref):
        av, sv = a_ref[...], s_ref[...]
        ca = (av * _rstd(av) * ga_ref[...]).astype(BF16)
        cs = (sv * _rstd(sv) * gs_ref[...]).astype(BF16)
        o_ref[...] = _dot(ca, w_ref[0:NA_WIDTH, :]) + _dot(cs, w_ref[NA_WIDTH:, :])

    row = _row_spec(NA_WIDTH)
    vec = _fix_spec((1, NA_WIDTH))
    return pl.pallas_call(
        body, grid=(N_ROW_TILES,), in_specs=[row, row, vec, vec, _fix_spec((D_MODEL, D_MODEL))],
        out_specs=_row_spec(D_MODEL), out_shape=SDS((SEQ, D_MODEL), F32), name="mix_out_fwd",
        compiler_params=_params(("parallel",)))(ona, os5, g_na, g_s5, wout)


def _mix_out_bwd(dmix, ona, os5, g_na, g_s5, wout):
    def body(dm_ref, a_ref, s_ref, ga_ref, gs_ref, w_ref, da_ref, ds_ref, dw_ref, dga_ref, dgs_ref):
        i = pl.program_id(0)
        dm = dm_ref[...]
        av, sv = a_ref[...], s_ref[...]
        ra, rs = _rstd(av), _rstd(sv)
        ga, gs = ga_ref[...], gs_ref[...]
        ca = (av * ra * ga).astype(BF16)
        cs = (sv * rs * gs).astype(BF16)
        dca = _dot_nt(dm, w_ref[0:NA_WIDTH, :])
        dcs = _dot_nt(dm, w_ref[NA_WIDTH:, :])
        da, dga = _rms_bwd(av, ra, ga, dca)
        ds, dgs = _rms_bwd(sv, rs, gs, dcs)
        da_ref[...] = da
        ds_ref[...] = ds

        @pl.when(i == 0)
        def _():
            dw_ref[...] = jnp.zeros_like(dw_ref)
            dga_ref[...] = jnp.zeros_like(dga_ref)
            dgs_ref[...] = jnp.zeros_like(dgs_ref)

        dw_ref[0:NA_WIDTH, :] += _dot_tn(ca, dm)
        dw_ref[NA_WIDTH:, :] += _dot_tn(cs, dm)
        dga_ref[...] += jnp.sum(dga, axis=0, keepdims=True)
        dgs_ref[...] += jnp.sum(dgs, axis=0, keepdims=True)

    row = _row_spec(NA_WIDTH)
    vec = _fix_spec((1, NA_WIDTH))
    mat = _fix_spec((D_MODEL, D_MODEL))
    return pl.pallas_call(
        body, grid=(N_ROW_TILES,), in_specs=[_row_spec(D_MODEL), row, row, vec, vec, mat],
        out_specs=[row, row, mat, vec, vec],
        out_shape=[SDS((SEQ, NA_WIDTH), F32)] * 2 + [SDS((D_MODEL, D_MODEL), F32), SDS((1, NA_WIDTH), F32), SDS((1, NA_WIDTH), F32)],
        name="mix_out_bwd", compiler_params=_params(("arbitrary",)))(dmix, ona, os5, g_na, g_s5, wout)


def _me():
    x, y, c = lax.axis_index("x"), lax.axis_index("y"), lax.axis_index("c")
    return x, y, c, 4 * x + 2 * y + c


def _peer(k):
    x, y, c, _ = _me()
    px = 1 - x if (k >> 2) & 1 else x
    py = 1 - y if (k >> 1) & 1 else y
    pc = 1 - c if k & 1 else c
    return (px, py, pc), 4 * px + 2 * py + pc


def _exchange(arrays, gather, name):
    n = len(arrays)

    def body(*refs):
        ins, outs = refs[:n], refs[n:2 * n]
        send_sems, recv_sems, local_sems = refs[2 * n:]
        _, _, _, me = _me()
        started = []
        for a in range(n):
            src_mine = ins[a] if gather else ins[a].at[me]
            local = pltpu.make_async_copy(src_mine, outs[a].at[me], local_sems.at[a])
            local.start()
            started.append(local)
        sends = []
        for k in range(1, N_DEV):
            peer, peer_idx = _peer(k)
            for a in range(n):
                src = ins[a] if gather else ins[a].at[peer_idx]
                cp = pltpu.make_async_remote_copy(src_ref=src, dst_ref=outs[a].at[me], send_sem=send_sems.at[a, k - 1],
                                                  recv_sem=recv_sems.at[a, k - 1], device_id=peer, device_id_type=MESH)
                cp.start()
                sends.append(cp)
        for k in range(1, N_DEV):
            peer, peer_idx = _peer(k)
            for a in range(n):
                src = ins[a] if gather else ins[a].at[peer_idx]
                pltpu.make_async_remote_copy(src_ref=src, dst_ref=outs[a].at[peer_idx], send_sem=send_sems.at[a, k - 1],
                                             recv_sem=recv_sems.at[a, k - 1], device_id=peer, device_id_type=MESH).wait_recv()
        for cp in sends:
            cp.wait_send()
        for local in started:
            local.wait()

    hbm = pl.BlockSpec(memory_space=pltpu.HBM)
    out_shape = [SDS((N_DEV,) + tuple(a.shape), a.dtype) if gather else SDS(a.shape, a.dtype) for a in arrays]
    return pl.pallas_call(
        body, in_specs=[hbm] * n, out_specs=[hbm] * n, out_shape=out_shape,
        scratch_shapes=[pltpu.SemaphoreType.DMA((n, N_DEV - 1)), pltpu.SemaphoreType.DMA((n, N_DEV - 1)),
                        pltpu.SemaphoreType.DMA((n,))],
        name=name)(*arrays)


_HBM = pl.BlockSpec(memory_space=pltpu.HBM)
_SEM = pl.BlockSpec(memory_space=pltpu.SEMAPHORE)
_EFFECT = pltpu.SideEffectType.DATAFLOW_SIDE_EFFECTING


def _land_shape(a, gather):
    return (N_DEV,) + tuple(a.shape) if gather else tuple(a.shape)


def _place_own(arrays, gather, name):
    n = len(arrays)

    def body(*refs):
        ins, lands, sems = refs[:n], refs[n:2 * n], refs[-1]
        _, _, _, me = _me()
        copies = [pltpu.make_async_copy(ins[a] if gather else ins[a].at[me], lands[a].at[me], sems.at[a]) for a in range(n)]
        for cp in copies:
            cp.start()
        for cp in copies:
            cp.wait()

    return pl.pallas_call(
        body, in_specs=[_HBM] * n, out_specs=[_HBM] * n,
        out_shape=[pltpu.HBM(_land_shape(a, gather), a.dtype) for a in arrays],
        scratch_shapes=[pltpu.SemaphoreType.DMA((n,))], name=name)(*arrays)


def _exchange_start(arrays, lands, gather, name):
    n = len(arrays)

    def body(*refs):
        ins, lnd = refs[:n], refs[n:2 * n]
        send_sems, recv_sems = refs[2 * n], refs[2 * n + 1]
        token = refs[-1]
        _, _, _, me = _me()
        for k in range(1, N_DEV):
            peer, peer_idx = _peer(k)
            for a in range(n):
                src = ins[a] if gather else ins[a].at[peer_idx]
                s = a * (N_DEV - 1) + k - 1
                pltpu.make_async_remote_copy(src_ref=src, dst_ref=lnd[a].at[me], send_sem=send_sems.at[s],
                                             recv_sem=recv_sems.at[s], device_id=peer, device_id_type=MESH).start()
        token[...] = jnp.zeros_like(token)

    sems = pltpu.SemaphoreType.DMA((n * (N_DEV - 1),))
    out = pl.pallas_call(
        body, name=name, in_specs=[_HBM] * (2 * n),
        out_shape=(sems, sems) + tuple(pltpu.HBM(a.shape, a.dtype) for a in list(arrays) + list(lands)) + (SDS((8, 128), F32),),
        out_specs=(_SEM, _SEM) + (_HBM,) * (2 * n) + (pl.BlockSpec(memory_space=pltpu.VMEM),),
        input_output_aliases={i: 2 + i for i in range(2 * n)},
        compiler_params=pltpu.CompilerParams(has_side_effects=_EFFECT),
    )(*[pltpu.with_memory_space_constraint(a, pltpu.HBM) for a in list(arrays) + list(lands)])
    return out[0], out[1], list(out[2:2 + n]), list(out[2 + n:2 + 2 * n]), out[-1]


def _exchange_wait(send_sems, recv_sems, arrays, lands, after, gather, name):
    n = len(arrays)

    def body(*refs):
        ins, lnd = refs[:n], refs[n:2 * n]
        send_sems, recv_sems = refs[2 * n], refs[2 * n + 1]
        for k in range(1, N_DEV):
            peer, peer_idx = _peer(k)
            for a in range(n):
                src = ins[a] if gather else ins[a].at[peer_idx]
                s = a * (N_DEV - 1) + k - 1
                cp = pltpu.make_async_remote_copy(src_ref=src, dst_ref=lnd[a].at[peer_idx], send_sem=send_sems.at[s],
                                                  recv_sem=recv_sems.at[s], device_id=peer, device_id_type=MESH)
                cp.wait_send()
                cp.wait_recv()

    out = pl.pallas_call(
        body, name=name, in_specs=[_HBM] * (2 * n) + [_SEM, _SEM, pl.BlockSpec(memory_space=pl.ANY)],
        out_shape=tuple(pltpu.HBM(a.shape, a.dtype) for a in list(arrays) + list(lands)),
        out_specs=(_HBM,) * (2 * n), input_output_aliases={i: i for i in range(2 * n)},
        compiler_params=pltpu.CompilerParams(has_side_effects=_EFFECT),
    )(*arrays, *lands, send_sems, recv_sems, after)
    return list(out[n:])


def _adamw_math(w, g, m, v):
    m = ADAM_B1 * m + (1.0 - ADAM_B1) * g
    v = ADAM_B2 * v + (1.0 - ADAM_B2) * (g * g)
    m_hat = m / (1.0 - ADAM_B1 ** ADAM_STEP)
    v_hat = v / (1.0 - ADAM_B2 ** ADAM_STEP)
    delta = -ADAM_LR * (m_hat / (jnp.sqrt(v_hat) + ADAM_EPS) + ADAM_WD * w)
    return delta, m, v


def _adamw(w, m, v, pieces, name):
    rows, cols = w.shape
    tile = rows
    for cand in (256, 176, 128, 64, 16):
        if rows > cand and rows % cand == 0:
            tile = cand
            break

    def body(w_ref, m_ref, v_ref, p_ref, g_ref, d_ref, mo_ref, vo_ref):
        g = p_ref[0].astype(F32)
        for p in range(1, N_DEV):
            g = g + p_ref[p].astype(F32)
        g_ref[...] = g
        d_ref[...], mo_ref[...], vo_ref[...] = _adamw_math(w_ref[...], g, m_ref[...], v_ref[...])

    blk = pl.BlockSpec((tile, cols), lambda i: (i, 0))
    return pl.pallas_call(
        body, grid=(rows // tile,), in_specs=[blk, blk, blk, pl.BlockSpec((N_DEV, tile, cols), lambda i: (0, i, 0))],
        out_specs=[blk] * 4, out_shape=[SDS((rows, cols), F32)] * 4, name=name,
        compiler_params=_params(("parallel",)))(w, m, v, pieces)


def _sum_pieces(p_ref):
    g = p_ref[0].astype(F32)
    for p in range(1, N_DEV):
        g = g + p_ref[p].astype(F32)
    return g


def _adamw_lead(w, m, v, g, tile, name):
    lead, b, c = w.shape

    def body(w_ref, m_ref, v_ref, g_ref, d_ref, mo_ref, vo_ref):
        d_ref[...], mo_ref[...], vo_ref[...] = _adamw_math(w_ref[...], g_ref[...], m_ref[...], v_ref[...])

    blk = pl.BlockSpec((tile, b, c), lambda i: (i, 0, 0))
    return pl.pallas_call(
        body, grid=(lead // tile,), in_specs=[blk] * 4, out_specs=[blk] * 3, out_shape=[SDS(w.shape, F32)] * 3, name=name,
        compiler_params=_params(("parallel",)))(w, m, v, g)


VEC_ROWS = ['ffn1_pre_g', 'ffn1_post_g', 'mix_pre_g', 'mix_post_g', 'ffn2_pre_g', 'ffn2_post_g', 'final_g',
            ('na_out_g', 's5_out_g'), ('s5_d', 's5_b_glu')]
VEC_NAMES = [n for row in VEC_ROWS for n in ((row,) if isinstance(row, str) else row)]
VEC_PACK_ROWS = 16


def _pack_vectors(grads):
    def body(*refs):
        o_ref = refs[-1]
        o_ref[...] = jnp.zeros_like(o_ref)
        k = 0
        for i, row in enumerate(VEC_ROWS):
            if isinstance(row, str):
                o_ref[i:i + 1, :] = refs[k][...]
                k += 1
            else:
                o_ref[i:i + 1, 0:NA_WIDTH] = refs[k][...]
                o_ref[i:i + 1, NA_WIDTH:] = refs[k + 1][...]
                k += 2

    return pl.pallas_call(body, out_shape=SDS((VEC_PACK_ROWS, D_MODEL), F32), name="pack_vectors",
                          compiler_params=_params())(*[grads[n] for n in VEC_NAMES])


def _adamw_small(packed8, dense8, vec_wmv, others):
    n_vec, n_oth = len(VEC_NAMES), len(others)

    def body(*refs):
        p_ref, dense_ref = refs[0], refs[1]
        refs[-1][...] = _sum_pieces(dense_ref)
        refs = refs[1:-1]
        ins = refs[1:1 + 3 * n_vec + 4 * n_oth]
        outs = refs[1 + 3 * n_vec + 4 * n_oth:]
        gsum = _sum_pieces(p_ref)
        k = 0
        for i, row in enumerate(VEC_ROWS):
            parts = [(row, gsum[i:i + 1, :])] if isinstance(row, str) else \
                [(row[0], gsum[i:i + 1, 0:NA_WIDTH]), (row[1], gsum[i:i + 1, NA_WIDTH:])]
            for _, g in parts:
                w_ref, m_ref, v_ref = ins[3 * k:3 * k + 3]
                outs[4 * k][...] = g
                outs[4 * k + 1][...], outs[4 * k + 2][...], outs[4 * k + 3][...] = _adamw_math(w_ref[...], g, m_ref[...], v_ref[...])
                k += 1
        for j in range(n_oth):
            w_ref, m_ref, v_ref, g_ref = ins[3 * n_vec + 4 * j:3 * n_vec + 4 * j + 4]
            g = _sum_pieces(g_ref)
            g = g[tuple(slice(0, s) for s in w_ref.shape)]
            o = outs[4 * (n_vec + j):4 * (n_vec + j) + 4]
            o[0][...] = g
            o[1][...], o[2][...], o[3][...] = _adamw_math(w_ref[...], g, m_ref[...], v_ref[...])

    args, out_shape = [packed8, dense8], []
    for w, m, v in vec_wmv:
        args += [w, m, v]
        out_shape += [SDS(w.shape, F32)] * 4
    for w, m, v, g in others:
        args += [w, m, v, g]
        out_shape += [SDS(w.shape, F32)] * 4
    out_shape += [SDS(dense8.shape[1:], F32)]
    return pl.pallas_call(body, out_shape=out_shape, name="adamw_small", compiler_params=_params())(*args)


def _perm_rows(x):
    return x.reshape(SCAN_BLOCKS, SCAN_T, x.shape[-1]).transpose(1, 0, 2).reshape(SEQ, x.shape[-1])


def _unperm_rows(x):
    return x.reshape(SCAN_T, SCAN_BLOCKS, x.shape[-1]).transpose(1, 0, 2).reshape(SEQ, x.shape[-1])


def _block_diag(x):
    eye = np.eye(8, dtype=bool)[None, None, :, None, :, None]
    full = jnp.where(eye, x[:, :, :, :, None, :], 0.0)
    return full.reshape(2, S5_CHUNKS, 8 * x.shape[3], 8 * x.shape[4])


def _diag_blocks(x, r, c):
    x6 = x.reshape(2, S5_CHUNKS, 8, r, 8, c)
    return jnp.stack([x6[:, :, g, :, g, :] for g in range(8)], axis=2)


def _dep(x, token):
    return x if token is None else x + token


def _local_step(x, target, meta, get_w, small, emit, token0=None):
    h0 = jnp.concatenate([meta, x], axis=0)
    tgt = jnp.concatenate([jnp.zeros((N_META, D_MODEL), F32), target], axis=0)

    wts = dict(get_w("ffn1", None))
    a1 = _prenorm(h0, _dep(small["ffn1_pre_g"], token0))
    gate1, up1, f1 = _ffn_fwd(a1, wts["ffn1_w_gate"], wts["ffn1_w_up"], wts["ffn1_w_down"], "ffn1_fwd")
    h1, a2 = _post_pre(f1, h0, small["ffn1_post_g"], small["mix_pre_g"], 0.5, "post_pre1")
    wts.update(get_w("w_in", a2))
    proj = _proj_fwd(a2, wts["w_in"])
    qkv = proj[:6].reshape(3, 2, SEQ, 4, HEAD_DIM).transpose(0, 1, 3, 2, 4).reshape(3, HEADS, SEQ, HEAD_DIM)
    u = proj[6:].transpose(1, 0, 2).reshape(SEQ, S5_WIDTH)
    bias = _rpb_expand(small["na_rpb"][0])
    o3 = _na_fwd(qkv[0], qkv[1], qkv[2], bias)
    ona = o3.transpose(1, 0, 2).reshape(SEQ, NA_WIDTH)

    lr = small["s5_lam_re"].reshape(64, S5_STATE)
    li = small["s5_lam_im"].reshape(64, S5_STATE)
    logdt = small["s5_log_dt"].reshape(64, 1)
    b_t = [small[n].reshape(64, S5_STATE, S5_GROUP).transpose(0, 2, 1) for n in ("s5_b_re", "s5_b_im")]
    lbr, lbi, bbr, bbi = _s5_prep(lr, li, logdt, b_t[0], b_t[1])
    are = lbr.reshape(2, S5_CHUNKS, 1, ST_W)
    aim = lbi.reshape(2, S5_CHUNKS, 1, ST_W)
    bre = _block_diag(bbr.reshape(2, S5_CHUNKS, 8, S5_GROUP, S5_STATE)).astype(BF16)
    bim = _block_diag(bbi.reshape(2, S5_CHUNKS, 8, S5_GROUP, S5_STATE)).astype(BF16)
    c_t = [small[n].reshape(2, S5_CHUNKS, 8, S5_GROUP, S5_STATE).transpose(0, 1, 2, 4, 3) for n in ("s5_c_re", "s5_c_im")]
    cre = _block_diag(c_t[0]).astype(BF16)
    cim = _block_diag(c_t[1]).astype(BF16)
    u_p = _perm_rows(u)
    sr, si, y2 = _s5_scan_fwd(u_p, bre, bim, are, aim, cre, cim)
    wts.update(get_w("mix", y2))
    os5_p, ypre_p = _s5_glu_fwd(u_p, y2, small["s5_d"], wts["s5_w_glu"], small["s5_b_glu"])
    os5 = _unperm_rows(os5_p)

    mix = _mix_out_fwd(ona, os5, small["na_out_g"], small["s5_out_g"], wts["w_out"])
    h2, a3 = _post_pre(mix, h1, small["mix_post_g"], small["ffn2_pre_g"], 1.0, "post_pre2")
    wts.update(get_w("ffn2", a3))
    gate2, up2, f2 = _ffn_fwd(a3, wts["ffn2_w_gate"], wts["ffn2_w_up"], wts["ffn2_w_down"], "ffn2_fwd")
    loss8, dh3, df2, g_final, g_ffn2_post = _final_loss(f2, h2, small["ffn2_post_g"], small["final_g"], tgt)

    da3, dwg2, dwu2, dwd2 = _ffn_bwd(df2, a3, gate2, up2, wts["ffn2_w_gate"], wts["ffn2_w_up"], wts["ffn2_w_down"], "ffn2_bwd")
    tok = emit("ffn2", {"ffn2_w_gate": dwg2, "ffn2_w_up": dwu2, "ffn2_w_down": dwd2})
    dh2, dmix, g_ffn2_pre, g_mix_post = _bwd_pre_post(da3, h2, _dep(small["ffn2_pre_g"], tok), dh3, mix, small["mix_post_g"], 1.0,
                                                      "bwd_pre_post2")
    dona, dos5, dwout, g_na_out, g_s5_out = _mix_out_bwd(dmix, ona, os5, small["na_out_g"], small["s5_out_g"], wts["w_out"])

    dypre_p, du_skip_p, dwglu, g_b_glu, g_s5_d = _s5_glu_bwd(_perm_rows(dos5), ypre_p, u_p, small["s5_d"], wts["s5_w_glu"],
                                                             small["s5_b_glu"])
    tok = emit("mix", {"s5_w_glu": dwglu.reshape(N_DEV, S5_WIDTH // N_DEV, S5_WIDTH).astype(BF16),
                       "w_out": dwout.reshape(N_DEV, D_MODEL // N_DEV, D_MODEL).astype(BF16)})
    du_p, dbr, dbi, dcr, dci, dar, dai = _s5_scan_bwd(dypre_p, du_skip_p, u_p, sr, si, bre, bim, _dep(are, tok), aim, cre, cim)
    du = _unperm_rows(du_p)
    dbbr = _diag_blocks(dbr, S5_GROUP, S5_STATE).reshape(64, S5_GROUP, S5_STATE)
    dbbi = _diag_blocks(dbi, S5_GROUP, S5_STATE).reshape(64, S5_GROUP, S5_STATE)
    g_lr, g_li, g_dt, g_br, g_bi = _s5_prep_bwd(lr, li, logdt, b_t[0], b_t[1], dar.reshape(64, S5_STATE),
                                                dai.reshape(64, S5_STATE), dbbr, dbbi)
    g_c = [_diag_blocks(d, S5_STATE, S5_GROUP).transpose(0, 1, 2, 4, 3).reshape(2 * S5_GROUPS, S5_GROUP, S5_STATE)
           for d in (dcr, dci)]

    do3 = dona.reshape(SEQ, HEADS, HEAD_DIM).transpose(1, 0, 2)
    dq, dk, dv, dbias = _na_bwd(qkv[0], qkv[1], qkv[2], bias, do3)
    g_rpb = _rpb_reduce(dbias)
    dense = jnp.stack([g.reshape(2 * S5_GROUPS, S5_STATE * S5_GROUP) for g in (g_br.transpose(0, 2, 1), g_bi.transpose(0, 2, 1), *g_c)])
    tok = emit("small", {"dense": dense, "na_rpb": g_rpb,
                         "s5_lam_re": g_lr.reshape(2, S5_GROUPS, S5_STATE), "s5_lam_im": g_li.reshape(2, S5_GROUPS, S5_STATE),
                         "s5_log_dt": g_dt.reshape(2, S5_GROUPS)})
    dqkv = jnp.stack([dq, dk, dv]).reshape(3, 2, 4, SEQ, HEAD_DIM).transpose(0, 1, 3, 2, 4).reshape(6, SEQ, IN_SHARD)
    dproj = jnp.concatenate([dqkv, du.reshape(SEQ, 2, IN_SHARD).transpose(1, 0, 2)], axis=0).astype(BF16)
    da2, dwin = _proj_bwd(dproj, a2, wts["w_in"])
    tok2 = emit("w_in", {"w_in": dwin})
    tok = tok if tok2 is None else tok + tok2
    dh1, df1, g_mix_pre, g_ffn1_post = _bwd_pre_post(da2, h1, _dep(small["mix_pre_g"], tok), dh2, f1, small["ffn1_post_g"], 0.5,
                                                     "bwd_pre_post1")
    da1, dwg1, dwu1, dwd1 = _ffn_bwd(df1, a1, gate1, up1, wts["ffn1_w_gate"], wts["ffn1_w_up"], wts["ffn1_w_down"], "ffn1_bwd")
    dh0, g_ffn1_pre = _bwd_pre_only(da1, h0, small["ffn1_pre_g"], dh1)

    last = {"ffn1_w_gate": dwg1, "ffn1_w_up": dwu1, "ffn1_w_down": dwd1}
    vec_g = {
        "ffn1_pre_g": g_ffn1_pre, "ffn1_post_g": g_ffn1_post, "mix_pre_g": g_mix_pre, "s5_d": g_s5_d, "s5_b_glu": g_b_glu,
        "na_out_g": g_na_out, "s5_out_g": g_s5_out, "mix_post_g": g_mix_post,
        "ffn2_pre_g": g_ffn2_pre, "ffn2_post_g": g_ffn2_post, "final_g": g_final,
    }
    return loss8[0, 0], dh0[N_META:], dh0[:N_META], last, vec_g


WEIGHT_NAMES = ['meta_tokens', 'ffn1_pre_g', 'ffn1_post_g', 'ffn1_w_gate', 'ffn1_w_up', 'ffn1_w_down', 'mix_pre_g', 'w_in',
                'na_rpb', 's5_lam_re', 's5_lam_im', 's5_log_dt', 's5_b_re', 's5_b_im', 's5_c_re', 's5_c_im', 's5_d',
                's5_w_glu', 's5_b_glu', 'na_out_g', 's5_out_g', 'w_out', 'mix_post_g', 'ffn2_pre_g', 'ffn2_post_g',
                'ffn2_w_gate', 'ffn2_w_up', 'ffn2_w_down', 'final_g']
BIG_NAMES = ['ffn1_w_gate', 'ffn1_w_up', 'ffn1_w_down', 'w_in', 's5_w_glu', 'w_out', 'ffn2_w_gate', 'ffn2_w_up', 'ffn2_w_down']
SMALL_NAMES = [n for n in WEIGHT_NAMES if n not in BIG_NAMES and n != 'meta_tokens']
WHOLE_NAMES = ['na_rpb', 's5_lam_re', 's5_lam_im', 's5_log_dt']
LEAD_NAMES = ['s5_b_re', 's5_b_im', 's5_c_re', 's5_c_im']


def kernel(x, meta_tokens, ffn1_pre_g, ffn1_post_g, ffn1_w_gate, ffn1_w_up, ffn1_w_down, mix_pre_g, w_in, na_rpb, s5_lam_re, s5_lam_im, s5_log_dt, s5_b_re, s5_b_im, s5_c_re, s5_c_im, s5_d, s5_w_glu, s5_b_glu, na_out_g, s5_out_g, w_out, mix_post_g, ffn2_pre_g, ffn2_post_g, ffn2_w_gate, ffn2_w_up, ffn2_w_down, final_g, loss_target, m_meta_tokens, m_ffn1_pre_g, m_ffn1_post_g, m_ffn1_w_gate, m_ffn1_w_up, m_ffn1_w_down, m_mix_pre_g, m_w_in, m_na_rpb, m_s5_lam_re, m_s5_lam_im, m_s5_log_dt, m_s5_b_re, m_s5_b_im, m_s5_c_re, m_s5_c_im, m_s5_d, m_s5_w_glu, m_s5_b_glu, m_na_out_g, m_s5_out_g, m_w_out, m_mix_post_g, m_ffn2_pre_g, m_ffn2_post_g, m_ffn2_w_gate, m_ffn2_w_up, m_ffn2_w_down, m_final_g, v_meta_tokens, v_ffn1_pre_g, v_ffn1_post_g, v_ffn1_w_gate, v_ffn1_w_up, v_ffn1_w_down, v_mix_pre_g, v_w_in, v_na_rpb, v_s5_lam_re, v_s5_lam_im, v_s5_log_dt, v_s5_b_re, v_s5_b_im, v_s5_c_re, v_s5_c_im, v_s5_d, v_s5_w_glu, v_s5_b_glu, v_na_out_g, v_s5_out_g, v_w_out, v_mix_post_g, v_ffn2_pre_g, v_ffn2_post_g, v_ffn2_w_gate, v_ffn2_w_up, v_ffn2_w_down, v_final_g):
    args = dict(locals())
    w = {n: args[n] for n in WEIGHT_NAMES}
    m = {n: args["m_" + n] for n in WEIGHT_NAMES}
    v = {n: args["v_" + n] for n in WEIGHT_NAMES}

    first = ["ffn1_w_gate", "ffn1_w_up", "ffn1_w_down"]
    gathered = _exchange([w[n][0].astype(BF16) for n in first] + [w["meta_tokens"]], True, "gather_ffn1")
    wts1 = dict(zip(first, gathered[:-1]))
    meta_full = gathered[-1].transpose(1, 0, 2).reshape(N_META, D_MODEL)
    small = {n: w[n] for n in SMALL_NAMES}

    pending = {}

    def start(group, names, arrays, gather):
        lands = _place_own(arrays, gather, "own_" + group)
        send_sems, recv_sems, arrays, lands, token = _exchange_start(arrays, lands, gather, "start_" + group)
        pending[group] = (names, send_sems, recv_sems, arrays, lands, gather)
        return token[0, 0]

    def finish(group, after):
        names, send_sems, recv_sems, arrays, lands, gather = pending.pop(group)
        return dict(zip(names, _exchange_wait(send_sems, recv_sems, arrays, lands, after, gather, "wait_" + group)))

    token0 = 0.0
    for group, names in (("w_in", ["w_in"]), ("mix", ["s5_w_glu", "w_out"]), ("ffn2", ["ffn2_w_gate", "ffn2_w_up", "ffn2_w_down"])):
        token0 = token0 + start(group, names, [w[n][0].astype(BF16) for n in names], True)

    def get_w(group, after):
        if group == "ffn1":
            return wts1
        got = finish(group, after)
        if group == "mix":
            got = {"s5_w_glu": got["s5_w_glu"].reshape(S5_WIDTH, S5_WIDTH), "w_out": got["w_out"].reshape(D_MODEL, D_MODEL)}
        return got

    def emit(group, grads):
        return start("g_" + group, list(grads), list(grads.values()), group == "small")

    loss_local, grad_x, gmeta, last, vec_g = _local_step(x[0], loss_target[0], meta_full, get_w, small, emit, token0)
    loss = lax.psum(loss_local, AXES)

    gmeta8 = gmeta.reshape(N_META, N_DEV, D_MODEL // N_DEV).transpose(1, 0, 2)
    pieces = dict(zip(first + ["meta_tokens"], _exchange([last[n] for n in first] + [gmeta8], False, "scatter_ffn1")))
    packed8 = _exchange([_pack_vectors(vec_g)], True, "gather_vectors")[0]
    for group in ("g_ffn2", "g_mix", "g_w_in"):
        pieces.update(finish(group, packed8))
    g8 = finish("g_small", packed8)

    res = {}
    for n in BIG_NAMES + ["meta_tokens"]:
        shape = w[n].shape
        w2 = w[n].reshape(shape[-2], shape[-1])
        outs = _adamw(w2, m[n].reshape(w2.shape), v[n].reshape(w2.shape), pieces[n], "adamw_" + n)
        res[n] = [o.reshape(shape) for o in outs]

    outs = _adamw_small(packed8, g8["dense"], [(w[n], m[n], v[n]) for n in VEC_NAMES],
                        [(w[n][0], m[n][0], v[n][0], g8[n]) for n in WHOLE_NAMES])
    for i, n in enumerate(VEC_NAMES + WHOLE_NAMES):
        res[n] = [o.reshape(w[n].shape) for o in outs[4 * i:4 * i + 4]]
    for i, n in enumerate(LEAD_NAMES):
        shape3 = (2 * S5_GROUPS,) + w[n].shape[-2:]
        g = outs[-1][i].reshape(shape3)
        upd = _adamw_lead(w[n].reshape(shape3), m[n].reshape(shape3), v[n].reshape(shape3), g, 8, "adamw_" + n)
        res[n] = [o.reshape(w[n].shape) for o in [g] + list(upd)]

    out = [loss, grad_x[None]]
    for kind in range(4):
        out += [res[n][kind] for n in WEIGHT_NAMES]
    return tuple(out)
```

```python
import functools
import math

import numpy as np
import jax
import jax.numpy as jnp
from jax import lax
from jax.experimental import pallas as pl
from jax.experimental.pallas import tpu as pltpu

F32 = jnp.float32
BF16 = jnp.bfloat16
SDS = jax.ShapeDtypeStruct

D_MODEL = 1024
N_TOK = 2048
N_META = 16
SEQ = N_TOK + N_META
ROW_TILE = 688
N_ROW_TILES = SEQ // ROW_TILE
N_DEV = 8
D_FF = 2816
FF_SHARD = D_FF // N_DEV
IN_SHARD = 256
NA_WIDTH = 512
S5_WIDTH = 512
HEADS = 8
HEAD_DIM = 64
GRID_W = 64
GRID_ROWS = N_TOK // GRID_W
KH = 8
KW = 16
KEYS = KH * GRID_W
NA_UNROLL = 2
S5_GROUPS = 32
S5_GROUP = 16
S5_STATE = 64
S5_CHUNKS = 4
CH_W = S5_WIDTH // S5_CHUNKS
ST_W = S5_GROUPS * S5_STATE // S5_CHUNKS
SCAN_BLOCKS = 8
SCAN_T = SEQ // SCAN_BLOCKS
RMS_EPS = 1e-6
NEG_INF = -1e30
ATT_SCALE = HEAD_DIM ** -0.5
ADAM_LR, ADAM_B1, ADAM_B2, ADAM_EPS, ADAM_WD, ADAM_STEP = 0.001, 0.9, 0.999, 1e-08, 0.01, 10
VMEM_LIMIT = 56 * 1024 * 1024
MESH = pl.DeviceIdType.MESH
AXES = ("x", "y", "c")


def _params(sem=None):
    return pltpu.CompilerParams(dimension_semantics=sem, vmem_limit_bytes=VMEM_LIMIT)


def _dot(a, b):
    return jnp.dot(a, b, preferred_element_type=F32)


def _dot_nt(a, b):
    return lax.dot_general(a, b, (((1,), (1,)), ((), ())), preferred_element_type=F32)


def _dot_tn(a, b):
    return lax.dot_general(a, b, (((0,), (0,)), ((), ())), preferred_element_type=F32)


def _rstd(x):
    return lax.rsqrt(jnp.mean(x * x, axis=-1, keepdims=True) + RMS_EPS)


def _rms_bwd(x, r, g, dy):
    dyg = dy * g
    xr = x * r
    dx = r * (dyg - xr * jnp.mean(dyg * xr, axis=-1, keepdims=True))
    return dx, dy * xr


def _rows(i, size=ROW_TILE):
    return pl.ds(pl.multiple_of(i * size, 16), size)


def _row_spec(width):
    return pl.BlockSpec((ROW_TILE, width), lambda i: (i, 0))


def _fix_spec(shape):
    return pl.BlockSpec(shape, lambda i: (0,) * len(shape))


def _split3(x):
    hi = x.astype(BF16)
    r1 = x - hi.astype(F32)
    mid = r1.astype(BF16)
    lo = (r1 - mid.astype(F32)).astype(BF16)
    return hi, mid, lo


def _prenorm(x, g):
    def body(x_ref, g_ref, a_ref):
        xv = x_ref[...]
        a_ref[...] = (xv * _rstd(xv) * g_ref[...]).astype(BF16)

    return pl.pallas_call(
        body, grid=(N_ROW_TILES,), in_specs=[_row_spec(D_MODEL), _fix_spec((1, D_MODEL))],
        out_specs=_row_spec(D_MODEL), out_shape=SDS((SEQ, D_MODEL), BF16), name="prenorm",
        compiler_params=_params(("parallel",)))(x, g)


def _post_pre(f, hres, g_post, g_next, scale, name):
    def body(f_ref, h_ref, gp_ref, gn_ref, ho_ref, a_ref):
        fv = f_ref[...]
        h = h_ref[...] + scale * (fv * _rstd(fv) * gp_ref[...])
        ho_ref[...] = h
        a_ref[...] = (h * _rstd(h) * gn_ref[...]).astype(BF16)

    return pl.pallas_call(
        body, grid=(N_ROW_TILES,),
        in_specs=[_row_spec(D_MODEL), _row_spec(D_MODEL), _fix_spec((1, D_MODEL)), _fix_spec((1, D_MODEL))],
        out_specs=[_row_spec(D_MODEL), _row_spec(D_MODEL)],
        out_shape=[SDS((SEQ, D_MODEL), F32), SDS((SEQ, D_MODEL), BF16)], name=name,
        compiler_params=_params(("parallel",)))(f, hres, g_post, g_next)


def _final_loss(f2, h2, g_post, g_final, target):
    def body(f_ref, h_ref, gp_ref, gf_ref, t_ref, loss_ref, dh_ref, df_ref, dgf_ref, dgp_ref):
        i = pl.program_id(0)
        fv = f_ref[...]
        r1 = _rstd(fv)
        gp = gp_ref[...]
        h3 = h_ref[...] + 0.5 * (fv * r1 * gp)
        r2 = _rstd(h3)
        gf = gf_ref[...]
        y = h3 * r2 * gf
        row = lax.broadcasted_iota(jnp.int32, (ROW_TILE, 1), 0) + i * ROW_TILE
        err = jnp.where(row >= N_META, y - t_ref[...], 0.0)
        part = 0.5 * jnp.sum(jnp.mean(err * err, axis=-1, keepdims=True))
        dy = err * (1.0 / D_MODEL)
        dh3, dgf = _rms_bwd(h3, r2, gf, dy)
        dh_ref[...] = dh3
        df, dgp = _rms_bwd(fv, r1, gp, 0.5 * dh3)
        df_ref[...] = df.astype(BF16)

        @pl.when(i == 0)
        def _():
            loss_ref[...] = jnp.zeros_like(loss_ref)
            dgf_ref[...] = jnp.zeros_like(dgf_ref)
            dgp_ref[...] = jnp.zeros_like(dgp_ref)

        loss_ref[...] += part
        dgf_ref[...] += jnp.sum(dgf, axis=0, keepdims=True)
        dgp_ref[...] += jnp.sum(dgp, axis=0, keepdims=True)

    gain = _fix_spec((1, D_MODEL))
    return pl.pallas_call(
        body, grid=(N_ROW_TILES,),
        in_specs=[_row_spec(D_MODEL), _row_spec(D_MODEL), gain, gain, _row_spec(D_MODEL)],
        out_specs=[_fix_spec((8, 128)), _row_spec(D_MODEL), _row_spec(D_MODEL), gain, gain],
        out_shape=[SDS((8, 128), F32), SDS((SEQ, D_MODEL), F32), SDS((SEQ, D_MODEL), BF16),
                   SDS((1, D_MODEL), F32), SDS((1, D_MODEL), F32)],
        name="final_loss", compiler_params=_params(("arbitrary",)))(f2, h2, g_post, g_final, target)


def _bwd_pre_post(da, h, g_pre, dh_res, fprev, g_post, scale, name):
    def body(da_ref, h_ref, gpre_ref, dhr_ref, f_ref, gpost_ref, dh_ref, df_ref, dgpre_ref, dgpost_ref):
        i = pl.program_id(0)
        hv = h_ref[...]
        dxa, dgpre = _rms_bwd(hv, _rstd(hv), gpre_ref[...], da_ref[...])
        dh = dhr_ref[...] + dxa
        dh_ref[...] = dh
        fv = f_ref[...]
        df, dgpost = _rms_bwd(fv, _rstd(fv), gpost_ref[...], scale * dh)
        df_ref[...] = df.astype(BF16)

        @pl.when(i == 0)
        def _():
            dgpre_ref[...] = jnp.zeros_like(dgpre_ref)
            dgpost_ref[...] = jnp.zeros_like(dgpost_ref)

        dgpre_ref[...] += jnp.sum(dgpre, axis=0, keepdims=True)
        dgpost_ref[...] += jnp.sum(dgpost, axis=0, keepdims=True)

    gain = _fix_spec((1, D_MODEL))
    row = _row_spec(D_MODEL)
    return pl.pallas_call(
        body, grid=(N_ROW_TILES,), in_specs=[row, row, gain, row, row, gain],
        out_specs=[row, row, gain, gain],
        out_shape=[SDS((SEQ, D_MODEL), F32), SDS((SEQ, D_MODEL), BF16), SDS((1, D_MODEL), F32), SDS((1, D_MODEL), F32)],
        name=name, compiler_params=_params(("arbitrary",)))(da, h, g_pre, dh_res, fprev, g_post)


def _bwd_pre_only(da, h, g_pre, dh_res):
    def body(da_ref, h_ref, gpre_ref, dhr_ref, dh_ref, dgpre_ref):
        i = pl.program_id(0)
        hv = h_ref[...]
        dxa, dgpre = _rms_bwd(hv, _rstd(hv), gpre_ref[...], da_ref[...])
        dh_ref[...] = dhr_ref[...] + dxa

        @pl.when(i == 0)
        def _():
            dgpre_ref[...] = jnp.zeros_like(dgpre_ref)

        dgpre_ref[...] += jnp.sum(dgpre, axis=0, keepdims=True)

    gain = _fix_spec((1, D_MODEL))
    row = _row_spec(D_MODEL)
    return pl.pallas_call(
        body, grid=(N_ROW_TILES,), in_specs=[row, row, gain, row], out_specs=[row, gain],
        out_shape=[SDS((SEQ, D_MODEL), F32), SDS((1, D_MODEL), F32)],
        name="bwd_pre_only", compiler_params=_params(("arbitrary",)))(da, h, g_pre, dh_res)


def _ffn_fwd(a, wg, wu, wd, name):
    def body(a_ref, wg_ref, wu_ref, wd_ref, gate_ref, up_ref, f_ref):
        j = pl.program_id(0)

        def tile(i, carry):
            rows = _rows(i)
            at = a_ref[rows, :]
            gate = _dot(at, wg_ref[...])
            up = _dot(at, wu_ref[...])
            gate_ref[rows, :] = gate
            up_ref[rows, :] = up
            act = (gate * jax.nn.sigmoid(gate) * up).astype(BF16)
            contrib = _dot(act, wd_ref[...])

            @pl.when(j == 0)
            def _():
                f_ref[rows, :] = contrib

            @pl.when(j != 0)
            def _():
                f_ref[rows, :] += contrib

            return carry

        lax.fori_loop(0, N_ROW_TILES, tile, 0)

    shard_cols = pl.BlockSpec((None, D_MODEL, FF_SHARD), lambda j: (j, 0, 0))
    shard_rows = pl.BlockSpec((None, FF_SHARD, D_MODEL), lambda j: (j, 0, 0))
    hid = pl.BlockSpec((None, SEQ, FF_SHARD), lambda j: (j, 0, 0))
    full = pl.BlockSpec((SEQ, D_MODEL), lambda j: (0, 0))
    return pl.pallas_call(
        body, grid=(N_DEV,), in_specs=[full, shard_cols, shard_cols, shard_rows], out_specs=[hid, hid, full],
        out_shape=[SDS((N_DEV, SEQ, FF_SHARD), F32), SDS((N_DEV, SEQ, FF_SHARD), F32), SDS((SEQ, D_MODEL), F32)],
        name=name, compiler_params=_params(("arbitrary",)))(a, wg, wu, wd)


def _ffn_bwd(df, a, gate, up, wg, wu, wd, name):
    def body(df_ref, a_ref, gate_ref, up_ref, wg_ref, wu_ref, wd_ref, da_ref, dwg_ref, dwu_ref, dwd_ref,
             acc_g, acc_u, acc_d):
        j = pl.program_id(0)

        def tile(i, carry):
            rows = _rows(i)
            dft = df_ref[rows, :]
            at = a_ref[rows, :]
            gate = gate_ref[rows, :]
            up = up_ref[rows, :]
            dact = _dot_nt(dft, wd_ref[...])
            sig = jax.nn.sigmoid(gate)
            silu = gate * sig
            dgate = (dact * up * (sig * (1.0 + gate * (1.0 - sig)))).astype(BF16)
            dup = (dact * silu).astype(BF16)
            act = (silu * up).astype(BF16)
            dwd = _dot_tn(act, dft)
            dwg = _dot_tn(at, dgate)
            dwu = _dot_tn(at, dup)
            dat = _dot_nt(dgate, wg_ref[...]) + _dot_nt(dup, wu_ref[...])

            @pl.when(i == 0)
            def _():
                acc_d[...] = dwd
                acc_g[...] = dwg
                acc_u[...] = dwu

            @pl.when(i != 0)
            def _():
                acc_d[...] += dwd
                acc_g[...] += dwg
                acc_u[...] += dwu

            @pl.when(j == 0)
            def _():
                da_ref[rows, :] = dat

            @pl.when(j != 0)
            def _():
                da_ref[rows, :] += dat

            return carry

        lax.fori_loop(0, N_ROW_TILES, tile, 0)
        dwg_ref[...] = acc_g[...].astype(BF16)
        dwu_ref[...] = acc_u[...].astype(BF16)
        dwd_ref[...] = acc_d[...].astype(BF16)

    shard_cols = pl.BlockSpec((None, D_MODEL, FF_SHARD), lambda j: (j, 0, 0))
    shard_rows = pl.BlockSpec((None, FF_SHARD, D_MODEL), lambda j: (j, 0, 0))
    hid = pl.BlockSpec((None, SEQ, FF_SHARD), lambda j: (j, 0, 0))
    full = pl.BlockSpec((SEQ, D_MODEL), lambda j: (0, 0))
    return pl.pallas_call(
        body, grid=(N_DEV,), in_specs=[full, full, hid, hid, shard_cols, shard_cols, shard_rows],
        out_specs=[full, shard_cols, shard_cols, shard_rows],
        out_shape=[SDS((SEQ, D_MODEL), F32), SDS((N_DEV, D_MODEL, FF_SHARD), BF16),
                   SDS((N_DEV, D_MODEL, FF_SHARD), BF16), SDS((N_DEV, FF_SHARD, D_MODEL), BF16)],
        scratch_shapes=[pltpu.VMEM((D_MODEL, FF_SHARD), F32), pltpu.VMEM((D_MODEL, FF_SHARD), F32),
                        pltpu.VMEM((FF_SHARD, D_MODEL), F32)],
        name=name, compiler_params=_params(("arbitrary",)))(df, a, gate, up, wg, wu, wd)


def _proj_fwd(a, w):
    def body(a_ref, w_ref, o_ref):
        def tile(i, carry):
            rows = _rows(i)
            o_ref[rows, :] = _dot(a_ref[rows, :], w_ref[...])
            return carry

        lax.fori_loop(0, N_ROW_TILES, tile, 0)

    return pl.pallas_call(
        body, grid=(N_DEV,),
        in_specs=[pl.BlockSpec((SEQ, D_MODEL), lambda j: (0, 0)), pl.BlockSpec((None, D_MODEL, IN_SHARD), lambda j: (j, 0, 0))],
        out_specs=pl.BlockSpec((None, SEQ, IN_SHARD), lambda j: (j, 0, 0)),
        out_shape=SDS((N_DEV, SEQ, IN_SHARD), F32), name="proj_fwd",
        compiler_params=_params(("parallel",)))(a, w)


def _proj_bwd(dproj, a, w):
    def body(dp_ref, a_ref, w_ref, da_ref, dw_ref, acc):
        j = pl.program_id(0)

        def tile(i, carry):
            rows = _rows(i)
            dpt = dp_ref[rows, :]
            dw = _dot_tn(a_ref[rows, :], dpt)
            dat = _dot_nt(dpt, w_ref[...])

            @pl.when(i == 0)
            def _():
                acc[...] = dw

            @pl.when(i != 0)
            def _():
                acc[...] += dw

            @pl.when(j == 0)
            def _():
                da_ref[rows, :] = dat

            @pl.when(j != 0)
            def _():
                da_ref[rows, :] += dat

            return carry

        lax.fori_loop(0, N_ROW_TILES, tile, 0)
        dw_ref[...] = acc[...].astype(BF16)

    full = pl.BlockSpec((SEQ, D_MODEL), lambda j: (0, 0))
    wspec = pl.BlockSpec((None, D_MODEL, IN_SHARD), lambda j: (j, 0, 0))
    return pl.pallas_call(
        body, grid=(N_DEV,),
        in_specs=[pl.BlockSpec((None, SEQ, IN_SHARD), lambda j: (j, 0, 0)), full, wspec],
        out_specs=[full, wspec],
        out_shape=[SDS((SEQ, D_MODEL), F32), SDS((N_DEV, D_MODEL, IN_SHARD), BF16)],
        scratch_shapes=[pltpu.VMEM((D_MODEL, IN_SHARD), F32)],
        name="proj_bwd", compiler_params=_params(("arbitrary",)))(dproj, a, w)


def _na_consts():
    c = np.arange(GRID_W)
    col_start = np.clip(c - KW // 2, 0, GRID_W - KW)
    col_in = (c[None, :] >= col_start[:, None]) & (c[None, :] < col_start[:, None] + KW)
    dc = np.clip(c[None, :] - c[:, None] + KW - 1, 0, 2 * KW - 2)
    onehot = np.zeros((128, GRID_W * GRID_W), np.float32)
    qq, kk = np.meshgrid(c, c, indexing="ij")
    onehot[dc[col_in], (qq * GRID_W + kk)[col_in]] = 1.0
    negmask = np.where(col_in, 0.0, NEG_INF).astype(np.float32).reshape(1, -1)
    sel = np.zeros((16, 64), np.float32)
    for off in range(8):
        for kh in range(KH):
            sel[off + kh, off * 8 + kh] = 1.0
    return onehot, negmask, sel


def _rpb_expand(rpb):
    onehot, negmask, _ = _na_consts()
    rows = HEADS * (2 * KH - 1)
    rpb_pad = jnp.pad(rpb.reshape(rows, 2 * KW - 1), ((0, 128 - rows), (0, 128 - (2 * KW - 1))))

    def body(r_ref, oh_ref, m_ref, t_ref):
        hi, mid, lo = _split3(r_ref[...])
        oh = oh_ref[...]
        t_ref[...] = _dot(hi, oh) + _dot(mid, oh) + _dot(lo, oh) + m_ref[...]

    table = pl.pallas_call(body, out_shape=SDS((128, GRID_W * GRID_W), F32), name="rpb_expand",
                           compiler_params=_params())(rpb_pad, jnp.asarray(onehot, BF16), jnp.asarray(negmask))
    t4 = table[:rows].reshape(HEADS, 2 * KH - 1, GRID_W, GRID_W)
    per_off = jnp.stack([t4[:, o:o + KH] for o in range(8)], axis=1)
    return per_off.transpose(0, 1, 3, 2, 4).reshape(HEADS, 8, GRID_W, KEYS)


def _rpb_reduce(dbias):
    onehot, _, sel = _na_consts()
    x = dbias.reshape(HEADS, 8, GRID_W, KH, GRID_W).transpose(0, 1, 3, 2, 4).reshape(HEADS, 64, GRID_W * GRID_W)

    def body(x_ref, oht_ref, sel_ref, o_ref):
        hi, mid, lo = _split3(x_ref[...])
        oht = oht_ref[...]
        y = _dot(hi, oht) + _dot(mid, oht) + _dot(lo, oht)
        hi, mid, lo = _split3(y)
        s = sel_ref[...]
        o_ref[...] = _dot(s, hi) + _dot(s, mid) + _dot(s, lo)

    return pl.pallas_call(
        body, grid=(HEADS,),
        in_specs=[pl.BlockSpec((None, 64, GRID_W * GRID_W), lambda h: (h, 0, 0)),
                  pl.BlockSpec((GRID_W * GRID_W, 128), lambda h: (0, 0)), pl.BlockSpec((16, 64), lambda h: (0, 0))],
        out_specs=pl.BlockSpec((None, 16, 128), lambda h: (h, 0, 0)),
        out_shape=SDS((HEADS, 16, 128), F32), name="rpb_reduce",
        compiler_params=_params(("parallel",)))(x, jnp.asarray(onehot.T, BF16), jnp.asarray(sel, BF16))


def _block_geometry(r):
    row_start = jnp.clip(r - KH // 2, 0, GRID_ROWS - KH)
    off = row_start - r + (KH - 1)
    q0 = pl.multiple_of(N_META + r * GRID_W, 16)
    k0 = pl.multiple_of(N_META + row_start * GRID_W, 16)
    return off, q0, k0


def _na_probs(q, kk, km, bias):
    s = _dot_nt(q, kk) * ATT_SCALE + bias
    sm = _dot_nt(q, km) * ATT_SCALE
    m = jnp.maximum(jnp.max(s, axis=-1, keepdims=True), jnp.max(sm, axis=-1, keepdims=True))
    p = jnp.exp(s - m)
    pm = jnp.exp(sm - m)
    inv = 1.0 / (jnp.sum(p, axis=-1, keepdims=True) + jnp.sum(pm, axis=-1, keepdims=True))
    return p * inv, pm * inv


def _meta_probs(qm, km):
    s = _dot_nt(qm, km) * ATT_SCALE
    p = jnp.exp(s - jnp.max(s, axis=-1, keepdims=True))
    return p / jnp.sum(p, axis=-1, keepdims=True)


def _na_fwd(q, k, v, bias):
    def body(q_ref, k_ref, v_ref, b_ref, o_ref):
        km = k_ref[0:N_META, :].astype(BF16)
        vm = v_ref[0:N_META, :].astype(BF16)
        pmm = _meta_probs(q_ref[0:N_META, :].astype(BF16), km)
        o_ref[0:N_META, :] = _dot(pmm.astype(BF16), vm)

        def block(r, carry):
            off, q0, k0 = _block_geometry(r)
            qb = q_ref[pl.ds(q0, GRID_W), :].astype(BF16)
            kk = k_ref[pl.ds(k0, KEYS), :].astype(BF16)
            vv = v_ref[pl.ds(k0, KEYS), :].astype(BF16)
            p, pm = _na_probs(qb, kk, km, b_ref[off])
            o_ref[pl.ds(q0, GRID_W), :] = _dot(p.astype(BF16), vv) + _dot(pm.astype(BF16), vm)
            return carry

        lax.fori_loop(0, GRID_ROWS, block, 0, unroll=NA_UNROLL)

    head = pl.BlockSpec((None, SEQ, HEAD_DIM), lambda h: (h, 0, 0))
    return pl.pallas_call(
        body, grid=(HEADS,), in_specs=[head, head, head, pl.BlockSpec((None, 8, GRID_W, KEYS), lambda h: (h, 0, 0, 0))],
        out_specs=head, out_shape=SDS((HEADS, SEQ, HEAD_DIM), F32), name="na_fwd",
        compiler_params=_params(("parallel",)))(q, k, v, bias)


def _na_bwd(q, k, v, bias, do):
    def body(q_ref, k_ref, v_ref, b_ref, do_ref, dq_ref, dk_ref, dv_ref, db_ref):
        km = k_ref[0:N_META, :].astype(BF16)
        vm = v_ref[0:N_META, :].astype(BF16)
        dk_ref[...] = jnp.zeros_like(dk_ref)
        dv_ref[...] = jnp.zeros_like(dv_ref)
        db_ref[...] = jnp.zeros_like(db_ref)

        qm = q_ref[0:N_META, :].astype(BF16)
        dom = do_ref[0:N_META, :].astype(BF16)
        pmm = _meta_probs(qm, km)
        dpm = _dot_nt(dom, vm)
        dsm = (pmm * (dpm - jnp.sum(pmm * dpm, axis=-1, keepdims=True)) * ATT_SCALE).astype(BF16)
        dq_ref[0:N_META, :] = _dot(dsm, km)
        dkm0 = _dot_tn(dsm, qm)
        dvm0 = _dot_tn(pmm.astype(BF16), dom)

        def block(r, carry):
            dkm, dvm = carry
            off, q0, k0 = _block_geometry(r)
            qb = q_ref[pl.ds(q0, GRID_W), :].astype(BF16)
            kk = k_ref[pl.ds(k0, KEYS), :].astype(BF16)
            vv = v_ref[pl.ds(k0, KEYS), :].astype(BF16)
            dob = do_ref[pl.ds(q0, GRID_W), :].astype(BF16)
            p, pm = _na_probs(qb, kk, km, b_ref[off])
            dp = _dot_nt(dob, vv)
            dpm_ = _dot_nt(dob, vm)
            delta = jnp.sum(p * dp, axis=-1, keepdims=True) + jnp.sum(pm * dpm_, axis=-1, keepdims=True)
            ds = p * (dp - delta)
            dsm_ = pm * (dpm_ - delta)
            db_ref[off] += ds
            dsb = (ds * ATT_SCALE).astype(BF16)
            dsmb = (dsm_ * ATT_SCALE).astype(BF16)
            dq_ref[pl.ds(q0, GRID_W), :] = _dot(dsb, kk) + _dot(dsmb, km)
            dk_ref[pl.ds(k0, KEYS), :] += _dot_tn(dsb, qb)
            dv_ref[pl.ds(k0, KEYS), :] += _dot_tn(p.astype(BF16), dob)
            return dkm + _dot_tn(dsmb, qb), dvm + _dot_tn(pm.astype(BF16), dob)

        dkm, dvm = lax.fori_loop(0, GRID_ROWS, block, (dkm0, dvm0), unroll=NA_UNROLL)
        dk_ref[0:N_META, :] = dkm
        dv_ref[0:N_META, :] = dvm

    head = pl.BlockSpec((None, SEQ, HEAD_DIM), lambda h: (h, 0, 0))
    bspec = pl.BlockSpec((None, 8, GRID_W, KEYS), lambda h: (h, 0, 0, 0))
    return pl.pallas_call(
        body, grid=(HEADS,), in_specs=[head, head, head, bspec, head], out_specs=[head, head, head, bspec],
        out_shape=[SDS((HEADS, SEQ, HEAD_DIM), F32)] * 3 + [SDS((HEADS, 8, GRID_W, KEYS), F32)],
        name="na_bwd", compiler_params=_params(("parallel",)))(q, k, v, bias, do)


def _cmul(ar, ai, br, bi):
    return ar * br - ai * bi, ar * bi + ai * br


def _cpow(ar, ai, n):
    rr, ri = None, None
    br, bi = ar, ai
    while n:
        if n & 1:
            rr, ri = (br, bi) if rr is None else _cmul(rr, ri, br, bi)
        n >>= 1
        if n:
            br, bi = _cmul(br, bi, br, bi)
    return rr, ri


def _s5_prep(lr, li, logdt, bre, bim):
    def body(lr_ref, li_ref, dt_ref, br_ref, bi_ref, lbr_ref, lbi_ref, bbr_ref, bbi_ref):
        lr_, li_ = lr_ref[...], li_ref[...]
        dt = jnp.exp(dt_ref[...])
        mag = jnp.exp(lr_ * dt)
        lbr = mag * jnp.cos(li_ * dt)
        lbi = mag * jnp.sin(li_ * dt)
        lbr_ref[...] = lbr
        lbi_ref[...] = lbi
        den = lr_ * lr_ + li_ * li_
        xr = lbr - 1.0
        cr = (xr * lr_ + lbi * li_) / den
        ci = (lbi * lr_ - xr * li_) / den
        br, bi = br_ref[...], bi_ref[...]
        bbr_ref[...] = cr[:, None, :] * br - ci[:, None, :] * bi
        bbi_ref[...] = cr[:, None, :] * bi + ci[:, None, :] * br

    n = 2 * S5_GROUPS
    return pl.pallas_call(
        body, out_shape=[SDS((n, S5_STATE), F32)] * 2 + [SDS((n, S5_GROUP, S5_STATE), F32)] * 2,
        name="s5_prep", compiler_params=_params())(lr, li, logdt, bre, bim)


def _s5_prep_bwd(lr, li, logdt, bre, bim, dar, dai, dbbr, dbbi):
    def body(lr_ref, li_ref, dt_ref, br_ref, bi_ref, dar_ref, dai_ref, dbr_ref, dbi_ref,
             glr_ref, gli_ref, gdt_ref, gbr_ref, gbi_ref):
        lr_, li_ = lr_ref[...], li_ref[...]
        dt = jnp.exp(dt_ref[...])
        mag = jnp.exp(lr_ * dt)
        lbr = mag * jnp.cos(li_ * dt)
        lbi = mag * jnp.sin(li_ * dt)
        den = lr_ * lr_ + li_ * li_
        xr = lbr - 1.0
        cr = (xr * lr_ + lbi * li_) / den
        ci = (lbi * lr_ - xr * li_) / den
        br, bi = br_ref[...], bi_ref[...]
        dbr, dbi = dbr_ref[...], dbi_ref[...]
        gbr_ref[...] = cr[:, None, :] * dbr + ci[:, None, :] * dbi
        gbi_ref[...] = cr[:, None, :] * dbi - ci[:, None, :] * dbr
        gcr = jnp.sum(dbr * br + dbi * bi, axis=1)
        gci = jnp.sum(dbi * br - dbr * bi, axis=1)
        ilr, ili = lr_ / den, li_ / den
        tr, ti = _cmul(gcr, gci, ilr, ili)
        glbr = dar_ref[...] + tr
        glbi = dai_ref[...] + ti
        dr_, di_ = _cmul(tr, ti, cr, -ci)
        gwr, gwi = _cmul(glbr, glbi, lbr, -lbi)
        glr_ref[...] = gwr * dt - dr_
        gli_ref[...] = gwi * dt - di_
        gdt_ref[...] = jnp.sum(gwr * lr_ + gwi * li_, axis=-1, keepdims=True) * dt

    n = 2 * S5_GROUPS
    return pl.pallas_call(
        body, out_shape=[SDS((n, S5_STATE), F32)] * 2 + [SDS((n, 1), F32)] + [SDS((n, S5_GROUP, S5_STATE), F32)] * 2,
        name="s5_prep_bwd", compiler_params=_params())(lr, li, logdt, bre, bim, dar, dai, dbbr, dbbi)


def _scan_local(xr_ref, xi_ref, ar8, ai8, reverse):
    def step(i, carry):
        sr, si = carry
        idx = (SCAN_T - 1 - i) if reverse else i
        rows = pl.ds(pl.multiple_of(idx * SCAN_BLOCKS, SCAN_BLOCKS), SCAN_BLOCKS)
        nr = ar8 * sr - ai8 * si + xr_ref[rows, :]
        ni = ar8 * si + ai8 * sr + xi_ref[rows, :]
        xr_ref[rows, :] = nr
        xi_ref[rows, :] = ni
        return nr, ni

    z = jnp.zeros(ar8.shape, F32)
    return lax.fori_loop(0, SCAN_T, step, (z, z))


def _scan_carries(er, ei, atr, ati, reverse):
    row = lax.broadcasted_iota(jnp.int32, er.shape, 0)
    cr = jnp.zeros((1, er.shape[1]), F32)
    ci = cr
    outr = jnp.zeros(er.shape, F32)
    outi = outr
    order = range(SCAN_BLOCKS - 1, -1, -1) if reverse else range(SCAN_BLOCKS)
    for b in order:
        outr = jnp.where(row == b, cr, outr)
        outi = jnp.where(row == b, ci, outi)
        nr, ni = _cmul(atr, ati, cr, ci)
        cr, ci = nr + er[b:b + 1, :], ni + ei[b:b + 1, :]
    return outr, outi


def _scan_fixup(xr_ref, xi_ref, cr8, ci8, ar8, ai8, reverse):
    def step(i, carry):
        pr, pi = carry
        idx = (SCAN_T - 1 - i) if reverse else i
        rows = pl.ds(pl.multiple_of(idx * SCAN_BLOCKS, SCAN_BLOCKS), SCAN_BLOCKS)
        fr, fi = _cmul(pr, pi, cr8, ci8)
        xr_ref[rows, :] += fr
        xi_ref[rows, :] += fi
        return _cmul(pr, pi, ar8, ai8)

    lax.fori_loop(0, SCAN_T, step, (ar8, ai8))


def _scan(xr_ref, xi_ref, ar, ai, reverse):
    n = ar.shape[1]
    ar8 = jnp.broadcast_to(ar, (SCAN_BLOCKS, n))
    ai8 = jnp.broadcast_to(ai, (SCAN_BLOCKS, n))
    er, ei = _scan_local(xr_ref, xi_ref, ar8, ai8, reverse)
    atr, ati = _cpow(ar, ai, SCAN_T)
    cr8, ci8 = _scan_carries(er, ei, atr, ati, reverse)
    _scan_fixup(xr_ref, xi_ref, cr8, ci8, ar8, ai8, reverse)


def _s5_specs():
    chan = pl.BlockSpec((SEQ, CH_W), lambda c, d: (0, c))
    chan2 = pl.BlockSpec((None, SEQ, CH_W), lambda c, d: (d, 0, c))
    state = pl.BlockSpec((None, SEQ, ST_W), lambda c, d: (d, 0, c))
    bmat = pl.BlockSpec((None, None, CH_W, ST_W), lambda c, d: (d, c, 0, 0))
    cmat = pl.BlockSpec((None, None, ST_W, CH_W), lambda c, d: (d, c, 0, 0))
    avec = pl.BlockSpec((None, None, 1, ST_W), lambda c, d: (d, c, 0, 0))
    return chan, chan2, state, bmat, cmat, avec


def _scan_by_direction(xr_ref, xi_ref, ar, ai, d, adjoint):
    @pl.when(d == 0)
    def _():
        _scan(xr_ref, xi_ref, ar, ai, reverse=adjoint)

    @pl.when(d == 1)
    def _():
        _scan(xr_ref, xi_ref, ar, ai, reverse=not adjoint)


def _s5_scan_fwd(u, bre, bim, are, aim, cre, cim):
    def body(u_ref, bre_ref, bim_ref, are_ref, aim_ref, cre_ref, cim_ref, sr_ref, si_ref, y_ref):
        ub = u_ref[...].astype(BF16)
        sr_ref[...] = _dot(ub, bre_ref[...])
        si_ref[...] = _dot(ub, bim_ref[...])
        _scan_by_direction(sr_ref, si_ref, are_ref[...], aim_ref[...], pl.program_id(1), adjoint=False)
        y_ref[...] = _dot(sr_ref[...].astype(BF16), cre_ref[...]) - _dot(si_ref[...].astype(BF16), cim_ref[...])

    chan, chan2, state, bmat, cmat, avec = _s5_specs()
    return pl.pallas_call(
        body, grid=(S5_CHUNKS, 2), in_specs=[chan, bmat, bmat, avec, avec, cmat, cmat], out_specs=[state, state, chan2],
        out_shape=[SDS((2, SEQ, S5_GROUPS * S5_STATE), F32)] * 2 + [SDS((2, SEQ, S5_WIDTH), F32)],
        name="s5_scan_fwd", compiler_params=_params(("parallel", "parallel")))(u, bre, bim, are, aim, cre, cim)


def _dlam(gr_ref, gi_ref, sr_ref, si_ref, reverse):
    tile = lambda i: pl.ds(pl.multiple_of(i * SCAN_BLOCKS, SCAN_BLOCKS), SCAN_BLOCKS)
    row = lax.broadcasted_iota(jnp.int32, (SCAN_BLOCKS, ST_W), 0)
    if reverse:
        edge, src, shift, empty, lo, hi, dprev = SCAN_T - 1, 0, SCAN_BLOCKS - 1, SCAN_BLOCKS - 1, 0, SCAN_T - 1, 1
    else:
        edge, src, shift, empty, lo, hi, dprev = 0, SCAN_T - 1, 1, 0, 1, SCAN_T, -1
    spr = jnp.where(row == empty, 0.0, pltpu.roll(sr_ref[tile(src), :], shift, 0))
    spi = jnp.where(row == empty, 0.0, pltpu.roll(si_ref[tile(src), :], shift, 0))
    acc0 = _cmul(gr_ref[tile(edge), :], gi_ref[tile(edge), :], spr, -spi)

    def step(i, carry):
        accr, acci = carry
        pr, pi = _cmul(gr_ref[tile(i), :], gi_ref[tile(i), :], sr_ref[tile(i + dprev), :], -si_ref[tile(i + dprev), :])
        return accr + pr, acci + pi

    accr, acci = lax.fori_loop(lo, hi, step, acc0)
    return jnp.sum(accr, axis=0, keepdims=True), jnp.sum(acci, axis=0, keepdims=True)


def _s5_scan_bwd(dy, du_skip, u, sr, si, bre, bim, are, aim, cre, cim):
    def body(dy_ref, dus_ref, u_ref, sr_ref, si_ref, bre_ref, bim_ref, are_ref, aim_ref, cre_ref, cim_ref,
             du_ref, dbr_ref, dbi_ref, dcr_ref, dci_ref, dar_ref, dai_ref, gr_ref, gi_ref):
        d = pl.program_id(1)
        dyb = dy_ref[...].astype(BF16)
        gr_ref[...] = _dot_nt(dyb, cre_ref[...])
        gi_ref[...] = -_dot_nt(dyb, cim_ref[...])
        dcr_ref[...] = _dot_tn(sr_ref[...].astype(BF16), dyb)
        dci_ref[...] = -_dot_tn(si_ref[...].astype(BF16), dyb)
        _scan_by_direction(gr_ref, gi_ref, are_ref[...], -aim_ref[...], d, adjoint=True)

        @pl.when(d == 0)
        def _():
            dar_ref[...], dai_ref[...] = _dlam(gr_ref, gi_ref, sr_ref, si_ref, reverse=False)
            du_ref[...] = dus_ref[...]

        @pl.when(d == 1)
        def _():
            dar_ref[...], dai_ref[...] = _dlam(gr_ref, gi_ref, sr_ref, si_ref, reverse=True)

        grb = gr_ref[...].astype(BF16)
        gib = gi_ref[...].astype(BF16)
        du_ref[...] += _dot_nt(grb, bre_ref[...]) + _dot_nt(gib, bim_ref[...])
        ub = u_ref[...].astype(BF16)
        dbr_ref[...] = _dot_tn(ub, grb)
        dbi_ref[...] = _dot_tn(ub, gib)

    chan, _, state, bmat, cmat, avec = _s5_specs()
    return pl.pallas_call(
        body, grid=(S5_CHUNKS, 2), in_specs=[chan, chan, chan, state, state, bmat, bmat, avec, avec, cmat, cmat],
        out_specs=[chan, bmat, bmat, cmat, cmat, avec, avec],
        out_shape=[SDS((SEQ, S5_WIDTH), F32)] + [SDS((2, S5_CHUNKS, CH_W, ST_W), F32)] * 2
                  + [SDS((2, S5_CHUNKS, ST_W, CH_W), F32)] * 2 + [SDS((2, S5_CHUNKS, 1, ST_W), F32)] * 2,
        scratch_shapes=[pltpu.VMEM((SEQ, ST_W), F32), pltpu.VMEM((SEQ, ST_W), F32)],
        name="s5_scan_bwd", compiler_params=_params(("parallel", "arbitrary")))(dy, du_skip, u, sr, si, bre, bim, are, aim, cre, cim)


_GELU_K = math.sqrt(2.0 / math.pi)
_GELU_C = 0.044715


def _gelu(x):
    t = jnp.tanh(_GELU_K * (x + _GELU_C * x * x * x))
    return 0.5 * x * (1.0 + t), t


def _s5_glu_fwd(u, y2, dskip, wglu, bglu):
    def body(u_ref, y0_ref, y1_ref, d_ref, w_ref, b_ref, o_ref, yp_ref):
        ypre = u_ref[...] * d_ref[...] + y0_ref[...] + y1_ref[...]
        yp_ref[...] = ypre
        y, _ = _gelu(ypre)
        z = _dot(y.astype(BF16), w_ref[...]) + b_ref[...]
        o_ref[...] = y * jax.nn.sigmoid(z)

    row = _row_spec(S5_WIDTH)
    vec = _fix_spec((1, S5_WIDTH))
    dir0 = pl.BlockSpec((None, ROW_TILE, S5_WIDTH), lambda i: (0, i, 0))
    dir1 = pl.BlockSpec((None, ROW_TILE, S5_WIDTH), lambda i: (1, i, 0))
    return pl.pallas_call(
        body, grid=(N_ROW_TILES,), in_specs=[row, dir0, dir1, vec, _fix_spec((S5_WIDTH, S5_WIDTH)), vec],
        out_specs=[row, row], out_shape=[SDS((SEQ, S5_WIDTH), F32)] * 2, name="s5_glu_fwd",
        compiler_params=_params(("parallel",)))(u, y2, y2, dskip, wglu, bglu)


def _s5_glu_bwd(do, ypre, u, dskip, wglu, bglu):
    def body(do_ref, yp_ref, u_ref, d_ref, w_ref, b_ref, dyp_ref, du_ref, dw_ref, db_ref, dd_ref):
        i = pl.program_id(0)
        ypre = yp_ref[...]
        y, t = _gelu(ypre)
        yb = y.astype(BF16)
        sg = jax.nn.sigmoid(_dot(yb, w_ref[...]) + b_ref[...])
        dov = do_ref[...]
        dz = dov * y * sg * (1.0 - sg)
        dzb = dz.astype(BF16)
        dy = dov * sg + _dot_nt(dzb, w_ref[...])
        dgelu = 0.5 * (1.0 + t) + 0.5 * ypre * (1.0 - t * t) * _GELU_K * (1.0 + 3.0 * _GELU_C * ypre * ypre)
        dyp = dy * dgelu
        dyp_ref[...] = dyp
        uv = u_ref[...]
        du_ref[...] = dyp * d_ref[...]

        @pl.when(i == 0)
        def _():
            dw_ref[...] = jnp.zeros_like(dw_ref)
            db_ref[...] = jnp.zeros_like(db_ref)
            dd_ref[...] = jnp.zeros_like(dd_ref)

        dw_ref[...] += _dot_tn(yb, dzb)
        db_ref[...] += jnp.sum(dz, axis=0, keepdims=True)
        dd_ref[...] += jnp.sum(dyp * uv, axis=0, keepdims=True)

    row = _row_spec(S5_WIDTH)
    vec = _fix_spec((1, S5_WIDTH))
    mat = _fix_spec((S5_WIDTH, S5_WIDTH))
    return pl.pallas_call(
        body, grid=(N_ROW_TILES,), in_specs=[row, row, row, vec, mat, vec], out_specs=[row, row, mat, vec, vec],
        out_shape=[SDS((SEQ, S5_WIDTH), F32)] * 2 + [SDS((S5_WIDTH, S5_WIDTH), F32), SDS((1, S5_WIDTH), F32), SDS((1, S5_WIDTH), F32)],
        name="s5_glu_bwd", compiler_params=_params(("arbitrary",)))(do, ypre, u, dskip, wglu, bglu)


def _mix_out_fwd(ona, os5, g_na, g_s5, wout):
    def body(a_ref, s_ref, ga_ref, gs_ref, w_ref, o_ref):
        av, sv = a_ref[...], s_ref[...]
        ca = (av * _rstd(av) * ga_ref[...]).astype(BF16)
        cs = (sv * _rstd(sv) * gs_ref[...]).astype(BF16)
        o_ref[...] = _dot(ca, w_ref[0:NA_WIDTH, :]) + _dot(cs, w_ref[NA_WIDTH:, :])

    row = _row_spec(NA_WIDTH)
    vec = _fix_spec((1, NA_WIDTH))
    return pl.pallas_call(
        body, grid=(N_ROW_TILES,), in_specs=[row, row, vec, vec, _fix_spec((D_MODEL, D_MODEL))],
        out_specs=_row_spec(D_MODEL), out_shape=SDS((SEQ, D_MODEL), F32), name="mix_out_fwd",
        compiler_params=_params(("parallel",)))(ona, os5, g_na, g_s5, wout)


def _mix_out_bwd(dmix, ona, os5, g_na, g_s5, wout):
    def body(dm_ref, a_ref, s_ref, ga_ref, gs_ref, w_ref, da_ref, ds_ref, dw_ref, dga_ref, dgs_ref):
        i = pl.program_id(0)
        dm = dm_ref[...]
        av, sv = a_ref[...], s_ref[...]
        ra, rs = _rstd(av), _rstd(sv)
        ga, gs = ga_ref[...], gs_ref[...]
        ca = (av * ra * ga).astype(BF16)
        cs = (sv * rs * gs).astype(BF16)
        dca = _dot_nt(dm, w_ref[0:NA_WIDTH, :])
        dcs = _dot_nt(dm, w_ref[NA_WIDTH:, :])
        da, dga = _rms_bwd(av, ra, ga, dca)
        ds, dgs = _rms_bwd(sv, rs, gs, dcs)
        da_ref[...] = da
        ds_ref[...] = ds

        @pl.when(i == 0)
        def _():
            dw_ref[...] = jnp.zeros_like(dw_ref)
            dga_ref[...] = jnp.zeros_like(dga_ref)
            dgs_ref[...] = jnp.zeros_like(dgs_ref)

        dw_ref[0:NA_WIDTH, :] += _dot_tn(ca, dm)
        dw_ref[NA_WIDTH:, :] += _dot_tn(cs, dm)
        dga_ref[...] += jnp.sum(dga, axis=0, keepdims=True)
        dgs_ref[...] += jnp.sum(dgs, axis=0, keepdims=True)

    row = _row_spec(NA_WIDTH)
    vec = _fix_spec((1, NA_WIDTH))
    mat = _fix_spec((D_MODEL, D_MODEL))
    return pl.pallas_call(
        body, grid=(N_ROW_TILES,), in_specs=[_row_spec(D_MODEL), row, row, vec, vec, mat],
        out_specs=[row, row, mat, vec, vec],
        out_shape=[SDS((SEQ, NA_WIDTH), F32)] * 2 + [SDS((D_MODEL, D_MODEL), F32), SDS((1, NA_WIDTH), F32), SDS((1, NA_WIDTH), F32)],
        name="mix_out_bwd", compiler_params=_params(("arbitrary",)))(dmix, ona, os5, g_na, g_s5, wout)


def _me():
    x, y, c = lax.axis_index("x"), lax.axis_index("y"), lax.axis_index("c")
    return x, y, c, 4 * x + 2 * y + c


def _peer(k):
    x, y, c, _ = _me()
    px = 1 - x if (k >> 2) & 1 else x
    py = 1 - y if (k >> 1) & 1 else y
    pc = 1 - c if k & 1 else c
    return (px, py, pc), 4 * px + 2 * py + pc


def _exchange(arrays, gather, name, after=()):
    n, n_after = len(arrays), len(after)

    def body(*refs):
        ins, outs = refs[:n], refs[n + n_after:2 * n + n_after]
        send_sems, recv_sems, local_sems = refs[2 * n + n_after:]
        _, _, _, me = _me()
        started = []
        for a in range(n):
            src_mine = ins[a] if gather else ins[a].at[me]
            local = pltpu.make_async_copy(src_mine, outs[a].at[me], local_sems.at[a])
            local.start()
            started.append(local)
        sends = []
        for k in range(1, N_DEV):
            peer, peer_idx = _peer(k)
            for a in range(n):
                src = ins[a] if gather else ins[a].at[peer_idx]
                cp = pltpu.make_async_remote_copy(src_ref=src, dst_ref=outs[a].at[me], send_sem=send_sems.at[a, k - 1],
                                                  recv_sem=recv_sems.at[a, k - 1], device_id=peer, device_id_type=MESH)
                cp.start()
                sends.append(cp)
        for k in range(1, N_DEV):
            peer, peer_idx = _peer(k)
            for a in range(n):
                src = ins[a] if gather else ins[a].at[peer_idx]
                pltpu.make_async_remote_copy(src_ref=src, dst_ref=outs[a].at[peer_idx], send_sem=send_sems.at[a, k - 1],
                                             recv_sem=recv_sems.at[a, k - 1], device_id=peer, device_id_type=MESH).wait_recv()
        for cp in sends:
            cp.wait_send()
        for local in started:
            local.wait()

    hbm = pl.BlockSpec(memory_space=pltpu.HBM)
    out_shape = [SDS((N_DEV,) + tuple(a.shape), a.dtype) if gather else SDS(a.shape, a.dtype) for a in arrays]
    return pl.pallas_call(
        body, in_specs=[hbm] * n + [pl.BlockSpec(memory_space=pl.ANY)] * n_after, out_specs=[hbm] * n, out_shape=out_shape,
        scratch_shapes=[pltpu.SemaphoreType.DMA((n, N_DEV - 1)), pltpu.SemaphoreType.DMA((n, N_DEV - 1)),
                        pltpu.SemaphoreType.DMA((n,))],
        name=name)(*arrays, *after)


_HBM = pl.BlockSpec(memory_space=pltpu.HBM)
_SEM = pl.BlockSpec(memory_space=pltpu.SEMAPHORE)
_EFFECT = pltpu.SideEffectType.DATAFLOW_SIDE_EFFECTING


def _land_shape(a, gather):
    return (N_DEV,) + tuple(a.shape) if gather else tuple(a.shape)


def _place_own(arrays, gather, name):
    n = len(arrays)
    _, _, _, me = _me()

    def body(me_ref, *refs):
        for a in range(n):
            refs[n + a][...] = refs[a][...]

    def own_slot(a):
        zeros = (0,) * (a.ndim - (0 if gather else 1))
        return lambda i, me_ref: (me_ref[0],) + zeros

    def whole(a):
        return lambda i, me_ref: (0,) * a.ndim

    in_specs = [pl.BlockSpec(a.shape, whole(a)) if gather else pl.BlockSpec((None,) + a.shape[1:], own_slot(a)) for a in arrays]
    out_specs = [pl.BlockSpec((None,) + (a.shape if gather else a.shape[1:]), own_slot(a)) for a in arrays]
    return pl.pallas_call(
        body, grid_spec=pltpu.PrefetchScalarGridSpec(num_scalar_prefetch=1, grid=(1,), in_specs=in_specs, out_specs=out_specs),
        out_shape=[SDS(_land_shape(a, gather), a.dtype) for a in arrays], name=name,
        compiler_params=_params(("arbitrary",)))(me.reshape(1).astype(jnp.int32), *arrays)


def _exchange_start(arrays, lands, gather, name):
    n = len(arrays)

    def body(*refs):
        ins, lnd = refs[:n], refs[n:2 * n]
        send_sems, recv_sems = refs[2 * n], refs[2 * n + 1]
        token = refs[-1]
        _, _, _, me = _me()
        for k in range(1, N_DEV):
            peer, peer_idx = _peer(k)
            for a in range(n):
                src = ins[a] if gather else ins[a].at[peer_idx]
                s = a * (N_DEV - 1) + k - 1
                pltpu.make_async_remote_copy(src_ref=src, dst_ref=lnd[a].at[me], send_sem=send_sems.at[s],
                                             recv_sem=recv_sems.at[s], device_id=peer, device_id_type=MESH).start()
        token[...] = jnp.zeros_like(token)

    sems = pltpu.SemaphoreType.DMA((n * (N_DEV - 1),))
    out = pl.pallas_call(
        body, name=name, in_specs=[_HBM] * (2 * n),
        out_shape=(sems, sems) + tuple(pltpu.HBM(a.shape, a.dtype) for a in list(arrays) + list(lands)) + (SDS((8, 128), F32),),
        out_specs=(_SEM, _SEM) + (_HBM,) * (2 * n) + (pl.BlockSpec(memory_space=pltpu.VMEM),),
        input_output_aliases={i: 2 + i for i in range(2 * n)},
        compiler_params=pltpu.CompilerParams(has_side_effects=_EFFECT),
    )(*[pltpu.with_memory_space_constraint(a, pltpu.HBM) for a in list(arrays) + list(lands)])
    return out[0], out[1], list(out[2:2 + n]), list(out[2 + n:2 + 2 * n]), out[-1]


def _exchange_wait(send_sems, recv_sems, arrays, lands, after, gather, name):
    n = len(arrays)

    def body(*refs):
        ins, lnd = refs[:n], refs[n:2 * n]
        send_sems, recv_sems = refs[2 * n], refs[2 * n + 1]
        for k in range(1, N_DEV):
            peer, peer_idx = _peer(k)
            for a in range(n):
                src = ins[a] if gather else ins[a].at[peer_idx]
                s = a * (N_DEV - 1) + k - 1
                cp = pltpu.make_async_remote_copy(src_ref=src, dst_ref=lnd[a].at[peer_idx], send_sem=send_sems.at[s],
                                                  recv_sem=recv_sems.at[s], device_id=peer, device_id_type=MESH)
                cp.wait_send()
                cp.wait_recv()

    after = list(after) if isinstance(after, (list, tuple)) else [after]
    out = pl.pallas_call(
        body, name=name, in_specs=[_HBM] * (2 * n) + [_SEM, _SEM] + [pl.BlockSpec(memory_space=pl.ANY)] * len(after),
        out_shape=tuple(pltpu.HBM(a.shape, a.dtype) for a in list(arrays) + list(lands)),
        out_specs=(_HBM,) * (2 * n), input_output_aliases={i: i for i in range(2 * n)},
        compiler_params=pltpu.CompilerParams(has_side_effects=_EFFECT),
    )(*arrays, *lands, send_sems, recv_sems, *after)
    return list(out[n:])


def _adamw_math(w, g, m, v):
    m = ADAM_B1 * m + (1.0 - ADAM_B1) * g
    v = ADAM_B2 * v + (1.0 - ADAM_B2) * (g * g)
    m_hat = m / (1.0 - ADAM_B1 ** ADAM_STEP)
    v_hat = v / (1.0 - ADAM_B2 ** ADAM_STEP)
    delta = -ADAM_LR * (m_hat / (jnp.sqrt(v_hat) + ADAM_EPS) + ADAM_WD * w)
    return delta, m, v


def _adamw(w, m, v, pieces, name):
    rows, cols = w.shape
    tile = rows
    for cand in (256, 176, 128, 64, 16):
        if rows > cand and rows % cand == 0:
            tile = cand
            break

    def body(w_ref, m_ref, v_ref, p_ref, g_ref, d_ref, mo_ref, vo_ref):
        g = p_ref[0].astype(F32)
        for p in range(1, N_DEV):
            g = g + p_ref[p].astype(F32)
        g_ref[...] = g
        d_ref[...], mo_ref[...], vo_ref[...] = _adamw_math(w_ref[...], g, m_ref[...], v_ref[...])

    blk = pl.BlockSpec((tile, cols), lambda i: (i, 0))
    return pl.pallas_call(
        body, grid=(rows // tile,), in_specs=[blk, blk, blk, pl.BlockSpec((N_DEV, tile, cols), lambda i: (0, i, 0))],
        out_specs=[blk] * 4, out_shape=[SDS((rows, cols), F32)] * 4, name=name,
        compiler_params=_params(("parallel",)))(w, m, v, pieces)


def _sum_pieces(p_ref):
    g = p_ref[0].astype(F32)
    for p in range(1, N_DEV):
        g = g + p_ref[p].astype(F32)
    return g


def _adamw_lead(w, m, v, g, tile, name):
    lead, b, c = w.shape

    def body(w_ref, m_ref, v_ref, g_ref, d_ref, mo_ref, vo_ref):
        d_ref[...], mo_ref[...], vo_ref[...] = _adamw_math(w_ref[...], g_ref[...], m_ref[...], v_ref[...])

    blk = pl.BlockSpec((tile, b, c), lambda i: (i, 0, 0))
    return pl.pallas_call(
        body, grid=(lead // tile,), in_specs=[blk] * 4, out_specs=[blk] * 3, out_shape=[SDS(w.shape, F32)] * 3, name=name,
        compiler_params=_params(("parallel",)))(w, m, v, g)


VEC_ROWS = ['ffn1_pre_g', 'ffn1_post_g', 'mix_pre_g', 'mix_post_g', 'ffn2_pre_g', 'ffn2_post_g', 'final_g',
            ('na_out_g', 's5_out_g'), ('s5_d', 's5_b_glu')]
VEC_NAMES = [n for row in VEC_ROWS for n in ((row,) if isinstance(row, str) else row)]
VEC_PACK_ROWS = 16


def _pack_vectors(grads):
    def body(*refs):
        o_ref = refs[-1]
        o_ref[...] = jnp.zeros_like(o_ref)
        k = 0
        for i, row in enumerate(VEC_ROWS):
            if isinstance(row, str):
                o_ref[i:i + 1, :] = refs[k][...]
                k += 1
            else:
                o_ref[i:i + 1, 0:NA_WIDTH] = refs[k][...]
                o_ref[i:i + 1, NA_WIDTH:] = refs[k + 1][...]
                k += 2

    return pl.pallas_call(body, out_shape=SDS((VEC_PACK_ROWS, D_MODEL), F32), name="pack_vectors",
                          compiler_params=_params())(*[grads[n] for n in VEC_NAMES])


def _sum8(pieces, name):
    def body(p_ref, o_ref):
        o_ref[...] = _sum_pieces(p_ref)

    return pl.pallas_call(body, out_shape=SDS(pieces.shape[1:], F32), name=name, compiler_params=_params())(pieces)


def _adamw_small(packed8, vec_wmv, others):
    n_vec, n_oth = len(VEC_NAMES), len(others)

    def body(*refs):
        p_ref = refs[0]
        ins = refs[1:1 + 3 * n_vec + 4 * n_oth]
        outs = refs[1 + 3 * n_vec + 4 * n_oth:]
        gsum = _sum_pieces(p_ref)
        k = 0
        for i, row in enumerate(VEC_ROWS):
            parts = [(row, gsum[i:i + 1, :])] if isinstance(row, str) else \
                [(row[0], gsum[i:i + 1, 0:NA_WIDTH]), (row[1], gsum[i:i + 1, NA_WIDTH:])]
            for _, g in parts:
                w_ref, m_ref, v_ref = ins[3 * k:3 * k + 3]
                outs[4 * k][...] = g
                outs[4 * k + 1][...], outs[4 * k + 2][...], outs[4 * k + 3][...] = _adamw_math(w_ref[...], g, m_ref[...], v_ref[...])
                k += 1
        for j in range(n_oth):
            w_ref, m_ref, v_ref, g_ref = ins[3 * n_vec + 4 * j:3 * n_vec + 4 * j + 4]
            g = _sum_pieces(g_ref)
            g = g[tuple(slice(0, s) for s in w_ref.shape)]
            o = outs[4 * (n_vec + j):4 * (n_vec + j) + 4]
            o[0][...] = g
            o[1][...], o[2][...], o[3][...] = _adamw_math(w_ref[...], g, m_ref[...], v_ref[...])

    args, out_shape = [packed8], []
    for w, m, v in vec_wmv:
        args += [w, m, v]
        out_shape += [SDS(w.shape, F32)] * 4
    for w, m, v, g in others:
        args += [w, m, v, g]
        out_shape += [SDS(w.shape, F32)] * 4
    return pl.pallas_call(body, out_shape=out_shape, name="adamw_small", compiler_params=_params())(*args)


def _perm_rows(x):
    return x.reshape(SCAN_BLOCKS, SCAN_T, x.shape[-1]).transpose(1, 0, 2).reshape(SEQ, x.shape[-1])


def _unperm_rows(x):
    return x.reshape(SCAN_T, SCAN_BLOCKS, x.shape[-1]).transpose(1, 0, 2).reshape(SEQ, x.shape[-1])


def _block_diag(x):
    eye = np.eye(8, dtype=bool)[None, None, :, None, :, None]
    full = jnp.where(eye, x[:, :, :, :, None, :], 0.0)
    return full.reshape(2, S5_CHUNKS, 8 * x.shape[3], 8 * x.shape[4])


def _diag_blocks(x, r, c):
    x6 = x.reshape(2, S5_CHUNKS, 8, r, 8, c)
    return jnp.stack([x6[:, :, g, :, g, :] for g in range(8)], axis=2)


def _dep(x, token):
    return x if token is None else x + token


def _local_step(x, target, get_w, small, emit):
    bias = _rpb_expand(small["na_rpb"][0])
    lr = small["s5_lam_re"].reshape(64, S5_STATE)
    li = small["s5_lam_im"].reshape(64, S5_STATE)
    logdt = small["s5_log_dt"].reshape(64, 1)
    b_t = [small[n].reshape(64, S5_STATE, S5_GROUP).transpose(0, 2, 1) for n in ("s5_b_re", "s5_b_im")]
    lbr, lbi, bbr, bbi = _s5_prep(lr, li, logdt, b_t[0], b_t[1])
    are = lbr.reshape(2, S5_CHUNKS, 1, ST_W)
    aim = lbi.reshape(2, S5_CHUNKS, 1, ST_W)
    bre = _block_diag(bbr.reshape(2, S5_CHUNKS, 8, S5_GROUP, S5_STATE)).astype(BF16)
    bim = _block_diag(bbi.reshape(2, S5_CHUNKS, 8, S5_GROUP, S5_STATE)).astype(BF16)
    c_t = [small[n].reshape(2, S5_CHUNKS, 8, S5_GROUP, S5_STATE).transpose(0, 1, 2, 4, 3) for n in ("s5_c_re", "s5_c_im")]
    cre = _block_diag(c_t[0]).astype(BF16)
    cim = _block_diag(c_t[1]).astype(BF16)
    tgt = jnp.concatenate([jnp.zeros((N_META, D_MODEL), F32), target], axis=0)

    wts = dict(get_w("ffn1", [bias, are, aim, bre, bim, cre, cim, tgt]))
    h0 = jnp.concatenate([wts["meta_tokens"], x], axis=0)
    a1 = _prenorm(h0, small["ffn1_pre_g"])
    gate1, up1, f1 = _ffn_fwd(a1, wts["ffn1_w_gate"], wts["ffn1_w_up"], wts["ffn1_w_down"], "ffn1_fwd")
    h1, a2 = _post_pre(f1, h0, small["ffn1_post_g"], small["mix_pre_g"], 0.5, "post_pre1")
    wts.update(get_w("w_in", a2))
    proj = _proj_fwd(a2, wts["w_in"])
    qkv = proj[:6].reshape(3, 2, SEQ, 4, HEAD_DIM).transpose(0, 1, 3, 2, 4).reshape(3, HEADS, SEQ, HEAD_DIM)
    u = proj[6:].transpose(1, 0, 2).reshape(SEQ, S5_WIDTH)
    o3 = _na_fwd(qkv[0], qkv[1], qkv[2], bias)
    ona = o3.transpose(1, 0, 2).reshape(SEQ, NA_WIDTH)
    u_p = _perm_rows(u)
    sr, si, y2 = _s5_scan_fwd(u_p, bre, bim, are, aim, cre, cim)
    wts.update(get_w("mix", y2))
    os5_p, ypre_p = _s5_glu_fwd(u_p, y2, small["s5_d"], wts["s5_w_glu"], small["s5_b_glu"])
    os5 = _unperm_rows(os5_p)

    mix = _mix_out_fwd(ona, os5, small["na_out_g"], small["s5_out_g"], wts["w_out"])
    h2, a3 = _post_pre(mix, h1, small["mix_post_g"], small["ffn2_pre_g"], 1.0, "post_pre2")
    wts.update(get_w("ffn2", a3))
    gate2, up2, f2 = _ffn_fwd(a3, wts["ffn2_w_gate"], wts["ffn2_w_up"], wts["ffn2_w_down"], "ffn2_fwd")
    loss8, dh3, df2, g_final, g_ffn2_post = _final_loss(f2, h2, small["ffn2_post_g"], small["final_g"], tgt)

    da3, dwg2, dwu2, dwd2 = _ffn_bwd(df2, a3, gate2, up2, wts["ffn2_w_gate"], wts["ffn2_w_up"], wts["ffn2_w_down"], "ffn2_bwd")
    tok = emit("ffn2", {"ffn2_w_gate": dwg2, "ffn2_w_up": dwu2, "ffn2_w_down": dwd2})
    dh2, dmix, g_ffn2_pre, g_mix_post = _bwd_pre_post(da3, h2, _dep(small["ffn2_pre_g"], tok), dh3, mix, small["mix_post_g"], 1.0,
                                                      "bwd_pre_post2")
    dona, dos5, dwout, g_na_out, g_s5_out = _mix_out_bwd(dmix, ona, os5, small["na_out_g"], small["s5_out_g"], wts["w_out"])

    dypre_p, du_skip_p, dwglu, g_b_glu, g_s5_d = _s5_glu_bwd(_perm_rows(dos5), ypre_p, u_p, small["s5_d"], wts["s5_w_glu"],
                                                             small["s5_b_glu"])
    tok = emit("mix", {"s5_w_glu": dwglu.reshape(N_DEV, S5_WIDTH // N_DEV, S5_WIDTH).astype(BF16),
                       "w_out": dwout.reshape(N_DEV, D_MODEL // N_DEV, D_MODEL).astype(BF16)})
    du_p, dbr, dbi, dcr, dci, dar, dai = _s5_scan_bwd(dypre_p, du_skip_p, u_p, sr, si, bre, bim, _dep(are, tok), aim, cre, cim)
    du = _unperm_rows(du_p)
    dbbr = _diag_blocks(dbr, S5_GROUP, S5_STATE).reshape(64, S5_GROUP, S5_STATE)
    dbbi = _diag_blocks(dbi, S5_GROUP, S5_STATE).reshape(64, S5_GROUP, S5_STATE)
    g_lr, g_li, g_dt, g_br, g_bi = _s5_prep_bwd(lr, li, logdt, b_t[0], b_t[1], dar.reshape(64, S5_STATE),
                                                dai.reshape(64, S5_STATE), dbbr, dbbi)
    g_c = [_diag_blocks(d, S5_STATE, S5_GROUP).transpose(0, 1, 2, 4, 3).reshape(2 * S5_GROUPS, S5_GROUP, S5_STATE)
           for d in (dcr, dci)]

    do3 = dona.reshape(SEQ, HEADS, HEAD_DIM).transpose(1, 0, 2)
    dq, dk, dv, dbias = _na_bwd(qkv[0], qkv[1], qkv[2], bias, do3)
    g_rpb = _rpb_reduce(dbias)
    dense = jnp.stack([g.reshape(2 * S5_GROUPS, S5_STATE * S5_GROUP) for g in (g_br.transpose(0, 2, 1), g_bi.transpose(0, 2, 1), *g_c)])
    tok = emit("small", {"dense": dense, "na_rpb": g_rpb,
                         "s5_lam_re": g_lr.reshape(2, S5_GROUPS, S5_STATE), "s5_lam_im": g_li.reshape(2, S5_GROUPS, S5_STATE),
                         "s5_log_dt": g_dt.reshape(2, S5_GROUPS)})
    dqkv = jnp.stack([dq, dk, dv]).reshape(3, 2, 4, SEQ, HEAD_DIM).transpose(0, 1, 3, 2, 4).reshape(6, SEQ, IN_SHARD)
    dproj = jnp.concatenate([dqkv, du.reshape(SEQ, 2, IN_SHARD).transpose(1, 0, 2)], axis=0).astype(BF16)
    da2, dwin = _proj_bwd(dproj, a2, wts["w_in"])
    tok2 = emit("w_in", {"w_in": dwin})
    tok = tok if tok2 is None else tok + tok2
    dh1, df1, g_mix_pre, g_ffn1_post = _bwd_pre_post(da2, h1, _dep(small["mix_pre_g"], tok), dh2, f1, small["ffn1_post_g"], 0.5,
                                                     "bwd_pre_post1")
    da1, dwg1, dwu1, dwd1 = _ffn_bwd(df1, a1, gate1, up1, wts["ffn1_w_gate"], wts["ffn1_w_up"], wts["ffn1_w_down"], "ffn1_bwd")
    emit("ffn1", {"ffn1_w_gate": dwg1, "ffn1_w_up": dwu1, "ffn1_w_down": dwd1})
    dh0, g_ffn1_pre = _bwd_pre_only(da1, h0, small["ffn1_pre_g"], dh1)

    vec_g = {
        "ffn1_pre_g": g_ffn1_pre, "ffn1_post_g": g_ffn1_post, "mix_pre_g": g_mix_pre, "s5_d": g_s5_d, "s5_b_glu": g_b_glu,
        "na_out_g": g_na_out, "s5_out_g": g_s5_out, "mix_post_g": g_mix_post,
        "ffn2_pre_g": g_ffn2_pre, "ffn2_post_g": g_ffn2_post, "final_g": g_final,
    }
    return loss8[0, 0], dh0[N_META:], dh0[:N_META], vec_g


WEIGHT_NAMES = ['meta_tokens', 'ffn1_pre_g', 'ffn1_post_g', 'ffn1_w_gate', 'ffn1_w_up', 'ffn1_w_down', 'mix_pre_g', 'w_in',
                'na_rpb', 's5_lam_re', 's5_lam_im', 's5_log_dt', 's5_b_re', 's5_b_im', 's5_c_re', 's5_c_im', 's5_d',
                's5_w_glu', 's5_b_glu', 'na_out_g', 's5_out_g', 'w_out', 'mix_post_g', 'ffn2_pre_g', 'ffn2_post_g',
                'ffn2_w_gate', 'ffn2_w_up', 'ffn2_w_down', 'final_g']
BIG_NAMES = ['ffn1_w_gate', 'ffn1_w_up', 'ffn1_w_down', 'w_in', 's5_w_glu', 'w_out', 'ffn2_w_gate', 'ffn2_w_up', 'ffn2_w_down']
SMALL_NAMES = [n for n in WEIGHT_NAMES if n not in BIG_NAMES and n != 'meta_tokens']
WHOLE_NAMES = ['na_rpb', 's5_lam_re', 's5_lam_im', 's5_log_dt']
LEAD_NAMES = ['s5_b_re', 's5_b_im', 's5_c_re', 's5_c_im']


def kernel(x, meta_tokens, ffn1_pre_g, ffn1_post_g, ffn1_w_gate, ffn1_w_up, ffn1_w_down, mix_pre_g, w_in, na_rpb, s5_lam_re, s5_lam_im, s5_log_dt, s5_b_re, s5_b_im, s5_c_re, s5_c_im, s5_d, s5_w_glu, s5_b_glu, na_out_g, s5_out_g, w_out, mix_post_g, ffn2_pre_g, ffn2_post_g, ffn2_w_gate, ffn2_w_up, ffn2_w_down, final_g, loss_target, m_meta_tokens, m_ffn1_pre_g, m_ffn1_post_g, m_ffn1_w_gate, m_ffn1_w_up, m_ffn1_w_down, m_mix_pre_g, m_w_in, m_na_rpb, m_s5_lam_re, m_s5_lam_im, m_s5_log_dt, m_s5_b_re, m_s5_b_im, m_s5_c_re, m_s5_c_im, m_s5_d, m_s5_w_glu, m_s5_b_glu, m_na_out_g, m_s5_out_g, m_w_out, m_mix_post_g, m_ffn2_pre_g, m_ffn2_post_g, m_ffn2_w_gate, m_ffn2_w_up, m_ffn2_w_down, m_final_g, v_meta_tokens, v_ffn1_pre_g, v_ffn1_post_g, v_ffn1_w_gate, v_ffn1_w_up, v_ffn1_w_down, v_mix_pre_g, v_w_in, v_na_rpb, v_s5_lam_re, v_s5_lam_im, v_s5_log_dt, v_s5_b_re, v_s5_b_im, v_s5_c_re, v_s5_c_im, v_s5_d, v_s5_w_glu, v_s5_b_glu, v_na_out_g, v_s5_out_g, v_w_out, v_mix_post_g, v_ffn2_pre_g, v_ffn2_post_g, v_ffn2_w_gate, v_ffn2_w_up, v_ffn2_w_down, v_final_g):
    args = dict(locals())
    w = {n: args[n] for n in WEIGHT_NAMES}
    m = {n: args["m_" + n] for n in WEIGHT_NAMES}
    v = {n: args["v_" + n] for n in WEIGHT_NAMES}

    small = {n: w[n] for n in SMALL_NAMES}

    pending = {}

    def start(group, names, arrays, gather):
        lands = _place_own(arrays, gather, "own_" + group)
        send_sems, recv_sems, arrays, lands, token = _exchange_start(arrays, lands, gather, "start_" + group)
        pending[group] = (names, send_sems, recv_sems, arrays, lands, gather)
        return token

    def finish(group, after):
        names, send_sems, recv_sems, arrays, lands, gather = pending.pop(group)
        return dict(zip(names, _exchange_wait(send_sems, recv_sems, arrays, lands, after, gather, "wait_" + group)))

    first = ["ffn1_w_gate", "ffn1_w_up", "ffn1_w_down"]
    token1 = start("ffn1", first + ["meta_tokens"], [w[n][0].astype(BF16) for n in first] + [w["meta_tokens"]], True)
    tokens = [start(group, names, [(w[n][0] + token1[0, 0]).astype(BF16) for n in names], True)
              for group, names in (("w_in", ["w_in"]), ("mix", ["s5_w_glu", "w_out"]),
                                   ("ffn2", ["ffn2_w_gate", "ffn2_w_up", "ffn2_w_down"]))]

    def get_w(group, after):
        if group == "ffn1":
            got = finish(group, list(after) + tokens)
            got["meta_tokens"] = got["meta_tokens"].transpose(1, 0, 2).reshape(N_META, D_MODEL)
        else:
            got = finish(group, after)
        if group == "mix":
            got = {"s5_w_glu": got["s5_w_glu"].reshape(S5_WIDTH, S5_WIDTH), "w_out": got["w_out"].reshape(D_MODEL, D_MODEL)}
        return got

    def emit(group, grads):
        return start("g_" + group, list(grads), list(grads.values()), group == "small")[0, 0]

    loss_local, grad_x, gmeta, vec_g = _local_step(x[0], loss_target[0], get_w, small, emit)
    loss = lax.psum(loss_local, AXES)
    res = {}

    def update_shard(n, pieces):
        shape = w[n].shape
        w2 = w[n].reshape(shape[-2], shape[-1])
        outs = _adamw(w2, m[n].reshape(w2.shape), v[n].reshape(w2.shape), pieces, "adamw_" + n)
        res[n] = [o.reshape(shape) for o in outs]

    for group in ("g_ffn2", "g_mix", "g_w_in"):
        for n, pieces in finish(group, grad_x).items():
            update_shard(n, pieces)
    g8 = finish("g_small", grad_x)
    dense = _sum8(g8["dense"], "sum_dense")
    for i, n in enumerate(LEAD_NAMES):
        shape3 = (2 * S5_GROUPS,) + w[n].shape[-2:]
        g = dense[i].reshape(shape3)
        upd = _adamw_lead(w[n].reshape(shape3), m[n].reshape(shape3), v[n].reshape(shape3), g, 8, "adamw_" + n)
        res[n] = [o.reshape(w[n].shape) for o in [g] + list(upd)]

    done = [res[n][1] for n in ("ffn2_w_gate", "ffn2_w_up", "ffn2_w_down", "w_in", "w_out", "s5_w_glu") + tuple(LEAD_NAMES)]
    packed8, gmeta8 = _exchange([_pack_vectors(vec_g), gmeta], True, "gather_vectors", after=done)
    for n, pieces in finish("g_ffn1", packed8).items():
        update_shard(n, pieces)
    _, _, _, me = _me()
    update_shard("meta_tokens", lax.dynamic_slice_in_dim(gmeta8, me * (D_MODEL // N_DEV), D_MODEL // N_DEV, axis=2))

    outs = _adamw_small(packed8, [(w[n], m[n], v[n]) for n in VEC_NAMES], [(w[n][0], m[n][0], v[n][0], g8[n]) for n in WHOLE_NAMES])
    for i, n in enumerate(VEC_NAMES + WHOLE_NAMES):
        res[n] = [o.reshape(w[n].shape) for o in outs[4 * i:4 * i + 4]]

    out = [loss, grad_x[None]]
    for kind in range(4):
        out += [res[n][kind] for n in WEIGHT_NAMES]
    return tuple(out)
```

```python
import functools
import math

import numpy as np
import jax
import jax.numpy as jnp
from jax import lax
from jax.experimental import pallas as pl
from jax.experimental.pallas import tpu as pltpu

F32 = jnp.float32
BF16 = jnp.bfloat16
SDS = jax.ShapeDtypeStruct

D_MODEL = 1024
N_TOK = 2048
N_META = 16
SEQ = N_TOK + N_META
ROW_TILE = 688
N_ROW_TILES = SEQ // ROW_TILE
N_DEV = 8
D_FF = 2816
FF_SHARD = D_FF // N_DEV
IN_SHARD = 256
NA_WIDTH = 512
S5_WIDTH = 512
HEADS = 8
HEAD_DIM = 64
GRID_W = 64
GRID_ROWS = N_TOK // GRID_W
KH = 8
KW = 16
NA_RB = 4
NA_KR = KH + NA_RB - 1
NA_BLOCKS = GRID_ROWS // NA_RB
NA_QB = NA_RB * GRID_W
NA_KB = NA_KR * GRID_W
NA_TYPES = 3
S5_GROUPS = 32
S5_GROUP = 16
S5_STATE = 64
S5_CHUNKS = 4
CH_W = S5_WIDTH // S5_CHUNKS
ST_W = S5_GROUPS * S5_STATE // S5_CHUNKS
SCAN_BLOCKS = 8
SCAN_T = SEQ // SCAN_BLOCKS
RMS_EPS = 1e-6
NEG_INF = -1e30
ATT_SCALE = HEAD_DIM ** -0.5
ADAM_LR, ADAM_B1, ADAM_B2, ADAM_EPS, ADAM_WD, ADAM_STEP = 0.001, 0.9, 0.999, 1e-08, 0.01, 10
VMEM_LIMIT = 56 * 1024 * 1024
MESH = pl.DeviceIdType.MESH
AXES = ("x", "y", "c")


def _params(sem=None):
    return pltpu.CompilerParams(dimension_semantics=sem, vmem_limit_bytes=VMEM_LIMIT)


def _dot(a, b):
    return jnp.dot(a, b, preferred_element_type=F32)


def _dot_nt(a, b):
    return lax.dot_general(a, b, (((1,), (1,)), ((), ())), preferred_element_type=F32)


def _dot_tn(a, b):
    return lax.dot_general(a, b, (((0,), (0,)), ((), ())), preferred_element_type=F32)


def _rstd(x):
    return lax.rsqrt(jnp.mean(x * x, axis=-1, keepdims=True) + RMS_EPS)


def _rms_bwd(x, r, g, dy):
    dyg = dy * g
    xr = x * r
    dx = r * (dyg - xr * jnp.mean(dyg * xr, axis=-1, keepdims=True))
    return dx, dy * xr


def _rows(i, size=ROW_TILE):
    return pl.ds(pl.multiple_of(i * size, 16), size)


def _row_spec(width):
    return pl.BlockSpec((ROW_TILE, width), lambda i: (i, 0))


def _fix_spec(shape):
    return pl.BlockSpec(shape, lambda i: (0,) * len(shape))


def _split3(x):
    hi = x.astype(BF16)
    r1 = x - hi.astype(F32)
    mid = r1.astype(BF16)
    lo = (r1 - mid.astype(F32)).astype(BF16)
    return hi, mid, lo


def _prenorm(x, g):
    def body(x_ref, g_ref, a_ref):
        xv = x_ref[...]
        a_ref[...] = (xv * _rstd(xv) * g_ref[...]).astype(BF16)

    return pl.pallas_call(
        body, grid=(N_ROW_TILES,), in_specs=[_row_spec(D_MODEL), _fix_spec((1, D_MODEL))],
        out_specs=_row_spec(D_MODEL), out_shape=SDS((SEQ, D_MODEL), BF16), name="prenorm",
        compiler_params=_params(("parallel",)))(x, g)


def _post_pre(f, hres, g_post, g_next, scale, name):
    def body(f_ref, h_ref, gp_ref, gn_ref, ho_ref, a_ref):
        fv = f_ref[...]
        h = h_ref[...] + scale * (fv * _rstd(fv) * gp_ref[...])
        ho_ref[...] = h
        a_ref[...] = (h * _rstd(h) * gn_ref[...]).astype(BF16)

    return pl.pallas_call(
        body, grid=(N_ROW_TILES,),
        in_specs=[_row_spec(D_MODEL), _row_spec(D_MODEL), _fix_spec((1, D_MODEL)), _fix_spec((1, D_MODEL))],
        out_specs=[_row_spec(D_MODEL), _row_spec(D_MODEL)],
        out_shape=[SDS((SEQ, D_MODEL), F32), SDS((SEQ, D_MODEL), BF16)], name=name,
        compiler_params=_params(("parallel",)))(f, hres, g_post, g_next)


def _final_loss(f2, h2, g_post, g_final, target):
    def body(f_ref, h_ref, gp_ref, gf_ref, t_ref, loss_ref, dh_ref, df_ref, dgf_ref, dgp_ref):
        i = pl.program_id(0)
        fv = f_ref[...]
        r1 = _rstd(fv)
        gp = gp_ref[...]
        h3 = h_ref[...] + 0.5 * (fv * r1 * gp)
        r2 = _rstd(h3)
        gf = gf_ref[...]
        y = h3 * r2 * gf
        row = lax.broadcasted_iota(jnp.int32, (ROW_TILE, 1), 0) + i * ROW_TILE
        err = jnp.where(row >= N_META, y - t_ref[...], 0.0)
        part = 0.5 * jnp.sum(jnp.mean(err * err, axis=-1, keepdims=True))
        dy = err * (1.0 / D_MODEL)
        dh3, dgf = _rms_bwd(h3, r2, gf, dy)
        dh_ref[...] = dh3
        df, dgp = _rms_bwd(fv, r1, gp, 0.5 * dh3)
        df_ref[...] = df.astype(BF16)

        @pl.when(i == 0)
        def _():
            loss_ref[...] = jnp.zeros_like(loss_ref)
            dgf_ref[...] = jnp.zeros_like(dgf_ref)
            dgp_ref[...] = jnp.zeros_like(dgp_ref)

        loss_ref[...] += part
        dgf_ref[...] += jnp.sum(dgf, axis=0, keepdims=True)
        dgp_ref[...] += jnp.sum(dgp, axis=0, keepdims=True)

    gain = _fix_spec((1, D_MODEL))
    return pl.pallas_call(
        body, grid=(N_ROW_TILES,),
        in_specs=[_row_spec(D_MODEL), _row_spec(D_MODEL), gain, gain, _row_spec(D_MODEL)],
        out_specs=[_fix_spec((8, 128)), _row_spec(D_MODEL), _row_spec(D_MODEL), gain, gain],
        out_shape=[SDS((8, 128), F32), SDS((SEQ, D_MODEL), F32), SDS((SEQ, D_MODEL), BF16),
                   SDS((1, D_MODEL), F32), SDS((1, D_MODEL), F32)],
        name="final_loss", compiler_params=_params(("arbitrary",)))(f2, h2, g_post, g_final, target)


def _bwd_pre_post(da, h, g_pre, dh_res, fprev, g_post, scale, name):
    def body(da_ref, h_ref, gpre_ref, dhr_ref, f_ref, gpost_ref, dh_ref, df_ref, dgpre_ref, dgpost_ref):
        i = pl.program_id(0)
        hv = h_ref[...]
        dxa, dgpre = _rms_bwd(hv, _rstd(hv), gpre_ref[...], da_ref[...])
        dh = dhr_ref[...] + dxa
        dh_ref[...] = dh
        fv = f_ref[...]
        df, dgpost = _rms_bwd(fv, _rstd(fv), gpost_ref[...], scale * dh)
        df_ref[...] = df.astype(BF16)

        @pl.when(i == 0)
        def _():
            dgpre_ref[...] = jnp.zeros_like(dgpre_ref)
            dgpost_ref[...] = jnp.zeros_like(dgpost_ref)

        dgpre_ref[...] += jnp.sum(dgpre, axis=0, keepdims=True)
        dgpost_ref[...] += jnp.sum(dgpost, axis=0, keepdims=True)

    gain = _fix_spec((1, D_MODEL))
    row = _row_spec(D_MODEL)
    return pl.pallas_call(
        body, grid=(N_ROW_TILES,), in_specs=[row, row, gain, row, row, gain],
        out_specs=[row, row, gain, gain],
        out_shape=[SDS((SEQ, D_MODEL), F32), SDS((SEQ, D_MODEL), BF16), SDS((1, D_MODEL), F32), SDS((1, D_MODEL), F32)],
        name=name, compiler_params=_params(("arbitrary",)))(da, h, g_pre, dh_res, fprev, g_post)


def _bwd_pre_only(da, h, g_pre, dh_res):
    def body(da_ref, h_ref, gpre_ref, dhr_ref, dh_ref, dgpre_ref):
        i = pl.program_id(0)
        hv = h_ref[...]
        dxa, dgpre = _rms_bwd(hv, _rstd(hv), gpre_ref[...], da_ref[...])
        dh_ref[...] = dhr_ref[...] + dxa

        @pl.when(i == 0)
        def _():
            dgpre_ref[...] = jnp.zeros_like(dgpre_ref)

        dgpre_ref[...] += jnp.sum(dgpre, axis=0, keepdims=True)

    gain = _fix_spec((1, D_MODEL))
    row = _row_spec(D_MODEL)
    return pl.pallas_call(
        body, grid=(N_ROW_TILES,), in_specs=[row, row, gain, row], out_specs=[row, gain],
        out_shape=[SDS((SEQ, D_MODEL), F32), SDS((1, D_MODEL), F32)],
        name="bwd_pre_only", compiler_params=_params(("arbitrary",)))(da, h, g_pre, dh_res)


def _ffn_fwd(a, wg, wu, wd, name):
    def body(a_ref, wg_ref, wu_ref, wd_ref, gate_ref, up_ref, f_ref):
        j = pl.program_id(0)

        def tile(i, carry):
            rows = _rows(i)
            at = a_ref[rows, :]
            gate = _dot(at, wg_ref[...])
            up = _dot(at, wu_ref[...])
            gate_ref[rows, :] = gate
            up_ref[rows, :] = up
            act = (gate * jax.nn.sigmoid(gate) * up).astype(BF16)
            contrib = _dot(act, wd_ref[...])

            @pl.when(j == 0)
            def _():
                f_ref[rows, :] = contrib

            @pl.when(j != 0)
            def _():
                f_ref[rows, :] += contrib

            return carry

        lax.fori_loop(0, N_ROW_TILES, tile, 0)

    shard_cols = pl.BlockSpec((None, D_MODEL, FF_SHARD), lambda j: (j, 0, 0))
    shard_rows = pl.BlockSpec((None, FF_SHARD, D_MODEL), lambda j: (j, 0, 0))
    hid = pl.BlockSpec((None, SEQ, FF_SHARD), lambda j: (j, 0, 0))
    full = pl.BlockSpec((SEQ, D_MODEL), lambda j: (0, 0))
    return pl.pallas_call(
        body, grid=(N_DEV,), in_specs=[full, shard_cols, shard_cols, shard_rows], out_specs=[hid, hid, full],
        out_shape=[SDS((N_DEV, SEQ, FF_SHARD), F32), SDS((N_DEV, SEQ, FF_SHARD), F32), SDS((SEQ, D_MODEL), F32)],
        name=name, compiler_params=_params(("arbitrary",)))(a, wg, wu, wd)


def _ffn_bwd(df, a, gate, up, wg, wu, wd, name):
    def body(df_ref, a_ref, gate_ref, up_ref, wg_ref, wu_ref, wd_ref, da_ref, dwg_ref, dwu_ref, dwd_ref,
             acc_g, acc_u, acc_d):
        j = pl.program_id(0)

        def tile(i, carry):
            rows = _rows(i)
            dft = df_ref[rows, :]
            at = a_ref[rows, :]
            gate = gate_ref[rows, :]
            up = up_ref[rows, :]
            dact = _dot_nt(dft, wd_ref[...])
            sig = jax.nn.sigmoid(gate)
            silu = gate * sig
            dgate = (dact * up * (sig * (1.0 + gate * (1.0 - sig)))).astype(BF16)
            dup = (dact * silu).astype(BF16)
            act = (silu * up).astype(BF16)
            dwd = _dot_tn(act, dft)
            dwg = _dot_tn(at, dgate)
            dwu = _dot_tn(at, dup)
            dat = _dot_nt(dgate, wg_ref[...]) + _dot_nt(dup, wu_ref[...])

            @pl.when(i == 0)
            def _():
                acc_d[...] = dwd
                acc_g[...] = dwg
                acc_u[...] = dwu

            @pl.when(i != 0)
            def _():
                acc_d[...] += dwd
                acc_g[...] += dwg
                acc_u[...] += dwu

            @pl.when(j == 0)
            def _():
                da_ref[rows, :] = dat

            @pl.when(j != 0)
            def _():
                da_ref[rows, :] += dat

            return carry

        lax.fori_loop(0, N_ROW_TILES, tile, 0)
        dwg_ref[...] = acc_g[...].astype(BF16)
        dwu_ref[...] = acc_u[...].astype(BF16)
        dwd_ref[...] = acc_d[...].astype(BF16)

    shard_cols = pl.BlockSpec((None, D_MODEL, FF_SHARD), lambda j: (j, 0, 0))
    shard_rows = pl.BlockSpec((None, FF_SHARD, D_MODEL), lambda j: (j, 0, 0))
    hid = pl.BlockSpec((None, SEQ, FF_SHARD), lambda j: (j, 0, 0))
    full = pl.BlockSpec((SEQ, D_MODEL), lambda j: (0, 0))
    return pl.pallas_call(
        body, grid=(N_DEV,), in_specs=[full, full, hid, hid, shard_cols, shard_cols, shard_rows],
        out_specs=[full, shard_cols, shard_cols, shard_rows],
        out_shape=[SDS((SEQ, D_MODEL), F32), SDS((N_DEV, D_MODEL, FF_SHARD), BF16),
                   SDS((N_DEV, D_MODEL, FF_SHARD), BF16), SDS((N_DEV, FF_SHARD, D_MODEL), BF16)],
        scratch_shapes=[pltpu.VMEM((D_MODEL, FF_SHARD), F32), pltpu.VMEM((D_MODEL, FF_SHARD), F32),
                        pltpu.VMEM((FF_SHARD, D_MODEL), F32)],
        name=name, compiler_params=_params(("arbitrary",)))(df, a, gate, up, wg, wu, wd)


def _proj_fwd(a, w):
    def body(a_ref, w_ref, o_ref):
        def tile(i, carry):
            rows = _rows(i)
            o_ref[rows, :] = _dot(a_ref[rows, :], w_ref[...])
            return carry

        lax.fori_loop(0, N_ROW_TILES, tile, 0)

    return pl.pallas_call(
        body, grid=(N_DEV,),
        in_specs=[pl.BlockSpec((SEQ, D_MODEL), lambda j: (0, 0)), pl.BlockSpec((None, D_MODEL, IN_SHARD), lambda j: (j, 0, 0))],
        out_specs=pl.BlockSpec((None, SEQ, IN_SHARD), lambda j: (j, 0, 0)),
        out_shape=SDS((N_DEV, SEQ, IN_SHARD), F32), name="proj_fwd",
        compiler_params=_params(("parallel",)))(a, w)


def _proj_bwd(dproj, a, w):
    def body(dp_ref, a_ref, w_ref, da_ref, dw_ref, acc):
        j = pl.program_id(0)

        def tile(i, carry):
            rows = _rows(i)
            dpt = dp_ref[rows, :]
            dw = _dot_tn(a_ref[rows, :], dpt)
            dat = _dot_nt(dpt, w_ref[...])

            @pl.when(i == 0)
            def _():
                acc[...] = dw

            @pl.when(i != 0)
            def _():
                acc[...] += dw

            @pl.when(j == 0)
            def _():
                da_ref[rows, :] = dat

            @pl.when(j != 0)
            def _():
                da_ref[rows, :] += dat

            return carry

        lax.fori_loop(0, N_ROW_TILES, tile, 0)
        dw_ref[...] = acc[...].astype(BF16)

    full = pl.BlockSpec((SEQ, D_MODEL), lambda j: (0, 0))
    wspec = pl.BlockSpec((None, D_MODEL, IN_SHARD), lambda j: (j, 0, 0))
    return pl.pallas_call(
        body, grid=(N_DEV,),
        in_specs=[pl.BlockSpec((None, SEQ, IN_SHARD), lambda j: (j, 0, 0)), full, wspec],
        out_specs=[full, wspec],
        out_shape=[SDS((SEQ, D_MODEL), F32), SDS((N_DEV, D_MODEL, IN_SHARD), BF16)],
        scratch_shapes=[pltpu.VMEM((D_MODEL, IN_SHARD), F32)],
        name="proj_bwd", compiler_params=_params(("arbitrary",)))(dproj, a, w)


def _na_consts():
    c = np.arange(GRID_W)
    col_start = np.clip(c - KW // 2, 0, GRID_W - KW)
    col_in = (c[None, :] >= col_start[:, None]) & (c[None, :] < col_start[:, None] + KW)
    dc = np.clip(c[None, :] - c[:, None] + KW - 1, 0, 2 * KW - 2)
    onehot = np.zeros((128, GRID_W * GRID_W), np.float32)
    qq, kk = np.meshgrid(c, c, indexing="ij")
    onehot[dc[col_in], (qq * GRID_W + kk)[col_in]] = 1.0
    negmask = np.where(col_in, 0.0, NEG_INF).astype(np.float32).reshape(1, -1)
    sel = np.zeros((16, NA_TYPES * NA_RB * NA_KR), np.float32)
    for t in range(NA_TYPES):
        for a in range(NA_RB):
            for b in range(NA_KR):
                dr = _na_pair(t, a, b)
                if dr is not None:
                    sel[dr, (t * NA_RB + a) * NA_KR + b] = 1.0
    return onehot, negmask, sel


def _na_pair(block_type, a, b):
    if block_type == 0:
        return b - a + KH - 1 if b < KH else None
    if block_type == 1:
        return b - a + KH // 2 - 1 if a <= b < a + KH else None
    return b - a if b >= NA_KR - KH else None


def _rpb_expand(rpb):
    onehot, negmask, _ = _na_consts()
    rows = HEADS * (2 * KH - 1)
    rpb_pad = jnp.pad(rpb.reshape(rows, 2 * KW - 1), ((0, 128 - rows), (0, 128 - (2 * KW - 1))))

    def body(r_ref, oh_ref, m_ref, t_ref):
        hi, mid, lo = _split3(r_ref[...])
        oh = oh_ref[...]
        t_ref[...] = _dot(hi, oh) + _dot(mid, oh) + _dot(lo, oh) + m_ref[...]

    table = pl.pallas_call(body, out_shape=SDS((128, GRID_W * GRID_W), F32), name="rpb_expand",
                           compiler_params=_params())(rpb_pad, jnp.asarray(onehot, BF16), jnp.asarray(negmask))
    t4 = table[:rows].reshape(HEADS, 2 * KH - 1, GRID_W, GRID_W)
    outside = jnp.full((HEADS, GRID_W, GRID_W), NEG_INF, F32)

    def slab(t, a, b):
        dr = _na_pair(t, a, b)
        return outside if dr is None else t4[:, dr]

    pairs = jnp.stack([jnp.stack([jnp.stack([slab(t, a, b) for b in range(NA_KR)], axis=1) for a in range(NA_RB)], axis=1)
                       for t in range(NA_TYPES)], axis=1)
    return pairs.transpose(0, 1, 2, 4, 3, 5).reshape(HEADS, NA_TYPES, NA_QB, NA_KB)


def _rpb_reduce(dbias):
    onehot, _, sel = _na_consts()
    n_pairs = NA_TYPES * NA_RB * NA_KR
    x = dbias.reshape(HEADS, NA_TYPES, NA_RB, GRID_W, NA_KR, GRID_W).transpose(0, 1, 2, 4, 3, 5).reshape(HEADS, n_pairs, GRID_W * GRID_W)

    def body(x_ref, oht_ref, sel_ref, o_ref):
        hi, mid, lo = _split3(x_ref[...])
        oht = oht_ref[...]
        y = _dot(hi, oht) + _dot(mid, oht) + _dot(lo, oht)
        hi, mid, lo = _split3(y)
        s = sel_ref[...]
        o_ref[...] = _dot(s, hi) + _dot(s, mid) + _dot(s, lo)

    return pl.pallas_call(
        body, grid=(HEADS,),
        in_specs=[pl.BlockSpec((None, n_pairs, GRID_W * GRID_W), lambda h: (h, 0, 0)),
                  pl.BlockSpec((GRID_W * GRID_W, 128), lambda h: (0, 0)), pl.BlockSpec((16, n_pairs), lambda h: (0, 0))],
        out_specs=pl.BlockSpec((None, 16, 128), lambda h: (h, 0, 0)),
        out_shape=SDS((HEADS, 16, 128), F32), name="rpb_reduce",
        compiler_params=_params(("parallel",)))(x, jnp.asarray(onehot.T, BF16), jnp.asarray(sel, BF16))


def _block_geometry(g):
    start = jnp.clip(g * NA_RB - KH // 2, 0, GRID_ROWS - NA_KR)
    block_type = jnp.where(g == 0, 0, jnp.where(g == NA_BLOCKS - 1, 2, 1))
    q0 = pl.multiple_of(N_META + g * NA_QB, 16)
    k0 = pl.multiple_of(N_META + start * GRID_W, 16)
    return block_type, q0, k0


def _na_probs(q, kk, km, bias):
    s = _dot_nt(q, kk) * ATT_SCALE + bias
    sm = _dot_nt(q, km) * ATT_SCALE
    m = jnp.maximum(jnp.max(s, axis=-1, keepdims=True), jnp.max(sm, axis=-1, keepdims=True))
    p = jnp.exp(s - m)
    pm = jnp.exp(sm - m)
    inv = 1.0 / (jnp.sum(p, axis=-1, keepdims=True) + jnp.sum(pm, axis=-1, keepdims=True))
    return p * inv, pm * inv


def _meta_probs(qm, km):
    s = _dot_nt(qm, km) * ATT_SCALE
    p = jnp.exp(s - jnp.max(s, axis=-1, keepdims=True))
    return p / jnp.sum(p, axis=-1, keepdims=True)


def _na_fwd(q, k, v, bias):
    def body(q_ref, k_ref, v_ref, b_ref, o_ref):
        km = k_ref[0:N_META, :].astype(BF16)
        vm = v_ref[0:N_META, :].astype(BF16)
        pmm = _meta_probs(q_ref[0:N_META, :].astype(BF16), km)
        o_ref[0:N_META, :] = _dot(pmm.astype(BF16), vm)

        def block(g, carry):
            block_type, q0, k0 = _block_geometry(g)
            qb = q_ref[pl.ds(q0, NA_QB), :].astype(BF16)
            kk = k_ref[pl.ds(k0, NA_KB), :].astype(BF16)
            vv = v_ref[pl.ds(k0, NA_KB), :].astype(BF16)
            p, pm = _na_probs(qb, kk, km, b_ref[block_type])
            o_ref[pl.ds(q0, NA_QB), :] = _dot(p.astype(BF16), vv) + _dot(pm.astype(BF16), vm)
            return carry

        lax.fori_loop(0, NA_BLOCKS, block, 0)

    head = pl.BlockSpec((None, SEQ, HEAD_DIM), lambda h: (h, 0, 0))
    return pl.pallas_call(
        body, grid=(HEADS,), in_specs=[head, head, head, pl.BlockSpec((None, NA_TYPES, NA_QB, NA_KB), lambda h: (h, 0, 0, 0))],
        out_specs=head, out_shape=SDS((HEADS, SEQ, HEAD_DIM), F32), name="na_fwd",
        compiler_params=_params(("parallel",)))(q, k, v, bias)


def _na_bwd(q, k, v, bias, do):
    def body(q_ref, k_ref, v_ref, b_ref, do_ref, dq_ref, dk_ref, dv_ref, db_ref):
        km = k_ref[0:N_META, :].astype(BF16)
        vm = v_ref[0:N_META, :].astype(BF16)
        dk_ref[...] = jnp.zeros_like(dk_ref)
        dv_ref[...] = jnp.zeros_like(dv_ref)
        db_ref[...] = jnp.zeros_like(db_ref)

        qm = q_ref[0:N_META, :].astype(BF16)
        dom = do_ref[0:N_META, :].astype(BF16)
        pmm = _meta_probs(qm, km)
        dpm = _dot_nt(dom, vm)
        dsm = (pmm * (dpm - jnp.sum(pmm * dpm, axis=-1, keepdims=True)) * ATT_SCALE).astype(BF16)
        dq_ref[0:N_META, :] = _dot(dsm, km)
        dkm0 = _dot_tn(dsm, qm)
        dvm0 = _dot_tn(pmm.astype(BF16), dom)

        def block(g, carry):
            dkm, dvm = carry
            block_type, q0, k0 = _block_geometry(g)
            qb = q_ref[pl.ds(q0, NA_QB), :].astype(BF16)
            kk = k_ref[pl.ds(k0, NA_KB), :].astype(BF16)
            vv = v_ref[pl.ds(k0, NA_KB), :].astype(BF16)
            dob = do_ref[pl.ds(q0, NA_QB), :].astype(BF16)
            p, pm = _na_probs(qb, kk, km, b_ref[block_type])
            dp = _dot_nt(dob, vv)
            dpm_ = _dot_nt(dob, vm)
            delta = jnp.sum(p * dp, axis=-1, keepdims=True) + jnp.sum(pm * dpm_, axis=-1, keepdims=True)
            ds = p * (dp - delta)
            dsm_ = pm * (dpm_ - delta)
            db_ref[block_type] += ds
            dsb = (ds * ATT_SCALE).astype(BF16)
            dsmb = (dsm_ * ATT_SCALE).astype(BF16)
            dq_ref[pl.ds(q0, NA_QB), :] = _dot(dsb, kk) + _dot(dsmb, km)
            dk_ref[pl.ds(k0, NA_KB), :] += _dot_tn(dsb, qb)
            dv_ref[pl.ds(k0, NA_KB), :] += _dot_tn(p.astype(BF16), dob)
            return dkm + _dot_tn(dsmb, qb), dvm + _dot_tn(pm.astype(BF16), dob)

        dkm, dvm = lax.fori_loop(0, NA_BLOCKS, block, (dkm0, dvm0))
        dk_ref[0:N_META, :] = dkm
        dv_ref[0:N_META, :] = dvm

    head = pl.BlockSpec((None, SEQ, HEAD_DIM), lambda h: (h, 0, 0))
    bspec = pl.BlockSpec((None, NA_TYPES, NA_QB, NA_KB), lambda h: (h, 0, 0, 0))
    return pl.pallas_call(
        body, grid=(HEADS,), in_specs=[head, head, head, bspec, head], out_specs=[head, head, head, bspec],
        out_shape=[SDS((HEADS, SEQ, HEAD_DIM), F32)] * 3 + [SDS((HEADS, NA_TYPES, NA_QB, NA_KB), F32)],
        name="na_bwd", compiler_params=_params(("parallel",)))(q, k, v, bias, do)


def _cmul(ar, ai, br, bi):
    return ar * br - ai * bi, ar * bi + ai * br


def _cpow(ar, ai, n):
    rr, ri = None, None
    br, bi = ar, ai
    while n:
        if n & 1:
            rr, ri = (br, bi) if rr is None else _cmul(rr, ri, br, bi)
        n >>= 1
        if n:
            br, bi = _cmul(br, bi, br, bi)
    return rr, ri


def _s5_prep(lr, li, logdt, bre, bim):
    def body(lr_ref, li_ref, dt_ref, br_ref, bi_ref, lbr_ref, lbi_ref, bbr_ref, bbi_ref):
        lr_, li_ = lr_ref[...], li_ref[...]
        dt = jnp.exp(dt_ref[...])
        mag = jnp.exp(lr_ * dt)
        lbr = mag * jnp.cos(li_ * dt)
        lbi = mag * jnp.sin(li_ * dt)
        lbr_ref[...] = lbr
        lbi_ref[...] = lbi
        den = lr_ * lr_ + li_ * li_
        xr = lbr - 1.0
        cr = (xr * lr_ + lbi * li_) / den
        ci = (lbi * lr_ - xr * li_) / den
        br, bi = br_ref[...], bi_ref[...]
        bbr_ref[...] = cr[:, None, :] * br - ci[:, None, :] * bi
        bbi_ref[...] = cr[:, None, :] * bi + ci[:, None, :] * br

    n = 2 * S5_GROUPS
    return pl.pallas_call(
        body, out_shape=[SDS((n, S5_STATE), F32)] * 2 + [SDS((n, S5_GROUP, S5_STATE), F32)] * 2,
        name="s5_prep", compiler_params=_params())(lr, li, logdt, bre, bim)


def _s5_prep_bwd(lr, li, logdt, bre, bim, dar, dai, dbbr, dbbi):
    def body(lr_ref, li_ref, dt_ref, br_ref, bi_ref, dar_ref, dai_ref, dbr_ref, dbi_ref,
             glr_ref, gli_ref, gdt_ref, gbr_ref, gbi_ref):
        lr_, li_ = lr_ref[...], li_ref[...]
        dt = jnp.exp(dt_ref[...])
        mag = jnp.exp(lr_ * dt)
        lbr = mag * jnp.cos(li_ * dt)
        lbi = mag * jnp.sin(li_ * dt)
        den = lr_ * lr_ + li_ * li_
        xr = lbr - 1.0
        cr = (xr * lr_ + lbi * li_) / den
        ci = (lbi * lr_ - xr * li_) / den
        br, bi = br_ref[...], bi_ref[...]
        dbr, dbi = dbr_ref[...], dbi_ref[...]
        gbr_ref[...] = cr[:, None, :] * dbr + ci[:, None, :] * dbi
        gbi_ref[...] = cr[:, None, :] * dbi - ci[:, None, :] * dbr
        gcr = jnp.sum(dbr * br + dbi * bi, axis=1)
        gci = jnp.sum(dbi * br - dbr * bi, axis=1)
        ilr, ili = lr_ / den, li_ / den
        tr, ti = _cmul(gcr, gci, ilr, ili)
        glbr = dar_ref[...] + tr
        glbi = dai_ref[...] + ti
        dr_, di_ = _cmul(tr, ti, cr, -ci)
        gwr, gwi = _cmul(glbr, glbi, lbr, -lbi)
        glr_ref[...] = gwr * dt - dr_
        gli_ref[...] = gwi * dt - di_
        gdt_ref[...] = jnp.sum(gwr * lr_ + gwi * li_, axis=-1, keepdims=True) * dt

    n = 2 * S5_GROUPS
    return pl.pallas_call(
        body, out_shape=[SDS((n, S5_STATE), F32)] * 2 + [SDS((n, 1), F32)] + [SDS((n, S5_GROUP, S5_STATE), F32)] * 2,
        name="s5_prep_bwd", compiler_params=_params())(lr, li, logdt, bre, bim, dar, dai, dbbr, dbbi)


def _scan_local(xr_ref, xi_ref, ar8, ai8, reverse):
    def step(i, carry):
        sr, si = carry
        idx = (SCAN_T - 1 - i) if reverse else i
        rows = pl.ds(pl.multiple_of(idx * SCAN_BLOCKS, SCAN_BLOCKS), SCAN_BLOCKS)
        nr = ar8 * sr - ai8 * si + xr_ref[rows, :]
        ni = ar8 * si + ai8 * sr + xi_ref[rows, :]
        xr_ref[rows, :] = nr
        xi_ref[rows, :] = ni
        return nr, ni

    z = jnp.zeros(ar8.shape, F32)
    return lax.fori_loop(0, SCAN_T, step, (z, z))


def _scan_carries(er, ei, atr, ati, reverse):
    row = lax.broadcasted_iota(jnp.int32, er.shape, 0)
    cr = jnp.zeros((1, er.shape[1]), F32)
    ci = cr
    outr = jnp.zeros(er.shape, F32)
    outi = outr
    order = range(SCAN_BLOCKS - 1, -1, -1) if reverse else range(SCAN_BLOCKS)
    for b in order:
        outr = jnp.where(row == b, cr, outr)
        outi = jnp.where(row == b, ci, outi)
        nr, ni = _cmul(atr, ati, cr, ci)
        cr, ci = nr + er[b:b + 1, :], ni + ei[b:b + 1, :]
    return outr, outi


def _scan_fixup(xr_ref, xi_ref, cr8, ci8, ar8, ai8, reverse):
    def step(i, carry):
        pr, pi = carry
        idx = (SCAN_T - 1 - i) if reverse else i
        rows = pl.ds(pl.multiple_of(idx * SCAN_BLOCKS, SCAN_BLOCKS), SCAN_BLOCKS)
        fr, fi = _cmul(pr, pi, cr8, ci8)
        xr_ref[rows, :] += fr
        xi_ref[rows, :] += fi
        return _cmul(pr, pi, ar8, ai8)

    lax.fori_loop(0, SCAN_T, step, (ar8, ai8))


def _scan(xr_ref, xi_ref, ar, ai, reverse):
    n = ar.shape[1]
    ar8 = jnp.broadcast_to(ar, (SCAN_BLOCKS, n))
    ai8 = jnp.broadcast_to(ai, (SCAN_BLOCKS, n))
    er, ei = _scan_local(xr_ref, xi_ref, ar8, ai8, reverse)
    atr, ati = _cpow(ar, ai, SCAN_T)
    cr8, ci8 = _scan_carries(er, ei, atr, ati, reverse)
    _scan_fixup(xr_ref, xi_ref, cr8, ci8, ar8, ai8, reverse)


def _s5_specs():
    chan = pl.BlockSpec((SEQ, CH_W), lambda c, d: (0, c))
    chan2 = pl.BlockSpec((None, SEQ, CH_W), lambda c, d: (d, 0, c))
    state = pl.BlockSpec((None, SEQ, ST_W), lambda c, d: (d, 0, c))
    bmat = pl.BlockSpec((None, None, CH_W, ST_W), lambda c, d: (d, c, 0, 0))
    cmat = pl.BlockSpec((None, None, ST_W, CH_W), lambda c, d: (d, c, 0, 0))
    avec = pl.BlockSpec((None, None, 1, ST_W), lambda c, d: (d, c, 0, 0))
    return chan, chan2, state, bmat, cmat, avec


def _scan_by_direction(xr_ref, xi_ref, ar, ai, d, adjoint):
    @pl.when(d == 0)
    def _():
        _scan(xr_ref, xi_ref, ar, ai, reverse=adjoint)

    @pl.when(d == 1)
    def _():
        _scan(xr_ref, xi_ref, ar, ai, reverse=not adjoint)


def _s5_scan_fwd(u, bre, bim, are, aim, cre, cim):
    def body(u_ref, bre_ref, bim_ref, are_ref, aim_ref, cre_ref, cim_ref, sr_ref, si_ref, y_ref):
        ub = u_ref[...].astype(BF16)
        sr_ref[...] = _dot(ub, bre_ref[...])
        si_ref[...] = _dot(ub, bim_ref[...])
        _scan_by_direction(sr_ref, si_ref, are_ref[...], aim_ref[...], pl.program_id(1), adjoint=False)
        y_ref[...] = _dot(sr_ref[...].astype(BF16), cre_ref[...]) - _dot(si_ref[...].astype(BF16), cim_ref[...])

    chan, chan2, state, bmat, cmat, avec = _s5_specs()
    return pl.pallas_call(
        body, grid=(S5_CHUNKS, 2), in_specs=[chan, bmat, bmat, avec, avec, cmat, cmat], out_specs=[state, state, chan2],
        out_shape=[SDS((2, SEQ, S5_GROUPS * S5_STATE), F32)] * 2 + [SDS((2, SEQ, S5_WIDTH), F32)],
        name="s5_scan_fwd", compiler_params=_params(("parallel", "parallel")))(u, bre, bim, are, aim, cre, cim)


def _dlam(gr_ref, gi_ref, sr_ref, si_ref, reverse):
    tile = lambda i: pl.ds(pl.multiple_of(i * SCAN_BLOCKS, SCAN_BLOCKS), SCAN_BLOCKS)
    row = lax.broadcasted_iota(jnp.int32, (SCAN_BLOCKS, ST_W), 0)
    if reverse:
        edge, src, shift, empty, lo, hi, dprev = SCAN_T - 1, 0, SCAN_BLOCKS - 1, SCAN_BLOCKS - 1, 0, SCAN_T - 1, 1
    else:
        edge, src, shift, empty, lo, hi, dprev = 0, SCAN_T - 1, 1, 0, 1, SCAN_T, -1
    spr = jnp.where(row == empty, 0.0, pltpu.roll(sr_ref[tile(src), :], shift, 0))
    spi = jnp.where(row == empty, 0.0, pltpu.roll(si_ref[tile(src), :], shift, 0))
    acc0 = _cmul(gr_ref[tile(edge), :], gi_ref[tile(edge), :], spr, -spi)

    def step(i, carry):
        accr, acci = carry
        pr, pi = _cmul(gr_ref[tile(i), :], gi_ref[tile(i), :], sr_ref[tile(i + dprev), :], -si_ref[tile(i + dprev), :])
        return accr + pr, acci + pi

    accr, acci = lax.fori_loop(lo, hi, step, acc0)
    return jnp.sum(accr, axis=0, keepdims=True), jnp.sum(acci, axis=0, keepdims=True)


def _s5_scan_bwd(dy, du_skip, u, sr, si, bre, bim, are, aim, cre, cim):
    def body(dy_ref, dus_ref, u_ref, sr_ref, si_ref, bre_ref, bim_ref, are_ref, aim_ref, cre_ref, cim_ref,
             du_ref, dbr_ref, dbi_ref, dcr_ref, dci_ref, dar_ref, dai_ref, gr_ref, gi_ref):
        d = pl.program_id(1)
        dyb = dy_ref[...].astype(BF16)
        gr_ref[...] = _dot_nt(dyb, cre_ref[...])
        gi_ref[...] = -_dot_nt(dyb, cim_ref[...])
        dcr_ref[...] = _dot_tn(sr_ref[...].astype(BF16), dyb)
        dci_ref[...] = -_dot_tn(si_ref[...].astype(BF16), dyb)
        _scan_by_direction(gr_ref, gi_ref, are_ref[...], -aim_ref[...], d, adjoint=True)

        @pl.when(d == 0)
        def _():
            dar_ref[...], dai_ref[...] = _dlam(gr_ref, gi_ref, sr_ref, si_ref, reverse=False)
            du_ref[...] = dus_ref[...]

        @pl.when(d == 1)
        def _():
            dar_ref[...], dai_ref[...] = _dlam(gr_ref, gi_ref, sr_ref, si_ref, reverse=True)

        grb = gr_ref[...].astype(BF16)
        gib = gi_ref[...].astype(BF16)
        du_ref[...] += _dot_nt(grb, bre_ref[...]) + _dot_nt(gib, bim_ref[...])
        ub = u_ref[...].astype(BF16)
        dbr_ref[...] = _dot_tn(ub, grb)
        dbi_ref[...] = _dot_tn(ub, gib)

    chan, _, state, bmat, cmat, avec = _s5_specs()
    return pl.pallas_call(
        body, grid=(S5_CHUNKS, 2), in_specs=[chan, chan, chan, state, state, bmat, bmat, avec, avec, cmat, cmat],
        out_specs=[chan, bmat, bmat, cmat, cmat, avec, avec],
        out_shape=[SDS((SEQ, S5_WIDTH), F32)] + [SDS((2, S5_CHUNKS, CH_W, ST_W), F32)] * 2
                  + [SDS((2, S5_CHUNKS, ST_W, CH_W), F32)] * 2 + [SDS((2, S5_CHUNKS, 1, ST_W), F32)] * 2,
        scratch_shapes=[pltpu.VMEM((SEQ, ST_W), F32), pltpu.VMEM((SEQ, ST_W), F32)],
        name="s5_scan_bwd", compiler_params=_params(("parallel", "arbitrary")))(dy, du_skip, u, sr, si, bre, bim, are, aim, cre, cim)


_GELU_K = math.sqrt(2.0 / math.pi)
_GELU_C = 0.044715


def _gelu(x):
    t = jnp.tanh(_GELU_K * (x + _GELU_C * x * x * x))
    return 0.5 * x * (1.0 + t), t


def _s5_glu_fwd(u, y2, dskip, wglu, bglu):
    def body(u_ref, y0_ref, y1_ref, d_ref, w_ref, b_ref, o_ref, yp_ref):
        ypre = u_ref[...] * d_ref[...] + y0_ref[...] + y1_ref[...]
        yp_ref[...] = ypre
        y, _ = _gelu(ypre)
        z = _dot(y.astype(BF16), w_ref[...]) + b_ref[...]
        o_ref[...] = y * jax.nn.sigmoid(z)

    row = _row_spec(S5_WIDTH)
    vec = _fix_spec((1, S5_WIDTH))
    dir0 = pl.BlockSpec((None, ROW_TILE, S5_WIDTH), lambda i: (0, i, 0))
    dir1 = pl.BlockSpec((None, ROW_TILE, S5_WIDTH), lambda i: (1, i, 0))
    return pl.pallas_call(
        body, grid=(N_ROW_TILES,), in_specs=[row, dir0, dir1, vec, _fix_spec((S5_WIDTH, S5_WIDTH)), vec],
        out_specs=[row, row], out_shape=[SDS((SEQ, S5_WIDTH), F32)] * 2, name="s5_glu_fwd",
        compiler_params=_params(("parallel",)))(u, y2, y2, dskip, wglu, bglu)


def _s5_glu_bwd(do, ypre, u, dskip, wglu, bglu):
    def body(do_ref, yp_ref, u_ref, d_ref, w_ref, b_ref, dyp_ref, du_ref, dw_ref, db_ref, dd_ref):
        i = pl.program_id(0)
        ypre = yp_ref[...]
        y, t = _gelu(ypre)
        yb = y.astype(BF16)
        sg = jax.nn.sigmoid(_dot(yb, w_ref[...]) + b_ref[...])
        dov = do_ref[...]
        dz = dov * y * sg * (1.0 - sg)
        dzb = dz.astype(BF16)
        dy = dov * sg + _dot_nt(dzb, w_ref[...])
        dgelu = 0.5 * (1.0 + t) + 0.5 * ypre * (1.0 - t * t) * _GELU_K * (1.0 + 3.0 * _GELU_C * ypre * ypre)
        dyp = dy * dgelu
        dyp_ref[...] = dyp
        uv = u_ref[...]
        du_ref[...] = dyp * d_ref[...]

        @pl.when(i == 0)
        def _():
            dw_ref[...] = jnp.zeros_like(dw_ref)
            db_ref[...] = jnp.zeros_like(db_ref)
            dd_ref[...] = jnp.zeros_like(dd_ref)

        dw_ref[...] += _dot_tn(yb, dzb)
        db_ref[...] += jnp.sum(dz, axis=0, keepdims=True)
        dd_ref[...] += jnp.sum(dyp * uv, axis=0, keepdims=True)

    row = _row_spec(S5_WIDTH)
    vec = _fix_spec((1, S5_WIDTH))
    mat = _fix_spec((S5_WIDTH, S5_WIDTH))
    return pl.pallas_call(
        body, grid=(N_ROW_TILES,), in_specs=[row, row, row, vec, mat, vec], out_specs=[row, row, mat, vec, vec],
        out_shape=[SDS((SEQ, S5_WIDTH), F32)] * 2 + [SDS((S5_WIDTH, S5_WIDTH), F32), SDS((1, S5_WIDTH), F32), SDS((1, S5_WIDTH), F32)],
        name="s5_glu_bwd", compiler_params=_params(("arbitrary",)))(do, ypre, u, dskip, wglu, bglu)


def _mix_out_fwd(ona, os5, g_na, g_s5, wout):
    def body(a_ref, s_ref, ga_ref, gs_ref, w_ref, o_ref):
        av, sv = a_ref[...], s_ref[...]
        ca = (av * _rstd(av) * ga_ref[...]).astype(BF16)
        cs = (sv * _rstd(sv) * gs_ref[...]).astype(BF16)
        o_ref[...] = _dot(ca, w_ref[0:NA_WIDTH, :]) + _dot(cs, w_ref[NA_WIDTH:, :])

    row = _row_spec(NA_WIDTH)
    vec = _fix_spec((1, NA_WIDTH))
    return pl.pallas_call(
        body, grid=(N_ROW_TILES,), in_specs=[row, row, vec, vec, _fix_spec((D_MODEL, D_MODEL))],
        out_specs=_row_spec(D_MODEL), out_shape=SDS((SEQ, D_MODEL), F32), name="mix_out_fwd",
        compiler_params=_params(("parallel",)))(ona, os5, g_na, g_s5, wout)


def _mix_out_bwd(dmix, ona, os5, g_na, g_s5, wout):
    def body(dm_ref, a_ref, s_ref, ga_ref, gs_ref, w_ref, da_ref, ds_ref, dw_ref, dga_ref, dgs_ref):
        i = pl.program_id(0)
        dm = dm_ref[...]
        av, sv = a_ref[...], s_ref[...]
        ra, rs = _rstd(av), _rstd(sv)
        ga, gs = ga_ref[...], gs_ref[...]
        ca = (av * ra * ga).astype(BF16)
        cs = (sv * rs * gs).astype(BF16)
        dca = _dot_nt(dm, w_ref[0:NA_WIDTH, :])
        dcs = _dot_nt(dm, w_ref[NA_WIDTH:, :])
        da, dga = _rms_bwd(av, ra, ga, dca)
        ds, dgs = _rms_bwd(sv, rs, gs, dcs)
        da_ref[...] = da
        ds_ref[...] = ds

        @pl.when(i == 0)
        def _():
            dw_ref[...] = jnp.zeros_like(dw_ref)
            dga_ref[...] = jnp.zeros_like(dga_ref)
            dgs_ref[...] = jnp.zeros_like(dgs_ref)

        dw_ref[0:NA_WIDTH, :] += _dot_tn(ca, dm)
        dw_ref[NA_WIDTH:, :] += _dot_tn(cs, dm)
        dga_ref[...] += jnp.sum(dga, axis=0, keepdims=True)
        dgs_ref[...] += jnp.sum(dgs, axis=0, keepdims=True)

    row = _row_spec(NA_WIDTH)
    vec = _fix_spec((1, NA_WIDTH))
    mat = _fix_spec((D_MODEL, D_MODEL))
    return pl.pallas_call(
        body, grid=(N_ROW_TILES,), in_specs=[_row_spec(D_MODEL), row, row, vec, vec, mat],
        out_specs=[row, row, mat, vec, vec],
        out_shape=[SDS((SEQ, NA_WIDTH), F32)] * 2 + [SDS((D_MODEL, D_MODEL), F32), SDS((1, NA_WIDTH), F32), SDS((1, NA_WIDTH), F32)],
        name="mix_out_bwd", compiler_params=_params(("arbitrary",)))(dmix, ona, os5, g_na, g_s5, wout)


def _me():
    x, y, c = lax.axis_index("x"), lax.axis_index("y"), lax.axis_index("c")
    return x, y, c, 4 * x + 2 * y + c


def _peer(k):
    x, y, c, _ = _me()
    px = 1 - x if (k >> 2) & 1 else x
    py = 1 - y if (k >> 1) & 1 else y
    pc = 1 - c if k & 1 else c
    return (px, py, pc), 4 * px + 2 * py + pc


def _exchange(arrays, gather, name, after=()):
    n, n_after = len(arrays), len(after)

    def body(*refs):
        ins, outs = refs[:n], refs[n + n_after:2 * n + n_after]
        send_sems, recv_sems, local_sems = refs[2 * n + n_after:]
        _, _, _, me = _me()
        started = []
        for a in range(n):
            src_mine = ins[a] if gather else ins[a].at[me]
            local = pltpu.make_async_copy(src_mine, outs[a].at[me], local_sems.at[a])
            local.start()
            started.append(local)
        sends = []
        for k in range(1, N_DEV):
            peer, peer_idx = _peer(k)
            for a in range(n):
                src = ins[a] if gather else ins[a].at[peer_idx]
                cp = pltpu.make_async_remote_copy(src_ref=src, dst_ref=outs[a].at[me], send_sem=send_sems.at[a, k - 1],
                                                  recv_sem=recv_sems.at[a, k - 1], device_id=peer, device_id_type=MESH)
                cp.start()
                sends.append(cp)
        for k in range(1, N_DEV):
            peer, peer_idx = _peer(k)
            for a in range(n):
                src = ins[a] if gather else ins[a].at[peer_idx]
                pltpu.make_async_remote_copy(src_ref=src, dst_ref=outs[a].at[peer_idx], send_sem=send_sems.at[a, k - 1],
                                             recv_sem=recv_sems.at[a, k - 1], device_id=peer, device_id_type=MESH).wait_recv()
        for cp in sends:
            cp.wait_send()
        for local in started:
            local.wait()

    hbm = pl.BlockSpec(memory_space=pltpu.HBM)
    out_shape = [SDS((N_DEV,) + tuple(a.shape), a.dtype) if gather else SDS(a.shape, a.dtype) for a in arrays]
    return pl.pallas_call(
        body, in_specs=[hbm] * n + [pl.BlockSpec(memory_space=pl.ANY)] * n_after, out_specs=[hbm] * n, out_shape=out_shape,
        scratch_shapes=[pltpu.SemaphoreType.DMA((n, N_DEV - 1)), pltpu.SemaphoreType.DMA((n, N_DEV - 1)),
                        pltpu.SemaphoreType.DMA((n,))],
        name=name)(*arrays, *after)


_HBM = pl.BlockSpec(memory_space=pltpu.HBM)
_SEM = pl.BlockSpec(memory_space=pltpu.SEMAPHORE)
_EFFECT = pltpu.SideEffectType.DATAFLOW_SIDE_EFFECTING


def _land_shape(a, gather):
    return (N_DEV,) + tuple(a.shape) if gather else tuple(a.shape)


def _place_own(arrays, gather, name):
    n = len(arrays)
    _, _, _, me = _me()

    def body(me_ref, *refs):
        for a in range(n):
            refs[n + a][...] = refs[a][...]

    def own_slot(a):
        zeros = (0,) * (a.ndim - (0 if gather else 1))
        return lambda i, me_ref: (me_ref[0],) + zeros

    def whole(a):
        return lambda i, me_ref: (0,) * a.ndim

    in_specs = [pl.BlockSpec(a.shape, whole(a)) if gather else pl.BlockSpec((None,) + a.shape[1:], own_slot(a)) for a in arrays]
    out_specs = [pl.BlockSpec((None,) + (a.shape if gather else a.shape[1:]), own_slot(a)) for a in arrays]
    return pl.pallas_call(
        body, grid_spec=pltpu.PrefetchScalarGridSpec(num_scalar_prefetch=1, grid=(1,), in_specs=in_specs, out_specs=out_specs),
        out_shape=[SDS(_land_shape(a, gather), a.dtype) for a in arrays], name=name,
        compiler_params=_params(("arbitrary",)))(me.reshape(1).astype(jnp.int32), *arrays)


def _exchange_start(arrays, lands, gather, name):
    n = len(arrays)

    def body(*refs):
        ins, lnd = refs[:n], refs[n:2 * n]
        send_sems, recv_sems = refs[2 * n], refs[2 * n + 1]
        token = refs[-1]
        _, _, _, me = _me()
        for k in range(1, N_DEV):
            peer, peer_idx = _peer(k)
            for a in range(n):
                src = ins[a] if gather else ins[a].at[peer_idx]
                s = a * (N_DEV - 1) + k - 1
                pltpu.make_async_remote_copy(src_ref=src, dst_ref=lnd[a].at[me], send_sem=send_sems.at[s],
                                             recv_sem=recv_sems.at[s], device_id=peer, device_id_type=MESH).start()
        token[...] = jnp.zeros_like(token)

    sems = pltpu.SemaphoreType.DMA((n * (N_DEV - 1),))
    out = pl.pallas_call(
        body, name=name, in_specs=[_HBM] * (2 * n),
        out_shape=(sems, sems) + tuple(pltpu.HBM(a.shape, a.dtype) for a in list(arrays) + list(lands)) + (SDS((8, 128), F32),),
        out_specs=(_SEM, _SEM) + (_HBM,) * (2 * n) + (pl.BlockSpec(memory_space=pltpu.VMEM),),
        input_output_aliases={i: 2 + i for i in range(2 * n)},
        compiler_params=pltpu.CompilerParams(has_side_effects=_EFFECT),
    )(*[pltpu.with_memory_space_constraint(a, pltpu.HBM) for a in list(arrays) + list(lands)])
    return out[0], out[1], list(out[2:2 + n]), list(out[2 + n:2 + 2 * n]), out[-1]


def _exchange_wait(send_sems, recv_sems, arrays, lands, after, gather, name):
    n = len(arrays)

    def body(*refs):
        ins, lnd = refs[:n], refs[n:2 * n]
        send_sems, recv_sems = refs[2 * n], refs[2 * n + 1]
        for k in range(1, N_DEV):
            peer, peer_idx = _peer(k)
            for a in range(n):
                src = ins[a] if gather else ins[a].at[peer_idx]
                s = a * (N_DEV - 1) + k - 1
                cp = pltpu.make_async_remote_copy(src_ref=src, dst_ref=lnd[a].at[peer_idx], send_sem=send_sems.at[s],
                                                  recv_sem=recv_sems.at[s], device_id=peer, device_id_type=MESH)
                cp.wait_send()
                cp.wait_recv()

    after = list(after) if isinstance(after, (list, tuple)) else [after]
    out = pl.pallas_call(
        body, name=name, in_specs=[_HBM] * (2 * n) + [_SEM, _SEM] + [pl.BlockSpec(memory_space=pl.ANY)] * len(after),
        out_shape=tuple(pltpu.HBM(a.shape, a.dtype) for a in list(arrays) + list(lands)),
        out_specs=(_HBM,) * (2 * n), input_output_aliases={i: i for i in range(2 * n)},
        compiler_params=pltpu.CompilerParams(has_side_effects=_EFFECT),
    )(*arrays, *lands, send_sems, recv_sems, *after)
    return list(out[n:])


def _adamw_math(w, g, m, v):
    m = ADAM_B1 * m + (1.0 - ADAM_B1) * g
    v = ADAM_B2 * v + (1.0 - ADAM_B2) * (g * g)
    m_hat = m / (1.0 - ADAM_B1 ** ADAM_STEP)
    v_hat = v / (1.0 - ADAM_B2 ** ADAM_STEP)
    delta = -ADAM_LR * (m_hat / (jnp.sqrt(v_hat) + ADAM_EPS) + ADAM_WD * w)
    return delta, m, v


def _adamw(w, m, v, pieces, name):
    rows, cols = w.shape
    tile = rows
    for cand in (256, 176, 128, 64, 16):
        if rows > cand and rows % cand == 0:
            tile = cand
            break

    def body(w_ref, m_ref, v_ref, p_ref, g_ref, d_ref, mo_ref, vo_ref):
        g = p_ref[0].astype(F32)
        for p in range(1, N_DEV):
            g = g + p_ref[p].astype(F32)
        g_ref[...] = g
        d_ref[...], mo_ref[...], vo_ref[...] = _adamw_math(w_ref[...], g, m_ref[...], v_ref[...])

    blk = pl.BlockSpec((tile, cols), lambda i: (i, 0))
    return pl.pallas_call(
        body, grid=(rows // tile,), in_specs=[blk, blk, blk, pl.BlockSpec((N_DEV, tile, cols), lambda i: (0, i, 0))],
        out_specs=[blk] * 4, out_shape=[SDS((rows, cols), F32)] * 4, name=name,
        compiler_params=_params(("parallel",)))(w, m, v, pieces)


def _sum_pieces(p_ref):
    g = p_ref[0].astype(F32)
    for p in range(1, N_DEV):
        g = g + p_ref[p].astype(F32)
    return g


def _adamw_lead(w, m, v, g, tile, name):
    lead, b, c = w.shape

    def body(w_ref, m_ref, v_ref, g_ref, d_ref, mo_ref, vo_ref):
        d_ref[...], mo_ref[...], vo_ref[...] = _adamw_math(w_ref[...], g_ref[...], m_ref[...], v_ref[...])

    blk = pl.BlockSpec((tile, b, c), lambda i: (i, 0, 0))
    return pl.pallas_call(
        body, grid=(lead // tile,), in_specs=[blk] * 4, out_specs=[blk] * 3, out_shape=[SDS(w.shape, F32)] * 3, name=name,
        compiler_params=_params(("parallel",)))(w, m, v, g)


VEC_ROWS = ['ffn1_pre_g', 'ffn1_post_g', 'mix_pre_g', 'mix_post_g', 'ffn2_pre_g', 'ffn2_post_g', 'final_g',
            ('na_out_g', 's5_out_g'), ('s5_d', 's5_b_glu')]
VEC_NAMES = [n for row in VEC_ROWS for n in ((row,) if isinstance(row, str) else row)]
VEC_PACK_ROWS = 16


def _pack_vectors(grads):
    def body(*refs):
        o_ref = refs[-1]
        o_ref[...] = jnp.zeros_like(o_ref)
        k = 0
        for i, row in enumerate(VEC_ROWS):
            if isinstance(row, str):
                o_ref[i:i + 1, :] = refs[k][...]
                k += 1
            else:
                o_ref[i:i + 1, 0:NA_WIDTH] = refs[k][...]
                o_ref[i:i + 1, NA_WIDTH:] = refs[k + 1][...]
                k += 2

    return pl.pallas_call(body, out_shape=SDS((VEC_PACK_ROWS, D_MODEL), F32), name="pack_vectors",
                          compiler_params=_params())(*[grads[n] for n in VEC_NAMES])


def _sum8(pieces, name):
    def body(p_ref, o_ref):
        o_ref[...] = _sum_pieces(p_ref)

    return pl.pallas_call(body, out_shape=SDS(pieces.shape[1:], F32), name=name, compiler_params=_params())(pieces)


def _adamw_small(packed8, vec_wmv, others):
    n_vec, n_oth = len(VEC_NAMES), len(others)

    def body(*refs):
        p_ref = refs[0]
        ins = refs[1:1 + 3 * n_vec + 4 * n_oth]
        outs = refs[1 + 3 * n_vec + 4 * n_oth:]
        gsum = _sum_pieces(p_ref)
        k = 0
        for i, row in enumerate(VEC_ROWS):
            parts = [(row, gsum[i:i + 1, :])] if isinstance(row, str) else \
                [(row[0], gsum[i:i + 1, 0:NA_WIDTH]), (row[1], gsum[i:i + 1, NA_WIDTH:])]
            for _, g in parts:
                w_ref, m_ref, v_ref = ins[3 * k:3 * k + 3]
                outs[4 * k][...] = g
                outs[4 * k + 1][...], outs[4 * k + 2][...], outs[4 * k + 3][...] = _adamw_math(w_ref[...], g, m_ref[...], v_ref[...])
                k += 1
        for j in range(n_oth):
            w_ref, m_ref, v_ref, g_ref = ins[3 * n_vec + 4 * j:3 * n_vec + 4 * j + 4]
            g = _sum_pieces(g_ref)
            g = g[tuple(slice(0, s) for s in w_ref.shape)]
            o = outs[4 * (n_vec + j):4 * (n_vec + j) + 4]
            o[0][...] = g
            o[1][...], o[2][...], o[3][...] = _adamw_math(w_ref[...], g, m_ref[...], v_ref[...])

    args, out_shape = [packed8], []
    for w, m, v in vec_wmv:
        args += [w, m, v]
        out_shape += [SDS(w.shape, F32)] * 4
    for w, m, v, g in others:
        args += [w, m, v, g]
        out_shape += [SDS(w.shape, F32)] * 4
    return pl.pallas_call(body, out_shape=out_shape, name="adamw_small", compiler_params=_params())(*args)


def _perm_rows(x):
    return x.reshape(SCAN_BLOCKS, SCAN_T, x.shape[-1]).transpose(1, 0, 2).reshape(SEQ, x.shape[-1])


def _unperm_rows(x):
    return x.reshape(SCAN_T, SCAN_BLOCKS, x.shape[-1]).transpose(1, 0, 2).reshape(SEQ, x.shape[-1])


def _block_diag(x):
    eye = np.eye(8, dtype=bool)[None, None, :, None, :, None]
    full = jnp.where(eye, x[:, :, :, :, None, :], 0.0)
    return full.reshape(2, S5_CHUNKS, 8 * x.shape[3], 8 * x.shape[4])


def _diag_blocks(x, r, c):
    x6 = x.reshape(2, S5_CHUNKS, 8, r, 8, c)
    return jnp.stack([x6[:, :, g, :, g, :] for g in range(8)], axis=2)


def _dep(x, token):
    return x if token is None else x + token


def _local_step(x, target, get_w, small, emit):
    bias = _rpb_expand(small["na_rpb"][0])
    lr = small["s5_lam_re"].reshape(64, S5_STATE)
    li = small["s5_lam_im"].reshape(64, S5_STATE)
    logdt = small["s5_log_dt"].reshape(64, 1)
    b_t = [small[n].reshape(64, S5_STATE, S5_GROUP).transpose(0, 2, 1) for n in ("s5_b_re", "s5_b_im")]
    lbr, lbi, bbr, bbi = _s5_prep(lr, li, logdt, b_t[0], b_t[1])
    are = lbr.reshape(2, S5_CHUNKS, 1, ST_W)
    aim = lbi.reshape(2, S5_CHUNKS, 1, ST_W)
    bre = _block_diag(bbr.reshape(2, S5_CHUNKS, 8, S5_GROUP, S5_STATE)).astype(BF16)
    bim = _block_diag(bbi.reshape(2, S5_CHUNKS, 8, S5_GROUP, S5_STATE)).astype(BF16)
    c_t = [small[n].reshape(2, S5_CHUNKS, 8, S5_GROUP, S5_STATE).transpose(0, 1, 2, 4, 3) for n in ("s5_c_re", "s5_c_im")]
    cre = _block_diag(c_t[0]).astype(BF16)
    cim = _block_diag(c_t[1]).astype(BF16)
    tgt = jnp.concatenate([jnp.zeros((N_META, D_MODEL), F32), target], axis=0)

    wts = dict(get_w("ffn1", [bias, are, aim, bre, bim, cre, cim, tgt]))
    h0 = jnp.concatenate([wts["meta_tokens"], x], axis=0)
    a1 = _prenorm(h0, small["ffn1_pre_g"])
    gate1, up1, f1 = _ffn_fwd(a1, wts["ffn1_w_gate"], wts["ffn1_w_up"], wts["ffn1_w_down"], "ffn1_fwd")
    h1, a2 = _post_pre(f1, h0, small["ffn1_post_g"], small["mix_pre_g"], 0.5, "post_pre1")
    wts.update(get_w("w_in", a2))
    proj = _proj_fwd(a2, wts["w_in"])
    qkv = proj[:6].reshape(3, 2, SEQ, 4, HEAD_DIM).transpose(0, 1, 3, 2, 4).reshape(3, HEADS, SEQ, HEAD_DIM)
    u = proj[6:].transpose(1, 0, 2).reshape(SEQ, S5_WIDTH)
    o3 = _na_fwd(qkv[0], qkv[1], qkv[2], bias)
    ona = o3.transpose(1, 0, 2).reshape(SEQ, NA_WIDTH)
    u_p = _perm_rows(u)
    sr, si, y2 = _s5_scan_fwd(u_p, bre, bim, are, aim, cre, cim)
    wts.update(get_w("mix", y2))
    os5_p, ypre_p = _s5_glu_fwd(u_p, y2, small["s5_d"], wts["s5_w_glu"], small["s5_b_glu"])
    os5 = _unperm_rows(os5_p)

    mix = _mix_out_fwd(ona, os5, small["na_out_g"], small["s5_out_g"], wts["w_out"])
    h2, a3 = _post_pre(mix, h1, small["mix_post_g"], small["ffn2_pre_g"], 1.0, "post_pre2")
    wts.update(get_w("ffn2", a3))
    gate2, up2, f2 = _ffn_fwd(a3, wts["ffn2_w_gate"], wts["ffn2_w_up"], wts["ffn2_w_down"], "ffn2_fwd")
    loss8, dh3, df2, g_final, g_ffn2_post = _final_loss(f2, h2, small["ffn2_post_g"], small["final_g"], tgt)

    da3, dwg2, dwu2, dwd2 = _ffn_bwd(df2, a3, gate2, up2, wts["ffn2_w_gate"], wts["ffn2_w_up"], wts["ffn2_w_down"], "ffn2_bwd")
    tok = emit("ffn2", {"ffn2_w_gate": dwg2, "ffn2_w_up": dwu2, "ffn2_w_down": dwd2})
    dh2, dmix, g_ffn2_pre, g_mix_post = _bwd_pre_post(da3, h2, _dep(small["ffn2_pre_g"], tok), dh3, mix, small["mix_post_g"], 1.0,
                                                      "bwd_pre_post2")
    dona, dos5, dwout, g_na_out, g_s5_out = _mix_out_bwd(dmix, ona, os5, small["na_out_g"], small["s5_out_g"], wts["w_out"])

    dypre_p, du_skip_p, dwglu, g_b_glu, g_s5_d = _s5_glu_bwd(_perm_rows(dos5), ypre_p, u_p, small["s5_d"], wts["s5_w_glu"],
                                                             small["s5_b_glu"])
    tok = emit("mix", {"s5_w_glu": dwglu.reshape(N_DEV, S5_WIDTH // N_DEV, S5_WIDTH).astype(BF16),
                       "w_out": dwout.reshape(N_DEV, D_MODEL // N_DEV, D_MODEL).astype(BF16)})
    du_p, dbr, dbi, dcr, dci, dar, dai = _s5_scan_bwd(dypre_p, du_skip_p, u_p, sr, si, bre, bim, _dep(are, tok), aim, cre, cim)
    du = _unperm_rows(du_p)
    dbbr = _diag_blocks(dbr, S5_GROUP, S5_STATE).reshape(64, S5_GROUP, S5_STATE)
    dbbi = _diag_blocks(dbi, S5_GROUP, S5_STATE).reshape(64, S5_GROUP, S5_STATE)
    g_lr, g_li, g_dt, g_br, g_bi = _s5_prep_bwd(lr, li, logdt, b_t[0], b_t[1], dar.reshape(64, S5_STATE),
                                                dai.reshape(64, S5_STATE), dbbr, dbbi)
    g_c = [_diag_blocks(d, S5_STATE, S5_GROUP).transpose(0, 1, 2, 4, 3).reshape(2 * S5_GROUPS, S5_GROUP, S5_STATE)
           for d in (dcr, dci)]

    do3 = dona.reshape(SEQ, HEADS, HEAD_DIM).transpose(1, 0, 2)
    dq, dk, dv, dbias = _na_bwd(qkv[0], qkv[1], qkv[2], bias, do3)
    g_rpb = _rpb_reduce(dbias)
    dense = jnp.stack([g.reshape(2 * S5_GROUPS, S5_STATE * S5_GROUP) for g in (g_br.transpose(0, 2, 1), g_bi.transpose(0, 2, 1), *g_c)])
    tok = emit("small", {"dense": dense, "na_rpb": g_rpb,
                         "s5_lam_re": g_lr.reshape(2, S5_GROUPS, S5_STATE), "s5_lam_im": g_li.reshape(2, S5_GROUPS, S5_STATE),
                         "s5_log_dt": g_dt.reshape(2, S5_GROUPS)})
    dqkv = jnp.stack([dq, dk, dv]).reshape(3, 2, 4, SEQ, HEAD_DIM).transpose(0, 1, 3, 2, 4).reshape(6, SEQ, IN_SHARD)
    dproj = jnp.concatenate([dqkv, du.reshape(SEQ, 2, IN_SHARD).transpose(1, 0, 2)], axis=0).astype(BF16)
    da2, dwin = _proj_bwd(dproj, a2, wts["w_in"])
    tok2 = emit("w_in", {"w_in": dwin})
    tok = tok if tok2 is None else tok + tok2
    dh1, df1, g_mix_pre, g_ffn1_post = _bwd_pre_post(da2, h1, _dep(small["mix_pre_g"], tok), dh2, f1, small["ffn1_post_g"], 0.5,
                                                     "bwd_pre_post1")
    da1, dwg1, dwu1, dwd1 = _ffn_bwd(df1, a1, gate1, up1, wts["ffn1_w_gate"], wts["ffn1_w_up"], wts["ffn1_w_down"], "ffn1_bwd")
    emit("ffn1", {"ffn1_w_gate": dwg1, "ffn1_w_up": dwu1, "ffn1_w_down": dwd1})
    dh0, g_ffn1_pre = _bwd_pre_only(da1, h0, small["ffn1_pre_g"], dh1)

    vec_g = {
        "ffn1_pre_g": g_ffn1_pre, "ffn1_post_g": g_ffn1_post, "mix_pre_g": g_mix_pre, "s5_d": g_s5_d, "s5_b_glu": g_b_glu,
        "na_out_g": g_na_out, "s5_out_g": g_s5_out, "mix_post_g": g_mix_post,
        "ffn2_pre_g": g_ffn2_pre, "ffn2_post_g": g_ffn2_post, "final_g": g_final,
    }
    return loss8[0, 0], dh0[N_META:], dh0[:N_META], vec_g


WEIGHT_NAMES = ['meta_tokens', 'ffn1_pre_g', 'ffn1_post_g', 'ffn1_w_gate', 'ffn1_w_up', 'ffn1_w_down', 'mix_pre_g', 'w_in',
                'na_rpb', 's5_lam_re', 's5_lam_im', 's5_log_dt', 's5_b_re', 's5_b_im', 's5_c_re', 's5_c_im', 's5_d',
                's5_w_glu', 's5_b_glu', 'na_out_g', 's5_out_g', 'w_out', 'mix_post_g', 'ffn2_pre_g', 'ffn2_post_g',
                'ffn2_w_gate', 'ffn2_w_up', 'ffn2_w_down', 'final_g']
BIG_NAMES = ['ffn1_w_gate', 'ffn1_w_up', 'ffn1_w_down', 'w_in', 's5_w_glu', 'w_out', 'ffn2_w_gate', 'ffn2_w_up', 'ffn2_w_down']
SMALL_NAMES = [n for n in WEIGHT_NAMES if n not in BIG_NAMES and n != 'meta_tokens']
WHOLE_NAMES = ['na_rpb', 's5_lam_re', 's5_lam_im', 's5_log_dt']
LEAD_NAMES = ['s5_b_re', 's5_b_im', 's5_c_re', 's5_c_im']


def kernel(x, meta_tokens, ffn1_pre_g, ffn1_post_g, ffn1_w_gate, ffn1_w_up, ffn1_w_down, mix_pre_g, w_in, na_rpb, s5_lam_re, s5_lam_im, s5_log_dt, s5_b_re, s5_b_im, s5_c_re, s5_c_im, s5_d, s5_w_glu, s5_b_glu, na_out_g, s5_out_g, w_out, mix_post_g, ffn2_pre_g, ffn2_post_g, ffn2_w_gate, ffn2_w_up, ffn2_w_down, final_g, loss_target, m_meta_tokens, m_ffn1_pre_g, m_ffn1_post_g, m_ffn1_w_gate, m_ffn1_w_up, m_ffn1_w_down, m_mix_pre_g, m_w_in, m_na_rpb, m_s5_lam_re, m_s5_lam_im, m_s5_log_dt, m_s5_b_re, m_s5_b_im, m_s5_c_re, m_s5_c_im, m_s5_d, m_s5_w_glu, m_s5_b_glu, m_na_out_g, m_s5_out_g, m_w_out, m_mix_post_g, m_ffn2_pre_g, m_ffn2_post_g, m_ffn2_w_gate, m_ffn2_w_up, m_ffn2_w_down, m_final_g, v_meta_tokens, v_ffn1_pre_g, v_ffn1_post_g, v_ffn1_w_gate, v_ffn1_w_up, v_ffn1_w_down, v_mix_pre_g, v_w_in, v_na_rpb, v_s5_lam_re, v_s5_lam_im, v_s5_log_dt, v_s5_b_re, v_s5_b_im, v_s5_c_re, v_s5_c_im, v_s5_d, v_s5_w_glu, v_s5_b_glu, v_na_out_g, v_s5_out_g, v_w_out, v_mix_post_g, v_ffn2_pre_g, v_ffn2_post_g, v_ffn2_w_gate, v_ffn2_w_up, v_ffn2_w_down, v_final_g):
    args = dict(locals())
    w = {n: args[n] for n in WEIGHT_NAMES}
    m = {n: args["m_" + n] for n in WEIGHT_NAMES}
    v = {n: args["v_" + n] for n in WEIGHT_NAMES}

    small = {n: w[n] for n in SMALL_NAMES}

    pending = {}

    def start(group, names, arrays, gather):
        lands = _place_own(arrays, gather, "own_" + group)
        send_sems, recv_sems, arrays, lands, token = _exchange_start(arrays, lands, gather, "start_" + group)
        pending[group] = (names, send_sems, recv_sems, arrays, lands, gather)
        return token

    def finish(group, after):
        names, send_sems, recv_sems, arrays, lands, gather = pending.pop(group)
        return dict(zip(names, _exchange_wait(send_sems, recv_sems, arrays, lands, after, gather, "wait_" + group)))

    first = ["ffn1_w_gate", "ffn1_w_up", "ffn1_w_down"]
    token1 = start("ffn1", first + ["meta_tokens"], [w[n][0].astype(BF16) for n in first] + [w["meta_tokens"]], True)
    tokens = [start(group, names, [(w[n][0] + token1[0, 0]).astype(BF16) for n in names], True)
              for group, names in (("w_in", ["w_in"]), ("mix", ["s5_w_glu", "w_out"]),
                                   ("ffn2", ["ffn2_w_gate", "ffn2_w_up", "ffn2_w_down"]))]

    def get_w(group, after):
        if group == "ffn1":
            got = finish(group, list(after) + tokens)
            got["meta_tokens"] = got["meta_tokens"].transpose(1, 0, 2).reshape(N_META, D_MODEL)
        else:
            got = finish(group, after)
        if group == "mix":
            got = {"s5_w_glu": got["s5_w_glu"].reshape(S5_WIDTH, S5_WIDTH), "w_out": got["w_out"].reshape(D_MODEL, D_MODEL)}
        return got

    def emit(group, grads):
        return start("g_" + group, list(grads), list(grads.values()), group == "small")[0, 0]

    loss_local, grad_x, gmeta, vec_g = _local_step(x[0], loss_target[0], get_w, small, emit)
    loss = lax.psum(loss_local, AXES)
    res = {}

    def update_shard(n, pieces):
        shape = w[n].shape
        w2 = w[n].reshape(shape[-2], shape[-1])
        outs = _adamw(w2, m[n].reshape(w2.shape), v[n].reshape(w2.shape), pieces, "adamw_" + n)
        res[n] = [o.reshape(shape) for o in outs]

    for group in ("g_ffn2", "g_mix", "g_w_in"):
        for n, pieces in finish(group, grad_x).items():
            update_shard(n, pieces)
    g8 = finish("g_small", grad_x)
    dense = _sum8(g8["dense"], "sum_dense")
    for i, n in enumerate(LEAD_NAMES):
        shape3 = (2 * S5_GROUPS,) + w[n].shape[-2:]
        g = dense[i].reshape(shape3)
        upd = _adamw_lead(w[n].reshape(shape3), m[n].reshape(shape3), v[n].reshape(shape3), g, 8, "adamw_" + n)
        res[n] = [o.reshape(w[n].shape) for o in [g] + list(upd)]

    done = [res[n][1] for n in ("ffn2_w_gate", "ffn2_w_up", "ffn2_w_down", "w_in", "w_out", "s5_w_glu") + tuple(LEAD_NAMES)]
    packed8, gmeta8 = _exchange([_pack_vectors(vec_g), gmeta], True, "gather_vectors", after=done)
    for n, pieces in finish("g_ffn1", packed8).items():
        update_shard(n, pieces)
    _, _, _, me = _me()
    update_shard("meta_tokens", lax.dynamic_slice_in_dim(gmeta8, me * (D_MODEL // N_DEV), D_MODEL // N_DEV, axis=2))

    outs = _adamw_small(packed8, [(w[n], m[n], v[n]) for n in VEC_NAMES], [(w[n][0], m[n][0], v[n][0], g8[n]) for n in WHOLE_NAMES])
    for i, n in enumerate(VEC_NAMES + WHOLE_NAMES):
        res[n] = [o.reshape(w[n].shape) for o in outs[4 * i:4 * i + 4]]

    out = [loss, grad_x[None]]
    for kind in range(4):
        out += [res[n][kind] for n in WEIGHT_NAMES]
    return tuple(out)
```

```python
import functools
import math

import numpy as np
import jax
import jax.numpy as jnp
from jax import lax
from jax.experimental import pallas as pl
from jax.experimental.pallas import tpu as pltpu

F32 = jnp.float32
BF16 = jnp.bfloat16
SDS = jax.ShapeDtypeStruct

D_MODEL = 1024
N_TOK = 2048
N_META = 16
SEQ = N_TOK + N_META
ROW_TILE = 688
N_ROW_TILES = SEQ // ROW_TILE
N_DEV = 8
D_FF = 2816
FF_SHARD = D_FF // N_DEV
IN_SHARD = 256
NA_WIDTH = 512
S5_WIDTH = 512
HEADS = 8
HEAD_DIM = 64
GRID_W = 64
GRID_ROWS = N_TOK // GRID_W
KH = 8
KW = 16
NA_RB = 4
NA_KR = KH + NA_RB - 1
NA_BLOCKS = GRID_ROWS // NA_RB
NA_QB = NA_RB * GRID_W
NA_KB = NA_KR * GRID_W
NA_TYPES = 3
S5_GROUPS = 32
S5_GROUP = 16
S5_STATE = 64
S5_CHUNKS = 4
CH_W = S5_WIDTH // S5_CHUNKS
ST_W = S5_GROUPS * S5_STATE // S5_CHUNKS
SCAN_BLOCKS = 8
SCAN_T = SEQ // SCAN_BLOCKS
RMS_EPS = 1e-6
NEG_INF = -1e30
ATT_SCALE = HEAD_DIM ** -0.5
ADAM_LR, ADAM_B1, ADAM_B2, ADAM_EPS, ADAM_WD, ADAM_STEP = 0.001, 0.9, 0.999, 1e-08, 0.01, 10
VMEM_LIMIT = 56 * 1024 * 1024
MESH = pl.DeviceIdType.MESH
AXES = ("x", "y", "c")


def _params(sem=None):
    return pltpu.CompilerParams(dimension_semantics=sem, vmem_limit_bytes=VMEM_LIMIT)


def _dot(a, b):
    return jnp.dot(a, b, preferred_element_type=F32)


def _dot_nt(a, b):
    return lax.dot_general(a, b, (((1,), (1,)), ((), ())), preferred_element_type=F32)


def _dot_tn(a, b):
    return lax.dot_general(a, b, (((0,), (0,)), ((), ())), preferred_element_type=F32)


def _rstd(x):
    return lax.rsqrt(jnp.mean(x * x, axis=-1, keepdims=True) + RMS_EPS)


def _rms_bwd(x, r, g, dy):
    dyg = dy * g
    xr = x * r
    dx = r * (dyg - xr * jnp.mean(dyg * xr, axis=-1, keepdims=True))
    return dx, dy * xr


def _rows(i, size=ROW_TILE):
    return pl.ds(pl.multiple_of(i * size, 16), size)


def _row_spec(width):
    return pl.BlockSpec((ROW_TILE, width), lambda i: (i, 0))


def _fix_spec(shape):
    return pl.BlockSpec(shape, lambda i: (0,) * len(shape))


def _split3(x):
    hi = x.astype(BF16)
    r1 = x - hi.astype(F32)
    mid = r1.astype(BF16)
    lo = (r1 - mid.astype(F32)).astype(BF16)
    return hi, mid, lo


def _prenorm(x, g):
    def body(x_ref, g_ref, a_ref):
        xv = x_ref[...]
        a_ref[...] = (xv * _rstd(xv) * g_ref[...]).astype(BF16)

    return pl.pallas_call(
        body, grid=(N_ROW_TILES,), in_specs=[_row_spec(D_MODEL), _fix_spec((1, D_MODEL))],
        out_specs=_row_spec(D_MODEL), out_shape=SDS((SEQ, D_MODEL), BF16), name="prenorm",
        compiler_params=_params(("parallel",)))(x, g)


def _post_pre(f, hres, g_post, g_next, scale, name):
    def body(f_ref, h_ref, gp_ref, gn_ref, ho_ref, a_ref):
        fv = f_ref[...]
        h = h_ref[...] + scale * (fv * _rstd(fv) * gp_ref[...])
        ho_ref[...] = h
        a_ref[...] = (h * _rstd(h) * gn_ref[...]).astype(BF16)

    return pl.pallas_call(
        body, grid=(N_ROW_TILES,),
        in_specs=[_row_spec(D_MODEL), _row_spec(D_MODEL), _fix_spec((1, D_MODEL)), _fix_spec((1, D_MODEL))],
        out_specs=[_row_spec(D_MODEL), _row_spec(D_MODEL)],
        out_shape=[SDS((SEQ, D_MODEL), F32), SDS((SEQ, D_MODEL), BF16)], name=name,
        compiler_params=_params(("parallel",)))(f, hres, g_post, g_next)


def _final_loss(f2, h2, g_post, g_final, target):
    def body(f_ref, h_ref, gp_ref, gf_ref, t_ref, loss_ref, dh_ref, df_ref, dgf_ref, dgp_ref):
        i = pl.program_id(0)
        fv = f_ref[...]
        r1 = _rstd(fv)
        gp = gp_ref[...]
        h3 = h_ref[...] + 0.5 * (fv * r1 * gp)
        r2 = _rstd(h3)
        gf = gf_ref[...]
        y = h3 * r2 * gf
        row = lax.broadcasted_iota(jnp.int32, (ROW_TILE, 1), 0) + i * ROW_TILE
        err = jnp.where(row >= N_META, y - t_ref[...], 0.0)
        part = 0.5 * jnp.sum(jnp.mean(err * err, axis=-1, keepdims=True))
        dy = err * (1.0 / D_MODEL)
        dh3, dgf = _rms_bwd(h3, r2, gf, dy)
        dh_ref[...] = dh3
        df, dgp = _rms_bwd(fv, r1, gp, 0.5 * dh3)
        df_ref[...] = df.astype(BF16)

        @pl.when(i == 0)
        def _():
            loss_ref[...] = jnp.zeros_like(loss_ref)
            dgf_ref[...] = jnp.zeros_like(dgf_ref)
            dgp_ref[...] = jnp.zeros_like(dgp_ref)

        loss_ref[...] += part
        dgf_ref[...] += jnp.sum(dgf, axis=0, keepdims=True)
        dgp_ref[...] += jnp.sum(dgp, axis=0, keepdims=True)

    gain = _fix_spec((1, D_MODEL))
    return pl.pallas_call(
        body, grid=(N_ROW_TILES,),
        in_specs=[_row_spec(D_MODEL), _row_spec(D_MODEL), gain, gain, _row_spec(D_MODEL)],
        out_specs=[_fix_spec((8, 128)), _row_spec(D_MODEL), _row_spec(D_MODEL), gain, gain],
        out_shape=[SDS((8, 128), F32), SDS((SEQ, D_MODEL), F32), SDS((SEQ, D_MODEL), BF16),
                   SDS((1, D_MODEL), F32), SDS((1, D_MODEL), F32)],
        name="final_loss", compiler_params=_params(("arbitrary",)))(f2, h2, g_post, g_final, target)


def _bwd_pre_post(da, h, g_pre, dh_res, fprev, g_post, scale, name):
    def body(da_ref, h_ref, gpre_ref, dhr_ref, f_ref, gpost_ref, dh_ref, df_ref, dgpre_ref, dgpost_ref):
        i = pl.program_id(0)
        hv = h_ref[...]
        dxa, dgpre = _rms_bwd(hv, _rstd(hv), gpre_ref[...], da_ref[...])
        dh = dhr_ref[...] + dxa
        dh_ref[...] = dh
        fv = f_ref[...]
        df, dgpost = _rms_bwd(fv, _rstd(fv), gpost_ref[...], scale * dh)
        df_ref[...] = df.astype(BF16)

        @pl.when(i == 0)
        def _():
            dgpre_ref[...] = jnp.zeros_like(dgpre_ref)
            dgpost_ref[...] = jnp.zeros_like(dgpost_ref)

        dgpre_ref[...] += jnp.sum(dgpre, axis=0, keepdims=True)
        dgpost_ref[...] += jnp.sum(dgpost, axis=0, keepdims=True)

    gain = _fix_spec((1, D_MODEL))
    row = _row_spec(D_MODEL)
    return pl.pallas_call(
        body, grid=(N_ROW_TILES,), in_specs=[row, row, gain, row, row, gain],
        out_specs=[row, row, gain, gain],
        out_shape=[SDS((SEQ, D_MODEL), F32), SDS((SEQ, D_MODEL), BF16), SDS((1, D_MODEL), F32), SDS((1, D_MODEL), F32)],
        name=name, compiler_params=_params(("arbitrary",)))(da, h, g_pre, dh_res, fprev, g_post)


def _bwd_pre_only(da, h, g_pre, dh_res):
    def body(da_ref, h_ref, gpre_ref, dhr_ref, dh_ref, dgpre_ref):
        i = pl.program_id(0)
        hv = h_ref[...]
        dxa, dgpre = _rms_bwd(hv, _rstd(hv), gpre_ref[...], da_ref[...])
        dh_ref[...] = dhr_ref[...] + dxa

        @pl.when(i == 0)
        def _():
            dgpre_ref[...] = jnp.zeros_like(dgpre_ref)

        dgpre_ref[...] += jnp.sum(dgpre, axis=0, keepdims=True)

    gain = _fix_spec((1, D_MODEL))
    row = _row_spec(D_MODEL)
    return pl.pallas_call(
        body, grid=(N_ROW_TILES,), in_specs=[row, row, gain, row], out_specs=[row, gain],
        out_shape=[SDS((SEQ, D_MODEL), F32), SDS((1, D_MODEL), F32)],
        name="bwd_pre_only", compiler_params=_params(("arbitrary",)))(da, h, g_pre, dh_res)


def _ffn_fwd(a, wg, wu, wd, name):
    def body(a_ref, wg_ref, wu_ref, wd_ref, gate_ref, up_ref, f_ref):
        j = pl.program_id(0)

        def tile(i, carry):
            rows = _rows(i)
            at = a_ref[rows, :]
            gate = _dot(at, wg_ref[...])
            up = _dot(at, wu_ref[...])
            gate_ref[rows, :] = gate
            up_ref[rows, :] = up
            act = (gate * jax.nn.sigmoid(gate) * up).astype(BF16)
            contrib = _dot(act, wd_ref[...])

            @pl.when(j == 0)
            def _():
                f_ref[rows, :] = contrib

            @pl.when(j != 0)
            def _():
                f_ref[rows, :] += contrib

            return carry

        lax.fori_loop(0, N_ROW_TILES, tile, 0)

    shard_cols = pl.BlockSpec((None, D_MODEL, FF_SHARD), lambda j: (j, 0, 0))
    shard_rows = pl.BlockSpec((None, FF_SHARD, D_MODEL), lambda j: (j, 0, 0))
    hid = pl.BlockSpec((None, SEQ, FF_SHARD), lambda j: (j, 0, 0))
    full = pl.BlockSpec((SEQ, D_MODEL), lambda j: (0, 0))
    return pl.pallas_call(
        body, grid=(N_DEV,), in_specs=[full, shard_cols, shard_cols, shard_rows], out_specs=[hid, hid, full],
        out_shape=[SDS((N_DEV, SEQ, FF_SHARD), F32), SDS((N_DEV, SEQ, FF_SHARD), F32), SDS((SEQ, D_MODEL), F32)],
        name=name, compiler_params=_params(("arbitrary",)))(a, wg, wu, wd)


def _ffn_bwd(df, a, gate, up, wg, wu, wd, name):
    def body(df_ref, a_ref, gate_ref, up_ref, wg_ref, wu_ref, wd_ref, da_ref, dwg_ref, dwu_ref, dwd_ref,
             acc_g, acc_u, acc_d):
        j = pl.program_id(0)

        def tile(i, carry):
            rows = _rows(i)
            dft = df_ref[rows, :]
            at = a_ref[rows, :]
            gate = gate_ref[rows, :]
            up = up_ref[rows, :]
            dact = _dot_nt(dft, wd_ref[...])
            sig = jax.nn.sigmoid(gate)
            silu = gate * sig
            dgate = (dact * up * (sig * (1.0 + gate * (1.0 - sig)))).astype(BF16)
            dup = (dact * silu).astype(BF16)
            act = (silu * up).astype(BF16)
            dwd = _dot_tn(act, dft)
            dwg = _dot_tn(at, dgate)
            dwu = _dot_tn(at, dup)
            dat = _dot_nt(dgate, wg_ref[...]) + _dot_nt(dup, wu_ref[...])

            @pl.when(i == 0)
            def _():
                acc_d[...] = dwd
                acc_g[...] = dwg
                acc_u[...] = dwu

            @pl.when(i != 0)
            def _():
                acc_d[...] += dwd
                acc_g[...] += dwg
                acc_u[...] += dwu

            @pl.when(j == 0)
            def _():
                da_ref[rows, :] = dat

            @pl.when(j != 0)
            def _():
                da_ref[rows, :] += dat

            return carry

        lax.fori_loop(0, N_ROW_TILES, tile, 0)
        dwg_ref[...] = acc_g[...].astype(BF16)
        dwu_ref[...] = acc_u[...].astype(BF16)
        dwd_ref[...] = acc_d[...].astype(BF16)

    shard_cols = pl.BlockSpec((None, D_MODEL, FF_SHARD), lambda j: (j, 0, 0))
    shard_rows = pl.BlockSpec((None, FF_SHARD, D_MODEL), lambda j: (j, 0, 0))
    hid = pl.BlockSpec((None, SEQ, FF_SHARD), lambda j: (j, 0, 0))
    full = pl.BlockSpec((SEQ, D_MODEL), lambda j: (0, 0))
    return pl.pallas_call(
        body, grid=(N_DEV,), in_specs=[full, full, hid, hid, shard_cols, shard_cols, shard_rows],
        out_specs=[full, shard_cols, shard_cols, shard_rows],
        out_shape=[SDS((SEQ, D_MODEL), F32), SDS((N_DEV, D_MODEL, FF_SHARD), BF16),
                   SDS((N_DEV, D_MODEL, FF_SHARD), BF16), SDS((N_DEV, FF_SHARD, D_MODEL), BF16)],
        scratch_shapes=[pltpu.VMEM((D_MODEL, FF_SHARD), F32), pltpu.VMEM((D_MODEL, FF_SHARD), F32),
                        pltpu.VMEM((FF_SHARD, D_MODEL), F32)],
        name=name, compiler_params=_params(("arbitrary",)))(df, a, gate, up, wg, wu, wd)


def _proj_fwd(a, w):
    def body(a_ref, w_ref, o_ref):
        def tile(i, carry):
            rows = _rows(i)
            o_ref[rows, :] = _dot(a_ref[rows, :], w_ref[...])
            return carry

        lax.fori_loop(0, N_ROW_TILES, tile, 0)

    return pl.pallas_call(
        body, grid=(N_DEV,),
        in_specs=[pl.BlockSpec((SEQ, D_MODEL), lambda j: (0, 0)), pl.BlockSpec((None, D_MODEL, IN_SHARD), lambda j: (j, 0, 0))],
        out_specs=pl.BlockSpec((None, SEQ, IN_SHARD), lambda j: (j, 0, 0)),
        out_shape=SDS((N_DEV, SEQ, IN_SHARD), F32), name="proj_fwd",
        compiler_params=_params(("parallel",)))(a, w)


def _proj_bwd(dproj, a, w):
    def body(dp_ref, a_ref, w_ref, da_ref, dw_ref, acc):
        j = pl.program_id(0)

        def tile(i, carry):
            rows = _rows(i)
            dpt = dp_ref[rows, :]
            dw = _dot_tn(a_ref[rows, :], dpt)
            dat = _dot_nt(dpt, w_ref[...])

            @pl.when(i == 0)
            def _():
                acc[...] = dw

            @pl.when(i != 0)
            def _():
                acc[...] += dw

            @pl.when(j == 0)
            def _():
                da_ref[rows, :] = dat

            @pl.when(j != 0)
            def _():
                da_ref[rows, :] += dat

            return carry

        lax.fori_loop(0, N_ROW_TILES, tile, 0)
        dw_ref[...] = acc[...].astype(BF16)

    full = pl.BlockSpec((SEQ, D_MODEL), lambda j: (0, 0))
    wspec = pl.BlockSpec((None, D_MODEL, IN_SHARD), lambda j: (j, 0, 0))
    return pl.pallas_call(
        body, grid=(N_DEV,),
        in_specs=[pl.BlockSpec((None, SEQ, IN_SHARD), lambda j: (j, 0, 0)), full, wspec],
        out_specs=[full, wspec],
        out_shape=[SDS((SEQ, D_MODEL), F32), SDS((N_DEV, D_MODEL, IN_SHARD), BF16)],
        scratch_shapes=[pltpu.VMEM((D_MODEL, IN_SHARD), F32)],
        name="proj_bwd", compiler_params=_params(("arbitrary",)))(dproj, a, w)


def _na_consts():
    c = np.arange(GRID_W)
    col_start = np.clip(c - KW // 2, 0, GRID_W - KW)
    col_in = (c[None, :] >= col_start[:, None]) & (c[None, :] < col_start[:, None] + KW)
    dc = np.clip(c[None, :] - c[:, None] + KW - 1, 0, 2 * KW - 2)
    onehot = np.zeros((128, GRID_W * GRID_W), np.float32)
    qq, kk = np.meshgrid(c, c, indexing="ij")
    onehot[dc[col_in], (qq * GRID_W + kk)[col_in]] = 1.0
    negmask = np.where(col_in, 0.0, NEG_INF).astype(np.float32).reshape(1, -1)
    return onehot, negmask


def _na_pair(block_type, a, b):
    if block_type == 0:
        return b - a + KH - 1 if b < KH else None
    if block_type == 1:
        return b - a + KH // 2 - 1 if a <= b < a + KH else None
    return b - a if b >= NA_KR - KH else None


def _rpb_expand(rpb):
    onehot, negmask = _na_consts()
    rows = HEADS * (2 * KH - 1)
    rpb_pad = jnp.pad(rpb.reshape(rows, 2 * KW - 1), ((0, 128 - rows), (0, 128 - (2 * KW - 1))))

    def body(r_ref, oh_ref, m_ref, t_ref):
        hi, mid, lo = _split3(r_ref[...])
        oh = oh_ref[...]
        t_ref[...] = _dot(hi, oh) + _dot(mid, oh) + _dot(lo, oh) + m_ref[...]

    table = pl.pallas_call(body, out_shape=SDS((128, GRID_W * GRID_W), F32), name="rpb_expand",
                           compiler_params=_params())(rpb_pad, jnp.asarray(onehot, BF16), jnp.asarray(negmask))
    return table[:rows].reshape(HEADS, 2 * KH - 1, GRID_W, GRID_W)


def _rpb_reduce(dslabs):
    onehot, _ = _na_consts()
    rows = HEADS * (2 * KH - 1)

    def body(x_ref, oht_ref, o_ref):
        hi, mid, lo = _split3(x_ref[...])
        oht = oht_ref[...]
        o_ref[...] = _dot(hi, oht) + _dot(mid, oht) + _dot(lo, oht)

    out = pl.pallas_call(body, out_shape=SDS((rows, 128), F32), name="rpb_reduce", compiler_params=_params())(
        dslabs.reshape(rows, GRID_W * GRID_W), jnp.asarray(onehot.T, BF16))
    return out.reshape(HEADS, 2 * KH - 1, 128)


def _bias_tiles(slab_ref, tile_ref):
    tile_ref[...] = jnp.full(tile_ref.shape, NEG_INF, F32)
    for t in range(NA_TYPES):
        for a in range(NA_RB):
            for b in range(NA_KR):
                dr = _na_pair(t, a, b)
                if dr is not None:
                    tile_ref[t, a * GRID_W:(a + 1) * GRID_W, b * GRID_W:(b + 1) * GRID_W] = slab_ref[dr]


def _bias_tiles_bwd(dtile_ref, dslab_ref):
    acc = {}
    for t in range(NA_TYPES):
        for a in range(NA_RB):
            for b in range(NA_KR):
                dr = _na_pair(t, a, b)
                if dr is not None:
                    part = dtile_ref[t, a * GRID_W:(a + 1) * GRID_W, b * GRID_W:(b + 1) * GRID_W]
                    acc[dr] = part if dr not in acc else acc[dr] + part
    for dr in range(2 * KH - 1):
        dslab_ref[dr] = acc[dr]


def _block_geometry(g):
    start = jnp.clip(g * NA_RB - KH // 2, 0, GRID_ROWS - NA_KR)
    block_type = jnp.where(g == 0, 0, jnp.where(g == NA_BLOCKS - 1, 2, 1))
    q0 = pl.multiple_of(N_META + g * NA_QB, 16)
    k0 = pl.multiple_of(N_META + start * GRID_W, 16)
    return block_type, q0, k0


def _na_probs(q, kk, km, bias):
    s = _dot_nt(q, kk) * ATT_SCALE + bias
    sm = _dot_nt(q, km) * ATT_SCALE
    m = jnp.maximum(jnp.max(s, axis=-1, keepdims=True), jnp.max(sm, axis=-1, keepdims=True))
    p = jnp.exp(s - m)
    pm = jnp.exp(sm - m)
    inv = 1.0 / (jnp.sum(p, axis=-1, keepdims=True) + jnp.sum(pm, axis=-1, keepdims=True))
    return p * inv, pm * inv


def _meta_probs(qm, km):
    s = _dot_nt(qm, km) * ATT_SCALE
    p = jnp.exp(s - jnp.max(s, axis=-1, keepdims=True))
    return p / jnp.sum(p, axis=-1, keepdims=True)


def _na_fwd(q, k, v, bias):
    def body(q_ref, k_ref, v_ref, slab_ref, o_ref, b_ref):
        _bias_tiles(slab_ref, b_ref)
        km = k_ref[0:N_META, :].astype(BF16)
        vm = v_ref[0:N_META, :].astype(BF16)
        pmm = _meta_probs(q_ref[0:N_META, :].astype(BF16), km)
        o_ref[0:N_META, :] = _dot(pmm.astype(BF16), vm)

        def block(g, carry):
            block_type, q0, k0 = _block_geometry(g)
            qb = q_ref[pl.ds(q0, NA_QB), :].astype(BF16)
            kk = k_ref[pl.ds(k0, NA_KB), :].astype(BF16)
            vv = v_ref[pl.ds(k0, NA_KB), :].astype(BF16)
            p, pm = _na_probs(qb, kk, km, b_ref[block_type])
            o_ref[pl.ds(q0, NA_QB), :] = _dot(p.astype(BF16), vv) + _dot(pm.astype(BF16), vm)
            return carry

        lax.fori_loop(0, NA_BLOCKS, block, 0)

    head = pl.BlockSpec((None, SEQ, HEAD_DIM), lambda h: (h, 0, 0))
    return pl.pallas_call(
        body, grid=(HEADS,), in_specs=[head, head, head, pl.BlockSpec((None, 2 * KH - 1, GRID_W, GRID_W), lambda h: (h, 0, 0, 0))],
        out_specs=head, out_shape=SDS((HEADS, SEQ, HEAD_DIM), F32), name="na_fwd",
        scratch_shapes=[pltpu.VMEM((NA_TYPES, NA_QB, NA_KB), F32)],
        compiler_params=_params(("parallel",)))(q, k, v, bias)


def _na_bwd(q, k, v, bias, do):
    def body(q_ref, k_ref, v_ref, slab_ref, do_ref, dq_ref, dk_ref, dv_ref, dslab_ref, b_ref, db_ref):
        _bias_tiles(slab_ref, b_ref)
        km = k_ref[0:N_META, :].astype(BF16)
        vm = v_ref[0:N_META, :].astype(BF16)
        dk_ref[...] = jnp.zeros_like(dk_ref)
        dv_ref[...] = jnp.zeros_like(dv_ref)
        db_ref[...] = jnp.zeros_like(db_ref)

        qm = q_ref[0:N_META, :].astype(BF16)
        dom = do_ref[0:N_META, :].astype(BF16)
        pmm = _meta_probs(qm, km)
        dpm = _dot_nt(dom, vm)
        dsm = (pmm * (dpm - jnp.sum(pmm * dpm, axis=-1, keepdims=True)) * ATT_SCALE).astype(BF16)
        dq_ref[0:N_META, :] = _dot(dsm, km)
        dkm0 = _dot_tn(dsm, qm)
        dvm0 = _dot_tn(pmm.astype(BF16), dom)

        def block(g, carry):
            dkm, dvm = carry
            block_type, q0, k0 = _block_geometry(g)
            qb = q_ref[pl.ds(q0, NA_QB), :].astype(BF16)
            kk = k_ref[pl.ds(k0, NA_KB), :].astype(BF16)
            vv = v_ref[pl.ds(k0, NA_KB), :].astype(BF16)
            dob = do_ref[pl.ds(q0, NA_QB), :].astype(BF16)
            p, pm = _na_probs(qb, kk, km, b_ref[block_type])
            dp = _dot_nt(dob, vv)
            dpm_ = _dot_nt(dob, vm)
            delta = jnp.sum(p * dp, axis=-1, keepdims=True) + jnp.sum(pm * dpm_, axis=-1, keepdims=True)
            ds = p * (dp - delta)
            dsm_ = pm * (dpm_ - delta)
            db_ref[block_type] += ds
            dsb = (ds * ATT_SCALE).astype(BF16)
            dsmb = (dsm_ * ATT_SCALE).astype(BF16)
            dq_ref[pl.ds(q0, NA_QB), :] = _dot(dsb, kk) + _dot(dsmb, km)
            dk_ref[pl.ds(k0, NA_KB), :] += _dot_tn(dsb, qb)
            dv_ref[pl.ds(k0, NA_KB), :] += _dot_tn(p.astype(BF16), dob)
            return dkm + _dot_tn(dsmb, qb), dvm + _dot_tn(pm.astype(BF16), dob)

        dkm, dvm = lax.fori_loop(0, NA_BLOCKS, block, (dkm0, dvm0))
        dk_ref[0:N_META, :] = dkm
        dv_ref[0:N_META, :] = dvm
        _bias_tiles_bwd(db_ref, dslab_ref)

    head = pl.BlockSpec((None, SEQ, HEAD_DIM), lambda h: (h, 0, 0))
    bspec = pl.BlockSpec((None, 2 * KH - 1, GRID_W, GRID_W), lambda h: (h, 0, 0, 0))
    return pl.pallas_call(
        body, grid=(HEADS,), in_specs=[head, head, head, bspec, head], out_specs=[head, head, head, bspec],
        out_shape=[SDS((HEADS, SEQ, HEAD_DIM), F32)] * 3 + [SDS((HEADS, 2 * KH - 1, GRID_W, GRID_W), F32)],
        scratch_shapes=[pltpu.VMEM((NA_TYPES, NA_QB, NA_KB), F32), pltpu.VMEM((NA_TYPES, NA_QB, NA_KB), F32)],
        name="na_bwd", compiler_params=_params(("parallel",)))(q, k, v, bias, do)


def _cmul(ar, ai, br, bi):
    return ar * br - ai * bi, ar * bi + ai * br


def _cpow(ar, ai, n):
    rr, ri = None, None
    br, bi = ar, ai
    while n:
        if n & 1:
            rr, ri = (br, bi) if rr is None else _cmul(rr, ri, br, bi)
        n >>= 1
        if n:
            br, bi = _cmul(br, bi, br, bi)
    return rr, ri


def _s5_prep(lr, li, logdt, bre, bim):
    def body(lr_ref, li_ref, dt_ref, br_ref, bi_ref, lbr_ref, lbi_ref, bbr_ref, bbi_ref):
        lr_, li_ = lr_ref[...], li_ref[...]
        dt = jnp.exp(dt_ref[...])
        mag = jnp.exp(lr_ * dt)
        lbr = mag * jnp.cos(li_ * dt)
        lbi = mag * jnp.sin(li_ * dt)
        lbr_ref[...] = lbr
        lbi_ref[...] = lbi
        den = lr_ * lr_ + li_ * li_
        xr = lbr - 1.0
        cr = (xr * lr_ + lbi * li_) / den
        ci = (lbi * lr_ - xr * li_) / den
        br, bi = br_ref[...], bi_ref[...]
        bbr_ref[...] = cr[:, None, :] * br - ci[:, None, :] * bi
        bbi_ref[...] = cr[:, None, :] * bi + ci[:, None, :] * br

    n = 2 * S5_GROUPS
    return pl.pallas_call(
        body, out_shape=[SDS((n, S5_STATE), F32)] * 2 + [SDS((n, S5_GROUP, S5_STATE), F32)] * 2,
        name="s5_prep", compiler_params=_params())(lr, li, logdt, bre, bim)


def _s5_prep_bwd(lr, li, logdt, bre, bim, dar, dai, dbbr, dbbi):
    def body(lr_ref, li_ref, dt_ref, br_ref, bi_ref, dar_ref, dai_ref, dbr_ref, dbi_ref,
             glr_ref, gli_ref, gdt_ref, gbr_ref, gbi_ref):
        lr_, li_ = lr_ref[...], li_ref[...]
        dt = jnp.exp(dt_ref[...])
        mag = jnp.exp(lr_ * dt)
        lbr = mag * jnp.cos(li_ * dt)
        lbi = mag * jnp.sin(li_ * dt)
        den = lr_ * lr_ + li_ * li_
        xr = lbr - 1.0
        cr = (xr * lr_ + lbi * li_) / den
        ci = (lbi * lr_ - xr * li_) / den
        br, bi = br_ref[...], bi_ref[...]
        dbr, dbi = dbr_ref[...], dbi_ref[...]
        gbr_ref[...] = cr[:, None, :] * dbr + ci[:, None, :] * dbi
        gbi_ref[...] = cr[:, None, :] * dbi - ci[:, None, :] * dbr
        gcr = jnp.sum(dbr * br + dbi * bi, axis=1)
        gci = jnp.sum(dbi * br - dbr * bi, axis=1)
        ilr, ili = lr_ / den, li_ / den
        tr, ti = _cmul(gcr, gci, ilr, ili)
        glbr = dar_ref[...] + tr
        glbi = dai_ref[...] + ti
        dr_, di_ = _cmul(tr, ti, cr, -ci)
        gwr, gwi = _cmul(glbr, glbi, lbr, -lbi)
        glr_ref[...] = gwr * dt - dr_
        gli_ref[...] = gwi * dt - di_
        gdt_ref[...] = jnp.sum(gwr * lr_ + gwi * li_, axis=-1, keepdims=True) * dt

    n = 2 * S5_GROUPS
    return pl.pallas_call(
        body, out_shape=[SDS((n, S5_STATE), F32)] * 2 + [SDS((n, 1), F32)] + [SDS((n, S5_GROUP, S5_STATE), F32)] * 2,
        name="s5_prep_bwd", compiler_params=_params())(lr, li, logdt, bre, bim, dar, dai, dbbr, dbbi)


def _scan_local(xr_ref, xi_ref, ar8, ai8, reverse):
    def step(i, carry):
        sr, si = carry
        idx = (SCAN_T - 1 - i) if reverse else i
        rows = pl.ds(pl.multiple_of(idx * SCAN_BLOCKS, SCAN_BLOCKS), SCAN_BLOCKS)
        nr = ar8 * sr - ai8 * si + xr_ref[rows, :]
        ni = ar8 * si + ai8 * sr + xi_ref[rows, :]
        xr_ref[rows, :] = nr
        xi_ref[rows, :] = ni
        return nr, ni

    z = jnp.zeros(ar8.shape, F32)
    return lax.fori_loop(0, SCAN_T, step, (z, z))


def _scan_carries(er, ei, atr, ati, reverse):
    row = lax.broadcasted_iota(jnp.int32, er.shape, 0)
    cr = jnp.zeros((1, er.shape[1]), F32)
    ci = cr
    outr = jnp.zeros(er.shape, F32)
    outi = outr
    order = range(SCAN_BLOCKS - 1, -1, -1) if reverse else range(SCAN_BLOCKS)
    for b in order:
        outr = jnp.where(row == b, cr, outr)
        outi = jnp.where(row == b, ci, outi)
        nr, ni = _cmul(atr, ati, cr, ci)
        cr, ci = nr + er[b:b + 1, :], ni + ei[b:b + 1, :]
    return outr, outi


def _scan_fixup(xr_ref, xi_ref, cr8, ci8, ar8, ai8, reverse):
    def step(i, carry):
        pr, pi = carry
        idx = (SCAN_T - 1 - i) if reverse else i
        rows = pl.ds(pl.multiple_of(idx * SCAN_BLOCKS, SCAN_BLOCKS), SCAN_BLOCKS)
        fr, fi = _cmul(pr, pi, cr8, ci8)
        xr_ref[rows, :] += fr
        xi_ref[rows, :] += fi
        return _cmul(pr, pi, ar8, ai8)

    lax.fori_loop(0, SCAN_T, step, (ar8, ai8))


def _scan(xr_ref, xi_ref, ar, ai, reverse):
    n = ar.shape[1]
    ar8 = jnp.broadcast_to(ar, (SCAN_BLOCKS, n))
    ai8 = jnp.broadcast_to(ai, (SCAN_BLOCKS, n))
    er, ei = _scan_local(xr_ref, xi_ref, ar8, ai8, reverse)
    atr, ati = _cpow(ar, ai, SCAN_T)
    cr8, ci8 = _scan_carries(er, ei, atr, ati, reverse)
    _scan_fixup(xr_ref, xi_ref, cr8, ci8, ar8, ai8, reverse)


def _s5_specs():
    chan = pl.BlockSpec((SEQ, CH_W), lambda c, d: (0, c))
    chan2 = pl.BlockSpec((None, SEQ, CH_W), lambda c, d: (d, 0, c))
    state = pl.BlockSpec((None, SEQ, ST_W), lambda c, d: (d, 0, c))
    bmat = pl.BlockSpec((None, None, CH_W, ST_W), lambda c, d: (d, c, 0, 0))
    cmat = pl.BlockSpec((None, None, ST_W, CH_W), lambda c, d: (d, c, 0, 0))
    avec = pl.BlockSpec((None, None, 1, ST_W), lambda c, d: (d, c, 0, 0))
    return chan, chan2, state, bmat, cmat, avec


def _scan_by_direction(xr_ref, xi_ref, ar, ai, d, adjoint):
    @pl.when(d == 0)
    def _():
        _scan(xr_ref, xi_ref, ar, ai, reverse=adjoint)

    @pl.when(d == 1)
    def _():
        _scan(xr_ref, xi_ref, ar, ai, reverse=not adjoint)


def _s5_scan_fwd(u, bre, bim, are, aim, cre, cim):
    def body(u_ref, bre_ref, bim_ref, are_ref, aim_ref, cre_ref, cim_ref, sr_ref, si_ref, y_ref):
        ub = u_ref[...].astype(BF16)
        sr_ref[...] = _dot(ub, bre_ref[...])
        si_ref[...] = _dot(ub, bim_ref[...])
        _scan_by_direction(sr_ref, si_ref, are_ref[...], aim_ref[...], pl.program_id(1), adjoint=False)
        y_ref[...] = _dot(sr_ref[...].astype(BF16), cre_ref[...]) - _dot(si_ref[...].astype(BF16), cim_ref[...])

    chan, chan2, state, bmat, cmat, avec = _s5_specs()
    return pl.pallas_call(
        body, grid=(S5_CHUNKS, 2), in_specs=[chan, bmat, bmat, avec, avec, cmat, cmat], out_specs=[state, state, chan2],
        out_shape=[SDS((2, SEQ, S5_GROUPS * S5_STATE), F32)] * 2 + [SDS((2, SEQ, S5_WIDTH), F32)],
        name="s5_scan_fwd", compiler_params=_params(("parallel", "parallel")))(u, bre, bim, are, aim, cre, cim)


def _dlam(gr_ref, gi_ref, sr_ref, si_ref, reverse):
    tile = lambda i: pl.ds(pl.multiple_of(i * SCAN_BLOCKS, SCAN_BLOCKS), SCAN_BLOCKS)
    row = lax.broadcasted_iota(jnp.int32, (SCAN_BLOCKS, ST_W), 0)
    if reverse:
        edge, src, shift, empty, lo, hi, dprev = SCAN_T - 1, 0, SCAN_BLOCKS - 1, SCAN_BLOCKS - 1, 0, SCAN_T - 1, 1
    else:
        edge, src, shift, empty, lo, hi, dprev = 0, SCAN_T - 1, 1, 0, 1, SCAN_T, -1
    spr = jnp.where(row == empty, 0.0, pltpu.roll(sr_ref[tile(src), :], shift, 0))
    spi = jnp.where(row == empty, 0.0, pltpu.roll(si_ref[tile(src), :], shift, 0))
    acc0 = _cmul(gr_ref[tile(edge), :], gi_ref[tile(edge), :], spr, -spi)

    def step(i, carry):
        accr, acci = carry
        pr, pi = _cmul(gr_ref[tile(i), :], gi_ref[tile(i), :], sr_ref[tile(i + dprev), :], -si_ref[tile(i + dprev), :])
        return accr + pr, acci + pi

    accr, acci = lax.fori_loop(lo, hi, step, acc0)
    return jnp.sum(accr, axis=0, keepdims=True), jnp.sum(acci, axis=0, keepdims=True)


def _s5_scan_bwd(dy, du_skip, u, sr, si, bre, bim, are, aim, cre, cim):
    def body(dy_ref, dus_ref, u_ref, sr_ref, si_ref, bre_ref, bim_ref, are_ref, aim_ref, cre_ref, cim_ref,
             du_ref, dbr_ref, dbi_ref, dcr_ref, dci_ref, dar_ref, dai_ref, gr_ref, gi_ref):
        d = pl.program_id(1)
        dyb = dy_ref[...].astype(BF16)
        gr_ref[...] = _dot_nt(dyb, cre_ref[...])
        gi_ref[...] = -_dot_nt(dyb, cim_ref[...])
        dcr_ref[...] = _dot_tn(sr_ref[...].astype(BF16), dyb)
        dci_ref[...] = -_dot_tn(si_ref[...].astype(BF16), dyb)
        _scan_by_direction(gr_ref, gi_ref, are_ref[...], -aim_ref[...], d, adjoint=True)

        @pl.when(d == 0)
        def _():
            dar_ref[...], dai_ref[...] = _dlam(gr_ref, gi_ref, sr_ref, si_ref, reverse=False)
            du_ref[...] = dus_ref[...]

        @pl.when(d == 1)
        def _():
            dar_ref[...], dai_ref[...] = _dlam(gr_ref, gi_ref, sr_ref, si_ref, reverse=True)

        grb = gr_ref[...].astype(BF16)
        gib = gi_ref[...].astype(BF16)
        du_ref[...] += _dot_nt(grb, bre_ref[...]) + _dot_nt(gib, bim_ref[...])
        ub = u_ref[...].astype(BF16)
        dbr_ref[...] = _dot_tn(ub, grb)
        dbi_ref[...] = _dot_tn(ub, gib)

    chan, _, state, bmat, cmat, avec = _s5_specs()
    return pl.pallas_call(
        body, grid=(S5_CHUNKS, 2), in_specs=[chan, chan, chan, state, state, bmat, bmat, avec, avec, cmat, cmat],
        out_specs=[chan, bmat, bmat, cmat, cmat, avec, avec],
        out_shape=[SDS((SEQ, S5_WIDTH), F32)] + [SDS((2, S5_CHUNKS, CH_W, ST_W), F32)] * 2
                  + [SDS((2, S5_CHUNKS, ST_W, CH_W), F32)] * 2 + [SDS((2, S5_CHUNKS, 1, ST_W), F32)] * 2,
        scratch_shapes=[pltpu.VMEM((SEQ, ST_W), F32), pltpu.VMEM((SEQ, ST_W), F32)],
        name="s5_scan_bwd", compiler_params=_params(("parallel", "arbitrary")))(dy, du_skip, u, sr, si, bre, bim, are, aim, cre, cim)


_GELU_K = math.sqrt(2.0 / math.pi)
_GELU_C = 0.044715


def _gelu(x):
    t = jnp.tanh(_GELU_K * (x + _GELU_C * x * x * x))
    return 0.5 * x * (1.0 + t), t


def _s5_glu_fwd(u, y2, dskip, wglu, bglu):
    def body(u_ref, y0_ref, y1_ref, d_ref, w_ref, b_ref, o_ref, yp_ref):
        ypre = u_ref[...] * d_ref[...] + y0_ref[...] + y1_ref[...]
        yp_ref[...] = ypre
        y, _ = _gelu(ypre)
        z = _dot(y.astype(BF16), w_ref[...]) + b_ref[...]
        o_ref[...] = y * jax.nn.sigmoid(z)

    row = _row_spec(S5_WIDTH)
    vec = _fix_spec((1, S5_WIDTH))
    dir0 = pl.BlockSpec((None, ROW_TILE, S5_WIDTH), lambda i: (0, i, 0))
    dir1 = pl.BlockSpec((None, ROW_TILE, S5_WIDTH), lambda i: (1, i, 0))
    return pl.pallas_call(
        body, grid=(N_ROW_TILES,), in_specs=[row, dir0, dir1, vec, _fix_spec((S5_WIDTH, S5_WIDTH)), vec],
        out_specs=[row, row], out_shape=[SDS((SEQ, S5_WIDTH), F32)] * 2, name="s5_glu_fwd",
        compiler_params=_params(("parallel",)))(u, y2, y2, dskip, wglu, bglu)


def _s5_glu_bwd(do, ypre, u, dskip, wglu, bglu):
    def body(do_ref, yp_ref, u_ref, d_ref, w_ref, b_ref, dyp_ref, du_ref, dw_ref, db_ref, dd_ref):
        i = pl.program_id(0)
        ypre = yp_ref[...]
        y, t = _gelu(ypre)
        yb = y.astype(BF16)
        sg = jax.nn.sigmoid(_dot(yb, w_ref[...]) + b_ref[...])
        dov = do_ref[...]
        dz = dov * y * sg * (1.0 - sg)
        dzb = dz.astype(BF16)
        dy = dov * sg + _dot_nt(dzb, w_ref[...])
        dgelu = 0.5 * (1.0 + t) + 0.5 * ypre * (1.0 - t * t) * _GELU_K * (1.0 + 3.0 * _GELU_C * ypre * ypre)
        dyp = dy * dgelu
        dyp_ref[...] = dyp
        uv = u_ref[...]
        du_ref[...] = dyp * d_ref[...]

        @pl.when(i == 0)
        def _():
            dw_ref[...] = jnp.zeros_like(dw_ref)
            db_ref[...] = jnp.zeros_like(db_ref)
            dd_ref[...] = jnp.zeros_like(dd_ref)

        dw_ref[...] += _dot_tn(yb, dzb)
        db_ref[...] += jnp.sum(dz, axis=0, keepdims=True)
        dd_ref[...] += jnp.sum(dyp * uv, axis=0, keepdims=True)

    row = _row_spec(S5_WIDTH)
    vec = _fix_spec((1, S5_WIDTH))
    mat = _fix_spec((S5_WIDTH, S5_WIDTH))
    return pl.pallas_call(
        body, grid=(N_ROW_TILES,), in_specs=[row, row, row, vec, mat, vec], out_specs=[row, row, mat, vec, vec],
        out_shape=[SDS((SEQ, S5_WIDTH), F32)] * 2 + [SDS((S5_WIDTH, S5_WIDTH), F32), SDS((1, S5_WIDTH), F32), SDS((1, S5_WIDTH), F32)],
        name="s5_glu_bwd", compiler_params=_params(("arbitrary",)))(do, ypre, u, dskip, wglu, bglu)


def _mix_out_fwd(ona, os5, g_na, g_s5, wout):
    def body(a_ref, s_ref, ga_ref, gs_ref, w_ref, o_ref):
        av, sv = a_ref[...], s_ref[...]
        ca = (av * _rstd(av) * ga_ref[...]).astype(BF16)
        cs = (sv * _rstd(sv) * gs_ref[...]).astype(BF16)
        o_ref[...] = _dot(ca, w_ref[0:NA_WIDTH, :]) + _dot(cs, w_ref[NA_WIDTH:, :])

    row = _row_spec(NA_WIDTH)
    vec = _fix_spec((1, NA_WIDTH))
    return pl.pallas_call(
        body, grid=(N_ROW_TILES,), in_specs=[row, row, vec, vec, _fix_spec((D_MODEL, D_MODEL))],
        out_specs=_row_spec(D_MODEL), out_shape=SDS((SEQ, D_MODEL), F32), name="mix_out_fwd",
        compiler_params=_params(("parallel",)))(ona, os5, g_na, g_s5, wout)


def _mix_out_bwd(dmix, ona, os5, g_na, g_s5, wout):
    def body(dm_ref, a_ref, s_ref, ga_ref, gs_ref, w_ref, da_ref, ds_ref, dw_ref, dga_ref, dgs_ref):
        i = pl.program_id(0)
        dm = dm_ref[...]
        av, sv = a_ref[...], s_ref[...]
        ra, rs = _rstd(av), _rstd(sv)
        ga, gs = ga_ref[...], gs_ref[...]
        ca = (av * ra * ga).astype(BF16)
        cs = (sv * rs * gs).astype(BF16)
        dca = _dot_nt(dm, w_ref[0:NA_WIDTH, :])
        dcs = _dot_nt(dm, w_ref[NA_WIDTH:, :])
        da, dga = _rms_bwd(av, ra, ga, dca)
        ds, dgs = _rms_bwd(sv, rs, gs, dcs)
        da_ref[...] = da
        ds_ref[...] = ds

        @pl.when(i == 0)
        def _():
            dw_ref[...] = jnp.zeros_like(dw_ref)
            dga_ref[...] = jnp.zeros_like(dga_ref)
            dgs_ref[...] = jnp.zeros_like(dgs_ref)

        dw_ref[0:NA_WIDTH, :] += _dot_tn(ca, dm)
        dw_ref[NA_WIDTH:, :] += _dot_tn(cs, dm)
        dga_ref[...] += jnp.sum(dga, axis=0, keepdims=True)
        dgs_ref[...] += jnp.sum(dgs, axis=0, keepdims=True)

    row = _row_spec(NA_WIDTH)
    vec = _fix_spec((1, NA_WIDTH))
    mat = _fix_spec((D_MODEL, D_MODEL))
    return pl.pallas_call(
        body, grid=(N_ROW_TILES,), in_specs=[_row_spec(D_MODEL), row, row, vec, vec, mat],
        out_specs=[row, row, mat, vec, vec],
        out_shape=[SDS((SEQ, NA_WIDTH), F32)] * 2 + [SDS((D_MODEL, D_MODEL), F32), SDS((1, NA_WIDTH), F32), SDS((1, NA_WIDTH), F32)],
        name="mix_out_bwd", compiler_params=_params(("arbitrary",)))(dmix, ona, os5, g_na, g_s5, wout)


def _me():
    x, y, c = lax.axis_index("x"), lax.axis_index("y"), lax.axis_index("c")
    return x, y, c, 4 * x + 2 * y + c


def _peer(k):
    x, y, c, _ = _me()
    px = 1 - x if (k >> 2) & 1 else x
    py = 1 - y if (k >> 1) & 1 else y
    pc = 1 - c if k & 1 else c
    return (px, py, pc), 4 * px + 2 * py + pc


def _exchange(arrays, gather, name, after=()):
    n, n_after = len(arrays), len(after)

    def body(*refs):
        ins, outs = refs[:n], refs[n + n_after:2 * n + n_after]
        send_sems, recv_sems, local_sems = refs[2 * n + n_after:]
        _, _, _, me = _me()
        started = []
        for a in range(n):
            src_mine = ins[a] if gather else ins[a].at[me]
            local = pltpu.make_async_copy(src_mine, outs[a].at[me], local_sems.at[a])
            local.start()
            started.append(local)
        sends = []
        for k in range(1, N_DEV):
            peer, peer_idx = _peer(k)
            for a in range(n):
                src = ins[a] if gather else ins[a].at[peer_idx]
                cp = pltpu.make_async_remote_copy(src_ref=src, dst_ref=outs[a].at[me], send_sem=send_sems.at[a, k - 1],
                                                  recv_sem=recv_sems.at[a, k - 1], device_id=peer, device_id_type=MESH)
                cp.start()
                sends.append(cp)
        for k in range(1, N_DEV):
            peer, peer_idx = _peer(k)
            for a in range(n):
                src = ins[a] if gather else ins[a].at[peer_idx]
                pltpu.make_async_remote_copy(src_ref=src, dst_ref=outs[a].at[peer_idx], send_sem=send_sems.at[a, k - 1],
                                             recv_sem=recv_sems.at[a, k - 1], device_id=peer, device_id_type=MESH).wait_recv()
        for cp in sends:
            cp.wait_send()
        for local in started:
            local.wait()

    hbm = pl.BlockSpec(memory_space=pltpu.HBM)
    out_shape = [SDS((N_DEV,) + tuple(a.shape), a.dtype) if gather else SDS(a.shape, a.dtype) for a in arrays]
    return pl.pallas_call(
        body, in_specs=[hbm] * n + [pl.BlockSpec(memory_space=pl.ANY)] * n_after, out_specs=[hbm] * n, out_shape=out_shape,
        scratch_shapes=[pltpu.SemaphoreType.DMA((n, N_DEV - 1)), pltpu.SemaphoreType.DMA((n, N_DEV - 1)),
                        pltpu.SemaphoreType.DMA((n,))],
        name=name)(*arrays, *after)


_HBM = pl.BlockSpec(memory_space=pltpu.HBM)
_SEM = pl.BlockSpec(memory_space=pltpu.SEMAPHORE)
_EFFECT = pltpu.SideEffectType.DATAFLOW_SIDE_EFFECTING


def _land_shape(a, gather):
    return (N_DEV,) + tuple(a.shape) if gather else tuple(a.shape)


def _place_own(arrays, gather, name):
    n = len(arrays)
    _, _, _, me = _me()

    def body(me_ref, *refs):
        for a in range(n):
            refs[n + a][...] = refs[a][...]

    def own_slot(a):
        zeros = (0,) * (a.ndim - (0 if gather else 1))
        return lambda i, me_ref: (me_ref[0],) + zeros

    def whole(a):
        return lambda i, me_ref: (0,) * a.ndim

    in_specs = [pl.BlockSpec(a.shape, whole(a)) if gather else pl.BlockSpec((None,) + a.shape[1:], own_slot(a)) for a in arrays]
    out_specs = [pl.BlockSpec((None,) + (a.shape if gather else a.shape[1:]), own_slot(a)) for a in arrays]
    return pl.pallas_call(
        body, grid_spec=pltpu.PrefetchScalarGridSpec(num_scalar_prefetch=1, grid=(1,), in_specs=in_specs, out_specs=out_specs),
        out_shape=[SDS(_land_shape(a, gather), a.dtype) for a in arrays], name=name,
        compiler_params=_params(("arbitrary",)))(me.reshape(1).astype(jnp.int32), *arrays)


def _exchange_start(arrays, lands, gather, name):
    n = len(arrays)

    def body(*refs):
        ins, lnd = refs[:n], refs[n:2 * n]
        send_sems, recv_sems = refs[2 * n], refs[2 * n + 1]
        token = refs[-1]
        _, _, _, me = _me()
        for k in range(1, N_DEV):
            peer, peer_idx = _peer(k)
            for a in range(n):
                src = ins[a] if gather else ins[a].at[peer_idx]
                s = a * (N_DEV - 1) + k - 1
                pltpu.make_async_remote_copy(src_ref=src, dst_ref=lnd[a].at[me], send_sem=send_sems.at[s],
                                             recv_sem=recv_sems.at[s], device_id=peer, device_id_type=MESH).start()
        token[...] = jnp.zeros_like(token)

    sems = pltpu.SemaphoreType.DMA((n * (N_DEV - 1),))
    out = pl.pallas_call(
        body, name=name, in_specs=[_HBM] * (2 * n),
        out_shape=(sems, sems) + tuple(pltpu.HBM(a.shape, a.dtype) for a in list(arrays) + list(lands)) + (SDS((8, 128), F32),),
        out_specs=(_SEM, _SEM) + (_HBM,) * (2 * n) + (pl.BlockSpec(memory_space=pltpu.VMEM),),
        input_output_aliases={i: 2 + i for i in range(2 * n)},
        compiler_params=pltpu.CompilerParams(has_side_effects=_EFFECT),
    )(*[pltpu.with_memory_space_constraint(a, pltpu.HBM) for a in list(arrays) + list(lands)])
    return out[0], out[1], list(out[2:2 + n]), list(out[2 + n:2 + 2 * n]), out[-1]


def _exchange_wait(send_sems, recv_sems, arrays, lands, after, gather, name):
    n = len(arrays)

    def body(*refs):
        ins, lnd = refs[:n], refs[n:2 * n]
        send_sems, recv_sems = refs[2 * n], refs[2 * n + 1]
        for k in range(1, N_DEV):
            peer, peer_idx = _peer(k)
            for a in range(n):
                src = ins[a] if gather else ins[a].at[peer_idx]
                s = a * (N_DEV - 1) + k - 1
                cp = pltpu.make_async_remote_copy(src_ref=src, dst_ref=lnd[a].at[peer_idx], send_sem=send_sems.at[s],
                                                  recv_sem=recv_sems.at[s], device_id=peer, device_id_type=MESH)
                cp.wait_send()
                cp.wait_recv()

        refs[-1][...] = jnp.zeros_like(refs[-1])

    after = list(after) if isinstance(after, (list, tuple)) else [after]
    out = pl.pallas_call(
        body, name=name, in_specs=[_HBM] * (2 * n) + [_SEM, _SEM] + [pl.BlockSpec(memory_space=pl.ANY)] * len(after),
        out_shape=tuple(pltpu.HBM(a.shape, a.dtype) for a in list(arrays) + list(lands)) + (SDS((8, 128), F32),),
        out_specs=(_HBM,) * (2 * n) + (pl.BlockSpec(memory_space=pltpu.VMEM),), input_output_aliases={i: i for i in range(2 * n)},
        compiler_params=pltpu.CompilerParams(has_side_effects=_EFFECT),
    )(*arrays, *lands, send_sems, recv_sems, *after)
    return list(out[n:2 * n]), out[-1]


def _adamw_math(w, g, m, v):
    m = ADAM_B1 * m + (1.0 - ADAM_B1) * g
    v = ADAM_B2 * v + (1.0 - ADAM_B2) * (g * g)
    m_hat = m / (1.0 - ADAM_B1 ** ADAM_STEP)
    v_hat = v / (1.0 - ADAM_B2 ** ADAM_STEP)
    delta = -ADAM_LR * (m_hat / (jnp.sqrt(v_hat) + ADAM_EPS) + ADAM_WD * w)
    return delta, m, v


def _adamw(w, m, v, pieces, name):
    rows, cols = w.shape[-2:]
    lead = w.ndim - 2
    tile = rows
    for cand in (256, 176, 128, 64, 16):
        if rows > cand and rows % cand == 0:
            tile = cand
            break

    def body(w_ref, m_ref, v_ref, p_ref, g_ref, d_ref, mo_ref, vo_ref):
        g = p_ref[0].astype(F32)
        for p in range(1, N_DEV):
            g = g + p_ref[p].astype(F32)
        g_ref[...] = g
        d_ref[...], mo_ref[...], vo_ref[...] = _adamw_math(w_ref[...], g, m_ref[...], v_ref[...])

    blk = pl.BlockSpec((None,) * lead + (tile, cols), lambda i: (0,) * lead + (i, 0))
    return pl.pallas_call(
        body, grid=(rows // tile,), in_specs=[blk, blk, blk, pl.BlockSpec((N_DEV, tile, cols), lambda i: (0, i, 0))],
        out_specs=[blk] * 4, out_shape=[SDS(w.shape, F32)] * 4, name=name,
        compiler_params=_params(("parallel",)))(w, m, v, pieces)


def _sum_pieces(p_ref):
    g = p_ref[0].astype(F32)
    for p in range(1, N_DEV):
        g = g + p_ref[p].astype(F32)
    return g


def _adamw_s5_mat(w, m, v, g, name):
    _, ndir, groups, b, c = w.shape
    per_dir = groups // 8

    def body(w_ref, m_ref, v_ref, g_ref, d_ref, mo_ref, vo_ref):
        d_ref[...], mo_ref[...], vo_ref[...] = _adamw_math(w_ref[...], g_ref[...], m_ref[...], v_ref[...])

    blk = pl.BlockSpec((None, None, 8, b, c), lambda i: (0, i // per_dir, i % per_dir, 0, 0))
    return pl.pallas_call(
        body, grid=(ndir * per_dir,), in_specs=[blk] * 4, out_specs=[blk] * 3, out_shape=[SDS(w.shape, F32)] * 3, name=name,
        compiler_params=_params(("parallel",)))(w, m, v, g)


VEC_ROWS = ['ffn1_pre_g', 'ffn1_post_g', 'mix_pre_g', 'mix_post_g', 'ffn2_pre_g', 'ffn2_post_g', 'final_g',
            ('na_out_g', 's5_out_g'), ('s5_d', 's5_b_glu')]
VEC_NAMES = [n for row in VEC_ROWS for n in ((row,) if isinstance(row, str) else row)]
VEC_PACK_ROWS = 16


def _pack_vectors(grads):
    def body(*refs):
        o_ref = refs[-1]
        o_ref[...] = jnp.zeros_like(o_ref)
        k = 0
        for i, row in enumerate(VEC_ROWS):
            if isinstance(row, str):
                o_ref[i:i + 1, :] = refs[k][...]
                k += 1
            else:
                o_ref[i:i + 1, 0:NA_WIDTH] = refs[k][...]
                o_ref[i:i + 1, NA_WIDTH:] = refs[k + 1][...]
                k += 2

    return pl.pallas_call(body, out_shape=SDS((VEC_PACK_ROWS, D_MODEL), F32), name="pack_vectors",
                          compiler_params=_params())(*[grads[n] for n in VEC_NAMES])


def _sum8(pieces, name):
    def body(p_ref, o_ref):
        o_ref[...] = _sum_pieces(p_ref)

    return pl.pallas_call(body, out_shape=SDS(pieces.shape[1:], F32), name=name, compiler_params=_params())(pieces)


def _adamw_small(packed8, vec_wmv, others):
    n_vec, n_oth = len(VEC_NAMES), len(others)

    def body(*refs):
        p_ref = refs[0]
        ins = refs[1:1 + 3 * n_vec + 4 * n_oth]
        outs = refs[1 + 3 * n_vec + 4 * n_oth:]
        gsum = _sum_pieces(p_ref)
        k = 0
        for i, row in enumerate(VEC_ROWS):
            parts = [(row, gsum[i:i + 1, :])] if isinstance(row, str) else \
                [(row[0], gsum[i:i + 1, 0:NA_WIDTH]), (row[1], gsum[i:i + 1, NA_WIDTH:])]
            for _, g in parts:
                w_ref, m_ref, v_ref = ins[3 * k:3 * k + 3]
                outs[4 * k][...] = g
                outs[4 * k + 1][...], outs[4 * k + 2][...], outs[4 * k + 3][...] = _adamw_math(w_ref[...], g, m_ref[...], v_ref[...])
                k += 1
        for j in range(n_oth):
            w_ref, m_ref, v_ref, g_ref = ins[3 * n_vec + 4 * j:3 * n_vec + 4 * j + 4]
            g = _sum_pieces(g_ref)
            g = g[tuple(slice(0, s) for s in w_ref.shape[1:])].reshape(w_ref.shape)
            o = outs[4 * (n_vec + j):4 * (n_vec + j) + 4]
            o[0][...] = g
            o[1][...], o[2][...], o[3][...] = _adamw_math(w_ref[...], g, m_ref[...], v_ref[...])

    args, out_shape = [packed8], []
    for w, m, v in vec_wmv:
        args += [w, m, v]
        out_shape += [SDS(w.shape, F32)] * 4
    for w, m, v, g in others:
        args += [w, m, v, g]
        out_shape += [SDS(w.shape, F32)] * 4
    return pl.pallas_call(body, out_shape=out_shape, name="adamw_small", compiler_params=_params())(*args)


def _perm_rows(x):
    return x.reshape(SCAN_BLOCKS, SCAN_T, x.shape[-1]).transpose(1, 0, 2).reshape(SEQ, x.shape[-1])


def _unperm_rows(x):
    return x.reshape(SCAN_T, SCAN_BLOCKS, x.shape[-1]).transpose(1, 0, 2).reshape(SEQ, x.shape[-1])


def _block_diag(x):
    eye = np.eye(8, dtype=bool)[None, None, :, None, :, None]
    full = jnp.where(eye, x[:, :, :, :, None, :], 0.0)
    return full.reshape(2, S5_CHUNKS, 8 * x.shape[3], 8 * x.shape[4])


def _diag_blocks(x, r, c):
    x6 = x.reshape(2, S5_CHUNKS, 8, r, 8, c)
    return jnp.stack([x6[:, :, g, :, g, :] for g in range(8)], axis=2)


def _dep(x, token):
    return x if token is None else x + token


def _local_step(x, target, get_w, small, emit):
    bias = _rpb_expand(small["na_rpb"][0])
    lr = small["s5_lam_re"].reshape(64, S5_STATE)
    li = small["s5_lam_im"].reshape(64, S5_STATE)
    logdt = small["s5_log_dt"].reshape(64, 1)
    b_t = [small[n].reshape(64, S5_STATE, S5_GROUP).transpose(0, 2, 1) for n in ("s5_b_re", "s5_b_im")]
    lbr, lbi, bbr, bbi = _s5_prep(lr, li, logdt, b_t[0], b_t[1])
    are = lbr.reshape(2, S5_CHUNKS, 1, ST_W)
    aim = lbi.reshape(2, S5_CHUNKS, 1, ST_W)
    bre = _block_diag(bbr.reshape(2, S5_CHUNKS, 8, S5_GROUP, S5_STATE)).astype(BF16)
    bim = _block_diag(bbi.reshape(2, S5_CHUNKS, 8, S5_GROUP, S5_STATE)).astype(BF16)
    c_t = [small[n].reshape(2, S5_CHUNKS, 8, S5_GROUP, S5_STATE).transpose(0, 1, 2, 4, 3) for n in ("s5_c_re", "s5_c_im")]
    cre = _block_diag(c_t[0]).astype(BF16)
    cim = _block_diag(c_t[1]).astype(BF16)
    tgt = jnp.concatenate([jnp.zeros((N_META, D_MODEL), F32), target], axis=0)

    wts = dict(get_w("ffn1", [bias, are, aim, bre, bim, cre, cim, tgt]))
    h0 = jnp.concatenate([wts["meta_tokens"], x], axis=0)
    a1 = _prenorm(h0, _dep(small["ffn1_pre_g"], wts.get("token")))
    gate1, up1, f1 = _ffn_fwd(a1, wts["ffn1_w_gate"], wts["ffn1_w_up"], wts["ffn1_w_down"], "ffn1_fwd")
    h1, a2 = _post_pre(f1, h0, small["ffn1_post_g"], small["mix_pre_g"], 0.5, "post_pre1")
    wts.update(get_w("w_in", a2))
    proj = _proj_fwd(a2, wts["w_in"])
    qkv = proj[:6].reshape(3, 2, SEQ, 4, HEAD_DIM).transpose(0, 1, 3, 2, 4).reshape(3, HEADS, SEQ, HEAD_DIM)
    u = proj[6:].transpose(1, 0, 2).reshape(SEQ, S5_WIDTH)
    o3 = _na_fwd(qkv[0], qkv[1], qkv[2], bias)
    ona = o3.transpose(1, 0, 2).reshape(SEQ, NA_WIDTH)
    u_p = _perm_rows(u)
    sr, si, y2 = _s5_scan_fwd(u_p, bre, bim, are, aim, cre, cim)
    wts.update(get_w("mix", y2))
    os5_p, ypre_p = _s5_glu_fwd(u_p, y2, small["s5_d"], wts["s5_w_glu"], small["s5_b_glu"])
    os5 = _unperm_rows(os5_p)

    mix = _mix_out_fwd(ona, os5, small["na_out_g"], small["s5_out_g"], wts["w_out"])
    h2, a3 = _post_pre(mix, h1, small["mix_post_g"], small["ffn2_pre_g"], 1.0, "post_pre2")
    wts.update(get_w("ffn2", a3))
    gate2, up2, f2 = _ffn_fwd(a3, wts["ffn2_w_gate"], wts["ffn2_w_up"], wts["ffn2_w_down"], "ffn2_fwd")
    loss8, dh3, df2, g_final, g_ffn2_post = _final_loss(f2, h2, small["ffn2_post_g"], small["final_g"], tgt)

    da3, dwg2, dwu2, dwd2 = _ffn_bwd(df2, a3, gate2, up2, wts["ffn2_w_gate"], wts["ffn2_w_up"], wts["ffn2_w_down"], "ffn2_bwd")
    tok = emit("ffn2", {"ffn2_w_gate": dwg2, "ffn2_w_up": dwu2, "ffn2_w_down": dwd2})
    dh2, dmix, g_ffn2_pre, g_mix_post = _bwd_pre_post(da3, h2, _dep(small["ffn2_pre_g"], tok), dh3, mix, small["mix_post_g"], 1.0,
                                                      "bwd_pre_post2")
    dona, dos5, dwout, g_na_out, g_s5_out = _mix_out_bwd(dmix, ona, os5, small["na_out_g"], small["s5_out_g"], wts["w_out"])

    dypre_p, du_skip_p, dwglu, g_b_glu, g_s5_d = _s5_glu_bwd(_perm_rows(dos5), ypre_p, u_p, small["s5_d"], wts["s5_w_glu"],
                                                             small["s5_b_glu"])
    tok = emit("mix", {"s5_w_glu": dwglu.reshape(N_DEV, S5_WIDTH // N_DEV, S5_WIDTH).astype(BF16),
                       "w_out": dwout.reshape(N_DEV, D_MODEL // N_DEV, D_MODEL).astype(BF16)})
    du_p, dbr, dbi, dcr, dci, dar, dai = _s5_scan_bwd(dypre_p, du_skip_p, u_p, sr, si, bre, bim, _dep(are, tok), aim, cre, cim)
    du = _unperm_rows(du_p)
    dbbr = _diag_blocks(dbr, S5_GROUP, S5_STATE).reshape(64, S5_GROUP, S5_STATE)
    dbbi = _diag_blocks(dbi, S5_GROUP, S5_STATE).reshape(64, S5_GROUP, S5_STATE)
    g_lr, g_li, g_dt, g_br, g_bi = _s5_prep_bwd(lr, li, logdt, b_t[0], b_t[1], dar.reshape(64, S5_STATE),
                                                dai.reshape(64, S5_STATE), dbbr, dbbi)
    g_c = [_diag_blocks(d, S5_STATE, S5_GROUP).transpose(0, 1, 2, 4, 3).reshape(2 * S5_GROUPS, S5_GROUP, S5_STATE)
           for d in (dcr, dci)]

    do3 = dona.reshape(SEQ, HEADS, HEAD_DIM).transpose(1, 0, 2)
    dq, dk, dv, dbias = _na_bwd(qkv[0], qkv[1], qkv[2], bias, do3)
    g_rpb = _rpb_reduce(dbias)
    dense = jnp.stack([g.reshape(2 * S5_GROUPS, S5_STATE * S5_GROUP) for g in (g_br.transpose(0, 2, 1), g_bi.transpose(0, 2, 1), *g_c)])
    tok = emit("small", {"dense": dense, "na_rpb": g_rpb,
                         "s5_lam_re": g_lr.reshape(2, S5_GROUPS, S5_STATE), "s5_lam_im": g_li.reshape(2, S5_GROUPS, S5_STATE),
                         "s5_log_dt": g_dt.reshape(2, S5_GROUPS)})
    dqkv = jnp.stack([dq, dk, dv]).reshape(3, 2, 4, SEQ, HEAD_DIM).transpose(0, 1, 3, 2, 4).reshape(6, SEQ, IN_SHARD)
    dproj = jnp.concatenate([dqkv, du.reshape(SEQ, 2, IN_SHARD).transpose(1, 0, 2)], axis=0).astype(BF16)
    da2, dwin = _proj_bwd(dproj, a2, wts["w_in"])
    tok2 = emit("w_in", {"w_in": dwin})
    tok = tok if tok2 is None else tok + tok2
    dh1, df1, g_mix_pre, g_ffn1_post = _bwd_pre_post(da2, h1, _dep(small["mix_pre_g"], tok), dh2, f1, small["ffn1_post_g"], 0.5,
                                                     "bwd_pre_post1")
    da1, dwg1, dwu1, dwd1 = _ffn_bwd(df1, a1, gate1, up1, wts["ffn1_w_gate"], wts["ffn1_w_up"], wts["ffn1_w_down"], "ffn1_bwd")
    emit("ffn1", {"ffn1_w_gate": dwg1, "ffn1_w_up": dwu1, "ffn1_w_down": dwd1})
    dh0, g_ffn1_pre = _bwd_pre_only(da1, h0, small["ffn1_pre_g"], dh1)

    vec_g = {
        "ffn1_pre_g": g_ffn1_pre, "ffn1_post_g": g_ffn1_post, "mix_pre_g": g_mix_pre, "s5_d": g_s5_d, "s5_b_glu": g_b_glu,
        "na_out_g": g_na_out, "s5_out_g": g_s5_out, "mix_post_g": g_mix_post,
        "ffn2_pre_g": g_ffn2_pre, "ffn2_post_g": g_ffn2_post, "final_g": g_final,
    }
    return loss8[0, 0], dh0[N_META:], dh0[:N_META], vec_g


WEIGHT_NAMES = ['meta_tokens', 'ffn1_pre_g', 'ffn1_post_g', 'ffn1_w_gate', 'ffn1_w_up', 'ffn1_w_down', 'mix_pre_g', 'w_in',
                'na_rpb', 's5_lam_re', 's5_lam_im', 's5_log_dt', 's5_b_re', 's5_b_im', 's5_c_re', 's5_c_im', 's5_d',
                's5_w_glu', 's5_b_glu', 'na_out_g', 's5_out_g', 'w_out', 'mix_post_g', 'ffn2_pre_g', 'ffn2_post_g',
                'ffn2_w_gate', 'ffn2_w_up', 'ffn2_w_down', 'final_g']
BIG_NAMES = ['ffn1_w_gate', 'ffn1_w_up', 'ffn1_w_down', 'w_in', 's5_w_glu', 'w_out', 'ffn2_w_gate', 'ffn2_w_up', 'ffn2_w_down']
SMALL_NAMES = [n for n in WEIGHT_NAMES if n not in BIG_NAMES and n != 'meta_tokens']
WHOLE_NAMES = ['na_rpb', 's5_lam_re', 's5_lam_im', 's5_log_dt']
LEAD_NAMES = ['s5_b_re', 's5_b_im', 's5_c_re', 's5_c_im']


def kernel(x, meta_tokens, ffn1_pre_g, ffn1_post_g, ffn1_w_gate, ffn1_w_up, ffn1_w_down, mix_pre_g, w_in, na_rpb, s5_lam_re, s5_lam_im, s5_log_dt, s5_b_re, s5_b_im, s5_c_re, s5_c_im, s5_d, s5_w_glu, s5_b_glu, na_out_g, s5_out_g, w_out, mix_post_g, ffn2_pre_g, ffn2_post_g, ffn2_w_gate, ffn2_w_up, ffn2_w_down, final_g, loss_target, m_meta_tokens, m_ffn1_pre_g, m_ffn1_post_g, m_ffn1_w_gate, m_ffn1_w_up, m_ffn1_w_down, m_mix_pre_g, m_w_in, m_na_rpb, m_s5_lam_re, m_s5_lam_im, m_s5_log_dt, m_s5_b_re, m_s5_b_im, m_s5_c_re, m_s5_c_im, m_s5_d, m_s5_w_glu, m_s5_b_glu, m_na_out_g, m_s5_out_g, m_w_out, m_mix_post_g, m_ffn2_pre_g, m_ffn2_post_g, m_ffn2_w_gate, m_ffn2_w_up, m_ffn2_w_down, m_final_g, v_meta_tokens, v_ffn1_pre_g, v_ffn1_post_g, v_ffn1_w_gate, v_ffn1_w_up, v_ffn1_w_down, v_mix_pre_g, v_w_in, v_na_rpb, v_s5_lam_re, v_s5_lam_im, v_s5_log_dt, v_s5_b_re, v_s5_b_im, v_s5_c_re, v_s5_c_im, v_s5_d, v_s5_w_glu, v_s5_b_glu, v_na_out_g, v_s5_out_g, v_w_out, v_mix_post_g, v_ffn2_pre_g, v_ffn2_post_g, v_ffn2_w_gate, v_ffn2_w_up, v_ffn2_w_down, v_final_g):
    args = dict(locals())
    w = {n: args[n] for n in WEIGHT_NAMES}
    m = {n: args["m_" + n] for n in WEIGHT_NAMES}
    v = {n: args["v_" + n] for n in WEIGHT_NAMES}

    small = {n: w[n] for n in SMALL_NAMES}

    pending = {}

    def start(group, names, arrays, gather):
        lands = _place_own(arrays, gather, "own_" + group)
        send_sems, recv_sems, arrays, lands, token = _exchange_start(arrays, lands, gather, "start_" + group)
        pending[group] = (names, send_sems, recv_sems, arrays, lands, gather)
        return token

    def finish(group, after):
        names, send_sems, recv_sems, arrays, lands, gather = pending.pop(group)
        lands, token = _exchange_wait(send_sems, recv_sems, arrays, lands, after, gather, "wait_" + group)
        return dict(zip(names, lands)), token

    first = ["ffn1_w_gate", "ffn1_w_up", "ffn1_w_down"]
    start("ffn1", first + ["meta_tokens"], [w[n][0].astype(BF16) for n in first] + [w["meta_tokens"]], True)

    def get_w(group, after):
        got, token = finish(group, after)
        if group == "ffn1":
            got["meta_tokens"] = got["meta_tokens"].transpose(1, 0, 2).reshape(N_META, D_MODEL)
            got["token"] = sum(start(g, names, [(w[n][0] + token[0, 0]).astype(BF16) for n in names], True)[0, 0]
                               for g, names in (("w_in", ["w_in"]), ("mix", ["s5_w_glu", "w_out"]),
                                                ("ffn2", ["ffn2_w_gate", "ffn2_w_up", "ffn2_w_down"])))
        if group == "mix":
            got = {"s5_w_glu": got["s5_w_glu"].reshape(S5_WIDTH, S5_WIDTH), "w_out": got["w_out"].reshape(D_MODEL, D_MODEL)}
        return got

    def emit(group, grads):
        return start("g_" + group, list(grads), list(grads.values()), group == "small")[0, 0]

    loss_local, grad_x, gmeta, vec_g = _local_step(x[0], loss_target[0], get_w, small, emit)
    loss = lax.psum(loss_local, AXES)
    res = {}

    def update_shard(n, pieces):
        res[n] = list(_adamw(w[n], m[n], v[n], pieces, "adamw_" + n))

    for group in ("g_ffn2", "g_mix", "g_w_in"):
        for n, pieces in finish(group, grad_x)[0].items():
            update_shard(n, pieces)
    g8 = finish("g_small", grad_x)[0]
    dense = _sum8(g8["dense"], "sum_dense")
    for i, n in enumerate(LEAD_NAMES):
        g = dense[i].reshape(w[n].shape)
        res[n] = [g] + list(_adamw_s5_mat(w[n], m[n], v[n], g, "adamw_" + n))

    done = [res[n][1] for n in ("ffn2_w_gate", "ffn2_w_up", "ffn2_w_down", "w_in", "w_out", "s5_w_glu") + tuple(LEAD_NAMES)]
    packed8, gmeta8 = _exchange([_pack_vectors(vec_g), gmeta], True, "gather_vectors", after=done)
    for n, pieces in finish("g_ffn1", packed8)[0].items():
        update_shard(n, pieces)
    _, _, _, me = _me()
    update_shard("meta_tokens", lax.dynamic_slice_in_dim(gmeta8, me * (D_MODEL // N_DEV), D_MODEL // N_DEV, axis=2))

    outs = _adamw_small(packed8, [(w[n], m[n], v[n]) for n in VEC_NAMES], [(w[n], m[n], v[n], g8[n]) for n in WHOLE_NAMES])
    for i, n in enumerate(VEC_NAMES + WHOLE_NAMES):
        res[n] = list(outs[4 * i:4 * i + 4])

    out = [loss, grad_x[None]]
    for kind in range(4):
        out += [res[n][kind] for n in WEIGHT_NAMES]
    return tuple(out)
```

```python
import functools
import math

import numpy as np
import jax
import jax.numpy as jnp
from jax import lax
from jax.experimental import pallas as pl
from jax.experimental.pallas import tpu as pltpu

F32 = jnp.float32
BF16 = jnp.bfloat16
SDS = jax.ShapeDtypeStruct

D_MODEL = 1024
N_TOK = 2048
N_META = 16
SEQ = N_TOK + N_META
ROW_TILE = 688
N_ROW_TILES = SEQ // ROW_TILE
N_DEV = 8
D_FF = 2816
FF_SHARD = D_FF // N_DEV
IN_SHARD = 256
NA_WIDTH = 512
S5_WIDTH = 512
HEADS = 8
HEAD_DIM = 64
GRID_W = 64
GRID_ROWS = N_TOK // GRID_W
KH = 8
KW = 16
NA_RB = 4
NA_KR = KH + NA_RB - 1
NA_BLOCKS = GRID_ROWS // NA_RB
NA_QB = NA_RB * GRID_W
NA_KB = NA_KR * GRID_W
NA_TYPES = 3
S5_GROUPS = 32
S5_GROUP = 16
S5_STATE = 64
S5_CHUNKS = 4
CH_W = S5_WIDTH // S5_CHUNKS
ST_W = S5_GROUPS * S5_STATE // S5_CHUNKS
SCAN_BLOCKS = 8
SCAN_T = SEQ // SCAN_BLOCKS
RMS_EPS = 1e-6
NEG_INF = -1e30
ATT_SCALE = HEAD_DIM ** -0.5
ADAM_LR, ADAM_B1, ADAM_B2, ADAM_EPS, ADAM_WD, ADAM_STEP = 0.001, 0.9, 0.999, 1e-08, 0.01, 10
VMEM_LIMIT = 56 * 1024 * 1024
MESH = pl.DeviceIdType.MESH
AXES = ("x", "y", "c")


def _params(sem=None):
    return pltpu.CompilerParams(dimension_semantics=sem, vmem_limit_bytes=VMEM_LIMIT)


def _dot(a, b):
    return jnp.dot(a, b, preferred_element_type=F32)


def _dot_nt(a, b):
    return lax.dot_general(a, b, (((1,), (1,)), ((), ())), preferred_element_type=F32)


def _dot_tn(a, b):
    return lax.dot_general(a, b, (((0,), (0,)), ((), ())), preferred_element_type=F32)


def _rstd(x):
    return lax.rsqrt(jnp.mean(x * x, axis=-1, keepdims=True) + RMS_EPS)


def _rms_bwd(x, r, g, dy):
    dyg = dy * g
    xr = x * r
    dx = r * (dyg - xr * jnp.mean(dyg * xr, axis=-1, keepdims=True))
    return dx, dy * xr


def _rows(i, size=ROW_TILE):
    return pl.ds(pl.multiple_of(i * size, 16), size)


def _row_spec(width):
    return pl.BlockSpec((ROW_TILE, width), lambda i: (i, 0))


def _fix_spec(shape):
    return pl.BlockSpec(shape, lambda i: (0,) * len(shape))


def _split3(x):
    hi = x.astype(BF16)
    r1 = x - hi.astype(F32)
    mid = r1.astype(BF16)
    lo = (r1 - mid.astype(F32)).astype(BF16)
    return hi, mid, lo


def _prenorm(x, g):
    def body(x_ref, g_ref, a_ref):
        xv = x_ref[...]
        a_ref[...] = (xv * _rstd(xv) * g_ref[...]).astype(BF16)

    return pl.pallas_call(
        body, grid=(N_ROW_TILES,), in_specs=[_row_spec(D_MODEL), _fix_spec((1, D_MODEL))],
        out_specs=_row_spec(D_MODEL), out_shape=SDS((SEQ, D_MODEL), BF16), name="prenorm",
        compiler_params=_params(("parallel",)))(x, g)


def _post_pre(f, hres, g_post, g_next, scale, name):
    def body(f_ref, h_ref, gp_ref, gn_ref, ho_ref, a_ref):
        fv = f_ref[...]
        h = h_ref[...] + scale * (fv * _rstd(fv) * gp_ref[...])
        ho_ref[...] = h
        a_ref[...] = (h * _rstd(h) * gn_ref[...]).astype(BF16)

    return pl.pallas_call(
        body, grid=(N_ROW_TILES,),
        in_specs=[_row_spec(D_MODEL), _row_spec(D_MODEL), _fix_spec((1, D_MODEL)), _fix_spec((1, D_MODEL))],
        out_specs=[_row_spec(D_MODEL), _row_spec(D_MODEL)],
        out_shape=[SDS((SEQ, D_MODEL), F32), SDS((SEQ, D_MODEL), BF16)], name=name,
        compiler_params=_params(("parallel",)))(f, hres, g_post, g_next)


def _final_loss(f2, h2, g_post, g_final, target):
    def body(f_ref, h_ref, gp_ref, gf_ref, t_ref, loss_ref, dh_ref, df_ref, dgf_ref, dgp_ref):
        i = pl.program_id(0)
        fv = f_ref[...]
        r1 = _rstd(fv)
        gp = gp_ref[...]
        h3 = h_ref[...] + 0.5 * (fv * r1 * gp)
        r2 = _rstd(h3)
        gf = gf_ref[...]
        y = h3 * r2 * gf
        row = lax.broadcasted_iota(jnp.int32, (ROW_TILE, 1), 0) + i * ROW_TILE
        err = jnp.where(row >= N_META, y - t_ref[...], 0.0)
        part = 0.5 * jnp.sum(jnp.mean(err * err, axis=-1, keepdims=True))
        dy = err * (1.0 / D_MODEL)
        dh3, dgf = _rms_bwd(h3, r2, gf, dy)
        dh_ref[...] = dh3
        df, dgp = _rms_bwd(fv, r1, gp, 0.5 * dh3)
        df_ref[...] = df.astype(BF16)

        @pl.when(i == 0)
        def _():
            loss_ref[...] = jnp.zeros_like(loss_ref)
            dgf_ref[...] = jnp.zeros_like(dgf_ref)
            dgp_ref[...] = jnp.zeros_like(dgp_ref)

        loss_ref[...] += part
        dgf_ref[...] += jnp.sum(dgf, axis=0, keepdims=True)
        dgp_ref[...] += jnp.sum(dgp, axis=0, keepdims=True)

    gain = _fix_spec((1, D_MODEL))
    return pl.pallas_call(
        body, grid=(N_ROW_TILES,),
        in_specs=[_row_spec(D_MODEL), _row_spec(D_MODEL), gain, gain, _row_spec(D_MODEL)],
        out_specs=[_fix_spec((8, 128)), _row_spec(D_MODEL), _row_spec(D_MODEL), gain, gain],
        out_shape=[SDS((8, 128), F32), SDS((SEQ, D_MODEL), F32), SDS((SEQ, D_MODEL), BF16),
                   SDS((1, D_MODEL), F32), SDS((1, D_MODEL), F32)],
        name="final_loss", compiler_params=_params(("arbitrary",)))(f2, h2, g_post, g_final, target)


def _bwd_pre_post(da, h, g_pre, dh_res, fprev, g_post, scale, name):
    def body(da_ref, h_ref, gpre_ref, dhr_ref, f_ref, gpost_ref, dh_ref, df_ref, dgpre_ref, dgpost_ref):
        i = pl.program_id(0)
        hv = h_ref[...]
        dxa, dgpre = _rms_bwd(hv, _rstd(hv), gpre_ref[...], da_ref[...])
        dh = dhr_ref[...] + dxa
        dh_ref[...] = dh
        fv = f_ref[...]
        df, dgpost = _rms_bwd(fv, _rstd(fv), gpost_ref[...], scale * dh)
        df_ref[...] = df.astype(BF16)

        @pl.when(i == 0)
        def _():
            dgpre_ref[...] = jnp.zeros_like(dgpre_ref)
            dgpost_ref[...] = jnp.zeros_like(dgpost_ref)

        dgpre_ref[...] += jnp.sum(dgpre, axis=0, keepdims=True)
        dgpost_ref[...] += jnp.sum(dgpost, axis=0, keepdims=True)

    gain = _fix_spec((1, D_MODEL))
    row = _row_spec(D_MODEL)
    return pl.pallas_call(
        body, grid=(N_ROW_TILES,), in_specs=[row, row, gain, row, row, gain],
        out_specs=[row, row, gain, gain],
        out_shape=[SDS((SEQ, D_MODEL), F32), SDS((SEQ, D_MODEL), BF16), SDS((1, D_MODEL), F32), SDS((1, D_MODEL), F32)],
        name=name, compiler_params=_params(("arbitrary",)))(da, h, g_pre, dh_res, fprev, g_post)


def _bwd_pre_only(da, h, g_pre, dh_res):
    def body(da_ref, h_ref, gpre_ref, dhr_ref, dh_ref, dgpre_ref):
        i = pl.program_id(0)
        hv = h_ref[...]
        dxa, dgpre = _rms_bwd(hv, _rstd(hv), gpre_ref[...], da_ref[...])
        dh_ref[...] = dhr_ref[...] + dxa

        @pl.when(i == 0)
        def _():
            dgpre_ref[...] = jnp.zeros_like(dgpre_ref)

        dgpre_ref[...] += jnp.sum(dgpre, axis=0, keepdims=True)

    gain = _fix_spec((1, D_MODEL))
    row = _row_spec(D_MODEL)
    return pl.pallas_call(
        body, grid=(N_ROW_TILES,), in_specs=[row, row, gain, row], out_specs=[row, gain],
        out_shape=[SDS((SEQ, D_MODEL), F32), SDS((1, D_MODEL), F32)],
        name="bwd_pre_only", compiler_params=_params(("arbitrary",)))(da, h, g_pre, dh_res)


def _ffn_fwd(a, wg, wu, wd, name):
    def body(a_ref, wg_ref, wu_ref, wd_ref, gate_ref, up_ref, f_ref):
        j = pl.program_id(0)

        def tile(i, carry):
            rows = _rows(i)
            at = a_ref[rows, :]
            gate = _dot(at, wg_ref[...])
            up = _dot(at, wu_ref[...])
            gate_ref[rows, :] = gate
            up_ref[rows, :] = up
            act = (gate * jax.nn.sigmoid(gate) * up).astype(BF16)
            contrib = _dot(act, wd_ref[...])

            @pl.when(j == 0)
            def _():
                f_ref[rows, :] = contrib

            @pl.when(j != 0)
            def _():
                f_ref[rows, :] += contrib

            return carry

        lax.fori_loop(0, N_ROW_TILES, tile, 0)

    shard_cols = pl.BlockSpec((None, D_MODEL, FF_SHARD), lambda j: (j, 0, 0))
    shard_rows = pl.BlockSpec((None, FF_SHARD, D_MODEL), lambda j: (j, 0, 0))
    hid = pl.BlockSpec((None, SEQ, FF_SHARD), lambda j: (j, 0, 0))
    full = pl.BlockSpec((SEQ, D_MODEL), lambda j: (0, 0))
    return pl.pallas_call(
        body, grid=(N_DEV,), in_specs=[full, shard_cols, shard_cols, shard_rows], out_specs=[hid, hid, full],
        out_shape=[SDS((N_DEV, SEQ, FF_SHARD), F32), SDS((N_DEV, SEQ, FF_SHARD), F32), SDS((SEQ, D_MODEL), F32)],
        name=name, compiler_params=_params(("arbitrary",)))(a, wg, wu, wd)


def _ffn_bwd(df, a, gate, up, wg, wu, wd, name):
    def body(df_ref, a_ref, gate_ref, up_ref, wg_ref, wu_ref, wd_ref, da_ref, dwg_ref, dwu_ref, dwd_ref,
             acc_g, acc_u, acc_d):
        j = pl.program_id(0)

        def tile(i, carry):
            rows = _rows(i)
            dft = df_ref[rows, :]
            at = a_ref[rows, :]
            gate = gate_ref[rows, :]
            up = up_ref[rows, :]
            dact = _dot_nt(dft, wd_ref[...])
            sig = jax.nn.sigmoid(gate)
            silu = gate * sig
            dgate = (dact * up * (sig * (1.0 + gate * (1.0 - sig)))).astype(BF16)
            dup = (dact * silu).astype(BF16)
            act = (silu * up).astype(BF16)
            dwd = _dot_tn(act, dft)
            dwg = _dot_tn(at, dgate)
            dwu = _dot_tn(at, dup)
            dat = _dot_nt(dgate, wg_ref[...]) + _dot_nt(dup, wu_ref[...])

            @pl.when(i == 0)
            def _():
                acc_d[...] = dwd
                acc_g[...] = dwg
                acc_u[...] = dwu

            @pl.when(i != 0)
            def _():
                acc_d[...] += dwd
                acc_g[...] += dwg
                acc_u[...] += dwu

            @pl.when(j == 0)
            def _():
                da_ref[rows, :] = dat

            @pl.when(j != 0)
            def _():
                da_ref[rows, :] += dat

            return carry

        lax.fori_loop(0, N_ROW_TILES, tile, 0)
        dwg_ref[...] = acc_g[...].astype(BF16)
        dwu_ref[...] = acc_u[...].astype(BF16)
        dwd_ref[...] = acc_d[...].astype(BF16)

    shard_cols = pl.BlockSpec((None, D_MODEL, FF_SHARD), lambda j: (j, 0, 0))
    shard_rows = pl.BlockSpec((None, FF_SHARD, D_MODEL), lambda j: (j, 0, 0))
    hid = pl.BlockSpec((None, SEQ, FF_SHARD), lambda j: (j, 0, 0))
    full = pl.BlockSpec((SEQ, D_MODEL), lambda j: (0, 0))
    return pl.pallas_call(
        body, grid=(N_DEV,), in_specs=[full, full, hid, hid, shard_cols, shard_cols, shard_rows],
        out_specs=[full, shard_cols, shard_cols, shard_rows],
        out_shape=[SDS((SEQ, D_MODEL), F32), SDS((N_DEV, D_MODEL, FF_SHARD), BF16),
                   SDS((N_DEV, D_MODEL, FF_SHARD), BF16), SDS((N_DEV, FF_SHARD, D_MODEL), BF16)],
        scratch_shapes=[pltpu.VMEM((D_MODEL, FF_SHARD), F32), pltpu.VMEM((D_MODEL, FF_SHARD), F32),
                        pltpu.VMEM((FF_SHARD, D_MODEL), F32)],
        name=name, compiler_params=_params(("arbitrary",)))(df, a, gate, up, wg, wu, wd)


def _proj_fwd(a, w):
    def body(a_ref, w_ref, o_ref):
        def tile(i, carry):
            rows = _rows(i)
            o_ref[rows, :] = _dot(a_ref[rows, :], w_ref[...])
            return carry

        lax.fori_loop(0, N_ROW_TILES, tile, 0)

    return pl.pallas_call(
        body, grid=(N_DEV,),
        in_specs=[pl.BlockSpec((SEQ, D_MODEL), lambda j: (0, 0)), pl.BlockSpec((None, D_MODEL, IN_SHARD), lambda j: (j, 0, 0))],
        out_specs=pl.BlockSpec((None, SEQ, IN_SHARD), lambda j: (j, 0, 0)),
        out_shape=SDS((N_DEV, SEQ, IN_SHARD), F32), name="proj_fwd",
        compiler_params=_params(("parallel",)))(a, w)


def _proj_bwd(dproj, a, w):
    def body(dp_ref, a_ref, w_ref, da_ref, dw_ref, acc):
        j = pl.program_id(0)

        def tile(i, carry):
            rows = _rows(i)
            dpt = dp_ref[rows, :]
            dw = _dot_tn(a_ref[rows, :], dpt)
            dat = _dot_nt(dpt, w_ref[...])

            @pl.when(i == 0)
            def _():
                acc[...] = dw

            @pl.when(i != 0)
            def _():
                acc[...] += dw

            @pl.when(j == 0)
            def _():
                da_ref[rows, :] = dat

            @pl.when(j != 0)
            def _():
                da_ref[rows, :] += dat

            return carry

        lax.fori_loop(0, N_ROW_TILES, tile, 0)
        dw_ref[...] = acc[...].astype(BF16)

    full = pl.BlockSpec((SEQ, D_MODEL), lambda j: (0, 0))
    wspec = pl.BlockSpec((None, D_MODEL, IN_SHARD), lambda j: (j, 0, 0))
    return pl.pallas_call(
        body, grid=(N_DEV,),
        in_specs=[pl.BlockSpec((None, SEQ, IN_SHARD), lambda j: (j, 0, 0)), full, wspec],
        out_specs=[full, wspec],
        out_shape=[SDS((SEQ, D_MODEL), F32), SDS((N_DEV, D_MODEL, IN_SHARD), BF16)],
        scratch_shapes=[pltpu.VMEM((D_MODEL, IN_SHARD), F32)],
        name="proj_bwd", compiler_params=_params(("arbitrary",)))(dproj, a, w)


def _na_consts():
    c = np.arange(GRID_W)
    col_start = np.clip(c - KW // 2, 0, GRID_W - KW)
    col_in = (c[None, :] >= col_start[:, None]) & (c[None, :] < col_start[:, None] + KW)
    dc = np.clip(c[None, :] - c[:, None] + KW - 1, 0, 2 * KW - 2)
    onehot = np.zeros((128, GRID_W * GRID_W), np.float32)
    qq, kk = np.meshgrid(c, c, indexing="ij")
    onehot[dc[col_in], (qq * GRID_W + kk)[col_in]] = 1.0
    negmask = np.where(col_in, 0.0, NEG_INF).astype(np.float32).reshape(1, -1)
    return onehot, negmask


def _na_pair(block_type, a, b):
    if block_type == 0:
        return b - a + KH - 1 if b < KH else None
    if block_type == 1:
        return b - a + KH // 2 - 1 if a <= b < a + KH else None
    return b - a if b >= NA_KR - KH else None


def _rpb_expand(rpb):
    onehot, negmask = _na_consts()
    rows = HEADS * (2 * KH - 1)
    rpb_pad = jnp.pad(rpb.reshape(rows, 2 * KW - 1), ((0, 128 - rows), (0, 128 - (2 * KW - 1))))

    def body(r_ref, oh_ref, m_ref, t_ref):
        hi, mid, lo = _split3(r_ref[...])
        oh = oh_ref[...]
        t_ref[...] = _dot(hi, oh) + _dot(mid, oh) + _dot(lo, oh) + m_ref[...]

    table = pl.pallas_call(body, out_shape=SDS((128, GRID_W * GRID_W), F32), name="rpb_expand",
                           compiler_params=_params())(rpb_pad, jnp.asarray(onehot, BF16), jnp.asarray(negmask))
    return table[:rows].reshape(HEADS, 2 * KH - 1, GRID_W, GRID_W)


def _rpb_reduce(dslabs):
    onehot, _ = _na_consts()
    rows = HEADS * (2 * KH - 1)

    def body(x_ref, oht_ref, o_ref):
        hi, mid, lo = _split3(x_ref[...])
        oht = oht_ref[...]
        o_ref[...] = _dot(hi, oht) + _dot(mid, oht) + _dot(lo, oht)

    out = pl.pallas_call(body, out_shape=SDS((rows, 128), F32), name="rpb_reduce", compiler_params=_params())(
        dslabs.reshape(rows, GRID_W * GRID_W), jnp.asarray(onehot.T, BF16))
    return out.reshape(HEADS, 2 * KH - 1, 128)


def _bias_tiles(slab_ref, tile_ref):
    tile_ref[...] = jnp.full(tile_ref.shape, NEG_INF, F32)
    for t in range(NA_TYPES):
        for a in range(NA_RB):
            for b in range(NA_KR):
                dr = _na_pair(t, a, b)
                if dr is not None:
                    tile_ref[t, a * GRID_W:(a + 1) * GRID_W, b * GRID_W:(b + 1) * GRID_W] = slab_ref[dr]


def _bias_tiles_bwd(dtile_ref, dslab_ref):
    acc = {}
    for t in range(NA_TYPES):
        for a in range(NA_RB):
            for b in range(NA_KR):
                dr = _na_pair(t, a, b)
                if dr is not None:
                    part = dtile_ref[t, a * GRID_W:(a + 1) * GRID_W, b * GRID_W:(b + 1) * GRID_W]
                    acc[dr] = part if dr not in acc else acc[dr] + part
    for dr in range(2 * KH - 1):
        dslab_ref[dr] = acc[dr]


def _block_geometry(g):
    start = jnp.clip(g * NA_RB - KH // 2, 0, GRID_ROWS - NA_KR)
    block_type = jnp.where(g == 0, 0, jnp.where(g == NA_BLOCKS - 1, 2, 1))
    q0 = pl.multiple_of(N_META + g * NA_QB, 16)
    k0 = pl.multiple_of(N_META + start * GRID_W, 16)
    return block_type, q0, k0


def _na_probs(q, kk, km, bias):
    s = _dot_nt(q, kk) * ATT_SCALE + bias
    sm = _dot_nt(q, km) * ATT_SCALE
    m = jnp.maximum(jnp.max(s, axis=-1, keepdims=True), jnp.max(sm, axis=-1, keepdims=True))
    p = jnp.exp(s - m)
    pm = jnp.exp(sm - m)
    inv = 1.0 / (jnp.sum(p, axis=-1, keepdims=True) + jnp.sum(pm, axis=-1, keepdims=True))
    return p * inv, pm * inv


def _meta_probs(qm, km):
    s = _dot_nt(qm, km) * ATT_SCALE
    p = jnp.exp(s - jnp.max(s, axis=-1, keepdims=True))
    return p / jnp.sum(p, axis=-1, keepdims=True)


def _na_fwd(q, k, v, bias):
    def body(q_ref, k_ref, v_ref, slab_ref, o_ref, b_ref):
        _bias_tiles(slab_ref, b_ref)
        km = k_ref[0:N_META, :].astype(BF16)
        vm = v_ref[0:N_META, :].astype(BF16)
        pmm = _meta_probs(q_ref[0:N_META, :].astype(BF16), km)
        o_ref[0:N_META, :] = _dot(pmm.astype(BF16), vm)

        def block(g, carry):
            block_type, q0, k0 = _block_geometry(g)
            qb = q_ref[pl.ds(q0, NA_QB), :].astype(BF16)
            kk = k_ref[pl.ds(k0, NA_KB), :].astype(BF16)
            vv = v_ref[pl.ds(k0, NA_KB), :].astype(BF16)
            p, pm = _na_probs(qb, kk, km, b_ref[block_type])
            o_ref[pl.ds(q0, NA_QB), :] = _dot(p.astype(BF16), vv) + _dot(pm.astype(BF16), vm)
            return carry

        lax.fori_loop(0, NA_BLOCKS, block, 0)

    head = pl.BlockSpec((None, SEQ, HEAD_DIM), lambda h: (h, 0, 0))
    return pl.pallas_call(
        body, grid=(HEADS,), in_specs=[head, head, head, pl.BlockSpec((None, 2 * KH - 1, GRID_W, GRID_W), lambda h: (h, 0, 0, 0))],
        out_specs=head, out_shape=SDS((HEADS, SEQ, HEAD_DIM), F32), name="na_fwd",
        scratch_shapes=[pltpu.VMEM((NA_TYPES, NA_QB, NA_KB), F32)],
        compiler_params=_params(("parallel",)))(q, k, v, bias)


def _na_bwd(q, k, v, bias, do):
    def body(q_ref, k_ref, v_ref, slab_ref, do_ref, dq_ref, dk_ref, dv_ref, dslab_ref, b_ref, db_ref):
        _bias_tiles(slab_ref, b_ref)
        km = k_ref[0:N_META, :].astype(BF16)
        vm = v_ref[0:N_META, :].astype(BF16)
        dk_ref[...] = jnp.zeros_like(dk_ref)
        dv_ref[...] = jnp.zeros_like(dv_ref)
        db_ref[...] = jnp.zeros_like(db_ref)

        qm = q_ref[0:N_META, :].astype(BF16)
        dom = do_ref[0:N_META, :].astype(BF16)
        pmm = _meta_probs(qm, km)
        dpm = _dot_nt(dom, vm)
        dsm = (pmm * (dpm - jnp.sum(pmm * dpm, axis=-1, keepdims=True)) * ATT_SCALE).astype(BF16)
        dq_ref[0:N_META, :] = _dot(dsm, km)
        dkm0 = _dot_tn(dsm, qm)
        dvm0 = _dot_tn(pmm.astype(BF16), dom)

        def block(g, carry):
            dkm, dvm = carry
            block_type, q0, k0 = _block_geometry(g)
            qb = q_ref[pl.ds(q0, NA_QB), :].astype(BF16)
            kk = k_ref[pl.ds(k0, NA_KB), :].astype(BF16)
            vv = v_ref[pl.ds(k0, NA_KB), :].astype(BF16)
            dob = do_ref[pl.ds(q0, NA_QB), :].astype(BF16)
            p, pm = _na_probs(qb, kk, km, b_ref[block_type])
            dp = _dot_nt(dob, vv)
            dpm_ = _dot_nt(dob, vm)
            delta = jnp.sum(p * dp, axis=-1, keepdims=True) + jnp.sum(pm * dpm_, axis=-1, keepdims=True)
            ds = p * (dp - delta)
            dsm_ = pm * (dpm_ - delta)
            db_ref[block_type] += ds
            dsb = (ds * ATT_SCALE).astype(BF16)
            dsmb = (dsm_ * ATT_SCALE).astype(BF16)
            dq_ref[pl.ds(q0, NA_QB), :] = _dot(dsb, kk) + _dot(dsmb, km)
            dk_ref[pl.ds(k0, NA_KB), :] += _dot_tn(dsb, qb)
            dv_ref[pl.ds(k0, NA_KB), :] += _dot_tn(p.astype(BF16), dob)
            return dkm + _dot_tn(dsmb, qb), dvm + _dot_tn(pm.astype(BF16), dob)

        dkm, dvm = lax.fori_loop(0, NA_BLOCKS, block, (dkm0, dvm0))
        dk_ref[0:N_META, :] = dkm
        dv_ref[0:N_META, :] = dvm
        _bias_tiles_bwd(db_ref, dslab_ref)

    head = pl.BlockSpec((None, SEQ, HEAD_DIM), lambda h: (h, 0, 0))
    bspec = pl.BlockSpec((None, 2 * KH - 1, GRID_W, GRID_W), lambda h: (h, 0, 0, 0))
    return pl.pallas_call(
        body, grid=(HEADS,), in_specs=[head, head, head, bspec, head], out_specs=[head, head, head, bspec],
        out_shape=[SDS((HEADS, SEQ, HEAD_DIM), F32)] * 3 + [SDS((HEADS, 2 * KH - 1, GRID_W, GRID_W), F32)],
        scratch_shapes=[pltpu.VMEM((NA_TYPES, NA_QB, NA_KB), F32), pltpu.VMEM((NA_TYPES, NA_QB, NA_KB), F32)],
        name="na_bwd", compiler_params=_params(("parallel",)))(q, k, v, bias, do)


def _cmul(ar, ai, br, bi):
    return ar * br - ai * bi, ar * bi + ai * br


def _cpow(ar, ai, n):
    rr, ri = None, None
    br, bi = ar, ai
    while n:
        if n & 1:
            rr, ri = (br, bi) if rr is None else _cmul(rr, ri, br, bi)
        n >>= 1
        if n:
            br, bi = _cmul(br, bi, br, bi)
    return rr, ri


def _s5_prep(lr, li, logdt, bre, bim):
    def body(lr_ref, li_ref, dt_ref, br_ref, bi_ref, lbr_ref, lbi_ref, bbr_ref, bbi_ref):
        lr_, li_ = lr_ref[...], li_ref[...]
        dt = jnp.exp(dt_ref[...])
        mag = jnp.exp(lr_ * dt)
        lbr = mag * jnp.cos(li_ * dt)
        lbi = mag * jnp.sin(li_ * dt)
        lbr_ref[...] = lbr
        lbi_ref[...] = lbi
        den = lr_ * lr_ + li_ * li_
        xr = lbr - 1.0
        cr = (xr * lr_ + lbi * li_) / den
        ci = (lbi * lr_ - xr * li_) / den
        br, bi = br_ref[...], bi_ref[...]
        bbr_ref[...] = cr[:, None, :] * br - ci[:, None, :] * bi
        bbi_ref[...] = cr[:, None, :] * bi + ci[:, None, :] * br

    n = 2 * S5_GROUPS
    return pl.pallas_call(
        body, out_shape=[SDS((n, S5_STATE), F32)] * 2 + [SDS((n, S5_GROUP, S5_STATE), F32)] * 2,
        name="s5_prep", compiler_params=_params())(lr, li, logdt, bre, bim)


def _s5_prep_bwd(lr, li, logdt, bre, bim, dar, dai, dbbr, dbbi):
    def body(lr_ref, li_ref, dt_ref, br_ref, bi_ref, dar_ref, dai_ref, dbr_ref, dbi_ref,
             glr_ref, gli_ref, gdt_ref, gbr_ref, gbi_ref):
        lr_, li_ = lr_ref[...], li_ref[...]
        dt = jnp.exp(dt_ref[...])
        mag = jnp.exp(lr_ * dt)
        lbr = mag * jnp.cos(li_ * dt)
        lbi = mag * jnp.sin(li_ * dt)
        den = lr_ * lr_ + li_ * li_
        xr = lbr - 1.0
        cr = (xr * lr_ + lbi * li_) / den
        ci = (lbi * lr_ - xr * li_) / den
        br, bi = br_ref[...], bi_ref[...]
        dbr, dbi = dbr_ref[...], dbi_ref[...]
        gbr_ref[...] = cr[:, None, :] * dbr + ci[:, None, :] * dbi
        gbi_ref[...] = cr[:, None, :] * dbi - ci[:, None, :] * dbr
        gcr = jnp.sum(dbr * br + dbi * bi, axis=1)
        gci = jnp.sum(dbi * br - dbr * bi, axis=1)
        ilr, ili = lr_ / den, li_ / den
        tr, ti = _cmul(gcr, gci, ilr, ili)
        glbr = dar_ref[...] + tr
        glbi = dai_ref[...] + ti
        dr_, di_ = _cmul(tr, ti, cr, -ci)
        gwr, gwi = _cmul(glbr, glbi, lbr, -lbi)
        glr_ref[...] = gwr * dt - dr_
        gli_ref[...] = gwi * dt - di_
        gdt_ref[...] = jnp.sum(gwr * lr_ + gwi * li_, axis=-1, keepdims=True) * dt

    n = 2 * S5_GROUPS
    return pl.pallas_call(
        body, out_shape=[SDS((n, S5_STATE), F32)] * 2 + [SDS((n, 1), F32)] + [SDS((n, S5_GROUP, S5_STATE), F32)] * 2,
        name="s5_prep_bwd", compiler_params=_params())(lr, li, logdt, bre, bim, dar, dai, dbbr, dbbi)


def _scan_local(xr_ref, xi_ref, ar8, ai8, reverse):
    def step(i, carry):
        sr, si = carry
        idx = (SCAN_T - 1 - i) if reverse else i
        rows = pl.ds(pl.multiple_of(idx * SCAN_BLOCKS, SCAN_BLOCKS), SCAN_BLOCKS)
        nr = ar8 * sr - ai8 * si + xr_ref[rows, :]
        ni = ar8 * si + ai8 * sr + xi_ref[rows, :]
        xr_ref[rows, :] = nr
        xi_ref[rows, :] = ni
        return nr, ni

    z = jnp.zeros(ar8.shape, F32)
    return lax.fori_loop(0, SCAN_T, step, (z, z))


def _scan_carries(er, ei, atr, ati, reverse):
    row = lax.broadcasted_iota(jnp.int32, er.shape, 0)
    cr = jnp.zeros((1, er.shape[1]), F32)
    ci = cr
    outr = jnp.zeros(er.shape, F32)
    outi = outr
    order = range(SCAN_BLOCKS - 1, -1, -1) if reverse else range(SCAN_BLOCKS)
    for b in order:
        outr = jnp.where(row == b, cr, outr)
        outi = jnp.where(row == b, ci, outi)
        nr, ni = _cmul(atr, ati, cr, ci)
        cr, ci = nr + er[b:b + 1, :], ni + ei[b:b + 1, :]
    return outr, outi


def _scan_fixup(xr_ref, xi_ref, cr8, ci8, ar8, ai8, reverse):
    def step(i, carry):
        pr, pi = carry
        idx = (SCAN_T - 1 - i) if reverse else i
        rows = pl.ds(pl.multiple_of(idx * SCAN_BLOCKS, SCAN_BLOCKS), SCAN_BLOCKS)
        fr, fi = _cmul(pr, pi, cr8, ci8)
        xr_ref[rows, :] += fr
        xi_ref[rows, :] += fi
        return _cmul(pr, pi, ar8, ai8)

    lax.fori_loop(0, SCAN_T, step, (ar8, ai8))


def _scan(xr_ref, xi_ref, ar, ai, reverse):
    n = ar.shape[1]
    ar8 = jnp.broadcast_to(ar, (SCAN_BLOCKS, n))
    ai8 = jnp.broadcast_to(ai, (SCAN_BLOCKS, n))
    er, ei = _scan_local(xr_ref, xi_ref, ar8, ai8, reverse)
    atr, ati = _cpow(ar, ai, SCAN_T)
    cr8, ci8 = _scan_carries(er, ei, atr, ati, reverse)
    _scan_fixup(xr_ref, xi_ref, cr8, ci8, ar8, ai8, reverse)


def _s5_specs():
    chan = pl.BlockSpec((SEQ, CH_W), lambda c, d: (0, c))
    chan2 = pl.BlockSpec((None, SEQ, CH_W), lambda c, d: (d, 0, c))
    state = pl.BlockSpec((None, SEQ, ST_W), lambda c, d: (d, 0, c))
    bmat = pl.BlockSpec((None, None, CH_W, ST_W), lambda c, d: (d, c, 0, 0))
    cmat = pl.BlockSpec((None, None, ST_W, CH_W), lambda c, d: (d, c, 0, 0))
    avec = pl.BlockSpec((None, None, 1, ST_W), lambda c, d: (d, c, 0, 0))
    return chan, chan2, state, bmat, cmat, avec


def _scan_by_direction(xr_ref, xi_ref, ar, ai, d, adjoint):
    @pl.when(d == 0)
    def _():
        _scan(xr_ref, xi_ref, ar, ai, reverse=adjoint)

    @pl.when(d == 1)
    def _():
        _scan(xr_ref, xi_ref, ar, ai, reverse=not adjoint)


def _s5_scan_fwd(u, bre, bim, are, aim, cre, cim):
    def body(u_ref, bre_ref, bim_ref, are_ref, aim_ref, cre_ref, cim_ref, sr_ref, si_ref, y_ref):
        ub = u_ref[...].astype(BF16)
        sr_ref[...] = _dot(ub, bre_ref[...])
        si_ref[...] = _dot(ub, bim_ref[...])
        _scan_by_direction(sr_ref, si_ref, are_ref[...], aim_ref[...], pl.program_id(1), adjoint=False)
        y_ref[...] = _dot(sr_ref[...].astype(BF16), cre_ref[...]) - _dot(si_ref[...].astype(BF16), cim_ref[...])

    chan, chan2, state, bmat, cmat, avec = _s5_specs()
    return pl.pallas_call(
        body, grid=(S5_CHUNKS, 2), in_specs=[chan, bmat, bmat, avec, avec, cmat, cmat], out_specs=[state, state, chan2],
        out_shape=[SDS((2, SEQ, S5_GROUPS * S5_STATE), F32)] * 2 + [SDS((2, SEQ, S5_WIDTH), F32)],
        name="s5_scan_fwd", compiler_params=_params(("parallel", "parallel")))(u, bre, bim, are, aim, cre, cim)


def _dlam(gr_ref, gi_ref, sr_ref, si_ref, reverse):
    tile = lambda i: pl.ds(pl.multiple_of(i * SCAN_BLOCKS, SCAN_BLOCKS), SCAN_BLOCKS)
    row = lax.broadcasted_iota(jnp.int32, (SCAN_BLOCKS, ST_W), 0)
    if reverse:
        edge, src, shift, empty, lo, hi, dprev = SCAN_T - 1, 0, SCAN_BLOCKS - 1, SCAN_BLOCKS - 1, 0, SCAN_T - 1, 1
    else:
        edge, src, shift, empty, lo, hi, dprev = 0, SCAN_T - 1, 1, 0, 1, SCAN_T, -1
    spr = jnp.where(row == empty, 0.0, pltpu.roll(sr_ref[tile(src), :], shift, 0))
    spi = jnp.where(row == empty, 0.0, pltpu.roll(si_ref[tile(src), :], shift, 0))
    acc0 = _cmul(gr_ref[tile(edge), :], gi_ref[tile(edge), :], spr, -spi)

    def step(i, carry):
        accr, acci = carry
        pr, pi = _cmul(gr_ref[tile(i), :], gi_ref[tile(i), :], sr_ref[tile(i + dprev), :], -si_ref[tile(i + dprev), :])
        return accr + pr, acci + pi

    accr, acci = lax.fori_loop(lo, hi, step, acc0)
    return jnp.sum(accr, axis=0, keepdims=True), jnp.sum(acci, axis=0, keepdims=True)


def _s5_scan_bwd(dy, du_skip, u, sr, si, bre, bim, are, aim, cre, cim):
    def body(dy_ref, dus_ref, u_ref, sr_ref, si_ref, bre_ref, bim_ref, are_ref, aim_ref, cre_ref, cim_ref,
             du_ref, dbr_ref, dbi_ref, dcr_ref, dci_ref, dar_ref, dai_ref, gr_ref, gi_ref):
        d = pl.program_id(1)
        dyb = dy_ref[...].astype(BF16)
        gr_ref[...] = _dot_nt(dyb, cre_ref[...])
        gi_ref[...] = -_dot_nt(dyb, cim_ref[...])
        dcr_ref[...] = _dot_tn(sr_ref[...].astype(BF16), dyb)
        dci_ref[...] = -_dot_tn(si_ref[...].astype(BF16), dyb)
        _scan_by_direction(gr_ref, gi_ref, are_ref[...], -aim_ref[...], d, adjoint=True)

        @pl.when(d == 0)
        def _():
            dar_ref[...], dai_ref[...] = _dlam(gr_ref, gi_ref, sr_ref, si_ref, reverse=False)
            du_ref[...] = dus_ref[...]

        @pl.when(d == 1)
        def _():
            dar_ref[...], dai_ref[...] = _dlam(gr_ref, gi_ref, sr_ref, si_ref, reverse=True)

        grb = gr_ref[...].astype(BF16)
        gib = gi_ref[...].astype(BF16)
        du_ref[...] += _dot_nt(grb, bre_ref[...]) + _dot_nt(gib, bim_ref[...])
        ub = u_ref[...].astype(BF16)
        dbr_ref[...] = _dot_tn(ub, grb)
        dbi_ref[...] = _dot_tn(ub, gib)

    chan, _, state, bmat, cmat, avec = _s5_specs()
    return pl.pallas_call(
        body, grid=(S5_CHUNKS, 2), in_specs=[chan, chan, chan, state, state, bmat, bmat, avec, avec, cmat, cmat],
        out_specs=[chan, bmat, bmat, cmat, cmat, avec, avec],
        out_shape=[SDS((SEQ, S5_WIDTH), F32)] + [SDS((2, S5_CHUNKS, CH_W, ST_W), F32)] * 2
                  + [SDS((2, S5_CHUNKS, ST_W, CH_W), F32)] * 2 + [SDS((2, S5_CHUNKS, 1, ST_W), F32)] * 2,
        scratch_shapes=[pltpu.VMEM((SEQ, ST_W), F32), pltpu.VMEM((SEQ, ST_W), F32)],
        name="s5_scan_bwd", compiler_params=_params(("parallel", "arbitrary")))(dy, du_skip, u, sr, si, bre, bim, are, aim, cre, cim)


_GELU_K = math.sqrt(2.0 / math.pi)
_GELU_C = 0.044715


def _gelu(x):
    t = jnp.tanh(_GELU_K * (x + _GELU_C * x * x * x))
    return 0.5 * x * (1.0 + t), t


def _s5_glu_fwd(u, y2, dskip, wglu, bglu):
    def body(u_ref, y0_ref, y1_ref, d_ref, w_ref, b_ref, o_ref, yp_ref):
        ypre = u_ref[...] * d_ref[...] + y0_ref[...] + y1_ref[...]
        yp_ref[...] = ypre
        y, _ = _gelu(ypre)
        z = _dot(y.astype(BF16), w_ref[...]) + b_ref[...]
        o_ref[...] = y * jax.nn.sigmoid(z)

    row = _row_spec(S5_WIDTH)
    vec = _fix_spec((1, S5_WIDTH))
    dir0 = pl.BlockSpec((None, ROW_TILE, S5_WIDTH), lambda i: (0, i, 0))
    dir1 = pl.BlockSpec((None, ROW_TILE, S5_WIDTH), lambda i: (1, i, 0))
    return pl.pallas_call(
        body, grid=(N_ROW_TILES,), in_specs=[row, dir0, dir1, vec, _fix_spec((S5_WIDTH, S5_WIDTH)), vec],
        out_specs=[row, row], out_shape=[SDS((SEQ, S5_WIDTH), F32)] * 2, name="s5_glu_fwd",
        compiler_params=_params(("parallel",)))(u, y2, y2, dskip, wglu, bglu)


def _s5_glu_bwd(do, ypre, u, dskip, wglu, bglu):
    def body(do_ref, yp_ref, u_ref, d_ref, w_ref, b_ref, dyp_ref, du_ref, dw_ref, db_ref, dd_ref):
        i = pl.program_id(0)
        ypre = yp_ref[...]
        y, t = _gelu(ypre)
        yb = y.astype(BF16)
        sg = jax.nn.sigmoid(_dot(yb, w_ref[...]) + b_ref[...])
        dov = do_ref[...]
        dz = dov * y * sg * (1.0 - sg)
        dzb = dz.astype(BF16)
        dy = dov * sg + _dot_nt(dzb, w_ref[...])
        dgelu = 0.5 * (1.0 + t) + 0.5 * ypre * (1.0 - t * t) * _GELU_K * (1.0 + 3.0 * _GELU_C * ypre * ypre)
        dyp = dy * dgelu
        dyp_ref[...] = dyp
        uv = u_ref[...]
        du_ref[...] = dyp * d_ref[...]

        @pl.when(i == 0)
        def _():
            dw_ref[...] = jnp.zeros_like(dw_ref)
            db_ref[...] = jnp.zeros_like(db_ref)
            dd_ref[...] = jnp.zeros_like(dd_ref)

        dw_ref[...] += _dot_tn(yb, dzb)
        db_ref[...] += jnp.sum(dz, axis=0, keepdims=True)
        dd_ref[...] += jnp.sum(dyp * uv, axis=0, keepdims=True)

    row = _row_spec(S5_WIDTH)
    vec = _fix_spec((1, S5_WIDTH))
    mat = _fix_spec((S5_WIDTH, S5_WIDTH))
    return pl.pallas_call(
        body, grid=(N_ROW_TILES,), in_specs=[row, row, row, vec, mat, vec], out_specs=[row, row, mat, vec, vec],
        out_shape=[SDS((SEQ, S5_WIDTH), F32)] * 2 + [SDS((S5_WIDTH, S5_WIDTH), F32), SDS((1, S5_WIDTH), F32), SDS((1, S5_WIDTH), F32)],
        name="s5_glu_bwd", compiler_params=_params(("arbitrary",)))(do, ypre, u, dskip, wglu, bglu)


def _mix_out_fwd(ona, os5, g_na, g_s5, wout):
    def body(a_ref, s_ref, ga_ref, gs_ref, w_ref, o_ref):
        av, sv = a_ref[...], s_ref[...]
        ca = (av * _rstd(av) * ga_ref[...]).astype(BF16)
        cs = (sv * _rstd(sv) * gs_ref[...]).astype(BF16)
        o_ref[...] = _dot(ca, w_ref[0:NA_WIDTH, :]) + _dot(cs, w_ref[NA_WIDTH:, :])

    row = _row_spec(NA_WIDTH)
    vec = _fix_spec((1, NA_WIDTH))
    return pl.pallas_call(
        body, grid=(N_ROW_TILES,), in_specs=[row, row, vec, vec, _fix_spec((D_MODEL, D_MODEL))],
        out_specs=_row_spec(D_MODEL), out_shape=SDS((SEQ, D_MODEL), F32), name="mix_out_fwd",
        compiler_params=_params(("parallel",)))(ona, os5, g_na, g_s5, wout)


def _mix_out_bwd(dmix, ona, os5, g_na, g_s5, wout):
    def body(dm_ref, a_ref, s_ref, ga_ref, gs_ref, w_ref, da_ref, ds_ref, dw_ref, dga_ref, dgs_ref):
        i = pl.program_id(0)
        dm = dm_ref[...]
        av, sv = a_ref[...], s_ref[...]
        ra, rs = _rstd(av), _rstd(sv)
        ga, gs = ga_ref[...], gs_ref[...]
        ca = (av * ra * ga).astype(BF16)
        cs = (sv * rs * gs).astype(BF16)
        dca = _dot_nt(dm, w_ref[0:NA_WIDTH, :])
        dcs = _dot_nt(dm, w_ref[NA_WIDTH:, :])
        da, dga = _rms_bwd(av, ra, ga, dca)
        ds, dgs = _rms_bwd(sv, rs, gs, dcs)
        da_ref[...] = da
        ds_ref[...] = ds

        @pl.when(i == 0)
        def _():
            dw_ref[...] = jnp.zeros_like(dw_ref)
            dga_ref[...] = jnp.zeros_like(dga_ref)
            dgs_ref[...] = jnp.zeros_like(dgs_ref)

        dw_ref[0:NA_WIDTH, :] += _dot_tn(ca, dm)
        dw_ref[NA_WIDTH:, :] += _dot_tn(cs, dm)
        dga_ref[...] += jnp.sum(dga, axis=0, keepdims=True)
        dgs_ref[...] += jnp.sum(dgs, axis=0, keepdims=True)

    row = _row_spec(NA_WIDTH)
    vec = _fix_spec((1, NA_WIDTH))
    mat = _fix_spec((D_MODEL, D_MODEL))
    return pl.pallas_call(
        body, grid=(N_ROW_TILES,), in_specs=[_row_spec(D_MODEL), row, row, vec, vec, mat],
        out_specs=[row, row, mat, vec, vec],
        out_shape=[SDS((SEQ, NA_WIDTH), F32)] * 2 + [SDS((D_MODEL, D_MODEL), F32), SDS((1, NA_WIDTH), F32), SDS((1, NA_WIDTH), F32)],
        name="mix_out_bwd", compiler_params=_params(("arbitrary",)))(dmix, ona, os5, g_na, g_s5, wout)


def _me():
    x, y, c = lax.axis_index("x"), lax.axis_index("y"), lax.axis_index("c")
    return x, y, c, 4 * x + 2 * y + c


def _peer(k):
    x, y, c, _ = _me()
    px = 1 - x if (k >> 2) & 1 else x
    py = 1 - y if (k >> 1) & 1 else y
    pc = 1 - c if k & 1 else c
    return (px, py, pc), 4 * px + 2 * py + pc


ALL_PEERS = (1, 2, 3, 4, 5, 6, 7)
CHIP_PEERS = (2, 4, 6)
SIBLING = 1


def _slot8(pos):
    return 4 * pos[0] + 2 * pos[1] + pos[2]


def _slot4(pos):
    return 2 * pos[0] + pos[1]


def _exchange(arrays, gather, name, after=()):
    n, n_after = len(arrays), len(after)

    def body(*refs):
        ins, outs = refs[:n], refs[n + n_after:2 * n + n_after]
        send_sems, recv_sems, local_sems = refs[2 * n + n_after:]
        _, _, _, me = _me()
        started = []
        for a in range(n):
            src_mine = ins[a] if gather else ins[a].at[me]
            local = pltpu.make_async_copy(src_mine, outs[a].at[me], local_sems.at[a])
            local.start()
            started.append(local)
        sends = []
        for k in range(1, N_DEV):
            peer, peer_idx = _peer(k)
            for a in range(n):
                src = ins[a] if gather else ins[a].at[peer_idx]
                cp = pltpu.make_async_remote_copy(src_ref=src, dst_ref=outs[a].at[me], send_sem=send_sems.at[a, k - 1],
                                                  recv_sem=recv_sems.at[a, k - 1], device_id=peer, device_id_type=MESH)
                cp.start()
                sends.append(cp)
        for k in range(1, N_DEV):
            peer, peer_idx = _peer(k)
            for a in range(n):
                src = ins[a] if gather else ins[a].at[peer_idx]
                pltpu.make_async_remote_copy(src_ref=src, dst_ref=outs[a].at[peer_idx], send_sem=send_sems.at[a, k - 1],
                                             recv_sem=recv_sems.at[a, k - 1], device_id=peer, device_id_type=MESH).wait_recv()
        for cp in sends:
            cp.wait_send()
        for local in started:
            local.wait()

    hbm = pl.BlockSpec(memory_space=pltpu.HBM)
    out_shape = [SDS((N_DEV,) + tuple(a.shape), a.dtype) if gather else SDS(a.shape, a.dtype) for a in arrays]
    return pl.pallas_call(
        body, in_specs=[hbm] * n + [pl.BlockSpec(memory_space=pl.ANY)] * n_after, out_specs=[hbm] * n, out_shape=out_shape,
        scratch_shapes=[pltpu.SemaphoreType.DMA((n, N_DEV - 1)), pltpu.SemaphoreType.DMA((n, N_DEV - 1)),
                        pltpu.SemaphoreType.DMA((n,))],
        name=name)(*arrays, *after)


_HBM = pl.BlockSpec(memory_space=pltpu.HBM)
_SEM = pl.BlockSpec(memory_space=pltpu.SEMAPHORE)
_EFFECT = pltpu.SideEffectType.DATAFLOW_SIDE_EFFECTING


def _land_shape(a, gather):
    return (N_DEV,) + tuple(a.shape) if gather else tuple(a.shape)


def _place_own(arrays, gather, name, slot=_slot8):
    n = len(arrays)
    me = slot(_me()[:3])

    def body(me_ref, *refs):
        for a in range(n):
            refs[n + a][...] = refs[a][...]

    def own_slot(a):
        zeros = (0,) * (a.ndim - (0 if gather else 1))
        return lambda i, me_ref: (me_ref[0],) + zeros

    def whole(a):
        return lambda i, me_ref: (0,) * a.ndim

    in_specs = [pl.BlockSpec(a.shape, whole(a)) if gather else pl.BlockSpec((None,) + a.shape[1:], own_slot(a)) for a in arrays]
    out_specs = [pl.BlockSpec((None,) + (a.shape if gather else a.shape[1:]), own_slot(a)) for a in arrays]
    return pl.pallas_call(
        body, grid_spec=pltpu.PrefetchScalarGridSpec(num_scalar_prefetch=1, grid=(1,), in_specs=in_specs, out_specs=out_specs),
        out_shape=[SDS(_land_shape(a, gather), a.dtype) for a in arrays], name=name,
        compiler_params=_params(("arbitrary",)))(me.reshape(1).astype(jnp.int32), *arrays)


def _exchange_start(arrays, lands, gather, name, peers=ALL_PEERS, slot=_slot8):
    n = len(arrays)

    def body(*refs):
        ins, lnd = refs[:n], refs[n:2 * n]
        send_sems, recv_sems = refs[2 * n], refs[2 * n + 1]
        token = refs[-1]
        me = slot(_me()[:3])
        for i, k in enumerate(peers):
            peer, _ = _peer(k)
            for a in range(n):
                src = ins[a] if gather else ins[a].at[slot(peer)]
                s = a * len(peers) + i
                pltpu.make_async_remote_copy(src_ref=src, dst_ref=lnd[a].at[me], send_sem=send_sems.at[s],
                                             recv_sem=recv_sems.at[s], device_id=peer, device_id_type=MESH).start()
        token[...] = jnp.zeros_like(token)

    sems = pltpu.SemaphoreType.DMA((n * len(peers),))
    out = pl.pallas_call(
        body, name=name, in_specs=[_HBM] * (2 * n),
        out_shape=(sems, sems) + tuple(pltpu.HBM(a.shape, a.dtype) for a in list(arrays) + list(lands)) + (SDS((8, 128), F32),),
        out_specs=(_SEM, _SEM) + (_HBM,) * (2 * n) + (pl.BlockSpec(memory_space=pltpu.VMEM),),
        input_output_aliases={i: 2 + i for i in range(2 * n)},
        compiler_params=pltpu.CompilerParams(has_side_effects=_EFFECT),
    )(*[pltpu.with_memory_space_constraint(a, pltpu.HBM) for a in list(arrays) + list(lands)])
    return out[0], out[1], list(out[2:2 + n]), list(out[2 + n:2 + 2 * n]), out[-1]


def _exchange_wait(send_sems, recv_sems, arrays, lands, after, gather, name, peers=ALL_PEERS, slot=_slot8):
    n = len(arrays)

    def body(*refs):
        ins, lnd = refs[:n], refs[n:2 * n]
        send_sems, recv_sems = refs[2 * n], refs[2 * n + 1]
        for i, k in enumerate(peers):
            peer, _ = _peer(k)
            for a in range(n):
                src = ins[a] if gather else ins[a].at[slot(peer)]
                s = a * len(peers) + i
                cp = pltpu.make_async_remote_copy(src_ref=src, dst_ref=lnd[a].at[slot(peer)], send_sem=send_sems.at[s],
                                                  recv_sem=recv_sems.at[s], device_id=peer, device_id_type=MESH)
                cp.wait_send()
                cp.wait_recv()

        refs[-1][...] = jnp.zeros_like(refs[-1])

    after = list(after) if isinstance(after, (list, tuple)) else [after]
    out = pl.pallas_call(
        body, name=name, in_specs=[_HBM] * (2 * n) + [_SEM, _SEM] + [pl.BlockSpec(memory_space=pl.ANY)] * len(after),
        out_shape=tuple(pltpu.HBM(a.shape, a.dtype) for a in list(arrays) + list(lands)) + (SDS((8, 128), F32),),
        out_specs=(_HBM,) * (2 * n) + (pl.BlockSpec(memory_space=pltpu.VMEM),), input_output_aliases={i: i for i in range(2 * n)},
        compiler_params=pltpu.CompilerParams(has_side_effects=_EFFECT),
    )(*arrays, *lands, send_sems, recv_sems, *after)
    return list(out[n:2 * n]), out[-1]


def _forward_sibling(lands, name):
    n = len(lands)

    def body(*refs):
        outs = refs[n:2 * n]
        send_sems, recv_sems = refs[2 * n:]
        x, y, c, _ = _me()
        sends = []
        for i, k in enumerate(CHIP_PEERS):
            peer, _ = _peer(k)
            for a in range(n):
                rows = outs[a].at[_slot8(peer)]
                cp = pltpu.make_async_remote_copy(src_ref=rows, dst_ref=rows, send_sem=send_sems.at[a, i], recv_sem=recv_sems.at[a, i],
                                                  device_id=(x, y, 1 - c), device_id_type=MESH)
                cp.start()
                sends.append(cp)
        for i, k in enumerate(CHIP_PEERS):
            (px, py, pc), _ = _peer(k)
            for a in range(n):
                rows = outs[a].at[_slot8((px, py, 1 - pc))]
                pltpu.make_async_remote_copy(src_ref=rows, dst_ref=rows, send_sem=send_sems.at[a, i], recv_sem=recv_sems.at[a, i],
                                             device_id=(x, y, 1 - c), device_id_type=MESH).wait_recv()
        for cp in sends:
            cp.wait_send()

    return pl.pallas_call(
        body, in_specs=[_HBM] * n, out_specs=[_HBM] * n, out_shape=[SDS(a.shape, a.dtype) for a in lands],
        input_output_aliases={i: i for i in range(n)},
        scratch_shapes=[pltpu.SemaphoreType.DMA((n, len(CHIP_PEERS))), pltpu.SemaphoreType.DMA((n, len(CHIP_PEERS)))],
        name=name)(*lands)


def _swap_sibling(arrays, name):
    n = len(arrays)
    chips = N_DEV // 2

    def body(*refs):
        ins, outs = refs[:n], refs[n:2 * n]
        send_sems, recv_sems = refs[2 * n:]
        x, y, c, _ = _me()
        sends = []
        for q in range(chips):
            for a in range(n):
                cp = pltpu.make_async_remote_copy(src_ref=ins[a].at[q, 1 - c], dst_ref=outs[a].at[q], send_sem=send_sems.at[a, q],
                                                  recv_sem=recv_sems.at[a, q], device_id=(x, y, 1 - c), device_id_type=MESH)
                cp.start()
                sends.append(cp)
        for cp in sends:
            cp.wait_recv()
        for cp in sends:
            cp.wait_send()

    return pl.pallas_call(
        body, in_specs=[_HBM] * n, out_specs=[_HBM] * n, out_shape=[SDS((chips,) + a.shape[2:], a.dtype) for a in arrays],
        scratch_shapes=[pltpu.SemaphoreType.DMA((n, chips)), pltpu.SemaphoreType.DMA((n, chips))], name=name)(*arrays)


def _sum_pairs(mine, theirs, name):
    chips, _, rows, cols = mine.shape
    c = lax.axis_index("c")

    def body(c_ref, a_ref, b_ref, o_ref):
        o_ref[...] = (a_ref[...].astype(F32) + b_ref[...].astype(F32)).astype(o_ref.dtype)

    return pl.pallas_call(
        body, grid_spec=pltpu.PrefetchScalarGridSpec(
            num_scalar_prefetch=1, grid=(chips,),
            in_specs=[pl.BlockSpec((None, None, rows, cols), lambda q, c_ref: (q, c_ref[0], 0, 0)),
                      pl.BlockSpec((None, rows, cols), lambda q, c_ref: (q, 0, 0))],
            out_specs=pl.BlockSpec((None, rows, cols), lambda q, c_ref: (q, 0, 0))),
        out_shape=SDS((chips, rows, cols), mine.dtype), name=name,
        compiler_params=_params(("parallel",)))(c.reshape(1).astype(jnp.int32), mine, theirs)


def _adamw_math(w, g, m, v):
    m = ADAM_B1 * m + (1.0 - ADAM_B1) * g
    v = ADAM_B2 * v + (1.0 - ADAM_B2) * (g * g)
    m_hat = m / (1.0 - ADAM_B1 ** ADAM_STEP)
    v_hat = v / (1.0 - ADAM_B2 ** ADAM_STEP)
    delta = -ADAM_LR * (m_hat / (jnp.sqrt(v_hat) + ADAM_EPS) + ADAM_WD * w)
    return delta, m, v


def _adamw(w, m, v, pieces, name):
    rows, cols = w.shape[-2:]
    lead = w.ndim - 2
    tile = rows
    for cand in (256, 176, 128, 64, 16):
        if rows > cand and rows % cand == 0:
            tile = cand
            break

    def body(w_ref, m_ref, v_ref, p_ref, g_ref, d_ref, mo_ref, vo_ref):
        g = _sum_pieces(p_ref)
        g_ref[...] = g
        d_ref[...], mo_ref[...], vo_ref[...] = _adamw_math(w_ref[...], g, m_ref[...], v_ref[...])

    blk = pl.BlockSpec((None,) * lead + (tile, cols), lambda i: (0,) * lead + (i, 0))
    return pl.pallas_call(
        body, grid=(rows // tile,), in_specs=[blk, blk, blk, pl.BlockSpec((pieces.shape[0], tile, cols), lambda i: (0, i, 0))],
        out_specs=[blk] * 4, out_shape=[SDS(w.shape, F32)] * 4, name=name,
        compiler_params=_params(("parallel",)))(w, m, v, pieces)


def _sum_pieces(p_ref):
    g = p_ref[0].astype(F32)
    for p in range(1, p_ref.shape[0]):
        g = g + p_ref[p].astype(F32)
    return g


def _adamw_s5_mat(w, m, v, g, name):
    _, ndir, groups, b, c = w.shape
    per_dir = groups // 8

    def body(w_ref, m_ref, v_ref, g_ref, d_ref, mo_ref, vo_ref):
        d_ref[...], mo_ref[...], vo_ref[...] = _adamw_math(w_ref[...], g_ref[...], m_ref[...], v_ref[...])

    blk = pl.BlockSpec((None, None, 8, b, c), lambda i: (0, i // per_dir, i % per_dir, 0, 0))
    return pl.pallas_call(
        body, grid=(ndir * per_dir,), in_specs=[blk] * 4, out_specs=[blk] * 3, out_shape=[SDS(w.shape, F32)] * 3, name=name,
        compiler_params=_params(("parallel",)))(w, m, v, g)


VEC_ROWS = ['ffn1_pre_g', 'ffn1_post_g', 'mix_pre_g', 'mix_post_g', 'ffn2_pre_g', 'ffn2_post_g', 'final_g',
            ('na_out_g', 's5_out_g'), ('s5_d', 's5_b_glu')]
VEC_NAMES = [n for row in VEC_ROWS for n in ((row,) if isinstance(row, str) else row)]
VEC_PACK_ROWS = 16


def _pack_vectors(grads):
    def body(*refs):
        o_ref = refs[-1]
        o_ref[...] = jnp.zeros_like(o_ref)
        k = 0
        for i, row in enumerate(VEC_ROWS):
            if isinstance(row, str):
                o_ref[i:i + 1, :] = refs[k][...]
                k += 1
            else:
                o_ref[i:i + 1, 0:NA_WIDTH] = refs[k][...]
                o_ref[i:i + 1, NA_WIDTH:] = refs[k + 1][...]
                k += 2

    return pl.pallas_call(body, out_shape=SDS((VEC_PACK_ROWS, D_MODEL), F32), name="pack_vectors",
                          compiler_params=_params())(*[grads[n] for n in VEC_NAMES])


def _sum8(pieces, name):
    def body(p_ref, o_ref):
        o_ref[...] = _sum_pieces(p_ref)

    return pl.pallas_call(body, out_shape=SDS(pieces.shape[1:], F32), name=name, compiler_params=_params())(pieces)


def _adamw_small(packed8, vec_wmv, others):
    n_vec, n_oth = len(VEC_NAMES), len(others)

    def body(*refs):
        p_ref = refs[0]
        ins = refs[1:1 + 3 * n_vec + 4 * n_oth]
        outs = refs[1 + 3 * n_vec + 4 * n_oth:]
        gsum = _sum_pieces(p_ref)
        k = 0
        for i, row in enumerate(VEC_ROWS):
            parts = [(row, gsum[i:i + 1, :])] if isinstance(row, str) else \
                [(row[0], gsum[i:i + 1, 0:NA_WIDTH]), (row[1], gsum[i:i + 1, NA_WIDTH:])]
            for _, g in parts:
                w_ref, m_ref, v_ref = ins[3 * k:3 * k + 3]
                outs[4 * k][...] = g
                outs[4 * k + 1][...], outs[4 * k + 2][...], outs[4 * k + 3][...] = _adamw_math(w_ref[...], g, m_ref[...], v_ref[...])
                k += 1
        for j in range(n_oth):
            w_ref, m_ref, v_ref, g_ref = ins[3 * n_vec + 4 * j:3 * n_vec + 4 * j + 4]
            g = _sum_pieces(g_ref)
            g = g[tuple(slice(0, s) for s in w_ref.shape[1:])].reshape(w_ref.shape)
            o = outs[4 * (n_vec + j):4 * (n_vec + j) + 4]
            o[0][...] = g
            o[1][...], o[2][...], o[3][...] = _adamw_math(w_ref[...], g, m_ref[...], v_ref[...])

    args, out_shape = [packed8], []
    for w, m, v in vec_wmv:
        args += [w, m, v]
        out_shape += [SDS(w.shape, F32)] * 4
    for w, m, v, g in others:
        args += [w, m, v, g]
        out_shape += [SDS(w.shape, F32)] * 4
    return pl.pallas_call(body, out_shape=out_shape, name="adamw_small", compiler_params=_params())(*args)


def _perm_rows(x):
    return x.reshape(SCAN_BLOCKS, SCAN_T, x.shape[-1]).transpose(1, 0, 2).reshape(SEQ, x.shape[-1])


def _unperm_rows(x):
    return x.reshape(SCAN_T, SCAN_BLOCKS, x.shape[-1]).transpose(1, 0, 2).reshape(SEQ, x.shape[-1])


def _block_diag(x):
    eye = np.eye(8, dtype=bool)[None, None, :, None, :, None]
    full = jnp.where(eye, x[:, :, :, :, None, :], 0.0)
    return full.reshape(2, S5_CHUNKS, 8 * x.shape[3], 8 * x.shape[4])


def _diag_blocks(x, r, c):
    x6 = x.reshape(2, S5_CHUNKS, 8, r, 8, c)
    return jnp.stack([x6[:, :, g, :, g, :] for g in range(8)], axis=2)


def _dep(x, token):
    return x if token is None else x + token


def _local_step(x, target, get_w, small, emit):
    bias = _rpb_expand(small["na_rpb"][0])
    lr = small["s5_lam_re"].reshape(64, S5_STATE)
    li = small["s5_lam_im"].reshape(64, S5_STATE)
    logdt = small["s5_log_dt"].reshape(64, 1)
    b_t = [small[n].reshape(64, S5_STATE, S5_GROUP).transpose(0, 2, 1) for n in ("s5_b_re", "s5_b_im")]
    lbr, lbi, bbr, bbi = _s5_prep(lr, li, logdt, b_t[0], b_t[1])
    are = lbr.reshape(2, S5_CHUNKS, 1, ST_W)
    aim = lbi.reshape(2, S5_CHUNKS, 1, ST_W)
    bre = _block_diag(bbr.reshape(2, S5_CHUNKS, 8, S5_GROUP, S5_STATE)).astype(BF16)
    bim = _block_diag(bbi.reshape(2, S5_CHUNKS, 8, S5_GROUP, S5_STATE)).astype(BF16)
    c_t = [small[n].reshape(2, S5_CHUNKS, 8, S5_GROUP, S5_STATE).transpose(0, 1, 2, 4, 3) for n in ("s5_c_re", "s5_c_im")]
    cre = _block_diag(c_t[0]).astype(BF16)
    cim = _block_diag(c_t[1]).astype(BF16)
    tgt = jnp.concatenate([jnp.zeros((N_META, D_MODEL), F32), target], axis=0)

    wts = dict(get_w("ffn1", [bias, are, aim, bre, bim, cre, cim, tgt]))
    h0 = jnp.concatenate([wts["meta_tokens"], x], axis=0)
    a1 = _prenorm(h0, _dep(small["ffn1_pre_g"], wts.get("token")))
    gate1, up1, f1 = _ffn_fwd(a1, wts["ffn1_w_gate"], wts["ffn1_w_up"], wts["ffn1_w_down"], "ffn1_fwd")
    h1, a2 = _post_pre(f1, h0, small["ffn1_post_g"], small["mix_pre_g"], 0.5, "post_pre1")
    wts.update(get_w("w_in", a2))
    proj = _proj_fwd(a2, wts["w_in"])
    qkv = proj[:6].reshape(3, 2, SEQ, 4, HEAD_DIM).transpose(0, 1, 3, 2, 4).reshape(3, HEADS, SEQ, HEAD_DIM)
    u = proj[6:].transpose(1, 0, 2).reshape(SEQ, S5_WIDTH)
    o3 = _na_fwd(qkv[0], qkv[1], qkv[2], bias)
    ona = o3.transpose(1, 0, 2).reshape(SEQ, NA_WIDTH)
    u_p = _perm_rows(u)
    sr, si, y2 = _s5_scan_fwd(u_p, bre, bim, are, aim, cre, cim)
    wts.update(get_w("mix", y2))
    os5_p, ypre_p = _s5_glu_fwd(u_p, y2, small["s5_d"], wts["s5_w_glu"], small["s5_b_glu"])
    os5 = _unperm_rows(os5_p)

    mix = _mix_out_fwd(ona, os5, small["na_out_g"], small["s5_out_g"], wts["w_out"])
    h2, a3 = _post_pre(mix, h1, small["mix_post_g"], small["ffn2_pre_g"], 1.0, "post_pre2")
    wts.update(get_w("ffn2", a3))
    gate2, up2, f2 = _ffn_fwd(a3, wts["ffn2_w_gate"], wts["ffn2_w_up"], wts["ffn2_w_down"], "ffn2_fwd")
    loss8, dh3, df2, g_final, g_ffn2_post = _final_loss(f2, h2, small["ffn2_post_g"], small["final_g"], tgt)

    da3, dwg2, dwu2, dwd2 = _ffn_bwd(df2, a3, gate2, up2, wts["ffn2_w_gate"], wts["ffn2_w_up"], wts["ffn2_w_down"], "ffn2_bwd")
    tok = emit("ffn2", {"ffn2_w_gate": dwg2, "ffn2_w_up": dwu2, "ffn2_w_down": dwd2})
    dh2, dmix, g_ffn2_pre, g_mix_post = _bwd_pre_post(da3, h2, _dep(small["ffn2_pre_g"], tok), dh3, mix, small["mix_post_g"], 1.0,
                                                      "bwd_pre_post2")
    dona, dos5, dwout, g_na_out, g_s5_out = _mix_out_bwd(dmix, ona, os5, small["na_out_g"], small["s5_out_g"], wts["w_out"])

    dypre_p, du_skip_p, dwglu, g_b_glu, g_s5_d = _s5_glu_bwd(_perm_rows(dos5), ypre_p, u_p, small["s5_d"], wts["s5_w_glu"],
                                                             small["s5_b_glu"])
    tok = emit("mix", {"s5_w_glu": dwglu.reshape(N_DEV, S5_WIDTH // N_DEV, S5_WIDTH).astype(BF16),
                       "w_out": dwout.reshape(N_DEV, D_MODEL // N_DEV, D_MODEL).astype(BF16)})
    du_p, dbr, dbi, dcr, dci, dar, dai = _s5_scan_bwd(dypre_p, du_skip_p, u_p, sr, si, bre, bim, _dep(are, tok), aim, cre, cim)
    du = _unperm_rows(du_p)
    dbbr = _diag_blocks(dbr, S5_GROUP, S5_STATE).reshape(64, S5_GROUP, S5_STATE)
    dbbi = _diag_blocks(dbi, S5_GROUP, S5_STATE).reshape(64, S5_GROUP, S5_STATE)
    g_lr, g_li, g_dt, g_br, g_bi = _s5_prep_bwd(lr, li, logdt, b_t[0], b_t[1], dar.reshape(64, S5_STATE),
                                                dai.reshape(64, S5_STATE), dbbr, dbbi)
    g_c = [_diag_blocks(d, S5_STATE, S5_GROUP).transpose(0, 1, 2, 4, 3).reshape(2 * S5_GROUPS, S5_GROUP, S5_STATE)
           for d in (dcr, dci)]

    do3 = dona.reshape(SEQ, HEADS, HEAD_DIM).transpose(1, 0, 2)
    dq, dk, dv, dbias = _na_bwd(qkv[0], qkv[1], qkv[2], bias, do3)
    g_rpb = _rpb_reduce(dbias)
    dense = jnp.stack([g.reshape(2 * S5_GROUPS, S5_STATE * S5_GROUP) for g in (g_br.transpose(0, 2, 1), g_bi.transpose(0, 2, 1), *g_c)])
    tok = emit("small", {"dense": dense, "na_rpb": g_rpb,
                         "s5_lam_re": g_lr.reshape(2, S5_GROUPS, S5_STATE), "s5_lam_im": g_li.reshape(2, S5_GROUPS, S5_STATE),
                         "s5_log_dt": g_dt.reshape(2, S5_GROUPS)})
    dqkv = jnp.stack([dq, dk, dv]).reshape(3, 2, 4, SEQ, HEAD_DIM).transpose(0, 1, 3, 2, 4).reshape(6, SEQ, IN_SHARD)
    dproj = jnp.concatenate([dqkv, du.reshape(SEQ, 2, IN_SHARD).transpose(1, 0, 2)], axis=0).astype(BF16)
    da2, dwin = _proj_bwd(dproj, a2, wts["w_in"])
    tok2 = emit("w_in", {"w_in": dwin})
    tok = tok if tok2 is None else tok + tok2
    dh1, df1, g_mix_pre, g_ffn1_post = _bwd_pre_post(da2, h1, _dep(small["mix_pre_g"], tok), dh2, f1, small["ffn1_post_g"], 0.5,
                                                     "bwd_pre_post1")
    da1, dwg1, dwu1, dwd1 = _ffn_bwd(df1, a1, gate1, up1, wts["ffn1_w_gate"], wts["ffn1_w_up"], wts["ffn1_w_down"], "ffn1_bwd")
    emit("ffn1", {"ffn1_w_gate": dwg1, "ffn1_w_up": dwu1, "ffn1_w_down": dwd1})
    dh0, g_ffn1_pre = _bwd_pre_only(da1, h0, small["ffn1_pre_g"], dh1)

    vec_g = {
        "ffn1_pre_g": g_ffn1_pre, "ffn1_post_g": g_ffn1_post, "mix_pre_g": g_mix_pre, "s5_d": g_s5_d, "s5_b_glu": g_b_glu,
        "na_out_g": g_na_out, "s5_out_g": g_s5_out, "mix_post_g": g_mix_post,
        "ffn2_pre_g": g_ffn2_pre, "ffn2_post_g": g_ffn2_post, "final_g": g_final,
    }
    return loss8[0, 0], dh0[N_META:], dh0[:N_META], vec_g


WEIGHT_NAMES = ['meta_tokens', 'ffn1_pre_g', 'ffn1_post_g', 'ffn1_w_gate', 'ffn1_w_up', 'ffn1_w_down', 'mix_pre_g', 'w_in',
                'na_rpb', 's5_lam_re', 's5_lam_im', 's5_log_dt', 's5_b_re', 's5_b_im', 's5_c_re', 's5_c_im', 's5_d',
                's5_w_glu', 's5_b_glu', 'na_out_g', 's5_out_g', 'w_out', 'mix_post_g', 'ffn2_pre_g', 'ffn2_post_g',
                'ffn2_w_gate', 'ffn2_w_up', 'ffn2_w_down', 'final_g']
BIG_NAMES = ['ffn1_w_gate', 'ffn1_w_up', 'ffn1_w_down', 'w_in', 's5_w_glu', 'w_out', 'ffn2_w_gate', 'ffn2_w_up', 'ffn2_w_down']
SMALL_NAMES = [n for n in WEIGHT_NAMES if n not in BIG_NAMES and n != 'meta_tokens']
WHOLE_NAMES = ['na_rpb', 's5_lam_re', 's5_lam_im', 's5_log_dt']
LEAD_NAMES = ['s5_b_re', 's5_b_im', 's5_c_re', 's5_c_im']


def kernel(x, meta_tokens, ffn1_pre_g, ffn1_post_g, ffn1_w_gate, ffn1_w_up, ffn1_w_down, mix_pre_g, w_in, na_rpb, s5_lam_re, s5_lam_im, s5_log_dt, s5_b_re, s5_b_im, s5_c_re, s5_c_im, s5_d, s5_w_glu, s5_b_glu, na_out_g, s5_out_g, w_out, mix_post_g, ffn2_pre_g, ffn2_post_g, ffn2_w_gate, ffn2_w_up, ffn2_w_down, final_g, loss_target, m_meta_tokens, m_ffn1_pre_g, m_ffn1_post_g, m_ffn1_w_gate, m_ffn1_w_up, m_ffn1_w_down, m_mix_pre_g, m_w_in, m_na_rpb, m_s5_lam_re, m_s5_lam_im, m_s5_log_dt, m_s5_b_re, m_s5_b_im, m_s5_c_re, m_s5_c_im, m_s5_d, m_s5_w_glu, m_s5_b_glu, m_na_out_g, m_s5_out_g, m_w_out, m_mix_post_g, m_ffn2_pre_g, m_ffn2_post_g, m_ffn2_w_gate, m_ffn2_w_up, m_ffn2_w_down, m_final_g, v_meta_tokens, v_ffn1_pre_g, v_ffn1_post_g, v_ffn1_w_gate, v_ffn1_w_up, v_ffn1_w_down, v_mix_pre_g, v_w_in, v_na_rpb, v_s5_lam_re, v_s5_lam_im, v_s5_log_dt, v_s5_b_re, v_s5_b_im, v_s5_c_re, v_s5_c_im, v_s5_d, v_s5_w_glu, v_s5_b_glu, v_na_out_g, v_s5_out_g, v_w_out, v_mix_post_g, v_ffn2_pre_g, v_ffn2_post_g, v_ffn2_w_gate, v_ffn2_w_up, v_ffn2_w_down, v_final_g):
    args = dict(locals())
    w = {n: args[n] for n in WEIGHT_NAMES}
    m = {n: args["m_" + n] for n in WEIGHT_NAMES}
    v = {n: args["v_" + n] for n in WEIGHT_NAMES}

    small = {n: w[n] for n in SMALL_NAMES}

    pending = {}

    def start(group, names, arrays, gather, peers=ALL_PEERS, slot=_slot8):
        lands = _place_own(arrays, gather, "own_" + group, slot)
        send_sems, recv_sems, arrays, lands, token = _exchange_start(arrays, lands, gather, "start_" + group, peers, slot)
        pending[group] = (names, send_sems, recv_sems, arrays, lands, gather, peers, slot)
        return token

    def finish(group, after):
        names, send_sems, recv_sems, arrays, lands, gather, peers, slot = pending.pop(group)
        lands, token = _exchange_wait(send_sems, recv_sems, arrays, lands, after, gather, "wait_" + group, peers, slot)
        return dict(zip(names, lands)), token

    first = ["ffn1_w_gate", "ffn1_w_up", "ffn1_w_down"]
    start("ffn1", first + ["meta_tokens"], [w[n][0].astype(BF16) for n in first] + [w["meta_tokens"]], True, (SIBLING,) + CHIP_PEERS)

    def get_w(group, after):
        got, token = finish(group, after)
        if group == "ffn1":
            got = dict(zip(got, _forward_sibling(list(got.values()), "forward_ffn1")))
            got["meta_tokens"] = got["meta_tokens"].transpose(1, 0, 2).reshape(N_META, D_MODEL)
            got["token"] = sum(start(g, names, [(w[n][0] + token[0, 0]).astype(BF16) for n in names], True)[0, 0]
                               for g, names in (("w_in", ["w_in"]), ("mix", ["s5_w_glu", "w_out"]),
                                                ("ffn2", ["ffn2_w_gate", "ffn2_w_up", "ffn2_w_down"])))
        if group == "mix":
            got = {"s5_w_glu": got["s5_w_glu"].reshape(S5_WIDTH, S5_WIDTH), "w_out": got["w_out"].reshape(D_MODEL, D_MODEL)}
        return got

    def emit(group, grads):
        if group == "ffn1":
            mine = [g.reshape((N_DEV // 2, 2) + g.shape[1:]) for g in grads.values()]
            theirs = _swap_sibling(mine, "swap_g_ffn1")
            sums = [_sum_pairs(a, b, "pair_sum_" + n) for n, a, b in zip(grads, mine, theirs)]
            return start("g_ffn1", list(grads), sums, False, CHIP_PEERS, _slot4)[0, 0]
        return start("g_" + group, list(grads), list(grads.values()), group == "small")[0, 0]

    loss_local, grad_x, gmeta, vec_g = _local_step(x[0], loss_target[0], get_w, small, emit)
    loss = lax.psum(loss_local, AXES)
    res = {}

    def update_shard(n, pieces):
        res[n] = list(_adamw(w[n], m[n], v[n], pieces, "adamw_" + n))

    for group in ("g_ffn2", "g_mix", "g_w_in"):
        for n, pieces in finish(group, grad_x)[0].items():
            update_shard(n, pieces)
    g8 = finish("g_small", grad_x)[0]
    dense = _sum8(g8["dense"], "sum_dense")
    for i, n in enumerate(LEAD_NAMES):
        g = dense[i].reshape(w[n].shape)
        res[n] = [g] + list(_adamw_s5_mat(w[n], m[n], v[n], g, "adamw_" + n))

    done = [res[n][1] for n in ("ffn2_w_gate", "ffn2_w_up", "ffn2_w_down", "w_in", "w_out", "s5_w_glu") + tuple(LEAD_NAMES)]
    packed8, gmeta8 = _exchange([_pack_vectors(vec_g), gmeta], True, "gather_vectors", after=done)
    for n, pieces in finish("g_ffn1", packed8)[0].items():
        update_shard(n, pieces)
    _, _, _, me = _me()
    update_shard("meta_tokens", lax.dynamic_slice_in_dim(gmeta8, me * (D_MODEL // N_DEV), D_MODEL // N_DEV, axis=2))

    outs = _adamw_small(packed8, [(w[n], m[n], v[n]) for n in VEC_NAMES], [(w[n], m[n], v[n], g8[n]) for n in WHOLE_NAMES])
    for i, n in enumerate(VEC_NAMES + WHOLE_NAMES):
        res[n] = list(outs[4 * i:4 * i + 4])

    out = [loss, grad_x[None]]
    for kind in range(4):
        out += [res[n][kind] for n in WEIGHT_NAMES]
    return tuple(out)
```

```python
import functools
import math

import numpy as np
import jax
import jax.numpy as jnp
from jax import lax
from jax.experimental import pallas as pl
from jax.experimental.pallas import tpu as pltpu

F32 = jnp.float32
BF16 = jnp.bfloat16
SDS = jax.ShapeDtypeStruct

D_MODEL = 1024
N_TOK = 2048
N_META = 16
SEQ = N_TOK + N_META
ROW_TILE = 688
N_ROW_TILES = SEQ // ROW_TILE
N_DEV = 8
D_FF = 2816
FF_SHARD = D_FF // N_DEV
FF_TILE = 256
IN_SHARD = 256
NA_WIDTH = 512
S5_WIDTH = 512
HEADS = 8
HEAD_DIM = 64
GRID_W = 64
GRID_ROWS = N_TOK // GRID_W
KH = 8
KW = 16
NA_RB = 4
NA_KR = KH + NA_RB - 1
NA_BLOCKS = GRID_ROWS // NA_RB
NA_QB = NA_RB * GRID_W
NA_KB = NA_KR * GRID_W
NA_TYPES = 3
S5_GROUPS = 32
S5_GROUP = 16
S5_STATE = 64
S5_CHUNKS = 4
CH_W = S5_WIDTH // S5_CHUNKS
ST_W = S5_GROUPS * S5_STATE // S5_CHUNKS
SCAN_BLOCKS = 8
SCAN_T = SEQ // SCAN_BLOCKS
RMS_EPS = 1e-6
NEG_INF = -1e30
ATT_SCALE = HEAD_DIM ** -0.5
ADAM_LR, ADAM_B1, ADAM_B2, ADAM_EPS, ADAM_WD, ADAM_STEP = 0.001, 0.9, 0.999, 1e-08, 0.01, 10
VMEM_LIMIT = 56 * 1024 * 1024
MESH = pl.DeviceIdType.MESH
AXES = ("x", "y", "c")


def _params(sem=None):
    return pltpu.CompilerParams(dimension_semantics=sem, vmem_limit_bytes=VMEM_LIMIT)


def _dot(a, b):
    return jnp.dot(a, b, preferred_element_type=F32)


def _dot_nt(a, b):
    return lax.dot_general(a, b, (((1,), (1,)), ((), ())), preferred_element_type=F32)


def _dot_tn(a, b):
    return lax.dot_general(a, b, (((0,), (0,)), ((), ())), preferred_element_type=F32)


def _rstd(x):
    return lax.rsqrt(jnp.mean(x * x, axis=-1, keepdims=True) + RMS_EPS)


def _rms_bwd(x, r, g, dy):
    dyg = dy * g
    xr = x * r
    dx = r * (dyg - xr * jnp.mean(dyg * xr, axis=-1, keepdims=True))
    return dx, dy * xr


def _rows(i, size=ROW_TILE):
    return pl.ds(pl.multiple_of(i * size, 16), size)


def _row_spec(width):
    return pl.BlockSpec((ROW_TILE, width), lambda i: (i, 0))


def _fix_spec(shape):
    return pl.BlockSpec(shape, lambda i: (0,) * len(shape))


def _split3(x):
    hi = x.astype(BF16)
    r1 = x - hi.astype(F32)
    mid = r1.astype(BF16)
    lo = (r1 - mid.astype(F32)).astype(BF16)
    return hi, mid, lo


def _prenorm(x, g):
    def body(x_ref, g_ref, a_ref):
        xv = x_ref[...]
        a_ref[...] = (xv * _rstd(xv) * g_ref[...]).astype(BF16)

    return pl.pallas_call(
        body, grid=(N_ROW_TILES,), in_specs=[_row_spec(D_MODEL), _fix_spec((1, D_MODEL))],
        out_specs=_row_spec(D_MODEL), out_shape=SDS((SEQ, D_MODEL), BF16), name="prenorm",
        compiler_params=_params(("parallel",)))(x, g)


def _post_pre(f, hres, g_post, g_next, scale, name):
    def body(f_ref, h_ref, gp_ref, gn_ref, ho_ref, a_ref):
        fv = f_ref[...]
        h = h_ref[...] + scale * (fv * _rstd(fv) * gp_ref[...])
        ho_ref[...] = h
        a_ref[...] = (h * _rstd(h) * gn_ref[...]).astype(BF16)

    return pl.pallas_call(
        body, grid=(N_ROW_TILES,),
        in_specs=[_row_spec(D_MODEL), _row_spec(D_MODEL), _fix_spec((1, D_MODEL)), _fix_spec((1, D_MODEL))],
        out_specs=[_row_spec(D_MODEL), _row_spec(D_MODEL)],
        out_shape=[SDS((SEQ, D_MODEL), F32), SDS((SEQ, D_MODEL), BF16)], name=name,
        compiler_params=_params(("parallel",)))(f, hres, g_post, g_next)


def _final_loss(f2, h2, g_post, g_final, target):
    def body(f_ref, h_ref, gp_ref, gf_ref, t_ref, loss_ref, dh_ref, df_ref, dgf_ref, dgp_ref):
        i = pl.program_id(0)
        fv = f_ref[...]
        r1 = _rstd(fv)
        gp = gp_ref[...]
        h3 = h_ref[...] + 0.5 * (fv * r1 * gp)
        r2 = _rstd(h3)
        gf = gf_ref[...]
        y = h3 * r2 * gf
        row = lax.broadcasted_iota(jnp.int32, (ROW_TILE, 1), 0) + i * ROW_TILE
        err = jnp.where(row >= N_META, y - t_ref[...], 0.0)
        part = 0.5 * jnp.sum(jnp.mean(err * err, axis=-1, keepdims=True))
        dy = err * (1.0 / D_MODEL)
        dh3, dgf = _rms_bwd(h3, r2, gf, dy)
        dh_ref[...] = dh3
        df, dgp = _rms_bwd(fv, r1, gp, 0.5 * dh3)
        df_ref[...] = df.astype(BF16)

        @pl.when(i == 0)
        def _():
            loss_ref[...] = jnp.zeros_like(loss_ref)
            dgf_ref[...] = jnp.zeros_like(dgf_ref)
            dgp_ref[...] = jnp.zeros_like(dgp_ref)

        loss_ref[...] += part
        dgf_ref[...] += jnp.sum(dgf, axis=0, keepdims=True)
        dgp_ref[...] += jnp.sum(dgp, axis=0, keepdims=True)

    gain = _fix_spec((1, D_MODEL))
    return pl.pallas_call(
        body, grid=(N_ROW_TILES,),
        in_specs=[_row_spec(D_MODEL), _row_spec(D_MODEL), gain, gain, _row_spec(D_MODEL)],
        out_specs=[_fix_spec((8, 128)), _row_spec(D_MODEL), _row_spec(D_MODEL), gain, gain],
        out_shape=[SDS((8, 128), F32), SDS((SEQ, D_MODEL), F32), SDS((SEQ, D_MODEL), BF16),
                   SDS((1, D_MODEL), F32), SDS((1, D_MODEL), F32)],
        name="final_loss", compiler_params=_params(("arbitrary",)))(f2, h2, g_post, g_final, target)


def _bwd_pre_post(da, h, g_pre, dh_res, fprev, g_post, scale, name):
    def body(da_ref, h_ref, gpre_ref, dhr_ref, f_ref, gpost_ref, dh_ref, df_ref, dgpre_ref, dgpost_ref):
        i = pl.program_id(0)
        hv = h_ref[...]
        dxa, dgpre = _rms_bwd(hv, _rstd(hv), gpre_ref[...], da_ref[...])
        dh = dhr_ref[...] + dxa
        dh_ref[...] = dh
        fv = f_ref[...]
        df, dgpost = _rms_bwd(fv, _rstd(fv), gpost_ref[...], scale * dh)
        df_ref[...] = df.astype(BF16)

        @pl.when(i == 0)
        def _():
            dgpre_ref[...] = jnp.zeros_like(dgpre_ref)
            dgpost_ref[...] = jnp.zeros_like(dgpost_ref)

        dgpre_ref[...] += jnp.sum(dgpre, axis=0, keepdims=True)
        dgpost_ref[...] += jnp.sum(dgpost, axis=0, keepdims=True)

    gain = _fix_spec((1, D_MODEL))
    row = _row_spec(D_MODEL)
    return pl.pallas_call(
        body, grid=(N_ROW_TILES,), in_specs=[row, row, gain, row, row, gain],
        out_specs=[row, row, gain, gain],
        out_shape=[SDS((SEQ, D_MODEL), F32), SDS((SEQ, D_MODEL), BF16), SDS((1, D_MODEL), F32), SDS((1, D_MODEL), F32)],
        name=name, compiler_params=_params(("arbitrary",)))(da, h, g_pre, dh_res, fprev, g_post)


def _bwd_pre_only(da, h, g_pre, dh_res):
    def body(da_ref, h_ref, gpre_ref, dhr_ref, dh_ref, dgpre_ref):
        i = pl.program_id(0)
        hv = h_ref[...]
        dxa, dgpre = _rms_bwd(hv, _rstd(hv), gpre_ref[...], da_ref[...])
        dh_ref[...] = dhr_ref[...] + dxa

        @pl.when(i == 0)
        def _():
            dgpre_ref[...] = jnp.zeros_like(dgpre_ref)

        dgpre_ref[...] += jnp.sum(dgpre, axis=0, keepdims=True)

    gain = _fix_spec((1, D_MODEL))
    row = _row_spec(D_MODEL)
    return pl.pallas_call(
        body, grid=(N_ROW_TILES,), in_specs=[row, row, gain, row], out_specs=[row, gain],
        out_shape=[SDS((SEQ, D_MODEL), F32), SDS((1, D_MODEL), F32)],
        name="bwd_pre_only", compiler_params=_params(("arbitrary",)))(da, h, g_pre, dh_res)


def _ffn_fwd(a, wg, wu, wd, name):
    def body(a_ref, wg_ref, wu_ref, wd_ref, gate_ref, up_ref, f_ref):
        j = pl.program_id(0)

        def tile(i, carry):
            rows = _rows(i)
            at = a_ref[rows, :]
            gate = _dot_nt(at, wg_ref[...])
            up = _dot_nt(at, wu_ref[...])
            gate_ref[rows, :] = gate
            up_ref[rows, :] = up
            act = (gate * jax.nn.sigmoid(gate) * up).astype(BF16)
            contrib = _dot(act, wd_ref[...])

            @pl.when(j == 0)
            def _():
                f_ref[rows, :] = contrib

            @pl.when(j != 0)
            def _():
                f_ref[rows, :] += contrib

            return carry

        lax.fori_loop(0, N_ROW_TILES, tile, 0)

    wtile = pl.BlockSpec((FF_TILE, D_MODEL), lambda j: (j, 0))
    hid = pl.BlockSpec((SEQ, FF_TILE), lambda j: (0, j))
    full = pl.BlockSpec((SEQ, D_MODEL), lambda j: (0, 0))
    return pl.pallas_call(
        body, grid=(D_FF // FF_TILE,), in_specs=[full, wtile, wtile, wtile], out_specs=[hid, hid, full],
        out_shape=[SDS((SEQ, D_FF), F32), SDS((SEQ, D_FF), F32), SDS((SEQ, D_MODEL), F32)],
        name=name, compiler_params=_params(("arbitrary",)))(a, wg, wu, wd)


def _ffn_bwd(df, a, gate, up, wg, wu, wd, name):
    def body(df_ref, a_ref, gate_ref, up_ref, wg_ref, wu_ref, wd_ref, da_ref, dwg_ref, dwu_ref, dwd_ref,
             acc_g, acc_u, acc_d):
        j = pl.program_id(0)

        def tile(i, carry):
            rows = _rows(i)
            dft = df_ref[rows, :]
            at = a_ref[rows, :]
            gate = gate_ref[rows, :]
            up = up_ref[rows, :]
            dact = _dot_nt(dft, wd_ref[...])
            sig = jax.nn.sigmoid(gate)
            silu = gate * sig
            dgate = (dact * up * (sig * (1.0 + gate * (1.0 - sig)))).astype(BF16)
            dup = (dact * silu).astype(BF16)
            act = (silu * up).astype(BF16)
            dwd = _dot_tn(act, dft)
            dwg = _dot_tn(dgate, at)
            dwu = _dot_tn(dup, at)
            dat = _dot(dgate, wg_ref[...]) + _dot(dup, wu_ref[...])

            @pl.when(i == 0)
            def _():
                acc_d[...] = dwd
                acc_g[...] = dwg
                acc_u[...] = dwu

            @pl.when(i != 0)
            def _():
                acc_d[...] += dwd
                acc_g[...] += dwg
                acc_u[...] += dwu

            @pl.when(j == 0)
            def _():
                da_ref[rows, :] = dat

            @pl.when(j != 0)
            def _():
                da_ref[rows, :] += dat

            return carry

        lax.fori_loop(0, N_ROW_TILES, tile, 0)
        dwg_ref[...] = acc_g[...].astype(BF16)
        dwu_ref[...] = acc_u[...].astype(BF16)
        dwd_ref[...] = acc_d[...].astype(BF16)

    wtile = pl.BlockSpec((FF_TILE, D_MODEL), lambda j: (j, 0))
    hid = pl.BlockSpec((SEQ, FF_TILE), lambda j: (0, j))
    full = pl.BlockSpec((SEQ, D_MODEL), lambda j: (0, 0))
    return pl.pallas_call(
        body, grid=(D_FF // FF_TILE,), in_specs=[full, full, hid, hid, wtile, wtile, wtile],
        out_specs=[full, wtile, wtile, wtile],
        out_shape=[SDS((SEQ, D_MODEL), F32)] + [SDS((D_FF, D_MODEL), BF16)] * 3,
        scratch_shapes=[pltpu.VMEM((FF_TILE, D_MODEL), F32)] * 3,
        name=name, compiler_params=_params(("arbitrary",)))(df, a, gate, up, wg, wu, wd)


def _proj_fwd(a, w):
    def body(a_ref, w_ref, o_ref):
        def tile(i, carry):
            rows = _rows(i)
            o_ref[rows, :] = _dot(a_ref[rows, :], w_ref[...])
            return carry

        lax.fori_loop(0, N_ROW_TILES, tile, 0)

    return pl.pallas_call(
        body, grid=(N_DEV,),
        in_specs=[pl.BlockSpec((SEQ, D_MODEL), lambda j: (0, 0)), pl.BlockSpec((None, D_MODEL, IN_SHARD), lambda j: (j, 0, 0))],
        out_specs=pl.BlockSpec((None, SEQ, IN_SHARD), lambda j: (j, 0, 0)),
        out_shape=SDS((N_DEV, SEQ, IN_SHARD), F32), name="proj_fwd",
        compiler_params=_params(("parallel",)))(a, w)


def _proj_bwd(dproj, a, w):
    def body(dp_ref, a_ref, w_ref, da_ref, dw_ref, acc):
        j = pl.program_id(0)

        def tile(i, carry):
            rows = _rows(i)
            dpt = dp_ref[rows, :]
            dw = _dot_tn(a_ref[rows, :], dpt)
            dat = _dot_nt(dpt, w_ref[...])

            @pl.when(i == 0)
            def _():
                acc[...] = dw

            @pl.when(i != 0)
            def _():
                acc[...] += dw

            @pl.when(j == 0)
            def _():
                da_ref[rows, :] = dat

            @pl.when(j != 0)
            def _():
                da_ref[rows, :] += dat

            return carry

        lax.fori_loop(0, N_ROW_TILES, tile, 0)
        dw_ref[...] = acc[...].astype(BF16)

    full = pl.BlockSpec((SEQ, D_MODEL), lambda j: (0, 0))
    wspec = pl.BlockSpec((None, D_MODEL, IN_SHARD), lambda j: (j, 0, 0))
    return pl.pallas_call(
        body, grid=(N_DEV,),
        in_specs=[pl.BlockSpec((None, SEQ, IN_SHARD), lambda j: (j, 0, 0)), full, wspec],
        out_specs=[full, wspec],
        out_shape=[SDS((SEQ, D_MODEL), F32), SDS((N_DEV, D_MODEL, IN_SHARD), BF16)],
        scratch_shapes=[pltpu.VMEM((D_MODEL, IN_SHARD), F32)],
        name="proj_bwd", compiler_params=_params(("arbitrary",)))(dproj, a, w)


def _na_consts():
    c = np.arange(GRID_W)
    col_start = np.clip(c - KW // 2, 0, GRID_W - KW)
    col_in = (c[None, :] >= col_start[:, None]) & (c[None, :] < col_start[:, None] + KW)
    dc = np.clip(c[None, :] - c[:, None] + KW - 1, 0, 2 * KW - 2)
    onehot = np.zeros((128, GRID_W * GRID_W), np.float32)
    qq, kk = np.meshgrid(c, c, indexing="ij")
    onehot[dc[col_in], (qq * GRID_W + kk)[col_in]] = 1.0
    negmask = np.where(col_in, 0.0, NEG_INF).astype(np.float32).reshape(1, -1)
    return onehot, negmask


def _na_pair(block_type, a, b):
    if block_type == 0:
        return b - a + KH - 1 if b < KH else None
    if block_type == 1:
        return b - a + KH // 2 - 1 if a <= b < a + KH else None
    return b - a if b >= NA_KR - KH else None


def _rpb_expand(rpb):
    onehot, negmask = _na_consts()
    rows = HEADS * (2 * KH - 1)
    rpb_pad = jnp.pad(rpb.reshape(rows, 2 * KW - 1), ((0, 128 - rows), (0, 128 - (2 * KW - 1))))

    def body(r_ref, oh_ref, m_ref, t_ref):
        hi, mid, lo = _split3(r_ref[...])
        oh = oh_ref[...]
        t_ref[...] = _dot(hi, oh) + _dot(mid, oh) + _dot(lo, oh) + m_ref[...]

    table = pl.pallas_call(body, out_shape=SDS((128, GRID_W * GRID_W), F32), name="rpb_expand",
                           compiler_params=_params())(rpb_pad, jnp.asarray(onehot, BF16), jnp.asarray(negmask))
    return table[:rows].reshape(HEADS, 2 * KH - 1, GRID_W, GRID_W)


def _rpb_reduce(dslabs):
    onehot, _ = _na_consts()
    rows = HEADS * (2 * KH - 1)

    def body(x_ref, oht_ref, o_ref):
        hi, mid, lo = _split3(x_ref[...])
        oht = oht_ref[...]
        o_ref[...] = _dot(hi, oht) + _dot(mid, oht) + _dot(lo, oht)

    out = pl.pallas_call(body, out_shape=SDS((rows, 128), F32), name="rpb_reduce", compiler_params=_params())(
        dslabs.reshape(rows, GRID_W * GRID_W), jnp.asarray(onehot.T, BF16))
    return out.reshape(HEADS, 2 * KH - 1, 128)


def _bias_tiles(slab_ref, tile_ref):
    tile_ref[...] = jnp.full(tile_ref.shape, NEG_INF, F32)
    for t in range(NA_TYPES):
        for a in range(NA_RB):
            for b in range(NA_KR):
                dr = _na_pair(t, a, b)
                if dr is not None:
                    tile_ref[t, a * GRID_W:(a + 1) * GRID_W, b * GRID_W:(b + 1) * GRID_W] = slab_ref[dr]


def _bias_tiles_bwd(dtile_ref, dslab_ref):
    acc = {}
    for t in range(NA_TYPES):
        for a in range(NA_RB):
            for b in range(NA_KR):
                dr = _na_pair(t, a, b)
                if dr is not None:
                    part = dtile_ref[t, a * GRID_W:(a + 1) * GRID_W, b * GRID_W:(b + 1) * GRID_W]
                    acc[dr] = part if dr not in acc else acc[dr] + part
    for dr in range(2 * KH - 1):
        dslab_ref[dr] = acc[dr]


def _block_geometry(g):
    start = jnp.clip(g * NA_RB - KH // 2, 0, GRID_ROWS - NA_KR)
    block_type = jnp.where(g == 0, 0, jnp.where(g == NA_BLOCKS - 1, 2, 1))
    q0 = pl.multiple_of(N_META + g * NA_QB, 16)
    k0 = pl.multiple_of(N_META + start * GRID_W, 16)
    return block_type, q0, k0


def _na_probs(q, kk, km, bias):
    s = _dot_nt(q, kk) * ATT_SCALE + bias
    sm = _dot_nt(q, km) * ATT_SCALE
    m = jnp.maximum(jnp.max(s, axis=-1, keepdims=True), jnp.max(sm, axis=-1, keepdims=True))
    p = jnp.exp(s - m)
    pm = jnp.exp(sm - m)
    inv = 1.0 / (jnp.sum(p, axis=-1, keepdims=True) + jnp.sum(pm, axis=-1, keepdims=True))
    return p * inv, pm * inv


def _meta_probs(qm, km):
    s = _dot_nt(qm, km) * ATT_SCALE
    p = jnp.exp(s - jnp.max(s, axis=-1, keepdims=True))
    return p / jnp.sum(p, axis=-1, keepdims=True)


def _na_fwd(q, k, v, bias):
    def body(q_ref, k_ref, v_ref, slab_ref, o_ref, b_ref):
        _bias_tiles(slab_ref, b_ref)
        km = k_ref[0:N_META, :].astype(BF16)
        vm = v_ref[0:N_META, :].astype(BF16)
        pmm = _meta_probs(q_ref[0:N_META, :].astype(BF16), km)
        o_ref[0:N_META, :] = _dot(pmm.astype(BF16), vm)

        def block(g, carry):
            block_type, q0, k0 = _block_geometry(g)
            qb = q_ref[pl.ds(q0, NA_QB), :].astype(BF16)
            kk = k_ref[pl.ds(k0, NA_KB), :].astype(BF16)
            vv = v_ref[pl.ds(k0, NA_KB), :].astype(BF16)
            p, pm = _na_probs(qb, kk, km, b_ref[block_type])
            o_ref[pl.ds(q0, NA_QB), :] = _dot(p.astype(BF16), vv) + _dot(pm.astype(BF16), vm)
            return carry

        lax.fori_loop(0, NA_BLOCKS, block, 0)

    head = pl.BlockSpec((None, SEQ, HEAD_DIM), lambda h: (h, 0, 0))
    return pl.pallas_call(
        body, grid=(HEADS,), in_specs=[head, head, head, pl.BlockSpec((None, 2 * KH - 1, GRID_W, GRID_W), lambda h: (h, 0, 0, 0))],
        out_specs=head, out_shape=SDS((HEADS, SEQ, HEAD_DIM), F32), name="na_fwd",
        scratch_shapes=[pltpu.VMEM((NA_TYPES, NA_QB, NA_KB), F32)],
        compiler_params=_params(("parallel",)))(q, k, v, bias)


def _na_bwd(q, k, v, bias, do):
    def body(q_ref, k_ref, v_ref, slab_ref, do_ref, dq_ref, dk_ref, dv_ref, dslab_ref, b_ref, db_ref):
        _bias_tiles(slab_ref, b_ref)
        km = k_ref[0:N_META, :].astype(BF16)
        vm = v_ref[0:N_META, :].astype(BF16)
        dk_ref[...] = jnp.zeros_like(dk_ref)
        dv_ref[...] = jnp.zeros_like(dv_ref)
        db_ref[...] = jnp.zeros_like(db_ref)

        qm = q_ref[0:N_META, :].astype(BF16)
        dom = do_ref[0:N_META, :].astype(BF16)
        pmm = _meta_probs(qm, km)
        dpm = _dot_nt(dom, vm)
        dsm = (pmm * (dpm - jnp.sum(pmm * dpm, axis=-1, keepdims=True)) * ATT_SCALE).astype(BF16)
        dq_ref[0:N_META, :] = _dot(dsm, km)
        dkm0 = _dot_tn(dsm, qm)
        dvm0 = _dot_tn(pmm.astype(BF16), dom)

        def block(g, carry):
            dkm, dvm = carry
            block_type, q0, k0 = _block_geometry(g)
            qb = q_ref[pl.ds(q0, NA_QB), :].astype(BF16)
            kk = k_ref[pl.ds(k0, NA_KB), :].astype(BF16)
            vv = v_ref[pl.ds(k0, NA_KB), :].astype(BF16)
            dob = do_ref[pl.ds(q0, NA_QB), :].astype(BF16)
            p, pm = _na_probs(qb, kk, km, b_ref[block_type])
            dp = _dot_nt(dob, vv)
            dpm_ = _dot_nt(dob, vm)
            delta = jnp.sum(p * dp, axis=-1, keepdims=True) + jnp.sum(pm * dpm_, axis=-1, keepdims=True)
            ds = p * (dp - delta)
            dsm_ = pm * (dpm_ - delta)
            db_ref[block_type] += ds
            dsb = (ds * ATT_SCALE).astype(BF16)
            dsmb = (dsm_ * ATT_SCALE).astype(BF16)
            dq_ref[pl.ds(q0, NA_QB), :] = _dot(dsb, kk) + _dot(dsmb, km)
            dk_ref[pl.ds(k0, NA_KB), :] += _dot_tn(dsb, qb)
            dv_ref[pl.ds(k0, NA_KB), :] += _dot_tn(p.astype(BF16), dob)
            return dkm + _dot_tn(dsmb, qb), dvm + _dot_tn(pm.astype(BF16), dob)

        dkm, dvm = lax.fori_loop(0, NA_BLOCKS, block, (dkm0, dvm0))
        dk_ref[0:N_META, :] = dkm
        dv_ref[0:N_META, :] = dvm
        _bias_tiles_bwd(db_ref, dslab_ref)

    head = pl.BlockSpec((None, SEQ, HEAD_DIM), lambda h: (h, 0, 0))
    bspec = pl.BlockSpec((None, 2 * KH - 1, GRID_W, GRID_W), lambda h: (h, 0, 0, 0))
    return pl.pallas_call(
        body, grid=(HEADS,), in_specs=[head, head, head, bspec, head], out_specs=[head, head, head, bspec],
        out_shape=[SDS((HEADS, SEQ, HEAD_DIM), F32)] * 3 + [SDS((HEADS, 2 * KH - 1, GRID_W, GRID_W), F32)],
        scratch_shapes=[pltpu.VMEM((NA_TYPES, NA_QB, NA_KB), F32), pltpu.VMEM((NA_TYPES, NA_QB, NA_KB), F32)],
        name="na_bwd", compiler_params=_params(("parallel",)))(q, k, v, bias, do)


def _cmul(ar, ai, br, bi):
    return ar * br - ai * bi, ar * bi + ai * br


def _cpow(ar, ai, n):
    rr, ri = None, None
    br, bi = ar, ai
    while n:
        if n & 1:
            rr, ri = (br, bi) if rr is None else _cmul(rr, ri, br, bi)
        n >>= 1
        if n:
            br, bi = _cmul(br, bi, br, bi)
    return rr, ri


def _s5_prep(lr, li, logdt, bre, bim):
    def body(lr_ref, li_ref, dt_ref, br_ref, bi_ref, lbr_ref, lbi_ref, bbr_ref, bbi_ref):
        lr_, li_ = lr_ref[...], li_ref[...]
        dt = jnp.exp(dt_ref[...])
        mag = jnp.exp(lr_ * dt)
        lbr = mag * jnp.cos(li_ * dt)
        lbi = mag * jnp.sin(li_ * dt)
        lbr_ref[...] = lbr
        lbi_ref[...] = lbi
        den = lr_ * lr_ + li_ * li_
        xr = lbr - 1.0
        cr = (xr * lr_ + lbi * li_) / den
        ci = (lbi * lr_ - xr * li_) / den
        br, bi = br_ref[...], bi_ref[...]
        bbr_ref[...] = cr[:, None, :] * br - ci[:, None, :] * bi
        bbi_ref[...] = cr[:, None, :] * bi + ci[:, None, :] * br

    n = 2 * S5_GROUPS
    return pl.pallas_call(
        body, out_shape=[SDS((n, S5_STATE), F32)] * 2 + [SDS((n, S5_GROUP, S5_STATE), F32)] * 2,
        name="s5_prep", compiler_params=_params())(lr, li, logdt, bre, bim)


def _s5_prep_bwd(lr, li, logdt, bre, bim, dar, dai, dbbr, dbbi):
    def body(lr_ref, li_ref, dt_ref, br_ref, bi_ref, dar_ref, dai_ref, dbr_ref, dbi_ref,
             glr_ref, gli_ref, gdt_ref, gbr_ref, gbi_ref):
        lr_, li_ = lr_ref[...], li_ref[...]
        dt = jnp.exp(dt_ref[...])
        mag = jnp.exp(lr_ * dt)
        lbr = mag * jnp.cos(li_ * dt)
        lbi = mag * jnp.sin(li_ * dt)
        den = lr_ * lr_ + li_ * li_
        xr = lbr - 1.0
        cr = (xr * lr_ + lbi * li_) / den
        ci = (lbi * lr_ - xr * li_) / den
        br, bi = br_ref[...], bi_ref[...]
        dbr, dbi = dbr_ref[...], dbi_ref[...]
        gbr_ref[...] = cr[:, None, :] * dbr + ci[:, None, :] * dbi
        gbi_ref[...] = cr[:, None, :] * dbi - ci[:, None, :] * dbr
        gcr = jnp.sum(dbr * br + dbi * bi, axis=1)
        gci = jnp.sum(dbi * br - dbr * bi, axis=1)
        ilr, ili = lr_ / den, li_ / den
        tr, ti = _cmul(gcr, gci, ilr, ili)
        glbr = dar_ref[...] + tr
        glbi = dai_ref[...] + ti
        dr_, di_ = _cmul(tr, ti, cr, -ci)
        gwr, gwi = _cmul(glbr, glbi, lbr, -lbi)
        glr_ref[...] = gwr * dt - dr_
        gli_ref[...] = gwi * dt - di_
        gdt_ref[...] = jnp.sum(gwr * lr_ + gwi * li_, axis=-1, keepdims=True) * dt

    n = 2 * S5_GROUPS
    return pl.pallas_call(
        body, out_shape=[SDS((n, S5_STATE), F32)] * 2 + [SDS((n, 1), F32)] + [SDS((n, S5_GROUP, S5_STATE), F32)] * 2,
        name="s5_prep_bwd", compiler_params=_params())(lr, li, logdt, bre, bim, dar, dai, dbbr, dbbi)


def _scan_local(xr_ref, xi_ref, ar8, ai8, reverse):
    def step(i, carry):
        sr, si = carry
        idx = (SCAN_T - 1 - i) if reverse else i
        rows = pl.ds(pl.multiple_of(idx * SCAN_BLOCKS, SCAN_BLOCKS), SCAN_BLOCKS)
        nr = ar8 * sr - ai8 * si + xr_ref[rows, :]
        ni = ar8 * si + ai8 * sr + xi_ref[rows, :]
        xr_ref[rows, :] = nr
        xi_ref[rows, :] = ni
        return nr, ni

    z = jnp.zeros(ar8.shape, F32)
    return lax.fori_loop(0, SCAN_T, step, (z, z))


def _scan_carries(er, ei, atr, ati, reverse):
    row = lax.broadcasted_iota(jnp.int32, er.shape, 0)
    cr = jnp.zeros((1, er.shape[1]), F32)
    ci = cr
    outr = jnp.zeros(er.shape, F32)
    outi = outr
    order = range(SCAN_BLOCKS - 1, -1, -1) if reverse else range(SCAN_BLOCKS)
    for b in order:
        outr = jnp.where(row == b, cr, outr)
        outi = jnp.where(row == b, ci, outi)
        nr, ni = _cmul(atr, ati, cr, ci)
        cr, ci = nr + er[b:b + 1, :], ni + ei[b:b + 1, :]
    return outr, outi


def _scan_fixup(xr_ref, xi_ref, cr8, ci8, ar8, ai8, reverse):
    def step(i, carry):
        pr, pi = carry
        idx = (SCAN_T - 1 - i) if reverse else i
        rows = pl.ds(pl.multiple_of(idx * SCAN_BLOCKS, SCAN_BLOCKS), SCAN_BLOCKS)
        fr, fi = _cmul(pr, pi, cr8, ci8)
        xr_ref[rows, :] += fr
        xi_ref[rows, :] += fi
        return _cmul(pr, pi, ar8, ai8)

    lax.fori_loop(0, SCAN_T, step, (ar8, ai8))


def _scan(xr_ref, xi_ref, ar, ai, reverse):
    n = ar.shape[1]
    ar8 = jnp.broadcast_to(ar, (SCAN_BLOCKS, n))
    ai8 = jnp.broadcast_to(ai, (SCAN_BLOCKS, n))
    er, ei = _scan_local(xr_ref, xi_ref, ar8, ai8, reverse)
    atr, ati = _cpow(ar, ai, SCAN_T)
    cr8, ci8 = _scan_carries(er, ei, atr, ati, reverse)
    _scan_fixup(xr_ref, xi_ref, cr8, ci8, ar8, ai8, reverse)


def _s5_specs():
    chan = pl.BlockSpec((SEQ, CH_W), lambda c, d: (0, c))
    chan2 = pl.BlockSpec((None, SEQ, CH_W), lambda c, d: (d, 0, c))
    state = pl.BlockSpec((None, SEQ, ST_W), lambda c, d: (d, 0, c))
    bmat = pl.BlockSpec((None, None, CH_W, ST_W), lambda c, d: (d, c, 0, 0))
    cmat = pl.BlockSpec((None, None, ST_W, CH_W), lambda c, d: (d, c, 0, 0))
    avec = pl.BlockSpec((None, None, 1, ST_W), lambda c, d: (d, c, 0, 0))
    return chan, chan2, state, bmat, cmat, avec


def _scan_by_direction(xr_ref, xi_ref, ar, ai, d, adjoint):
    @pl.when(d == 0)
    def _():
        _scan(xr_ref, xi_ref, ar, ai, reverse=adjoint)

    @pl.when(d == 1)
    def _():
        _scan(xr_ref, xi_ref, ar, ai, reverse=not adjoint)


def _s5_scan_fwd(u, bre, bim, are, aim, cre, cim):
    def body(u_ref, bre_ref, bim_ref, are_ref, aim_ref, cre_ref, cim_ref, sr_ref, si_ref, y_ref):
        ub = u_ref[...].astype(BF16)
        sr_ref[...] = _dot(ub, bre_ref[...])
        si_ref[...] = _dot(ub, bim_ref[...])
        _scan_by_direction(sr_ref, si_ref, are_ref[...], aim_ref[...], pl.program_id(1), adjoint=False)
        y_ref[...] = _dot(sr_ref[...].astype(BF16), cre_ref[...]) - _dot(si_ref[...].astype(BF16), cim_ref[...])

    chan, chan2, state, bmat, cmat, avec = _s5_specs()
    return pl.pallas_call(
        body, grid=(S5_CHUNKS, 2), in_specs=[chan, bmat, bmat, avec, avec, cmat, cmat], out_specs=[state, state, chan2],
        out_shape=[SDS((2, SEQ, S5_GROUPS * S5_STATE), F32)] * 2 + [SDS((2, SEQ, S5_WIDTH), F32)],
        name="s5_scan_fwd", compiler_params=_params(("parallel", "parallel")))(u, bre, bim, are, aim, cre, cim)


def _dlam(gr_ref, gi_ref, sr_ref, si_ref, reverse):
    tile = lambda i: pl.ds(pl.multiple_of(i * SCAN_BLOCKS, SCAN_BLOCKS), SCAN_BLOCKS)
    row = lax.broadcasted_iota(jnp.int32, (SCAN_BLOCKS, ST_W), 0)
    if reverse:
        edge, src, shift, empty, lo, hi, dprev = SCAN_T - 1, 0, SCAN_BLOCKS - 1, SCAN_BLOCKS - 1, 0, SCAN_T - 1, 1
    else:
        edge, src, shift, empty, lo, hi, dprev = 0, SCAN_T - 1, 1, 0, 1, SCAN_T, -1
    spr = jnp.where(row == empty, 0.0, pltpu.roll(sr_ref[tile(src), :], shift, 0))
    spi = jnp.where(row == empty, 0.0, pltpu.roll(si_ref[tile(src), :], shift, 0))
    acc0 = _cmul(gr_ref[tile(edge), :], gi_ref[tile(edge), :], spr, -spi)

    def step(i, carry):
        accr, acci = carry
        pr, pi = _cmul(gr_ref[tile(i), :], gi_ref[tile(i), :], sr_ref[tile(i + dprev), :], -si_ref[tile(i + dprev), :])
        return accr + pr, acci + pi

    accr, acci = lax.fori_loop(lo, hi, step, acc0)
    return jnp.sum(accr, axis=0, keepdims=True), jnp.sum(acci, axis=0, keepdims=True)


def _s5_scan_bwd(dy, du_skip, u, sr, si, bre, bim, are, aim, cre, cim):
    def body(dy_ref, dus_ref, u_ref, sr_ref, si_ref, bre_ref, bim_ref, are_ref, aim_ref, cre_ref, cim_ref,
             du_ref, dbr_ref, dbi_ref, dcr_ref, dci_ref, dar_ref, dai_ref, gr_ref, gi_ref):
        d = pl.program_id(1)
        dyb = dy_ref[...].astype(BF16)
        gr_ref[...] = _dot_nt(dyb, cre_ref[...])
        gi_ref[...] = -_dot_nt(dyb, cim_ref[...])
        dcr_ref[...] = _dot_tn(sr_ref[...].astype(BF16), dyb)
        dci_ref[...] = -_dot_tn(si_ref[...].astype(BF16), dyb)
        _scan_by_direction(gr_ref, gi_ref, are_ref[...], -aim_ref[...], d, adjoint=True)

        @pl.when(d == 0)
        def _():
            dar_ref[...], dai_ref[...] = _dlam(gr_ref, gi_ref, sr_ref, si_ref, reverse=False)
            du_ref[...] = dus_ref[...]

        @pl.when(d == 1)
        def _():
            dar_ref[...], dai_ref[...] = _dlam(gr_ref, gi_ref, sr_ref, si_ref, reverse=True)

        grb = gr_ref[...].astype(BF16)
        gib = gi_ref[...].astype(BF16)
        du_ref[...] += _dot_nt(grb, bre_ref[...]) + _dot_nt(gib, bim_ref[...])
        ub = u_ref[...].astype(BF16)
        dbr_ref[...] = _dot_tn(ub, grb)
        dbi_ref[...] = _dot_tn(ub, gib)

    chan, _, state, bmat, cmat, avec = _s5_specs()
    return pl.pallas_call(
        body, grid=(S5_CHUNKS, 2), in_specs=[chan, chan, chan, state, state, bmat, bmat, avec, avec, cmat, cmat],
        out_specs=[chan, bmat, bmat, cmat, cmat, avec, avec],
        out_shape=[SDS((SEQ, S5_WIDTH), F32)] + [SDS((2, S5_CHUNKS, CH_W, ST_W), F32)] * 2
                  + [SDS((2, S5_CHUNKS, ST_W, CH_W), F32)] * 2 + [SDS((2, S5_CHUNKS, 1, ST_W), F32)] * 2,
        scratch_shapes=[pltpu.VMEM((SEQ, ST_W), F32), pltpu.VMEM((SEQ, ST_W), F32)],
        name="s5_scan_bwd", compiler_params=_params(("parallel", "arbitrary")))(dy, du_skip, u, sr, si, bre, bim, are, aim, cre, cim)


_GELU_K = math.sqrt(2.0 / math.pi)
_GELU_C = 0.044715


def _gelu(x):
    t = jnp.tanh(_GELU_K * (x + _GELU_C * x * x * x))
    return 0.5 * x * (1.0 + t), t


def _s5_glu_fwd(u, y2, dskip, wglu, bglu):
    def body(u_ref, y0_ref, y1_ref, d_ref, w_ref, b_ref, o_ref, yp_ref):
        ypre = u_ref[...] * d_ref[...] + y0_ref[...] + y1_ref[...]
        yp_ref[...] = ypre
        y, _ = _gelu(ypre)
        z = _dot(y.astype(BF16), w_ref[...]) + b_ref[...]
        o_ref[...] = y * jax.nn.sigmoid(z)

    row = _row_spec(S5_WIDTH)
    vec = _fix_spec((1, S5_WIDTH))
    dir0 = pl.BlockSpec((None, ROW_TILE, S5_WIDTH), lambda i: (0, i, 0))
    dir1 = pl.BlockSpec((None, ROW_TILE, S5_WIDTH), lambda i: (1, i, 0))
    return pl.pallas_call(
        body, grid=(N_ROW_TILES,), in_specs=[row, dir0, dir1, vec, _fix_spec((S5_WIDTH, S5_WIDTH)), vec],
        out_specs=[row, row], out_shape=[SDS((SEQ, S5_WIDTH), F32)] * 2, name="s5_glu_fwd",
        compiler_params=_params(("parallel",)))(u, y2, y2, dskip, wglu, bglu)


def _s5_glu_bwd(do, ypre, u, dskip, wglu, bglu):
    def body(do_ref, yp_ref, u_ref, d_ref, w_ref, b_ref, dyp_ref, du_ref, dw_ref, db_ref, dd_ref):
        i = pl.program_id(0)
        ypre = yp_ref[...]
        y, t = _gelu(ypre)
        yb = y.astype(BF16)
        sg = jax.nn.sigmoid(_dot(yb, w_ref[...]) + b_ref[...])
        dov = do_ref[...]
        dz = dov * y * sg * (1.0 - sg)
        dzb = dz.astype(BF16)
        dy = dov * sg + _dot_nt(dzb, w_ref[...])
        dgelu = 0.5 * (1.0 + t) + 0.5 * ypre * (1.0 - t * t) * _GELU_K * (1.0 + 3.0 * _GELU_C * ypre * ypre)
        dyp = dy * dgelu
        dyp_ref[...] = dyp
        uv = u_ref[...]
        du_ref[...] = dyp * d_ref[...]

        @pl.when(i == 0)
        def _():
            dw_ref[...] = jnp.zeros_like(dw_ref)
            db_ref[...] = jnp.zeros_like(db_ref)
            dd_ref[...] = jnp.zeros_like(dd_ref)

        dw_ref[...] += _dot_tn(yb, dzb)
        db_ref[...] += jnp.sum(dz, axis=0, keepdims=True)
        dd_ref[...] += jnp.sum(dyp * uv, axis=0, keepdims=True)

    row = _row_spec(S5_WIDTH)
    vec = _fix_spec((1, S5_WIDTH))
    mat = _fix_spec((S5_WIDTH, S5_WIDTH))
    return pl.pallas_call(
        body, grid=(N_ROW_TILES,), in_specs=[row, row, row, vec, mat, vec], out_specs=[row, row, mat, vec, vec],
        out_shape=[SDS((SEQ, S5_WIDTH), F32)] * 2 + [SDS((S5_WIDTH, S5_WIDTH), F32), SDS((1, S5_WIDTH), F32), SDS((1, S5_WIDTH), F32)],
        name="s5_glu_bwd", compiler_params=_params(("arbitrary",)))(do, ypre, u, dskip, wglu, bglu)


def _mix_out_fwd(ona, os5, g_na, g_s5, wout):
    def body(a_ref, s_ref, ga_ref, gs_ref, w_ref, o_ref):
        av, sv = a_ref[...], s_ref[...]
        ca = (av * _rstd(av) * ga_ref[...]).astype(BF16)
        cs = (sv * _rstd(sv) * gs_ref[...]).astype(BF16)
        o_ref[...] = _dot(ca, w_ref[0:NA_WIDTH, :]) + _dot(cs, w_ref[NA_WIDTH:, :])

    row = _row_spec(NA_WIDTH)
    vec = _fix_spec((1, NA_WIDTH))
    return pl.pallas_call(
        body, grid=(N_ROW_TILES,), in_specs=[row, row, vec, vec, _fix_spec((D_MODEL, D_MODEL))],
        out_specs=_row_spec(D_MODEL), out_shape=SDS((SEQ, D_MODEL), F32), name="mix_out_fwd",
        compiler_params=_params(("parallel",)))(ona, os5, g_na, g_s5, wout)


def _mix_out_bwd(dmix, ona, os5, g_na, g_s5, wout):
    def body(dm_ref, a_ref, s_ref, ga_ref, gs_ref, w_ref, da_ref, ds_ref, dw_ref, dga_ref, dgs_ref):
        i = pl.program_id(0)
        dm = dm_ref[...]
        av, sv = a_ref[...], s_ref[...]
        ra, rs = _rstd(av), _rstd(sv)
        ga, gs = ga_ref[...], gs_ref[...]
        ca = (av * ra * ga).astype(BF16)
        cs = (sv * rs * gs).astype(BF16)
        dca = _dot_nt(dm, w_ref[0:NA_WIDTH, :])
        dcs = _dot_nt(dm, w_ref[NA_WIDTH:, :])
        da, dga = _rms_bwd(av, ra, ga, dca)
        ds, dgs = _rms_bwd(sv, rs, gs, dcs)
        da_ref[...] = da
        ds_ref[...] = ds

        @pl.when(i == 0)
        def _():
            dw_ref[...] = jnp.zeros_like(dw_ref)
            dga_ref[...] = jnp.zeros_like(dga_ref)
            dgs_ref[...] = jnp.zeros_like(dgs_ref)

        dw_ref[0:NA_WIDTH, :] += _dot_tn(ca, dm)
        dw_ref[NA_WIDTH:, :] += _dot_tn(cs, dm)
        dga_ref[...] += jnp.sum(dga, axis=0, keepdims=True)
        dgs_ref[...] += jnp.sum(dgs, axis=0, keepdims=True)

    row = _row_spec(NA_WIDTH)
    vec = _fix_spec((1, NA_WIDTH))
    mat = _fix_spec((D_MODEL, D_MODEL))
    return pl.pallas_call(
        body, grid=(N_ROW_TILES,), in_specs=[_row_spec(D_MODEL), row, row, vec, vec, mat],
        out_specs=[row, row, mat, vec, vec],
        out_shape=[SDS((SEQ, NA_WIDTH), F32)] * 2 + [SDS((D_MODEL, D_MODEL), F32), SDS((1, NA_WIDTH), F32), SDS((1, NA_WIDTH), F32)],
        name="mix_out_bwd", compiler_params=_params(("arbitrary",)))(dmix, ona, os5, g_na, g_s5, wout)


def _me():
    x, y, c = lax.axis_index("x"), lax.axis_index("y"), lax.axis_index("c")
    return x, y, c, 4 * x + 2 * y + c


def _peer(k):
    x, y, c, _ = _me()
    px = 1 - x if (k >> 2) & 1 else x
    py = 1 - y if (k >> 1) & 1 else y
    pc = 1 - c if k & 1 else c
    return (px, py, pc), 4 * px + 2 * py + pc


ALL_PEERS = (1, 2, 3, 4, 5, 6, 7)
CHIP_PEERS = (2, 4, 6)
SIBLING = 1


def _slot8(pos):
    return 4 * pos[0] + 2 * pos[1] + pos[2]


def _slot4(pos):
    return 2 * pos[0] + pos[1]


def _exchange(arrays, gather, name, after=()):
    n, n_after = len(arrays), len(after)

    def body(*refs):
        ins, outs = refs[:n], refs[n + n_after:2 * n + n_after]
        send_sems, recv_sems, local_sems = refs[2 * n + n_after:]
        _, _, _, me = _me()
        started = []
        for a in range(n):
            src_mine = ins[a] if gather else ins[a].at[me]
            local = pltpu.make_async_copy(src_mine, outs[a].at[me], local_sems.at[a])
            local.start()
            started.append(local)
        sends = []
        for k in range(1, N_DEV):
            peer, peer_idx = _peer(k)
            for a in range(n):
                src = ins[a] if gather else ins[a].at[peer_idx]
                cp = pltpu.make_async_remote_copy(src_ref=src, dst_ref=outs[a].at[me], send_sem=send_sems.at[a, k - 1],
                                                  recv_sem=recv_sems.at[a, k - 1], device_id=peer, device_id_type=MESH)
                cp.start()
                sends.append(cp)
        for k in range(1, N_DEV):
            peer, peer_idx = _peer(k)
            for a in range(n):
                src = ins[a] if gather else ins[a].at[peer_idx]
                pltpu.make_async_remote_copy(src_ref=src, dst_ref=outs[a].at[peer_idx], send_sem=send_sems.at[a, k - 1],
                                             recv_sem=recv_sems.at[a, k - 1], device_id=peer, device_id_type=MESH).wait_recv()
        for cp in sends:
            cp.wait_send()
        for local in started:
            local.wait()

    hbm = pl.BlockSpec(memory_space=pltpu.HBM)
    out_shape = [SDS((N_DEV,) + tuple(a.shape), a.dtype) if gather else SDS(a.shape, a.dtype) for a in arrays]
    return pl.pallas_call(
        body, in_specs=[hbm] * n + [pl.BlockSpec(memory_space=pl.ANY)] * n_after, out_specs=[hbm] * n, out_shape=out_shape,
        scratch_shapes=[pltpu.SemaphoreType.DMA((n, N_DEV - 1)), pltpu.SemaphoreType.DMA((n, N_DEV - 1)),
                        pltpu.SemaphoreType.DMA((n,))],
        name=name)(*arrays, *after)


_HBM = pl.BlockSpec(memory_space=pltpu.HBM)
_SEM = pl.BlockSpec(memory_space=pltpu.SEMAPHORE)
_EFFECT = pltpu.SideEffectType.DATAFLOW_SIDE_EFFECTING


def _land_shape(a, gather):
    return (N_DEV,) + tuple(a.shape) if gather else tuple(a.shape)


def _place_own(arrays, gather, name, slot=_slot8):
    n = len(arrays)
    me = slot(_me()[:3])

    def body(me_ref, *refs):
        for a in range(n):
            refs[n + a][...] = refs[a][...]

    def own_slot(a):
        zeros = (0,) * (a.ndim - (0 if gather else 1))
        return lambda i, me_ref: (me_ref[0],) + zeros

    def whole(a):
        return lambda i, me_ref: (0,) * a.ndim

    in_specs = [pl.BlockSpec(a.shape, whole(a)) if gather else pl.BlockSpec((None,) + a.shape[1:], own_slot(a)) for a in arrays]
    out_specs = [pl.BlockSpec((None,) + (a.shape if gather else a.shape[1:]), own_slot(a)) for a in arrays]
    return pl.pallas_call(
        body, grid_spec=pltpu.PrefetchScalarGridSpec(num_scalar_prefetch=1, grid=(1,), in_specs=in_specs, out_specs=out_specs),
        out_shape=[SDS(_land_shape(a, gather), a.dtype) for a in arrays], name=name,
        compiler_params=_params(("arbitrary",)))(me.reshape(1).astype(jnp.int32), *arrays)


def _exchange_start(arrays, lands, gather, name, peers=ALL_PEERS, slot=_slot8):
    n = len(arrays)

    def body(*refs):
        ins, lnd = refs[:n], refs[n:2 * n]
        send_sems, recv_sems = refs[2 * n], refs[2 * n + 1]
        token = refs[-1]
        me = slot(_me()[:3])
        for i, k in enumerate(peers):
            peer, _ = _peer(k)
            for a in range(n):
                src = ins[a] if gather else ins[a].at[slot(peer)]
                s = a * len(peers) + i
                pltpu.make_async_remote_copy(src_ref=src, dst_ref=lnd[a].at[me], send_sem=send_sems.at[s],
                                             recv_sem=recv_sems.at[s], device_id=peer, device_id_type=MESH).start()
        token[...] = jnp.zeros_like(token)

    sems = pltpu.SemaphoreType.DMA((n * len(peers),))
    out = pl.pallas_call(
        body, name=name, in_specs=[_HBM] * (2 * n),
        out_shape=(sems, sems) + tuple(pltpu.HBM(a.shape, a.dtype) for a in list(arrays) + list(lands)) + (SDS((8, 128), F32),),
        out_specs=(_SEM, _SEM) + (_HBM,) * (2 * n) + (pl.BlockSpec(memory_space=pltpu.VMEM),),
        input_output_aliases={i: 2 + i for i in range(2 * n)},
        compiler_params=pltpu.CompilerParams(has_side_effects=_EFFECT),
    )(*[pltpu.with_memory_space_constraint(a, pltpu.HBM) for a in list(arrays) + list(lands)])
    return out[0], out[1], list(out[2:2 + n]), list(out[2 + n:2 + 2 * n]), out[-1]


def _exchange_wait(send_sems, recv_sems, arrays, lands, after, gather, name, peers=ALL_PEERS, slot=_slot8):
    n = len(arrays)

    def body(*refs):
        ins, lnd = refs[:n], refs[n:2 * n]
        send_sems, recv_sems = refs[2 * n], refs[2 * n + 1]
        for i, k in enumerate(peers):
            peer, _ = _peer(k)
            for a in range(n):
                src = ins[a] if gather else ins[a].at[slot(peer)]
                s = a * len(peers) + i
                cp = pltpu.make_async_remote_copy(src_ref=src, dst_ref=lnd[a].at[slot(peer)], send_sem=send_sems.at[s],
                                                  recv_sem=recv_sems.at[s], device_id=peer, device_id_type=MESH)
                cp.wait_send()
                cp.wait_recv()

        refs[-1][...] = jnp.zeros_like(refs[-1])

    after = list(after) if isinstance(after, (list, tuple)) else [after]
    out = pl.pallas_call(
        body, name=name, in_specs=[_HBM] * (2 * n) + [_SEM, _SEM] + [pl.BlockSpec(memory_space=pl.ANY)] * len(after),
        out_shape=tuple(pltpu.HBM(a.shape, a.dtype) for a in list(arrays) + list(lands)) + (SDS((8, 128), F32),),
        out_specs=(_HBM,) * (2 * n) + (pl.BlockSpec(memory_space=pltpu.VMEM),), input_output_aliases={i: i for i in range(2 * n)},
        compiler_params=pltpu.CompilerParams(has_side_effects=_EFFECT),
    )(*arrays, *lands, send_sems, recv_sems, *after)
    return list(out[n:2 * n]), out[-1]


def _forward_sibling(lands, name):
    n = len(lands)

    def body(*refs):
        outs = refs[n:2 * n]
        send_sems, recv_sems = refs[2 * n:]
        x, y, c, _ = _me()
        sends = []
        for i, k in enumerate(CHIP_PEERS):
            peer, _ = _peer(k)
            for a in range(n):
                rows = outs[a].at[_slot8(peer)]
                cp = pltpu.make_async_remote_copy(src_ref=rows, dst_ref=rows, send_sem=send_sems.at[a, i], recv_sem=recv_sems.at[a, i],
                                                  device_id=(x, y, 1 - c), device_id_type=MESH)
                cp.start()
                sends.append(cp)
        for i, k in enumerate(CHIP_PEERS):
            (px, py, pc), _ = _peer(k)
            for a in range(n):
                rows = outs[a].at[_slot8((px, py, 1 - pc))]
                pltpu.make_async_remote_copy(src_ref=rows, dst_ref=rows, send_sem=send_sems.at[a, i], recv_sem=recv_sems.at[a, i],
                                             device_id=(x, y, 1 - c), device_id_type=MESH).wait_recv()
        for cp in sends:
            cp.wait_send()

    return pl.pallas_call(
        body, in_specs=[_HBM] * n, out_specs=[_HBM] * n, out_shape=[SDS(a.shape, a.dtype) for a in lands],
        input_output_aliases={i: i for i in range(n)},
        scratch_shapes=[pltpu.SemaphoreType.DMA((n, len(CHIP_PEERS))), pltpu.SemaphoreType.DMA((n, len(CHIP_PEERS)))],
        name=name)(*lands)


def _swap_sibling(arrays, name):
    n = len(arrays)
    chips = N_DEV // 2

    def body(*refs):
        ins, outs = refs[:n], refs[n:2 * n]
        send_sems, recv_sems = refs[2 * n:]
        x, y, c, _ = _me()
        sends = []
        for q in range(chips):
            for a in range(n):
                cp = pltpu.make_async_remote_copy(src_ref=ins[a].at[q, 1 - c], dst_ref=outs[a].at[q], send_sem=send_sems.at[a, q],
                                                  recv_sem=recv_sems.at[a, q], device_id=(x, y, 1 - c), device_id_type=MESH)
                cp.start()
                sends.append(cp)
        for cp in sends:
            cp.wait_recv()
        for cp in sends:
            cp.wait_send()

    return pl.pallas_call(
        body, in_specs=[_HBM] * n, out_specs=[_HBM] * n, out_shape=[SDS((chips,) + a.shape[2:], a.dtype) for a in arrays],
        scratch_shapes=[pltpu.SemaphoreType.DMA((n, chips)), pltpu.SemaphoreType.DMA((n, chips))], name=name)(*arrays)


def _sum_pairs(mine, theirs, name):
    chips, _, rows, cols = mine.shape
    c = lax.axis_index("c")

    def body(c_ref, a_ref, b_ref, o_ref):
        o_ref[...] = (a_ref[...].astype(F32) + b_ref[...].astype(F32)).astype(o_ref.dtype)

    return pl.pallas_call(
        body, grid_spec=pltpu.PrefetchScalarGridSpec(
            num_scalar_prefetch=1, grid=(chips,),
            in_specs=[pl.BlockSpec((None, None, rows, cols), lambda q, c_ref: (q, c_ref[0], 0, 0)),
                      pl.BlockSpec((None, rows, cols), lambda q, c_ref: (q, 0, 0))],
            out_specs=pl.BlockSpec((None, rows, cols), lambda q, c_ref: (q, 0, 0))),
        out_shape=SDS((chips, rows, cols), mine.dtype), name=name,
        compiler_params=_params(("parallel",)))(c.reshape(1).astype(jnp.int32), mine, theirs)


def _adamw_math(w, g, m, v):
    m = ADAM_B1 * m + (1.0 - ADAM_B1) * g
    v = ADAM_B2 * v + (1.0 - ADAM_B2) * (g * g)
    m_hat = m / (1.0 - ADAM_B1 ** ADAM_STEP)
    v_hat = v / (1.0 - ADAM_B2 ** ADAM_STEP)
    delta = -ADAM_LR * (m_hat / (jnp.sqrt(v_hat) + ADAM_EPS) + ADAM_WD * w)
    return delta, m, v


def _adamw(w, m, v, pieces, name):
    rows, cols = w.shape[-2:]
    lead = w.ndim - 2
    tile = rows
    for cand in (256, 176, 128, 64, 16):
        if rows > cand and rows % cand == 0:
            tile = cand
            break

    def body(w_ref, m_ref, v_ref, p_ref, g_ref, d_ref, mo_ref, vo_ref):
        g = _sum_pieces(p_ref)
        g_ref[...] = g
        d_ref[...], mo_ref[...], vo_ref[...] = _adamw_math(w_ref[...], g, m_ref[...], v_ref[...])

    blk = pl.BlockSpec((None,) * lead + (tile, cols), lambda i: (0,) * lead + (i, 0))
    return pl.pallas_call(
        body, grid=(rows // tile,), in_specs=[blk, blk, blk, pl.BlockSpec((pieces.shape[0], tile, cols), lambda i: (0, i, 0))],
        out_specs=[blk] * 4, out_shape=[SDS(w.shape, F32)] * 4, name=name,
        compiler_params=_params(("parallel",)))(w, m, v, pieces)


def _sum_pieces(p_ref):
    g = p_ref[0].astype(F32)
    for p in range(1, p_ref.shape[0]):
        g = g + p_ref[p].astype(F32)
    return g


def _adamw_s5_mat(w, m, v, g, name):
    _, ndir, groups, b, c = w.shape
    per_dir = groups // 8

    def body(w_ref, m_ref, v_ref, g_ref, d_ref, mo_ref, vo_ref):
        d_ref[...], mo_ref[...], vo_ref[...] = _adamw_math(w_ref[...], g_ref[...], m_ref[...], v_ref[...])

    blk = pl.BlockSpec((None, None, 8, b, c), lambda i: (0, i // per_dir, i % per_dir, 0, 0))
    return pl.pallas_call(
        body, grid=(ndir * per_dir,), in_specs=[blk] * 4, out_specs=[blk] * 3, out_shape=[SDS(w.shape, F32)] * 3, name=name,
        compiler_params=_params(("parallel",)))(w, m, v, g)


VEC_ROWS = ['ffn1_pre_g', 'ffn1_post_g', 'mix_pre_g', 'mix_post_g', 'ffn2_pre_g', 'ffn2_post_g', 'final_g',
            ('na_out_g', 's5_out_g'), ('s5_d', 's5_b_glu')]
VEC_NAMES = [n for row in VEC_ROWS for n in ((row,) if isinstance(row, str) else row)]
VEC_PACK_ROWS = 16


def _pack_vectors(grads):
    def body(*refs):
        o_ref = refs[-1]
        o_ref[...] = jnp.zeros_like(o_ref)
        k = 0
        for i, row in enumerate(VEC_ROWS):
            if isinstance(row, str):
                o_ref[i:i + 1, :] = refs[k][...]
                k += 1
            else:
                o_ref[i:i + 1, 0:NA_WIDTH] = refs[k][...]
                o_ref[i:i + 1, NA_WIDTH:] = refs[k + 1][...]
                k += 2

    return pl.pallas_call(body, out_shape=SDS((VEC_PACK_ROWS, D_MODEL), F32), name="pack_vectors",
                          compiler_params=_params())(*[grads[n] for n in VEC_NAMES])


def _sum8(pieces, name):
    def body(p_ref, o_ref):
        o_ref[...] = _sum_pieces(p_ref)

    return pl.pallas_call(body, out_shape=SDS(pieces.shape[1:], F32), name=name, compiler_params=_params())(pieces)


def _adamw_small(packed8, vec_wmv, others):
    n_vec, n_oth = len(VEC_NAMES), len(others)

    def body(*refs):
        p_ref = refs[0]
        ins = refs[1:1 + 3 * n_vec + 4 * n_oth]
        outs = refs[1 + 3 * n_vec + 4 * n_oth:]
        gsum = _sum_pieces(p_ref)
        k = 0
        for i, row in enumerate(VEC_ROWS):
            parts = [(row, gsum[i:i + 1, :])] if isinstance(row, str) else \
                [(row[0], gsum[i:i + 1, 0:NA_WIDTH]), (row[1], gsum[i:i + 1, NA_WIDTH:])]
            for _, g in parts:
                w_ref, m_ref, v_ref = ins[3 * k:3 * k + 3]
                outs[4 * k][...] = g
                outs[4 * k + 1][...], outs[4 * k + 2][...], outs[4 * k + 3][...] = _adamw_math(w_ref[...], g, m_ref[...], v_ref[...])
                k += 1
        for j in range(n_oth):
            w_ref, m_ref, v_ref, g_ref = ins[3 * n_vec + 4 * j:3 * n_vec + 4 * j + 4]
            g = _sum_pieces(g_ref)
            g = g[tuple(slice(0, s) for s in w_ref.shape[1:])].reshape(w_ref.shape)
            o = outs[4 * (n_vec + j):4 * (n_vec + j) + 4]
            o[0][...] = g
            o[1][...], o[2][...], o[3][...] = _adamw_math(w_ref[...], g, m_ref[...], v_ref[...])

    args, out_shape = [packed8], []
    for w, m, v in vec_wmv:
        args += [w, m, v]
        out_shape += [SDS(w.shape, F32)] * 4
    for w, m, v, g in others:
        args += [w, m, v, g]
        out_shape += [SDS(w.shape, F32)] * 4
    return pl.pallas_call(body, out_shape=out_shape, name="adamw_small", compiler_params=_params())(*args)


def _perm_rows(x):
    return x.reshape(SCAN_BLOCKS, SCAN_T, x.shape[-1]).transpose(1, 0, 2).reshape(SEQ, x.shape[-1])


def _unperm_rows(x):
    return x.reshape(SCAN_T, SCAN_BLOCKS, x.shape[-1]).transpose(1, 0, 2).reshape(SEQ, x.shape[-1])


def _block_diag(x):
    eye = np.eye(8, dtype=bool)[None, None, :, None, :, None]
    full = jnp.where(eye, x[:, :, :, :, None, :], 0.0)
    return full.reshape(2, S5_CHUNKS, 8 * x.shape[3], 8 * x.shape[4])


def _diag_blocks(x, r, c):
    x6 = x.reshape(2, S5_CHUNKS, 8, r, 8, c)
    return jnp.stack([x6[:, :, g, :, g, :] for g in range(8)], axis=2)


STORED_SWAPPED = {"ffn1_w_gate": (1, 2), "ffn1_w_up": (1, 2), "ffn2_w_gate": (1, 2), "ffn2_w_up": (1, 2),
                  "s5_b_re": (3, 4), "s5_b_im": (3, 4)}


def _stored(name, x):
    return jnp.swapaxes(x, *STORED_SWAPPED[name]) if name in STORED_SWAPPED else x


def _dep(x, token):
    return x if token is None else x + token


def _local_step(x, target, get_w, small, emit):
    bias = _rpb_expand(small["na_rpb"][0])
    lr = small["s5_lam_re"].reshape(64, S5_STATE)
    li = small["s5_lam_im"].reshape(64, S5_STATE)
    logdt = small["s5_log_dt"].reshape(64, 1)
    b_t = [_stored(n, small[n]).reshape(64, S5_GROUP, S5_STATE) for n in ("s5_b_re", "s5_b_im")]
    lbr, lbi, bbr, bbi = _s5_prep(lr, li, logdt, b_t[0], b_t[1])
    are = lbr.reshape(2, S5_CHUNKS, 1, ST_W)
    aim = lbi.reshape(2, S5_CHUNKS, 1, ST_W)
    bre = _block_diag(bbr.reshape(2, S5_CHUNKS, 8, S5_GROUP, S5_STATE)).astype(BF16)
    bim = _block_diag(bbi.reshape(2, S5_CHUNKS, 8, S5_GROUP, S5_STATE)).astype(BF16)
    c_t = [small[n].reshape(2, S5_CHUNKS, 8, S5_GROUP, S5_STATE).transpose(0, 1, 2, 4, 3) for n in ("s5_c_re", "s5_c_im")]
    cre = _block_diag(c_t[0]).astype(BF16)
    cim = _block_diag(c_t[1]).astype(BF16)
    tgt = jnp.concatenate([jnp.zeros((N_META, D_MODEL), F32), target], axis=0)

    wts = dict(get_w("ffn1", [bias, are, aim, bre, bim, cre, cim, tgt]))
    h0 = jnp.concatenate([wts["meta_tokens"], x], axis=0)
    a1 = _prenorm(h0, _dep(small["ffn1_pre_g"], wts.get("token")))
    gate1, up1, f1 = _ffn_fwd(a1, wts["ffn1_w_gate"], wts["ffn1_w_up"], wts["ffn1_w_down"], "ffn1_fwd")
    h1, a2 = _post_pre(f1, h0, small["ffn1_post_g"], small["mix_pre_g"], 0.5, "post_pre1")
    wts.update(get_w("w_in", a2))
    proj = _proj_fwd(a2, wts["w_in"])
    qkv = proj[:6].reshape(3, 2, SEQ, 4, HEAD_DIM).transpose(0, 1, 3, 2, 4).reshape(3, HEADS, SEQ, HEAD_DIM)
    u = proj[6:].transpose(1, 0, 2).reshape(SEQ, S5_WIDTH)
    o3 = _na_fwd(qkv[0], qkv[1], qkv[2], bias)
    ona = o3.transpose(1, 0, 2).reshape(SEQ, NA_WIDTH)
    u_p = _perm_rows(u)
    sr, si, y2 = _s5_scan_fwd(u_p, bre, bim, are, aim, cre, cim)
    wts.update(get_w("mix", y2))
    os5_p, ypre_p = _s5_glu_fwd(u_p, y2, small["s5_d"], wts["s5_w_glu"], small["s5_b_glu"])
    os5 = _unperm_rows(os5_p)

    mix = _mix_out_fwd(ona, os5, small["na_out_g"], small["s5_out_g"], wts["w_out"])
    h2, a3 = _post_pre(mix, h1, small["mix_post_g"], small["ffn2_pre_g"], 1.0, "post_pre2")
    wts.update(get_w("ffn2", a3))
    gate2, up2, f2 = _ffn_fwd(a3, wts["ffn2_w_gate"], wts["ffn2_w_up"], wts["ffn2_w_down"], "ffn2_fwd")
    loss8, dh3, df2, g_final, g_ffn2_post = _final_loss(f2, h2, small["ffn2_post_g"], small["final_g"], tgt)

    da3, dwg2, dwu2, dwd2 = _ffn_bwd(df2, a3, gate2, up2, wts["ffn2_w_gate"], wts["ffn2_w_up"], wts["ffn2_w_down"], "ffn2_bwd")
    tok = emit("ffn2", {"ffn2_w_gate": dwg2, "ffn2_w_up": dwu2, "ffn2_w_down": dwd2})
    dh2, dmix, g_ffn2_pre, g_mix_post = _bwd_pre_post(da3, h2, _dep(small["ffn2_pre_g"], tok), dh3, mix, small["mix_post_g"], 1.0,
                                                      "bwd_pre_post2")
    dona, dos5, dwout, g_na_out, g_s5_out = _mix_out_bwd(dmix, ona, os5, small["na_out_g"], small["s5_out_g"], wts["w_out"])

    dypre_p, du_skip_p, dwglu, g_b_glu, g_s5_d = _s5_glu_bwd(_perm_rows(dos5), ypre_p, u_p, small["s5_d"], wts["s5_w_glu"],
                                                             small["s5_b_glu"])
    tok = emit("mix", {"s5_w_glu": dwglu.reshape(N_DEV, S5_WIDTH // N_DEV, S5_WIDTH).astype(BF16),
                       "w_out": dwout.reshape(N_DEV, D_MODEL // N_DEV, D_MODEL).astype(BF16)})
    du_p, dbr, dbi, dcr, dci, dar, dai = _s5_scan_bwd(dypre_p, du_skip_p, u_p, sr, si, bre, bim, _dep(are, tok), aim, cre, cim)
    du = _unperm_rows(du_p)
    dbbr = _diag_blocks(dbr, S5_GROUP, S5_STATE).reshape(64, S5_GROUP, S5_STATE)
    dbbi = _diag_blocks(dbi, S5_GROUP, S5_STATE).reshape(64, S5_GROUP, S5_STATE)
    g_lr, g_li, g_dt, g_br, g_bi = _s5_prep_bwd(lr, li, logdt, b_t[0], b_t[1], dar.reshape(64, S5_STATE),
                                                dai.reshape(64, S5_STATE), dbbr, dbbi)
    g_c = [_diag_blocks(d, S5_STATE, S5_GROUP).transpose(0, 1, 2, 4, 3).reshape(2 * S5_GROUPS, S5_GROUP, S5_STATE)
           for d in (dcr, dci)]

    do3 = dona.reshape(SEQ, HEADS, HEAD_DIM).transpose(1, 0, 2)
    dq, dk, dv, dbias = _na_bwd(qkv[0], qkv[1], qkv[2], bias, do3)
    g_rpb = _rpb_reduce(dbias)
    dense = jnp.stack([g.reshape(2 * S5_GROUPS, S5_STATE * S5_GROUP) for g in (g_br, g_bi, *g_c)])
    tok = emit("small", {"dense": dense, "na_rpb": g_rpb,
                         "s5_lam_re": g_lr.reshape(2, S5_GROUPS, S5_STATE), "s5_lam_im": g_li.reshape(2, S5_GROUPS, S5_STATE),
                         "s5_log_dt": g_dt.reshape(2, S5_GROUPS)})
    dqkv = jnp.stack([dq, dk, dv]).reshape(3, 2, 4, SEQ, HEAD_DIM).transpose(0, 1, 3, 2, 4).reshape(6, SEQ, IN_SHARD)
    dproj = jnp.concatenate([dqkv, du.reshape(SEQ, 2, IN_SHARD).transpose(1, 0, 2)], axis=0).astype(BF16)
    da2, dwin = _proj_bwd(dproj, a2, wts["w_in"])
    tok2 = emit("w_in", {"w_in": dwin})
    tok = tok if tok2 is None else tok + tok2
    dh1, df1, g_mix_pre, g_ffn1_post = _bwd_pre_post(da2, h1, _dep(small["mix_pre_g"], tok), dh2, f1, small["ffn1_post_g"], 0.5,
                                                     "bwd_pre_post1")
    da1, dwg1, dwu1, dwd1 = _ffn_bwd(df1, a1, gate1, up1, wts["ffn1_w_gate"], wts["ffn1_w_up"], wts["ffn1_w_down"], "ffn1_bwd")
    emit("ffn1", {"ffn1_w_gate": dwg1, "ffn1_w_up": dwu1, "ffn1_w_down": dwd1})
    dh0, g_ffn1_pre = _bwd_pre_only(da1, h0, small["ffn1_pre_g"], dh1)

    vec_g = {
        "ffn1_pre_g": g_ffn1_pre, "ffn1_post_g": g_ffn1_post, "mix_pre_g": g_mix_pre, "s5_d": g_s5_d, "s5_b_glu": g_b_glu,
        "na_out_g": g_na_out, "s5_out_g": g_s5_out, "mix_post_g": g_mix_post,
        "ffn2_pre_g": g_ffn2_pre, "ffn2_post_g": g_ffn2_post, "final_g": g_final,
    }
    return loss8[0, 0], dh0[N_META:], dh0[:N_META], vec_g


WEIGHT_NAMES = ['meta_tokens', 'ffn1_pre_g', 'ffn1_post_g', 'ffn1_w_gate', 'ffn1_w_up', 'ffn1_w_down', 'mix_pre_g', 'w_in',
                'na_rpb', 's5_lam_re', 's5_lam_im', 's5_log_dt', 's5_b_re', 's5_b_im', 's5_c_re', 's5_c_im', 's5_d',
                's5_w_glu', 's5_b_glu', 'na_out_g', 's5_out_g', 'w_out', 'mix_post_g', 'ffn2_pre_g', 'ffn2_post_g',
                'ffn2_w_gate', 'ffn2_w_up', 'ffn2_w_down', 'final_g']
BIG_NAMES = ['ffn1_w_gate', 'ffn1_w_up', 'ffn1_w_down', 'w_in', 's5_w_glu', 'w_out', 'ffn2_w_gate', 'ffn2_w_up', 'ffn2_w_down']
SMALL_NAMES = [n for n in WEIGHT_NAMES if n not in BIG_NAMES and n != 'meta_tokens']
WHOLE_NAMES = ['na_rpb', 's5_lam_re', 's5_lam_im', 's5_log_dt']
LEAD_NAMES = ['s5_b_re', 's5_b_im', 's5_c_re', 's5_c_im']


def kernel(x, meta_tokens, ffn1_pre_g, ffn1_post_g, ffn1_w_gate, ffn1_w_up, ffn1_w_down, mix_pre_g, w_in, na_rpb, s5_lam_re, s5_lam_im, s5_log_dt, s5_b_re, s5_b_im, s5_c_re, s5_c_im, s5_d, s5_w_glu, s5_b_glu, na_out_g, s5_out_g, w_out, mix_post_g, ffn2_pre_g, ffn2_post_g, ffn2_w_gate, ffn2_w_up, ffn2_w_down, final_g, loss_target, m_meta_tokens, m_ffn1_pre_g, m_ffn1_post_g, m_ffn1_w_gate, m_ffn1_w_up, m_ffn1_w_down, m_mix_pre_g, m_w_in, m_na_rpb, m_s5_lam_re, m_s5_lam_im, m_s5_log_dt, m_s5_b_re, m_s5_b_im, m_s5_c_re, m_s5_c_im, m_s5_d, m_s5_w_glu, m_s5_b_glu, m_na_out_g, m_s5_out_g, m_w_out, m_mix_post_g, m_ffn2_pre_g, m_ffn2_post_g, m_ffn2_w_gate, m_ffn2_w_up, m_ffn2_w_down, m_final_g, v_meta_tokens, v_ffn1_pre_g, v_ffn1_post_g, v_ffn1_w_gate, v_ffn1_w_up, v_ffn1_w_down, v_mix_pre_g, v_w_in, v_na_rpb, v_s5_lam_re, v_s5_lam_im, v_s5_log_dt, v_s5_b_re, v_s5_b_im, v_s5_c_re, v_s5_c_im, v_s5_d, v_s5_w_glu, v_s5_b_glu, v_na_out_g, v_s5_out_g, v_w_out, v_mix_post_g, v_ffn2_pre_g, v_ffn2_post_g, v_ffn2_w_gate, v_ffn2_w_up, v_ffn2_w_down, v_final_g):
    args = dict(locals())
    w = {n: args[n] for n in WEIGHT_NAMES}
    m = {n: args["m_" + n] for n in WEIGHT_NAMES}
    v = {n: args["v_" + n] for n in WEIGHT_NAMES}

    small = {n: w[n] for n in SMALL_NAMES}

    pending = {}

    def start(group, names, arrays, gather, peers=ALL_PEERS, slot=_slot8):
        lands = _place_own(arrays, gather, "own_" + group, slot)
        send_sems, recv_sems, arrays, lands, token = _exchange_start(arrays, lands, gather, "start_" + group, peers, slot)
        pending[group] = (names, send_sems, recv_sems, arrays, lands, gather, peers, slot)
        return token

    def finish(group, after):
        names, send_sems, recv_sems, arrays, lands, gather, peers, slot = pending.pop(group)
        lands, token = _exchange_wait(send_sems, recv_sems, arrays, lands, after, gather, "wait_" + group, peers, slot)
        return dict(zip(names, lands)), token

    first = ["ffn1_w_gate", "ffn1_w_up", "ffn1_w_down"]
    def shard(n, token=None):
        s = _stored(n, w[n])[0]
        return (s if token is None else s + token[0, 0]).astype(BF16)

    ffn_names = ("ffn1_w_gate", "ffn1_w_up", "ffn1_w_down", "ffn2_w_gate", "ffn2_w_up", "ffn2_w_down")
    start("ffn1", first + ["meta_tokens"], [shard(n) for n in first] + [w["meta_tokens"]], True, (SIBLING,) + CHIP_PEERS)

    def get_w(group, after):
        got, token = finish(group, after)
        if group == "ffn1":
            got = dict(zip(got, _forward_sibling(list(got.values()), "forward_ffn1")))
            got["meta_tokens"] = got["meta_tokens"].transpose(1, 0, 2).reshape(N_META, D_MODEL)
            got["token"] = sum(start(g, names, [shard(n, token) for n in names], True)[0, 0]
                               for g, names in (("w_in", ["w_in"]), ("mix", ["s5_w_glu", "w_out"]),
                                                ("ffn2", ["ffn2_w_gate", "ffn2_w_up", "ffn2_w_down"])))
        if group == "mix":
            got = {"s5_w_glu": got["s5_w_glu"].reshape(S5_WIDTH, S5_WIDTH), "w_out": got["w_out"].reshape(D_MODEL, D_MODEL)}
        return {n: (a.reshape(D_FF, D_MODEL) if n in ffn_names else a) for n, a in got.items()}

    tokens = {}

    def emit(group, grads):
        grads = {n: (g.reshape(N_DEV, FF_SHARD, D_MODEL) if n in ffn_names else g) for n, g in grads.items()}
        if group == "ffn1":
            mine = [g.reshape((N_DEV // 2, 2) + g.shape[1:]) for g in grads.values()]
            theirs = _swap_sibling(mine, "swap_g_ffn1")
            sums = [_sum_pairs(a, b, "pair_sum_" + n) for n, a, b in zip(grads, mine, theirs)]
            tokens[group] = start("g_ffn1", list(grads), sums, False, CHIP_PEERS, _slot4)
        else:
            tokens[group] = start("g_" + group, list(grads), list(grads.values()), group == "small")
        return tokens[group][0, 0]

    loss_local, grad_x, gmeta, vec_g = _local_step(x[0], loss_target[0], get_w, small, emit)
    loss = lax.psum(loss_local, AXES)
    res = {}

    def update_shard(n, pieces):
        outs = _adamw(_stored(n, w[n]), _stored(n, m[n]), _stored(n, v[n]), pieces, "adamw_" + n)
        res[n] = [_stored(n, o) for o in outs]

    late = [grad_x, tokens["ffn1"]]
    for group in ("g_ffn2", "g_mix", "g_w_in"):
        for n, pieces in finish(group, late)[0].items():
            update_shard(n, pieces)
    g8 = finish("g_small", late)[0]
    dense = _sum8(g8["dense"], "sum_dense")
    for i, n in enumerate(LEAD_NAMES):
        g = dense[i].reshape(_stored(n, w[n]).shape)
        upd = _adamw_s5_mat(_stored(n, w[n]), _stored(n, m[n]), _stored(n, v[n]), g, "adamw_" + n)
        res[n] = [_stored(n, o) for o in [g] + list(upd)]

    done = [res[n][1] for n in ("ffn2_w_gate", "ffn2_w_up", "ffn2_w_down", "w_in", "w_out", "s5_w_glu") + tuple(LEAD_NAMES)]
    packed8, gmeta8 = _exchange([_pack_vectors(vec_g), gmeta], True, "gather_vectors", after=done)
    for n, pieces in finish("g_ffn1", packed8)[0].items():
        update_shard(n, pieces)
    _, _, _, me = _me()
    update_shard("meta_tokens", lax.dynamic_slice_in_dim(gmeta8, me * (D_MODEL // N_DEV), D_MODEL // N_DEV, axis=2))

    outs = _adamw_small(packed8, [(w[n], m[n], v[n]) for n in VEC_NAMES], [(w[n], m[n], v[n], g8[n]) for n in WHOLE_NAMES])
    for i, n in enumerate(VEC_NAMES + WHOLE_NAMES):
        res[n] = list(outs[4 * i:4 * i + 4])

    out = [loss, grad_x[None]]
    for kind in range(4):
        out += [res[n][kind] for n in WEIGHT_NAMES]
    return tuple(out)
```

```python
import functools
import math

import numpy as np
import jax
import jax.numpy as jnp
from jax import lax
from jax.experimental import pallas as pl
from jax.experimental.pallas import tpu as pltpu

F32 = jnp.float32
BF16 = jnp.bfloat16
SDS = jax.ShapeDtypeStruct

D_MODEL = 1024
N_TOK = 2048
N_META = 16
SEQ = N_TOK + N_META
ROW_TILE = 688
N_ROW_TILES = SEQ // ROW_TILE
N_DEV = 8
D_FF = 2816
FF_SHARD = D_FF // N_DEV
FF_TILE = 256
IN_SHARD = 256
NA_WIDTH = 512
S5_WIDTH = 512
HEADS = 8
HEAD_DIM = 64
GRID_W = 64
GRID_ROWS = N_TOK // GRID_W
KH = 8
KW = 16
NA_RB = 4
NA_KR = KH + NA_RB - 1
NA_BLOCKS = GRID_ROWS // NA_RB
NA_QB = NA_RB * GRID_W
NA_KB = NA_KR * GRID_W
NA_TYPES = 3
S5_GROUPS = 32
S5_GROUP = 16
S5_STATE = 64
S5_CHUNKS = 4
CH_W = S5_WIDTH // S5_CHUNKS
ST_W = S5_GROUPS * S5_STATE // S5_CHUNKS
SCAN_BLOCKS = 8
SCAN_T = SEQ // SCAN_BLOCKS
RMS_EPS = 1e-6
NEG_INF = -1e30
ATT_SCALE = HEAD_DIM ** -0.5
ADAM_LR, ADAM_B1, ADAM_B2, ADAM_EPS, ADAM_WD, ADAM_STEP = 0.001, 0.9, 0.999, 1e-08, 0.01, 10
VMEM_LIMIT = 56 * 1024 * 1024
MESH = pl.DeviceIdType.MESH
AXES = ("x", "y", "c")


def _params(sem=None):
    return pltpu.CompilerParams(dimension_semantics=sem, vmem_limit_bytes=VMEM_LIMIT)


def _dot(a, b):
    return jnp.dot(a, b, preferred_element_type=F32)


def _dot_nt(a, b):
    return lax.dot_general(a, b, (((1,), (1,)), ((), ())), preferred_element_type=F32)


def _dot_tn(a, b):
    return lax.dot_general(a, b, (((0,), (0,)), ((), ())), preferred_element_type=F32)


def _rstd(x):
    return lax.rsqrt(jnp.mean(x * x, axis=-1, keepdims=True) + RMS_EPS)


def _rms_bwd(x, r, g, dy):
    dyg = dy * g
    xr = x * r
    dx = r * (dyg - xr * jnp.mean(dyg * xr, axis=-1, keepdims=True))
    return dx, dy * xr


def _rows(i, size=ROW_TILE):
    return pl.ds(pl.multiple_of(i * size, 16), size)


def _row_spec(width):
    return pl.BlockSpec((ROW_TILE, width), lambda i: (i, 0))


def _fix_spec(shape):
    return pl.BlockSpec(shape, lambda i: (0,) * len(shape))


def _split3(x):
    hi = x.astype(BF16)
    r1 = x - hi.astype(F32)
    mid = r1.astype(BF16)
    lo = (r1 - mid.astype(F32)).astype(BF16)
    return hi, mid, lo


def _prenorm(x, g):
    def body(x_ref, g_ref, a_ref):
        xv = x_ref[...]
        a_ref[...] = (xv * _rstd(xv) * g_ref[...]).astype(BF16)

    return pl.pallas_call(
        body, grid=(N_ROW_TILES,), in_specs=[_row_spec(D_MODEL), _fix_spec((1, D_MODEL))],
        out_specs=_row_spec(D_MODEL), out_shape=SDS((SEQ, D_MODEL), BF16), name="prenorm",
        compiler_params=_params(("parallel",)))(x, g)


def _post_pre(f, hres, g_post, g_next, scale, name):
    def body(f_ref, h_ref, gp_ref, gn_ref, ho_ref, a_ref):
        fv = f_ref[...]
        h = h_ref[...] + scale * (fv * _rstd(fv) * gp_ref[...])
        ho_ref[...] = h
        a_ref[...] = (h * _rstd(h) * gn_ref[...]).astype(BF16)

    return pl.pallas_call(
        body, grid=(N_ROW_TILES,),
        in_specs=[_row_spec(D_MODEL), _row_spec(D_MODEL), _fix_spec((1, D_MODEL)), _fix_spec((1, D_MODEL))],
        out_specs=[_row_spec(D_MODEL), _row_spec(D_MODEL)],
        out_shape=[SDS((SEQ, D_MODEL), F32), SDS((SEQ, D_MODEL), BF16)], name=name,
        compiler_params=_params(("parallel",)))(f, hres, g_post, g_next)


def _final_loss(f2, h2, g_post, g_final, target):
    def body(f_ref, h_ref, gp_ref, gf_ref, t_ref, loss_ref, dh_ref, df_ref, dgf_ref, dgp_ref):
        i = pl.program_id(0)
        fv = f_ref[...]
        r1 = _rstd(fv)
        gp = gp_ref[...]
        h3 = h_ref[...] + 0.5 * (fv * r1 * gp)
        r2 = _rstd(h3)
        gf = gf_ref[...]
        y = h3 * r2 * gf
        row = lax.broadcasted_iota(jnp.int32, (ROW_TILE, 1), 0) + i * ROW_TILE
        err = jnp.where(row >= N_META, y - t_ref[...], 0.0)
        part = 0.5 * jnp.sum(jnp.mean(err * err, axis=-1, keepdims=True))
        dy = err * (1.0 / D_MODEL)
        dh3, dgf = _rms_bwd(h3, r2, gf, dy)
        dh_ref[...] = dh3
        df, dgp = _rms_bwd(fv, r1, gp, 0.5 * dh3)
        df_ref[...] = df.astype(BF16)

        @pl.when(i == 0)
        def _():
            loss_ref[...] = jnp.zeros_like(loss_ref)
            dgf_ref[...] = jnp.zeros_like(dgf_ref)
            dgp_ref[...] = jnp.zeros_like(dgp_ref)

        loss_ref[...] += part
        dgf_ref[...] += jnp.sum(dgf, axis=0, keepdims=True)
        dgp_ref[...] += jnp.sum(dgp, axis=0, keepdims=True)

    gain = _fix_spec((1, D_MODEL))
    return pl.pallas_call(
        body, grid=(N_ROW_TILES,),
        in_specs=[_row_spec(D_MODEL), _row_spec(D_MODEL), gain, gain, _row_spec(D_MODEL)],
        out_specs=[_fix_spec((8, 128)), _row_spec(D_MODEL), _row_spec(D_MODEL), gain, gain],
        out_shape=[SDS((8, 128), F32), SDS((SEQ, D_MODEL), F32), SDS((SEQ, D_MODEL), BF16),
                   SDS((1, D_MODEL), F32), SDS((1, D_MODEL), F32)],
        name="final_loss", compiler_params=_params(("arbitrary",)))(f2, h2, g_post, g_final, target)


def _bwd_pre_post(da, h, g_pre, dh_res, fprev, g_post, scale, name):
    def body(da_ref, h_ref, gpre_ref, dhr_ref, f_ref, gpost_ref, dh_ref, df_ref, dgpre_ref, dgpost_ref):
        i = pl.program_id(0)
        hv = h_ref[...]
        dxa, dgpre = _rms_bwd(hv, _rstd(hv), gpre_ref[...], da_ref[...])
        dh = dhr_ref[...] + dxa
        dh_ref[...] = dh
        fv = f_ref[...]
        df, dgpost = _rms_bwd(fv, _rstd(fv), gpost_ref[...], scale * dh)
        df_ref[...] = df.astype(BF16)

        @pl.when(i == 0)
        def _():
            dgpre_ref[...] = jnp.zeros_like(dgpre_ref)
            dgpost_ref[...] = jnp.zeros_like(dgpost_ref)

        dgpre_ref[...] += jnp.sum(dgpre, axis=0, keepdims=True)
        dgpost_ref[...] += jnp.sum(dgpost, axis=0, keepdims=True)

    gain = _fix_spec((1, D_MODEL))
    row = _row_spec(D_MODEL)
    return pl.pallas_call(
        body, grid=(N_ROW_TILES,), in_specs=[row, row, gain, row, row, gain],
        out_specs=[row, row, gain, gain],
        out_shape=[SDS((SEQ, D_MODEL), F32), SDS((SEQ, D_MODEL), BF16), SDS((1, D_MODEL), F32), SDS((1, D_MODEL), F32)],
        name=name, compiler_params=_params(("arbitrary",)))(da, h, g_pre, dh_res, fprev, g_post)


def _bwd_pre_only(da, h, g_pre, dh_res):
    def body(da_ref, h_ref, gpre_ref, dhr_ref, dh_ref, dgpre_ref):
        i = pl.program_id(0)
        hv = h_ref[...]
        dxa, dgpre = _rms_bwd(hv, _rstd(hv), gpre_ref[...], da_ref[...])
        dh_ref[...] = dhr_ref[...] + dxa

        @pl.when(i == 0)
        def _():
            dgpre_ref[...] = jnp.zeros_like(dgpre_ref)

        dgpre_ref[...] += jnp.sum(dgpre, axis=0, keepdims=True)

    gain = _fix_spec((1, D_MODEL))
    row = _row_spec(D_MODEL)
    return pl.pallas_call(
        body, grid=(N_ROW_TILES,), in_specs=[row, row, gain, row], out_specs=[row, gain],
        out_shape=[SDS((SEQ, D_MODEL), F32), SDS((1, D_MODEL), F32)],
        name="bwd_pre_only", compiler_params=_params(("arbitrary",)))(da, h, g_pre, dh_res)


def _ffn_fwd(a, wg, wu, wd, name, after=()):
    def body(a_ref, wg_ref, wu_ref, wd_ref, *rest):
        gate_ref, up_ref, f_ref = rest[len(after):]
        j = pl.program_id(0)

        def tile(i, carry):
            rows = _rows(i)
            at = a_ref[rows, :]
            gate = _dot_nt(at, wg_ref[...])
            up = _dot_nt(at, wu_ref[...])
            gate_ref[rows, :] = gate
            up_ref[rows, :] = up
            act = (gate * jax.nn.sigmoid(gate) * up).astype(BF16)
            contrib = _dot(act, wd_ref[...])

            @pl.when(j == 0)
            def _():
                f_ref[rows, :] = contrib

            @pl.when(j != 0)
            def _():
                f_ref[rows, :] += contrib

            return carry

        lax.fori_loop(0, N_ROW_TILES, tile, 0)

    wtile = pl.BlockSpec((FF_TILE, D_MODEL), lambda j: (j, 0))
    hid = pl.BlockSpec((SEQ, FF_TILE), lambda j: (0, j))
    full = pl.BlockSpec((SEQ, D_MODEL), lambda j: (0, 0))
    return pl.pallas_call(
        body, grid=(D_FF // FF_TILE,), in_specs=[full, wtile, wtile, wtile] + [pl.BlockSpec(memory_space=pl.ANY)] * len(after),
        out_specs=[hid, hid, full],
        out_shape=[SDS((SEQ, D_FF), F32), SDS((SEQ, D_FF), F32), SDS((SEQ, D_MODEL), F32)],
        name=name, compiler_params=_params(("arbitrary",)))(a, wg, wu, wd, *after)


def _ffn_bwd(df, a, gate, up, wg, wu, wd, name):
    def body(df_ref, a_ref, gate_ref, up_ref, wg_ref, wu_ref, wd_ref, da_ref, dwg_ref, dwu_ref, dwd_ref,
             acc_g, acc_u, acc_d):
        j = pl.program_id(0)

        def tile(i, carry):
            rows = _rows(i)
            dft = df_ref[rows, :]
            at = a_ref[rows, :]
            gate = gate_ref[rows, :]
            up = up_ref[rows, :]
            dact = _dot_nt(dft, wd_ref[...])
            sig = jax.nn.sigmoid(gate)
            silu = gate * sig
            dgate = (dact * up * (sig * (1.0 + gate * (1.0 - sig)))).astype(BF16)
            dup = (dact * silu).astype(BF16)
            act = (silu * up).astype(BF16)
            dwd = _dot_tn(act, dft)
            dwg = _dot_tn(dgate, at)
            dwu = _dot_tn(dup, at)
            dat = _dot(dgate, wg_ref[...]) + _dot(dup, wu_ref[...])

            @pl.when(i == 0)
            def _():
                acc_d[...] = dwd
                acc_g[...] = dwg
                acc_u[...] = dwu

            @pl.when(i != 0)
            def _():
                acc_d[...] += dwd
                acc_g[...] += dwg
                acc_u[...] += dwu

            @pl.when(j == 0)
            def _():
                da_ref[rows, :] = dat

            @pl.when(j != 0)
            def _():
                da_ref[rows, :] += dat

            return carry

        lax.fori_loop(0, N_ROW_TILES, tile, 0)
        dwg_ref[...] = acc_g[...].astype(BF16)
        dwu_ref[...] = acc_u[...].astype(BF16)
        dwd_ref[...] = acc_d[...].astype(BF16)

    wtile = pl.BlockSpec((FF_TILE, D_MODEL), lambda j: (j, 0))
    hid = pl.BlockSpec((SEQ, FF_TILE), lambda j: (0, j))
    full = pl.BlockSpec((SEQ, D_MODEL), lambda j: (0, 0))
    return pl.pallas_call(
        body, grid=(D_FF // FF_TILE,), in_specs=[full, full, hid, hid, wtile, wtile, wtile],
        out_specs=[full, wtile, wtile, wtile],
        out_shape=[SDS((SEQ, D_MODEL), F32)] + [SDS((D_FF, D_MODEL), BF16)] * 3,
        scratch_shapes=[pltpu.VMEM((FF_TILE, D_MODEL), F32)] * 3,
        name=name, compiler_params=_params(("arbitrary",)))(df, a, gate, up, wg, wu, wd)


def _proj_fwd(a, w):
    def body(a_ref, w_ref, o_ref):
        def tile(i, carry):
            rows = _rows(i)
            o_ref[rows, :] = _dot(a_ref[rows, :], w_ref[...])
            return carry

        lax.fori_loop(0, N_ROW_TILES, tile, 0)

    return pl.pallas_call(
        body, grid=(N_DEV,),
        in_specs=[pl.BlockSpec((SEQ, D_MODEL), lambda j: (0, 0)), pl.BlockSpec((None, D_MODEL, IN_SHARD), lambda j: (j, 0, 0))],
        out_specs=pl.BlockSpec((None, SEQ, IN_SHARD), lambda j: (j, 0, 0)),
        out_shape=SDS((N_DEV, SEQ, IN_SHARD), F32), name="proj_fwd",
        compiler_params=_params(("parallel",)))(a, w)


def _proj_bwd(dproj, a, w):
    def body(dp_ref, a_ref, w_ref, da_ref, dw_ref, acc):
        j = pl.program_id(0)

        def tile(i, carry):
            rows = _rows(i)
            dpt = dp_ref[rows, :]
            dw = _dot_tn(a_ref[rows, :], dpt)
            dat = _dot_nt(dpt, w_ref[...])

            @pl.when(i == 0)
            def _():
                acc[...] = dw

            @pl.when(i != 0)
            def _():
                acc[...] += dw

            @pl.when(j == 0)
            def _():
                da_ref[rows, :] = dat

            @pl.when(j != 0)
            def _():
                da_ref[rows, :] += dat

            return carry

        lax.fori_loop(0, N_ROW_TILES, tile, 0)
        dw_ref[...] = acc[...].astype(BF16)

    full = pl.BlockSpec((SEQ, D_MODEL), lambda j: (0, 0))
    wspec = pl.BlockSpec((None, D_MODEL, IN_SHARD), lambda j: (j, 0, 0))
    return pl.pallas_call(
        body, grid=(N_DEV,),
        in_specs=[pl.BlockSpec((None, SEQ, IN_SHARD), lambda j: (j, 0, 0)), full, wspec],
        out_specs=[full, wspec],
        out_shape=[SDS((SEQ, D_MODEL), F32), SDS((N_DEV, D_MODEL, IN_SHARD), BF16)],
        scratch_shapes=[pltpu.VMEM((D_MODEL, IN_SHARD), F32)],
        name="proj_bwd", compiler_params=_params(("arbitrary",)))(dproj, a, w)


def _na_consts():
    c = np.arange(GRID_W)
    col_start = np.clip(c - KW // 2, 0, GRID_W - KW)
    col_in = (c[None, :] >= col_start[:, None]) & (c[None, :] < col_start[:, None] + KW)
    dc = np.clip(c[None, :] - c[:, None] + KW - 1, 0, 2 * KW - 2)
    onehot = np.zeros((128, GRID_W * GRID_W), np.float32)
    qq, kk = np.meshgrid(c, c, indexing="ij")
    onehot[dc[col_in], (qq * GRID_W + kk)[col_in]] = 1.0
    negmask = np.where(col_in, 0.0, NEG_INF).astype(np.float32).reshape(1, -1)
    return onehot, negmask


def _na_pair(block_type, a, b):
    if block_type == 0:
        return b - a + KH - 1 if b < KH else None
    if block_type == 1:
        return b - a + KH // 2 - 1 if a <= b < a + KH else None
    return b - a if b >= NA_KR - KH else None


def _rpb_expand(rpb):
    onehot, negmask = _na_consts()
    rows = HEADS * (2 * KH - 1)
    rpb_pad = jnp.pad(rpb.reshape(rows, 2 * KW - 1), ((0, 128 - rows), (0, 128 - (2 * KW - 1))))

    def body(r_ref, oh_ref, m_ref, t_ref):
        hi, mid, lo = _split3(r_ref[...])
        oh = oh_ref[...]
        t_ref[...] = _dot(hi, oh) + _dot(mid, oh) + _dot(lo, oh) + m_ref[...]

    table = pl.pallas_call(body, out_shape=SDS((128, GRID_W * GRID_W), F32), name="rpb_expand",
                           compiler_params=_params())(rpb_pad, jnp.asarray(onehot, BF16), jnp.asarray(negmask))
    return table[:rows].reshape(HEADS, 2 * KH - 1, GRID_W, GRID_W)


def _rpb_reduce(dslabs):
    onehot, _ = _na_consts()
    rows = HEADS * (2 * KH - 1)

    def body(x_ref, oht_ref, o_ref):
        hi, mid, lo = _split3(x_ref[...])
        oht = oht_ref[...]
        o_ref[...] = _dot(hi, oht) + _dot(mid, oht) + _dot(lo, oht)

    out = pl.pallas_call(body, out_shape=SDS((rows, 128), F32), name="rpb_reduce", compiler_params=_params())(
        dslabs.reshape(rows, GRID_W * GRID_W), jnp.asarray(onehot.T, BF16))
    return out.reshape(HEADS, 2 * KH - 1, 128)


def _bias_tiles(slab_ref, tile_ref):
    tile_ref[...] = jnp.full(tile_ref.shape, NEG_INF, F32)
    for t in range(NA_TYPES):
        for a in range(NA_RB):
            for b in range(NA_KR):
                dr = _na_pair(t, a, b)
                if dr is not None:
                    tile_ref[t, a * GRID_W:(a + 1) * GRID_W, b * GRID_W:(b + 1) * GRID_W] = slab_ref[dr]


def _bias_tiles_bwd(dtile_ref, dslab_ref):
    acc = {}
    for t in range(NA_TYPES):
        for a in range(NA_RB):
            for b in range(NA_KR):
                dr = _na_pair(t, a, b)
                if dr is not None:
                    part = dtile_ref[t, a * GRID_W:(a + 1) * GRID_W, b * GRID_W:(b + 1) * GRID_W]
                    acc[dr] = part if dr not in acc else acc[dr] + part
    for dr in range(2 * KH - 1):
        dslab_ref[dr] = acc[dr]


def _block_geometry(g):
    start = jnp.clip(g * NA_RB - KH // 2, 0, GRID_ROWS - NA_KR)
    block_type = jnp.where(g == 0, 0, jnp.where(g == NA_BLOCKS - 1, 2, 1))
    q0 = pl.multiple_of(N_META + g * NA_QB, 16)
    k0 = pl.multiple_of(N_META + start * GRID_W, 16)
    return block_type, q0, k0


def _na_probs(q, kk, km, bias):
    s = _dot_nt(q, kk) * ATT_SCALE + bias
    sm = _dot_nt(q, km) * ATT_SCALE
    m = jnp.maximum(jnp.max(s, axis=-1, keepdims=True), jnp.max(sm, axis=-1, keepdims=True))
    p = jnp.exp(s - m)
    pm = jnp.exp(sm - m)
    inv = 1.0 / (jnp.sum(p, axis=-1, keepdims=True) + jnp.sum(pm, axis=-1, keepdims=True))
    return p * inv, pm * inv


def _meta_probs(qm, km):
    s = _dot_nt(qm, km) * ATT_SCALE
    p = jnp.exp(s - jnp.max(s, axis=-1, keepdims=True))
    return p / jnp.sum(p, axis=-1, keepdims=True)


def _na_fwd(q, k, v, bias):
    def body(q_ref, k_ref, v_ref, slab_ref, o_ref, b_ref):
        _bias_tiles(slab_ref, b_ref)
        km = k_ref[0:N_META, :].astype(BF16)
        vm = v_ref[0:N_META, :].astype(BF16)
        pmm = _meta_probs(q_ref[0:N_META, :].astype(BF16), km)
        o_ref[0:N_META, :] = _dot(pmm.astype(BF16), vm)

        def block(g, carry):
            block_type, q0, k0 = _block_geometry(g)
            qb = q_ref[pl.ds(q0, NA_QB), :].astype(BF16)
            kk = k_ref[pl.ds(k0, NA_KB), :].astype(BF16)
            vv = v_ref[pl.ds(k0, NA_KB), :].astype(BF16)
            p, pm = _na_probs(qb, kk, km, b_ref[block_type])
            o_ref[pl.ds(q0, NA_QB), :] = _dot(p.astype(BF16), vv) + _dot(pm.astype(BF16), vm)
            return carry

        lax.fori_loop(0, NA_BLOCKS, block, 0)

    head = pl.BlockSpec((None, SEQ, HEAD_DIM), lambda h: (h, 0, 0))
    return pl.pallas_call(
        body, grid=(HEADS,), in_specs=[head, head, head, pl.BlockSpec((None, 2 * KH - 1, GRID_W, GRID_W), lambda h: (h, 0, 0, 0))],
        out_specs=head, out_shape=SDS((HEADS, SEQ, HEAD_DIM), F32), name="na_fwd",
        scratch_shapes=[pltpu.VMEM((NA_TYPES, NA_QB, NA_KB), F32)],
        compiler_params=_params(("parallel",)))(q, k, v, bias)


def _na_bwd(q, k, v, bias, do):
    def body(q_ref, k_ref, v_ref, slab_ref, do_ref, dq_ref, dk_ref, dv_ref, dslab_ref, b_ref, db_ref):
        _bias_tiles(slab_ref, b_ref)
        km = k_ref[0:N_META, :].astype(BF16)
        vm = v_ref[0:N_META, :].astype(BF16)
        dk_ref[...] = jnp.zeros_like(dk_ref)
        dv_ref[...] = jnp.zeros_like(dv_ref)
        db_ref[...] = jnp.zeros_like(db_ref)

        qm = q_ref[0:N_META, :].astype(BF16)
        dom = do_ref[0:N_META, :].astype(BF16)
        pmm = _meta_probs(qm, km)
        dpm = _dot_nt(dom, vm)
        dsm = (pmm * (dpm - jnp.sum(pmm * dpm, axis=-1, keepdims=True)) * ATT_SCALE).astype(BF16)
        dq_ref[0:N_META, :] = _dot(dsm, km)
        dkm0 = _dot_tn(dsm, qm)
        dvm0 = _dot_tn(pmm.astype(BF16), dom)

        def block(g, carry):
            dkm, dvm = carry
            block_type, q0, k0 = _block_geometry(g)
            qb = q_ref[pl.ds(q0, NA_QB), :].astype(BF16)
            kk = k_ref[pl.ds(k0, NA_KB), :].astype(BF16)
            vv = v_ref[pl.ds(k0, NA_KB), :].astype(BF16)
            dob = do_ref[pl.ds(q0, NA_QB), :].astype(BF16)
            p, pm = _na_probs(qb, kk, km, b_ref[block_type])
            dp = _dot_nt(dob, vv)
            dpm_ = _dot_nt(dob, vm)
            delta = jnp.sum(p * dp, axis=-1, keepdims=True) + jnp.sum(pm * dpm_, axis=-1, keepdims=True)
            ds = p * (dp - delta)
            dsm_ = pm * (dpm_ - delta)
            db_ref[block_type] += ds
            dsb = (ds * ATT_SCALE).astype(BF16)
            dsmb = (dsm_ * ATT_SCALE).astype(BF16)
            dq_ref[pl.ds(q0, NA_QB), :] = _dot(dsb, kk) + _dot(dsmb, km)
            dk_ref[pl.ds(k0, NA_KB), :] += _dot_tn(dsb, qb)
            dv_ref[pl.ds(k0, NA_KB), :] += _dot_tn(p.astype(BF16), dob)
            return dkm + _dot_tn(dsmb, qb), dvm + _dot_tn(pm.astype(BF16), dob)

        dkm, dvm = lax.fori_loop(0, NA_BLOCKS, block, (dkm0, dvm0))
        dk_ref[0:N_META, :] = dkm
        dv_ref[0:N_META, :] = dvm
        _bias_tiles_bwd(db_ref, dslab_ref)

    head = pl.BlockSpec((None, SEQ, HEAD_DIM), lambda h: (h, 0, 0))
    bspec = pl.BlockSpec((None, 2 * KH - 1, GRID_W, GRID_W), lambda h: (h, 0, 0, 0))
    return pl.pallas_call(
        body, grid=(HEADS,), in_specs=[head, head, head, bspec, head], out_specs=[head, head, head, bspec],
        out_shape=[SDS((HEADS, SEQ, HEAD_DIM), F32)] * 3 + [SDS((HEADS, 2 * KH - 1, GRID_W, GRID_W), F32)],
        scratch_shapes=[pltpu.VMEM((NA_TYPES, NA_QB, NA_KB), F32), pltpu.VMEM((NA_TYPES, NA_QB, NA_KB), F32)],
        name="na_bwd", compiler_params=_params(("parallel",)))(q, k, v, bias, do)


def _cmul(ar, ai, br, bi):
    return ar * br - ai * bi, ar * bi + ai * br


def _cpow(ar, ai, n):
    rr, ri = None, None
    br, bi = ar, ai
    while n:
        if n & 1:
            rr, ri = (br, bi) if rr is None else _cmul(rr, ri, br, bi)
        n >>= 1
        if n:
            br, bi = _cmul(br, bi, br, bi)
    return rr, ri


def _s5_prep(lr, li, logdt, bre, bim):
    def body(lr_ref, li_ref, dt_ref, br_ref, bi_ref, lbr_ref, lbi_ref, bbr_ref, bbi_ref):
        lr_, li_ = lr_ref[...], li_ref[...]
        dt = jnp.exp(dt_ref[...])
        mag = jnp.exp(lr_ * dt)
        lbr = mag * jnp.cos(li_ * dt)
        lbi = mag * jnp.sin(li_ * dt)
        lbr_ref[...] = lbr
        lbi_ref[...] = lbi
        den = lr_ * lr_ + li_ * li_
        xr = lbr - 1.0
        cr = (xr * lr_ + lbi * li_) / den
        ci = (lbi * lr_ - xr * li_) / den
        br, bi = br_ref[...], bi_ref[...]
        bbr_ref[...] = cr[:, None, :] * br - ci[:, None, :] * bi
        bbi_ref[...] = cr[:, None, :] * bi + ci[:, None, :] * br

    n = 2 * S5_GROUPS
    return pl.pallas_call(
        body, out_shape=[SDS((n, S5_STATE), F32)] * 2 + [SDS((n, S5_GROUP, S5_STATE), F32)] * 2,
        name="s5_prep", compiler_params=_params())(lr, li, logdt, bre, bim)


def _s5_prep_bwd(lr, li, logdt, bre, bim, dar, dai, dbbr, dbbi):
    def body(lr_ref, li_ref, dt_ref, br_ref, bi_ref, dar_ref, dai_ref, dbr_ref, dbi_ref,
             glr_ref, gli_ref, gdt_ref, gbr_ref, gbi_ref):
        lr_, li_ = lr_ref[...], li_ref[...]
        dt = jnp.exp(dt_ref[...])
        mag = jnp.exp(lr_ * dt)
        lbr = mag * jnp.cos(li_ * dt)
        lbi = mag * jnp.sin(li_ * dt)
        den = lr_ * lr_ + li_ * li_
        xr = lbr - 1.0
        cr = (xr * lr_ + lbi * li_) / den
        ci = (lbi * lr_ - xr * li_) / den
        br, bi = br_ref[...], bi_ref[...]
        dbr, dbi = dbr_ref[...], dbi_ref[...]
        gbr_ref[...] = cr[:, None, :] * dbr + ci[:, None, :] * dbi
        gbi_ref[...] = cr[:, None, :] * dbi - ci[:, None, :] * dbr
        gcr = jnp.sum(dbr * br + dbi * bi, axis=1)
        gci = jnp.sum(dbi * br - dbr * bi, axis=1)
        ilr, ili = lr_ / den, li_ / den
        tr, ti = _cmul(gcr, gci, ilr, ili)
        glbr = dar_ref[...] + tr
        glbi = dai_ref[...] + ti
        dr_, di_ = _cmul(tr, ti, cr, -ci)
        gwr, gwi = _cmul(glbr, glbi, lbr, -lbi)
        glr_ref[...] = gwr * dt - dr_
        gli_ref[...] = gwi * dt - di_
        gdt_ref[...] = jnp.sum(gwr * lr_ + gwi * li_, axis=-1, keepdims=True) * dt

    n = 2 * S5_GROUPS
    return pl.pallas_call(
        body, out_shape=[SDS((n, S5_STATE), F32)] * 2 + [SDS((n, 1), F32)] + [SDS((n, S5_GROUP, S5_STATE), F32)] * 2,
        name="s5_prep_bwd", compiler_params=_params())(lr, li, logdt, bre, bim, dar, dai, dbbr, dbbi)


def _scan_local(xr_ref, xi_ref, ar8, ai8, reverse):
    def step(i, carry):
        sr, si = carry
        idx = (SCAN_T - 1 - i) if reverse else i
        rows = pl.ds(pl.multiple_of(idx * SCAN_BLOCKS, SCAN_BLOCKS), SCAN_BLOCKS)
        nr = ar8 * sr - ai8 * si + xr_ref[rows, :]
        ni = ar8 * si + ai8 * sr + xi_ref[rows, :]
        xr_ref[rows, :] = nr
        xi_ref[rows, :] = ni
        return nr, ni

    z = jnp.zeros(ar8.shape, F32)
    return lax.fori_loop(0, SCAN_T, step, (z, z))


def _scan_carries(er, ei, atr, ati, reverse):
    row = lax.broadcasted_iota(jnp.int32, er.shape, 0)
    cr = jnp.zeros((1, er.shape[1]), F32)
    ci = cr
    outr = jnp.zeros(er.shape, F32)
    outi = outr
    order = range(SCAN_BLOCKS - 1, -1, -1) if reverse else range(SCAN_BLOCKS)
    for b in order:
        outr = jnp.where(row == b, cr, outr)
        outi = jnp.where(row == b, ci, outi)
        nr, ni = _cmul(atr, ati, cr, ci)
        cr, ci = nr + er[b:b + 1, :], ni + ei[b:b + 1, :]
    return outr, outi


def _scan_fixup(xr_ref, xi_ref, cr8, ci8, ar8, ai8, reverse):
    def step(i, carry):
        pr, pi = carry
        idx = (SCAN_T - 1 - i) if reverse else i
        rows = pl.ds(pl.multiple_of(idx * SCAN_BLOCKS, SCAN_BLOCKS), SCAN_BLOCKS)
        fr, fi = _cmul(pr, pi, cr8, ci8)
        xr_ref[rows, :] += fr
        xi_ref[rows, :] += fi
        return _cmul(pr, pi, ar8, ai8)

    lax.fori_loop(0, SCAN_T, step, (ar8, ai8))


def _scan(xr_ref, xi_ref, ar, ai, reverse):
    n = ar.shape[1]
    ar8 = jnp.broadcast_to(ar, (SCAN_BLOCKS, n))
    ai8 = jnp.broadcast_to(ai, (SCAN_BLOCKS, n))
    er, ei = _scan_local(xr_ref, xi_ref, ar8, ai8, reverse)
    atr, ati = _cpow(ar, ai, SCAN_T)
    cr8, ci8 = _scan_carries(er, ei, atr, ati, reverse)
    _scan_fixup(xr_ref, xi_ref, cr8, ci8, ar8, ai8, reverse)


def _s5_specs():
    chan = pl.BlockSpec((SEQ, CH_W), lambda c, d: (0, c))
    chan2 = pl.BlockSpec((None, SEQ, CH_W), lambda c, d: (d, 0, c))
    state = pl.BlockSpec((None, SEQ, ST_W), lambda c, d: (d, 0, c))
    bmat = pl.BlockSpec((None, None, CH_W, ST_W), lambda c, d: (d, c, 0, 0))
    cmat = pl.BlockSpec((None, None, ST_W, CH_W), lambda c, d: (d, c, 0, 0))
    avec = pl.BlockSpec((None, None, 1, ST_W), lambda c, d: (d, c, 0, 0))
    return chan, chan2, state, bmat, cmat, avec


def _scan_by_direction(xr_ref, xi_ref, ar, ai, d, adjoint):
    @pl.when(d == 0)
    def _():
        _scan(xr_ref, xi_ref, ar, ai, reverse=adjoint)

    @pl.when(d == 1)
    def _():
        _scan(xr_ref, xi_ref, ar, ai, reverse=not adjoint)


def _s5_scan_fwd(u, bre, bim, are, aim, cre, cim):
    def body(u_ref, bre_ref, bim_ref, are_ref, aim_ref, cre_ref, cim_ref, sr_ref, si_ref, y_ref):
        ub = u_ref[...].astype(BF16)
        sr_ref[...] = _dot(ub, bre_ref[...])
        si_ref[...] = _dot(ub, bim_ref[...])
        _scan_by_direction(sr_ref, si_ref, are_ref[...], aim_ref[...], pl.program_id(1), adjoint=False)
        y_ref[...] = _dot(sr_ref[...].astype(BF16), cre_ref[...]) - _dot(si_ref[...].astype(BF16), cim_ref[...])

    chan, chan2, state, bmat, cmat, avec = _s5_specs()
    return pl.pallas_call(
        body, grid=(S5_CHUNKS, 2), in_specs=[chan, bmat, bmat, avec, avec, cmat, cmat], out_specs=[state, state, chan2],
        out_shape=[SDS((2, SEQ, S5_GROUPS * S5_STATE), F32)] * 2 + [SDS((2, SEQ, S5_WIDTH), F32)],
        name="s5_scan_fwd", compiler_params=_params(("parallel", "parallel")))(u, bre, bim, are, aim, cre, cim)


def _dlam(gr_ref, gi_ref, sr_ref, si_ref, reverse):
    tile = lambda i: pl.ds(pl.multiple_of(i * SCAN_BLOCKS, SCAN_BLOCKS), SCAN_BLOCKS)
    row = lax.broadcasted_iota(jnp.int32, (SCAN_BLOCKS, ST_W), 0)
    if reverse:
        edge, src, shift, empty, lo, hi, dprev = SCAN_T - 1, 0, SCAN_BLOCKS - 1, SCAN_BLOCKS - 1, 0, SCAN_T - 1, 1
    else:
        edge, src, shift, empty, lo, hi, dprev = 0, SCAN_T - 1, 1, 0, 1, SCAN_T, -1
    spr = jnp.where(row == empty, 0.0, pltpu.roll(sr_ref[tile(src), :], shift, 0))
    spi = jnp.where(row == empty, 0.0, pltpu.roll(si_ref[tile(src), :], shift, 0))
    acc0 = _cmul(gr_ref[tile(edge), :], gi_ref[tile(edge), :], spr, -spi)

    def step(i, carry):
        accr, acci = carry
        pr, pi = _cmul(gr_ref[tile(i), :], gi_ref[tile(i), :], sr_ref[tile(i + dprev), :], -si_ref[tile(i + dprev), :])
        return accr + pr, acci + pi

    accr, acci = lax.fori_loop(lo, hi, step, acc0)
    return jnp.sum(accr, axis=0, keepdims=True), jnp.sum(acci, axis=0, keepdims=True)


def _s5_scan_bwd(dy, du_skip, u, sr, si, bre, bim, are, aim, cre, cim):
    def body(dy_ref, dus_ref, u_ref, sr_ref, si_ref, bre_ref, bim_ref, are_ref, aim_ref, cre_ref, cim_ref,
             du_ref, dbr_ref, dbi_ref, dcr_ref, dci_ref, dar_ref, dai_ref, gr_ref, gi_ref):
        d = pl.program_id(1)
        dyb = dy_ref[...].astype(BF16)
        gr_ref[...] = _dot_nt(dyb, cre_ref[...])
        gi_ref[...] = -_dot_nt(dyb, cim_ref[...])
        dcr_ref[...] = _dot_tn(sr_ref[...].astype(BF16), dyb)
        dci_ref[...] = -_dot_tn(si_ref[...].astype(BF16), dyb)
        _scan_by_direction(gr_ref, gi_ref, are_ref[...], -aim_ref[...], d, adjoint=True)

        @pl.when(d == 0)
        def _():
            dar_ref[...], dai_ref[...] = _dlam(gr_ref, gi_ref, sr_ref, si_ref, reverse=False)
            du_ref[...] = dus_ref[...]

        @pl.when(d == 1)
        def _():
            dar_ref[...], dai_ref[...] = _dlam(gr_ref, gi_ref, sr_ref, si_ref, reverse=True)

        grb = gr_ref[...].astype(BF16)
        gib = gi_ref[...].astype(BF16)
        du_ref[...] += _dot_nt(grb, bre_ref[...]) + _dot_nt(gib, bim_ref[...])
        ub = u_ref[...].astype(BF16)
        dbr_ref[...] = _dot_tn(ub, grb)
        dbi_ref[...] = _dot_tn(ub, gib)

    chan, _, state, bmat, cmat, avec = _s5_specs()
    return pl.pallas_call(
        body, grid=(S5_CHUNKS, 2), in_specs=[chan, chan, chan, state, state, bmat, bmat, avec, avec, cmat, cmat],
        out_specs=[chan, bmat, bmat, cmat, cmat, avec, avec],
        out_shape=[SDS((SEQ, S5_WIDTH), F32)] + [SDS((2, S5_CHUNKS, CH_W, ST_W), F32)] * 2
                  + [SDS((2, S5_CHUNKS, ST_W, CH_W), F32)] * 2 + [SDS((2, S5_CHUNKS, 1, ST_W), F32)] * 2,
        scratch_shapes=[pltpu.VMEM((SEQ, ST_W), F32), pltpu.VMEM((SEQ, ST_W), F32)],
        name="s5_scan_bwd", compiler_params=_params(("parallel", "arbitrary")))(dy, du_skip, u, sr, si, bre, bim, are, aim, cre, cim)


_GELU_K = math.sqrt(2.0 / math.pi)
_GELU_C = 0.044715


def _gelu(x):
    t = jnp.tanh(_GELU_K * (x + _GELU_C * x * x * x))
    return 0.5 * x * (1.0 + t), t


def _s5_glu_fwd(u, y2, dskip, wglu, bglu):
    def body(u_ref, y0_ref, y1_ref, d_ref, w_ref, b_ref, o_ref, yp_ref):
        ypre = u_ref[...] * d_ref[...] + y0_ref[...] + y1_ref[...]
        yp_ref[...] = ypre
        y, _ = _gelu(ypre)
        z = _dot(y.astype(BF16), w_ref[...]) + b_ref[...]
        o_ref[...] = y * jax.nn.sigmoid(z)

    row = _row_spec(S5_WIDTH)
    vec = _fix_spec((1, S5_WIDTH))
    dir0 = pl.BlockSpec((None, ROW_TILE, S5_WIDTH), lambda i: (0, i, 0))
    dir1 = pl.BlockSpec((None, ROW_TILE, S5_WIDTH), lambda i: (1, i, 0))
    return pl.pallas_call(
        body, grid=(N_ROW_TILES,), in_specs=[row, dir0, dir1, vec, _fix_spec((S5_WIDTH, S5_WIDTH)), vec],
        out_specs=[row, row], out_shape=[SDS((SEQ, S5_WIDTH), F32)] * 2, name="s5_glu_fwd",
        compiler_params=_params(("parallel",)))(u, y2, y2, dskip, wglu, bglu)


def _s5_glu_bwd(do, ypre, u, dskip, wglu, bglu):
    def body(do_ref, yp_ref, u_ref, d_ref, w_ref, b_ref, dyp_ref, du_ref, dw_ref, db_ref, dd_ref):
        i = pl.program_id(0)
        ypre = yp_ref[...]
        y, t = _gelu(ypre)
        yb = y.astype(BF16)
        sg = jax.nn.sigmoid(_dot(yb, w_ref[...]) + b_ref[...])
        dov = do_ref[...]
        dz = dov * y * sg * (1.0 - sg)
        dzb = dz.astype(BF16)
        dy = dov * sg + _dot_nt(dzb, w_ref[...])
        dgelu = 0.5 * (1.0 + t) + 0.5 * ypre * (1.0 - t * t) * _GELU_K * (1.0 + 3.0 * _GELU_C * ypre * ypre)
        dyp = dy * dgelu
        dyp_ref[...] = dyp
        uv = u_ref[...]
        du_ref[...] = dyp * d_ref[...]

        @pl.when(i == 0)
        def _():
            dw_ref[...] = jnp.zeros_like(dw_ref)
            db_ref[...] = jnp.zeros_like(db_ref)
            dd_ref[...] = jnp.zeros_like(dd_ref)

        dw_ref[...] += _dot_tn(yb, dzb)
        db_ref[...] += jnp.sum(dz, axis=0, keepdims=True)
        dd_ref[...] += jnp.sum(dyp * uv, axis=0, keepdims=True)

    row = _row_spec(S5_WIDTH)
    vec = _fix_spec((1, S5_WIDTH))
    mat = _fix_spec((S5_WIDTH, S5_WIDTH))
    return pl.pallas_call(
        body, grid=(N_ROW_TILES,), in_specs=[row, row, row, vec, mat, vec], out_specs=[row, row, mat, vec, vec],
        out_shape=[SDS((SEQ, S5_WIDTH), F32)] * 2 + [SDS((S5_WIDTH, S5_WIDTH), F32), SDS((1, S5_WIDTH), F32), SDS((1, S5_WIDTH), F32)],
        name="s5_glu_bwd", compiler_params=_params(("arbitrary",)))(do, ypre, u, dskip, wglu, bglu)


def _mix_out_fwd(ona, os5, g_na, g_s5, wout):
    def body(a_ref, s_ref, ga_ref, gs_ref, w_ref, o_ref):
        av, sv = a_ref[...], s_ref[...]
        ca = (av * _rstd(av) * ga_ref[...]).astype(BF16)
        cs = (sv * _rstd(sv) * gs_ref[...]).astype(BF16)
        o_ref[...] = _dot(ca, w_ref[0:NA_WIDTH, :]) + _dot(cs, w_ref[NA_WIDTH:, :])

    row = _row_spec(NA_WIDTH)
    vec = _fix_spec((1, NA_WIDTH))
    return pl.pallas_call(
        body, grid=(N_ROW_TILES,), in_specs=[row, row, vec, vec, _fix_spec((D_MODEL, D_MODEL))],
        out_specs=_row_spec(D_MODEL), out_shape=SDS((SEQ, D_MODEL), F32), name="mix_out_fwd",
        compiler_params=_params(("parallel",)))(ona, os5, g_na, g_s5, wout)


def _mix_out_bwd(dmix, ona, os5, g_na, g_s5, wout):
    def body(dm_ref, a_ref, s_ref, ga_ref, gs_ref, w_ref, da_ref, ds_ref, dw_ref, dga_ref, dgs_ref):
        i = pl.program_id(0)
        dm = dm_ref[...]
        av, sv = a_ref[...], s_ref[...]
        ra, rs = _rstd(av), _rstd(sv)
        ga, gs = ga_ref[...], gs_ref[...]
        ca = (av * ra * ga).astype(BF16)
        cs = (sv * rs * gs).astype(BF16)
        dca = _dot_nt(dm, w_ref[0:NA_WIDTH, :])
        dcs = _dot_nt(dm, w_ref[NA_WIDTH:, :])
        da, dga = _rms_bwd(av, ra, ga, dca)
        ds, dgs = _rms_bwd(sv, rs, gs, dcs)
        da_ref[...] = da
        ds_ref[...] = ds

        @pl.when(i == 0)
        def _():
            dw_ref[...] = jnp.zeros_like(dw_ref)
            dga_ref[...] = jnp.zeros_like(dga_ref)
            dgs_ref[...] = jnp.zeros_like(dgs_ref)

        dw_ref[0:NA_WIDTH, :] += _dot_tn(ca, dm)
        dw_ref[NA_WIDTH:, :] += _dot_tn(cs, dm)
        dga_ref[...] += jnp.sum(dga, axis=0, keepdims=True)
        dgs_ref[...] += jnp.sum(dgs, axis=0, keepdims=True)

    row = _row_spec(NA_WIDTH)
    vec = _fix_spec((1, NA_WIDTH))
    mat = _fix_spec((D_MODEL, D_MODEL))
    return pl.pallas_call(
        body, grid=(N_ROW_TILES,), in_specs=[_row_spec(D_MODEL), row, row, vec, vec, mat],
        out_specs=[row, row, mat, vec, vec],
        out_shape=[SDS((SEQ, NA_WIDTH), F32)] * 2 + [SDS((D_MODEL, D_MODEL), F32), SDS((1, NA_WIDTH), F32), SDS((1, NA_WIDTH), F32)],
        name="mix_out_bwd", compiler_params=_params(("arbitrary",)))(dmix, ona, os5, g_na, g_s5, wout)


def _me():
    x, y, c = lax.axis_index("x"), lax.axis_index("y"), lax.axis_index("c")
    return x, y, c, 4 * x + 2 * y + c


def _peer(k):
    x, y, c, _ = _me()
    px = 1 - x if (k >> 2) & 1 else x
    py = 1 - y if (k >> 1) & 1 else y
    pc = 1 - c if k & 1 else c
    return (px, py, pc), 4 * px + 2 * py + pc


ALL_PEERS = (1, 2, 3, 4, 5, 6, 7)
CHIP_PEERS = (2, 4, 6)
SIBLING = 1


def _slot8(pos):
    return 4 * pos[0] + 2 * pos[1] + pos[2]


def _slot4(pos):
    return 2 * pos[0] + pos[1]


def _exchange(arrays, gather, name, after=()):
    n, n_after = len(arrays), len(after)

    def body(*refs):
        ins, outs = refs[:n], refs[n + n_after:2 * n + n_after]
        send_sems, recv_sems, local_sems = refs[2 * n + n_after:]
        _, _, _, me = _me()
        started = []
        for a in range(n):
            src_mine = ins[a] if gather else ins[a].at[me]
            local = pltpu.make_async_copy(src_mine, outs[a].at[me], local_sems.at[a])
            local.start()
            started.append(local)
        sends = []
        for k in range(1, N_DEV):
            peer, peer_idx = _peer(k)
            for a in range(n):
                src = ins[a] if gather else ins[a].at[peer_idx]
                cp = pltpu.make_async_remote_copy(src_ref=src, dst_ref=outs[a].at[me], send_sem=send_sems.at[a, k - 1],
                                                  recv_sem=recv_sems.at[a, k - 1], device_id=peer, device_id_type=MESH)
                cp.start()
                sends.append(cp)
        for k in range(1, N_DEV):
            peer, peer_idx = _peer(k)
            for a in range(n):
                src = ins[a] if gather else ins[a].at[peer_idx]
                pltpu.make_async_remote_copy(src_ref=src, dst_ref=outs[a].at[peer_idx], send_sem=send_sems.at[a, k - 1],
                                             recv_sem=recv_sems.at[a, k - 1], device_id=peer, device_id_type=MESH).wait_recv()
        for cp in sends:
            cp.wait_send()
        for local in started:
            local.wait()

    hbm = pl.BlockSpec(memory_space=pltpu.HBM)
    out_shape = [SDS((N_DEV,) + tuple(a.shape), a.dtype) if gather else SDS(a.shape, a.dtype) for a in arrays]
    return pl.pallas_call(
        body, in_specs=[hbm] * n + [pl.BlockSpec(memory_space=pl.ANY)] * n_after, out_specs=[hbm] * n, out_shape=out_shape,
        scratch_shapes=[pltpu.SemaphoreType.DMA((n, N_DEV - 1)), pltpu.SemaphoreType.DMA((n, N_DEV - 1)),
                        pltpu.SemaphoreType.DMA((n,))],
        name=name)(*arrays, *after)


_HBM = pl.BlockSpec(memory_space=pltpu.HBM)
_SEM = pl.BlockSpec(memory_space=pltpu.SEMAPHORE)
_EFFECT = pltpu.SideEffectType.DATAFLOW_SIDE_EFFECTING


def _land_shape(a, gather):
    return (N_DEV,) + tuple(a.shape) if gather else tuple(a.shape)


def _place_own(arrays, gather, name, slot=_slot8):
    n = len(arrays)
    me = slot(_me()[:3])

    def body(me_ref, *refs):
        for a in range(n):
            refs[n + a][...] = refs[a][...]

    def own_slot(a):
        zeros = (0,) * (a.ndim - (0 if gather else 1))
        return lambda i, me_ref: (me_ref[0],) + zeros

    def whole(a):
        return lambda i, me_ref: (0,) * a.ndim

    in_specs = [pl.BlockSpec(a.shape, whole(a)) if gather else pl.BlockSpec((None,) + a.shape[1:], own_slot(a)) for a in arrays]
    out_specs = [pl.BlockSpec((None,) + (a.shape if gather else a.shape[1:]), own_slot(a)) for a in arrays]
    return pl.pallas_call(
        body, grid_spec=pltpu.PrefetchScalarGridSpec(num_scalar_prefetch=1, grid=(1,), in_specs=in_specs, out_specs=out_specs),
        out_shape=[SDS(_land_shape(a, gather), a.dtype) for a in arrays], name=name,
        compiler_params=_params(("arbitrary",)))(me.reshape(1).astype(jnp.int32), *arrays)


def _exchange_start(arrays, lands, gather, name, peers=ALL_PEERS, slot=_slot8):
    n = len(arrays)

    def body(*refs):
        ins, lnd = refs[:n], refs[n:2 * n]
        send_sems, recv_sems = refs[2 * n], refs[2 * n + 1]
        token = refs[-1]
        me = slot(_me()[:3])
        for i, k in enumerate(peers):
            peer, _ = _peer(k)
            for a in range(n):
                src = ins[a] if gather else ins[a].at[slot(peer)]
                s = a * len(peers) + i
                pltpu.make_async_remote_copy(src_ref=src, dst_ref=lnd[a].at[me], send_sem=send_sems.at[s],
                                             recv_sem=recv_sems.at[s], device_id=peer, device_id_type=MESH).start()
        token[...] = jnp.zeros_like(token)

    sems = pltpu.SemaphoreType.DMA((n * len(peers),))
    out = pl.pallas_call(
        body, name=name, in_specs=[_HBM] * (2 * n),
        out_shape=(sems, sems) + tuple(pltpu.HBM(a.shape, a.dtype) for a in list(arrays) + list(lands)) + (SDS((8, 128), F32),),
        out_specs=(_SEM, _SEM) + (_HBM,) * (2 * n) + (pl.BlockSpec(memory_space=pltpu.VMEM),),
        input_output_aliases={i: 2 + i for i in range(2 * n)},
        compiler_params=pltpu.CompilerParams(has_side_effects=_EFFECT),
    )(*[pltpu.with_memory_space_constraint(a, pltpu.HBM) for a in list(arrays) + list(lands)])
    return out[0], out[1], list(out[2:2 + n]), list(out[2 + n:2 + 2 * n]), out[-1]


def _exchange_wait(send_sems, recv_sems, arrays, lands, after, gather, name, peers=ALL_PEERS, slot=_slot8):
    n = len(arrays)

    def body(*refs):
        ins, lnd = refs[:n], refs[n:2 * n]
        send_sems, recv_sems = refs[2 * n], refs[2 * n + 1]
        for i, k in enumerate(peers):
            peer, _ = _peer(k)
            for a in range(n):
                src = ins[a] if gather else ins[a].at[slot(peer)]
                s = a * len(peers) + i
                cp = pltpu.make_async_remote_copy(src_ref=src, dst_ref=lnd[a].at[slot(peer)], send_sem=send_sems.at[s],
                                                  recv_sem=recv_sems.at[s], device_id=peer, device_id_type=MESH)
                cp.wait_send()
                cp.wait_recv()

        refs[-1][...] = jnp.zeros_like(refs[-1])

    after = list(after) if isinstance(after, (list, tuple)) else [after]
    out = pl.pallas_call(
        body, name=name, in_specs=[_HBM] * (2 * n) + [_SEM, _SEM] + [pl.BlockSpec(memory_space=pl.ANY)] * len(after),
        out_shape=tuple(pltpu.HBM(a.shape, a.dtype) for a in list(arrays) + list(lands)) + (SDS((8, 128), F32),),
        out_specs=(_HBM,) * (2 * n) + (pl.BlockSpec(memory_space=pltpu.VMEM),), input_output_aliases={i: i for i in range(2 * n)},
        compiler_params=pltpu.CompilerParams(has_side_effects=_EFFECT),
    )(*arrays, *lands, send_sems, recv_sems, *after)
    return list(out[n:2 * n]), out[-1]


def _forward_sibling(lands, name):
    n = len(lands)

    def body(*refs):
        outs = refs[n:2 * n]
        send_sems, recv_sems = refs[2 * n:]
        x, y, c, _ = _me()
        sends = []
        for i, k in enumerate(CHIP_PEERS):
            peer, _ = _peer(k)
            for a in range(n):
                rows = outs[a].at[_slot8(peer)]
                cp = pltpu.make_async_remote_copy(src_ref=rows, dst_ref=rows, send_sem=send_sems.at[a, i], recv_sem=recv_sems.at[a, i],
                                                  device_id=(x, y, 1 - c), device_id_type=MESH)
                cp.start()
                sends.append(cp)
        for i, k in enumerate(CHIP_PEERS):
            (px, py, pc), _ = _peer(k)
            for a in range(n):
                rows = outs[a].at[_slot8((px, py, 1 - pc))]
                pltpu.make_async_remote_copy(src_ref=rows, dst_ref=rows, send_sem=send_sems.at[a, i], recv_sem=recv_sems.at[a, i],
                                             device_id=(x, y, 1 - c), device_id_type=MESH).wait_recv()
        for cp in sends:
            cp.wait_send()

    return pl.pallas_call(
        body, in_specs=[_HBM] * n, out_specs=[_HBM] * n, out_shape=[SDS(a.shape, a.dtype) for a in lands],
        input_output_aliases={i: i for i in range(n)},
        scratch_shapes=[pltpu.SemaphoreType.DMA((n, len(CHIP_PEERS))), pltpu.SemaphoreType.DMA((n, len(CHIP_PEERS)))],
        name=name)(*lands)


def _swap_sibling(arrays, name):
    n = len(arrays)
    chips = N_DEV // 2

    def body(*refs):
        ins, outs = refs[:n], refs[n:2 * n]
        send_sems, recv_sems = refs[2 * n:]
        x, y, c, _ = _me()
        sends = []
        for q in range(chips):
            for a in range(n):
                cp = pltpu.make_async_remote_copy(src_ref=ins[a].at[q, 1 - c], dst_ref=outs[a].at[q], send_sem=send_sems.at[a, q],
                                                  recv_sem=recv_sems.at[a, q], device_id=(x, y, 1 - c), device_id_type=MESH)
                cp.start()
                sends.append(cp)
        for cp in sends:
            cp.wait_recv()
        for cp in sends:
            cp.wait_send()

    return pl.pallas_call(
        body, in_specs=[_HBM] * n, out_specs=[_HBM] * n, out_shape=[SDS((chips,) + a.shape[2:], a.dtype) for a in arrays],
        scratch_shapes=[pltpu.SemaphoreType.DMA((n, chips)), pltpu.SemaphoreType.DMA((n, chips))], name=name)(*arrays)


def _sum_pairs(mine, theirs, name):
    chips, _, rows, cols = mine.shape
    c = lax.axis_index("c")

    def body(c_ref, a_ref, b_ref, o_ref):
        o_ref[...] = (a_ref[...].astype(F32) + b_ref[...].astype(F32)).astype(o_ref.dtype)

    return pl.pallas_call(
        body, grid_spec=pltpu.PrefetchScalarGridSpec(
            num_scalar_prefetch=1, grid=(chips,),
            in_specs=[pl.BlockSpec((None, None, rows, cols), lambda q, c_ref: (q, c_ref[0], 0, 0)),
                      pl.BlockSpec((None, rows, cols), lambda q, c_ref: (q, 0, 0))],
            out_specs=pl.BlockSpec((None, rows, cols), lambda q, c_ref: (q, 0, 0))),
        out_shape=SDS((chips, rows, cols), mine.dtype), name=name,
        compiler_params=_params(("parallel",)))(c.reshape(1).astype(jnp.int32), mine, theirs)


def _adamw_math(w, g, m, v):
    m = ADAM_B1 * m + (1.0 - ADAM_B1) * g
    v = ADAM_B2 * v + (1.0 - ADAM_B2) * (g * g)
    m_hat = m / (1.0 - ADAM_B1 ** ADAM_STEP)
    v_hat = v / (1.0 - ADAM_B2 ** ADAM_STEP)
    delta = -ADAM_LR * (m_hat / (jnp.sqrt(v_hat) + ADAM_EPS) + ADAM_WD * w)
    return delta, m, v


def _adamw(w, m, v, pieces, name):
    rows, cols = w.shape[-2:]
    lead = w.ndim - 2
    tile = rows
    for cand in (256, 176, 128, 64, 16):
        if rows > cand and rows % cand == 0:
            tile = cand
            break

    def body(w_ref, m_ref, v_ref, p_ref, g_ref, d_ref, mo_ref, vo_ref):
        g = _sum_pieces(p_ref)
        g_ref[...] = g
        d_ref[...], mo_ref[...], vo_ref[...] = _adamw_math(w_ref[...], g, m_ref[...], v_ref[...])

    blk = pl.BlockSpec((None,) * lead + (tile, cols), lambda i: (0,) * lead + (i, 0))
    return pl.pallas_call(
        body, grid=(rows // tile,), in_specs=[blk, blk, blk, pl.BlockSpec((pieces.shape[0], tile, cols), lambda i: (0, i, 0))],
        out_specs=[blk] * 4, out_shape=[SDS(w.shape, F32)] * 4, name=name,
        compiler_params=_params(("parallel",)))(w, m, v, pieces)


def _sum_pieces(p_ref):
    g = p_ref[0].astype(F32)
    for p in range(1, p_ref.shape[0]):
        g = g + p_ref[p].astype(F32)
    return g


def _adamw_s5_mat(w, m, v, g, name):
    _, ndir, groups, b, c = w.shape
    per_dir = groups // 8

    def body(w_ref, m_ref, v_ref, g_ref, d_ref, mo_ref, vo_ref):
        d_ref[...], mo_ref[...], vo_ref[...] = _adamw_math(w_ref[...], g_ref[...], m_ref[...], v_ref[...])

    blk = pl.BlockSpec((None, None, 8, b, c), lambda i: (0, i // per_dir, i % per_dir, 0, 0))
    return pl.pallas_call(
        body, grid=(ndir * per_dir,), in_specs=[blk] * 4, out_specs=[blk] * 3, out_shape=[SDS(w.shape, F32)] * 3, name=name,
        compiler_params=_params(("parallel",)))(w, m, v, g)


VEC_ROWS = ['ffn1_pre_g', 'ffn1_post_g', 'mix_pre_g', 'mix_post_g', 'ffn2_pre_g', 'ffn2_post_g', 'final_g',
            ('na_out_g', 's5_out_g'), ('s5_d', 's5_b_glu')]
VEC_NAMES = [n for row in VEC_ROWS for n in ((row,) if isinstance(row, str) else row)]
VEC_PACK_ROWS = 16
LOSS_ROW = len(VEC_ROWS)


def _pack_vectors(grads, loss8):
    def body(*refs):
        o_ref = refs[-1]
        o_ref[...] = jnp.zeros_like(o_ref)
        o_ref[LOSS_ROW:LOSS_ROW + 1, 0:128] = refs[-2][0:1, :]
        k = 0
        for i, row in enumerate(VEC_ROWS):
            if isinstance(row, str):
                o_ref[i:i + 1, :] = refs[k][...]
                k += 1
            else:
                o_ref[i:i + 1, 0:NA_WIDTH] = refs[k][...]
                o_ref[i:i + 1, NA_WIDTH:] = refs[k + 1][...]
                k += 2

    return pl.pallas_call(body, out_shape=SDS((VEC_PACK_ROWS, D_MODEL), F32), name="pack_vectors",
                          compiler_params=_params())(*[grads[n] for n in VEC_NAMES], loss8)


def _sum8(pieces, name):
    def body(p_ref, o_ref):
        o_ref[...] = _sum_pieces(p_ref)

    return pl.pallas_call(body, out_shape=SDS(pieces.shape[1:], F32), name=name, compiler_params=_params())(pieces)


def _adamw_small(packed8, vec_wmv, others):
    n_vec, n_oth = len(VEC_NAMES), len(others)

    def body(*refs):
        p_ref = refs[0]
        ins = refs[1:1 + 3 * n_vec + 4 * n_oth]
        outs = refs[1 + 3 * n_vec + 4 * n_oth:]
        gsum = _sum_pieces(p_ref)
        outs[-1][...] = gsum[LOSS_ROW:LOSS_ROW + 1, 0:128]
        k = 0
        for i, row in enumerate(VEC_ROWS):
            parts = [(row, gsum[i:i + 1, :])] if isinstance(row, str) else \
                [(row[0], gsum[i:i + 1, 0:NA_WIDTH]), (row[1], gsum[i:i + 1, NA_WIDTH:])]
            for _, g in parts:
                w_ref, m_ref, v_ref = ins[3 * k:3 * k + 3]
                outs[4 * k][...] = g
                outs[4 * k + 1][...], outs[4 * k + 2][...], outs[4 * k + 3][...] = _adamw_math(w_ref[...], g, m_ref[...], v_ref[...])
                k += 1
        for j in range(n_oth):
            w_ref, m_ref, v_ref, g_ref = ins[3 * n_vec + 4 * j:3 * n_vec + 4 * j + 4]
            g = _sum_pieces(g_ref)
            g = g[tuple(slice(0, s) for s in w_ref.shape[1:])].reshape(w_ref.shape)
            o = outs[4 * (n_vec + j):4 * (n_vec + j) + 4]
            o[0][...] = g
            o[1][...], o[2][...], o[3][...] = _adamw_math(w_ref[...], g, m_ref[...], v_ref[...])

    args, out_shape = [packed8], []
    for w, m, v in vec_wmv:
        args += [w, m, v]
        out_shape += [SDS(w.shape, F32)] * 4
    for w, m, v, g in others:
        args += [w, m, v, g]
        out_shape += [SDS(w.shape, F32)] * 4
    out_shape += [SDS((1, 128), F32)]
    return pl.pallas_call(body, out_shape=out_shape, name="adamw_small", compiler_params=_params())(*args)


def _perm_rows(x):
    return x.reshape(SCAN_BLOCKS, SCAN_T, x.shape[-1]).transpose(1, 0, 2).reshape(SEQ, x.shape[-1])


def _unperm_rows(x):
    return x.reshape(SCAN_T, SCAN_BLOCKS, x.shape[-1]).transpose(1, 0, 2).reshape(SEQ, x.shape[-1])


def _block_diag(x):
    eye = np.eye(8, dtype=bool)[None, None, :, None, :, None]
    full = jnp.where(eye, x[:, :, :, :, None, :], 0.0)
    return full.reshape(2, S5_CHUNKS, 8 * x.shape[3], 8 * x.shape[4])


def _diag_blocks(x, r, c):
    x6 = x.reshape(2, S5_CHUNKS, 8, r, 8, c)
    return jnp.stack([x6[:, :, g, :, g, :] for g in range(8)], axis=2)


STORED_SWAPPED = {"ffn1_w_gate": (1, 2), "ffn1_w_up": (1, 2), "ffn2_w_gate": (1, 2), "ffn2_w_up": (1, 2),
                  "s5_b_re": (3, 4), "s5_b_im": (3, 4)}


def _stored(name, x):
    return jnp.swapaxes(x, *STORED_SWAPPED[name]) if name in STORED_SWAPPED else x


def _dep(x, token):
    return x if token is None else x + token


def _local_step(x, target, get_w, small, emit):
    bias = _rpb_expand(small["na_rpb"][0])
    lr = small["s5_lam_re"].reshape(64, S5_STATE)
    li = small["s5_lam_im"].reshape(64, S5_STATE)
    logdt = small["s5_log_dt"].reshape(64, 1)
    b_t = [_stored(n, small[n]).reshape(64, S5_GROUP, S5_STATE) for n in ("s5_b_re", "s5_b_im")]
    lbr, lbi, bbr, bbi = _s5_prep(lr, li, logdt, b_t[0], b_t[1])
    are = lbr.reshape(2, S5_CHUNKS, 1, ST_W)
    aim = lbi.reshape(2, S5_CHUNKS, 1, ST_W)
    bre = _block_diag(bbr.reshape(2, S5_CHUNKS, 8, S5_GROUP, S5_STATE)).astype(BF16)
    bim = _block_diag(bbi.reshape(2, S5_CHUNKS, 8, S5_GROUP, S5_STATE)).astype(BF16)
    c_t = [small[n].reshape(2, S5_CHUNKS, 8, S5_GROUP, S5_STATE).transpose(0, 1, 2, 4, 3) for n in ("s5_c_re", "s5_c_im")]
    cre = _block_diag(c_t[0]).astype(BF16)
    cim = _block_diag(c_t[1]).astype(BF16)
    tgt = jnp.concatenate([jnp.zeros((N_META, D_MODEL), F32), target], axis=0)

    h0 = jnp.concatenate([get_w("meta", None)["meta_tokens"], x], axis=0)
    a1 = _prenorm(h0, small["ffn1_pre_g"])
    wts = dict(get_w("ffn1", [bias, are, aim, bre, bim, cre, cim, tgt, a1]))
    gate1, up1, f1 = _ffn_fwd(a1, wts["ffn1_w_gate"], wts["ffn1_w_up"], wts["ffn1_w_down"], "ffn1_fwd",
                              after=wts.get("tokens", ()))
    h1, a2 = _post_pre(f1, h0, small["ffn1_post_g"], small["mix_pre_g"], 0.5, "post_pre1")
    wts.update(get_w("w_in", a2))
    proj = _proj_fwd(a2, wts["w_in"])
    qkv = proj[:6].reshape(3, 2, SEQ, 4, HEAD_DIM).transpose(0, 1, 3, 2, 4).reshape(3, HEADS, SEQ, HEAD_DIM)
    u = proj[6:].transpose(1, 0, 2).reshape(SEQ, S5_WIDTH)
    o3 = _na_fwd(qkv[0], qkv[1], qkv[2], bias)
    ona = o3.transpose(1, 0, 2).reshape(SEQ, NA_WIDTH)
    u_p = _perm_rows(u)
    sr, si, y2 = _s5_scan_fwd(u_p, bre, bim, are, aim, cre, cim)
    wts.update(get_w("mix", y2))
    os5_p, ypre_p = _s5_glu_fwd(u_p, y2, small["s5_d"], wts["s5_w_glu"], small["s5_b_glu"])
    os5 = _unperm_rows(os5_p)

    mix = _mix_out_fwd(ona, os5, small["na_out_g"], small["s5_out_g"], wts["w_out"])
    h2, a3 = _post_pre(mix, h1, small["mix_post_g"], small["ffn2_pre_g"], 1.0, "post_pre2")
    wts.update(get_w("ffn2", a3))
    gate2, up2, f2 = _ffn_fwd(a3, wts["ffn2_w_gate"], wts["ffn2_w_up"], wts["ffn2_w_down"], "ffn2_fwd")
    loss8, dh3, df2, g_final, g_ffn2_post = _final_loss(f2, h2, small["ffn2_post_g"], small["final_g"], tgt)

    da3, dwg2, dwu2, dwd2 = _ffn_bwd(df2, a3, gate2, up2, wts["ffn2_w_gate"], wts["ffn2_w_up"], wts["ffn2_w_down"], "ffn2_bwd")
    tok = emit("ffn2", {"ffn2_w_gate": dwg2, "ffn2_w_up": dwu2, "ffn2_w_down": dwd2})
    dh2, dmix, g_ffn2_pre, g_mix_post = _bwd_pre_post(da3, h2, _dep(small["ffn2_pre_g"], tok), dh3, mix, small["mix_post_g"], 1.0,
                                                      "bwd_pre_post2")
    dona, dos5, dwout, g_na_out, g_s5_out = _mix_out_bwd(dmix, ona, os5, small["na_out_g"], small["s5_out_g"], wts["w_out"])

    dypre_p, du_skip_p, dwglu, g_b_glu, g_s5_d = _s5_glu_bwd(_perm_rows(dos5), ypre_p, u_p, small["s5_d"], wts["s5_w_glu"],
                                                             small["s5_b_glu"])
    tok = emit("mix", {"s5_w_glu": dwglu.reshape(N_DEV, S5_WIDTH // N_DEV, S5_WIDTH).astype(BF16),
                       "w_out": dwout.reshape(N_DEV, D_MODEL // N_DEV, D_MODEL).astype(BF16)})
    du_p, dbr, dbi, dcr, dci, dar, dai = _s5_scan_bwd(dypre_p, du_skip_p, u_p, sr, si, bre, bim, _dep(are, tok), aim, cre, cim)
    du = _unperm_rows(du_p)
    dbbr = _diag_blocks(dbr, S5_GROUP, S5_STATE).reshape(64, S5_GROUP, S5_STATE)
    dbbi = _diag_blocks(dbi, S5_GROUP, S5_STATE).reshape(64, S5_GROUP, S5_STATE)
    g_lr, g_li, g_dt, g_br, g_bi = _s5_prep_bwd(lr, li, logdt, b_t[0], b_t[1], dar.reshape(64, S5_STATE),
                                                dai.reshape(64, S5_STATE), dbbr, dbbi)
    g_c = [_diag_blocks(d, S5_STATE, S5_GROUP).transpose(0, 1, 2, 4, 3).reshape(2 * S5_GROUPS, S5_GROUP, S5_STATE)
           for d in (dcr, dci)]

    do3 = dona.reshape(SEQ, HEADS, HEAD_DIM).transpose(1, 0, 2)
    dq, dk, dv, dbias = _na_bwd(qkv[0], qkv[1], qkv[2], bias, do3)
    g_rpb = _rpb_reduce(dbias)
    dense = jnp.stack([g.reshape(2 * S5_GROUPS, S5_STATE * S5_GROUP) for g in (g_br, g_bi, *g_c)])
    tok = emit("small", {"dense": dense, "na_rpb": g_rpb,
                         "s5_lam_re": g_lr.reshape(2, S5_GROUPS, S5_STATE), "s5_lam_im": g_li.reshape(2, S5_GROUPS, S5_STATE),
                         "s5_log_dt": g_dt.reshape(2, S5_GROUPS)})
    dqkv = jnp.stack([dq, dk, dv]).reshape(3, 2, 4, SEQ, HEAD_DIM).transpose(0, 1, 3, 2, 4).reshape(6, SEQ, IN_SHARD)
    dproj = jnp.concatenate([dqkv, du.reshape(SEQ, 2, IN_SHARD).transpose(1, 0, 2)], axis=0).astype(BF16)
    da2, dwin = _proj_bwd(dproj, a2, wts["w_in"])
    tok2 = emit("w_in", {"w_in": dwin})
    tok = tok if tok2 is None else tok + tok2
    dh1, df1, g_mix_pre, g_ffn1_post = _bwd_pre_post(da2, h1, _dep(small["mix_pre_g"], tok), dh2, f1, small["ffn1_post_g"], 0.5,
                                                     "bwd_pre_post1")
    da1, dwg1, dwu1, dwd1 = _ffn_bwd(df1, a1, gate1, up1, wts["ffn1_w_gate"], wts["ffn1_w_up"], wts["ffn1_w_down"], "ffn1_bwd")
    emit("ffn1", {"ffn1_w_gate": dwg1, "ffn1_w_up": dwu1, "ffn1_w_down": dwd1})
    dh0, g_ffn1_pre = _bwd_pre_only(da1, h0, small["ffn1_pre_g"], dh1)

    vec_g = {
        "ffn1_pre_g": g_ffn1_pre, "ffn1_post_g": g_ffn1_post, "mix_pre_g": g_mix_pre, "s5_d": g_s5_d, "s5_b_glu": g_b_glu,
        "na_out_g": g_na_out, "s5_out_g": g_s5_out, "mix_post_g": g_mix_post,
        "ffn2_pre_g": g_ffn2_pre, "ffn2_post_g": g_ffn2_post, "final_g": g_final,
    }
    return loss8, dh0[N_META:], dh0[:N_META], vec_g


WEIGHT_NAMES = ['meta_tokens', 'ffn1_pre_g', 'ffn1_post_g', 'ffn1_w_gate', 'ffn1_w_up', 'ffn1_w_down', 'mix_pre_g', 'w_in',
                'na_rpb', 's5_lam_re', 's5_lam_im', 's5_log_dt', 's5_b_re', 's5_b_im', 's5_c_re', 's5_c_im', 's5_d',
                's5_w_glu', 's5_b_glu', 'na_out_g', 's5_out_g', 'w_out', 'mix_post_g', 'ffn2_pre_g', 'ffn2_post_g',
                'ffn2_w_gate', 'ffn2_w_up', 'ffn2_w_down', 'final_g']
BIG_NAMES = ['ffn1_w_gate', 'ffn1_w_up', 'ffn1_w_down', 'w_in', 's5_w_glu', 'w_out', 'ffn2_w_gate', 'ffn2_w_up', 'ffn2_w_down']
SMALL_NAMES = [n for n in WEIGHT_NAMES if n not in BIG_NAMES and n != 'meta_tokens']
WHOLE_NAMES = ['na_rpb', 's5_lam_re', 's5_lam_im', 's5_log_dt']
LEAD_NAMES = ['s5_b_re', 's5_b_im', 's5_c_re', 's5_c_im']


def kernel(x, meta_tokens, ffn1_pre_g, ffn1_post_g, ffn1_w_gate, ffn1_w_up, ffn1_w_down, mix_pre_g, w_in, na_rpb, s5_lam_re, s5_lam_im, s5_log_dt, s5_b_re, s5_b_im, s5_c_re, s5_c_im, s5_d, s5_w_glu, s5_b_glu, na_out_g, s5_out_g, w_out, mix_post_g, ffn2_pre_g, ffn2_post_g, ffn2_w_gate, ffn2_w_up, ffn2_w_down, final_g, loss_target, m_meta_tokens, m_ffn1_pre_g, m_ffn1_post_g, m_ffn1_w_gate, m_ffn1_w_up, m_ffn1_w_down, m_mix_pre_g, m_w_in, m_na_rpb, m_s5_lam_re, m_s5_lam_im, m_s5_log_dt, m_s5_b_re, m_s5_b_im, m_s5_c_re, m_s5_c_im, m_s5_d, m_s5_w_glu, m_s5_b_glu, m_na_out_g, m_s5_out_g, m_w_out, m_mix_post_g, m_ffn2_pre_g, m_ffn2_post_g, m_ffn2_w_gate, m_ffn2_w_up, m_ffn2_w_down, m_final_g, v_meta_tokens, v_ffn1_pre_g, v_ffn1_post_g, v_ffn1_w_gate, v_ffn1_w_up, v_ffn1_w_down, v_mix_pre_g, v_w_in, v_na_rpb, v_s5_lam_re, v_s5_lam_im, v_s5_log_dt, v_s5_b_re, v_s5_b_im, v_s5_c_re, v_s5_c_im, v_s5_d, v_s5_w_glu, v_s5_b_glu, v_na_out_g, v_s5_out_g, v_w_out, v_mix_post_g, v_ffn2_pre_g, v_ffn2_post_g, v_ffn2_w_gate, v_ffn2_w_up, v_ffn2_w_down, v_final_g):
    args = dict(locals())
    w = {n: args[n] for n in WEIGHT_NAMES}
    m = {n: args["m_" + n] for n in WEIGHT_NAMES}
    v = {n: args["v_" + n] for n in WEIGHT_NAMES}

    small = {n: w[n] for n in SMALL_NAMES}

    pending = {}

    def start(group, names, arrays, gather, peers=ALL_PEERS, slot=_slot8):
        lands = _place_own(arrays, gather, "own_" + group, slot)
        send_sems, recv_sems, arrays, lands, token = _exchange_start(arrays, lands, gather, "start_" + group, peers, slot)
        pending[group] = (names, send_sems, recv_sems, arrays, lands, gather, peers, slot)
        return token

    def finish(group, after):
        names, send_sems, recv_sems, arrays, lands, gather, peers, slot = pending.pop(group)
        lands, token = _exchange_wait(send_sems, recv_sems, arrays, lands, after, gather, "wait_" + group, peers, slot)
        return dict(zip(names, lands)), token

    first = ["ffn1_w_gate", "ffn1_w_up", "ffn1_w_down"]
    def shard(n):
        return _stored(n, w[n])[0].astype(BF16)

    ffn_names = ("ffn1_w_gate", "ffn1_w_up", "ffn1_w_down", "ffn2_w_gate", "ffn2_w_up", "ffn2_w_down")
    later_groups = (("w_in", ["w_in"]), ("mix", ["s5_w_glu", "w_out"]), ("ffn2", ["ffn2_w_gate", "ffn2_w_up", "ffn2_w_down"]))
    meta_full = _exchange([w["meta_tokens"]], True, "gather_meta")[0].transpose(1, 0, 2).reshape(N_META, D_MODEL)
    start("ffn1", first, [shard(n) for n in first], True, (SIBLING,) + CHIP_PEERS)
    later_shards = {n: shard(n) for _, names in later_groups for n in names}

    def get_w(group, after):
        if group == "meta":
            return {"meta_tokens": meta_full}
        if group == "ffn1":
            after = list(after) + list(later_shards.values())
        got, token = finish(group, after)
        if group == "ffn1":
            got = dict(zip(got, _forward_sibling(list(got.values()), "forward_ffn1")))
            got["tokens"] = [start(g, names + ["order"], [later_shards[n] for n in names] + [token], True) for g, names in later_groups]
        if group == "mix":
            got = {"s5_w_glu": got["s5_w_glu"].reshape(S5_WIDTH, S5_WIDTH), "w_out": got["w_out"].reshape(D_MODEL, D_MODEL)}
        return {n: (a.reshape(D_FF, D_MODEL) if n in ffn_names else a) for n, a in got.items()}

    tokens = {}

    def emit(group, grads):
        grads = {n: (g.reshape(N_DEV, FF_SHARD, D_MODEL) if n in ffn_names else g) for n, g in grads.items()}
        if group == "ffn1":
            mine = [g.reshape((N_DEV // 2, 2) + g.shape[1:]) for g in grads.values()]
            theirs = _swap_sibling(mine, "swap_g_ffn1")
            sums = [_sum_pairs(a, b, "pair_sum_" + n) for n, a, b in zip(grads, mine, theirs)]
            tokens[group] = start("g_ffn1", list(grads), sums, False, CHIP_PEERS, _slot4)
        else:
            tokens[group] = start("g_" + group, list(grads), list(grads.values()), group == "small")
        return tokens[group][0, 0]

    loss8, grad_x, gmeta, vec_g = _local_step(x[0], loss_target[0], get_w, small, emit)
    res = {}

    def update_shard(n, pieces):
        outs = _adamw(_stored(n, w[n]), _stored(n, m[n]), _stored(n, v[n]), pieces, "adamw_" + n)
        res[n] = [_stored(n, o) for o in outs]

    late = [grad_x, tokens["ffn1"]]
    for group in ("g_ffn2", "g_mix", "g_w_in"):
        for n, pieces in finish(group, late)[0].items():
            update_shard(n, pieces)
    g8 = finish("g_small", late)[0]
    dense = _sum8(g8["dense"], "sum_dense")
    for i, n in enumerate(LEAD_NAMES):
        g = dense[i].reshape(_stored(n, w[n]).shape)
        upd = _adamw_s5_mat(_stored(n, w[n]), _stored(n, m[n]), _stored(n, v[n]), g, "adamw_" + n)
        res[n] = [_stored(n, o) for o in [g] + list(upd)]

    done = [res[n][1] for n in ("ffn2_w_gate", "ffn2_w_up", "ffn2_w_down", "w_in", "w_out", "s5_w_glu") + tuple(LEAD_NAMES)]
    packed8, gmeta8 = _exchange([_pack_vectors(vec_g, loss8), gmeta], True, "gather_vectors", after=done)
    for n, pieces in finish("g_ffn1", packed8)[0].items():
        update_shard(n, pieces)
    _, _, _, me = _me()
    update_shard("meta_tokens", lax.dynamic_slice_in_dim(gmeta8, me * (D_MODEL // N_DEV), D_MODEL // N_DEV, axis=2))

    outs = _adamw_small(packed8, [(w[n], m[n], v[n]) for n in VEC_NAMES], [(w[n], m[n], v[n], g8[n]) for n in WHOLE_NAMES])
    for i, n in enumerate(VEC_NAMES + WHOLE_NAMES):
        res[n] = list(outs[4 * i:4 * i + 4])

    out = [outs[-1][0, 0], grad_x[None]]
    for kind in range(4):
        out += [res[n][kind] for n in WEIGHT_NAMES]
    return tuple(out)
```

```python
import functools
import math

import numpy as np
import jax
import jax.numpy as jnp
from jax import lax
from jax.experimental import pallas as pl
from jax.experimental.pallas import tpu as pltpu

F32 = jnp.float32
BF16 = jnp.bfloat16
SDS = jax.ShapeDtypeStruct

D_MODEL = 1024
N_TOK = 2048
N_META = 16
SEQ = N_TOK + N_META
ROW_TILE = 688
N_ROW_TILES = SEQ // ROW_TILE
N_DEV = 8
D_FF = 2816
FF_SHARD = D_FF // N_DEV
FF_TILE = 256
IN_SHARD = 256
NA_WIDTH = 512
S5_WIDTH = 512
HEADS = 8
HEAD_DIM = 64
GRID_W = 64
GRID_ROWS = N_TOK // GRID_W
KH = 8
KW = 16
NA_RB = 4
NA_KR = KH + NA_RB - 1
NA_BLOCKS = GRID_ROWS // NA_RB
NA_QB = NA_RB * GRID_W
NA_KB = NA_KR * GRID_W
NA_TYPES = 3
S5_GROUPS = 32
S5_GROUP = 16
S5_STATE = 64
S5_CHUNKS = 4
CH_W = S5_WIDTH // S5_CHUNKS
ST_W = S5_GROUPS * S5_STATE // S5_CHUNKS
SCAN_BLOCKS = 8
SCAN_T = SEQ // SCAN_BLOCKS
RMS_EPS = 1e-6
NEG_INF = -1e30
ATT_SCALE = HEAD_DIM ** -0.5
ADAM_LR, ADAM_B1, ADAM_B2, ADAM_EPS, ADAM_WD, ADAM_STEP = 0.001, 0.9, 0.999, 1e-08, 0.01, 10
VMEM_LIMIT = 56 * 1024 * 1024
MESH = pl.DeviceIdType.MESH
AXES = ("x", "y", "c")


def _params(sem=None):
    return pltpu.CompilerParams(dimension_semantics=sem, vmem_limit_bytes=VMEM_LIMIT)


def _dot(a, b):
    return jnp.dot(a, b, preferred_element_type=F32)


def _dot_nt(a, b):
    return lax.dot_general(a, b, (((1,), (1,)), ((), ())), preferred_element_type=F32)


def _dot_tn(a, b):
    return lax.dot_general(a, b, (((0,), (0,)), ((), ())), preferred_element_type=F32)


def _rstd(x):
    return lax.rsqrt(jnp.mean(x * x, axis=-1, keepdims=True) + RMS_EPS)


def _rms_bwd(x, r, g, dy):
    dyg = dy * g
    xr = x * r
    dx = r * (dyg - xr * jnp.mean(dyg * xr, axis=-1, keepdims=True))
    return dx, dy * xr


def _rows(i, size=ROW_TILE):
    return pl.ds(pl.multiple_of(i * size, 16), size)


def _row_spec(width):
    return pl.BlockSpec((ROW_TILE, width), lambda i: (i, 0))


def _fix_spec(shape):
    return pl.BlockSpec(shape, lambda i: (0,) * len(shape))


def _split3(x):
    hi = x.astype(BF16)
    r1 = x - hi.astype(F32)
    mid = r1.astype(BF16)
    lo = (r1 - mid.astype(F32)).astype(BF16)
    return hi, mid, lo


def _prenorm(x, g):
    def body(x_ref, g_ref, a_ref):
        xv = x_ref[...]
        a_ref[...] = (xv * _rstd(xv) * g_ref[...]).astype(BF16)

    return pl.pallas_call(
        body, grid=(N_ROW_TILES,), in_specs=[_row_spec(D_MODEL), _fix_spec((1, D_MODEL))],
        out_specs=_row_spec(D_MODEL), out_shape=SDS((SEQ, D_MODEL), BF16), name="prenorm",
        compiler_params=_params(("parallel",)))(x, g)


def _post_pre(f, hres, g_post, g_next, scale, name):
    def body(f_ref, h_ref, gp_ref, gn_ref, ho_ref, a_ref):
        fv = f_ref[...]
        h = h_ref[...] + scale * (fv * _rstd(fv) * gp_ref[...])
        ho_ref[...] = h
        a_ref[...] = (h * _rstd(h) * gn_ref[...]).astype(BF16)

    return pl.pallas_call(
        body, grid=(N_ROW_TILES,),
        in_specs=[_row_spec(D_MODEL), _row_spec(D_MODEL), _fix_spec((1, D_MODEL)), _fix_spec((1, D_MODEL))],
        out_specs=[_row_spec(D_MODEL), _row_spec(D_MODEL)],
        out_shape=[SDS((SEQ, D_MODEL), F32), SDS((SEQ, D_MODEL), BF16)], name=name,
        compiler_params=_params(("parallel",)))(f, hres, g_post, g_next)


def _final_loss(f2, h2, g_post, g_final, target):
    def body(f_ref, h_ref, gp_ref, gf_ref, t_ref, loss_ref, dh_ref, df_ref, dgf_ref, dgp_ref):
        i = pl.program_id(0)
        fv = f_ref[...]
        r1 = _rstd(fv)
        gp = gp_ref[...]
        h3 = h_ref[...] + 0.5 * (fv * r1 * gp)
        r2 = _rstd(h3)
        gf = gf_ref[...]
        y = h3 * r2 * gf
        row = lax.broadcasted_iota(jnp.int32, (ROW_TILE, 1), 0) + i * ROW_TILE
        err = jnp.where(row >= N_META, y - t_ref[...], 0.0)
        part = 0.5 * jnp.sum(jnp.mean(err * err, axis=-1, keepdims=True))
        dy = err * (1.0 / D_MODEL)
        dh3, dgf = _rms_bwd(h3, r2, gf, dy)
        dh_ref[...] = dh3
        df, dgp = _rms_bwd(fv, r1, gp, 0.5 * dh3)
        df_ref[...] = df.astype(BF16)

        @pl.when(i == 0)
        def _():
            loss_ref[...] = jnp.zeros_like(loss_ref)
            dgf_ref[...] = jnp.zeros_like(dgf_ref)
            dgp_ref[...] = jnp.zeros_like(dgp_ref)

        loss_ref[...] += part
        dgf_ref[...] += jnp.sum(dgf, axis=0, keepdims=True)
        dgp_ref[...] += jnp.sum(dgp, axis=0, keepdims=True)

    gain = _fix_spec((1, D_MODEL))
    return pl.pallas_call(
        body, grid=(N_ROW_TILES,),
        in_specs=[_row_spec(D_MODEL), _row_spec(D_MODEL), gain, gain, _row_spec(D_MODEL)],
        out_specs=[_fix_spec((8, 128)), _row_spec(D_MODEL), _row_spec(D_MODEL), gain, gain],
        out_shape=[SDS((8, 128), F32), SDS((SEQ, D_MODEL), F32), SDS((SEQ, D_MODEL), BF16),
                   SDS((1, D_MODEL), F32), SDS((1, D_MODEL), F32)],
        name="final_loss", compiler_params=_params(("arbitrary",)))(f2, h2, g_post, g_final, target)


def _bwd_pre_post(da, h, g_pre, dh_res, fprev, g_post, scale, name):
    def body(da_ref, h_ref, gpre_ref, dhr_ref, f_ref, gpost_ref, dh_ref, df_ref, dgpre_ref, dgpost_ref):
        i = pl.program_id(0)
        hv = h_ref[...]
        dxa, dgpre = _rms_bwd(hv, _rstd(hv), gpre_ref[...], da_ref[...])
        dh = dhr_ref[...] + dxa
        dh_ref[...] = dh
        fv = f_ref[...]
        df, dgpost = _rms_bwd(fv, _rstd(fv), gpost_ref[...], scale * dh)
        df_ref[...] = df.astype(BF16)

        @pl.when(i == 0)
        def _():
            dgpre_ref[...] = jnp.zeros_like(dgpre_ref)
            dgpost_ref[...] = jnp.zeros_like(dgpost_ref)

        dgpre_ref[...] += jnp.sum(dgpre, axis=0, keepdims=True)
        dgpost_ref[...] += jnp.sum(dgpost, axis=0, keepdims=True)

    gain = _fix_spec((1, D_MODEL))
    row = _row_spec(D_MODEL)
    return pl.pallas_call(
        body, grid=(N_ROW_TILES,), in_specs=[row, row, gain, row, row, gain],
        out_specs=[row, row, gain, gain],
        out_shape=[SDS((SEQ, D_MODEL), F32), SDS((SEQ, D_MODEL), BF16), SDS((1, D_MODEL), F32), SDS((1, D_MODEL), F32)],
        name=name, compiler_params=_params(("arbitrary",)))(da, h, g_pre, dh_res, fprev, g_post)


def _bwd_pre_only(da, h, g_pre, dh_res):
    def body(da_ref, h_ref, gpre_ref, dhr_ref, dh_ref, dgpre_ref):
        i = pl.program_id(0)
        hv = h_ref[...]
        dxa, dgpre = _rms_bwd(hv, _rstd(hv), gpre_ref[...], da_ref[...])
        dh_ref[...] = dhr_ref[...] + dxa

        @pl.when(i == 0)
        def _():
            dgpre_ref[...] = jnp.zeros_like(dgpre_ref)

        dgpre_ref[...] += jnp.sum(dgpre, axis=0, keepdims=True)

    gain = _fix_spec((1, D_MODEL))
    row = _row_spec(D_MODEL)
    return pl.pallas_call(
        body, grid=(N_ROW_TILES,), in_specs=[row, row, gain, row], out_specs=[row, gain],
        out_shape=[SDS((SEQ, D_MODEL), F32), SDS((1, D_MODEL), F32)],
        name="bwd_pre_only", compiler_params=_params(("arbitrary",)))(da, h, g_pre, dh_res)


def _ffn_fwd(a, wg, wu, wd, name, after=()):
    def body(a_ref, wg_ref, wu_ref, wd_ref, *rest):
        gate_ref, up_ref, f_ref = rest[len(after):]
        j = pl.program_id(0)

        def tile(i, carry):
            rows = _rows(i)
            at = a_ref[rows, :]
            gate = _dot_nt(at, wg_ref[...])
            up = _dot_nt(at, wu_ref[...])
            gate_ref[rows, :] = gate.astype(BF16)
            up_ref[rows, :] = up.astype(BF16)
            act = (gate * jax.nn.sigmoid(gate) * up).astype(BF16)
            contrib = _dot(act, wd_ref[...])

            @pl.when(j == 0)
            def _():
                f_ref[rows, :] = contrib

            @pl.when(j != 0)
            def _():
                f_ref[rows, :] += contrib

            return carry

        lax.fori_loop(0, N_ROW_TILES, tile, 0)

    wtile = pl.BlockSpec((FF_TILE, D_MODEL), lambda j: (j, 0))
    hid = pl.BlockSpec((SEQ, FF_TILE), lambda j: (0, j))
    full = pl.BlockSpec((SEQ, D_MODEL), lambda j: (0, 0))
    return pl.pallas_call(
        body, grid=(D_FF // FF_TILE,), in_specs=[full, wtile, wtile, wtile] + [pl.BlockSpec(memory_space=pl.ANY)] * len(after),
        out_specs=[hid, hid, full],
        out_shape=[SDS((SEQ, D_FF), BF16), SDS((SEQ, D_FF), BF16), SDS((SEQ, D_MODEL), F32)],
        name=name, compiler_params=_params(("arbitrary",)))(a, wg, wu, wd, *after)


def _ffn_bwd(df, a, gate, up, wg, wu, wd, name):
    def body(df_ref, a_ref, gate_ref, up_ref, wg_ref, wu_ref, wd_ref, da_ref, dwg_ref, dwu_ref, dwd_ref,
             acc_g, acc_u, acc_d):
        j = pl.program_id(0)

        def tile(i, carry):
            rows = _rows(i)
            dft = df_ref[rows, :]
            at = a_ref[rows, :]
            gate = gate_ref[rows, :].astype(F32)
            up = up_ref[rows, :].astype(F32)
            dact = _dot_nt(dft, wd_ref[...])
            sig = jax.nn.sigmoid(gate)
            silu = gate * sig
            dgate = (dact * up * (sig * (1.0 + gate * (1.0 - sig)))).astype(BF16)
            dup = (dact * silu).astype(BF16)
            act = (silu * up).astype(BF16)
            dwd = _dot_tn(act, dft)
            dwg = _dot_tn(dgate, at)
            dwu = _dot_tn(dup, at)
            dat = _dot(dgate, wg_ref[...]) + _dot(dup, wu_ref[...])

            @pl.when(i == 0)
            def _():
                acc_d[...] = dwd
                acc_g[...] = dwg
                acc_u[...] = dwu

            @pl.when(i != 0)
            def _():
                acc_d[...] += dwd
                acc_g[...] += dwg
                acc_u[...] += dwu

            @pl.when(j == 0)
            def _():
                da_ref[rows, :] = dat

            @pl.when(j != 0)
            def _():
                da_ref[rows, :] += dat

            return carry

        lax.fori_loop(0, N_ROW_TILES, tile, 0)
        dwg_ref[...] = acc_g[...].astype(BF16)
        dwu_ref[...] = acc_u[...].astype(BF16)
        dwd_ref[...] = acc_d[...].astype(BF16)

    wtile = pl.BlockSpec((FF_TILE, D_MODEL), lambda j: (j, 0))
    hid = pl.BlockSpec((SEQ, FF_TILE), lambda j: (0, j))
    full = pl.BlockSpec((SEQ, D_MODEL), lambda j: (0, 0))
    return pl.pallas_call(
        body, grid=(D_FF // FF_TILE,), in_specs=[full, full, hid, hid, wtile, wtile, wtile],
        out_specs=[full, wtile, wtile, wtile],
        out_shape=[SDS((SEQ, D_MODEL), F32)] + [SDS((D_FF, D_MODEL), BF16)] * 3,
        scratch_shapes=[pltpu.VMEM((FF_TILE, D_MODEL), F32)] * 3,
        name=name, compiler_params=_params(("arbitrary",)))(df, a, gate, up, wg, wu, wd)


def _proj_fwd(a, w):
    def body(a_ref, w_ref, o_ref):
        def tile(i, carry):
            rows = _rows(i)
            o_ref[rows, :] = _dot(a_ref[rows, :], w_ref[...])
            return carry

        lax.fori_loop(0, N_ROW_TILES, tile, 0)

    return pl.pallas_call(
        body, grid=(N_DEV,),
        in_specs=[pl.BlockSpec((SEQ, D_MODEL), lambda j: (0, 0)), pl.BlockSpec((None, D_MODEL, IN_SHARD), lambda j: (j, 0, 0))],
        out_specs=pl.BlockSpec((None, SEQ, IN_SHARD), lambda j: (j, 0, 0)),
        out_shape=SDS((N_DEV, SEQ, IN_SHARD), F32), name="proj_fwd",
        compiler_params=_params(("parallel",)))(a, w)


def _proj_bwd(dproj, a, w):
    def body(dp_ref, a_ref, w_ref, da_ref, dw_ref, acc):
        j = pl.program_id(0)

        def tile(i, carry):
            rows = _rows(i)
            dpt = dp_ref[rows, :]
            dw = _dot_tn(a_ref[rows, :], dpt)
            dat = _dot_nt(dpt, w_ref[...])

            @pl.when(i == 0)
            def _():
                acc[...] = dw

            @pl.when(i != 0)
            def _():
                acc[...] += dw

            @pl.when(j == 0)
            def _():
                da_ref[rows, :] = dat

            @pl.when(j != 0)
            def _():
                da_ref[rows, :] += dat

            return carry

        lax.fori_loop(0, N_ROW_TILES, tile, 0)
        dw_ref[...] = acc[...].astype(BF16)

    full = pl.BlockSpec((SEQ, D_MODEL), lambda j: (0, 0))
    wspec = pl.BlockSpec((None, D_MODEL, IN_SHARD), lambda j: (j, 0, 0))
    return pl.pallas_call(
        body, grid=(N_DEV,),
        in_specs=[pl.BlockSpec((None, SEQ, IN_SHARD), lambda j: (j, 0, 0)), full, wspec],
        out_specs=[full, wspec],
        out_shape=[SDS((SEQ, D_MODEL), F32), SDS((N_DEV, D_MODEL, IN_SHARD), BF16)],
        scratch_shapes=[pltpu.VMEM((D_MODEL, IN_SHARD), F32)],
        name="proj_bwd", compiler_params=_params(("arbitrary",)))(dproj, a, w)


def _na_consts():
    c = np.arange(GRID_W)
    col_start = np.clip(c - KW // 2, 0, GRID_W - KW)
    col_in = (c[None, :] >= col_start[:, None]) & (c[None, :] < col_start[:, None] + KW)
    dc = np.clip(c[None, :] - c[:, None] + KW - 1, 0, 2 * KW - 2)
    onehot = np.zeros((128, GRID_W * GRID_W), np.float32)
    qq, kk = np.meshgrid(c, c, indexing="ij")
    onehot[dc[col_in], (qq * GRID_W + kk)[col_in]] = 1.0
    negmask = np.where(col_in, 0.0, NEG_INF).astype(np.float32).reshape(1, -1)
    return onehot, negmask


def _na_pair(block_type, a, b):
    if block_type == 0:
        return b - a + KH - 1 if b < KH else None
    if block_type == 1:
        return b - a + KH // 2 - 1 if a <= b < a + KH else None
    return b - a if b >= NA_KR - KH else None


def _rpb_expand(rpb):
    onehot, negmask = _na_consts()
    rows = HEADS * (2 * KH - 1)
    rpb_pad = jnp.pad(rpb.reshape(rows, 2 * KW - 1), ((0, 128 - rows), (0, 128 - (2 * KW - 1))))

    def body(r_ref, oh_ref, m_ref, t_ref):
        hi, mid, lo = _split3(r_ref[...])
        oh = oh_ref[...]
        t_ref[...] = _dot(hi, oh) + _dot(mid, oh) + _dot(lo, oh) + m_ref[...]

    table = pl.pallas_call(body, out_shape=SDS((128, GRID_W * GRID_W), F32), name="rpb_expand",
                           compiler_params=_params())(rpb_pad, jnp.asarray(onehot, BF16), jnp.asarray(negmask))
    return table[:rows].reshape(HEADS, 2 * KH - 1, GRID_W, GRID_W)


def _rpb_reduce(dslabs):
    onehot, _ = _na_consts()
    rows = HEADS * (2 * KH - 1)

    def body(x_ref, oht_ref, o_ref):
        hi, mid, lo = _split3(x_ref[...])
        oht = oht_ref[...]
        o_ref[...] = _dot(hi, oht) + _dot(mid, oht) + _dot(lo, oht)

    out = pl.pallas_call(body, out_shape=SDS((rows, 128), F32), name="rpb_reduce", compiler_params=_params())(
        dslabs.reshape(rows, GRID_W * GRID_W), jnp.asarray(onehot.T, BF16))
    return out.reshape(HEADS, 2 * KH - 1, 128)


def _bias_tiles(slab_ref, tile_ref):
    tile_ref[...] = jnp.full(tile_ref.shape, NEG_INF, F32)
    for t in range(NA_TYPES):
        for a in range(NA_RB):
            for b in range(NA_KR):
                dr = _na_pair(t, a, b)
                if dr is not None:
                    tile_ref[t, a * GRID_W:(a + 1) * GRID_W, b * GRID_W:(b + 1) * GRID_W] = slab_ref[dr]


def _bias_tiles_bwd(dtile_ref, dslab_ref):
    acc = {}
    for t in range(NA_TYPES):
        for a in range(NA_RB):
            for b in range(NA_KR):
                dr = _na_pair(t, a, b)
                if dr is not None:
                    part = dtile_ref[t, a * GRID_W:(a + 1) * GRID_W, b * GRID_W:(b + 1) * GRID_W]
                    acc[dr] = part if dr not in acc else acc[dr] + part
    for dr in range(2 * KH - 1):
        dslab_ref[dr] = acc[dr]


def _block_geometry(g):
    start = jnp.clip(g * NA_RB - KH // 2, 0, GRID_ROWS - NA_KR)
    block_type = jnp.where(g == 0, 0, jnp.where(g == NA_BLOCKS - 1, 2, 1))
    q0 = pl.multiple_of(N_META + g * NA_QB, 16)
    k0 = pl.multiple_of(N_META + start * GRID_W, 16)
    return block_type, q0, k0


def _na_probs(q, kk, km, bias):
    s = _dot_nt(q, kk) * ATT_SCALE + bias
    sm = _dot_nt(q, km) * ATT_SCALE
    m = jnp.maximum(jnp.max(s, axis=-1, keepdims=True), jnp.max(sm, axis=-1, keepdims=True))
    p = jnp.exp(s - m)
    pm = jnp.exp(sm - m)
    inv = 1.0 / (jnp.sum(p, axis=-1, keepdims=True) + jnp.sum(pm, axis=-1, keepdims=True))
    return p * inv, pm * inv


def _meta_probs(qm, km):
    s = _dot_nt(qm, km) * ATT_SCALE
    p = jnp.exp(s - jnp.max(s, axis=-1, keepdims=True))
    return p / jnp.sum(p, axis=-1, keepdims=True)


def _na_fwd(q, k, v, bias):
    def body(q_ref, k_ref, v_ref, slab_ref, o_ref, b_ref):
        _bias_tiles(slab_ref, b_ref)
        km = k_ref[0:N_META, :].astype(BF16)
        vm = v_ref[0:N_META, :].astype(BF16)
        pmm = _meta_probs(q_ref[0:N_META, :].astype(BF16), km)
        o_ref[0:N_META, :] = _dot(pmm.astype(BF16), vm)

        def block(g, carry):
            block_type, q0, k0 = _block_geometry(g)
            qb = q_ref[pl.ds(q0, NA_QB), :].astype(BF16)
            kk = k_ref[pl.ds(k0, NA_KB), :].astype(BF16)
            vv = v_ref[pl.ds(k0, NA_KB), :].astype(BF16)
            p, pm = _na_probs(qb, kk, km, b_ref[block_type])
            o_ref[pl.ds(q0, NA_QB), :] = _dot(p.astype(BF16), vv) + _dot(pm.astype(BF16), vm)
            return carry

        lax.fori_loop(0, NA_BLOCKS, block, 0)

    head = pl.BlockSpec((None, SEQ, HEAD_DIM), lambda h: (h, 0, 0))
    return pl.pallas_call(
        body, grid=(HEADS,), in_specs=[head, head, head, pl.BlockSpec((None, 2 * KH - 1, GRID_W, GRID_W), lambda h: (h, 0, 0, 0))],
        out_specs=head, out_shape=SDS((HEADS, SEQ, HEAD_DIM), F32), name="na_fwd",
        scratch_shapes=[pltpu.VMEM((NA_TYPES, NA_QB, NA_KB), F32)],
        compiler_params=_params(("parallel",)))(q, k, v, bias)


def _na_bwd(q, k, v, bias, do):
    def body(q_ref, k_ref, v_ref, slab_ref, do_ref, dq_ref, dk_ref, dv_ref, dslab_ref, b_ref, db_ref):
        _bias_tiles(slab_ref, b_ref)
        km = k_ref[0:N_META, :].astype(BF16)
        vm = v_ref[0:N_META, :].astype(BF16)
        dk_ref[...] = jnp.zeros_like(dk_ref)
        dv_ref[...] = jnp.zeros_like(dv_ref)
        db_ref[...] = jnp.zeros_like(db_ref)

        qm = q_ref[0:N_META, :].astype(BF16)
        dom = do_ref[0:N_META, :].astype(BF16)
        pmm = _meta_probs(qm, km)
        dpm = _dot_nt(dom, vm)
        dsm = (pmm * (dpm - jnp.sum(pmm * dpm, axis=-1, keepdims=True)) * ATT_SCALE).astype(BF16)
        dq_ref[0:N_META, :] = _dot(dsm, km)
        dkm0 = _dot_tn(dsm, qm)
        dvm0 = _dot_tn(pmm.astype(BF16), dom)

        def block(g, carry):
            dkm, dvm = carry
            block_type, q0, k0 = _block_geometry(g)
            qb = q_ref[pl.ds(q0, NA_QB), :].astype(BF16)
            kk = k_ref[pl.ds(k0, NA_KB), :].astype(BF16)
            vv = v_ref[pl.ds(k0, NA_KB), :].astype(BF16)
            dob = do_ref[pl.ds(q0, NA_QB), :].astype(BF16)
            p, pm = _na_probs(qb, kk, km, b_ref[block_type])
            dp = _dot_nt(dob, vv)
            dpm_ = _dot_nt(dob, vm)
            delta = jnp.sum(p * dp, axis=-1, keepdims=True) + jnp.sum(pm * dpm_, axis=-1, keepdims=True)
            ds = p * (dp - delta)
            dsm_ = pm * (dpm_ - delta)
            db_ref[block_type] += ds
            dsb = (ds * ATT_SCALE).astype(BF16)
            dsmb = (dsm_ * ATT_SCALE).astype(BF16)
            dq_ref[pl.ds(q0, NA_QB), :] = _dot(dsb, kk) + _dot(dsmb, km)
            dk_ref[pl.ds(k0, NA_KB), :] += _dot_tn(dsb, qb)
            dv_ref[pl.ds(k0, NA_KB), :] += _dot_tn(p.astype(BF16), dob)
            return dkm + _dot_tn(dsmb, qb), dvm + _dot_tn(pm.astype(BF16), dob)

        dkm, dvm = lax.fori_loop(0, NA_BLOCKS, block, (dkm0, dvm0))
        dk_ref[0:N_META, :] = dkm
        dv_ref[0:N_META, :] = dvm
        _bias_tiles_bwd(db_ref, dslab_ref)

    head = pl.BlockSpec((None, SEQ, HEAD_DIM), lambda h: (h, 0, 0))
    bspec = pl.BlockSpec((None, 2 * KH - 1, GRID_W, GRID_W), lambda h: (h, 0, 0, 0))
    return pl.pallas_call(
        body, grid=(HEADS,), in_specs=[head, head, head, bspec, head], out_specs=[head, head, head, bspec],
        out_shape=[SDS((HEADS, SEQ, HEAD_DIM), F32)] * 3 + [SDS((HEADS, 2 * KH - 1, GRID_W, GRID_W), F32)],
        scratch_shapes=[pltpu.VMEM((NA_TYPES, NA_QB, NA_KB), F32), pltpu.VMEM((NA_TYPES, NA_QB, NA_KB), F32)],
        name="na_bwd", compiler_params=_params(("parallel",)))(q, k, v, bias, do)


def _cmul(ar, ai, br, bi):
    return ar * br - ai * bi, ar * bi + ai * br


def _cpow(ar, ai, n):
    rr, ri = None, None
    br, bi = ar, ai
    while n:
        if n & 1:
            rr, ri = (br, bi) if rr is None else _cmul(rr, ri, br, bi)
        n >>= 1
        if n:
            br, bi = _cmul(br, bi, br, bi)
    return rr, ri


def _s5_prep(lr, li, logdt, bre, bim):
    def body(lr_ref, li_ref, dt_ref, br_ref, bi_ref, lbr_ref, lbi_ref, bbr_ref, bbi_ref):
        lr_, li_ = lr_ref[...], li_ref[...]
        dt = jnp.exp(dt_ref[...])
        mag = jnp.exp(lr_ * dt)
        lbr = mag * jnp.cos(li_ * dt)
        lbi = mag * jnp.sin(li_ * dt)
        lbr_ref[...] = lbr
        lbi_ref[...] = lbi
        den = lr_ * lr_ + li_ * li_
        xr = lbr - 1.0
        cr = (xr * lr_ + lbi * li_) / den
        ci = (lbi * lr_ - xr * li_) / den
        br, bi = br_ref[...], bi_ref[...]
        bbr_ref[...] = cr[:, None, :] * br - ci[:, None, :] * bi
        bbi_ref[...] = cr[:, None, :] * bi + ci[:, None, :] * br

    n = 2 * S5_GROUPS
    return pl.pallas_call(
        body, out_shape=[SDS((n, S5_STATE), F32)] * 2 + [SDS((n, S5_GROUP, S5_STATE), F32)] * 2,
        name="s5_prep", compiler_params=_params())(lr, li, logdt, bre, bim)


def _s5_prep_bwd(lr, li, logdt, bre, bim, dar, dai, dbbr, dbbi):
    def body(lr_ref, li_ref, dt_ref, br_ref, bi_ref, dar_ref, dai_ref, dbr_ref, dbi_ref,
             glr_ref, gli_ref, gdt_ref, gbr_ref, gbi_ref):
        lr_, li_ = lr_ref[...], li_ref[...]
        dt = jnp.exp(dt_ref[...])
        mag = jnp.exp(lr_ * dt)
        lbr = mag * jnp.cos(li_ * dt)
        lbi = mag * jnp.sin(li_ * dt)
        den = lr_ * lr_ + li_ * li_
        xr = lbr - 1.0
        cr = (xr * lr_ + lbi * li_) / den
        ci = (lbi * lr_ - xr * li_) / den
        br, bi = br_ref[...], bi_ref[...]
        dbr, dbi = dbr_ref[...], dbi_ref[...]
        gbr_ref[...] = cr[:, None, :] * dbr + ci[:, None, :] * dbi
        gbi_ref[...] = cr[:, None, :] * dbi - ci[:, None, :] * dbr
        gcr = jnp.sum(dbr * br + dbi * bi, axis=1)
        gci = jnp.sum(dbi * br - dbr * bi, axis=1)
        ilr, ili = lr_ / den, li_ / den
        tr, ti = _cmul(gcr, gci, ilr, ili)
        glbr = dar_ref[...] + tr
        glbi = dai_ref[...] + ti
        dr_, di_ = _cmul(tr, ti, cr, -ci)
        gwr, gwi = _cmul(glbr, glbi, lbr, -lbi)
        glr_ref[...] = gwr * dt - dr_
        gli_ref[...] = gwi * dt - di_
        gdt_ref[...] = jnp.sum(gwr * lr_ + gwi * li_, axis=-1, keepdims=True) * dt

    n = 2 * S5_GROUPS
    return pl.pallas_call(
        body, out_shape=[SDS((n, S5_STATE), F32)] * 2 + [SDS((n, 1), F32)] + [SDS((n, S5_GROUP, S5_STATE), F32)] * 2,
        name="s5_prep_bwd", compiler_params=_params())(lr, li, logdt, bre, bim, dar, dai, dbbr, dbbi)


def _scan_local(xr_ref, xi_ref, ar8, ai8, reverse):
    def step(i, carry):
        sr, si = carry
        idx = (SCAN_T - 1 - i) if reverse else i
        rows = pl.ds(pl.multiple_of(idx * SCAN_BLOCKS, SCAN_BLOCKS), SCAN_BLOCKS)
        nr = ar8 * sr - ai8 * si + xr_ref[rows, :]
        ni = ar8 * si + ai8 * sr + xi_ref[rows, :]
        xr_ref[rows, :] = nr
        xi_ref[rows, :] = ni
        return nr, ni

    z = jnp.zeros(ar8.shape, F32)
    return lax.fori_loop(0, SCAN_T, step, (z, z))


def _scan_carries(er, ei, atr, ati, reverse):
    row = lax.broadcasted_iota(jnp.int32, er.shape, 0)
    cr = jnp.zeros((1, er.shape[1]), F32)
    ci = cr
    outr = jnp.zeros(er.shape, F32)
    outi = outr
    order = range(SCAN_BLOCKS - 1, -1, -1) if reverse else range(SCAN_BLOCKS)
    for b in order:
        outr = jnp.where(row == b, cr, outr)
        outi = jnp.where(row == b, ci, outi)
        nr, ni = _cmul(atr, ati, cr, ci)
        cr, ci = nr + er[b:b + 1, :], ni + ei[b:b + 1, :]
    return outr, outi


def _scan_fixup(xr_ref, xi_ref, cr8, ci8, ar8, ai8, reverse):
    def step(i, carry):
        pr, pi = carry
        idx = (SCAN_T - 1 - i) if reverse else i
        rows = pl.ds(pl.multiple_of(idx * SCAN_BLOCKS, SCAN_BLOCKS), SCAN_BLOCKS)
        fr, fi = _cmul(pr, pi, cr8, ci8)
        xr_ref[rows, :] += fr
        xi_ref[rows, :] += fi
        return _cmul(pr, pi, ar8, ai8)

    lax.fori_loop(0, SCAN_T, step, (ar8, ai8), unroll=2)


def _scan(xr_ref, xi_ref, ar, ai, reverse):
    n = ar.shape[1]
    ar8 = jnp.broadcast_to(ar, (SCAN_BLOCKS, n))
    ai8 = jnp.broadcast_to(ai, (SCAN_BLOCKS, n))
    er, ei = _scan_local(xr_ref, xi_ref, ar8, ai8, reverse)
    atr, ati = _cpow(ar, ai, SCAN_T)
    cr8, ci8 = _scan_carries(er, ei, atr, ati, reverse)
    _scan_fixup(xr_ref, xi_ref, cr8, ci8, ar8, ai8, reverse)


def _s5_specs():
    chan = pl.BlockSpec((SEQ, CH_W), lambda c, d: (0, c))
    chan2 = pl.BlockSpec((None, SEQ, CH_W), lambda c, d: (d, 0, c))
    state = pl.BlockSpec((None, SEQ, ST_W), lambda c, d: (d, 0, c))
    bmat = pl.BlockSpec((None, None, CH_W, ST_W), lambda c, d: (d, c, 0, 0))
    cmat = pl.BlockSpec((None, None, ST_W, CH_W), lambda c, d: (d, c, 0, 0))
    avec = pl.BlockSpec((None, None, 1, ST_W), lambda c, d: (d, c, 0, 0))
    return chan, chan2, state, bmat, cmat, avec


def _scan_by_direction(xr_ref, xi_ref, ar, ai, d, adjoint):
    @pl.when(d == 0)
    def _():
        _scan(xr_ref, xi_ref, ar, ai, reverse=adjoint)

    @pl.when(d == 1)
    def _():
        _scan(xr_ref, xi_ref, ar, ai, reverse=not adjoint)


def _s5_scan_fwd(u, bre, bim, are, aim, cre, cim):
    def body(u_ref, bre_ref, bim_ref, are_ref, aim_ref, cre_ref, cim_ref, sr_ref, si_ref, y_ref):
        ub = u_ref[...].astype(BF16)
        sr_ref[...] = _dot(ub, bre_ref[...])
        si_ref[...] = _dot(ub, bim_ref[...])
        _scan_by_direction(sr_ref, si_ref, are_ref[...], aim_ref[...], pl.program_id(1), adjoint=False)
        y_ref[...] = _dot(sr_ref[...].astype(BF16), cre_ref[...]) - _dot(si_ref[...].astype(BF16), cim_ref[...])

    chan, chan2, state, bmat, cmat, avec = _s5_specs()
    return pl.pallas_call(
        body, grid=(S5_CHUNKS, 2), in_specs=[chan, bmat, bmat, avec, avec, cmat, cmat], out_specs=[state, state, chan2],
        out_shape=[SDS((2, SEQ, S5_GROUPS * S5_STATE), F32)] * 2 + [SDS((2, SEQ, S5_WIDTH), F32)],
        name="s5_scan_fwd", compiler_params=_params(("parallel", "parallel")))(u, bre, bim, are, aim, cre, cim)


def _dlam(gr_ref, gi_ref, sr_ref, si_ref, reverse):
    tile = lambda i: pl.ds(pl.multiple_of(i * SCAN_BLOCKS, SCAN_BLOCKS), SCAN_BLOCKS)
    row = lax.broadcasted_iota(jnp.int32, (SCAN_BLOCKS, ST_W), 0)
    if reverse:
        edge, src, shift, empty, lo, hi, dprev = SCAN_T - 1, 0, SCAN_BLOCKS - 1, SCAN_BLOCKS - 1, 0, SCAN_T - 1, 1
    else:
        edge, src, shift, empty, lo, hi, dprev = 0, SCAN_T - 1, 1, 0, 1, SCAN_T, -1
    spr = jnp.where(row == empty, 0.0, pltpu.roll(sr_ref[tile(src), :], shift, 0))
    spi = jnp.where(row == empty, 0.0, pltpu.roll(si_ref[tile(src), :], shift, 0))
    acc0 = _cmul(gr_ref[tile(edge), :], gi_ref[tile(edge), :], spr, -spi)

    def step(i, carry):
        accr, acci = carry
        pr, pi = _cmul(gr_ref[tile(i), :], gi_ref[tile(i), :], sr_ref[tile(i + dprev), :], -si_ref[tile(i + dprev), :])
        return accr + pr, acci + pi

    accr, acci = lax.fori_loop(lo, hi, step, acc0)
    return jnp.sum(accr, axis=0, keepdims=True), jnp.sum(acci, axis=0, keepdims=True)


def _s5_scan_bwd(dy, du_skip, u, sr, si, bre, bim, are, aim, cre, cim):
    def body(dy_ref, dus_ref, u_ref, sr_ref, si_ref, bre_ref, bim_ref, are_ref, aim_ref, cre_ref, cim_ref,
             du_ref, dbr_ref, dbi_ref, dcr_ref, dci_ref, dar_ref, dai_ref, gr_ref, gi_ref):
        d = pl.program_id(1)
        dyb = dy_ref[...].astype(BF16)
        gr_ref[...] = _dot_nt(dyb, cre_ref[...])
        gi_ref[...] = -_dot_nt(dyb, cim_ref[...])
        dcr_ref[...] = _dot_tn(sr_ref[...].astype(BF16), dyb)
        dci_ref[...] = -_dot_tn(si_ref[...].astype(BF16), dyb)
        _scan_by_direction(gr_ref, gi_ref, are_ref[...], -aim_ref[...], d, adjoint=True)

        @pl.when(d == 0)
        def _():
            dar_ref[...], dai_ref[...] = _dlam(gr_ref, gi_ref, sr_ref, si_ref, reverse=False)
            du_ref[...] = dus_ref[...]

        @pl.when(d == 1)
        def _():
            dar_ref[...], dai_ref[...] = _dlam(gr_ref, gi_ref, sr_ref, si_ref, reverse=True)

        grb = gr_ref[...].astype(BF16)
        gib = gi_ref[...].astype(BF16)
        du_ref[...] += _dot_nt(grb, bre_ref[...]) + _dot_nt(gib, bim_ref[...])
        ub = u_ref[...].astype(BF16)
        dbr_ref[...] = _dot_tn(ub, grb)
        dbi_ref[...] = _dot_tn(ub, gib)

    chan, _, state, bmat, cmat, avec = _s5_specs()
    return pl.pallas_call(
        body, grid=(S5_CHUNKS, 2), in_specs=[chan, chan, chan, state, state, bmat, bmat, avec, avec, cmat, cmat],
        out_specs=[chan, bmat, bmat, cmat, cmat, avec, avec],
        out_shape=[SDS((SEQ, S5_WIDTH), F32)] + [SDS((2, S5_CHUNKS, CH_W, ST_W), F32)] * 2
                  + [SDS((2, S5_CHUNKS, ST_W, CH_W), F32)] * 2 + [SDS((2, S5_CHUNKS, 1, ST_W), F32)] * 2,
        scratch_shapes=[pltpu.VMEM((SEQ, ST_W), F32), pltpu.VMEM((SEQ, ST_W), F32)],
        name="s5_scan_bwd", compiler_params=_params(("parallel", "arbitrary")))(dy, du_skip, u, sr, si, bre, bim, are, aim, cre, cim)


_GELU_K = math.sqrt(2.0 / math.pi)
_GELU_C = 0.044715


def _gelu(x):
    t = jnp.tanh(_GELU_K * (x + _GELU_C * x * x * x))
    return 0.5 * x * (1.0 + t), t


def _s5_glu_fwd(u, y2, dskip, wglu, bglu):
    def body(u_ref, y0_ref, y1_ref, d_ref, w_ref, b_ref, o_ref, yp_ref):
        ypre = u_ref[...] * d_ref[...] + y0_ref[...] + y1_ref[...]
        yp_ref[...] = ypre
        y, _ = _gelu(ypre)
        z = _dot(y.astype(BF16), w_ref[...]) + b_ref[...]
        o_ref[...] = y * jax.nn.sigmoid(z)

    row = _row_spec(S5_WIDTH)
    vec = _fix_spec((1, S5_WIDTH))
    dir0 = pl.BlockSpec((None, ROW_TILE, S5_WIDTH), lambda i: (0, i, 0))
    dir1 = pl.BlockSpec((None, ROW_TILE, S5_WIDTH), lambda i: (1, i, 0))
    return pl.pallas_call(
        body, grid=(N_ROW_TILES,), in_specs=[row, dir0, dir1, vec, _fix_spec((S5_WIDTH, S5_WIDTH)), vec],
        out_specs=[row, row], out_shape=[SDS((SEQ, S5_WIDTH), F32)] * 2, name="s5_glu_fwd",
        compiler_params=_params(("parallel",)))(u, y2, y2, dskip, wglu, bglu)


def _s5_glu_bwd(do, ypre, u, dskip, wglu, bglu):
    def body(do_ref, yp_ref, u_ref, d_ref, w_ref, b_ref, dyp_ref, du_ref, dw_ref, db_ref, dd_ref):
        i = pl.program_id(0)
        ypre = yp_ref[...]
        y, t = _gelu(ypre)
        yb = y.astype(BF16)
        sg = jax.nn.sigmoid(_dot(yb, w_ref[...]) + b_ref[...])
        dov = do_ref[...]
        dz = dov * y * sg * (1.0 - sg)
        dzb = dz.astype(BF16)
        dy = dov * sg + _dot_nt(dzb, w_ref[...])
        dgelu = 0.5 * (1.0 + t) + 0.5 * ypre * (1.0 - t * t) * _GELU_K * (1.0 + 3.0 * _GELU_C * ypre * ypre)
        dyp = dy * dgelu
        dyp_ref[...] = dyp
        uv = u_ref[...]
        du_ref[...] = dyp * d_ref[...]

        @pl.when(i == 0)
        def _():
            dw_ref[...] = jnp.zeros_like(dw_ref)
            db_ref[...] = jnp.zeros_like(db_ref)
            dd_ref[...] = jnp.zeros_like(dd_ref)

        dw_ref[...] += _dot_tn(yb, dzb)
        db_ref[...] += jnp.sum(dz, axis=0, keepdims=True)
        dd_ref[...] += jnp.sum(dyp * uv, axis=0, keepdims=True)

    row = _row_spec(S5_WIDTH)
    vec = _fix_spec((1, S5_WIDTH))
    mat = _fix_spec((S5_WIDTH, S5_WIDTH))
    return pl.pallas_call(
        body, grid=(N_ROW_TILES,), in_specs=[row, row, row, vec, mat, vec], out_specs=[row, row, mat, vec, vec],
        out_shape=[SDS((SEQ, S5_WIDTH), F32)] * 2 + [SDS((S5_WIDTH, S5_WIDTH), F32), SDS((1, S5_WIDTH), F32), SDS((1, S5_WIDTH), F32)],
        name="s5_glu_bwd", compiler_params=_params(("arbitrary",)))(do, ypre, u, dskip, wglu, bglu)


def _mix_out_fwd(ona, os5, g_na, g_s5, wout):
    def body(a_ref, s_ref, ga_ref, gs_ref, w_ref, o_ref):
        av, sv = a_ref[...], s_ref[...]
        ca = (av * _rstd(av) * ga_ref[...]).astype(BF16)
        cs = (sv * _rstd(sv) * gs_ref[...]).astype(BF16)
        o_ref[...] = _dot(ca, w_ref[0:NA_WIDTH, :]) + _dot(cs, w_ref[NA_WIDTH:, :])

    row = _row_spec(NA_WIDTH)
    vec = _fix_spec((1, NA_WIDTH))
    return pl.pallas_call(
        body, grid=(N_ROW_TILES,), in_specs=[row, row, vec, vec, _fix_spec((D_MODEL, D_MODEL))],
        out_specs=_row_spec(D_MODEL), out_shape=SDS((SEQ, D_MODEL), F32), name="mix_out_fwd",
        compiler_params=_params(("parallel",)))(ona, os5, g_na, g_s5, wout)


def _mix_out_bwd(dmix, ona, os5, g_na, g_s5, wout):
    def body(dm_ref, a_ref, s_ref, ga_ref, gs_ref, w_ref, da_ref, ds_ref, dw_ref, dga_ref, dgs_ref):
        i = pl.program_id(0)
        dm = dm_ref[...]
        av, sv = a_ref[...], s_ref[...]
        ra, rs = _rstd(av), _rstd(sv)
        ga, gs = ga_ref[...], gs_ref[...]
        ca = (av * ra * ga).astype(BF16)
        cs = (sv * rs * gs).astype(BF16)
        dca = _dot_nt(dm, w_ref[0:NA_WIDTH, :])
        dcs = _dot_nt(dm, w_ref[NA_WIDTH:, :])
        da, dga = _rms_bwd(av, ra, ga, dca)
        ds, dgs = _rms_bwd(sv, rs, gs, dcs)
        da_ref[...] = da
        ds_ref[...] = ds

        @pl.when(i == 0)
        def _():
            dw_ref[...] = jnp.zeros_like(dw_ref)
            dga_ref[...] = jnp.zeros_like(dga_ref)
            dgs_ref[...] = jnp.zeros_like(dgs_ref)

        dw_ref[0:NA_WIDTH, :] += _dot_tn(ca, dm)
        dw_ref[NA_WIDTH:, :] += _dot_tn(cs, dm)
        dga_ref[...] += jnp.sum(dga, axis=0, keepdims=True)
        dgs_ref[...] += jnp.sum(dgs, axis=0, keepdims=True)

    row = _row_spec(NA_WIDTH)
    vec = _fix_spec((1, NA_WIDTH))
    mat = _fix_spec((D_MODEL, D_MODEL))
    return pl.pallas_call(
        body, grid=(N_ROW_TILES,), in_specs=[_row_spec(D_MODEL), row, row, vec, vec, mat],
        out_specs=[row, row, mat, vec, vec],
        out_shape=[SDS((SEQ, NA_WIDTH), F32)] * 2 + [SDS((D_MODEL, D_MODEL), F32), SDS((1, NA_WIDTH), F32), SDS((1, NA_WIDTH), F32)],
        name="mix_out_bwd", compiler_params=_params(("arbitrary",)))(dmix, ona, os5, g_na, g_s5, wout)


def _me():
    x, y, c = lax.axis_index("x"), lax.axis_index("y"), lax.axis_index("c")
    return x, y, c, 4 * x + 2 * y + c


def _peer(k):
    x, y, c, _ = _me()
    px = 1 - x if (k >> 2) & 1 else x
    py = 1 - y if (k >> 1) & 1 else y
    pc = 1 - c if k & 1 else c
    return (px, py, pc), 4 * px + 2 * py + pc


ALL_PEERS = (1, 2, 3, 4, 5, 6, 7)
CHIP_PEERS = (2, 4, 6)
SIBLING = 1


def _slot8(pos):
    return 4 * pos[0] + 2 * pos[1] + pos[2]


def _slot4(pos):
    return 2 * pos[0] + pos[1]


def _exchange(arrays, gather, name, after=()):
    n, n_after = len(arrays), len(after)

    def body(*refs):
        ins, outs = refs[:n], refs[n + n_after:2 * n + n_after]
        send_sems, recv_sems, local_sems = refs[2 * n + n_after:]
        _, _, _, me = _me()
        started = []
        for a in range(n):
            src_mine = ins[a] if gather else ins[a].at[me]
            local = pltpu.make_async_copy(src_mine, outs[a].at[me], local_sems.at[a])
            local.start()
            started.append(local)
        sends = []
        for k in range(1, N_DEV):
            peer, peer_idx = _peer(k)
            for a in range(n):
                src = ins[a] if gather else ins[a].at[peer_idx]
                cp = pltpu.make_async_remote_copy(src_ref=src, dst_ref=outs[a].at[me], send_sem=send_sems.at[a, k - 1],
                                                  recv_sem=recv_sems.at[a, k - 1], device_id=peer, device_id_type=MESH)
                cp.start()
                sends.append(cp)
        for k in range(1, N_DEV):
            peer, peer_idx = _peer(k)
            for a in range(n):
                src = ins[a] if gather else ins[a].at[peer_idx]
                pltpu.make_async_remote_copy(src_ref=src, dst_ref=outs[a].at[peer_idx], send_sem=send_sems.at[a, k - 1],
                                             recv_sem=recv_sems.at[a, k - 1], device_id=peer, device_id_type=MESH).wait_recv()
        for cp in sends:
            cp.wait_send()
        for local in started:
            local.wait()

    hbm = pl.BlockSpec(memory_space=pltpu.HBM)
    out_shape = [SDS((N_DEV,) + tuple(a.shape), a.dtype) if gather else SDS(a.shape, a.dtype) for a in arrays]
    return pl.pallas_call(
        body, in_specs=[hbm] * n + [pl.BlockSpec(memory_space=pl.ANY)] * n_after, out_specs=[hbm] * n, out_shape=out_shape,
        scratch_shapes=[pltpu.SemaphoreType.DMA((n, N_DEV - 1)), pltpu.SemaphoreType.DMA((n, N_DEV - 1)),
                        pltpu.SemaphoreType.DMA((n,))],
        name=name)(*arrays, *after)


_HBM = pl.BlockSpec(memory_space=pltpu.HBM)
_SEM = pl.BlockSpec(memory_space=pltpu.SEMAPHORE)
_EFFECT = pltpu.SideEffectType.DATAFLOW_SIDE_EFFECTING


def _land_shape(a, gather):
    return (N_DEV,) + tuple(a.shape) if gather else tuple(a.shape)


def _place_own(arrays, gather, name, slot=_slot8):
    n = len(arrays)
    me = slot(_me()[:3])

    def body(me_ref, *refs):
        for a in range(n):
            refs[n + a][...] = refs[a][...]

    def own_slot(a):
        zeros = (0,) * (a.ndim - (0 if gather else 1))
        return lambda i, me_ref: (me_ref[0],) + zeros

    def whole(a):
        return lambda i, me_ref: (0,) * a.ndim

    in_specs = [pl.BlockSpec(a.shape, whole(a)) if gather else pl.BlockSpec((None,) + a.shape[1:], own_slot(a)) for a in arrays]
    out_specs = [pl.BlockSpec((None,) + (a.shape if gather else a.shape[1:]), own_slot(a)) for a in arrays]
    return pl.pallas_call(
        body, grid_spec=pltpu.PrefetchScalarGridSpec(num_scalar_prefetch=1, grid=(1,), in_specs=in_specs, out_specs=out_specs),
        out_shape=[SDS(_land_shape(a, gather), a.dtype) for a in arrays], name=name,
        compiler_params=_params(("arbitrary",)))(me.reshape(1).astype(jnp.int32), *arrays)


def _exchange_start(arrays, lands, gather, name, peers=ALL_PEERS, slot=_slot8):
    n = len(arrays)

    def body(*refs):
        ins, lnd = refs[:n], refs[n:2 * n]
        send_sems, recv_sems = refs[2 * n], refs[2 * n + 1]
        token = refs[-1]
        me = slot(_me()[:3])
        for i, k in enumerate(peers):
            peer, _ = _peer(k)
            for a in range(n):
                src = ins[a] if gather else ins[a].at[slot(peer)]
                s = a * len(peers) + i
                pltpu.make_async_remote_copy(src_ref=src, dst_ref=lnd[a].at[me], send_sem=send_sems.at[s],
                                             recv_sem=recv_sems.at[s], device_id=peer, device_id_type=MESH).start()
        token[...] = jnp.zeros_like(token)

    sems = pltpu.SemaphoreType.DMA((n * len(peers),))
    out = pl.pallas_call(
        body, name=name, in_specs=[_HBM] * (2 * n),
        out_shape=(sems, sems) + tuple(pltpu.HBM(a.shape, a.dtype) for a in list(arrays) + list(lands)) + (SDS((8, 128), F32),),
        out_specs=(_SEM, _SEM) + (_HBM,) * (2 * n) + (pl.BlockSpec(memory_space=pltpu.VMEM),),
        input_output_aliases={i: 2 + i for i in range(2 * n)},
        compiler_params=pltpu.CompilerParams(has_side_effects=_EFFECT),
    )(*[pltpu.with_memory_space_constraint(a, pltpu.HBM) for a in list(arrays) + list(lands)])
    return out[0], out[1], list(out[2:2 + n]), list(out[2 + n:2 + 2 * n]), out[-1]


def _exchange_wait(send_sems, recv_sems, arrays, lands, after, gather, name, peers=ALL_PEERS, slot=_slot8):
    n = len(arrays)

    def body(*refs):
        ins, lnd = refs[:n], refs[n:2 * n]
        send_sems, recv_sems = refs[2 * n], refs[2 * n + 1]
        for i, k in enumerate(peers):
            peer, _ = _peer(k)
            for a in range(n):
                src = ins[a] if gather else ins[a].at[slot(peer)]
                s = a * len(peers) + i
                cp = pltpu.make_async_remote_copy(src_ref=src, dst_ref=lnd[a].at[slot(peer)], send_sem=send_sems.at[s],
                                                  recv_sem=recv_sems.at[s], device_id=peer, device_id_type=MESH)
                cp.wait_send()
                cp.wait_recv()

        refs[-1][...] = jnp.zeros_like(refs[-1])

    after = list(after) if isinstance(after, (list, tuple)) else [after]
    out = pl.pallas_call(
        body, name=name, in_specs=[_HBM] * (2 * n) + [_SEM, _SEM] + [pl.BlockSpec(memory_space=pl.ANY)] * len(after),
        out_shape=tuple(pltpu.HBM(a.shape, a.dtype) for a in list(arrays) + list(lands)) + (SDS((8, 128), F32),),
        out_specs=(_HBM,) * (2 * n) + (pl.BlockSpec(memory_space=pltpu.VMEM),), input_output_aliases={i: i for i in range(2 * n)},
        compiler_params=pltpu.CompilerParams(has_side_effects=_EFFECT),
    )(*arrays, *lands, send_sems, recv_sems, *after)
    return list(out[n:2 * n]), out[-1]


def _forward_sibling(lands, name):
    n = len(lands)

    def body(*refs):
        outs = refs[n:2 * n]
        send_sems, recv_sems = refs[2 * n:]
        x, y, c, _ = _me()
        sends = []
        for i, k in enumerate(CHIP_PEERS):
            peer, _ = _peer(k)
            for a in range(n):
                rows = outs[a].at[_slot8(peer)]
                cp = pltpu.make_async_remote_copy(src_ref=rows, dst_ref=rows, send_sem=send_sems.at[a, i], recv_sem=recv_sems.at[a, i],
                                                  device_id=(x, y, 1 - c), device_id_type=MESH)
                cp.start()
                sends.append(cp)
        for i, k in enumerate(CHIP_PEERS):
            (px, py, pc), _ = _peer(k)
            for a in range(n):
                rows = outs[a].at[_slot8((px, py, 1 - pc))]
                pltpu.make_async_remote_copy(src_ref=rows, dst_ref=rows, send_sem=send_sems.at[a, i], recv_sem=recv_sems.at[a, i],
                                             device_id=(x, y, 1 - c), device_id_type=MESH).wait_recv()
        for cp in sends:
            cp.wait_send()

    return pl.pallas_call(
        body, in_specs=[_HBM] * n, out_specs=[_HBM] * n, out_shape=[SDS(a.shape, a.dtype) for a in lands],
        input_output_aliases={i: i for i in range(n)},
        scratch_shapes=[pltpu.SemaphoreType.DMA((n, len(CHIP_PEERS))), pltpu.SemaphoreType.DMA((n, len(CHIP_PEERS)))],
        name=name)(*lands)


def _swap_sibling(arrays, name):
    n = len(arrays)
    chips = N_DEV // 2

    def body(*refs):
        ins, outs = refs[:n], refs[n:2 * n]
        send_sems, recv_sems = refs[2 * n:]
        x, y, c, _ = _me()
        sends = []
        for q in range(chips):
            for a in range(n):
                cp = pltpu.make_async_remote_copy(src_ref=ins[a].at[q, 1 - c], dst_ref=outs[a].at[q], send_sem=send_sems.at[a, q],
                                                  recv_sem=recv_sems.at[a, q], device_id=(x, y, 1 - c), device_id_type=MESH)
                cp.start()
                sends.append(cp)
        for cp in sends:
            cp.wait_recv()
        for cp in sends:
            cp.wait_send()

    return pl.pallas_call(
        body, in_specs=[_HBM] * n, out_specs=[_HBM] * n, out_shape=[SDS((chips,) + a.shape[2:], a.dtype) for a in arrays],
        scratch_shapes=[pltpu.SemaphoreType.DMA((n, chips)), pltpu.SemaphoreType.DMA((n, chips))], name=name)(*arrays)


def _sum_pairs(mine, theirs, name):
    chips, _, rows, cols = mine.shape
    c = lax.axis_index("c")

    def body(c_ref, a_ref, b_ref, o_ref):
        o_ref[...] = (a_ref[...].astype(F32) + b_ref[...].astype(F32)).astype(o_ref.dtype)

    return pl.pallas_call(
        body, grid_spec=pltpu.PrefetchScalarGridSpec(
            num_scalar_prefetch=1, grid=(chips,),
            in_specs=[pl.BlockSpec((None, None, rows, cols), lambda q, c_ref: (q, c_ref[0], 0, 0)),
                      pl.BlockSpec((None, rows, cols), lambda q, c_ref: (q, 0, 0))],
            out_specs=pl.BlockSpec((None, rows, cols), lambda q, c_ref: (q, 0, 0))),
        out_shape=SDS((chips, rows, cols), mine.dtype), name=name,
        compiler_params=_params(("parallel",)))(c.reshape(1).astype(jnp.int32), mine, theirs)


def _adamw_math(w, g, m, v):
    m = ADAM_B1 * m + (1.0 - ADAM_B1) * g
    v = ADAM_B2 * v + (1.0 - ADAM_B2) * (g * g)
    m_hat = m / (1.0 - ADAM_B1 ** ADAM_STEP)
    v_hat = v / (1.0 - ADAM_B2 ** ADAM_STEP)
    delta = -ADAM_LR * (m_hat / (jnp.sqrt(v_hat) + ADAM_EPS) + ADAM_WD * w)
    return delta, m, v


def _adamw(w, m, v, pieces, name):
    rows, cols = w.shape[-2:]
    lead = w.ndim - 2
    tile = rows
    for cand in (256, 176, 128, 64, 16):
        if rows > cand and rows % cand == 0:
            tile = cand
            break

    def body(w_ref, m_ref, v_ref, p_ref, g_ref, d_ref, mo_ref, vo_ref):
        g = _sum_pieces(p_ref)
        g_ref[...] = g
        d_ref[...], mo_ref[...], vo_ref[...] = _adamw_math(w_ref[...], g, m_ref[...], v_ref[...])

    blk = pl.BlockSpec((None,) * lead + (tile, cols), lambda i: (0,) * lead + (i, 0))
    return pl.pallas_call(
        body, grid=(rows // tile,), in_specs=[blk, blk, blk, pl.BlockSpec((pieces.shape[0], tile, cols), lambda i: (0, i, 0))],
        out_specs=[blk] * 4, out_shape=[SDS(w.shape, F32)] * 4, name=name,
        compiler_params=_params(("parallel",)))(w, m, v, pieces)


def _sum_pieces(p_ref):
    g = p_ref[0].astype(F32)
    for p in range(1, p_ref.shape[0]):
        g = g + p_ref[p].astype(F32)
    return g


def _adamw_s5_mat(w, m, v, g, name):
    _, ndir, groups, b, c = w.shape
    per_dir = groups // 8

    def body(w_ref, m_ref, v_ref, g_ref, d_ref, mo_ref, vo_ref):
        d_ref[...], mo_ref[...], vo_ref[...] = _adamw_math(w_ref[...], g_ref[...], m_ref[...], v_ref[...])

    blk = pl.BlockSpec((None, None, 8, b, c), lambda i: (0, i // per_dir, i % per_dir, 0, 0))
    return pl.pallas_call(
        body, grid=(ndir * per_dir,), in_specs=[blk] * 4, out_specs=[blk] * 3, out_shape=[SDS(w.shape, F32)] * 3, name=name,
        compiler_params=_params(("parallel",)))(w, m, v, g)


VEC_ROWS = ['ffn1_pre_g', 'ffn1_post_g', 'mix_pre_g', 'mix_post_g', 'ffn2_pre_g', 'ffn2_post_g', 'final_g',
            ('na_out_g', 's5_out_g'), ('s5_d', 's5_b_glu')]
VEC_NAMES = [n for row in VEC_ROWS for n in ((row,) if isinstance(row, str) else row)]
VEC_PACK_ROWS = 16
LOSS_ROW = len(VEC_ROWS)


def _pack_vectors(grads, loss8):
    def body(*refs):
        o_ref = refs[-1]
        o_ref[...] = jnp.zeros_like(o_ref)
        o_ref[LOSS_ROW:LOSS_ROW + 1, 0:128] = refs[-2][0:1, :]
        k = 0
        for i, row in enumerate(VEC_ROWS):
            if isinstance(row, str):
                o_ref[i:i + 1, :] = refs[k][...]
                k += 1
            else:
                o_ref[i:i + 1, 0:NA_WIDTH] = refs[k][...]
                o_ref[i:i + 1, NA_WIDTH:] = refs[k + 1][...]
                k += 2

    return pl.pallas_call(body, out_shape=SDS((VEC_PACK_ROWS, D_MODEL), F32), name="pack_vectors",
                          compiler_params=_params())(*[grads[n] for n in VEC_NAMES], loss8)


def _sum8(pieces, name):
    def body(p_ref, o_ref):
        o_ref[...] = _sum_pieces(p_ref)

    return pl.pallas_call(body, out_shape=SDS(pieces.shape[1:], F32), name=name, compiler_params=_params())(pieces)


def _adamw_small(packed8, vec_wmv, others):
    n_vec, n_oth = len(VEC_NAMES), len(others)

    def body(*refs):
        p_ref = refs[0]
        ins = refs[1:1 + 3 * n_vec + 4 * n_oth]
        outs = refs[1 + 3 * n_vec + 4 * n_oth:]
        gsum = _sum_pieces(p_ref)
        outs[-1][...] = gsum[LOSS_ROW:LOSS_ROW + 1, 0:128]
        k = 0
        for i, row in enumerate(VEC_ROWS):
            parts = [(row, gsum[i:i + 1, :])] if isinstance(row, str) else \
                [(row[0], gsum[i:i + 1, 0:NA_WIDTH]), (row[1], gsum[i:i + 1, NA_WIDTH:])]
            for _, g in parts:
                w_ref, m_ref, v_ref = ins[3 * k:3 * k + 3]
                outs[4 * k][...] = g
                outs[4 * k + 1][...], outs[4 * k + 2][...], outs[4 * k + 3][...] = _adamw_math(w_ref[...], g, m_ref[...], v_ref[...])
                k += 1
        for j in range(n_oth):
            w_ref, m_ref, v_ref, g_ref = ins[3 * n_vec + 4 * j:3 * n_vec + 4 * j + 4]
            g = _sum_pieces(g_ref)
            g = g[tuple(slice(0, s) for s in w_ref.shape[1:])].reshape(w_ref.shape)
            o = outs[4 * (n_vec + j):4 * (n_vec + j) + 4]
            o[0][...] = g
            o[1][...], o[2][...], o[3][...] = _adamw_math(w_ref[...], g, m_ref[...], v_ref[...])

    args, out_shape = [packed8], []
    for w, m, v in vec_wmv:
        args += [w, m, v]
        out_shape += [SDS(w.shape, F32)] * 4
    for w, m, v, g in others:
        args += [w, m, v, g]
        out_shape += [SDS(w.shape, F32)] * 4
    out_shape += [SDS((1, 128), F32)]
    return pl.pallas_call(body, out_shape=out_shape, name="adamw_small", compiler_params=_params())(*args)


def _perm_rows(x):
    return x.reshape(SCAN_BLOCKS, SCAN_T, x.shape[-1]).transpose(1, 0, 2).reshape(SEQ, x.shape[-1])


def _unperm_rows(x):
    return x.reshape(SCAN_T, SCAN_BLOCKS, x.shape[-1]).transpose(1, 0, 2).reshape(SEQ, x.shape[-1])


def _block_diag(x):
    eye = np.eye(8, dtype=bool)[None, None, :, None, :, None]
    full = jnp.where(eye, x[:, :, :, :, None, :], 0.0)
    return full.reshape(2, S5_CHUNKS, 8 * x.shape[3], 8 * x.shape[4])


def _diag_blocks(x, r, c):
    x6 = x.reshape(2, S5_CHUNKS, 8, r, 8, c)
    return jnp.stack([x6[:, :, g, :, g, :] for g in range(8)], axis=2)


STORED_SWAPPED = {"ffn1_w_gate": (1, 2), "ffn1_w_up": (1, 2), "ffn2_w_gate": (1, 2), "ffn2_w_up": (1, 2),
                  "s5_b_re": (3, 4), "s5_b_im": (3, 4)}


def _stored(name, x):
    return jnp.swapaxes(x, *STORED_SWAPPED[name]) if name in STORED_SWAPPED else x


def _dep(x, token):
    return x if token is None else x + token


def _local_step(x, target, get_w, small, emit):
    bias = _rpb_expand(small["na_rpb"][0])
    lr = small["s5_lam_re"].reshape(64, S5_STATE)
    li = small["s5_lam_im"].reshape(64, S5_STATE)
    logdt = small["s5_log_dt"].reshape(64, 1)
    b_t = [_stored(n, small[n]).reshape(64, S5_GROUP, S5_STATE) for n in ("s5_b_re", "s5_b_im")]
    lbr, lbi, bbr, bbi = _s5_prep(lr, li, logdt, b_t[0], b_t[1])
    are = lbr.reshape(2, S5_CHUNKS, 1, ST_W)
    aim = lbi.reshape(2, S5_CHUNKS, 1, ST_W)
    bre = _block_diag(bbr.reshape(2, S5_CHUNKS, 8, S5_GROUP, S5_STATE)).astype(BF16)
    bim = _block_diag(bbi.reshape(2, S5_CHUNKS, 8, S5_GROUP, S5_STATE)).astype(BF16)
    c_t = [small[n].reshape(2, S5_CHUNKS, 8, S5_GROUP, S5_STATE).transpose(0, 1, 2, 4, 3) for n in ("s5_c_re", "s5_c_im")]
    cre = _block_diag(c_t[0]).astype(BF16)
    cim = _block_diag(c_t[1]).astype(BF16)
    tgt = jnp.concatenate([jnp.zeros((N_META, D_MODEL), F32), target], axis=0)

    h0 = jnp.concatenate([get_w("meta", None)["meta_tokens"], x], axis=0)
    a1 = _prenorm(h0, small["ffn1_pre_g"])
    wts = dict(get_w("ffn1", [bias, are, aim, bre, bim, cre, cim, tgt, a1]))
    gate1, up1, f1 = _ffn_fwd(a1, wts["ffn1_w_gate"], wts["ffn1_w_up"], wts["ffn1_w_down"], "ffn1_fwd",
                              after=wts.get("tokens", ()))
    h1, a2 = _post_pre(f1, h0, small["ffn1_post_g"], small["mix_pre_g"], 0.5, "post_pre1")
    wts.update(get_w("w_in", a2))
    proj = _proj_fwd(a2, wts["w_in"])
    qkv = proj[:6].reshape(3, 2, SEQ, 4, HEAD_DIM).transpose(0, 1, 3, 2, 4).reshape(3, HEADS, SEQ, HEAD_DIM)
    u = proj[6:].transpose(1, 0, 2).reshape(SEQ, S5_WIDTH)
    o3 = _na_fwd(qkv[0], qkv[1], qkv[2], bias)
    ona = o3.transpose(1, 0, 2).reshape(SEQ, NA_WIDTH)
    u_p = _perm_rows(u)
    sr, si, y2 = _s5_scan_fwd(u_p, bre, bim, are, aim, cre, cim)
    wts.update(get_w("mix", y2))
    os5_p, ypre_p = _s5_glu_fwd(u_p, y2, small["s5_d"], wts["s5_w_glu"], small["s5_b_glu"])
    os5 = _unperm_rows(os5_p)

    mix = _mix_out_fwd(ona, os5, small["na_out_g"], small["s5_out_g"], wts["w_out"])
    h2, a3 = _post_pre(mix, h1, small["mix_post_g"], small["ffn2_pre_g"], 1.0, "post_pre2")
    wts.update(get_w("ffn2", a3))
    gate2, up2, f2 = _ffn_fwd(a3, wts["ffn2_w_gate"], wts["ffn2_w_up"], wts["ffn2_w_down"], "ffn2_fwd")
    loss8, dh3, df2, g_final, g_ffn2_post = _final_loss(f2, h2, small["ffn2_post_g"], small["final_g"], tgt)

    da3, dwg2, dwu2, dwd2 = _ffn_bwd(df2, a3, gate2, up2, wts["ffn2_w_gate"], wts["ffn2_w_up"], wts["ffn2_w_down"], "ffn2_bwd")
    tok = emit("ffn2", {"ffn2_w_gate": dwg2, "ffn2_w_up": dwu2, "ffn2_w_down": dwd2})
    dh2, dmix, g_ffn2_pre, g_mix_post = _bwd_pre_post(da3, h2, _dep(small["ffn2_pre_g"], tok), dh3, mix, small["mix_post_g"], 1.0,
                                                      "bwd_pre_post2")
    dona, dos5, dwout, g_na_out, g_s5_out = _mix_out_bwd(dmix, ona, os5, small["na_out_g"], small["s5_out_g"], wts["w_out"])

    dypre_p, du_skip_p, dwglu, g_b_glu, g_s5_d = _s5_glu_bwd(_perm_rows(dos5), ypre_p, u_p, small["s5_d"], wts["s5_w_glu"],
                                                             small["s5_b_glu"])
    tok = emit("mix", {"s5_w_glu": dwglu.reshape(N_DEV, S5_WIDTH // N_DEV, S5_WIDTH).astype(BF16),
                       "w_out": dwout.reshape(N_DEV, D_MODEL // N_DEV, D_MODEL).astype(BF16)})
    du_p, dbr, dbi, dcr, dci, dar, dai = _s5_scan_bwd(dypre_p, du_skip_p, u_p, sr, si, bre, bim, _dep(are, tok), aim, cre, cim)
    du = _unperm_rows(du_p)
    dbbr = _diag_blocks(dbr, S5_GROUP, S5_STATE).reshape(64, S5_GROUP, S5_STATE)
    dbbi = _diag_blocks(dbi, S5_GROUP, S5_STATE).reshape(64, S5_GROUP, S5_STATE)
    g_lr, g_li, g_dt, g_br, g_bi = _s5_prep_bwd(lr, li, logdt, b_t[0], b_t[1], dar.reshape(64, S5_STATE),
                                                dai.reshape(64, S5_STATE), dbbr, dbbi)
    g_c = [_diag_blocks(d, S5_STATE, S5_GROUP).transpose(0, 1, 2, 4, 3).reshape(2 * S5_GROUPS, S5_GROUP, S5_STATE)
           for d in (dcr, dci)]

    do3 = dona.reshape(SEQ, HEADS, HEAD_DIM).transpose(1, 0, 2)
    dq, dk, dv, dbias = _na_bwd(qkv[0], qkv[1], qkv[2], bias, do3)
    g_rpb = _rpb_reduce(dbias)
    dense = jnp.stack([g.reshape(2 * S5_GROUPS, S5_STATE * S5_GROUP) for g in (g_br, g_bi, *g_c)])
    tok = emit("small", {"dense": dense, "na_rpb": g_rpb,
                         "s5_lam_re": g_lr.reshape(2, S5_GROUPS, S5_STATE), "s5_lam_im": g_li.reshape(2, S5_GROUPS, S5_STATE),
                         "s5_log_dt": g_dt.reshape(2, S5_GROUPS)})
    dqkv = jnp.stack([dq, dk, dv]).reshape(3, 2, 4, SEQ, HEAD_DIM).transpose(0, 1, 3, 2, 4).reshape(6, SEQ, IN_SHARD)
    dproj = jnp.concatenate([dqkv, du.reshape(SEQ, 2, IN_SHARD).transpose(1, 0, 2)], axis=0).astype(BF16)
    da2, dwin = _proj_bwd(dproj, a2, wts["w_in"])
    tok2 = emit("w_in", {"w_in": dwin})
    tok = tok if tok2 is None else tok + tok2
    dh1, df1, g_mix_pre, g_ffn1_post = _bwd_pre_post(da2, h1, _dep(small["mix_pre_g"], tok), dh2, f1, small["ffn1_post_g"], 0.5,
                                                     "bwd_pre_post1")
    da1, dwg1, dwu1, dwd1 = _ffn_bwd(df1, a1, gate1, up1, wts["ffn1_w_gate"], wts["ffn1_w_up"], wts["ffn1_w_down"], "ffn1_bwd")
    emit("ffn1", {"ffn1_w_gate": dwg1, "ffn1_w_up": dwu1, "ffn1_w_down": dwd1})
    dh0, g_ffn1_pre = _bwd_pre_only(da1, h0, small["ffn1_pre_g"], dh1)

    vec_g = {
        "ffn1_pre_g": g_ffn1_pre, "ffn1_post_g": g_ffn1_post, "mix_pre_g": g_mix_pre, "s5_d": g_s5_d, "s5_b_glu": g_b_glu,
        "na_out_g": g_na_out, "s5_out_g": g_s5_out, "mix_post_g": g_mix_post,
        "ffn2_pre_g": g_ffn2_pre, "ffn2_post_g": g_ffn2_post, "final_g": g_final,
    }
    return loss8, dh0[N_META:], dh0[:N_META], vec_g


WEIGHT_NAMES = ['meta_tokens', 'ffn1_pre_g', 'ffn1_post_g', 'ffn1_w_gate', 'ffn1_w_up', 'ffn1_w_down', 'mix_pre_g', 'w_in',
                'na_rpb', 's5_lam_re', 's5_lam_im', 's5_log_dt', 's5_b_re', 's5_b_im', 's5_c_re', 's5_c_im', 's5_d',
                's5_w_glu', 's5_b_glu', 'na_out_g', 's5_out_g', 'w_out', 'mix_post_g', 'ffn2_pre_g', 'ffn2_post_g',
                'ffn2_w_gate', 'ffn2_w_up', 'ffn2_w_down', 'final_g']
BIG_NAMES = ['ffn1_w_gate', 'ffn1_w_up', 'ffn1_w_down', 'w_in', 's5_w_glu', 'w_out', 'ffn2_w_gate', 'ffn2_w_up', 'ffn2_w_down']
SMALL_NAMES = [n for n in WEIGHT_NAMES if n not in BIG_NAMES and n != 'meta_tokens']
WHOLE_NAMES = ['na_rpb', 's5_lam_re', 's5_lam_im', 's5_log_dt']
LEAD_NAMES = ['s5_b_re', 's5_b_im', 's5_c_re', 's5_c_im']


def kernel(x, meta_tokens, ffn1_pre_g, ffn1_post_g, ffn1_w_gate, ffn1_w_up, ffn1_w_down, mix_pre_g, w_in, na_rpb, s5_lam_re, s5_lam_im, s5_log_dt, s5_b_re, s5_b_im, s5_c_re, s5_c_im, s5_d, s5_w_glu, s5_b_glu, na_out_g, s5_out_g, w_out, mix_post_g, ffn2_pre_g, ffn2_post_g, ffn2_w_gate, ffn2_w_up, ffn2_w_down, final_g, loss_target, m_meta_tokens, m_ffn1_pre_g, m_ffn1_post_g, m_ffn1_w_gate, m_ffn1_w_up, m_ffn1_w_down, m_mix_pre_g, m_w_in, m_na_rpb, m_s5_lam_re, m_s5_lam_im, m_s5_log_dt, m_s5_b_re, m_s5_b_im, m_s5_c_re, m_s5_c_im, m_s5_d, m_s5_w_glu, m_s5_b_glu, m_na_out_g, m_s5_out_g, m_w_out, m_mix_post_g, m_ffn2_pre_g, m_ffn2_post_g, m_ffn2_w_gate, m_ffn2_w_up, m_ffn2_w_down, m_final_g, v_meta_tokens, v_ffn1_pre_g, v_ffn1_post_g, v_ffn1_w_gate, v_ffn1_w_up, v_ffn1_w_down, v_mix_pre_g, v_w_in, v_na_rpb, v_s5_lam_re, v_s5_lam_im, v_s5_log_dt, v_s5_b_re, v_s5_b_im, v_s5_c_re, v_s5_c_im, v_s5_d, v_s5_w_glu, v_s5_b_glu, v_na_out_g, v_s5_out_g, v_w_out, v_mix_post_g, v_ffn2_pre_g, v_ffn2_post_g, v_ffn2_w_gate, v_ffn2_w_up, v_ffn2_w_down, v_final_g):
    args = dict(locals())
    w = {n: args[n] for n in WEIGHT_NAMES}
    m = {n: args["m_" + n] for n in WEIGHT_NAMES}
    v = {n: args["v_" + n] for n in WEIGHT_NAMES}

    small = {n: w[n] for n in SMALL_NAMES}

    pending = {}

    def start(group, names, arrays, gather, peers=ALL_PEERS, slot=_slot8):
        lands = _place_own(arrays, gather, "own_" + group, slot)
        send_sems, recv_sems, arrays, lands, token = _exchange_start(arrays, lands, gather, "start_" + group, peers, slot)
        pending[group] = (names, send_sems, recv_sems, arrays, lands, gather, peers, slot)
        return token

    def finish(group, after):
        names, send_sems, recv_sems, arrays, lands, gather, peers, slot = pending.pop(group)
        lands, token = _exchange_wait(send_sems, recv_sems, arrays, lands, after, gather, "wait_" + group, peers, slot)
        return dict(zip(names, lands)), token

    first = ["ffn1_w_gate", "ffn1_w_up", "ffn1_w_down"]
    def shard(n, token=None):
        return _dep(_stored(n, w[n])[0], None if token is None else token[0, 0]).astype(BF16)

    ffn_names = ("ffn1_w_gate", "ffn1_w_up", "ffn1_w_down", "ffn2_w_gate", "ffn2_w_up", "ffn2_w_down")
    later_groups = (("w_in", ["w_in"]), ("mix", ["s5_w_glu", "w_out"]), ("ffn2", ["ffn2_w_gate", "ffn2_w_up", "ffn2_w_down"]))
    token1 = start("ffn1", first, [shard(n) for n in first], True, (SIBLING,) + CHIP_PEERS)
    meta_full = _exchange([w["meta_tokens"]], True, "gather_meta", after=[token1])[0].transpose(1, 0, 2).reshape(N_META, D_MODEL)
    later_shards = {n: shard(n, token1) for _, names in later_groups for n in names}
    for n in ("na_rpb", "s5_lam_re"):
        small[n] = _dep(small[n], token1[0, 0])

    def get_w(group, after):
        if group == "meta":
            return {"meta_tokens": meta_full}
        if group == "ffn1":
            after = list(after) + list(later_shards.values())
        got, token = finish(group, after)
        if group == "ffn1":
            got = dict(zip(got, _forward_sibling(list(got.values()), "forward_ffn1")))
            got["tokens"] = [start(g, names + ["order"], [later_shards[n] for n in names] + [token], True) for g, names in later_groups]
        if group == "mix":
            got = {"s5_w_glu": got["s5_w_glu"].reshape(S5_WIDTH, S5_WIDTH), "w_out": got["w_out"].reshape(D_MODEL, D_MODEL)}
        return {n: (a.reshape(D_FF, D_MODEL) if n in ffn_names else a) for n, a in got.items()}

    tokens = {}

    def emit(group, grads):
        grads = {n: (g.reshape(N_DEV, FF_SHARD, D_MODEL) if n in ffn_names else g) for n, g in grads.items()}
        if group == "ffn1":
            mine = [g.reshape((N_DEV // 2, 2) + g.shape[1:]) for g in grads.values()]
            theirs = _swap_sibling(mine, "swap_g_ffn1")
            sums = [_sum_pairs(a, b, "pair_sum_" + n) for n, a, b in zip(grads, mine, theirs)]
            tokens[group] = start("g_ffn1", list(grads), sums, False, CHIP_PEERS, _slot4)
        else:
            tokens[group] = start("g_" + group, list(grads), list(grads.values()), group == "small")
        return tokens[group][0, 0]

    loss8, grad_x, gmeta, vec_g = _local_step(x[0], loss_target[0], get_w, small, emit)
    res = {}

    def update_shard(n, pieces):
        outs = _adamw(_stored(n, w[n]), _stored(n, m[n]), _stored(n, v[n]), pieces, "adamw_" + n)
        res[n] = [_stored(n, o) for o in outs]

    late = [grad_x, tokens["ffn1"]]
    for group in ("g_ffn2", "g_mix", "g_w_in"):
        for n, pieces in finish(group, late)[0].items():
            update_shard(n, pieces)
    g8 = finish("g_small", late)[0]
    dense = _sum8(g8["dense"], "sum_dense")
    for i, n in enumerate(LEAD_NAMES):
        g = dense[i].reshape(_stored(n, w[n]).shape)
        upd = _adamw_s5_mat(_stored(n, w[n]), _stored(n, m[n]), _stored(n, v[n]), g, "adamw_" + n)
        res[n] = [_stored(n, o) for o in [g] + list(upd)]

    done = [res[n][1] for n in ("ffn2_w_gate", "ffn2_w_up", "ffn2_w_down", "w_in", "w_out", "s5_w_glu") + tuple(LEAD_NAMES)]
    packed8, gmeta8 = _exchange([_pack_vectors(vec_g, loss8), gmeta], True, "gather_vectors", after=done)
    for n, pieces in finish("g_ffn1", packed8)[0].items():
        update_shard(n, pieces)
    _, _, _, me = _me()
    update_shard("meta_tokens", lax.dynamic_slice_in_dim(gmeta8, me * (D_MODEL // N_DEV), D_MODEL // N_DEV, axis=2))

    outs = _adamw_small(packed8, [(w[n], m[n], v[n]) for n in VEC_NAMES], [(w[n], m[n], v[n], g8[n]) for n in WHOLE_NAMES])
    for i, n in enumerate(VEC_NAMES + WHOLE_NAMES):
        res[n] = list(outs[4 * i:4 * i + 4])

    out = [outs[-1][0, 0], grad_x[None]]
    for kind in range(4):
        out += [res[n][kind] for n in WEIGHT_NAMES]
    return tuple(out)
```

```python
import functools
import math

import numpy as np
import jax
import jax.numpy as jnp
from jax import lax
from jax.experimental import pallas as pl
from jax.experimental.pallas import tpu as pltpu

F32 = jnp.float32
BF16 = jnp.bfloat16
SDS = jax.ShapeDtypeStruct

D_MODEL = 1024
N_TOK = 2048
N_META = 16
SEQ = N_TOK + N_META
ROW_TILE = 688
N_ROW_TILES = SEQ // ROW_TILE
N_DEV = 8
D_FF = 2816
FF_SHARD = D_FF // N_DEV
FF_TILE = 256
IN_SHARD = 256
NA_WIDTH = 512
S5_WIDTH = 512
HEADS = 8
HEAD_DIM = 64
GRID_W = 64
GRID_ROWS = N_TOK // GRID_W
KH = 8
KW = 16
NA_RB = 4
NA_KR = KH + NA_RB - 1
NA_BLOCKS = GRID_ROWS // NA_RB
NA_QB = NA_RB * GRID_W
NA_KB = NA_KR * GRID_W
NA_TYPES = 3
S5_GROUPS = 32
S5_GROUP = 16
S5_STATE = 64
S5_CHUNKS = 4
CH_W = S5_WIDTH // S5_CHUNKS
ST_W = S5_GROUPS * S5_STATE // S5_CHUNKS
SCAN_BLOCKS = 8
SCAN_T = SEQ // SCAN_BLOCKS
RMS_EPS = 1e-6
NEG_INF = -1e30
ATT_SCALE = HEAD_DIM ** -0.5
ADAM_LR, ADAM_B1, ADAM_B2, ADAM_EPS, ADAM_WD, ADAM_STEP = 0.001, 0.9, 0.999, 1e-08, 0.01, 10
VMEM_LIMIT = 56 * 1024 * 1024
MESH = pl.DeviceIdType.MESH
AXES = ("x", "y", "c")


def _params(sem=None):
    return pltpu.CompilerParams(dimension_semantics=sem, vmem_limit_bytes=VMEM_LIMIT)


def _dot(a, b):
    return jnp.dot(a, b, preferred_element_type=F32)


def _dot_nt(a, b):
    return lax.dot_general(a, b, (((1,), (1,)), ((), ())), preferred_element_type=F32)


def _dot_tn(a, b):
    return lax.dot_general(a, b, (((0,), (0,)), ((), ())), preferred_element_type=F32)


def _rstd(x):
    return lax.rsqrt(jnp.mean(x * x, axis=-1, keepdims=True) + RMS_EPS)


def _rms_bwd(x, r, g, dy):
    dyg = dy * g
    xr = x * r
    dx = r * (dyg - xr * jnp.mean(dyg * xr, axis=-1, keepdims=True))
    return dx, dy * xr


def _rows(i, size=ROW_TILE):
    return pl.ds(pl.multiple_of(i * size, 16), size)


def _row_spec(width):
    return pl.BlockSpec((ROW_TILE, width), lambda i: (i, 0))


def _fix_spec(shape):
    return pl.BlockSpec(shape, lambda i: (0,) * len(shape))


def _split3(x):
    hi = x.astype(BF16)
    r1 = x - hi.astype(F32)
    mid = r1.astype(BF16)
    lo = (r1 - mid.astype(F32)).astype(BF16)
    return hi, mid, lo


def _prenorm(x, g):
    def body(x_ref, g_ref, a_ref):
        xv = x_ref[...]
        a_ref[...] = (xv * _rstd(xv) * g_ref[...]).astype(BF16)

    return pl.pallas_call(
        body, grid=(N_ROW_TILES,), in_specs=[_row_spec(D_MODEL), _fix_spec((1, D_MODEL))],
        out_specs=_row_spec(D_MODEL), out_shape=SDS((SEQ, D_MODEL), BF16), name="prenorm",
        compiler_params=_params(("parallel",)))(x, g)


def _post_pre(f, hres, g_post, g_next, scale, name):
    def body(f_ref, h_ref, gp_ref, gn_ref, ho_ref, a_ref):
        fv = f_ref[...]
        h = h_ref[...] + scale * (fv * _rstd(fv) * gp_ref[...])
        ho_ref[...] = h
        a_ref[...] = (h * _rstd(h) * gn_ref[...]).astype(BF16)

    return pl.pallas_call(
        body, grid=(N_ROW_TILES,),
        in_specs=[_row_spec(D_MODEL), _row_spec(D_MODEL), _fix_spec((1, D_MODEL)), _fix_spec((1, D_MODEL))],
        out_specs=[_row_spec(D_MODEL), _row_spec(D_MODEL)],
        out_shape=[SDS((SEQ, D_MODEL), F32), SDS((SEQ, D_MODEL), BF16)], name=name,
        compiler_params=_params(("parallel",)))(f, hres, g_post, g_next)


def _final_loss(f2, h2, g_post, g_final, target):
    def body(f_ref, h_ref, gp_ref, gf_ref, t_ref, loss_ref, dh_ref, df_ref, dgf_ref, dgp_ref):
        i = pl.program_id(0)
        fv = f_ref[...]
        r1 = _rstd(fv)
        gp = gp_ref[...]
        h3 = h_ref[...] + 0.5 * (fv * r1 * gp)
        r2 = _rstd(h3)
        gf = gf_ref[...]
        y = h3 * r2 * gf
        row = lax.broadcasted_iota(jnp.int32, (ROW_TILE, 1), 0) + i * ROW_TILE
        err = jnp.where(row >= N_META, y - t_ref[...], 0.0)
        part = 0.5 * jnp.sum(jnp.mean(err * err, axis=-1, keepdims=True))
        dy = err * (1.0 / D_MODEL)
        dh3, dgf = _rms_bwd(h3, r2, gf, dy)
        dh_ref[...] = dh3
        df, dgp = _rms_bwd(fv, r1, gp, 0.5 * dh3)
        df_ref[...] = df.astype(BF16)

        @pl.when(i == 0)
        def _():
            loss_ref[...] = jnp.zeros_like(loss_ref)
            dgf_ref[...] = jnp.zeros_like(dgf_ref)
            dgp_ref[...] = jnp.zeros_like(dgp_ref)

        loss_ref[...] += part
        dgf_ref[...] += jnp.sum(dgf, axis=0, keepdims=True)
        dgp_ref[...] += jnp.sum(dgp, axis=0, keepdims=True)

    gain = _fix_spec((1, D_MODEL))
    return pl.pallas_call(
        body, grid=(N_ROW_TILES,),
        in_specs=[_row_spec(D_MODEL), _row_spec(D_MODEL), gain, gain, _row_spec(D_MODEL)],
        out_specs=[_fix_spec((8, 128)), _row_spec(D_MODEL), _row_spec(D_MODEL), gain, gain],
        out_shape=[SDS((8, 128), F32), SDS((SEQ, D_MODEL), F32), SDS((SEQ, D_MODEL), BF16),
                   SDS((1, D_MODEL), F32), SDS((1, D_MODEL), F32)],
        name="final_loss", compiler_params=_params(("arbitrary",)))(f2, h2, g_post, g_final, target)


def _bwd_pre_post(da, h, g_pre, dh_res, fprev, g_post, scale, name):
    def body(da_ref, h_ref, gpre_ref, dhr_ref, f_ref, gpost_ref, dh_ref, df_ref, dgpre_ref, dgpost_ref):
        i = pl.program_id(0)
        hv = h_ref[...]
        dxa, dgpre = _rms_bwd(hv, _rstd(hv), gpre_ref[...], da_ref[...])
        dh = dhr_ref[...] + dxa
        dh_ref[...] = dh
        fv = f_ref[...]
        df, dgpost = _rms_bwd(fv, _rstd(fv), gpost_ref[...], scale * dh)
        df_ref[...] = df.astype(BF16)

        @pl.when(i == 0)
        def _():
            dgpre_ref[...] = jnp.zeros_like(dgpre_ref)
            dgpost_ref[...] = jnp.zeros_like(dgpost_ref)

        dgpre_ref[...] += jnp.sum(dgpre, axis=0, keepdims=True)
        dgpost_ref[...] += jnp.sum(dgpost, axis=0, keepdims=True)

    gain = _fix_spec((1, D_MODEL))
    row = _row_spec(D_MODEL)
    return pl.pallas_call(
        body, grid=(N_ROW_TILES,), in_specs=[row, row, gain, row, row, gain],
        out_specs=[row, row, gain, gain],
        out_shape=[SDS((SEQ, D_MODEL), F32), SDS((SEQ, D_MODEL), BF16), SDS((1, D_MODEL), F32), SDS((1, D_MODEL), F32)],
        name=name, compiler_params=_params(("arbitrary",)))(da, h, g_pre, dh_res, fprev, g_post)


def _bwd_pre_only(da, h, g_pre, dh_res):
    def body(da_ref, h_ref, gpre_ref, dhr_ref, dh_ref, dgpre_ref):
        i = pl.program_id(0)
        hv = h_ref[...]
        dxa, dgpre = _rms_bwd(hv, _rstd(hv), gpre_ref[...], da_ref[...])
        dh_ref[...] = dhr_ref[...] + dxa

        @pl.when(i == 0)
        def _():
            dgpre_ref[...] = jnp.zeros_like(dgpre_ref)

        dgpre_ref[...] += jnp.sum(dgpre, axis=0, keepdims=True)

    gain = _fix_spec((1, D_MODEL))
    row = _row_spec(D_MODEL)
    return pl.pallas_call(
        body, grid=(N_ROW_TILES,), in_specs=[row, row, gain, row], out_specs=[row, gain],
        out_shape=[SDS((SEQ, D_MODEL), F32), SDS((1, D_MODEL), F32)],
        name="bwd_pre_only", compiler_params=_params(("arbitrary",)))(da, h, g_pre, dh_res)


def _ffn_fwd(a, wg, wu, wd, name, after=()):
    def body(a_ref, wg_ref, wu_ref, wd_ref, *rest):
        gate_ref, up_ref, f_ref = rest[len(after):]
        j = pl.program_id(0)

        def tile(i, carry):
            rows = _rows(i)
            at = a_ref[rows, :]
            gate = _dot_nt(at, wg_ref[...])
            up = _dot_nt(at, wu_ref[...])
            gate_ref[rows, :] = gate.astype(BF16)
            up_ref[rows, :] = up.astype(BF16)
            act = (gate * jax.nn.sigmoid(gate) * up).astype(BF16)
            contrib = _dot(act, wd_ref[...])

            @pl.when(j == 0)
            def _():
                f_ref[rows, :] = contrib

            @pl.when(j != 0)
            def _():
                f_ref[rows, :] += contrib

            return carry

        lax.fori_loop(0, N_ROW_TILES, tile, 0)

    wtile = pl.BlockSpec((FF_TILE, D_MODEL), lambda j: (j, 0))
    hid = pl.BlockSpec((SEQ, FF_TILE), lambda j: (0, j))
    full = pl.BlockSpec((SEQ, D_MODEL), lambda j: (0, 0))
    return pl.pallas_call(
        body, grid=(D_FF // FF_TILE,), in_specs=[full, wtile, wtile, wtile] + [pl.BlockSpec(memory_space=pl.ANY)] * len(after),
        out_specs=[hid, hid, full],
        out_shape=[SDS((SEQ, D_FF), BF16), SDS((SEQ, D_FF), BF16), SDS((SEQ, D_MODEL), F32)],
        name=name, compiler_params=_params(("arbitrary",)))(a, wg, wu, wd, *after)


def _ffn_bwd(df, a, gate, up, wg, wu, wd, name):
    def body(df_ref, a_ref, gate_ref, up_ref, wg_ref, wu_ref, wd_ref, da_ref, dwg_ref, dwu_ref, dwd_ref,
             acc_g, acc_u, acc_d):
        j = pl.program_id(0)

        def tile(i, carry):
            rows = _rows(i)
            dft = df_ref[rows, :]
            at = a_ref[rows, :]
            gate = gate_ref[rows, :].astype(F32)
            up = up_ref[rows, :].astype(F32)
            dact = _dot_nt(dft, wd_ref[...])
            sig = jax.nn.sigmoid(gate)
            silu = gate * sig
            dgate = (dact * up * (sig * (1.0 + gate * (1.0 - sig)))).astype(BF16)
            dup = (dact * silu).astype(BF16)
            act = (silu * up).astype(BF16)
            dwd = _dot_tn(act, dft)
            dwg = _dot_tn(dgate, at)
            dwu = _dot_tn(dup, at)
            dat = _dot(dgate, wg_ref[...]) + _dot(dup, wu_ref[...])

            @pl.when(i == 0)
            def _():
                acc_d[...] = dwd
                acc_g[...] = dwg
                acc_u[...] = dwu

            @pl.when(i != 0)
            def _():
                acc_d[...] += dwd
                acc_g[...] += dwg
                acc_u[...] += dwu

            @pl.when(j == 0)
            def _():
                da_ref[rows, :] = dat

            @pl.when(j != 0)
            def _():
                da_ref[rows, :] += dat

            return carry

        lax.fori_loop(0, N_ROW_TILES, tile, 0)
        dwg_ref[...] = acc_g[...].astype(BF16)
        dwu_ref[...] = acc_u[...].astype(BF16)
        dwd_ref[...] = acc_d[...].astype(BF16)

    wtile = pl.BlockSpec((FF_TILE, D_MODEL), lambda j: (j, 0))
    hid = pl.BlockSpec((SEQ, FF_TILE), lambda j: (0, j))
    full = pl.BlockSpec((SEQ, D_MODEL), lambda j: (0, 0))
    return pl.pallas_call(
        body, grid=(D_FF // FF_TILE,), in_specs=[full, full, hid, hid, wtile, wtile, wtile],
        out_specs=[full, wtile, wtile, wtile],
        out_shape=[SDS((SEQ, D_MODEL), F32)] + [SDS((D_FF, D_MODEL), BF16)] * 3,
        scratch_shapes=[pltpu.VMEM((FF_TILE, D_MODEL), F32)] * 3,
        name=name, compiler_params=_params(("arbitrary",)))(df, a, gate, up, wg, wu, wd)


HEADS_PER_BLOCK = IN_SHARD // HEAD_DIM
QKV_BLOCKS = 3 * NA_WIDTH // IN_SHARD


def _proj_heads(a, w):
    def body(a_ref, w_ref, o_ref):
        def tile(i, carry):
            rows = _rows(i)
            res = _dot(a_ref[rows, :], w_ref[...])
            for sub in range(HEADS_PER_BLOCK):
                o_ref[sub, rows, :] = res[:, sub * HEAD_DIM:(sub + 1) * HEAD_DIM]
            return carry

        lax.fori_loop(0, N_ROW_TILES, tile, 0)

    return pl.pallas_call(
        body, grid=(QKV_BLOCKS,),
        in_specs=[pl.BlockSpec((SEQ, D_MODEL), lambda j: (0, 0)), pl.BlockSpec((None, D_MODEL, IN_SHARD), lambda j: (j, 0, 0))],
        out_specs=pl.BlockSpec((HEADS_PER_BLOCK, SEQ, HEAD_DIM), lambda j: (j, 0, 0)),
        out_shape=SDS((3 * HEADS, SEQ, HEAD_DIM), F32), name="proj_heads",
        compiler_params=_params(("parallel",)))(a, w)


def _proj_u(a, w):
    def body(a_ref, w_ref, o_ref):
        def tile(i, carry):
            rows = _rows(i)
            o_ref[rows, :] = _dot(a_ref[rows, :], w_ref[...])
            return carry

        lax.fori_loop(0, N_ROW_TILES, tile, 0)

    return pl.pallas_call(
        body, grid=(N_DEV - QKV_BLOCKS,),
        in_specs=[pl.BlockSpec((SEQ, D_MODEL), lambda j: (0, 0)),
                  pl.BlockSpec((None, D_MODEL, IN_SHARD), lambda j: (j + QKV_BLOCKS, 0, 0))],
        out_specs=pl.BlockSpec((SEQ, IN_SHARD), lambda j: (0, j)),
        out_shape=SDS((SEQ, S5_WIDTH), F32), name="proj_u",
        compiler_params=_params(("parallel",)))(a, w)


def _proj_bwd(dq, dk, dv, du, a, w):
    def body(dq_ref, dk_ref, dv_ref, du_ref, a_ref, w_ref, da_ref, dw_ref, acc, dp_ref):
        j = pl.program_id(0)

        for which, src in enumerate((dq_ref, dk_ref, dv_ref)):
            @pl.when((j >= 2 * which) & (j < 2 * which + 2))
            def _(src=src):
                dp_ref[...] = jnp.concatenate([src[sub] for sub in range(HEADS_PER_BLOCK)], axis=-1).astype(BF16)

        @pl.when(j >= QKV_BLOCKS)
        def _():
            dp_ref[...] = du_ref[...].astype(BF16)

        def tile(i, carry):
            rows = _rows(i)
            dpt = dp_ref[rows, :]
            dw = _dot_tn(a_ref[rows, :], dpt)
            dat = _dot_nt(dpt, w_ref[...])

            @pl.when(i == 0)
            def _():
                acc[...] = dw

            @pl.when(i != 0)
            def _():
                acc[...] += dw

            @pl.when(j == 0)
            def _():
                da_ref[rows, :] = dat

            @pl.when(j != 0)
            def _():
                da_ref[rows, :] += dat

            return carry

        lax.fori_loop(0, N_ROW_TILES, tile, 0)
        dw_ref[...] = acc[...].astype(BF16)

    full = pl.BlockSpec((SEQ, D_MODEL), lambda j: (0, 0))
    wspec = pl.BlockSpec((None, D_MODEL, IN_SHARD), lambda j: (j, 0, 0))

    def heads(which):
        return pl.BlockSpec((HEADS_PER_BLOCK, SEQ, HEAD_DIM), lambda j: (jnp.clip(j - 2 * which, 0, 1), 0, 0))

    return pl.pallas_call(
        body, grid=(N_DEV,),
        in_specs=[heads(0), heads(1), heads(2),
                  pl.BlockSpec((SEQ, IN_SHARD), lambda j: (0, jnp.clip(j - QKV_BLOCKS, 0, 1))), full, wspec],
        out_specs=[full, wspec],
        out_shape=[SDS((SEQ, D_MODEL), F32), SDS((N_DEV, D_MODEL, IN_SHARD), BF16)],
        scratch_shapes=[pltpu.VMEM((D_MODEL, IN_SHARD), F32), pltpu.VMEM((SEQ, IN_SHARD), BF16)],
        name="proj_bwd", compiler_params=_params(("arbitrary",)))(dq, dk, dv, du, a, w)


def _na_consts():
    c = np.arange(GRID_W)
    col_start = np.clip(c - KW // 2, 0, GRID_W - KW)
    col_in = (c[None, :] >= col_start[:, None]) & (c[None, :] < col_start[:, None] + KW)
    dc = np.clip(c[None, :] - c[:, None] + KW - 1, 0, 2 * KW - 2)
    onehot = np.zeros((128, GRID_W * GRID_W), np.float32)
    qq, kk = np.meshgrid(c, c, indexing="ij")
    onehot[dc[col_in], (qq * GRID_W + kk)[col_in]] = 1.0
    negmask = np.where(col_in, 0.0, NEG_INF).astype(np.float32).reshape(1, -1)
    return onehot, negmask


def _na_pair(block_type, a, b):
    if block_type == 0:
        return b - a + KH - 1 if b < KH else None
    if block_type == 1:
        return b - a + KH // 2 - 1 if a <= b < a + KH else None
    return b - a if b >= NA_KR - KH else None


def _rpb_expand(rpb):
    onehot, negmask = _na_consts()
    rows = HEADS * (2 * KH - 1)
    rpb_pad = jnp.pad(rpb.reshape(rows, 2 * KW - 1), ((0, 128 - rows), (0, 128 - (2 * KW - 1))))

    def body(r_ref, oh_ref, m_ref, t_ref):
        hi, mid, lo = _split3(r_ref[...])
        oh = oh_ref[...]
        t_ref[...] = _dot(hi, oh) + _dot(mid, oh) + _dot(lo, oh) + m_ref[...]

    table = pl.pallas_call(body, out_shape=SDS((128, GRID_W * GRID_W), F32), name="rpb_expand",
                           compiler_params=_params())(rpb_pad, jnp.asarray(onehot, BF16), jnp.asarray(negmask))
    return table[:rows].reshape(HEADS, 2 * KH - 1, GRID_W, GRID_W)


def _rpb_reduce(dslabs):
    onehot, _ = _na_consts()
    rows = HEADS * (2 * KH - 1)

    def body(x_ref, oht_ref, o_ref):
        hi, mid, lo = _split3(x_ref[...])
        oht = oht_ref[...]
        o_ref[...] = _dot(hi, oht) + _dot(mid, oht) + _dot(lo, oht)

    out = pl.pallas_call(body, out_shape=SDS((rows, 128), F32), name="rpb_reduce", compiler_params=_params())(
        dslabs.reshape(rows, GRID_W * GRID_W), jnp.asarray(onehot.T, BF16))
    return out.reshape(HEADS, 2 * KH - 1, 128)


def _bias_tiles(slab_ref, tile_ref):
    tile_ref[...] = jnp.full(tile_ref.shape, NEG_INF, F32)
    for t in range(NA_TYPES):
        for a in range(NA_RB):
            for b in range(NA_KR):
                dr = _na_pair(t, a, b)
                if dr is not None:
                    tile_ref[t, a * GRID_W:(a + 1) * GRID_W, b * GRID_W:(b + 1) * GRID_W] = slab_ref[dr]


def _bias_tiles_bwd(dtile_ref, dslab_ref):
    acc = {}
    for t in range(NA_TYPES):
        for a in range(NA_RB):
            for b in range(NA_KR):
                dr = _na_pair(t, a, b)
                if dr is not None:
                    part = dtile_ref[t, a * GRID_W:(a + 1) * GRID_W, b * GRID_W:(b + 1) * GRID_W]
                    acc[dr] = part if dr not in acc else acc[dr] + part
    for dr in range(2 * KH - 1):
        dslab_ref[dr] = acc[dr]


def _block_geometry(g):
    start = jnp.clip(g * NA_RB - KH // 2, 0, GRID_ROWS - NA_KR)
    block_type = jnp.where(g == 0, 0, jnp.where(g == NA_BLOCKS - 1, 2, 1))
    q0 = pl.multiple_of(N_META + g * NA_QB, 16)
    k0 = pl.multiple_of(N_META + start * GRID_W, 16)
    return block_type, q0, k0


def _na_probs(q, kk, km, bias):
    s = _dot_nt(q, kk) * ATT_SCALE + bias
    sm = _dot_nt(q, km) * ATT_SCALE
    m = jnp.maximum(jnp.max(s, axis=-1, keepdims=True), jnp.max(sm, axis=-1, keepdims=True))
    p = jnp.exp(s - m)
    pm = jnp.exp(sm - m)
    inv = 1.0 / (jnp.sum(p, axis=-1, keepdims=True) + jnp.sum(pm, axis=-1, keepdims=True))
    return p * inv, pm * inv


def _meta_probs(qm, km):
    s = _dot_nt(qm, km) * ATT_SCALE
    p = jnp.exp(s - jnp.max(s, axis=-1, keepdims=True))
    return p / jnp.sum(p, axis=-1, keepdims=True)


def _qkv_specs():
    return [pl.BlockSpec((None, SEQ, HEAD_DIM), lambda h, which=which: (h + which * HEADS, 0, 0)) for which in range(3)]


def _na_fwd(qkv, bias):
    def body(q_ref, k_ref, v_ref, slab_ref, o_ref, b_ref):
        _bias_tiles(slab_ref, b_ref)
        km = k_ref[0:N_META, :].astype(BF16)
        vm = v_ref[0:N_META, :].astype(BF16)
        pmm = _meta_probs(q_ref[0:N_META, :].astype(BF16), km)
        o_ref[0:N_META, :] = _dot(pmm.astype(BF16), vm)

        def block(g, carry):
            block_type, q0, k0 = _block_geometry(g)
            qb = q_ref[pl.ds(q0, NA_QB), :].astype(BF16)
            kk = k_ref[pl.ds(k0, NA_KB), :].astype(BF16)
            vv = v_ref[pl.ds(k0, NA_KB), :].astype(BF16)
            p, pm = _na_probs(qb, kk, km, b_ref[block_type])
            o_ref[pl.ds(q0, NA_QB), :] = _dot(p.astype(BF16), vv) + _dot(pm.astype(BF16), vm)
            return carry

        lax.fori_loop(0, NA_BLOCKS, block, 0)

    head = pl.BlockSpec((None, SEQ, HEAD_DIM), lambda h: (h, 0, 0))
    return pl.pallas_call(
        body, grid=(HEADS,), in_specs=_qkv_specs() + [pl.BlockSpec((None, 2 * KH - 1, GRID_W, GRID_W), lambda h: (h, 0, 0, 0))],
        out_specs=head, out_shape=SDS((HEADS, SEQ, HEAD_DIM), F32), name="na_fwd",
        scratch_shapes=[pltpu.VMEM((NA_TYPES, NA_QB, NA_KB), F32)],
        compiler_params=_params(("parallel",)))(qkv, qkv, qkv, bias)


def _na_bwd(qkv, bias, do):
    def body(q_ref, k_ref, v_ref, slab_ref, do_ref, dq_ref, dk_ref, dv_ref, dslab_ref, b_ref, db_ref):
        _bias_tiles(slab_ref, b_ref)
        km = k_ref[0:N_META, :].astype(BF16)
        vm = v_ref[0:N_META, :].astype(BF16)
        dk_ref[...] = jnp.zeros_like(dk_ref)
        dv_ref[...] = jnp.zeros_like(dv_ref)
        db_ref[...] = jnp.zeros_like(db_ref)

        qm = q_ref[0:N_META, :].astype(BF16)
        dom = do_ref[0:N_META, :].astype(BF16)
        pmm = _meta_probs(qm, km)
        dpm = _dot_nt(dom, vm)
        dsm = (pmm * (dpm - jnp.sum(pmm * dpm, axis=-1, keepdims=True)) * ATT_SCALE).astype(BF16)
        dq_ref[0:N_META, :] = _dot(dsm, km)
        dkm0 = _dot_tn(dsm, qm)
        dvm0 = _dot_tn(pmm.astype(BF16), dom)

        def block(g, carry):
            dkm, dvm = carry
            block_type, q0, k0 = _block_geometry(g)
            qb = q_ref[pl.ds(q0, NA_QB), :].astype(BF16)
            kk = k_ref[pl.ds(k0, NA_KB), :].astype(BF16)
            vv = v_ref[pl.ds(k0, NA_KB), :].astype(BF16)
            dob = do_ref[pl.ds(q0, NA_QB), :].astype(BF16)
            p, pm = _na_probs(qb, kk, km, b_ref[block_type])
            dp = _dot_nt(dob, vv)
            dpm_ = _dot_nt(dob, vm)
            delta = jnp.sum(p * dp, axis=-1, keepdims=True) + jnp.sum(pm * dpm_, axis=-1, keepdims=True)
            ds = p * (dp - delta)
            dsm_ = pm * (dpm_ - delta)
            db_ref[block_type] += ds
            dsb = (ds * ATT_SCALE).astype(BF16)
            dsmb = (dsm_ * ATT_SCALE).astype(BF16)
            dq_ref[pl.ds(q0, NA_QB), :] = _dot(dsb, kk) + _dot(dsmb, km)
            dk_ref[pl.ds(k0, NA_KB), :] += _dot_tn(dsb, qb)
            dv_ref[pl.ds(k0, NA_KB), :] += _dot_tn(p.astype(BF16), dob)
            return dkm + _dot_tn(dsmb, qb), dvm + _dot_tn(pm.astype(BF16), dob)

        dkm, dvm = lax.fori_loop(0, NA_BLOCKS, block, (dkm0, dvm0))
        dk_ref[0:N_META, :] = dkm
        dv_ref[0:N_META, :] = dvm
        _bias_tiles_bwd(db_ref, dslab_ref)

    head = pl.BlockSpec((None, SEQ, HEAD_DIM), lambda h: (h, 0, 0))
    bspec = pl.BlockSpec((None, 2 * KH - 1, GRID_W, GRID_W), lambda h: (h, 0, 0, 0))
    return pl.pallas_call(
        body, grid=(HEADS,), in_specs=_qkv_specs() + [bspec, head], out_specs=[head, head, head, bspec],
        out_shape=[SDS((HEADS, SEQ, HEAD_DIM), F32)] * 3 + [SDS((HEADS, 2 * KH - 1, GRID_W, GRID_W), F32)],
        scratch_shapes=[pltpu.VMEM((NA_TYPES, NA_QB, NA_KB), F32), pltpu.VMEM((NA_TYPES, NA_QB, NA_KB), F32)],
        name="na_bwd", compiler_params=_params(("parallel",)))(qkv, qkv, qkv, bias, do)


def _cmul(ar, ai, br, bi):
    return ar * br - ai * bi, ar * bi + ai * br


def _cpow(ar, ai, n):
    rr, ri = None, None
    br, bi = ar, ai
    while n:
        if n & 1:
            rr, ri = (br, bi) if rr is None else _cmul(rr, ri, br, bi)
        n >>= 1
        if n:
            br, bi = _cmul(br, bi, br, bi)
    return rr, ri


def _s5_prep(lr, li, logdt, bre, bim):
    def body(lr_ref, li_ref, dt_ref, br_ref, bi_ref, lbr_ref, lbi_ref, bbr_ref, bbi_ref):
        lr_, li_ = lr_ref[...], li_ref[...]
        dt = jnp.exp(dt_ref[...])
        mag = jnp.exp(lr_ * dt)
        lbr = mag * jnp.cos(li_ * dt)
        lbi = mag * jnp.sin(li_ * dt)
        lbr_ref[...] = lbr
        lbi_ref[...] = lbi
        den = lr_ * lr_ + li_ * li_
        xr = lbr - 1.0
        cr = (xr * lr_ + lbi * li_) / den
        ci = (lbi * lr_ - xr * li_) / den
        br, bi = br_ref[...], bi_ref[...]
        bbr_ref[...] = cr[:, None, :] * br - ci[:, None, :] * bi
        bbi_ref[...] = cr[:, None, :] * bi + ci[:, None, :] * br

    n = 2 * S5_GROUPS
    return pl.pallas_call(
        body, out_shape=[SDS((n, S5_STATE), F32)] * 2 + [SDS((n, S5_GROUP, S5_STATE), F32)] * 2,
        name="s5_prep", compiler_params=_params())(lr, li, logdt, bre, bim)


def _s5_prep_bwd(lr, li, logdt, bre, bim, dar, dai, dbbr, dbbi):
    def body(lr_ref, li_ref, dt_ref, br_ref, bi_ref, dar_ref, dai_ref, dbr_ref, dbi_ref,
             glr_ref, gli_ref, gdt_ref, gbr_ref, gbi_ref):
        lr_, li_ = lr_ref[...], li_ref[...]
        dt = jnp.exp(dt_ref[...])
        mag = jnp.exp(lr_ * dt)
        lbr = mag * jnp.cos(li_ * dt)
        lbi = mag * jnp.sin(li_ * dt)
        den = lr_ * lr_ + li_ * li_
        xr = lbr - 1.0
        cr = (xr * lr_ + lbi * li_) / den
        ci = (lbi * lr_ - xr * li_) / den
        br, bi = br_ref[...], bi_ref[...]
        dbr, dbi = dbr_ref[...], dbi_ref[...]
        gbr_ref[...] = cr[:, None, :] * dbr + ci[:, None, :] * dbi
        gbi_ref[...] = cr[:, None, :] * dbi - ci[:, None, :] * dbr
        gcr = jnp.sum(dbr * br + dbi * bi, axis=1)
        gci = jnp.sum(dbi * br - dbr * bi, axis=1)
        ilr, ili = lr_ / den, li_ / den
        tr, ti = _cmul(gcr, gci, ilr, ili)
        glbr = dar_ref[...] + tr
        glbi = dai_ref[...] + ti
        dr_, di_ = _cmul(tr, ti, cr, -ci)
        gwr, gwi = _cmul(glbr, glbi, lbr, -lbi)
        glr_ref[...] = gwr * dt - dr_
        gli_ref[...] = gwi * dt - di_
        gdt_ref[...] = jnp.sum(gwr * lr_ + gwi * li_, axis=-1, keepdims=True) * dt

    n = 2 * S5_GROUPS
    return pl.pallas_call(
        body, out_shape=[SDS((n, S5_STATE), F32)] * 2 + [SDS((n, 1), F32)] + [SDS((n, S5_GROUP, S5_STATE), F32)] * 2,
        name="s5_prep_bwd", compiler_params=_params())(lr, li, logdt, bre, bim, dar, dai, dbbr, dbbi)


def _scan_local(xr_ref, xi_ref, ar8, ai8, reverse):
    def step(i, carry):
        sr, si = carry
        idx = (SCAN_T - 1 - i) if reverse else i
        rows = pl.ds(pl.multiple_of(idx * SCAN_BLOCKS, SCAN_BLOCKS), SCAN_BLOCKS)
        nr = ar8 * sr - ai8 * si + xr_ref[rows, :]
        ni = ar8 * si + ai8 * sr + xi_ref[rows, :]
        xr_ref[rows, :] = nr
        xi_ref[rows, :] = ni
        return nr, ni

    z = jnp.zeros(ar8.shape, F32)
    return lax.fori_loop(0, SCAN_T, step, (z, z))


def _scan_carries(er, ei, atr, ati, reverse):
    row = lax.broadcasted_iota(jnp.int32, er.shape, 0)
    cr = jnp.zeros((1, er.shape[1]), F32)
    ci = cr
    outr = jnp.zeros(er.shape, F32)
    outi = outr
    order = range(SCAN_BLOCKS - 1, -1, -1) if reverse else range(SCAN_BLOCKS)
    for b in order:
        outr = jnp.where(row == b, cr, outr)
        outi = jnp.where(row == b, ci, outi)
        nr, ni = _cmul(atr, ati, cr, ci)
        cr, ci = nr + er[b:b + 1, :], ni + ei[b:b + 1, :]
    return outr, outi


def _scan_fixup(xr_ref, xi_ref, cr8, ci8, ar8, ai8, reverse):
    def step(i, carry):
        pr, pi = carry
        idx = (SCAN_T - 1 - i) if reverse else i
        rows = pl.ds(pl.multiple_of(idx * SCAN_BLOCKS, SCAN_BLOCKS), SCAN_BLOCKS)
        fr, fi = _cmul(pr, pi, cr8, ci8)
        xr_ref[rows, :] += fr
        xi_ref[rows, :] += fi
        return _cmul(pr, pi, ar8, ai8)

    lax.fori_loop(0, SCAN_T, step, (ar8, ai8), unroll=2)


def _scan(xr_ref, xi_ref, ar, ai, reverse):
    n = ar.shape[1]
    ar8 = jnp.broadcast_to(ar, (SCAN_BLOCKS, n))
    ai8 = jnp.broadcast_to(ai, (SCAN_BLOCKS, n))
    er, ei = _scan_local(xr_ref, xi_ref, ar8, ai8, reverse)
    atr, ati = _cpow(ar, ai, SCAN_T)
    cr8, ci8 = _scan_carries(er, ei, atr, ati, reverse)
    _scan_fixup(xr_ref, xi_ref, cr8, ci8, ar8, ai8, reverse)


def _s5_specs():
    chan = pl.BlockSpec((SEQ, CH_W), lambda c, d: (0, c))
    chan2 = pl.BlockSpec((None, SEQ, CH_W), lambda c, d: (d, 0, c))
    state = pl.BlockSpec((None, SEQ, ST_W), lambda c, d: (d, 0, c))
    bmat = pl.BlockSpec((None, None, CH_W, ST_W), lambda c, d: (d, c, 0, 0))
    cmat = pl.BlockSpec((None, None, ST_W, CH_W), lambda c, d: (d, c, 0, 0))
    avec = pl.BlockSpec((None, None, 1, ST_W), lambda c, d: (d, c, 0, 0))
    return chan, chan2, state, bmat, cmat, avec


def _scan_by_direction(xr_ref, xi_ref, ar, ai, d, adjoint):
    @pl.when(d == 0)
    def _():
        _scan(xr_ref, xi_ref, ar, ai, reverse=adjoint)

    @pl.when(d == 1)
    def _():
        _scan(xr_ref, xi_ref, ar, ai, reverse=not adjoint)


def _s5_scan_fwd(u, bre, bim, are, aim, cre, cim):
    def body(u_ref, bre_ref, bim_ref, are_ref, aim_ref, cre_ref, cim_ref, sr_ref, si_ref, y_ref):
        ub = u_ref[...].astype(BF16)
        sr_ref[...] = _dot(ub, bre_ref[...])
        si_ref[...] = _dot(ub, bim_ref[...])
        _scan_by_direction(sr_ref, si_ref, are_ref[...], aim_ref[...], pl.program_id(1), adjoint=False)
        y_ref[...] = _dot(sr_ref[...].astype(BF16), cre_ref[...]) - _dot(si_ref[...].astype(BF16), cim_ref[...])

    chan, chan2, state, bmat, cmat, avec = _s5_specs()
    return pl.pallas_call(
        body, grid=(S5_CHUNKS, 2), in_specs=[chan, bmat, bmat, avec, avec, cmat, cmat], out_specs=[state, state, chan2],
        out_shape=[SDS((2, SEQ, S5_GROUPS * S5_STATE), F32)] * 2 + [SDS((2, SEQ, S5_WIDTH), F32)],
        name="s5_scan_fwd", compiler_params=_params(("parallel", "parallel")))(u, bre, bim, are, aim, cre, cim)


def _dlam(gr_ref, gi_ref, sr_ref, si_ref, reverse):
    tile = lambda i: pl.ds(pl.multiple_of(i * SCAN_BLOCKS, SCAN_BLOCKS), SCAN_BLOCKS)
    row = lax.broadcasted_iota(jnp.int32, (SCAN_BLOCKS, ST_W), 0)
    if reverse:
        edge, src, shift, empty, lo, hi, dprev = SCAN_T - 1, 0, SCAN_BLOCKS - 1, SCAN_BLOCKS - 1, 0, SCAN_T - 1, 1
    else:
        edge, src, shift, empty, lo, hi, dprev = 0, SCAN_T - 1, 1, 0, 1, SCAN_T, -1
    spr = jnp.where(row == empty, 0.0, pltpu.roll(sr_ref[tile(src), :], shift, 0))
    spi = jnp.where(row == empty, 0.0, pltpu.roll(si_ref[tile(src), :], shift, 0))
    acc0 = _cmul(gr_ref[tile(edge), :], gi_ref[tile(edge), :], spr, -spi)

    def step(i, carry):
        accr, acci = carry
        pr, pi = _cmul(gr_ref[tile(i), :], gi_ref[tile(i), :], sr_ref[tile(i + dprev), :], -si_ref[tile(i + dprev), :])
        return accr + pr, acci + pi

    accr, acci = lax.fori_loop(lo, hi, step, acc0)
    return jnp.sum(accr, axis=0, keepdims=True), jnp.sum(acci, axis=0, keepdims=True)


def _s5_scan_bwd(dy, du_skip, u, sr, si, bre, bim, are, aim, cre, cim):
    def body(dy_ref, dus_ref, u_ref, sr_ref, si_ref, bre_ref, bim_ref, are_ref, aim_ref, cre_ref, cim_ref,
             du_ref, dbr_ref, dbi_ref, dcr_ref, dci_ref, dar_ref, dai_ref, gr_ref, gi_ref):
        d = pl.program_id(1)
        dyb = dy_ref[...].astype(BF16)
        gr_ref[...] = _dot_nt(dyb, cre_ref[...])
        gi_ref[...] = -_dot_nt(dyb, cim_ref[...])
        dcr_ref[...] = _dot_tn(sr_ref[...].astype(BF16), dyb)
        dci_ref[...] = -_dot_tn(si_ref[...].astype(BF16), dyb)
        _scan_by_direction(gr_ref, gi_ref, are_ref[...], -aim_ref[...], d, adjoint=True)

        @pl.when(d == 0)
        def _():
            dar_ref[...], dai_ref[...] = _dlam(gr_ref, gi_ref, sr_ref, si_ref, reverse=False)
            du_ref[...] = dus_ref[...]

        @pl.when(d == 1)
        def _():
            dar_ref[...], dai_ref[...] = _dlam(gr_ref, gi_ref, sr_ref, si_ref, reverse=True)

        grb = gr_ref[...].astype(BF16)
        gib = gi_ref[...].astype(BF16)
        du_ref[...] += _dot_nt(grb, bre_ref[...]) + _dot_nt(gib, bim_ref[...])
        ub = u_ref[...].astype(BF16)
        dbr_ref[...] = _dot_tn(ub, grb)
        dbi_ref[...] = _dot_tn(ub, gib)

    chan, _, state, bmat, cmat, avec = _s5_specs()
    return pl.pallas_call(
        body, grid=(S5_CHUNKS, 2), in_specs=[chan, chan, chan, state, state, bmat, bmat, avec, avec, cmat, cmat],
        out_specs=[chan, bmat, bmat, cmat, cmat, avec, avec],
        out_shape=[SDS((SEQ, S5_WIDTH), F32)] + [SDS((2, S5_CHUNKS, CH_W, ST_W), F32)] * 2
                  + [SDS((2, S5_CHUNKS, ST_W, CH_W), F32)] * 2 + [SDS((2, S5_CHUNKS, 1, ST_W), F32)] * 2,
        scratch_shapes=[pltpu.VMEM((SEQ, ST_W), F32), pltpu.VMEM((SEQ, ST_W), F32)],
        name="s5_scan_bwd", compiler_params=_params(("parallel", "arbitrary")))(dy, du_skip, u, sr, si, bre, bim, are, aim, cre, cim)


_GELU_K = math.sqrt(2.0 / math.pi)
_GELU_C = 0.044715


def _gelu(x):
    t = jnp.tanh(_GELU_K * (x + _GELU_C * x * x * x))
    return 0.5 * x * (1.0 + t), t


def _s5_glu_fwd(u, y2, dskip, wglu, bglu):
    def body(u_ref, y0_ref, y1_ref, d_ref, w_ref, b_ref, o_ref, yp_ref):
        ypre = u_ref[...] * d_ref[...] + y0_ref[...] + y1_ref[...]
        yp_ref[...] = ypre
        y, _ = _gelu(ypre)
        z = _dot(y.astype(BF16), w_ref[...]) + b_ref[...]
        o_ref[...] = y * jax.nn.sigmoid(z)

    row = _row_spec(S5_WIDTH)
    vec = _fix_spec((1, S5_WIDTH))
    dir0 = pl.BlockSpec((None, ROW_TILE, S5_WIDTH), lambda i: (0, i, 0))
    dir1 = pl.BlockSpec((None, ROW_TILE, S5_WIDTH), lambda i: (1, i, 0))
    return pl.pallas_call(
        body, grid=(N_ROW_TILES,), in_specs=[row, dir0, dir1, vec, _fix_spec((S5_WIDTH, S5_WIDTH)), vec],
        out_specs=[row, row], out_shape=[SDS((SEQ, S5_WIDTH), F32)] * 2, name="s5_glu_fwd",
        compiler_params=_params(("parallel",)))(u, y2, y2, dskip, wglu, bglu)


def _s5_glu_bwd(do, ypre, u, dskip, wglu, bglu):
    def body(do_ref, yp_ref, u_ref, d_ref, w_ref, b_ref, dyp_ref, du_ref, dw_ref, db_ref, dd_ref):
        i = pl.program_id(0)
        ypre = yp_ref[...]
        y, t = _gelu(ypre)
        yb = y.astype(BF16)
        sg = jax.nn.sigmoid(_dot(yb, w_ref[...]) + b_ref[...])
        dov = do_ref[...]
        dz = dov * y * sg * (1.0 - sg)
        dzb = dz.astype(BF16)
        dy = dov * sg + _dot_nt(dzb, w_ref[...])
        dgelu = 0.5 * (1.0 + t) + 0.5 * ypre * (1.0 - t * t) * _GELU_K * (1.0 + 3.0 * _GELU_C * ypre * ypre)
        dyp = dy * dgelu
        dyp_ref[...] = dyp
        uv = u_ref[...]
        du_ref[...] = dyp * d_ref[...]

        @pl.when(i == 0)
        def _():
            dw_ref[...] = jnp.zeros_like(dw_ref)
            db_ref[...] = jnp.zeros_like(db_ref)
            dd_ref[...] = jnp.zeros_like(dd_ref)

        dw_ref[...] += _dot_tn(yb, dzb)
        db_ref[...] += jnp.sum(dz, axis=0, keepdims=True)
        dd_ref[...] += jnp.sum(dyp * uv, axis=0, keepdims=True)

    row = _row_spec(S5_WIDTH)
    vec = _fix_spec((1, S5_WIDTH))
    mat = _fix_spec((S5_WIDTH, S5_WIDTH))
    return pl.pallas_call(
        body, grid=(N_ROW_TILES,), in_specs=[row, row, row, vec, mat, vec], out_specs=[row, row, mat, vec, vec],
        out_shape=[SDS((SEQ, S5_WIDTH), F32)] * 2 + [SDS((S5_WIDTH, S5_WIDTH), F32), SDS((1, S5_WIDTH), F32), SDS((1, S5_WIDTH), F32)],
        name="s5_glu_bwd", compiler_params=_params(("arbitrary",)))(do, ypre, u, dskip, wglu, bglu)


def _heads_side_by_side(o_ref):
    return jnp.concatenate([o_ref[h] for h in range(HEADS)], axis=-1)


def _mix_out_fwd(ona, os5, g_na, g_s5, wout):
    def body(a_ref, s_ref, ga_ref, gs_ref, w_ref, o_ref):
        av, sv = _heads_side_by_side(a_ref), s_ref[...]
        ca = (av * _rstd(av) * ga_ref[...]).astype(BF16)
        cs = (sv * _rstd(sv) * gs_ref[...]).astype(BF16)
        o_ref[...] = _dot(ca, w_ref[0:NA_WIDTH, :]) + _dot(cs, w_ref[NA_WIDTH:, :])

    row = _row_spec(NA_WIDTH)
    vec = _fix_spec((1, NA_WIDTH))
    heads = pl.BlockSpec((HEADS, ROW_TILE, HEAD_DIM), lambda i: (0, i, 0))
    return pl.pallas_call(
        body, grid=(N_ROW_TILES,), in_specs=[heads, row, vec, vec, _fix_spec((D_MODEL, D_MODEL))],
        out_specs=_row_spec(D_MODEL), out_shape=SDS((SEQ, D_MODEL), F32), name="mix_out_fwd",
        compiler_params=_params(("parallel",)))(ona, os5, g_na, g_s5, wout)


def _mix_out_bwd(dmix, ona, os5, g_na, g_s5, wout):
    def body(dm_ref, a_ref, s_ref, ga_ref, gs_ref, w_ref, da_ref, ds_ref, dw_ref, dga_ref, dgs_ref):
        i = pl.program_id(0)
        dm = dm_ref[...]
        av, sv = _heads_side_by_side(a_ref), s_ref[...]
        ra, rs = _rstd(av), _rstd(sv)
        ga, gs = ga_ref[...], gs_ref[...]
        ca = (av * ra * ga).astype(BF16)
        cs = (sv * rs * gs).astype(BF16)
        dca = _dot_nt(dm, w_ref[0:NA_WIDTH, :])
        dcs = _dot_nt(dm, w_ref[NA_WIDTH:, :])
        da, dga = _rms_bwd(av, ra, ga, dca)
        ds, dgs = _rms_bwd(sv, rs, gs, dcs)
        for h in range(HEADS):
            da_ref[h] = da[:, h * HEAD_DIM:(h + 1) * HEAD_DIM]
        ds_ref[...] = ds

        @pl.when(i == 0)
        def _():
            dw_ref[...] = jnp.zeros_like(dw_ref)
            dga_ref[...] = jnp.zeros_like(dga_ref)
            dgs_ref[...] = jnp.zeros_like(dgs_ref)

        dw_ref[0:NA_WIDTH, :] += _dot_tn(ca, dm)
        dw_ref[NA_WIDTH:, :] += _dot_tn(cs, dm)
        dga_ref[...] += jnp.sum(dga, axis=0, keepdims=True)
        dgs_ref[...] += jnp.sum(dgs, axis=0, keepdims=True)

    row = _row_spec(NA_WIDTH)
    vec = _fix_spec((1, NA_WIDTH))
    mat = _fix_spec((D_MODEL, D_MODEL))
    heads = pl.BlockSpec((HEADS, ROW_TILE, HEAD_DIM), lambda i: (0, i, 0))
    return pl.pallas_call(
        body, grid=(N_ROW_TILES,), in_specs=[_row_spec(D_MODEL), heads, row, vec, vec, mat],
        out_specs=[heads, row, mat, vec, vec],
        out_shape=[SDS((HEADS, SEQ, HEAD_DIM), F32), SDS((SEQ, NA_WIDTH), F32), SDS((D_MODEL, D_MODEL), F32),
                   SDS((1, NA_WIDTH), F32), SDS((1, NA_WIDTH), F32)],
        name="mix_out_bwd", compiler_params=_params(("arbitrary",)))(dmix, ona, os5, g_na, g_s5, wout)


def _me():
    x, y, c = lax.axis_index("x"), lax.axis_index("y"), lax.axis_index("c")
    return x, y, c, 4 * x + 2 * y + c


def _peer(k):
    x, y, c, _ = _me()
    px = 1 - x if (k >> 2) & 1 else x
    py = 1 - y if (k >> 1) & 1 else y
    pc = 1 - c if k & 1 else c
    return (px, py, pc), 4 * px + 2 * py + pc


ALL_PEERS = (1, 2, 3, 4, 5, 6, 7)
CHIP_PEERS = (2, 4, 6)
SIBLING = 1


def _slot8(pos):
    return 4 * pos[0] + 2 * pos[1] + pos[2]


def _slot4(pos):
    return 2 * pos[0] + pos[1]


def _exchange(arrays, gather, name, after=()):
    n, n_after = len(arrays), len(after)

    def body(*refs):
        ins, outs = refs[:n], refs[n + n_after:2 * n + n_after]
        token = refs[2 * n + n_after]
        send_sems, recv_sems, local_sems = refs[2 * n + n_after + 1:]
        token[...] = jnp.zeros_like(token)
        _, _, _, me = _me()
        started = []
        for a in range(n):
            src_mine = ins[a] if gather else ins[a].at[me]
            local = pltpu.make_async_copy(src_mine, outs[a].at[me], local_sems.at[a])
            local.start()
            started.append(local)
        sends = []
        for k in range(1, N_DEV):
            peer, peer_idx = _peer(k)
            for a in range(n):
                src = ins[a] if gather else ins[a].at[peer_idx]
                cp = pltpu.make_async_remote_copy(src_ref=src, dst_ref=outs[a].at[me], send_sem=send_sems.at[a, k - 1],
                                                  recv_sem=recv_sems.at[a, k - 1], device_id=peer, device_id_type=MESH)
                cp.start()
                sends.append(cp)
        for k in range(1, N_DEV):
            peer, peer_idx = _peer(k)
            for a in range(n):
                src = ins[a] if gather else ins[a].at[peer_idx]
                pltpu.make_async_remote_copy(src_ref=src, dst_ref=outs[a].at[peer_idx], send_sem=send_sems.at[a, k - 1],
                                             recv_sem=recv_sems.at[a, k - 1], device_id=peer, device_id_type=MESH).wait_recv()
        for cp in sends:
            cp.wait_send()
        for local in started:
            local.wait()

    hbm = pl.BlockSpec(memory_space=pltpu.HBM)
    out_shape = [SDS((N_DEV,) + tuple(a.shape), a.dtype) if gather else SDS(a.shape, a.dtype) for a in arrays]
    out = pl.pallas_call(
        body, in_specs=[hbm] * n + [pl.BlockSpec(memory_space=pl.ANY)] * n_after,
        out_specs=[hbm] * n + [pl.BlockSpec(memory_space=pltpu.VMEM)], out_shape=out_shape + [SDS((8, 128), F32)],
        scratch_shapes=[pltpu.SemaphoreType.DMA((n, N_DEV - 1)), pltpu.SemaphoreType.DMA((n, N_DEV - 1)),
                        pltpu.SemaphoreType.DMA((n,))],
        name=name)(*arrays, *after)
    return list(out[:n]), out[n]


_HBM = pl.BlockSpec(memory_space=pltpu.HBM)
_SEM = pl.BlockSpec(memory_space=pltpu.SEMAPHORE)
_EFFECT = pltpu.SideEffectType.DATAFLOW_SIDE_EFFECTING


def _land_shape(a, gather):
    return (N_DEV,) + tuple(a.shape) if gather else tuple(a.shape)


def _place_own(arrays, gather, name, slot=_slot8):
    n = len(arrays)
    me = slot(_me()[:3])

    def body(me_ref, *refs):
        for a in range(n):
            refs[n + a][...] = refs[a][...]

    def own_slot(a):
        zeros = (0,) * (a.ndim - (0 if gather else 1))
        return lambda i, me_ref: (me_ref[0],) + zeros

    def whole(a):
        return lambda i, me_ref: (0,) * a.ndim

    in_specs = [pl.BlockSpec(a.shape, whole(a)) if gather else pl.BlockSpec((None,) + a.shape[1:], own_slot(a)) for a in arrays]
    out_specs = [pl.BlockSpec((None,) + (a.shape if gather else a.shape[1:]), own_slot(a)) for a in arrays]
    return pl.pallas_call(
        body, grid_spec=pltpu.PrefetchScalarGridSpec(num_scalar_prefetch=1, grid=(1,), in_specs=in_specs, out_specs=out_specs),
        out_shape=[SDS(_land_shape(a, gather), a.dtype) for a in arrays], name=name,
        compiler_params=_params(("arbitrary",)))(me.reshape(1).astype(jnp.int32), *arrays)


def _exchange_start(arrays, lands, gather, name, peers=ALL_PEERS, slot=_slot8):
    n = len(arrays)

    def body(*refs):
        ins, lnd = refs[:n], refs[n:2 * n]
        send_sems, recv_sems = refs[2 * n], refs[2 * n + 1]
        token = refs[-1]
        me = slot(_me()[:3])
        for i, k in enumerate(peers):
            peer, _ = _peer(k)
            for a in range(n):
                src = ins[a] if gather else ins[a].at[slot(peer)]
                s = a * len(peers) + i
                pltpu.make_async_remote_copy(src_ref=src, dst_ref=lnd[a].at[me], send_sem=send_sems.at[s],
                                             recv_sem=recv_sems.at[s], device_id=peer, device_id_type=MESH).start()
        token[...] = jnp.zeros_like(token)

    sems = pltpu.SemaphoreType.DMA((n * len(peers),))
    out = pl.pallas_call(
        body, name=name, in_specs=[_HBM] * (2 * n),
        out_shape=(sems, sems) + tuple(pltpu.HBM(a.shape, a.dtype) for a in list(arrays) + list(lands)) + (SDS((8, 128), F32),),
        out_specs=(_SEM, _SEM) + (_HBM,) * (2 * n) + (pl.BlockSpec(memory_space=pltpu.VMEM),),
        input_output_aliases={i: 2 + i for i in range(2 * n)},
        compiler_params=pltpu.CompilerParams(has_side_effects=_EFFECT),
    )(*[pltpu.with_memory_space_constraint(a, pltpu.HBM) for a in list(arrays) + list(lands)])
    return out[0], out[1], list(out[2:2 + n]), list(out[2 + n:2 + 2 * n]), out[-1]


def _exchange_wait(send_sems, recv_sems, arrays, lands, after, gather, name, peers=ALL_PEERS, slot=_slot8):
    n = len(arrays)

    def body(*refs):
        ins, lnd = refs[:n], refs[n:2 * n]
        send_sems, recv_sems = refs[2 * n], refs[2 * n + 1]
        for i, k in enumerate(peers):
            peer, _ = _peer(k)
            for a in range(n):
                src = ins[a] if gather else ins[a].at[slot(peer)]
                s = a * len(peers) + i
                cp = pltpu.make_async_remote_copy(src_ref=src, dst_ref=lnd[a].at[slot(peer)], send_sem=send_sems.at[s],
                                                  recv_sem=recv_sems.at[s], device_id=peer, device_id_type=MESH)
                cp.wait_send()
                cp.wait_recv()

        refs[-1][...] = jnp.zeros_like(refs[-1])

    after = list(after) if isinstance(after, (list, tuple)) else [after]
    out = pl.pallas_call(
        body, name=name, in_specs=[_HBM] * (2 * n) + [_SEM, _SEM] + [pl.BlockSpec(memory_space=pl.ANY)] * len(after),
        out_shape=tuple(pltpu.HBM(a.shape, a.dtype) for a in list(arrays) + list(lands)) + (SDS((8, 128), F32),),
        out_specs=(_HBM,) * (2 * n) + (pl.BlockSpec(memory_space=pltpu.VMEM),), input_output_aliases={i: i for i in range(2 * n)},
        compiler_params=pltpu.CompilerParams(has_side_effects=_EFFECT),
    )(*arrays, *lands, send_sems, recv_sems, *after)
    return list(out[n:2 * n]), out[-1]


def _forward_sibling(lands, name):
    n = len(lands)

    def body(*refs):
        outs = refs[n:2 * n]
        send_sems, recv_sems = refs[2 * n:]
        x, y, c, _ = _me()
        sends = []
        for i, k in enumerate(CHIP_PEERS):
            peer, _ = _peer(k)
            for a in range(n):
                rows = outs[a].at[_slot8(peer)]
                cp = pltpu.make_async_remote_copy(src_ref=rows, dst_ref=rows, send_sem=send_sems.at[a, i], recv_sem=recv_sems.at[a, i],
                                                  device_id=(x, y, 1 - c), device_id_type=MESH)
                cp.start()
                sends.append(cp)
        for i, k in enumerate(CHIP_PEERS):
            (px, py, pc), _ = _peer(k)
            for a in range(n):
                rows = outs[a].at[_slot8((px, py, 1 - pc))]
                pltpu.make_async_remote_copy(src_ref=rows, dst_ref=rows, send_sem=send_sems.at[a, i], recv_sem=recv_sems.at[a, i],
                                             device_id=(x, y, 1 - c), device_id_type=MESH).wait_recv()
        for cp in sends:
            cp.wait_send()

    return pl.pallas_call(
        body, in_specs=[_HBM] * n, out_specs=[_HBM] * n, out_shape=[SDS(a.shape, a.dtype) for a in lands],
        input_output_aliases={i: i for i in range(n)},
        scratch_shapes=[pltpu.SemaphoreType.DMA((n, len(CHIP_PEERS))), pltpu.SemaphoreType.DMA((n, len(CHIP_PEERS)))],
        name=name)(*lands)


def _swap_sibling(arrays, name):
    n = len(arrays)
    chips = N_DEV // 2

    def body(*refs):
        ins, outs = refs[:n], refs[n:2 * n]
        send_sems, recv_sems = refs[2 * n:]
        x, y, c, _ = _me()
        sends = []
        for q in range(chips):
            for a in range(n):
                cp = pltpu.make_async_remote_copy(src_ref=ins[a].at[q, 1 - c], dst_ref=outs[a].at[q], send_sem=send_sems.at[a, q],
                                                  recv_sem=recv_sems.at[a, q], device_id=(x, y, 1 - c), device_id_type=MESH)
                cp.start()
                sends.append(cp)
        for cp in sends:
            cp.wait_recv()
        for cp in sends:
            cp.wait_send()

    return pl.pallas_call(
        body, in_specs=[_HBM] * n, out_specs=[_HBM] * n, out_shape=[SDS((chips,) + a.shape[2:], a.dtype) for a in arrays],
        scratch_shapes=[pltpu.SemaphoreType.DMA((n, chips)), pltpu.SemaphoreType.DMA((n, chips))], name=name)(*arrays)


def _sum_pairs(mine, theirs, name):
    chips, _, rows, cols = mine.shape
    c = lax.axis_index("c")

    def body(c_ref, a_ref, b_ref, o_ref):
        o_ref[...] = (a_ref[...].astype(F32) + b_ref[...].astype(F32)).astype(o_ref.dtype)

    return pl.pallas_call(
        body, grid_spec=pltpu.PrefetchScalarGridSpec(
            num_scalar_prefetch=1, grid=(chips,),
            in_specs=[pl.BlockSpec((None, None, rows, cols), lambda q, c_ref: (q, c_ref[0], 0, 0)),
                      pl.BlockSpec((None, rows, cols), lambda q, c_ref: (q, 0, 0))],
            out_specs=pl.BlockSpec((None, rows, cols), lambda q, c_ref: (q, 0, 0))),
        out_shape=SDS((chips, rows, cols), mine.dtype), name=name,
        compiler_params=_params(("parallel",)))(c.reshape(1).astype(jnp.int32), mine, theirs)


def _adamw_math(w, g, m, v):
    m = ADAM_B1 * m + (1.0 - ADAM_B1) * g
    v = ADAM_B2 * v + (1.0 - ADAM_B2) * (g * g)
    m_hat = m / (1.0 - ADAM_B1 ** ADAM_STEP)
    v_hat = v / (1.0 - ADAM_B2 ** ADAM_STEP)
    delta = -ADAM_LR * (m_hat / (jnp.sqrt(v_hat) + ADAM_EPS) + ADAM_WD * w)
    return delta, m, v


def _adamw(w, m, v, pieces, name):
    rows, cols = w.shape[-2:]
    lead = w.ndim - 2
    tile = rows
    for cand in (256, 176, 128, 64, 16):
        if rows > cand and rows % cand == 0:
            tile = cand
            break

    def body(w_ref, m_ref, v_ref, p_ref, g_ref, d_ref, mo_ref, vo_ref):
        g = _sum_pieces(p_ref)
        g_ref[...] = g
        d_ref[...], mo_ref[...], vo_ref[...] = _adamw_math(w_ref[...], g, m_ref[...], v_ref[...])

    blk = pl.BlockSpec((None,) * lead + (tile, cols), lambda i: (0,) * lead + (i, 0))
    return pl.pallas_call(
        body, grid=(rows // tile,), in_specs=[blk, blk, blk, pl.BlockSpec((pieces.shape[0], tile, cols), lambda i: (0, i, 0))],
        out_specs=[blk] * 4, out_shape=[SDS(w.shape, F32)] * 4, name=name,
        compiler_params=_params(("parallel",)))(w, m, v, pieces)


def _sum_pieces(p_ref):
    g = p_ref[0].astype(F32)
    for p in range(1, p_ref.shape[0]):
        g = g + p_ref[p].astype(F32)
    return g


def _adamw_s5_mat(w, m, v, g, name):
    _, ndir, groups, b, c = w.shape
    per_dir = groups // 8

    def body(w_ref, m_ref, v_ref, g_ref, d_ref, mo_ref, vo_ref):
        d_ref[...], mo_ref[...], vo_ref[...] = _adamw_math(w_ref[...], g_ref[...], m_ref[...], v_ref[...])

    blk = pl.BlockSpec((None, None, 8, b, c), lambda i: (0, i // per_dir, i % per_dir, 0, 0))
    return pl.pallas_call(
        body, grid=(ndir * per_dir,), in_specs=[blk] * 4, out_specs=[blk] * 3, out_shape=[SDS(w.shape, F32)] * 3, name=name,
        compiler_params=_params(("parallel",)))(w, m, v, g)


VEC_ROWS = ['ffn1_pre_g', 'ffn1_post_g', 'mix_pre_g', 'mix_post_g', 'ffn2_pre_g', 'ffn2_post_g', 'final_g',
            ('na_out_g', 's5_out_g'), ('s5_d', 's5_b_glu')]
VEC_NAMES = [n for row in VEC_ROWS for n in ((row,) if isinstance(row, str) else row)]
VEC_PACK_ROWS = 16
LOSS_ROW = len(VEC_ROWS)


def _pack_vectors(grads, loss8):
    def body(*refs):
        o_ref = refs[-1]
        o_ref[...] = jnp.zeros_like(o_ref)
        o_ref[LOSS_ROW:LOSS_ROW + 1, 0:128] = refs[-2][0:1, :]
        k = 0
        for i, row in enumerate(VEC_ROWS):
            if isinstance(row, str):
                o_ref[i:i + 1, :] = refs[k][...]
                k += 1
            else:
                o_ref[i:i + 1, 0:NA_WIDTH] = refs[k][...]
                o_ref[i:i + 1, NA_WIDTH:] = refs[k + 1][...]
                k += 2

    return pl.pallas_call(body, out_shape=SDS((VEC_PACK_ROWS, D_MODEL), F32), name="pack_vectors",
                          compiler_params=_params())(*[grads[n] for n in VEC_NAMES], loss8)


def _sum8(pieces, name):
    def body(p_ref, o_ref):
        o_ref[...] = _sum_pieces(p_ref)

    return pl.pallas_call(body, out_shape=SDS(pieces.shape[1:], F32), name=name, compiler_params=_params())(pieces)


def _adamw_small(packed8, vec_wmv, others):
    n_vec, n_oth = len(VEC_NAMES), len(others)

    def body(*refs):
        p_ref = refs[0]
        ins = refs[1:1 + 3 * n_vec + 4 * n_oth]
        outs = refs[1 + 3 * n_vec + 4 * n_oth:]
        gsum = _sum_pieces(p_ref)
        outs[-1][...] = gsum[LOSS_ROW:LOSS_ROW + 1, 0:128]
        k = 0
        for i, row in enumerate(VEC_ROWS):
            parts = [(row, gsum[i:i + 1, :])] if isinstance(row, str) else \
                [(row[0], gsum[i:i + 1, 0:NA_WIDTH]), (row[1], gsum[i:i + 1, NA_WIDTH:])]
            for _, g in parts:
                w_ref, m_ref, v_ref = ins[3 * k:3 * k + 3]
                outs[4 * k][...] = g
                outs[4 * k + 1][...], outs[4 * k + 2][...], outs[4 * k + 3][...] = _adamw_math(w_ref[...], g, m_ref[...], v_ref[...])
                k += 1
        for j in range(n_oth):
            w_ref, m_ref, v_ref, g_ref = ins[3 * n_vec + 4 * j:3 * n_vec + 4 * j + 4]
            g = _sum_pieces(g_ref)
            g = g[tuple(slice(0, s) for s in w_ref.shape[1:])].reshape(w_ref.shape)
            o = outs[4 * (n_vec + j):4 * (n_vec + j) + 4]
            o[0][...] = g
            o[1][...], o[2][...], o[3][...] = _adamw_math(w_ref[...], g, m_ref[...], v_ref[...])

    args, out_shape = [packed8], []
    for w, m, v in vec_wmv:
        args += [w, m, v]
        out_shape += [SDS(w.shape, F32)] * 4
    for w, m, v, g in others:
        args += [w, m, v, g]
        out_shape += [SDS(w.shape, F32)] * 4
    out_shape += [SDS((1, 128), F32)]
    return pl.pallas_call(body, out_shape=out_shape, name="adamw_small", compiler_params=_params())(*args)


def _perm_rows(x):
    return x.reshape(SCAN_BLOCKS, SCAN_T, x.shape[-1]).transpose(1, 0, 2).reshape(SEQ, x.shape[-1])


def _unperm_rows(x):
    return x.reshape(SCAN_T, SCAN_BLOCKS, x.shape[-1]).transpose(1, 0, 2).reshape(SEQ, x.shape[-1])


def _block_diag(x):
    eye = np.eye(8, dtype=bool)[None, None, :, None, :, None]
    full = jnp.where(eye, x[:, :, :, :, None, :], 0.0)
    return full.reshape(2, S5_CHUNKS, 8 * x.shape[3], 8 * x.shape[4])


def _diag_blocks(x, r, c):
    x6 = x.reshape(2, S5_CHUNKS, 8, r, 8, c)
    return jnp.stack([x6[:, :, g, :, g, :] for g in range(8)], axis=2)


STORED_SWAPPED = {"ffn1_w_gate": (1, 2), "ffn1_w_up": (1, 2), "ffn2_w_gate": (1, 2), "ffn2_w_up": (1, 2),
                  "s5_b_re": (3, 4), "s5_b_im": (3, 4)}


def _stored(name, x):
    return jnp.swapaxes(x, *STORED_SWAPPED[name]) if name in STORED_SWAPPED else x


def _dep(x, token):
    return x if token is None else x + token


def _local_step(x, target, get_w, small, emit):
    bias = _rpb_expand(small["na_rpb"][0])
    lr = small["s5_lam_re"].reshape(64, S5_STATE)
    li = small["s5_lam_im"].reshape(64, S5_STATE)
    logdt = small["s5_log_dt"].reshape(64, 1)
    b_t = [_stored(n, small[n]).reshape(64, S5_GROUP, S5_STATE) for n in ("s5_b_re", "s5_b_im")]
    lbr, lbi, bbr, bbi = _s5_prep(lr, li, logdt, b_t[0], b_t[1])
    are = lbr.reshape(2, S5_CHUNKS, 1, ST_W)
    aim = lbi.reshape(2, S5_CHUNKS, 1, ST_W)
    bre = _block_diag(bbr.reshape(2, S5_CHUNKS, 8, S5_GROUP, S5_STATE)).astype(BF16)
    bim = _block_diag(bbi.reshape(2, S5_CHUNKS, 8, S5_GROUP, S5_STATE)).astype(BF16)
    c_t = [small[n].reshape(2, S5_CHUNKS, 8, S5_GROUP, S5_STATE).transpose(0, 1, 2, 4, 3) for n in ("s5_c_re", "s5_c_im")]
    cre = _block_diag(c_t[0]).astype(BF16)
    cim = _block_diag(c_t[1]).astype(BF16)
    tgt = jnp.concatenate([jnp.zeros((N_META, D_MODEL), F32), target], axis=0)

    h0 = jnp.concatenate([get_w("meta", None)["meta_tokens"], x], axis=0)
    a1 = _prenorm(h0, small["ffn1_pre_g"])
    wts = dict(get_w("ffn1", [bias, are, aim, bre, bim, cre, cim, tgt, a1]))
    gate1, up1, f1 = _ffn_fwd(a1, wts["ffn1_w_gate"], wts["ffn1_w_up"], wts["ffn1_w_down"], "ffn1_fwd",
                              after=wts.get("tokens", ()))
    h1, a2 = _post_pre(f1, h0, small["ffn1_post_g"], small["mix_pre_g"], 0.5, "post_pre1")
    wts.update(get_w("w_in", a2))
    qkv = _proj_heads(a2, wts["w_in"])
    u = _proj_u(a2, wts["w_in"])
    ona = _na_fwd(qkv, bias)
    u_p = _perm_rows(u)
    sr, si, y2 = _s5_scan_fwd(u_p, bre, bim, are, aim, cre, cim)
    wts.update(get_w("mix", y2))
    os5_p, ypre_p = _s5_glu_fwd(u_p, y2, small["s5_d"], wts["s5_w_glu"], small["s5_b_glu"])
    os5 = _unperm_rows(os5_p)

    mix = _mix_out_fwd(ona, os5, small["na_out_g"], small["s5_out_g"], wts["w_out"])
    h2, a3 = _post_pre(mix, h1, small["mix_post_g"], small["ffn2_pre_g"], 1.0, "post_pre2")
    wts.update(get_w("ffn2", a3))
    gate2, up2, f2 = _ffn_fwd(a3, wts["ffn2_w_gate"], wts["ffn2_w_up"], wts["ffn2_w_down"], "ffn2_fwd")
    loss8, dh3, df2, g_final, g_ffn2_post = _final_loss(f2, h2, small["ffn2_post_g"], small["final_g"], tgt)

    da3, dwg2, dwu2, dwd2 = _ffn_bwd(df2, a3, gate2, up2, wts["ffn2_w_gate"], wts["ffn2_w_up"], wts["ffn2_w_down"], "ffn2_bwd")
    tok = emit("ffn2", {"ffn2_w_gate": dwg2, "ffn2_w_up": dwu2, "ffn2_w_down": dwd2})
    dh2, dmix, g_ffn2_pre, g_mix_post = _bwd_pre_post(da3, h2, _dep(small["ffn2_pre_g"], tok), dh3, mix, small["mix_post_g"], 1.0,
                                                      "bwd_pre_post2")
    dona, dos5, dwout, g_na_out, g_s5_out = _mix_out_bwd(dmix, ona, os5, small["na_out_g"], small["s5_out_g"], wts["w_out"])

    dypre_p, du_skip_p, dwglu, g_b_glu, g_s5_d = _s5_glu_bwd(_perm_rows(dos5), ypre_p, u_p, small["s5_d"], wts["s5_w_glu"],
                                                             small["s5_b_glu"])
    tok = emit("mix", {"s5_w_glu": dwglu.reshape(N_DEV, S5_WIDTH // N_DEV, S5_WIDTH).astype(BF16),
                       "w_out": dwout.reshape(N_DEV, D_MODEL // N_DEV, D_MODEL).astype(BF16)})
    du_p, dbr, dbi, dcr, dci, dar, dai = _s5_scan_bwd(dypre_p, du_skip_p, u_p, sr, si, bre, bim, _dep(are, tok), aim, cre, cim)
    du = _unperm_rows(du_p)
    dbbr = _diag_blocks(dbr, S5_GROUP, S5_STATE).reshape(64, S5_GROUP, S5_STATE)
    dbbi = _diag_blocks(dbi, S5_GROUP, S5_STATE).reshape(64, S5_GROUP, S5_STATE)
    g_lr, g_li, g_dt, g_br, g_bi = _s5_prep_bwd(lr, li, logdt, b_t[0], b_t[1], dar.reshape(64, S5_STATE),
                                                dai.reshape(64, S5_STATE), dbbr, dbbi)
    g_c = [_diag_blocks(d, S5_STATE, S5_GROUP).transpose(0, 1, 2, 4, 3).reshape(2 * S5_GROUPS, S5_GROUP, S5_STATE)
           for d in (dcr, dci)]

    dq, dk, dv, dbias = _na_bwd(qkv, bias, dona)
    g_rpb = _rpb_reduce(dbias)
    dense = jnp.stack([g.reshape(2 * S5_GROUPS, S5_STATE * S5_GROUP) for g in (g_br, g_bi, *g_c)])
    tok = emit("small", {"dense": dense, "na_rpb": g_rpb,
                         "s5_lam_re": g_lr.reshape(2, S5_GROUPS, S5_STATE), "s5_lam_im": g_li.reshape(2, S5_GROUPS, S5_STATE),
                         "s5_log_dt": g_dt.reshape(2, S5_GROUPS)})
    da2, dwin = _proj_bwd(dq, dk, dv, du, a2, wts["w_in"])
    tok2 = emit("w_in", {"w_in": dwin})
    tok = tok if tok2 is None else tok + tok2
    dh1, df1, g_mix_pre, g_ffn1_post = _bwd_pre_post(da2, h1, _dep(small["mix_pre_g"], tok), dh2, f1, small["ffn1_post_g"], 0.5,
                                                     "bwd_pre_post1")
    da1, dwg1, dwu1, dwd1 = _ffn_bwd(df1, a1, gate1, up1, wts["ffn1_w_gate"], wts["ffn1_w_up"], wts["ffn1_w_down"], "ffn1_bwd")
    emit("ffn1", {"ffn1_w_gate": dwg1, "ffn1_w_up": dwu1, "ffn1_w_down": dwd1})
    dh0, g_ffn1_pre = _bwd_pre_only(da1, h0, small["ffn1_pre_g"], dh1)

    vec_g = {
        "ffn1_pre_g": g_ffn1_pre, "ffn1_post_g": g_ffn1_post, "mix_pre_g": g_mix_pre, "s5_d": g_s5_d, "s5_b_glu": g_b_glu,
        "na_out_g": g_na_out, "s5_out_g": g_s5_out, "mix_post_g": g_mix_post,
        "ffn2_pre_g": g_ffn2_pre, "ffn2_post_g": g_ffn2_post, "final_g": g_final,
    }
    return loss8, dh0[N_META:], dh0[:N_META], vec_g


WEIGHT_NAMES = ['meta_tokens', 'ffn1_pre_g', 'ffn1_post_g', 'ffn1_w_gate', 'ffn1_w_up', 'ffn1_w_down', 'mix_pre_g', 'w_in',
                'na_rpb', 's5_lam_re', 's5_lam_im', 's5_log_dt', 's5_b_re', 's5_b_im', 's5_c_re', 's5_c_im', 's5_d',
                's5_w_glu', 's5_b_glu', 'na_out_g', 's5_out_g', 'w_out', 'mix_post_g', 'ffn2_pre_g', 'ffn2_post_g',
                'ffn2_w_gate', 'ffn2_w_up', 'ffn2_w_down', 'final_g']
BIG_NAMES = ['ffn1_w_gate', 'ffn1_w_up', 'ffn1_w_down', 'w_in', 's5_w_glu', 'w_out', 'ffn2_w_gate', 'ffn2_w_up', 'ffn2_w_down']
SMALL_NAMES = [n for n in WEIGHT_NAMES if n not in BIG_NAMES and n != 'meta_tokens']
WHOLE_NAMES = ['na_rpb', 's5_lam_re', 's5_lam_im', 's5_log_dt']
LEAD_NAMES = ['s5_b_re', 's5_b_im', 's5_c_re', 's5_c_im']


def kernel(x, meta_tokens, ffn1_pre_g, ffn1_post_g, ffn1_w_gate, ffn1_w_up, ffn1_w_down, mix_pre_g, w_in, na_rpb, s5_lam_re, s5_lam_im, s5_log_dt, s5_b_re, s5_b_im, s5_c_re, s5_c_im, s5_d, s5_w_glu, s5_b_glu, na_out_g, s5_out_g, w_out, mix_post_g, ffn2_pre_g, ffn2_post_g, ffn2_w_gate, ffn2_w_up, ffn2_w_down, final_g, loss_target, m_meta_tokens, m_ffn1_pre_g, m_ffn1_post_g, m_ffn1_w_gate, m_ffn1_w_up, m_ffn1_w_down, m_mix_pre_g, m_w_in, m_na_rpb, m_s5_lam_re, m_s5_lam_im, m_s5_log_dt, m_s5_b_re, m_s5_b_im, m_s5_c_re, m_s5_c_im, m_s5_d, m_s5_w_glu, m_s5_b_glu, m_na_out_g, m_s5_out_g, m_w_out, m_mix_post_g, m_ffn2_pre_g, m_ffn2_post_g, m_ffn2_w_gate, m_ffn2_w_up, m_ffn2_w_down, m_final_g, v_meta_tokens, v_ffn1_pre_g, v_ffn1_post_g, v_ffn1_w_gate, v_ffn1_w_up, v_ffn1_w_down, v_mix_pre_g, v_w_in, v_na_rpb, v_s5_lam_re, v_s5_lam_im, v_s5_log_dt, v_s5_b_re, v_s5_b_im, v_s5_c_re, v_s5_c_im, v_s5_d, v_s5_w_glu, v_s5_b_glu, v_na_out_g, v_s5_out_g, v_w_out, v_mix_post_g, v_ffn2_pre_g, v_ffn2_post_g, v_ffn2_w_gate, v_ffn2_w_up, v_ffn2_w_down, v_final_g):
    args = dict(locals())
    w = {n: args[n] for n in WEIGHT_NAMES}
    m = {n: args["m_" + n] for n in WEIGHT_NAMES}
    v = {n: args["v_" + n] for n in WEIGHT_NAMES}

    small = {n: w[n] for n in SMALL_NAMES}

    pending = {}

    def start(group, names, arrays, gather, peers=ALL_PEERS, slot=_slot8):
        lands = _place_own(arrays, gather, "own_" + group, slot)
        send_sems, recv_sems, arrays, lands, token = _exchange_start(arrays, lands, gather, "start_" + group, peers, slot)
        pending[group] = (names, send_sems, recv_sems, arrays, lands, gather, peers, slot)
        return token

    def finish(group, after):
        names, send_sems, recv_sems, arrays, lands, gather, peers, slot = pending.pop(group)
        lands, token = _exchange_wait(send_sems, recv_sems, arrays, lands, after, gather, "wait_" + group, peers, slot)
        return dict(zip(names, lands)), token

    first = ["ffn1_w_gate", "ffn1_w_up", "ffn1_w_down"]
    def shard(n, token=None):
        return _dep(_stored(n, w[n])[0], None if token is None else token[0, 0]).astype(BF16)

    ffn_names = ("ffn1_w_gate", "ffn1_w_up", "ffn1_w_down", "ffn2_w_gate", "ffn2_w_up", "ffn2_w_down")
    later_groups = (("w_in", ["w_in"]), ("mix", ["s5_w_glu", "w_out"]), ("ffn2", ["ffn2_w_gate", "ffn2_w_up", "ffn2_w_down"]))
    (meta_full,), token0 = _exchange([w["meta_tokens"]], True, "gather_meta")
    token1 = start("ffn1", first, [shard(n, token0) for n in first], True, (SIBLING,) + CHIP_PEERS)
    meta_full = _dep(meta_full.transpose(1, 0, 2).reshape(N_META, D_MODEL), token1[0, 0])
    later_shards = {n: shard(n, token1) for _, names in later_groups for n in names}
    for n in ("na_rpb", "s5_lam_re"):
        small[n] = _dep(small[n], token1[0, 0])

    def get_w(group, after):
        if group == "meta":
            return {"meta_tokens": meta_full}
        if group == "ffn1":
            after = list(after) + list(later_shards.values())
        got, token = finish(group, after)
        if group == "ffn1":
            got = dict(zip(got, _forward_sibling(list(got.values()), "forward_ffn1")))
            got["tokens"] = [start(g, names + ["order"], [later_shards[n] for n in names] + [token], True) for g, names in later_groups]
        if group == "mix":
            got = {"s5_w_glu": got["s5_w_glu"].reshape(S5_WIDTH, S5_WIDTH), "w_out": got["w_out"].reshape(D_MODEL, D_MODEL)}
        return {n: (a.reshape(D_FF, D_MODEL) if n in ffn_names else a) for n, a in got.items()}

    tokens = {}

    def emit(group, grads):
        grads = {n: (g.reshape(N_DEV, FF_SHARD, D_MODEL) if n in ffn_names else g) for n, g in grads.items()}
        if group == "ffn1":
            mine = [g.reshape((N_DEV // 2, 2) + g.shape[1:]) for g in grads.values()]
            theirs = _swap_sibling(mine, "swap_g_ffn1")
            sums = [_sum_pairs(a, b, "pair_sum_" + n) for n, a, b in zip(grads, mine, theirs)]
            tokens[group] = start("g_ffn1", list(grads), sums, False, CHIP_PEERS, _slot4)
        else:
            tokens[group] = start("g_" + group, list(grads), list(grads.values()), group == "small")
        return tokens[group][0, 0]

    loss8, grad_x, gmeta, vec_g = _local_step(x[0], loss_target[0], get_w, small, emit)
    res = {}

    def update_shard(n, pieces):
        outs = _adamw(_stored(n, w[n]), _stored(n, m[n]), _stored(n, v[n]), pieces, "adamw_" + n)
        res[n] = [_stored(n, o) for o in outs]

    late = [grad_x, tokens["ffn1"]]
    for group in ("g_ffn2", "g_mix", "g_w_in"):
        for n, pieces in finish(group, late)[0].items():
            update_shard(n, pieces)
    g8 = finish("g_small", late)[0]
    dense = _sum8(g8["dense"], "sum_dense")
    for i, n in enumerate(LEAD_NAMES):
        g = dense[i].reshape(_stored(n, w[n]).shape)
        upd = _adamw_s5_mat(_stored(n, w[n]), _stored(n, m[n]), _stored(n, v[n]), g, "adamw_" + n)
        res[n] = [_stored(n, o) for o in [g] + list(upd)]

    done = [res[n][1] for n in ("ffn2_w_gate", "ffn2_w_up", "ffn2_w_down", "w_in", "w_out", "s5_w_glu") + tuple(LEAD_NAMES)]
    (packed8, gmeta8), _ = _exchange([_pack_vectors(vec_g, loss8), gmeta], True, "gather_vectors", after=done)
    for n, pieces in finish("g_ffn1", packed8)[0].items():
        update_shard(n, pieces)
    _, _, _, me = _me()
    update_shard("meta_tokens", lax.dynamic_slice_in_dim(gmeta8, me * (D_MODEL // N_DEV), D_MODEL // N_DEV, axis=2))

    outs = _adamw_small(packed8, [(w[n], m[n], v[n]) for n in VEC_NAMES], [(w[n], m[n], v[n], g8[n]) for n in WHOLE_NAMES])
    for i, n in enumerate(VEC_NAMES + WHOLE_NAMES):
        res[n] = list(outs[4 * i:4 * i + 4])

    out = [outs[-1][0, 0], grad_x[None]]
    for kind in range(4):
        out += [res[n][kind] for n in WEIGHT_NAMES]
    return tuple(out)
```

```python
import functools
import math

import numpy as np
import jax
import jax.numpy as jnp
from jax import lax
from jax.experimental import pallas as pl
from jax.experimental.pallas import tpu as pltpu

F32 = jnp.float32
BF16 = jnp.bfloat16
SDS = jax.ShapeDtypeStruct

D_MODEL = 1024
N_TOK = 2048
N_META = 16
SEQ = N_TOK + N_META
ROW_TILE = 688
N_ROW_TILES = SEQ // ROW_TILE
N_DEV = 8
D_FF = 2816
FF_SHARD = D_FF // N_DEV
FF_TILE = 256
IN_SHARD = 256
NA_WIDTH = 512
S5_WIDTH = 512
HEADS = 8
HEAD_DIM = 64
GRID_W = 64
GRID_ROWS = N_TOK // GRID_W
KH = 8
KW = 16
NA_RB = 4
NA_KR = KH + NA_RB - 1
NA_BLOCKS = GRID_ROWS // NA_RB
NA_QB = NA_RB * GRID_W
NA_KB = NA_KR * GRID_W
NA_TYPES = 3
S5_GROUPS = 32
S5_GROUP = 16
S5_STATE = 64
S5_CHUNKS = 4
CH_W = S5_WIDTH // S5_CHUNKS
ST_W = S5_GROUPS * S5_STATE // S5_CHUNKS
SCAN_BLOCKS = 8
SCAN_T = SEQ // SCAN_BLOCKS
RMS_EPS = 1e-6
NEG_INF = -1e30
ATT_SCALE = HEAD_DIM ** -0.5
ADAM_LR, ADAM_B1, ADAM_B2, ADAM_EPS, ADAM_WD, ADAM_STEP = 0.001, 0.9, 0.999, 1e-08, 0.01, 10
VMEM_LIMIT = 56 * 1024 * 1024
MESH = pl.DeviceIdType.MESH
AXES = ("x", "y", "c")


def _params(sem=None):
    return pltpu.CompilerParams(dimension_semantics=sem, vmem_limit_bytes=VMEM_LIMIT)


def _dot(a, b):
    return jnp.dot(a, b, preferred_element_type=F32)


def _dot_nt(a, b):
    return lax.dot_general(a, b, (((1,), (1,)), ((), ())), preferred_element_type=F32)


def _dot_tn(a, b):
    return lax.dot_general(a, b, (((0,), (0,)), ((), ())), preferred_element_type=F32)


def _rstd(x):
    return lax.rsqrt(jnp.mean(x * x, axis=-1, keepdims=True) + RMS_EPS)


def _rms_bwd(x, r, g, dy):
    dyg = dy * g
    xr = x * r
    dx = r * (dyg - xr * jnp.mean(dyg * xr, axis=-1, keepdims=True))
    return dx, dy * xr


def _rows(i, size=ROW_TILE):
    return pl.ds(pl.multiple_of(i * size, 16), size)


def _row_spec(width):
    return pl.BlockSpec((ROW_TILE, width), lambda i: (i, 0))


def _fix_spec(shape):
    return pl.BlockSpec(shape, lambda i: (0,) * len(shape))


def _split3(x):
    hi = x.astype(BF16)
    r1 = x - hi.astype(F32)
    mid = r1.astype(BF16)
    lo = (r1 - mid.astype(F32)).astype(BF16)
    return hi, mid, lo


def _embed_prenorm(meta, x, g):
    def body(m_ref, x_ref, g_ref, h_ref, a_ref):
        h_ref[0:N_META, :] = m_ref[...]
        h_ref[N_META:, :] = x_ref[...]
        for i in range(N_ROW_TILES):
            rows = slice(i * ROW_TILE, (i + 1) * ROW_TILE)
            hv = h_ref[rows, :]
            a_ref[rows, :] = (hv * _rstd(hv) * g_ref[...]).astype(BF16)

    return pl.pallas_call(
        body, out_shape=[SDS((SEQ, D_MODEL), F32), SDS((SEQ, D_MODEL), BF16)], name="embed_prenorm",
        compiler_params=_params())(meta, x, g)


def _post_pre(f, hres, g_post, g_next, scale, name):
    def body(f_ref, h_ref, gp_ref, gn_ref, ho_ref, a_ref):
        fv = f_ref[...]
        h = h_ref[...] + scale * (fv * _rstd(fv) * gp_ref[...])
        ho_ref[...] = h
        a_ref[...] = (h * _rstd(h) * gn_ref[...]).astype(BF16)

    return pl.pallas_call(
        body, grid=(N_ROW_TILES,),
        in_specs=[_row_spec(D_MODEL), _row_spec(D_MODEL), _fix_spec((1, D_MODEL)), _fix_spec((1, D_MODEL))],
        out_specs=[_row_spec(D_MODEL), _row_spec(D_MODEL)],
        out_shape=[SDS((SEQ, D_MODEL), F32), SDS((SEQ, D_MODEL), BF16)], name=name,
        compiler_params=_params(("parallel",)))(f, hres, g_post, g_next)


def _final_loss(f2, h2, g_post, g_final, target):
    def body(f_ref, h_ref, gp_ref, gf_ref, t_ref, loss_ref, dh_ref, df_ref, dgf_ref, dgp_ref):
        i = pl.program_id(0)
        fv = f_ref[...]
        r1 = _rstd(fv)
        gp = gp_ref[...]
        h3 = h_ref[...] + 0.5 * (fv * r1 * gp)
        r2 = _rstd(h3)
        gf = gf_ref[...]
        y = h3 * r2 * gf
        row = lax.broadcasted_iota(jnp.int32, (ROW_TILE, 1), 0) + i * ROW_TILE
        err = jnp.where(row >= N_META, y - t_ref[...], 0.0)
        part = 0.5 * jnp.sum(jnp.mean(err * err, axis=-1, keepdims=True))
        dy = err * (1.0 / D_MODEL)
        dh3, dgf = _rms_bwd(h3, r2, gf, dy)
        dh_ref[...] = dh3
        df, dgp = _rms_bwd(fv, r1, gp, 0.5 * dh3)
        df_ref[...] = df.astype(BF16)

        @pl.when(i == 0)
        def _():
            loss_ref[...] = jnp.zeros_like(loss_ref)
            dgf_ref[...] = jnp.zeros_like(dgf_ref)
            dgp_ref[...] = jnp.zeros_like(dgp_ref)

        loss_ref[...] += part
        dgf_ref[...] += jnp.sum(dgf, axis=0, keepdims=True)
        dgp_ref[...] += jnp.sum(dgp, axis=0, keepdims=True)

    gain = _fix_spec((1, D_MODEL))
    return pl.pallas_call(
        body, grid=(N_ROW_TILES,),
        in_specs=[_row_spec(D_MODEL), _row_spec(D_MODEL), gain, gain, _row_spec(D_MODEL)],
        out_specs=[_fix_spec((8, 128)), _row_spec(D_MODEL), _row_spec(D_MODEL), gain, gain],
        out_shape=[SDS((8, 128), F32), SDS((SEQ, D_MODEL), F32), SDS((SEQ, D_MODEL), BF16),
                   SDS((1, D_MODEL), F32), SDS((1, D_MODEL), F32)],
        name="final_loss", compiler_params=_params(("arbitrary",)))(f2, h2, g_post, g_final, target)


def _bwd_pre_post(da, h, g_pre, dh_res, fprev, g_post, scale, name):
    def body(da_ref, h_ref, gpre_ref, dhr_ref, f_ref, gpost_ref, dh_ref, df_ref, dgpre_ref, dgpost_ref):
        i = pl.program_id(0)
        hv = h_ref[...]
        dxa, dgpre = _rms_bwd(hv, _rstd(hv), gpre_ref[...], da_ref[...])
        dh = dhr_ref[...] + dxa
        dh_ref[...] = dh
        fv = f_ref[...]
        df, dgpost = _rms_bwd(fv, _rstd(fv), gpost_ref[...], scale * dh)
        df_ref[...] = df.astype(BF16)

        @pl.when(i == 0)
        def _():
            dgpre_ref[...] = jnp.zeros_like(dgpre_ref)
            dgpost_ref[...] = jnp.zeros_like(dgpost_ref)

        dgpre_ref[...] += jnp.sum(dgpre, axis=0, keepdims=True)
        dgpost_ref[...] += jnp.sum(dgpost, axis=0, keepdims=True)

    gain = _fix_spec((1, D_MODEL))
    row = _row_spec(D_MODEL)
    return pl.pallas_call(
        body, grid=(N_ROW_TILES,), in_specs=[row, row, gain, row, row, gain],
        out_specs=[row, row, gain, gain],
        out_shape=[SDS((SEQ, D_MODEL), F32), SDS((SEQ, D_MODEL), BF16), SDS((1, D_MODEL), F32), SDS((1, D_MODEL), F32)],
        name=name, compiler_params=_params(("arbitrary",)))(da, h, g_pre, dh_res, fprev, g_post)


def _bwd_embed(da, h, g_pre, dh_res):
    def body(da_ref, h_ref, gpre_ref, dhr_ref, gx_ref, gm_ref, dgpre_ref):
        total = jnp.zeros((1, D_MODEL), F32)
        for i in range(N_ROW_TILES):
            rows = slice(i * ROW_TILE, (i + 1) * ROW_TILE)
            hv = h_ref[rows, :]
            dxa, dgpre = _rms_bwd(hv, _rstd(hv), gpre_ref[...], da_ref[rows, :])
            dh = dhr_ref[rows, :] + dxa
            total = total + jnp.sum(dgpre, axis=0, keepdims=True)
            if i == 0:
                gm_ref[...] = dh[0:N_META, :]
                gx_ref[0:ROW_TILE - N_META, :] = dh[N_META:, :]
            else:
                gx_ref[i * ROW_TILE - N_META:(i + 1) * ROW_TILE - N_META, :] = dh
        dgpre_ref[...] = total

    return pl.pallas_call(
        body, out_shape=[SDS((N_TOK, D_MODEL), F32), SDS((N_META, D_MODEL), F32), SDS((1, D_MODEL), F32)],
        name="bwd_embed", compiler_params=_params())(da, h, g_pre, dh_res)


def _ffn_fwd(a, wg, wu, wd, name, after=()):
    def body(a_ref, wg_ref, wu_ref, wd_ref, *rest):
        gate_ref, up_ref, f_ref = rest[len(after):]
        j = pl.program_id(0)

        def tile(i, carry):
            rows = _rows(i)
            at = a_ref[rows, :]
            gate = _dot_nt(at, wg_ref[...])
            up = _dot_nt(at, wu_ref[...])
            gate_ref[rows, :] = gate.astype(BF16)
            up_ref[rows, :] = up.astype(BF16)
            act = (gate * jax.nn.sigmoid(gate) * up).astype(BF16)
            contrib = _dot(act, wd_ref[...])

            @pl.when(j == 0)
            def _():
                f_ref[rows, :] = contrib

            @pl.when(j != 0)
            def _():
                f_ref[rows, :] += contrib

            return carry

        lax.fori_loop(0, N_ROW_TILES, tile, 0)

    wtile = pl.BlockSpec((FF_TILE, D_MODEL), lambda j: (j, 0))
    hid = pl.BlockSpec((SEQ, FF_TILE), lambda j: (0, j))
    full = pl.BlockSpec((SEQ, D_MODEL), lambda j: (0, 0))
    return pl.pallas_call(
        body, grid=(D_FF // FF_TILE,), in_specs=[full, wtile, wtile, wtile] + [pl.BlockSpec(memory_space=pl.ANY)] * len(after),
        out_specs=[hid, hid, full],
        out_shape=[SDS((SEQ, D_FF), BF16), SDS((SEQ, D_FF), BF16), SDS((SEQ, D_MODEL), F32)],
        name=name, compiler_params=_params(("arbitrary",)))(a, wg, wu, wd, *after)


def _ffn_bwd(df, a, gate, up, wg, wu, wd, name):
    def body(df_ref, a_ref, gate_ref, up_ref, wg_ref, wu_ref, wd_ref, da_ref, dwg_ref, dwu_ref, dwd_ref,
             dgate_s, dup_s, act_s):
        j = pl.program_id(0)

        def tile(i, carry):
            rows = _rows(i)
            dft = df_ref[rows, :]
            gate = gate_ref[rows, :].astype(F32)
            up = up_ref[rows, :].astype(F32)
            dact = _dot_nt(dft, wd_ref[...])
            sig = jax.nn.sigmoid(gate)
            silu = gate * sig
            dgate = (dact * up * (sig * (1.0 + gate * (1.0 - sig)))).astype(BF16)
            dup = (dact * silu).astype(BF16)
            dgate_s[rows, :] = dgate
            dup_s[rows, :] = dup
            act_s[rows, :] = (silu * up).astype(BF16)
            dat = _dot(dgate, wg_ref[...]) + _dot(dup, wu_ref[...])

            @pl.when(j == 0)
            def _():
                da_ref[rows, :] = dat

            @pl.when(j != 0)
            def _():
                da_ref[rows, :] += dat

            return carry

        lax.fori_loop(0, N_ROW_TILES, tile, 0)
        dwd_ref[...] = _dot_tn(act_s[...], df_ref[...]).astype(BF16)
        dwg_ref[...] = _dot_tn(dgate_s[...], a_ref[...]).astype(BF16)
        dwu_ref[...] = _dot_tn(dup_s[...], a_ref[...]).astype(BF16)

    wtile = pl.BlockSpec((FF_TILE, D_MODEL), lambda j: (j, 0))
    hid = pl.BlockSpec((SEQ, FF_TILE), lambda j: (0, j))
    full = pl.BlockSpec((SEQ, D_MODEL), lambda j: (0, 0))
    return pl.pallas_call(
        body, grid=(D_FF // FF_TILE,), in_specs=[full, full, hid, hid, wtile, wtile, wtile],
        out_specs=[full, wtile, wtile, wtile],
        out_shape=[SDS((SEQ, D_MODEL), F32)] + [SDS((D_FF, D_MODEL), BF16)] * 3,
        scratch_shapes=[pltpu.VMEM((SEQ, FF_TILE), BF16)] * 3,
        name=name, compiler_params=_params(("arbitrary",)))(df, a, gate, up, wg, wu, wd)


HEADS_PER_BLOCK = IN_SHARD // HEAD_DIM
QKV_BLOCKS = 3 * NA_WIDTH // IN_SHARD


def _proj_heads(a, w):
    def body(a_ref, w_ref, o_ref):
        def tile(i, carry):
            rows = _rows(i)
            res = _dot(a_ref[rows, :], w_ref[...])
            for sub in range(HEADS_PER_BLOCK):
                o_ref[sub, rows, :] = res[:, sub * HEAD_DIM:(sub + 1) * HEAD_DIM]
            return carry

        lax.fori_loop(0, N_ROW_TILES, tile, 0)

    return pl.pallas_call(
        body, grid=(QKV_BLOCKS,),
        in_specs=[pl.BlockSpec((SEQ, D_MODEL), lambda j: (0, 0)), pl.BlockSpec((None, D_MODEL, IN_SHARD), lambda j: (j, 0, 0))],
        out_specs=pl.BlockSpec((HEADS_PER_BLOCK, SEQ, HEAD_DIM), lambda j: (j, 0, 0)),
        out_shape=SDS((3 * HEADS, SEQ, HEAD_DIM), F32), name="proj_heads",
        compiler_params=_params(("parallel",)))(a, w)


def _proj_u(a, w):
    def body(a_ref, w_ref, o_ref):
        def tile(i, carry):
            rows = _rows(i)
            o_ref[rows, :] = _dot(a_ref[rows, :], w_ref[...])
            return carry

        lax.fori_loop(0, N_ROW_TILES, tile, 0)

    return pl.pallas_call(
        body, grid=(N_DEV - QKV_BLOCKS,),
        in_specs=[pl.BlockSpec((SEQ, D_MODEL), lambda j: (0, 0)),
                  pl.BlockSpec((None, D_MODEL, IN_SHARD), lambda j: (j + QKV_BLOCKS, 0, 0))],
        out_specs=pl.BlockSpec((SEQ, IN_SHARD), lambda j: (0, j)),
        out_shape=SDS((SEQ, S5_WIDTH), F32), name="proj_u",
        compiler_params=_params(("parallel",)))(a, w)


def _proj_bwd(dq, dk, dv, du, a, w):
    def body(dq_ref, dk_ref, dv_ref, du_ref, a_ref, w_ref, da_ref, dw_ref, dp_ref):
        j = pl.program_id(0)

        for which, src in enumerate((dq_ref, dk_ref, dv_ref)):
            @pl.when((j >= 2 * which) & (j < 2 * which + 2))
            def _(src=src):
                dp_ref[...] = jnp.concatenate([src[sub] for sub in range(HEADS_PER_BLOCK)], axis=-1).astype(BF16)

        @pl.when(j >= QKV_BLOCKS)
        def _():
            dp_ref[...] = du_ref[...].astype(BF16)

        def tile(i, carry):
            rows = _rows(i)
            dat = _dot_nt(dp_ref[rows, :], w_ref[...])

            @pl.when(j == 0)
            def _():
                da_ref[rows, :] = dat

            @pl.when(j != 0)
            def _():
                da_ref[rows, :] += dat

            return carry

        lax.fori_loop(0, N_ROW_TILES, tile, 0)
        dw_ref[...] = _dot_tn(a_ref[...], dp_ref[...]).astype(BF16)

    full = pl.BlockSpec((SEQ, D_MODEL), lambda j: (0, 0))
    wspec = pl.BlockSpec((None, D_MODEL, IN_SHARD), lambda j: (j, 0, 0))

    def heads(which):
        return pl.BlockSpec((HEADS_PER_BLOCK, SEQ, HEAD_DIM), lambda j: (jnp.clip(j - 2 * which, 0, 1), 0, 0))

    return pl.pallas_call(
        body, grid=(N_DEV,),
        in_specs=[heads(0), heads(1), heads(2),
                  pl.BlockSpec((SEQ, IN_SHARD), lambda j: (0, jnp.clip(j - QKV_BLOCKS, 0, 1))), full, wspec],
        out_specs=[full, wspec],
        out_shape=[SDS((SEQ, D_MODEL), F32), SDS((N_DEV, D_MODEL, IN_SHARD), BF16)],
        scratch_shapes=[pltpu.VMEM((SEQ, IN_SHARD), BF16)],
        name="proj_bwd", compiler_params=_params(("arbitrary",)))(dq, dk, dv, du, a, w)


def _na_consts():
    c = np.arange(GRID_W)
    col_start = np.clip(c - KW // 2, 0, GRID_W - KW)
    col_in = (c[None, :] >= col_start[:, None]) & (c[None, :] < col_start[:, None] + KW)
    dc = np.clip(c[None, :] - c[:, None] + KW - 1, 0, 2 * KW - 2)
    onehot = np.zeros((128, GRID_W * GRID_W), np.float32)
    qq, kk = np.meshgrid(c, c, indexing="ij")
    onehot[dc[col_in], (qq * GRID_W + kk)[col_in]] = 1.0
    negmask = np.where(col_in, 0.0, NEG_INF).astype(np.float32).reshape(1, -1)
    return onehot, negmask


def _na_pair(block_type, a, b):
    if block_type == 0:
        return b - a + KH - 1 if b < KH else None
    if block_type == 1:
        return b - a + KH // 2 - 1 if a <= b < a + KH else None
    return b - a if b >= NA_KR - KH else None


def _rpb_expand(rpb):
    onehot, negmask = _na_consts()
    rows = HEADS * (2 * KH - 1)
    rpb_pad = jnp.pad(rpb.reshape(rows, 2 * KW - 1), ((0, 128 - rows), (0, 128 - (2 * KW - 1))))

    def body(r_ref, oh_ref, m_ref, t_ref):
        hi, mid, lo = _split3(r_ref[...])
        oh = oh_ref[...]
        t_ref[...] = _dot(hi, oh) + _dot(mid, oh) + _dot(lo, oh) + m_ref[...]

    table = pl.pallas_call(body, out_shape=SDS((128, GRID_W * GRID_W), F32), name="rpb_expand",
                           compiler_params=_params())(rpb_pad, jnp.asarray(onehot, BF16), jnp.asarray(negmask))
    return table[:rows].reshape(HEADS, 2 * KH - 1, GRID_W, GRID_W)


def _rpb_reduce(dslabs):
    onehot, _ = _na_consts()
    rows = HEADS * (2 * KH - 1)

    def body(x_ref, oht_ref, o_ref):
        hi, mid, lo = _split3(x_ref[...])
        oht = oht_ref[...]
        o_ref[...] = _dot(hi, oht) + _dot(mid, oht) + _dot(lo, oht)

    out = pl.pallas_call(body, out_shape=SDS((rows, 128), F32), name="rpb_reduce", compiler_params=_params())(
        dslabs.reshape(rows, GRID_W * GRID_W), jnp.asarray(onehot.T, BF16))
    return out.reshape(HEADS, 2 * KH - 1, 128)


def _bias_tiles(slab_ref, tile_ref):
    tile_ref[...] = jnp.full(tile_ref.shape, NEG_INF, F32)
    for t in range(NA_TYPES):
        for a in range(NA_RB):
            for b in range(NA_KR):
                dr = _na_pair(t, a, b)
                if dr is not None:
                    tile_ref[t, a * GRID_W:(a + 1) * GRID_W, b * GRID_W:(b + 1) * GRID_W] = slab_ref[dr]


def _bias_tiles_bwd(dtile_ref, dslab_ref):
    acc = {}
    for t in range(NA_TYPES):
        for a in range(NA_RB):
            for b in range(NA_KR):
                dr = _na_pair(t, a, b)
                if dr is not None:
                    part = dtile_ref[t, a * GRID_W:(a + 1) * GRID_W, b * GRID_W:(b + 1) * GRID_W]
                    acc[dr] = part if dr not in acc else acc[dr] + part
    for dr in range(2 * KH - 1):
        dslab_ref[dr] = acc[dr]


def _block_geometry(g):
    start = jnp.clip(g * NA_RB - KH // 2, 0, GRID_ROWS - NA_KR)
    block_type = jnp.where(g == 0, 0, jnp.where(g == NA_BLOCKS - 1, 2, 1))
    q0 = pl.multiple_of(N_META + g * NA_QB, 16)
    k0 = pl.multiple_of(N_META + start * GRID_W, 16)
    return block_type, q0, k0


def _na_probs(q, kk, km, bias):
    s = _dot_nt(q, kk) * ATT_SCALE + bias
    sm = _dot_nt(q, km) * ATT_SCALE
    m = jnp.maximum(jnp.max(s, axis=-1, keepdims=True), jnp.max(sm, axis=-1, keepdims=True))
    p = jnp.exp(s - m)
    pm = jnp.exp(sm - m)
    inv = 1.0 / (jnp.sum(p, axis=-1, keepdims=True) + jnp.sum(pm, axis=-1, keepdims=True))
    return p * inv, pm * inv


def _meta_probs(qm, km):
    s = _dot_nt(qm, km) * ATT_SCALE
    p = jnp.exp(s - jnp.max(s, axis=-1, keepdims=True))
    return p / jnp.sum(p, axis=-1, keepdims=True)


def _qkv_specs():
    return [pl.BlockSpec((None, SEQ, HEAD_DIM), lambda h, which=which: (h + which * HEADS, 0, 0)) for which in range(3)]


def _na_fwd(qkv, bias):
    def body(q_ref, k_ref, v_ref, slab_ref, o_ref, b_ref):
        _bias_tiles(slab_ref, b_ref)
        km = k_ref[0:N_META, :].astype(BF16)
        vm = v_ref[0:N_META, :].astype(BF16)
        pmm = _meta_probs(q_ref[0:N_META, :].astype(BF16), km)
        o_ref[0:N_META, :] = _dot(pmm.astype(BF16), vm)

        def block(g, carry):
            block_type, q0, k0 = _block_geometry(g)
            qb = q_ref[pl.ds(q0, NA_QB), :].astype(BF16)
            kk = k_ref[pl.ds(k0, NA_KB), :].astype(BF16)
            vv = v_ref[pl.ds(k0, NA_KB), :].astype(BF16)
            p, pm = _na_probs(qb, kk, km, b_ref[block_type])
            o_ref[pl.ds(q0, NA_QB), :] = _dot(p.astype(BF16), vv) + _dot(pm.astype(BF16), vm)
            return carry

        lax.fori_loop(0, NA_BLOCKS, block, 0)

    head = pl.BlockSpec((None, SEQ, HEAD_DIM), lambda h: (h, 0, 0))
    return pl.pallas_call(
        body, grid=(HEADS,), in_specs=_qkv_specs() + [pl.BlockSpec((None, 2 * KH - 1, GRID_W, GRID_W), lambda h: (h, 0, 0, 0))],
        out_specs=head, out_shape=SDS((HEADS, SEQ, HEAD_DIM), F32), name="na_fwd",
        scratch_shapes=[pltpu.VMEM((NA_TYPES, NA_QB, NA_KB), F32)],
        compiler_params=_params(("parallel",)))(qkv, qkv, qkv, bias)


def _na_bwd(qkv, bias, do):
    def body(q_ref, k_ref, v_ref, slab_ref, do_ref, dq_ref, dk_ref, dv_ref, dslab_ref, b_ref, db_ref):
        _bias_tiles(slab_ref, b_ref)
        km = k_ref[0:N_META, :].astype(BF16)
        vm = v_ref[0:N_META, :].astype(BF16)
        dk_ref[...] = jnp.zeros_like(dk_ref)
        dv_ref[...] = jnp.zeros_like(dv_ref)
        db_ref[...] = jnp.zeros_like(db_ref)

        qm = q_ref[0:N_META, :].astype(BF16)
        dom = do_ref[0:N_META, :].astype(BF16)
        pmm = _meta_probs(qm, km)
        dpm = _dot_nt(dom, vm)
        dsm = (pmm * (dpm - jnp.sum(pmm * dpm, axis=-1, keepdims=True)) * ATT_SCALE).astype(BF16)
        dq_ref[0:N_META, :] = _dot(dsm, km)
        dkm0 = _dot_tn(dsm, qm)
        dvm0 = _dot_tn(pmm.astype(BF16), dom)

        def block(g, carry):
            dkm, dvm = carry
            block_type, q0, k0 = _block_geometry(g)
            qb = q_ref[pl.ds(q0, NA_QB), :].astype(BF16)
            kk = k_ref[pl.ds(k0, NA_KB), :].astype(BF16)
            vv = v_ref[pl.ds(k0, NA_KB), :].astype(BF16)
            dob = do_ref[pl.ds(q0, NA_QB), :].astype(BF16)
            p, pm = _na_probs(qb, kk, km, b_ref[block_type])
            dp = _dot_nt(dob, vv)
            dpm_ = _dot_nt(dob, vm)
            delta = jnp.sum(p * dp, axis=-1, keepdims=True) + jnp.sum(pm * dpm_, axis=-1, keepdims=True)
            ds = p * (dp - delta)
            dsm_ = pm * (dpm_ - delta)
            db_ref[block_type] += ds
            dsb = (ds * ATT_SCALE).astype(BF16)
            dsmb = (dsm_ * ATT_SCALE).astype(BF16)
            dq_ref[pl.ds(q0, NA_QB), :] = _dot(dsb, kk) + _dot(dsmb, km)
            dk_ref[pl.ds(k0, NA_KB), :] += _dot_tn(dsb, qb)
            dv_ref[pl.ds(k0, NA_KB), :] += _dot_tn(p.astype(BF16), dob)
            return dkm + _dot_tn(dsmb, qb), dvm + _dot_tn(pm.astype(BF16), dob)

        dkm, dvm = lax.fori_loop(0, NA_BLOCKS, block, (dkm0, dvm0))
        dk_ref[0:N_META, :] = dkm
        dv_ref[0:N_META, :] = dvm
        _bias_tiles_bwd(db_ref, dslab_ref)

    head = pl.BlockSpec((None, SEQ, HEAD_DIM), lambda h: (h, 0, 0))
    bspec = pl.BlockSpec((None, 2 * KH - 1, GRID_W, GRID_W), lambda h: (h, 0, 0, 0))
    return pl.pallas_call(
        body, grid=(HEADS,), in_specs=_qkv_specs() + [bspec, head], out_specs=[head, head, head, bspec],
        out_shape=[SDS((HEADS, SEQ, HEAD_DIM), F32)] * 3 + [SDS((HEADS, 2 * KH - 1, GRID_W, GRID_W), F32)],
        scratch_shapes=[pltpu.VMEM((NA_TYPES, NA_QB, NA_KB), F32), pltpu.VMEM((NA_TYPES, NA_QB, NA_KB), F32)],
        name="na_bwd", compiler_params=_params(("parallel",)))(qkv, qkv, qkv, bias, do)


def _cmul(ar, ai, br, bi):
    return ar * br - ai * bi, ar * bi + ai * br


def _cpow(ar, ai, n):
    rr, ri = None, None
    br, bi = ar, ai
    while n:
        if n & 1:
            rr, ri = (br, bi) if rr is None else _cmul(rr, ri, br, bi)
        n >>= 1
        if n:
            br, bi = _cmul(br, bi, br, bi)
    return rr, ri


def _s5_prep(lr, li, logdt, bre, bim):
    def body(lr_ref, li_ref, dt_ref, br_ref, bi_ref, lbr_ref, lbi_ref, bbr_ref, bbi_ref):
        lr_, li_ = lr_ref[...], li_ref[...]
        dt = jnp.exp(dt_ref[...])
        mag = jnp.exp(lr_ * dt)
        lbr = mag * jnp.cos(li_ * dt)
        lbi = mag * jnp.sin(li_ * dt)
        lbr_ref[...] = lbr
        lbi_ref[...] = lbi
        den = lr_ * lr_ + li_ * li_
        xr = lbr - 1.0
        cr = (xr * lr_ + lbi * li_) / den
        ci = (lbi * lr_ - xr * li_) / den
        br, bi = br_ref[...], bi_ref[...]
        bbr_ref[...] = cr[:, None, :] * br - ci[:, None, :] * bi
        bbi_ref[...] = cr[:, None, :] * bi + ci[:, None, :] * br

    n = 2 * S5_GROUPS
    return pl.pallas_call(
        body, out_shape=[SDS((n, S5_STATE), F32)] * 2 + [SDS((n, S5_GROUP, S5_STATE), F32)] * 2,
        name="s5_prep", compiler_params=_params())(lr, li, logdt, bre, bim)


def _s5_prep_bwd(lr, li, logdt, bre, bim, dar, dai, dbbr, dbbi):
    def body(lr_ref, li_ref, dt_ref, br_ref, bi_ref, dar_ref, dai_ref, dbr_ref, dbi_ref,
             glr_ref, gli_ref, gdt_ref, gbr_ref, gbi_ref):
        lr_, li_ = lr_ref[...], li_ref[...]
        dt = jnp.exp(dt_ref[...])
        mag = jnp.exp(lr_ * dt)
        lbr = mag * jnp.cos(li_ * dt)
        lbi = mag * jnp.sin(li_ * dt)
        den = lr_ * lr_ + li_ * li_
        xr = lbr - 1.0
        cr = (xr * lr_ + lbi * li_) / den
        ci = (lbi * lr_ - xr * li_) / den
        br, bi = br_ref[...], bi_ref[...]
        dbr, dbi = dbr_ref[...], dbi_ref[...]
        gbr_ref[...] = cr[:, None, :] * dbr + ci[:, None, :] * dbi
        gbi_ref[...] = cr[:, None, :] * dbi - ci[:, None, :] * dbr
        gcr = jnp.sum(dbr * br + dbi * bi, axis=1)
        gci = jnp.sum(dbi * br - dbr * bi, axis=1)
        ilr, ili = lr_ / den, li_ / den
        tr, ti = _cmul(gcr, gci, ilr, ili)
        glbr = dar_ref[...] + tr
        glbi = dai_ref[...] + ti
        dr_, di_ = _cmul(tr, ti, cr, -ci)
        gwr, gwi = _cmul(glbr, glbi, lbr, -lbi)
        glr_ref[...] = gwr * dt - dr_
        gli_ref[...] = gwi * dt - di_
        gdt_ref[...] = jnp.sum(gwr * lr_ + gwi * li_, axis=-1, keepdims=True) * dt

    n = 2 * S5_GROUPS
    return pl.pallas_call(
        body, out_shape=[SDS((n, S5_STATE), F32)] * 2 + [SDS((n, 1), F32)] + [SDS((n, S5_GROUP, S5_STATE), F32)] * 2,
        name="s5_prep_bwd", compiler_params=_params())(lr, li, logdt, bre, bim, dar, dai, dbbr, dbbi)


def _scan_local(xr_ref, xi_ref, ar8, ai8, reverse):
    def step(i, carry):
        sr, si = carry
        idx = (SCAN_T - 1 - i) if reverse else i
        rows = pl.ds(pl.multiple_of(idx * SCAN_BLOCKS, SCAN_BLOCKS), SCAN_BLOCKS)
        nr = ar8 * sr - ai8 * si + xr_ref[rows, :]
        ni = ar8 * si + ai8 * sr + xi_ref[rows, :]
        xr_ref[rows, :] = nr
        xi_ref[rows, :] = ni
        return nr, ni

    z = jnp.zeros(ar8.shape, F32)
    return lax.fori_loop(0, SCAN_T, step, (z, z))


def _scan_carries(er, ei, atr, ati, reverse):
    row = lax.broadcasted_iota(jnp.int32, er.shape, 0)
    cr = jnp.zeros((1, er.shape[1]), F32)
    ci = cr
    outr = jnp.zeros(er.shape, F32)
    outi = outr
    order = range(SCAN_BLOCKS - 1, -1, -1) if reverse else range(SCAN_BLOCKS)
    for b in order:
        outr = jnp.where(row == b, cr, outr)
        outi = jnp.where(row == b, ci, outi)
        nr, ni = _cmul(atr, ati, cr, ci)
        cr, ci = nr + er[b:b + 1, :], ni + ei[b:b + 1, :]
    return outr, outi


def _scan_fixup(xr_ref, xi_ref, cr8, ci8, ar8, ai8, reverse):
    def step(i, carry):
        pr, pi = carry
        idx = (SCAN_T - 1 - i) if reverse else i
        rows = pl.ds(pl.multiple_of(idx * SCAN_BLOCKS, SCAN_BLOCKS), SCAN_BLOCKS)
        fr, fi = _cmul(pr, pi, cr8, ci8)
        xr_ref[rows, :] += fr
        xi_ref[rows, :] += fi
        return _cmul(pr, pi, ar8, ai8)

    lax.fori_loop(0, SCAN_T, step, (ar8, ai8), unroll=2)


def _scan(xr_ref, xi_ref, ar, ai, reverse):
    n = ar.shape[1]
    ar8 = jnp.broadcast_to(ar, (SCAN_BLOCKS, n))
    ai8 = jnp.broadcast_to(ai, (SCAN_BLOCKS, n))
    er, ei = _scan_local(xr_ref, xi_ref, ar8, ai8, reverse)
    atr, ati = _cpow(ar, ai, SCAN_T)
    cr8, ci8 = _scan_carries(er, ei, atr, ati, reverse)
    _scan_fixup(xr_ref, xi_ref, cr8, ci8, ar8, ai8, reverse)


def _s5_specs():
    chan = pl.BlockSpec((SEQ, CH_W), lambda c, d: (0, c))
    chan2 = pl.BlockSpec((None, SEQ, CH_W), lambda c, d: (d, 0, c))
    state = pl.BlockSpec((None, SEQ, ST_W), lambda c, d: (d, 0, c))
    bmat = pl.BlockSpec((None, None, CH_W, ST_W), lambda c, d: (d, c, 0, 0))
    cmat = pl.BlockSpec((None, None, ST_W, CH_W), lambda c, d: (d, c, 0, 0))
    avec = pl.BlockSpec((None, None, 1, ST_W), lambda c, d: (d, c, 0, 0))
    return chan, chan2, state, bmat, cmat, avec


def _scan_by_direction(xr_ref, xi_ref, ar, ai, d, adjoint):
    @pl.when(d == 0)
    def _():
        _scan(xr_ref, xi_ref, ar, ai, reverse=adjoint)

    @pl.when(d == 1)
    def _():
        _scan(xr_ref, xi_ref, ar, ai, reverse=not adjoint)


def _s5_scan_fwd(u, bre, bim, are, aim, cre, cim):
    def body(u_ref, bre_ref, bim_ref, are_ref, aim_ref, cre_ref, cim_ref, sr_ref, si_ref, y_ref):
        ub = u_ref[...].astype(BF16)
        sr_ref[...] = _dot(ub, bre_ref[...])
        si_ref[...] = _dot(ub, bim_ref[...])
        _scan_by_direction(sr_ref, si_ref, are_ref[...], aim_ref[...], pl.program_id(1), adjoint=False)
        y_ref[...] = _dot(sr_ref[...].astype(BF16), cre_ref[...]) - _dot(si_ref[...].astype(BF16), cim_ref[...])

    chan, chan2, state, bmat, cmat, avec = _s5_specs()
    return pl.pallas_call(
        body, grid=(S5_CHUNKS, 2), in_specs=[chan, bmat, bmat, avec, avec, cmat, cmat], out_specs=[state, state, chan2],
        out_shape=[SDS((2, SEQ, S5_GROUPS * S5_STATE), F32)] * 2 + [SDS((2, SEQ, S5_WIDTH), F32)],
        name="s5_scan_fwd", compiler_params=_params(("parallel", "parallel")))(u, bre, bim, are, aim, cre, cim)


def _dlam(gr_ref, gi_ref, sr_ref, si_ref, reverse):
    tile = lambda i: pl.ds(pl.multiple_of(i * SCAN_BLOCKS, SCAN_BLOCKS), SCAN_BLOCKS)
    row = lax.broadcasted_iota(jnp.int32, (SCAN_BLOCKS, ST_W), 0)
    if reverse:
        edge, src, shift, empty, lo, hi, dprev = SCAN_T - 1, 0, SCAN_BLOCKS - 1, SCAN_BLOCKS - 1, 0, SCAN_T - 1, 1
    else:
        edge, src, shift, empty, lo, hi, dprev = 0, SCAN_T - 1, 1, 0, 1, SCAN_T, -1
    spr = jnp.where(row == empty, 0.0, pltpu.roll(sr_ref[tile(src), :], shift, 0))
    spi = jnp.where(row == empty, 0.0, pltpu.roll(si_ref[tile(src), :], shift, 0))
    acc0 = _cmul(gr_ref[tile(edge), :], gi_ref[tile(edge), :], spr, -spi)

    def step(i, carry):
        accr, acci = carry
        pr, pi = _cmul(gr_ref[tile(i), :], gi_ref[tile(i), :], sr_ref[tile(i + dprev), :], -si_ref[tile(i + dprev), :])
        return accr + pr, acci + pi

    accr, acci = lax.fori_loop(lo, hi, step, acc0)
    return jnp.sum(accr, axis=0, keepdims=True), jnp.sum(acci, axis=0, keepdims=True)


def _s5_scan_bwd(dy, du_skip, u, sr, si, bre, bim, are, aim, cre, cim):
    def body(dy_ref, dus_ref, u_ref, sr_ref, si_ref, bre_ref, bim_ref, are_ref, aim_ref, cre_ref, cim_ref,
             du_ref, dbr_ref, dbi_ref, dcr_ref, dci_ref, dar_ref, dai_ref, gr_ref, gi_ref):
        d = pl.program_id(1)
        dyb = dy_ref[...].astype(BF16)
        gr_ref[...] = _dot_nt(dyb, cre_ref[...])
        gi_ref[...] = -_dot_nt(dyb, cim_ref[...])
        dcr_ref[...] = _dot_tn(sr_ref[...].astype(BF16), dyb)
        dci_ref[...] = -_dot_tn(si_ref[...].astype(BF16), dyb)
        _scan_by_direction(gr_ref, gi_ref, are_ref[...], -aim_ref[...], d, adjoint=True)

        @pl.when(d == 0)
        def _():
            dar_ref[...], dai_ref[...] = _dlam(gr_ref, gi_ref, sr_ref, si_ref, reverse=False)
            du_ref[...] = dus_ref[...]

        @pl.when(d == 1)
        def _():
            dar_ref[...], dai_ref[...] = _dlam(gr_ref, gi_ref, sr_ref, si_ref, reverse=True)

        grb = gr_ref[...].astype(BF16)
        gib = gi_ref[...].astype(BF16)
        du_ref[...] += _dot_nt(grb, bre_ref[...]) + _dot_nt(gib, bim_ref[...])
        ub = u_ref[...].astype(BF16)
        dbr_ref[...] = _dot_tn(ub, grb)
        dbi_ref[...] = _dot_tn(ub, gib)

    chan, _, state, bmat, cmat, avec = _s5_specs()
    return pl.pallas_call(
        body, grid=(S5_CHUNKS, 2), in_specs=[chan, chan, chan, state, state, bmat, bmat, avec, avec, cmat, cmat],
        out_specs=[chan, bmat, bmat, cmat, cmat, avec, avec],
        out_shape=[SDS((SEQ, S5_WIDTH), F32)] + [SDS((2, S5_CHUNKS, CH_W, ST_W), F32)] * 2
                  + [SDS((2, S5_CHUNKS, ST_W, CH_W), F32)] * 2 + [SDS((2, S5_CHUNKS, 1, ST_W), F32)] * 2,
        scratch_shapes=[pltpu.VMEM((SEQ, ST_W), F32), pltpu.VMEM((SEQ, ST_W), F32)],
        name="s5_scan_bwd", compiler_params=_params(("parallel", "arbitrary")))(dy, du_skip, u, sr, si, bre, bim, are, aim, cre, cim)


_GELU_K = math.sqrt(2.0 / math.pi)
_GELU_C = 0.044715


def _gelu(x):
    t = jnp.tanh(_GELU_K * (x + _GELU_C * x * x * x))
    return 0.5 * x * (1.0 + t), t


def _s5_glu_fwd(u, y2, dskip, wglu, bglu):
    def body(u_ref, y0_ref, y1_ref, d_ref, w_ref, b_ref, o_ref, yp_ref):
        ypre = u_ref[...] * d_ref[...] + y0_ref[...] + y1_ref[...]
        yp_ref[...] = ypre
        y, _ = _gelu(ypre)
        z = _dot(y.astype(BF16), w_ref[...]) + b_ref[...]
        o_ref[...] = y * jax.nn.sigmoid(z)

    row = _row_spec(S5_WIDTH)
    vec = _fix_spec((1, S5_WIDTH))
    dir0 = pl.BlockSpec((None, ROW_TILE, S5_WIDTH), lambda i: (0, i, 0))
    dir1 = pl.BlockSpec((None, ROW_TILE, S5_WIDTH), lambda i: (1, i, 0))
    return pl.pallas_call(
        body, grid=(N_ROW_TILES,), in_specs=[row, dir0, dir1, vec, _fix_spec((S5_WIDTH, S5_WIDTH)), vec],
        out_specs=[row, row], out_shape=[SDS((SEQ, S5_WIDTH), F32)] * 2, name="s5_glu_fwd",
        compiler_params=_params(("parallel",)))(u, y2, y2, dskip, wglu, bglu)


def _s5_glu_bwd(do, ypre, u, dskip, wglu, bglu):
    def body(do_ref, yp_ref, u_ref, d_ref, w_ref, b_ref, dyp_ref, du_ref, dw_ref, db_ref, dd_ref):
        i = pl.program_id(0)
        ypre = yp_ref[...]
        y, t = _gelu(ypre)
        yb = y.astype(BF16)
        sg = jax.nn.sigmoid(_dot(yb, w_ref[...]) + b_ref[...])
        dov = do_ref[...]
        dz = dov * y * sg * (1.0 - sg)
        dzb = dz.astype(BF16)
        dy = dov * sg + _dot_nt(dzb, w_ref[...])
        dgelu = 0.5 * (1.0 + t) + 0.5 * ypre * (1.0 - t * t) * _GELU_K * (1.0 + 3.0 * _GELU_C * ypre * ypre)
        dyp = dy * dgelu
        dyp_ref[...] = dyp
        uv = u_ref[...]
        du_ref[...] = dyp * d_ref[...]

        @pl.when(i == 0)
        def _():
            dw_ref[...] = jnp.zeros_like(dw_ref)
            db_ref[...] = jnp.zeros_like(db_ref)
            dd_ref[...] = jnp.zeros_like(dd_ref)

        dw_ref[...] += _dot_tn(yb, dzb)
        db_ref[...] += jnp.sum(dz, axis=0, keepdims=True)
        dd_ref[...] += jnp.sum(dyp * uv, axis=0, keepdims=True)

    row = _row_spec(S5_WIDTH)
    vec = _fix_spec((1, S5_WIDTH))
    mat = _fix_spec((S5_WIDTH, S5_WIDTH))
    return pl.pallas_call(
        body, grid=(N_ROW_TILES,), in_specs=[row, row, row, vec, mat, vec], out_specs=[row, row, mat, vec, vec],
        out_shape=[SDS((SEQ, S5_WIDTH), F32)] * 2 + [SDS((S5_WIDTH, S5_WIDTH), F32), SDS((1, S5_WIDTH), F32), SDS((1, S5_WIDTH), F32)],
        name="s5_glu_bwd", compiler_params=_params(("arbitrary",)))(do, ypre, u, dskip, wglu, bglu)


def _heads_side_by_side(o_ref):
    return jnp.concatenate([o_ref[h] for h in range(HEADS)], axis=-1)


def _mix_out_fwd(ona, os5, g_na, g_s5, wout):
    def body(a_ref, s_ref, ga_ref, gs_ref, w_ref, o_ref):
        av, sv = _heads_side_by_side(a_ref), s_ref[...]
        ca = (av * _rstd(av) * ga_ref[...]).astype(BF16)
        cs = (sv * _rstd(sv) * gs_ref[...]).astype(BF16)
        o_ref[...] = _dot(ca, w_ref[0:NA_WIDTH, :]) + _dot(cs, w_ref[NA_WIDTH:, :])

    row = _row_spec(NA_WIDTH)
    vec = _fix_spec((1, NA_WIDTH))
    heads = pl.BlockSpec((HEADS, ROW_TILE, HEAD_DIM), lambda i: (0, i, 0))
    return pl.pallas_call(
        body, grid=(N_ROW_TILES,), in_specs=[heads, row, vec, vec, _fix_spec((D_MODEL, D_MODEL))],
        out_specs=_row_spec(D_MODEL), out_shape=SDS((SEQ, D_MODEL), F32), name="mix_out_fwd",
        compiler_params=_params(("parallel",)))(ona, os5, g_na, g_s5, wout)


def _mix_out_bwd(dmix, ona, os5, g_na, g_s5, wout):
    def body(dm_ref, a_ref, s_ref, ga_ref, gs_ref, w_ref, da_ref, ds_ref, dw_ref, dga_ref, dgs_ref):
        i = pl.program_id(0)
        dm = dm_ref[...]
        av, sv = _heads_side_by_side(a_ref), s_ref[...]
        ra, rs = _rstd(av), _rstd(sv)
        ga, gs = ga_ref[...], gs_ref[...]
        ca = (av * ra * ga).astype(BF16)
        cs = (sv * rs * gs).astype(BF16)
        dca = _dot_nt(dm, w_ref[0:NA_WIDTH, :])
        dcs = _dot_nt(dm, w_ref[NA_WIDTH:, :])
        da, dga = _rms_bwd(av, ra, ga, dca)
        ds, dgs = _rms_bwd(sv, rs, gs, dcs)
        for h in range(HEADS):
            da_ref[h] = da[:, h * HEAD_DIM:(h + 1) * HEAD_DIM]
        ds_ref[...] = ds

        @pl.when(i == 0)
        def _():
            dw_ref[...] = jnp.zeros_like(dw_ref)
            dga_ref[...] = jnp.zeros_like(dga_ref)
            dgs_ref[...] = jnp.zeros_like(dgs_ref)

        dw_ref[0:NA_WIDTH, :] += _dot_tn(ca, dm)
        dw_ref[NA_WIDTH:, :] += _dot_tn(cs, dm)
        dga_ref[...] += jnp.sum(dga, axis=0, keepdims=True)
        dgs_ref[...] += jnp.sum(dgs, axis=0, keepdims=True)

    row = _row_spec(NA_WIDTH)
    vec = _fix_spec((1, NA_WIDTH))
    mat = _fix_spec((D_MODEL, D_MODEL))
    heads = pl.BlockSpec((HEADS, ROW_TILE, HEAD_DIM), lambda i: (0, i, 0))
    return pl.pallas_call(
        body, grid=(N_ROW_TILES,), in_specs=[_row_spec(D_MODEL), heads, row, vec, vec, mat],
        out_specs=[heads, row, mat, vec, vec],
        out_shape=[SDS((HEADS, SEQ, HEAD_DIM), F32), SDS((SEQ, NA_WIDTH), F32), SDS((D_MODEL, D_MODEL), F32),
                   SDS((1, NA_WIDTH), F32), SDS((1, NA_WIDTH), F32)],
        name="mix_out_bwd", compiler_params=_params(("arbitrary",)))(dmix, ona, os5, g_na, g_s5, wout)


def _me():
    x, y, c = lax.axis_index("x"), lax.axis_index("y"), lax.axis_index("c")
    return x, y, c, 4 * x + 2 * y + c


def _peer(k):
    x, y, c, _ = _me()
    px = 1 - x if (k >> 2) & 1 else x
    py = 1 - y if (k >> 1) & 1 else y
    pc = 1 - c if k & 1 else c
    return (px, py, pc), 4 * px + 2 * py + pc


ALL_PEERS = (1, 2, 3, 4, 5, 6, 7)
CHIP_PEERS = (2, 4, 6)
SIBLING = 1


def _slot8(pos):
    return 4 * pos[0] + 2 * pos[1] + pos[2]


def _slot4(pos):
    return 2 * pos[0] + pos[1]


def _exchange(arrays, gather, name, after=()):
    n, n_after = len(arrays), len(after)

    def body(*refs):
        ins, outs = refs[:n], refs[n + n_after:2 * n + n_after]
        token = refs[2 * n + n_after]
        send_sems, recv_sems, local_sems = refs[2 * n + n_after + 1:]
        token[...] = jnp.zeros_like(token)
        _, _, _, me = _me()
        started = []
        for a in range(n):
            src_mine = ins[a] if gather else ins[a].at[me]
            local = pltpu.make_async_copy(src_mine, outs[a].at[me], local_sems.at[a])
            local.start()
            started.append(local)
        sends = []
        for k in range(1, N_DEV):
            peer, peer_idx = _peer(k)
            for a in range(n):
                src = ins[a] if gather else ins[a].at[peer_idx]
                cp = pltpu.make_async_remote_copy(src_ref=src, dst_ref=outs[a].at[me], send_sem=send_sems.at[a, k - 1],
                                                  recv_sem=recv_sems.at[a, k - 1], device_id=peer, device_id_type=MESH)
                cp.start()
                sends.append(cp)
        for k in range(1, N_DEV):
            peer, peer_idx = _peer(k)
            for a in range(n):
                src = ins[a] if gather else ins[a].at[peer_idx]
                pltpu.make_async_remote_copy(src_ref=src, dst_ref=outs[a].at[peer_idx], send_sem=send_sems.at[a, k - 1],
                                             recv_sem=recv_sems.at[a, k - 1], device_id=peer, device_id_type=MESH).wait_recv()
        for cp in sends:
            cp.wait_send()
        for local in started:
            local.wait()

    hbm = pl.BlockSpec(memory_space=pltpu.HBM)
    out_shape = [SDS((N_DEV,) + tuple(a.shape), a.dtype) if gather else SDS(a.shape, a.dtype) for a in arrays]
    out = pl.pallas_call(
        body, in_specs=[hbm] * n + [pl.BlockSpec(memory_space=pl.ANY)] * n_after,
        out_specs=[hbm] * n + [pl.BlockSpec(memory_space=pltpu.VMEM)], out_shape=out_shape + [SDS((8, 128), F32)],
        scratch_shapes=[pltpu.SemaphoreType.DMA((n, N_DEV - 1)), pltpu.SemaphoreType.DMA((n, N_DEV - 1)),
                        pltpu.SemaphoreType.DMA((n,))],
        name=name)(*arrays, *after)
    return list(out[:n]), out[n]


_HBM = pl.BlockSpec(memory_space=pltpu.HBM)
_SEM = pl.BlockSpec(memory_space=pltpu.SEMAPHORE)
_EFFECT = pltpu.SideEffectType.DATAFLOW_SIDE_EFFECTING


def _land_shape(a, gather):
    return (N_DEV,) + tuple(a.shape) if gather else tuple(a.shape)


def _place_own(arrays, gather, name, slot=_slot8):
    n = len(arrays)
    me = slot(_me()[:3])

    def body(me_ref, *refs):
        for a in range(n):
            refs[n + a][...] = refs[a][...]

    def own_slot(a):
        zeros = (0,) * (a.ndim - (0 if gather else 1))
        return lambda i, me_ref: (me_ref[0],) + zeros

    def whole(a):
        return lambda i, me_ref: (0,) * a.ndim

    in_specs = [pl.BlockSpec(a.shape, whole(a)) if gather else pl.BlockSpec((None,) + a.shape[1:], own_slot(a)) for a in arrays]
    out_specs = [pl.BlockSpec((None,) + (a.shape if gather else a.shape[1:]), own_slot(a)) for a in arrays]
    return pl.pallas_call(
        body, grid_spec=pltpu.PrefetchScalarGridSpec(num_scalar_prefetch=1, grid=(1,), in_specs=in_specs, out_specs=out_specs),
        out_shape=[SDS(_land_shape(a, gather), a.dtype) for a in arrays], name=name,
        compiler_params=_params(("arbitrary",)))(me.reshape(1).astype(jnp.int32), *arrays)


def _exchange_start(arrays, lands, gather, name, peers=ALL_PEERS, slot=_slot8):
    n = len(arrays)

    def body(*refs):
        ins, lnd = refs[:n], refs[n:2 * n]
        send_sems, recv_sems = refs[2 * n], refs[2 * n + 1]
        token = refs[-1]
        me = slot(_me()[:3])
        for i, k in enumerate(peers):
            peer, _ = _peer(k)
            for a in range(n):
                src = ins[a] if gather else ins[a].at[slot(peer)]
                s = a * len(peers) + i
                pltpu.make_async_remote_copy(src_ref=src, dst_ref=lnd[a].at[me], send_sem=send_sems.at[s],
                                             recv_sem=recv_sems.at[s], device_id=peer, device_id_type=MESH).start()
        token[...] = jnp.zeros_like(token)

    sems = pltpu.SemaphoreType.DMA((n * len(peers),))
    out = pl.pallas_call(
        body, name=name, in_specs=[_HBM] * (2 * n),
        out_shape=(sems, sems) + tuple(pltpu.HBM(a.shape, a.dtype) for a in list(arrays) + list(lands)) + (SDS((8, 128), F32),),
        out_specs=(_SEM, _SEM) + (_HBM,) * (2 * n) + (pl.BlockSpec(memory_space=pltpu.VMEM),),
        input_output_aliases={i: 2 + i for i in range(2 * n)},
        compiler_params=pltpu.CompilerParams(has_side_effects=_EFFECT),
    )(*[pltpu.with_memory_space_constraint(a, pltpu.HBM) for a in list(arrays) + list(lands)])
    return out[0], out[1], list(out[2:2 + n]), list(out[2 + n:2 + 2 * n]), out[-1]


def _exchange_wait(send_sems, recv_sems, arrays, lands, after, gather, name, peers=ALL_PEERS, slot=_slot8):
    n = len(arrays)

    def body(*refs):
        ins, lnd = refs[:n], refs[n:2 * n]
        send_sems, recv_sems = refs[2 * n], refs[2 * n + 1]
        for i, k in enumerate(peers):
            peer, _ = _peer(k)
            for a in range(n):
                src = ins[a] if gather else ins[a].at[slot(peer)]
                s = a * len(peers) + i
                cp = pltpu.make_async_remote_copy(src_ref=src, dst_ref=lnd[a].at[slot(peer)], send_sem=send_sems.at[s],
                                                  recv_sem=recv_sems.at[s], device_id=peer, device_id_type=MESH)
                cp.wait_send()
                cp.wait_recv()

        refs[-1][...] = jnp.zeros_like(refs[-1])

    after = list(after) if isinstance(after, (list, tuple)) else [after]
    out = pl.pallas_call(
        body, name=name, in_specs=[_HBM] * (2 * n) + [_SEM, _SEM] + [pl.BlockSpec(memory_space=pl.ANY)] * len(after),
        out_shape=tuple(pltpu.HBM(a.shape, a.dtype) for a in list(arrays) + list(lands)) + (SDS((8, 128), F32),),
        out_specs=(_HBM,) * (2 * n) + (pl.BlockSpec(memory_space=pltpu.VMEM),), input_output_aliases={i: i for i in range(2 * n)},
        compiler_params=pltpu.CompilerParams(has_side_effects=_EFFECT),
    )(*arrays, *lands, send_sems, recv_sems, *after)
    return list(out[n:2 * n]), out[-1]


def _forward_sibling(lands, name):
    n = len(lands)

    def body(*refs):
        outs = refs[n:2 * n]
        send_sems, recv_sems = refs[2 * n:]
        x, y, c, _ = _me()
        sends = []
        for i, k in enumerate(CHIP_PEERS):
            peer, _ = _peer(k)
            for a in range(n):
                rows = outs[a].at[_slot8(peer)]
                cp = pltpu.make_async_remote_copy(src_ref=rows, dst_ref=rows, send_sem=send_sems.at[a, i], recv_sem=recv_sems.at[a, i],
                                                  device_id=(x, y, 1 - c), device_id_type=MESH)
                cp.start()
                sends.append(cp)
        for i, k in enumerate(CHIP_PEERS):
            (px, py, pc), _ = _peer(k)
            for a in range(n):
                rows = outs[a].at[_slot8((px, py, 1 - pc))]
                pltpu.make_async_remote_copy(src_ref=rows, dst_ref=rows, send_sem=send_sems.at[a, i], recv_sem=recv_sems.at[a, i],
                                             device_id=(x, y, 1 - c), device_id_type=MESH).wait_recv()
        for cp in sends:
            cp.wait_send()

    return pl.pallas_call(
        body, in_specs=[_HBM] * n, out_specs=[_HBM] * n, out_shape=[SDS(a.shape, a.dtype) for a in lands],
        input_output_aliases={i: i for i in range(n)},
        scratch_shapes=[pltpu.SemaphoreType.DMA((n, len(CHIP_PEERS))), pltpu.SemaphoreType.DMA((n, len(CHIP_PEERS)))],
        name=name)(*lands)


def _swap_sibling(arrays, name):
    n = len(arrays)
    chips = N_DEV // 2

    def body(*refs):
        ins, outs = refs[:n], refs[n:2 * n]
        send_sems, recv_sems = refs[2 * n:]
        x, y, c, _ = _me()
        sends = []
        for q in range(chips):
            for a in range(n):
                cp = pltpu.make_async_remote_copy(src_ref=ins[a].at[q, 1 - c], dst_ref=outs[a].at[q], send_sem=send_sems.at[a, q],
                                                  recv_sem=recv_sems.at[a, q], device_id=(x, y, 1 - c), device_id_type=MESH)
                cp.start()
                sends.append(cp)
        for cp in sends:
            cp.wait_recv()
        for cp in sends:
            cp.wait_send()

    return pl.pallas_call(
        body, in_specs=[_HBM] * n, out_specs=[_HBM] * n, out_shape=[SDS((chips,) + a.shape[2:], a.dtype) for a in arrays],
        scratch_shapes=[pltpu.SemaphoreType.DMA((n, chips)), pltpu.SemaphoreType.DMA((n, chips))], name=name)(*arrays)


def _sum_pairs(mine, theirs, name):
    chips, _, rows, cols = mine.shape
    c = lax.axis_index("c")

    def body(c_ref, a_ref, b_ref, o_ref):
        o_ref[...] = (a_ref[...].astype(F32) + b_ref[...].astype(F32)).astype(o_ref.dtype)

    return pl.pallas_call(
        body, grid_spec=pltpu.PrefetchScalarGridSpec(
            num_scalar_prefetch=1, grid=(chips,),
            in_specs=[pl.BlockSpec((None, None, rows, cols), lambda q, c_ref: (q, c_ref[0], 0, 0)),
                      pl.BlockSpec((None, rows, cols), lambda q, c_ref: (q, 0, 0))],
            out_specs=pl.BlockSpec((None, rows, cols), lambda q, c_ref: (q, 0, 0))),
        out_shape=SDS((chips, rows, cols), mine.dtype), name=name,
        compiler_params=_params(("parallel",)))(c.reshape(1).astype(jnp.int32), mine, theirs)


def _adamw_math(w, g, m, v):
    m = ADAM_B1 * m + (1.0 - ADAM_B1) * g
    v = ADAM_B2 * v + (1.0 - ADAM_B2) * (g * g)
    m_hat = m / (1.0 - ADAM_B1 ** ADAM_STEP)
    v_hat = v / (1.0 - ADAM_B2 ** ADAM_STEP)
    delta = -ADAM_LR * (m_hat / (jnp.sqrt(v_hat) + ADAM_EPS) + ADAM_WD * w)
    return delta, m, v


def _adamw(w, m, v, pieces, name):
    rows, cols = w.shape[-2:]
    lead = w.ndim - 2
    tile = rows
    for cand in (256, 176, 128, 64, 16):
        if rows > cand and rows % cand == 0:
            tile = cand
            break

    def body(w_ref, m_ref, v_ref, p_ref, g_ref, d_ref, mo_ref, vo_ref):
        g = _sum_pieces(p_ref)
        g_ref[...] = g
        d_ref[...], mo_ref[...], vo_ref[...] = _adamw_math(w_ref[...], g, m_ref[...], v_ref[...])

    blk = pl.BlockSpec((None,) * lead + (tile, cols), lambda i: (0,) * lead + (i, 0))
    return pl.pallas_call(
        body, grid=(rows // tile,), in_specs=[blk, blk, blk, pl.BlockSpec((pieces.shape[0], tile, cols), lambda i: (0, i, 0))],
        out_specs=[blk] * 4, out_shape=[SDS(w.shape, F32)] * 4, name=name,
        compiler_params=_params(("parallel",)))(w, m, v, pieces)


def _sum_pieces(p_ref):
    g = p_ref[0].astype(F32)
    for p in range(1, p_ref.shape[0]):
        g = g + p_ref[p].astype(F32)
    return g


def _adamw_s5_mat(w, m, v, g, name):
    _, ndir, groups, b, c = w.shape
    per_dir = groups // 8

    def body(w_ref, m_ref, v_ref, g_ref, d_ref, mo_ref, vo_ref):
        d_ref[...], mo_ref[...], vo_ref[...] = _adamw_math(w_ref[...], g_ref[...], m_ref[...], v_ref[...])

    blk = pl.BlockSpec((None, None, 8, b, c), lambda i: (0, i // per_dir, i % per_dir, 0, 0))
    return pl.pallas_call(
        body, grid=(ndir * per_dir,), in_specs=[blk] * 4, out_specs=[blk] * 3, out_shape=[SDS(w.shape, F32)] * 3, name=name,
        compiler_params=_params(("parallel",)))(w, m, v, g)


VEC_ROWS = ['ffn1_pre_g', 'ffn1_post_g', 'mix_pre_g', 'mix_post_g', 'ffn2_pre_g', 'ffn2_post_g', 'final_g',
            ('na_out_g', 's5_out_g'), ('s5_d', 's5_b_glu')]
VEC_NAMES = [n for row in VEC_ROWS for n in ((row,) if isinstance(row, str) else row)]
VEC_PACK_ROWS = 16
LOSS_ROW = len(VEC_ROWS)


def _pack_vectors(grads, loss8):
    def body(*refs):
        o_ref = refs[-1]
        o_ref[...] = jnp.zeros_like(o_ref)
        o_ref[LOSS_ROW:LOSS_ROW + 1, 0:128] = refs[-2][0:1, :]
        k = 0
        for i, row in enumerate(VEC_ROWS):
            if isinstance(row, str):
                o_ref[i:i + 1, :] = refs[k][...]
                k += 1
            else:
                o_ref[i:i + 1, 0:NA_WIDTH] = refs[k][...]
                o_ref[i:i + 1, NA_WIDTH:] = refs[k + 1][...]
                k += 2

    return pl.pallas_call(body, out_shape=SDS((VEC_PACK_ROWS, D_MODEL), F32), name="pack_vectors",
                          compiler_params=_params())(*[grads[n] for n in VEC_NAMES], loss8)


def _sum8(pieces, name):
    def body(p_ref, o_ref):
        o_ref[...] = _sum_pieces(p_ref)

    return pl.pallas_call(body, out_shape=SDS(pieces.shape[1:], F32), name=name, compiler_params=_params())(pieces)


def _adamw_small(packed8, vec_wmv, others):
    n_vec, n_oth = len(VEC_NAMES), len(others)

    def body(*refs):
        p_ref = refs[0]
        ins = refs[1:1 + 3 * n_vec + 4 * n_oth]
        outs = refs[1 + 3 * n_vec + 4 * n_oth:]
        gsum = _sum_pieces(p_ref)
        outs[-1][...] = gsum[LOSS_ROW:LOSS_ROW + 1, 0:128]
        k = 0
        for i, row in enumerate(VEC_ROWS):
            parts = [(row, gsum[i:i + 1, :])] if isinstance(row, str) else \
                [(row[0], gsum[i:i + 1, 0:NA_WIDTH]), (row[1], gsum[i:i + 1, NA_WIDTH:])]
            for _, g in parts:
                w_ref, m_ref, v_ref = ins[3 * k:3 * k + 3]
                outs[4 * k][...] = g
                outs[4 * k + 1][...], outs[4 * k + 2][...], outs[4 * k + 3][...] = _adamw_math(w_ref[...], g, m_ref[...], v_ref[...])
                k += 1
        for j in range(n_oth):
            w_ref, m_ref, v_ref, g_ref = ins[3 * n_vec + 4 * j:3 * n_vec + 4 * j + 4]
            g = _sum_pieces(g_ref)
            g = g[tuple(slice(0, s) for s in w_ref.shape[1:])].reshape(w_ref.shape)
            o = outs[4 * (n_vec + j):4 * (n_vec + j) + 4]
            o[0][...] = g
            o[1][...], o[2][...], o[3][...] = _adamw_math(w_ref[...], g, m_ref[...], v_ref[...])

    args, out_shape = [packed8], []
    for w, m, v in vec_wmv:
        args += [w, m, v]
        out_shape += [SDS(w.shape, F32)] * 4
    for w, m, v, g in others:
        args += [w, m, v, g]
        out_shape += [SDS(w.shape, F32)] * 4
    out_shape += [SDS((1, 128), F32)]
    return pl.pallas_call(body, out_shape=out_shape, name="adamw_small", compiler_params=_params())(*args)


def _perm_rows(x):
    return x.reshape(SCAN_BLOCKS, SCAN_T, x.shape[-1]).transpose(1, 0, 2).reshape(SEQ, x.shape[-1])


def _unperm_rows(x):
    return x.reshape(SCAN_T, SCAN_BLOCKS, x.shape[-1]).transpose(1, 0, 2).reshape(SEQ, x.shape[-1])


def _block_diag(x):
    eye = np.eye(8, dtype=bool)[None, None, :, None, :, None]
    full = jnp.where(eye, x[:, :, :, :, None, :], 0.0)
    return full.reshape(2, S5_CHUNKS, 8 * x.shape[3], 8 * x.shape[4])


def _diag_blocks(x, r, c):
    x6 = x.reshape(2, S5_CHUNKS, 8, r, 8, c)
    return jnp.stack([x6[:, :, g, :, g, :] for g in range(8)], axis=2)


STORED_SWAPPED = {"ffn1_w_gate": (1, 2), "ffn1_w_up": (1, 2), "ffn2_w_gate": (1, 2), "ffn2_w_up": (1, 2),
                  "s5_b_re": (3, 4), "s5_b_im": (3, 4)}


def _stored(name, x):
    return jnp.swapaxes(x, *STORED_SWAPPED[name]) if name in STORED_SWAPPED else x


def _dep(x, token):
    return x if token is None else x + token


def _local_step(x, target, get_w, small, emit):
    bias = _rpb_expand(small["na_rpb"][0])
    lr = small["s5_lam_re"].reshape(64, S5_STATE)
    li = small["s5_lam_im"].reshape(64, S5_STATE)
    logdt = small["s5_log_dt"].reshape(64, 1)
    b_t = [_stored(n, small[n]).reshape(64, S5_GROUP, S5_STATE) for n in ("s5_b_re", "s5_b_im")]
    lbr, lbi, bbr, bbi = _s5_prep(lr, li, logdt, b_t[0], b_t[1])
    are = lbr.reshape(2, S5_CHUNKS, 1, ST_W)
    aim = lbi.reshape(2, S5_CHUNKS, 1, ST_W)
    bre = _block_diag(bbr.reshape(2, S5_CHUNKS, 8, S5_GROUP, S5_STATE)).astype(BF16)
    bim = _block_diag(bbi.reshape(2, S5_CHUNKS, 8, S5_GROUP, S5_STATE)).astype(BF16)
    c_t = [small[n].reshape(2, S5_CHUNKS, 8, S5_GROUP, S5_STATE).transpose(0, 1, 2, 4, 3) for n in ("s5_c_re", "s5_c_im")]
    cre = _block_diag(c_t[0]).astype(BF16)
    cim = _block_diag(c_t[1]).astype(BF16)
    tgt = jnp.concatenate([jnp.zeros((N_META, D_MODEL), F32), target], axis=0)

    h0, a1 = _embed_prenorm(get_w("meta", None)["meta_tokens"], x, small["ffn1_pre_g"])
    wts = dict(get_w("ffn1", [bias, are, aim, bre, bim, cre, cim, tgt, a1]))
    gate1, up1, f1 = _ffn_fwd(a1, wts["ffn1_w_gate"], wts["ffn1_w_up"], wts["ffn1_w_down"], "ffn1_fwd",
                              after=wts.get("tokens", ()))
    h1, a2 = _post_pre(f1, h0, small["ffn1_post_g"], small["mix_pre_g"], 0.5, "post_pre1")
    wts.update(get_w("w_in", a2))
    qkv = _proj_heads(a2, wts["w_in"])
    u = _proj_u(a2, wts["w_in"])
    ona = _na_fwd(qkv, bias)
    u_p = _perm_rows(u)
    sr, si, y2 = _s5_scan_fwd(u_p, bre, bim, are, aim, cre, cim)
    wts.update(get_w("mix", y2))
    os5_p, ypre_p = _s5_glu_fwd(u_p, y2, small["s5_d"], wts["s5_w_glu"], small["s5_b_glu"])
    os5 = _unperm_rows(os5_p)

    mix = _mix_out_fwd(ona, os5, small["na_out_g"], small["s5_out_g"], wts["w_out"])
    h2, a3 = _post_pre(mix, h1, small["mix_post_g"], small["ffn2_pre_g"], 1.0, "post_pre2")
    wts.update(get_w("ffn2", a3))
    gate2, up2, f2 = _ffn_fwd(a3, wts["ffn2_w_gate"], wts["ffn2_w_up"], wts["ffn2_w_down"], "ffn2_fwd")
    loss8, dh3, df2, g_final, g_ffn2_post = _final_loss(f2, h2, small["ffn2_post_g"], small["final_g"], tgt)

    da3, dwg2, dwu2, dwd2 = _ffn_bwd(df2, a3, gate2, up2, wts["ffn2_w_gate"], wts["ffn2_w_up"], wts["ffn2_w_down"], "ffn2_bwd")
    tok = emit("ffn2", {"ffn2_w_gate": dwg2, "ffn2_w_up": dwu2, "ffn2_w_down": dwd2})
    dh2, dmix, g_ffn2_pre, g_mix_post = _bwd_pre_post(da3, h2, _dep(small["ffn2_pre_g"], tok), dh3, mix, small["mix_post_g"], 1.0,
                                                      "bwd_pre_post2")
    dona, dos5, dwout, g_na_out, g_s5_out = _mix_out_bwd(dmix, ona, os5, small["na_out_g"], small["s5_out_g"], wts["w_out"])

    dypre_p, du_skip_p, dwglu, g_b_glu, g_s5_d = _s5_glu_bwd(_perm_rows(dos5), ypre_p, u_p, small["s5_d"], wts["s5_w_glu"],
                                                             small["s5_b_glu"])
    tok = emit("mix", {"s5_w_glu": dwglu.reshape(N_DEV, S5_WIDTH // N_DEV, S5_WIDTH).astype(BF16),
                       "w_out": dwout.reshape(N_DEV, D_MODEL // N_DEV, D_MODEL).astype(BF16)})
    du_p, dbr, dbi, dcr, dci, dar, dai = _s5_scan_bwd(dypre_p, du_skip_p, u_p, sr, si, bre, bim, _dep(are, tok), aim, cre, cim)
    du = _unperm_rows(du_p)
    dbbr = _diag_blocks(dbr, S5_GROUP, S5_STATE).reshape(64, S5_GROUP, S5_STATE)
    dbbi = _diag_blocks(dbi, S5_GROUP, S5_STATE).reshape(64, S5_GROUP, S5_STATE)
    g_lr, g_li, g_dt, g_br, g_bi = _s5_prep_bwd(lr, li, logdt, b_t[0], b_t[1], dar.reshape(64, S5_STATE),
                                                dai.reshape(64, S5_STATE), dbbr, dbbi)
    g_c = [_diag_blocks(d, S5_STATE, S5_GROUP).transpose(0, 1, 2, 4, 3).reshape(2 * S5_GROUPS, S5_GROUP, S5_STATE)
           for d in (dcr, dci)]

    dq, dk, dv, dbias = _na_bwd(qkv, bias, dona)
    g_rpb = _rpb_reduce(dbias)
    dense = jnp.stack([g.reshape(2 * S5_GROUPS, S5_STATE * S5_GROUP) for g in (g_br, g_bi, *g_c)])
    tok = emit("small", {"dense": dense, "na_rpb": g_rpb,
                         "s5_lam_re": g_lr.reshape(2, S5_GROUPS, S5_STATE), "s5_lam_im": g_li.reshape(2, S5_GROUPS, S5_STATE),
                         "s5_log_dt": g_dt.reshape(2, S5_GROUPS)})
    da2, dwin = _proj_bwd(dq, dk, dv, du, a2, wts["w_in"])
    tok2 = emit("w_in", {"w_in": dwin})
    tok = tok if tok2 is None else tok + tok2
    dh1, df1, g_mix_pre, g_ffn1_post = _bwd_pre_post(da2, h1, _dep(small["mix_pre_g"], tok), dh2, f1, small["ffn1_post_g"], 0.5,
                                                     "bwd_pre_post1")
    da1, dwg1, dwu1, dwd1 = _ffn_bwd(df1, a1, gate1, up1, wts["ffn1_w_gate"], wts["ffn1_w_up"], wts["ffn1_w_down"], "ffn1_bwd")
    emit("ffn1", {"ffn1_w_gate": dwg1, "ffn1_w_up": dwu1, "ffn1_w_down": dwd1})
    grad_x, grad_meta, g_ffn1_pre = _bwd_embed(da1, h0, small["ffn1_pre_g"], dh1)

    vec_g = {
        "ffn1_pre_g": g_ffn1_pre, "ffn1_post_g": g_ffn1_post, "mix_pre_g": g_mix_pre, "s5_d": g_s5_d, "s5_b_glu": g_b_glu,
        "na_out_g": g_na_out, "s5_out_g": g_s5_out, "mix_post_g": g_mix_post,
        "ffn2_pre_g": g_ffn2_pre, "ffn2_post_g": g_ffn2_post, "final_g": g_final,
    }
    return loss8, grad_x, grad_meta, vec_g


WEIGHT_NAMES = ['meta_tokens', 'ffn1_pre_g', 'ffn1_post_g', 'ffn1_w_gate', 'ffn1_w_up', 'ffn1_w_down', 'mix_pre_g', 'w_in',
                'na_rpb', 's5_lam_re', 's5_lam_im', 's5_log_dt', 's5_b_re', 's5_b_im', 's5_c_re', 's5_c_im', 's5_d',
                's5_w_glu', 's5_b_glu', 'na_out_g', 's5_out_g', 'w_out', 'mix_post_g', 'ffn2_pre_g', 'ffn2_post_g',
                'ffn2_w_gate', 'ffn2_w_up', 'ffn2_w_down', 'final_g']
BIG_NAMES = ['ffn1_w_gate', 'ffn1_w_up', 'ffn1_w_down', 'w_in', 's5_w_glu', 'w_out', 'ffn2_w_gate', 'ffn2_w_up', 'ffn2_w_down']
SMALL_NAMES = [n for n in WEIGHT_NAMES if n not in BIG_NAMES and n != 'meta_tokens']
WHOLE_NAMES = ['na_rpb', 's5_lam_re', 's5_lam_im', 's5_log_dt']
LEAD_NAMES = ['s5_b_re', 's5_b_im', 's5_c_re', 's5_c_im']


def kernel(x, meta_tokens, ffn1_pre_g, ffn1_post_g, ffn1_w_gate, ffn1_w_up, ffn1_w_down, mix_pre_g, w_in, na_rpb, s5_lam_re, s5_lam_im, s5_log_dt, s5_b_re, s5_b_im, s5_c_re, s5_c_im, s5_d, s5_w_glu, s5_b_glu, na_out_g, s5_out_g, w_out, mix_post_g, ffn2_pre_g, ffn2_post_g, ffn2_w_gate, ffn2_w_up, ffn2_w_down, final_g, loss_target, m_meta_tokens, m_ffn1_pre_g, m_ffn1_post_g, m_ffn1_w_gate, m_ffn1_w_up, m_ffn1_w_down, m_mix_pre_g, m_w_in, m_na_rpb, m_s5_lam_re, m_s5_lam_im, m_s5_log_dt, m_s5_b_re, m_s5_b_im, m_s5_c_re, m_s5_c_im, m_s5_d, m_s5_w_glu, m_s5_b_glu, m_na_out_g, m_s5_out_g, m_w_out, m_mix_post_g, m_ffn2_pre_g, m_ffn2_post_g, m_ffn2_w_gate, m_ffn2_w_up, m_ffn2_w_down, m_final_g, v_meta_tokens, v_ffn1_pre_g, v_ffn1_post_g, v_ffn1_w_gate, v_ffn1_w_up, v_ffn1_w_down, v_mix_pre_g, v_w_in, v_na_rpb, v_s5_lam_re, v_s5_lam_im, v_s5_log_dt, v_s5_b_re, v_s5_b_im, v_s5_c_re, v_s5_c_im, v_s5_d, v_s5_w_glu, v_s5_b_glu, v_na_out_g, v_s5_out_g, v_w_out, v_mix_post_g, v_ffn2_pre_g, v_ffn2_post_g, v_ffn2_w_gate, v_ffn2_w_up, v_ffn2_w_down, v_final_g):
    args = dict(locals())
    w = {n: args[n] for n in WEIGHT_NAMES}
    m = {n: args["m_" + n] for n in WEIGHT_NAMES}
    v = {n: args["v_" + n] for n in WEIGHT_NAMES}

    small = {n: w[n] for n in SMALL_NAMES}

    pending = {}

    def start(group, names, arrays, gather, peers=ALL_PEERS, slot=_slot8):
        lands = _place_own(arrays, gather, "own_" + group, slot)
        send_sems, recv_sems, arrays, lands, token = _exchange_start(arrays, lands, gather, "start_" + group, peers, slot)
        pending[group] = (names, send_sems, recv_sems, arrays, lands, gather, peers, slot)
        return token

    def finish(group, after):
        names, send_sems, recv_sems, arrays, lands, gather, peers, slot = pending.pop(group)
        lands, token = _exchange_wait(send_sems, recv_sems, arrays, lands, after, gather, "wait_" + group, peers, slot)
        return dict(zip(names, lands)), token

    first = ["ffn1_w_gate", "ffn1_w_up", "ffn1_w_down"]
    def shard(n, token=None):
        return _dep(_stored(n, w[n])[0], None if token is None else token[0, 0]).astype(BF16)

    ffn_names = ("ffn1_w_gate", "ffn1_w_up", "ffn1_w_down", "ffn2_w_gate", "ffn2_w_up", "ffn2_w_down")
    later_groups = (("w_in", ["w_in"]), ("mix", ["s5_w_glu", "w_out"]), ("ffn2", ["ffn2_w_gate", "ffn2_w_up", "ffn2_w_down"]))
    (meta_full,), token0 = _exchange([w["meta_tokens"]], True, "gather_meta")
    token1 = start("ffn1", first, [shard(n, token0) for n in first], True, (SIBLING,) + CHIP_PEERS)
    meta_full = _dep(meta_full.transpose(1, 0, 2).reshape(N_META, D_MODEL), token1[0, 0])
    later_shards = {n: shard(n, token1) for _, names in later_groups for n in names}
    for n in ("na_rpb", "s5_lam_re"):
        small[n] = _dep(small[n], token1[0, 0])

    def get_w(group, after):
        if group == "meta":
            return {"meta_tokens": meta_full}
        if group == "ffn1":
            after = list(after) + list(later_shards.values())
        got, token = finish(group, after)
        if group == "ffn1":
            got = dict(zip(got, _forward_sibling(list(got.values()), "forward_ffn1")))
            got["tokens"] = [start(g, names + ["order"], [later_shards[n] for n in names] + [token], True) for g, names in later_groups]
        if group == "mix":
            got = {"s5_w_glu": got["s5_w_glu"].reshape(S5_WIDTH, S5_WIDTH), "w_out": got["w_out"].reshape(D_MODEL, D_MODEL)}
        return {n: (a.reshape(D_FF, D_MODEL) if n in ffn_names else a) for n, a in got.items()}

    tokens = {}

    def emit(group, grads):
        grads = {n: (g.reshape(N_DEV, FF_SHARD, D_MODEL) if n in ffn_names else g) for n, g in grads.items()}
        if group == "ffn1":
            mine = [g.reshape((N_DEV // 2, 2) + g.shape[1:]) for g in grads.values()]
            theirs = _swap_sibling(mine, "swap_g_ffn1")
            sums = [_sum_pairs(a, b, "pair_sum_" + n) for n, a, b in zip(grads, mine, theirs)]
            tokens[group] = start("g_ffn1", list(grads), sums, False, CHIP_PEERS, _slot4)
        else:
            tokens[group] = start("g_" + group, list(grads), list(grads.values()), group == "small")
        return tokens[group][0, 0]

    loss8, grad_x, gmeta, vec_g = _local_step(x[0], loss_target[0], get_w, small, emit)
    res = {}

    def update_shard(n, pieces):
        outs = _adamw(_stored(n, w[n]), _stored(n, m[n]), _stored(n, v[n]), pieces, "adamw_" + n)
        res[n] = [_stored(n, o) for o in outs]

    late = [grad_x, tokens["ffn1"]]
    for group in ("g_ffn2", "g_mix", "g_w_in"):
        for n, pieces in finish(group, late)[0].items():
            update_shard(n, pieces)
    g8 = finish("g_small", late)[0]
    dense = _sum8(g8["dense"], "sum_dense")
    for i, n in enumerate(LEAD_NAMES):
        g = dense[i].reshape(_stored(n, w[n]).shape)
        upd = _adamw_s5_mat(_stored(n, w[n]), _stored(n, m[n]), _stored(n, v[n]), g, "adamw_" + n)
        res[n] = [_stored(n, o) for o in [g] + list(upd)]

    done = [res[n][1] for n in ("ffn2_w_gate", "ffn2_w_up", "ffn2_w_down", "w_in", "w_out", "s5_w_glu") + tuple(LEAD_NAMES)]
    (packed8, gmeta8), _ = _exchange([_pack_vectors(vec_g, loss8), gmeta], True, "gather_vectors", after=done)
    for n, pieces in finish("g_ffn1", packed8)[0].items():
        update_shard(n, pieces)
    _, _, _, me = _me()
    update_shard("meta_tokens", lax.dynamic_slice_in_dim(gmeta8, me * (D_MODEL // N_DEV), D_MODEL // N_DEV, axis=2))

    outs = _adamw_small(packed8, [(w[n], m[n], v[n]) for n in VEC_NAMES], [(w[n], m[n], v[n], g8[n]) for n in WHOLE_NAMES])
    for i, n in enumerate(VEC_NAMES + WHOLE_NAMES):
        res[n] = list(outs[4 * i:4 * i + 4])

    out = [outs[-1][0, 0], grad_x[None]]
    for kind in range(4):
        out += [res[n][kind] for n in WEIGHT_NAMES]
    return tuple(out)
```

```python
import functools
import math

import numpy as np
import jax
import jax.numpy as jnp
from jax import lax
from jax.experimental import pallas as pl
from jax.experimental.pallas import tpu as pltpu

F32 = jnp.float32
BF16 = jnp.bfloat16
SDS = jax.ShapeDtypeStruct

D_MODEL = 1024
N_TOK = 2048
N_META = 16
SEQ = N_TOK + N_META
ROW_TILE = 688
N_ROW_TILES = SEQ // ROW_TILE
N_DEV = 8
D_FF = 2816
FF_SHARD = D_FF // N_DEV
FF_TILE = 256
IN_SHARD = 256
NA_WIDTH = 512
S5_WIDTH = 512
HEADS = 8
HEAD_DIM = 64
GRID_W = 64
GRID_ROWS = N_TOK // GRID_W
KH = 8
KW = 16
NA_RB = 4
NA_KR = KH + NA_RB - 1
NA_BLOCKS = GRID_ROWS // NA_RB
NA_QB = NA_RB * GRID_W
NA_KB = NA_KR * GRID_W
NA_TYPES = 3
S5_GROUPS = 32
S5_GROUP = 16
S5_STATE = 64
S5_CHUNKS = 4
CH_W = S5_WIDTH // S5_CHUNKS
ST_W = S5_GROUPS * S5_STATE // S5_CHUNKS
SCAN_BLOCKS = 8
SCAN_T = SEQ // SCAN_BLOCKS
RMS_EPS = 1e-6
NEG_INF = -1e30
ATT_SCALE = HEAD_DIM ** -0.5
ADAM_LR, ADAM_B1, ADAM_B2, ADAM_EPS, ADAM_WD, ADAM_STEP = 0.001, 0.9, 0.999, 1e-08, 0.01, 10
VMEM_LIMIT = 56 * 1024 * 1024
MESH = pl.DeviceIdType.MESH
AXES = ("x", "y", "c")


def _params(sem=None):
    return pltpu.CompilerParams(dimension_semantics=sem, vmem_limit_bytes=VMEM_LIMIT)


def _dot(a, b):
    return jnp.dot(a, b, preferred_element_type=F32)


def _dot_nt(a, b):
    return lax.dot_general(a, b, (((1,), (1,)), ((), ())), preferred_element_type=F32)


def _dot_tn(a, b):
    return lax.dot_general(a, b, (((0,), (0,)), ((), ())), preferred_element_type=F32)


def _rstd(x):
    return lax.rsqrt(jnp.mean(x * x, axis=-1, keepdims=True) + RMS_EPS)


def _rms_bwd(x, r, g, dy):
    dyg = dy * g
    xr = x * r
    dx = r * (dyg - xr * jnp.mean(dyg * xr, axis=-1, keepdims=True))
    return dx, dy * xr


def _rows(i, size=ROW_TILE):
    return pl.ds(pl.multiple_of(i * size, 16), size)


def _row_spec(width):
    return pl.BlockSpec((ROW_TILE, width), lambda i: (i, 0))


def _fix_spec(shape):
    return pl.BlockSpec(shape, lambda i: (0,) * len(shape))


def _split3(x):
    hi = x.astype(BF16)
    r1 = x - hi.astype(F32)
    mid = r1.astype(BF16)
    lo = (r1 - mid.astype(F32)).astype(BF16)
    return hi, mid, lo


def _embed_prenorm(meta, x, g):
    def body(m_ref, x_ref, g_ref, h_ref, a_ref):
        h_ref[0:N_META, :] = m_ref[...]
        h_ref[N_META:, :] = x_ref[...]
        for i in range(N_ROW_TILES):
            rows = slice(i * ROW_TILE, (i + 1) * ROW_TILE)
            hv = h_ref[rows, :]
            a_ref[rows, :] = (hv * _rstd(hv) * g_ref[...]).astype(BF16)

    return pl.pallas_call(
        body, out_shape=[SDS((SEQ, D_MODEL), F32), SDS((SEQ, D_MODEL), BF16)], name="embed_prenorm",
        compiler_params=_params())(meta, x, g)


def _post_pre(f, hres, g_post, g_next, scale, name):
    def body(f_ref, h_ref, gp_ref, gn_ref, ho_ref, a_ref):
        fv = f_ref[...]
        h = h_ref[...] + scale * (fv * _rstd(fv) * gp_ref[...])
        ho_ref[...] = h
        a_ref[...] = (h * _rstd(h) * gn_ref[...]).astype(BF16)

    return pl.pallas_call(
        body, grid=(N_ROW_TILES,),
        in_specs=[_row_spec(D_MODEL), _row_spec(D_MODEL), _fix_spec((1, D_MODEL)), _fix_spec((1, D_MODEL))],
        out_specs=[_row_spec(D_MODEL), _row_spec(D_MODEL)],
        out_shape=[SDS((SEQ, D_MODEL), F32), SDS((SEQ, D_MODEL), BF16)], name=name,
        compiler_params=_params(("parallel",)))(f, hres, g_post, g_next)


def _final_loss(f2, h2, g_post, g_final, target):
    def body(f_ref, h_ref, gp_ref, gf_ref, t_ref, loss_ref, dh_ref, df_ref, dgf_ref, dgp_ref):
        i = pl.program_id(0)
        fv = f_ref[...]
        r1 = _rstd(fv)
        gp = gp_ref[...]
        h3 = h_ref[...] + 0.5 * (fv * r1 * gp)
        r2 = _rstd(h3)
        gf = gf_ref[...]
        y = h3 * r2 * gf
        row = lax.broadcasted_iota(jnp.int32, (ROW_TILE, 1), 0) + i * ROW_TILE
        err = jnp.where(row >= N_META, y - t_ref[...], 0.0)
        part = 0.5 * jnp.sum(jnp.mean(err * err, axis=-1, keepdims=True))
        dy = err * (1.0 / D_MODEL)
        dh3, dgf = _rms_bwd(h3, r2, gf, dy)
        dh_ref[...] = dh3
        df, dgp = _rms_bwd(fv, r1, gp, 0.5 * dh3)
        df_ref[...] = df.astype(BF16)

        @pl.when(i == 0)
        def _():
            loss_ref[...] = jnp.zeros_like(loss_ref)
            dgf_ref[...] = jnp.zeros_like(dgf_ref)
            dgp_ref[...] = jnp.zeros_like(dgp_ref)

        loss_ref[...] += part
        dgf_ref[...] += jnp.sum(dgf, axis=0, keepdims=True)
        dgp_ref[...] += jnp.sum(dgp, axis=0, keepdims=True)

    gain = _fix_spec((1, D_MODEL))
    return pl.pallas_call(
        body, grid=(N_ROW_TILES,),
        in_specs=[_row_spec(D_MODEL), _row_spec(D_MODEL), gain, gain, _row_spec(D_MODEL)],
        out_specs=[_fix_spec((8, 128)), _row_spec(D_MODEL), _row_spec(D_MODEL), gain, gain],
        out_shape=[SDS((8, 128), F32), SDS((SEQ, D_MODEL), F32), SDS((SEQ, D_MODEL), BF16),
                   SDS((1, D_MODEL), F32), SDS((1, D_MODEL), F32)],
        name="final_loss", compiler_params=_params(("arbitrary",)))(f2, h2, g_post, g_final, target)


def _bwd_pre_post(da, h, g_pre, dh_res, fprev, g_post, scale, name):
    def body(da_ref, h_ref, gpre_ref, dhr_ref, f_ref, gpost_ref, dh_ref, df_ref, dgpre_ref, dgpost_ref):
        i = pl.program_id(0)
        hv = h_ref[...]
        dxa, dgpre = _rms_bwd(hv, _rstd(hv), gpre_ref[...], da_ref[...])
        dh = dhr_ref[...] + dxa
        dh_ref[...] = dh
        fv = f_ref[...]
        df, dgpost = _rms_bwd(fv, _rstd(fv), gpost_ref[...], scale * dh)
        df_ref[...] = df.astype(BF16)

        @pl.when(i == 0)
        def _():
            dgpre_ref[...] = jnp.zeros_like(dgpre_ref)
            dgpost_ref[...] = jnp.zeros_like(dgpost_ref)

        dgpre_ref[...] += jnp.sum(dgpre, axis=0, keepdims=True)
        dgpost_ref[...] += jnp.sum(dgpost, axis=0, keepdims=True)

    gain = _fix_spec((1, D_MODEL))
    row = _row_spec(D_MODEL)
    return pl.pallas_call(
        body, grid=(N_ROW_TILES,), in_specs=[row, row, gain, row, row, gain],
        out_specs=[row, row, gain, gain],
        out_shape=[SDS((SEQ, D_MODEL), F32), SDS((SEQ, D_MODEL), BF16), SDS((1, D_MODEL), F32), SDS((1, D_MODEL), F32)],
        name=name, compiler_params=_params(("arbitrary",)))(da, h, g_pre, dh_res, fprev, g_post)


def _bwd_embed(da, h, g_pre, dh_res):
    def body(da_ref, h_ref, gpre_ref, dhr_ref, gx_ref, gm_ref, dgpre_ref):
        total = jnp.zeros((1, D_MODEL), F32)
        for i in range(N_ROW_TILES):
            rows = slice(i * ROW_TILE, (i + 1) * ROW_TILE)
            hv = h_ref[rows, :]
            dxa, dgpre = _rms_bwd(hv, _rstd(hv), gpre_ref[...], da_ref[rows, :])
            dh = dhr_ref[rows, :] + dxa
            total = total + jnp.sum(dgpre, axis=0, keepdims=True)
            if i == 0:
                gm_ref[...] = dh[0:N_META, :]
                gx_ref[0:ROW_TILE - N_META, :] = dh[N_META:, :]
            else:
                gx_ref[i * ROW_TILE - N_META:(i + 1) * ROW_TILE - N_META, :] = dh
        dgpre_ref[...] = total

    return pl.pallas_call(
        body, out_shape=[SDS((N_TOK, D_MODEL), F32), SDS((N_META, D_MODEL), F32), SDS((1, D_MODEL), F32)],
        name="bwd_embed", compiler_params=_params())(da, h, g_pre, dh_res)


def _ffn_fwd(a, wg, wu, wd, name, after=()):
    def body(a_ref, wg_ref, wu_ref, wd_ref, *rest):
        gate_ref, up_ref, f_ref = rest[len(after):]
        j = pl.program_id(0)

        def tile(i, carry):
            rows = _rows(i)
            at = a_ref[rows, :]
            gate = _dot_nt(at, wg_ref[...])
            up = _dot_nt(at, wu_ref[...])
            gate_ref[rows, :] = gate.astype(BF16)
            up_ref[rows, :] = up.astype(BF16)
            act = (gate * jax.nn.sigmoid(gate) * up).astype(BF16)
            contrib = _dot(act, wd_ref[...])

            @pl.when(j == 0)
            def _():
                f_ref[rows, :] = contrib

            @pl.when(j != 0)
            def _():
                f_ref[rows, :] += contrib

            return carry

        lax.fori_loop(0, N_ROW_TILES, tile, 0)

    wtile = pl.BlockSpec((FF_TILE, D_MODEL), lambda j: (j, 0))
    hid = pl.BlockSpec((SEQ, FF_TILE), lambda j: (0, j))
    full = pl.BlockSpec((SEQ, D_MODEL), lambda j: (0, 0))
    return pl.pallas_call(
        body, grid=(D_FF // FF_TILE,), in_specs=[full, wtile, wtile, wtile] + [pl.BlockSpec(memory_space=pl.ANY)] * len(after),
        out_specs=[hid, hid, full],
        out_shape=[SDS((SEQ, D_FF), BF16), SDS((SEQ, D_FF), BF16), SDS((SEQ, D_MODEL), F32)],
        name=name, compiler_params=_params(("arbitrary",)))(a, wg, wu, wd, *after)


def _ffn_bwd(df, a, gate, up, wg, wu, wd, name):
    def body(df_ref, a_ref, gate_ref, up_ref, wg_ref, wu_ref, wd_ref, da_ref, dwg_ref, dwu_ref, dwd_ref,
             acc_g, acc_u, acc_d):
        j = pl.program_id(0)

        def tile(i, carry):
            rows = _rows(i)
            dft = df_ref[rows, :]
            at = a_ref[rows, :]
            gate = gate_ref[rows, :].astype(F32)
            up = up_ref[rows, :].astype(F32)
            dact = _dot_nt(dft, wd_ref[...])
            sig = jax.nn.sigmoid(gate)
            silu = gate * sig
            dgate = (dact * up * (sig * (1.0 + gate * (1.0 - sig)))).astype(BF16)
            dup = (dact * silu).astype(BF16)
            act = (silu * up).astype(BF16)
            dwd = _dot_tn(act, dft)
            dwg = _dot_tn(dgate, at)
            dwu = _dot_tn(dup, at)
            dat = _dot(dgate, wg_ref[...]) + _dot(dup, wu_ref[...])

            @pl.when(i == 0)
            def _():
                acc_d[...] = dwd
                acc_g[...] = dwg
                acc_u[...] = dwu

            @pl.when(i != 0)
            def _():
                acc_d[...] += dwd
                acc_g[...] += dwg
                acc_u[...] += dwu

            @pl.when(j == 0)
            def _():
                da_ref[rows, :] = dat

            @pl.when(j != 0)
            def _():
                da_ref[rows, :] += dat

            return carry

        lax.fori_loop(0, N_ROW_TILES, tile, 0)
        dwg_ref[...] = acc_g[...].astype(BF16)
        dwu_ref[...] = acc_u[...].astype(BF16)
        dwd_ref[...] = acc_d[...].astype(BF16)

    wtile = pl.BlockSpec((FF_TILE, D_MODEL), lambda j: (j, 0))
    hid = pl.BlockSpec((SEQ, FF_TILE), lambda j: (0, j))
    full = pl.BlockSpec((SEQ, D_MODEL), lambda j: (0, 0))
    return pl.pallas_call(
        body, grid=(D_FF // FF_TILE,), in_specs=[full, full, hid, hid, wtile, wtile, wtile],
        out_specs=[full, wtile, wtile, wtile],
        out_shape=[SDS((SEQ, D_MODEL), F32)] + [SDS((D_FF, D_MODEL), BF16)] * 3,
        scratch_shapes=[pltpu.VMEM((FF_TILE, D_MODEL), F32)] * 3,
        name=name, compiler_params=_params(("arbitrary",)))(df, a, gate, up, wg, wu, wd)


HEADS_PER_BLOCK = IN_SHARD // HEAD_DIM
QKV_BLOCKS = 3 * NA_WIDTH // IN_SHARD


def _proj_heads(a, w):
    def body(a_ref, w_ref, o_ref):
        def tile(i, carry):
            rows = _rows(i)
            res = _dot(a_ref[rows, :], w_ref[...])
            for sub in range(HEADS_PER_BLOCK):
                o_ref[sub, rows, :] = res[:, sub * HEAD_DIM:(sub + 1) * HEAD_DIM]
            return carry

        lax.fori_loop(0, N_ROW_TILES, tile, 0)

    return pl.pallas_call(
        body, grid=(QKV_BLOCKS,),
        in_specs=[pl.BlockSpec((SEQ, D_MODEL), lambda j: (0, 0)), pl.BlockSpec((None, D_MODEL, IN_SHARD), lambda j: (j, 0, 0))],
        out_specs=pl.BlockSpec((HEADS_PER_BLOCK, SEQ, HEAD_DIM), lambda j: (j, 0, 0)),
        out_shape=SDS((3 * HEADS, SEQ, HEAD_DIM), F32), name="proj_heads",
        compiler_params=_params(("parallel",)))(a, w)


def _proj_u(a, w):
    def body(a_ref, w_ref, o_ref):
        def tile(i, carry):
            rows = _rows(i)
            o_ref[rows, :] = _dot(a_ref[rows, :], w_ref[...])
            return carry

        lax.fori_loop(0, N_ROW_TILES, tile, 0)

    return pl.pallas_call(
        body, grid=(N_DEV - QKV_BLOCKS,),
        in_specs=[pl.BlockSpec((SEQ, D_MODEL), lambda j: (0, 0)),
                  pl.BlockSpec((None, D_MODEL, IN_SHARD), lambda j: (j + QKV_BLOCKS, 0, 0))],
        out_specs=pl.BlockSpec((SEQ, IN_SHARD), lambda j: (0, j)),
        out_shape=SDS((SEQ, S5_WIDTH), F32), name="proj_u",
        compiler_params=_params(("parallel",)))(a, w)


def _proj_bwd(dq, dk, dv, du, a, w):
    def body(dq_ref, dk_ref, dv_ref, du_ref, a_ref, w_ref, da_ref, dw_ref, acc, dp_ref):
        j = pl.program_id(0)

        for which, src in enumerate((dq_ref, dk_ref, dv_ref)):
            @pl.when((j >= 2 * which) & (j < 2 * which + 2))
            def _(src=src):
                dp_ref[...] = jnp.concatenate([src[sub] for sub in range(HEADS_PER_BLOCK)], axis=-1).astype(BF16)

        @pl.when(j >= QKV_BLOCKS)
        def _():
            dp_ref[...] = du_ref[...].astype(BF16)

        def tile(i, carry):
            rows = _rows(i)
            dpt = dp_ref[rows, :]
            dw = _dot_tn(a_ref[rows, :], dpt)
            dat = _dot_nt(dpt, w_ref[...])

            @pl.when(i == 0)
            def _():
                acc[...] = dw

            @pl.when(i != 0)
            def _():
                acc[...] += dw

            @pl.when(j == 0)
            def _():
                da_ref[rows, :] = dat

            @pl.when(j != 0)
            def _():
                da_ref[rows, :] += dat

            return carry

        lax.fori_loop(0, N_ROW_TILES, tile, 0)
        dw_ref[...] = acc[...].astype(BF16)

    full = pl.BlockSpec((SEQ, D_MODEL), lambda j: (0, 0))
    wspec = pl.BlockSpec((None, D_MODEL, IN_SHARD), lambda j: (j, 0, 0))

    def heads(which):
        return pl.BlockSpec((HEADS_PER_BLOCK, SEQ, HEAD_DIM), lambda j: (jnp.clip(j - 2 * which, 0, 1), 0, 0))

    return pl.pallas_call(
        body, grid=(N_DEV,),
        in_specs=[heads(0), heads(1), heads(2),
                  pl.BlockSpec((SEQ, IN_SHARD), lambda j: (0, jnp.clip(j - QKV_BLOCKS, 0, 1))), full, wspec],
        out_specs=[full, wspec],
        out_shape=[SDS((SEQ, D_MODEL), F32), SDS((N_DEV, D_MODEL, IN_SHARD), BF16)],
        scratch_shapes=[pltpu.VMEM((D_MODEL, IN_SHARD), F32), pltpu.VMEM((SEQ, IN_SHARD), BF16)],
        name="proj_bwd", compiler_params=_params(("arbitrary",)))(dq, dk, dv, du, a, w)


def _na_consts():
    c = np.arange(GRID_W)
    col_start = np.clip(c - KW // 2, 0, GRID_W - KW)
    col_in = (c[None, :] >= col_start[:, None]) & (c[None, :] < col_start[:, None] + KW)
    dc = np.clip(c[None, :] - c[:, None] + KW - 1, 0, 2 * KW - 2)
    onehot = np.zeros((128, GRID_W * GRID_W), np.float32)
    qq, kk = np.meshgrid(c, c, indexing="ij")
    onehot[dc[col_in], (qq * GRID_W + kk)[col_in]] = 1.0
    negmask = np.where(col_in, 0.0, NEG_INF).astype(np.float32).reshape(1, -1)
    return onehot, negmask


def _na_pair(block_type, a, b):
    if block_type == 0:
        return b - a + KH - 1 if b < KH else None
    if block_type == 1:
        return b - a + KH // 2 - 1 if a <= b < a + KH else None
    return b - a if b >= NA_KR - KH else None


def _rpb_expand(rpb):
    onehot, negmask = _na_consts()
    rows = HEADS * (2 * KH - 1)
    rpb_pad = jnp.pad(rpb.reshape(rows, 2 * KW - 1), ((0, 128 - rows), (0, 128 - (2 * KW - 1))))

    def body(r_ref, oh_ref, m_ref, t_ref):
        hi, mid, lo = _split3(r_ref[...])
        oh = oh_ref[...]
        t_ref[...] = _dot(hi, oh) + _dot(mid, oh) + _dot(lo, oh) + m_ref[...]

    table = pl.pallas_call(body, out_shape=SDS((128, GRID_W * GRID_W), F32), name="rpb_expand",
                           compiler_params=_params())(rpb_pad, jnp.asarray(onehot, BF16), jnp.asarray(negmask))
    return table[:rows].reshape(HEADS, 2 * KH - 1, GRID_W, GRID_W)


def _rpb_reduce(dslabs):
    onehot, _ = _na_consts()
    rows = HEADS * (2 * KH - 1)

    def body(x_ref, oht_ref, o_ref):
        hi, mid, lo = _split3(x_ref[...])
        oht = oht_ref[...]
        o_ref[...] = _dot(hi, oht) + _dot(mid, oht) + _dot(lo, oht)

    out = pl.pallas_call(body, out_shape=SDS((rows, 128), F32), name="rpb_reduce", compiler_params=_params())(
        dslabs.reshape(rows, GRID_W * GRID_W), jnp.asarray(onehot.T, BF16))
    return out.reshape(HEADS, 2 * KH - 1, 128)


def _bias_tiles(slab_ref, tile_ref):
    tile_ref[...] = jnp.full(tile_ref.shape, NEG_INF, F32)
    for t in range(NA_TYPES):
        for a in range(NA_RB):
            for b in range(NA_KR):
                dr = _na_pair(t, a, b)
                if dr is not None:
                    tile_ref[t, a * GRID_W:(a + 1) * GRID_W, b * GRID_W:(b + 1) * GRID_W] = slab_ref[dr]


def _bias_tiles_bwd(dtile_ref, dslab_ref):
    acc = {}
    for t in range(NA_TYPES):
        for a in range(NA_RB):
            for b in range(NA_KR):
                dr = _na_pair(t, a, b)
                if dr is not None:
                    part = dtile_ref[t, a * GRID_W:(a + 1) * GRID_W, b * GRID_W:(b + 1) * GRID_W]
                    acc[dr] = part if dr not in acc else acc[dr] + part
    for dr in range(2 * KH - 1):
        dslab_ref[dr] = acc[dr]


def _block_geometry(g):
    start = jnp.clip(g * NA_RB - KH // 2, 0, GRID_ROWS - NA_KR)
    block_type = jnp.where(g == 0, 0, jnp.where(g == NA_BLOCKS - 1, 2, 1))
    q0 = pl.multiple_of(N_META + g * NA_QB, 16)
    k0 = pl.multiple_of(N_META + start * GRID_W, 16)
    return block_type, q0, k0


def _na_probs(q, kk, km, bias):
    s = _dot_nt(q, kk) * ATT_SCALE + bias
    sm = _dot_nt(q, km) * ATT_SCALE
    m = jnp.maximum(jnp.max(s, axis=-1, keepdims=True), jnp.max(sm, axis=-1, keepdims=True))
    p = jnp.exp(s - m)
    pm = jnp.exp(sm - m)
    inv = 1.0 / (jnp.sum(p, axis=-1, keepdims=True) + jnp.sum(pm, axis=-1, keepdims=True))
    return p * inv, pm * inv


def _meta_probs(qm, km):
    s = _dot_nt(qm, km) * ATT_SCALE
    p = jnp.exp(s - jnp.max(s, axis=-1, keepdims=True))
    return p / jnp.sum(p, axis=-1, keepdims=True)


def _qkv_specs():
    return [pl.BlockSpec((None, SEQ, HEAD_DIM), lambda h, which=which: (h + which * HEADS, 0, 0)) for which in range(3)]


def _na_fwd(qkv, bias):
    def body(q_ref, k_ref, v_ref, slab_ref, o_ref, b_ref):
        _bias_tiles(slab_ref, b_ref)
        km = k_ref[0:N_META, :].astype(BF16)
        vm = v_ref[0:N_META, :].astype(BF16)
        pmm = _meta_probs(q_ref[0:N_META, :].astype(BF16), km)
        o_ref[0:N_META, :] = _dot(pmm.astype(BF16), vm)

        def block(g, carry):
            block_type, q0, k0 = _block_geometry(g)
            qb = q_ref[pl.ds(q0, NA_QB), :].astype(BF16)
            kk = k_ref[pl.ds(k0, NA_KB), :].astype(BF16)
            vv = v_ref[pl.ds(k0, NA_KB), :].astype(BF16)
            p, pm = _na_probs(qb, kk, km, b_ref[block_type])
            o_ref[pl.ds(q0, NA_QB), :] = _dot(p.astype(BF16), vv) + _dot(pm.astype(BF16), vm)
            return carry

        lax.fori_loop(0, NA_BLOCKS, block, 0)

    head = pl.BlockSpec((None, SEQ, HEAD_DIM), lambda h: (h, 0, 0))
    return pl.pallas_call(
        body, grid=(HEADS,), in_specs=_qkv_specs() + [pl.BlockSpec((None, 2 * KH - 1, GRID_W, GRID_W), lambda h: (h, 0, 0, 0))],
        out_specs=head, out_shape=SDS((HEADS, SEQ, HEAD_DIM), F32), name="na_fwd",
        scratch_shapes=[pltpu.VMEM((NA_TYPES, NA_QB, NA_KB), F32)],
        compiler_params=_params(("parallel",)))(qkv, qkv, qkv, bias)


def _na_bwd(qkv, bias, do):
    def body(q_ref, k_ref, v_ref, slab_ref, do_ref, dq_ref, dk_ref, dv_ref, dslab_ref, b_ref, db_ref):
        _bias_tiles(slab_ref, b_ref)
        km = k_ref[0:N_META, :].astype(BF16)
        vm = v_ref[0:N_META, :].astype(BF16)
        dk_ref[...] = jnp.zeros_like(dk_ref)
        dv_ref[...] = jnp.zeros_like(dv_ref)
        db_ref[...] = jnp.zeros_like(db_ref)

        qm = q_ref[0:N_META, :].astype(BF16)
        dom = do_ref[0:N_META, :].astype(BF16)
        pmm = _meta_probs(qm, km)
        dpm = _dot_nt(dom, vm)
        dsm = (pmm * (dpm - jnp.sum(pmm * dpm, axis=-1, keepdims=True)) * ATT_SCALE).astype(BF16)
        dq_ref[0:N_META, :] = _dot(dsm, km)
        dkm0 = _dot_tn(dsm, qm)
        dvm0 = _dot_tn(pmm.astype(BF16), dom)

        def block(g, carry):
            dkm, dvm = carry
            block_type, q0, k0 = _block_geometry(g)
            qb = q_ref[pl.ds(q0, NA_QB), :].astype(BF16)
            kk = k_ref[pl.ds(k0, NA_KB), :].astype(BF16)
            vv = v_ref[pl.ds(k0, NA_KB), :].astype(BF16)
            dob = do_ref[pl.ds(q0, NA_QB), :].astype(BF16)
            p, pm = _na_probs(qb, kk, km, b_ref[block_type])
            dp = _dot_nt(dob, vv)
            dpm_ = _dot_nt(dob, vm)
            delta = jnp.sum(p * dp, axis=-1, keepdims=True) + jnp.sum(pm * dpm_, axis=-1, keepdims=True)
            ds = p * (dp - delta)
            dsm_ = pm * (dpm_ - delta)
            db_ref[block_type] += ds
            dsb = (ds * ATT_SCALE).astype(BF16)
            dsmb = (dsm_ * ATT_SCALE).astype(BF16)
            dq_ref[pl.ds(q0, NA_QB), :] = _dot(dsb, kk) + _dot(dsmb, km)
            dk_ref[pl.ds(k0, NA_KB), :] += _dot_tn(dsb, qb)
            dv_ref[pl.ds(k0, NA_KB), :] += _dot_tn(p.astype(BF16), dob)
            return dkm + _dot_tn(dsmb, qb), dvm + _dot_tn(pm.astype(BF16), dob)

        dkm, dvm = lax.fori_loop(0, NA_BLOCKS, block, (dkm0, dvm0))
        dk_ref[0:N_META, :] = dkm
        dv_ref[0:N_META, :] = dvm
        _bias_tiles_bwd(db_ref, dslab_ref)

    head = pl.BlockSpec((None, SEQ, HEAD_DIM), lambda h: (h, 0, 0))
    bspec = pl.BlockSpec((None, 2 * KH - 1, GRID_W, GRID_W), lambda h: (h, 0, 0, 0))
    return pl.pallas_call(
        body, grid=(HEADS,), in_specs=_qkv_specs() + [bspec, head], out_specs=[head, head, head, bspec],
        out_shape=[SDS((HEADS, SEQ, HEAD_DIM), F32)] * 3 + [SDS((HEADS, 2 * KH - 1, GRID_W, GRID_W), F32)],
        scratch_shapes=[pltpu.VMEM((NA_TYPES, NA_QB, NA_KB), F32), pltpu.VMEM((NA_TYPES, NA_QB, NA_KB), F32)],
        name="na_bwd", compiler_params=_params(("parallel",)))(qkv, qkv, qkv, bias, do)


def _cmul(ar, ai, br, bi):
    return ar * br - ai * bi, ar * bi + ai * br


def _cpow(ar, ai, n):
    rr, ri = None, None
    br, bi = ar, ai
    while n:
        if n & 1:
            rr, ri = (br, bi) if rr is None else _cmul(rr, ri, br, bi)
        n >>= 1
        if n:
            br, bi = _cmul(br, bi, br, bi)
    return rr, ri


def _s5_prep(lr, li, logdt, bre, bim):
    def body(lr_ref, li_ref, dt_ref, br_ref, bi_ref, lbr_ref, lbi_ref, bbr_ref, bbi_ref):
        lr_, li_ = lr_ref[...], li_ref[...]
        dt = jnp.exp(dt_ref[...])
        mag = jnp.exp(lr_ * dt)
        lbr = mag * jnp.cos(li_ * dt)
        lbi = mag * jnp.sin(li_ * dt)
        lbr_ref[...] = lbr
        lbi_ref[...] = lbi
        den = lr_ * lr_ + li_ * li_
        xr = lbr - 1.0
        cr = (xr * lr_ + lbi * li_) / den
        ci = (lbi * lr_ - xr * li_) / den
        br, bi = br_ref[...], bi_ref[...]
        bbr_ref[...] = cr[:, None, :] * br - ci[:, None, :] * bi
        bbi_ref[...] = cr[:, None, :] * bi + ci[:, None, :] * br

    n = 2 * S5_GROUPS
    return pl.pallas_call(
        body, out_shape=[SDS((n, S5_STATE), F32)] * 2 + [SDS((n, S5_GROUP, S5_STATE), F32)] * 2,
        name="s5_prep", compiler_params=_params())(lr, li, logdt, bre, bim)


def _s5_prep_bwd(lr, li, logdt, bre, bim, dar, dai, dbbr, dbbi):
    def body(lr_ref, li_ref, dt_ref, br_ref, bi_ref, dar_ref, dai_ref, dbr_ref, dbi_ref,
             glr_ref, gli_ref, gdt_ref, gbr_ref, gbi_ref):
        lr_, li_ = lr_ref[...], li_ref[...]
        dt = jnp.exp(dt_ref[...])
        mag = jnp.exp(lr_ * dt)
        lbr = mag * jnp.cos(li_ * dt)
        lbi = mag * jnp.sin(li_ * dt)
        den = lr_ * lr_ + li_ * li_
        xr = lbr - 1.0
        cr = (xr * lr_ + lbi * li_) / den
        ci = (lbi * lr_ - xr * li_) / den
        br, bi = br_ref[...], bi_ref[...]
        dbr, dbi = dbr_ref[...], dbi_ref[...]
        gbr_ref[...] = cr[:, None, :] * dbr + ci[:, None, :] * dbi
        gbi_ref[...] = cr[:, None, :] * dbi - ci[:, None, :] * dbr
        gcr = jnp.sum(dbr * br + dbi * bi, axis=1)
        gci = jnp.sum(dbi * br - dbr * bi, axis=1)
        ilr, ili = lr_ / den, li_ / den
        tr, ti = _cmul(gcr, gci, ilr, ili)
        glbr = dar_ref[...] + tr
        glbi = dai_ref[...] + ti
        dr_, di_ = _cmul(tr, ti, cr, -ci)
        gwr, gwi = _cmul(glbr, glbi, lbr, -lbi)
        glr_ref[...] = gwr * dt - dr_
        gli_ref[...] = gwi * dt - di_
        gdt_ref[...] = jnp.sum(gwr * lr_ + gwi * li_, axis=-1, keepdims=True) * dt

    n = 2 * S5_GROUPS
    return pl.pallas_call(
        body, out_shape=[SDS((n, S5_STATE), F32)] * 2 + [SDS((n, 1), F32)] + [SDS((n, S5_GROUP, S5_STATE), F32)] * 2,
        name="s5_prep_bwd", compiler_params=_params())(lr, li, logdt, bre, bim, dar, dai, dbbr, dbbi)


def _scan_local(xr_ref, xi_ref, ar8, ai8, reverse):
    def step(i, carry):
        sr, si = carry
        idx = (SCAN_T - 1 - i) if reverse else i
        rows = pl.ds(pl.multiple_of(idx * SCAN_BLOCKS, SCAN_BLOCKS), SCAN_BLOCKS)
        nr = ar8 * sr - ai8 * si + xr_ref[rows, :]
        ni = ar8 * si + ai8 * sr + xi_ref[rows, :]
        xr_ref[rows, :] = nr
        xi_ref[rows, :] = ni
        return nr, ni

    z = jnp.zeros(ar8.shape, F32)
    return lax.fori_loop(0, SCAN_T, step, (z, z))


def _scan_carries(er, ei, atr, ati, reverse):
    row = lax.broadcasted_iota(jnp.int32, er.shape, 0)
    cr = jnp.zeros((1, er.shape[1]), F32)
    ci = cr
    outr = jnp.zeros(er.shape, F32)
    outi = outr
    order = range(SCAN_BLOCKS - 1, -1, -1) if reverse else range(SCAN_BLOCKS)
    for b in order:
        outr = jnp.where(row == b, cr, outr)
        outi = jnp.where(row == b, ci, outi)
        nr, ni = _cmul(atr, ati, cr, ci)
        cr, ci = nr + er[b:b + 1, :], ni + ei[b:b + 1, :]
    return outr, outi


def _scan_fixup(xr_ref, xi_ref, cr8, ci8, ar8, ai8, reverse):
    def step(i, carry):
        pr, pi = carry
        idx = (SCAN_T - 1 - i) if reverse else i
        rows = pl.ds(pl.multiple_of(idx * SCAN_BLOCKS, SCAN_BLOCKS), SCAN_BLOCKS)
        fr, fi = _cmul(pr, pi, cr8, ci8)
        xr_ref[rows, :] += fr
        xi_ref[rows, :] += fi
        return _cmul(pr, pi, ar8, ai8)

    lax.fori_loop(0, SCAN_T, step, (ar8, ai8), unroll=2)


def _scan(xr_ref, xi_ref, ar, ai, reverse):
    n = ar.shape[1]
    ar8 = jnp.broadcast_to(ar, (SCAN_BLOCKS, n))
    ai8 = jnp.broadcast_to(ai, (SCAN_BLOCKS, n))
    er, ei = _scan_local(xr_ref, xi_ref, ar8, ai8, reverse)
    atr, ati = _cpow(ar, ai, SCAN_T)
    cr8, ci8 = _scan_carries(er, ei, atr, ati, reverse)
    _scan_fixup(xr_ref, xi_ref, cr8, ci8, ar8, ai8, reverse)


def _s5_specs():
    chan = pl.BlockSpec((SEQ, CH_W), lambda c, d: (0, c))
    chan2 = pl.BlockSpec((None, SEQ, CH_W), lambda c, d: (d, 0, c))
    state = pl.BlockSpec((None, SEQ, ST_W), lambda c, d: (d, 0, c))
    bmat = pl.BlockSpec((None, None, CH_W, ST_W), lambda c, d: (d, c, 0, 0))
    cmat = pl.BlockSpec((None, None, ST_W, CH_W), lambda c, d: (d, c, 0, 0))
    avec = pl.BlockSpec((None, None, 1, ST_W), lambda c, d: (d, c, 0, 0))
    return chan, chan2, state, bmat, cmat, avec


def _scan_by_direction(xr_ref, xi_ref, ar, ai, d, adjoint):
    @pl.when(d == 0)
    def _():
        _scan(xr_ref, xi_ref, ar, ai, reverse=adjoint)

    @pl.when(d == 1)
    def _():
        _scan(xr_ref, xi_ref, ar, ai, reverse=not adjoint)


def _s5_scan_fwd(u, bre, bim, are, aim, cre, cim):
    def body(u_ref, bre_ref, bim_ref, are_ref, aim_ref, cre_ref, cim_ref, sr_ref, si_ref, y_ref):
        ub = u_ref[...].astype(BF16)
        sr_ref[...] = _dot(ub, bre_ref[...])
        si_ref[...] = _dot(ub, bim_ref[...])
        _scan_by_direction(sr_ref, si_ref, are_ref[...], aim_ref[...], pl.program_id(1), adjoint=False)
        y_ref[...] = _dot(sr_ref[...].astype(BF16), cre_ref[...]) - _dot(si_ref[...].astype(BF16), cim_ref[...])

    chan, chan2, state, bmat, cmat, avec = _s5_specs()
    return pl.pallas_call(
        body, grid=(S5_CHUNKS, 2), in_specs=[chan, bmat, bmat, avec, avec, cmat, cmat], out_specs=[state, state, chan2],
        out_shape=[SDS((2, SEQ, S5_GROUPS * S5_STATE), F32)] * 2 + [SDS((2, SEQ, S5_WIDTH), F32)],
        name="s5_scan_fwd", compiler_params=_params(("parallel", "parallel")))(u, bre, bim, are, aim, cre, cim)


def _dlam(gr_ref, gi_ref, sr_ref, si_ref, reverse):
    tile = lambda i: pl.ds(pl.multiple_of(i * SCAN_BLOCKS, SCAN_BLOCKS), SCAN_BLOCKS)
    row = lax.broadcasted_iota(jnp.int32, (SCAN_BLOCKS, ST_W), 0)
    if reverse:
        edge, src, shift, empty, lo, hi, dprev = SCAN_T - 1, 0, SCAN_BLOCKS - 1, SCAN_BLOCKS - 1, 0, SCAN_T - 1, 1
    else:
        edge, src, shift, empty, lo, hi, dprev = 0, SCAN_T - 1, 1, 0, 1, SCAN_T, -1
    spr = jnp.where(row == empty, 0.0, pltpu.roll(sr_ref[tile(src), :], shift, 0))
    spi = jnp.where(row == empty, 0.0, pltpu.roll(si_ref[tile(src), :], shift, 0))
    acc0 = _cmul(gr_ref[tile(edge), :], gi_ref[tile(edge), :], spr, -spi)

    def step(i, carry):
        accr, acci = carry
        pr, pi = _cmul(gr_ref[tile(i), :], gi_ref[tile(i), :], sr_ref[tile(i + dprev), :], -si_ref[tile(i + dprev), :])
        return accr + pr, acci + pi

    accr, acci = lax.fori_loop(lo, hi, step, acc0)
    return jnp.sum(accr, axis=0, keepdims=True), jnp.sum(acci, axis=0, keepdims=True)


def _s5_scan_bwd(dy, du_skip, u, sr, si, bre, bim, are, aim, cre, cim):
    def body(dy_ref, dus_ref, u_ref, sr_ref, si_ref, bre_ref, bim_ref, are_ref, aim_ref, cre_ref, cim_ref,
             du_ref, dbr_ref, dbi_ref, dcr_ref, dci_ref, dar_ref, dai_ref, gr_ref, gi_ref):
        d = pl.program_id(1)
        dyb = dy_ref[...].astype(BF16)
        gr_ref[...] = _dot_nt(dyb, cre_ref[...])
        gi_ref[...] = -_dot_nt(dyb, cim_ref[...])
        dcr_ref[...] = _dot_tn(sr_ref[...].astype(BF16), dyb)
        dci_ref[...] = -_dot_tn(si_ref[...].astype(BF16), dyb)
        _scan_by_direction(gr_ref, gi_ref, are_ref[...], -aim_ref[...], d, adjoint=True)

        @pl.when(d == 0)
        def _():
            dar_ref[...], dai_ref[...] = _dlam(gr_ref, gi_ref, sr_ref, si_ref, reverse=False)
            du_ref[...] = dus_ref[...]

        @pl.when(d == 1)
        def _():
            dar_ref[...], dai_ref[...] = _dlam(gr_ref, gi_ref, sr_ref, si_ref, reverse=True)

        grb = gr_ref[...].astype(BF16)
        gib = gi_ref[...].astype(BF16)
        du_ref[...] += _dot_nt(grb, bre_ref[...]) + _dot_nt(gib, bim_ref[...])
        ub = u_ref[...].astype(BF16)
        dbr_ref[...] = _dot_tn(ub, grb)
        dbi_ref[...] = _dot_tn(ub, gib)

    chan, _, state, bmat, cmat, avec = _s5_specs()
    return pl.pallas_call(
        body, grid=(S5_CHUNKS, 2), in_specs=[chan, chan, chan, state, state, bmat, bmat, avec, avec, cmat, cmat],
        out_specs=[chan, bmat, bmat, cmat, cmat, avec, avec],
        out_shape=[SDS((SEQ, S5_WIDTH), F32)] + [SDS((2, S5_CHUNKS, CH_W, ST_W), F32)] * 2
                  + [SDS((2, S5_CHUNKS, ST_W, CH_W), F32)] * 2 + [SDS((2, S5_CHUNKS, 1, ST_W), F32)] * 2,
        scratch_shapes=[pltpu.VMEM((SEQ, ST_W), F32), pltpu.VMEM((SEQ, ST_W), F32)],
        name="s5_scan_bwd", compiler_params=_params(("parallel", "arbitrary")))(dy, du_skip, u, sr, si, bre, bim, are, aim, cre, cim)


_GELU_K = math.sqrt(2.0 / math.pi)
_GELU_C = 0.044715


def _gelu(x):
    t = jnp.tanh(_GELU_K * (x + _GELU_C * x * x * x))
    return 0.5 * x * (1.0 + t), t


def _s5_glu_fwd(u, y2, dskip, wglu, bglu):
    def body(u_ref, y0_ref, y1_ref, d_ref, w_ref, b_ref, o_ref, yp_ref):
        ypre = u_ref[...] * d_ref[...] + y0_ref[...] + y1_ref[...]
        yp_ref[...] = ypre
        y, _ = _gelu(ypre)
        z = _dot(y.astype(BF16), w_ref[...]) + b_ref[...]
        o_ref[...] = y * jax.nn.sigmoid(z)

    row = _row_spec(S5_WIDTH)
    vec = _fix_spec((1, S5_WIDTH))
    dir0 = pl.BlockSpec((None, ROW_TILE, S5_WIDTH), lambda i: (0, i, 0))
    dir1 = pl.BlockSpec((None, ROW_TILE, S5_WIDTH), lambda i: (1, i, 0))
    return pl.pallas_call(
        body, grid=(N_ROW_TILES,), in_specs=[row, dir0, dir1, vec, _fix_spec((S5_WIDTH, S5_WIDTH)), vec],
        out_specs=[row, row], out_shape=[SDS((SEQ, S5_WIDTH), F32)] * 2, name="s5_glu_fwd",
        compiler_params=_params(("parallel",)))(u, y2, y2, dskip, wglu, bglu)


def _s5_glu_bwd(do, ypre, u, dskip, wglu, bglu):
    def body(do_ref, yp_ref, u_ref, d_ref, w_ref, b_ref, dyp_ref, du_ref, dw_ref, db_ref, dd_ref):
        i = pl.program_id(0)
        ypre = yp_ref[...]
        y, t = _gelu(ypre)
        yb = y.astype(BF16)
        sg = jax.nn.sigmoid(_dot(yb, w_ref[...]) + b_ref[...])
        dov = do_ref[...]
        dz = dov * y * sg * (1.0 - sg)
        dzb = dz.astype(BF16)
        dy = dov * sg + _dot_nt(dzb, w_ref[...])
        dgelu = 0.5 * (1.0 + t) + 0.5 * ypre * (1.0 - t * t) * _GELU_K * (1.0 + 3.0 * _GELU_C * ypre * ypre)
        dyp = dy * dgelu
        dyp_ref[...] = dyp
        uv = u_ref[...]
        du_ref[...] = dyp * d_ref[...]

        @pl.when(i == 0)
        def _():
            dw_ref[...] = jnp.zeros_like(dw_ref)
            db_ref[...] = jnp.zeros_like(db_ref)
            dd_ref[...] = jnp.zeros_like(dd_ref)

        dw_ref[...] += _dot_tn(yb, dzb)
        db_ref[...] += jnp.sum(dz, axis=0, keepdims=True)
        dd_ref[...] += jnp.sum(dyp * uv, axis=0, keepdims=True)

    row = _row_spec(S5_WIDTH)
    vec = _fix_spec((1, S5_WIDTH))
    mat = _fix_spec((S5_WIDTH, S5_WIDTH))
    return pl.pallas_call(
        body, grid=(N_ROW_TILES,), in_specs=[row, row, row, vec, mat, vec], out_specs=[row, row, mat, vec, vec],
        out_shape=[SDS((SEQ, S5_WIDTH), F32)] * 2 + [SDS((S5_WIDTH, S5_WIDTH), F32), SDS((1, S5_WIDTH), F32), SDS((1, S5_WIDTH), F32)],
        name="s5_glu_bwd", compiler_params=_params(("arbitrary",)))(do, ypre, u, dskip, wglu, bglu)


def _heads_side_by_side(o_ref):
    return jnp.concatenate([o_ref[h] for h in range(HEADS)], axis=-1)


def _mix_out_fwd(ona, os5, g_na, g_s5, wout):
    def body(a_ref, s_ref, ga_ref, gs_ref, w_ref, o_ref):
        av, sv = _heads_side_by_side(a_ref), s_ref[...]
        ca = (av * _rstd(av) * ga_ref[...]).astype(BF16)
        cs = (sv * _rstd(sv) * gs_ref[...]).astype(BF16)
        o_ref[...] = _dot(ca, w_ref[0:NA_WIDTH, :]) + _dot(cs, w_ref[NA_WIDTH:, :])

    row = _row_spec(NA_WIDTH)
    vec = _fix_spec((1, NA_WIDTH))
    heads = pl.BlockSpec((HEADS, ROW_TILE, HEAD_DIM), lambda i: (0, i, 0))
    return pl.pallas_call(
        body, grid=(N_ROW_TILES,), in_specs=[heads, row, vec, vec, _fix_spec((D_MODEL, D_MODEL))],
        out_specs=_row_spec(D_MODEL), out_shape=SDS((SEQ, D_MODEL), F32), name="mix_out_fwd",
        compiler_params=_params(("parallel",)))(ona, os5, g_na, g_s5, wout)


def _mix_out_bwd(dmix, ona, os5, g_na, g_s5, wout):
    def body(dm_ref, a_ref, s_ref, ga_ref, gs_ref, w_ref, da_ref, ds_ref, dw_ref, dga_ref, dgs_ref):
        i = pl.program_id(0)
        dm = dm_ref[...]
        av, sv = _heads_side_by_side(a_ref), s_ref[...]
        ra, rs = _rstd(av), _rstd(sv)
        ga, gs = ga_ref[...], gs_ref[...]
        ca = (av * ra * ga).astype(BF16)
        cs = (sv * rs * gs).astype(BF16)
        dca = _dot_nt(dm, w_ref[0:NA_WIDTH, :])
        dcs = _dot_nt(dm, w_ref[NA_WIDTH:, :])
        da, dga = _rms_bwd(av, ra, ga, dca)
        ds, dgs = _rms_bwd(sv, rs, gs, dcs)
        for h in range(HEADS):
            da_ref[h] = da[:, h * HEAD_DIM:(h + 1) * HEAD_DIM]
        ds_ref[...] = ds

        @pl.when(i == 0)
        def _():
            dw_ref[...] = jnp.zeros_like(dw_ref)
            dga_ref[...] = jnp.zeros_like(dga_ref)
            dgs_ref[...] = jnp.zeros_like(dgs_ref)

        dw_ref[0:NA_WIDTH, :] += _dot_tn(ca, dm)
        dw_ref[NA_WIDTH:, :] += _dot_tn(cs, dm)
        dga_ref[...] += jnp.sum(dga, axis=0, keepdims=True)
        dgs_ref[...] += jnp.sum(dgs, axis=0, keepdims=True)

    row = _row_spec(NA_WIDTH)
    vec = _fix_spec((1, NA_WIDTH))
    mat = _fix_spec((D_MODEL, D_MODEL))
    heads = pl.BlockSpec((HEADS, ROW_TILE, HEAD_DIM), lambda i: (0, i, 0))
    return pl.pallas_call(
        body, grid=(N_ROW_TILES,), in_specs=[_row_spec(D_MODEL), heads, row, vec, vec, mat],
        out_specs=[heads, row, mat, vec, vec],
        out_shape=[SDS((HEADS, SEQ, HEAD_DIM), F32), SDS((SEQ, NA_WIDTH), F32), SDS((D_MODEL, D_MODEL), F32),
                   SDS((1, NA_WIDTH), F32), SDS((1, NA_WIDTH), F32)],
        name="mix_out_bwd", compiler_params=_params(("arbitrary",)))(dmix, ona, os5, g_na, g_s5, wout)


def _me():
    x, y, c = lax.axis_index("x"), lax.axis_index("y"), lax.axis_index("c")
    return x, y, c, 4 * x + 2 * y + c


def _peer(k):
    x, y, c, _ = _me()
    px = 1 - x if (k >> 2) & 1 else x
    py = 1 - y if (k >> 1) & 1 else y
    pc = 1 - c if k & 1 else c
    return (px, py, pc), 4 * px + 2 * py + pc


ALL_PEERS = (1, 2, 3, 4, 5, 6, 7)
CHIP_PEERS = (2, 4, 6)
SIBLING = 1


def _slot8(pos):
    return 4 * pos[0] + 2 * pos[1] + pos[2]


def _slot4(pos):
    return 2 * pos[0] + pos[1]


def _exchange(arrays, gather, name, after=()):
    n, n_after = len(arrays), len(after)

    def body(*refs):
        ins, outs = refs[:n], refs[n + n_after:2 * n + n_after]
        token = refs[2 * n + n_after]
        send_sems, recv_sems, local_sems = refs[2 * n + n_after + 1:]
        token[...] = jnp.zeros_like(token)
        _, _, _, me = _me()
        started = []
        for a in range(n):
            src_mine = ins[a] if gather else ins[a].at[me]
            local = pltpu.make_async_copy(src_mine, outs[a].at[me], local_sems.at[a])
            local.start()
            started.append(local)
        sends = []
        for k in range(1, N_DEV):
            peer, peer_idx = _peer(k)
            for a in range(n):
                src = ins[a] if gather else ins[a].at[peer_idx]
                cp = pltpu.make_async_remote_copy(src_ref=src, dst_ref=outs[a].at[me], send_sem=send_sems.at[a, k - 1],
                                                  recv_sem=recv_sems.at[a, k - 1], device_id=peer, device_id_type=MESH)
                cp.start()
                sends.append(cp)
        for k in range(1, N_DEV):
            peer, peer_idx = _peer(k)
            for a in range(n):
                src = ins[a] if gather else ins[a].at[peer_idx]
                pltpu.make_async_remote_copy(src_ref=src, dst_ref=outs[a].at[peer_idx], send_sem=send_sems.at[a, k - 1],
                                             recv_sem=recv_sems.at[a, k - 1], device_id=peer, device_id_type=MESH).wait_recv()
        for cp in sends:
            cp.wait_send()
        for local in started:
            local.wait()

    hbm = pl.BlockSpec(memory_space=pltpu.HBM)
    out_shape = [SDS((N_DEV,) + tuple(a.shape), a.dtype) if gather else SDS(a.shape, a.dtype) for a in arrays]
    out = pl.pallas_call(
        body, in_specs=[hbm] * n + [pl.BlockSpec(memory_space=pl.ANY)] * n_after,
        out_specs=[hbm] * n + [pl.BlockSpec(memory_space=pltpu.VMEM)], out_shape=out_shape + [SDS((8, 128), F32)],
        scratch_shapes=[pltpu.SemaphoreType.DMA((n, N_DEV - 1)), pltpu.SemaphoreType.DMA((n, N_DEV - 1)),
                        pltpu.SemaphoreType.DMA((n,))],
        name=name)(*arrays, *after)
    return list(out[:n]), out[n]


_HBM = pl.BlockSpec(memory_space=pltpu.HBM)
_SEM = pl.BlockSpec(memory_space=pltpu.SEMAPHORE)
_EFFECT = pltpu.SideEffectType.DATAFLOW_SIDE_EFFECTING


def _land_shape(a, gather):
    return (N_DEV,) + tuple(a.shape) if gather else tuple(a.shape)


def _place_own(arrays, gather, name, slot=_slot8):
    n = len(arrays)
    me = slot(_me()[:3])

    def body(me_ref, *refs):
        for a in range(n):
            refs[n + a][...] = refs[a][...]

    def own_slot(a):
        zeros = (0,) * (a.ndim - (0 if gather else 1))
        return lambda i, me_ref: (me_ref[0],) + zeros

    def whole(a):
        return lambda i, me_ref: (0,) * a.ndim

    in_specs = [pl.BlockSpec(a.shape, whole(a)) if gather else pl.BlockSpec((None,) + a.shape[1:], own_slot(a)) for a in arrays]
    out_specs = [pl.BlockSpec((None,) + (a.shape if gather else a.shape[1:]), own_slot(a)) for a in arrays]
    return pl.pallas_call(
        body, grid_spec=pltpu.PrefetchScalarGridSpec(num_scalar_prefetch=1, grid=(1,), in_specs=in_specs, out_specs=out_specs),
        out_shape=[SDS(_land_shape(a, gather), a.dtype) for a in arrays], name=name,
        compiler_params=_params(("arbitrary",)))(me.reshape(1).astype(jnp.int32), *arrays)


def _exchange_start(arrays, lands, gather, name, peers=ALL_PEERS, slot=_slot8):
    n = len(arrays)

    def body(*refs):
        ins, lnd = refs[:n], refs[n:2 * n]
        send_sems, recv_sems = refs[2 * n], refs[2 * n + 1]
        token = refs[-1]
        me = slot(_me()[:3])
        for i, k in enumerate(peers):
            peer, _ = _peer(k)
            for a in range(n):
                src = ins[a] if gather else ins[a].at[slot(peer)]
                s = a * len(peers) + i
                pltpu.make_async_remote_copy(src_ref=src, dst_ref=lnd[a].at[me], send_sem=send_sems.at[s],
                                             recv_sem=recv_sems.at[s], device_id=peer, device_id_type=MESH).start()
        token[...] = jnp.zeros_like(token)

    sems = pltpu.SemaphoreType.DMA((n * len(peers),))
    out = pl.pallas_call(
        body, name=name, in_specs=[_HBM] * (2 * n),
        out_shape=(sems, sems) + tuple(pltpu.HBM(a.shape, a.dtype) for a in list(arrays) + list(lands)) + (SDS((8, 128), F32),),
        out_specs=(_SEM, _SEM) + (_HBM,) * (2 * n) + (pl.BlockSpec(memory_space=pltpu.VMEM),),
        input_output_aliases={i: 2 + i for i in range(2 * n)},
        compiler_params=pltpu.CompilerParams(has_side_effects=_EFFECT),
    )(*[pltpu.with_memory_space_constraint(a, pltpu.HBM) for a in list(arrays) + list(lands)])
    return out[0], out[1], list(out[2:2 + n]), list(out[2 + n:2 + 2 * n]), out[-1]


def _exchange_wait(send_sems, recv_sems, arrays, lands, after, gather, name, peers=ALL_PEERS, slot=_slot8):
    n = len(arrays)

    def body(*refs):
        ins, lnd = refs[:n], refs[n:2 * n]
        send_sems, recv_sems = refs[2 * n], refs[2 * n + 1]
        for i, k in enumerate(peers):
            peer, _ = _peer(k)
            for a in range(n):
                src = ins[a] if gather else ins[a].at[slot(peer)]
                s = a * len(peers) + i
                cp = pltpu.make_async_remote_copy(src_ref=src, dst_ref=lnd[a].at[slot(peer)], send_sem=send_sems.at[s],
                                                  recv_sem=recv_sems.at[s], device_id=peer, device_id_type=MESH)
                cp.wait_send()
                cp.wait_recv()

        refs[-1][...] = jnp.zeros_like(refs[-1])

    after = list(after) if isinstance(after, (list, tuple)) else [after]
    out = pl.pallas_call(
        body, name=name, in_specs=[_HBM] * (2 * n) + [_SEM, _SEM] + [pl.BlockSpec(memory_space=pl.ANY)] * len(after),
        out_shape=tuple(pltpu.HBM(a.shape, a.dtype) for a in list(arrays) + list(lands)) + (SDS((8, 128), F32),),
        out_specs=(_HBM,) * (2 * n) + (pl.BlockSpec(memory_space=pltpu.VMEM),), input_output_aliases={i: i for i in range(2 * n)},
        compiler_params=pltpu.CompilerParams(has_side_effects=_EFFECT),
    )(*arrays, *lands, send_sems, recv_sems, *after)
    return list(out[n:2 * n]), out[-1]


def _forward_sibling(lands, name):
    n = len(lands)

    def body(*refs):
        outs = refs[n:2 * n]
        send_sems, recv_sems = refs[2 * n:]
        x, y, c, _ = _me()
        sends = []
        for i, k in enumerate(CHIP_PEERS):
            peer, _ = _peer(k)
            for a in range(n):
                rows = outs[a].at[_slot8(peer)]
                cp = pltpu.make_async_remote_copy(src_ref=rows, dst_ref=rows, send_sem=send_sems.at[a, i], recv_sem=recv_sems.at[a, i],
                                                  device_id=(x, y, 1 - c), device_id_type=MESH)
                cp.start()
                sends.append(cp)
        for i, k in enumerate(CHIP_PEERS):
            (px, py, pc), _ = _peer(k)
            for a in range(n):
                rows = outs[a].at[_slot8((px, py, 1 - pc))]
                pltpu.make_async_remote_copy(src_ref=rows, dst_ref=rows, send_sem=send_sems.at[a, i], recv_sem=recv_sems.at[a, i],
                                             device_id=(x, y, 1 - c), device_id_type=MESH).wait_recv()
        for cp in sends:
            cp.wait_send()

    return pl.pallas_call(
        body, in_specs=[_HBM] * n, out_specs=[_HBM] * n, out_shape=[SDS(a.shape, a.dtype) for a in lands],
        input_output_aliases={i: i for i in range(n)},
        scratch_shapes=[pltpu.SemaphoreType.DMA((n, len(CHIP_PEERS))), pltpu.SemaphoreType.DMA((n, len(CHIP_PEERS)))],
        name=name)(*lands)


def _swap_sibling(arrays, name):
    n = len(arrays)
    chips = N_DEV // 2

    def body(*refs):
        ins, outs = refs[:n], refs[n:2 * n]
        send_sems, recv_sems = refs[2 * n:]
        x, y, c, _ = _me()
        sends = []
        for q in range(chips):
            for a in range(n):
                cp = pltpu.make_async_remote_copy(src_ref=ins[a].at[q, 1 - c], dst_ref=outs[a].at[q], send_sem=send_sems.at[a, q],
                                                  recv_sem=recv_sems.at[a, q], device_id=(x, y, 1 - c), device_id_type=MESH)
                cp.start()
                sends.append(cp)
        for cp in sends:
            cp.wait_recv()
        for cp in sends:
            cp.wait_send()

    return pl.pallas_call(
        body, in_specs=[_HBM] * n, out_specs=[_HBM] * n, out_shape=[SDS((chips,) + a.shape[2:], a.dtype) for a in arrays],
        scratch_shapes=[pltpu.SemaphoreType.DMA((n, chips)), pltpu.SemaphoreType.DMA((n, chips))], name=name)(*arrays)


def _sum_pairs(mine, theirs, name):
    chips, _, rows, cols = mine.shape
    c = lax.axis_index("c")

    def body(c_ref, a_ref, b_ref, o_ref):
        o_ref[...] = (a_ref[...].astype(F32) + b_ref[...].astype(F32)).astype(o_ref.dtype)

    return pl.pallas_call(
        body, grid_spec=pltpu.PrefetchScalarGridSpec(
            num_scalar_prefetch=1, grid=(chips,),
            in_specs=[pl.BlockSpec((None, None, rows, cols), lambda q, c_ref: (q, c_ref[0], 0, 0)),
                      pl.BlockSpec((None, rows, cols), lambda q, c_ref: (q, 0, 0))],
            out_specs=pl.BlockSpec((None, rows, cols), lambda q, c_ref: (q, 0, 0))),
        out_shape=SDS((chips, rows, cols), mine.dtype), name=name,
        compiler_params=_params(("parallel",)))(c.reshape(1).astype(jnp.int32), mine, theirs)


def _adamw_math(w, g, m, v):
    m = ADAM_B1 * m + (1.0 - ADAM_B1) * g
    v = ADAM_B2 * v + (1.0 - ADAM_B2) * (g * g)
    m_hat = m / (1.0 - ADAM_B1 ** ADAM_STEP)
    v_hat = v / (1.0 - ADAM_B2 ** ADAM_STEP)
    delta = -ADAM_LR * (m_hat / (jnp.sqrt(v_hat) + ADAM_EPS) + ADAM_WD * w)
    return delta, m, v


def _adamw(w, m, v, pieces, name):
    rows, cols = w.shape[-2:]
    lead = w.ndim - 2
    tile = rows
    for cand in (256, 176, 128, 64, 16):
        if rows > cand and rows % cand == 0:
            tile = cand
            break

    def body(w_ref, m_ref, v_ref, p_ref, g_ref, d_ref, mo_ref, vo_ref):
        g = _sum_pieces(p_ref)
        g_ref[...] = g
        d_ref[...], mo_ref[...], vo_ref[...] = _adamw_math(w_ref[...], g, m_ref[...], v_ref[...])

    blk = pl.BlockSpec((None,) * lead + (tile, cols), lambda i: (0,) * lead + (i, 0))
    return pl.pallas_call(
        body, grid=(rows // tile,), in_specs=[blk, blk, blk, pl.BlockSpec((pieces.shape[0], tile, cols), lambda i: (0, i, 0))],
        out_specs=[blk] * 4, out_shape=[SDS(w.shape, F32)] * 4, name=name,
        compiler_params=_params(("parallel",)))(w, m, v, pieces)


def _sum_pieces(p_ref):
    g = p_ref[0].astype(F32)
    for p in range(1, p_ref.shape[0]):
        g = g + p_ref[p].astype(F32)
    return g


def _adamw_s5_mat(w, m, v, g, name):
    _, ndir, groups, b, c = w.shape
    per_dir = groups // 8

    def body(w_ref, m_ref, v_ref, g_ref, d_ref, mo_ref, vo_ref):
        d_ref[...], mo_ref[...], vo_ref[...] = _adamw_math(w_ref[...], g_ref[...], m_ref[...], v_ref[...])

    blk = pl.BlockSpec((None, None, 8, b, c), lambda i: (0, i // per_dir, i % per_dir, 0, 0))
    return pl.pallas_call(
        body, grid=(ndir * per_dir,), in_specs=[blk] * 4, out_specs=[blk] * 3, out_shape=[SDS(w.shape, F32)] * 3, name=name,
        compiler_params=_params(("parallel",)))(w, m, v, g)


VEC_ROWS = ['ffn1_pre_g', 'ffn1_post_g', 'mix_pre_g', 'mix_post_g', 'ffn2_pre_g', 'ffn2_post_g', 'final_g',
            ('na_out_g', 's5_out_g'), ('s5_d', 's5_b_glu')]
VEC_NAMES = [n for row in VEC_ROWS for n in ((row,) if isinstance(row, str) else row)]
VEC_PACK_ROWS = 16
LOSS_ROW = len(VEC_ROWS)


def _pack_vectors(grads, loss8):
    def body(*refs):
        o_ref = refs[-1]
        o_ref[...] = jnp.zeros_like(o_ref)
        o_ref[LOSS_ROW:LOSS_ROW + 1, 0:128] = refs[-2][0:1, :]
        k = 0
        for i, row in enumerate(VEC_ROWS):
            if isinstance(row, str):
                o_ref[i:i + 1, :] = refs[k][...]
                k += 1
            else:
                o_ref[i:i + 1, 0:NA_WIDTH] = refs[k][...]
                o_ref[i:i + 1, NA_WIDTH:] = refs[k + 1][...]
                k += 2

    return pl.pallas_call(body, out_shape=SDS((VEC_PACK_ROWS, D_MODEL), F32), name="pack_vectors",
                          compiler_params=_params())(*[grads[n] for n in VEC_NAMES], loss8)


def _sum8(pieces, name):
    def body(p_ref, o_ref):
        o_ref[...] = _sum_pieces(p_ref)

    return pl.pallas_call(body, out_shape=SDS(pieces.shape[1:], F32), name=name, compiler_params=_params())(pieces)


def _adamw_small(packed8, vec_wmv, others):
    n_vec, n_oth = len(VEC_NAMES), len(others)

    def body(*refs):
        p_ref = refs[0]
        ins = refs[1:1 + 3 * n_vec + 4 * n_oth]
        outs = refs[1 + 3 * n_vec + 4 * n_oth:]
        gsum = _sum_pieces(p_ref)
        outs[-1][...] = gsum[LOSS_ROW:LOSS_ROW + 1, 0:128]
        k = 0
        for i, row in enumerate(VEC_ROWS):
            parts = [(row, gsum[i:i + 1, :])] if isinstance(row, str) else \
                [(row[0], gsum[i:i + 1, 0:NA_WIDTH]), (row[1], gsum[i:i + 1, NA_WIDTH:])]
            for _, g in parts:
                w_ref, m_ref, v_ref = ins[3 * k:3 * k + 3]
                outs[4 * k][...] = g
                outs[4 * k + 1][...], outs[4 * k + 2][...], outs[4 * k + 3][...] = _adamw_math(w_ref[...], g, m_ref[...], v_ref[...])
                k += 1
        for j in range(n_oth):
            w_ref, m_ref, v_ref, g_ref = ins[3 * n_vec + 4 * j:3 * n_vec + 4 * j + 4]
            g = _sum_pieces(g_ref)
            g = g[tuple(slice(0, s) for s in w_ref.shape[1:])].reshape(w_ref.shape)
            o = outs[4 * (n_vec + j):4 * (n_vec + j) + 4]
            o[0][...] = g
            o[1][...], o[2][...], o[3][...] = _adamw_math(w_ref[...], g, m_ref[...], v_ref[...])

    args, out_shape = [packed8], []
    for w, m, v in vec_wmv:
        args += [w, m, v]
        out_shape += [SDS(w.shape, F32)] * 4
    for w, m, v, g in others:
        args += [w, m, v, g]
        out_shape += [SDS(w.shape, F32)] * 4
    out_shape += [SDS((1, 128), F32)]
    return pl.pallas_call(body, out_shape=out_shape, name="adamw_small", compiler_params=_params())(*args)


def _perm_rows(x):
    return x.reshape(SCAN_BLOCKS, SCAN_T, x.shape[-1]).transpose(1, 0, 2).reshape(SEQ, x.shape[-1])


def _unperm_rows(x):
    return x.reshape(SCAN_T, SCAN_BLOCKS, x.shape[-1]).transpose(1, 0, 2).reshape(SEQ, x.shape[-1])


def _block_diag(x):
    eye = np.eye(8, dtype=bool)[None, None, :, None, :, None]
    full = jnp.where(eye, x[:, :, :, :, None, :], 0.0)
    return full.reshape(2, S5_CHUNKS, 8 * x.shape[3], 8 * x.shape[4])


def _diag_blocks(x, r, c):
    x6 = x.reshape(2, S5_CHUNKS, 8, r, 8, c)
    return jnp.stack([x6[:, :, g, :, g, :] for g in range(8)], axis=2)


STORED_SWAPPED = {"ffn1_w_gate": (1, 2), "ffn1_w_up": (1, 2), "ffn2_w_gate": (1, 2), "ffn2_w_up": (1, 2),
                  "s5_b_re": (3, 4), "s5_b_im": (3, 4)}


def _stored(name, x):
    return jnp.swapaxes(x, *STORED_SWAPPED[name]) if name in STORED_SWAPPED else x


def _dep(x, token):
    return x if token is None else x + token


def _local_step(x, target, get_w, small, emit):
    bias = _rpb_expand(small["na_rpb"][0])
    lr = small["s5_lam_re"].reshape(64, S5_STATE)
    li = small["s5_lam_im"].reshape(64, S5_STATE)
    logdt = small["s5_log_dt"].reshape(64, 1)
    b_t = [_stored(n, small[n]).reshape(64, S5_GROUP, S5_STATE) for n in ("s5_b_re", "s5_b_im")]
    lbr, lbi, bbr, bbi = _s5_prep(lr, li, logdt, b_t[0], b_t[1])
    are = lbr.reshape(2, S5_CHUNKS, 1, ST_W)
    aim = lbi.reshape(2, S5_CHUNKS, 1, ST_W)
    bre = _block_diag(bbr.reshape(2, S5_CHUNKS, 8, S5_GROUP, S5_STATE)).astype(BF16)
    bim = _block_diag(bbi.reshape(2, S5_CHUNKS, 8, S5_GROUP, S5_STATE)).astype(BF16)
    c_t = [small[n].reshape(2, S5_CHUNKS, 8, S5_GROUP, S5_STATE).transpose(0, 1, 2, 4, 3) for n in ("s5_c_re", "s5_c_im")]
    cre = _block_diag(c_t[0]).astype(BF16)
    cim = _block_diag(c_t[1]).astype(BF16)
    tgt = jnp.concatenate([jnp.zeros((N_META, D_MODEL), F32), target], axis=0)

    h0, a1 = _embed_prenorm(get_w("meta", None)["meta_tokens"], x, small["ffn1_pre_g"])
    wts = dict(get_w("ffn1", [bias, are, aim, bre, bim, cre, cim, tgt, a1]))
    gate1, up1, f1 = _ffn_fwd(a1, wts["ffn1_w_gate"], wts["ffn1_w_up"], wts["ffn1_w_down"], "ffn1_fwd",
                              after=wts.get("tokens", ()))
    h1, a2 = _post_pre(f1, h0, small["ffn1_post_g"], small["mix_pre_g"], 0.5, "post_pre1")
    wts.update(get_w("w_in", a2))
    qkv = _proj_heads(a2, wts["w_in"])
    u = _proj_u(a2, wts["w_in"])
    ona = _na_fwd(qkv, bias)
    u_p = _perm_rows(u)
    sr, si, y2 = _s5_scan_fwd(u_p, bre, bim, are, aim, cre, cim)
    wts.update(get_w("mix", y2))
    os5_p, ypre_p = _s5_glu_fwd(u_p, y2, small["s5_d"], wts["s5_w_glu"], small["s5_b_glu"])
    os5 = _unperm_rows(os5_p)

    mix = _mix_out_fwd(ona, os5, small["na_out_g"], small["s5_out_g"], wts["w_out"])
    h2, a3 = _post_pre(mix, h1, small["mix_post_g"], small["ffn2_pre_g"], 1.0, "post_pre2")
    wts.update(get_w("ffn2", a3))
    gate2, up2, f2 = _ffn_fwd(a3, wts["ffn2_w_gate"], wts["ffn2_w_up"], wts["ffn2_w_down"], "ffn2_fwd")
    loss8, dh3, df2, g_final, g_ffn2_post = _final_loss(f2, h2, small["ffn2_post_g"], small["final_g"], tgt)

    da3, dwg2, dwu2, dwd2 = _ffn_bwd(df2, a3, gate2, up2, wts["ffn2_w_gate"], wts["ffn2_w_up"], wts["ffn2_w_down"], "ffn2_bwd")
    tok = emit("ffn2", {"ffn2_w_gate": dwg2, "ffn2_w_up": dwu2, "ffn2_w_down": dwd2})
    dh2, dmix, g_ffn2_pre, g_mix_post = _bwd_pre_post(da3, h2, _dep(small["ffn2_pre_g"], tok), dh3, mix, small["mix_post_g"], 1.0,
                                                      "bwd_pre_post2")
    dona, dos5, dwout, g_na_out, g_s5_out = _mix_out_bwd(dmix, ona, os5, small["na_out_g"], small["s5_out_g"], wts["w_out"])

    dypre_p, du_skip_p, dwglu, g_b_glu, g_s5_d = _s5_glu_bwd(_perm_rows(dos5), ypre_p, u_p, small["s5_d"], wts["s5_w_glu"],
                                                             small["s5_b_glu"])
    tok = emit("mix", {"s5_w_glu": dwglu.reshape(N_DEV, S5_WIDTH // N_DEV, S5_WIDTH).astype(BF16),
                       "w_out": dwout.reshape(N_DEV, D_MODEL // N_DEV, D_MODEL).astype(BF16)})
    du_p, dbr, dbi, dcr, dci, dar, dai = _s5_scan_bwd(dypre_p, du_skip_p, u_p, sr, si, bre, bim, _dep(are, tok), aim, cre, cim)
    du = _unperm_rows(du_p)
    dbbr = _diag_blocks(dbr, S5_GROUP, S5_STATE).reshape(64, S5_GROUP, S5_STATE)
    dbbi = _diag_blocks(dbi, S5_GROUP, S5_STATE).reshape(64, S5_GROUP, S5_STATE)
    g_lr, g_li, g_dt, g_br, g_bi = _s5_prep_bwd(lr, li, logdt, b_t[0], b_t[1], dar.reshape(64, S5_STATE),
                                                dai.reshape(64, S5_STATE), dbbr, dbbi)
    g_c = [_diag_blocks(d, S5_STATE, S5_GROUP).transpose(0, 1, 2, 4, 3).reshape(2 * S5_GROUPS, S5_GROUP, S5_STATE)
           for d in (dcr, dci)]

    dq, dk, dv, dbias = _na_bwd(qkv, bias, dona)
    g_rpb = _rpb_reduce(dbias)
    dense = jnp.stack([g.reshape(2 * S5_GROUPS, S5_STATE * S5_GROUP) for g in (g_br, g_bi, *g_c)])
    tok = emit("small", {"dense": dense, "na_rpb": g_rpb,
                         "s5_lam_re": g_lr.reshape(2, S5_GROUPS, S5_STATE), "s5_lam_im": g_li.reshape(2, S5_GROUPS, S5_STATE),
                         "s5_log_dt": g_dt.reshape(2, S5_GROUPS)})
    da2, dwin = _proj_bwd(dq, dk, dv, du, a2, wts["w_in"])
    tok2 = emit("w_in", {"w_in": dwin})
    tok = tok if tok2 is None else tok + tok2
    dh1, df1, g_mix_pre, g_ffn1_post = _bwd_pre_post(da2, h1, _dep(small["mix_pre_g"], tok), dh2, f1, small["ffn1_post_g"], 0.5,
                                                     "bwd_pre_post1")
    da1, dwg1, dwu1, dwd1 = _ffn_bwd(df1, a1, gate1, up1, wts["ffn1_w_gate"], wts["ffn1_w_up"], wts["ffn1_w_down"], "ffn1_bwd")
    emit("ffn1", {"ffn1_w_gate": dwg1, "ffn1_w_up": dwu1, "ffn1_w_down": dwd1})
    grad_x, grad_meta, g_ffn1_pre = _bwd_embed(da1, h0, small["ffn1_pre_g"], dh1)

    vec_g = {
        "ffn1_pre_g": g_ffn1_pre, "ffn1_post_g": g_ffn1_post, "mix_pre_g": g_mix_pre, "s5_d": g_s5_d, "s5_b_glu": g_b_glu,
        "na_out_g": g_na_out, "s5_out_g": g_s5_out, "mix_post_g": g_mix_post,
        "ffn2_pre_g": g_ffn2_pre, "ffn2_post_g": g_ffn2_post, "final_g": g_final,
    }
    return loss8, grad_x, grad_meta, vec_g


WEIGHT_NAMES = ['meta_tokens', 'ffn1_pre_g', 'ffn1_post_g', 'ffn1_w_gate', 'ffn1_w_up', 'ffn1_w_down', 'mix_pre_g', 'w_in',
                'na_rpb', 's5_lam_re', 's5_lam_im', 's5_log_dt', 's5_b_re', 's5_b_im', 's5_c_re', 's5_c_im', 's5_d',
                's5_w_glu', 's5_b_glu', 'na_out_g', 's5_out_g', 'w_out', 'mix_post_g', 'ffn2_pre_g', 'ffn2_post_g',
                'ffn2_w_gate', 'ffn2_w_up', 'ffn2_w_down', 'final_g']
BIG_NAMES = ['ffn1_w_gate', 'ffn1_w_up', 'ffn1_w_down', 'w_in', 's5_w_glu', 'w_out', 'ffn2_w_gate', 'ffn2_w_up', 'ffn2_w_down']
SMALL_NAMES = [n for n in WEIGHT_NAMES if n not in BIG_NAMES and n != 'meta_tokens']
WHOLE_NAMES = ['na_rpb', 's5_lam_re', 's5_lam_im', 's5_log_dt']
LEAD_NAMES = ['s5_b_re', 's5_b_im', 's5_c_re', 's5_c_im']


def kernel(x, meta_tokens, ffn1_pre_g, ffn1_post_g, ffn1_w_gate, ffn1_w_up, ffn1_w_down, mix_pre_g, w_in, na_rpb, s5_lam_re, s5_lam_im, s5_log_dt, s5_b_re, s5_b_im, s5_c_re, s5_c_im, s5_d, s5_w_glu, s5_b_glu, na_out_g, s5_out_g, w_out, mix_post_g, ffn2_pre_g, ffn2_post_g, ffn2_w_gate, ffn2_w_up, ffn2_w_down, final_g, loss_target, m_meta_tokens, m_ffn1_pre_g, m_ffn1_post_g, m_ffn1_w_gate, m_ffn1_w_up, m_ffn1_w_down, m_mix_pre_g, m_w_in, m_na_rpb, m_s5_lam_re, m_s5_lam_im, m_s5_log_dt, m_s5_b_re, m_s5_b_im, m_s5_c_re, m_s5_c_im, m_s5_d, m_s5_w_glu, m_s5_b_glu, m_na_out_g, m_s5_out_g, m_w_out, m_mix_post_g, m_ffn2_pre_g, m_ffn2_post_g, m_ffn2_w_gate, m_ffn2_w_up, m_ffn2_w_down, m_final_g, v_meta_tokens, v_ffn1_pre_g, v_ffn1_post_g, v_ffn1_w_gate, v_ffn1_w_up, v_ffn1_w_down, v_mix_pre_g, v_w_in, v_na_rpb, v_s5_lam_re, v_s5_lam_im, v_s5_log_dt, v_s5_b_re, v_s5_b_im, v_s5_c_re, v_s5_c_im, v_s5_d, v_s5_w_glu, v_s5_b_glu, v_na_out_g, v_s5_out_g, v_w_out, v_mix_post_g, v_ffn2_pre_g, v_ffn2_post_g, v_ffn2_w_gate, v_ffn2_w_up, v_ffn2_w_down, v_final_g):
    args = dict(locals())
    w = {n: args[n] for n in WEIGHT_NAMES}
    m = {n: args["m_" + n] for n in WEIGHT_NAMES}
    v = {n: args["v_" + n] for n in WEIGHT_NAMES}

    small = {n: w[n] for n in SMALL_NAMES}

    pending = {}

    def start(group, names, arrays, gather, peers=ALL_PEERS, slot=_slot8):
        lands = _place_own(arrays, gather, "own_" + group, slot)
        send_sems, recv_sems, arrays, lands, token = _exchange_start(arrays, lands, gather, "start_" + group, peers, slot)
        pending[group] = (names, send_sems, recv_sems, arrays, lands, gather, peers, slot)
        return token

    def finish(group, after):
        names, send_sems, recv_sems, arrays, lands, gather, peers, slot = pending.pop(group)
        lands, token = _exchange_wait(send_sems, recv_sems, arrays, lands, after, gather, "wait_" + group, peers, slot)
        return dict(zip(names, lands)), token

    first = ["ffn1_w_gate", "ffn1_w_up", "ffn1_w_down"]
    def shard(n, token=None):
        return _dep(_stored(n, w[n])[0], None if token is None else token[0, 0]).astype(BF16)

    ffn_names = ("ffn1_w_gate", "ffn1_w_up", "ffn1_w_down", "ffn2_w_gate", "ffn2_w_up", "ffn2_w_down")
    later_groups = (("w_in", ["w_in"]), ("mix", ["s5_w_glu", "w_out"]), ("ffn2", ["ffn2_w_gate", "ffn2_w_up", "ffn2_w_down"]))
    (meta_full,), token0 = _exchange([w["meta_tokens"]], True, "gather_meta")
    token1 = start("ffn1", first, [shard(n, token0) for n in first], True, (SIBLING,) + CHIP_PEERS)
    meta_full = _dep(meta_full.transpose(1, 0, 2).reshape(N_META, D_MODEL), token1[0, 0])
    later_shards = {n: shard(n, token1) for _, names in later_groups for n in names}
    for n in ("na_rpb", "s5_lam_re"):
        small[n] = _dep(small[n], token1[0, 0])

    def get_w(group, after):
        if group == "meta":
            return {"meta_tokens": meta_full}
        if group == "ffn1":
            after = list(after) + list(later_shards.values())
        got, token = finish(group, after)
        if group == "ffn1":
            got = dict(zip(got, _forward_sibling(list(got.values()), "forward_ffn1")))
            got["tokens"] = [start(g, names + ["order"], [later_shards[n] for n in names] + [token], True) for g, names in later_groups]
        if group == "mix":
            got = {"s5_w_glu": got["s5_w_glu"].reshape(S5_WIDTH, S5_WIDTH), "w_out": got["w_out"].reshape(D_MODEL, D_MODEL)}
        return {n: (a.reshape(D_FF, D_MODEL) if n in ffn_names else a) for n, a in got.items()}

    tokens = {}

    def emit(group, grads):
        grads = {n: (g.reshape(N_DEV, FF_SHARD, D_MODEL) if n in ffn_names else g) for n, g in grads.items()}
        if group == "ffn1":
            mine = [g.reshape((N_DEV // 2, 2) + g.shape[1:]) for g in grads.values()]
            theirs = _swap_sibling(mine, "swap_g_ffn1")
            sums = [_sum_pairs(a, b, "pair_sum_" + n) for n, a, b in zip(grads, mine, theirs)]
            tokens[group] = start("g_ffn1", list(grads), sums, False, CHIP_PEERS, _slot4)
        else:
            tokens[group] = start("g_" + group, list(grads), list(grads.values()), group == "small")
        return tokens[group][0, 0]

    loss8, grad_x, gmeta, vec_g = _local_step(x[0], loss_target[0], get_w, small, emit)
    res = {}

    def update_shard(n, pieces):
        outs = _adamw(_stored(n, w[n]), _stored(n, m[n]), _stored(n, v[n]), pieces, "adamw_" + n)
        res[n] = [_stored(n, o) for o in outs]

    late = [grad_x, tokens["ffn1"]]
    for group in ("g_ffn2", "g_mix", "g_w_in"):
        for n, pieces in finish(group, late)[0].items():
            update_shard(n, pieces)
    g8 = finish("g_small", late)[0]
    dense = _sum8(g8["dense"], "sum_dense")
    for i, n in enumerate(LEAD_NAMES):
        g = dense[i].reshape(_stored(n, w[n]).shape)
        upd = _adamw_s5_mat(_stored(n, w[n]), _stored(n, m[n]), _stored(n, v[n]), g, "adamw_" + n)
        res[n] = [_stored(n, o) for o in [g] + list(upd)]

    done = [res[n][1] for n in ("ffn2_w_gate", "ffn2_w_up", "ffn2_w_down", "w_in", "w_out", "s5_w_glu") + tuple(LEAD_NAMES)]
    (packed8, gmeta8), _ = _exchange([_pack_vectors(vec_g, loss8), gmeta], True, "gather_vectors", after=done)
    for n, pieces in finish("g_ffn1", packed8)[0].items():
        update_shard(n, pieces)
    _, _, _, me = _me()
    update_shard("meta_tokens", lax.dynamic_slice_in_dim(gmeta8, me * (D_MODEL // N_DEV), D_MODEL // N_DEV, axis=2))

    outs = _adamw_small(packed8, [(w[n], m[n], v[n]) for n in VEC_NAMES], [(w[n], m[n], v[n], g8[n]) for n in WHOLE_NAMES])
    for i, n in enumerate(VEC_NAMES + WHOLE_NAMES):
        res[n] = list(outs[4 * i:4 * i + 4])

    out = [outs[-1][0, 0], grad_x[None]]
    for kind in range(4):
        out += [res[n][kind] for n in WEIGHT_NAMES]
    return tuple(out)
```

```python
import functools
import math

import numpy as np
import jax
import jax.numpy as jnp
from jax import lax
from jax.experimental import pallas as pl
from jax.experimental.pallas import tpu as pltpu

F32 = jnp.float32
BF16 = jnp.bfloat16
SDS = jax.ShapeDtypeStruct

D_MODEL = 1024
N_TOK = 2048
N_META = 16
SEQ = N_TOK + N_META
ROW_TILE = 688
N_ROW_TILES = SEQ // ROW_TILE
N_DEV = 8
D_FF = 2816
FF_SHARD = D_FF // N_DEV
FF_TILE = 256
IN_SHARD = 256
NA_WIDTH = 512
S5_WIDTH = 512
HEADS = 8
HEAD_DIM = 64
GRID_W = 64
GRID_ROWS = N_TOK // GRID_W
KH = 8
KW = 16
NA_RB = 4
NA_KR = KH + NA_RB - 1
NA_BLOCKS = GRID_ROWS // NA_RB
NA_QB = NA_RB * GRID_W
NA_KB = NA_KR * GRID_W
NA_TYPES = 3
S5_GROUPS = 32
S5_GROUP = 16
S5_STATE = 64
S5_CHUNKS = 4
CH_W = S5_WIDTH // S5_CHUNKS
ST_W = S5_GROUPS * S5_STATE // S5_CHUNKS
SCAN_BLOCKS = 8
SCAN_T = SEQ // SCAN_BLOCKS
RMS_EPS = 1e-6
NEG_INF = -1e30
ATT_SCALE = HEAD_DIM ** -0.5
ADAM_LR, ADAM_B1, ADAM_B2, ADAM_EPS, ADAM_WD, ADAM_STEP = 0.001, 0.9, 0.999, 1e-08, 0.01, 10
VMEM_LIMIT = 56 * 1024 * 1024
MESH = pl.DeviceIdType.MESH
AXES = ("x", "y", "c")


def _params(sem=None):
    return pltpu.CompilerParams(dimension_semantics=sem, vmem_limit_bytes=VMEM_LIMIT)


def _dot(a, b):
    return jnp.dot(a, b, preferred_element_type=F32)


def _dot_nt(a, b):
    return lax.dot_general(a, b, (((1,), (1,)), ((), ())), preferred_element_type=F32)


def _dot_tn(a, b):
    return lax.dot_general(a, b, (((0,), (0,)), ((), ())), preferred_element_type=F32)


def _rstd(x):
    return lax.rsqrt(jnp.mean(x * x, axis=-1, keepdims=True) + RMS_EPS)


def _rms_bwd(x, r, g, dy):
    dyg = dy * g
    xr = x * r
    dx = r * (dyg - xr * jnp.mean(dyg * xr, axis=-1, keepdims=True))
    return dx, dy * xr


def _rows(i, size=ROW_TILE):
    return pl.ds(pl.multiple_of(i * size, 16), size)


def _row_spec(width):
    return pl.BlockSpec((ROW_TILE, width), lambda i: (i, 0))


def _fix_spec(shape):
    return pl.BlockSpec(shape, lambda i: (0,) * len(shape))


def _split3(x):
    hi = x.astype(BF16)
    r1 = x - hi.astype(F32)
    mid = r1.astype(BF16)
    lo = (r1 - mid.astype(F32)).astype(BF16)
    return hi, mid, lo


def _embed_prenorm(meta, x, g):
    def body(m_ref, x_ref, g_ref, h_ref, a_ref):
        h_ref[0:N_META, :] = m_ref[...]
        h_ref[N_META:, :] = x_ref[...]
        for i in range(N_ROW_TILES):
            rows = slice(i * ROW_TILE, (i + 1) * ROW_TILE)
            hv = h_ref[rows, :]
            a_ref[rows, :] = (hv * _rstd(hv) * g_ref[...]).astype(BF16)

    return pl.pallas_call(
        body, out_shape=[SDS((SEQ, D_MODEL), F32), SDS((SEQ, D_MODEL), BF16)], name="embed_prenorm",
        compiler_params=_params())(meta, x, g)


def _post_pre(f, hres, g_post, g_next, scale, name):
    def body(f_ref, h_ref, gp_ref, gn_ref, ho_ref, a_ref):
        fv = f_ref[...]
        h = h_ref[...] + scale * (fv * _rstd(fv) * gp_ref[...])
        ho_ref[...] = h
        a_ref[...] = (h * _rstd(h) * gn_ref[...]).astype(BF16)

    return pl.pallas_call(
        body, grid=(N_ROW_TILES,),
        in_specs=[_row_spec(D_MODEL), _row_spec(D_MODEL), _fix_spec((1, D_MODEL)), _fix_spec((1, D_MODEL))],
        out_specs=[_row_spec(D_MODEL), _row_spec(D_MODEL)],
        out_shape=[SDS((SEQ, D_MODEL), F32), SDS((SEQ, D_MODEL), BF16)], name=name,
        compiler_params=_params(("parallel",)))(f, hres, g_post, g_next)


def _final_loss(f2, h2, g_post, g_final, target):
    def body(f_ref, h_ref, gp_ref, gf_ref, t_ref, loss_ref, dh_ref, df_ref, dgf_ref, dgp_ref):
        i = pl.program_id(0)
        fv = f_ref[...]
        r1 = _rstd(fv)
        gp = gp_ref[...]
        h3 = h_ref[...] + 0.5 * (fv * r1 * gp)
        r2 = _rstd(h3)
        gf = gf_ref[...]
        y = h3 * r2 * gf
        row = lax.broadcasted_iota(jnp.int32, (ROW_TILE, 1), 0) + i * ROW_TILE
        err = jnp.where(row >= N_META, y - t_ref[...], 0.0)
        part = 0.5 * jnp.sum(jnp.mean(err * err, axis=-1, keepdims=True))
        dy = err * (1.0 / D_MODEL)
        dh3, dgf = _rms_bwd(h3, r2, gf, dy)
        dh_ref[...] = dh3
        df, dgp = _rms_bwd(fv, r1, gp, 0.5 * dh3)
        df_ref[...] = df.astype(BF16)

        @pl.when(i == 0)
        def _():
            loss_ref[...] = jnp.zeros_like(loss_ref)
            dgf_ref[...] = jnp.zeros_like(dgf_ref)
            dgp_ref[...] = jnp.zeros_like(dgp_ref)

        loss_ref[...] += part
        dgf_ref[...] += jnp.sum(dgf, axis=0, keepdims=True)
        dgp_ref[...] += jnp.sum(dgp, axis=0, keepdims=True)

    gain = _fix_spec((1, D_MODEL))
    return pl.pallas_call(
        body, grid=(N_ROW_TILES,),
        in_specs=[_row_spec(D_MODEL), _row_spec(D_MODEL), gain, gain, _row_spec(D_MODEL)],
        out_specs=[_fix_spec((8, 128)), _row_spec(D_MODEL), _row_spec(D_MODEL), gain, gain],
        out_shape=[SDS((8, 128), F32), SDS((SEQ, D_MODEL), F32), SDS((SEQ, D_MODEL), BF16),
                   SDS((1, D_MODEL), F32), SDS((1, D_MODEL), F32)],
        name="final_loss", compiler_params=_params(("arbitrary",)))(f2, h2, g_post, g_final, target)


def _bwd_pre_post(da, h, g_pre, dh_res, fprev, g_post, scale, name):
    def body(da_ref, h_ref, gpre_ref, dhr_ref, f_ref, gpost_ref, dh_ref, df_ref, dgpre_ref, dgpost_ref):
        i = pl.program_id(0)
        hv = h_ref[...]
        dxa, dgpre = _rms_bwd(hv, _rstd(hv), gpre_ref[...], da_ref[...])
        dh = dhr_ref[...] + dxa
        dh_ref[...] = dh
        fv = f_ref[...]
        df, dgpost = _rms_bwd(fv, _rstd(fv), gpost_ref[...], scale * dh)
        df_ref[...] = df.astype(BF16)

        @pl.when(i == 0)
        def _():
            dgpre_ref[...] = jnp.zeros_like(dgpre_ref)
            dgpost_ref[...] = jnp.zeros_like(dgpost_ref)

        dgpre_ref[...] += jnp.sum(dgpre, axis=0, keepdims=True)
        dgpost_ref[...] += jnp.sum(dgpost, axis=0, keepdims=True)

    gain = _fix_spec((1, D_MODEL))
    row = _row_spec(D_MODEL)
    return pl.pallas_call(
        body, grid=(N_ROW_TILES,), in_specs=[row, row, gain, row, row, gain],
        out_specs=[row, row, gain, gain],
        out_shape=[SDS((SEQ, D_MODEL), F32), SDS((SEQ, D_MODEL), BF16), SDS((1, D_MODEL), F32), SDS((1, D_MODEL), F32)],
        name=name, compiler_params=_params(("arbitrary",)))(da, h, g_pre, dh_res, fprev, g_post)


def _bwd_embed(da, h, g_pre, dh_res):
    def body(da_ref, h_ref, gpre_ref, dhr_ref, gx_ref, gm_ref, dgpre_ref):
        total = jnp.zeros((1, D_MODEL), F32)
        for i in range(N_ROW_TILES):
            rows = slice(i * ROW_TILE, (i + 1) * ROW_TILE)
            hv = h_ref[rows, :]
            dxa, dgpre = _rms_bwd(hv, _rstd(hv), gpre_ref[...], da_ref[rows, :])
            dh = dhr_ref[rows, :] + dxa
            total = total + jnp.sum(dgpre, axis=0, keepdims=True)
            if i == 0:
                gm_ref[...] = dh[0:N_META, :]
                gx_ref[0:ROW_TILE - N_META, :] = dh[N_META:, :]
            else:
                gx_ref[i * ROW_TILE - N_META:(i + 1) * ROW_TILE - N_META, :] = dh
        dgpre_ref[...] = total

    return pl.pallas_call(
        body, out_shape=[SDS((N_TOK, D_MODEL), F32), SDS((N_META, D_MODEL), F32), SDS((1, D_MODEL), F32)],
        name="bwd_embed", compiler_params=_params())(da, h, g_pre, dh_res)


def _ffn_fwd(a, wg, wu, wd, name, after=()):
    def body(a_ref, wg_ref, wu_ref, wd_ref, *rest):
        gate_ref, up_ref, f_ref = rest[len(after):]
        j = pl.program_id(0)

        def tile(i, carry):
            rows = _rows(i)
            at = a_ref[rows, :]
            gate = _dot_nt(at, wg_ref[...])
            up = _dot_nt(at, wu_ref[...])
            gate_ref[rows, :] = gate.astype(BF16)
            up_ref[rows, :] = up.astype(BF16)
            act = (gate * jax.nn.sigmoid(gate) * up).astype(BF16)
            contrib = _dot(act, wd_ref[...])

            @pl.when(j == 0)
            def _():
                f_ref[rows, :] = contrib

            @pl.when(j != 0)
            def _():
                f_ref[rows, :] += contrib

            return carry

        lax.fori_loop(0, N_ROW_TILES, tile, 0)

    wtile = pl.BlockSpec((FF_TILE, D_MODEL), lambda j: (j, 0))
    hid = pl.BlockSpec((SEQ, FF_TILE), lambda j: (0, j))
    full = pl.BlockSpec((SEQ, D_MODEL), lambda j: (0, 0))
    return pl.pallas_call(
        body, grid=(D_FF // FF_TILE,), in_specs=[full, wtile, wtile, wtile] + [pl.BlockSpec(memory_space=pl.ANY)] * len(after),
        out_specs=[hid, hid, full],
        out_shape=[SDS((SEQ, D_FF), BF16), SDS((SEQ, D_FF), BF16), SDS((SEQ, D_MODEL), F32)],
        name=name, compiler_params=_params(("arbitrary",)))(a, wg, wu, wd, *after)


def _ffn_bwd(df, a, gate, up, wg, wu, wd, name):
    def body(df_ref, a_ref, gate_ref, up_ref, wg_ref, wu_ref, wd_ref, da_ref, dwg_ref, dwu_ref, dwd_ref,
             acc_g, acc_u, acc_d):
        j = pl.program_id(0)

        def tile(i, carry):
            rows = _rows(i)
            dft = df_ref[rows, :]
            at = a_ref[rows, :]
            gate = gate_ref[rows, :].astype(F32)
            up = up_ref[rows, :].astype(F32)
            dact = _dot_nt(dft, wd_ref[...])
            sig = jax.nn.sigmoid(gate)
            silu = gate * sig
            dgate = (dact * up * (sig * (1.0 + gate * (1.0 - sig)))).astype(BF16)
            dup = (dact * silu).astype(BF16)
            act = (silu * up).astype(BF16)
            dwd = _dot_tn(act, dft)
            dwg = _dot_tn(dgate, at)
            dwu = _dot_tn(dup, at)
            dat = _dot(dgate, wg_ref[...]) + _dot(dup, wu_ref[...])

            @pl.when(i == 0)
            def _():
                acc_d[...] = dwd
                acc_g[...] = dwg
                acc_u[...] = dwu

            @pl.when(i != 0)
            def _():
                acc_d[...] += dwd
                acc_g[...] += dwg
                acc_u[...] += dwu

            @pl.when(j == 0)
            def _():
                da_ref[rows, :] = dat

            @pl.when(j != 0)
            def _():
                da_ref[rows, :] += dat

            return carry

        lax.fori_loop(0, N_ROW_TILES, tile, 0)
        dwg_ref[...] = acc_g[...].astype(BF16)
        dwu_ref[...] = acc_u[...].astype(BF16)
        dwd_ref[...] = acc_d[...].astype(BF16)

    wtile = pl.BlockSpec((FF_TILE, D_MODEL), lambda j: (j, 0))
    hid = pl.BlockSpec((SEQ, FF_TILE), lambda j: (0, j))
    full = pl.BlockSpec((SEQ, D_MODEL), lambda j: (0, 0))
    return pl.pallas_call(
        body, grid=(D_FF // FF_TILE,), in_specs=[full, full, hid, hid, wtile, wtile, wtile],
        out_specs=[full, wtile, wtile, wtile],
        out_shape=[SDS((SEQ, D_MODEL), F32)] + [SDS((D_FF, D_MODEL), BF16)] * 3,
        scratch_shapes=[pltpu.VMEM((FF_TILE, D_MODEL), F32)] * 3,
        name=name, compiler_params=_params(("arbitrary",)))(df, a, gate, up, wg, wu, wd)


HEADS_PER_BLOCK = IN_SHARD // HEAD_DIM
QKV_BLOCKS = 3 * NA_WIDTH // IN_SHARD


def _proj_heads(a, w):
    def body(a_ref, w_ref, o_ref):
        def tile(i, carry):
            rows = _rows(i)
            res = _dot(a_ref[rows, :], w_ref[...])
            for sub in range(HEADS_PER_BLOCK):
                o_ref[sub, rows, :] = res[:, sub * HEAD_DIM:(sub + 1) * HEAD_DIM]
            return carry

        lax.fori_loop(0, N_ROW_TILES, tile, 0)

    return pl.pallas_call(
        body, grid=(QKV_BLOCKS,),
        in_specs=[pl.BlockSpec((SEQ, D_MODEL), lambda j: (0, 0)), pl.BlockSpec((None, D_MODEL, IN_SHARD), lambda j: (j, 0, 0))],
        out_specs=pl.BlockSpec((HEADS_PER_BLOCK, SEQ, HEAD_DIM), lambda j: (j, 0, 0)),
        out_shape=SDS((3 * HEADS, SEQ, HEAD_DIM), F32), name="proj_heads",
        compiler_params=_params(("parallel",)))(a, w)


def _proj_u(a, w):
    def body(a_ref, w_ref, o_ref):
        def tile(i, carry):
            rows = _rows(i)
            o_ref[rows, :] = _dot(a_ref[rows, :], w_ref[...])
            return carry

        lax.fori_loop(0, N_ROW_TILES, tile, 0)

    return pl.pallas_call(
        body, grid=(N_DEV - QKV_BLOCKS,),
        in_specs=[pl.BlockSpec((SEQ, D_MODEL), lambda j: (0, 0)),
                  pl.BlockSpec((None, D_MODEL, IN_SHARD), lambda j: (j + QKV_BLOCKS, 0, 0))],
        out_specs=pl.BlockSpec((SEQ, IN_SHARD), lambda j: (0, j)),
        out_shape=SDS((SEQ, S5_WIDTH), F32), name="proj_u",
        compiler_params=_params(("parallel",)))(a, w)


def _proj_bwd(dq, dk, dv, du, a, w):
    def body(dq_ref, dk_ref, dv_ref, du_ref, a_ref, w_ref, da_ref, dw_ref, acc, dp_ref):
        j = pl.program_id(0)

        for which, src in enumerate((dq_ref, dk_ref, dv_ref)):
            @pl.when((j >= 2 * which) & (j < 2 * which + 2))
            def _(src=src):
                dp_ref[...] = jnp.concatenate([src[sub] for sub in range(HEADS_PER_BLOCK)], axis=-1).astype(BF16)

        @pl.when(j >= QKV_BLOCKS)
        def _():
            dp_ref[...] = du_ref[...].astype(BF16)

        def tile(i, carry):
            rows = _rows(i)
            dpt = dp_ref[rows, :]
            dw = _dot_tn(a_ref[rows, :], dpt)
            dat = _dot_nt(dpt, w_ref[...])

            @pl.when(i == 0)
            def _():
                acc[...] = dw

            @pl.when(i != 0)
            def _():
                acc[...] += dw

            @pl.when(j == 0)
            def _():
                da_ref[rows, :] = dat

            @pl.when(j != 0)
            def _():
                da_ref[rows, :] += dat

            return carry

        lax.fori_loop(0, N_ROW_TILES, tile, 0)
        dw_ref[...] = acc[...].astype(BF16)

    full = pl.BlockSpec((SEQ, D_MODEL), lambda j: (0, 0))
    wspec = pl.BlockSpec((None, D_MODEL, IN_SHARD), lambda j: (j, 0, 0))

    def heads(which):
        return pl.BlockSpec((HEADS_PER_BLOCK, SEQ, HEAD_DIM), lambda j: (jnp.clip(j - 2 * which, 0, 1), 0, 0))

    return pl.pallas_call(
        body, grid=(N_DEV,),
        in_specs=[heads(0), heads(1), heads(2),
                  pl.BlockSpec((SEQ, IN_SHARD), lambda j: (0, jnp.clip(j - QKV_BLOCKS, 0, 1))), full, wspec],
        out_specs=[full, wspec],
        out_shape=[SDS((SEQ, D_MODEL), F32), SDS((N_DEV, D_MODEL, IN_SHARD), BF16)],
        scratch_shapes=[pltpu.VMEM((D_MODEL, IN_SHARD), F32), pltpu.VMEM((SEQ, IN_SHARD), BF16)],
        name="proj_bwd", compiler_params=_params(("arbitrary",)))(dq, dk, dv, du, a, w)


def _na_consts():
    c = np.arange(GRID_W)
    col_start = np.clip(c - KW // 2, 0, GRID_W - KW)
    col_in = (c[None, :] >= col_start[:, None]) & (c[None, :] < col_start[:, None] + KW)
    dc = np.clip(c[None, :] - c[:, None] + KW - 1, 0, 2 * KW - 2)
    onehot = np.zeros((128, GRID_W * GRID_W), np.float32)
    qq, kk = np.meshgrid(c, c, indexing="ij")
    onehot[dc[col_in], (qq * GRID_W + kk)[col_in]] = 1.0
    negmask = np.where(col_in, 0.0, NEG_INF).astype(np.float32).reshape(1, -1)
    return onehot, negmask


def _na_pair(block_type, a, b):
    if block_type == 0:
        return b - a + KH - 1 if b < KH else None
    if block_type == 1:
        return b - a + KH // 2 - 1 if a <= b < a + KH else None
    return b - a if b >= NA_KR - KH else None


def _rpb_expand(rpb):
    onehot, negmask = _na_consts()
    rows = HEADS * (2 * KH - 1)
    rpb_pad = jnp.pad(rpb.reshape(rows, 2 * KW - 1), ((0, 128 - rows), (0, 128 - (2 * KW - 1))))

    def body(r_ref, oh_ref, m_ref, t_ref):
        hi, mid, lo = _split3(r_ref[...])
        oh = oh_ref[...]
        t_ref[...] = _dot(hi, oh) + _dot(mid, oh) + _dot(lo, oh) + m_ref[...]

    table = pl.pallas_call(body, out_shape=SDS((128, GRID_W * GRID_W), F32), name="rpb_expand",
                           compiler_params=_params())(rpb_pad, jnp.asarray(onehot, BF16), jnp.asarray(negmask))
    return table[:rows].reshape(HEADS, 2 * KH - 1, GRID_W, GRID_W)


def _rpb_reduce(dslabs):
    onehot, _ = _na_consts()
    rows = HEADS * (2 * KH - 1)

    def body(x_ref, oht_ref, o_ref):
        hi, mid, lo = _split3(x_ref[...])
        oht = oht_ref[...]
        o_ref[...] = _dot(hi, oht) + _dot(mid, oht) + _dot(lo, oht)

    out = pl.pallas_call(body, out_shape=SDS((rows, 128), F32), name="rpb_reduce", compiler_params=_params())(
        dslabs.reshape(rows, GRID_W * GRID_W), jnp.asarray(onehot.T, BF16))
    return out.reshape(HEADS, 2 * KH - 1, 128)


def _bias_tiles(slab_ref, tile_ref):
    tile_ref[...] = jnp.full(tile_ref.shape, NEG_INF, F32)
    for t in range(NA_TYPES):
        for a in range(NA_RB):
            for b in range(NA_KR):
                dr = _na_pair(t, a, b)
                if dr is not None:
                    tile_ref[t, a * GRID_W:(a + 1) * GRID_W, b * GRID_W:(b + 1) * GRID_W] = slab_ref[dr]


def _bias_tiles_bwd(dtile_ref, dslab_ref):
    acc = {}
    for t in range(NA_TYPES):
        for a in range(NA_RB):
            for b in range(NA_KR):
                dr = _na_pair(t, a, b)
                if dr is not None:
                    part = dtile_ref[t, a * GRID_W:(a + 1) * GRID_W, b * GRID_W:(b + 1) * GRID_W]
                    acc[dr] = part if dr not in acc else acc[dr] + part
    for dr in range(2 * KH - 1):
        dslab_ref[dr] = acc[dr]


def _block_geometry(g):
    start = jnp.clip(g * NA_RB - KH // 2, 0, GRID_ROWS - NA_KR)
    block_type = jnp.where(g == 0, 0, jnp.where(g == NA_BLOCKS - 1, 2, 1))
    q0 = pl.multiple_of(N_META + g * NA_QB, 16)
    k0 = pl.multiple_of(N_META + start * GRID_W, 16)
    return block_type, q0, k0


def _na_probs(q, kk, km, bias):
    s = _dot_nt(q, kk) * ATT_SCALE + bias
    sm = _dot_nt(q, km) * ATT_SCALE
    m = jnp.maximum(jnp.max(s, axis=-1, keepdims=True), jnp.max(sm, axis=-1, keepdims=True))
    p = jnp.exp(s - m)
    pm = jnp.exp(sm - m)
    inv = 1.0 / (jnp.sum(p, axis=-1, keepdims=True) + jnp.sum(pm, axis=-1, keepdims=True))
    return p * inv, pm * inv


def _meta_probs(qm, km):
    s = _dot_nt(qm, km) * ATT_SCALE
    p = jnp.exp(s - jnp.max(s, axis=-1, keepdims=True))
    return p / jnp.sum(p, axis=-1, keepdims=True)


def _qkv_specs():
    return [pl.BlockSpec((None, SEQ, HEAD_DIM), lambda h, which=which: (h + which * HEADS, 0, 0)) for which in range(3)]


def _na_fwd(qkv, bias):
    def body(q_ref, k_ref, v_ref, slab_ref, o_ref, b_ref):
        _bias_tiles(slab_ref, b_ref)
        km = k_ref[0:N_META, :].astype(BF16)
        vm = v_ref[0:N_META, :].astype(BF16)
        pmm = _meta_probs(q_ref[0:N_META, :].astype(BF16), km)
        o_ref[0:N_META, :] = _dot(pmm.astype(BF16), vm)

        def block(g, carry):
            block_type, q0, k0 = _block_geometry(g)
            qb = q_ref[pl.ds(q0, NA_QB), :].astype(BF16)
            kk = k_ref[pl.ds(k0, NA_KB), :].astype(BF16)
            vv = v_ref[pl.ds(k0, NA_KB), :].astype(BF16)
            p, pm = _na_probs(qb, kk, km, b_ref[block_type])
            o_ref[pl.ds(q0, NA_QB), :] = _dot(p.astype(BF16), vv) + _dot(pm.astype(BF16), vm)
            return carry

        lax.fori_loop(0, NA_BLOCKS, block, 0)

    head = pl.BlockSpec((None, SEQ, HEAD_DIM), lambda h: (h, 0, 0))
    return pl.pallas_call(
        body, grid=(HEADS,), in_specs=_qkv_specs() + [pl.BlockSpec((None, 2 * KH - 1, GRID_W, GRID_W), lambda h: (h, 0, 0, 0))],
        out_specs=head, out_shape=SDS((HEADS, SEQ, HEAD_DIM), F32), name="na_fwd",
        scratch_shapes=[pltpu.VMEM((NA_TYPES, NA_QB, NA_KB), F32)],
        compiler_params=_params(("parallel",)))(qkv, qkv, qkv, bias)


def _na_bwd(qkv, bias, do):
    def body(q_ref, k_ref, v_ref, slab_ref, do_ref, dq_ref, dk_ref, dv_ref, dslab_ref, b_ref, db_ref):
        _bias_tiles(slab_ref, b_ref)
        km = k_ref[0:N_META, :].astype(BF16)
        vm = v_ref[0:N_META, :].astype(BF16)
        dk_ref[...] = jnp.zeros_like(dk_ref)
        dv_ref[...] = jnp.zeros_like(dv_ref)
        db_ref[...] = jnp.zeros_like(db_ref)

        qm = q_ref[0:N_META, :].astype(BF16)
        dom = do_ref[0:N_META, :].astype(BF16)
        pmm = _meta_probs(qm, km)
        dpm = _dot_nt(dom, vm)
        dsm = (pmm * (dpm - jnp.sum(pmm * dpm, axis=-1, keepdims=True)) * ATT_SCALE).astype(BF16)
        dq_ref[0:N_META, :] = _dot(dsm, km)
        dkm0 = _dot_tn(dsm, qm)
        dvm0 = _dot_tn(pmm.astype(BF16), dom)

        def block(g, carry):
            dkm, dvm = carry
            block_type, q0, k0 = _block_geometry(g)
            qb = q_ref[pl.ds(q0, NA_QB), :].astype(BF16)
            kk = k_ref[pl.ds(k0, NA_KB), :].astype(BF16)
            vv = v_ref[pl.ds(k0, NA_KB), :].astype(BF16)
            dob = do_ref[pl.ds(q0, NA_QB), :].astype(BF16)
            p, pm = _na_probs(qb, kk, km, b_ref[block_type])
            dp = _dot_nt(dob, vv)
            dpm_ = _dot_nt(dob, vm)
            delta = jnp.sum(p * dp, axis=-1, keepdims=True) + jnp.sum(pm * dpm_, axis=-1, keepdims=True)
            ds = p * (dp - delta)
            dsm_ = pm * (dpm_ - delta)
            db_ref[block_type] += ds
            dsb = (ds * ATT_SCALE).astype(BF16)
            dsmb = (dsm_ * ATT_SCALE).astype(BF16)
            dq_ref[pl.ds(q0, NA_QB), :] = _dot(dsb, kk) + _dot(dsmb, km)
            dk_ref[pl.ds(k0, NA_KB), :] += _dot_tn(dsb, qb)
            dv_ref[pl.ds(k0, NA_KB), :] += _dot_tn(p.astype(BF16), dob)
            return dkm + _dot_tn(dsmb, qb), dvm + _dot_tn(pm.astype(BF16), dob)

        dkm, dvm = lax.fori_loop(0, NA_BLOCKS, block, (dkm0, dvm0))
        dk_ref[0:N_META, :] = dkm
        dv_ref[0:N_META, :] = dvm
        _bias_tiles_bwd(db_ref, dslab_ref)

    head = pl.BlockSpec((None, SEQ, HEAD_DIM), lambda h: (h, 0, 0))
    bspec = pl.BlockSpec((None, 2 * KH - 1, GRID_W, GRID_W), lambda h: (h, 0, 0, 0))
    return pl.pallas_call(
        body, grid=(HEADS,), in_specs=_qkv_specs() + [bspec, head], out_specs=[head, head, head, bspec],
        out_shape=[SDS((HEADS, SEQ, HEAD_DIM), F32)] * 3 + [SDS((HEADS, 2 * KH - 1, GRID_W, GRID_W), F32)],
        scratch_shapes=[pltpu.VMEM((NA_TYPES, NA_QB, NA_KB), F32), pltpu.VMEM((NA_TYPES, NA_QB, NA_KB), F32)],
        name="na_bwd", compiler_params=_params(("parallel",)))(qkv, qkv, qkv, bias, do)


def _cmul(ar, ai, br, bi):
    return ar * br - ai * bi, ar * bi + ai * br


def _cpow(ar, ai, n):
    rr, ri = None, None
    br, bi = ar, ai
    while n:
        if n & 1:
            rr, ri = (br, bi) if rr is None else _cmul(rr, ri, br, bi)
        n >>= 1
        if n:
            br, bi = _cmul(br, bi, br, bi)
    return rr, ri


def _s5_prep(lr, li, logdt, bre, bim):
    def body(lr_ref, li_ref, dt_ref, br_ref, bi_ref, lbr_ref, lbi_ref, bbr_ref, bbi_ref):
        lr_, li_ = lr_ref[...], li_ref[...]
        dt = jnp.exp(dt_ref[...])
        mag = jnp.exp(lr_ * dt)
        lbr = mag * jnp.cos(li_ * dt)
        lbi = mag * jnp.sin(li_ * dt)
        lbr_ref[...] = lbr
        lbi_ref[...] = lbi
        den = lr_ * lr_ + li_ * li_
        xr = lbr - 1.0
        cr = (xr * lr_ + lbi * li_) / den
        ci = (lbi * lr_ - xr * li_) / den
        br, bi = br_ref[...], bi_ref[...]
        bbr_ref[...] = cr[:, None, :] * br - ci[:, None, :] * bi
        bbi_ref[...] = cr[:, None, :] * bi + ci[:, None, :] * br

    n = 2 * S5_GROUPS
    return pl.pallas_call(
        body, out_shape=[SDS((n, S5_STATE), F32)] * 2 + [SDS((n, S5_GROUP, S5_STATE), F32)] * 2,
        name="s5_prep", compiler_params=_params())(lr, li, logdt, bre, bim)


def _s5_prep_bwd(lr, li, logdt, bre, bim, dar, dai, dbbr, dbbi):
    def body(lr_ref, li_ref, dt_ref, br_ref, bi_ref, dar_ref, dai_ref, dbr_ref, dbi_ref,
             glr_ref, gli_ref, gdt_ref, gbr_ref, gbi_ref):
        lr_, li_ = lr_ref[...], li_ref[...]
        dt = jnp.exp(dt_ref[...])
        mag = jnp.exp(lr_ * dt)
        lbr = mag * jnp.cos(li_ * dt)
        lbi = mag * jnp.sin(li_ * dt)
        den = lr_ * lr_ + li_ * li_
        xr = lbr - 1.0
        cr = (xr * lr_ + lbi * li_) / den
        ci = (lbi * lr_ - xr * li_) / den
        br, bi = br_ref[...], bi_ref[...]
        dbr, dbi = dbr_ref[...], dbi_ref[...]
        gbr_ref[...] = cr[:, None, :] * dbr + ci[:, None, :] * dbi
        gbi_ref[...] = cr[:, None, :] * dbi - ci[:, None, :] * dbr
        gcr = jnp.sum(dbr * br + dbi * bi, axis=1)
        gci = jnp.sum(dbi * br - dbr * bi, axis=1)
        ilr, ili = lr_ / den, li_ / den
        tr, ti = _cmul(gcr, gci, ilr, ili)
        glbr = dar_ref[...] + tr
        glbi = dai_ref[...] + ti
        dr_, di_ = _cmul(tr, ti, cr, -ci)
        gwr, gwi = _cmul(glbr, glbi, lbr, -lbi)
        glr_ref[...] = gwr * dt - dr_
        gli_ref[...] = gwi * dt - di_
        gdt_ref[...] = jnp.sum(gwr * lr_ + gwi * li_, axis=-1, keepdims=True) * dt

    n = 2 * S5_GROUPS
    return pl.pallas_call(
        body, out_shape=[SDS((n, S5_STATE), F32)] * 2 + [SDS((n, 1), F32)] + [SDS((n, S5_GROUP, S5_STATE), F32)] * 2,
        name="s5_prep_bwd", compiler_params=_params())(lr, li, logdt, bre, bim, dar, dai, dbbr, dbbi)


def _scan_local(xr_ref, xi_ref, ar8, ai8, reverse):
    def step(i, carry):
        sr, si = carry
        idx = (SCAN_T - 1 - i) if reverse else i
        rows = pl.ds(pl.multiple_of(idx * SCAN_BLOCKS, SCAN_BLOCKS), SCAN_BLOCKS)
        nr = ar8 * sr - ai8 * si + xr_ref[rows, :]
        ni = ar8 * si + ai8 * sr + xi_ref[rows, :]
        xr_ref[rows, :] = nr
        xi_ref[rows, :] = ni
        return nr, ni

    z = jnp.zeros(ar8.shape, F32)
    return lax.fori_loop(0, SCAN_T, step, (z, z))


def _scan_carries(er, ei, atr, ati, reverse):
    row = lax.broadcasted_iota(jnp.int32, er.shape, 0)
    cr = jnp.zeros((1, er.shape[1]), F32)
    ci = cr
    outr = jnp.zeros(er.shape, F32)
    outi = outr
    order = range(SCAN_BLOCKS - 1, -1, -1) if reverse else range(SCAN_BLOCKS)
    for b in order:
        outr = jnp.where(row == b, cr, outr)
        outi = jnp.where(row == b, ci, outi)
        nr, ni = _cmul(atr, ati, cr, ci)
        cr, ci = nr + er[b:b + 1, :], ni + ei[b:b + 1, :]
    return outr, outi


def _scan_fixup(xr_ref, xi_ref, cr8, ci8, ar8, ai8, reverse):
    def step(i, carry):
        pr, pi = carry
        idx = (SCAN_T - 1 - i) if reverse else i
        rows = pl.ds(pl.multiple_of(idx * SCAN_BLOCKS, SCAN_BLOCKS), SCAN_BLOCKS)
        fr, fi = _cmul(pr, pi, cr8, ci8)
        xr_ref[rows, :] += fr
        xi_ref[rows, :] += fi
        return _cmul(pr, pi, ar8, ai8)

    lax.fori_loop(0, SCAN_T, step, (ar8, ai8), unroll=2)


def _scan(xr_ref, xi_ref, ar, ai, reverse):
    n = ar.shape[1]
    ar8 = jnp.broadcast_to(ar, (SCAN_BLOCKS, n))
    ai8 = jnp.broadcast_to(ai, (SCAN_BLOCKS, n))
    er, ei = _scan_local(xr_ref, xi_ref, ar8, ai8, reverse)
    atr, ati = _cpow(ar, ai, SCAN_T)
    cr8, ci8 = _scan_carries(er, ei, atr, ati, reverse)
    _scan_fixup(xr_ref, xi_ref, cr8, ci8, ar8, ai8, reverse)


def _s5_specs():
    chan = pl.BlockSpec((SEQ, CH_W), lambda c, d: (0, c))
    chan2 = pl.BlockSpec((None, SEQ, CH_W), lambda c, d: (d, 0, c))
    state = pl.BlockSpec((None, SEQ, ST_W), lambda c, d: (d, 0, c))
    bmat = pl.BlockSpec((None, None, CH_W, ST_W), lambda c, d: (d, c, 0, 0))
    cmat = pl.BlockSpec((None, None, ST_W, CH_W), lambda c, d: (d, c, 0, 0))
    avec = pl.BlockSpec((None, None, 1, ST_W), lambda c, d: (d, c, 0, 0))
    return chan, chan2, state, bmat, cmat, avec


def _scan_by_direction(xr_ref, xi_ref, ar, ai, d, adjoint):
    @pl.when(d == 0)
    def _():
        _scan(xr_ref, xi_ref, ar, ai, reverse=adjoint)

    @pl.when(d == 1)
    def _():
        _scan(xr_ref, xi_ref, ar, ai, reverse=not adjoint)


def _to_scan_order(src_ref, dst_ref):
    def step(i, carry):
        dst_ref[pl.ds(pl.multiple_of(i * SCAN_BLOCKS, SCAN_BLOCKS), SCAN_BLOCKS), :] = src_ref[pl.ds(i, SCAN_BLOCKS, stride=SCAN_T), :]
        return carry

    lax.fori_loop(0, SCAN_T, step, 0)


def _to_time_order(src_ref, dst_ref):
    def step(i, carry):
        dst_ref[pl.ds(i, SCAN_BLOCKS, stride=SCAN_T), :] = src_ref[pl.ds(pl.multiple_of(i * SCAN_BLOCKS, SCAN_BLOCKS), SCAN_BLOCKS), :]
        return carry

    lax.fori_loop(0, SCAN_T, step, 0)


def _s5_scan_fwd(u, bre, bim, are, aim, cre, cim):
    def body(u_ref, bre_ref, bim_ref, are_ref, aim_ref, cre_ref, cim_ref, sr_ref, si_ref, y_ref, tmp_ref):
        _to_scan_order(u_ref, tmp_ref)
        ub = tmp_ref[...].astype(BF16)
        sr_ref[...] = _dot(ub, bre_ref[...])
        si_ref[...] = _dot(ub, bim_ref[...])
        _scan_by_direction(sr_ref, si_ref, are_ref[...], aim_ref[...], pl.program_id(1), adjoint=False)
        tmp_ref[...] = _dot(sr_ref[...].astype(BF16), cre_ref[...]) - _dot(si_ref[...].astype(BF16), cim_ref[...])
        _to_time_order(tmp_ref, y_ref)

    chan, chan2, state, bmat, cmat, avec = _s5_specs()
    return pl.pallas_call(
        body, grid=(S5_CHUNKS, 2), in_specs=[chan, bmat, bmat, avec, avec, cmat, cmat], out_specs=[state, state, chan2],
        out_shape=[SDS((2, SEQ, S5_GROUPS * S5_STATE), F32)] * 2 + [SDS((2, SEQ, S5_WIDTH), F32)],
        scratch_shapes=[pltpu.VMEM((SEQ, CH_W), F32)],
        name="s5_scan_fwd", compiler_params=_params(("parallel", "parallel")))(u, bre, bim, are, aim, cre, cim)


def _dlam(gr_ref, gi_ref, sr_ref, si_ref, reverse):
    tile = lambda i: pl.ds(pl.multiple_of(i * SCAN_BLOCKS, SCAN_BLOCKS), SCAN_BLOCKS)
    row = lax.broadcasted_iota(jnp.int32, (SCAN_BLOCKS, ST_W), 0)
    if reverse:
        edge, src, shift, empty, lo, hi, dprev = SCAN_T - 1, 0, SCAN_BLOCKS - 1, SCAN_BLOCKS - 1, 0, SCAN_T - 1, 1
    else:
        edge, src, shift, empty, lo, hi, dprev = 0, SCAN_T - 1, 1, 0, 1, SCAN_T, -1
    spr = jnp.where(row == empty, 0.0, pltpu.roll(sr_ref[tile(src), :], shift, 0))
    spi = jnp.where(row == empty, 0.0, pltpu.roll(si_ref[tile(src), :], shift, 0))
    acc0 = _cmul(gr_ref[tile(edge), :], gi_ref[tile(edge), :], spr, -spi)

    def step(i, carry):
        accr, acci = carry
        pr, pi = _cmul(gr_ref[tile(i), :], gi_ref[tile(i), :], sr_ref[tile(i + dprev), :], -si_ref[tile(i + dprev), :])
        return accr + pr, acci + pi

    accr, acci = lax.fori_loop(lo, hi, step, acc0)
    return jnp.sum(accr, axis=0, keepdims=True), jnp.sum(acci, axis=0, keepdims=True)


def _s5_scan_bwd(dy, du_skip, u, sr, si, bre, bim, are, aim, cre, cim):
    def body(dy_nat, dus_nat, u_nat, sr_ref, si_ref, bre_ref, bim_ref, are_ref, aim_ref, cre_ref, cim_ref,
             du_nat, dbr_ref, dbi_ref, dcr_ref, dci_ref, dar_ref, dai_ref, gr_ref, gi_ref, dy_ref, du_ref, u_ref):
        d = pl.program_id(1)

        @pl.when(d == 0)
        def _():
            _to_scan_order(dy_nat, dy_ref)
            _to_scan_order(u_nat, u_ref)
            _to_scan_order(dus_nat, du_ref)

        dyb = dy_ref[...].astype(BF16)
        gr_ref[...] = _dot_nt(dyb, cre_ref[...])
        gi_ref[...] = -_dot_nt(dyb, cim_ref[...])
        dcr_ref[...] = _dot_tn(sr_ref[...].astype(BF16), dyb)
        dci_ref[...] = -_dot_tn(si_ref[...].astype(BF16), dyb)
        _scan_by_direction(gr_ref, gi_ref, are_ref[...], -aim_ref[...], d, adjoint=True)

        @pl.when(d == 0)
        def _():
            dar_ref[...], dai_ref[...] = _dlam(gr_ref, gi_ref, sr_ref, si_ref, reverse=False)

        @pl.when(d == 1)
        def _():
            dar_ref[...], dai_ref[...] = _dlam(gr_ref, gi_ref, sr_ref, si_ref, reverse=True)

        grb = gr_ref[...].astype(BF16)
        gib = gi_ref[...].astype(BF16)
        du_ref[...] += _dot_nt(grb, bre_ref[...]) + _dot_nt(gib, bim_ref[...])
        ub = u_ref[...].astype(BF16)
        dbr_ref[...] = _dot_tn(ub, grb)
        dbi_ref[...] = _dot_tn(ub, gib)

        @pl.when(d == 1)
        def _():
            _to_time_order(du_ref, du_nat)

    chan, _, state, bmat, cmat, avec = _s5_specs()
    return pl.pallas_call(
        body, grid=(S5_CHUNKS, 2), in_specs=[chan, chan, chan, state, state, bmat, bmat, avec, avec, cmat, cmat],
        out_specs=[chan, bmat, bmat, cmat, cmat, avec, avec],
        out_shape=[SDS((SEQ, S5_WIDTH), F32)] + [SDS((2, S5_CHUNKS, CH_W, ST_W), F32)] * 2
                  + [SDS((2, S5_CHUNKS, ST_W, CH_W), F32)] * 2 + [SDS((2, S5_CHUNKS, 1, ST_W), F32)] * 2,
        scratch_shapes=[pltpu.VMEM((SEQ, ST_W), F32), pltpu.VMEM((SEQ, ST_W), F32)] + [pltpu.VMEM((SEQ, CH_W), F32)] * 3,
        name="s5_scan_bwd", compiler_params=_params(("parallel", "arbitrary")))(dy, du_skip, u, sr, si, bre, bim, are, aim, cre, cim)


_GELU_K = math.sqrt(2.0 / math.pi)
_GELU_C = 0.044715


def _gelu(x):
    t = jnp.tanh(_GELU_K * (x + _GELU_C * x * x * x))
    return 0.5 * x * (1.0 + t), t


def _s5_glu_fwd(u, y2, dskip, wglu, bglu):
    def body(u_ref, y0_ref, y1_ref, d_ref, w_ref, b_ref, o_ref, yp_ref):
        ypre = u_ref[...] * d_ref[...] + y0_ref[...] + y1_ref[...]
        yp_ref[...] = ypre
        y, _ = _gelu(ypre)
        z = _dot(y.astype(BF16), w_ref[...]) + b_ref[...]
        o_ref[...] = y * jax.nn.sigmoid(z)

    row = _row_spec(S5_WIDTH)
    vec = _fix_spec((1, S5_WIDTH))
    dir0 = pl.BlockSpec((None, ROW_TILE, S5_WIDTH), lambda i: (0, i, 0))
    dir1 = pl.BlockSpec((None, ROW_TILE, S5_WIDTH), lambda i: (1, i, 0))
    return pl.pallas_call(
        body, grid=(N_ROW_TILES,), in_specs=[row, dir0, dir1, vec, _fix_spec((S5_WIDTH, S5_WIDTH)), vec],
        out_specs=[row, row], out_shape=[SDS((SEQ, S5_WIDTH), F32)] * 2, name="s5_glu_fwd",
        compiler_params=_params(("parallel",)))(u, y2, y2, dskip, wglu, bglu)


def _s5_glu_bwd(do, ypre, u, dskip, wglu, bglu):
    def body(do_ref, yp_ref, u_ref, d_ref, w_ref, b_ref, dyp_ref, du_ref, dw_ref, db_ref, dd_ref):
        i = pl.program_id(0)
        ypre = yp_ref[...]
        y, t = _gelu(ypre)
        yb = y.astype(BF16)
        sg = jax.nn.sigmoid(_dot(yb, w_ref[...]) + b_ref[...])
        dov = do_ref[...]
        dz = dov * y * sg * (1.0 - sg)
        dzb = dz.astype(BF16)
        dy = dov * sg + _dot_nt(dzb, w_ref[...])
        dgelu = 0.5 * (1.0 + t) + 0.5 * ypre * (1.0 - t * t) * _GELU_K * (1.0 + 3.0 * _GELU_C * ypre * ypre)
        dyp = dy * dgelu
        dyp_ref[...] = dyp
        uv = u_ref[...]
        du_ref[...] = dyp * d_ref[...]

        @pl.when(i == 0)
        def _():
            dw_ref[...] = jnp.zeros_like(dw_ref)
            db_ref[...] = jnp.zeros_like(db_ref)
            dd_ref[...] = jnp.zeros_like(dd_ref)

        dw_ref[...] += _dot_tn(yb, dzb)
        db_ref[...] += jnp.sum(dz, axis=0, keepdims=True)
        dd_ref[...] += jnp.sum(dyp * uv, axis=0, keepdims=True)

    row = _row_spec(S5_WIDTH)
    vec = _fix_spec((1, S5_WIDTH))
    mat = _fix_spec((S5_WIDTH, S5_WIDTH))
    return pl.pallas_call(
        body, grid=(N_ROW_TILES,), in_specs=[row, row, row, vec, mat, vec], out_specs=[row, row, mat, vec, vec],
        out_shape=[SDS((SEQ, S5_WIDTH), F32)] * 2 + [SDS((S5_WIDTH, S5_WIDTH), F32), SDS((1, S5_WIDTH), F32), SDS((1, S5_WIDTH), F32)],
        name="s5_glu_bwd", compiler_params=_params(("arbitrary",)))(do, ypre, u, dskip, wglu, bglu)


def _heads_side_by_side(o_ref):
    return jnp.concatenate([o_ref[h] for h in range(HEADS)], axis=-1)


def _mix_out_fwd(ona, os5, g_na, g_s5, wout):
    def body(a_ref, s_ref, ga_ref, gs_ref, w_ref, o_ref):
        av, sv = _heads_side_by_side(a_ref), s_ref[...]
        ca = (av * _rstd(av) * ga_ref[...]).astype(BF16)
        cs = (sv * _rstd(sv) * gs_ref[...]).astype(BF16)
        o_ref[...] = _dot(ca, w_ref[0:NA_WIDTH, :]) + _dot(cs, w_ref[NA_WIDTH:, :])

    row = _row_spec(NA_WIDTH)
    vec = _fix_spec((1, NA_WIDTH))
    heads = pl.BlockSpec((HEADS, ROW_TILE, HEAD_DIM), lambda i: (0, i, 0))
    return pl.pallas_call(
        body, grid=(N_ROW_TILES,), in_specs=[heads, row, vec, vec, _fix_spec((D_MODEL, D_MODEL))],
        out_specs=_row_spec(D_MODEL), out_shape=SDS((SEQ, D_MODEL), F32), name="mix_out_fwd",
        compiler_params=_params(("parallel",)))(ona, os5, g_na, g_s5, wout)


def _mix_out_bwd(dmix, ona, os5, g_na, g_s5, wout):
    def body(dm_ref, a_ref, s_ref, ga_ref, gs_ref, w_ref, da_ref, ds_ref, dw_ref, dga_ref, dgs_ref):
        i = pl.program_id(0)
        dm = dm_ref[...]
        av, sv = _heads_side_by_side(a_ref), s_ref[...]
        ra, rs = _rstd(av), _rstd(sv)
        ga, gs = ga_ref[...], gs_ref[...]
        ca = (av * ra * ga).astype(BF16)
        cs = (sv * rs * gs).astype(BF16)
        dca = _dot_nt(dm, w_ref[0:NA_WIDTH, :])
        dcs = _dot_nt(dm, w_ref[NA_WIDTH:, :])
        da, dga = _rms_bwd(av, ra, ga, dca)
        ds, dgs = _rms_bwd(sv, rs, gs, dcs)
        for h in range(HEADS):
            da_ref[h] = da[:, h * HEAD_DIM:(h + 1) * HEAD_DIM]
        ds_ref[...] = ds

        @pl.when(i == 0)
        def _():
            dw_ref[...] = jnp.zeros_like(dw_ref)
            dga_ref[...] = jnp.zeros_like(dga_ref)
            dgs_ref[...] = jnp.zeros_like(dgs_ref)

        dw_ref[0:NA_WIDTH, :] += _dot_tn(ca, dm)
        dw_ref[NA_WIDTH:, :] += _dot_tn(cs, dm)
        dga_ref[...] += jnp.sum(dga, axis=0, keepdims=True)
        dgs_ref[...] += jnp.sum(dgs, axis=0, keepdims=True)

    row = _row_spec(NA_WIDTH)
    vec = _fix_spec((1, NA_WIDTH))
    mat = _fix_spec((D_MODEL, D_MODEL))
    heads = pl.BlockSpec((HEADS, ROW_TILE, HEAD_DIM), lambda i: (0, i, 0))
    return pl.pallas_call(
        body, grid=(N_ROW_TILES,), in_specs=[_row_spec(D_MODEL), heads, row, vec, vec, mat],
        out_specs=[heads, row, mat, vec, vec],
        out_shape=[SDS((HEADS, SEQ, HEAD_DIM), F32), SDS((SEQ, NA_WIDTH), F32), SDS((D_MODEL, D_MODEL), F32),
                   SDS((1, NA_WIDTH), F32), SDS((1, NA_WIDTH), F32)],
        name="mix_out_bwd", compiler_params=_params(("arbitrary",)))(dmix, ona, os5, g_na, g_s5, wout)


def _me():
    x, y, c = lax.axis_index("x"), lax.axis_index("y"), lax.axis_index("c")
    return x, y, c, 4 * x + 2 * y + c


def _peer(k):
    x, y, c, _ = _me()
    px = 1 - x if (k >> 2) & 1 else x
    py = 1 - y if (k >> 1) & 1 else y
    pc = 1 - c if k & 1 else c
    return (px, py, pc), 4 * px + 2 * py + pc


ALL_PEERS = (1, 2, 3, 4, 5, 6, 7)
CHIP_PEERS = (2, 4, 6)
SIBLING = 1


def _slot8(pos):
    return 4 * pos[0] + 2 * pos[1] + pos[2]


def _slot4(pos):
    return 2 * pos[0] + pos[1]


def _exchange(arrays, gather, name, after=()):
    n, n_after = len(arrays), len(after)

    def body(*refs):
        ins, outs = refs[:n], refs[n + n_after:2 * n + n_after]
        token = refs[2 * n + n_after]
        send_sems, recv_sems, local_sems = refs[2 * n + n_after + 1:]
        token[...] = jnp.zeros_like(token)
        _, _, _, me = _me()
        started = []
        for a in range(n):
            src_mine = ins[a] if gather else ins[a].at[me]
            local = pltpu.make_async_copy(src_mine, outs[a].at[me], local_sems.at[a])
            local.start()
            started.append(local)
        sends = []
        for k in range(1, N_DEV):
            peer, peer_idx = _peer(k)
            for a in range(n):
                src = ins[a] if gather else ins[a].at[peer_idx]
                cp = pltpu.make_async_remote_copy(src_ref=src, dst_ref=outs[a].at[me], send_sem=send_sems.at[a, k - 1],
                                                  recv_sem=recv_sems.at[a, k - 1], device_id=peer, device_id_type=MESH)
                cp.start()
                sends.append(cp)
        for k in range(1, N_DEV):
            peer, peer_idx = _peer(k)
            for a in range(n):
                src = ins[a] if gather else ins[a].at[peer_idx]
                pltpu.make_async_remote_copy(src_ref=src, dst_ref=outs[a].at[peer_idx], send_sem=send_sems.at[a, k - 1],
                                             recv_sem=recv_sems.at[a, k - 1], device_id=peer, device_id_type=MESH).wait_recv()
        for cp in sends:
            cp.wait_send()
        for local in started:
            local.wait()

    hbm = pl.BlockSpec(memory_space=pltpu.HBM)
    out_shape = [SDS((N_DEV,) + tuple(a.shape), a.dtype) if gather else SDS(a.shape, a.dtype) for a in arrays]
    out = pl.pallas_call(
        body, in_specs=[hbm] * n + [pl.BlockSpec(memory_space=pl.ANY)] * n_after,
        out_specs=[hbm] * n + [pl.BlockSpec(memory_space=pltpu.VMEM)], out_shape=out_shape + [SDS((8, 128), F32)],
        scratch_shapes=[pltpu.SemaphoreType.DMA((n, N_DEV - 1)), pltpu.SemaphoreType.DMA((n, N_DEV - 1)),
                        pltpu.SemaphoreType.DMA((n,))],
        name=name)(*arrays, *after)
    return list(out[:n]), out[n]


_HBM = pl.BlockSpec(memory_space=pltpu.HBM)
_SEM = pl.BlockSpec(memory_space=pltpu.SEMAPHORE)
_EFFECT = pltpu.SideEffectType.DATAFLOW_SIDE_EFFECTING


def _land_shape(a, gather):
    return (N_DEV,) + tuple(a.shape) if gather else tuple(a.shape)


def _place_own(arrays, gather, name, slot=_slot8):
    n = len(arrays)
    me = slot(_me()[:3])

    def body(me_ref, *refs):
        for a in range(n):
            refs[n + a][...] = refs[a][...]

    def own_slot(a):
        zeros = (0,) * (a.ndim - (0 if gather else 1))
        return lambda i, me_ref: (me_ref[0],) + zeros

    def whole(a):
        return lambda i, me_ref: (0,) * a.ndim

    in_specs = [pl.BlockSpec(a.shape, whole(a)) if gather else pl.BlockSpec((None,) + a.shape[1:], own_slot(a)) for a in arrays]
    out_specs = [pl.BlockSpec((None,) + (a.shape if gather else a.shape[1:]), own_slot(a)) for a in arrays]
    return pl.pallas_call(
        body, grid_spec=pltpu.PrefetchScalarGridSpec(num_scalar_prefetch=1, grid=(1,), in_specs=in_specs, out_specs=out_specs),
        out_shape=[SDS(_land_shape(a, gather), a.dtype) for a in arrays], name=name,
        compiler_params=_params(("arbitrary",)))(me.reshape(1).astype(jnp.int32), *arrays)


def _exchange_start(arrays, lands, gather, name, peers=ALL_PEERS, slot=_slot8):
    n = len(arrays)

    def body(*refs):
        ins, lnd = refs[:n], refs[n:2 * n]
        send_sems, recv_sems = refs[2 * n], refs[2 * n + 1]
        token = refs[-1]
        me = slot(_me()[:3])
        for i, k in enumerate(peers):
            peer, _ = _peer(k)
            for a in range(n):
                src = ins[a] if gather else ins[a].at[slot(peer)]
                s = a * len(peers) + i
                pltpu.make_async_remote_copy(src_ref=src, dst_ref=lnd[a].at[me], send_sem=send_sems.at[s],
                                             recv_sem=recv_sems.at[s], device_id=peer, device_id_type=MESH).start()
        token[...] = jnp.zeros_like(token)

    sems = pltpu.SemaphoreType.DMA((n * len(peers),))
    out = pl.pallas_call(
        body, name=name, in_specs=[_HBM] * (2 * n),
        out_shape=(sems, sems) + tuple(pltpu.HBM(a.shape, a.dtype) for a in list(arrays) + list(lands)) + (SDS((8, 128), F32),),
        out_specs=(_SEM, _SEM) + (_HBM,) * (2 * n) + (pl.BlockSpec(memory_space=pltpu.VMEM),),
        input_output_aliases={i: 2 + i for i in range(2 * n)},
        compiler_params=pltpu.CompilerParams(has_side_effects=_EFFECT),
    )(*[pltpu.with_memory_space_constraint(a, pltpu.HBM) for a in list(arrays) + list(lands)])
    return out[0], out[1], list(out[2:2 + n]), list(out[2 + n:2 + 2 * n]), out[-1]


def _exchange_wait(send_sems, recv_sems, arrays, lands, after, gather, name, peers=ALL_PEERS, slot=_slot8):
    n = len(arrays)

    def body(*refs):
        ins, lnd = refs[:n], refs[n:2 * n]
        send_sems, recv_sems = refs[2 * n], refs[2 * n + 1]
        for i, k in enumerate(peers):
            peer, _ = _peer(k)
            for a in range(n):
                src = ins[a] if gather else ins[a].at[slot(peer)]
                s = a * len(peers) + i
                cp = pltpu.make_async_remote_copy(src_ref=src, dst_ref=lnd[a].at[slot(peer)], send_sem=send_sems.at[s],
                                                  recv_sem=recv_sems.at[s], device_id=peer, device_id_type=MESH)
                cp.wait_send()
                cp.wait_recv()

        refs[-1][...] = jnp.zeros_like(refs[-1])

    after = list(after) if isinstance(after, (list, tuple)) else [after]
    out = pl.pallas_call(
        body, name=name, in_specs=[_HBM] * (2 * n) + [_SEM, _SEM] + [pl.BlockSpec(memory_space=pl.ANY)] * len(after),
        out_shape=tuple(pltpu.HBM(a.shape, a.dtype) for a in list(arrays) + list(lands)) + (SDS((8, 128), F32),),
        out_specs=(_HBM,) * (2 * n) + (pl.BlockSpec(memory_space=pltpu.VMEM),), input_output_aliases={i: i for i in range(2 * n)},
        compiler_params=pltpu.CompilerParams(has_side_effects=_EFFECT),
    )(*arrays, *lands, send_sems, recv_sems, *after)
    return list(out[n:2 * n]), out[-1]


def _forward_sibling(lands, name):
    n = len(lands)

    def body(*refs):
        outs = refs[n:2 * n]
        send_sems, recv_sems = refs[2 * n:]
        x, y, c, _ = _me()
        sends = []
        for i, k in enumerate(CHIP_PEERS):
            peer, _ = _peer(k)
            for a in range(n):
                rows = outs[a].at[_slot8(peer)]
                cp = pltpu.make_async_remote_copy(src_ref=rows, dst_ref=rows, send_sem=send_sems.at[a, i], recv_sem=recv_sems.at[a, i],
                                                  device_id=(x, y, 1 - c), device_id_type=MESH)
                cp.start()
                sends.append(cp)
        for i, k in enumerate(CHIP_PEERS):
            (px, py, pc), _ = _peer(k)
            for a in range(n):
                rows = outs[a].at[_slot8((px, py, 1 - pc))]
                pltpu.make_async_remote_copy(src_ref=rows, dst_ref=rows, send_sem=send_sems.at[a, i], recv_sem=recv_sems.at[a, i],
                                             device_id=(x, y, 1 - c), device_id_type=MESH).wait_recv()
        for cp in sends:
            cp.wait_send()

    return pl.pallas_call(
        body, in_specs=[_HBM] * n, out_specs=[_HBM] * n, out_shape=[SDS(a.shape, a.dtype) for a in lands],
        input_output_aliases={i: i for i in range(n)},
        scratch_shapes=[pltpu.SemaphoreType.DMA((n, len(CHIP_PEERS))), pltpu.SemaphoreType.DMA((n, len(CHIP_PEERS)))],
        name=name)(*lands)


def _swap_sibling(arrays, name, after=()):
    n, n_after = len(arrays), len(after)
    chips = N_DEV // 2

    def body(*refs):
        ins, outs = refs[:n], refs[n + n_after:2 * n + n_after]
        send_sems, recv_sems = refs[2 * n + n_after:]
        x, y, c, _ = _me()
        sends = []
        for q in range(chips):
            for a in range(n):
                cp = pltpu.make_async_remote_copy(src_ref=ins[a].at[q, 1 - c], dst_ref=outs[a].at[q], send_sem=send_sems.at[a, q],
                                                  recv_sem=recv_sems.at[a, q], device_id=(x, y, 1 - c), device_id_type=MESH)
                cp.start()
                sends.append(cp)
        for cp in sends:
            cp.wait_recv()
        for cp in sends:
            cp.wait_send()

    return pl.pallas_call(
        body, in_specs=[_HBM] * n + [pl.BlockSpec(memory_space=pl.ANY)] * n_after, out_specs=[_HBM] * n,
        out_shape=[SDS((chips,) + a.shape[2:], a.dtype) for a in arrays],
        scratch_shapes=[pltpu.SemaphoreType.DMA((n, chips)), pltpu.SemaphoreType.DMA((n, chips))], name=name)(*arrays, *after)


def _sum_pairs(mine, theirs, name):
    chips, _, rows, cols = mine.shape
    c = lax.axis_index("c")

    def body(c_ref, a_ref, b_ref, o_ref):
        o_ref[...] = (a_ref[...].astype(F32) + b_ref[...].astype(F32)).astype(o_ref.dtype)

    return pl.pallas_call(
        body, grid_spec=pltpu.PrefetchScalarGridSpec(
            num_scalar_prefetch=1, grid=(chips,),
            in_specs=[pl.BlockSpec((None, None, rows, cols), lambda q, c_ref: (q, c_ref[0], 0, 0)),
                      pl.BlockSpec((None, rows, cols), lambda q, c_ref: (q, 0, 0))],
            out_specs=pl.BlockSpec((None, rows, cols), lambda q, c_ref: (q, 0, 0))),
        out_shape=SDS((chips, rows, cols), mine.dtype), name=name,
        compiler_params=_params(("parallel",)))(c.reshape(1).astype(jnp.int32), mine, theirs)


def _adamw_math(w, g, m, v):
    m = ADAM_B1 * m + (1.0 - ADAM_B1) * g
    v = ADAM_B2 * v + (1.0 - ADAM_B2) * (g * g)
    m_hat = m / (1.0 - ADAM_B1 ** ADAM_STEP)
    v_hat = v / (1.0 - ADAM_B2 ** ADAM_STEP)
    delta = -ADAM_LR * (m_hat / (jnp.sqrt(v_hat) + ADAM_EPS) + ADAM_WD * w)
    return delta, m, v


def _adamw(w, m, v, pieces, name):
    rows, cols = w.shape[-2:]
    lead = w.ndim - 2
    tile = rows
    for cand in (256, 176, 128, 64, 16):
        if rows > cand and rows % cand == 0:
            tile = cand
            break

    def body(w_ref, m_ref, v_ref, p_ref, g_ref, d_ref, mo_ref, vo_ref):
        g = _sum_pieces(p_ref)
        g_ref[...] = g
        d_ref[...], mo_ref[...], vo_ref[...] = _adamw_math(w_ref[...], g, m_ref[...], v_ref[...])

    blk = pl.BlockSpec((None,) * lead + (tile, cols), lambda i: (0,) * lead + (i, 0))
    return pl.pallas_call(
        body, grid=(rows // tile,), in_specs=[blk, blk, blk, pl.BlockSpec((pieces.shape[0], tile, cols), lambda i: (0, i, 0))],
        out_specs=[blk] * 4, out_shape=[SDS(w.shape, F32)] * 4, name=name,
        compiler_params=_params(("parallel",)))(w, m, v, pieces)


def _sum_pieces(p_ref):
    g = p_ref[0].astype(F32)
    for p in range(1, p_ref.shape[0]):
        g = g + p_ref[p].astype(F32)
    return g


def _adamw_s5_mat(w, m, v, g, name):
    _, ndir, groups, b, c = w.shape
    per_dir = groups // 8

    def body(w_ref, m_ref, v_ref, g_ref, d_ref, mo_ref, vo_ref):
        d_ref[...], mo_ref[...], vo_ref[...] = _adamw_math(w_ref[...], g_ref[...], m_ref[...], v_ref[...])

    blk = pl.BlockSpec((None, None, 8, b, c), lambda i: (0, i // per_dir, i % per_dir, 0, 0))
    return pl.pallas_call(
        body, grid=(ndir * per_dir,), in_specs=[blk] * 4, out_specs=[blk] * 3, out_shape=[SDS(w.shape, F32)] * 3, name=name,
        compiler_params=_params(("parallel",)))(w, m, v, g)


VEC_ROWS = ['ffn1_pre_g', 'ffn1_post_g', 'mix_pre_g', 'mix_post_g', 'ffn2_pre_g', 'ffn2_post_g', 'final_g',
            ('na_out_g', 's5_out_g'), ('s5_d', 's5_b_glu')]
VEC_NAMES = [n for row in VEC_ROWS for n in ((row,) if isinstance(row, str) else row)]
VEC_PACK_ROWS = 16
LOSS_ROW = len(VEC_ROWS)


def _pack_vectors(grads, loss8):
    def body(*refs):
        o_ref = refs[-1]
        o_ref[...] = jnp.zeros_like(o_ref)
        o_ref[LOSS_ROW:LOSS_ROW + 1, 0:128] = refs[-2][0:1, :]
        k = 0
        for i, row in enumerate(VEC_ROWS):
            if isinstance(row, str):
                o_ref[i:i + 1, :] = refs[k][...]
                k += 1
            else:
                o_ref[i:i + 1, 0:NA_WIDTH] = refs[k][...]
                o_ref[i:i + 1, NA_WIDTH:] = refs[k + 1][...]
                k += 2

    return pl.pallas_call(body, out_shape=SDS((VEC_PACK_ROWS, D_MODEL), F32), name="pack_vectors",
                          compiler_params=_params())(*[grads[n] for n in VEC_NAMES], loss8)


def _sum8(pieces, name):
    def body(p_ref, o_ref):
        o_ref[...] = _sum_pieces(p_ref)

    return pl.pallas_call(body, out_shape=SDS(pieces.shape[1:], F32), name=name, compiler_params=_params())(pieces)


def _adamw_small(packed8, vec_wmv, others):
    n_vec, n_oth = len(VEC_NAMES), len(others)

    def body(*refs):
        p_ref = refs[0]
        ins = refs[1:1 + 3 * n_vec + 4 * n_oth]
        outs = refs[1 + 3 * n_vec + 4 * n_oth:]
        gsum = _sum_pieces(p_ref)
        outs[-1][...] = gsum[LOSS_ROW:LOSS_ROW + 1, 0:128]
        k = 0
        for i, row in enumerate(VEC_ROWS):
            parts = [(row, gsum[i:i + 1, :])] if isinstance(row, str) else \
                [(row[0], gsum[i:i + 1, 0:NA_WIDTH]), (row[1], gsum[i:i + 1, NA_WIDTH:])]
            for _, g in parts:
                w_ref, m_ref, v_ref = ins[3 * k:3 * k + 3]
                outs[4 * k][...] = g
                outs[4 * k + 1][...], outs[4 * k + 2][...], outs[4 * k + 3][...] = _adamw_math(w_ref[...], g, m_ref[...], v_ref[...])
                k += 1
        for j in range(n_oth):
            w_ref, m_ref, v_ref, g_ref = ins[3 * n_vec + 4 * j:3 * n_vec + 4 * j + 4]
            g = _sum_pieces(g_ref)
            g = g[tuple(slice(0, s) for s in w_ref.shape[1:])].reshape(w_ref.shape)
            o = outs[4 * (n_vec + j):4 * (n_vec + j) + 4]
            o[0][...] = g
            o[1][...], o[2][...], o[3][...] = _adamw_math(w_ref[...], g, m_ref[...], v_ref[...])

    args, out_shape = [packed8], []
    for w, m, v in vec_wmv:
        args += [w, m, v]
        out_shape += [SDS(w.shape, F32)] * 4
    for w, m, v, g in others:
        args += [w, m, v, g]
        out_shape += [SDS(w.shape, F32)] * 4
    out_shape += [SDS((1, 128), F32)]
    return pl.pallas_call(body, out_shape=out_shape, name="adamw_small", compiler_params=_params())(*args)


def _block_diag(x):
    eye = np.eye(8, dtype=bool)[None, None, :, None, :, None]
    full = jnp.where(eye, x[:, :, :, :, None, :], 0.0)
    return full.reshape(2, S5_CHUNKS, 8 * x.shape[3], 8 * x.shape[4])


def _diag_blocks(x, r, c):
    x6 = x.reshape(2, S5_CHUNKS, 8, r, 8, c)
    return jnp.stack([x6[:, :, g, :, g, :] for g in range(8)], axis=2)


STORED_SWAPPED = {"ffn1_w_gate": (1, 2), "ffn1_w_up": (1, 2), "ffn2_w_gate": (1, 2), "ffn2_w_up": (1, 2),
                  "s5_b_re": (3, 4), "s5_b_im": (3, 4)}


def _stored(name, x):
    return jnp.swapaxes(x, *STORED_SWAPPED[name]) if name in STORED_SWAPPED else x


def _dep(x, token):
    return x if token is None else x + token


def _local_step(x, target, get_w, small, emit):
    bias = _rpb_expand(small["na_rpb"][0])
    lr = small["s5_lam_re"].reshape(64, S5_STATE)
    li = small["s5_lam_im"].reshape(64, S5_STATE)
    logdt = small["s5_log_dt"].reshape(64, 1)
    b_t = [_stored(n, small[n]).reshape(64, S5_GROUP, S5_STATE) for n in ("s5_b_re", "s5_b_im")]
    lbr, lbi, bbr, bbi = _s5_prep(lr, li, logdt, b_t[0], b_t[1])
    are = lbr.reshape(2, S5_CHUNKS, 1, ST_W)
    aim = lbi.reshape(2, S5_CHUNKS, 1, ST_W)
    bre = _block_diag(bbr.reshape(2, S5_CHUNKS, 8, S5_GROUP, S5_STATE)).astype(BF16)
    bim = _block_diag(bbi.reshape(2, S5_CHUNKS, 8, S5_GROUP, S5_STATE)).astype(BF16)
    c_t = [small[n].reshape(2, S5_CHUNKS, 8, S5_GROUP, S5_STATE).transpose(0, 1, 2, 4, 3) for n in ("s5_c_re", "s5_c_im")]
    cre = _block_diag(c_t[0]).astype(BF16)
    cim = _block_diag(c_t[1]).astype(BF16)
    tgt = jnp.concatenate([jnp.zeros((N_META, D_MODEL), F32), target], axis=0)

    h0, a1 = _embed_prenorm(get_w("meta", None)["meta_tokens"], x, small["ffn1_pre_g"])
    wts = dict(get_w("ffn1", [bias, are, aim, bre, bim, cre, cim, tgt, a1]))
    gate1, up1, f1 = _ffn_fwd(a1, wts["ffn1_w_gate"], wts["ffn1_w_up"], wts["ffn1_w_down"], "ffn1_fwd",
                              after=wts.get("tokens", ()))
    h1, a2 = _post_pre(f1, h0, small["ffn1_post_g"], small["mix_pre_g"], 0.5, "post_pre1")
    wts.update(get_w("w_in", a2))
    qkv = _proj_heads(a2, wts["w_in"])
    u = _proj_u(a2, wts["w_in"])
    ona = _na_fwd(qkv, bias)
    sr, si, y2 = _s5_scan_fwd(u, bre, bim, are, aim, cre, cim)
    wts.update(get_w("mix", y2))
    os5, ypre = _s5_glu_fwd(u, y2, small["s5_d"], wts["s5_w_glu"], small["s5_b_glu"])

    mix = _mix_out_fwd(ona, os5, small["na_out_g"], small["s5_out_g"], wts["w_out"])
    h2, a3 = _post_pre(mix, h1, small["mix_post_g"], small["ffn2_pre_g"], 1.0, "post_pre2")
    wts.update(get_w("ffn2", a3))
    gate2, up2, f2 = _ffn_fwd(a3, wts["ffn2_w_gate"], wts["ffn2_w_up"], wts["ffn2_w_down"], "ffn2_fwd")
    loss8, dh3, df2, g_final, g_ffn2_post = _final_loss(f2, h2, small["ffn2_post_g"], small["final_g"], tgt)

    da3, dwg2, dwu2, dwd2 = _ffn_bwd(df2, a3, gate2, up2, wts["ffn2_w_gate"], wts["ffn2_w_up"], wts["ffn2_w_down"], "ffn2_bwd")
    tok = emit("ffn2", {"ffn2_w_gate": dwg2, "ffn2_w_up": dwu2, "ffn2_w_down": dwd2})
    dh2, dmix, g_ffn2_pre, g_mix_post = _bwd_pre_post(da3, h2, _dep(small["ffn2_pre_g"], tok), dh3, mix, small["mix_post_g"], 1.0,
                                                      "bwd_pre_post2")
    dona, dos5, dwout, g_na_out, g_s5_out = _mix_out_bwd(dmix, ona, os5, small["na_out_g"], small["s5_out_g"], wts["w_out"])

    dypre, du_skip, dwglu, g_b_glu, g_s5_d = _s5_glu_bwd(dos5, ypre, u, small["s5_d"], wts["s5_w_glu"], small["s5_b_glu"])
    tok = emit("mix", {"s5_w_glu": dwglu.reshape(N_DEV, S5_WIDTH // N_DEV, S5_WIDTH).astype(BF16),
                       "w_out": dwout.reshape(N_DEV, D_MODEL // N_DEV, D_MODEL).astype(BF16)})
    du, dbr, dbi, dcr, dci, dar, dai = _s5_scan_bwd(dypre, du_skip, u, sr, si, bre, bim, _dep(are, tok), aim, cre, cim)
    dbbr = _diag_blocks(dbr, S5_GROUP, S5_STATE).reshape(64, S5_GROUP, S5_STATE)
    dbbi = _diag_blocks(dbi, S5_GROUP, S5_STATE).reshape(64, S5_GROUP, S5_STATE)
    g_lr, g_li, g_dt, g_br, g_bi = _s5_prep_bwd(lr, li, logdt, b_t[0], b_t[1], dar.reshape(64, S5_STATE),
                                                dai.reshape(64, S5_STATE), dbbr, dbbi)
    g_c = [_diag_blocks(d, S5_STATE, S5_GROUP).transpose(0, 1, 2, 4, 3).reshape(2 * S5_GROUPS, S5_GROUP, S5_STATE)
           for d in (dcr, dci)]

    dq, dk, dv, dbias = _na_bwd(qkv, bias, dona)
    g_rpb = _rpb_reduce(dbias)
    dense = jnp.stack([g.reshape(2 * S5_GROUPS, S5_STATE * S5_GROUP) for g in (g_br, g_bi, *g_c)])
    tok = emit("small", {"dense": dense, "na_rpb": g_rpb,
                         "s5_lam_re": g_lr.reshape(2, S5_GROUPS, S5_STATE), "s5_lam_im": g_li.reshape(2, S5_GROUPS, S5_STATE),
                         "s5_log_dt": g_dt.reshape(2, S5_GROUPS)})
    da2, dwin = _proj_bwd(dq, dk, dv, du, a2, wts["w_in"])
    tok2 = emit("w_in", {"w_in": dwin})
    tok = tok if tok2 is None else tok + tok2
    dh1, df1, g_mix_pre, g_ffn1_post = _bwd_pre_post(da2, h1, _dep(small["mix_pre_g"], tok), dh2, f1, small["ffn1_post_g"], 0.5,
                                                     "bwd_pre_post1")
    da1, dwg1, dwu1, dwd1 = _ffn_bwd(df1, a1, gate1, up1, wts["ffn1_w_gate"], wts["ffn1_w_up"], wts["ffn1_w_down"], "ffn1_bwd")
    grad_x, grad_meta, g_ffn1_pre = _bwd_embed(da1, h0, small["ffn1_pre_g"], dh1)
    vec_g = {
        "ffn1_pre_g": g_ffn1_pre, "ffn1_post_g": g_ffn1_post, "mix_pre_g": g_mix_pre, "s5_d": g_s5_d, "s5_b_glu": g_b_glu,
        "na_out_g": g_na_out, "s5_out_g": g_s5_out, "mix_post_g": g_mix_post,
        "ffn2_pre_g": g_ffn2_pre, "ffn2_post_g": g_ffn2_post, "final_g": g_final,
    }
    emit("vec", {"packed": _pack_vectors(vec_g, loss8), "meta_tokens": grad_meta})
    emit("ffn1", {"ffn1_w_gate": dwg1, "ffn1_w_up": dwu1, "ffn1_w_down": dwd1})
    return grad_x


WEIGHT_NAMES = ['meta_tokens', 'ffn1_pre_g', 'ffn1_post_g', 'ffn1_w_gate', 'ffn1_w_up', 'ffn1_w_down', 'mix_pre_g', 'w_in',
                'na_rpb', 's5_lam_re', 's5_lam_im', 's5_log_dt', 's5_b_re', 's5_b_im', 's5_c_re', 's5_c_im', 's5_d',
                's5_w_glu', 's5_b_glu', 'na_out_g', 's5_out_g', 'w_out', 'mix_post_g', 'ffn2_pre_g', 'ffn2_post_g',
                'ffn2_w_gate', 'ffn2_w_up', 'ffn2_w_down', 'final_g']
BIG_NAMES = ['ffn1_w_gate', 'ffn1_w_up', 'ffn1_w_down', 'w_in', 's5_w_glu', 'w_out', 'ffn2_w_gate', 'ffn2_w_up', 'ffn2_w_down']
SMALL_NAMES = [n for n in WEIGHT_NAMES if n not in BIG_NAMES and n != 'meta_tokens']
WHOLE_NAMES = ['na_rpb', 's5_lam_re', 's5_lam_im', 's5_log_dt']
LEAD_NAMES = ['s5_b_re', 's5_b_im', 's5_c_re', 's5_c_im']


def kernel(x, meta_tokens, ffn1_pre_g, ffn1_post_g, ffn1_w_gate, ffn1_w_up, ffn1_w_down, mix_pre_g, w_in, na_rpb, s5_lam_re, s5_lam_im, s5_log_dt, s5_b_re, s5_b_im, s5_c_re, s5_c_im, s5_d, s5_w_glu, s5_b_glu, na_out_g, s5_out_g, w_out, mix_post_g, ffn2_pre_g, ffn2_post_g, ffn2_w_gate, ffn2_w_up, ffn2_w_down, final_g, loss_target, m_meta_tokens, m_ffn1_pre_g, m_ffn1_post_g, m_ffn1_w_gate, m_ffn1_w_up, m_ffn1_w_down, m_mix_pre_g, m_w_in, m_na_rpb, m_s5_lam_re, m_s5_lam_im, m_s5_log_dt, m_s5_b_re, m_s5_b_im, m_s5_c_re, m_s5_c_im, m_s5_d, m_s5_w_glu, m_s5_b_glu, m_na_out_g, m_s5_out_g, m_w_out, m_mix_post_g, m_ffn2_pre_g, m_ffn2_post_g, m_ffn2_w_gate, m_ffn2_w_up, m_ffn2_w_down, m_final_g, v_meta_tokens, v_ffn1_pre_g, v_ffn1_post_g, v_ffn1_w_gate, v_ffn1_w_up, v_ffn1_w_down, v_mix_pre_g, v_w_in, v_na_rpb, v_s5_lam_re, v_s5_lam_im, v_s5_log_dt, v_s5_b_re, v_s5_b_im, v_s5_c_re, v_s5_c_im, v_s5_d, v_s5_w_glu, v_s5_b_glu, v_na_out_g, v_s5_out_g, v_w_out, v_mix_post_g, v_ffn2_pre_g, v_ffn2_post_g, v_ffn2_w_gate, v_ffn2_w_up, v_ffn2_w_down, v_final_g):
    args = dict(locals())
    w = {n: args[n] for n in WEIGHT_NAMES}
    m = {n: args["m_" + n] for n in WEIGHT_NAMES}
    v = {n: args["v_" + n] for n in WEIGHT_NAMES}

    small = {n: w[n] for n in SMALL_NAMES}

    pending = {}

    def start(group, names, arrays, gather, peers=ALL_PEERS, slot=_slot8):
        lands = _place_own(arrays, gather, "own_" + group, slot)
        send_sems, recv_sems, arrays, lands, token = _exchange_start(arrays, lands, gather, "start_" + group, peers, slot)
        pending[group] = (names, send_sems, recv_sems, arrays, lands, gather, peers, slot)
        return token

    def finish(group, after):
        names, send_sems, recv_sems, arrays, lands, gather, peers, slot = pending.pop(group)
        lands, token = _exchange_wait(send_sems, recv_sems, arrays, lands, after, gather, "wait_" + group, peers, slot)
        return dict(zip(names, lands)), token

    first = ["ffn1_w_gate", "ffn1_w_up", "ffn1_w_down"]
    def shard(n, token=None):
        return _dep(_stored(n, w[n])[0], None if token is None else token[0, 0]).astype(BF16)

    ffn_names = ("ffn1_w_gate", "ffn1_w_up", "ffn1_w_down", "ffn2_w_gate", "ffn2_w_up", "ffn2_w_down")
    later_groups = (("w_in", ["w_in"]), ("mix", ["s5_w_glu", "w_out"]), ("ffn2", ["ffn2_w_gate", "ffn2_w_up", "ffn2_w_down"]))
    (meta_full,), token0 = _exchange([w["meta_tokens"]], True, "gather_meta")
    token1 = start("ffn1", first, [shard(n, token0) for n in first], True, (SIBLING,) + CHIP_PEERS)
    meta_full = _dep(meta_full.transpose(1, 0, 2).reshape(N_META, D_MODEL), token1[0, 0])
    later_shards = {n: shard(n, token1) for _, names in later_groups for n in names}
    for n in ("na_rpb", "s5_lam_re"):
        small[n] = _dep(small[n], token1[0, 0])

    def get_w(group, after):
        if group == "meta":
            return {"meta_tokens": meta_full}
        if group == "ffn1":
            after = list(after) + list(later_shards.values())
        got, token = finish(group, after)
        if group == "ffn1":
            got = dict(zip(got, _forward_sibling(list(got.values()), "forward_ffn1")))
            got["tokens"] = [start(g, names + ["order"], [later_shards[n] for n in names] + [token], True) for g, names in later_groups]
        if group == "mix":
            got = {"s5_w_glu": got["s5_w_glu"].reshape(S5_WIDTH, S5_WIDTH), "w_out": got["w_out"].reshape(D_MODEL, D_MODEL)}
        return {n: (a.reshape(D_FF, D_MODEL) if n in ffn_names else a) for n, a in got.items()}

    tokens = {}

    def emit(group, grads):
        grads = {n: (g.reshape(N_DEV, FF_SHARD, D_MODEL) if n in ffn_names else g) for n, g in grads.items()}
        if group == "ffn1":
            mine = [g.reshape((N_DEV // 2, 2) + g.shape[1:]) for g in grads.values()]
            theirs = _swap_sibling(mine, "swap_g_ffn1", after=[tokens["vec"]])
            sums = [_sum_pairs(a, b, "pair_sum_" + n) for n, a, b in zip(grads, mine, theirs)]
            tokens[group] = start("g_ffn1", list(grads), sums, False, CHIP_PEERS, _slot4)
        else:
            tokens[group] = start("g_" + group, list(grads), list(grads.values()), group in ("small", "vec"))
        return tokens[group][0, 0]

    grad_x = _local_step(x[0], loss_target[0], get_w, small, emit)
    res = {}

    def update_shard(n, pieces):
        outs = _adamw(_stored(n, w[n]), _stored(n, m[n]), _stored(n, v[n]), pieces, "adamw_" + n)
        res[n] = [_stored(n, o) for o in outs]

    late = [grad_x, tokens["ffn1"]]
    for group in ("g_ffn2", "g_mix", "g_w_in"):
        for n, pieces in finish(group, late)[0].items():
            update_shard(n, pieces)
    g8 = finish("g_small", late)[0]
    dense = _sum8(g8["dense"], "sum_dense")
    for i, n in enumerate(LEAD_NAMES):
        g = dense[i].reshape(_stored(n, w[n]).shape)
        upd = _adamw_s5_mat(_stored(n, w[n]), _stored(n, m[n]), _stored(n, v[n]), g, "adamw_" + n)
        res[n] = [_stored(n, o) for o in [g] + list(upd)]

    done = [res[n][1] for n in ("ffn2_w_gate", "ffn2_w_up", "ffn2_w_down", "w_in", "w_out", "s5_w_glu") + tuple(LEAD_NAMES)]
    got = finish("g_vec", done)[0]
    packed8, gmeta8 = got["packed"], got["meta_tokens"]
    for n, pieces in finish("g_ffn1", packed8)[0].items():
        update_shard(n, pieces)
    _, _, _, me = _me()
    update_shard("meta_tokens", lax.dynamic_slice_in_dim(gmeta8, me * (D_MODEL // N_DEV), D_MODEL // N_DEV, axis=2))

    outs = _adamw_small(packed8, [(w[n], m[n], v[n]) for n in VEC_NAMES], [(w[n], m[n], v[n], g8[n]) for n in WHOLE_NAMES])
    for i, n in enumerate(VEC_NAMES + WHOLE_NAMES):
        res[n] = list(outs[4 * i:4 * i + 4])

    out = [outs[-1][0, 0], grad_x[None]]
    for kind in range(4):
        out += [res[n][kind] for n in WEIGHT_NAMES]
    return tuple(out)
```

```python
import functools
import math

import numpy as np
import jax
import jax.numpy as jnp
from jax import lax
from jax.experimental import pallas as pl
from jax.experimental.pallas import tpu as pltpu

F32 = jnp.float32
BF16 = jnp.bfloat16
SDS = jax.ShapeDtypeStruct

D_MODEL = 1024
N_TOK = 2048
N_META = 16
SEQ = N_TOK + N_META
ROW_TILE = 688
N_ROW_TILES = SEQ // ROW_TILE
N_DEV = 8
D_FF = 2816
FF_SHARD = D_FF // N_DEV
FF_TILE = 256
IN_SHARD = 256
NA_WIDTH = 512
S5_WIDTH = 512
HEADS = 8
HEAD_DIM = 64
GRID_W = 64
GRID_ROWS = N_TOK // GRID_W
KH = 8
KW = 16
NA_RB = 4
NA_KR = KH + NA_RB - 1
NA_BLOCKS = GRID_ROWS // NA_RB
NA_QB = NA_RB * GRID_W
NA_KB = NA_KR * GRID_W
NA_TYPES = 3
S5_GROUPS = 32
S5_GROUP = 16
S5_STATE = 64
S5_CHUNKS = 4
CH_W = S5_WIDTH // S5_CHUNKS
ST_W = S5_GROUPS * S5_STATE // S5_CHUNKS
SCAN_BLOCKS = 8
SCAN_T = SEQ // SCAN_BLOCKS
RMS_EPS = 1e-6
NEG_INF = -1e30
ATT_SCALE = HEAD_DIM ** -0.5
ADAM_LR, ADAM_B1, ADAM_B2, ADAM_EPS, ADAM_WD, ADAM_STEP = 0.001, 0.9, 0.999, 1e-08, 0.01, 10
VMEM_LIMIT = 56 * 1024 * 1024
MESH = pl.DeviceIdType.MESH
AXES = ("x", "y", "c")


def _params(sem=None):
    return pltpu.CompilerParams(dimension_semantics=sem, vmem_limit_bytes=VMEM_LIMIT)


def _dot(a, b):
    return jnp.dot(a, b, preferred_element_type=F32)


def _dot_nt(a, b):
    return lax.dot_general(a, b, (((1,), (1,)), ((), ())), preferred_element_type=F32)


def _dot_tn(a, b):
    return lax.dot_general(a, b, (((0,), (0,)), ((), ())), preferred_element_type=F32)


def _rstd(x):
    return lax.rsqrt(jnp.mean(x * x, axis=-1, keepdims=True) + RMS_EPS)


def _rms_bwd(x, r, g, dy):
    dyg = dy * g
    xr = x * r
    dx = r * (dyg - xr * jnp.mean(dyg * xr, axis=-1, keepdims=True))
    return dx, dy * xr


def _rows(i, size=ROW_TILE):
    return pl.ds(pl.multiple_of(i * size, 16), size)


def _row_spec(width):
    return pl.BlockSpec((ROW_TILE, width), lambda i: (i, 0))


def _fix_spec(shape):
    return pl.BlockSpec(shape, lambda i: (0,) * len(shape))


def _split3(x):
    hi = x.astype(BF16)
    r1 = x - hi.astype(F32)
    mid = r1.astype(BF16)
    lo = (r1 - mid.astype(F32)).astype(BF16)
    return hi, mid, lo


def _embed_prenorm(meta, x, g):
    def body(m_ref, x_ref, g_ref, h_ref, a_ref):
        h_ref[0:N_META, :] = m_ref[...]
        h_ref[N_META:, :] = x_ref[...]
        for i in range(N_ROW_TILES):
            rows = slice(i * ROW_TILE, (i + 1) * ROW_TILE)
            hv = h_ref[rows, :]
            a_ref[rows, :] = (hv * _rstd(hv) * g_ref[...]).astype(BF16)

    return pl.pallas_call(
        body, out_shape=[SDS((SEQ, D_MODEL), F32), SDS((SEQ, D_MODEL), BF16)], name="embed_prenorm",
        compiler_params=_params())(meta, x, g)


def _post_pre(f, hres, g_post, g_next, scale, name):
    def body(f_ref, h_ref, gp_ref, gn_ref, ho_ref, a_ref):
        fv = f_ref[...]
        h = h_ref[...] + scale * (fv * _rstd(fv) * gp_ref[...])
        ho_ref[...] = h
        a_ref[...] = (h * _rstd(h) * gn_ref[...]).astype(BF16)

    return pl.pallas_call(
        body, grid=(N_ROW_TILES,),
        in_specs=[_row_spec(D_MODEL), _row_spec(D_MODEL), _fix_spec((1, D_MODEL)), _fix_spec((1, D_MODEL))],
        out_specs=[_row_spec(D_MODEL), _row_spec(D_MODEL)],
        out_shape=[SDS((SEQ, D_MODEL), F32), SDS((SEQ, D_MODEL), BF16)], name=name,
        compiler_params=_params(("parallel",)))(f, hres, g_post, g_next)


def _final_loss(f2, h2, g_post, g_final, target):
    def body(f_ref, h_ref, gp_ref, gf_ref, t_ref, loss_ref, dh_ref, df_ref, dgf_ref, dgp_ref):
        i = pl.program_id(0)
        fv = f_ref[...]
        r1 = _rstd(fv)
        gp = gp_ref[...]
        h3 = h_ref[...] + 0.5 * (fv * r1 * gp)
        r2 = _rstd(h3)
        gf = gf_ref[...]
        y = h3 * r2 * gf
        row = lax.broadcasted_iota(jnp.int32, (ROW_TILE, 1), 0) + i * ROW_TILE
        err = jnp.where(row >= N_META, y - t_ref[...], 0.0)
        part = 0.5 * jnp.sum(jnp.mean(err * err, axis=-1, keepdims=True))
        dy = err * (1.0 / D_MODEL)
        dh3, dgf = _rms_bwd(h3, r2, gf, dy)
        dh_ref[...] = dh3
        df, dgp = _rms_bwd(fv, r1, gp, 0.5 * dh3)
        df_ref[...] = df.astype(BF16)

        @pl.when(i == 0)
        def _():
            loss_ref[...] = jnp.zeros_like(loss_ref)
            dgf_ref[...] = jnp.zeros_like(dgf_ref)
            dgp_ref[...] = jnp.zeros_like(dgp_ref)

        loss_ref[...] += part
        dgf_ref[...] += jnp.sum(dgf, axis=0, keepdims=True)
        dgp_ref[...] += jnp.sum(dgp, axis=0, keepdims=True)

    gain = _fix_spec((1, D_MODEL))
    return pl.pallas_call(
        body, grid=(N_ROW_TILES,),
        in_specs=[_row_spec(D_MODEL), _row_spec(D_MODEL), gain, gain, _row_spec(D_MODEL)],
        out_specs=[_fix_spec((8, 128)), _row_spec(D_MODEL), _row_spec(D_MODEL), gain, gain],
        out_shape=[SDS((8, 128), F32), SDS((SEQ, D_MODEL), F32), SDS((SEQ, D_MODEL), BF16),
                   SDS((1, D_MODEL), F32), SDS((1, D_MODEL), F32)],
        name="final_loss", compiler_params=_params(("arbitrary",)))(f2, h2, g_post, g_final, target)


def _bwd_pre_post(da, h, g_pre, dh_res, fprev, g_post, scale, name):
    def body(da_ref, h_ref, gpre_ref, dhr_ref, f_ref, gpost_ref, dh_ref, df_ref, dgpre_ref, dgpost_ref):
        i = pl.program_id(0)
        hv = h_ref[...]
        dxa, dgpre = _rms_bwd(hv, _rstd(hv), gpre_ref[...], da_ref[...])
        dh = dhr_ref[...] + dxa
        dh_ref[...] = dh
        fv = f_ref[...]
        df, dgpost = _rms_bwd(fv, _rstd(fv), gpost_ref[...], scale * dh)
        df_ref[...] = df.astype(BF16)

        @pl.when(i == 0)
        def _():
            dgpre_ref[...] = jnp.zeros_like(dgpre_ref)
            dgpost_ref[...] = jnp.zeros_like(dgpost_ref)

        dgpre_ref[...] += jnp.sum(dgpre, axis=0, keepdims=True)
        dgpost_ref[...] += jnp.sum(dgpost, axis=0, keepdims=True)

    gain = _fix_spec((1, D_MODEL))
    row = _row_spec(D_MODEL)
    return pl.pallas_call(
        body, grid=(N_ROW_TILES,), in_specs=[row, row, gain, row, row, gain],
        out_specs=[row, row, gain, gain],
        out_shape=[SDS((SEQ, D_MODEL), F32), SDS((SEQ, D_MODEL), BF16), SDS((1, D_MODEL), F32), SDS((1, D_MODEL), F32)],
        name=name, compiler_params=_params(("arbitrary",)))(da, h, g_pre, dh_res, fprev, g_post)


def _bwd_embed(da, h, g_pre, dh_res):
    def body(da_ref, h_ref, gpre_ref, dhr_ref, gx_ref, gm_ref, dgpre_ref):
        total = jnp.zeros((1, D_MODEL), F32)
        for i in range(N_ROW_TILES):
            rows = slice(i * ROW_TILE, (i + 1) * ROW_TILE)
            hv = h_ref[rows, :]
            dxa, dgpre = _rms_bwd(hv, _rstd(hv), gpre_ref[...], da_ref[rows, :])
            dh = dhr_ref[rows, :] + dxa
            total = total + jnp.sum(dgpre, axis=0, keepdims=True)
            if i == 0:
                gm_ref[...] = dh[0:N_META, :]
                gx_ref[0:ROW_TILE - N_META, :] = dh[N_META:, :]
            else:
                gx_ref[i * ROW_TILE - N_META:(i + 1) * ROW_TILE - N_META, :] = dh
        dgpre_ref[...] = total

    return pl.pallas_call(
        body, out_shape=[SDS((N_TOK, D_MODEL), F32), SDS((N_META, D_MODEL), F32), SDS((1, D_MODEL), F32)],
        name="bwd_embed", compiler_params=_params())(da, h, g_pre, dh_res)


def _ffn_fwd(a, wg, wu, wd, name, after=()):
    def body(a_ref, wg_ref, wu_ref, wd_ref, *rest):
        gate_ref, up_ref, f_ref = rest[len(after):]
        j = pl.program_id(0)

        def tile(i, carry):
            rows = _rows(i)
            at = a_ref[rows, :]
            gate = _dot_nt(at, wg_ref[...])
            up = _dot_nt(at, wu_ref[...])
            gate_ref[rows, :] = gate.astype(BF16)
            up_ref[rows, :] = up.astype(BF16)
            act = (gate * jax.nn.sigmoid(gate) * up).astype(BF16)
            contrib = _dot(act, wd_ref[...])

            @pl.when(j == 0)
            def _():
                f_ref[rows, :] = contrib

            @pl.when(j != 0)
            def _():
                f_ref[rows, :] += contrib

            return carry

        lax.fori_loop(0, N_ROW_TILES, tile, 0)

    wtile = pl.BlockSpec((FF_TILE, D_MODEL), lambda j: (j, 0))
    hid = pl.BlockSpec((SEQ, FF_TILE), lambda j: (0, j))
    full = pl.BlockSpec((SEQ, D_MODEL), lambda j: (0, 0))
    return pl.pallas_call(
        body, grid=(D_FF // FF_TILE,), in_specs=[full, wtile, wtile, wtile] + [pl.BlockSpec(memory_space=pl.ANY)] * len(after),
        out_specs=[hid, hid, full],
        out_shape=[SDS((SEQ, D_FF), BF16), SDS((SEQ, D_FF), BF16), SDS((SEQ, D_MODEL), F32)],
        name=name, compiler_params=_params(("arbitrary",)))(a, wg, wu, wd, *after)


def _ffn_bwd(df, a, gate, up, wg, wu, wd, name):
    def body(df_ref, a_ref, gate_ref, up_ref, wg_ref, wu_ref, wd_ref, da_ref, dwg_ref, dwu_ref, dwd_ref,
             acc_g, acc_u, acc_d):
        j = pl.program_id(0)

        def tile(i, carry):
            rows = _rows(i)
            dft = df_ref[rows, :]
            at = a_ref[rows, :]
            gate = gate_ref[rows, :].astype(F32)
            up = up_ref[rows, :].astype(F32)
            dact = _dot_nt(dft, wd_ref[...])
            sig = jax.nn.sigmoid(gate)
            silu = gate * sig
            dgate = (dact * up * (sig * (1.0 + gate * (1.0 - sig)))).astype(BF16)
            dup = (dact * silu).astype(BF16)
            act = (silu * up).astype(BF16)
            dwd = _dot_tn(act, dft)
            dwg = _dot_tn(dgate, at)
            dwu = _dot_tn(dup, at)
            dat = _dot(dgate, wg_ref[...]) + _dot(dup, wu_ref[...])

            @pl.when(i == 0)
            def _():
                acc_d[...] = dwd
                acc_g[...] = dwg
                acc_u[...] = dwu

            @pl.when(i != 0)
            def _():
                acc_d[...] += dwd
                acc_g[...] += dwg
                acc_u[...] += dwu

            @pl.when(j == 0)
            def _():
                da_ref[rows, :] = dat

            @pl.when(j != 0)
            def _():
                da_ref[rows, :] += dat

            return carry

        lax.fori_loop(0, N_ROW_TILES, tile, 0)
        dwg_ref[...] = acc_g[...].astype(BF16)
        dwu_ref[...] = acc_u[...].astype(BF16)
        dwd_ref[...] = acc_d[...].astype(BF16)

    wtile = pl.BlockSpec((FF_TILE, D_MODEL), lambda j: (j, 0))
    hid = pl.BlockSpec((SEQ, FF_TILE), lambda j: (0, j))
    full = pl.BlockSpec((SEQ, D_MODEL), lambda j: (0, 0))
    return pl.pallas_call(
        body, grid=(D_FF // FF_TILE,), in_specs=[full, full, hid, hid, wtile, wtile, wtile],
        out_specs=[full, wtile, wtile, wtile],
        out_shape=[SDS((SEQ, D_MODEL), F32)] + [SDS((D_FF, D_MODEL), BF16)] * 3,
        scratch_shapes=[pltpu.VMEM((FF_TILE, D_MODEL), F32)] * 3,
        name=name, compiler_params=_params(("arbitrary",)))(df, a, gate, up, wg, wu, wd)


HEADS_PER_BLOCK = IN_SHARD // HEAD_DIM
QKV_BLOCKS = 3 * NA_WIDTH // IN_SHARD


def _proj_heads(a, w):
    def body(a_ref, w_ref, o_ref):
        def tile(i, carry):
            rows = _rows(i)
            res = _dot(a_ref[rows, :], w_ref[...])
            for sub in range(HEADS_PER_BLOCK):
                o_ref[sub, rows, :] = res[:, sub * HEAD_DIM:(sub + 1) * HEAD_DIM]
            return carry

        lax.fori_loop(0, N_ROW_TILES, tile, 0)

    return pl.pallas_call(
        body, grid=(QKV_BLOCKS,),
        in_specs=[pl.BlockSpec((SEQ, D_MODEL), lambda j: (0, 0)), pl.BlockSpec((None, D_MODEL, IN_SHARD), lambda j: (j, 0, 0))],
        out_specs=pl.BlockSpec((HEADS_PER_BLOCK, SEQ, HEAD_DIM), lambda j: (j, 0, 0)),
        out_shape=SDS((3 * HEADS, SEQ, HEAD_DIM), F32), name="proj_heads",
        compiler_params=_params(("parallel",)))(a, w)


def _proj_u(a, w):
    def body(a_ref, w_ref, o_ref):
        def tile(i, carry):
            rows = _rows(i)
            o_ref[rows, :] = _dot(a_ref[rows, :], w_ref[...])
            return carry

        lax.fori_loop(0, N_ROW_TILES, tile, 0)

    return pl.pallas_call(
        body, grid=(N_DEV - QKV_BLOCKS,),
        in_specs=[pl.BlockSpec((SEQ, D_MODEL), lambda j: (0, 0)),
                  pl.BlockSpec((None, D_MODEL, IN_SHARD), lambda j: (j + QKV_BLOCKS, 0, 0))],
        out_specs=pl.BlockSpec((SEQ, IN_SHARD), lambda j: (0, j)),
        out_shape=SDS((SEQ, S5_WIDTH), F32), name="proj_u",
        compiler_params=_params(("parallel",)))(a, w)


def _proj_bwd(dq, dk, dv, du, a, w):
    def body(dq_ref, dk_ref, dv_ref, du_ref, a_ref, w_ref, da_ref, dw_ref, acc, dp_ref):
        j = pl.program_id(0)

        for which, src in enumerate((dq_ref, dk_ref, dv_ref)):
            @pl.when((j >= 2 * which) & (j < 2 * which + 2))
            def _(src=src):
                dp_ref[...] = jnp.concatenate([src[sub] for sub in range(HEADS_PER_BLOCK)], axis=-1).astype(BF16)

        @pl.when(j >= QKV_BLOCKS)
        def _():
            dp_ref[...] = du_ref[...].astype(BF16)

        def tile(i, carry):
            rows = _rows(i)
            dpt = dp_ref[rows, :]
            dw = _dot_tn(a_ref[rows, :], dpt)
            dat = _dot_nt(dpt, w_ref[...])

            @pl.when(i == 0)
            def _():
                acc[...] = dw

            @pl.when(i != 0)
            def _():
                acc[...] += dw

            @pl.when(j == 0)
            def _():
                da_ref[rows, :] = dat

            @pl.when(j != 0)
            def _():
                da_ref[rows, :] += dat

            return carry

        lax.fori_loop(0, N_ROW_TILES, tile, 0)
        dw_ref[...] = acc[...].astype(BF16)

    full = pl.BlockSpec((SEQ, D_MODEL), lambda j: (0, 0))
    wspec = pl.BlockSpec((None, D_MODEL, IN_SHARD), lambda j: (j, 0, 0))

    def heads(which):
        return pl.BlockSpec((HEADS_PER_BLOCK, SEQ, HEAD_DIM), lambda j: (jnp.clip(j - 2 * which, 0, 1), 0, 0))

    return pl.pallas_call(
        body, grid=(N_DEV,),
        in_specs=[heads(0), heads(1), heads(2),
                  pl.BlockSpec((SEQ, IN_SHARD), lambda j: (0, jnp.clip(j - QKV_BLOCKS, 0, 1))), full, wspec],
        out_specs=[full, wspec],
        out_shape=[SDS((SEQ, D_MODEL), F32), SDS((N_DEV, D_MODEL, IN_SHARD), BF16)],
        scratch_shapes=[pltpu.VMEM((D_MODEL, IN_SHARD), F32), pltpu.VMEM((SEQ, IN_SHARD), BF16)],
        name="proj_bwd", compiler_params=_params(("arbitrary",)))(dq, dk, dv, du, a, w)


def _na_consts():
    c = np.arange(GRID_W)
    col_start = np.clip(c - KW // 2, 0, GRID_W - KW)
    col_in = (c[None, :] >= col_start[:, None]) & (c[None, :] < col_start[:, None] + KW)
    dc = np.clip(c[None, :] - c[:, None] + KW - 1, 0, 2 * KW - 2)
    onehot = np.zeros((128, GRID_W * GRID_W), np.float32)
    qq, kk = np.meshgrid(c, c, indexing="ij")
    onehot[dc[col_in], (qq * GRID_W + kk)[col_in]] = 1.0
    negmask = np.where(col_in, 0.0, NEG_INF).astype(np.float32).reshape(1, -1)
    return onehot, negmask


def _na_pair(block_type, a, b):
    if block_type == 0:
        return b - a + KH - 1 if b < KH else None
    if block_type == 1:
        return b - a + KH // 2 - 1 if a <= b < a + KH else None
    return b - a if b >= NA_KR - KH else None


def _rpb_expand(rpb):
    onehot, negmask = _na_consts()
    rows = HEADS * (2 * KH - 1)
    rpb_pad = jnp.pad(rpb.reshape(rows, 2 * KW - 1), ((0, 128 - rows), (0, 128 - (2 * KW - 1))))

    def body(r_ref, oh_ref, m_ref, t_ref):
        hi, mid, lo = _split3(r_ref[...])
        oh = oh_ref[...]
        t_ref[...] = _dot(hi, oh) + _dot(mid, oh) + _dot(lo, oh) + m_ref[...]

    table = pl.pallas_call(body, out_shape=SDS((128, GRID_W * GRID_W), F32), name="rpb_expand",
                           compiler_params=_params())(rpb_pad, jnp.asarray(onehot, BF16), jnp.asarray(negmask))
    return table[:rows].reshape(HEADS, 2 * KH - 1, GRID_W, GRID_W)


def _rpb_reduce(dslabs):
    onehot, _ = _na_consts()
    rows = HEADS * (2 * KH - 1)

    def body(x_ref, oht_ref, o_ref):
        hi, mid, lo = _split3(x_ref[...])
        oht = oht_ref[...]
        o_ref[...] = _dot(hi, oht) + _dot(mid, oht) + _dot(lo, oht)

    out = pl.pallas_call(body, out_shape=SDS((rows, 128), F32), name="rpb_reduce", compiler_params=_params())(
        dslabs.reshape(rows, GRID_W * GRID_W), jnp.asarray(onehot.T, BF16))
    return out.reshape(HEADS, 2 * KH - 1, 128)


def _bias_tiles(slab_ref, tile_ref):
    tile_ref[...] = jnp.full(tile_ref.shape, NEG_INF, F32)
    for t in range(NA_TYPES):
        for a in range(NA_RB):
            for b in range(NA_KR):
                dr = _na_pair(t, a, b)
                if dr is not None:
                    tile_ref[t, a * GRID_W:(a + 1) * GRID_W, b * GRID_W:(b + 1) * GRID_W] = slab_ref[dr]


def _bias_tiles_bwd(dtile_ref, dslab_ref):
    acc = {}
    for t in range(NA_TYPES):
        for a in range(NA_RB):
            for b in range(NA_KR):
                dr = _na_pair(t, a, b)
                if dr is not None:
                    part = dtile_ref[t, a * GRID_W:(a + 1) * GRID_W, b * GRID_W:(b + 1) * GRID_W]
                    acc[dr] = part if dr not in acc else acc[dr] + part
    for dr in range(2 * KH - 1):
        dslab_ref[dr] = acc[dr]


def _block_geometry(g):
    start = jnp.clip(g * NA_RB - KH // 2, 0, GRID_ROWS - NA_KR)
    block_type = jnp.where(g == 0, 0, jnp.where(g == NA_BLOCKS - 1, 2, 1))
    q0 = pl.multiple_of(N_META + g * NA_QB, 16)
    k0 = pl.multiple_of(N_META + start * GRID_W, 16)
    return block_type, q0, k0


def _na_probs(q, kk, km, bias):
    s = _dot_nt(q, kk) * ATT_SCALE + bias
    sm = _dot_nt(q, km) * ATT_SCALE
    m = jnp.maximum(jnp.max(s, axis=-1, keepdims=True), jnp.max(sm, axis=-1, keepdims=True))
    p = jnp.exp(s - m)
    pm = jnp.exp(sm - m)
    inv = 1.0 / (jnp.sum(p, axis=-1, keepdims=True) + jnp.sum(pm, axis=-1, keepdims=True))
    return p * inv, pm * inv


def _meta_probs(qm, km):
    s = _dot_nt(qm, km) * ATT_SCALE
    p = jnp.exp(s - jnp.max(s, axis=-1, keepdims=True))
    return p / jnp.sum(p, axis=-1, keepdims=True)


def _qkv_specs():
    return [pl.BlockSpec((None, SEQ, HEAD_DIM), lambda h, which=which: (h + which * HEADS, 0, 0)) for which in range(3)]


def _na_fwd(qkv, bias):
    def body(q_ref, k_ref, v_ref, slab_ref, o_ref, b_ref):
        _bias_tiles(slab_ref, b_ref)
        km = k_ref[0:N_META, :].astype(BF16)
        vm = v_ref[0:N_META, :].astype(BF16)
        pmm = _meta_probs(q_ref[0:N_META, :].astype(BF16), km)
        o_ref[0:N_META, :] = _dot(pmm.astype(BF16), vm)

        def block(g, carry):
            block_type, q0, k0 = _block_geometry(g)
            qb = q_ref[pl.ds(q0, NA_QB), :].astype(BF16)
            kk = k_ref[pl.ds(k0, NA_KB), :].astype(BF16)
            vv = v_ref[pl.ds(k0, NA_KB), :].astype(BF16)
            p, pm = _na_probs(qb, kk, km, b_ref[block_type])
            o_ref[pl.ds(q0, NA_QB), :] = _dot(p.astype(BF16), vv) + _dot(pm.astype(BF16), vm)
            return carry

        lax.fori_loop(0, NA_BLOCKS, block, 0)

    head = pl.BlockSpec((None, SEQ, HEAD_DIM), lambda h: (h, 0, 0))
    return pl.pallas_call(
        body, grid=(HEADS,), in_specs=_qkv_specs() + [pl.BlockSpec((None, 2 * KH - 1, GRID_W, GRID_W), lambda h: (h, 0, 0, 0))],
        out_specs=head, out_shape=SDS((HEADS, SEQ, HEAD_DIM), F32), name="na_fwd",
        scratch_shapes=[pltpu.VMEM((NA_TYPES, NA_QB, NA_KB), F32)],
        compiler_params=_params(("parallel",)))(qkv, qkv, qkv, bias)


def _na_bwd(qkv, bias, do):
    def body(q_ref, k_ref, v_ref, slab_ref, do_ref, dq_ref, dk_ref, dv_ref, dslab_ref, b_ref, db_ref):
        _bias_tiles(slab_ref, b_ref)
        km = k_ref[0:N_META, :].astype(BF16)
        vm = v_ref[0:N_META, :].astype(BF16)
        dk_ref[...] = jnp.zeros_like(dk_ref)
        dv_ref[...] = jnp.zeros_like(dv_ref)
        db_ref[...] = jnp.zeros_like(db_ref)

        qm = q_ref[0:N_META, :].astype(BF16)
        dom = do_ref[0:N_META, :].astype(BF16)
        pmm = _meta_probs(qm, km)
        dpm = _dot_nt(dom, vm)
        dsm = (pmm * (dpm - jnp.sum(pmm * dpm, axis=-1, keepdims=True)) * ATT_SCALE).astype(BF16)
        dq_ref[0:N_META, :] = _dot(dsm, km)
        dkm0 = _dot_tn(dsm, qm)
        dvm0 = _dot_tn(pmm.astype(BF16), dom)

        def block(g, carry):
            dkm, dvm = carry
            block_type, q0, k0 = _block_geometry(g)
            qb = q_ref[pl.ds(q0, NA_QB), :].astype(BF16)
            kk = k_ref[pl.ds(k0, NA_KB), :].astype(BF16)
            vv = v_ref[pl.ds(k0, NA_KB), :].astype(BF16)
            dob = do_ref[pl.ds(q0, NA_QB), :].astype(BF16)
            p, pm = _na_probs(qb, kk, km, b_ref[block_type])
            dp = _dot_nt(dob, vv)
            dpm_ = _dot_nt(dob, vm)
            delta = jnp.sum(p * dp, axis=-1, keepdims=True) + jnp.sum(pm * dpm_, axis=-1, keepdims=True)
            ds = p * (dp - delta)
            dsm_ = pm * (dpm_ - delta)
            db_ref[block_type] += ds
            dsb = (ds * ATT_SCALE).astype(BF16)
            dsmb = (dsm_ * ATT_SCALE).astype(BF16)
            dq_ref[pl.ds(q0, NA_QB), :] = _dot(dsb, kk) + _dot(dsmb, km)
            dk_ref[pl.ds(k0, NA_KB), :] += _dot_tn(dsb, qb)
            dv_ref[pl.ds(k0, NA_KB), :] += _dot_tn(p.astype(BF16), dob)
            return dkm + _dot_tn(dsmb, qb), dvm + _dot_tn(pm.astype(BF16), dob)

        dkm, dvm = lax.fori_loop(0, NA_BLOCKS, block, (dkm0, dvm0))
        dk_ref[0:N_META, :] = dkm
        dv_ref[0:N_META, :] = dvm
        _bias_tiles_bwd(db_ref, dslab_ref)

    head = pl.BlockSpec((None, SEQ, HEAD_DIM), lambda h: (h, 0, 0))
    bspec = pl.BlockSpec((None, 2 * KH - 1, GRID_W, GRID_W), lambda h: (h, 0, 0, 0))
    return pl.pallas_call(
        body, grid=(HEADS,), in_specs=_qkv_specs() + [bspec, head], out_specs=[head, head, head, bspec],
        out_shape=[SDS((HEADS, SEQ, HEAD_DIM), F32)] * 3 + [SDS((HEADS, 2 * KH - 1, GRID_W, GRID_W), F32)],
        scratch_shapes=[pltpu.VMEM((NA_TYPES, NA_QB, NA_KB), F32), pltpu.VMEM((NA_TYPES, NA_QB, NA_KB), F32)],
        name="na_bwd", compiler_params=_params(("parallel",)))(qkv, qkv, qkv, bias, do)


def _cmul(ar, ai, br, bi):
    return ar * br - ai * bi, ar * bi + ai * br


def _cpow(ar, ai, n):
    rr, ri = None, None
    br, bi = ar, ai
    while n:
        if n & 1:
            rr, ri = (br, bi) if rr is None else _cmul(rr, ri, br, bi)
        n >>= 1
        if n:
            br, bi = _cmul(br, bi, br, bi)
    return rr, ri


def _s5_prep(lr, li, logdt, bre, bim):
    def body(lr_ref, li_ref, dt_ref, br_ref, bi_ref, lbr_ref, lbi_ref, bbr_ref, bbi_ref):
        lr_, li_ = lr_ref[...], li_ref[...]
        dt = jnp.exp(dt_ref[...])
        mag = jnp.exp(lr_ * dt)
        lbr = mag * jnp.cos(li_ * dt)
        lbi = mag * jnp.sin(li_ * dt)
        lbr_ref[...] = lbr
        lbi_ref[...] = lbi
        den = lr_ * lr_ + li_ * li_
        xr = lbr - 1.0
        cr = (xr * lr_ + lbi * li_) / den
        ci = (lbi * lr_ - xr * li_) / den
        br, bi = br_ref[...], bi_ref[...]
        bbr_ref[...] = cr[:, None, :] * br - ci[:, None, :] * bi
        bbi_ref[...] = cr[:, None, :] * bi + ci[:, None, :] * br

    n = 2 * S5_GROUPS
    return pl.pallas_call(
        body, out_shape=[SDS((n, S5_STATE), F32)] * 2 + [SDS((n, S5_GROUP, S5_STATE), F32)] * 2,
        name="s5_prep", compiler_params=_params())(lr, li, logdt, bre, bim)


def _s5_prep_bwd(lr, li, logdt, bre, bim, dar, dai, dbbr, dbbi):
    def body(lr_ref, li_ref, dt_ref, br_ref, bi_ref, dar_ref, dai_ref, dbr_ref, dbi_ref,
             glr_ref, gli_ref, gdt_ref, gbr_ref, gbi_ref):
        lr_, li_ = lr_ref[...], li_ref[...]
        dt = jnp.exp(dt_ref[...])
        mag = jnp.exp(lr_ * dt)
        lbr = mag * jnp.cos(li_ * dt)
        lbi = mag * jnp.sin(li_ * dt)
        den = lr_ * lr_ + li_ * li_
        xr = lbr - 1.0
        cr = (xr * lr_ + lbi * li_) / den
        ci = (lbi * lr_ - xr * li_) / den
        br, bi = br_ref[...], bi_ref[...]
        dbr, dbi = dbr_ref[...], dbi_ref[...]
        gbr_ref[...] = cr[:, None, :] * dbr + ci[:, None, :] * dbi
        gbi_ref[...] = cr[:, None, :] * dbi - ci[:, None, :] * dbr
        gcr = jnp.sum(dbr * br + dbi * bi, axis=1)
        gci = jnp.sum(dbi * br - dbr * bi, axis=1)
        ilr, ili = lr_ / den, li_ / den
        tr, ti = _cmul(gcr, gci, ilr, ili)
        glbr = dar_ref[...] + tr
        glbi = dai_ref[...] + ti
        dr_, di_ = _cmul(tr, ti, cr, -ci)
        gwr, gwi = _cmul(glbr, glbi, lbr, -lbi)
        glr_ref[...] = gwr * dt - dr_
        gli_ref[...] = gwi * dt - di_
        gdt_ref[...] = jnp.sum(gwr * lr_ + gwi * li_, axis=-1, keepdims=True) * dt

    n = 2 * S5_GROUPS
    return pl.pallas_call(
        body, out_shape=[SDS((n, S5_STATE), F32)] * 2 + [SDS((n, 1), F32)] + [SDS((n, S5_GROUP, S5_STATE), F32)] * 2,
        name="s5_prep_bwd", compiler_params=_params())(lr, li, logdt, bre, bim, dar, dai, dbbr, dbbi)


def _scan_local(xr_ref, xi_ref, ar8, ai8, reverse):
    def step(i, carry):
        sr, si = carry
        idx = (SCAN_T - 1 - i) if reverse else i
        rows = pl.ds(pl.multiple_of(idx * SCAN_BLOCKS, SCAN_BLOCKS), SCAN_BLOCKS)
        nr = ar8 * sr - ai8 * si + xr_ref[rows, :]
        ni = ar8 * si + ai8 * sr + xi_ref[rows, :]
        xr_ref[rows, :] = nr
        xi_ref[rows, :] = ni
        return nr, ni

    z = jnp.zeros(ar8.shape, F32)
    return lax.fori_loop(0, SCAN_T, step, (z, z))


def _scan_carries(er, ei, atr, ati, reverse):
    row = lax.broadcasted_iota(jnp.int32, er.shape, 0)
    cr = jnp.zeros((1, er.shape[1]), F32)
    ci = cr
    outr = jnp.zeros(er.shape, F32)
    outi = outr
    order = range(SCAN_BLOCKS - 1, -1, -1) if reverse else range(SCAN_BLOCKS)
    for b in order:
        outr = jnp.where(row == b, cr, outr)
        outi = jnp.where(row == b, ci, outi)
        nr, ni = _cmul(atr, ati, cr, ci)
        cr, ci = nr + er[b:b + 1, :], ni + ei[b:b + 1, :]
    return outr, outi


def _scan_fixup(xr_ref, xi_ref, cr8, ci8, ar8, ai8, reverse):
    def step(i, carry):
        pr, pi = carry
        idx = (SCAN_T - 1 - i) if reverse else i
        rows = pl.ds(pl.multiple_of(idx * SCAN_BLOCKS, SCAN_BLOCKS), SCAN_BLOCKS)
        fr, fi = _cmul(pr, pi, cr8, ci8)
        xr_ref[rows, :] += fr
        xi_ref[rows, :] += fi
        return _cmul(pr, pi, ar8, ai8)

    lax.fori_loop(0, SCAN_T, step, (ar8, ai8), unroll=2)


def _scan(xr_ref, xi_ref, ar, ai, reverse):
    n = ar.shape[1]
    ar8 = jnp.broadcast_to(ar, (SCAN_BLOCKS, n))
    ai8 = jnp.broadcast_to(ai, (SCAN_BLOCKS, n))
    er, ei = _scan_local(xr_ref, xi_ref, ar8, ai8, reverse)
    atr, ati = _cpow(ar, ai, SCAN_T)
    cr8, ci8 = _scan_carries(er, ei, atr, ati, reverse)
    _scan_fixup(xr_ref, xi_ref, cr8, ci8, ar8, ai8, reverse)


def _s5_specs():
    chan = pl.BlockSpec((SEQ, CH_W), lambda c, d: (0, c))
    chan2 = pl.BlockSpec((None, SEQ, CH_W), lambda c, d: (d, 0, c))
    state = pl.BlockSpec((None, SEQ, ST_W), lambda c, d: (d, 0, c))
    bmat = pl.BlockSpec((None, None, CH_W, ST_W), lambda c, d: (d, c, 0, 0))
    cmat = pl.BlockSpec((None, None, ST_W, CH_W), lambda c, d: (d, c, 0, 0))
    avec = pl.BlockSpec((None, None, 1, ST_W), lambda c, d: (d, c, 0, 0))
    return chan, chan2, state, bmat, cmat, avec


def _scan_by_direction(xr_ref, xi_ref, ar, ai, d, adjoint):
    @pl.when(d == 0)
    def _():
        _scan(xr_ref, xi_ref, ar, ai, reverse=adjoint)

    @pl.when(d == 1)
    def _():
        _scan(xr_ref, xi_ref, ar, ai, reverse=not adjoint)


def _s5_scan_fwd(u, bre, bim, are, aim, cre, cim):
    def body(u_ref, bre_ref, bim_ref, are_ref, aim_ref, cre_ref, cim_ref, sr_ref, si_ref, y_ref):
        ub = u_ref[...].astype(BF16)
        sr_ref[...] = _dot(ub, bre_ref[...])
        si_ref[...] = _dot(ub, bim_ref[...])
        _scan_by_direction(sr_ref, si_ref, are_ref[...], aim_ref[...], pl.program_id(1), adjoint=False)
        y_ref[...] = _dot(sr_ref[...].astype(BF16), cre_ref[...]) - _dot(si_ref[...].astype(BF16), cim_ref[...])

    chan, chan2, state, bmat, cmat, avec = _s5_specs()
    return pl.pallas_call(
        body, grid=(S5_CHUNKS, 2), in_specs=[chan, bmat, bmat, avec, avec, cmat, cmat], out_specs=[state, state, chan2],
        out_shape=[SDS((2, SEQ, S5_GROUPS * S5_STATE), F32)] * 2 + [SDS((2, SEQ, S5_WIDTH), F32)],
        name="s5_scan_fwd", compiler_params=_params(("parallel", "parallel")))(u, bre, bim, are, aim, cre, cim)


def _dlam(gr_ref, gi_ref, sr_ref, si_ref, reverse):
    tile = lambda i: pl.ds(pl.multiple_of(i * SCAN_BLOCKS, SCAN_BLOCKS), SCAN_BLOCKS)
    row = lax.broadcasted_iota(jnp.int32, (SCAN_BLOCKS, ST_W), 0)
    if reverse:
        edge, src, shift, empty, lo, hi, dprev = SCAN_T - 1, 0, SCAN_BLOCKS - 1, SCAN_BLOCKS - 1, 0, SCAN_T - 1, 1
    else:
        edge, src, shift, empty, lo, hi, dprev = 0, SCAN_T - 1, 1, 0, 1, SCAN_T, -1
    spr = jnp.where(row == empty, 0.0, pltpu.roll(sr_ref[tile(src), :], shift, 0))
    spi = jnp.where(row == empty, 0.0, pltpu.roll(si_ref[tile(src), :], shift, 0))
    acc0 = _cmul(gr_ref[tile(edge), :], gi_ref[tile(edge), :], spr, -spi)

    def step(i, carry):
        accr, acci = carry
        pr, pi = _cmul(gr_ref[tile(i), :], gi_ref[tile(i), :], sr_ref[tile(i + dprev), :], -si_ref[tile(i + dprev), :])
        return accr + pr, acci + pi

    accr, acci = lax.fori_loop(lo, hi, step, acc0)
    return jnp.sum(accr, axis=0, keepdims=True), jnp.sum(acci, axis=0, keepdims=True)


def _diag_out(out_ref, full):
    for g in range(8):
        out_ref[g] = full[g * S5_GROUP:(g + 1) * S5_GROUP, g * S5_STATE:(g + 1) * S5_STATE]


def _s5_scan_bwd(dy, du_skip, u, sr, si, bre, bim, are, aim, cre, cim):
    def body(dy_ref, dus_ref, u_ref, sr_ref, si_ref, bre_ref, bim_ref, are_ref, aim_ref, cre_ref, cim_ref,
             du_ref, dbr_ref, dbi_ref, dcr_ref, dci_ref, dar_ref, dai_ref, gr_ref, gi_ref):
        d = pl.program_id(1)
        dyb = dy_ref[...].astype(BF16)
        gr_ref[...] = _dot_nt(dyb, cre_ref[...])
        gi_ref[...] = -_dot_nt(dyb, cim_ref[...])
        _diag_out(dcr_ref, _dot_tn(dyb, sr_ref[...].astype(BF16)))
        _diag_out(dci_ref, -_dot_tn(dyb, si_ref[...].astype(BF16)))
        _scan_by_direction(gr_ref, gi_ref, are_ref[...], -aim_ref[...], d, adjoint=True)

        @pl.when(d == 0)
        def _():
            dar_ref[...], dai_ref[...] = _dlam(gr_ref, gi_ref, sr_ref, si_ref, reverse=False)
            du_ref[...] = dus_ref[...]

        @pl.when(d == 1)
        def _():
            dar_ref[...], dai_ref[...] = _dlam(gr_ref, gi_ref, sr_ref, si_ref, reverse=True)

        grb = gr_ref[...].astype(BF16)
        gib = gi_ref[...].astype(BF16)
        du_ref[...] += _dot_nt(grb, bre_ref[...]) + _dot_nt(gib, bim_ref[...])
        ub = u_ref[...].astype(BF16)
        _diag_out(dbr_ref, _dot_tn(ub, grb))
        _diag_out(dbi_ref, _dot_tn(ub, gib))

    chan, _, state, bmat, cmat, avec = _s5_specs()
    diag = pl.BlockSpec((None, None, 8, S5_GROUP, S5_STATE), lambda c, d: (d, c, 0, 0, 0))
    return pl.pallas_call(
        body, grid=(S5_CHUNKS, 2), in_specs=[chan, chan, chan, state, state, bmat, bmat, avec, avec, cmat, cmat],
        out_specs=[chan, diag, diag, diag, diag, avec, avec],
        out_shape=[SDS((SEQ, S5_WIDTH), F32)] + [SDS((2, S5_CHUNKS, 8, S5_GROUP, S5_STATE), F32)] * 4
                  + [SDS((2, S5_CHUNKS, 1, ST_W), F32)] * 2,
        scratch_shapes=[pltpu.VMEM((SEQ, ST_W), F32), pltpu.VMEM((SEQ, ST_W), F32)],
        name="s5_scan_bwd", compiler_params=_params(("parallel", "arbitrary")))(dy, du_skip, u, sr, si, bre, bim, are, aim, cre, cim)


_GELU_K = math.sqrt(2.0 / math.pi)
_GELU_C = 0.044715


def _gelu(x):
    t = jnp.tanh(_GELU_K * (x + _GELU_C * x * x * x))
    return 0.5 * x * (1.0 + t), t


def _s5_glu_fwd(u, y2, dskip, wglu, bglu):
    def body(u_ref, y0_ref, y1_ref, d_ref, w_ref, b_ref, o_ref, yp_ref):
        ypre = u_ref[...] * d_ref[...] + y0_ref[...] + y1_ref[...]
        yp_ref[...] = ypre
        y, _ = _gelu(ypre)
        z = _dot(y.astype(BF16), w_ref[...]) + b_ref[...]
        o_ref[...] = y * jax.nn.sigmoid(z)

    row = _row_spec(S5_WIDTH)
    vec = _fix_spec((1, S5_WIDTH))
    dir0 = pl.BlockSpec((None, ROW_TILE, S5_WIDTH), lambda i: (0, i, 0))
    dir1 = pl.BlockSpec((None, ROW_TILE, S5_WIDTH), lambda i: (1, i, 0))
    return pl.pallas_call(
        body, grid=(N_ROW_TILES,), in_specs=[row, dir0, dir1, vec, _fix_spec((S5_WIDTH, S5_WIDTH)), vec],
        out_specs=[row, row], out_shape=[SDS((SEQ, S5_WIDTH), F32)] * 2, name="s5_glu_fwd",
        compiler_params=_params(("parallel",)))(u, y2, y2, dskip, wglu, bglu)


def _s5_glu_bwd(do, ypre, u, dskip, wglu, bglu):
    def body(do_ref, yp_ref, u_ref, d_ref, w_ref, b_ref, dyp_ref, du_ref, dw_ref, db_ref, dd_ref):
        i = pl.program_id(0)
        ypre = yp_ref[...]
        y, t = _gelu(ypre)
        yb = y.astype(BF16)
        sg = jax.nn.sigmoid(_dot(yb, w_ref[...]) + b_ref[...])
        dov = do_ref[...]
        dz = dov * y * sg * (1.0 - sg)
        dzb = dz.astype(BF16)
        dy = dov * sg + _dot_nt(dzb, w_ref[...])
        dgelu = 0.5 * (1.0 + t) + 0.5 * ypre * (1.0 - t * t) * _GELU_K * (1.0 + 3.0 * _GELU_C * ypre * ypre)
        dyp = dy * dgelu
        dyp_ref[...] = dyp
        uv = u_ref[...]
        du_ref[...] = dyp * d_ref[...]

        @pl.when(i == 0)
        def _():
            dw_ref[...] = jnp.zeros_like(dw_ref)
            db_ref[...] = jnp.zeros_like(db_ref)
            dd_ref[...] = jnp.zeros_like(dd_ref)

        dw_ref[...] += _dot_tn(yb, dzb)
        db_ref[...] += jnp.sum(dz, axis=0, keepdims=True)
        dd_ref[...] += jnp.sum(dyp * uv, axis=0, keepdims=True)

    row = _row_spec(S5_WIDTH)
    vec = _fix_spec((1, S5_WIDTH))
    mat = _fix_spec((S5_WIDTH, S5_WIDTH))
    return pl.pallas_call(
        body, grid=(N_ROW_TILES,), in_specs=[row, row, row, vec, mat, vec], out_specs=[row, row, mat, vec, vec],
        out_shape=[SDS((SEQ, S5_WIDTH), F32)] * 2 + [SDS((S5_WIDTH, S5_WIDTH), F32), SDS((1, S5_WIDTH), F32), SDS((1, S5_WIDTH), F32)],
        name="s5_glu_bwd", compiler_params=_params(("arbitrary",)))(do, ypre, u, dskip, wglu, bglu)


def _heads_side_by_side(o_ref):
    return jnp.concatenate([o_ref[h] for h in range(HEADS)], axis=-1)


def _mix_out_fwd(ona, os5, g_na, g_s5, wout):
    def body(a_ref, s_ref, ga_ref, gs_ref, w_ref, o_ref):
        av, sv = _heads_side_by_side(a_ref), s_ref[...]
        ca = (av * _rstd(av) * ga_ref[...]).astype(BF16)
        cs = (sv * _rstd(sv) * gs_ref[...]).astype(BF16)
        o_ref[...] = _dot(ca, w_ref[0:NA_WIDTH, :]) + _dot(cs, w_ref[NA_WIDTH:, :])

    row = _row_spec(NA_WIDTH)
    vec = _fix_spec((1, NA_WIDTH))
    heads = pl.BlockSpec((HEADS, ROW_TILE, HEAD_DIM), lambda i: (0, i, 0))
    return pl.pallas_call(
        body, grid=(N_ROW_TILES,), in_specs=[heads, row, vec, vec, _fix_spec((D_MODEL, D_MODEL))],
        out_specs=_row_spec(D_MODEL), out_shape=SDS((SEQ, D_MODEL), F32), name="mix_out_fwd",
        compiler_params=_params(("parallel",)))(ona, os5, g_na, g_s5, wout)


def _mix_out_bwd(dmix, ona, os5, g_na, g_s5, wout):
    def body(dm_ref, a_ref, s_ref, ga_ref, gs_ref, w_ref, da_ref, ds_ref, dw_ref, dga_ref, dgs_ref):
        i = pl.program_id(0)
        dm = dm_ref[...]
        av, sv = _heads_side_by_side(a_ref), s_ref[...]
        ra, rs = _rstd(av), _rstd(sv)
        ga, gs = ga_ref[...], gs_ref[...]
        ca = (av * ra * ga).astype(BF16)
        cs = (sv * rs * gs).astype(BF16)
        dca = _dot_nt(dm, w_ref[0:NA_WIDTH, :])
        dcs = _dot_nt(dm, w_ref[NA_WIDTH:, :])
        da, dga = _rms_bwd(av, ra, ga, dca)
        ds, dgs = _rms_bwd(sv, rs, gs, dcs)
        for h in range(HEADS):
            da_ref[h] = da[:, h * HEAD_DIM:(h + 1) * HEAD_DIM]
        ds_ref[...] = ds

        @pl.when(i == 0)
        def _():
            dw_ref[...] = jnp.zeros_like(dw_ref)
            dga_ref[...] = jnp.zeros_like(dga_ref)
            dgs_ref[...] = jnp.zeros_like(dgs_ref)

        dw_ref[0:NA_WIDTH, :] += _dot_tn(ca, dm)
        dw_ref[NA_WIDTH:, :] += _dot_tn(cs, dm)
        dga_ref[...] += jnp.sum(dga, axis=0, keepdims=True)
        dgs_ref[...] += jnp.sum(dgs, axis=0, keepdims=True)

    row = _row_spec(NA_WIDTH)
    vec = _fix_spec((1, NA_WIDTH))
    mat = _fix_spec((D_MODEL, D_MODEL))
    heads = pl.BlockSpec((HEADS, ROW_TILE, HEAD_DIM), lambda i: (0, i, 0))
    return pl.pallas_call(
        body, grid=(N_ROW_TILES,), in_specs=[_row_spec(D_MODEL), heads, row, vec, vec, mat],
        out_specs=[heads, row, mat, vec, vec],
        out_shape=[SDS((HEADS, SEQ, HEAD_DIM), F32), SDS((SEQ, NA_WIDTH), F32), SDS((D_MODEL, D_MODEL), F32),
                   SDS((1, NA_WIDTH), F32), SDS((1, NA_WIDTH), F32)],
        name="mix_out_bwd", compiler_params=_params(("arbitrary",)))(dmix, ona, os5, g_na, g_s5, wout)


def _me():
    x, y, c = lax.axis_index("x"), lax.axis_index("y"), lax.axis_index("c")
    return x, y, c, 4 * x + 2 * y + c


def _peer(k):
    x, y, c, _ = _me()
    px = 1 - x if (k >> 2) & 1 else x
    py = 1 - y if (k >> 1) & 1 else y
    pc = 1 - c if k & 1 else c
    return (px, py, pc), 4 * px + 2 * py + pc


ALL_PEERS = (1, 2, 3, 4, 5, 6, 7)
CHIP_PEERS = (2, 4, 6)
SIBLING = 1


def _slot8(pos):
    return 4 * pos[0] + 2 * pos[1] + pos[2]


def _slot4(pos):
    return 2 * pos[0] + pos[1]


def _exchange(arrays, gather, name, after=()):
    n, n_after = len(arrays), len(after)

    def body(*refs):
        ins, outs = refs[:n], refs[n + n_after:2 * n + n_after]
        token = refs[2 * n + n_after]
        send_sems, recv_sems, local_sems = refs[2 * n + n_after + 1:]
        token[...] = jnp.zeros_like(token)
        _, _, _, me = _me()
        started = []
        for a in range(n):
            src_mine = ins[a] if gather else ins[a].at[me]
            local = pltpu.make_async_copy(src_mine, outs[a].at[me], local_sems.at[a])
            local.start()
            started.append(local)
        sends = []
        for k in range(1, N_DEV):
            peer, peer_idx = _peer(k)
            for a in range(n):
                src = ins[a] if gather else ins[a].at[peer_idx]
                cp = pltpu.make_async_remote_copy(src_ref=src, dst_ref=outs[a].at[me], send_sem=send_sems.at[a, k - 1],
                                                  recv_sem=recv_sems.at[a, k - 1], device_id=peer, device_id_type=MESH)
                cp.start()
                sends.append(cp)
        for k in range(1, N_DEV):
            peer, peer_idx = _peer(k)
            for a in range(n):
                src = ins[a] if gather else ins[a].at[peer_idx]
                pltpu.make_async_remote_copy(src_ref=src, dst_ref=outs[a].at[peer_idx], send_sem=send_sems.at[a, k - 1],
                                             recv_sem=recv_sems.at[a, k - 1], device_id=peer, device_id_type=MESH).wait_recv()
        for cp in sends:
            cp.wait_send()
        for local in started:
            local.wait()

    hbm = pl.BlockSpec(memory_space=pltpu.HBM)
    out_shape = [SDS((N_DEV,) + tuple(a.shape), a.dtype) if gather else SDS(a.shape, a.dtype) for a in arrays]
    out = pl.pallas_call(
        body, in_specs=[hbm] * n + [pl.BlockSpec(memory_space=pl.ANY)] * n_after,
        out_specs=[hbm] * n + [pl.BlockSpec(memory_space=pltpu.VMEM)], out_shape=out_shape + [SDS((8, 128), F32)],
        scratch_shapes=[pltpu.SemaphoreType.DMA((n, N_DEV - 1)), pltpu.SemaphoreType.DMA((n, N_DEV - 1)),
                        pltpu.SemaphoreType.DMA((n,))],
        name=name)(*arrays, *after)
    return list(out[:n]), out[n]


_HBM = pl.BlockSpec(memory_space=pltpu.HBM)
_SEM = pl.BlockSpec(memory_space=pltpu.SEMAPHORE)
_EFFECT = pltpu.SideEffectType.DATAFLOW_SIDE_EFFECTING


def _land_shape(a, gather):
    return (N_DEV,) + tuple(a.shape) if gather else tuple(a.shape)


def _place_own(arrays, gather, name, slot=_slot8):
    n = len(arrays)
    me = slot(_me()[:3])

    def body(me_ref, *refs):
        for a in range(n):
            refs[n + a][...] = refs[a][...]

    def own_slot(a):
        zeros = (0,) * (a.ndim - (0 if gather else 1))
        return lambda i, me_ref: (me_ref[0],) + zeros

    def whole(a):
        return lambda i, me_ref: (0,) * a.ndim

    in_specs = [pl.BlockSpec(a.shape, whole(a)) if gather else pl.BlockSpec((None,) + a.shape[1:], own_slot(a)) for a in arrays]
    out_specs = [pl.BlockSpec((None,) + (a.shape if gather else a.shape[1:]), own_slot(a)) for a in arrays]
    return pl.pallas_call(
        body, grid_spec=pltpu.PrefetchScalarGridSpec(num_scalar_prefetch=1, grid=(1,), in_specs=in_specs, out_specs=out_specs),
        out_shape=[SDS(_land_shape(a, gather), a.dtype) for a in arrays], name=name,
        compiler_params=_params(("arbitrary",)))(me.reshape(1).astype(jnp.int32), *arrays)


def _exchange_start(arrays, lands, gather, name, peers=ALL_PEERS, slot=_slot8):
    n = len(arrays)

    def body(*refs):
        ins, lnd = refs[:n], refs[n:2 * n]
        send_sems, recv_sems = refs[2 * n], refs[2 * n + 1]
        token = refs[-1]
        me = slot(_me()[:3])
        for i, k in enumerate(peers):
            peer, _ = _peer(k)
            for a in range(n):
                src = ins[a] if gather else ins[a].at[slot(peer)]
                s = a * len(peers) + i
                pltpu.make_async_remote_copy(src_ref=src, dst_ref=lnd[a].at[me], send_sem=send_sems.at[s],
                                             recv_sem=recv_sems.at[s], device_id=peer, device_id_type=MESH).start()
        token[...] = jnp.zeros_like(token)

    sems = pltpu.SemaphoreType.DMA((n * len(peers),))
    out = pl.pallas_call(
        body, name=name, in_specs=[_HBM] * (2 * n),
        out_shape=(sems, sems) + tuple(pltpu.HBM(a.shape, a.dtype) for a in list(arrays) + list(lands)) + (SDS((8, 128), F32),),
        out_specs=(_SEM, _SEM) + (_HBM,) * (2 * n) + (pl.BlockSpec(memory_space=pltpu.VMEM),),
        input_output_aliases={i: 2 + i for i in range(2 * n)},
        compiler_params=pltpu.CompilerParams(has_side_effects=_EFFECT),
    )(*[pltpu.with_memory_space_constraint(a, pltpu.HBM) for a in list(arrays) + list(lands)])
    return out[0], out[1], list(out[2:2 + n]), list(out[2 + n:2 + 2 * n]), out[-1]


def _exchange_wait(send_sems, recv_sems, arrays, lands, after, gather, name, peers=ALL_PEERS, slot=_slot8):
    n = len(arrays)

    def body(*refs):
        ins, lnd = refs[:n], refs[n:2 * n]
        send_sems, recv_sems = refs[2 * n], refs[2 * n + 1]
        for i, k in enumerate(peers):
            peer, _ = _peer(k)
            for a in range(n):
                src = ins[a] if gather else ins[a].at[slot(peer)]
                s = a * len(peers) + i
                cp = pltpu.make_async_remote_copy(src_ref=src, dst_ref=lnd[a].at[slot(peer)], send_sem=send_sems.at[s],
                                                  recv_sem=recv_sems.at[s], device_id=peer, device_id_type=MESH)
                cp.wait_send()
                cp.wait_recv()

        refs[-1][...] = jnp.zeros_like(refs[-1])

    after = list(after) if isinstance(after, (list, tuple)) else [after]
    out = pl.pallas_call(
        body, name=name, in_specs=[_HBM] * (2 * n) + [_SEM, _SEM] + [pl.BlockSpec(memory_space=pl.ANY)] * len(after),
        out_shape=tuple(pltpu.HBM(a.shape, a.dtype) for a in list(arrays) + list(lands)) + (SDS((8, 128), F32),),
        out_specs=(_HBM,) * (2 * n) + (pl.BlockSpec(memory_space=pltpu.VMEM),), input_output_aliases={i: i for i in range(2 * n)},
        compiler_params=pltpu.CompilerParams(has_side_effects=_EFFECT),
    )(*arrays, *lands, send_sems, recv_sems, *after)
    return list(out[n:2 * n]), out[-1]


def _forward_sibling(lands, name):
    n = len(lands)

    def body(*refs):
        outs = refs[n:2 * n]
        send_sems, recv_sems = refs[2 * n:]
        x, y, c, _ = _me()
        sends = []
        for i, k in enumerate(CHIP_PEERS):
            peer, _ = _peer(k)
            for a in range(n):
                rows = outs[a].at[_slot8(peer)]
                cp = pltpu.make_async_remote_copy(src_ref=rows, dst_ref=rows, send_sem=send_sems.at[a, i], recv_sem=recv_sems.at[a, i],
                                                  device_id=(x, y, 1 - c), device_id_type=MESH)
                cp.start()
                sends.append(cp)
        for i, k in enumerate(CHIP_PEERS):
            (px, py, pc), _ = _peer(k)
            for a in range(n):
                rows = outs[a].at[_slot8((px, py, 1 - pc))]
                pltpu.make_async_remote_copy(src_ref=rows, dst_ref=rows, send_sem=send_sems.at[a, i], recv_sem=recv_sems.at[a, i],
                                             device_id=(x, y, 1 - c), device_id_type=MESH).wait_recv()
        for cp in sends:
            cp.wait_send()

    return pl.pallas_call(
        body, in_specs=[_HBM] * n, out_specs=[_HBM] * n, out_shape=[SDS(a.shape, a.dtype) for a in lands],
        input_output_aliases={i: i for i in range(n)},
        scratch_shapes=[pltpu.SemaphoreType.DMA((n, len(CHIP_PEERS))), pltpu.SemaphoreType.DMA((n, len(CHIP_PEERS)))],
        name=name)(*lands)


def _swap_sibling(arrays, name, after=()):
    n, n_after = len(arrays), len(after)
    chips = N_DEV // 2

    def body(*refs):
        ins, outs = refs[:n], refs[n + n_after:2 * n + n_after]
        send_sems, recv_sems = refs[2 * n + n_after:]
        x, y, c, _ = _me()
        sends = []
        for q in range(chips):
            for a in range(n):
                cp = pltpu.make_async_remote_copy(src_ref=ins[a].at[q, 1 - c], dst_ref=outs[a].at[q], send_sem=send_sems.at[a, q],
                                                  recv_sem=recv_sems.at[a, q], device_id=(x, y, 1 - c), device_id_type=MESH)
                cp.start()
                sends.append(cp)
        for cp in sends:
            cp.wait_recv()
        for cp in sends:
            cp.wait_send()

    return pl.pallas_call(
        body, in_specs=[_HBM] * n + [pl.BlockSpec(memory_space=pl.ANY)] * n_after, out_specs=[_HBM] * n,
        out_shape=[SDS((chips,) + a.shape[2:], a.dtype) for a in arrays],
        scratch_shapes=[pltpu.SemaphoreType.DMA((n, chips)), pltpu.SemaphoreType.DMA((n, chips))], name=name)(*arrays, *after)


def _sum_pairs(mine, theirs, name):
    chips, _, rows, cols = mine.shape
    c = lax.axis_index("c")

    def body(c_ref, a_ref, b_ref, o_ref):
        o_ref[...] = (a_ref[...].astype(F32) + b_ref[...].astype(F32)).astype(o_ref.dtype)

    return pl.pallas_call(
        body, grid_spec=pltpu.PrefetchScalarGridSpec(
            num_scalar_prefetch=1, grid=(chips,),
            in_specs=[pl.BlockSpec((None, None, rows, cols), lambda q, c_ref: (q, c_ref[0], 0, 0)),
                      pl.BlockSpec((None, rows, cols), lambda q, c_ref: (q, 0, 0))],
            out_specs=pl.BlockSpec((None, rows, cols), lambda q, c_ref: (q, 0, 0))),
        out_shape=SDS((chips, rows, cols), mine.dtype), name=name,
        compiler_params=_params(("parallel",)))(c.reshape(1).astype(jnp.int32), mine, theirs)


def _adamw_math(w, g, m, v):
    m = ADAM_B1 * m + (1.0 - ADAM_B1) * g
    v = ADAM_B2 * v + (1.0 - ADAM_B2) * (g * g)
    m_hat = m / (1.0 - ADAM_B1 ** ADAM_STEP)
    v_hat = v / (1.0 - ADAM_B2 ** ADAM_STEP)
    delta = -ADAM_LR * (m_hat / (jnp.sqrt(v_hat) + ADAM_EPS) + ADAM_WD * w)
    return delta, m, v


def _adamw(w, m, v, pieces, name):
    rows, cols = w.shape[-2:]
    lead = w.ndim - 2
    tile = rows
    for cand in (256, 176, 128, 64, 16):
        if rows > cand and rows % cand == 0:
            tile = cand
            break

    def body(w_ref, m_ref, v_ref, p_ref, g_ref, d_ref, mo_ref, vo_ref):
        g = _sum_pieces(p_ref)
        g_ref[...] = g
        d_ref[...], mo_ref[...], vo_ref[...] = _adamw_math(w_ref[...], g, m_ref[...], v_ref[...])

    blk = pl.BlockSpec((None,) * lead + (tile, cols), lambda i: (0,) * lead + (i, 0))
    return pl.pallas_call(
        body, grid=(rows // tile,), in_specs=[blk, blk, blk, pl.BlockSpec((pieces.shape[0], tile, cols), lambda i: (0, i, 0))],
        out_specs=[blk] * 4, out_shape=[SDS(w.shape, F32)] * 4, name=name,
        compiler_params=_params(("parallel",)))(w, m, v, pieces)


def _sum_pieces(p_ref):
    g = p_ref[0].astype(F32)
    for p in range(1, p_ref.shape[0]):
        g = g + p_ref[p].astype(F32)
    return g


def _adamw_s5_mat(w, m, v, g, name):
    _, ndir, groups, b, c = w.shape
    per_dir = groups // 8

    def body(w_ref, m_ref, v_ref, g_ref, d_ref, mo_ref, vo_ref):
        d_ref[...], mo_ref[...], vo_ref[...] = _adamw_math(w_ref[...], g_ref[...], m_ref[...], v_ref[...])

    blk = pl.BlockSpec((None, None, 8, b, c), lambda i: (0, i // per_dir, i % per_dir, 0, 0))
    return pl.pallas_call(
        body, grid=(ndir * per_dir,), in_specs=[blk] * 4, out_specs=[blk] * 3, out_shape=[SDS(w.shape, F32)] * 3, name=name,
        compiler_params=_params(("parallel",)))(w, m, v, g)


VEC_ROWS = ['ffn1_pre_g', 'ffn1_post_g', 'mix_pre_g', 'mix_post_g', 'ffn2_pre_g', 'ffn2_post_g', 'final_g',
            ('na_out_g', 's5_out_g'), ('s5_d', 's5_b_glu')]
VEC_NAMES = [n for row in VEC_ROWS for n in ((row,) if isinstance(row, str) else row)]
VEC_PACK_ROWS = 16
LOSS_ROW = len(VEC_ROWS)


def _pack_vectors(grads, loss8):
    def body(*refs):
        o_ref = refs[-1]
        o_ref[...] = jnp.zeros_like(o_ref)
        o_ref[LOSS_ROW:LOSS_ROW + 1, 0:128] = refs[-2][0:1, :]
        k = 0
        for i, row in enumerate(VEC_ROWS):
            if isinstance(row, str):
                o_ref[i:i + 1, :] = refs[k][...]
                k += 1
            else:
                o_ref[i:i + 1, 0:NA_WIDTH] = refs[k][...]
                o_ref[i:i + 1, NA_WIDTH:] = refs[k + 1][...]
                k += 2

    return pl.pallas_call(body, out_shape=SDS((VEC_PACK_ROWS, D_MODEL), F32), name="pack_vectors",
                          compiler_params=_params())(*[grads[n] for n in VEC_NAMES], loss8)


def _sum8(pieces, name):
    def body(p_ref, o_ref):
        o_ref[...] = _sum_pieces(p_ref)

    return pl.pallas_call(body, out_shape=SDS(pieces.shape[1:], F32), name=name, compiler_params=_params())(pieces)


def _adamw_small(packed8, vec_wmv, others):
    n_vec, n_oth = len(VEC_NAMES), len(others)

    def body(*refs):
        p_ref = refs[0]
        ins = refs[1:1 + 3 * n_vec + 4 * n_oth]
        outs = refs[1 + 3 * n_vec + 4 * n_oth:]
        gsum = _sum_pieces(p_ref)
        outs[-1][...] = gsum[LOSS_ROW:LOSS_ROW + 1, 0:128]
        k = 0
        for i, row in enumerate(VEC_ROWS):
            parts = [(row, gsum[i:i + 1, :])] if isinstance(row, str) else \
                [(row[0], gsum[i:i + 1, 0:NA_WIDTH]), (row[1], gsum[i:i + 1, NA_WIDTH:])]
            for _, g in parts:
                w_ref, m_ref, v_ref = ins[3 * k:3 * k + 3]
                outs[4 * k][...] = g
                outs[4 * k + 1][...], outs[4 * k + 2][...], outs[4 * k + 3][...] = _adamw_math(w_ref[...], g, m_ref[...], v_ref[...])
                k += 1
        for j in range(n_oth):
            w_ref, m_ref, v_ref, g_ref = ins[3 * n_vec + 4 * j:3 * n_vec + 4 * j + 4]
            g = _sum_pieces(g_ref)
            g = g[tuple(slice(0, s) for s in w_ref.shape[1:])].reshape(w_ref.shape)
            o = outs[4 * (n_vec + j):4 * (n_vec + j) + 4]
            o[0][...] = g
            o[1][...], o[2][...], o[3][...] = _adamw_math(w_ref[...], g, m_ref[...], v_ref[...])

    args, out_shape = [packed8], []
    for w, m, v in vec_wmv:
        args += [w, m, v]
        out_shape += [SDS(w.shape, F32)] * 4
    for w, m, v, g in others:
        args += [w, m, v, g]
        out_shape += [SDS(w.shape, F32)] * 4
    out_shape += [SDS((1, 128), F32)]
    return pl.pallas_call(body, out_shape=out_shape, name="adamw_small", compiler_params=_params())(*args)


def _perm_rows(x):
    return x.reshape(SCAN_BLOCKS, SCAN_T, x.shape[-1]).transpose(1, 0, 2).reshape(SEQ, x.shape[-1])


def _unperm_rows(x):
    return x.reshape(SCAN_T, SCAN_BLOCKS, x.shape[-1]).transpose(1, 0, 2).reshape(SEQ, x.shape[-1])


def _block_diag(x):
    eye = np.eye(8, dtype=bool)[None, None, :, None, :, None]
    full = jnp.where(eye, x[:, :, :, :, None, :], 0.0)
    return full.reshape(2, S5_CHUNKS, 8 * x.shape[3], 8 * x.shape[4])


STORED_SWAPPED = {"ffn1_w_gate": (1, 2), "ffn1_w_up": (1, 2), "ffn2_w_gate": (1, 2), "ffn2_w_up": (1, 2),
                  "s5_b_re": (3, 4), "s5_b_im": (3, 4)}


def _stored(name, x):
    return jnp.swapaxes(x, *STORED_SWAPPED[name]) if name in STORED_SWAPPED else x


def _dep(x, token):
    return x if token is None else x + token


def _local_step(x, target, get_w, small, emit):
    bias = _rpb_expand(small["na_rpb"][0])
    lr = small["s5_lam_re"].reshape(64, S5_STATE)
    li = small["s5_lam_im"].reshape(64, S5_STATE)
    logdt = small["s5_log_dt"].reshape(64, 1)
    b_t = [_stored(n, small[n]).reshape(64, S5_GROUP, S5_STATE) for n in ("s5_b_re", "s5_b_im")]
    lbr, lbi, bbr, bbi = _s5_prep(lr, li, logdt, b_t[0], b_t[1])
    are = lbr.reshape(2, S5_CHUNKS, 1, ST_W)
    aim = lbi.reshape(2, S5_CHUNKS, 1, ST_W)
    bre = _block_diag(bbr.reshape(2, S5_CHUNKS, 8, S5_GROUP, S5_STATE)).astype(BF16)
    bim = _block_diag(bbi.reshape(2, S5_CHUNKS, 8, S5_GROUP, S5_STATE)).astype(BF16)
    c_t = [small[n].reshape(2, S5_CHUNKS, 8, S5_GROUP, S5_STATE).transpose(0, 1, 2, 4, 3) for n in ("s5_c_re", "s5_c_im")]
    cre = _block_diag(c_t[0]).astype(BF16)
    cim = _block_diag(c_t[1]).astype(BF16)
    tgt = jnp.concatenate([jnp.zeros((N_META, D_MODEL), F32), target], axis=0)

    h0, a1 = _embed_prenorm(get_w("meta", None)["meta_tokens"], x, small["ffn1_pre_g"])
    wts = dict(get_w("ffn1", [bias, are, aim, bre, bim, cre, cim, tgt, a1]))
    gate1, up1, f1 = _ffn_fwd(a1, wts["ffn1_w_gate"], wts["ffn1_w_up"], wts["ffn1_w_down"], "ffn1_fwd",
                              after=wts.get("tokens", ()))
    h1, a2 = _post_pre(f1, h0, small["ffn1_post_g"], small["mix_pre_g"], 0.5, "post_pre1")
    wts.update(get_w("w_in", a2))
    qkv = _proj_heads(a2, wts["w_in"])
    u = _proj_u(a2, wts["w_in"])
    ona = _na_fwd(qkv, bias)
    u_p = _perm_rows(u)
    sr, si, y2 = _s5_scan_fwd(u_p, bre, bim, are, aim, cre, cim)
    wts.update(get_w("mix", y2))
    os5_p, ypre_p = _s5_glu_fwd(u_p, y2, small["s5_d"], wts["s5_w_glu"], small["s5_b_glu"])
    os5 = _unperm_rows(os5_p)

    mix = _mix_out_fwd(ona, os5, small["na_out_g"], small["s5_out_g"], wts["w_out"])
    h2, a3 = _post_pre(mix, h1, small["mix_post_g"], small["ffn2_pre_g"], 1.0, "post_pre2")
    wts.update(get_w("ffn2", a3))
    gate2, up2, f2 = _ffn_fwd(a3, wts["ffn2_w_gate"], wts["ffn2_w_up"], wts["ffn2_w_down"], "ffn2_fwd")
    loss8, dh3, df2, g_final, g_ffn2_post = _final_loss(f2, h2, small["ffn2_post_g"], small["final_g"], tgt)

    da3, dwg2, dwu2, dwd2 = _ffn_bwd(df2, a3, gate2, up2, wts["ffn2_w_gate"], wts["ffn2_w_up"], wts["ffn2_w_down"], "ffn2_bwd")
    tok = emit("ffn2", {"ffn2_w_gate": dwg2, "ffn2_w_up": dwu2, "ffn2_w_down": dwd2})
    dh2, dmix, g_ffn2_pre, g_mix_post = _bwd_pre_post(da3, h2, _dep(small["ffn2_pre_g"], tok), dh3, mix, small["mix_post_g"], 1.0,
                                                      "bwd_pre_post2")
    dona, dos5, dwout, g_na_out, g_s5_out = _mix_out_bwd(dmix, ona, os5, small["na_out_g"], small["s5_out_g"], wts["w_out"])

    dypre_p, du_skip_p, dwglu, g_b_glu, g_s5_d = _s5_glu_bwd(_perm_rows(dos5), ypre_p, u_p, small["s5_d"], wts["s5_w_glu"],
                                                             small["s5_b_glu"])
    tok = emit("mix", {"s5_w_glu": dwglu.reshape(N_DEV, S5_WIDTH // N_DEV, S5_WIDTH).astype(BF16),
                       "w_out": dwout.reshape(N_DEV, D_MODEL // N_DEV, D_MODEL).astype(BF16)})
    du_p, dbr, dbi, dcr, dci, dar, dai = _s5_scan_bwd(dypre_p, du_skip_p, u_p, sr, si, bre, bim, _dep(are, tok), aim, cre, cim)
    du = _unperm_rows(du_p)
    per_group = (2 * S5_GROUPS, S5_GROUP, S5_STATE)
    g_lr, g_li, g_dt, g_br, g_bi = _s5_prep_bwd(lr, li, logdt, b_t[0], b_t[1], dar.reshape(64, S5_STATE),
                                                dai.reshape(64, S5_STATE), dbr.reshape(per_group), dbi.reshape(per_group))
    g_c = [dcr.reshape(per_group), dci.reshape(per_group)]

    dq, dk, dv, dbias = _na_bwd(qkv, bias, dona)
    g_rpb = _rpb_reduce(dbias)
    dense = jnp.stack([g.reshape(2 * S5_GROUPS, S5_STATE * S5_GROUP) for g in (g_br, g_bi, *g_c)])
    tok = emit("small", {"dense": dense, "na_rpb": g_rpb,
                         "s5_lam_re": g_lr.reshape(2, S5_GROUPS, S5_STATE), "s5_lam_im": g_li.reshape(2, S5_GROUPS, S5_STATE),
                         "s5_log_dt": g_dt.reshape(2, S5_GROUPS)})
    da2, dwin = _proj_bwd(dq, dk, dv, du, a2, wts["w_in"])
    tok2 = emit("w_in", {"w_in": dwin})
    tok = tok if tok2 is None else tok + tok2
    dh1, df1, g_mix_pre, g_ffn1_post = _bwd_pre_post(da2, h1, _dep(small["mix_pre_g"], tok), dh2, f1, small["ffn1_post_g"], 0.5,
                                                     "bwd_pre_post1")
    da1, dwg1, dwu1, dwd1 = _ffn_bwd(df1, a1, gate1, up1, wts["ffn1_w_gate"], wts["ffn1_w_up"], wts["ffn1_w_down"], "ffn1_bwd")
    grad_x, grad_meta, g_ffn1_pre = _bwd_embed(da1, h0, small["ffn1_pre_g"], dh1)
    vec_g = {
        "ffn1_pre_g": g_ffn1_pre, "ffn1_post_g": g_ffn1_post, "mix_pre_g": g_mix_pre, "s5_d": g_s5_d, "s5_b_glu": g_b_glu,
        "na_out_g": g_na_out, "s5_out_g": g_s5_out, "mix_post_g": g_mix_post,
        "ffn2_pre_g": g_ffn2_pre, "ffn2_post_g": g_ffn2_post, "final_g": g_final,
    }
    emit("vec", {"packed": _pack_vectors(vec_g, loss8), "meta_tokens": grad_meta})
    emit("ffn1", {"ffn1_w_gate": dwg1, "ffn1_w_up": dwu1, "ffn1_w_down": dwd1})
    return grad_x


WEIGHT_NAMES = ['meta_tokens', 'ffn1_pre_g', 'ffn1_post_g', 'ffn1_w_gate', 'ffn1_w_up', 'ffn1_w_down', 'mix_pre_g', 'w_in',
                'na_rpb', 's5_lam_re', 's5_lam_im', 's5_log_dt', 's5_b_re', 's5_b_im', 's5_c_re', 's5_c_im', 's5_d',
                's5_w_glu', 's5_b_glu', 'na_out_g', 's5_out_g', 'w_out', 'mix_post_g', 'ffn2_pre_g', 'ffn2_post_g',
                'ffn2_w_gate', 'ffn2_w_up', 'ffn2_w_down', 'final_g']
BIG_NAMES = ['ffn1_w_gate', 'ffn1_w_up', 'ffn1_w_down', 'w_in', 's5_w_glu', 'w_out', 'ffn2_w_gate', 'ffn2_w_up', 'ffn2_w_down']
SMALL_NAMES = [n for n in WEIGHT_NAMES if n not in BIG_NAMES and n != 'meta_tokens']
WHOLE_NAMES = ['na_rpb', 's5_lam_re', 's5_lam_im', 's5_log_dt']
LEAD_NAMES = ['s5_b_re', 's5_b_im', 's5_c_re', 's5_c_im']


def kernel(x, meta_tokens, ffn1_pre_g, ffn1_post_g, ffn1_w_gate, ffn1_w_up, ffn1_w_down, mix_pre_g, w_in, na_rpb, s5_lam_re, s5_lam_im, s5_log_dt, s5_b_re, s5_b_im, s5_c_re, s5_c_im, s5_d, s5_w_glu, s5_b_glu, na_out_g, s5_out_g, w_out, mix_post_g, ffn2_pre_g, ffn2_post_g, ffn2_w_gate, ffn2_w_up, ffn2_w_down, final_g, loss_target, m_meta_tokens, m_ffn1_pre_g, m_ffn1_post_g, m_ffn1_w_gate, m_ffn1_w_up, m_ffn1_w_down, m_mix_pre_g, m_w_in, m_na_rpb, m_s5_lam_re, m_s5_lam_im, m_s5_log_dt, m_s5_b_re, m_s5_b_im, m_s5_c_re, m_s5_c_im, m_s5_d, m_s5_w_glu, m_s5_b_glu, m_na_out_g, m_s5_out_g, m_w_out, m_mix_post_g, m_ffn2_pre_g, m_ffn2_post_g, m_ffn2_w_gate, m_ffn2_w_up, m_ffn2_w_down, m_final_g, v_meta_tokens, v_ffn1_pre_g, v_ffn1_post_g, v_ffn1_w_gate, v_ffn1_w_up, v_ffn1_w_down, v_mix_pre_g, v_w_in, v_na_rpb, v_s5_lam_re, v_s5_lam_im, v_s5_log_dt, v_s5_b_re, v_s5_b_im, v_s5_c_re, v_s5_c_im, v_s5_d, v_s5_w_glu, v_s5_b_glu, v_na_out_g, v_s5_out_g, v_w_out, v_mix_post_g, v_ffn2_pre_g, v_ffn2_post_g, v_ffn2_w_gate, v_ffn2_w_up, v_ffn2_w_down, v_final_g):
    args = dict(locals())
    w = {n: args[n] for n in WEIGHT_NAMES}
    m = {n: args["m_" + n] for n in WEIGHT_NAMES}
    v = {n: args["v_" + n] for n in WEIGHT_NAMES}

    small = {n: w[n] for n in SMALL_NAMES}

    pending = {}

    def start(group, names, arrays, gather, peers=ALL_PEERS, slot=_slot8):
        lands = _place_own(arrays, gather, "own_" + group, slot)
        send_sems, recv_sems, arrays, lands, token = _exchange_start(arrays, lands, gather, "start_" + group, peers, slot)
        pending[group] = (names, send_sems, recv_sems, arrays, lands, gather, peers, slot)
        return token

    def finish(group, after):
        names, send_sems, recv_sems, arrays, lands, gather, peers, slot = pending.pop(group)
        lands, token = _exchange_wait(send_sems, recv_sems, arrays, lands, after, gather, "wait_" + group, peers, slot)
        return dict(zip(names, lands)), token

    first = ["ffn1_w_gate", "ffn1_w_up", "ffn1_w_down"]
    def shard(n, token=None):
        return _dep(_stored(n, w[n])[0], None if token is None else token[0, 0]).astype(BF16)

    ffn_names = ("ffn1_w_gate", "ffn1_w_up", "ffn1_w_down", "ffn2_w_gate", "ffn2_w_up", "ffn2_w_down")
    later_groups = (("w_in", ["w_in"]), ("mix", ["s5_w_glu", "w_out"]), ("ffn2", ["ffn2_w_gate", "ffn2_w_up", "ffn2_w_down"]))
    (meta_full,), token0 = _exchange([w["meta_tokens"]], True, "gather_meta")
    token1 = start("ffn1", first, [shard(n, token0) for n in first], True, (SIBLING,) + CHIP_PEERS)
    meta_full = _dep(meta_full.transpose(1, 0, 2).reshape(N_META, D_MODEL), token1[0, 0])
    later_shards = {n: shard(n, token1) for _, names in later_groups for n in names}
    for n in ("na_rpb", "s5_lam_re"):
        small[n] = _dep(small[n], token1[0, 0])

    def get_w(group, after):
        if group == "meta":
            return {"meta_tokens": meta_full}
        if group == "ffn1":
            after = list(after) + list(later_shards.values())
        got, token = finish(group, after)
        if group == "ffn1":
            got = dict(zip(got, _forward_sibling(list(got.values()), "forward_ffn1")))
            got["tokens"] = [start(g, names + ["order"], [later_shards[n] for n in names] + [token], True) for g, names in later_groups]
        if group == "mix":
            got = {"s5_w_glu": got["s5_w_glu"].reshape(S5_WIDTH, S5_WIDTH), "w_out": got["w_out"].reshape(D_MODEL, D_MODEL)}
        return {n: (a.reshape(D_FF, D_MODEL) if n in ffn_names else a) for n, a in got.items()}

    tokens = {}

    def emit(group, grads):
        grads = {n: (g.reshape(N_DEV, FF_SHARD, D_MODEL) if n in ffn_names else g) for n, g in grads.items()}
        if group == "ffn1":
            mine = [g.reshape((N_DEV // 2, 2) + g.shape[1:]) for g in grads.values()]
            theirs = _swap_sibling(mine, "swap_g_ffn1", after=[tokens["vec"]])
            sums = [_sum_pairs(a, b, "pair_sum_" + n) for n, a, b in zip(grads, mine, theirs)]
            tokens[group] = start("g_ffn1", list(grads), sums, False, CHIP_PEERS, _slot4)
        else:
            tokens[group] = start("g_" + group, list(grads), list(grads.values()), group in ("small", "vec"))
        return tokens[group][0, 0]

    grad_x = _local_step(x[0], loss_target[0], get_w, small, emit)
    res = {}

    def update_shard(n, pieces):
        outs = _adamw(_stored(n, w[n]), _stored(n, m[n]), _stored(n, v[n]), pieces, "adamw_" + n)
        res[n] = [_stored(n, o) for o in outs]

    late = [grad_x, tokens["ffn1"]]
    for group in ("g_ffn2", "g_mix", "g_w_in"):
        for n, pieces in finish(group, late)[0].items():
            update_shard(n, pieces)
    g8 = finish("g_small", late)[0]
    dense = _sum8(g8["dense"], "sum_dense")
    for i, n in enumerate(LEAD_NAMES):
        g = dense[i].reshape(_stored(n, w[n]).shape)
        upd = _adamw_s5_mat(_stored(n, w[n]), _stored(n, m[n]), _stored(n, v[n]), g, "adamw_" + n)
        res[n] = [_stored(n, o) for o in [g] + list(upd)]

    done = [res[n][1] for n in ("ffn2_w_gate", "ffn2_w_up", "ffn2_w_down", "w_in", "w_out", "s5_w_glu") + tuple(LEAD_NAMES)]
    got = finish("g_vec", done)[0]
    packed8, gmeta8 = got["packed"], got["meta_tokens"]
    for n, pieces in finish("g_ffn1", packed8)[0].items():
        update_shard(n, pieces)
    _, _, _, me = _me()
    update_shard("meta_tokens", lax.dynamic_slice_in_dim(gmeta8, me * (D_MODEL // N_DEV), D_MODEL // N_DEV, axis=2))

    outs = _adamw_small(packed8, [(w[n], m[n], v[n]) for n in VEC_NAMES], [(w[n], m[n], v[n], g8[n]) for n in WHOLE_NAMES])
    for i, n in enumerate(VEC_NAMES + WHOLE_NAMES):
        res[n] = list(outs[4 * i:4 * i + 4])

    out = [outs[-1][0, 0], grad_x[None]]
    for kind in range(4):
        out += [res[n][kind] for n in WEIGHT_NAMES]
    return tuple(out)
```

```python
import math

import numpy as np
import jax
import jax.numpy as jnp
from jax import lax
from jax.experimental import pallas as pl
from jax.experimental.pallas import tpu as pltpu

F32 = jnp.float32
BF16 = jnp.bfloat16
SDS = jax.ShapeDtypeStruct

D_MODEL = 1024
N_TOK = 2048
N_META = 16
SEQ = N_TOK + N_META
ROW_TILE = 688
N_ROW_TILES = SEQ // ROW_TILE
N_DEV = 8
D_FF = 2816
FF_SHARD = D_FF // N_DEV
FF_TILE = 256
IN_SHARD = 256
NA_WIDTH = 512
S5_WIDTH = 512
HEADS = 8
HEAD_DIM = 64
GRID_W = 64
GRID_ROWS = N_TOK // GRID_W
KH = 8
KW = 16
NA_RB = 4
NA_KR = KH + NA_RB - 1
NA_BLOCKS = GRID_ROWS // NA_RB
NA_QB = NA_RB * GRID_W
NA_KB = NA_KR * GRID_W
NA_TYPES = 3
S5_GROUPS = 32
S5_GROUP = 16
S5_STATE = 64
S5_CHUNKS = 4
CH_W = S5_WIDTH // S5_CHUNKS
ST_W = S5_GROUPS * S5_STATE // S5_CHUNKS
SCAN_BLOCKS = 8
SCAN_T = SEQ // SCAN_BLOCKS
RMS_EPS = 1e-6
NEG_INF = -1e30
ATT_SCALE = HEAD_DIM ** -0.5
ADAM_LR, ADAM_B1, ADAM_B2, ADAM_EPS, ADAM_WD, ADAM_STEP = 0.001, 0.9, 0.999, 1e-08, 0.01, 10
VMEM_LIMIT = 56 * 1024 * 1024
MESH = pl.DeviceIdType.MESH


def _params(sem=None):
    return pltpu.CompilerParams(dimension_semantics=sem, vmem_limit_bytes=VMEM_LIMIT)


def _dot(a, b):
    return jnp.dot(a, b, preferred_element_type=F32)


def _dot_nt(a, b):
    return lax.dot_general(a, b, (((1,), (1,)), ((), ())), preferred_element_type=F32)


def _dot_tn(a, b):
    return lax.dot_general(a, b, (((0,), (0,)), ((), ())), preferred_element_type=F32)


def _rstd(x):
    return lax.rsqrt(jnp.mean(x * x, axis=-1, keepdims=True) + RMS_EPS)


def _rms_bwd(x, r, g, dy):
    dyg = dy * g
    xr = x * r
    dx = r * (dyg - xr * jnp.mean(dyg * xr, axis=-1, keepdims=True))
    return dx, dy * xr


def _rows(i, size=ROW_TILE):
    return pl.ds(pl.multiple_of(i * size, 16), size)


def _row_spec(width):
    return pl.BlockSpec((ROW_TILE, width), lambda i: (i, 0))


def _fix_spec(shape):
    return pl.BlockSpec(shape, lambda i: (0,) * len(shape))


def _split3(x):
    hi = x.astype(BF16)
    r1 = x - hi.astype(F32)
    mid = r1.astype(BF16)
    lo = (r1 - mid.astype(F32)).astype(BF16)
    return hi, mid, lo


def _embed_prenorm(meta, x, g):
    def body(m_ref, x_ref, g_ref, h_ref, a_ref):
        h_ref[0:N_META, :] = m_ref[...]
        h_ref[N_META:, :] = x_ref[...]
        for i in range(N_ROW_TILES):
            rows = slice(i * ROW_TILE, (i + 1) * ROW_TILE)
            hv = h_ref[rows, :]
            a_ref[rows, :] = (hv * _rstd(hv) * g_ref[...]).astype(BF16)

    return pl.pallas_call(
        body, out_shape=[SDS((SEQ, D_MODEL), F32), SDS((SEQ, D_MODEL), BF16)], name="embed_prenorm",
        compiler_params=_params())(meta, x, g)


def _post_pre(f, hres, g_post, g_next, scale, name):
    def body(f_ref, h_ref, gp_ref, gn_ref, ho_ref, a_ref):
        fv = f_ref[...]
        h = h_ref[...] + scale * (fv * _rstd(fv) * gp_ref[...])
        ho_ref[...] = h
        a_ref[...] = (h * _rstd(h) * gn_ref[...]).astype(BF16)

    return pl.pallas_call(
        body, grid=(N_ROW_TILES,),
        in_specs=[_row_spec(D_MODEL), _row_spec(D_MODEL), _fix_spec((1, D_MODEL)), _fix_spec((1, D_MODEL))],
        out_specs=[_row_spec(D_MODEL), _row_spec(D_MODEL)],
        out_shape=[SDS((SEQ, D_MODEL), F32), SDS((SEQ, D_MODEL), BF16)], name=name,
        compiler_params=_params(("parallel",)))(f, hres, g_post, g_next)


def _final_loss(f2, h2, g_post, g_final, target):
    def body(f_ref, h_ref, gp_ref, gf_ref, t_ref, loss_ref, dh_ref, df_ref, dgf_ref, dgp_ref):
        i = pl.program_id(0)
        fv = f_ref[...]
        r1 = _rstd(fv)
        gp = gp_ref[...]
        h3 = h_ref[...] + 0.5 * (fv * r1 * gp)
        r2 = _rstd(h3)
        gf = gf_ref[...]
        y = h3 * r2 * gf
        row = lax.broadcasted_iota(jnp.int32, (ROW_TILE, 1), 0) + i * ROW_TILE
        err = jnp.where(row >= N_META, y - t_ref[...], 0.0)
        part = 0.5 * jnp.sum(jnp.mean(err * err, axis=-1, keepdims=True))
        dy = err * (1.0 / D_MODEL)
        dh3, dgf = _rms_bwd(h3, r2, gf, dy)
        dh_ref[...] = dh3
        df, dgp = _rms_bwd(fv, r1, gp, 0.5 * dh3)
        df_ref[...] = df.astype(BF16)

        @pl.when(i == 0)
        def _():
            loss_ref[...] = jnp.zeros_like(loss_ref)
            dgf_ref[...] = jnp.zeros_like(dgf_ref)
            dgp_ref[...] = jnp.zeros_like(dgp_ref)

        loss_ref[...] += part
        dgf_ref[...] += jnp.sum(dgf, axis=0, keepdims=True)
        dgp_ref[...] += jnp.sum(dgp, axis=0, keepdims=True)

    gain = _fix_spec((1, D_MODEL))
    return pl.pallas_call(
        body, grid=(N_ROW_TILES,),
        in_specs=[_row_spec(D_MODEL), _row_spec(D_MODEL), gain, gain, _row_spec(D_MODEL)],
        out_specs=[_fix_spec((8, 128)), _row_spec(D_MODEL), _row_spec(D_MODEL), gain, gain],
        out_shape=[SDS((8, 128), F32), SDS((SEQ, D_MODEL), F32), SDS((SEQ, D_MODEL), BF16),
                   SDS((1, D_MODEL), F32), SDS((1, D_MODEL), F32)],
        name="final_loss", compiler_params=_params(("arbitrary",)))(f2, h2, g_post, g_final, target)


def _bwd_pre_post(da, h, g_pre, dh_res, fprev, g_post, scale, name):
    def body(da_ref, h_ref, gpre_ref, dhr_ref, f_ref, gpost_ref, dh_ref, df_ref, dgpre_ref, dgpost_ref):
        i = pl.program_id(0)
        hv = h_ref[...]
        dxa, dgpre = _rms_bwd(hv, _rstd(hv), gpre_ref[...], da_ref[...])
        dh = dhr_ref[...] + dxa
        dh_ref[...] = dh
        fv = f_ref[...]
        df, dgpost = _rms_bwd(fv, _rstd(fv), gpost_ref[...], scale * dh)
        df_ref[...] = df.astype(BF16)

        @pl.when(i == 0)
        def _():
            dgpre_ref[...] = jnp.zeros_like(dgpre_ref)
            dgpost_ref[...] = jnp.zeros_like(dgpost_ref)

        dgpre_ref[...] += jnp.sum(dgpre, axis=0, keepdims=True)
        dgpost_ref[...] += jnp.sum(dgpost, axis=0, keepdims=True)

    gain = _fix_spec((1, D_MODEL))
    row = _row_spec(D_MODEL)
    return pl.pallas_call(
        body, grid=(N_ROW_TILES,), in_specs=[row, row, gain, row, row, gain],
        out_specs=[row, row, gain, gain],
        out_shape=[SDS((SEQ, D_MODEL), F32), SDS((SEQ, D_MODEL), BF16), SDS((1, D_MODEL), F32), SDS((1, D_MODEL), F32)],
        name=name, compiler_params=_params(("arbitrary",)))(da, h, g_pre, dh_res, fprev, g_post)


def _bwd_embed(da, h, g_pre, dh_res):
    def body(da_ref, h_ref, gpre_ref, dhr_ref, gx_ref, gm_ref, dgpre_ref):
        total = jnp.zeros((1, D_MODEL), F32)
        for i in range(N_ROW_TILES):
            rows = slice(i * ROW_TILE, (i + 1) * ROW_TILE)
            hv = h_ref[rows, :]
            dxa, dgpre = _rms_bwd(hv, _rstd(hv), gpre_ref[...], da_ref[rows, :])
            dh = dhr_ref[rows, :] + dxa
            total = total + jnp.sum(dgpre, axis=0, keepdims=True)
            if i == 0:
                gm_ref[...] = dh[0:N_META, :]
                gx_ref[0:ROW_TILE - N_META, :] = dh[N_META:, :]
            else:
                gx_ref[i * ROW_TILE - N_META:(i + 1) * ROW_TILE - N_META, :] = dh
        dgpre_ref[...] = total

    return pl.pallas_call(
        body, out_shape=[SDS((N_TOK, D_MODEL), F32), SDS((N_META, D_MODEL), F32), SDS((1, D_MODEL), F32)],
        name="bwd_embed", compiler_params=_params())(da, h, g_pre, dh_res)


def _ffn_fwd(a, wg, wu, wd, name, after=()):
    def body(a_ref, wg_ref, wu_ref, wd_ref, *rest):
        gate_ref, up_ref, f_ref = rest[len(after):]
        j = pl.program_id(0)

        def tile(i, carry):
            rows = _rows(i)
            at = a_ref[rows, :]
            gate = _dot_nt(at, wg_ref[...])
            up = _dot_nt(at, wu_ref[...])
            gate_ref[rows, :] = gate.astype(BF16)
            up_ref[rows, :] = up.astype(BF16)
            act = (gate * jax.nn.sigmoid(gate) * up).astype(BF16)
            contrib = _dot(act, wd_ref[...])

            @pl.when(j == 0)
            def _():
                f_ref[rows, :] = contrib

            @pl.when(j != 0)
            def _():
                f_ref[rows, :] += contrib

            return carry

        lax.fori_loop(0, N_ROW_TILES, tile, 0)

    wtile = pl.BlockSpec((FF_TILE, D_MODEL), lambda j: (j, 0))
    hid = pl.BlockSpec((SEQ, FF_TILE), lambda j: (0, j))
    full = pl.BlockSpec((SEQ, D_MODEL), lambda j: (0, 0))
    return pl.pallas_call(
        body, grid=(D_FF // FF_TILE,), in_specs=[full, wtile, wtile, wtile] + [pl.BlockSpec(memory_space=pl.ANY)] * len(after),
        out_specs=[hid, hid, full],
        out_shape=[SDS((SEQ, D_FF), BF16), SDS((SEQ, D_FF), BF16), SDS((SEQ, D_MODEL), F32)],
        name=name, compiler_params=_params(("arbitrary",)))(a, wg, wu, wd, *after)


def _ffn_bwd(df, a, gate, up, wg, wu, wd, name):
    def body(df_ref, a_ref, gate_ref, up_ref, wg_ref, wu_ref, wd_ref, da_ref, dwg_ref, dwu_ref, dwd_ref,
             acc_g, acc_u, acc_d):
        j = pl.program_id(0)

        def tile(i, carry):
            rows = _rows(i)
            dft = df_ref[rows, :]
            at = a_ref[rows, :]
            gate = gate_ref[rows, :].astype(F32)
            up = up_ref[rows, :].astype(F32)
            dact = _dot_nt(dft, wd_ref[...])
            sig = jax.nn.sigmoid(gate)
            silu = gate * sig
            dgate = (dact * up * (sig * (1.0 + gate * (1.0 - sig)))).astype(BF16)
            dup = (dact * silu).astype(BF16)
            act = (silu * up).astype(BF16)
            dwd = _dot_tn(act, dft)
            dwg = _dot_tn(dgate, at)
            dwu = _dot_tn(dup, at)
            dat = _dot(dgate, wg_ref[...]) + _dot(dup, wu_ref[...])

            @pl.when(i == 0)
            def _():
                acc_d[...] = dwd
                acc_g[...] = dwg
                acc_u[...] = dwu

            @pl.when(i != 0)
            def _():
                acc_d[...] += dwd
                acc_g[...] += dwg
                acc_u[...] += dwu

            @pl.when(j == 0)
            def _():
                da_ref[rows, :] = dat

            @pl.when(j != 0)
            def _():
                da_ref[rows, :] += dat

            return carry

        lax.fori_loop(0, N_ROW_TILES, tile, 0)
        dwg_ref[...] = acc_g[...].astype(BF16)
        dwu_ref[...] = acc_u[...].astype(BF16)
        dwd_ref[...] = acc_d[...].astype(BF16)

    wtile = pl.BlockSpec((FF_TILE, D_MODEL), lambda j: (j, 0))
    hid = pl.BlockSpec((SEQ, FF_TILE), lambda j: (0, j))
    full = pl.BlockSpec((SEQ, D_MODEL), lambda j: (0, 0))
    return pl.pallas_call(
        body, grid=(D_FF // FF_TILE,), in_specs=[full, full, hid, hid, wtile, wtile, wtile],
        out_specs=[full, wtile, wtile, wtile],
        out_shape=[SDS((SEQ, D_MODEL), F32)] + [SDS((D_FF, D_MODEL), BF16)] * 3,
        scratch_shapes=[pltpu.VMEM((FF_TILE, D_MODEL), F32)] * 3,
        name=name, compiler_params=_params(("arbitrary",)))(df, a, gate, up, wg, wu, wd)


HEADS_PER_BLOCK = IN_SHARD // HEAD_DIM
QKV_BLOCKS = 3 * NA_WIDTH // IN_SHARD


def _proj_heads(a, w):
    def body(a_ref, w_ref, o_ref):
        def tile(i, carry):
            rows = _rows(i)
            res = _dot(a_ref[rows, :], w_ref[...])
            for sub in range(HEADS_PER_BLOCK):
                o_ref[sub, rows, :] = res[:, sub * HEAD_DIM:(sub + 1) * HEAD_DIM]
            return carry

        lax.fori_loop(0, N_ROW_TILES, tile, 0)

    return pl.pallas_call(
        body, grid=(QKV_BLOCKS,),
        in_specs=[pl.BlockSpec((SEQ, D_MODEL), lambda j: (0, 0)), pl.BlockSpec((None, D_MODEL, IN_SHARD), lambda j: (j, 0, 0))],
        out_specs=pl.BlockSpec((HEADS_PER_BLOCK, SEQ, HEAD_DIM), lambda j: (j, 0, 0)),
        out_shape=SDS((3 * HEADS, SEQ, HEAD_DIM), F32), name="proj_heads",
        compiler_params=_params(("parallel",)))(a, w)


def _proj_u(a, w):
    def body(a_ref, w_ref, o_ref):
        def tile(i, carry):
            rows = _rows(i)
            o_ref[rows, :] = _dot(a_ref[rows, :], w_ref[...])
            return carry

        lax.fori_loop(0, N_ROW_TILES, tile, 0)

    return pl.pallas_call(
        body, grid=(N_DEV - QKV_BLOCKS,),
        in_specs=[pl.BlockSpec((SEQ, D_MODEL), lambda j: (0, 0)),
                  pl.BlockSpec((None, D_MODEL, IN_SHARD), lambda j: (j + QKV_BLOCKS, 0, 0))],
        out_specs=pl.BlockSpec((SEQ, IN_SHARD), lambda j: (0, j)),
        out_shape=SDS((SEQ, S5_WIDTH), F32), name="proj_u",
        compiler_params=_params(("parallel",)))(a, w)


def _proj_bwd(dq, dk, dv, du, a, w):
    def body(dq_ref, dk_ref, dv_ref, du_ref, a_ref, w_ref, da_ref, dw_ref, acc, dp_ref):
        j = pl.program_id(0)

        for which, src in enumerate((dq_ref, dk_ref, dv_ref)):
            @pl.when((j >= 2 * which) & (j < 2 * which + 2))
            def _(src=src):
                dp_ref[...] = jnp.concatenate([src[sub] for sub in range(HEADS_PER_BLOCK)], axis=-1).astype(BF16)

        @pl.when(j >= QKV_BLOCKS)
        def _():
            dp_ref[...] = du_ref[...].astype(BF16)

        def tile(i, carry):
            rows = _rows(i)
            dpt = dp_ref[rows, :]
            dw = _dot_tn(a_ref[rows, :], dpt)
            dat = _dot_nt(dpt, w_ref[...])

            @pl.when(i == 0)
            def _():
                acc[...] = dw

            @pl.when(i != 0)
            def _():
                acc[...] += dw

            @pl.when(j == 0)
            def _():
                da_ref[rows, :] = dat

            @pl.when(j != 0)
            def _():
                da_ref[rows, :] += dat

            return carry

        lax.fori_loop(0, N_ROW_TILES, tile, 0)
        dw_ref[...] = acc[...].astype(BF16)

    full = pl.BlockSpec((SEQ, D_MODEL), lambda j: (0, 0))
    wspec = pl.BlockSpec((None, D_MODEL, IN_SHARD), lambda j: (j, 0, 0))

    def heads(which):
        return pl.BlockSpec((HEADS_PER_BLOCK, SEQ, HEAD_DIM), lambda j: (jnp.clip(j - 2 * which, 0, 1), 0, 0))

    return pl.pallas_call(
        body, grid=(N_DEV,),
        in_specs=[heads(0), heads(1), heads(2),
                  pl.BlockSpec((SEQ, IN_SHARD), lambda j: (0, jnp.clip(j - QKV_BLOCKS, 0, 1))), full, wspec],
        out_specs=[full, wspec],
        out_shape=[SDS((SEQ, D_MODEL), F32), SDS((N_DEV, D_MODEL, IN_SHARD), BF16)],
        scratch_shapes=[pltpu.VMEM((D_MODEL, IN_SHARD), F32), pltpu.VMEM((SEQ, IN_SHARD), BF16)],
        name="proj_bwd", compiler_params=_params(("arbitrary",)))(dq, dk, dv, du, a, w)


def _na_consts():
    c = np.arange(GRID_W)
    col_start = np.clip(c - KW // 2, 0, GRID_W - KW)
    col_in = (c[None, :] >= col_start[:, None]) & (c[None, :] < col_start[:, None] + KW)
    dc = np.clip(c[None, :] - c[:, None] + KW - 1, 0, 2 * KW - 2)
    onehot = np.zeros((128, GRID_W * GRID_W), np.float32)
    qq, kk = np.meshgrid(c, c, indexing="ij")
    onehot[dc[col_in], (qq * GRID_W + kk)[col_in]] = 1.0
    negmask = np.where(col_in, 0.0, NEG_INF).astype(np.float32).reshape(1, -1)
    return onehot, negmask


def _na_pair(block_type, a, b):
    if block_type == 0:
        return b - a + KH - 1 if b < KH else None
    if block_type == 1:
        return b - a + KH // 2 - 1 if a <= b < a + KH else None
    return b - a if b >= NA_KR - KH else None


def _rpb_expand(rpb):
    onehot, negmask = _na_consts()
    rows = HEADS * (2 * KH - 1)
    rpb_pad = jnp.pad(rpb.reshape(rows, 2 * KW - 1), ((0, 128 - rows), (0, 128 - (2 * KW - 1))))

    def body(r_ref, oh_ref, m_ref, t_ref):
        hi, mid, lo = _split3(r_ref[...])
        oh = oh_ref[...]
        t_ref[...] = _dot(hi, oh) + _dot(mid, oh) + _dot(lo, oh) + m_ref[...]

    table = pl.pallas_call(body, out_shape=SDS((128, GRID_W * GRID_W), F32), name="rpb_expand",
                           compiler_params=_params())(rpb_pad, jnp.asarray(onehot, BF16), jnp.asarray(negmask))
    return table[:rows].reshape(HEADS, 2 * KH - 1, GRID_W, GRID_W)


def _rpb_reduce(dslabs):
    onehot, _ = _na_consts()
    rows = HEADS * (2 * KH - 1)

    def body(x_ref, oht_ref, o_ref):
        hi, mid, lo = _split3(x_ref[...])
        oht = oht_ref[...]
        o_ref[...] = _dot(hi, oht) + _dot(mid, oht) + _dot(lo, oht)

    out = pl.pallas_call(body, out_shape=SDS((rows, 128), F32), name="rpb_reduce", compiler_params=_params())(
        dslabs.reshape(rows, GRID_W * GRID_W), jnp.asarray(onehot.T, BF16))
    return out.reshape(HEADS, 2 * KH - 1, 128)


def _bias_tiles(slab_ref, tile_ref):
    tile_ref[...] = jnp.full(tile_ref.shape, NEG_INF, F32)
    for t in range(NA_TYPES):
        for a in range(NA_RB):
            for b in range(NA_KR):
                dr = _na_pair(t, a, b)
                if dr is not None:
                    tile_ref[t, a * GRID_W:(a + 1) * GRID_W, b * GRID_W:(b + 1) * GRID_W] = slab_ref[dr]


def _bias_tiles_bwd(dtile_ref, dslab_ref):
    acc = {}
    for t in range(NA_TYPES):
        for a in range(NA_RB):
            for b in range(NA_KR):
                dr = _na_pair(t, a, b)
                if dr is not None:
                    part = dtile_ref[t, a * GRID_W:(a + 1) * GRID_W, b * GRID_W:(b + 1) * GRID_W]
                    acc[dr] = part if dr not in acc else acc[dr] + part
    for dr in range(2 * KH - 1):
        dslab_ref[dr] = acc[dr]


def _block_geometry(g):
    start = jnp.clip(g * NA_RB - KH // 2, 0, GRID_ROWS - NA_KR)
    block_type = jnp.where(g == 0, 0, jnp.where(g == NA_BLOCKS - 1, 2, 1))
    q0 = pl.multiple_of(N_META + g * NA_QB, 16)
    k0 = pl.multiple_of(N_META + start * GRID_W, 16)
    return block_type, q0, k0


def _na_probs(q, kk, km, bias):
    s = _dot_nt(q, kk) * ATT_SCALE + bias
    sm = _dot_nt(q, km) * ATT_SCALE
    m = jnp.maximum(jnp.max(s, axis=-1, keepdims=True), jnp.max(sm, axis=-1, keepdims=True))
    p = jnp.exp(s - m)
    pm = jnp.exp(sm - m)
    inv = 1.0 / (jnp.sum(p, axis=-1, keepdims=True) + jnp.sum(pm, axis=-1, keepdims=True))
    return p * inv, pm * inv


def _meta_probs(qm, km):
    s = _dot_nt(qm, km) * ATT_SCALE
    p = jnp.exp(s - jnp.max(s, axis=-1, keepdims=True))
    return p / jnp.sum(p, axis=-1, keepdims=True)


def _qkv_specs():
    return [pl.BlockSpec((None, SEQ, HEAD_DIM), lambda h, which=which: (h + which * HEADS, 0, 0)) for which in range(3)]


def _na_fwd(qkv, bias):
    def body(q_ref, k_ref, v_ref, slab_ref, o_ref, b_ref):
        _bias_tiles(slab_ref, b_ref)
        km = k_ref[0:N_META, :].astype(BF16)
        vm = v_ref[0:N_META, :].astype(BF16)
        pmm = _meta_probs(q_ref[0:N_META, :].astype(BF16), km)
        o_ref[0:N_META, :] = _dot(pmm.astype(BF16), vm)

        def block(g, carry):
            block_type, q0, k0 = _block_geometry(g)
            qb = q_ref[pl.ds(q0, NA_QB), :].astype(BF16)
            kk = k_ref[pl.ds(k0, NA_KB), :].astype(BF16)
            vv = v_ref[pl.ds(k0, NA_KB), :].astype(BF16)
            p, pm = _na_probs(qb, kk, km, b_ref[block_type])
            o_ref[pl.ds(q0, NA_QB), :] = _dot(p.astype(BF16), vv) + _dot(pm.astype(BF16), vm)
            return carry

        lax.fori_loop(0, NA_BLOCKS, block, 0)

    head = pl.BlockSpec((None, SEQ, HEAD_DIM), lambda h: (h, 0, 0))
    return pl.pallas_call(
        body, grid=(HEADS,), in_specs=_qkv_specs() + [pl.BlockSpec((None, 2 * KH - 1, GRID_W, GRID_W), lambda h: (h, 0, 0, 0))],
        out_specs=head, out_shape=SDS((HEADS, SEQ, HEAD_DIM), F32), name="na_fwd",
        scratch_shapes=[pltpu.VMEM((NA_TYPES, NA_QB, NA_KB), F32)],
        compiler_params=_params(("parallel",)))(qkv, qkv, qkv, bias)


def _na_bwd(qkv, bias, do):
    def body(q_ref, k_ref, v_ref, slab_ref, do_ref, dq_ref, dk_ref, dv_ref, dslab_ref, b_ref, db_ref):
        _bias_tiles(slab_ref, b_ref)
        km = k_ref[0:N_META, :].astype(BF16)
        vm = v_ref[0:N_META, :].astype(BF16)
        dk_ref[...] = jnp.zeros_like(dk_ref)
        dv_ref[...] = jnp.zeros_like(dv_ref)
        db_ref[...] = jnp.zeros_like(db_ref)

        qm = q_ref[0:N_META, :].astype(BF16)
        dom = do_ref[0:N_META, :].astype(BF16)
        pmm = _meta_probs(qm, km)
        dpm = _dot_nt(dom, vm)
        dsm = (pmm * (dpm - jnp.sum(pmm * dpm, axis=-1, keepdims=True)) * ATT_SCALE).astype(BF16)
        dq_ref[0:N_META, :] = _dot(dsm, km)
        dkm0 = _dot_tn(dsm, qm)
        dvm0 = _dot_tn(pmm.astype(BF16), dom)

        def block(g, carry):
            dkm, dvm = carry
            block_type, q0, k0 = _block_geometry(g)
            qb = q_ref[pl.ds(q0, NA_QB), :].astype(BF16)
            kk = k_ref[pl.ds(k0, NA_KB), :].astype(BF16)
            vv = v_ref[pl.ds(k0, NA_KB), :].astype(BF16)
            dob = do_ref[pl.ds(q0, NA_QB), :].astype(BF16)
            p, pm = _na_probs(qb, kk, km, b_ref[block_type])
            dp = _dot_nt(dob, vv)
            dpm_ = _dot_nt(dob, vm)
            delta = jnp.sum(p * dp, axis=-1, keepdims=True) + jnp.sum(pm * dpm_, axis=-1, keepdims=True)
            ds = p * (dp - delta)
            dsm_ = pm * (dpm_ - delta)
            db_ref[block_type] += ds
            dsb = (ds * ATT_SCALE).astype(BF16)
            dsmb = (dsm_ * ATT_SCALE).astype(BF16)
            dq_ref[pl.ds(q0, NA_QB), :] = _dot(dsb, kk) + _dot(dsmb, km)
            dk_ref[pl.ds(k0, NA_KB), :] += _dot_tn(dsb, qb)
            dv_ref[pl.ds(k0, NA_KB), :] += _dot_tn(p.astype(BF16), dob)
            return dkm + _dot_tn(dsmb, qb), dvm + _dot_tn(pm.astype(BF16), dob)

        dkm, dvm = lax.fori_loop(0, NA_BLOCKS, block, (dkm0, dvm0))
        dk_ref[0:N_META, :] = dkm
        dv_ref[0:N_META, :] = dvm
        _bias_tiles_bwd(db_ref, dslab_ref)

    head = pl.BlockSpec((None, SEQ, HEAD_DIM), lambda h: (h, 0, 0))
    bspec = pl.BlockSpec((None, 2 * KH - 1, GRID_W, GRID_W), lambda h: (h, 0, 0, 0))
    return pl.pallas_call(
        body, grid=(HEADS,), in_specs=_qkv_specs() + [bspec, head], out_specs=[head, head, head, bspec],
        out_shape=[SDS((HEADS, SEQ, HEAD_DIM), F32)] * 3 + [SDS((HEADS, 2 * KH - 1, GRID_W, GRID_W), F32)],
        scratch_shapes=[pltpu.VMEM((NA_TYPES, NA_QB, NA_KB), F32), pltpu.VMEM((NA_TYPES, NA_QB, NA_KB), F32)],
        name="na_bwd", compiler_params=_params(("parallel",)))(qkv, qkv, qkv, bias, do)


def _cmul(ar, ai, br, bi):
    return ar * br - ai * bi, ar * bi + ai * br


def _cpow(ar, ai, n):
    rr, ri = None, None
    br, bi = ar, ai
    while n:
        if n & 1:
            rr, ri = (br, bi) if rr is None else _cmul(rr, ri, br, bi)
        n >>= 1
        if n:
            br, bi = _cmul(br, bi, br, bi)
    return rr, ri


def _s5_prep(lr, li, logdt, bre, bim):
    def body(lr_ref, li_ref, dt_ref, br_ref, bi_ref, lbr_ref, lbi_ref, bbr_ref, bbi_ref):
        lr_, li_ = lr_ref[...], li_ref[...]
        dt = jnp.exp(dt_ref[...])
        mag = jnp.exp(lr_ * dt)
        lbr = mag * jnp.cos(li_ * dt)
        lbi = mag * jnp.sin(li_ * dt)
        lbr_ref[...] = lbr
        lbi_ref[...] = lbi
        den = lr_ * lr_ + li_ * li_
        xr = lbr - 1.0
        cr = (xr * lr_ + lbi * li_) / den
        ci = (lbi * lr_ - xr * li_) / den
        br, bi = br_ref[...], bi_ref[...]
        bbr_ref[...] = cr[:, None, :] * br - ci[:, None, :] * bi
        bbi_ref[...] = cr[:, None, :] * bi + ci[:, None, :] * br

    n = 2 * S5_GROUPS
    return pl.pallas_call(
        body, out_shape=[SDS((n, S5_STATE), F32)] * 2 + [SDS((n, S5_GROUP, S5_STATE), F32)] * 2,
        name="s5_prep", compiler_params=_params())(lr, li, logdt, bre, bim)


def _s5_prep_bwd(lr, li, logdt, bre, bim, dar, dai, dbbr, dbbi):
    def body(lr_ref, li_ref, dt_ref, br_ref, bi_ref, dar_ref, dai_ref, dbr_ref, dbi_ref,
             glr_ref, gli_ref, gdt_ref, gbr_ref, gbi_ref):
        lr_, li_ = lr_ref[...], li_ref[...]
        dt = jnp.exp(dt_ref[...])
        mag = jnp.exp(lr_ * dt)
        lbr = mag * jnp.cos(li_ * dt)
        lbi = mag * jnp.sin(li_ * dt)
        den = lr_ * lr_ + li_ * li_
        xr = lbr - 1.0
        cr = (xr * lr_ + lbi * li_) / den
        ci = (lbi * lr_ - xr * li_) / den
        br, bi = br_ref[...], bi_ref[...]
        dbr, dbi = dbr_ref[...], dbi_ref[...]
        gbr_ref[...] = cr[:, None, :] * dbr + ci[:, None, :] * dbi
        gbi_ref[...] = cr[:, None, :] * dbi - ci[:, None, :] * dbr
        gcr = jnp.sum(dbr * br + dbi * bi, axis=1)
        gci = jnp.sum(dbi * br - dbr * bi, axis=1)
        ilr, ili = lr_ / den, li_ / den
        tr, ti = _cmul(gcr, gci, ilr, ili)
        glbr = dar_ref[...] + tr
        glbi = dai_ref[...] + ti
        dr_, di_ = _cmul(tr, ti, cr, -ci)
        gwr, gwi = _cmul(glbr, glbi, lbr, -lbi)
        glr_ref[...] = gwr * dt - dr_
        gli_ref[...] = gwi * dt - di_
        gdt_ref[...] = jnp.sum(gwr * lr_ + gwi * li_, axis=-1, keepdims=True) * dt

    n = 2 * S5_GROUPS
    return pl.pallas_call(
        body, out_shape=[SDS((n, S5_STATE), F32)] * 2 + [SDS((n, 1), F32)] + [SDS((n, S5_GROUP, S5_STATE), F32)] * 2,
        name="s5_prep_bwd", compiler_params=_params())(lr, li, logdt, bre, bim, dar, dai, dbbr, dbbi)


def _scan_local(xr_ref, xi_ref, ar8, ai8, reverse):
    def step(i, carry):
        sr, si = carry
        idx = (SCAN_T - 1 - i) if reverse else i
        rows = pl.ds(pl.multiple_of(idx * SCAN_BLOCKS, SCAN_BLOCKS), SCAN_BLOCKS)
        nr = ar8 * sr - ai8 * si + xr_ref[rows, :]
        ni = ar8 * si + ai8 * sr + xi_ref[rows, :]
        xr_ref[rows, :] = nr
        xi_ref[rows, :] = ni
        return nr, ni

    z = jnp.zeros(ar8.shape, F32)
    return lax.fori_loop(0, SCAN_T, step, (z, z))


def _scan_carries(er, ei, atr, ati, reverse):
    row = lax.broadcasted_iota(jnp.int32, er.shape, 0)
    cr = jnp.zeros((1, er.shape[1]), F32)
    ci = cr
    outr = jnp.zeros(er.shape, F32)
    outi = outr
    order = range(SCAN_BLOCKS - 1, -1, -1) if reverse else range(SCAN_BLOCKS)
    for b in order:
        outr = jnp.where(row == b, cr, outr)
        outi = jnp.where(row == b, ci, outi)
        nr, ni = _cmul(atr, ati, cr, ci)
        cr, ci = nr + er[b:b + 1, :], ni + ei[b:b + 1, :]
    return outr, outi


def _scan_fixup(xr_ref, xi_ref, cr8, ci8, ar8, ai8, reverse, pair=None):
    tile = lambda idx: pl.ds(pl.multiple_of(idx * SCAN_BLOCKS, SCAN_BLOCKS), SCAN_BLOCKS)

    def fix(idx, pr, pi):
        fr, fi = _cmul(pr, pi, cr8, ci8)
        nr, ni = xr_ref[tile(idx), :] + fr, xi_ref[tile(idx), :] + fi
        xr_ref[tile(idx), :] = nr
        xi_ref[tile(idx), :] = ni
        return nr, ni

    if pair is None:
        def step(i, carry):
            pr, pi = carry
            fix((SCAN_T - 1 - i) if reverse else i, pr, pi)
            return _cmul(pr, pi, ar8, ai8)

        lax.fori_loop(0, SCAN_T, step, (ar8, ai8), unroll=2)
        return None

    sr_ref, si_ref = pair
    earlier = -1 if reverse else 1

    def step(i, carry):
        pr, pi, accr, acci = carry
        idx = (SCAN_T - 1 - i) if reverse else i
        nr, ni = fix(idx, pr, pi)
        qr, qi = _cmul(nr, ni, sr_ref[tile(idx + earlier), :], -si_ref[tile(idx + earlier), :])
        pr, pi = _cmul(pr, pi, ar8, ai8)
        return pr, pi, accr + qr, acci + qi

    z = jnp.zeros(ar8.shape, F32)
    pr, pi, accr, acci = lax.fori_loop(0, SCAN_T - 1, step, (ar8, ai8, z, z))
    edge, src, shift, empty = (0, SCAN_T - 1, 1, 0) if reverse else (SCAN_T - 1, 0, SCAN_BLOCKS - 1, SCAN_BLOCKS - 1)
    nr, ni = fix(edge, pr, pi)
    row = lax.broadcasted_iota(jnp.int32, ar8.shape, 0)
    spr = jnp.where(row == empty, 0.0, pltpu.roll(sr_ref[tile(src), :], shift, 0))
    spi = jnp.where(row == empty, 0.0, pltpu.roll(si_ref[tile(src), :], shift, 0))
    qr, qi = _cmul(nr, ni, spr, -spi)
    return jnp.sum(accr + qr, axis=0, keepdims=True), jnp.sum(acci + qi, axis=0, keepdims=True)


def _scan(xr_ref, xi_ref, ar, ai, reverse, pair=None):
    n = ar.shape[1]
    ar8 = jnp.broadcast_to(ar, (SCAN_BLOCKS, n))
    ai8 = jnp.broadcast_to(ai, (SCAN_BLOCKS, n))
    er, ei = _scan_local(xr_ref, xi_ref, ar8, ai8, reverse)
    atr, ati = _cpow(ar, ai, SCAN_T)
    cr8, ci8 = _scan_carries(er, ei, atr, ati, reverse)
    return _scan_fixup(xr_ref, xi_ref, cr8, ci8, ar8, ai8, reverse, pair)


def _s5_specs():
    chan = pl.BlockSpec((SEQ, CH_W), lambda c, d: (0, c))
    chan2 = pl.BlockSpec((None, SEQ, CH_W), lambda c, d: (d, 0, c))
    state = pl.BlockSpec((None, SEQ, ST_W), lambda c, d: (d, 0, c))
    bmat = pl.BlockSpec((None, None, CH_W, ST_W), lambda c, d: (d, c, 0, 0))
    cmat = pl.BlockSpec((None, None, ST_W, CH_W), lambda c, d: (d, c, 0, 0))
    avec = pl.BlockSpec((None, None, 1, ST_W), lambda c, d: (d, c, 0, 0))
    return chan, chan2, state, bmat, cmat, avec


def _scan_by_direction(xr_ref, xi_ref, ar, ai, d, adjoint, pair=None, da_out=None):
    for direction in range(2):
        @pl.when(d == direction)
        def _(direction=direction):
            res = _scan(xr_ref, xi_ref, ar, ai, adjoint != (direction == 1), pair)
            if pair is not None:
                da_out[0][...], da_out[1][...] = res


def _s5_scan_fwd(u, bre, bim, are, aim, cre, cim):
    def body(u_ref, bre_ref, bim_ref, are_ref, aim_ref, cre_ref, cim_ref, sr_ref, si_ref, y_ref):
        ub = u_ref[...].astype(BF16)
        sr_ref[...] = _dot(ub, bre_ref[...])
        si_ref[...] = _dot(ub, bim_ref[...])
        _scan_by_direction(sr_ref, si_ref, are_ref[...], aim_ref[...], pl.program_id(1), adjoint=False)
        y_ref[...] = _dot(sr_ref[...].astype(BF16), cre_ref[...]) - _dot(si_ref[...].astype(BF16), cim_ref[...])

    chan, chan2, state, bmat, cmat, avec = _s5_specs()
    return pl.pallas_call(
        body, grid=(S5_CHUNKS, 2), in_specs=[chan, bmat, bmat, avec, avec, cmat, cmat], out_specs=[state, state, chan2],
        out_shape=[SDS((2, SEQ, S5_GROUPS * S5_STATE), F32)] * 2 + [SDS((2, SEQ, S5_WIDTH), F32)],
        name="s5_scan_fwd", compiler_params=_params(("parallel", "parallel")))(u, bre, bim, are, aim, cre, cim)


def _diag_out(out_ref, full):
    for g in range(8):
        out_ref[g] = full[g * S5_GROUP:(g + 1) * S5_GROUP, g * S5_STATE:(g + 1) * S5_STATE]


def _s5_scan_bwd(dy, du_skip, u, sr, si, bre, bim, are, aim, cre, cim):
    def body(dy_ref, dus_ref, u_ref, sr_ref, si_ref, bre_ref, bim_ref, are_ref, aim_ref, cre_ref, cim_ref,
             du_ref, dbr_ref, dbi_ref, dcr_ref, dci_ref, dar_ref, dai_ref, gr_ref, gi_ref):
        d = pl.program_id(1)
        dyb = dy_ref[...].astype(BF16)
        gr_ref[...] = _dot_nt(dyb, cre_ref[...])
        gi_ref[...] = -_dot_nt(dyb, cim_ref[...])
        _diag_out(dcr_ref, _dot_tn(dyb, sr_ref[...].astype(BF16)))
        _diag_out(dci_ref, -_dot_tn(dyb, si_ref[...].astype(BF16)))
        _scan_by_direction(gr_ref, gi_ref, are_ref[...], -aim_ref[...], d, adjoint=True, pair=(sr_ref, si_ref),
                           da_out=(dar_ref, dai_ref))

        @pl.when(d == 0)
        def _():
            du_ref[...] = dus_ref[...]

        grb = gr_ref[...].astype(BF16)
        gib = gi_ref[...].astype(BF16)
        du_ref[...] += _dot_nt(grb, bre_ref[...]) + _dot_nt(gib, bim_ref[...])
        ub = u_ref[...].astype(BF16)
        _diag_out(dbr_ref, _dot_tn(ub, grb))
        _diag_out(dbi_ref, _dot_tn(ub, gib))

    chan, _, state, bmat, cmat, avec = _s5_specs()
    diag = pl.BlockSpec((None, None, 8, S5_GROUP, S5_STATE), lambda c, d: (d, c, 0, 0, 0))
    return pl.pallas_call(
        body, grid=(S5_CHUNKS, 2), in_specs=[chan, chan, chan, state, state, bmat, bmat, avec, avec, cmat, cmat],
        out_specs=[chan, diag, diag, diag, diag, avec, avec],
        out_shape=[SDS((SEQ, S5_WIDTH), F32)] + [SDS((2, S5_CHUNKS, 8, S5_GROUP, S5_STATE), F32)] * 4
                  + [SDS((2, S5_CHUNKS, 1, ST_W), F32)] * 2,
        scratch_shapes=[pltpu.VMEM((SEQ, ST_W), F32), pltpu.VMEM((SEQ, ST_W), F32)],
        name="s5_scan_bwd", compiler_params=_params(("parallel", "arbitrary")))(dy, du_skip, u, sr, si, bre, bim, are, aim, cre, cim)


_GELU_K = math.sqrt(2.0 / math.pi)
_GELU_C = 0.044715


def _gelu(x):
    t = jnp.tanh(_GELU_K * (x + _GELU_C * x * x * x))
    return 0.5 * x * (1.0 + t), t


def _s5_glu_fwd(u, y2, dskip, wglu, bglu):
    def body(u_ref, y0_ref, y1_ref, d_ref, w_ref, b_ref, o_ref, yp_ref):
        ypre = u_ref[...] * d_ref[...] + y0_ref[...] + y1_ref[...]
        yp_ref[...] = ypre
        y, _ = _gelu(ypre)
        z = _dot(y.astype(BF16), w_ref[...]) + b_ref[...]
        o_ref[...] = y * jax.nn.sigmoid(z)

    row = _row_spec(S5_WIDTH)
    vec = _fix_spec((1, S5_WIDTH))
    dir0 = pl.BlockSpec((None, ROW_TILE, S5_WIDTH), lambda i: (0, i, 0))
    dir1 = pl.BlockSpec((None, ROW_TILE, S5_WIDTH), lambda i: (1, i, 0))
    return pl.pallas_call(
        body, grid=(N_ROW_TILES,), in_specs=[row, dir0, dir1, vec, _fix_spec((S5_WIDTH, S5_WIDTH)), vec],
        out_specs=[row, row], out_shape=[SDS((SEQ, S5_WIDTH), F32)] * 2, name="s5_glu_fwd",
        compiler_params=_params(("parallel",)))(u, y2, y2, dskip, wglu, bglu)


def _s5_glu_bwd(do, ypre, u, dskip, wglu, bglu):
    def body(do_ref, yp_ref, u_ref, d_ref, w_ref, b_ref, dyp_ref, du_ref, dw_ref, db_ref, dd_ref):
        i = pl.program_id(0)
        ypre = yp_ref[...]
        y, t = _gelu(ypre)
        yb = y.astype(BF16)
        sg = jax.nn.sigmoid(_dot(yb, w_ref[...]) + b_ref[...])
        dov = do_ref[...]
        dz = dov * y * sg * (1.0 - sg)
        dzb = dz.astype(BF16)
        dy = dov * sg + _dot_nt(dzb, w_ref[...])
        dgelu = 0.5 * (1.0 + t) + 0.5 * ypre * (1.0 - t * t) * _GELU_K * (1.0 + 3.0 * _GELU_C * ypre * ypre)
        dyp = dy * dgelu
        dyp_ref[...] = dyp
        uv = u_ref[...]
        du_ref[...] = dyp * d_ref[...]

        @pl.when(i == 0)
        def _():
            dw_ref[...] = jnp.zeros_like(dw_ref)
            db_ref[...] = jnp.zeros_like(db_ref)
            dd_ref[...] = jnp.zeros_like(dd_ref)

        dw_ref[...] += _dot_tn(yb, dzb)
        db_ref[...] += jnp.sum(dz, axis=0, keepdims=True)
        dd_ref[...] += jnp.sum(dyp * uv, axis=0, keepdims=True)

    row = _row_spec(S5_WIDTH)
    vec = _fix_spec((1, S5_WIDTH))
    mat = _fix_spec((S5_WIDTH, S5_WIDTH))
    return pl.pallas_call(
        body, grid=(N_ROW_TILES,), in_specs=[row, row, row, vec, mat, vec], out_specs=[row, row, mat, vec, vec],
        out_shape=[SDS((SEQ, S5_WIDTH), F32)] * 2 + [SDS((S5_WIDTH, S5_WIDTH), F32), SDS((1, S5_WIDTH), F32), SDS((1, S5_WIDTH), F32)],
        name="s5_glu_bwd", compiler_params=_params(("arbitrary",)))(do, ypre, u, dskip, wglu, bglu)


def _heads_side_by_side(o_ref):
    return jnp.concatenate([o_ref[h] for h in range(HEADS)], axis=-1)


def _mix_out_fwd(ona, os5, g_na, g_s5, wout):
    def body(a_ref, s_ref, ga_ref, gs_ref, w_ref, o_ref):
        av, sv = _heads_side_by_side(a_ref), s_ref[...]
        ca = (av * _rstd(av) * ga_ref[...]).astype(BF16)
        cs = (sv * _rstd(sv) * gs_ref[...]).astype(BF16)
        o_ref[...] = _dot(ca, w_ref[0:NA_WIDTH, :]) + _dot(cs, w_ref[NA_WIDTH:, :])

    row = _row_spec(NA_WIDTH)
    vec = _fix_spec((1, NA_WIDTH))
    heads = pl.BlockSpec((HEADS, ROW_TILE, HEAD_DIM), lambda i: (0, i, 0))
    return pl.pallas_call(
        body, grid=(N_ROW_TILES,), in_specs=[heads, row, vec, vec, _fix_spec((D_MODEL, D_MODEL))],
        out_specs=_row_spec(D_MODEL), out_shape=SDS((SEQ, D_MODEL), F32), name="mix_out_fwd",
        compiler_params=_params(("parallel",)))(ona, os5, g_na, g_s5, wout)


def _mix_out_bwd(dmix, ona, os5, g_na, g_s5, wout):
    def body(dm_ref, a_ref, s_ref, ga_ref, gs_ref, w_ref, da_ref, ds_ref, dw_ref, dga_ref, dgs_ref):
        i = pl.program_id(0)
        dm = dm_ref[...]
        av, sv = _heads_side_by_side(a_ref), s_ref[...]
        ra, rs = _rstd(av), _rstd(sv)
        ga, gs = ga_ref[...], gs_ref[...]
        ca = (av * ra * ga).astype(BF16)
        cs = (sv * rs * gs).astype(BF16)
        dca = _dot_nt(dm, w_ref[0:NA_WIDTH, :])
        dcs = _dot_nt(dm, w_ref[NA_WIDTH:, :])
        da, dga = _rms_bwd(av, ra, ga, dca)
        ds, dgs = _rms_bwd(sv, rs, gs, dcs)
        for h in range(HEADS):
            da_ref[h] = da[:, h * HEAD_DIM:(h + 1) * HEAD_DIM]
        ds_ref[...] = ds

        @pl.when(i == 0)
        def _():
            dw_ref[...] = jnp.zeros_like(dw_ref)
            dga_ref[...] = jnp.zeros_like(dga_ref)
            dgs_ref[...] = jnp.zeros_like(dgs_ref)

        dw_ref[0:NA_WIDTH, :] += _dot_tn(ca, dm)
        dw_ref[NA_WIDTH:, :] += _dot_tn(cs, dm)
        dga_ref[...] += jnp.sum(dga, axis=0, keepdims=True)
        dgs_ref[...] += jnp.sum(dgs, axis=0, keepdims=True)

    row = _row_spec(NA_WIDTH)
    vec = _fix_spec((1, NA_WIDTH))
    mat = _fix_spec((D_MODEL, D_MODEL))
    heads = pl.BlockSpec((HEADS, ROW_TILE, HEAD_DIM), lambda i: (0, i, 0))
    return pl.pallas_call(
        body, grid=(N_ROW_TILES,), in_specs=[_row_spec(D_MODEL), heads, row, vec, vec, mat],
        out_specs=[heads, row, mat, vec, vec],
        out_shape=[SDS((HEADS, SEQ, HEAD_DIM), F32), SDS((SEQ, NA_WIDTH), F32), SDS((D_MODEL, D_MODEL), F32),
                   SDS((1, NA_WIDTH), F32), SDS((1, NA_WIDTH), F32)],
        name="mix_out_bwd", compiler_params=_params(("arbitrary",)))(dmix, ona, os5, g_na, g_s5, wout)


def _me():
    x, y, c = lax.axis_index("x"), lax.axis_index("y"), lax.axis_index("c")
    return x, y, c, 4 * x + 2 * y + c


def _peer(k):
    x, y, c, _ = _me()
    px = 1 - x if (k >> 2) & 1 else x
    py = 1 - y if (k >> 1) & 1 else y
    pc = 1 - c if k & 1 else c
    return (px, py, pc), 4 * px + 2 * py + pc


ALL_PEERS = (1, 2, 3, 4, 5, 6, 7)
CHIP_PEERS = (2, 4, 6)
SIBLING = 1


def _slot8(pos):
    return 4 * pos[0] + 2 * pos[1] + pos[2]


def _slot4(pos):
    return 2 * pos[0] + pos[1]


def _exchange(arrays, gather, name, after=()):
    n, n_after = len(arrays), len(after)

    def body(*refs):
        ins, outs = refs[:n], refs[n + n_after:2 * n + n_after]
        token = refs[2 * n + n_after]
        send_sems, recv_sems, local_sems = refs[2 * n + n_after + 1:]
        token[...] = jnp.zeros_like(token)
        _, _, _, me = _me()
        started = []
        for a in range(n):
            src_mine = ins[a] if gather else ins[a].at[me]
            local = pltpu.make_async_copy(src_mine, outs[a].at[me], local_sems.at[a])
            local.start()
            started.append(local)
        sends = []
        for k in range(1, N_DEV):
            peer, peer_idx = _peer(k)
            for a in range(n):
                src = ins[a] if gather else ins[a].at[peer_idx]
                cp = pltpu.make_async_remote_copy(src_ref=src, dst_ref=outs[a].at[me], send_sem=send_sems.at[a, k - 1],
                                                  recv_sem=recv_sems.at[a, k - 1], device_id=peer, device_id_type=MESH)
                cp.start()
                sends.append(cp)
        for k in range(1, N_DEV):
            peer, peer_idx = _peer(k)
            for a in range(n):
                src = ins[a] if gather else ins[a].at[peer_idx]
                pltpu.make_async_remote_copy(src_ref=src, dst_ref=outs[a].at[peer_idx], send_sem=send_sems.at[a, k - 1],
                                             recv_sem=recv_sems.at[a, k - 1], device_id=peer, device_id_type=MESH).wait_recv()
        for cp in sends:
            cp.wait_send()
        for local in started:
            local.wait()

    hbm = pl.BlockSpec(memory_space=pltpu.HBM)
    out_shape = [SDS((N_DEV,) + tuple(a.shape), a.dtype) if gather else SDS(a.shape, a.dtype) for a in arrays]
    out = pl.pallas_call(
        body, in_specs=[hbm] * n + [pl.BlockSpec(memory_space=pl.ANY)] * n_after,
        out_specs=[hbm] * n + [pl.BlockSpec(memory_space=pltpu.VMEM)], out_shape=out_shape + [SDS((8, 128), F32)],
        scratch_shapes=[pltpu.SemaphoreType.DMA((n, N_DEV - 1)), pltpu.SemaphoreType.DMA((n, N_DEV - 1)),
                        pltpu.SemaphoreType.DMA((n,))],
        name=name)(*arrays, *after)
    return list(out[:n]), out[n]


_HBM = pl.BlockSpec(memory_space=pltpu.HBM)
_SEM = pl.BlockSpec(memory_space=pltpu.SEMAPHORE)
_EFFECT = pltpu.SideEffectType.DATAFLOW_SIDE_EFFECTING


def _land_shape(a, gather):
    return (N_DEV,) + tuple(a.shape) if gather else tuple(a.shape)


def _place_own(arrays, gather, name, slot=_slot8):
    n = len(arrays)
    me = slot(_me()[:3])

    def body(me_ref, *refs):
        for a in range(n):
            refs[n + a][...] = refs[a][...]

    def own_slot(a):
        zeros = (0,) * (a.ndim - (0 if gather else 1))
        return lambda i, me_ref: (me_ref[0],) + zeros

    def whole(a):
        return lambda i, me_ref: (0,) * a.ndim

    in_specs = [pl.BlockSpec(a.shape, whole(a)) if gather else pl.BlockSpec((None,) + a.shape[1:], own_slot(a)) for a in arrays]
    out_specs = [pl.BlockSpec((None,) + (a.shape if gather else a.shape[1:]), own_slot(a)) for a in arrays]
    return pl.pallas_call(
        body, grid_spec=pltpu.PrefetchScalarGridSpec(num_scalar_prefetch=1, grid=(1,), in_specs=in_specs, out_specs=out_specs),
        out_shape=[SDS(_land_shape(a, gather), a.dtype) for a in arrays], name=name,
        compiler_params=_params(("arbitrary",)))(me.reshape(1).astype(jnp.int32), *arrays)


def _exchange_start(arrays, lands, gather, name, peers=ALL_PEERS, slot=_slot8):
    n = len(arrays)

    def body(*refs):
        ins, lnd = refs[:n], refs[n:2 * n]
        send_sems, recv_sems = refs[2 * n], refs[2 * n + 1]
        token = refs[-1]
        me = slot(_me()[:3])
        for i, k in enumerate(peers):
            peer, _ = _peer(k)
            for a in range(n):
                src = ins[a] if gather else ins[a].at[slot(peer)]
                s = a * len(peers) + i
                pltpu.make_async_remote_copy(src_ref=src, dst_ref=lnd[a].at[me], send_sem=send_sems.at[s],
                                             recv_sem=recv_sems.at[s], device_id=peer, device_id_type=MESH).start()
        token[...] = jnp.zeros_like(token)

    sems = pltpu.SemaphoreType.DMA((n * len(peers),))
    out = pl.pallas_call(
        body, name=name, in_specs=[_HBM] * (2 * n),
        out_shape=(sems, sems) + tuple(pltpu.HBM(a.shape, a.dtype) for a in list(arrays) + list(lands)) + (SDS((8, 128), F32),),
        out_specs=(_SEM, _SEM) + (_HBM,) * (2 * n) + (pl.BlockSpec(memory_space=pltpu.VMEM),),
        input_output_aliases={i: 2 + i for i in range(2 * n)},
        compiler_params=pltpu.CompilerParams(has_side_effects=_EFFECT),
    )(*[pltpu.with_memory_space_constraint(a, pltpu.HBM) for a in list(arrays) + list(lands)])
    return out[0], out[1], list(out[2:2 + n]), list(out[2 + n:2 + 2 * n]), out[-1]


def _exchange_wait(send_sems, recv_sems, arrays, lands, after, gather, name, peers=ALL_PEERS, slot=_slot8):
    n = len(arrays)

    def body(*refs):
        ins, lnd = refs[:n], refs[n:2 * n]
        send_sems, recv_sems = refs[2 * n], refs[2 * n + 1]
        for i, k in enumerate(peers):
            peer, _ = _peer(k)
            for a in range(n):
                src = ins[a] if gather else ins[a].at[slot(peer)]
                s = a * len(peers) + i
                cp = pltpu.make_async_remote_copy(src_ref=src, dst_ref=lnd[a].at[slot(peer)], send_sem=send_sems.at[s],
                                                  recv_sem=recv_sems.at[s], device_id=peer, device_id_type=MESH)
                cp.wait_send()
                cp.wait_recv()

        refs[-1][...] = jnp.zeros_like(refs[-1])

    after = list(after) if isinstance(after, (list, tuple)) else [after]
    out = pl.pallas_call(
        body, name=name, in_specs=[_HBM] * (2 * n) + [_SEM, _SEM] + [pl.BlockSpec(memory_space=pl.ANY)] * len(after),
        out_shape=tuple(pltpu.HBM(a.shape, a.dtype) for a in list(arrays) + list(lands)) + (SDS((8, 128), F32),),
        out_specs=(_HBM,) * (2 * n) + (pl.BlockSpec(memory_space=pltpu.VMEM),), input_output_aliases={i: i for i in range(2 * n)},
        compiler_params=pltpu.CompilerParams(has_side_effects=_EFFECT),
    )(*arrays, *lands, send_sems, recv_sems, *after)
    return list(out[n:2 * n]), out[-1]


def _forward_sibling(lands, name):
    n = len(lands)

    def body(*refs):
        outs = refs[n:2 * n]
        send_sems, recv_sems = refs[2 * n:]
        x, y, c, _ = _me()
        sends = []
        for i, k in enumerate(CHIP_PEERS):
            peer, _ = _peer(k)
            for a in range(n):
                rows = outs[a].at[_slot8(peer)]
                cp = pltpu.make_async_remote_copy(src_ref=rows, dst_ref=rows, send_sem=send_sems.at[a, i], recv_sem=recv_sems.at[a, i],
                                                  device_id=(x, y, 1 - c), device_id_type=MESH)
                cp.start()
                sends.append(cp)
        for i, k in enumerate(CHIP_PEERS):
            (px, py, pc), _ = _peer(k)
            for a in range(n):
                rows = outs[a].at[_slot8((px, py, 1 - pc))]
                pltpu.make_async_remote_copy(src_ref=rows, dst_ref=rows, send_sem=send_sems.at[a, i], recv_sem=recv_sems.at[a, i],
                                             device_id=(x, y, 1 - c), device_id_type=MESH).wait_recv()
        for cp in sends:
            cp.wait_send()

    return pl.pallas_call(
        body, in_specs=[_HBM] * n, out_specs=[_HBM] * n, out_shape=[SDS(a.shape, a.dtype) for a in lands],
        input_output_aliases={i: i for i in range(n)},
        scratch_shapes=[pltpu.SemaphoreType.DMA((n, len(CHIP_PEERS))), pltpu.SemaphoreType.DMA((n, len(CHIP_PEERS)))],
        name=name)(*lands)


def _swap_sibling(arrays, name, after=()):
    n, n_after = len(arrays), len(after)
    chips = N_DEV // 2

    def body(*refs):
        ins, outs = refs[:n], refs[n + n_after:2 * n + n_after]
        send_sems, recv_sems = refs[2 * n + n_after:]
        x, y, c, _ = _me()
        sends = []
        for q in range(chips):
            for a in range(n):
                cp = pltpu.make_async_remote_copy(src_ref=ins[a].at[q, 1 - c], dst_ref=outs[a].at[q], send_sem=send_sems.at[a, q],
                                                  recv_sem=recv_sems.at[a, q], device_id=(x, y, 1 - c), device_id_type=MESH)
                cp.start()
                sends.append(cp)
        for cp in sends:
            cp.wait_recv()
        for cp in sends:
            cp.wait_send()

    return pl.pallas_call(
        body, in_specs=[_HBM] * n + [pl.BlockSpec(memory_space=pl.ANY)] * n_after, out_specs=[_HBM] * n,
        out_shape=[SDS((chips,) + a.shape[2:], a.dtype) for a in arrays],
        scratch_shapes=[pltpu.SemaphoreType.DMA((n, chips)), pltpu.SemaphoreType.DMA((n, chips))], name=name)(*arrays, *after)


def _sum_pairs(mine, theirs, name):
    chips, _, rows, cols = mine.shape
    c = lax.axis_index("c")

    def body(c_ref, a_ref, b_ref, o_ref):
        o_ref[...] = (a_ref[...].astype(F32) + b_ref[...].astype(F32)).astype(o_ref.dtype)

    return pl.pallas_call(
        body, grid_spec=pltpu.PrefetchScalarGridSpec(
            num_scalar_prefetch=1, grid=(chips,),
            in_specs=[pl.BlockSpec((None, None, rows, cols), lambda q, c_ref: (q, c_ref[0], 0, 0)),
                      pl.BlockSpec((None, rows, cols), lambda q, c_ref: (q, 0, 0))],
            out_specs=pl.BlockSpec((None, rows, cols), lambda q, c_ref: (q, 0, 0))),
        out_shape=SDS((chips, rows, cols), mine.dtype), name=name,
        compiler_params=_params(("parallel",)))(c.reshape(1).astype(jnp.int32), mine, theirs)


def _adamw_math(w, g, m, v):
    m = ADAM_B1 * m + (1.0 - ADAM_B1) * g
    v = ADAM_B2 * v + (1.0 - ADAM_B2) * (g * g)
    m_hat = m / (1.0 - ADAM_B1 ** ADAM_STEP)
    v_hat = v / (1.0 - ADAM_B2 ** ADAM_STEP)
    delta = -ADAM_LR * (m_hat / (jnp.sqrt(v_hat) + ADAM_EPS) + ADAM_WD * w)
    return delta, m, v


def _adamw(w, m, v, pieces, name):
    rows, cols = w.shape[-2:]
    lead = w.ndim - 2
    tile = rows
    for cand in (256, 176, 128, 64, 16):
        if rows > cand and rows % cand == 0:
            tile = cand
            break

    def body(w_ref, m_ref, v_ref, p_ref, g_ref, d_ref, mo_ref, vo_ref):
        g = _sum_pieces(p_ref)
        g_ref[...] = g
        d_ref[...], mo_ref[...], vo_ref[...] = _adamw_math(w_ref[...], g, m_ref[...], v_ref[...])

    blk = pl.BlockSpec((None,) * lead + (tile, cols), lambda i: (0,) * lead + (i, 0))
    return pl.pallas_call(
        body, grid=(rows // tile,), in_specs=[blk, blk, blk, pl.BlockSpec((pieces.shape[0], tile, cols), lambda i: (0, i, 0))],
        out_specs=[blk] * 4, out_shape=[SDS(w.shape, F32)] * 4, name=name,
        compiler_params=_params(("parallel",)))(w, m, v, pieces)


def _sum_pieces(p_ref):
    g = p_ref[0].astype(F32)
    for p in range(1, p_ref.shape[0]):
        g = g + p_ref[p].astype(F32)
    return g


def _adamw_s5_mat(w, m, v, g, name):
    _, ndir, groups, b, c = w.shape
    per_dir = groups // 8

    def body(w_ref, m_ref, v_ref, g_ref, d_ref, mo_ref, vo_ref):
        d_ref[...], mo_ref[...], vo_ref[...] = _adamw_math(w_ref[...], g_ref[...], m_ref[...], v_ref[...])

    blk = pl.BlockSpec((None, None, 8, b, c), lambda i: (0, i // per_dir, i % per_dir, 0, 0))
    return pl.pallas_call(
        body, grid=(ndir * per_dir,), in_specs=[blk] * 4, out_specs=[blk] * 3, out_shape=[SDS(w.shape, F32)] * 3, name=name,
        compiler_params=_params(("parallel",)))(w, m, v, g)


VEC_ROWS = ['ffn1_pre_g', 'ffn1_post_g', 'mix_pre_g', 'mix_post_g', 'ffn2_pre_g', 'ffn2_post_g', 'final_g',
            ('na_out_g', 's5_out_g'), ('s5_d', 's5_b_glu')]
VEC_NAMES = [n for row in VEC_ROWS for n in ((row,) if isinstance(row, str) else row)]
VEC_PACK_ROWS = 16
LOSS_ROW = len(VEC_ROWS)


def _pack_vectors(grads, loss8):
    def body(*refs):
        o_ref = refs[-1]
        o_ref[...] = jnp.zeros_like(o_ref)
        o_ref[LOSS_ROW:LOSS_ROW + 1, 0:128] = refs[-2][0:1, :]
        k = 0
        for i, row in enumerate(VEC_ROWS):
            if isinstance(row, str):
                o_ref[i:i + 1, :] = refs[k][...]
                k += 1
            else:
                o_ref[i:i + 1, 0:NA_WIDTH] = refs[k][...]
                o_ref[i:i + 1, NA_WIDTH:] = refs[k + 1][...]
                k += 2

    return pl.pallas_call(body, out_shape=SDS((VEC_PACK_ROWS, D_MODEL), F32), name="pack_vectors",
                          compiler_params=_params())(*[grads[n] for n in VEC_NAMES], loss8)


def _sum8(pieces, name):
    def body(p_ref, o_ref):
        o_ref[...] = _sum_pieces(p_ref)

    return pl.pallas_call(body, out_shape=SDS(pieces.shape[1:], F32), name=name, compiler_params=_params())(pieces)


def _adamw_small(packed8, vec_wmv, others):
    n_vec, n_oth = len(VEC_NAMES), len(others)

    def body(*refs):
        p_ref = refs[0]
        ins = refs[1:1 + 3 * n_vec + 4 * n_oth]
        outs = refs[1 + 3 * n_vec + 4 * n_oth:]
        gsum = _sum_pieces(p_ref)
        outs[-1][...] = gsum[LOSS_ROW:LOSS_ROW + 1, 0:128]
        k = 0
        for i, row in enumerate(VEC_ROWS):
            parts = [(row, gsum[i:i + 1, :])] if isinstance(row, str) else \
                [(row[0], gsum[i:i + 1, 0:NA_WIDTH]), (row[1], gsum[i:i + 1, NA_WIDTH:])]
            for _, g in parts:
                w_ref, m_ref, v_ref = ins[3 * k:3 * k + 3]
                outs[4 * k][...] = g
                outs[4 * k + 1][...], outs[4 * k + 2][...], outs[4 * k + 3][...] = _adamw_math(w_ref[...], g, m_ref[...], v_ref[...])
                k += 1
        for j in range(n_oth):
            w_ref, m_ref, v_ref, g_ref = ins[3 * n_vec + 4 * j:3 * n_vec + 4 * j + 4]
            g = _sum_pieces(g_ref)
            g = g[tuple(slice(0, s) for s in w_ref.shape[1:])].reshape(w_ref.shape)
            o = outs[4 * (n_vec + j):4 * (n_vec + j) + 4]
            o[0][...] = g
            o[1][...], o[2][...], o[3][...] = _adamw_math(w_ref[...], g, m_ref[...], v_ref[...])

    args, out_shape = [packed8], []
    for w, m, v in vec_wmv:
        args += [w, m, v]
        out_shape += [SDS(w.shape, F32)] * 4
    for w, m, v, g in others:
        args += [w, m, v, g]
        out_shape += [SDS(w.shape, F32)] * 4
    out_shape += [SDS((1, 128), F32)]
    return pl.pallas_call(body, out_shape=out_shape, name="adamw_small", compiler_params=_params())(*args)


def _perm_rows(x):
    return x.reshape(SCAN_BLOCKS, SCAN_T, x.shape[-1]).transpose(1, 0, 2).reshape(SEQ, x.shape[-1])


def _unperm_rows(x):
    return x.reshape(SCAN_T, SCAN_BLOCKS, x.shape[-1]).transpose(1, 0, 2).reshape(SEQ, x.shape[-1])


def _block_diag(x):
    eye = np.eye(8, dtype=bool)[None, None, :, None, :, None]
    full = jnp.where(eye, x[:, :, :, :, None, :], 0.0)
    return full.reshape(2, S5_CHUNKS, 8 * x.shape[3], 8 * x.shape[4])


STORED_SWAPPED = {"ffn1_w_gate": (1, 2), "ffn1_w_up": (1, 2), "ffn2_w_gate": (1, 2), "ffn2_w_up": (1, 2),
                  "s5_b_re": (3, 4), "s5_b_im": (3, 4)}


def _stored(name, x):
    return jnp.swapaxes(x, *STORED_SWAPPED[name]) if name in STORED_SWAPPED else x


def _dep(x, token):
    return x if token is None else x + token


def _local_step(x, target, get_w, small, emit):
    bias = _rpb_expand(small["na_rpb"][0])
    lr = small["s5_lam_re"].reshape(64, S5_STATE)
    li = small["s5_lam_im"].reshape(64, S5_STATE)
    logdt = small["s5_log_dt"].reshape(64, 1)
    b_t = [_stored(n, small[n]).reshape(64, S5_GROUP, S5_STATE) for n in ("s5_b_re", "s5_b_im")]
    lbr, lbi, bbr, bbi = _s5_prep(lr, li, logdt, b_t[0], b_t[1])
    are = lbr.reshape(2, S5_CHUNKS, 1, ST_W)
    aim = lbi.reshape(2, S5_CHUNKS, 1, ST_W)
    bre = _block_diag(bbr.reshape(2, S5_CHUNKS, 8, S5_GROUP, S5_STATE)).astype(BF16)
    bim = _block_diag(bbi.reshape(2, S5_CHUNKS, 8, S5_GROUP, S5_STATE)).astype(BF16)
    c_t = [small[n].reshape(2, S5_CHUNKS, 8, S5_GROUP, S5_STATE).transpose(0, 1, 2, 4, 3) for n in ("s5_c_re", "s5_c_im")]
    cre = _block_diag(c_t[0]).astype(BF16)
    cim = _block_diag(c_t[1]).astype(BF16)
    tgt = jnp.concatenate([jnp.zeros((N_META, D_MODEL), F32), target], axis=0)

    h0, a1 = _embed_prenorm(get_w("meta", None)["meta_tokens"], x, small["ffn1_pre_g"])
    wts = dict(get_w("ffn1", [bias, are, aim, bre, bim, cre, cim, tgt, a1]))
    gate1, up1, f1 = _ffn_fwd(a1, wts["ffn1_w_gate"], wts["ffn1_w_up"], wts["ffn1_w_down"], "ffn1_fwd",
                              after=wts.get("tokens", ()))
    h1, a2 = _post_pre(f1, h0, small["ffn1_post_g"], small["mix_pre_g"], 0.5, "post_pre1")
    wts.update(get_w("w_in", a2))
    qkv = _proj_heads(a2, wts["w_in"])
    u = _proj_u(a2, wts["w_in"])
    ona = _na_fwd(qkv, bias)
    u_p = _perm_rows(u)
    sr, si, y2 = _s5_scan_fwd(u_p, bre, bim, are, aim, cre, cim)
    wts.update(get_w("mix", y2))
    os5_p, ypre_p = _s5_glu_fwd(u_p, y2, small["s5_d"], wts["s5_w_glu"], small["s5_b_glu"])
    os5 = _unperm_rows(os5_p)

    mix = _mix_out_fwd(ona, os5, small["na_out_g"], small["s5_out_g"], wts["w_out"])
    h2, a3 = _post_pre(mix, h1, small["mix_post_g"], small["ffn2_pre_g"], 1.0, "post_pre2")
    wts.update(get_w("ffn2", a3))
    gate2, up2, f2 = _ffn_fwd(a3, wts["ffn2_w_gate"], wts["ffn2_w_up"], wts["ffn2_w_down"], "ffn2_fwd")
    loss8, dh3, df2, g_final, g_ffn2_post = _final_loss(f2, h2, small["ffn2_post_g"], small["final_g"], tgt)

    da3, dwg2, dwu2, dwd2 = _ffn_bwd(df2, a3, gate2, up2, wts["ffn2_w_gate"], wts["ffn2_w_up"], wts["ffn2_w_down"], "ffn2_bwd")
    tok = emit("ffn2", {"ffn2_w_gate": dwg2, "ffn2_w_up": dwu2, "ffn2_w_down": dwd2})
    dh2, dmix, g_ffn2_pre, g_mix_post = _bwd_pre_post(da3, h2, _dep(small["ffn2_pre_g"], tok), dh3, mix, small["mix_post_g"], 1.0,
                                                      "bwd_pre_post2")
    dona, dos5, dwout, g_na_out, g_s5_out = _mix_out_bwd(dmix, ona, os5, small["na_out_g"], small["s5_out_g"], wts["w_out"])

    dypre_p, du_skip_p, dwglu, g_b_glu, g_s5_d = _s5_glu_bwd(_perm_rows(dos5), ypre_p, u_p, small["s5_d"], wts["s5_w_glu"],
                                                             small["s5_b_glu"])
    tok = emit("mix", {"s5_w_glu": dwglu.reshape(N_DEV, S5_WIDTH // N_DEV, S5_WIDTH).astype(BF16),
                       "w_out": dwout.reshape(N_DEV, D_MODEL // N_DEV, D_MODEL).astype(BF16)})
    du_p, dbr, dbi, dcr, dci, dar, dai = _s5_scan_bwd(dypre_p, du_skip_p, u_p, sr, si, bre, bim, _dep(are, tok), aim, cre, cim)
    du = _unperm_rows(du_p)
    per_group = (2 * S5_GROUPS, S5_GROUP, S5_STATE)
    g_lr, g_li, g_dt, g_br, g_bi = _s5_prep_bwd(lr, li, logdt, b_t[0], b_t[1], dar.reshape(64, S5_STATE),
                                                dai.reshape(64, S5_STATE), dbr.reshape(per_group), dbi.reshape(per_group))
    g_c = [dcr.reshape(per_group), dci.reshape(per_group)]

    dq, dk, dv, dbias = _na_bwd(qkv, bias, dona)
    g_rpb = _rpb_reduce(dbias)
    dense = jnp.stack([g.reshape(2 * S5_GROUPS, S5_STATE * S5_GROUP) for g in (g_br, g_bi, *g_c)])
    tok = emit("small", {"dense": dense, "na_rpb": g_rpb,
                         "s5_lam_re": g_lr.reshape(2, S5_GROUPS, S5_STATE), "s5_lam_im": g_li.reshape(2, S5_GROUPS, S5_STATE),
                         "s5_log_dt": g_dt.reshape(2, S5_GROUPS)})
    da2, dwin = _proj_bwd(dq, dk, dv, du, a2, wts["w_in"])
    tok2 = emit("w_in", {"w_in": dwin})
    tok = tok if tok2 is None else tok + tok2
    dh1, df1, g_mix_pre, g_ffn1_post = _bwd_pre_post(da2, h1, _dep(small["mix_pre_g"], tok), dh2, f1, small["ffn1_post_g"], 0.5,
                                                     "bwd_pre_post1")
    da1, dwg1, dwu1, dwd1 = _ffn_bwd(df1, a1, gate1, up1, wts["ffn1_w_gate"], wts["ffn1_w_up"], wts["ffn1_w_down"], "ffn1_bwd")
    grad_x, grad_meta, g_ffn1_pre = _bwd_embed(da1, h0, small["ffn1_pre_g"], dh1)
    vec_g = {
        "ffn1_pre_g": g_ffn1_pre, "ffn1_post_g": g_ffn1_post, "mix_pre_g": g_mix_pre, "s5_d": g_s5_d, "s5_b_glu": g_b_glu,
        "na_out_g": g_na_out, "s5_out_g": g_s5_out, "mix_post_g": g_mix_post,
        "ffn2_pre_g": g_ffn2_pre, "ffn2_post_g": g_ffn2_post, "final_g": g_final,
    }
    emit("vec", {"packed": _pack_vectors(vec_g, loss8), "meta_tokens": grad_meta})
    emit("ffn1", {"ffn1_w_gate": dwg1, "ffn1_w_up": dwu1, "ffn1_w_down": dwd1})
    return grad_x


WEIGHT_NAMES = ['meta_tokens', 'ffn1_pre_g', 'ffn1_post_g', 'ffn1_w_gate', 'ffn1_w_up', 'ffn1_w_down', 'mix_pre_g', 'w_in',
                'na_rpb', 's5_lam_re', 's5_lam_im', 's5_log_dt', 's5_b_re', 's5_b_im', 's5_c_re', 's5_c_im', 's5_d',
                's5_w_glu', 's5_b_glu', 'na_out_g', 's5_out_g', 'w_out', 'mix_post_g', 'ffn2_pre_g', 'ffn2_post_g',
                'ffn2_w_gate', 'ffn2_w_up', 'ffn2_w_down', 'final_g']
BIG_NAMES = ['ffn1_w_gate', 'ffn1_w_up', 'ffn1_w_down', 'w_in', 's5_w_glu', 'w_out', 'ffn2_w_gate', 'ffn2_w_up', 'ffn2_w_down']
SMALL_NAMES = [n for n in WEIGHT_NAMES if n not in BIG_NAMES and n != 'meta_tokens']
WHOLE_NAMES = ['na_rpb', 's5_lam_re', 's5_lam_im', 's5_log_dt']
LEAD_NAMES = ['s5_b_re', 's5_b_im', 's5_c_re', 's5_c_im']


def kernel(x, meta_tokens, ffn1_pre_g, ffn1_post_g, ffn1_w_gate, ffn1_w_up, ffn1_w_down, mix_pre_g, w_in, na_rpb, s5_lam_re, s5_lam_im, s5_log_dt, s5_b_re, s5_b_im, s5_c_re, s5_c_im, s5_d, s5_w_glu, s5_b_glu, na_out_g, s5_out_g, w_out, mix_post_g, ffn2_pre_g, ffn2_post_g, ffn2_w_gate, ffn2_w_up, ffn2_w_down, final_g, loss_target, m_meta_tokens, m_ffn1_pre_g, m_ffn1_post_g, m_ffn1_w_gate, m_ffn1_w_up, m_ffn1_w_down, m_mix_pre_g, m_w_in, m_na_rpb, m_s5_lam_re, m_s5_lam_im, m_s5_log_dt, m_s5_b_re, m_s5_b_im, m_s5_c_re, m_s5_c_im, m_s5_d, m_s5_w_glu, m_s5_b_glu, m_na_out_g, m_s5_out_g, m_w_out, m_mix_post_g, m_ffn2_pre_g, m_ffn2_post_g, m_ffn2_w_gate, m_ffn2_w_up, m_ffn2_w_down, m_final_g, v_meta_tokens, v_ffn1_pre_g, v_ffn1_post_g, v_ffn1_w_gate, v_ffn1_w_up, v_ffn1_w_down, v_mix_pre_g, v_w_in, v_na_rpb, v_s5_lam_re, v_s5_lam_im, v_s5_log_dt, v_s5_b_re, v_s5_b_im, v_s5_c_re, v_s5_c_im, v_s5_d, v_s5_w_glu, v_s5_b_glu, v_na_out_g, v_s5_out_g, v_w_out, v_mix_post_g, v_ffn2_pre_g, v_ffn2_post_g, v_ffn2_w_gate, v_ffn2_w_up, v_ffn2_w_down, v_final_g):
    args = dict(locals())
    w = {n: args[n] for n in WEIGHT_NAMES}
    m = {n: args["m_" + n] for n in WEIGHT_NAMES}
    v = {n: args["v_" + n] for n in WEIGHT_NAMES}

    small = {n: w[n] for n in SMALL_NAMES}

    pending = {}

    def start(group, names, arrays, gather, peers=ALL_PEERS, slot=_slot8):
        lands = _place_own(arrays, gather, "own_" + group, slot)
        send_sems, recv_sems, arrays, lands, token = _exchange_start(arrays, lands, gather, "start_" + group, peers, slot)
        pending[group] = (names, send_sems, recv_sems, arrays, lands, gather, peers, slot)
        return token

    def finish(group, after):
        names, send_sems, recv_sems, arrays, lands, gather, peers, slot = pending.pop(group)
        lands, token = _exchange_wait(send_sems, recv_sems, arrays, lands, after, gather, "wait_" + group, peers, slot)
        return dict(zip(names, lands)), token

    first = ["ffn1_w_gate", "ffn1_w_up", "ffn1_w_down"]
    def shard(n, token=None):
        return _dep(_stored(n, w[n])[0], None if token is None else token[0, 0]).astype(BF16)

    ffn_names = ("ffn1_w_gate", "ffn1_w_up", "ffn1_w_down", "ffn2_w_gate", "ffn2_w_up", "ffn2_w_down")
    later_groups = (("w_in", ["w_in"]), ("mix", ["s5_w_glu", "w_out"]), ("ffn2", ["ffn2_w_gate", "ffn2_w_up", "ffn2_w_down"]))
    (meta_full,), token0 = _exchange([w["meta_tokens"]], True, "gather_meta")
    token1 = start("ffn1", first, [shard(n, token0) for n in first], True, (SIBLING,) + CHIP_PEERS)
    meta_full = _dep(meta_full.transpose(1, 0, 2).reshape(N_META, D_MODEL), token1[0, 0])
    later_shards = {n: shard(n, token1) for _, names in later_groups for n in names}
    for n in ("na_rpb", "s5_lam_re"):
        small[n] = _dep(small[n], token1[0, 0])

    def get_w(group, after):
        if group == "meta":
            return {"meta_tokens": meta_full}
        if group == "ffn1":
            after = list(after) + list(later_shards.values())
        got, token = finish(group, after)
        if group == "ffn1":
            got = dict(zip(got, _forward_sibling(list(got.values()), "forward_ffn1")))
            got["tokens"] = [start(g, names + ["order"], [later_shards[n] for n in names] + [token], True) for g, names in later_groups]
        if group == "mix":
            got = {"s5_w_glu": got["s5_w_glu"].reshape(S5_WIDTH, S5_WIDTH), "w_out": got["w_out"].reshape(D_MODEL, D_MODEL)}
        return {n: (a.reshape(D_FF, D_MODEL) if n in ffn_names else a) for n, a in got.items()}

    tokens = {}

    def emit(group, grads):
        grads = {n: (g.reshape(N_DEV, FF_SHARD, D_MODEL) if n in ffn_names else g) for n, g in grads.items()}
        if group == "ffn1":
            mine = [g.reshape((N_DEV // 2, 2) + g.shape[1:]) for g in grads.values()]
            theirs = _swap_sibling(mine, "swap_g_ffn1", after=[tokens["vec"]])
            sums = [_sum_pairs(a, b, "pair_sum_" + n) for n, a, b in zip(grads, mine, theirs)]
            tokens[group] = start("g_ffn1", list(grads), sums, False, CHIP_PEERS, _slot4)
        else:
            tokens[group] = start("g_" + group, list(grads), list(grads.values()), group in ("small", "vec"))
        return tokens[group][0, 0]

    grad_x = _local_step(x[0], loss_target[0], get_w, small, emit)
    res = {}

    def update_shard(n, pieces):
        outs = _adamw(_stored(n, w[n]), _stored(n, m[n]), _stored(n, v[n]), pieces, "adamw_" + n)
        res[n] = [_stored(n, o) for o in outs]

    late = [grad_x, tokens["ffn1"]]
    for group in ("g_ffn2", "g_mix", "g_w_in"):
        for n, pieces in finish(group, late)[0].items():
            update_shard(n, pieces)
    g8 = finish("g_small", late)[0]
    dense = _sum8(g8["dense"], "sum_dense")
    for i, n in enumerate(LEAD_NAMES):
        g = dense[i].reshape(_stored(n, w[n]).shape)
        upd = _adamw_s5_mat(_stored(n, w[n]), _stored(n, m[n]), _stored(n, v[n]), g, "adamw_" + n)
        res[n] = [_stored(n, o) for o in [g] + list(upd)]

    done = [res[n][1] for n in ("ffn2_w_gate", "ffn2_w_up", "ffn2_w_down", "w_in", "w_out", "s5_w_glu") + tuple(LEAD_NAMES)]
    got = finish("g_vec", done)[0]
    packed8, gmeta8 = got["packed"], got["meta_tokens"]
    for n, pieces in finish("g_ffn1", packed8)[0].items():
        update_shard(n, pieces)
    _, _, _, me = _me()
    update_shard("meta_tokens", lax.dynamic_slice_in_dim(gmeta8, me * (D_MODEL // N_DEV), D_MODEL // N_DEV, axis=2))

    outs = _adamw_small(packed8, [(w[n], m[n], v[n]) for n in VEC_NAMES], [(w[n], m[n], v[n], g8[n]) for n in WHOLE_NAMES])
    for i, n in enumerate(VEC_NAMES + WHOLE_NAMES):
        res[n] = list(outs[4 * i:4 * i + 4])

    out = [outs[-1][0, 0], grad_x[None]]
    for kind in range(4):
        out += [res[n][kind] for n in WEIGHT_NAMES]
    return tuple(out)
```

```python
import math

import numpy as np
import jax
import jax.numpy as jnp
from jax import lax
from jax.experimental import pallas as pl
from jax.experimental.pallas import tpu as pltpu

F32 = jnp.float32
BF16 = jnp.bfloat16
SDS = jax.ShapeDtypeStruct

D_MODEL = 1024
N_TOK = 2048
N_META = 16
SEQ = N_TOK + N_META
ROW_TILE = 688
N_ROW_TILES = SEQ // ROW_TILE
N_DEV = 8
D_FF = 2816
FF_SHARD = D_FF // N_DEV
FF_TILE = 256
IN_SHARD = 256
NA_WIDTH = 512
S5_WIDTH = 512
HEADS = 8
HEAD_DIM = 64
GRID_W = 64
GRID_ROWS = N_TOK // GRID_W
KH = 8
KW = 16
NA_RB = 4
NA_KR = KH + NA_RB - 1
NA_BLOCKS = GRID_ROWS // NA_RB
NA_QB = NA_RB * GRID_W
NA_KB = NA_KR * GRID_W
NA_TYPES = 3
S5_GROUPS = 32
S5_GROUP = 16
S5_STATE = 64
S5_CHUNKS = 4
CH_W = S5_WIDTH // S5_CHUNKS
ST_W = S5_GROUPS * S5_STATE // S5_CHUNKS
SCAN_BLOCKS = 8
SCAN_T = SEQ // SCAN_BLOCKS
RMS_EPS = 1e-6
NEG_INF = -1e30
ATT_SCALE = HEAD_DIM ** -0.5
ADAM_LR, ADAM_B1, ADAM_B2, ADAM_EPS, ADAM_WD, ADAM_STEP = 0.001, 0.9, 0.999, 1e-08, 0.01, 10
VMEM_LIMIT = 56 * 1024 * 1024
MESH = pl.DeviceIdType.MESH


def _params(sem=None):
    return pltpu.CompilerParams(dimension_semantics=sem, vmem_limit_bytes=VMEM_LIMIT)


def _dot(a, b):
    return jnp.dot(a, b, preferred_element_type=F32)


def _dot_nt(a, b):
    return lax.dot_general(a, b, (((1,), (1,)), ((), ())), preferred_element_type=F32)


def _dot_tn(a, b):
    return lax.dot_general(a, b, (((0,), (0,)), ((), ())), preferred_element_type=F32)


def _rstd(x):
    return lax.rsqrt(jnp.mean(x * x, axis=-1, keepdims=True) + RMS_EPS)


def _rms_bwd(x, r, g, dy):
    dyg = dy * g
    xr = x * r
    dx = r * (dyg - xr * jnp.mean(dyg * xr, axis=-1, keepdims=True))
    return dx, dy * xr


def _rows(i, size=ROW_TILE):
    return pl.ds(pl.multiple_of(i * size, 16), size)


def _row_spec(width):
    return pl.BlockSpec((ROW_TILE, width), lambda i: (i, 0))


def _fix_spec(shape):
    return pl.BlockSpec(shape, lambda i: (0,) * len(shape))


def _split3(x):
    hi = x.astype(BF16)
    r1 = x - hi.astype(F32)
    mid = r1.astype(BF16)
    lo = (r1 - mid.astype(F32)).astype(BF16)
    return hi, mid, lo


def _embed_prenorm(meta, x, g):
    def body(m_ref, x_ref, g_ref, h_ref, a_ref):
        h_ref[0:N_META, :] = m_ref[...]
        h_ref[N_META:, :] = x_ref[...]
        for i in range(N_ROW_TILES):
            rows = slice(i * ROW_TILE, (i + 1) * ROW_TILE)
            hv = h_ref[rows, :]
            a_ref[rows, :] = (hv * _rstd(hv) * g_ref[...]).astype(BF16)

    return pl.pallas_call(
        body, out_shape=[SDS((SEQ, D_MODEL), F32), SDS((SEQ, D_MODEL), BF16)], name="embed_prenorm",
        compiler_params=_params())(meta, x, g)


def _post_pre(f, hres, g_post, g_next, scale, name):
    def body(f_ref, h_ref, gp_ref, gn_ref, ho_ref, a_ref):
        fv = f_ref[...]
        h = h_ref[...] + scale * (fv * _rstd(fv) * gp_ref[...])
        ho_ref[...] = h
        a_ref[...] = (h * _rstd(h) * gn_ref[...]).astype(BF16)

    return pl.pallas_call(
        body, grid=(N_ROW_TILES,),
        in_specs=[_row_spec(D_MODEL), _row_spec(D_MODEL), _fix_spec((1, D_MODEL)), _fix_spec((1, D_MODEL))],
        out_specs=[_row_spec(D_MODEL), _row_spec(D_MODEL)],
        out_shape=[SDS((SEQ, D_MODEL), F32), SDS((SEQ, D_MODEL), BF16)], name=name,
        compiler_params=_params(("parallel",)))(f, hres, g_post, g_next)


def _final_loss(f2, h2, g_post, g_final, target):
    def body(f_ref, h_ref, gp_ref, gf_ref, t_ref, loss_ref, dh_ref, df_ref, dgf_ref, dgp_ref):
        i = pl.program_id(0)
        fv = f_ref[...]
        r1 = _rstd(fv)
        gp = gp_ref[...]
        h3 = h_ref[...] + 0.5 * (fv * r1 * gp)
        r2 = _rstd(h3)
        gf = gf_ref[...]
        y = h3 * r2 * gf
        row = lax.broadcasted_iota(jnp.int32, (ROW_TILE, 1), 0) + i * ROW_TILE
        err = jnp.where(row >= N_META, y - t_ref[...], 0.0)
        part = 0.5 * jnp.sum(jnp.mean(err * err, axis=-1, keepdims=True))
        dy = err * (1.0 / D_MODEL)
        dh3, dgf = _rms_bwd(h3, r2, gf, dy)
        dh_ref[...] = dh3
        df, dgp = _rms_bwd(fv, r1, gp, 0.5 * dh3)
        df_ref[...] = df.astype(BF16)

        @pl.when(i == 0)
        def _():
            loss_ref[...] = jnp.zeros_like(loss_ref)
            dgf_ref[...] = jnp.zeros_like(dgf_ref)
            dgp_ref[...] = jnp.zeros_like(dgp_ref)

        loss_ref[...] += part
        dgf_ref[...] += jnp.sum(dgf, axis=0, keepdims=True)
        dgp_ref[...] += jnp.sum(dgp, axis=0, keepdims=True)

    gain = _fix_spec((1, D_MODEL))
    return pl.pallas_call(
        body, grid=(N_ROW_TILES,),
        in_specs=[_row_spec(D_MODEL), _row_spec(D_MODEL), gain, gain, _row_spec(D_MODEL)],
        out_specs=[_fix_spec((8, 128)), _row_spec(D_MODEL), _row_spec(D_MODEL), gain, gain],
        out_shape=[SDS((8, 128), F32), SDS((SEQ, D_MODEL), F32), SDS((SEQ, D_MODEL), BF16),
                   SDS((1, D_MODEL), F32), SDS((1, D_MODEL), F32)],
        name="final_loss", compiler_params=_params(("arbitrary",)))(f2, h2, g_post, g_final, target)


def _bwd_pre_post(da, h, g_pre, dh_res, fprev, g_post, scale, name):
    def body(da_ref, h_ref, gpre_ref, dhr_ref, f_ref, gpost_ref, dh_ref, df_ref, dgpre_ref, dgpost_ref):
        i = pl.program_id(0)
        hv = h_ref[...]
        dxa, dgpre = _rms_bwd(hv, _rstd(hv), gpre_ref[...], da_ref[...])
        dh = dhr_ref[...] + dxa
        dh_ref[...] = dh
        fv = f_ref[...]
        df, dgpost = _rms_bwd(fv, _rstd(fv), gpost_ref[...], scale * dh)
        df_ref[...] = df.astype(BF16)

        @pl.when(i == 0)
        def _():
            dgpre_ref[...] = jnp.zeros_like(dgpre_ref)
            dgpost_ref[...] = jnp.zeros_like(dgpost_ref)

        dgpre_ref[...] += jnp.sum(dgpre, axis=0, keepdims=True)
        dgpost_ref[...] += jnp.sum(dgpost, axis=0, keepdims=True)

    gain = _fix_spec((1, D_MODEL))
    row = _row_spec(D_MODEL)
    return pl.pallas_call(
        body, grid=(N_ROW_TILES,), in_specs=[row, row, gain, row, row, gain],
        out_specs=[row, row, gain, gain],
        out_shape=[SDS((SEQ, D_MODEL), F32), SDS((SEQ, D_MODEL), BF16), SDS((1, D_MODEL), F32), SDS((1, D_MODEL), F32)],
        name=name, compiler_params=_params(("arbitrary",)))(da, h, g_pre, dh_res, fprev, g_post)


def _bwd_embed(da, h, g_pre, dh_res):
    def body(da_ref, h_ref, gpre_ref, dhr_ref, gx_ref, gm_ref, dgpre_ref):
        total = jnp.zeros((1, D_MODEL), F32)
        for i in range(N_ROW_TILES):
            rows = slice(i * ROW_TILE, (i + 1) * ROW_TILE)
            hv = h_ref[rows, :]
            dxa, dgpre = _rms_bwd(hv, _rstd(hv), gpre_ref[...], da_ref[rows, :])
            dh = dhr_ref[rows, :] + dxa
            total = total + jnp.sum(dgpre, axis=0, keepdims=True)
            if i == 0:
                gm_ref[...] = dh[0:N_META, :]
                gx_ref[0:ROW_TILE - N_META, :] = dh[N_META:, :]
            else:
                gx_ref[i * ROW_TILE - N_META:(i + 1) * ROW_TILE - N_META, :] = dh
        dgpre_ref[...] = total

    return pl.pallas_call(
        body, out_shape=[SDS((N_TOK, D_MODEL), F32), SDS((N_META, D_MODEL), F32), SDS((1, D_MODEL), F32)],
        name="bwd_embed", compiler_params=_params())(da, h, g_pre, dh_res)


def _ffn_fwd(a, wg, wu, wd, name, after=()):
    def body(a_ref, wg_ref, wu_ref, wd_ref, *rest):
        gate_ref, up_ref, f_ref = rest[len(after):]
        j = pl.program_id(0)
        wgu = jnp.concatenate([wg_ref[...], wu_ref[...]], axis=0)

        def tile(i, carry):
            rows = _rows(i)
            at = a_ref[rows, :]
            gu = _dot_nt(at, wgu)
            gate, up = gu[:, 0:FF_TILE], gu[:, FF_TILE:]
            gate_ref[rows, :] = gate.astype(BF16)
            up_ref[rows, :] = up.astype(BF16)
            act = (gate * jax.nn.sigmoid(gate) * up).astype(BF16)
            contrib = _dot(act, wd_ref[...])

            @pl.when(j == 0)
            def _():
                f_ref[rows, :] = contrib

            @pl.when(j != 0)
            def _():
                f_ref[rows, :] += contrib

            return carry

        lax.fori_loop(0, N_ROW_TILES, tile, 0)

    wtile = pl.BlockSpec((FF_TILE, D_MODEL), lambda j: (j, 0))
    hid = pl.BlockSpec((SEQ, FF_TILE), lambda j: (0, j))
    full = pl.BlockSpec((SEQ, D_MODEL), lambda j: (0, 0))
    return pl.pallas_call(
        body, grid=(D_FF // FF_TILE,), in_specs=[full, wtile, wtile, wtile] + [pl.BlockSpec(memory_space=pl.ANY)] * len(after),
        out_specs=[hid, hid, full],
        out_shape=[SDS((SEQ, D_FF), BF16), SDS((SEQ, D_FF), BF16), SDS((SEQ, D_MODEL), F32)],
        name=name, compiler_params=_params(("arbitrary",)))(a, wg, wu, wd, *after)


def _ffn_bwd(df, a, gate, up, wg, wu, wd, name):
    def body(df_ref, a_ref, gate_ref, up_ref, wg_ref, wu_ref, wd_ref, da_ref, dwg_ref, dwu_ref, dwd_ref,
             acc_gu, acc_d):
        j = pl.program_id(0)
        wgu = jnp.concatenate([wg_ref[...], wu_ref[...]], axis=0)

        def tile(i, carry):
            rows = _rows(i)
            dft = df_ref[rows, :]
            at = a_ref[rows, :]
            gate = gate_ref[rows, :].astype(F32)
            up = up_ref[rows, :].astype(F32)
            dact = _dot_nt(dft, wd_ref[...])
            sig = jax.nn.sigmoid(gate)
            silu = gate * sig
            dgu = jnp.concatenate([(dact * up * (sig * (1.0 + gate * (1.0 - sig)))).astype(BF16),
                                   (dact * silu).astype(BF16)], axis=1)
            act = (silu * up).astype(BF16)
            dwd = _dot_tn(act, dft)
            dwgu = _dot_tn(dgu, at)
            dat = _dot(dgu, wgu)

            @pl.when(i == 0)
            def _():
                acc_d[...] = dwd
                acc_gu[...] = dwgu

            @pl.when(i != 0)
            def _():
                acc_d[...] += dwd
                acc_gu[...] += dwgu

            @pl.when(j == 0)
            def _():
                da_ref[rows, :] = dat

            @pl.when(j != 0)
            def _():
                da_ref[rows, :] += dat

            return carry

        lax.fori_loop(0, N_ROW_TILES, tile, 0)
        dwg_ref[...] = acc_gu[0:FF_TILE, :].astype(BF16)
        dwu_ref[...] = acc_gu[FF_TILE:, :].astype(BF16)
        dwd_ref[...] = acc_d[...].astype(BF16)

    wtile = pl.BlockSpec((FF_TILE, D_MODEL), lambda j: (j, 0))
    hid = pl.BlockSpec((SEQ, FF_TILE), lambda j: (0, j))
    full = pl.BlockSpec((SEQ, D_MODEL), lambda j: (0, 0))
    return pl.pallas_call(
        body, grid=(D_FF // FF_TILE,), in_specs=[full, full, hid, hid, wtile, wtile, wtile],
        out_specs=[full, wtile, wtile, wtile],
        out_shape=[SDS((SEQ, D_MODEL), F32)] + [SDS((D_FF, D_MODEL), BF16)] * 3,
        scratch_shapes=[pltpu.VMEM((2 * FF_TILE, D_MODEL), F32), pltpu.VMEM((FF_TILE, D_MODEL), F32)],
        name=name, compiler_params=_params(("arbitrary",)))(df, a, gate, up, wg, wu, wd)


HEADS_PER_BLOCK = IN_SHARD // HEAD_DIM
QKV_BLOCKS = 3 * NA_WIDTH // IN_SHARD


def _proj_heads(a, w):
    def body(a_ref, w_ref, o_ref):
        def tile(i, carry):
            rows = _rows(i)
            res = _dot(a_ref[rows, :], w_ref[...])
            for sub in range(HEADS_PER_BLOCK):
                o_ref[sub, rows, :] = res[:, sub * HEAD_DIM:(sub + 1) * HEAD_DIM]
            return carry

        lax.fori_loop(0, N_ROW_TILES, tile, 0)

    return pl.pallas_call(
        body, grid=(QKV_BLOCKS,),
        in_specs=[pl.BlockSpec((SEQ, D_MODEL), lambda j: (0, 0)), pl.BlockSpec((None, D_MODEL, IN_SHARD), lambda j: (j, 0, 0))],
        out_specs=pl.BlockSpec((HEADS_PER_BLOCK, SEQ, HEAD_DIM), lambda j: (j, 0, 0)),
        out_shape=SDS((3 * HEADS, SEQ, HEAD_DIM), F32), name="proj_heads",
        compiler_params=_params(("parallel",)))(a, w)


def _proj_u(a, w):
    def body(a_ref, w_ref, o_ref):
        def tile(i, carry):
            rows = _rows(i)
            o_ref[rows, :] = _dot(a_ref[rows, :], w_ref[...])
            return carry

        lax.fori_loop(0, N_ROW_TILES, tile, 0)

    return pl.pallas_call(
        body, grid=(N_DEV - QKV_BLOCKS,),
        in_specs=[pl.BlockSpec((SEQ, D_MODEL), lambda j: (0, 0)),
                  pl.BlockSpec((None, D_MODEL, IN_SHARD), lambda j: (j + QKV_BLOCKS, 0, 0))],
        out_specs=pl.BlockSpec((SEQ, IN_SHARD), lambda j: (0, j)),
        out_shape=SDS((SEQ, S5_WIDTH), F32), name="proj_u",
        compiler_params=_params(("parallel",)))(a, w)


def _proj_bwd(dq, dk, dv, du, a, w):
    def body(dq_ref, dk_ref, dv_ref, du_ref, a_ref, w_ref, da_ref, dw_ref, acc, dp_ref):
        j = pl.program_id(0)

        for which, src in enumerate((dq_ref, dk_ref, dv_ref)):
            @pl.when((j >= 2 * which) & (j < 2 * which + 2))
            def _(src=src):
                dp_ref[...] = jnp.concatenate([src[sub] for sub in range(HEADS_PER_BLOCK)], axis=-1).astype(BF16)

        @pl.when(j >= QKV_BLOCKS)
        def _():
            dp_ref[...] = du_ref[...].astype(BF16)

        def tile(i, carry):
            rows = _rows(i)
            dpt = dp_ref[rows, :]
            dw = _dot_tn(a_ref[rows, :], dpt)
            dat = _dot_nt(dpt, w_ref[...])

            @pl.when(i == 0)
            def _():
                acc[...] = dw

            @pl.when(i != 0)
            def _():
                acc[...] += dw

            @pl.when(j == 0)
            def _():
                da_ref[rows, :] = dat

            @pl.when(j != 0)
            def _():
                da_ref[rows, :] += dat

            return carry

        lax.fori_loop(0, N_ROW_TILES, tile, 0)
        dw_ref[...] = acc[...].astype(BF16)

    full = pl.BlockSpec((SEQ, D_MODEL), lambda j: (0, 0))
    wspec = pl.BlockSpec((None, D_MODEL, IN_SHARD), lambda j: (j, 0, 0))

    def heads(which):
        return pl.BlockSpec((HEADS_PER_BLOCK, SEQ, HEAD_DIM), lambda j: (jnp.clip(j - 2 * which, 0, 1), 0, 0))

    return pl.pallas_call(
        body, grid=(N_DEV,),
        in_specs=[heads(0), heads(1), heads(2),
                  pl.BlockSpec((SEQ, IN_SHARD), lambda j: (0, jnp.clip(j - QKV_BLOCKS, 0, 1))), full, wspec],
        out_specs=[full, wspec],
        out_shape=[SDS((SEQ, D_MODEL), F32), SDS((N_DEV, D_MODEL, IN_SHARD), BF16)],
        scratch_shapes=[pltpu.VMEM((D_MODEL, IN_SHARD), F32), pltpu.VMEM((SEQ, IN_SHARD), BF16)],
        name="proj_bwd", compiler_params=_params(("arbitrary",)))(dq, dk, dv, du, a, w)


def _na_consts():
    c = np.arange(GRID_W)
    col_start = np.clip(c - KW // 2, 0, GRID_W - KW)
    col_in = (c[None, :] >= col_start[:, None]) & (c[None, :] < col_start[:, None] + KW)
    dc = np.clip(c[None, :] - c[:, None] + KW - 1, 0, 2 * KW - 2)
    onehot = np.zeros((128, GRID_W * GRID_W), np.float32)
    qq, kk = np.meshgrid(c, c, indexing="ij")
    onehot[dc[col_in], (qq * GRID_W + kk)[col_in]] = 1.0
    negmask = np.where(col_in, 0.0, NEG_INF).astype(np.float32).reshape(1, -1)
    return onehot, negmask


def _na_pair(block_type, a, b):
    if block_type == 0:
        return b - a + KH - 1 if b < KH else None
    if block_type == 1:
        return b - a + KH // 2 - 1 if a <= b < a + KH else None
    return b - a if b >= NA_KR - KH else None


def _rpb_expand(rpb):
    onehot, negmask = _na_consts()
    rows = HEADS * (2 * KH - 1)
    rpb_pad = jnp.pad(rpb.reshape(rows, 2 * KW - 1), ((0, 128 - rows), (0, 128 - (2 * KW - 1))))

    def body(r_ref, oh_ref, m_ref, t_ref):
        hi, mid, lo = _split3(r_ref[...])
        oh = oh_ref[...]
        t_ref[...] = _dot(hi, oh) + _dot(mid, oh) + _dot(lo, oh) + m_ref[...]

    table = pl.pallas_call(body, out_shape=SDS((128, GRID_W * GRID_W), F32), name="rpb_expand",
                           compiler_params=_params())(rpb_pad, jnp.asarray(onehot, BF16), jnp.asarray(negmask))
    return table[:rows].reshape(HEADS, 2 * KH - 1, GRID_W, GRID_W)


def _rpb_reduce(dslabs):
    onehot, _ = _na_consts()
    rows = HEADS * (2 * KH - 1)

    def body(x_ref, oht_ref, o_ref):
        hi, mid, lo = _split3(x_ref[...])
        oht = oht_ref[...]
        o_ref[...] = _dot(hi, oht) + _dot(mid, oht) + _dot(lo, oht)

    out = pl.pallas_call(body, out_shape=SDS((rows, 128), F32), name="rpb_reduce", compiler_params=_params())(
        dslabs.reshape(rows, GRID_W * GRID_W), jnp.asarray(onehot.T, BF16))
    return out.reshape(HEADS, 2 * KH - 1, 128)


def _bias_tiles(slab_ref, tile_ref):
    tile_ref[...] = jnp.full(tile_ref.shape, NEG_INF, F32)
    for t in range(NA_TYPES):
        for a in range(NA_RB):
            for b in range(NA_KR):
                dr = _na_pair(t, a, b)
                if dr is not None:
                    tile_ref[t, a * GRID_W:(a + 1) * GRID_W, b * GRID_W:(b + 1) * GRID_W] = slab_ref[dr]


def _bias_tiles_bwd(dtile_ref, dslab_ref):
    acc = {}
    for t in range(NA_TYPES):
        for a in range(NA_RB):
            for b in range(NA_KR):
                dr = _na_pair(t, a, b)
                if dr is not None:
                    part = dtile_ref[t, a * GRID_W:(a + 1) * GRID_W, b * GRID_W:(b + 1) * GRID_W]
                    acc[dr] = part if dr not in acc else acc[dr] + part
    for dr in range(2 * KH - 1):
        dslab_ref[dr] = acc[dr]


def _block_geometry(g):
    start = jnp.clip(g * NA_RB - KH // 2, 0, GRID_ROWS - NA_KR)
    block_type = jnp.where(g == 0, 0, jnp.where(g == NA_BLOCKS - 1, 2, 1))
    q0 = pl.multiple_of(N_META + g * NA_QB, 16)
    k0 = pl.multiple_of(N_META + start * GRID_W, 16)
    return block_type, q0, k0


def _na_probs(q, kk, km, bias):
    s = _dot_nt(q, kk) * ATT_SCALE + bias
    sm = _dot_nt(q, km) * ATT_SCALE
    m = jnp.maximum(jnp.max(s, axis=-1, keepdims=True), jnp.max(sm, axis=-1, keepdims=True))
    p = jnp.exp(s - m)
    pm = jnp.exp(sm - m)
    inv = 1.0 / (jnp.sum(p, axis=-1, keepdims=True) + jnp.sum(pm, axis=-1, keepdims=True))
    return p * inv, pm * inv


def _meta_probs(qm, km):
    s = _dot_nt(qm, km) * ATT_SCALE
    p = jnp.exp(s - jnp.max(s, axis=-1, keepdims=True))
    return p / jnp.sum(p, axis=-1, keepdims=True)


def _qkv_specs():
    return [pl.BlockSpec((None, SEQ, HEAD_DIM), lambda h, which=which: (h + which * HEADS, 0, 0)) for which in range(3)]


def _na_fwd(qkv, bias):
    def body(q_ref, k_ref, v_ref, slab_ref, o_ref, b_ref):
        _bias_tiles(slab_ref, b_ref)
        km = k_ref[0:N_META, :].astype(BF16)
        vm = v_ref[0:N_META, :].astype(BF16)
        pmm = _meta_probs(q_ref[0:N_META, :].astype(BF16), km)
        o_ref[0:N_META, :] = _dot(pmm.astype(BF16), vm)

        def block(g, carry):
            block_type, q0, k0 = _block_geometry(g)
            qb = q_ref[pl.ds(q0, NA_QB), :].astype(BF16)
            kk = k_ref[pl.ds(k0, NA_KB), :].astype(BF16)
            vv = v_ref[pl.ds(k0, NA_KB), :].astype(BF16)
            p, pm = _na_probs(qb, kk, km, b_ref[block_type])
            o_ref[pl.ds(q0, NA_QB), :] = _dot(p.astype(BF16), vv) + _dot(pm.astype(BF16), vm)
            return carry

        lax.fori_loop(0, NA_BLOCKS, block, 0)

    head = pl.BlockSpec((None, SEQ, HEAD_DIM), lambda h: (h, 0, 0))
    return pl.pallas_call(
        body, grid=(HEADS,), in_specs=_qkv_specs() + [pl.BlockSpec((None, 2 * KH - 1, GRID_W, GRID_W), lambda h: (h, 0, 0, 0))],
        out_specs=head, out_shape=SDS((HEADS, SEQ, HEAD_DIM), F32), name="na_fwd",
        scratch_shapes=[pltpu.VMEM((NA_TYPES, NA_QB, NA_KB), F32)],
        compiler_params=_params(("parallel",)))(qkv, qkv, qkv, bias)


def _na_bwd(qkv, bias, do):
    def body(q_ref, k_ref, v_ref, slab_ref, do_ref, dq_ref, dk_ref, dv_ref, dslab_ref, b_ref, db_ref):
        _bias_tiles(slab_ref, b_ref)
        km = k_ref[0:N_META, :].astype(BF16)
        vm = v_ref[0:N_META, :].astype(BF16)
        dk_ref[...] = jnp.zeros_like(dk_ref)
        dv_ref[...] = jnp.zeros_like(dv_ref)
        db_ref[...] = jnp.zeros_like(db_ref)

        qm = q_ref[0:N_META, :].astype(BF16)
        dom = do_ref[0:N_META, :].astype(BF16)
        pmm = _meta_probs(qm, km)
        dpm = _dot_nt(dom, vm)
        dsm = (pmm * (dpm - jnp.sum(pmm * dpm, axis=-1, keepdims=True)) * ATT_SCALE).astype(BF16)
        dq_ref[0:N_META, :] = _dot(dsm, km)
        dkm0 = _dot_tn(dsm, qm)
        dvm0 = _dot_tn(pmm.astype(BF16), dom)

        def block(g, carry):
            dkm, dvm = carry
            block_type, q0, k0 = _block_geometry(g)
            qb = q_ref[pl.ds(q0, NA_QB), :].astype(BF16)
            kk = k_ref[pl.ds(k0, NA_KB), :].astype(BF16)
            vv = v_ref[pl.ds(k0, NA_KB), :].astype(BF16)
            dob = do_ref[pl.ds(q0, NA_QB), :].astype(BF16)
            p, pm = _na_probs(qb, kk, km, b_ref[block_type])
            dp = _dot_nt(dob, vv)
            dpm_ = _dot_nt(dob, vm)
            delta = jnp.sum(p * dp, axis=-1, keepdims=True) + jnp.sum(pm * dpm_, axis=-1, keepdims=True)
            ds = p * (dp - delta)
            dsm_ = pm * (dpm_ - delta)
            db_ref[block_type] += ds
            dsb = (ds * ATT_SCALE).astype(BF16)
            dsmb = (dsm_ * ATT_SCALE).astype(BF16)
            dq_ref[pl.ds(q0, NA_QB), :] = _dot(dsb, kk) + _dot(dsmb, km)
            dk_ref[pl.ds(k0, NA_KB), :] += _dot_tn(dsb, qb)
            dv_ref[pl.ds(k0, NA_KB), :] += _dot_tn(p.astype(BF16), dob)
            return dkm + _dot_tn(dsmb, qb), dvm + _dot_tn(pm.astype(BF16), dob)

        dkm, dvm = lax.fori_loop(0, NA_BLOCKS, block, (dkm0, dvm0))
        dk_ref[0:N_META, :] = dkm
        dv_ref[0:N_META, :] = dvm
        _bias_tiles_bwd(db_ref, dslab_ref)

    head = pl.BlockSpec((None, SEQ, HEAD_DIM), lambda h: (h, 0, 0))
    bspec = pl.BlockSpec((None, 2 * KH - 1, GRID_W, GRID_W), lambda h: (h, 0, 0, 0))
    return pl.pallas_call(
        body, grid=(HEADS,), in_specs=_qkv_specs() + [bspec, head], out_specs=[head, head, head, bspec],
        out_shape=[SDS((HEADS, SEQ, HEAD_DIM), F32)] * 3 + [SDS((HEADS, 2 * KH - 1, GRID_W, GRID_W), F32)],
        scratch_shapes=[pltpu.VMEM((NA_TYPES, NA_QB, NA_KB), F32), pltpu.VMEM((NA_TYPES, NA_QB, NA_KB), F32)],
        name="na_bwd", compiler_params=_params(("parallel",)))(qkv, qkv, qkv, bias, do)


def _cmul(ar, ai, br, bi):
    return ar * br - ai * bi, ar * bi + ai * br


def _cpow(ar, ai, n):
    rr, ri = None, None
    br, bi = ar, ai
    while n:
        if n & 1:
            rr, ri = (br, bi) if rr is None else _cmul(rr, ri, br, bi)
        n >>= 1
        if n:
            br, bi = _cmul(br, bi, br, bi)
    return rr, ri


def _s5_prep(lr, li, logdt, bre, bim):
    def body(lr_ref, li_ref, dt_ref, br_ref, bi_ref, lbr_ref, lbi_ref, bbr_ref, bbi_ref):
        lr_, li_ = lr_ref[...], li_ref[...]
        dt = jnp.exp(dt_ref[...])
        mag = jnp.exp(lr_ * dt)
        lbr = mag * jnp.cos(li_ * dt)
        lbi = mag * jnp.sin(li_ * dt)
        lbr_ref[...] = lbr
        lbi_ref[...] = lbi
        den = lr_ * lr_ + li_ * li_
        xr = lbr - 1.0
        cr = (xr * lr_ + lbi * li_) / den
        ci = (lbi * lr_ - xr * li_) / den
        br, bi = br_ref[...], bi_ref[...]
        bbr_ref[...] = cr[:, None, :] * br - ci[:, None, :] * bi
        bbi_ref[...] = cr[:, None, :] * bi + ci[:, None, :] * br

    n = 2 * S5_GROUPS
    return pl.pallas_call(
        body, out_shape=[SDS((n, S5_STATE), F32)] * 2 + [SDS((n, S5_GROUP, S5_STATE), F32)] * 2,
        name="s5_prep", compiler_params=_params())(lr, li, logdt, bre, bim)


def _s5_prep_bwd(lr, li, logdt, bre, bim, dar, dai, dbbr, dbbi):
    def body(lr_ref, li_ref, dt_ref, br_ref, bi_ref, dar_ref, dai_ref, dbr_ref, dbi_ref,
             glr_ref, gli_ref, gdt_ref, gbr_ref, gbi_ref):
        lr_, li_ = lr_ref[...], li_ref[...]
        dt = jnp.exp(dt_ref[...])
        mag = jnp.exp(lr_ * dt)
        lbr = mag * jnp.cos(li_ * dt)
        lbi = mag * jnp.sin(li_ * dt)
        den = lr_ * lr_ + li_ * li_
        xr = lbr - 1.0
        cr = (xr * lr_ + lbi * li_) / den
        ci = (lbi * lr_ - xr * li_) / den
        br, bi = br_ref[...], bi_ref[...]
        dbr, dbi = dbr_ref[...], dbi_ref[...]
        gbr_ref[...] = cr[:, None, :] * dbr + ci[:, None, :] * dbi
        gbi_ref[...] = cr[:, None, :] * dbi - ci[:, None, :] * dbr
        gcr = jnp.sum(dbr * br + dbi * bi, axis=1)
        gci = jnp.sum(dbi * br - dbr * bi, axis=1)
        ilr, ili = lr_ / den, li_ / den
        tr, ti = _cmul(gcr, gci, ilr, ili)
        glbr = dar_ref[...] + tr
        glbi = dai_ref[...] + ti
        dr_, di_ = _cmul(tr, ti, cr, -ci)
        gwr, gwi = _cmul(glbr, glbi, lbr, -lbi)
        glr_ref[...] = gwr * dt - dr_
        gli_ref[...] = gwi * dt - di_
        gdt_ref[...] = jnp.sum(gwr * lr_ + gwi * li_, axis=-1, keepdims=True) * dt

    n = 2 * S5_GROUPS
    return pl.pallas_call(
        body, out_shape=[SDS((n, S5_STATE), F32)] * 2 + [SDS((n, 1), F32)] + [SDS((n, S5_GROUP, S5_STATE), F32)] * 2,
        name="s5_prep_bwd", compiler_params=_params())(lr, li, logdt, bre, bim, dar, dai, dbbr, dbbi)


def _scan_local(xr_ref, xi_ref, ar8, ai8, reverse):
    def step(i, carry):
        sr, si = carry
        idx = (SCAN_T - 1 - i) if reverse else i
        rows = pl.ds(pl.multiple_of(idx * SCAN_BLOCKS, SCAN_BLOCKS), SCAN_BLOCKS)
        nr = ar8 * sr - ai8 * si + xr_ref[rows, :]
        ni = ar8 * si + ai8 * sr + xi_ref[rows, :]
        xr_ref[rows, :] = nr
        xi_ref[rows, :] = ni
        return nr, ni

    z = jnp.zeros(ar8.shape, F32)
    return lax.fori_loop(0, SCAN_T, step, (z, z))


def _scan_carries(er, ei, atr, ati, reverse):
    row = lax.broadcasted_iota(jnp.int32, er.shape, 0)
    cr = jnp.zeros((1, er.shape[1]), F32)
    ci = cr
    outr = jnp.zeros(er.shape, F32)
    outi = outr
    order = range(SCAN_BLOCKS - 1, -1, -1) if reverse else range(SCAN_BLOCKS)
    for b in order:
        outr = jnp.where(row == b, cr, outr)
        outi = jnp.where(row == b, ci, outi)
        nr, ni = _cmul(atr, ati, cr, ci)
        cr, ci = nr + er[b:b + 1, :], ni + ei[b:b + 1, :]
    return outr, outi


def _scan_fixup(xr_ref, xi_ref, cr8, ci8, ar8, ai8, reverse, pair=None):
    tile = lambda idx: pl.ds(pl.multiple_of(idx * SCAN_BLOCKS, SCAN_BLOCKS), SCAN_BLOCKS)

    def fix(idx, pr, pi):
        fr, fi = _cmul(pr, pi, cr8, ci8)
        nr, ni = xr_ref[tile(idx), :] + fr, xi_ref[tile(idx), :] + fi
        xr_ref[tile(idx), :] = nr
        xi_ref[tile(idx), :] = ni
        return nr, ni

    if pair is None:
        def step(i, carry):
            pr, pi = carry
            fix((SCAN_T - 1 - i) if reverse else i, pr, pi)
            return _cmul(pr, pi, ar8, ai8)

        lax.fori_loop(0, SCAN_T, step, (ar8, ai8), unroll=2)
        return None

    sr_ref, si_ref = pair
    earlier = -1 if reverse else 1

    def step(i, carry):
        pr, pi, accr, acci = carry
        idx = (SCAN_T - 1 - i) if reverse else i
        nr, ni = fix(idx, pr, pi)
        qr, qi = _cmul(nr, ni, sr_ref[tile(idx + earlier), :], -si_ref[tile(idx + earlier), :])
        pr, pi = _cmul(pr, pi, ar8, ai8)
        return pr, pi, accr + qr, acci + qi

    z = jnp.zeros(ar8.shape, F32)
    pr, pi, accr, acci = lax.fori_loop(0, SCAN_T - 1, step, (ar8, ai8, z, z))
    edge, src, shift, empty = (0, SCAN_T - 1, 1, 0) if reverse else (SCAN_T - 1, 0, SCAN_BLOCKS - 1, SCAN_BLOCKS - 1)
    nr, ni = fix(edge, pr, pi)
    row = lax.broadcasted_iota(jnp.int32, ar8.shape, 0)
    spr = jnp.where(row == empty, 0.0, pltpu.roll(sr_ref[tile(src), :], shift, 0))
    spi = jnp.where(row == empty, 0.0, pltpu.roll(si_ref[tile(src), :], shift, 0))
    qr, qi = _cmul(nr, ni, spr, -spi)
    return jnp.sum(accr + qr, axis=0, keepdims=True), jnp.sum(acci + qi, axis=0, keepdims=True)


def _scan(xr_ref, xi_ref, ar, ai, reverse, pair=None):
    n = ar.shape[1]
    ar8 = jnp.broadcast_to(ar, (SCAN_BLOCKS, n))
    ai8 = jnp.broadcast_to(ai, (SCAN_BLOCKS, n))
    er, ei = _scan_local(xr_ref, xi_ref, ar8, ai8, reverse)
    atr, ati = _cpow(ar, ai, SCAN_T)
    cr8, ci8 = _scan_carries(er, ei, atr, ati, reverse)
    return _scan_fixup(xr_ref, xi_ref, cr8, ci8, ar8, ai8, reverse, pair)


def _s5_specs():
    chan = pl.BlockSpec((SEQ, CH_W), lambda c, d: (0, c))
    chan2 = pl.BlockSpec((None, SEQ, CH_W), lambda c, d: (d, 0, c))
    state = pl.BlockSpec((None, SEQ, ST_W), lambda c, d: (d, 0, c))
    bmat = pl.BlockSpec((None, None, CH_W, ST_W), lambda c, d: (d, c, 0, 0))
    cmat = pl.BlockSpec((None, None, ST_W, CH_W), lambda c, d: (d, c, 0, 0))
    avec = pl.BlockSpec((None, None, 1, ST_W), lambda c, d: (d, c, 0, 0))
    return chan, chan2, state, bmat, cmat, avec


def _scan_by_direction(xr_ref, xi_ref, ar, ai, d, adjoint, pair=None, da_out=None):
    for direction in range(2):
        @pl.when(d == direction)
        def _(direction=direction):
            res = _scan(xr_ref, xi_ref, ar, ai, adjoint != (direction == 1), pair)
            if pair is not None:
                da_out[0][...], da_out[1][...] = res


def _s5_scan_fwd(u, bre, bim, are, aim, cre, cim):
    def body(u_ref, bre_ref, bim_ref, are_ref, aim_ref, cre_ref, cim_ref, sr_ref, si_ref, y_ref):
        ub = u_ref[...].astype(BF16)
        sr_ref[...] = _dot(ub, bre_ref[...])
        si_ref[...] = _dot(ub, bim_ref[...])
        _scan_by_direction(sr_ref, si_ref, are_ref[...], aim_ref[...], pl.program_id(1), adjoint=False)
        y_ref[...] = _dot(sr_ref[...].astype(BF16), cre_ref[...]) - _dot(si_ref[...].astype(BF16), cim_ref[...])

    chan, chan2, state, bmat, cmat, avec = _s5_specs()
    return pl.pallas_call(
        body, grid=(S5_CHUNKS, 2), in_specs=[chan, bmat, bmat, avec, avec, cmat, cmat], out_specs=[state, state, chan2],
        out_shape=[SDS((2, SEQ, S5_GROUPS * S5_STATE), F32)] * 2 + [SDS((2, SEQ, S5_WIDTH), F32)],
        name="s5_scan_fwd", compiler_params=_params(("parallel", "parallel")))(u, bre, bim, are, aim, cre, cim)


def _diag_out(out_ref, full):
    for g in range(8):
        out_ref[g] = full[g * S5_GROUP:(g + 1) * S5_GROUP, g * S5_STATE:(g + 1) * S5_STATE]


def _s5_scan_bwd(dy, du_skip, u, sr, si, bre, bim, are, aim, cre, cim):
    def body(dy_ref, dus_ref, u_ref, sr_ref, si_ref, bre_ref, bim_ref, are_ref, aim_ref, cre_ref, cim_ref,
             du_ref, dbr_ref, dbi_ref, dcr_ref, dci_ref, dar_ref, dai_ref, gr_ref, gi_ref):
        d = pl.program_id(1)
        dyb = dy_ref[...].astype(BF16)
        gr_ref[...] = _dot_nt(dyb, cre_ref[...])
        gi_ref[...] = -_dot_nt(dyb, cim_ref[...])
        _diag_out(dcr_ref, _dot_tn(dyb, sr_ref[...].astype(BF16)))
        _diag_out(dci_ref, -_dot_tn(dyb, si_ref[...].astype(BF16)))
        _scan_by_direction(gr_ref, gi_ref, are_ref[...], -aim_ref[...], d, adjoint=True, pair=(sr_ref, si_ref),
                           da_out=(dar_ref, dai_ref))

        @pl.when(d == 0)
        def _():
            du_ref[...] = dus_ref[...]

        grb = gr_ref[...].astype(BF16)
        gib = gi_ref[...].astype(BF16)
        du_ref[...] += _dot_nt(grb, bre_ref[...]) + _dot_nt(gib, bim_ref[...])
        ub = u_ref[...].astype(BF16)
        _diag_out(dbr_ref, _dot_tn(ub, grb))
        _diag_out(dbi_ref, _dot_tn(ub, gib))

    chan, _, state, bmat, cmat, avec = _s5_specs()
    diag = pl.BlockSpec((None, None, 8, S5_GROUP, S5_STATE), lambda c, d: (d, c, 0, 0, 0))
    return pl.pallas_call(
        body, grid=(S5_CHUNKS, 2), in_specs=[chan, chan, chan, state, state, bmat, bmat, avec, avec, cmat, cmat],
        out_specs=[chan, diag, diag, diag, diag, avec, avec],
        out_shape=[SDS((SEQ, S5_WIDTH), F32)] + [SDS((2, S5_CHUNKS, 8, S5_GROUP, S5_STATE), F32)] * 4
                  + [SDS((2, S5_CHUNKS, 1, ST_W), F32)] * 2,
        scratch_shapes=[pltpu.VMEM((SEQ, ST_W), F32), pltpu.VMEM((SEQ, ST_W), F32)],
        name="s5_scan_bwd", compiler_params=_params(("parallel", "arbitrary")))(dy, du_skip, u, sr, si, bre, bim, are, aim, cre, cim)


_GELU_K = math.sqrt(2.0 / math.pi)
_GELU_C = 0.044715


def _gelu(x):
    t = jnp.tanh(_GELU_K * (x + _GELU_C * x * x * x))
    return 0.5 * x * (1.0 + t), t


def _s5_glu_fwd(u, y2, dskip, wglu, bglu):
    def body(u_ref, y0_ref, y1_ref, d_ref, w_ref, b_ref, o_ref, yp_ref):
        ypre = u_ref[...] * d_ref[...] + y0_ref[...] + y1_ref[...]
        yp_ref[...] = ypre
        y, _ = _gelu(ypre)
        z = _dot(y.astype(BF16), w_ref[...]) + b_ref[...]
        o_ref[...] = y * jax.nn.sigmoid(z)

    row = _row_spec(S5_WIDTH)
    vec = _fix_spec((1, S5_WIDTH))
    dir0 = pl.BlockSpec((None, ROW_TILE, S5_WIDTH), lambda i: (0, i, 0))
    dir1 = pl.BlockSpec((None, ROW_TILE, S5_WIDTH), lambda i: (1, i, 0))
    return pl.pallas_call(
        body, grid=(N_ROW_TILES,), in_specs=[row, dir0, dir1, vec, _fix_spec((S5_WIDTH, S5_WIDTH)), vec],
        out_specs=[row, row], out_shape=[SDS((SEQ, S5_WIDTH), F32)] * 2, name="s5_glu_fwd",
        compiler_params=_params(("parallel",)))(u, y2, y2, dskip, wglu, bglu)


def _s5_glu_bwd(do, ypre, u, dskip, wglu, bglu):
    def body(do_ref, yp_ref, u_ref, d_ref, w_ref, b_ref, dyp_ref, du_ref, dw_ref, db_ref, dd_ref):
        i = pl.program_id(0)
        ypre = yp_ref[...]
        y, t = _gelu(ypre)
        yb = y.astype(BF16)
        sg = jax.nn.sigmoid(_dot(yb, w_ref[...]) + b_ref[...])
        dov = do_ref[...]
        dz = dov * y * sg * (1.0 - sg)
        dzb = dz.astype(BF16)
        dy = dov * sg + _dot_nt(dzb, w_ref[...])
        dgelu = 0.5 * (1.0 + t) + 0.5 * ypre * (1.0 - t * t) * _GELU_K * (1.0 + 3.0 * _GELU_C * ypre * ypre)
        dyp = dy * dgelu
        dyp_ref[...] = dyp
        uv = u_ref[...]
        du_ref[...] = dyp * d_ref[...]

        @pl.when(i == 0)
        def _():
            dw_ref[...] = jnp.zeros_like(dw_ref)
            db_ref[...] = jnp.zeros_like(db_ref)
            dd_ref[...] = jnp.zeros_like(dd_ref)

        dw_ref[...] += _dot_tn(yb, dzb)
        db_ref[...] += jnp.sum(dz, axis=0, keepdims=True)
        dd_ref[...] += jnp.sum(dyp * uv, axis=0, keepdims=True)

    row = _row_spec(S5_WIDTH)
    vec = _fix_spec((1, S5_WIDTH))
    mat = _fix_spec((S5_WIDTH, S5_WIDTH))
    return pl.pallas_call(
        body, grid=(N_ROW_TILES,), in_specs=[row, row, row, vec, mat, vec], out_specs=[row, row, mat, vec, vec],
        out_shape=[SDS((SEQ, S5_WIDTH), F32)] * 2 + [SDS((S5_WIDTH, S5_WIDTH), F32), SDS((1, S5_WIDTH), F32), SDS((1, S5_WIDTH), F32)],
        name="s5_glu_bwd", compiler_params=_params(("arbitrary",)))(do, ypre, u, dskip, wglu, bglu)


def _heads_side_by_side(o_ref):
    return jnp.concatenate([o_ref[h] for h in range(HEADS)], axis=-1)


def _mix_out_fwd(ona, os5, g_na, g_s5, wout):
    def body(a_ref, s_ref, ga_ref, gs_ref, w_ref, o_ref):
        av, sv = _heads_side_by_side(a_ref), s_ref[...]
        ca = (av * _rstd(av) * ga_ref[...]).astype(BF16)
        cs = (sv * _rstd(sv) * gs_ref[...]).astype(BF16)
        o_ref[...] = _dot(ca, w_ref[0:NA_WIDTH, :]) + _dot(cs, w_ref[NA_WIDTH:, :])

    row = _row_spec(NA_WIDTH)
    vec = _fix_spec((1, NA_WIDTH))
    heads = pl.BlockSpec((HEADS, ROW_TILE, HEAD_DIM), lambda i: (0, i, 0))
    return pl.pallas_call(
        body, grid=(N_ROW_TILES,), in_specs=[heads, row, vec, vec, _fix_spec((D_MODEL, D_MODEL))],
        out_specs=_row_spec(D_MODEL), out_shape=SDS((SEQ, D_MODEL), F32), name="mix_out_fwd",
        compiler_params=_params(("parallel",)))(ona, os5, g_na, g_s5, wout)


def _mix_out_bwd(dmix, ona, os5, g_na, g_s5, wout):
    def body(dm_ref, a_ref, s_ref, ga_ref, gs_ref, w_ref, da_ref, ds_ref, dw_ref, dga_ref, dgs_ref):
        i = pl.program_id(0)
        dm = dm_ref[...]
        av, sv = _heads_side_by_side(a_ref), s_ref[...]
        ra, rs = _rstd(av), _rstd(sv)
        ga, gs = ga_ref[...], gs_ref[...]
        ca = (av * ra * ga).astype(BF16)
        cs = (sv * rs * gs).astype(BF16)
        dca = _dot_nt(dm, w_ref[0:NA_WIDTH, :])
        dcs = _dot_nt(dm, w_ref[NA_WIDTH:, :])
        da, dga = _rms_bwd(av, ra, ga, dca)
        ds, dgs = _rms_bwd(sv, rs, gs, dcs)
        for h in range(HEADS):
            da_ref[h] = da[:, h * HEAD_DIM:(h + 1) * HEAD_DIM]
        ds_ref[...] = ds

        @pl.when(i == 0)
        def _():
            dw_ref[...] = jnp.zeros_like(dw_ref)
            dga_ref[...] = jnp.zeros_like(dga_ref)
            dgs_ref[...] = jnp.zeros_like(dgs_ref)

        dw_ref[0:NA_WIDTH, :] += _dot_tn(ca, dm)
        dw_ref[NA_WIDTH:, :] += _dot_tn(cs, dm)
        dga_ref[...] += jnp.sum(dga, axis=0, keepdims=True)
        dgs_ref[...] += jnp.sum(dgs, axis=0, keepdims=True)

    row = _row_spec(NA_WIDTH)
    vec = _fix_spec((1, NA_WIDTH))
    mat = _fix_spec((D_MODEL, D_MODEL))
    heads = pl.BlockSpec((HEADS, ROW_TILE, HEAD_DIM), lambda i: (0, i, 0))
    return pl.pallas_call(
        body, grid=(N_ROW_TILES,), in_specs=[_row_spec(D_MODEL), heads, row, vec, vec, mat],
        out_specs=[heads, row, mat, vec, vec],
        out_shape=[SDS((HEADS, SEQ, HEAD_DIM), F32), SDS((SEQ, NA_WIDTH), F32), SDS((D_MODEL, D_MODEL), F32),
                   SDS((1, NA_WIDTH), F32), SDS((1, NA_WIDTH), F32)],
        name="mix_out_bwd", compiler_params=_params(("arbitrary",)))(dmix, ona, os5, g_na, g_s5, wout)


def _me():
    x, y, c = lax.axis_index("x"), lax.axis_index("y"), lax.axis_index("c")
    return x, y, c, 4 * x + 2 * y + c


def _peer(k):
    x, y, c, _ = _me()
    px = 1 - x if (k >> 2) & 1 else x
    py = 1 - y if (k >> 1) & 1 else y
    pc = 1 - c if k & 1 else c
    return (px, py, pc), 4 * px + 2 * py + pc


ALL_PEERS = (1, 2, 3, 4, 5, 6, 7)
CHIP_PEERS = (2, 4, 6)
SIBLING = 1


def _slot8(pos):
    return 4 * pos[0] + 2 * pos[1] + pos[2]


def _slot4(pos):
    return 2 * pos[0] + pos[1]


def _exchange(arrays, gather, name, after=()):
    n, n_after = len(arrays), len(after)

    def body(*refs):
        ins, outs = refs[:n], refs[n + n_after:2 * n + n_after]
        token = refs[2 * n + n_after]
        send_sems, recv_sems, local_sems = refs[2 * n + n_after + 1:]
        token[...] = jnp.zeros_like(token)
        _, _, _, me = _me()
        started = []
        for a in range(n):
            src_mine = ins[a] if gather else ins[a].at[me]
            local = pltpu.make_async_copy(src_mine, outs[a].at[me], local_sems.at[a])
            local.start()
            started.append(local)
        sends = []
        for k in range(1, N_DEV):
            peer, peer_idx = _peer(k)
            for a in range(n):
                src = ins[a] if gather else ins[a].at[peer_idx]
                cp = pltpu.make_async_remote_copy(src_ref=src, dst_ref=outs[a].at[me], send_sem=send_sems.at[a, k - 1],
                                                  recv_sem=recv_sems.at[a, k - 1], device_id=peer, device_id_type=MESH)
                cp.start()
                sends.append(cp)
        for k in range(1, N_DEV):
            peer, peer_idx = _peer(k)
            for a in range(n):
                src = ins[a] if gather else ins[a].at[peer_idx]
                pltpu.make_async_remote_copy(src_ref=src, dst_ref=outs[a].at[peer_idx], send_sem=send_sems.at[a, k - 1],
                                             recv_sem=recv_sems.at[a, k - 1], device_id=peer, device_id_type=MESH).wait_recv()
        for cp in sends:
            cp.wait_send()
        for local in started:
            local.wait()

    hbm = pl.BlockSpec(memory_space=pltpu.HBM)
    out_shape = [SDS((N_DEV,) + tuple(a.shape), a.dtype) if gather else SDS(a.shape, a.dtype) for a in arrays]
    out = pl.pallas_call(
        body, in_specs=[hbm] * n + [pl.BlockSpec(memory_space=pl.ANY)] * n_after,
        out_specs=[hbm] * n + [pl.BlockSpec(memory_space=pltpu.VMEM)], out_shape=out_shape + [SDS((8, 128), F32)],
        scratch_shapes=[pltpu.SemaphoreType.DMA((n, N_DEV - 1)), pltpu.SemaphoreType.DMA((n, N_DEV - 1)),
                        pltpu.SemaphoreType.DMA((n,))],
        name=name)(*arrays, *after)
    return list(out[:n]), out[n]


_HBM = pl.BlockSpec(memory_space=pltpu.HBM)
_SEM = pl.BlockSpec(memory_space=pltpu.SEMAPHORE)
_EFFECT = pltpu.SideEffectType.DATAFLOW_SIDE_EFFECTING


def _land_shape(a, gather):
    return (N_DEV,) + tuple(a.shape) if gather else tuple(a.shape)


def _place_own(arrays, gather, name, slot=_slot8):
    n = len(arrays)
    me = slot(_me()[:3])

    def body(me_ref, *refs):
        for a in range(n):
            refs[n + a][...] = refs[a][...]

    def own_slot(a):
        zeros = (0,) * (a.ndim - (0 if gather else 1))
        return lambda i, me_ref: (me_ref[0],) + zeros

    def whole(a):
        return lambda i, me_ref: (0,) * a.ndim

    in_specs = [pl.BlockSpec(a.shape, whole(a)) if gather else pl.BlockSpec((None,) + a.shape[1:], own_slot(a)) for a in arrays]
    out_specs = [pl.BlockSpec((None,) + (a.shape if gather else a.shape[1:]), own_slot(a)) for a in arrays]
    return pl.pallas_call(
        body, grid_spec=pltpu.PrefetchScalarGridSpec(num_scalar_prefetch=1, grid=(1,), in_specs=in_specs, out_specs=out_specs),
        out_shape=[SDS(_land_shape(a, gather), a.dtype) for a in arrays], name=name,
        compiler_params=_params(("arbitrary",)))(me.reshape(1).astype(jnp.int32), *arrays)


def _exchange_start(arrays, lands, gather, name, peers=ALL_PEERS, slot=_slot8):
    n = len(arrays)

    def body(*refs):
        ins, lnd = refs[:n], refs[n:2 * n]
        send_sems, recv_sems = refs[2 * n], refs[2 * n + 1]
        token = refs[-1]
        me = slot(_me()[:3])
        for i, k in enumerate(peers):
            peer, _ = _peer(k)
            for a in range(n):
                src = ins[a] if gather else ins[a].at[slot(peer)]
                s = a * len(peers) + i
                pltpu.make_async_remote_copy(src_ref=src, dst_ref=lnd[a].at[me], send_sem=send_sems.at[s],
                                             recv_sem=recv_sems.at[s], device_id=peer, device_id_type=MESH).start()
        token[...] = jnp.zeros_like(token)

    sems = pltpu.SemaphoreType.DMA((n * len(peers),))
    out = pl.pallas_call(
        body, name=name, in_specs=[_HBM] * (2 * n),
        out_shape=(sems, sems) + tuple(pltpu.HBM(a.shape, a.dtype) for a in list(arrays) + list(lands)) + (SDS((8, 128), F32),),
        out_specs=(_SEM, _SEM) + (_HBM,) * (2 * n) + (pl.BlockSpec(memory_space=pltpu.VMEM),),
        input_output_aliases={i: 2 + i for i in range(2 * n)},
        compiler_params=pltpu.CompilerParams(has_side_effects=_EFFECT),
    )(*[pltpu.with_memory_space_constraint(a, pltpu.HBM) for a in list(arrays) + list(lands)])
    return out[0], out[1], list(out[2:2 + n]), list(out[2 + n:2 + 2 * n]), out[-1]


def _exchange_wait(send_sems, recv_sems, arrays, lands, after, gather, name, peers=ALL_PEERS, slot=_slot8):
    n = len(arrays)

    def body(*refs):
        ins, lnd = refs[:n], refs[n:2 * n]
        send_sems, recv_sems = refs[2 * n], refs[2 * n + 1]
        for i, k in enumerate(peers):
            peer, _ = _peer(k)
            for a in range(n):
                src = ins[a] if gather else ins[a].at[slot(peer)]
                s = a * len(peers) + i
                cp = pltpu.make_async_remote_copy(src_ref=src, dst_ref=lnd[a].at[slot(peer)], send_sem=send_sems.at[s],
                                                  recv_sem=recv_sems.at[s], device_id=peer, device_id_type=MESH)
                cp.wait_send()
                cp.wait_recv()

        refs[-1][...] = jnp.zeros_like(refs[-1])

    after = list(after) if isinstance(after, (list, tuple)) else [after]
    out = pl.pallas_call(
        body, name=name, in_specs=[_HBM] * (2 * n) + [_SEM, _SEM] + [pl.BlockSpec(memory_space=pl.ANY)] * len(after),
        out_shape=tuple(pltpu.HBM(a.shape, a.dtype) for a in list(arrays) + list(lands)) + (SDS((8, 128), F32),),
        out_specs=(_HBM,) * (2 * n) + (pl.BlockSpec(memory_space=pltpu.VMEM),), input_output_aliases={i: i for i in range(2 * n)},
        compiler_params=pltpu.CompilerParams(has_side_effects=_EFFECT),
    )(*arrays, *lands, send_sems, recv_sems, *after)
    return list(out[n:2 * n]), out[-1]


def _forward_sibling(lands, name):
    n = len(lands)

    def body(*refs):
        outs = refs[n:2 * n]
        send_sems, recv_sems = refs[2 * n:]
        x, y, c, _ = _me()
        sends = []
        for i, k in enumerate(CHIP_PEERS):
            peer, _ = _peer(k)
            for a in range(n):
                rows = outs[a].at[_slot8(peer)]
                cp = pltpu.make_async_remote_copy(src_ref=rows, dst_ref=rows, send_sem=send_sems.at[a, i], recv_sem=recv_sems.at[a, i],
                                                  device_id=(x, y, 1 - c), device_id_type=MESH)
                cp.start()
                sends.append(cp)
        for i, k in enumerate(CHIP_PEERS):
            (px, py, pc), _ = _peer(k)
            for a in range(n):
                rows = outs[a].at[_slot8((px, py, 1 - pc))]
                pltpu.make_async_remote_copy(src_ref=rows, dst_ref=rows, send_sem=send_sems.at[a, i], recv_sem=recv_sems.at[a, i],
                                             device_id=(x, y, 1 - c), device_id_type=MESH).wait_recv()
        for cp in sends:
            cp.wait_send()

    return pl.pallas_call(
        body, in_specs=[_HBM] * n, out_specs=[_HBM] * n, out_shape=[SDS(a.shape, a.dtype) for a in lands],
        input_output_aliases={i: i for i in range(n)},
        scratch_shapes=[pltpu.SemaphoreType.DMA((n, len(CHIP_PEERS))), pltpu.SemaphoreType.DMA((n, len(CHIP_PEERS)))],
        name=name)(*lands)


def _swap_sibling(arrays, name, after=()):
    n, n_after = len(arrays), len(after)
    chips = N_DEV // 2

    def body(*refs):
        ins, outs = refs[:n], refs[n + n_after:2 * n + n_after]
        send_sems, recv_sems = refs[2 * n + n_after:]
        x, y, c, _ = _me()
        sends = []
        for q in range(chips):
            for a in range(n):
                cp = pltpu.make_async_remote_copy(src_ref=ins[a].at[q, 1 - c], dst_ref=outs[a].at[q], send_sem=send_sems.at[a, q],
                                                  recv_sem=recv_sems.at[a, q], device_id=(x, y, 1 - c), device_id_type=MESH)
                cp.start()
                sends.append(cp)
        for cp in sends:
            cp.wait_recv()
        for cp in sends:
            cp.wait_send()

    return pl.pallas_call(
        body, in_specs=[_HBM] * n + [pl.BlockSpec(memory_space=pl.ANY)] * n_after, out_specs=[_HBM] * n,
        out_shape=[SDS((chips,) + a.shape[2:], a.dtype) for a in arrays],
        scratch_shapes=[pltpu.SemaphoreType.DMA((n, chips)), pltpu.SemaphoreType.DMA((n, chips))], name=name)(*arrays, *after)


def _sum_pairs(mine, theirs, name):
    chips, _, rows, cols = mine.shape
    c = lax.axis_index("c")

    def body(c_ref, a_ref, b_ref, o_ref):
        o_ref[...] = (a_ref[...].astype(F32) + b_ref[...].astype(F32)).astype(o_ref.dtype)

    return pl.pallas_call(
        body, grid_spec=pltpu.PrefetchScalarGridSpec(
            num_scalar_prefetch=1, grid=(chips,),
            in_specs=[pl.BlockSpec((None, None, rows, cols), lambda q, c_ref: (q, c_ref[0], 0, 0)),
                      pl.BlockSpec((None, rows, cols), lambda q, c_ref: (q, 0, 0))],
            out_specs=pl.BlockSpec((None, rows, cols), lambda q, c_ref: (q, 0, 0))),
        out_shape=SDS((chips, rows, cols), mine.dtype), name=name,
        compiler_params=_params(("parallel",)))(c.reshape(1).astype(jnp.int32), mine, theirs)


def _adamw_math(w, g, m, v):
    m = ADAM_B1 * m + (1.0 - ADAM_B1) * g
    v = ADAM_B2 * v + (1.0 - ADAM_B2) * (g * g)
    m_hat = m / (1.0 - ADAM_B1 ** ADAM_STEP)
    v_hat = v / (1.0 - ADAM_B2 ** ADAM_STEP)
    delta = -ADAM_LR * (m_hat / (jnp.sqrt(v_hat) + ADAM_EPS) + ADAM_WD * w)
    return delta, m, v


def _adamw(w, m, v, pieces, name):
    rows, cols = w.shape[-2:]
    lead = w.ndim - 2
    tile = rows
    for cand in (256, 176, 128, 64, 16):
        if rows > cand and rows % cand == 0:
            tile = cand
            break

    def body(w_ref, m_ref, v_ref, p_ref, g_ref, d_ref, mo_ref, vo_ref):
        g = _sum_pieces(p_ref)
        g_ref[...] = g
        d_ref[...], mo_ref[...], vo_ref[...] = _adamw_math(w_ref[...], g, m_ref[...], v_ref[...])

    blk = pl.BlockSpec((None,) * lead + (tile, cols), lambda i: (0,) * lead + (i, 0))
    return pl.pallas_call(
        body, grid=(rows // tile,), in_specs=[blk, blk, blk, pl.BlockSpec((pieces.shape[0], tile, cols), lambda i: (0, i, 0))],
        out_specs=[blk] * 4, out_shape=[SDS(w.shape, F32)] * 4, name=name,
        compiler_params=_params(("parallel",)))(w, m, v, pieces)


def _sum_pieces(p_ref):
    g = p_ref[0].astype(F32)
    for p in range(1, p_ref.shape[0]):
        g = g + p_ref[p].astype(F32)
    return g


def _adamw_s5_mat(w, m, v, g, name):
    _, ndir, groups, b, c = w.shape
    per_dir = groups // 8

    def body(w_ref, m_ref, v_ref, g_ref, d_ref, mo_ref, vo_ref):
        d_ref[...], mo_ref[...], vo_ref[...] = _adamw_math(w_ref[...], g_ref[...], m_ref[...], v_ref[...])

    blk = pl.BlockSpec((None, None, 8, b, c), lambda i: (0, i // per_dir, i % per_dir, 0, 0))
    return pl.pallas_call(
        body, grid=(ndir * per_dir,), in_specs=[blk] * 4, out_specs=[blk] * 3, out_shape=[SDS(w.shape, F32)] * 3, name=name,
        compiler_params=_params(("parallel",)))(w, m, v, g)


VEC_ROWS = ['ffn1_pre_g', 'ffn1_post_g', 'mix_pre_g', 'mix_post_g', 'ffn2_pre_g', 'ffn2_post_g', 'final_g',
            ('na_out_g', 's5_out_g'), ('s5_d', 's5_b_glu')]
VEC_NAMES = [n for row in VEC_ROWS for n in ((row,) if isinstance(row, str) else row)]
VEC_PACK_ROWS = 16
LOSS_ROW = len(VEC_ROWS)


def _pack_vectors(grads, loss8):
    def body(*refs):
        o_ref = refs[-1]
        o_ref[...] = jnp.zeros_like(o_ref)
        o_ref[LOSS_ROW:LOSS_ROW + 1, 0:128] = refs[-2][0:1, :]
        k = 0
        for i, row in enumerate(VEC_ROWS):
            if isinstance(row, str):
                o_ref[i:i + 1, :] = refs[k][...]
                k += 1
            else:
                o_ref[i:i + 1, 0:NA_WIDTH] = refs[k][...]
                o_ref[i:i + 1, NA_WIDTH:] = refs[k + 1][...]
                k += 2

    return pl.pallas_call(body, out_shape=SDS((VEC_PACK_ROWS, D_MODEL), F32), name="pack_vectors",
                          compiler_params=_params())(*[grads[n] for n in VEC_NAMES], loss8)


def _sum8(pieces, name):
    def body(p_ref, o_ref):
        o_ref[...] = _sum_pieces(p_ref)

    return pl.pallas_call(body, out_shape=SDS(pieces.shape[1:], F32), name=name, compiler_params=_params())(pieces)


def _adamw_small(packed8, vec_wmv, others):
    n_vec, n_oth = len(VEC_NAMES), len(others)

    def body(*refs):
        p_ref = refs[0]
        ins = refs[1:1 + 3 * n_vec + 4 * n_oth]
        outs = refs[1 + 3 * n_vec + 4 * n_oth:]
        gsum = _sum_pieces(p_ref)
        outs[-1][...] = gsum[LOSS_ROW:LOSS_ROW + 1, 0:128]
        k = 0
        for i, row in enumerate(VEC_ROWS):
            parts = [(row, gsum[i:i + 1, :])] if isinstance(row, str) else \
                [(row[0], gsum[i:i + 1, 0:NA_WIDTH]), (row[1], gsum[i:i + 1, NA_WIDTH:])]
            for _, g in parts:
                w_ref, m_ref, v_ref = ins[3 * k:3 * k + 3]
                outs[4 * k][...] = g
                outs[4 * k + 1][...], outs[4 * k + 2][...], outs[4 * k + 3][...] = _adamw_math(w_ref[...], g, m_ref[...], v_ref[...])
                k += 1
        for j in range(n_oth):
            w_ref, m_ref, v_ref, g_ref = ins[3 * n_vec + 4 * j:3 * n_vec + 4 * j + 4]
            g = _sum_pieces(g_ref)
            g = g[tuple(slice(0, s) for s in w_ref.shape[1:])].reshape(w_ref.shape)
            o = outs[4 * (n_vec + j):4 * (n_vec + j) + 4]
            o[0][...] = g
            o[1][...], o[2][...], o[3][...] = _adamw_math(w_ref[...], g, m_ref[...], v_ref[...])

    args, out_shape = [packed8], []
    for w, m, v in vec_wmv:
        args += [w, m, v]
        out_shape += [SDS(w.shape, F32)] * 4
    for w, m, v, g in others:
        args += [w, m, v, g]
        out_shape += [SDS(w.shape, F32)] * 4
    out_shape += [SDS((1, 128), F32)]
    return pl.pallas_call(body, out_shape=out_shape, name="adamw_small", compiler_params=_params())(*args)


def _perm_rows(x):
    return x.reshape(SCAN_BLOCKS, SCAN_T, x.shape[-1]).transpose(1, 0, 2).reshape(SEQ, x.shape[-1])


def _unperm_rows(x):
    return x.reshape(SCAN_T, SCAN_BLOCKS, x.shape[-1]).transpose(1, 0, 2).reshape(SEQ, x.shape[-1])


def _block_diag(x):
    eye = np.eye(8, dtype=bool)[None, None, :, None, :, None]
    full = jnp.where(eye, x[:, :, :, :, None, :], 0.0)
    return full.reshape(2, S5_CHUNKS, 8 * x.shape[3], 8 * x.shape[4])


STORED_SWAPPED = {"ffn1_w_gate": (1, 2), "ffn1_w_up": (1, 2), "ffn2_w_gate": (1, 2), "ffn2_w_up": (1, 2),
                  "s5_b_re": (3, 4), "s5_b_im": (3, 4)}


def _stored(name, x):
    return jnp.swapaxes(x, *STORED_SWAPPED[name]) if name in STORED_SWAPPED else x


def _dep(x, token):
    return x if token is None else x + token


def _local_step(x, target, get_w, small, emit):
    bias = _rpb_expand(small["na_rpb"][0])
    lr = small["s5_lam_re"].reshape(64, S5_STATE)
    li = small["s5_lam_im"].reshape(64, S5_STATE)
    logdt = small["s5_log_dt"].reshape(64, 1)
    b_t = [_stored(n, small[n]).reshape(64, S5_GROUP, S5_STATE) for n in ("s5_b_re", "s5_b_im")]
    lbr, lbi, bbr, bbi = _s5_prep(lr, li, logdt, b_t[0], b_t[1])
    are = lbr.reshape(2, S5_CHUNKS, 1, ST_W)
    aim = lbi.reshape(2, S5_CHUNKS, 1, ST_W)
    bre = _block_diag(bbr.reshape(2, S5_CHUNKS, 8, S5_GROUP, S5_STATE)).astype(BF16)
    bim = _block_diag(bbi.reshape(2, S5_CHUNKS, 8, S5_GROUP, S5_STATE)).astype(BF16)
    c_t = [small[n].reshape(2, S5_CHUNKS, 8, S5_GROUP, S5_STATE).transpose(0, 1, 2, 4, 3) for n in ("s5_c_re", "s5_c_im")]
    cre = _block_diag(c_t[0]).astype(BF16)
    cim = _block_diag(c_t[1]).astype(BF16)
    tgt = jnp.concatenate([jnp.zeros((N_META, D_MODEL), F32), target], axis=0)

    h0, a1 = _embed_prenorm(get_w("meta", None)["meta_tokens"], x, small["ffn1_pre_g"])
    wts = dict(get_w("ffn1", [bias, are, aim, bre, bim, cre, cim, tgt, a1]))
    gate1, up1, f1 = _ffn_fwd(a1, wts["ffn1_w_gate"], wts["ffn1_w_up"], wts["ffn1_w_down"], "ffn1_fwd",
                              after=wts.get("tokens", ()))
    h1, a2 = _post_pre(f1, h0, small["ffn1_post_g"], small["mix_pre_g"], 0.5, "post_pre1")
    wts.update(get_w("w_in", a2))
    qkv = _proj_heads(a2, wts["w_in"])
    u = _proj_u(a2, wts["w_in"])
    ona = _na_fwd(qkv, bias)
    u_p = _perm_rows(u)
    sr, si, y2 = _s5_scan_fwd(u_p, bre, bim, are, aim, cre, cim)
    wts.update(get_w("mix", y2))
    os5_p, ypre_p = _s5_glu_fwd(u_p, y2, small["s5_d"], wts["s5_w_glu"], small["s5_b_glu"])
    os5 = _unperm_rows(os5_p)

    mix = _mix_out_fwd(ona, os5, small["na_out_g"], small["s5_out_g"], wts["w_out"])
    h2, a3 = _post_pre(mix, h1, small["mix_post_g"], small["ffn2_pre_g"], 1.0, "post_pre2")
    wts.update(get_w("ffn2", a3))
    gate2, up2, f2 = _ffn_fwd(a3, wts["ffn2_w_gate"], wts["ffn2_w_up"], wts["ffn2_w_down"], "ffn2_fwd")
    loss8, dh3, df2, g_final, g_ffn2_post = _final_loss(f2, h2, small["ffn2_post_g"], small["final_g"], tgt)

    da3, dwg2, dwu2, dwd2 = _ffn_bwd(df2, a3, gate2, up2, wts["ffn2_w_gate"], wts["ffn2_w_up"], wts["ffn2_w_down"], "ffn2_bwd")
    tok = emit("ffn2", {"ffn2_w_gate": dwg2, "ffn2_w_up": dwu2, "ffn2_w_down": dwd2})
    dh2, dmix, g_ffn2_pre, g_mix_post = _bwd_pre_post(da3, h2, _dep(small["ffn2_pre_g"], tok), dh3, mix, small["mix_post_g"], 1.0,
                                                      "bwd_pre_post2")
    dona, dos5, dwout, g_na_out, g_s5_out = _mix_out_bwd(dmix, ona, os5, small["na_out_g"], small["s5_out_g"], wts["w_out"])

    dypre_p, du_skip_p, dwglu, g_b_glu, g_s5_d = _s5_glu_bwd(_perm_rows(dos5), ypre_p, u_p, small["s5_d"], wts["s5_w_glu"],
                                                             small["s5_b_glu"])
    tok = emit("mix", {"s5_w_glu": dwglu.reshape(N_DEV, S5_WIDTH // N_DEV, S5_WIDTH).astype(BF16),
                       "w_out": dwout.reshape(N_DEV, D_MODEL // N_DEV, D_MODEL).astype(BF16)})
    du_p, dbr, dbi, dcr, dci, dar, dai = _s5_scan_bwd(dypre_p, du_skip_p, u_p, sr, si, bre, bim, _dep(are, tok), aim, cre, cim)
    du = _unperm_rows(du_p)
    per_group = (2 * S5_GROUPS, S5_GROUP, S5_STATE)
    g_lr, g_li, g_dt, g_br, g_bi = _s5_prep_bwd(lr, li, logdt, b_t[0], b_t[1], dar.reshape(64, S5_STATE),
                                                dai.reshape(64, S5_STATE), dbr.reshape(per_group), dbi.reshape(per_group))
    g_c = [dcr.reshape(per_group), dci.reshape(per_group)]

    dq, dk, dv, dbias = _na_bwd(qkv, bias, dona)
    g_rpb = _rpb_reduce(dbias)
    dense = jnp.stack([g.reshape(2 * S5_GROUPS, S5_STATE * S5_GROUP) for g in (g_br, g_bi, *g_c)])
    tok = emit("small", {"dense": dense, "na_rpb": g_rpb,
                         "s5_lam_re": g_lr.reshape(2, S5_GROUPS, S5_STATE), "s5_lam_im": g_li.reshape(2, S5_GROUPS, S5_STATE),
                         "s5_log_dt": g_dt.reshape(2, S5_GROUPS)})
    da2, dwin = _proj_bwd(dq, dk, dv, du, a2, wts["w_in"])
    tok2 = emit("w_in", {"w_in": dwin})
    tok = tok if tok2 is None else tok + tok2
    dh1, df1, g_mix_pre, g_ffn1_post = _bwd_pre_post(da2, h1, _dep(small["mix_pre_g"], tok), dh2, f1, small["ffn1_post_g"], 0.5,
                                                     "bwd_pre_post1")
    da1, dwg1, dwu1, dwd1 = _ffn_bwd(df1, a1, gate1, up1, wts["ffn1_w_gate"], wts["ffn1_w_up"], wts["ffn1_w_down"], "ffn1_bwd")
    grad_x, grad_meta, g_ffn1_pre = _bwd_embed(da1, h0, small["ffn1_pre_g"], dh1)
    vec_g = {
        "ffn1_pre_g": g_ffn1_pre, "ffn1_post_g": g_ffn1_post, "mix_pre_g": g_mix_pre, "s5_d": g_s5_d, "s5_b_glu": g_b_glu,
        "na_out_g": g_na_out, "s5_out_g": g_s5_out, "mix_post_g": g_mix_post,
        "ffn2_pre_g": g_ffn2_pre, "ffn2_post_g": g_ffn2_post, "final_g": g_final,
    }
    emit("vec", {"packed": _pack_vectors(vec_g, loss8), "meta_tokens": grad_meta})
    emit("ffn1", {"ffn1_w_gate": dwg1, "ffn1_w_up": dwu1, "ffn1_w_down": dwd1})
    return grad_x


WEIGHT_NAMES = ['meta_tokens', 'ffn1_pre_g', 'ffn1_post_g', 'ffn1_w_gate', 'ffn1_w_up', 'ffn1_w_down', 'mix_pre_g', 'w_in',
                'na_rpb', 's5_lam_re', 's5_lam_im', 's5_log_dt', 's5_b_re', 's5_b_im', 's5_c_re', 's5_c_im', 's5_d',
                's5_w_glu', 's5_b_glu', 'na_out_g', 's5_out_g', 'w_out', 'mix_post_g', 'ffn2_pre_g', 'ffn2_post_g',
                'ffn2_w_gate', 'ffn2_w_up', 'ffn2_w_down', 'final_g']
BIG_NAMES = ['ffn1_w_gate', 'ffn1_w_up', 'ffn1_w_down', 'w_in', 's5_w_glu', 'w_out', 'ffn2_w_gate', 'ffn2_w_up', 'ffn2_w_down']
SMALL_NAMES = [n for n in WEIGHT_NAMES if n not in BIG_NAMES and n != 'meta_tokens']
WHOLE_NAMES = ['na_rpb', 's5_lam_re', 's5_lam_im', 's5_log_dt']
LEAD_NAMES = ['s5_b_re', 's5_b_im', 's5_c_re', 's5_c_im']


def kernel(x, meta_tokens, ffn1_pre_g, ffn1_post_g, ffn1_w_gate, ffn1_w_up, ffn1_w_down, mix_pre_g, w_in, na_rpb, s5_lam_re, s5_lam_im, s5_log_dt, s5_b_re, s5_b_im, s5_c_re, s5_c_im, s5_d, s5_w_glu, s5_b_glu, na_out_g, s5_out_g, w_out, mix_post_g, ffn2_pre_g, ffn2_post_g, ffn2_w_gate, ffn2_w_up, ffn2_w_down, final_g, loss_target, m_meta_tokens, m_ffn1_pre_g, m_ffn1_post_g, m_ffn1_w_gate, m_ffn1_w_up, m_ffn1_w_down, m_mix_pre_g, m_w_in, m_na_rpb, m_s5_lam_re, m_s5_lam_im, m_s5_log_dt, m_s5_b_re, m_s5_b_im, m_s5_c_re, m_s5_c_im, m_s5_d, m_s5_w_glu, m_s5_b_glu, m_na_out_g, m_s5_out_g, m_w_out, m_mix_post_g, m_ffn2_pre_g, m_ffn2_post_g, m_ffn2_w_gate, m_ffn2_w_up, m_ffn2_w_down, m_final_g, v_meta_tokens, v_ffn1_pre_g, v_ffn1_post_g, v_ffn1_w_gate, v_ffn1_w_up, v_ffn1_w_down, v_mix_pre_g, v_w_in, v_na_rpb, v_s5_lam_re, v_s5_lam_im, v_s5_log_dt, v_s5_b_re, v_s5_b_im, v_s5_c_re, v_s5_c_im, v_s5_d, v_s5_w_glu, v_s5_b_glu, v_na_out_g, v_s5_out_g, v_w_out, v_mix_post_g, v_ffn2_pre_g, v_ffn2_post_g, v_ffn2_w_gate, v_ffn2_w_up, v_ffn2_w_down, v_final_g):
    args = dict(locals())
    w = {n: args[n] for n in WEIGHT_NAMES}
    m = {n: args["m_" + n] for n in WEIGHT_NAMES}
    v = {n: args["v_" + n] for n in WEIGHT_NAMES}

    small = {n: w[n] for n in SMALL_NAMES}

    pending = {}

    def start(group, names, arrays, gather, peers=ALL_PEERS, slot=_slot8):
        lands = _place_own(arrays, gather, "own_" + group, slot)
        send_sems, recv_sems, arrays, lands, token = _exchange_start(arrays, lands, gather, "start_" + group, peers, slot)
        pending[group] = (names, send_sems, recv_sems, arrays, lands, gather, peers, slot)
        return token

    def finish(group, after):
        names, send_sems, recv_sems, arrays, lands, gather, peers, slot = pending.pop(group)
        lands, token = _exchange_wait(send_sems, recv_sems, arrays, lands, after, gather, "wait_" + group, peers, slot)
        return dict(zip(names, lands)), token

    first = ["ffn1_w_gate", "ffn1_w_up", "ffn1_w_down"]
    def shard(n, token=None):
        return _dep(_stored(n, w[n])[0], None if token is None else token[0, 0]).astype(BF16)

    ffn_names = ("ffn1_w_gate", "ffn1_w_up", "ffn1_w_down", "ffn2_w_gate", "ffn2_w_up", "ffn2_w_down")
    later_groups = (("w_in", ["w_in"]), ("mix", ["s5_w_glu", "w_out"]), ("ffn2", ["ffn2_w_gate", "ffn2_w_up", "ffn2_w_down"]))
    (meta_full,), token0 = _exchange([w["meta_tokens"]], True, "gather_meta")
    token1 = start("ffn1", first, [shard(n, token0) for n in first], True, (SIBLING,) + CHIP_PEERS)
    meta_full = _dep(meta_full.transpose(1, 0, 2).reshape(N_META, D_MODEL), token1[0, 0])
    later_shards = {n: shard(n, token1) for _, names in later_groups for n in names}
    for n in ("na_rpb", "s5_lam_re"):
        small[n] = _dep(small[n], token1[0, 0])

    def get_w(group, after):
        if group == "meta":
            return {"meta_tokens": meta_full}
        if group == "ffn1":
            after = list(after) + list(later_shards.values())
        got, token = finish(group, after)
        if group == "ffn1":
            got = dict(zip(got, _forward_sibling(list(got.values()), "forward_ffn1")))
            got["tokens"] = [start(g, names + ["order"], [later_shards[n] for n in names] + [token], True) for g, names in later_groups]
        if group == "mix":
            got = {"s5_w_glu": got["s5_w_glu"].reshape(S5_WIDTH, S5_WIDTH), "w_out": got["w_out"].reshape(D_MODEL, D_MODEL)}
        return {n: (a.reshape(D_FF, D_MODEL) if n in ffn_names else a) for n, a in got.items()}

    tokens = {}

    def emit(group, grads):
        grads = {n: (g.reshape(N_DEV, FF_SHARD, D_MODEL) if n in ffn_names else g) for n, g in grads.items()}
        if group == "ffn1":
            mine = [g.reshape((N_DEV // 2, 2) + g.shape[1:]) for g in grads.values()]
            theirs = _swap_sibling(mine, "swap_g_ffn1", after=[tokens["vec"]])
            sums = [_sum_pairs(a, b, "pair_sum_" + n) for n, a, b in zip(grads, mine, theirs)]
            tokens[group] = start("g_ffn1", list(grads), sums, False, CHIP_PEERS, _slot4)
        else:
            tokens[group] = start("g_" + group, list(grads), list(grads.values()), group in ("small", "vec"))
        return tokens[group][0, 0]

    grad_x = _local_step(x[0], loss_target[0], get_w, small, emit)
    res = {}

    def update_shard(n, pieces):
        outs = _adamw(_stored(n, w[n]), _stored(n, m[n]), _stored(n, v[n]), pieces, "adamw_" + n)
        res[n] = [_stored(n, o) for o in outs]

    late = [grad_x, tokens["ffn1"]]
    for group in ("g_ffn2", "g_mix", "g_w_in"):
        for n, pieces in finish(group, late)[0].items():
            update_shard(n, pieces)
    g8 = finish("g_small", late)[0]
    dense = _sum8(g8["dense"], "sum_dense")
    for i, n in enumerate(LEAD_NAMES):
        g = dense[i].reshape(_stored(n, w[n]).shape)
        upd = _adamw_s5_mat(_stored(n, w[n]), _stored(n, m[n]), _stored(n, v[n]), g, "adamw_" + n)
        res[n] = [_stored(n, o) for o in [g] + list(upd)]

    done = [res[n][1] for n in ("ffn2_w_gate", "ffn2_w_up", "ffn2_w_down", "w_in", "w_out", "s5_w_glu") + tuple(LEAD_NAMES)]
    got = finish("g_vec", done)[0]
    packed8, gmeta8 = got["packed"], got["meta_tokens"]
    for n, pieces in finish("g_ffn1", packed8)[0].items():
        update_shard(n, pieces)
    _, _, _, me = _me()
    update_shard("meta_tokens", lax.dynamic_slice_in_dim(gmeta8, me * (D_MODEL // N_DEV), D_MODEL // N_DEV, axis=2))

    outs = _adamw_small(packed8, [(w[n], m[n], v[n]) for n in VEC_NAMES], [(w[n], m[n], v[n], g8[n]) for n in WHOLE_NAMES])
    for i, n in enumerate(VEC_NAMES + WHOLE_NAMES):
        res[n] = list(outs[4 * i:4 * i + 4])

    out = [outs[-1][0, 0], grad_x[None]]
    for kind in range(4):
        out += [res[n][kind] for n in WEIGHT_NAMES]
    return tuple(out)
```

```python
import math

import numpy as np
import jax
import jax.numpy as jnp
from jax import lax
from jax.experimental import pallas as pl
from jax.experimental.pallas import tpu as pltpu

F32 = jnp.float32
BF16 = jnp.bfloat16
SDS = jax.ShapeDtypeStruct

D_MODEL = 1024
N_TOK = 2048
N_META = 16
SEQ = N_TOK + N_META
ROW_TILE = 688
N_ROW_TILES = SEQ // ROW_TILE
N_DEV = 8
D_FF = 2816
FF_SHARD = D_FF // N_DEV
FF_TILE = 256
IN_SHARD = 256
NA_WIDTH = 512
S5_WIDTH = 512
HEADS = 8
HEAD_DIM = 64
GRID_W = 64
GRID_ROWS = N_TOK // GRID_W
KH = 8
KW = 16
NA_RB = 4
NA_KR = KH + NA_RB - 1
NA_BLOCKS = GRID_ROWS // NA_RB
NA_QB = NA_RB * GRID_W
NA_KB = NA_KR * GRID_W
NA_TYPES = 3
S5_GROUPS = 32
S5_GROUP = 16
S5_STATE = 64
S5_CHUNKS = 4
CH_W = S5_WIDTH // S5_CHUNKS
ST_W = S5_GROUPS * S5_STATE // S5_CHUNKS
SCAN_BLOCKS = 8
SCAN_T = SEQ // SCAN_BLOCKS
RMS_EPS = 1e-6
NEG_INF = -1e30
ATT_SCALE = HEAD_DIM ** -0.5
ADAM_LR, ADAM_B1, ADAM_B2, ADAM_EPS, ADAM_WD, ADAM_STEP = 0.001, 0.9, 0.999, 1e-08, 0.01, 10
VMEM_LIMIT = 56 * 1024 * 1024
MESH = pl.DeviceIdType.MESH


def _params(sem=None):
    return pltpu.CompilerParams(dimension_semantics=sem, vmem_limit_bytes=VMEM_LIMIT)


def _dot(a, b):
    return jnp.dot(a, b, preferred_element_type=F32)


def _dot_nt(a, b):
    return lax.dot_general(a, b, (((1,), (1,)), ((), ())), preferred_element_type=F32)


def _dot_tn(a, b):
    return lax.dot_general(a, b, (((0,), (0,)), ((), ())), preferred_element_type=F32)


def _rstd(x):
    return lax.rsqrt(jnp.mean(x * x, axis=-1, keepdims=True) + RMS_EPS)


def _rms_bwd(x, r, g, dy):
    dyg = dy * g
    xr = x * r
    dx = r * (dyg - xr * jnp.mean(dyg * xr, axis=-1, keepdims=True))
    return dx, dy * xr


def _rows(i, size=ROW_TILE):
    return pl.ds(pl.multiple_of(i * size, 16), size)


def _row_spec(width):
    return pl.BlockSpec((ROW_TILE, width), lambda i: (i, 0))


def _fix_spec(shape):
    return pl.BlockSpec(shape, lambda i: (0,) * len(shape))


def _split3(x):
    hi = x.astype(BF16)
    r1 = x - hi.astype(F32)
    mid = r1.astype(BF16)
    lo = (r1 - mid.astype(F32)).astype(BF16)
    return hi, mid, lo


def _embed_prenorm(meta, x, g):
    def body(m_ref, x_ref, g_ref, h_ref, a_ref):
        h_ref[0:N_META, :] = m_ref[...]
        h_ref[N_META:, :] = x_ref[...]
        for i in range(N_ROW_TILES):
            rows = slice(i * ROW_TILE, (i + 1) * ROW_TILE)
            hv = h_ref[rows, :]
            a_ref[rows, :] = (hv * _rstd(hv) * g_ref[...]).astype(BF16)

    return pl.pallas_call(
        body, out_shape=[SDS((SEQ, D_MODEL), F32), SDS((SEQ, D_MODEL), BF16)], name="embed_prenorm",
        compiler_params=_params())(meta, x, g)


def _post_pre(f, hres, g_post, g_next, scale, name):
    def body(f_ref, h_ref, gp_ref, gn_ref, ho_ref, a_ref):
        fv = f_ref[...]
        h = h_ref[...] + scale * (fv * _rstd(fv) * gp_ref[...])
        ho_ref[...] = h
        a_ref[...] = (h * _rstd(h) * gn_ref[...]).astype(BF16)

    return pl.pallas_call(
        body, grid=(N_ROW_TILES,),
        in_specs=[_row_spec(D_MODEL), _row_spec(D_MODEL), _fix_spec((1, D_MODEL)), _fix_spec((1, D_MODEL))],
        out_specs=[_row_spec(D_MODEL), _row_spec(D_MODEL)],
        out_shape=[SDS((SEQ, D_MODEL), F32), SDS((SEQ, D_MODEL), BF16)], name=name,
        compiler_params=_params(("parallel",)))(f, hres, g_post, g_next)


def _final_loss(f2, h2, g_post, g_final, target):
    def body(f_ref, h_ref, gp_ref, gf_ref, t_ref, loss_ref, dh_ref, df_ref, dgf_ref, dgp_ref):
        i = pl.program_id(0)
        fv = f_ref[...]
        r1 = _rstd(fv)
        gp = gp_ref[...]
        h3 = h_ref[...] + 0.5 * (fv * r1 * gp)
        r2 = _rstd(h3)
        gf = gf_ref[...]
        y = h3 * r2 * gf
        row = lax.broadcasted_iota(jnp.int32, (ROW_TILE, 1), 0) + i * ROW_TILE
        err = jnp.where(row >= N_META, y - t_ref[...], 0.0)
        part = 0.5 * jnp.sum(jnp.mean(err * err, axis=-1, keepdims=True))
        dy = err * (1.0 / D_MODEL)
        dh3, dgf = _rms_bwd(h3, r2, gf, dy)
        dh_ref[...] = dh3
        df, dgp = _rms_bwd(fv, r1, gp, 0.5 * dh3)
        df_ref[...] = df.astype(BF16)

        @pl.when(i == 0)
        def _():
            loss_ref[...] = jnp.zeros_like(loss_ref)
            dgf_ref[...] = jnp.zeros_like(dgf_ref)
            dgp_ref[...] = jnp.zeros_like(dgp_ref)

        loss_ref[...] += part
        dgf_ref[...] += jnp.sum(dgf, axis=0, keepdims=True)
        dgp_ref[...] += jnp.sum(dgp, axis=0, keepdims=True)

    gain = _fix_spec((1, D_MODEL))
    return pl.pallas_call(
        body, grid=(N_ROW_TILES,),
        in_specs=[_row_spec(D_MODEL), _row_spec(D_MODEL), gain, gain, _row_spec(D_MODEL)],
        out_specs=[_fix_spec((8, 128)), _row_spec(D_MODEL), _row_spec(D_MODEL), gain, gain],
        out_shape=[SDS((8, 128), F32), SDS((SEQ, D_MODEL), F32), SDS((SEQ, D_MODEL), BF16),
                   SDS((1, D_MODEL), F32), SDS((1, D_MODEL), F32)],
        name="final_loss", compiler_params=_params(("arbitrary",)))(f2, h2, g_post, g_final, target)


def _bwd_pre_post(da, h, g_pre, dh_res, fprev, g_post, scale, name):
    def body(da_ref, h_ref, gpre_ref, dhr_ref, f_ref, gpost_ref, dh_ref, df_ref, dgpre_ref, dgpost_ref):
        i = pl.program_id(0)
        hv = h_ref[...]
        dxa, dgpre = _rms_bwd(hv, _rstd(hv), gpre_ref[...], da_ref[...])
        dh = dhr_ref[...] + dxa
        dh_ref[...] = dh
        fv = f_ref[...]
        df, dgpost = _rms_bwd(fv, _rstd(fv), gpost_ref[...], scale * dh)
        df_ref[...] = df.astype(BF16)

        @pl.when(i == 0)
        def _():
            dgpre_ref[...] = jnp.zeros_like(dgpre_ref)
            dgpost_ref[...] = jnp.zeros_like(dgpost_ref)

        dgpre_ref[...] += jnp.sum(dgpre, axis=0, keepdims=True)
        dgpost_ref[...] += jnp.sum(dgpost, axis=0, keepdims=True)

    gain = _fix_spec((1, D_MODEL))
    row = _row_spec(D_MODEL)
    return pl.pallas_call(
        body, grid=(N_ROW_TILES,), in_specs=[row, row, gain, row, row, gain],
        out_specs=[row, row, gain, gain],
        out_shape=[SDS((SEQ, D_MODEL), F32), SDS((SEQ, D_MODEL), BF16), SDS((1, D_MODEL), F32), SDS((1, D_MODEL), F32)],
        name=name, compiler_params=_params(("arbitrary",)))(da, h, g_pre, dh_res, fprev, g_post)


def _bwd_embed(da, h, g_pre, dh_res):
    def body(da_ref, h_ref, gpre_ref, dhr_ref, gx_ref, gm_ref, dgpre_ref):
        total = jnp.zeros((1, D_MODEL), F32)
        for i in range(N_ROW_TILES):
            rows = slice(i * ROW_TILE, (i + 1) * ROW_TILE)
            hv = h_ref[rows, :]
            dxa, dgpre = _rms_bwd(hv, _rstd(hv), gpre_ref[...], da_ref[rows, :])
            dh = dhr_ref[rows, :] + dxa
            total = total + jnp.sum(dgpre, axis=0, keepdims=True)
            if i == 0:
                gm_ref[...] = dh[0:N_META, :]
                gx_ref[0:ROW_TILE - N_META, :] = dh[N_META:, :]
            else:
                gx_ref[i * ROW_TILE - N_META:(i + 1) * ROW_TILE - N_META, :] = dh
        dgpre_ref[...] = total

    return pl.pallas_call(
        body, out_shape=[SDS((N_TOK, D_MODEL), F32), SDS((N_META, D_MODEL), F32), SDS((1, D_MODEL), F32)],
        name="bwd_embed", compiler_params=_params())(da, h, g_pre, dh_res)


def _ffn_fwd(a, wg, wu, wd, name, after=()):
    def body(a_ref, wg_ref, wu_ref, wd_ref, *rest):
        gate_ref, up_ref, f_ref = rest[len(after):]
        j = pl.program_id(0)

        def tile(i, carry):
            rows = _rows(i)
            at = a_ref[rows, :]
            gate = _dot_nt(at, wg_ref[...])
            up = _dot_nt(at, wu_ref[...])
            gate_ref[rows, :] = gate.astype(BF16)
            up_ref[rows, :] = up.astype(BF16)
            act = (gate * jax.nn.sigmoid(gate) * up).astype(BF16)
            contrib = _dot(act, wd_ref[...])

            @pl.when(j == 0)
            def _():
                f_ref[rows, :] = contrib

            @pl.when(j != 0)
            def _():
                f_ref[rows, :] += contrib

            return carry

        lax.fori_loop(0, N_ROW_TILES, tile, 0)

    wtile = pl.BlockSpec((FF_TILE, D_MODEL), lambda j: (j, 0))
    hid = pl.BlockSpec((SEQ, FF_TILE), lambda j: (0, j))
    full = pl.BlockSpec((SEQ, D_MODEL), lambda j: (0, 0))
    return pl.pallas_call(
        body, grid=(D_FF // FF_TILE,), in_specs=[full, wtile, wtile, wtile] + [pl.BlockSpec(memory_space=pl.ANY)] * len(after),
        out_specs=[hid, hid, full],
        out_shape=[SDS((SEQ, D_FF), BF16), SDS((SEQ, D_FF), BF16), SDS((SEQ, D_MODEL), F32)],
        name=name, compiler_params=_params(("arbitrary",)))(a, wg, wu, wd, *after)


def _ffn_bwd(df, a, gate, up, wg, wu, wd, name):
    def body(df_ref, a_ref, gate_ref, up_ref, wg_ref, wu_ref, wd_ref, da_ref, dwg_ref, dwu_ref, dwd_ref,
             acc_g, acc_u, acc_d):
        j = pl.program_id(0)

        def tile(i, carry):
            rows = _rows(i)
            dft = df_ref[rows, :]
            at = a_ref[rows, :]
            gate = gate_ref[rows, :].astype(F32)
            up = up_ref[rows, :].astype(F32)
            dact = _dot_nt(dft, wd_ref[...])
            sig = jax.nn.sigmoid(gate)
            silu = gate * sig
            dgate = (dact * up * (sig * (1.0 + gate * (1.0 - sig)))).astype(BF16)
            dup = (dact * silu).astype(BF16)
            act = (silu * up).astype(BF16)
            dwd = _dot_tn(act, dft)
            dwg = _dot_tn(dgate, at)
            dwu = _dot_tn(dup, at)
            dat = _dot(dgate, wg_ref[...]) + _dot(dup, wu_ref[...])

            @pl.when(i == 0)
            def _():
                acc_d[...] = dwd
                acc_g[...] = dwg
                acc_u[...] = dwu

            @pl.when(i != 0)
            def _():
                acc_d[...] += dwd
                acc_g[...] += dwg
                acc_u[...] += dwu

            @pl.when(j == 0)
            def _():
                da_ref[rows, :] = dat

            @pl.when(j != 0)
            def _():
                da_ref[rows, :] += dat

            return carry

        lax.fori_loop(0, N_ROW_TILES, tile, 0)
        dwg_ref[...] = acc_g[...].astype(BF16)
        dwu_ref[...] = acc_u[...].astype(BF16)
        dwd_ref[...] = acc_d[...].astype(BF16)

    wtile = pl.BlockSpec((FF_TILE, D_MODEL), lambda j: (j, 0))
    hid = pl.BlockSpec((SEQ, FF_TILE), lambda j: (0, j))
    full = pl.BlockSpec((SEQ, D_MODEL), lambda j: (0, 0))
    return pl.pallas_call(
        body, grid=(D_FF // FF_TILE,), in_specs=[full, full, hid, hid, wtile, wtile, wtile],
        out_specs=[full, wtile, wtile, wtile],
        out_shape=[SDS((SEQ, D_MODEL), F32)] + [SDS((D_FF, D_MODEL), BF16)] * 3,
        scratch_shapes=[pltpu.VMEM((FF_TILE, D_MODEL), F32)] * 3,
        name=name, compiler_params=_params(("arbitrary",)))(df, a, gate, up, wg, wu, wd)


HEADS_PER_BLOCK = IN_SHARD // HEAD_DIM
QKV_BLOCKS = 3 * NA_WIDTH // IN_SHARD


def _proj_heads(a, w):
    def body(a_ref, w_ref, o_ref):
        def tile(i, carry):
            rows = _rows(i)
            res = _dot(a_ref[rows, :], w_ref[...])
            for sub in range(HEADS_PER_BLOCK):
                o_ref[sub, rows, :] = res[:, sub * HEAD_DIM:(sub + 1) * HEAD_DIM]
            return carry

        lax.fori_loop(0, N_ROW_TILES, tile, 0)

    return pl.pallas_call(
        body, grid=(QKV_BLOCKS,),
        in_specs=[pl.BlockSpec((SEQ, D_MODEL), lambda j: (0, 0)), pl.BlockSpec((None, D_MODEL, IN_SHARD), lambda j: (j, 0, 0))],
        out_specs=pl.BlockSpec((HEADS_PER_BLOCK, SEQ, HEAD_DIM), lambda j: (j, 0, 0)),
        out_shape=SDS((3 * HEADS, SEQ, HEAD_DIM), F32), name="proj_heads",
        compiler_params=_params(("parallel",)))(a, w)


def _proj_u(a, w):
    def body(a_ref, w_ref, o_ref):
        def tile(i, carry):
            rows = _rows(i)
            o_ref[rows, :] = _dot(a_ref[rows, :], w_ref[...])
            return carry

        lax.fori_loop(0, N_ROW_TILES, tile, 0)

    return pl.pallas_call(
        body, grid=(N_DEV - QKV_BLOCKS,),
        in_specs=[pl.BlockSpec((SEQ, D_MODEL), lambda j: (0, 0)),
                  pl.BlockSpec((None, D_MODEL, IN_SHARD), lambda j: (j + QKV_BLOCKS, 0, 0))],
        out_specs=pl.BlockSpec((SEQ, IN_SHARD), lambda j: (0, j)),
        out_shape=SDS((SEQ, S5_WIDTH), F32), name="proj_u",
        compiler_params=_params(("parallel",)))(a, w)


def _proj_bwd(dq, dk, dv, du, a, w):
    def body(dq_ref, dk_ref, dv_ref, du_ref, a_ref, w_ref, da_ref, dw_ref, acc, dp_ref):
        j = pl.program_id(0)

        for which, src in enumerate((dq_ref, dk_ref, dv_ref)):
            @pl.when((j >= 2 * which) & (j < 2 * which + 2))
            def _(src=src):
                dp_ref[...] = jnp.concatenate([src[sub] for sub in range(HEADS_PER_BLOCK)], axis=-1).astype(BF16)

        @pl.when(j >= QKV_BLOCKS)
        def _():
            dp_ref[...] = du_ref[...].astype(BF16)

        def tile(i, carry):
            rows = _rows(i)
            dpt = dp_ref[rows, :]
            dw = _dot_tn(a_ref[rows, :], dpt)
            dat = _dot_nt(dpt, w_ref[...])

            @pl.when(i == 0)
            def _():
                acc[...] = dw

            @pl.when(i != 0)
            def _():
                acc[...] += dw

            @pl.when(j == 0)
            def _():
                da_ref[rows, :] = dat

            @pl.when(j != 0)
            def _():
                da_ref[rows, :] += dat

            return carry

        lax.fori_loop(0, N_ROW_TILES, tile, 0)
        dw_ref[...] = acc[...].astype(BF16)

    full = pl.BlockSpec((SEQ, D_MODEL), lambda j: (0, 0))
    wspec = pl.BlockSpec((None, D_MODEL, IN_SHARD), lambda j: (j, 0, 0))

    def heads(which):
        return pl.BlockSpec((HEADS_PER_BLOCK, SEQ, HEAD_DIM), lambda j: (jnp.clip(j - 2 * which, 0, 1), 0, 0))

    return pl.pallas_call(
        body, grid=(N_DEV,),
        in_specs=[heads(0), heads(1), heads(2),
                  pl.BlockSpec((SEQ, IN_SHARD), lambda j: (0, jnp.clip(j - QKV_BLOCKS, 0, 1))), full, wspec],
        out_specs=[full, wspec],
        out_shape=[SDS((SEQ, D_MODEL), F32), SDS((N_DEV, D_MODEL, IN_SHARD), BF16)],
        scratch_shapes=[pltpu.VMEM((D_MODEL, IN_SHARD), F32), pltpu.VMEM((SEQ, IN_SHARD), BF16)],
        name="proj_bwd", compiler_params=_params(("arbitrary",)))(dq, dk, dv, du, a, w)


def _na_consts():
    c = np.arange(GRID_W)
    col_start = np.clip(c - KW // 2, 0, GRID_W - KW)
    col_in = (c[None, :] >= col_start[:, None]) & (c[None, :] < col_start[:, None] + KW)
    dc = np.clip(c[None, :] - c[:, None] + KW - 1, 0, 2 * KW - 2)
    onehot = np.zeros((128, GRID_W * GRID_W), np.float32)
    qq, kk = np.meshgrid(c, c, indexing="ij")
    onehot[dc[col_in], (qq * GRID_W + kk)[col_in]] = 1.0
    negmask = np.where(col_in, 0.0, NEG_INF).astype(np.float32).reshape(1, -1)
    return onehot, negmask


def _na_pair(block_type, a, b):
    if block_type == 0:
        return b - a + KH - 1 if b < KH else None
    if block_type == 1:
        return b - a + KH // 2 - 1 if a <= b < a + KH else None
    return b - a if b >= NA_KR - KH else None


def _rpb_expand(rpb):
    onehot, negmask = _na_consts()
    rows = HEADS * (2 * KH - 1)
    rpb_pad = jnp.pad(rpb.reshape(rows, 2 * KW - 1), ((0, 128 - rows), (0, 128 - (2 * KW - 1))))

    def body(r_ref, oh_ref, m_ref, t_ref):
        hi, mid, lo = _split3(r_ref[...])
        oh = oh_ref[...]
        t_ref[...] = _dot(hi, oh) + _dot(mid, oh) + _dot(lo, oh) + m_ref[...]

    table = pl.pallas_call(body, out_shape=SDS((128, GRID_W * GRID_W), F32), name="rpb_expand",
                           compiler_params=_params())(rpb_pad, jnp.asarray(onehot, BF16), jnp.asarray(negmask))
    return table[:rows].reshape(HEADS, 2 * KH - 1, GRID_W, GRID_W)


def _rpb_reduce(dslabs):
    onehot, _ = _na_consts()
    rows = HEADS * (2 * KH - 1)

    def body(x_ref, oht_ref, o_ref):
        hi, mid, lo = _split3(x_ref[...])
        oht = oht_ref[...]
        o_ref[...] = _dot(hi, oht) + _dot(mid, oht) + _dot(lo, oht)

    out = pl.pallas_call(body, out_shape=SDS((rows, 128), F32), name="rpb_reduce", compiler_params=_params())(
        dslabs.reshape(rows, GRID_W * GRID_W), jnp.asarray(onehot.T, BF16))
    return out.reshape(HEADS, 2 * KH - 1, 128)


def _bias_tiles(slab_ref, tile_ref):
    tile_ref[...] = jnp.full(tile_ref.shape, NEG_INF, F32)
    for t in range(NA_TYPES):
        for a in range(NA_RB):
            for b in range(NA_KR):
                dr = _na_pair(t, a, b)
                if dr is not None:
                    tile_ref[t, a * GRID_W:(a + 1) * GRID_W, b * GRID_W:(b + 1) * GRID_W] = slab_ref[dr]


def _bias_tiles_bwd(dtile_ref, dslab_ref):
    acc = {}
    for t in range(NA_TYPES):
        for a in range(NA_RB):
            for b in range(NA_KR):
                dr = _na_pair(t, a, b)
                if dr is not None:
                    part = dtile_ref[t, a * GRID_W:(a + 1) * GRID_W, b * GRID_W:(b + 1) * GRID_W]
                    acc[dr] = part if dr not in acc else acc[dr] + part
    for dr in range(2 * KH - 1):
        dslab_ref[dr] = acc[dr]


def _block_geometry(g):
    start = jnp.clip(g * NA_RB - KH // 2, 0, GRID_ROWS - NA_KR)
    block_type = jnp.where(g == 0, 0, jnp.where(g == NA_BLOCKS - 1, 2, 1))
    q0 = pl.multiple_of(N_META + g * NA_QB, 16)
    k0 = pl.multiple_of(N_META + start * GRID_W, 16)
    return block_type, q0, k0


def _na_probs(q, kk, km, bias):
    s = _dot_nt(q, kk) * ATT_SCALE + bias
    sm = _dot_nt(q, km) * ATT_SCALE
    m = jnp.maximum(jnp.max(s, axis=-1, keepdims=True), jnp.max(sm, axis=-1, keepdims=True))
    p = jnp.exp(s - m)
    pm = jnp.exp(sm - m)
    inv = 1.0 / (jnp.sum(p, axis=-1, keepdims=True) + jnp.sum(pm, axis=-1, keepdims=True))
    return p * inv, pm * inv


def _meta_probs(qm, km):
    s = _dot_nt(qm, km) * ATT_SCALE
    p = jnp.exp(s - jnp.max(s, axis=-1, keepdims=True))
    return p / jnp.sum(p, axis=-1, keepdims=True)


def _qkv_specs():
    return [pl.BlockSpec((None, SEQ, HEAD_DIM), lambda h, which=which: (h + which * HEADS, 0, 0)) for which in range(3)]


def _na_fwd(qkv, bias):
    def body(q_ref, k_ref, v_ref, slab_ref, o_ref, b_ref):
        _bias_tiles(slab_ref, b_ref)
        km = k_ref[0:N_META, :].astype(BF16)
        vm = v_ref[0:N_META, :].astype(BF16)
        pmm = _meta_probs(q_ref[0:N_META, :].astype(BF16), km)
        o_ref[0:N_META, :] = _dot(pmm.astype(BF16), vm)

        def block(g, carry):
            block_type, q0, k0 = _block_geometry(g)
            qb = q_ref[pl.ds(q0, NA_QB), :].astype(BF16)
            kk = k_ref[pl.ds(k0, NA_KB), :].astype(BF16)
            vv = v_ref[pl.ds(k0, NA_KB), :].astype(BF16)
            p, pm = _na_probs(qb, kk, km, b_ref[block_type])
            o_ref[pl.ds(q0, NA_QB), :] = _dot(p.astype(BF16), vv) + _dot(pm.astype(BF16), vm)
            return carry

        lax.fori_loop(0, NA_BLOCKS, block, 0)

    head = pl.BlockSpec((None, SEQ, HEAD_DIM), lambda h: (h, 0, 0))
    return pl.pallas_call(
        body, grid=(HEADS,), in_specs=_qkv_specs() + [pl.BlockSpec((None, 2 * KH - 1, GRID_W, GRID_W), lambda h: (h, 0, 0, 0))],
        out_specs=head, out_shape=SDS((HEADS, SEQ, HEAD_DIM), F32), name="na_fwd",
        scratch_shapes=[pltpu.VMEM((NA_TYPES, NA_QB, NA_KB), F32)],
        compiler_params=_params(("parallel",)))(qkv, qkv, qkv, bias)


def _na_bwd(qkv, bias, do):
    def body(q_ref, k_ref, v_ref, slab_ref, do_ref, dq_ref, dk_ref, dv_ref, dslab_ref, b_ref, db_ref):
        _bias_tiles(slab_ref, b_ref)
        km = k_ref[0:N_META, :].astype(BF16)
        vm = v_ref[0:N_META, :].astype(BF16)
        dk_ref[...] = jnp.zeros_like(dk_ref)
        dv_ref[...] = jnp.zeros_like(dv_ref)
        db_ref[...] = jnp.zeros_like(db_ref)

        qm = q_ref[0:N_META, :].astype(BF16)
        dom = do_ref[0:N_META, :].astype(BF16)
        pmm = _meta_probs(qm, km)
        dpm = _dot_nt(dom, vm)
        dsm = (pmm * (dpm - jnp.sum(pmm * dpm, axis=-1, keepdims=True)) * ATT_SCALE).astype(BF16)
        dq_ref[0:N_META, :] = _dot(dsm, km)
        dkm0 = _dot_tn(dsm, qm)
        dvm0 = _dot_tn(pmm.astype(BF16), dom)

        def block(g, carry):
            dkm, dvm = carry
            block_type, q0, k0 = _block_geometry(g)
            qb = q_ref[pl.ds(q0, NA_QB), :].astype(BF16)
            kk = k_ref[pl.ds(k0, NA_KB), :].astype(BF16)
            vv = v_ref[pl.ds(k0, NA_KB), :].astype(BF16)
            dob = do_ref[pl.ds(q0, NA_QB), :].astype(BF16)
            p, pm = _na_probs(qb, kk, km, b_ref[block_type])
            dp = _dot_nt(dob, vv)
            dpm_ = _dot_nt(dob, vm)
            delta = jnp.sum(p * dp, axis=-1, keepdims=True) + jnp.sum(pm * dpm_, axis=-1, keepdims=True)
            ds = p * (dp - delta)
            dsm_ = pm * (dpm_ - delta)
            db_ref[block_type] += ds
            dsb = (ds * ATT_SCALE).astype(BF16)
            dsmb = (dsm_ * ATT_SCALE).astype(BF16)
            dq_ref[pl.ds(q0, NA_QB), :] = _dot(dsb, kk) + _dot(dsmb, km)
            dk_ref[pl.ds(k0, NA_KB), :] += _dot_tn(dsb, qb)
            dv_ref[pl.ds(k0, NA_KB), :] += _dot_tn(p.astype(BF16), dob)
            return dkm + _dot_tn(dsmb, qb), dvm + _dot_tn(pm.astype(BF16), dob)

        dkm, dvm = lax.fori_loop(0, NA_BLOCKS, block, (dkm0, dvm0))
        dk_ref[0:N_META, :] = dkm
        dv_ref[0:N_META, :] = dvm
        _bias_tiles_bwd(db_ref, dslab_ref)

    head = pl.BlockSpec((None, SEQ, HEAD_DIM), lambda h: (h, 0, 0))
    bspec = pl.BlockSpec((None, 2 * KH - 1, GRID_W, GRID_W), lambda h: (h, 0, 0, 0))
    return pl.pallas_call(
        body, grid=(HEADS,), in_specs=_qkv_specs() + [bspec, head], out_specs=[head, head, head, bspec],
        out_shape=[SDS((HEADS, SEQ, HEAD_DIM), F32)] * 3 + [SDS((HEADS, 2 * KH - 1, GRID_W, GRID_W), F32)],
        scratch_shapes=[pltpu.VMEM((NA_TYPES, NA_QB, NA_KB), F32), pltpu.VMEM((NA_TYPES, NA_QB, NA_KB), F32)],
        name="na_bwd", compiler_params=_params(("parallel",)))(qkv, qkv, qkv, bias, do)


def _cmul(ar, ai, br, bi):
    return ar * br - ai * bi, ar * bi + ai * br


def _cpow(ar, ai, n):
    rr, ri = None, None
    br, bi = ar, ai
    while n:
        if n & 1:
            rr, ri = (br, bi) if rr is None else _cmul(rr, ri, br, bi)
        n >>= 1
        if n:
            br, bi = _cmul(br, bi, br, bi)
    return rr, ri


def _s5_prep(lr, li, logdt, bre, bim):
    def body(lr_ref, li_ref, dt_ref, br_ref, bi_ref, lbr_ref, lbi_ref, bbr_ref, bbi_ref):
        lr_, li_ = lr_ref[...], li_ref[...]
        dt = jnp.exp(dt_ref[...])
        mag = jnp.exp(lr_ * dt)
        lbr = mag * jnp.cos(li_ * dt)
        lbi = mag * jnp.sin(li_ * dt)
        lbr_ref[...] = lbr
        lbi_ref[...] = lbi
        den = lr_ * lr_ + li_ * li_
        xr = lbr - 1.0
        cr = (xr * lr_ + lbi * li_) / den
        ci = (lbi * lr_ - xr * li_) / den
        br, bi = br_ref[...], bi_ref[...]
        bbr_ref[...] = cr[:, None, :] * br - ci[:, None, :] * bi
        bbi_ref[...] = cr[:, None, :] * bi + ci[:, None, :] * br

    n = 2 * S5_GROUPS
    return pl.pallas_call(
        body, out_shape=[SDS((n, S5_STATE), F32)] * 2 + [SDS((n, S5_GROUP, S5_STATE), F32)] * 2,
        name="s5_prep", compiler_params=_params())(lr, li, logdt, bre, bim)


def _s5_prep_bwd(lr, li, logdt, bre, bim, dar, dai, dbbr, dbbi):
    def body(lr_ref, li_ref, dt_ref, br_ref, bi_ref, dar_ref, dai_ref, dbr_ref, dbi_ref,
             glr_ref, gli_ref, gdt_ref, gbr_ref, gbi_ref):
        lr_, li_ = lr_ref[...], li_ref[...]
        dt = jnp.exp(dt_ref[...])
        mag = jnp.exp(lr_ * dt)
        lbr = mag * jnp.cos(li_ * dt)
        lbi = mag * jnp.sin(li_ * dt)
        den = lr_ * lr_ + li_ * li_
        xr = lbr - 1.0
        cr = (xr * lr_ + lbi * li_) / den
        ci = (lbi * lr_ - xr * li_) / den
        br, bi = br_ref[...], bi_ref[...]
        dbr, dbi = dbr_ref[...], dbi_ref[...]
        gbr_ref[...] = cr[:, None, :] * dbr + ci[:, None, :] * dbi
        gbi_ref[...] = cr[:, None, :] * dbi - ci[:, None, :] * dbr
        gcr = jnp.sum(dbr * br + dbi * bi, axis=1)
        gci = jnp.sum(dbi * br - dbr * bi, axis=1)
        ilr, ili = lr_ / den, li_ / den
        tr, ti = _cmul(gcr, gci, ilr, ili)
        glbr = dar_ref[...] + tr
        glbi = dai_ref[...] + ti
        dr_, di_ = _cmul(tr, ti, cr, -ci)
        gwr, gwi = _cmul(glbr, glbi, lbr, -lbi)
        glr_ref[...] = gwr * dt - dr_
        gli_ref[...] = gwi * dt - di_
        gdt_ref[...] = jnp.sum(gwr * lr_ + gwi * li_, axis=-1, keepdims=True) * dt

    n = 2 * S5_GROUPS
    return pl.pallas_call(
        body, out_shape=[SDS((n, S5_STATE), F32)] * 2 + [SDS((n, 1), F32)] + [SDS((n, S5_GROUP, S5_STATE), F32)] * 2,
        name="s5_prep_bwd", compiler_params=_params())(lr, li, logdt, bre, bim, dar, dai, dbbr, dbbi)


def _scan_local(xr_ref, xi_ref, ar8, ai8, reverse):
    def step(i, carry):
        sr, si = carry
        idx = (SCAN_T - 1 - i) if reverse else i
        rows = pl.ds(pl.multiple_of(idx * SCAN_BLOCKS, SCAN_BLOCKS), SCAN_BLOCKS)
        nr = ar8 * sr - ai8 * si + xr_ref[rows, :]
        ni = ar8 * si + ai8 * sr + xi_ref[rows, :]
        xr_ref[rows, :] = nr
        xi_ref[rows, :] = ni
        return nr, ni

    z = jnp.zeros(ar8.shape, F32)
    return lax.fori_loop(0, SCAN_T, step, (z, z))


def _scan_carries(er, ei, atr, ati, reverse):
    row = lax.broadcasted_iota(jnp.int32, er.shape, 0)
    cr = jnp.zeros((1, er.shape[1]), F32)
    ci = cr
    outr = jnp.zeros(er.shape, F32)
    outi = outr
    order = range(SCAN_BLOCKS - 1, -1, -1) if reverse else range(SCAN_BLOCKS)
    for b in order:
        outr = jnp.where(row == b, cr, outr)
        outi = jnp.where(row == b, ci, outi)
        nr, ni = _cmul(atr, ati, cr, ci)
        cr, ci = nr + er[b:b + 1, :], ni + ei[b:b + 1, :]
    return outr, outi


def _scan_fixup(xr_ref, xi_ref, cr8, ci8, ar8, ai8, reverse, pair=None):
    tile = lambda idx: pl.ds(pl.multiple_of(idx * SCAN_BLOCKS, SCAN_BLOCKS), SCAN_BLOCKS)

    def fix(idx, pr, pi):
        fr, fi = _cmul(pr, pi, cr8, ci8)
        nr, ni = xr_ref[tile(idx), :] + fr, xi_ref[tile(idx), :] + fi
        xr_ref[tile(idx), :] = nr
        xi_ref[tile(idx), :] = ni
        return nr, ni

    if pair is None:
        def step(i, carry):
            pr, pi = carry
            fix((SCAN_T - 1 - i) if reverse else i, pr, pi)
            return _cmul(pr, pi, ar8, ai8)

        lax.fori_loop(0, SCAN_T, step, (ar8, ai8), unroll=2)
        return None

    sr_ref, si_ref = pair
    earlier = -1 if reverse else 1

    def step(i, carry):
        pr, pi, accr, acci = carry
        idx = (SCAN_T - 1 - i) if reverse else i
        nr, ni = fix(idx, pr, pi)
        qr, qi = _cmul(nr, ni, sr_ref[tile(idx + earlier), :], -si_ref[tile(idx + earlier), :])
        pr, pi = _cmul(pr, pi, ar8, ai8)
        return pr, pi, accr + qr, acci + qi

    z = jnp.zeros(ar8.shape, F32)
    pr, pi, accr, acci = lax.fori_loop(0, SCAN_T - 1, step, (ar8, ai8, z, z))
    edge, src, shift, empty = (0, SCAN_T - 1, 1, 0) if reverse else (SCAN_T - 1, 0, SCAN_BLOCKS - 1, SCAN_BLOCKS - 1)
    nr, ni = fix(edge, pr, pi)
    row = lax.broadcasted_iota(jnp.int32, ar8.shape, 0)
    spr = jnp.where(row == empty, 0.0, pltpu.roll(sr_ref[tile(src), :], shift, 0))
    spi = jnp.where(row == empty, 0.0, pltpu.roll(si_ref[tile(src), :], shift, 0))
    qr, qi = _cmul(nr, ni, spr, -spi)
    return jnp.sum(accr + qr, axis=0, keepdims=True), jnp.sum(acci + qi, axis=0, keepdims=True)


def _scan(xr_ref, xi_ref, ar, ai, reverse, pair=None):
    n = ar.shape[1]
    ar8 = jnp.broadcast_to(ar, (SCAN_BLOCKS, n))
    ai8 = jnp.broadcast_to(ai, (SCAN_BLOCKS, n))
    er, ei = _scan_local(xr_ref, xi_ref, ar8, ai8, reverse)
    atr, ati = _cpow(ar, ai, SCAN_T)
    cr8, ci8 = _scan_carries(er, ei, atr, ati, reverse)
    return _scan_fixup(xr_ref, xi_ref, cr8, ci8, ar8, ai8, reverse, pair)


def _s5_specs():
    chan = pl.BlockSpec((SEQ, CH_W), lambda c, d: (0, c))
    chan2 = pl.BlockSpec((None, SEQ, CH_W), lambda c, d: (d, 0, c))
    state = pl.BlockSpec((None, SEQ, ST_W), lambda c, d: (d, 0, c))
    bmat = pl.BlockSpec((None, None, CH_W, ST_W), lambda c, d: (d, c, 0, 0))
    cmat = pl.BlockSpec((None, None, ST_W, CH_W), lambda c, d: (d, c, 0, 0))
    avec = pl.BlockSpec((None, None, 1, ST_W), lambda c, d: (d, c, 0, 0))
    return chan, chan2, state, bmat, cmat, avec


def _scan_by_direction(xr_ref, xi_ref, ar, ai, d, adjoint, pair=None, da_out=None):
    for direction in range(2):
        @pl.when(d == direction)
        def _(direction=direction):
            res = _scan(xr_ref, xi_ref, ar, ai, adjoint != (direction == 1), pair)
            if pair is not None:
                da_out[0][...], da_out[1][...] = res


def _s5_scan_fwd(u, bre, bim, are, aim, cre, cim):
    def body(u_ref, bre_ref, bim_ref, are_ref, aim_ref, cre_ref, cim_ref, sr_ref, si_ref, y_ref):
        ub = u_ref[...].astype(BF16)
        sr_ref[...] = _dot(ub, bre_ref[...])
        si_ref[...] = _dot(ub, bim_ref[...])
        _scan_by_direction(sr_ref, si_ref, are_ref[...], aim_ref[...], pl.program_id(1), adjoint=False)
        y_ref[...] = _dot(sr_ref[...].astype(BF16), cre_ref[...]) - _dot(si_ref[...].astype(BF16), cim_ref[...])

    chan, chan2, state, bmat, cmat, avec = _s5_specs()
    return pl.pallas_call(
        body, grid=(S5_CHUNKS, 2), in_specs=[chan, bmat, bmat, avec, avec, cmat, cmat], out_specs=[state, state, chan2],
        out_shape=[SDS((2, SEQ, S5_GROUPS * S5_STATE), F32)] * 2 + [SDS((2, SEQ, S5_WIDTH), F32)],
        name="s5_scan_fwd", compiler_params=_params(("parallel", "parallel")))(u, bre, bim, are, aim, cre, cim)


def _diag_out(out_ref, full):
    for g in range(8):
        out_ref[g] = full[g * S5_GROUP:(g + 1) * S5_GROUP, g * S5_STATE:(g + 1) * S5_STATE]


def _s5_scan_bwd(dy, du_skip, u, sr, si, bre, bim, are, aim, cre, cim):
    def body(dy_ref, dus_ref, u_ref, sr_ref, si_ref, bre_ref, bim_ref, are_ref, aim_ref, cre_ref, cim_ref,
             du_ref, dbr_ref, dbi_ref, dcr_ref, dci_ref, dar_ref, dai_ref, gr_ref, gi_ref):
        d = pl.program_id(1)
        dyb = dy_ref[...].astype(BF16)
        gr_ref[...] = _dot_nt(dyb, cre_ref[...])
        gi_ref[...] = -_dot_nt(dyb, cim_ref[...])
        _diag_out(dcr_ref, _dot_tn(dyb, sr_ref[...].astype(BF16)))
        _diag_out(dci_ref, -_dot_tn(dyb, si_ref[...].astype(BF16)))
        _scan_by_direction(gr_ref, gi_ref, are_ref[...], -aim_ref[...], d, adjoint=True, pair=(sr_ref, si_ref),
                           da_out=(dar_ref, dai_ref))

        @pl.when(d == 0)
        def _():
            du_ref[...] = dus_ref[...]

        grb = gr_ref[...].astype(BF16)
        gib = gi_ref[...].astype(BF16)
        du_ref[...] += _dot_nt(grb, bre_ref[...]) + _dot_nt(gib, bim_ref[...])
        ub = u_ref[...].astype(BF16)
        _diag_out(dbr_ref, _dot_tn(ub, grb))
        _diag_out(dbi_ref, _dot_tn(ub, gib))

    chan, _, state, bmat, cmat, avec = _s5_specs()
    diag = pl.BlockSpec((None, None, 8, S5_GROUP, S5_STATE), lambda c, d: (d, c, 0, 0, 0))
    return pl.pallas_call(
        body, grid=(S5_CHUNKS, 2), in_specs=[chan, chan, chan, state, state, bmat, bmat, avec, avec, cmat, cmat],
        out_specs=[chan, diag, diag, diag, diag, avec, avec],
        out_shape=[SDS((SEQ, S5_WIDTH), F32)] + [SDS((2, S5_CHUNKS, 8, S5_GROUP, S5_STATE), F32)] * 4
                  + [SDS((2, S5_CHUNKS, 1, ST_W), F32)] * 2,
        scratch_shapes=[pltpu.VMEM((SEQ, ST_W), F32), pltpu.VMEM((SEQ, ST_W), F32)],
        name="s5_scan_bwd", compiler_params=_params(("parallel", "arbitrary")))(dy, du_skip, u, sr, si, bre, bim, are, aim, cre, cim)


_GELU_K = math.sqrt(2.0 / math.pi)
_GELU_C = 0.044715


def _gelu(x):
    t = jnp.tanh(_GELU_K * (x + _GELU_C * x * x * x))
    return 0.5 * x * (1.0 + t), t


def _s5_glu_fwd(u, y2, dskip, wglu, bglu):
    def body(u_ref, y0_ref, y1_ref, d_ref, w_ref, b_ref, o_ref, yp_ref):
        ypre = u_ref[...] * d_ref[...] + y0_ref[...] + y1_ref[...]
        yp_ref[...] = ypre
        y, _ = _gelu(ypre)
        z = _dot(y.astype(BF16), w_ref[...]) + b_ref[...]
        o_ref[...] = y * jax.nn.sigmoid(z)

    row = _row_spec(S5_WIDTH)
    vec = _fix_spec((1, S5_WIDTH))
    dir0 = pl.BlockSpec((None, ROW_TILE, S5_WIDTH), lambda i: (0, i, 0))
    dir1 = pl.BlockSpec((None, ROW_TILE, S5_WIDTH), lambda i: (1, i, 0))
    return pl.pallas_call(
        body, grid=(N_ROW_TILES,), in_specs=[row, dir0, dir1, vec, _fix_spec((S5_WIDTH, S5_WIDTH)), vec],
        out_specs=[row, row], out_shape=[SDS((SEQ, S5_WIDTH), F32)] * 2, name="s5_glu_fwd",
        compiler_params=_params(("parallel",)))(u, y2, y2, dskip, wglu, bglu)


def _s5_glu_bwd(do, ypre, u, dskip, wglu, bglu):
    def body(do_ref, yp_ref, u_ref, d_ref, w_ref, b_ref, dyp_ref, du_ref, dw_ref, db_ref, dd_ref):
        i = pl.program_id(0)
        ypre = yp_ref[...]
        y, t = _gelu(ypre)
        yb = y.astype(BF16)
        sg = jax.nn.sigmoid(_dot(yb, w_ref[...]) + b_ref[...])
        dov = do_ref[...]
        dz = dov * y * sg * (1.0 - sg)
        dzb = dz.astype(BF16)
        dy = dov * sg + _dot_nt(dzb, w_ref[...])
        dgelu = 0.5 * (1.0 + t) + 0.5 * ypre * (1.0 - t * t) * _GELU_K * (1.0 + 3.0 * _GELU_C * ypre * ypre)
        dyp = dy * dgelu
        dyp_ref[...] = dyp
        uv = u_ref[...]
        du_ref[...] = dyp * d_ref[...]

        @pl.when(i == 0)
        def _():
            dw_ref[...] = jnp.zeros_like(dw_ref)
            db_ref[...] = jnp.zeros_like(db_ref)
            dd_ref[...] = jnp.zeros_like(dd_ref)

        dw_ref[...] += _dot_tn(yb, dzb)
        db_ref[...] += jnp.sum(dz, axis=0, keepdims=True)
        dd_ref[...] += jnp.sum(dyp * uv, axis=0, keepdims=True)

    row = _row_spec(S5_WIDTH)
    vec = _fix_spec((1, S5_WIDTH))
    mat = _fix_spec((S5_WIDTH, S5_WIDTH))
    return pl.pallas_call(
        body, grid=(N_ROW_TILES,), in_specs=[row, row, row, vec, mat, vec], out_specs=[row, row, mat, vec, vec],
        out_shape=[SDS((SEQ, S5_WIDTH), F32)] * 2 + [SDS((S5_WIDTH, S5_WIDTH), F32), SDS((1, S5_WIDTH), F32), SDS((1, S5_WIDTH), F32)],
        name="s5_glu_bwd", compiler_params=_params(("arbitrary",)))(do, ypre, u, dskip, wglu, bglu)


def _heads_side_by_side(o_ref):
    return jnp.concatenate([o_ref[h] for h in range(HEADS)], axis=-1)


def _mix_out_fwd(ona, os5, g_na, g_s5, wout):
    def body(a_ref, s_ref, ga_ref, gs_ref, w_ref, o_ref):
        av, sv = _heads_side_by_side(a_ref), s_ref[...]
        ca = (av * _rstd(av) * ga_ref[...]).astype(BF16)
        cs = (sv * _rstd(sv) * gs_ref[...]).astype(BF16)
        o_ref[...] = _dot(ca, w_ref[0:NA_WIDTH, :]) + _dot(cs, w_ref[NA_WIDTH:, :])

    row = _row_spec(NA_WIDTH)
    vec = _fix_spec((1, NA_WIDTH))
    heads = pl.BlockSpec((HEADS, ROW_TILE, HEAD_DIM), lambda i: (0, i, 0))
    return pl.pallas_call(
        body, grid=(N_ROW_TILES,), in_specs=[heads, row, vec, vec, _fix_spec((D_MODEL, D_MODEL))],
        out_specs=_row_spec(D_MODEL), out_shape=SDS((SEQ, D_MODEL), F32), name="mix_out_fwd",
        compiler_params=_params(("parallel",)))(ona, os5, g_na, g_s5, wout)


def _mix_out_bwd(dmix, ona, os5, g_na, g_s5, wout):
    def body(dm_ref, a_ref, s_ref, ga_ref, gs_ref, w_ref, da_ref, ds_ref, dw_ref, dga_ref, dgs_ref):
        i = pl.program_id(0)
        dm = dm_ref[...]
        av, sv = _heads_side_by_side(a_ref), s_ref[...]
        ra, rs = _rstd(av), _rstd(sv)
        ga, gs = ga_ref[...], gs_ref[...]
        ca = (av * ra * ga).astype(BF16)
        cs = (sv * rs * gs).astype(BF16)
        dca = _dot_nt(dm, w_ref[0:NA_WIDTH, :])
        dcs = _dot_nt(dm, w_ref[NA_WIDTH:, :])
        da, dga = _rms_bwd(av, ra, ga, dca)
        ds, dgs = _rms_bwd(sv, rs, gs, dcs)
        for h in range(HEADS):
            da_ref[h] = da[:, h * HEAD_DIM:(h + 1) * HEAD_DIM]
        ds_ref[...] = ds

        @pl.when(i == 0)
        def _():
            dw_ref[...] = jnp.zeros_like(dw_ref)
            dga_ref[...] = jnp.zeros_like(dga_ref)
            dgs_ref[...] = jnp.zeros_like(dgs_ref)

        dw_ref[0:NA_WIDTH, :] += _dot_tn(ca, dm)
        dw_ref[NA_WIDTH:, :] += _dot_tn(cs, dm)
        dga_ref[...] += jnp.sum(dga, axis=0, keepdims=True)
        dgs_ref[...] += jnp.sum(dgs, axis=0, keepdims=True)

    row = _row_spec(NA_WIDTH)
    vec = _fix_spec((1, NA_WIDTH))
    mat = _fix_spec((D_MODEL, D_MODEL))
    heads = pl.BlockSpec((HEADS, ROW_TILE, HEAD_DIM), lambda i: (0, i, 0))
    return pl.pallas_call(
        body, grid=(N_ROW_TILES,), in_specs=[_row_spec(D_MODEL), heads, row, vec, vec, mat],
        out_specs=[heads, row, mat, vec, vec],
        out_shape=[SDS((HEADS, SEQ, HEAD_DIM), F32), SDS((SEQ, NA_WIDTH), F32), SDS((D_MODEL, D_MODEL), F32),
                   SDS((1, NA_WIDTH), F32), SDS((1, NA_WIDTH), F32)],
        name="mix_out_bwd", compiler_params=_params(("arbitrary",)))(dmix, ona, os5, g_na, g_s5, wout)


def _me():
    x, y, c = lax.axis_index("x"), lax.axis_index("y"), lax.axis_index("c")
    return x, y, c, 4 * x + 2 * y + c


def _peer(k):
    x, y, c, _ = _me()
    px = 1 - x if (k >> 2) & 1 else x
    py = 1 - y if (k >> 1) & 1 else y
    pc = 1 - c if k & 1 else c
    return (px, py, pc), 4 * px + 2 * py + pc


ALL_PEERS = (1, 2, 3, 4, 5, 6, 7)
CHIP_PEERS = (2, 4, 6)
SIBLING = 1


def _slot8(pos):
    return 4 * pos[0] + 2 * pos[1] + pos[2]


def _slot4(pos):
    return 2 * pos[0] + pos[1]


def _exchange(arrays, gather, name, after=()):
    n, n_after = len(arrays), len(after)

    def body(*refs):
        ins, outs = refs[:n], refs[n + n_after:2 * n + n_after]
        token = refs[2 * n + n_after]
        send_sems, recv_sems, local_sems = refs[2 * n + n_after + 1:]
        token[...] = jnp.zeros_like(token)
        _, _, _, me = _me()
        started = []
        for a in range(n):
            src_mine = ins[a] if gather else ins[a].at[me]
            local = pltpu.make_async_copy(src_mine, outs[a].at[me], local_sems.at[a])
            local.start()
            started.append(local)
        sends = []
        for k in range(1, N_DEV):
            peer, peer_idx = _peer(k)
            for a in range(n):
                src = ins[a] if gather else ins[a].at[peer_idx]
                cp = pltpu.make_async_remote_copy(src_ref=src, dst_ref=outs[a].at[me], send_sem=send_sems.at[a, k - 1],
                                                  recv_sem=recv_sems.at[a, k - 1], device_id=peer, device_id_type=MESH)
                cp.start()
                sends.append(cp)
        for k in range(1, N_DEV):
            peer, peer_idx = _peer(k)
            for a in range(n):
                src = ins[a] if gather else ins[a].at[peer_idx]
                pltpu.make_async_remote_copy(src_ref=src, dst_ref=outs[a].at[peer_idx], send_sem=send_sems.at[a, k - 1],
                                             recv_sem=recv_sems.at[a, k - 1], device_id=peer, device_id_type=MESH).wait_recv()
        for cp in sends:
            cp.wait_send()
        for local in started:
            local.wait()

    hbm = pl.BlockSpec(memory_space=pltpu.HBM)
    out_shape = [SDS((N_DEV,) + tuple(a.shape), a.dtype) if gather else SDS(a.shape, a.dtype) for a in arrays]
    out = pl.pallas_call(
        body, in_specs=[hbm] * n + [pl.BlockSpec(memory_space=pl.ANY)] * n_after,
        out_specs=[hbm] * n + [pl.BlockSpec(memory_space=pltpu.VMEM)], out_shape=out_shape + [SDS((8, 128), F32)],
        scratch_shapes=[pltpu.SemaphoreType.DMA((n, N_DEV - 1)), pltpu.SemaphoreType.DMA((n, N_DEV - 1)),
                        pltpu.SemaphoreType.DMA((n,))],
        name=name)(*arrays, *after)
    return list(out[:n]), out[n]


_HBM = pl.BlockSpec(memory_space=pltpu.HBM)
_SEM = pl.BlockSpec(memory_space=pltpu.SEMAPHORE)
_EFFECT = pltpu.SideEffectType.DATAFLOW_SIDE_EFFECTING


def _land_shape(a, gather):
    return (N_DEV,) + tuple(a.shape) if gather else tuple(a.shape)


def _place_own(arrays, gather, name, slot=_slot8):
    n = len(arrays)
    me = slot(_me()[:3])

    def body(me_ref, *refs):
        for a in range(n):
            refs[n + a][...] = refs[a][...]

    def own_slot(a):
        zeros = (0,) * (a.ndim - (0 if gather else 1))
        return lambda i, me_ref: (me_ref[0],) + zeros

    def whole(a):
        return lambda i, me_ref: (0,) * a.ndim

    in_specs = [pl.BlockSpec(a.shape, whole(a)) if gather else pl.BlockSpec((None,) + a.shape[1:], own_slot(a)) for a in arrays]
    out_specs = [pl.BlockSpec((None,) + (a.shape if gather else a.shape[1:]), own_slot(a)) for a in arrays]
    return pl.pallas_call(
        body, grid_spec=pltpu.PrefetchScalarGridSpec(num_scalar_prefetch=1, grid=(1,), in_specs=in_specs, out_specs=out_specs),
        out_shape=[SDS(_land_shape(a, gather), a.dtype) for a in arrays], name=name,
        compiler_params=_params(("arbitrary",)))(me.reshape(1).astype(jnp.int32), *arrays)


def _exchange_start(arrays, lands, gather, name, peers=ALL_PEERS, slot=_slot8, own=False):
    n = len(arrays)

    def body(*refs):
        ins, lnd = refs[:n], refs[n:2 * n]
        send_sems, recv_sems = refs[2 * n], refs[2 * n + 1]
        token = refs[-1]
        me = slot(_me()[:3])
        for i, k in enumerate(peers):
            peer, _ = _peer(k)
            for a in range(n):
                src = ins[a] if gather else ins[a].at[slot(peer)]
                s = a * len(peers) + i
                pltpu.make_async_remote_copy(src_ref=src, dst_ref=lnd[a].at[me], send_sem=send_sems.at[s],
                                             recv_sem=recv_sems.at[s], device_id=peer, device_id_type=MESH).start()
        if own:
            for a in range(n):
                pltpu.make_async_copy(ins[a] if gather else ins[a].at[me], lnd[a].at[me], recv_sems.at[n * len(peers) + a]).start()
        token[...] = jnp.zeros_like(token)

    sems = pltpu.SemaphoreType.DMA((n * (len(peers) + int(own)),))
    out = pl.pallas_call(
        body, name=name, in_specs=[_HBM] * (2 * n),
        out_shape=(sems, sems) + tuple(pltpu.HBM(a.shape, a.dtype) for a in list(arrays) + list(lands)) + (SDS((8, 128), F32),),
        out_specs=(_SEM, _SEM) + (_HBM,) * (2 * n) + (pl.BlockSpec(memory_space=pltpu.VMEM),),
        input_output_aliases={i: 2 + i for i in range(2 * n)},
        compiler_params=pltpu.CompilerParams(has_side_effects=_EFFECT),
    )(*[pltpu.with_memory_space_constraint(a, pltpu.HBM) for a in list(arrays) + list(lands)])
    return out[0], out[1], list(out[2:2 + n]), list(out[2 + n:2 + 2 * n]), out[-1]


def _exchange_wait(send_sems, recv_sems, arrays, lands, after, gather, name, peers=ALL_PEERS, slot=_slot8, own=False):
    n = len(arrays)

    def body(*refs):
        ins, lnd = refs[:n], refs[n:2 * n]
        send_sems, recv_sems = refs[2 * n], refs[2 * n + 1]
        if own:
            me = slot(_me()[:3])
            for a in range(n):
                pltpu.make_async_copy(ins[a] if gather else ins[a].at[me], lnd[a].at[me], recv_sems.at[n * len(peers) + a]).wait()
        for i, k in enumerate(peers):
            peer, _ = _peer(k)
            for a in range(n):
                src = ins[a] if gather else ins[a].at[slot(peer)]
                s = a * len(peers) + i
                cp = pltpu.make_async_remote_copy(src_ref=src, dst_ref=lnd[a].at[slot(peer)], send_sem=send_sems.at[s],
                                                  recv_sem=recv_sems.at[s], device_id=peer, device_id_type=MESH)
                cp.wait_send()
                cp.wait_recv()

        refs[-1][...] = jnp.zeros_like(refs[-1])

    after = list(after) if isinstance(after, (list, tuple)) else [after]
    out = pl.pallas_call(
        body, name=name, in_specs=[_HBM] * (2 * n) + [_SEM, _SEM] + [pl.BlockSpec(memory_space=pl.ANY)] * len(after),
        out_shape=tuple(pltpu.HBM(a.shape, a.dtype) for a in list(arrays) + list(lands)) + (SDS((8, 128), F32),),
        out_specs=(_HBM,) * (2 * n) + (pl.BlockSpec(memory_space=pltpu.VMEM),), input_output_aliases={i: i for i in range(2 * n)},
        compiler_params=pltpu.CompilerParams(has_side_effects=_EFFECT),
    )(*arrays, *lands, send_sems, recv_sems, *after)
    return list(out[n:2 * n]), out[-1]


def _forward_sibling(lands, name):
    n = len(lands)

    def body(*refs):
        outs = refs[n:2 * n]
        send_sems, recv_sems = refs[2 * n:]
        x, y, c, _ = _me()
        sends = []
        for i, k in enumerate(CHIP_PEERS):
            peer, _ = _peer(k)
            for a in range(n):
                rows = outs[a].at[_slot8(peer)]
                cp = pltpu.make_async_remote_copy(src_ref=rows, dst_ref=rows, send_sem=send_sems.at[a, i], recv_sem=recv_sems.at[a, i],
                                                  device_id=(x, y, 1 - c), device_id_type=MESH)
                cp.start()
                sends.append(cp)
        for i, k in enumerate(CHIP_PEERS):
            (px, py, pc), _ = _peer(k)
            for a in range(n):
                rows = outs[a].at[_slot8((px, py, 1 - pc))]
                pltpu.make_async_remote_copy(src_ref=rows, dst_ref=rows, send_sem=send_sems.at[a, i], recv_sem=recv_sems.at[a, i],
                                             device_id=(x, y, 1 - c), device_id_type=MESH).wait_recv()
        for cp in sends:
            cp.wait_send()

    return pl.pallas_call(
        body, in_specs=[_HBM] * n, out_specs=[_HBM] * n, out_shape=[SDS(a.shape, a.dtype) for a in lands],
        input_output_aliases={i: i for i in range(n)},
        scratch_shapes=[pltpu.SemaphoreType.DMA((n, len(CHIP_PEERS))), pltpu.SemaphoreType.DMA((n, len(CHIP_PEERS)))],
        name=name)(*lands)


def _swap_sibling(arrays, name, after=()):
    n, n_after = len(arrays), len(after)
    chips = N_DEV // 2

    def body(*refs):
        ins, outs = refs[:n], refs[n + n_after:2 * n + n_after]
        send_sems, recv_sems = refs[2 * n + n_after:]
        x, y, c, _ = _me()
        sends = []
        for q in range(chips):
            for a in range(n):
                cp = pltpu.make_async_remote_copy(src_ref=ins[a].at[q, 1 - c], dst_ref=outs[a].at[q], send_sem=send_sems.at[a, q],
                                                  recv_sem=recv_sems.at[a, q], device_id=(x, y, 1 - c), device_id_type=MESH)
                cp.start()
                sends.append(cp)
        for cp in sends:
            cp.wait_recv()
        for cp in sends:
            cp.wait_send()

    return pl.pallas_call(
        body, in_specs=[_HBM] * n + [pl.BlockSpec(memory_space=pl.ANY)] * n_after, out_specs=[_HBM] * n,
        out_shape=[SDS((chips,) + a.shape[2:], a.dtype) for a in arrays],
        scratch_shapes=[pltpu.SemaphoreType.DMA((n, chips)), pltpu.SemaphoreType.DMA((n, chips))], name=name)(*arrays, *after)


def _sum_pairs(mine, theirs, name):
    chips, _, rows, cols = mine.shape
    c = lax.axis_index("c")

    def body(c_ref, a_ref, b_ref, o_ref):
        o_ref[...] = (a_ref[...].astype(F32) + b_ref[...].astype(F32)).astype(o_ref.dtype)

    return pl.pallas_call(
        body, grid_spec=pltpu.PrefetchScalarGridSpec(
            num_scalar_prefetch=1, grid=(chips,),
            in_specs=[pl.BlockSpec((None, None, rows, cols), lambda q, c_ref: (q, c_ref[0], 0, 0)),
                      pl.BlockSpec((None, rows, cols), lambda q, c_ref: (q, 0, 0))],
            out_specs=pl.BlockSpec((None, rows, cols), lambda q, c_ref: (q, 0, 0))),
        out_shape=SDS((chips, rows, cols), mine.dtype), name=name,
        compiler_params=_params(("parallel",)))(c.reshape(1).astype(jnp.int32), mine, theirs)


def _adamw_math(w, g, m, v):
    m = ADAM_B1 * m + (1.0 - ADAM_B1) * g
    v = ADAM_B2 * v + (1.0 - ADAM_B2) * (g * g)
    m_hat = m / (1.0 - ADAM_B1 ** ADAM_STEP)
    v_hat = v / (1.0 - ADAM_B2 ** ADAM_STEP)
    delta = -ADAM_LR * (m_hat / (jnp.sqrt(v_hat) + ADAM_EPS) + ADAM_WD * w)
    return delta, m, v


def _adamw(w, m, v, pieces, name):
    rows, cols = w.shape[-2:]
    lead = w.ndim - 2
    tile = rows
    for cand in (256, 176, 128, 64, 16):
        if rows > cand and rows % cand == 0:
            tile = cand
            break

    def body(w_ref, m_ref, v_ref, p_ref, g_ref, d_ref, mo_ref, vo_ref):
        g = _sum_pieces(p_ref)
        g_ref[...] = g
        d_ref[...], mo_ref[...], vo_ref[...] = _adamw_math(w_ref[...], g, m_ref[...], v_ref[...])

    blk = pl.BlockSpec((None,) * lead + (tile, cols), lambda i: (0,) * lead + (i, 0))
    return pl.pallas_call(
        body, grid=(rows // tile,), in_specs=[blk, blk, blk, pl.BlockSpec((pieces.shape[0], tile, cols), lambda i: (0, i, 0))],
        out_specs=[blk] * 4, out_shape=[SDS(w.shape, F32)] * 4, name=name,
        compiler_params=_params(("parallel",)))(w, m, v, pieces)


def _sum_pieces(p_ref):
    g = p_ref[0].astype(F32)
    for p in range(1, p_ref.shape[0]):
        g = g + p_ref[p].astype(F32)
    return g


def _adamw_s5_mat(w, m, v, g, name):
    _, ndir, groups, b, c = w.shape
    per_dir = groups // 8

    def body(w_ref, m_ref, v_ref, g_ref, d_ref, mo_ref, vo_ref):
        d_ref[...], mo_ref[...], vo_ref[...] = _adamw_math(w_ref[...], g_ref[...], m_ref[...], v_ref[...])

    blk = pl.BlockSpec((None, None, 8, b, c), lambda i: (0, i // per_dir, i % per_dir, 0, 0))
    return pl.pallas_call(
        body, grid=(ndir * per_dir,), in_specs=[blk] * 4, out_specs=[blk] * 3, out_shape=[SDS(w.shape, F32)] * 3, name=name,
        compiler_params=_params(("parallel",)))(w, m, v, g)


VEC_ROWS = ['ffn1_pre_g', 'ffn1_post_g', 'mix_pre_g', 'mix_post_g', 'ffn2_pre_g', 'ffn2_post_g', 'final_g',
            ('na_out_g', 's5_out_g'), ('s5_d', 's5_b_glu')]
VEC_NAMES = [n for row in VEC_ROWS for n in ((row,) if isinstance(row, str) else row)]
VEC_PACK_ROWS = 16
LOSS_ROW = len(VEC_ROWS)


def _pack_vectors(grads, loss8):
    def body(*refs):
        o_ref = refs[-1]
        o_ref[...] = jnp.zeros_like(o_ref)
        o_ref[LOSS_ROW:LOSS_ROW + 1, 0:128] = refs[-2][0:1, :]
        k = 0
        for i, row in enumerate(VEC_ROWS):
            if isinstance(row, str):
                o_ref[i:i + 1, :] = refs[k][...]
                k += 1
            else:
                o_ref[i:i + 1, 0:NA_WIDTH] = refs[k][...]
                o_ref[i:i + 1, NA_WIDTH:] = refs[k + 1][...]
                k += 2

    return pl.pallas_call(body, out_shape=SDS((VEC_PACK_ROWS, D_MODEL), F32), name="pack_vectors",
                          compiler_params=_params())(*[grads[n] for n in VEC_NAMES], loss8)


def _sum8(pieces, name):
    def body(p_ref, o_ref):
        o_ref[...] = _sum_pieces(p_ref)

    return pl.pallas_call(body, out_shape=SDS(pieces.shape[1:], F32), name=name, compiler_params=_params())(pieces)


def _adamw_small(packed8, vec_wmv, others):
    n_vec, n_oth = len(VEC_NAMES), len(others)

    def body(*refs):
        p_ref = refs[0]
        ins = refs[1:1 + 3 * n_vec + 4 * n_oth]
        outs = refs[1 + 3 * n_vec + 4 * n_oth:]
        gsum = _sum_pieces(p_ref)
        outs[-1][...] = gsum[LOSS_ROW:LOSS_ROW + 1, 0:128]
        k = 0
        for i, row in enumerate(VEC_ROWS):
            parts = [(row, gsum[i:i + 1, :])] if isinstance(row, str) else \
                [(row[0], gsum[i:i + 1, 0:NA_WIDTH]), (row[1], gsum[i:i + 1, NA_WIDTH:])]
            for _, g in parts:
                w_ref, m_ref, v_ref = ins[3 * k:3 * k + 3]
                outs[4 * k][...] = g
                outs[4 * k + 1][...], outs[4 * k + 2][...], outs[4 * k + 3][...] = _adamw_math(w_ref[...], g, m_ref[...], v_ref[...])
                k += 1
        for j in range(n_oth):
            w_ref, m_ref, v_ref, g_ref = ins[3 * n_vec + 4 * j:3 * n_vec + 4 * j + 4]
            g = _sum_pieces(g_ref)
            g = g[tuple(slice(0, s) for s in w_ref.shape[1:])].reshape(w_ref.shape)
            o = outs[4 * (n_vec + j):4 * (n_vec + j) + 4]
            o[0][...] = g
            o[1][...], o[2][...], o[3][...] = _adamw_math(w_ref[...], g, m_ref[...], v_ref[...])

    args, out_shape = [packed8], []
    for w, m, v in vec_wmv:
        args += [w, m, v]
        out_shape += [SDS(w.shape, F32)] * 4
    for w, m, v, g in others:
        args += [w, m, v, g]
        out_shape += [SDS(w.shape, F32)] * 4
    out_shape += [SDS((1, 128), F32)]
    return pl.pallas_call(body, out_shape=out_shape, name="adamw_small", compiler_params=_params())(*args)


def _perm_rows(x):
    return x.reshape(SCAN_BLOCKS, SCAN_T, x.shape[-1]).transpose(1, 0, 2).reshape(SEQ, x.shape[-1])


def _unperm_rows(x):
    return x.reshape(SCAN_T, SCAN_BLOCKS, x.shape[-1]).transpose(1, 0, 2).reshape(SEQ, x.shape[-1])


def _block_diag(x):
    eye = np.eye(8, dtype=bool)[None, None, :, None, :, None]
    full = jnp.where(eye, x[:, :, :, :, None, :], 0.0)
    return full.reshape(2, S5_CHUNKS, 8 * x.shape[3], 8 * x.shape[4])


STORED_SWAPPED = {"ffn1_w_gate": (1, 2), "ffn1_w_up": (1, 2), "ffn2_w_gate": (1, 2), "ffn2_w_up": (1, 2),
                  "s5_b_re": (3, 4), "s5_b_im": (3, 4)}


def _stored(name, x):
    return jnp.swapaxes(x, *STORED_SWAPPED[name]) if name in STORED_SWAPPED else x


def _dep(x, token):
    return x if token is None else x + token


def _local_step(x, target, get_w, small, emit):
    bias = _rpb_expand(small["na_rpb"][0])
    lr = small["s5_lam_re"].reshape(64, S5_STATE)
    li = small["s5_lam_im"].reshape(64, S5_STATE)
    logdt = small["s5_log_dt"].reshape(64, 1)
    b_t = [_stored(n, small[n]).reshape(64, S5_GROUP, S5_STATE) for n in ("s5_b_re", "s5_b_im")]
    lbr, lbi, bbr, bbi = _s5_prep(lr, li, logdt, b_t[0], b_t[1])
    are = lbr.reshape(2, S5_CHUNKS, 1, ST_W)
    aim = lbi.reshape(2, S5_CHUNKS, 1, ST_W)
    bre = _block_diag(bbr.reshape(2, S5_CHUNKS, 8, S5_GROUP, S5_STATE)).astype(BF16)
    bim = _block_diag(bbi.reshape(2, S5_CHUNKS, 8, S5_GROUP, S5_STATE)).astype(BF16)
    c_t = [small[n].reshape(2, S5_CHUNKS, 8, S5_GROUP, S5_STATE).transpose(0, 1, 2, 4, 3) for n in ("s5_c_re", "s5_c_im")]
    cre = _block_diag(c_t[0]).astype(BF16)
    cim = _block_diag(c_t[1]).astype(BF16)
    tgt = jnp.concatenate([jnp.zeros((N_META, D_MODEL), F32), target], axis=0)

    h0, a1 = _embed_prenorm(get_w("meta", None)["meta_tokens"], x, small["ffn1_pre_g"])
    wts = dict(get_w("ffn1", [bias, are, aim, bre, bim, cre, cim, tgt, a1]))
    gate1, up1, f1 = _ffn_fwd(a1, wts["ffn1_w_gate"], wts["ffn1_w_up"], wts["ffn1_w_down"], "ffn1_fwd",
                              after=wts.get("tokens", ()))
    h1, a2 = _post_pre(f1, h0, small["ffn1_post_g"], small["mix_pre_g"], 0.5, "post_pre1")
    wts.update(get_w("w_in", a2))
    qkv = _proj_heads(a2, wts["w_in"])
    u = _proj_u(a2, wts["w_in"])
    ona = _na_fwd(qkv, bias)
    u_p = _perm_rows(u)
    sr, si, y2 = _s5_scan_fwd(u_p, bre, bim, are, aim, cre, cim)
    wts.update(get_w("mix", y2))
    os5_p, ypre_p = _s5_glu_fwd(u_p, y2, small["s5_d"], wts["s5_w_glu"], small["s5_b_glu"])
    os5 = _unperm_rows(os5_p)

    mix = _mix_out_fwd(ona, os5, small["na_out_g"], small["s5_out_g"], wts["w_out"])
    h2, a3 = _post_pre(mix, h1, small["mix_post_g"], small["ffn2_pre_g"], 1.0, "post_pre2")
    wts.update(get_w("ffn2", a3))
    gate2, up2, f2 = _ffn_fwd(a3, wts["ffn2_w_gate"], wts["ffn2_w_up"], wts["ffn2_w_down"], "ffn2_fwd")
    loss8, dh3, df2, g_final, g_ffn2_post = _final_loss(f2, h2, small["ffn2_post_g"], small["final_g"], tgt)

    da3, dwg2, dwu2, dwd2 = _ffn_bwd(df2, a3, gate2, up2, wts["ffn2_w_gate"], wts["ffn2_w_up"], wts["ffn2_w_down"], "ffn2_bwd")
    tok = emit("ffn2", {"ffn2_w_gate": dwg2, "ffn2_w_up": dwu2, "ffn2_w_down": dwd2})
    dh2, dmix, g_ffn2_pre, g_mix_post = _bwd_pre_post(da3, h2, _dep(small["ffn2_pre_g"], tok), dh3, mix, small["mix_post_g"], 1.0,
                                                      "bwd_pre_post2")
    dona, dos5, dwout, g_na_out, g_s5_out = _mix_out_bwd(dmix, ona, os5, small["na_out_g"], small["s5_out_g"], wts["w_out"])

    dypre_p, du_skip_p, dwglu, g_b_glu, g_s5_d = _s5_glu_bwd(_perm_rows(dos5), ypre_p, u_p, small["s5_d"], wts["s5_w_glu"],
                                                             small["s5_b_glu"])
    tok = emit("mix", {"s5_w_glu": dwglu.reshape(N_DEV, S5_WIDTH // N_DEV, S5_WIDTH).astype(BF16),
                       "w_out": dwout.reshape(N_DEV, D_MODEL // N_DEV, D_MODEL).astype(BF16)})
    du_p, dbr, dbi, dcr, dci, dar, dai = _s5_scan_bwd(dypre_p, du_skip_p, u_p, sr, si, bre, bim, _dep(are, tok), aim, cre, cim)
    du = _unperm_rows(du_p)
    per_group = (2 * S5_GROUPS, S5_GROUP, S5_STATE)
    g_lr, g_li, g_dt, g_br, g_bi = _s5_prep_bwd(lr, li, logdt, b_t[0], b_t[1], dar.reshape(64, S5_STATE),
                                                dai.reshape(64, S5_STATE), dbr.reshape(per_group), dbi.reshape(per_group))
    g_c = [dcr.reshape(per_group), dci.reshape(per_group)]

    dq, dk, dv, dbias = _na_bwd(qkv, bias, dona)
    g_rpb = _rpb_reduce(dbias)
    dense = jnp.stack([g.reshape(2 * S5_GROUPS, S5_STATE * S5_GROUP) for g in (g_br, g_bi, *g_c)])
    tok = emit("small", {"dense": dense, "na_rpb": g_rpb,
                         "s5_lam_re": g_lr.reshape(2, S5_GROUPS, S5_STATE), "s5_lam_im": g_li.reshape(2, S5_GROUPS, S5_STATE),
                         "s5_log_dt": g_dt.reshape(2, S5_GROUPS)})
    da2, dwin = _proj_bwd(dq, dk, dv, du, a2, wts["w_in"])
    tok2 = emit("w_in", {"w_in": dwin})
    tok = tok if tok2 is None else tok + tok2
    dh1, df1, g_mix_pre, g_ffn1_post = _bwd_pre_post(da2, h1, _dep(small["mix_pre_g"], tok), dh2, f1, small["ffn1_post_g"], 0.5,
                                                     "bwd_pre_post1")
    da1, dwg1, dwu1, dwd1 = _ffn_bwd(df1, a1, gate1, up1, wts["ffn1_w_gate"], wts["ffn1_w_up"], wts["ffn1_w_down"], "ffn1_bwd")
    grad_x, grad_meta, g_ffn1_pre = _bwd_embed(da1, h0, small["ffn1_pre_g"], dh1)
    vec_g = {
        "ffn1_pre_g": g_ffn1_pre, "ffn1_post_g": g_ffn1_post, "mix_pre_g": g_mix_pre, "s5_d": g_s5_d, "s5_b_glu": g_b_glu,
        "na_out_g": g_na_out, "s5_out_g": g_s5_out, "mix_post_g": g_mix_post,
        "ffn2_pre_g": g_ffn2_pre, "ffn2_post_g": g_ffn2_post, "final_g": g_final,
    }
    emit("vec", {"packed": _pack_vectors(vec_g, loss8), "meta_tokens": grad_meta})
    emit("ffn1", {"ffn1_w_gate": dwg1, "ffn1_w_up": dwu1, "ffn1_w_down": dwd1})
    return grad_x


WEIGHT_NAMES = ['meta_tokens', 'ffn1_pre_g', 'ffn1_post_g', 'ffn1_w_gate', 'ffn1_w_up', 'ffn1_w_down', 'mix_pre_g', 'w_in',
                'na_rpb', 's5_lam_re', 's5_lam_im', 's5_log_dt', 's5_b_re', 's5_b_im', 's5_c_re', 's5_c_im', 's5_d',
                's5_w_glu', 's5_b_glu', 'na_out_g', 's5_out_g', 'w_out', 'mix_post_g', 'ffn2_pre_g', 'ffn2_post_g',
                'ffn2_w_gate', 'ffn2_w_up', 'ffn2_w_down', 'final_g']
BIG_NAMES = ['ffn1_w_gate', 'ffn1_w_up', 'ffn1_w_down', 'w_in', 's5_w_glu', 'w_out', 'ffn2_w_gate', 'ffn2_w_up', 'ffn2_w_down']
SMALL_NAMES = [n for n in WEIGHT_NAMES if n not in BIG_NAMES and n != 'meta_tokens']
WHOLE_NAMES = ['na_rpb', 's5_lam_re', 's5_lam_im', 's5_log_dt']
LEAD_NAMES = ['s5_b_re', 's5_b_im', 's5_c_re', 's5_c_im']


def kernel(x, meta_tokens, ffn1_pre_g, ffn1_post_g, ffn1_w_gate, ffn1_w_up, ffn1_w_down, mix_pre_g, w_in, na_rpb, s5_lam_re, s5_lam_im, s5_log_dt, s5_b_re, s5_b_im, s5_c_re, s5_c_im, s5_d, s5_w_glu, s5_b_glu, na_out_g, s5_out_g, w_out, mix_post_g, ffn2_pre_g, ffn2_post_g, ffn2_w_gate, ffn2_w_up, ffn2_w_down, final_g, loss_target, m_meta_tokens, m_ffn1_pre_g, m_ffn1_post_g, m_ffn1_w_gate, m_ffn1_w_up, m_ffn1_w_down, m_mix_pre_g, m_w_in, m_na_rpb, m_s5_lam_re, m_s5_lam_im, m_s5_log_dt, m_s5_b_re, m_s5_b_im, m_s5_c_re, m_s5_c_im, m_s5_d, m_s5_w_glu, m_s5_b_glu, m_na_out_g, m_s5_out_g, m_w_out, m_mix_post_g, m_ffn2_pre_g, m_ffn2_post_g, m_ffn2_w_gate, m_ffn2_w_up, m_ffn2_w_down, m_final_g, v_meta_tokens, v_ffn1_pre_g, v_ffn1_post_g, v_ffn1_w_gate, v_ffn1_w_up, v_ffn1_w_down, v_mix_pre_g, v_w_in, v_na_rpb, v_s5_lam_re, v_s5_lam_im, v_s5_log_dt, v_s5_b_re, v_s5_b_im, v_s5_c_re, v_s5_c_im, v_s5_d, v_s5_w_glu, v_s5_b_glu, v_na_out_g, v_s5_out_g, v_w_out, v_mix_post_g, v_ffn2_pre_g, v_ffn2_post_g, v_ffn2_w_gate, v_ffn2_w_up, v_ffn2_w_down, v_final_g):
    args = dict(locals())
    w = {n: args[n] for n in WEIGHT_NAMES}
    m = {n: args["m_" + n] for n in WEIGHT_NAMES}
    v = {n: args["v_" + n] for n in WEIGHT_NAMES}

    small = {n: w[n] for n in SMALL_NAMES}

    pending = {}

    def start(group, names, arrays, gather, peers=ALL_PEERS, slot=_slot8, own_in_flight=True):
        if own_in_flight:
            n_slots = N_DEV if slot is _slot8 else N_DEV // 2
            lands = [lax.empty((n_slots,) + a.shape if gather else a.shape, a.dtype) for a in arrays]
        else:
            lands = _place_own(arrays, gather, "own_" + group, slot)
        send_sems, recv_sems, arrays, lands, token = _exchange_start(arrays, lands, gather, "start_" + group, peers, slot,
                                                                     own_in_flight)
        pending[group] = (names, send_sems, recv_sems, arrays, lands, gather, peers, slot, own_in_flight)
        return token

    def finish(group, after):
        names, send_sems, recv_sems, arrays, lands, gather, peers, slot, own = pending.pop(group)
        lands, token = _exchange_wait(send_sems, recv_sems, arrays, lands, after, gather, "wait_" + group, peers, slot, own)
        return dict(zip(names, lands)), token

    first = ["ffn1_w_gate", "ffn1_w_up", "ffn1_w_down"]
    def shard(n, token=None):
        return _dep(_stored(n, w[n])[0], None if token is None else token[0, 0]).astype(BF16)

    ffn_names = ("ffn1_w_gate", "ffn1_w_up", "ffn1_w_down", "ffn2_w_gate", "ffn2_w_up", "ffn2_w_down")
    later_groups = (("w_in", ["w_in"]), ("mix", ["s5_w_glu", "w_out"]), ("ffn2", ["ffn2_w_gate", "ffn2_w_up", "ffn2_w_down"]))
    (meta_full,), token0 = _exchange([w["meta_tokens"]], True, "gather_meta")
    token1 = start("ffn1", first, [shard(n, token0) for n in first], True, (SIBLING,) + CHIP_PEERS, own_in_flight=False)
    meta_full = _dep(meta_full.transpose(1, 0, 2).reshape(N_META, D_MODEL), token1[0, 0])
    later_shards = {n: shard(n, token1) for _, names in later_groups for n in names}
    for n in ("na_rpb", "s5_lam_re"):
        small[n] = _dep(small[n], token1[0, 0])

    def get_w(group, after):
        if group == "meta":
            return {"meta_tokens": meta_full}
        if group == "ffn1":
            after = list(after) + list(later_shards.values())
        got, token = finish(group, after)
        if group == "ffn1":
            got = dict(zip(got, _forward_sibling(list(got.values()), "forward_ffn1")))
            got["tokens"] = [start(g, names + ["order"], [later_shards[n] for n in names] + [token], True) for g, names in later_groups]
        if group == "mix":
            got = {"s5_w_glu": got["s5_w_glu"].reshape(S5_WIDTH, S5_WIDTH), "w_out": got["w_out"].reshape(D_MODEL, D_MODEL)}
        return {n: (a.reshape(D_FF, D_MODEL) if n in ffn_names else a) for n, a in got.items()}

    tokens = {}

    def emit(group, grads):
        grads = {n: (g.reshape(N_DEV, FF_SHARD, D_MODEL) if n in ffn_names else g) for n, g in grads.items()}
        if group == "ffn1":
            mine = [g.reshape((N_DEV // 2, 2) + g.shape[1:]) for g in grads.values()]
            theirs = _swap_sibling(mine, "swap_g_ffn1", after=[tokens["vec"]])
            sums = [_sum_pairs(a, b, "pair_sum_" + n) for n, a, b in zip(grads, mine, theirs)]
            tokens[group] = start("g_ffn1", list(grads), sums, False, CHIP_PEERS, _slot4)
        else:
            tokens[group] = start("g_" + group, list(grads), list(grads.values()), group in ("small", "vec"))
        return tokens[group][0, 0]

    grad_x = _local_step(x[0], loss_target[0], get_w, small, emit)
    res = {}

    def update_shard(n, pieces):
        outs = _adamw(_stored(n, w[n]), _stored(n, m[n]), _stored(n, v[n]), pieces, "adamw_" + n)
        res[n] = [_stored(n, o) for o in outs]

    late = [grad_x, tokens["ffn1"]]
    for group in ("g_ffn2", "g_mix", "g_w_in"):
        for n, pieces in finish(group, late)[0].items():
            update_shard(n, pieces)
    g8 = finish("g_small", late)[0]
    dense = _sum8(g8["dense"], "sum_dense")
    for i, n in enumerate(LEAD_NAMES):
        g = dense[i].reshape(_stored(n, w[n]).shape)
        upd = _adamw_s5_mat(_stored(n, w[n]), _stored(n, m[n]), _stored(n, v[n]), g, "adamw_" + n)
        res[n] = [_stored(n, o) for o in [g] + list(upd)]

    done = [res[n][1] for n in ("ffn2_w_gate", "ffn2_w_up", "ffn2_w_down", "w_in", "w_out", "s5_w_glu") + tuple(LEAD_NAMES)]
    got = finish("g_vec", done)[0]
    packed8, gmeta8 = got["packed"], got["meta_tokens"]
    for n, pieces in finish("g_ffn1", packed8)[0].items():
        update_shard(n, pieces)
    _, _, _, me = _me()
    update_shard("meta_tokens", lax.dynamic_slice_in_dim(gmeta8, me * (D_MODEL // N_DEV), D_MODEL // N_DEV, axis=2))

    outs = _adamw_small(packed8, [(w[n], m[n], v[n]) for n in VEC_NAMES], [(w[n], m[n], v[n], g8[n]) for n in WHOLE_NAMES])
    for i, n in enumerate(VEC_NAMES + WHOLE_NAMES):
        res[n] = list(outs[4 * i:4 * i + 4])

    out = [outs[-1][0, 0], grad_x[None]]
    for kind in range(4):
        out += [res[n][kind] for n in WEIGHT_NAMES]
    return tuple(out)
```

```python
import math

import numpy as np
import jax
import jax.numpy as jnp
from jax import lax
from jax.experimental import pallas as pl
from jax.experimental.pallas import tpu as pltpu

F32 = jnp.float32
BF16 = jnp.bfloat16
SDS = jax.ShapeDtypeStruct

D_MODEL = 1024
N_TOK = 2048
N_META = 16
SEQ = N_TOK + N_META
ROW_TILE = 688
N_ROW_TILES = SEQ // ROW_TILE
N_DEV = 8
D_FF = 2816
FF_SHARD = D_FF // N_DEV
FF_TILE = 256
IN_SHARD = 256
NA_WIDTH = 512
S5_WIDTH = 512
HEADS = 8
HEAD_DIM = 64
GRID_W = 64
GRID_ROWS = N_TOK // GRID_W
KH = 8
KW = 16
NA_RB = 4
NA_KR = KH + NA_RB - 1
NA_BLOCKS = GRID_ROWS // NA_RB
NA_QB = NA_RB * GRID_W
NA_KB = NA_KR * GRID_W
NA_TYPES = 3
S5_GROUPS = 32
S5_GROUP = 16
S5_STATE = 64
S5_CHUNKS = 4
CH_W = S5_WIDTH // S5_CHUNKS
ST_W = S5_GROUPS * S5_STATE // S5_CHUNKS
SCAN_BLOCKS = 8
SCAN_T = SEQ // SCAN_BLOCKS
RMS_EPS = 1e-6
NEG_INF = -1e30
ATT_SCALE = HEAD_DIM ** -0.5
ADAM_LR, ADAM_B1, ADAM_B2, ADAM_EPS, ADAM_WD, ADAM_STEP = 0.001, 0.9, 0.999, 1e-08, 0.01, 10
VMEM_LIMIT = 56 * 1024 * 1024
MESH = pl.DeviceIdType.MESH


def _params(sem=None):
    return pltpu.CompilerParams(dimension_semantics=sem, vmem_limit_bytes=VMEM_LIMIT)


def _dot(a, b):
    return jnp.dot(a, b, preferred_element_type=F32)


def _dot_nt(a, b):
    return lax.dot_general(a, b, (((1,), (1,)), ((), ())), preferred_element_type=F32)


def _dot_tn(a, b):
    return lax.dot_general(a, b, (((0,), (0,)), ((), ())), preferred_element_type=F32)


def _rstd(x):
    return lax.rsqrt(jnp.mean(x * x, axis=-1, keepdims=True) + RMS_EPS)


def _rms_bwd(x, r, g, dy):
    dyg = dy * g
    xr = x * r
    dx = r * (dyg - xr * jnp.mean(dyg * xr, axis=-1, keepdims=True))
    return dx, dy * xr


def _rows(i, size=ROW_TILE):
    return pl.ds(pl.multiple_of(i * size, 16), size)


def _row_spec(width):
    return pl.BlockSpec((ROW_TILE, width), lambda i: (i, 0))


def _fix_spec(shape):
    return pl.BlockSpec(shape, lambda i: (0,) * len(shape))


def _split3(x):
    hi = x.astype(BF16)
    r1 = x - hi.astype(F32)
    mid = r1.astype(BF16)
    lo = (r1 - mid.astype(F32)).astype(BF16)
    return hi, mid, lo


def _embed_prenorm(meta, x, g):
    def body(m_ref, x_ref, g_ref, h_ref, a_ref):
        h_ref[0:N_META, :] = m_ref[...]
        h_ref[N_META:, :] = x_ref[...]
        for i in range(N_ROW_TILES):
            rows = slice(i * ROW_TILE, (i + 1) * ROW_TILE)
            hv = h_ref[rows, :]
            a_ref[rows, :] = (hv * _rstd(hv) * g_ref[...]).astype(BF16)

    return pl.pallas_call(
        body, out_shape=[SDS((SEQ, D_MODEL), F32), SDS((SEQ, D_MODEL), BF16)], name="embed_prenorm",
        compiler_params=_params())(meta, x, g)


def _post_pre(f, hres, g_post, g_next, scale, name):
    def body(f_ref, h_ref, gp_ref, gn_ref, ho_ref, a_ref):
        fv = f_ref[...]
        h = h_ref[...] + scale * (fv * _rstd(fv) * gp_ref[...])
        ho_ref[...] = h
        a_ref[...] = (h * _rstd(h) * gn_ref[...]).astype(BF16)

    return pl.pallas_call(
        body, grid=(N_ROW_TILES,),
        in_specs=[_row_spec(D_MODEL), _row_spec(D_MODEL), _fix_spec((1, D_MODEL)), _fix_spec((1, D_MODEL))],
        out_specs=[_row_spec(D_MODEL), _row_spec(D_MODEL)],
        out_shape=[SDS((SEQ, D_MODEL), F32), SDS((SEQ, D_MODEL), BF16)], name=name,
        compiler_params=_params(("parallel",)))(f, hres, g_post, g_next)


def _final_loss(f2, h2, g_post, g_final, target):
    def body(f_ref, h_ref, gp_ref, gf_ref, t_ref, loss_ref, dh_ref, df_ref, dgf_ref, dgp_ref):
        i = pl.program_id(0)
        fv = f_ref[...]
        r1 = _rstd(fv)
        gp = gp_ref[...]
        h3 = h_ref[...] + 0.5 * (fv * r1 * gp)
        r2 = _rstd(h3)
        gf = gf_ref[...]
        y = h3 * r2 * gf
        row = lax.broadcasted_iota(jnp.int32, (ROW_TILE, 1), 0) + i * ROW_TILE
        err = jnp.where(row >= N_META, y - t_ref[...], 0.0)
        part = 0.5 * jnp.sum(jnp.mean(err * err, axis=-1, keepdims=True))
        dy = err * (1.0 / D_MODEL)
        dh3, dgf = _rms_bwd(h3, r2, gf, dy)
        dh_ref[...] = dh3
        df, dgp = _rms_bwd(fv, r1, gp, 0.5 * dh3)
        df_ref[...] = df.astype(BF16)

        @pl.when(i == 0)
        def _():
            loss_ref[...] = jnp.zeros_like(loss_ref)
            dgf_ref[...] = jnp.zeros_like(dgf_ref)
            dgp_ref[...] = jnp.zeros_like(dgp_ref)

        loss_ref[...] += part
        dgf_ref[...] += jnp.sum(dgf, axis=0, keepdims=True)
        dgp_ref[...] += jnp.sum(dgp, axis=0, keepdims=True)

    gain = _fix_spec((1, D_MODEL))
    return pl.pallas_call(
        body, grid=(N_ROW_TILES,),
        in_specs=[_row_spec(D_MODEL), _row_spec(D_MODEL), gain, gain, _row_spec(D_MODEL)],
        out_specs=[_fix_spec((8, 128)), _row_spec(D_MODEL), _row_spec(D_MODEL), gain, gain],
        out_shape=[SDS((8, 128), F32), SDS((SEQ, D_MODEL), F32), SDS((SEQ, D_MODEL), BF16),
                   SDS((1, D_MODEL), F32), SDS((1, D_MODEL), F32)],
        name="final_loss", compiler_params=_params(("arbitrary",)))(f2, h2, g_post, g_final, target)


def _bwd_pre_post(da, h, g_pre, dh_res, fprev, g_post, scale, name):
    def body(da_ref, h_ref, gpre_ref, dhr_ref, f_ref, gpost_ref, dh_ref, df_ref, dgpre_ref, dgpost_ref):
        i = pl.program_id(0)
        hv = h_ref[...]
        dxa, dgpre = _rms_bwd(hv, _rstd(hv), gpre_ref[...], da_ref[...])
        dh = dhr_ref[...] + dxa
        dh_ref[...] = dh
        fv = f_ref[...]
        df, dgpost = _rms_bwd(fv, _rstd(fv), gpost_ref[...], scale * dh)
        df_ref[...] = df.astype(BF16)

        @pl.when(i == 0)
        def _():
            dgpre_ref[...] = jnp.zeros_like(dgpre_ref)
            dgpost_ref[...] = jnp.zeros_like(dgpost_ref)

        dgpre_ref[...] += jnp.sum(dgpre, axis=0, keepdims=True)
        dgpost_ref[...] += jnp.sum(dgpost, axis=0, keepdims=True)

    gain = _fix_spec((1, D_MODEL))
    row = _row_spec(D_MODEL)
    return pl.pallas_call(
        body, grid=(N_ROW_TILES,), in_specs=[row, row, gain, row, row, gain],
        out_specs=[row, row, gain, gain],
        out_shape=[SDS((SEQ, D_MODEL), F32), SDS((SEQ, D_MODEL), BF16), SDS((1, D_MODEL), F32), SDS((1, D_MODEL), F32)],
        name=name, compiler_params=_params(("arbitrary",)))(da, h, g_pre, dh_res, fprev, g_post)


def _bwd_embed(da, h, g_pre, dh_res):
    def body(da_ref, h_ref, gpre_ref, dhr_ref, gx_ref, gm_ref, dgpre_ref):
        total = jnp.zeros((1, D_MODEL), F32)
        for i in range(N_ROW_TILES):
            rows = slice(i * ROW_TILE, (i + 1) * ROW_TILE)
            hv = h_ref[rows, :]
            dxa, dgpre = _rms_bwd(hv, _rstd(hv), gpre_ref[...], da_ref[rows, :])
            dh = dhr_ref[rows, :] + dxa
            total = total + jnp.sum(dgpre, axis=0, keepdims=True)
            if i == 0:
                gm_ref[...] = dh[0:N_META, :]
                gx_ref[0:ROW_TILE - N_META, :] = dh[N_META:, :]
            else:
                gx_ref[i * ROW_TILE - N_META:(i + 1) * ROW_TILE - N_META, :] = dh
        dgpre_ref[...] = total

    return pl.pallas_call(
        body, out_shape=[SDS((N_TOK, D_MODEL), F32), SDS((N_META, D_MODEL), F32), SDS((1, D_MODEL), F32)],
        name="bwd_embed", compiler_params=_params())(da, h, g_pre, dh_res)


def _ffn_fwd(a, wg, wu, wd, name, after=()):
    def body(a_ref, wg_ref, wu_ref, wd_ref, *rest):
        gate_ref, up_ref, f_ref = rest[len(after):]
        j = pl.program_id(0)

        def tile(i, carry):
            rows = _rows(i)
            at = a_ref[rows, :]
            gate = _dot_nt(at, wg_ref[...])
            up = _dot_nt(at, wu_ref[...])
            gate_ref[rows, :] = gate.astype(BF16)
            up_ref[rows, :] = up.astype(BF16)
            act = (gate * jax.nn.sigmoid(gate) * up).astype(BF16)
            contrib = _dot(act, wd_ref[...])

            @pl.when(j == 0)
            def _():
                f_ref[rows, :] = contrib

            @pl.when(j != 0)
            def _():
                f_ref[rows, :] += contrib

            return carry

        lax.fori_loop(0, N_ROW_TILES, tile, 0)

    wtile = pl.BlockSpec((FF_TILE, D_MODEL), lambda j: (j, 0))
    hid = pl.BlockSpec((SEQ, FF_TILE), lambda j: (0, j))
    full = pl.BlockSpec((SEQ, D_MODEL), lambda j: (0, 0))
    return pl.pallas_call(
        body, grid=(D_FF // FF_TILE,), in_specs=[full, wtile, wtile, wtile] + [pl.BlockSpec(memory_space=pl.ANY)] * len(after),
        out_specs=[hid, hid, full],
        out_shape=[SDS((SEQ, D_FF), BF16), SDS((SEQ, D_FF), BF16), SDS((SEQ, D_MODEL), F32)],
        name=name, compiler_params=_params(("arbitrary",)))(a, wg, wu, wd, *after)


def _ffn_bwd(df, a, gate, up, wg, wu, wd, name):
    def body(df_ref, a_ref, gate_ref, up_ref, wg_ref, wu_ref, wd_ref, da_ref, dwg_ref, dwu_ref, dwd_ref,
             acc_g, acc_u, acc_d):
        j = pl.program_id(0)

        def tile(i, carry):
            rows = _rows(i)
            dft = df_ref[rows, :]
            at = a_ref[rows, :]
            gate = gate_ref[rows, :].astype(F32)
            up = up_ref[rows, :].astype(F32)
            dact = _dot_nt(dft, wd_ref[...])
            sig = jax.nn.sigmoid(gate)
            silu = gate * sig
            dgate = (dact * up * (sig * (1.0 + gate * (1.0 - sig)))).astype(BF16)
            dup = (dact * silu).astype(BF16)
            act = (silu * up).astype(BF16)
            dwd = _dot_tn(act, dft)
            dwg = _dot_tn(dgate, at)
            dwu = _dot_tn(dup, at)
            dat = _dot(dgate, wg_ref[...]) + _dot(dup, wu_ref[...])

            @pl.when(i == 0)
            def _():
                acc_d[...] = dwd
                acc_g[...] = dwg
                acc_u[...] = dwu

            @pl.when(i != 0)
            def _():
                acc_d[...] += dwd
                acc_g[...] += dwg
                acc_u[...] += dwu

            @pl.when(j == 0)
            def _():
                da_ref[rows, :] = dat

            @pl.when(j != 0)
            def _():
                da_ref[rows, :] += dat

            return carry

        lax.fori_loop(0, N_ROW_TILES, tile, 0)
        dwg_ref[...] = acc_g[...].astype(BF16)
        dwu_ref[...] = acc_u[...].astype(BF16)
        dwd_ref[...] = acc_d[...].astype(BF16)

    wtile = pl.BlockSpec((FF_TILE, D_MODEL), lambda j: (j, 0))
    hid = pl.BlockSpec((SEQ, FF_TILE), lambda j: (0, j))
    full = pl.BlockSpec((SEQ, D_MODEL), lambda j: (0, 0))
    return pl.pallas_call(
        body, grid=(D_FF // FF_TILE,), in_specs=[full, full, hid, hid, wtile, wtile, wtile],
        out_specs=[full, wtile, wtile, wtile],
        out_shape=[SDS((SEQ, D_MODEL), F32)] + [SDS((D_FF, D_MODEL), BF16)] * 3,
        scratch_shapes=[pltpu.VMEM((FF_TILE, D_MODEL), F32)] * 3,
        name=name, compiler_params=_params(("arbitrary",)))(df, a, gate, up, wg, wu, wd)


HEADS_PER_BLOCK = IN_SHARD // HEAD_DIM
QKV_BLOCKS = 3 * NA_WIDTH // IN_SHARD


def _proj_heads(a, w):
    def body(a_ref, w_ref, o_ref):
        def tile(i, carry):
            rows = _rows(i)
            res = _dot(a_ref[rows, :], w_ref[...])
            for sub in range(HEADS_PER_BLOCK):
                o_ref[sub, rows, :] = res[:, sub * HEAD_DIM:(sub + 1) * HEAD_DIM]
            return carry

        lax.fori_loop(0, N_ROW_TILES, tile, 0)

    return pl.pallas_call(
        body, grid=(QKV_BLOCKS,),
        in_specs=[pl.BlockSpec((SEQ, D_MODEL), lambda j: (0, 0)), pl.BlockSpec((None, D_MODEL, IN_SHARD), lambda j: (j, 0, 0))],
        out_specs=pl.BlockSpec((HEADS_PER_BLOCK, SEQ, HEAD_DIM), lambda j: (j, 0, 0)),
        out_shape=SDS((3 * HEADS, SEQ, HEAD_DIM), F32), name="proj_heads",
        compiler_params=_params(("parallel",)))(a, w)


def _proj_u(a, w):
    def body(a_ref, w_ref, o_ref):
        def tile(i, carry):
            rows = _rows(i)
            o_ref[rows, :] = _dot(a_ref[rows, :], w_ref[...])
            return carry

        lax.fori_loop(0, N_ROW_TILES, tile, 0)

    return pl.pallas_call(
        body, grid=(N_DEV - QKV_BLOCKS,),
        in_specs=[pl.BlockSpec((SEQ, D_MODEL), lambda j: (0, 0)),
                  pl.BlockSpec((None, D_MODEL, IN_SHARD), lambda j: (j + QKV_BLOCKS, 0, 0))],
        out_specs=pl.BlockSpec((SEQ, IN_SHARD), lambda j: (0, j)),
        out_shape=SDS((SEQ, S5_WIDTH), F32), name="proj_u",
        compiler_params=_params(("parallel",)))(a, w)


def _proj_bwd(dq, dk, dv, du, a, w):
    def body(dq_ref, dk_ref, dv_ref, du_ref, a_ref, w_ref, da_ref, dw_ref, acc, dp_ref):
        j = pl.program_id(0)

        for which, src in enumerate((dq_ref, dk_ref, dv_ref)):
            @pl.when((j >= 2 * which) & (j < 2 * which + 2))
            def _(src=src):
                dp_ref[...] = jnp.concatenate([src[sub] for sub in range(HEADS_PER_BLOCK)], axis=-1).astype(BF16)

        @pl.when(j >= QKV_BLOCKS)
        def _():
            dp_ref[...] = du_ref[...].astype(BF16)

        def tile(i, carry):
            rows = _rows(i)
            dpt = dp_ref[rows, :]
            dw = _dot_tn(a_ref[rows, :], dpt)
            dat = _dot_nt(dpt, w_ref[...])

            @pl.when(i == 0)
            def _():
                acc[...] = dw

            @pl.when(i != 0)
            def _():
                acc[...] += dw

            @pl.when(j == 0)
            def _():
                da_ref[rows, :] = dat

            @pl.when(j != 0)
            def _():
                da_ref[rows, :] += dat

            return carry

        lax.fori_loop(0, N_ROW_TILES, tile, 0)
        dw_ref[...] = acc[...].astype(BF16)

    full = pl.BlockSpec((SEQ, D_MODEL), lambda j: (0, 0))
    wspec = pl.BlockSpec((None, D_MODEL, IN_SHARD), lambda j: (j, 0, 0))

    def heads(which):
        return pl.BlockSpec((HEADS_PER_BLOCK, SEQ, HEAD_DIM), lambda j: (jnp.clip(j - 2 * which, 0, 1), 0, 0))

    return pl.pallas_call(
        body, grid=(N_DEV,),
        in_specs=[heads(0), heads(1), heads(2),
                  pl.BlockSpec((SEQ, IN_SHARD), lambda j: (0, jnp.clip(j - QKV_BLOCKS, 0, 1))), full, wspec],
        out_specs=[full, wspec],
        out_shape=[SDS((SEQ, D_MODEL), F32), SDS((N_DEV, D_MODEL, IN_SHARD), BF16)],
        scratch_shapes=[pltpu.VMEM((D_MODEL, IN_SHARD), F32), pltpu.VMEM((SEQ, IN_SHARD), BF16)],
        name="proj_bwd", compiler_params=_params(("arbitrary",)))(dq, dk, dv, du, a, w)


def _na_consts():
    c = np.arange(GRID_W)
    col_start = np.clip(c - KW // 2, 0, GRID_W - KW)
    col_in = (c[None, :] >= col_start[:, None]) & (c[None, :] < col_start[:, None] + KW)
    dc = np.clip(c[None, :] - c[:, None] + KW - 1, 0, 2 * KW - 2)
    onehot = np.zeros((128, GRID_W * GRID_W), np.float32)
    qq, kk = np.meshgrid(c, c, indexing="ij")
    onehot[dc[col_in], (qq * GRID_W + kk)[col_in]] = 1.0
    negmask = np.where(col_in, 0.0, NEG_INF).astype(np.float32).reshape(1, -1)
    return onehot, negmask


def _na_pair(block_type, a, b):
    if block_type == 0:
        return b - a + KH - 1 if b < KH else None
    if block_type == 1:
        return b - a + KH // 2 - 1 if a <= b < a + KH else None
    return b - a if b >= NA_KR - KH else None


def _rpb_expand(rpb):
    onehot, negmask = _na_consts()
    rows = HEADS * (2 * KH - 1)
    rpb_pad = jnp.pad(rpb.reshape(rows, 2 * KW - 1), ((0, 128 - rows), (0, 128 - (2 * KW - 1))))

    def body(r_ref, oh_ref, m_ref, t_ref):
        hi, mid, lo = _split3(r_ref[...])
        oh = oh_ref[...]
        t_ref[...] = _dot(hi, oh) + _dot(mid, oh) + _dot(lo, oh) + m_ref[...]

    table = pl.pallas_call(body, out_shape=SDS((128, GRID_W * GRID_W), F32), name="rpb_expand",
                           compiler_params=_params())(rpb_pad, jnp.asarray(onehot, BF16), jnp.asarray(negmask))
    return table[:rows].reshape(HEADS, 2 * KH - 1, GRID_W, GRID_W)


def _rpb_reduce(dslabs):
    onehot, _ = _na_consts()
    rows = HEADS * (2 * KH - 1)

    def body(x_ref, oht_ref, o_ref):
        hi, mid, lo = _split3(x_ref[...])
        oht = oht_ref[...]
        o_ref[...] = _dot(hi, oht) + _dot(mid, oht) + _dot(lo, oht)

    out = pl.pallas_call(body, out_shape=SDS((rows, 128), F32), name="rpb_reduce", compiler_params=_params())(
        dslabs.reshape(rows, GRID_W * GRID_W), jnp.asarray(onehot.T, BF16))
    return out.reshape(HEADS, 2 * KH - 1, 128)


def _bias_tiles(slab_ref, tile_ref):
    tile_ref[...] = jnp.full(tile_ref.shape, NEG_INF, F32)
    for t in range(NA_TYPES):
        for a in range(NA_RB):
            for b in range(NA_KR):
                dr = _na_pair(t, a, b)
                if dr is not None:
                    tile_ref[t, a * GRID_W:(a + 1) * GRID_W, b * GRID_W:(b + 1) * GRID_W] = slab_ref[dr]


def _bias_tiles_bwd(dtile_ref, dslab_ref):
    acc = {}
    for t in range(NA_TYPES):
        for a in range(NA_RB):
            for b in range(NA_KR):
                dr = _na_pair(t, a, b)
                if dr is not None:
                    part = dtile_ref[t, a * GRID_W:(a + 1) * GRID_W, b * GRID_W:(b + 1) * GRID_W]
                    acc[dr] = part if dr not in acc else acc[dr] + part
    for dr in range(2 * KH - 1):
        dslab_ref[dr] = acc[dr]


def _block_geometry(g):
    start = jnp.clip(g * NA_RB - KH // 2, 0, GRID_ROWS - NA_KR)
    block_type = jnp.where(g == 0, 0, jnp.where(g == NA_BLOCKS - 1, 2, 1))
    q0 = pl.multiple_of(N_META + g * NA_QB, 16)
    k0 = pl.multiple_of(N_META + start * GRID_W, 16)
    return block_type, q0, k0


def _na_probs(q, kk, km, bias):
    s = _dot_nt(q, kk) * ATT_SCALE + bias
    sm = _dot_nt(q, km) * ATT_SCALE
    m = jnp.maximum(jnp.max(s, axis=-1, keepdims=True), jnp.max(sm, axis=-1, keepdims=True))
    p = jnp.exp(s - m)
    pm = jnp.exp(sm - m)
    inv = 1.0 / (jnp.sum(p, axis=-1, keepdims=True) + jnp.sum(pm, axis=-1, keepdims=True))
    return p * inv, pm * inv


def _meta_probs(qm, km):
    s = _dot_nt(qm, km) * ATT_SCALE
    p = jnp.exp(s - jnp.max(s, axis=-1, keepdims=True))
    return p / jnp.sum(p, axis=-1, keepdims=True)


def _qkv_specs():
    return [pl.BlockSpec((None, SEQ, HEAD_DIM), lambda h, which=which: (h + which * HEADS, 0, 0)) for which in range(3)]


def _na_fwd(qkv, bias):
    def body(q_ref, k_ref, v_ref, slab_ref, o_ref, b_ref):
        _bias_tiles(slab_ref, b_ref)
        km = k_ref[0:N_META, :].astype(BF16)
        vm = v_ref[0:N_META, :].astype(BF16)
        pmm = _meta_probs(q_ref[0:N_META, :].astype(BF16), km)
        o_ref[0:N_META, :] = _dot(pmm.astype(BF16), vm)

        def block(g, carry):
            block_type, q0, k0 = _block_geometry(g)
            qb = q_ref[pl.ds(q0, NA_QB), :].astype(BF16)
            kk = k_ref[pl.ds(k0, NA_KB), :].astype(BF16)
            vv = v_ref[pl.ds(k0, NA_KB), :].astype(BF16)
            p, pm = _na_probs(qb, kk, km, b_ref[block_type])
            o_ref[pl.ds(q0, NA_QB), :] = _dot(p.astype(BF16), vv) + _dot(pm.astype(BF16), vm)
            return carry

        lax.fori_loop(0, NA_BLOCKS, block, 0)

    head = pl.BlockSpec((None, SEQ, HEAD_DIM), lambda h: (h, 0, 0))
    return pl.pallas_call(
        body, grid=(HEADS,), in_specs=_qkv_specs() + [pl.BlockSpec((None, 2 * KH - 1, GRID_W, GRID_W), lambda h: (h, 0, 0, 0))],
        out_specs=head, out_shape=SDS((HEADS, SEQ, HEAD_DIM), F32), name="na_fwd",
        scratch_shapes=[pltpu.VMEM((NA_TYPES, NA_QB, NA_KB), F32)],
        compiler_params=_params(("parallel",)))(qkv, qkv, qkv, bias)


def _na_bwd(qkv, bias, do):
    def body(q_ref, k_ref, v_ref, slab_ref, do_ref, dq_ref, dk_ref, dv_ref, dslab_ref, b_ref, db_ref):
        _bias_tiles(slab_ref, b_ref)
        km = k_ref[0:N_META, :].astype(BF16)
        vm = v_ref[0:N_META, :].astype(BF16)
        dk_ref[...] = jnp.zeros_like(dk_ref)
        dv_ref[...] = jnp.zeros_like(dv_ref)
        db_ref[...] = jnp.zeros_like(db_ref)

        qm = q_ref[0:N_META, :].astype(BF16)
        dom = do_ref[0:N_META, :].astype(BF16)
        pmm = _meta_probs(qm, km)
        dpm = _dot_nt(dom, vm)
        dsm = (pmm * (dpm - jnp.sum(pmm * dpm, axis=-1, keepdims=True)) * ATT_SCALE).astype(BF16)
        dq_ref[0:N_META, :] = _dot(dsm, km)
        dkm0 = _dot_tn(dsm, qm)
        dvm0 = _dot_tn(pmm.astype(BF16), dom)

        def block(g, carry):
            dkm, dvm = carry
            block_type, q0, k0 = _block_geometry(g)
            qb = q_ref[pl.ds(q0, NA_QB), :].astype(BF16)
            kk = k_ref[pl.ds(k0, NA_KB), :].astype(BF16)
            vv = v_ref[pl.ds(k0, NA_KB), :].astype(BF16)
            dob = do_ref[pl.ds(q0, NA_QB), :].astype(BF16)
            p, pm = _na_probs(qb, kk, km, b_ref[block_type])
            dp = _dot_nt(dob, vv)
            dpm_ = _dot_nt(dob, vm)
            delta = jnp.sum(p * dp, axis=-1, keepdims=True) + jnp.sum(pm * dpm_, axis=-1, keepdims=True)
            ds = p * (dp - delta)
            dsm_ = pm * (dpm_ - delta)
            db_ref[block_type] += ds
            dsb = (ds * ATT_SCALE).astype(BF16)
            dsmb = (dsm_ * ATT_SCALE).astype(BF16)
            dq_ref[pl.ds(q0, NA_QB), :] = _dot(dsb, kk) + _dot(dsmb, km)
            dk_ref[pl.ds(k0, NA_KB), :] += _dot_tn(dsb, qb)
            dv_ref[pl.ds(k0, NA_KB), :] += _dot_tn(p.astype(BF16), dob)
            return dkm + _dot_tn(dsmb, qb), dvm + _dot_tn(pm.astype(BF16), dob)

        dkm, dvm = lax.fori_loop(0, NA_BLOCKS, block, (dkm0, dvm0))
        dk_ref[0:N_META, :] = dkm
        dv_ref[0:N_META, :] = dvm
        _bias_tiles_bwd(db_ref, dslab_ref)

    head = pl.BlockSpec((None, SEQ, HEAD_DIM), lambda h: (h, 0, 0))
    bspec = pl.BlockSpec((None, 2 * KH - 1, GRID_W, GRID_W), lambda h: (h, 0, 0, 0))
    return pl.pallas_call(
        body, grid=(HEADS,), in_specs=_qkv_specs() + [bspec, head], out_specs=[head, head, head, bspec],
        out_shape=[SDS((HEADS, SEQ, HEAD_DIM), F32)] * 3 + [SDS((HEADS, 2 * KH - 1, GRID_W, GRID_W), F32)],
        scratch_shapes=[pltpu.VMEM((NA_TYPES, NA_QB, NA_KB), F32), pltpu.VMEM((NA_TYPES, NA_QB, NA_KB), F32)],
        name="na_bwd", compiler_params=_params(("parallel",)))(qkv, qkv, qkv, bias, do)


def _cmul(ar, ai, br, bi):
    return ar * br - ai * bi, ar * bi + ai * br


def _cpow(ar, ai, n):
    rr, ri = None, None
    br, bi = ar, ai
    while n:
        if n & 1:
            rr, ri = (br, bi) if rr is None else _cmul(rr, ri, br, bi)
        n >>= 1
        if n:
            br, bi = _cmul(br, bi, br, bi)
    return rr, ri


def _s5_prep(lr, li, logdt, bre, bim):
    def body(lr_ref, li_ref, dt_ref, br_ref, bi_ref, lbr_ref, lbi_ref, bbr_ref, bbi_ref):
        lr_, li_ = lr_ref[...], li_ref[...]
        dt = jnp.exp(dt_ref[...])
        mag = jnp.exp(lr_ * dt)
        lbr = mag * jnp.cos(li_ * dt)
        lbi = mag * jnp.sin(li_ * dt)
        lbr_ref[...] = lbr
        lbi_ref[...] = lbi
        den = lr_ * lr_ + li_ * li_
        xr = lbr - 1.0
        cr = (xr * lr_ + lbi * li_) / den
        ci = (lbi * lr_ - xr * li_) / den
        br, bi = br_ref[...], bi_ref[...]
        bbr_ref[...] = cr[:, None, :] * br - ci[:, None, :] * bi
        bbi_ref[...] = cr[:, None, :] * bi + ci[:, None, :] * br

    n = 2 * S5_GROUPS
    return pl.pallas_call(
        body, out_shape=[SDS((n, S5_STATE), F32)] * 2 + [SDS((n, S5_GROUP, S5_STATE), F32)] * 2,
        name="s5_prep", compiler_params=_params())(lr, li, logdt, bre, bim)


def _s5_prep_bwd(lr, li, logdt, bre, bim, dar, dai, dbbr, dbbi):
    def body(lr_ref, li_ref, dt_ref, br_ref, bi_ref, dar_ref, dai_ref, dbr_ref, dbi_ref,
             glr_ref, gli_ref, gdt_ref, gbr_ref, gbi_ref):
        lr_, li_ = lr_ref[...], li_ref[...]
        dt = jnp.exp(dt_ref[...])
        mag = jnp.exp(lr_ * dt)
        lbr = mag * jnp.cos(li_ * dt)
        lbi = mag * jnp.sin(li_ * dt)
        den = lr_ * lr_ + li_ * li_
        xr = lbr - 1.0
        cr = (xr * lr_ + lbi * li_) / den
        ci = (lbi * lr_ - xr * li_) / den
        br, bi = br_ref[...], bi_ref[...]
        dbr, dbi = dbr_ref[...], dbi_ref[...]
        gbr_ref[...] = cr[:, None, :] * dbr + ci[:, None, :] * dbi
        gbi_ref[...] = cr[:, None, :] * dbi - ci[:, None, :] * dbr
        gcr = jnp.sum(dbr * br + dbi * bi, axis=1)
        gci = jnp.sum(dbi * br - dbr * bi, axis=1)
        ilr, ili = lr_ / den, li_ / den
        tr, ti = _cmul(gcr, gci, ilr, ili)
        glbr = dar_ref[...] + tr
        glbi = dai_ref[...] + ti
        dr_, di_ = _cmul(tr, ti, cr, -ci)
        gwr, gwi = _cmul(glbr, glbi, lbr, -lbi)
        glr_ref[...] = gwr * dt - dr_
        gli_ref[...] = gwi * dt - di_
        gdt_ref[...] = jnp.sum(gwr * lr_ + gwi * li_, axis=-1, keepdims=True) * dt

    n = 2 * S5_GROUPS
    return pl.pallas_call(
        body, out_shape=[SDS((n, S5_STATE), F32)] * 2 + [SDS((n, 1), F32)] + [SDS((n, S5_GROUP, S5_STATE), F32)] * 2,
        name="s5_prep_bwd", compiler_params=_params())(lr, li, logdt, bre, bim, dar, dai, dbbr, dbbi)


def _scan_local(xr_ref, xi_ref, ar8, ai8, reverse):
    def step(i, carry):
        sr, si = carry
        idx = (SCAN_T - 1 - i) if reverse else i
        rows = pl.ds(pl.multiple_of(idx * SCAN_BLOCKS, SCAN_BLOCKS), SCAN_BLOCKS)
        nr = ar8 * sr - ai8 * si + xr_ref[rows, :]
        ni = ar8 * si + ai8 * sr + xi_ref[rows, :]
        xr_ref[rows, :] = nr
        xi_ref[rows, :] = ni
        return nr, ni

    z = jnp.zeros(ar8.shape, F32)
    return lax.fori_loop(0, SCAN_T, step, (z, z))


def _scan_carries(er, ei, atr, ati, reverse):
    row = lax.broadcasted_iota(jnp.int32, er.shape, 0)
    cr = jnp.zeros((1, er.shape[1]), F32)
    ci = cr
    outr = jnp.zeros(er.shape, F32)
    outi = outr
    order = range(SCAN_BLOCKS - 1, -1, -1) if reverse else range(SCAN_BLOCKS)
    for b in order:
        outr = jnp.where(row == b, cr, outr)
        outi = jnp.where(row == b, ci, outi)
        nr, ni = _cmul(atr, ati, cr, ci)
        cr, ci = nr + er[b:b + 1, :], ni + ei[b:b + 1, :]
    return outr, outi


def _scan_fixup(xr_ref, xi_ref, cr8, ci8, ar8, ai8, reverse, pair=None):
    tile = lambda idx: pl.ds(pl.multiple_of(idx * SCAN_BLOCKS, SCAN_BLOCKS), SCAN_BLOCKS)

    def fix(idx, pr, pi):
        fr, fi = _cmul(pr, pi, cr8, ci8)
        nr, ni = xr_ref[tile(idx), :] + fr, xi_ref[tile(idx), :] + fi
        xr_ref[tile(idx), :] = nr
        xi_ref[tile(idx), :] = ni
        return nr, ni

    if pair is None:
        def step(i, carry):
            pr, pi = carry
            fix((SCAN_T - 1 - i) if reverse else i, pr, pi)
            return _cmul(pr, pi, ar8, ai8)

        lax.fori_loop(0, SCAN_T, step, (ar8, ai8), unroll=2)
        return None

    sr_ref, si_ref = pair
    earlier = -1 if reverse else 1

    def step(i, carry):
        pr, pi, accr, acci = carry
        idx = (SCAN_T - 1 - i) if reverse else i
        nr, ni = fix(idx, pr, pi)
        qr, qi = _cmul(nr, ni, sr_ref[tile(idx + earlier), :], -si_ref[tile(idx + earlier), :])
        pr, pi = _cmul(pr, pi, ar8, ai8)
        return pr, pi, accr + qr, acci + qi

    z = jnp.zeros(ar8.shape, F32)
    pr, pi, accr, acci = lax.fori_loop(0, SCAN_T - 1, step, (ar8, ai8, z, z))
    edge, src, shift, empty = (0, SCAN_T - 1, 1, 0) if reverse else (SCAN_T - 1, 0, SCAN_BLOCKS - 1, SCAN_BLOCKS - 1)
    nr, ni = fix(edge, pr, pi)
    row = lax.broadcasted_iota(jnp.int32, ar8.shape, 0)
    spr = jnp.where(row == empty, 0.0, pltpu.roll(sr_ref[tile(src), :], shift, 0))
    spi = jnp.where(row == empty, 0.0, pltpu.roll(si_ref[tile(src), :], shift, 0))
    qr, qi = _cmul(nr, ni, spr, -spi)
    return jnp.sum(accr + qr, axis=0, keepdims=True), jnp.sum(acci + qi, axis=0, keepdims=True)


def _scan(xr_ref, xi_ref, ar, ai, reverse, pair=None):
    n = ar.shape[1]
    ar8 = jnp.broadcast_to(ar, (SCAN_BLOCKS, n))
    ai8 = jnp.broadcast_to(ai, (SCAN_BLOCKS, n))
    er, ei = _scan_local(xr_ref, xi_ref, ar8, ai8, reverse)
    atr, ati = _cpow(ar, ai, SCAN_T)
    cr8, ci8 = _scan_carries(er, ei, atr, ati, reverse)
    return _scan_fixup(xr_ref, xi_ref, cr8, ci8, ar8, ai8, reverse, pair)


def _s5_specs():
    chan = pl.BlockSpec((SEQ, CH_W), lambda c, d: (0, c))
    chan2 = pl.BlockSpec((None, SEQ, CH_W), lambda c, d: (d, 0, c))
    state = pl.BlockSpec((None, SEQ, ST_W), lambda c, d: (d, 0, c))
    bmat = pl.BlockSpec((None, None, CH_W, ST_W), lambda c, d: (d, c, 0, 0))
    cmat = pl.BlockSpec((None, None, ST_W, CH_W), lambda c, d: (d, c, 0, 0))
    avec = pl.BlockSpec((None, None, 1, ST_W), lambda c, d: (d, c, 0, 0))
    return chan, chan2, state, bmat, cmat, avec


def _scan_by_direction(xr_ref, xi_ref, ar, ai, d, adjoint, pair=None, da_out=None):
    for direction in range(2):
        @pl.when(d == direction)
        def _(direction=direction):
            res = _scan(xr_ref, xi_ref, ar, ai, adjoint != (direction == 1), pair)
            if pair is not None:
                da_out[0][...], da_out[1][...] = res


def _s5_scan_fwd(u, bre, bim, are, aim, cre, cim):
    def body(u_ref, bre_ref, bim_ref, are_ref, aim_ref, cre_ref, cim_ref, sr_ref, si_ref, y_ref):
        ub = u_ref[...].astype(BF16)
        sr_ref[...] = _dot(ub, bre_ref[...])
        si_ref[...] = _dot(ub, bim_ref[...])
        _scan_by_direction(sr_ref, si_ref, are_ref[...], aim_ref[...], pl.program_id(1), adjoint=False)
        y_ref[...] = _dot(sr_ref[...].astype(BF16), cre_ref[...]) - _dot(si_ref[...].astype(BF16), cim_ref[...])

    chan, chan2, state, bmat, cmat, avec = _s5_specs()
    return pl.pallas_call(
        body, grid=(S5_CHUNKS, 2), in_specs=[chan, bmat, bmat, avec, avec, cmat, cmat], out_specs=[state, state, chan2],
        out_shape=[SDS((2, SEQ, S5_GROUPS * S5_STATE), F32)] * 2 + [SDS((2, SEQ, S5_WIDTH), F32)],
        name="s5_scan_fwd", compiler_params=_params(("parallel", "parallel")))(u, bre, bim, are, aim, cre, cim)


def _diag_out(out_ref, full):
    for g in range(8):
        out_ref[g] = full[g * S5_GROUP:(g + 1) * S5_GROUP, g * S5_STATE:(g + 1) * S5_STATE]


def _s5_scan_bwd(dy, du_skip, u, sr, si, bre, bim, are, aim, cre, cim):
    def body(dy_ref, dus_ref, u_ref, sr_ref, si_ref, bre_ref, bim_ref, are_ref, aim_ref, cre_ref, cim_ref,
             du_ref, dbr_ref, dbi_ref, dcr_ref, dci_ref, dar_ref, dai_ref, gr_ref, gi_ref):
        d = pl.program_id(1)
        dyb = dy_ref[...].astype(BF16)
        gr_ref[...] = _dot_nt(dyb, cre_ref[...])
        gi_ref[...] = -_dot_nt(dyb, cim_ref[...])
        _diag_out(dcr_ref, _dot_tn(dyb, sr_ref[...].astype(BF16)))
        _diag_out(dci_ref, -_dot_tn(dyb, si_ref[...].astype(BF16)))
        _scan_by_direction(gr_ref, gi_ref, are_ref[...], -aim_ref[...], d, adjoint=True, pair=(sr_ref, si_ref),
                           da_out=(dar_ref, dai_ref))

        @pl.when(d == 0)
        def _():
            du_ref[...] = dus_ref[...]

        grb = gr_ref[...].astype(BF16)
        gib = gi_ref[...].astype(BF16)
        du_ref[...] += _dot_nt(grb, bre_ref[...]) + _dot_nt(gib, bim_ref[...])
        ub = u_ref[...].astype(BF16)
        _diag_out(dbr_ref, _dot_tn(ub, grb))
        _diag_out(dbi_ref, _dot_tn(ub, gib))

    chan, _, state, bmat, cmat, avec = _s5_specs()
    diag = pl.BlockSpec((None, None, 8, S5_GROUP, S5_STATE), lambda c, d: (d, c, 0, 0, 0))
    return pl.pallas_call(
        body, grid=(S5_CHUNKS, 2), in_specs=[chan, chan, chan, state, state, bmat, bmat, avec, avec, cmat, cmat],
        out_specs=[chan, diag, diag, diag, diag, avec, avec],
        out_shape=[SDS((SEQ, S5_WIDTH), F32)] + [SDS((2, S5_CHUNKS, 8, S5_GROUP, S5_STATE), F32)] * 4
                  + [SDS((2, S5_CHUNKS, 1, ST_W), F32)] * 2,
        scratch_shapes=[pltpu.VMEM((SEQ, ST_W), F32), pltpu.VMEM((SEQ, ST_W), F32)],
        name="s5_scan_bwd", compiler_params=_params(("parallel", "arbitrary")))(dy, du_skip, u, sr, si, bre, bim, are, aim, cre, cim)


_GELU_K = math.sqrt(2.0 / math.pi)
_GELU_C = 0.044715


def _gelu(x):
    t = jnp.tanh(_GELU_K * (x + _GELU_C * x * x * x))
    return 0.5 * x * (1.0 + t), t


def _s5_glu_fwd(u, y2, dskip, wglu, bglu):
    def body(u_ref, y0_ref, y1_ref, d_ref, w_ref, b_ref, o_ref, yp_ref):
        ypre = u_ref[...] * d_ref[...] + y0_ref[...] + y1_ref[...]
        yp_ref[...] = ypre
        y, _ = _gelu(ypre)
        z = _dot(y.astype(BF16), w_ref[...]) + b_ref[...]
        o_ref[...] = y * jax.nn.sigmoid(z)

    row = _row_spec(S5_WIDTH)
    vec = _fix_spec((1, S5_WIDTH))
    dir0 = pl.BlockSpec((None, ROW_TILE, S5_WIDTH), lambda i: (0, i, 0))
    dir1 = pl.BlockSpec((None, ROW_TILE, S5_WIDTH), lambda i: (1, i, 0))
    return pl.pallas_call(
        body, grid=(N_ROW_TILES,), in_specs=[row, dir0, dir1, vec, _fix_spec((S5_WIDTH, S5_WIDTH)), vec],
        out_specs=[row, row], out_shape=[SDS((SEQ, S5_WIDTH), F32)] * 2, name="s5_glu_fwd",
        compiler_params=_params(("parallel",)))(u, y2, y2, dskip, wglu, bglu)


def _s5_glu_bwd(do, ypre, u, dskip, wglu, bglu):
    def body(do_ref, yp_ref, u_ref, d_ref, w_ref, b_ref, dyp_ref, du_ref, dw_ref, db_ref, dd_ref):
        i = pl.program_id(0)
        ypre = yp_ref[...]
        y, t = _gelu(ypre)
        yb = y.astype(BF16)
        sg = jax.nn.sigmoid(_dot(yb, w_ref[...]) + b_ref[...])
        dov = do_ref[...]
        dz = dov * y * sg * (1.0 - sg)
        dzb = dz.astype(BF16)
        dy = dov * sg + _dot_nt(dzb, w_ref[...])
        dgelu = 0.5 * (1.0 + t) + 0.5 * ypre * (1.0 - t * t) * _GELU_K * (1.0 + 3.0 * _GELU_C * ypre * ypre)
        dyp = dy * dgelu
        dyp_ref[...] = dyp
        uv = u_ref[...]
        du_ref[...] = dyp * d_ref[...]

        @pl.when(i == 0)
        def _():
            dw_ref[...] = jnp.zeros_like(dw_ref)
            db_ref[...] = jnp.zeros_like(db_ref)
            dd_ref[...] = jnp.zeros_like(dd_ref)

        dw_ref[...] += _dot_tn(yb, dzb)
        db_ref[...] += jnp.sum(dz, axis=0, keepdims=True)
        dd_ref[...] += jnp.sum(dyp * uv, axis=0, keepdims=True)

    row = _row_spec(S5_WIDTH)
    vec = _fix_spec((1, S5_WIDTH))
    mat = _fix_spec((S5_WIDTH, S5_WIDTH))
    return pl.pallas_call(
        body, grid=(N_ROW_TILES,), in_specs=[row, row, row, vec, mat, vec], out_specs=[row, row, mat, vec, vec],
        out_shape=[SDS((SEQ, S5_WIDTH), F32)] * 2 + [SDS((S5_WIDTH, S5_WIDTH), F32), SDS((1, S5_WIDTH), F32), SDS((1, S5_WIDTH), F32)],
        name="s5_glu_bwd", compiler_params=_params(("arbitrary",)))(do, ypre, u, dskip, wglu, bglu)


def _heads_side_by_side(o_ref):
    return jnp.concatenate([o_ref[h] for h in range(HEADS)], axis=-1)


def _mix_out_fwd(ona, os5, g_na, g_s5, wout):
    def body(a_ref, s_ref, ga_ref, gs_ref, w_ref, o_ref):
        av, sv = _heads_side_by_side(a_ref), s_ref[...]
        ca = (av * _rstd(av) * ga_ref[...]).astype(BF16)
        cs = (sv * _rstd(sv) * gs_ref[...]).astype(BF16)
        o_ref[...] = _dot(ca, w_ref[0:NA_WIDTH, :]) + _dot(cs, w_ref[NA_WIDTH:, :])

    row = _row_spec(NA_WIDTH)
    vec = _fix_spec((1, NA_WIDTH))
    heads = pl.BlockSpec((HEADS, ROW_TILE, HEAD_DIM), lambda i: (0, i, 0))
    return pl.pallas_call(
        body, grid=(N_ROW_TILES,), in_specs=[heads, row, vec, vec, _fix_spec((D_MODEL, D_MODEL))],
        out_specs=_row_spec(D_MODEL), out_shape=SDS((SEQ, D_MODEL), F32), name="mix_out_fwd",
        compiler_params=_params(("parallel",)))(ona, os5, g_na, g_s5, wout)


def _mix_out_bwd(dmix, ona, os5, g_na, g_s5, wout):
    def body(dm_ref, a_ref, s_ref, ga_ref, gs_ref, w_ref, da_ref, ds_ref, dw_ref, dga_ref, dgs_ref):
        i = pl.program_id(0)
        dm = dm_ref[...]
        av, sv = _heads_side_by_side(a_ref), s_ref[...]
        ra, rs = _rstd(av), _rstd(sv)
        ga, gs = ga_ref[...], gs_ref[...]
        ca = (av * ra * ga).astype(BF16)
        cs = (sv * rs * gs).astype(BF16)
        dca = _dot_nt(dm, w_ref[0:NA_WIDTH, :])
        dcs = _dot_nt(dm, w_ref[NA_WIDTH:, :])
        da, dga = _rms_bwd(av, ra, ga, dca)
        ds, dgs = _rms_bwd(sv, rs, gs, dcs)
        for h in range(HEADS):
            da_ref[h] = da[:, h * HEAD_DIM:(h + 1) * HEAD_DIM]
        ds_ref[...] = ds

        @pl.when(i == 0)
        def _():
            dw_ref[...] = jnp.zeros_like(dw_ref)
            dga_ref[...] = jnp.zeros_like(dga_ref)
            dgs_ref[...] = jnp.zeros_like(dgs_ref)

        dw_ref[0:NA_WIDTH, :] += _dot_tn(ca, dm)
        dw_ref[NA_WIDTH:, :] += _dot_tn(cs, dm)
        dga_ref[...] += jnp.sum(dga, axis=0, keepdims=True)
        dgs_ref[...] += jnp.sum(dgs, axis=0, keepdims=True)

    row = _row_spec(NA_WIDTH)
    vec = _fix_spec((1, NA_WIDTH))
    mat = _fix_spec((D_MODEL, D_MODEL))
    heads = pl.BlockSpec((HEADS, ROW_TILE, HEAD_DIM), lambda i: (0, i, 0))
    return pl.pallas_call(
        body, grid=(N_ROW_TILES,), in_specs=[_row_spec(D_MODEL), heads, row, vec, vec, mat],
        out_specs=[heads, row, mat, vec, vec],
        out_shape=[SDS((HEADS, SEQ, HEAD_DIM), F32), SDS((SEQ, NA_WIDTH), F32), SDS((D_MODEL, D_MODEL), F32),
                   SDS((1, NA_WIDTH), F32), SDS((1, NA_WIDTH), F32)],
        name="mix_out_bwd", compiler_params=_params(("arbitrary",)))(dmix, ona, os5, g_na, g_s5, wout)


def _me():
    x, y, c = lax.axis_index("x"), lax.axis_index("y"), lax.axis_index("c")
    return x, y, c, 4 * x + 2 * y + c


def _peer(k):
    x, y, c, _ = _me()
    px = 1 - x if (k >> 2) & 1 else x
    py = 1 - y if (k >> 1) & 1 else y
    pc = 1 - c if k & 1 else c
    return (px, py, pc), 4 * px + 2 * py + pc


ALL_PEERS = (1, 2, 3, 4, 5, 6, 7)
CHIP_PEERS = (2, 4, 6)
SIBLING = 1


def _slot8(pos):
    return 4 * pos[0] + 2 * pos[1] + pos[2]


def _slot4(pos):
    return 2 * pos[0] + pos[1]


_HBM = pl.BlockSpec(memory_space=pltpu.HBM)
_SEM = pl.BlockSpec(memory_space=pltpu.SEMAPHORE)
_EFFECT = pltpu.SideEffectType.DATAFLOW_SIDE_EFFECTING


def _exchange_start(arrays, lands, gather, name, peers=ALL_PEERS, slot=_slot8, own=True):
    n = len(arrays)

    def body(*refs):
        ins, lnd = refs[:n], refs[n:2 * n]
        send_sems, recv_sems = refs[2 * n], refs[2 * n + 1]
        token = refs[-1]
        me = slot(_me()[:3])
        for i, k in enumerate(peers):
            peer, _ = _peer(k)
            for a in range(n):
                src = ins[a] if gather else ins[a].at[slot(peer)]
                s = a * len(peers) + i
                pltpu.make_async_remote_copy(src_ref=src, dst_ref=lnd[a].at[me], send_sem=send_sems.at[s],
                                             recv_sem=recv_sems.at[s], device_id=peer, device_id_type=MESH).start()
        if own:
            for a in range(n):
                pltpu.make_async_copy(ins[a] if gather else ins[a].at[me], lnd[a].at[me], recv_sems.at[n * len(peers) + a]).start()
        token[...] = jnp.zeros_like(token)

    sems = pltpu.SemaphoreType.DMA((n * (len(peers) + int(own)),))
    out = pl.pallas_call(
        body, name=name, in_specs=[_HBM] * (2 * n),
        out_shape=(sems, sems) + tuple(pltpu.HBM(a.shape, a.dtype) for a in list(arrays) + list(lands)) + (SDS((8, 128), F32),),
        out_specs=(_SEM, _SEM) + (_HBM,) * (2 * n) + (pl.BlockSpec(memory_space=pltpu.VMEM),),
        input_output_aliases={i: 2 + i for i in range(2 * n)},
        compiler_params=pltpu.CompilerParams(has_side_effects=_EFFECT),
    )(*[pltpu.with_memory_space_constraint(a, pltpu.HBM) for a in list(arrays) + list(lands)])
    return out[0], out[1], list(out[2:2 + n]), list(out[2 + n:2 + 2 * n]), out[-1]


def _exchange_wait(send_sems, recv_sems, arrays, lands, after, gather, name, peers=ALL_PEERS, slot=_slot8, own=True):
    n = len(arrays)

    def body(*refs):
        ins, lnd = refs[:n], refs[n:2 * n]
        send_sems, recv_sems = refs[2 * n], refs[2 * n + 1]
        if own:
            me = slot(_me()[:3])
            for a in range(n):
                pltpu.make_async_copy(ins[a] if gather else ins[a].at[me], lnd[a].at[me], recv_sems.at[n * len(peers) + a]).wait()
        for i, k in enumerate(peers):
            peer, _ = _peer(k)
            for a in range(n):
                src = ins[a] if gather else ins[a].at[slot(peer)]
                s = a * len(peers) + i
                cp = pltpu.make_async_remote_copy(src_ref=src, dst_ref=lnd[a].at[slot(peer)], send_sem=send_sems.at[s],
                                                  recv_sem=recv_sems.at[s], device_id=peer, device_id_type=MESH)
                cp.wait_send()
                cp.wait_recv()

        refs[-1][...] = jnp.zeros_like(refs[-1])

    after = list(after) if isinstance(after, (list, tuple)) else [after]
    out = pl.pallas_call(
        body, name=name, in_specs=[_HBM] * (2 * n) + [_SEM, _SEM] + [pl.BlockSpec(memory_space=pl.ANY)] * len(after),
        out_shape=tuple(pltpu.HBM(a.shape, a.dtype) for a in list(arrays) + list(lands)) + (SDS((8, 128), F32),),
        out_specs=(_HBM,) * (2 * n) + (pl.BlockSpec(memory_space=pltpu.VMEM),), input_output_aliases={i: i for i in range(2 * n)},
        compiler_params=pltpu.CompilerParams(has_side_effects=_EFFECT),
    )(*arrays, *lands, send_sems, recv_sems, *after)
    return list(out[n:2 * n]), out[-1]


def _forward_sibling(lands, name):
    n = len(lands)

    def body(*refs):
        outs = refs[n:2 * n]
        send_sems, recv_sems = refs[2 * n:]
        x, y, c, _ = _me()
        sends = []
        for i, k in enumerate(CHIP_PEERS):
            peer, _ = _peer(k)
            for a in range(n):
                rows = outs[a].at[_slot8(peer)]
                cp = pltpu.make_async_remote_copy(src_ref=rows, dst_ref=rows, send_sem=send_sems.at[a, i], recv_sem=recv_sems.at[a, i],
                                                  device_id=(x, y, 1 - c), device_id_type=MESH)
                cp.start()
                sends.append(cp)
        for i, k in enumerate(CHIP_PEERS):
            (px, py, pc), _ = _peer(k)
            for a in range(n):
                rows = outs[a].at[_slot8((px, py, 1 - pc))]
                pltpu.make_async_remote_copy(src_ref=rows, dst_ref=rows, send_sem=send_sems.at[a, i], recv_sem=recv_sems.at[a, i],
                                             device_id=(x, y, 1 - c), device_id_type=MESH).wait_recv()
        for cp in sends:
            cp.wait_send()

    return pl.pallas_call(
        body, in_specs=[_HBM] * n, out_specs=[_HBM] * n, out_shape=[SDS(a.shape, a.dtype) for a in lands],
        input_output_aliases={i: i for i in range(n)},
        scratch_shapes=[pltpu.SemaphoreType.DMA((n, len(CHIP_PEERS))), pltpu.SemaphoreType.DMA((n, len(CHIP_PEERS)))],
        name=name)(*lands)


def _swap_sibling(arrays, name, after=()):
    n, n_after = len(arrays), len(after)
    chips = N_DEV // 2

    def body(*refs):
        ins, outs = refs[:n], refs[n + n_after:2 * n + n_after]
        send_sems, recv_sems = refs[2 * n + n_after:]
        x, y, c, _ = _me()
        sends = []
        for q in range(chips):
            for a in range(n):
                cp = pltpu.make_async_remote_copy(src_ref=ins[a].at[q, 1 - c], dst_ref=outs[a].at[q], send_sem=send_sems.at[a, q],
                                                  recv_sem=recv_sems.at[a, q], device_id=(x, y, 1 - c), device_id_type=MESH)
                cp.start()
                sends.append(cp)
        for cp in sends:
            cp.wait_recv()
        for cp in sends:
            cp.wait_send()

    return pl.pallas_call(
        body, in_specs=[_HBM] * n + [pl.BlockSpec(memory_space=pl.ANY)] * n_after, out_specs=[_HBM] * n,
        out_shape=[SDS((chips,) + a.shape[2:], a.dtype) for a in arrays],
        scratch_shapes=[pltpu.SemaphoreType.DMA((n, chips)), pltpu.SemaphoreType.DMA((n, chips))], name=name)(*arrays, *after)


def _sum_pairs(mine, theirs, name):
    n = len(mine)
    chips = mine[0].shape[0]
    c = lax.axis_index("c")

    def body(c_ref, *refs):
        for a in range(n):
            refs[2 * n + a][...] = (refs[a][...].astype(F32) + refs[n + a][...].astype(F32)).astype(refs[2 * n + a].dtype)

    def pair(a):
        return pl.BlockSpec((None, None) + a.shape[2:], lambda q, c_ref: (q, c_ref[0], 0, 0))

    def single(a):
        return pl.BlockSpec((None,) + a.shape[2:], lambda q, c_ref: (q, 0, 0))

    return pl.pallas_call(
        body, grid_spec=pltpu.PrefetchScalarGridSpec(
            num_scalar_prefetch=1, grid=(chips,), in_specs=[pair(a) for a in mine] + [single(a) for a in mine],
            out_specs=[single(a) for a in mine]),
        out_shape=[SDS((chips,) + a.shape[2:], a.dtype) for a in mine], name=name,
        compiler_params=_params(("parallel",)))(c.reshape(1).astype(jnp.int32), *mine, *theirs)


def _adamw_math(w, g, m, v):
    m = ADAM_B1 * m + (1.0 - ADAM_B1) * g
    v = ADAM_B2 * v + (1.0 - ADAM_B2) * (g * g)
    m_hat = m / (1.0 - ADAM_B1 ** ADAM_STEP)
    v_hat = v / (1.0 - ADAM_B2 ** ADAM_STEP)
    delta = -ADAM_LR * (m_hat / (jnp.sqrt(v_hat) + ADAM_EPS) + ADAM_WD * w)
    return delta, m, v


def _adamw(w, m, v, pieces, name):
    rows, cols = w.shape[-2:]
    lead = w.ndim - 2
    tile = rows
    for cand in (256, 176, 128, 64, 16):
        if rows > cand and rows % cand == 0:
            tile = cand
            break

    def body(w_ref, m_ref, v_ref, p_ref, g_ref, d_ref, mo_ref, vo_ref):
        g = _sum_pieces(p_ref)
        g_ref[...] = g
        d_ref[...], mo_ref[...], vo_ref[...] = _adamw_math(w_ref[...], g, m_ref[...], v_ref[...])

    blk = pl.BlockSpec((None,) * lead + (tile, cols), lambda i: (0,) * lead + (i, 0))
    return pl.pallas_call(
        body, grid=(rows // tile,), in_specs=[blk, blk, blk, pl.BlockSpec((pieces.shape[0], tile, cols), lambda i: (0, i, 0))],
        out_specs=[blk] * 4, out_shape=[SDS(w.shape, F32)] * 4, name=name,
        compiler_params=_params(("parallel",)))(w, m, v, pieces)


def _sum_pieces(p_ref):
    g = p_ref[0].astype(F32)
    for p in range(1, p_ref.shape[0]):
        g = g + p_ref[p].astype(F32)
    return g


def _adamw_s5_mat(w, m, v, g, name):
    _, ndir, groups, b, c = w.shape
    per_dir = groups // 8

    def body(w_ref, m_ref, v_ref, g_ref, d_ref, mo_ref, vo_ref):
        d_ref[...], mo_ref[...], vo_ref[...] = _adamw_math(w_ref[...], g_ref[...], m_ref[...], v_ref[...])

    blk = pl.BlockSpec((None, None, 8, b, c), lambda i: (0, i // per_dir, i % per_dir, 0, 0))
    return pl.pallas_call(
        body, grid=(ndir * per_dir,), in_specs=[blk] * 4, out_specs=[blk] * 3, out_shape=[SDS(w.shape, F32)] * 3, name=name,
        compiler_params=_params(("parallel",)))(w, m, v, g)


VEC_ROWS = ['ffn1_pre_g', 'ffn1_post_g', 'mix_pre_g', 'mix_post_g', 'ffn2_pre_g', 'ffn2_post_g', 'final_g',
            ('na_out_g', 's5_out_g'), ('s5_d', 's5_b_glu')]
VEC_NAMES = [n for row in VEC_ROWS for n in ((row,) if isinstance(row, str) else row)]
VEC_PACK_ROWS = 16
LOSS_ROW = len(VEC_ROWS)


def _pack_vectors(grads, loss8):
    def body(*refs):
        o_ref = refs[-1]
        o_ref[...] = jnp.zeros_like(o_ref)
        o_ref[LOSS_ROW:LOSS_ROW + 1, 0:128] = refs[-2][0:1, :]
        k = 0
        for i, row in enumerate(VEC_ROWS):
            if isinstance(row, str):
                o_ref[i:i + 1, :] = refs[k][...]
                k += 1
            else:
                o_ref[i:i + 1, 0:NA_WIDTH] = refs[k][...]
                o_ref[i:i + 1, NA_WIDTH:] = refs[k + 1][...]
                k += 2

    return pl.pallas_call(body, out_shape=SDS((VEC_PACK_ROWS, D_MODEL), F32), name="pack_vectors",
                          compiler_params=_params())(*[grads[n] for n in VEC_NAMES], loss8)


def _sum8(pieces, name):
    def body(p_ref, o_ref):
        o_ref[...] = _sum_pieces(p_ref)

    return pl.pallas_call(body, out_shape=SDS(pieces.shape[1:], F32), name=name, compiler_params=_params())(pieces)


def _adamw_small(packed8, vec_wmv, others):
    n_vec, n_oth = len(VEC_NAMES), len(others)

    def body(*refs):
        p_ref = refs[0]
        ins = refs[1:1 + 3 * n_vec + 4 * n_oth]
        outs = refs[1 + 3 * n_vec + 4 * n_oth:]
        gsum = _sum_pieces(p_ref)
        outs[-1][...] = gsum[LOSS_ROW:LOSS_ROW + 1, 0:128]
        k = 0
        for i, row in enumerate(VEC_ROWS):
            parts = [(row, gsum[i:i + 1, :])] if isinstance(row, str) else \
                [(row[0], gsum[i:i + 1, 0:NA_WIDTH]), (row[1], gsum[i:i + 1, NA_WIDTH:])]
            for _, g in parts:
                w_ref, m_ref, v_ref = ins[3 * k:3 * k + 3]
                outs[4 * k][...] = g
                outs[4 * k + 1][...], outs[4 * k + 2][...], outs[4 * k + 3][...] = _adamw_math(w_ref[...], g, m_ref[...], v_ref[...])
                k += 1
        for j in range(n_oth):
            w_ref, m_ref, v_ref, g_ref = ins[3 * n_vec + 4 * j:3 * n_vec + 4 * j + 4]
            g = _sum_pieces(g_ref)
            g = g[tuple(slice(0, s) for s in w_ref.shape[1:])].reshape(w_ref.shape)
            o = outs[4 * (n_vec + j):4 * (n_vec + j) + 4]
            o[0][...] = g
            o[1][...], o[2][...], o[3][...] = _adamw_math(w_ref[...], g, m_ref[...], v_ref[...])

    args, out_shape = [packed8], []
    for w, m, v in vec_wmv:
        args += [w, m, v]
        out_shape += [SDS(w.shape, F32)] * 4
    for w, m, v, g in others:
        args += [w, m, v, g]
        out_shape += [SDS(w.shape, F32)] * 4
    out_shape += [SDS((1, 128), F32)]
    return pl.pallas_call(body, out_shape=out_shape, name="adamw_small", compiler_params=_params())(*args)


def _perm_rows(x):
    return x.reshape(SCAN_BLOCKS, SCAN_T, x.shape[-1]).transpose(1, 0, 2).reshape(SEQ, x.shape[-1])


def _unperm_rows(x):
    return x.reshape(SCAN_T, SCAN_BLOCKS, x.shape[-1]).transpose(1, 0, 2).reshape(SEQ, x.shape[-1])


def _block_diag(x):
    eye = np.eye(8, dtype=bool)[None, None, :, None, :, None]
    full = jnp.where(eye, x[:, :, :, :, None, :], 0.0)
    return full.reshape(2, S5_CHUNKS, 8 * x.shape[3], 8 * x.shape[4])


STORED_SWAPPED = {"ffn1_w_gate": (1, 2), "ffn1_w_up": (1, 2), "ffn2_w_gate": (1, 2), "ffn2_w_up": (1, 2),
                  "s5_b_re": (3, 4), "s5_b_im": (3, 4)}


def _stored(name, x):
    return jnp.swapaxes(x, *STORED_SWAPPED[name]) if name in STORED_SWAPPED else x


def _dep(x, token):
    return x if token is None else x + token


def _local_step(x, target, get_w, small, emit):
    bias = _rpb_expand(small["na_rpb"][0])
    lr = small["s5_lam_re"].reshape(64, S5_STATE)
    li = small["s5_lam_im"].reshape(64, S5_STATE)
    logdt = small["s5_log_dt"].reshape(64, 1)
    b_t = [_stored(n, small[n]).reshape(64, S5_GROUP, S5_STATE) for n in ("s5_b_re", "s5_b_im")]
    lbr, lbi, bbr, bbi = _s5_prep(lr, li, logdt, b_t[0], b_t[1])
    are = lbr.reshape(2, S5_CHUNKS, 1, ST_W)
    aim = lbi.reshape(2, S5_CHUNKS, 1, ST_W)
    bre = _block_diag(bbr.reshape(2, S5_CHUNKS, 8, S5_GROUP, S5_STATE)).astype(BF16)
    bim = _block_diag(bbi.reshape(2, S5_CHUNKS, 8, S5_GROUP, S5_STATE)).astype(BF16)
    c_t = [small[n].reshape(2, S5_CHUNKS, 8, S5_GROUP, S5_STATE).transpose(0, 1, 2, 4, 3) for n in ("s5_c_re", "s5_c_im")]
    cre = _block_diag(c_t[0]).astype(BF16)
    cim = _block_diag(c_t[1]).astype(BF16)
    tgt = jnp.concatenate([jnp.zeros((N_META, D_MODEL), F32), target], axis=0)

    h0, a1 = _embed_prenorm(get_w("meta", None)["meta_tokens"], x, small["ffn1_pre_g"])
    wts = dict(get_w("ffn1", [bias, are, aim, bre, bim, cre, cim, tgt, a1]))
    gate1, up1, f1 = _ffn_fwd(a1, wts["ffn1_w_gate"], wts["ffn1_w_up"], wts["ffn1_w_down"], "ffn1_fwd",
                              after=wts.get("tokens", ()))
    h1, a2 = _post_pre(f1, h0, small["ffn1_post_g"], small["mix_pre_g"], 0.5, "post_pre1")
    wts.update(get_w("w_in", a2))
    qkv = _proj_heads(a2, wts["w_in"])
    u = _proj_u(a2, wts["w_in"])
    ona = _na_fwd(qkv, bias)
    u_p = _perm_rows(u)
    sr, si, y2 = _s5_scan_fwd(u_p, bre, bim, are, aim, cre, cim)
    wts.update(get_w("mix", y2))
    os5_p, ypre_p = _s5_glu_fwd(u_p, y2, small["s5_d"], wts["s5_w_glu"], small["s5_b_glu"])
    os5 = _unperm_rows(os5_p)

    mix = _mix_out_fwd(ona, os5, small["na_out_g"], small["s5_out_g"], wts["w_out"])
    h2, a3 = _post_pre(mix, h1, small["mix_post_g"], small["ffn2_pre_g"], 1.0, "post_pre2")
    wts.update(get_w("ffn2", a3))
    gate2, up2, f2 = _ffn_fwd(a3, wts["ffn2_w_gate"], wts["ffn2_w_up"], wts["ffn2_w_down"], "ffn2_fwd")
    loss8, dh3, df2, g_final, g_ffn2_post = _final_loss(f2, h2, small["ffn2_post_g"], small["final_g"], tgt)

    da3, dwg2, dwu2, dwd2 = _ffn_bwd(df2, a3, gate2, up2, wts["ffn2_w_gate"], wts["ffn2_w_up"], wts["ffn2_w_down"], "ffn2_bwd")
    tok = emit("ffn2", {"ffn2_w_gate": dwg2, "ffn2_w_up": dwu2, "ffn2_w_down": dwd2})
    dh2, dmix, g_ffn2_pre, g_mix_post = _bwd_pre_post(da3, h2, _dep(small["ffn2_pre_g"], tok), dh3, mix, small["mix_post_g"], 1.0,
                                                      "bwd_pre_post2")
    dona, dos5, dwout, g_na_out, g_s5_out = _mix_out_bwd(dmix, ona, os5, small["na_out_g"], small["s5_out_g"], wts["w_out"])

    dypre_p, du_skip_p, dwglu, g_b_glu, g_s5_d = _s5_glu_bwd(_perm_rows(dos5), ypre_p, u_p, small["s5_d"], wts["s5_w_glu"],
                                                             small["s5_b_glu"])
    tok = emit("mix", {"s5_w_glu": dwglu.reshape(N_DEV, S5_WIDTH // N_DEV, S5_WIDTH).astype(BF16),
                       "w_out": dwout.reshape(N_DEV, D_MODEL // N_DEV, D_MODEL).astype(BF16)})
    du_p, dbr, dbi, dcr, dci, dar, dai = _s5_scan_bwd(dypre_p, du_skip_p, u_p, sr, si, bre, bim, _dep(are, tok), aim, cre, cim)
    du = _unperm_rows(du_p)
    per_group = (2 * S5_GROUPS, S5_GROUP, S5_STATE)
    g_lr, g_li, g_dt, g_br, g_bi = _s5_prep_bwd(lr, li, logdt, b_t[0], b_t[1], dar.reshape(64, S5_STATE),
                                                dai.reshape(64, S5_STATE), dbr.reshape(per_group), dbi.reshape(per_group))
    g_c = [dcr.reshape(per_group), dci.reshape(per_group)]

    dq, dk, dv, dbias = _na_bwd(qkv, bias, dona)
    g_rpb = _rpb_reduce(dbias)
    dense = jnp.stack([g.reshape(2 * S5_GROUPS, S5_STATE * S5_GROUP) for g in (g_br, g_bi, *g_c)])
    tok = emit("small", {"dense": dense, "na_rpb": g_rpb,
                         "s5_lam_re": g_lr.reshape(2, S5_GROUPS, S5_STATE), "s5_lam_im": g_li.reshape(2, S5_GROUPS, S5_STATE),
                         "s5_log_dt": g_dt.reshape(2, S5_GROUPS)})
    da2, dwin = _proj_bwd(dq, dk, dv, du, a2, wts["w_in"])
    tok2 = emit("w_in", {"w_in": dwin})
    tok = tok if tok2 is None else tok + tok2
    dh1, df1, g_mix_pre, g_ffn1_post = _bwd_pre_post(da2, h1, _dep(small["mix_pre_g"], tok), dh2, f1, small["ffn1_post_g"], 0.5,
                                                     "bwd_pre_post1")
    da1, dwg1, dwu1, dwd1 = _ffn_bwd(df1, a1, gate1, up1, wts["ffn1_w_gate"], wts["ffn1_w_up"], wts["ffn1_w_down"], "ffn1_bwd")
    grad_x, grad_meta, g_ffn1_pre = _bwd_embed(da1, h0, small["ffn1_pre_g"], dh1)
    vec_g = {
        "ffn1_pre_g": g_ffn1_pre, "ffn1_post_g": g_ffn1_post, "mix_pre_g": g_mix_pre, "s5_d": g_s5_d, "s5_b_glu": g_b_glu,
        "na_out_g": g_na_out, "s5_out_g": g_s5_out, "mix_post_g": g_mix_post,
        "ffn2_pre_g": g_ffn2_pre, "ffn2_post_g": g_ffn2_post, "final_g": g_final,
    }
    emit("vec", {"packed": _pack_vectors(vec_g, loss8), "meta_tokens": grad_meta})
    emit("ffn1", {"ffn1_w_gate": dwg1, "ffn1_w_up": dwu1, "ffn1_w_down": dwd1})
    return grad_x


WEIGHT_NAMES = ['meta_tokens', 'ffn1_pre_g', 'ffn1_post_g', 'ffn1_w_gate', 'ffn1_w_up', 'ffn1_w_down', 'mix_pre_g', 'w_in',
                'na_rpb', 's5_lam_re', 's5_lam_im', 's5_log_dt', 's5_b_re', 's5_b_im', 's5_c_re', 's5_c_im', 's5_d',
                's5_w_glu', 's5_b_glu', 'na_out_g', 's5_out_g', 'w_out', 'mix_post_g', 'ffn2_pre_g', 'ffn2_post_g',
                'ffn2_w_gate', 'ffn2_w_up', 'ffn2_w_down', 'final_g']
BIG_NAMES = ['ffn1_w_gate', 'ffn1_w_up', 'ffn1_w_down', 'w_in', 's5_w_glu', 'w_out', 'ffn2_w_gate', 'ffn2_w_up', 'ffn2_w_down']
SMALL_NAMES = [n for n in WEIGHT_NAMES if n not in BIG_NAMES and n != 'meta_tokens']
WHOLE_NAMES = ['na_rpb', 's5_lam_re', 's5_lam_im', 's5_log_dt']
LEAD_NAMES = ['s5_b_re', 's5_b_im', 's5_c_re', 's5_c_im']


def kernel(x, meta_tokens, ffn1_pre_g, ffn1_post_g, ffn1_w_gate, ffn1_w_up, ffn1_w_down, mix_pre_g, w_in, na_rpb, s5_lam_re, s5_lam_im, s5_log_dt, s5_b_re, s5_b_im, s5_c_re, s5_c_im, s5_d, s5_w_glu, s5_b_glu, na_out_g, s5_out_g, w_out, mix_post_g, ffn2_pre_g, ffn2_post_g, ffn2_w_gate, ffn2_w_up, ffn2_w_down, final_g, loss_target, m_meta_tokens, m_ffn1_pre_g, m_ffn1_post_g, m_ffn1_w_gate, m_ffn1_w_up, m_ffn1_w_down, m_mix_pre_g, m_w_in, m_na_rpb, m_s5_lam_re, m_s5_lam_im, m_s5_log_dt, m_s5_b_re, m_s5_b_im, m_s5_c_re, m_s5_c_im, m_s5_d, m_s5_w_glu, m_s5_b_glu, m_na_out_g, m_s5_out_g, m_w_out, m_mix_post_g, m_ffn2_pre_g, m_ffn2_post_g, m_ffn2_w_gate, m_ffn2_w_up, m_ffn2_w_down, m_final_g, v_meta_tokens, v_ffn1_pre_g, v_ffn1_post_g, v_ffn1_w_gate, v_ffn1_w_up, v_ffn1_w_down, v_mix_pre_g, v_w_in, v_na_rpb, v_s5_lam_re, v_s5_lam_im, v_s5_log_dt, v_s5_b_re, v_s5_b_im, v_s5_c_re, v_s5_c_im, v_s5_d, v_s5_w_glu, v_s5_b_glu, v_na_out_g, v_s5_out_g, v_w_out, v_mix_post_g, v_ffn2_pre_g, v_ffn2_post_g, v_ffn2_w_gate, v_ffn2_w_up, v_ffn2_w_down, v_final_g):
    args = dict(locals())
    w = {n: args[n] for n in WEIGHT_NAMES}
    m = {n: args["m_" + n] for n in WEIGHT_NAMES}
    v = {n: args["v_" + n] for n in WEIGHT_NAMES}

    small = {n: w[n] for n in SMALL_NAMES}

    pending = {}

    def start(group, names, arrays, gather, peers=ALL_PEERS, slot=_slot8):
        n_slots = N_DEV if slot is _slot8 else N_DEV // 2
        lands = [lax.empty((n_slots,) + a.shape if gather else a.shape, a.dtype) for a in arrays]
        send_sems, recv_sems, arrays, lands, token = _exchange_start(arrays, lands, gather, "start_" + group, peers, slot)
        pending[group] = (names, send_sems, recv_sems, arrays, lands, gather, peers, slot)
        return token

    def finish(group, after):
        names, send_sems, recv_sems, arrays, lands, gather, peers, slot = pending.pop(group)
        lands, token = _exchange_wait(send_sems, recv_sems, arrays, lands, after, gather, "wait_" + group, peers, slot)
        return dict(zip(names, lands)), token

    first = ["ffn1_w_gate", "ffn1_w_up", "ffn1_w_down"]
    def shard(n, token=None):
        return _dep(_stored(n, w[n])[0], None if token is None else token[0, 0]).astype(BF16)

    ffn_names = ("ffn1_w_gate", "ffn1_w_up", "ffn1_w_down", "ffn2_w_gate", "ffn2_w_up", "ffn2_w_down")
    later_groups = (("w_in", ["w_in"]), ("mix", ["s5_w_glu", "w_out"]), ("ffn2", ["ffn2_w_gate", "ffn2_w_up", "ffn2_w_down"]))
    token0 = start("meta", ["meta_tokens"], [w["meta_tokens"]], True)
    token1 = start("ffn1", first, [shard(n, token0) for n in first], True, (SIBLING,) + CHIP_PEERS)
    meta_full = finish("meta", [token1])[0]["meta_tokens"].transpose(1, 0, 2).reshape(N_META, D_MODEL)
    later_shards = {n: shard(n, token1) for _, names in later_groups for n in names}
    for n in ("na_rpb", "s5_lam_re"):
        small[n] = _dep(small[n], token1[0, 0])

    def get_w(group, after):
        if group == "meta":
            return {"meta_tokens": meta_full}
        if group == "ffn1":
            after = list(after) + list(later_shards.values())
        got, token = finish(group, after)
        if group == "ffn1":
            got = dict(zip(got, _forward_sibling(list(got.values()), "forward_ffn1")))
            got["tokens"] = [start(g, names + ["order"], [later_shards[n] for n in names] + [token], True) for g, names in later_groups]
        if group == "mix":
            got = {"s5_w_glu": got["s5_w_glu"].reshape(S5_WIDTH, S5_WIDTH), "w_out": got["w_out"].reshape(D_MODEL, D_MODEL)}
        return {n: (a.reshape(D_FF, D_MODEL) if n in ffn_names else a) for n, a in got.items()}

    tokens = {}

    def emit(group, grads):
        grads = {n: (g.reshape(N_DEV, FF_SHARD, D_MODEL) if n in ffn_names else g) for n, g in grads.items()}
        if group == "ffn1":
            mine = [g.reshape((N_DEV // 2, 2) + g.shape[1:]) for g in grads.values()]
            theirs = _swap_sibling(mine, "swap_g_ffn1", after=[tokens["vec"]])
            sums = _sum_pairs(mine, theirs, "pair_sum_g_ffn1")
            tokens[group] = start("g_ffn1", list(grads), sums, False, CHIP_PEERS, _slot4)
        else:
            tokens[group] = start("g_" + group, list(grads), list(grads.values()), group in ("small", "vec"))
        return tokens[group][0, 0]

    grad_x = _local_step(x[0], loss_target[0], get_w, small, emit)
    res = {}

    def update_shard(n, pieces):
        outs = _adamw(_stored(n, w[n]), _stored(n, m[n]), _stored(n, v[n]), pieces, "adamw_" + n)
        res[n] = [_stored(n, o) for o in outs]

    late = [grad_x, tokens["ffn1"]]
    for group in ("g_ffn2", "g_mix", "g_w_in"):
        for n, pieces in finish(group, late)[0].items():
            update_shard(n, pieces)
    g8 = finish("g_small", late)[0]
    dense = _sum8(g8["dense"], "sum_dense")
    for i, n in enumerate(LEAD_NAMES):
        g = dense[i].reshape(_stored(n, w[n]).shape)
        upd = _adamw_s5_mat(_stored(n, w[n]), _stored(n, m[n]), _stored(n, v[n]), g, "adamw_" + n)
        res[n] = [_stored(n, o) for o in [g] + list(upd)]

    done = [res[n][1] for n in ("ffn2_w_gate", "ffn2_w_up", "ffn2_w_down", "w_in", "w_out", "s5_w_glu") + tuple(LEAD_NAMES)]
    got = finish("g_vec", done)[0]
    packed8, gmeta8 = got["packed"], got["meta_tokens"]
    for n, pieces in finish("g_ffn1", packed8)[0].items():
        update_shard(n, pieces)
    _, _, _, me = _me()
    update_shard("meta_tokens", lax.dynamic_slice_in_dim(gmeta8, me * (D_MODEL // N_DEV), D_MODEL // N_DEV, axis=2))

    outs = _adamw_small(packed8, [(w[n], m[n], v[n]) for n in VEC_NAMES], [(w[n], m[n], v[n], g8[n]) for n in WHOLE_NAMES])
    for i, n in enumerate(VEC_NAMES + WHOLE_NAMES):
        res[n] = list(outs[4 * i:4 * i + 4])

    out = [outs[-1][0, 0], grad_x[None]]
    for kind in range(4):
        out += [res[n][kind] for n in WEIGHT_NAMES]
    return tuple(out)
```

```python
import math

import numpy as np
import jax
import jax.numpy as jnp
from jax import lax
from jax.experimental import pallas as pl
from jax.experimental.pallas import tpu as pltpu

F32 = jnp.float32
BF16 = jnp.bfloat16
SDS = jax.ShapeDtypeStruct

D_MODEL = 1024
N_TOK = 2048
N_META = 16
SEQ = N_TOK + N_META
ROW_TILE = 688
N_ROW_TILES = SEQ // ROW_TILE
N_DEV = 8
D_FF = 2816
FF_SHARD = D_FF // N_DEV
FF_TILE = 256
IN_SHARD = 256
NA_WIDTH = 512
S5_WIDTH = 512
HEADS = 8
HEAD_DIM = 64
GRID_W = 64
GRID_ROWS = N_TOK // GRID_W
KH = 8
KW = 16
NA_RB = 4
NA_KR = KH + NA_RB - 1
NA_BLOCKS = GRID_ROWS // NA_RB
NA_QB = NA_RB * GRID_W
NA_KB = NA_KR * GRID_W
NA_TYPES = 3
S5_GROUPS = 32
S5_GROUP = 16
S5_STATE = 64
S5_CHUNKS = 4
CH_W = S5_WIDTH // S5_CHUNKS
ST_W = S5_GROUPS * S5_STATE // S5_CHUNKS
SCAN_BLOCKS = 8
SCAN_T = SEQ // SCAN_BLOCKS
RMS_EPS = 1e-6
NEG_INF = -1e30
ATT_SCALE = HEAD_DIM ** -0.5
ADAM_LR, ADAM_B1, ADAM_B2, ADAM_EPS, ADAM_WD, ADAM_STEP = 0.001, 0.9, 0.999, 1e-08, 0.01, 10
VMEM_LIMIT = 56 * 1024 * 1024
MESH = pl.DeviceIdType.MESH


def _params(sem=None):
    return pltpu.CompilerParams(dimension_semantics=sem, vmem_limit_bytes=VMEM_LIMIT)


def _dot(a, b):
    return jnp.dot(a, b, preferred_element_type=F32)


def _dot_nt(a, b):
    return lax.dot_general(a, b, (((1,), (1,)), ((), ())), preferred_element_type=F32)


def _dot_tn(a, b):
    return lax.dot_general(a, b, (((0,), (0,)), ((), ())), preferred_element_type=F32)


def _rstd(x):
    return lax.rsqrt(jnp.mean(x * x, axis=-1, keepdims=True) + RMS_EPS)


def _rms_bwd(x, r, g, dy):
    dyg = dy * g
    xr = x * r
    dx = r * (dyg - xr * jnp.mean(dyg * xr, axis=-1, keepdims=True))
    return dx, dy * xr


def _rows(i, size=ROW_TILE):
    return pl.ds(pl.multiple_of(i * size, 16), size)


def _row_spec(width):
    return pl.BlockSpec((ROW_TILE, width), lambda i: (i, 0))


def _fix_spec(shape):
    return pl.BlockSpec(shape, lambda i: (0,) * len(shape))


def _split3(x):
    hi = x.astype(BF16)
    r1 = x - hi.astype(F32)
    mid = r1.astype(BF16)
    lo = (r1 - mid.astype(F32)).astype(BF16)
    return hi, mid, lo


def _embed_prenorm(meta, x, g):
    def body(m_ref, x_ref, g_ref, h_ref, a_ref):
        h_ref[0:N_META, :] = m_ref[...]
        h_ref[N_META:, :] = x_ref[...]
        for i in range(N_ROW_TILES):
            rows = slice(i * ROW_TILE, (i + 1) * ROW_TILE)
            hv = h_ref[rows, :]
            a_ref[rows, :] = (hv * _rstd(hv) * g_ref[...]).astype(BF16)

    return pl.pallas_call(
        body, out_shape=[SDS((SEQ, D_MODEL), F32), SDS((SEQ, D_MODEL), BF16)], name="embed_prenorm",
        compiler_params=_params())(meta, x, g)


def _post_pre(f, hres, g_post, g_next, scale, name):
    def body(f_ref, h_ref, gp_ref, gn_ref, ho_ref, a_ref):
        fv = f_ref[...]
        h = h_ref[...] + scale * (fv * _rstd(fv) * gp_ref[...])
        ho_ref[...] = h
        a_ref[...] = (h * _rstd(h) * gn_ref[...]).astype(BF16)

    return pl.pallas_call(
        body, grid=(N_ROW_TILES,),
        in_specs=[_row_spec(D_MODEL), _row_spec(D_MODEL), _fix_spec((1, D_MODEL)), _fix_spec((1, D_MODEL))],
        out_specs=[_row_spec(D_MODEL), _row_spec(D_MODEL)],
        out_shape=[SDS((SEQ, D_MODEL), F32), SDS((SEQ, D_MODEL), BF16)], name=name,
        compiler_params=_params(("parallel",)))(f, hres, g_post, g_next)


def _final_loss(f2, h2, g_post, g_final, target):
    def body(f_ref, h_ref, gp_ref, gf_ref, t_ref, loss_ref, dh_ref, df_ref, dgf_ref, dgp_ref):
        i = pl.program_id(0)
        fv = f_ref[...]
        r1 = _rstd(fv)
        gp = gp_ref[...]
        h3 = h_ref[...] + 0.5 * (fv * r1 * gp)
        r2 = _rstd(h3)
        gf = gf_ref[...]
        y = h3 * r2 * gf
        row = lax.broadcasted_iota(jnp.int32, (ROW_TILE, 1), 0) + i * ROW_TILE
        err = jnp.where(row >= N_META, y - t_ref[...], 0.0)
        part = 0.5 * jnp.sum(jnp.mean(err * err, axis=-1, keepdims=True))
        dy = err * (1.0 / D_MODEL)
        dh3, dgf = _rms_bwd(h3, r2, gf, dy)
        dh_ref[...] = dh3
        df, dgp = _rms_bwd(fv, r1, gp, 0.5 * dh3)
        df_ref[...] = df.astype(BF16)

        @pl.when(i == 0)
        def _():
            loss_ref[...] = jnp.zeros_like(loss_ref)
            dgf_ref[...] = jnp.zeros_like(dgf_ref)
            dgp_ref[...] = jnp.zeros_like(dgp_ref)

        loss_ref[...] += part
        dgf_ref[...] += jnp.sum(dgf, axis=0, keepdims=True)
        dgp_ref[...] += jnp.sum(dgp, axis=0, keepdims=True)

    gain = _fix_spec((1, D_MODEL))
    return pl.pallas_call(
        body, grid=(N_ROW_TILES,),
        in_specs=[_row_spec(D_MODEL), _row_spec(D_MODEL), gain, gain, _row_spec(D_MODEL)],
        out_specs=[_fix_spec((8, 128)), _row_spec(D_MODEL), _row_spec(D_MODEL), gain, gain],
        out_shape=[SDS((8, 128), F32), SDS((SEQ, D_MODEL), F32), SDS((SEQ, D_MODEL), BF16),
                   SDS((1, D_MODEL), F32), SDS((1, D_MODEL), F32)],
        name="final_loss", compiler_params=_params(("arbitrary",)))(f2, h2, g_post, g_final, target)


def _bwd_pre_post(da, h, g_pre, dh_res, fprev, g_post, scale, name):
    def body(da_ref, h_ref, gpre_ref, dhr_ref, f_ref, gpost_ref, dh_ref, df_ref, dgpre_ref, dgpost_ref):
        i = pl.program_id(0)
        hv = h_ref[...]
        dxa, dgpre = _rms_bwd(hv, _rstd(hv), gpre_ref[...], da_ref[...])
        dh = dhr_ref[...] + dxa
        dh_ref[...] = dh
        fv = f_ref[...]
        df, dgpost = _rms_bwd(fv, _rstd(fv), gpost_ref[...], scale * dh)
        df_ref[...] = df.astype(BF16)

        @pl.when(i == 0)
        def _():
            dgpre_ref[...] = jnp.zeros_like(dgpre_ref)
            dgpost_ref[...] = jnp.zeros_like(dgpost_ref)

        dgpre_ref[...] += jnp.sum(dgpre, axis=0, keepdims=True)
        dgpost_ref[...] += jnp.sum(dgpost, axis=0, keepdims=True)

    gain = _fix_spec((1, D_MODEL))
    row = _row_spec(D_MODEL)
    return pl.pallas_call(
        body, grid=(N_ROW_TILES,), in_specs=[row, row, gain, row, row, gain],
        out_specs=[row, row, gain, gain],
        out_shape=[SDS((SEQ, D_MODEL), F32), SDS((SEQ, D_MODEL), BF16), SDS((1, D_MODEL), F32), SDS((1, D_MODEL), F32)],
        name=name, compiler_params=_params(("arbitrary",)))(da, h, g_pre, dh_res, fprev, g_post)


def _bwd_embed(da, h, g_pre, dh_res):
    def body(da_ref, h_ref, gpre_ref, dhr_ref, gx_ref, gm_ref, dgpre_ref):
        total = jnp.zeros((1, D_MODEL), F32)
        for i in range(N_ROW_TILES):
            rows = slice(i * ROW_TILE, (i + 1) * ROW_TILE)
            hv = h_ref[rows, :]
            dxa, dgpre = _rms_bwd(hv, _rstd(hv), gpre_ref[...], da_ref[rows, :])
            dh = dhr_ref[rows, :] + dxa
            total = total + jnp.sum(dgpre, axis=0, keepdims=True)
            if i == 0:
                gm_ref[...] = dh[0:N_META, :]
                gx_ref[0:ROW_TILE - N_META, :] = dh[N_META:, :]
            else:
                gx_ref[i * ROW_TILE - N_META:(i + 1) * ROW_TILE - N_META, :] = dh
        dgpre_ref[...] = total

    return pl.pallas_call(
        body, out_shape=[SDS((N_TOK, D_MODEL), F32), SDS((N_META, D_MODEL), F32), SDS((1, D_MODEL), F32)],
        name="bwd_embed", compiler_params=_params())(da, h, g_pre, dh_res)


def _ffn_fwd(a, wg, wu, wd, name, after=()):
    def body(a_ref, wg_ref, wu_ref, wd_ref, *rest):
        gate_ref, up_ref, f_ref = rest[len(after):]
        j = pl.program_id(0)

        def tile(i, carry):
            rows = _rows(i)
            at = a_ref[rows, :]
            gate = _dot_nt(at, wg_ref[...])
            up = _dot_nt(at, wu_ref[...])
            gate_ref[rows, :] = gate.astype(BF16)
            up_ref[rows, :] = up.astype(BF16)
            act = (gate * jax.nn.sigmoid(gate) * up).astype(BF16)
            contrib = _dot(act, wd_ref[...])

            @pl.when(j == 0)
            def _():
                f_ref[rows, :] = contrib

            @pl.when(j != 0)
            def _():
                f_ref[rows, :] += contrib

            return carry

        lax.fori_loop(0, N_ROW_TILES, tile, 0)

    wtile = pl.BlockSpec((FF_TILE, D_MODEL), lambda j: (j, 0))
    hid = pl.BlockSpec((SEQ, FF_TILE), lambda j: (0, j))
    full = pl.BlockSpec((SEQ, D_MODEL), lambda j: (0, 0))
    return pl.pallas_call(
        body, grid=(D_FF // FF_TILE,), in_specs=[full, wtile, wtile, wtile] + [pl.BlockSpec(memory_space=pl.ANY)] * len(after),
        out_specs=[hid, hid, full],
        out_shape=[SDS((SEQ, D_FF), BF16), SDS((SEQ, D_FF), BF16), SDS((SEQ, D_MODEL), F32)],
        name=name, compiler_params=_params(("arbitrary",)))(a, wg, wu, wd, *after)


def _ffn_bwd(df, a, gate, up, wg, wu, wd, name):
    def body(df_ref, a_ref, gate_ref, up_ref, wg_ref, wu_ref, wd_ref, da_ref, dwg_ref, dwu_ref, dwd_ref,
             acc_g, acc_u, acc_d):
        j = pl.program_id(0)

        def tile(i, carry):
            rows = _rows(i)
            dft = df_ref[rows, :]
            at = a_ref[rows, :]
            gate = gate_ref[rows, :].astype(F32)
            up = up_ref[rows, :].astype(F32)
            dact = _dot_nt(dft, wd_ref[...])
            sig = jax.nn.sigmoid(gate)
            silu = gate * sig
            dgate = (dact * up * (sig * (1.0 + gate * (1.0 - sig)))).astype(BF16)
            dup = (dact * silu).astype(BF16)
            act = (silu * up).astype(BF16)
            dwd = _dot_tn(act, dft)
            dwg = _dot_tn(dgate, at)
            dwu = _dot_tn(dup, at)
            dat = _dot(dgate, wg_ref[...]) + _dot(dup, wu_ref[...])

            @pl.when(i == 0)
            def _():
                acc_d[...] = dwd
                acc_g[...] = dwg
                acc_u[...] = dwu

            @pl.when(i != 0)
            def _():
                acc_d[...] += dwd
                acc_g[...] += dwg
                acc_u[...] += dwu

            @pl.when(j == 0)
            def _():
                da_ref[rows, :] = dat

            @pl.when(j != 0)
            def _():
                da_ref[rows, :] += dat

            return carry

        lax.fori_loop(0, N_ROW_TILES, tile, 0)
        dwg_ref[...] = acc_g[...].astype(BF16)
        dwu_ref[...] = acc_u[...].astype(BF16)
        dwd_ref[...] = acc_d[...].astype(BF16)

    wtile = pl.BlockSpec((FF_TILE, D_MODEL), lambda j: (j, 0))
    hid = pl.BlockSpec((SEQ, FF_TILE), lambda j: (0, j))
    full = pl.BlockSpec((SEQ, D_MODEL), lambda j: (0, 0))
    return pl.pallas_call(
        body, grid=(D_FF // FF_TILE,), in_specs=[full, full, hid, hid, wtile, wtile, wtile],
        out_specs=[full, wtile, wtile, wtile],
        out_shape=[SDS((SEQ, D_MODEL), F32)] + [SDS((D_FF, D_MODEL), BF16)] * 3,
        scratch_shapes=[pltpu.VMEM((FF_TILE, D_MODEL), F32)] * 3,
        name=name, compiler_params=_params(("arbitrary",)))(df, a, gate, up, wg, wu, wd)


HEADS_PER_BLOCK = IN_SHARD // HEAD_DIM
QKV_BLOCKS = 3 * NA_WIDTH // IN_SHARD


def _proj_heads(a, w):
    def body(a_ref, w_ref, o_ref):
        def tile(i, carry):
            rows = _rows(i)
            res = _dot(a_ref[rows, :], w_ref[...])
            for sub in range(HEADS_PER_BLOCK):
                o_ref[sub, rows, :] = res[:, sub * HEAD_DIM:(sub + 1) * HEAD_DIM]
            return carry

        lax.fori_loop(0, N_ROW_TILES, tile, 0)

    return pl.pallas_call(
        body, grid=(QKV_BLOCKS,),
        in_specs=[pl.BlockSpec((SEQ, D_MODEL), lambda j: (0, 0)), pl.BlockSpec((None, D_MODEL, IN_SHARD), lambda j: (j, 0, 0))],
        out_specs=pl.BlockSpec((HEADS_PER_BLOCK, SEQ, HEAD_DIM), lambda j: (j, 0, 0)),
        out_shape=SDS((3 * HEADS, SEQ, HEAD_DIM), F32), name="proj_heads",
        compiler_params=_params(("parallel",)))(a, w)


def _proj_u(a, w):
    def body(a_ref, w_ref, o_ref):
        def tile(i, carry):
            rows = _rows(i)
            o_ref[rows, :] = _dot(a_ref[rows, :], w_ref[...])
            return carry

        lax.fori_loop(0, N_ROW_TILES, tile, 0)

    return pl.pallas_call(
        body, grid=(N_DEV - QKV_BLOCKS,),
        in_specs=[pl.BlockSpec((SEQ, D_MODEL), lambda j: (0, 0)),
                  pl.BlockSpec((None, D_MODEL, IN_SHARD), lambda j: (j + QKV_BLOCKS, 0, 0))],
        out_specs=pl.BlockSpec((SEQ, IN_SHARD), lambda j: (0, j)),
        out_shape=SDS((SEQ, S5_WIDTH), F32), name="proj_u",
        compiler_params=_params(("parallel",)))(a, w)


def _proj_bwd(dq, dk, dv, du, a, w):
    def body(dq_ref, dk_ref, dv_ref, du_ref, a_ref, w_ref, da_ref, dw_ref, acc, dp_ref):
        j = pl.program_id(0)

        for which, src in enumerate((dq_ref, dk_ref, dv_ref)):
            @pl.when((j >= 2 * which) & (j < 2 * which + 2))
            def _(src=src):
                dp_ref[...] = jnp.concatenate([src[sub] for sub in range(HEADS_PER_BLOCK)], axis=-1).astype(BF16)

        @pl.when(j >= QKV_BLOCKS)
        def _():
            dp_ref[...] = du_ref[...].astype(BF16)

        def tile(i, carry):
            rows = _rows(i)
            dpt = dp_ref[rows, :]
            dw = _dot_tn(a_ref[rows, :], dpt)
            dat = _dot_nt(dpt, w_ref[...])

            @pl.when(i == 0)
            def _():
                acc[...] = dw

            @pl.when(i != 0)
            def _():
                acc[...] += dw

            @pl.when(j == 0)
            def _():
                da_ref[rows, :] = dat

            @pl.when(j != 0)
            def _():
                da_ref[rows, :] += dat

            return carry

        lax.fori_loop(0, N_ROW_TILES, tile, 0)
        dw_ref[...] = acc[...].astype(BF16)

    full = pl.BlockSpec((SEQ, D_MODEL), lambda j: (0, 0))
    wspec = pl.BlockSpec((None, D_MODEL, IN_SHARD), lambda j: (j, 0, 0))

    def heads(which):
        return pl.BlockSpec((HEADS_PER_BLOCK, SEQ, HEAD_DIM), lambda j: (jnp.clip(j - 2 * which, 0, 1), 0, 0))

    return pl.pallas_call(
        body, grid=(N_DEV,),
        in_specs=[heads(0), heads(1), heads(2),
                  pl.BlockSpec((SEQ, IN_SHARD), lambda j: (0, jnp.clip(j - QKV_BLOCKS, 0, 1))), full, wspec],
        out_specs=[full, wspec],
        out_shape=[SDS((SEQ, D_MODEL), F32), SDS((N_DEV, D_MODEL, IN_SHARD), BF16)],
        scratch_shapes=[pltpu.VMEM((D_MODEL, IN_SHARD), F32), pltpu.VMEM((SEQ, IN_SHARD), BF16)],
        name="proj_bwd", compiler_params=_params(("arbitrary",)))(dq, dk, dv, du, a, w)


def _na_consts():
    c = np.arange(GRID_W)
    col_start = np.clip(c - KW // 2, 0, GRID_W - KW)
    col_in = (c[None, :] >= col_start[:, None]) & (c[None, :] < col_start[:, None] + KW)
    dc = np.clip(c[None, :] - c[:, None] + KW - 1, 0, 2 * KW - 2)
    onehot = np.zeros((128, GRID_W * GRID_W), np.float32)
    qq, kk = np.meshgrid(c, c, indexing="ij")
    onehot[dc[col_in], (qq * GRID_W + kk)[col_in]] = 1.0
    negmask = np.where(col_in, 0.0, NEG_INF).astype(np.float32).reshape(1, -1)
    return onehot, negmask


def _na_pair(block_type, a, b):
    if block_type == 0:
        return b - a + KH - 1 if b < KH else None
    if block_type == 1:
        return b - a + KH // 2 - 1 if a <= b < a + KH else None
    return b - a if b >= NA_KR - KH else None


def _rpb_expand(rpb):
    onehot, negmask = _na_consts()
    rows = HEADS * (2 * KH - 1)
    rpb_pad = jnp.pad(rpb.reshape(rows, 2 * KW - 1), ((0, 128 - rows), (0, 128 - (2 * KW - 1))))

    def body(r_ref, oh_ref, m_ref, t_ref):
        hi, mid, lo = _split3(r_ref[...])
        oh = oh_ref[...]
        t_ref[...] = _dot(hi, oh) + _dot(mid, oh) + _dot(lo, oh) + m_ref[...]

    table = pl.pallas_call(body, out_shape=SDS((128, GRID_W * GRID_W), F32), name="rpb_expand",
                           compiler_params=_params())(rpb_pad, jnp.asarray(onehot, BF16), jnp.asarray(negmask))
    return table[:rows].reshape(HEADS, 2 * KH - 1, GRID_W, GRID_W)


def _rpb_reduce(dslabs):
    onehot, _ = _na_consts()
    rows = HEADS * (2 * KH - 1)

    def body(x_ref, oht_ref, o_ref):
        hi, mid, lo = _split3(x_ref[...])
        oht = oht_ref[...]
        o_ref[...] = _dot(hi, oht) + _dot(mid, oht) + _dot(lo, oht)

    out = pl.pallas_call(body, out_shape=SDS((rows, 128), F32), name="rpb_reduce", compiler_params=_params())(
        dslabs.reshape(rows, GRID_W * GRID_W), jnp.asarray(onehot.T, BF16))
    return out.reshape(HEADS, 2 * KH - 1, 128)


def _bias_tiles(slab_ref, tile_ref):
    tile_ref[...] = jnp.full(tile_ref.shape, NEG_INF, F32)
    for t in range(NA_TYPES):
        for a in range(NA_RB):
            for b in range(NA_KR):
                dr = _na_pair(t, a, b)
                if dr is not None:
                    tile_ref[t, a * GRID_W:(a + 1) * GRID_W, b * GRID_W:(b + 1) * GRID_W] = slab_ref[dr]


def _bias_tiles_bwd(dtile_ref, dslab_ref):
    acc = {}
    for t in range(NA_TYPES):
        for a in range(NA_RB):
            for b in range(NA_KR):
                dr = _na_pair(t, a, b)
                if dr is not None:
                    part = dtile_ref[t, a * GRID_W:(a + 1) * GRID_W, b * GRID_W:(b + 1) * GRID_W]
                    acc[dr] = part if dr not in acc else acc[dr] + part
    for dr in range(2 * KH - 1):
        dslab_ref[dr] = acc[dr]


def _block_geometry(g):
    start = jnp.clip(g * NA_RB - KH // 2, 0, GRID_ROWS - NA_KR)
    block_type = jnp.where(g == 0, 0, jnp.where(g == NA_BLOCKS - 1, 2, 1))
    q0 = pl.multiple_of(N_META + g * NA_QB, 16)
    k0 = pl.multiple_of(N_META + start * GRID_W, 16)
    return block_type, q0, k0


def _na_probs(q, kk, km, bias):
    s = _dot_nt(q, kk) * ATT_SCALE + bias
    sm = _dot_nt(q, km) * ATT_SCALE
    m = jnp.maximum(jnp.max(s, axis=-1, keepdims=True), jnp.max(sm, axis=-1, keepdims=True))
    p = jnp.exp(s - m)
    pm = jnp.exp(sm - m)
    inv = 1.0 / (jnp.sum(p, axis=-1, keepdims=True) + jnp.sum(pm, axis=-1, keepdims=True))
    return p * inv, pm * inv


def _meta_probs(qm, km):
    s = _dot_nt(qm, km) * ATT_SCALE
    p = jnp.exp(s - jnp.max(s, axis=-1, keepdims=True))
    return p / jnp.sum(p, axis=-1, keepdims=True)


def _qkv_specs():
    return [pl.BlockSpec((None, SEQ, HEAD_DIM), lambda h, which=which: (h + which * HEADS, 0, 0)) for which in range(3)]


def _na_fwd(qkv, bias):
    def body(q_ref, k_ref, v_ref, slab_ref, o_ref, b_ref):
        _bias_tiles(slab_ref, b_ref)
        km = k_ref[0:N_META, :].astype(BF16)
        vm = v_ref[0:N_META, :].astype(BF16)
        pmm = _meta_probs(q_ref[0:N_META, :].astype(BF16), km)
        o_ref[0:N_META, :] = _dot(pmm.astype(BF16), vm)

        def block(g, carry):
            block_type, q0, k0 = _block_geometry(g)
            qb = q_ref[pl.ds(q0, NA_QB), :].astype(BF16)
            kk = k_ref[pl.ds(k0, NA_KB), :].astype(BF16)
            vv = v_ref[pl.ds(k0, NA_KB), :].astype(BF16)
            p, pm = _na_probs(qb, kk, km, b_ref[block_type])
            o_ref[pl.ds(q0, NA_QB), :] = _dot(p.astype(BF16), vv) + _dot(pm.astype(BF16), vm)
            return carry

        lax.fori_loop(0, NA_BLOCKS, block, 0)

    head = pl.BlockSpec((None, SEQ, HEAD_DIM), lambda h: (h, 0, 0))
    return pl.pallas_call(
        body, grid=(HEADS,), in_specs=_qkv_specs() + [pl.BlockSpec((None, 2 * KH - 1, GRID_W, GRID_W), lambda h: (h, 0, 0, 0))],
        out_specs=head, out_shape=SDS((HEADS, SEQ, HEAD_DIM), F32), name="na_fwd",
        scratch_shapes=[pltpu.VMEM((NA_TYPES, NA_QB, NA_KB), F32)],
        compiler_params=_params(("parallel",)))(qkv, qkv, qkv, bias)


def _na_bwd(qkv, bias, do):
    def body(q_ref, k_ref, v_ref, slab_ref, do_ref, dq_ref, dk_ref, dv_ref, dslab_ref, b_ref, db_ref):
        _bias_tiles(slab_ref, b_ref)
        km = k_ref[0:N_META, :].astype(BF16)
        vm = v_ref[0:N_META, :].astype(BF16)
        dk_ref[...] = jnp.zeros_like(dk_ref)
        dv_ref[...] = jnp.zeros_like(dv_ref)
        db_ref[...] = jnp.zeros_like(db_ref)

        qm = q_ref[0:N_META, :].astype(BF16)
        dom = do_ref[0:N_META, :].astype(BF16)
        pmm = _meta_probs(qm, km)
        dpm = _dot_nt(dom, vm)
        dsm = (pmm * (dpm - jnp.sum(pmm * dpm, axis=-1, keepdims=True)) * ATT_SCALE).astype(BF16)
        dq_ref[0:N_META, :] = _dot(dsm, km)
        dkm0 = _dot_tn(dsm, qm)
        dvm0 = _dot_tn(pmm.astype(BF16), dom)

        def block(g, carry):
            dkm, dvm = carry
            block_type, q0, k0 = _block_geometry(g)
            qb = q_ref[pl.ds(q0, NA_QB), :].astype(BF16)
            kk = k_ref[pl.ds(k0, NA_KB), :].astype(BF16)
            vv = v_ref[pl.ds(k0, NA_KB), :].astype(BF16)
            dob = do_ref[pl.ds(q0, NA_QB), :].astype(BF16)
            p, pm = _na_probs(qb, kk, km, b_ref[block_type])
            dp = _dot_nt(dob, vv)
            dpm_ = _dot_nt(dob, vm)
            delta = jnp.sum(p * dp, axis=-1, keepdims=True) + jnp.sum(pm * dpm_, axis=-1, keepdims=True)
            ds = p * (dp - delta)
            dsm_ = pm * (dpm_ - delta)
            db_ref[block_type] += ds
            dsb = (ds * ATT_SCALE).astype(BF16)
            dsmb = (dsm_ * ATT_SCALE).astype(BF16)
            dq_ref[pl.ds(q0, NA_QB), :] = _dot(dsb, kk) + _dot(dsmb, km)
            dk_ref[pl.ds(k0, NA_KB), :] += _dot_tn(dsb, qb)
            dv_ref[pl.ds(k0, NA_KB), :] += _dot_tn(p.astype(BF16), dob)
            return dkm + _dot_tn(dsmb, qb), dvm + _dot_tn(pm.astype(BF16), dob)

        dkm, dvm = lax.fori_loop(0, NA_BLOCKS, block, (dkm0, dvm0))
        dk_ref[0:N_META, :] = dkm
        dv_ref[0:N_META, :] = dvm
        _bias_tiles_bwd(db_ref, dslab_ref)

    head = pl.BlockSpec((None, SEQ, HEAD_DIM), lambda h: (h, 0, 0))
    bspec = pl.BlockSpec((None, 2 * KH - 1, GRID_W, GRID_W), lambda h: (h, 0, 0, 0))
    return pl.pallas_call(
        body, grid=(HEADS,), in_specs=_qkv_specs() + [bspec, head], out_specs=[head, head, head, bspec],
        out_shape=[SDS((HEADS, SEQ, HEAD_DIM), F32)] * 3 + [SDS((HEADS, 2 * KH - 1, GRID_W, GRID_W), F32)],
        scratch_shapes=[pltpu.VMEM((NA_TYPES, NA_QB, NA_KB), F32), pltpu.VMEM((NA_TYPES, NA_QB, NA_KB), F32)],
        name="na_bwd", compiler_params=_params(("parallel",)))(qkv, qkv, qkv, bias, do)


def _cmul(ar, ai, br, bi):
    return ar * br - ai * bi, ar * bi + ai * br


def _cpow(ar, ai, n):
    rr, ri = None, None
    br, bi = ar, ai
    while n:
        if n & 1:
            rr, ri = (br, bi) if rr is None else _cmul(rr, ri, br, bi)
        n >>= 1
        if n:
            br, bi = _cmul(br, bi, br, bi)
    return rr, ri


def _s5_prep(lr, li, logdt, bre, bim):
    def body(lr_ref, li_ref, dt_ref, br_ref, bi_ref, lbr_ref, lbi_ref, bbr_ref, bbi_ref):
        lr_, li_ = lr_ref[...], li_ref[...]
        dt = jnp.exp(dt_ref[...])
        mag = jnp.exp(lr_ * dt)
        lbr = mag * jnp.cos(li_ * dt)
        lbi = mag * jnp.sin(li_ * dt)
        lbr_ref[...] = lbr
        lbi_ref[...] = lbi
        den = lr_ * lr_ + li_ * li_
        xr = lbr - 1.0
        cr = (xr * lr_ + lbi * li_) / den
        ci = (lbi * lr_ - xr * li_) / den
        br, bi = br_ref[...], bi_ref[...]
        bbr_ref[...] = cr[:, None, :] * br - ci[:, None, :] * bi
        bbi_ref[...] = cr[:, None, :] * bi + ci[:, None, :] * br

    n = 2 * S5_GROUPS
    return pl.pallas_call(
        body, out_shape=[SDS((n, S5_STATE), F32)] * 2 + [SDS((n, S5_GROUP, S5_STATE), F32)] * 2,
        name="s5_prep", compiler_params=_params())(lr, li, logdt, bre, bim)


def _s5_prep_bwd(lr, li, logdt, bre, bim, dar, dai, dbbr, dbbi):
    def body(lr_ref, li_ref, dt_ref, br_ref, bi_ref, dar_ref, dai_ref, dbr_ref, dbi_ref,
             glr_ref, gli_ref, gdt_ref, gbr_ref, gbi_ref):
        lr_, li_ = lr_ref[...], li_ref[...]
        dt = jnp.exp(dt_ref[...])
        mag = jnp.exp(lr_ * dt)
        lbr = mag * jnp.cos(li_ * dt)
        lbi = mag * jnp.sin(li_ * dt)
        den = lr_ * lr_ + li_ * li_
        xr = lbr - 1.0
        cr = (xr * lr_ + lbi * li_) / den
        ci = (lbi * lr_ - xr * li_) / den
        br, bi = br_ref[...], bi_ref[...]
        dbr, dbi = dbr_ref[...], dbi_ref[...]
        gbr_ref[...] = cr[:, None, :] * dbr + ci[:, None, :] * dbi
        gbi_ref[...] = cr[:, None, :] * dbi - ci[:, None, :] * dbr
        gcr = jnp.sum(dbr * br + dbi * bi, axis=1)
        gci = jnp.sum(dbi * br - dbr * bi, axis=1)
        ilr, ili = lr_ / den, li_ / den
        tr, ti = _cmul(gcr, gci, ilr, ili)
        glbr = dar_ref[...] + tr
        glbi = dai_ref[...] + ti
        dr_, di_ = _cmul(tr, ti, cr, -ci)
        gwr, gwi = _cmul(glbr, glbi, lbr, -lbi)
        glr_ref[...] = gwr * dt - dr_
        gli_ref[...] = gwi * dt - di_
        gdt_ref[...] = jnp.sum(gwr * lr_ + gwi * li_, axis=-1, keepdims=True) * dt

    n = 2 * S5_GROUPS
    return pl.pallas_call(
        body, out_shape=[SDS((n, S5_STATE), F32)] * 2 + [SDS((n, 1), F32)] + [SDS((n, S5_GROUP, S5_STATE), F32)] * 2,
        name="s5_prep_bwd", compiler_params=_params())(lr, li, logdt, bre, bim, dar, dai, dbbr, dbbi)


def _scan_local(xr_ref, xi_ref, ar8, ai8, reverse):
    def step(i, carry):
        sr, si = carry
        idx = (SCAN_T - 1 - i) if reverse else i
        rows = pl.ds(pl.multiple_of(idx * SCAN_BLOCKS, SCAN_BLOCKS), SCAN_BLOCKS)
        nr = ar8 * sr - ai8 * si + xr_ref[rows, :]
        ni = ar8 * si + ai8 * sr + xi_ref[rows, :]
        xr_ref[rows, :] = nr
        xi_ref[rows, :] = ni
        return nr, ni

    z = jnp.zeros(ar8.shape, F32)
    return lax.fori_loop(0, SCAN_T, step, (z, z))


def _scan_carries(er, ei, atr, ati, reverse):
    row = lax.broadcasted_iota(jnp.int32, er.shape, 0)
    cr = jnp.zeros((1, er.shape[1]), F32)
    ci = cr
    outr = jnp.zeros(er.shape, F32)
    outi = outr
    order = range(SCAN_BLOCKS - 1, -1, -1) if reverse else range(SCAN_BLOCKS)
    for b in order:
        outr = jnp.where(row == b, cr, outr)
        outi = jnp.where(row == b, ci, outi)
        nr, ni = _cmul(atr, ati, cr, ci)
        cr, ci = nr + er[b:b + 1, :], ni + ei[b:b + 1, :]
    return outr, outi


def _scan_fixup(xr_ref, xi_ref, cr8, ci8, ar8, ai8, reverse, pair=None):
    tile = lambda idx: pl.ds(pl.multiple_of(idx * SCAN_BLOCKS, SCAN_BLOCKS), SCAN_BLOCKS)

    def fix(idx, pr, pi):
        fr, fi = _cmul(pr, pi, cr8, ci8)
        nr, ni = xr_ref[tile(idx), :] + fr, xi_ref[tile(idx), :] + fi
        xr_ref[tile(idx), :] = nr
        xi_ref[tile(idx), :] = ni
        return nr, ni

    if pair is None:
        def step(i, carry):
            pr, pi = carry
            fix((SCAN_T - 1 - i) if reverse else i, pr, pi)
            return _cmul(pr, pi, ar8, ai8)

        lax.fori_loop(0, SCAN_T, step, (ar8, ai8), unroll=2)
        return None

    sr_ref, si_ref = pair
    earlier = -1 if reverse else 1

    def step(i, carry):
        pr, pi, accr, acci = carry
        idx = (SCAN_T - 1 - i) if reverse else i
        nr, ni = fix(idx, pr, pi)
        qr, qi = _cmul(nr, ni, sr_ref[tile(idx + earlier), :], -si_ref[tile(idx + earlier), :])
        pr, pi = _cmul(pr, pi, ar8, ai8)
        return pr, pi, accr + qr, acci + qi

    z = jnp.zeros(ar8.shape, F32)
    pr, pi, accr, acci = lax.fori_loop(0, SCAN_T - 1, step, (ar8, ai8, z, z))
    edge, src, shift, empty = (0, SCAN_T - 1, 1, 0) if reverse else (SCAN_T - 1, 0, SCAN_BLOCKS - 1, SCAN_BLOCKS - 1)
    nr, ni = fix(edge, pr, pi)
    row = lax.broadcasted_iota(jnp.int32, ar8.shape, 0)
    spr = jnp.where(row == empty, 0.0, pltpu.roll(sr_ref[tile(src), :], shift, 0))
    spi = jnp.where(row == empty, 0.0, pltpu.roll(si_ref[tile(src), :], shift, 0))
    qr, qi = _cmul(nr, ni, spr, -spi)
    return jnp.sum(accr + qr, axis=0, keepdims=True), jnp.sum(acci + qi, axis=0, keepdims=True)


def _scan(xr_ref, xi_ref, ar, ai, reverse, pair=None):
    n = ar.shape[1]
    ar8 = jnp.broadcast_to(ar, (SCAN_BLOCKS, n))
    ai8 = jnp.broadcast_to(ai, (SCAN_BLOCKS, n))
    er, ei = _scan_local(xr_ref, xi_ref, ar8, ai8, reverse)
    atr, ati = _cpow(ar, ai, SCAN_T)
    cr8, ci8 = _scan_carries(er, ei, atr, ati, reverse)
    return _scan_fixup(xr_ref, xi_ref, cr8, ci8, ar8, ai8, reverse, pair)


def _s5_specs():
    chan = pl.BlockSpec((SEQ, CH_W), lambda c, d: (0, c))
    chan2 = pl.BlockSpec((None, SEQ, CH_W), lambda c, d: (d, 0, c))
    state = pl.BlockSpec((None, SEQ, ST_W), lambda c, d: (d, 0, c))
    bmat = pl.BlockSpec((None, None, CH_W, ST_W), lambda c, d: (d, c, 0, 0))
    cmat = pl.BlockSpec((None, None, ST_W, CH_W), lambda c, d: (d, c, 0, 0))
    avec = pl.BlockSpec((None, None, 1, ST_W), lambda c, d: (d, c, 0, 0))
    return chan, chan2, state, bmat, cmat, avec


def _scan_by_direction(xr_ref, xi_ref, ar, ai, d, adjoint, pair=None, da_out=None):
    for direction in range(2):
        @pl.when(d == direction)
        def _(direction=direction):
            res = _scan(xr_ref, xi_ref, ar, ai, adjoint != (direction == 1), pair)
            if pair is not None:
                da_out[0][...], da_out[1][...] = res


def _s5_scan_fwd(u, bre, bim, are, aim, cre, cim):
    def body(u_ref, bre_ref, bim_ref, are_ref, aim_ref, cre_ref, cim_ref, sr_ref, si_ref, y_ref):
        ub = u_ref[...].astype(BF16)
        sr_ref[...] = _dot(ub, bre_ref[...])
        si_ref[...] = _dot(ub, bim_ref[...])
        _scan_by_direction(sr_ref, si_ref, are_ref[...], aim_ref[...], pl.program_id(1), adjoint=False)
        y_ref[...] = _dot(sr_ref[...].astype(BF16), cre_ref[...]) - _dot(si_ref[...].astype(BF16), cim_ref[...])

    chan, chan2, state, bmat, cmat, avec = _s5_specs()
    return pl.pallas_call(
        body, grid=(S5_CHUNKS, 2), in_specs=[chan, bmat, bmat, avec, avec, cmat, cmat], out_specs=[state, state, chan2],
        out_shape=[SDS((2, SEQ, S5_GROUPS * S5_STATE), F32)] * 2 + [SDS((2, SEQ, S5_WIDTH), F32)],
        name="s5_scan_fwd", compiler_params=_params(("parallel", "parallel")))(u, bre, bim, are, aim, cre, cim)


def _diag_out(out_ref, full):
    for g in range(8):
        out_ref[g] = full[g * S5_GROUP:(g + 1) * S5_GROUP, g * S5_STATE:(g + 1) * S5_STATE]


def _s5_scan_bwd(dy, du_skip, u, sr, si, bre, bim, are, aim, cre, cim):
    def body(dy_ref, dus_ref, u_ref, sr_ref, si_ref, bre_ref, bim_ref, are_ref, aim_ref, cre_ref, cim_ref,
             du_ref, dbr_ref, dbi_ref, dcr_ref, dci_ref, dar_ref, dai_ref, gr_ref, gi_ref):
        d = pl.program_id(1)
        dyb = dy_ref[...].astype(BF16)
        gr_ref[...] = _dot_nt(dyb, cre_ref[...])
        gi_ref[...] = -_dot_nt(dyb, cim_ref[...])
        _diag_out(dcr_ref, _dot_tn(dyb, sr_ref[...].astype(BF16)))
        _diag_out(dci_ref, -_dot_tn(dyb, si_ref[...].astype(BF16)))
        _scan_by_direction(gr_ref, gi_ref, are_ref[...], -aim_ref[...], d, adjoint=True, pair=(sr_ref, si_ref),
                           da_out=(dar_ref, dai_ref))

        @pl.when(d == 0)
        def _():
            du_ref[...] = dus_ref[...]

        grb = gr_ref[...].astype(BF16)
        gib = gi_ref[...].astype(BF16)
        du_ref[...] += _dot_nt(grb, bre_ref[...]) + _dot_nt(gib, bim_ref[...])
        ub = u_ref[...].astype(BF16)
        _diag_out(dbr_ref, _dot_tn(ub, grb))
        _diag_out(dbi_ref, _dot_tn(ub, gib))

    chan, _, state, bmat, cmat, avec = _s5_specs()
    diag = pl.BlockSpec((None, None, 8, S5_GROUP, S5_STATE), lambda c, d: (d, c, 0, 0, 0))
    return pl.pallas_call(
        body, grid=(S5_CHUNKS, 2), in_specs=[chan, chan, chan, state, state, bmat, bmat, avec, avec, cmat, cmat],
        out_specs=[chan, diag, diag, diag, diag, avec, avec],
        out_shape=[SDS((SEQ, S5_WIDTH), F32)] + [SDS((2, S5_CHUNKS, 8, S5_GROUP, S5_STATE), F32)] * 4
                  + [SDS((2, S5_CHUNKS, 1, ST_W), F32)] * 2,
        scratch_shapes=[pltpu.VMEM((SEQ, ST_W), F32), pltpu.VMEM((SEQ, ST_W), F32)],
        name="s5_scan_bwd", compiler_params=_params(("parallel", "arbitrary")))(dy, du_skip, u, sr, si, bre, bim, are, aim, cre, cim)


_GELU_K = math.sqrt(2.0 / math.pi)
_GELU_C = 0.044715


def _gelu(x):
    t = jnp.tanh(_GELU_K * (x + _GELU_C * x * x * x))
    return 0.5 * x * (1.0 + t), t


def _s5_glu_fwd(u, y2, dskip, wglu, bglu):
    def body(u_ref, y0_ref, y1_ref, d_ref, w_ref, b_ref, o_ref, yp_ref):
        ypre = u_ref[...] * d_ref[...] + y0_ref[...] + y1_ref[...]
        yp_ref[...] = ypre
        y, _ = _gelu(ypre)
        z = _dot(y.astype(BF16), w_ref[...]) + b_ref[...]
        o_ref[...] = y * jax.nn.sigmoid(z)

    row = _row_spec(S5_WIDTH)
    vec = _fix_spec((1, S5_WIDTH))
    dir0 = pl.BlockSpec((None, ROW_TILE, S5_WIDTH), lambda i: (0, i, 0))
    dir1 = pl.BlockSpec((None, ROW_TILE, S5_WIDTH), lambda i: (1, i, 0))
    return pl.pallas_call(
        body, grid=(N_ROW_TILES,), in_specs=[row, dir0, dir1, vec, _fix_spec((S5_WIDTH, S5_WIDTH)), vec],
        out_specs=[row, row], out_shape=[SDS((SEQ, S5_WIDTH), F32)] * 2, name="s5_glu_fwd",
        compiler_params=_params(("parallel",)))(u, y2, y2, dskip, wglu, bglu)


def _s5_glu_bwd(do, ypre, u, dskip, wglu, bglu):
    def body(do_ref, yp_ref, u_ref, d_ref, w_ref, b_ref, dyp_ref, du_ref, dw_ref, db_ref, dd_ref):
        i = pl.program_id(0)
        ypre = yp_ref[...]
        y, t = _gelu(ypre)
        yb = y.astype(BF16)
        sg = jax.nn.sigmoid(_dot(yb, w_ref[...]) + b_ref[...])
        dov = do_ref[...]
        dz = dov * y * sg * (1.0 - sg)
        dzb = dz.astype(BF16)
        dy = dov * sg + _dot_nt(dzb, w_ref[...])
        dgelu = 0.5 * (1.0 + t) + 0.5 * ypre * (1.0 - t * t) * _GELU_K * (1.0 + 3.0 * _GELU_C * ypre * ypre)
        dyp = dy * dgelu
        dyp_ref[...] = dyp
        uv = u_ref[...]
        du_ref[...] = dyp * d_ref[...]

        @pl.when(i == 0)
        def _():
            dw_ref[...] = jnp.zeros_like(dw_ref)
            db_ref[...] = jnp.zeros_like(db_ref)
            dd_ref[...] = jnp.zeros_like(dd_ref)

        dw_ref[...] += _dot_tn(yb, dzb)
        db_ref[...] += jnp.sum(dz, axis=0, keepdims=True)
        dd_ref[...] += jnp.sum(dyp * uv, axis=0, keepdims=True)

    row = _row_spec(S5_WIDTH)
    vec = _fix_spec((1, S5_WIDTH))
    mat = _fix_spec((S5_WIDTH, S5_WIDTH))
    return pl.pallas_call(
        body, grid=(N_ROW_TILES,), in_specs=[row, row, row, vec, mat, vec], out_specs=[row, row, mat, vec, vec],
        out_shape=[SDS((SEQ, S5_WIDTH), F32)] * 2 + [SDS((S5_WIDTH, S5_WIDTH), F32), SDS((1, S5_WIDTH), F32), SDS((1, S5_WIDTH), F32)],
        name="s5_glu_bwd", compiler_params=_params(("arbitrary",)))(do, ypre, u, dskip, wglu, bglu)


def _heads_side_by_side(o_ref):
    return jnp.concatenate([o_ref[h] for h in range(HEADS)], axis=-1)


def _mix_out_fwd(ona, os5, g_na, g_s5, wout):
    def body(a_ref, s_ref, ga_ref, gs_ref, w_ref, o_ref):
        av, sv = _heads_side_by_side(a_ref), s_ref[...]
        ca = (av * _rstd(av) * ga_ref[...]).astype(BF16)
        cs = (sv * _rstd(sv) * gs_ref[...]).astype(BF16)
        o_ref[...] = _dot(ca, w_ref[0:NA_WIDTH, :]) + _dot(cs, w_ref[NA_WIDTH:, :])

    row = _row_spec(NA_WIDTH)
    vec = _fix_spec((1, NA_WIDTH))
    heads = pl.BlockSpec((HEADS, ROW_TILE, HEAD_DIM), lambda i: (0, i, 0))
    return pl.pallas_call(
        body, grid=(N_ROW_TILES,), in_specs=[heads, row, vec, vec, _fix_spec((D_MODEL, D_MODEL))],
        out_specs=_row_spec(D_MODEL), out_shape=SDS((SEQ, D_MODEL), F32), name="mix_out_fwd",
        compiler_params=_params(("parallel",)))(ona, os5, g_na, g_s5, wout)


def _mix_out_bwd(dmix, ona, os5, g_na, g_s5, wout):
    def body(dm_ref, a_ref, s_ref, ga_ref, gs_ref, w_ref, da_ref, ds_ref, dw_ref, dga_ref, dgs_ref):
        i = pl.program_id(0)
        dm = dm_ref[...]
        av, sv = _heads_side_by_side(a_ref), s_ref[...]
        ra, rs = _rstd(av), _rstd(sv)
        ga, gs = ga_ref[...], gs_ref[...]
        ca = (av * ra * ga).astype(BF16)
        cs = (sv * rs * gs).astype(BF16)
        dca = _dot_nt(dm, w_ref[0:NA_WIDTH, :])
        dcs = _dot_nt(dm, w_ref[NA_WIDTH:, :])
        da, dga = _rms_bwd(av, ra, ga, dca)
        ds, dgs = _rms_bwd(sv, rs, gs, dcs)
        for h in range(HEADS):
            da_ref[h] = da[:, h * HEAD_DIM:(h + 1) * HEAD_DIM]
        ds_ref[...] = ds

        @pl.when(i == 0)
        def _():
            dw_ref[...] = jnp.zeros_like(dw_ref)
            dga_ref[...] = jnp.zeros_like(dga_ref)
            dgs_ref[...] = jnp.zeros_like(dgs_ref)

        dw_ref[0:NA_WIDTH, :] += _dot_tn(ca, dm)
        dw_ref[NA_WIDTH:, :] += _dot_tn(cs, dm)
        dga_ref[...] += jnp.sum(dga, axis=0, keepdims=True)
        dgs_ref[...] += jnp.sum(dgs, axis=0, keepdims=True)

    row = _row_spec(NA_WIDTH)
    vec = _fix_spec((1, NA_WIDTH))
    mat = _fix_spec((D_MODEL, D_MODEL))
    heads = pl.BlockSpec((HEADS, ROW_TILE, HEAD_DIM), lambda i: (0, i, 0))
    return pl.pallas_call(
        body, grid=(N_ROW_TILES,), in_specs=[_row_spec(D_MODEL), heads, row, vec, vec, mat],
        out_specs=[heads, row, mat, vec, vec],
        out_shape=[SDS((HEADS, SEQ, HEAD_DIM), F32), SDS((SEQ, NA_WIDTH), F32), SDS((D_MODEL, D_MODEL), F32),
                   SDS((1, NA_WIDTH), F32), SDS((1, NA_WIDTH), F32)],
        name="mix_out_bwd", compiler_params=_params(("arbitrary",)))(dmix, ona, os5, g_na, g_s5, wout)


def _me():
    x, y, c = lax.axis_index("x"), lax.axis_index("y"), lax.axis_index("c")
    return x, y, c, 4 * x + 2 * y + c


def _peer(k):
    x, y, c, _ = _me()
    px = 1 - x if (k >> 2) & 1 else x
    py = 1 - y if (k >> 1) & 1 else y
    pc = 1 - c if k & 1 else c
    return (px, py, pc), 4 * px + 2 * py + pc


ALL_PEERS = (1, 2, 3, 4, 5, 6, 7)
CHIP_PEERS = (2, 4, 6)
SIBLING = 1


def _slot8(pos):
    return 4 * pos[0] + 2 * pos[1] + pos[2]


def _slot4(pos):
    return 2 * pos[0] + pos[1]


_HBM = pl.BlockSpec(memory_space=pltpu.HBM)
_SEM = pl.BlockSpec(memory_space=pltpu.SEMAPHORE)
_EFFECT = pltpu.SideEffectType.DATAFLOW_SIDE_EFFECTING


def _exchange_start(arrays, lands, gather, name, peers=ALL_PEERS, slot=_slot8, own=True):
    n = len(arrays)

    def body(*refs):
        ins, lnd = refs[:n], refs[n:2 * n]
        send_sems, recv_sems = refs[2 * n], refs[2 * n + 1]
        token = refs[-1]
        me = slot(_me()[:3])
        for i, k in enumerate(peers):
            peer, _ = _peer(k)
            for a in range(n):
                src = ins[a] if gather else ins[a].at[slot(peer)]
                s = a * len(peers) + i
                pltpu.make_async_remote_copy(src_ref=src, dst_ref=lnd[a].at[me], send_sem=send_sems.at[s],
                                             recv_sem=recv_sems.at[s], device_id=peer, device_id_type=MESH).start()
        if own:
            for a in range(n):
                pltpu.make_async_copy(ins[a] if gather else ins[a].at[me], lnd[a].at[me], recv_sems.at[n * len(peers) + a]).start()
        token[...] = jnp.zeros_like(token)

    sems = pltpu.SemaphoreType.DMA((n * (len(peers) + int(own)),))
    out = pl.pallas_call(
        body, name=name, in_specs=[_HBM] * (2 * n),
        out_shape=(sems, sems) + tuple(pltpu.HBM(a.shape, a.dtype) for a in list(arrays) + list(lands)) + (SDS((8, 128), F32),),
        out_specs=(_SEM, _SEM) + (_HBM,) * (2 * n) + (pl.BlockSpec(memory_space=pltpu.VMEM),),
        input_output_aliases={i: 2 + i for i in range(2 * n)},
        compiler_params=pltpu.CompilerParams(has_side_effects=_EFFECT),
    )(*[pltpu.with_memory_space_constraint(a, pltpu.HBM) for a in list(arrays) + list(lands)])
    return out[0], out[1], list(out[2:2 + n]), list(out[2 + n:2 + 2 * n]), out[-1]


def _exchange_wait(send_sems, recv_sems, arrays, lands, after, gather, name, peers=ALL_PEERS, slot=_slot8, own=True):
    n = len(arrays)

    def body(*refs):
        ins, lnd = refs[:n], refs[n:2 * n]
        send_sems, recv_sems = refs[2 * n], refs[2 * n + 1]
        if own:
            me = slot(_me()[:3])
            for a in range(n):
                pltpu.make_async_copy(ins[a] if gather else ins[a].at[me], lnd[a].at[me], recv_sems.at[n * len(peers) + a]).wait()
        for i, k in enumerate(peers):
            peer, _ = _peer(k)
            for a in range(n):
                src = ins[a] if gather else ins[a].at[slot(peer)]
                s = a * len(peers) + i
                cp = pltpu.make_async_remote_copy(src_ref=src, dst_ref=lnd[a].at[slot(peer)], send_sem=send_sems.at[s],
                                                  recv_sem=recv_sems.at[s], device_id=peer, device_id_type=MESH)
                cp.wait_send()
                cp.wait_recv()

        refs[-1][...] = jnp.zeros_like(refs[-1])

    after = list(after) if isinstance(after, (list, tuple)) else [after]
    out = pl.pallas_call(
        body, name=name, in_specs=[_HBM] * (2 * n) + [_SEM, _SEM] + [pl.BlockSpec(memory_space=pl.ANY)] * len(after),
        out_shape=tuple(pltpu.HBM(a.shape, a.dtype) for a in list(arrays) + list(lands)) + (SDS((8, 128), F32),),
        out_specs=(_HBM,) * (2 * n) + (pl.BlockSpec(memory_space=pltpu.VMEM),), input_output_aliases={i: i for i in range(2 * n)},
        compiler_params=pltpu.CompilerParams(has_side_effects=_EFFECT),
    )(*arrays, *lands, send_sems, recv_sems, *after)
    return list(out[n:2 * n]), out[-1]


def _forward_sibling(lands, name):
    n = len(lands)

    def body(*refs):
        outs = refs[n:2 * n]
        send_sems, recv_sems = refs[2 * n:]
        x, y, c, _ = _me()
        sends = []
        for i, k in enumerate(CHIP_PEERS):
            peer, _ = _peer(k)
            for a in range(n):
                rows = outs[a].at[_slot8(peer)]
                cp = pltpu.make_async_remote_copy(src_ref=rows, dst_ref=rows, send_sem=send_sems.at[a, i], recv_sem=recv_sems.at[a, i],
                                                  device_id=(x, y, 1 - c), device_id_type=MESH)
                cp.start()
                sends.append(cp)
        for i, k in enumerate(CHIP_PEERS):
            (px, py, pc), _ = _peer(k)
            for a in range(n):
                rows = outs[a].at[_slot8((px, py, 1 - pc))]
                pltpu.make_async_remote_copy(src_ref=rows, dst_ref=rows, send_sem=send_sems.at[a, i], recv_sem=recv_sems.at[a, i],
                                             device_id=(x, y, 1 - c), device_id_type=MESH).wait_recv()
        for cp in sends:
            cp.wait_send()

    return pl.pallas_call(
        body, in_specs=[_HBM] * n, out_specs=[_HBM] * n, out_shape=[SDS(a.shape, a.dtype) for a in lands],
        input_output_aliases={i: i for i in range(n)},
        scratch_shapes=[pltpu.SemaphoreType.DMA((n, len(CHIP_PEERS))), pltpu.SemaphoreType.DMA((n, len(CHIP_PEERS)))],
        name=name)(*lands)


def _swap_sibling(arrays, name, after=()):
    n, n_after = len(arrays), len(after)
    chips = N_DEV // 2

    def body(*refs):
        ins, outs = refs[:n], refs[n + n_after:2 * n + n_after]
        send_sems, recv_sems = refs[2 * n + n_after:]
        x, y, c, _ = _me()
        sends = []
        for q in range(chips):
            for a in range(n):
                cp = pltpu.make_async_remote_copy(src_ref=ins[a].at[q, 1 - c], dst_ref=outs[a].at[q], send_sem=send_sems.at[a, q],
                                                  recv_sem=recv_sems.at[a, q], device_id=(x, y, 1 - c), device_id_type=MESH)
                cp.start()
                sends.append(cp)
        for cp in sends:
            cp.wait_recv()
        for cp in sends:
            cp.wait_send()

    return pl.pallas_call(
        body, in_specs=[_HBM] * n + [pl.BlockSpec(memory_space=pl.ANY)] * n_after, out_specs=[_HBM] * n,
        out_shape=[SDS((chips,) + a.shape[2:], a.dtype) for a in arrays],
        scratch_shapes=[pltpu.SemaphoreType.DMA((n, chips)), pltpu.SemaphoreType.DMA((n, chips))], name=name)(*arrays, *after)


def _sum_pairs(mine, theirs, name):
    n = len(mine)
    chips = mine[0].shape[0]
    c = lax.axis_index("c")

    def body(c_ref, *refs):
        for a in range(n):
            refs[2 * n + a][...] = (refs[a][...].astype(F32) + refs[n + a][...].astype(F32)).astype(refs[2 * n + a].dtype)

    def pair(a):
        return pl.BlockSpec((None, None) + a.shape[2:], lambda q, c_ref: (q, c_ref[0], 0, 0))

    def single(a):
        return pl.BlockSpec((None,) + a.shape[2:], lambda q, c_ref: (q, 0, 0))

    return pl.pallas_call(
        body, grid_spec=pltpu.PrefetchScalarGridSpec(
            num_scalar_prefetch=1, grid=(chips,), in_specs=[pair(a) for a in mine] + [single(a) for a in mine],
            out_specs=[single(a) for a in mine]),
        out_shape=[SDS((chips,) + a.shape[2:], a.dtype) for a in mine], name=name,
        compiler_params=_params(("parallel",)))(c.reshape(1).astype(jnp.int32), *mine, *theirs)


def _adamw_math(w, g, m, v):
    m = ADAM_B1 * m + (1.0 - ADAM_B1) * g
    v = ADAM_B2 * v + (1.0 - ADAM_B2) * (g * g)
    m_hat = m / (1.0 - ADAM_B1 ** ADAM_STEP)
    v_hat = v / (1.0 - ADAM_B2 ** ADAM_STEP)
    delta = -ADAM_LR * (m_hat / (jnp.sqrt(v_hat) + ADAM_EPS) + ADAM_WD * w)
    return delta, m, v


def _adamw(w, m, v, pieces, name):
    rows, cols = w.shape[-2:]
    lead = w.ndim - 2
    tile = rows
    for cand in (256, 176, 128, 64, 16):
        if rows > cand and rows % cand == 0:
            tile = cand
            break

    def body(w_ref, m_ref, v_ref, p_ref, g_ref, d_ref, mo_ref, vo_ref):
        g = _sum_pieces(p_ref)
        g_ref[...] = g
        d_ref[...], mo_ref[...], vo_ref[...] = _adamw_math(w_ref[...], g, m_ref[...], v_ref[...])

    blk = pl.BlockSpec((None,) * lead + (tile, cols), lambda i: (0,) * lead + (i, 0))
    return pl.pallas_call(
        body, grid=(rows // tile,), in_specs=[blk, blk, blk, pl.BlockSpec((pieces.shape[0], tile, cols), lambda i: (0, i, 0))],
        out_specs=[blk] * 4, out_shape=[SDS(w.shape, F32)] * 4, name=name,
        compiler_params=_params(("parallel",)))(w, m, v, pieces)


def _sum_pieces(p_ref):
    g = p_ref[0].astype(F32)
    for p in range(1, p_ref.shape[0]):
        g = g + p_ref[p].astype(F32)
    return g


def _adamw_s5_mat(wmvg, name):
    n = len(wmvg)
    _, ndir, groups, b, c = wmvg[0][0].shape
    per_dir = groups // 8

    def body(*refs):
        for p in range(n):
            w_ref, m_ref, v_ref, g_ref = refs[4 * p:4 * p + 4]
            d_ref, mo_ref, vo_ref = refs[4 * n + 3 * p:4 * n + 3 * p + 3]
            d_ref[...], mo_ref[...], vo_ref[...] = _adamw_math(w_ref[...], g_ref[...], m_ref[...], v_ref[...])

    blk = pl.BlockSpec((None, None, 8, b, c), lambda i: (0, i // per_dir, i % per_dir, 0, 0))
    return pl.pallas_call(
        body, grid=(ndir * per_dir,), in_specs=[blk] * (4 * n), out_specs=[blk] * (3 * n),
        out_shape=[SDS(wmvg[0][0].shape, F32)] * (3 * n), name=name,
        compiler_params=_params(("parallel",)))(*[x for group in wmvg for x in group])


VEC_ROWS = ['ffn1_pre_g', 'ffn1_post_g', 'mix_pre_g', 'mix_post_g', 'ffn2_pre_g', 'ffn2_post_g', 'final_g',
            ('na_out_g', 's5_out_g'), ('s5_d', 's5_b_glu')]
VEC_NAMES = [n for row in VEC_ROWS for n in ((row,) if isinstance(row, str) else row)]
VEC_PACK_ROWS = 16
LOSS_ROW = len(VEC_ROWS)


def _pack_vectors(grads, loss8):
    def body(*refs):
        o_ref = refs[-1]
        o_ref[...] = jnp.zeros_like(o_ref)
        o_ref[LOSS_ROW:LOSS_ROW + 1, 0:128] = refs[-2][0:1, :]
        k = 0
        for i, row in enumerate(VEC_ROWS):
            if isinstance(row, str):
                o_ref[i:i + 1, :] = refs[k][...]
                k += 1
            else:
                o_ref[i:i + 1, 0:NA_WIDTH] = refs[k][...]
                o_ref[i:i + 1, NA_WIDTH:] = refs[k + 1][...]
                k += 2

    return pl.pallas_call(body, out_shape=SDS((VEC_PACK_ROWS, D_MODEL), F32), name="pack_vectors",
                          compiler_params=_params())(*[grads[n] for n in VEC_NAMES], loss8)


def _sum8(pieces, name):
    def body(p_ref, o_ref):
        o_ref[...] = _sum_pieces(p_ref)

    return pl.pallas_call(body, out_shape=SDS(pieces.shape[1:], F32), name=name, compiler_params=_params())(pieces)


def _adamw_small(packed8, vec_wmv, others):
    n_vec, n_oth = len(VEC_NAMES), len(others)

    def body(*refs):
        p_ref = refs[0]
        ins = refs[1:1 + 3 * n_vec + 4 * n_oth]
        outs = refs[1 + 3 * n_vec + 4 * n_oth:]
        gsum = _sum_pieces(p_ref)
        outs[-1][...] = gsum[LOSS_ROW:LOSS_ROW + 1, 0:128]
        k = 0
        for i, row in enumerate(VEC_ROWS):
            parts = [(row, gsum[i:i + 1, :])] if isinstance(row, str) else \
                [(row[0], gsum[i:i + 1, 0:NA_WIDTH]), (row[1], gsum[i:i + 1, NA_WIDTH:])]
            for _, g in parts:
                w_ref, m_ref, v_ref = ins[3 * k:3 * k + 3]
                outs[4 * k][...] = g
                outs[4 * k + 1][...], outs[4 * k + 2][...], outs[4 * k + 3][...] = _adamw_math(w_ref[...], g, m_ref[...], v_ref[...])
                k += 1
        for j in range(n_oth):
            w_ref, m_ref, v_ref, g_ref = ins[3 * n_vec + 4 * j:3 * n_vec + 4 * j + 4]
            g = _sum_pieces(g_ref)
            g = g[tuple(slice(0, s) for s in w_ref.shape[1:])].reshape(w_ref.shape)
            o = outs[4 * (n_vec + j):4 * (n_vec + j) + 4]
            o[0][...] = g
            o[1][...], o[2][...], o[3][...] = _adamw_math(w_ref[...], g, m_ref[...], v_ref[...])

    args, out_shape = [packed8], []
    for w, m, v in vec_wmv:
        args += [w, m, v]
        out_shape += [SDS(w.shape, F32)] * 4
    for w, m, v, g in others:
        args += [w, m, v, g]
        out_shape += [SDS(w.shape, F32)] * 4
    out_shape += [SDS((1, 128), F32)]
    return pl.pallas_call(body, out_shape=out_shape, name="adamw_small", compiler_params=_params())(*args)


def _perm_rows(x):
    return x.reshape(SCAN_BLOCKS, SCAN_T, x.shape[-1]).transpose(1, 0, 2).reshape(SEQ, x.shape[-1])


def _unperm_rows(x):
    return x.reshape(SCAN_T, SCAN_BLOCKS, x.shape[-1]).transpose(1, 0, 2).reshape(SEQ, x.shape[-1])


def _block_diag(x):
    eye = np.eye(8, dtype=bool)[None, None, :, None, :, None]
    full = jnp.where(eye, x[:, :, :, :, None, :], 0.0)
    return full.reshape(2, S5_CHUNKS, 8 * x.shape[3], 8 * x.shape[4])


STORED_SWAPPED = {"ffn1_w_gate": (1, 2), "ffn1_w_up": (1, 2), "ffn2_w_gate": (1, 2), "ffn2_w_up": (1, 2),
                  "s5_b_re": (3, 4), "s5_b_im": (3, 4)}


def _stored(name, x):
    return jnp.swapaxes(x, *STORED_SWAPPED[name]) if name in STORED_SWAPPED else x


def _dep(x, token):
    return x if token is None else x + token


def _local_step(x, target, get_w, small, emit):
    bias = _rpb_expand(small["na_rpb"][0])
    lr = small["s5_lam_re"].reshape(64, S5_STATE)
    li = small["s5_lam_im"].reshape(64, S5_STATE)
    logdt = small["s5_log_dt"].reshape(64, 1)
    b_t = [_stored(n, small[n]).reshape(64, S5_GROUP, S5_STATE) for n in ("s5_b_re", "s5_b_im")]
    lbr, lbi, bbr, bbi = _s5_prep(lr, li, logdt, b_t[0], b_t[1])
    are = lbr.reshape(2, S5_CHUNKS, 1, ST_W)
    aim = lbi.reshape(2, S5_CHUNKS, 1, ST_W)
    bre = _block_diag(bbr.reshape(2, S5_CHUNKS, 8, S5_GROUP, S5_STATE)).astype(BF16)
    bim = _block_diag(bbi.reshape(2, S5_CHUNKS, 8, S5_GROUP, S5_STATE)).astype(BF16)
    c_t = [small[n].reshape(2, S5_CHUNKS, 8, S5_GROUP, S5_STATE).transpose(0, 1, 2, 4, 3) for n in ("s5_c_re", "s5_c_im")]
    cre = _block_diag(c_t[0]).astype(BF16)
    cim = _block_diag(c_t[1]).astype(BF16)
    tgt = jnp.concatenate([jnp.zeros((N_META, D_MODEL), F32), target], axis=0)

    h0, a1 = _embed_prenorm(get_w("meta", None)["meta_tokens"], x, small["ffn1_pre_g"])
    wts = dict(get_w("ffn1", [bias, are, aim, bre, bim, cre, cim, tgt, a1]))
    gate1, up1, f1 = _ffn_fwd(a1, wts["ffn1_w_gate"], wts["ffn1_w_up"], wts["ffn1_w_down"], "ffn1_fwd",
                              after=wts.get("tokens", ()))
    h1, a2 = _post_pre(f1, h0, small["ffn1_post_g"], small["mix_pre_g"], 0.5, "post_pre1")
    wts.update(get_w("w_in", a2))
    qkv = _proj_heads(a2, wts["w_in"])
    u = _proj_u(a2, wts["w_in"])
    ona = _na_fwd(qkv, bias)
    u_p = _perm_rows(u)
    sr, si, y2 = _s5_scan_fwd(u_p, bre, bim, are, aim, cre, cim)
    wts.update(get_w("mix", y2))
    os5_p, ypre_p = _s5_glu_fwd(u_p, y2, small["s5_d"], wts["s5_w_glu"], small["s5_b_glu"])
    os5 = _unperm_rows(os5_p)

    mix = _mix_out_fwd(ona, os5, small["na_out_g"], small["s5_out_g"], wts["w_out"])
    h2, a3 = _post_pre(mix, h1, small["mix_post_g"], small["ffn2_pre_g"], 1.0, "post_pre2")
    wts.update(get_w("ffn2", a3))
    gate2, up2, f2 = _ffn_fwd(a3, wts["ffn2_w_gate"], wts["ffn2_w_up"], wts["ffn2_w_down"], "ffn2_fwd")
    loss8, dh3, df2, g_final, g_ffn2_post = _final_loss(f2, h2, small["ffn2_post_g"], small["final_g"], tgt)

    da3, dwg2, dwu2, dwd2 = _ffn_bwd(df2, a3, gate2, up2, wts["ffn2_w_gate"], wts["ffn2_w_up"], wts["ffn2_w_down"], "ffn2_bwd")
    tok = emit("ffn2", {"ffn2_w_gate": dwg2, "ffn2_w_up": dwu2, "ffn2_w_down": dwd2})
    dh2, dmix, g_ffn2_pre, g_mix_post = _bwd_pre_post(da3, h2, _dep(small["ffn2_pre_g"], tok), dh3, mix, small["mix_post_g"], 1.0,
                                                      "bwd_pre_post2")
    dona, dos5, dwout, g_na_out, g_s5_out = _mix_out_bwd(dmix, ona, os5, small["na_out_g"], small["s5_out_g"], wts["w_out"])

    dypre_p, du_skip_p, dwglu, g_b_glu, g_s5_d = _s5_glu_bwd(_perm_rows(dos5), ypre_p, u_p, small["s5_d"], wts["s5_w_glu"],
                                                             small["s5_b_glu"])
    tok = emit("mix", {"s5_w_glu": dwglu.reshape(N_DEV, S5_WIDTH // N_DEV, S5_WIDTH).astype(BF16),
                       "w_out": dwout.reshape(N_DEV, D_MODEL // N_DEV, D_MODEL).astype(BF16)})
    du_p, dbr, dbi, dcr, dci, dar, dai = _s5_scan_bwd(dypre_p, du_skip_p, u_p, sr, si, bre, bim, _dep(are, tok), aim, cre, cim)
    du = _unperm_rows(du_p)
    per_group = (2 * S5_GROUPS, S5_GROUP, S5_STATE)
    g_lr, g_li, g_dt, g_br, g_bi = _s5_prep_bwd(lr, li, logdt, b_t[0], b_t[1], dar.reshape(64, S5_STATE),
                                                dai.reshape(64, S5_STATE), dbr.reshape(per_group), dbi.reshape(per_group))
    g_c = [dcr.reshape(per_group), dci.reshape(per_group)]

    dq, dk, dv, dbias = _na_bwd(qkv, bias, dona)
    g_rpb = _rpb_reduce(dbias)
    dense = jnp.stack([g.reshape(2 * S5_GROUPS, S5_STATE * S5_GROUP) for g in (g_br, g_bi, *g_c)])
    tok = emit("small", {"dense": dense, "na_rpb": g_rpb,
                         "s5_lam_re": g_lr.reshape(2, S5_GROUPS, S5_STATE), "s5_lam_im": g_li.reshape(2, S5_GROUPS, S5_STATE),
                         "s5_log_dt": g_dt.reshape(2, S5_GROUPS)})
    da2, dwin = _proj_bwd(dq, dk, dv, du, a2, wts["w_in"])
    tok2 = emit("w_in", {"w_in": dwin})
    tok = tok if tok2 is None else tok + tok2
    dh1, df1, g_mix_pre, g_ffn1_post = _bwd_pre_post(da2, h1, _dep(small["mix_pre_g"], tok), dh2, f1, small["ffn1_post_g"], 0.5,
                                                     "bwd_pre_post1")
    da1, dwg1, dwu1, dwd1 = _ffn_bwd(df1, a1, gate1, up1, wts["ffn1_w_gate"], wts["ffn1_w_up"], wts["ffn1_w_down"], "ffn1_bwd")
    grad_x, grad_meta, g_ffn1_pre = _bwd_embed(da1, h0, small["ffn1_pre_g"], dh1)
    vec_g = {
        "ffn1_pre_g": g_ffn1_pre, "ffn1_post_g": g_ffn1_post, "mix_pre_g": g_mix_pre, "s5_d": g_s5_d, "s5_b_glu": g_b_glu,
        "na_out_g": g_na_out, "s5_out_g": g_s5_out, "mix_post_g": g_mix_post,
        "ffn2_pre_g": g_ffn2_pre, "ffn2_post_g": g_ffn2_post, "final_g": g_final,
    }
    emit("vec", {"packed": _pack_vectors(vec_g, loss8), "meta_tokens": grad_meta})
    emit("ffn1", {"ffn1_w_gate": dwg1, "ffn1_w_up": dwu1, "ffn1_w_down": dwd1})
    return grad_x


WEIGHT_NAMES = ['meta_tokens', 'ffn1_pre_g', 'ffn1_post_g', 'ffn1_w_gate', 'ffn1_w_up', 'ffn1_w_down', 'mix_pre_g', 'w_in',
                'na_rpb', 's5_lam_re', 's5_lam_im', 's5_log_dt', 's5_b_re', 's5_b_im', 's5_c_re', 's5_c_im', 's5_d',
                's5_w_glu', 's5_b_glu', 'na_out_g', 's5_out_g', 'w_out', 'mix_post_g', 'ffn2_pre_g', 'ffn2_post_g',
                'ffn2_w_gate', 'ffn2_w_up', 'ffn2_w_down', 'final_g']
BIG_NAMES = ['ffn1_w_gate', 'ffn1_w_up', 'ffn1_w_down', 'w_in', 's5_w_glu', 'w_out', 'ffn2_w_gate', 'ffn2_w_up', 'ffn2_w_down']
SMALL_NAMES = [n for n in WEIGHT_NAMES if n not in BIG_NAMES and n != 'meta_tokens']
WHOLE_NAMES = ['na_rpb', 's5_lam_re', 's5_lam_im', 's5_log_dt']
LEAD_NAMES = ['s5_b_re', 's5_b_im', 's5_c_re', 's5_c_im']


def kernel(x, meta_tokens, ffn1_pre_g, ffn1_post_g, ffn1_w_gate, ffn1_w_up, ffn1_w_down, mix_pre_g, w_in, na_rpb, s5_lam_re, s5_lam_im, s5_log_dt, s5_b_re, s5_b_im, s5_c_re, s5_c_im, s5_d, s5_w_glu, s5_b_glu, na_out_g, s5_out_g, w_out, mix_post_g, ffn2_pre_g, ffn2_post_g, ffn2_w_gate, ffn2_w_up, ffn2_w_down, final_g, loss_target, m_meta_tokens, m_ffn1_pre_g, m_ffn1_post_g, m_ffn1_w_gate, m_ffn1_w_up, m_ffn1_w_down, m_mix_pre_g, m_w_in, m_na_rpb, m_s5_lam_re, m_s5_lam_im, m_s5_log_dt, m_s5_b_re, m_s5_b_im, m_s5_c_re, m_s5_c_im, m_s5_d, m_s5_w_glu, m_s5_b_glu, m_na_out_g, m_s5_out_g, m_w_out, m_mix_post_g, m_ffn2_pre_g, m_ffn2_post_g, m_ffn2_w_gate, m_ffn2_w_up, m_ffn2_w_down, m_final_g, v_meta_tokens, v_ffn1_pre_g, v_ffn1_post_g, v_ffn1_w_gate, v_ffn1_w_up, v_ffn1_w_down, v_mix_pre_g, v_w_in, v_na_rpb, v_s5_lam_re, v_s5_lam_im, v_s5_log_dt, v_s5_b_re, v_s5_b_im, v_s5_c_re, v_s5_c_im, v_s5_d, v_s5_w_glu, v_s5_b_glu, v_na_out_g, v_s5_out_g, v_w_out, v_mix_post_g, v_ffn2_pre_g, v_ffn2_post_g, v_ffn2_w_gate, v_ffn2_w_up, v_ffn2_w_down, v_final_g):
    args = dict(locals())
    w = {n: args[n] for n in WEIGHT_NAMES}
    m = {n: args["m_" + n] for n in WEIGHT_NAMES}
    v = {n: args["v_" + n] for n in WEIGHT_NAMES}

    small = {n: w[n] for n in SMALL_NAMES}

    pending = {}

    def start(group, names, arrays, gather, peers=ALL_PEERS, slot=_slot8):
        n_slots = N_DEV if slot is _slot8 else N_DEV // 2
        lands = [lax.empty((n_slots,) + a.shape if gather else a.shape, a.dtype) for a in arrays]
        send_sems, recv_sems, arrays, lands, token = _exchange_start(arrays, lands, gather, "start_" + group, peers, slot)
        pending[group] = (names, send_sems, recv_sems, arrays, lands, gather, peers, slot)
        return token

    def finish(group, after):
        names, send_sems, recv_sems, arrays, lands, gather, peers, slot = pending.pop(group)
        lands, token = _exchange_wait(send_sems, recv_sems, arrays, lands, after, gather, "wait_" + group, peers, slot)
        return dict(zip(names, lands)), token

    first = ["ffn1_w_gate", "ffn1_w_up", "ffn1_w_down"]
    def shard(n, token=None):
        return _dep(_stored(n, w[n])[0], None if token is None else token[0, 0]).astype(BF16)

    ffn_names = ("ffn1_w_gate", "ffn1_w_up", "ffn1_w_down", "ffn2_w_gate", "ffn2_w_up", "ffn2_w_down")
    later_groups = (("w_in", ["w_in"]), ("mix", ["s5_w_glu", "w_out"]), ("ffn2", ["ffn2_w_gate", "ffn2_w_up", "ffn2_w_down"]))
    token0 = start("meta", ["meta_tokens"], [w["meta_tokens"]], True)
    token1 = start("ffn1", first, [shard(n, token0) for n in first], True, (SIBLING,) + CHIP_PEERS)
    meta_full = finish("meta", [token1])[0]["meta_tokens"].transpose(1, 0, 2).reshape(N_META, D_MODEL)
    later_shards = {n: shard(n, token1) for _, names in later_groups for n in names}
    for n in ("na_rpb", "s5_lam_re"):
        small[n] = _dep(small[n], token1[0, 0])

    def get_w(group, after):
        if group == "meta":
            return {"meta_tokens": meta_full}
        if group == "ffn1":
            after = list(after) + list(later_shards.values())
        got, token = finish(group, after)
        if group == "ffn1":
            got = dict(zip(got, _forward_sibling(list(got.values()), "forward_ffn1")))
            got["tokens"] = [start(g, names + ["order"], [later_shards[n] for n in names] + [token], True) for g, names in later_groups]
        if group == "mix":
            got = {"s5_w_glu": got["s5_w_glu"].reshape(S5_WIDTH, S5_WIDTH), "w_out": got["w_out"].reshape(D_MODEL, D_MODEL)}
        return {n: (a.reshape(D_FF, D_MODEL) if n in ffn_names else a) for n, a in got.items()}

    tokens = {}

    def emit(group, grads):
        grads = {n: (g.reshape(N_DEV, FF_SHARD, D_MODEL) if n in ffn_names else g) for n, g in grads.items()}
        if group == "ffn1":
            mine = [g.reshape((N_DEV // 2, 2) + g.shape[1:]) for g in grads.values()]
            theirs = _swap_sibling(mine, "swap_g_ffn1", after=[tokens["vec"]])
            sums = _sum_pairs(mine, theirs, "pair_sum_g_ffn1")
            tokens[group] = start("g_ffn1", list(grads), sums, False, CHIP_PEERS, _slot4)
        else:
            tokens[group] = start("g_" + group, list(grads), list(grads.values()), group in ("small", "vec"))
        return tokens[group][0, 0]

    grad_x = _local_step(x[0], loss_target[0], get_w, small, emit)
    res = {}

    def update_shard(n, pieces):
        outs = _adamw(_stored(n, w[n]), _stored(n, m[n]), _stored(n, v[n]), pieces, "adamw_" + n)
        res[n] = [_stored(n, o) for o in outs]

    late = [grad_x, tokens["ffn1"]]
    for group in ("g_ffn2", "g_mix", "g_w_in"):
        for n, pieces in finish(group, late)[0].items():
            update_shard(n, pieces)
    g8 = finish("g_small", late)[0]
    dense = _sum8(g8["dense"], "sum_dense")
    grads = [dense[i].reshape(_stored(n, w[n]).shape) for i, n in enumerate(LEAD_NAMES)]
    upd = _adamw_s5_mat([(_stored(n, w[n]), _stored(n, m[n]), _stored(n, v[n]), g) for n, g in zip(LEAD_NAMES, grads)],
                        "adamw_s5_matrices")
    for i, n in enumerate(LEAD_NAMES):
        res[n] = [_stored(n, o) for o in [grads[i]] + list(upd[3 * i:3 * i + 3])]

    done = [res[n][1] for n in ("ffn2_w_gate", "ffn2_w_up", "ffn2_w_down", "w_in", "w_out", "s5_w_glu") + tuple(LEAD_NAMES)]
    got = finish("g_vec", done)[0]
    packed8, gmeta8 = got["packed"], got["meta_tokens"]
    for n, pieces in finish("g_ffn1", packed8)[0].items():
        update_shard(n, pieces)
    _, _, _, me = _me()
    update_shard("meta_tokens", lax.dynamic_slice_in_dim(gmeta8, me * (D_MODEL // N_DEV), D_MODEL // N_DEV, axis=2))

    outs = _adamw_small(packed8, [(w[n], m[n], v[n]) for n in VEC_NAMES], [(w[n], m[n], v[n], g8[n]) for n in WHOLE_NAMES])
    for i, n in enumerate(VEC_NAMES + WHOLE_NAMES):
        res[n] = list(outs[4 * i:4 * i + 4])

    out = [outs[-1][0, 0], grad_x[None]]
    for kind in range(4):
        out += [res[n][kind] for n in WEIGHT_NAMES]
    return tuple(out)
```

```python
import math

import numpy as np
import jax
import jax.numpy as jnp
from jax import lax
from jax.experimental import pallas as pl
from jax.experimental.pallas import tpu as pltpu

F32 = jnp.float32
BF16 = jnp.bfloat16
SDS = jax.ShapeDtypeStruct

D_MODEL = 1024
N_TOK = 2048
N_META = 16
SEQ = N_TOK + N_META
ROW_TILE = 688
N_ROW_TILES = SEQ // ROW_TILE
N_DEV = 8
D_FF = 2816
FF_SHARD = D_FF // N_DEV
FF_TILE = 256
IN_SHARD = 256
NA_WIDTH = 512
S5_WIDTH = 512
HEADS = 8
HEAD_DIM = 64
GRID_W = 64
GRID_ROWS = N_TOK // GRID_W
KH = 8
KW = 16
NA_RB = 4
NA_KR = KH + NA_RB - 1
NA_BLOCKS = GRID_ROWS // NA_RB
NA_QB = NA_RB * GRID_W
NA_KB = NA_KR * GRID_W
NA_TYPES = 3
S5_GROUPS = 32
S5_GROUP = 16
S5_STATE = 64
S5_CHUNKS = 4
CH_W = S5_WIDTH // S5_CHUNKS
ST_W = S5_GROUPS * S5_STATE // S5_CHUNKS
SCAN_BLOCKS = 8
SCAN_T = SEQ // SCAN_BLOCKS
RMS_EPS = 1e-6
NEG_INF = -1e30
ATT_SCALE = HEAD_DIM ** -0.5
ADAM_LR, ADAM_B1, ADAM_B2, ADAM_EPS, ADAM_WD, ADAM_STEP = 0.001, 0.9, 0.999, 1e-08, 0.01, 10
VMEM_LIMIT = 56 * 1024 * 1024
MESH = pl.DeviceIdType.MESH


def _params(sem=None):
    return pltpu.CompilerParams(dimension_semantics=sem, vmem_limit_bytes=VMEM_LIMIT)


def _dot(a, b):
    return jnp.dot(a, b, preferred_element_type=F32)


def _dot_nt(a, b):
    return lax.dot_general(a, b, (((1,), (1,)), ((), ())), preferred_element_type=F32)


def _dot_tn(a, b):
    return lax.dot_general(a, b, (((0,), (0,)), ((), ())), preferred_element_type=F32)


def _rstd(x):
    return lax.rsqrt(jnp.mean(x * x, axis=-1, keepdims=True) + RMS_EPS)


def _rms_bwd(x, r, g, dy):
    dyg = dy * g
    xr = x * r
    dx = r * (dyg - xr * jnp.mean(dyg * xr, axis=-1, keepdims=True))
    return dx, dy * xr


def _rows(i, size=ROW_TILE):
    return pl.ds(pl.multiple_of(i * size, 16), size)


def _row_spec(width):
    return pl.BlockSpec((ROW_TILE, width), lambda i: (i, 0))


def _fix_spec(shape):
    return pl.BlockSpec(shape, lambda i: (0,) * len(shape))


def _split3(x):
    hi = x.astype(BF16)
    r1 = x - hi.astype(F32)
    mid = r1.astype(BF16)
    lo = (r1 - mid.astype(F32)).astype(BF16)
    return hi, mid, lo


def _embed_prenorm(meta, x, g):
    def body(m_ref, x_ref, g_ref, h_ref, a_ref):
        h_ref[0:N_META, :] = m_ref[...]
        h_ref[N_META:, :] = x_ref[...]
        for i in range(N_ROW_TILES):
            rows = slice(i * ROW_TILE, (i + 1) * ROW_TILE)
            hv = h_ref[rows, :]
            a_ref[rows, :] = (hv * _rstd(hv) * g_ref[...]).astype(BF16)

    return pl.pallas_call(
        body, out_shape=[SDS((SEQ, D_MODEL), F32), SDS((SEQ, D_MODEL), BF16)], name="embed_prenorm",
        compiler_params=_params())(meta, x, g)


def _post_pre(f, hres, g_post, g_next, scale, name):
    def body(f_ref, h_ref, gp_ref, gn_ref, ho_ref, a_ref):
        fv = f_ref[...]
        h = h_ref[...] + scale * (fv * _rstd(fv) * gp_ref[...])
        ho_ref[...] = h
        a_ref[...] = (h * _rstd(h) * gn_ref[...]).astype(BF16)

    return pl.pallas_call(
        body, grid=(N_ROW_TILES,),
        in_specs=[_row_spec(D_MODEL), _row_spec(D_MODEL), _fix_spec((1, D_MODEL)), _fix_spec((1, D_MODEL))],
        out_specs=[_row_spec(D_MODEL), _row_spec(D_MODEL)],
        out_shape=[SDS((SEQ, D_MODEL), F32), SDS((SEQ, D_MODEL), BF16)], name=name,
        compiler_params=_params(("parallel",)))(f, hres, g_post, g_next)


def _final_loss(f2, h2, g_post, g_final, target):
    def body(f_ref, h_ref, gp_ref, gf_ref, t_ref, loss_ref, dh_ref, df_ref, dgf_ref, dgp_ref):
        i = pl.program_id(0)
        fv = f_ref[...]
        r1 = _rstd(fv)
        gp = gp_ref[...]
        h3 = h_ref[...] + 0.5 * (fv * r1 * gp)
        r2 = _rstd(h3)
        gf = gf_ref[...]
        y = h3 * r2 * gf
        row = lax.broadcasted_iota(jnp.int32, (ROW_TILE, 1), 0) + i * ROW_TILE
        err = jnp.where(row >= N_META, y - t_ref[...], 0.0)
        part = 0.5 * jnp.sum(jnp.mean(err * err, axis=-1, keepdims=True))
        dy = err * (1.0 / D_MODEL)
        dh3, dgf = _rms_bwd(h3, r2, gf, dy)
        dh_ref[...] = dh3
        df, dgp = _rms_bwd(fv, r1, gp, 0.5 * dh3)
        df_ref[...] = df.astype(BF16)

        @pl.when(i == 0)
        def _():
            loss_ref[...] = jnp.zeros_like(loss_ref)
            dgf_ref[...] = jnp.zeros_like(dgf_ref)
            dgp_ref[...] = jnp.zeros_like(dgp_ref)

        loss_ref[...] += part
        dgf_ref[...] += jnp.sum(dgf, axis=0, keepdims=True)
        dgp_ref[...] += jnp.sum(dgp, axis=0, keepdims=True)

    gain = _fix_spec((1, D_MODEL))
    return pl.pallas_call(
        body, grid=(N_ROW_TILES,),
        in_specs=[_row_spec(D_MODEL), _row_spec(D_MODEL), gain, gain, _row_spec(D_MODEL)],
        out_specs=[_fix_spec((8, 128)), _row_spec(D_MODEL), _row_spec(D_MODEL), gain, gain],
        out_shape=[SDS((8, 128), F32), SDS((SEQ, D_MODEL), F32), SDS((SEQ, D_MODEL), BF16),
                   SDS((1, D_MODEL), F32), SDS((1, D_MODEL), F32)],
        name="final_loss", compiler_params=_params(("arbitrary",)))(f2, h2, g_post, g_final, target)


def _bwd_pre_post(da, h, g_pre, dh_res, fprev, g_post, scale, name):
    def body(da_ref, h_ref, gpre_ref, dhr_ref, f_ref, gpost_ref, dh_ref, df_ref, dgpre_ref, dgpost_ref):
        i = pl.program_id(0)
        hv = h_ref[...]
        dxa, dgpre = _rms_bwd(hv, _rstd(hv), gpre_ref[...], da_ref[...])
        dh = dhr_ref[...] + dxa
        dh_ref[...] = dh
        fv = f_ref[...]
        df, dgpost = _rms_bwd(fv, _rstd(fv), gpost_ref[...], scale * dh)
        df_ref[...] = df.astype(BF16)

        @pl.when(i == 0)
        def _():
            dgpre_ref[...] = jnp.zeros_like(dgpre_ref)
            dgpost_ref[...] = jnp.zeros_like(dgpost_ref)

        dgpre_ref[...] += jnp.sum(dgpre, axis=0, keepdims=True)
        dgpost_ref[...] += jnp.sum(dgpost, axis=0, keepdims=True)

    gain = _fix_spec((1, D_MODEL))
    row = _row_spec(D_MODEL)
    return pl.pallas_call(
        body, grid=(N_ROW_TILES,), in_specs=[row, row, gain, row, row, gain],
        out_specs=[row, row, gain, gain],
        out_shape=[SDS((SEQ, D_MODEL), F32), SDS((SEQ, D_MODEL), BF16), SDS((1, D_MODEL), F32), SDS((1, D_MODEL), F32)],
        name=name, compiler_params=_params(("arbitrary",)))(da, h, g_pre, dh_res, fprev, g_post)


def _bwd_embed(da, h, g_pre, dh_res):
    def body(da_ref, h_ref, gpre_ref, dhr_ref, gx_ref, gm_ref, dgpre_ref):
        total = jnp.zeros((1, D_MODEL), F32)
        for i in range(N_ROW_TILES):
            rows = slice(i * ROW_TILE, (i + 1) * ROW_TILE)
            hv = h_ref[rows, :]
            dxa, dgpre = _rms_bwd(hv, _rstd(hv), gpre_ref[...], da_ref[rows, :])
            dh = dhr_ref[rows, :] + dxa
            total = total + jnp.sum(dgpre, axis=0, keepdims=True)
            if i == 0:
                gm_ref[...] = dh[0:N_META, :]
                gx_ref[0:ROW_TILE - N_META, :] = dh[N_META:, :]
            else:
                gx_ref[i * ROW_TILE - N_META:(i + 1) * ROW_TILE - N_META, :] = dh
        dgpre_ref[...] = total

    return pl.pallas_call(
        body, out_shape=[SDS((N_TOK, D_MODEL), F32), SDS((N_META, D_MODEL), F32), SDS((1, D_MODEL), F32)],
        name="bwd_embed", compiler_params=_params())(da, h, g_pre, dh_res)


def _ffn_fwd(a, wg, wu, wd, name, after=()):
    def body(a_ref, wg_ref, wu_ref, wd_ref, *rest):
        gate_ref, up_ref, f_ref = rest[len(after):]
        j = pl.program_id(0)

        def tile(i, carry):
            rows = _rows(i)
            at = a_ref[rows, :]
            gate = _dot_nt(at, wg_ref[...])
            up = _dot_nt(at, wu_ref[...])
            gate_ref[rows, :] = gate.astype(BF16)
            up_ref[rows, :] = up.astype(BF16)
            act = (gate * jax.nn.sigmoid(gate) * up).astype(BF16)
            contrib = _dot(act, wd_ref[...])

            @pl.when(j == 0)
            def _():
                f_ref[rows, :] = contrib

            @pl.when(j != 0)
            def _():
                f_ref[rows, :] += contrib

            return carry

        lax.fori_loop(0, N_ROW_TILES, tile, 0)

    wtile = pl.BlockSpec((FF_TILE, D_MODEL), lambda j: (j, 0))
    hid = pl.BlockSpec((SEQ, FF_TILE), lambda j: (0, j))
    full = pl.BlockSpec((SEQ, D_MODEL), lambda j: (0, 0))
    return pl.pallas_call(
        body, grid=(D_FF // FF_TILE,), in_specs=[full, wtile, wtile, wtile] + [pl.BlockSpec(memory_space=pl.ANY)] * len(after),
        out_specs=[hid, hid, full],
        out_shape=[SDS((SEQ, D_FF), BF16), SDS((SEQ, D_FF), BF16), SDS((SEQ, D_MODEL), F32)],
        name=name, compiler_params=_params(("arbitrary",)))(a, wg, wu, wd, *after)


def _ffn_bwd(df, a, gate, up, wg, wu, wd, name):
    def body(df_ref, a_ref, gate_ref, up_ref, wg_ref, wu_ref, wd_ref, da_ref, dwg_ref, dwu_ref, dwd_ref,
             acc_g, acc_u, acc_d):
        j = pl.program_id(0)

        def tile(i, carry):
            rows = _rows(i)
            dft = df_ref[rows, :]
            at = a_ref[rows, :]
            gate = gate_ref[rows, :].astype(F32)
            up = up_ref[rows, :].astype(F32)
            dact = _dot_nt(dft, wd_ref[...])
            sig = jax.nn.sigmoid(gate)
            silu = gate * sig
            dgate = (dact * up * (sig * (1.0 + gate * (1.0 - sig)))).astype(BF16)
            dup = (dact * silu).astype(BF16)
            act = (silu * up).astype(BF16)
            dwd = _dot_tn(act, dft)
            dwg = _dot_tn(dgate, at)
            dwu = _dot_tn(dup, at)
            dat = _dot(dgate, wg_ref[...]) + _dot(dup, wu_ref[...])

            @pl.when(i == 0)
            def _():
                acc_d[...] = dwd
                acc_g[...] = dwg
                acc_u[...] = dwu

            @pl.when(i != 0)
            def _():
                acc_d[...] += dwd
                acc_g[...] += dwg
                acc_u[...] += dwu

            @pl.when(j == 0)
            def _():
                da_ref[rows, :] = dat

            @pl.when(j != 0)
            def _():
                da_ref[rows, :] += dat

            return carry

        lax.fori_loop(0, N_ROW_TILES, tile, 0)
        dwg_ref[...] = acc_g[...].astype(BF16)
        dwu_ref[...] = acc_u[...].astype(BF16)
        dwd_ref[...] = acc_d[...].astype(BF16)

    wtile = pl.BlockSpec((FF_TILE, D_MODEL), lambda j: (j, 0))
    hid = pl.BlockSpec((SEQ, FF_TILE), lambda j: (0, j))
    full = pl.BlockSpec((SEQ, D_MODEL), lambda j: (0, 0))
    return pl.pallas_call(
        body, grid=(D_FF // FF_TILE,), in_specs=[full, full, hid, hid, wtile, wtile, wtile],
        out_specs=[full, wtile, wtile, wtile],
        out_shape=[SDS((SEQ, D_MODEL), F32)] + [SDS((D_FF, D_MODEL), BF16)] * 3,
        scratch_shapes=[pltpu.VMEM((FF_TILE, D_MODEL), F32)] * 3,
        name=name, compiler_params=_params(("arbitrary",)))(df, a, gate, up, wg, wu, wd)


HEADS_PER_BLOCK = IN_SHARD // HEAD_DIM
QKV_BLOCKS = 3 * NA_WIDTH // IN_SHARD


def _proj_heads(a, w):
    def body(a_ref, w_ref, o_ref):
        def tile(i, carry):
            rows = _rows(i)
            res = _dot(a_ref[rows, :], w_ref[...])
            for sub in range(HEADS_PER_BLOCK):
                o_ref[sub, rows, :] = res[:, sub * HEAD_DIM:(sub + 1) * HEAD_DIM]
            return carry

        lax.fori_loop(0, N_ROW_TILES, tile, 0)

    return pl.pallas_call(
        body, grid=(QKV_BLOCKS,),
        in_specs=[pl.BlockSpec((SEQ, D_MODEL), lambda j: (0, 0)), pl.BlockSpec((None, D_MODEL, IN_SHARD), lambda j: (j, 0, 0))],
        out_specs=pl.BlockSpec((HEADS_PER_BLOCK, SEQ, HEAD_DIM), lambda j: (j, 0, 0)),
        out_shape=SDS((3 * HEADS, SEQ, HEAD_DIM), F32), name="proj_heads",
        compiler_params=_params(("parallel",)))(a, w)


def _proj_u(a, w):
    def body(a_ref, w_ref, o_ref):
        def tile(i, carry):
            rows = _rows(i)
            o_ref[rows, :] = _dot(a_ref[rows, :], w_ref[...])
            return carry

        lax.fori_loop(0, N_ROW_TILES, tile, 0)

    return pl.pallas_call(
        body, grid=(N_DEV - QKV_BLOCKS,),
        in_specs=[pl.BlockSpec((SEQ, D_MODEL), lambda j: (0, 0)),
                  pl.BlockSpec((None, D_MODEL, IN_SHARD), lambda j: (j + QKV_BLOCKS, 0, 0))],
        out_specs=pl.BlockSpec((SEQ, IN_SHARD), lambda j: (0, j)),
        out_shape=SDS((SEQ, S5_WIDTH), F32), name="proj_u",
        compiler_params=_params(("parallel",)))(a, w)


def _proj_bwd(dq, dk, dv, du, a, w):
    def body(dq_ref, dk_ref, dv_ref, du_ref, a_ref, w_ref, da_ref, dw_ref, acc, dp_ref):
        j = pl.program_id(0)

        for which, src in enumerate((dq_ref, dk_ref, dv_ref)):
            @pl.when((j >= 2 * which) & (j < 2 * which + 2))
            def _(src=src):
                dp_ref[...] = jnp.concatenate([src[sub] for sub in range(HEADS_PER_BLOCK)], axis=-1).astype(BF16)

        @pl.when(j >= QKV_BLOCKS)
        def _():
            dp_ref[...] = du_ref[...].astype(BF16)

        def tile(i, carry):
            rows = _rows(i)
            dpt = dp_ref[rows, :]
            dw = _dot_tn(a_ref[rows, :], dpt)
            dat = _dot_nt(dpt, w_ref[...])

            @pl.when(i == 0)
            def _():
                acc[...] = dw

            @pl.when(i != 0)
            def _():
                acc[...] += dw

            @pl.when(j == 0)
            def _():
                da_ref[rows, :] = dat

            @pl.when(j != 0)
            def _():
                da_ref[rows, :] += dat

            return carry

        lax.fori_loop(0, N_ROW_TILES, tile, 0)
        dw_ref[...] = acc[...].astype(BF16)

    full = pl.BlockSpec((SEQ, D_MODEL), lambda j: (0, 0))
    wspec = pl.BlockSpec((None, D_MODEL, IN_SHARD), lambda j: (j, 0, 0))

    def heads(which):
        return pl.BlockSpec((HEADS_PER_BLOCK, SEQ, HEAD_DIM), lambda j: (jnp.clip(j - 2 * which, 0, 1), 0, 0))

    return pl.pallas_call(
        body, grid=(N_DEV,),
        in_specs=[heads(0), heads(1), heads(2),
                  pl.BlockSpec((SEQ, IN_SHARD), lambda j: (0, jnp.clip(j - QKV_BLOCKS, 0, 1))), full, wspec],
        out_specs=[full, wspec],
        out_shape=[SDS((SEQ, D_MODEL), F32), SDS((N_DEV, D_MODEL, IN_SHARD), BF16)],
        scratch_shapes=[pltpu.VMEM((D_MODEL, IN_SHARD), F32), pltpu.VMEM((SEQ, IN_SHARD), BF16)],
        name="proj_bwd", compiler_params=_params(("arbitrary",)))(dq, dk, dv, du, a, w)


def _na_consts():
    c = np.arange(GRID_W)
    col_start = np.clip(c - KW // 2, 0, GRID_W - KW)
    col_in = (c[None, :] >= col_start[:, None]) & (c[None, :] < col_start[:, None] + KW)
    dc = np.clip(c[None, :] - c[:, None] + KW - 1, 0, 2 * KW - 2)
    onehot = np.zeros((128, GRID_W * GRID_W), np.float32)
    qq, kk = np.meshgrid(c, c, indexing="ij")
    onehot[dc[col_in], (qq * GRID_W + kk)[col_in]] = 1.0
    negmask = np.where(col_in, 0.0, NEG_INF).astype(np.float32).reshape(1, -1)
    return onehot, negmask


def _na_pair(block_type, a, b):
    if block_type == 0:
        return b - a + KH - 1 if b < KH else None
    if block_type == 1:
        return b - a + KH // 2 - 1 if a <= b < a + KH else None
    return b - a if b >= NA_KR - KH else None


def _rpb_expand(rpb):
    onehot, negmask = _na_consts()
    rows = HEADS * (2 * KH - 1)
    rpb_pad = jnp.pad(rpb.reshape(rows, 2 * KW - 1), ((0, 128 - rows), (0, 128 - (2 * KW - 1))))

    def body(r_ref, oh_ref, m_ref, t_ref):
        hi, mid, lo = _split3(r_ref[...])
        oh = oh_ref[...]
        t_ref[...] = _dot(hi, oh) + _dot(mid, oh) + _dot(lo, oh) + m_ref[...]

    table = pl.pallas_call(body, out_shape=SDS((128, GRID_W * GRID_W), F32), name="rpb_expand",
                           compiler_params=_params())(rpb_pad, jnp.asarray(onehot, BF16), jnp.asarray(negmask))
    return table[:rows].reshape(HEADS, 2 * KH - 1, GRID_W, GRID_W)


def _rpb_reduce(dslabs):
    onehot, _ = _na_consts()
    rows = HEADS * (2 * KH - 1)

    def body(x_ref, oht_ref, o_ref):
        hi, mid, lo = _split3(x_ref[...])
        oht = oht_ref[...]
        o_ref[...] = _dot(hi, oht) + _dot(mid, oht) + _dot(lo, oht)

    out = pl.pallas_call(body, out_shape=SDS((rows, 128), F32), name="rpb_reduce", compiler_params=_params())(
        dslabs.reshape(rows, GRID_W * GRID_W), jnp.asarray(onehot.T, BF16))
    return out.reshape(HEADS, 2 * KH - 1, 128)


def _bias_tiles(slab_ref, tile_ref):
    tile_ref[...] = jnp.full(tile_ref.shape, NEG_INF, F32)
    for t in range(NA_TYPES):
        for a in range(NA_RB):
            for b in range(NA_KR):
                dr = _na_pair(t, a, b)
                if dr is not None:
                    tile_ref[t, a * GRID_W:(a + 1) * GRID_W, b * GRID_W:(b + 1) * GRID_W] = slab_ref[dr]


def _bias_tiles_bwd(dtile_ref, dslab_ref):
    acc = {}
    for t in range(NA_TYPES):
        for a in range(NA_RB):
            for b in range(NA_KR):
                dr = _na_pair(t, a, b)
                if dr is not None:
                    part = dtile_ref[t, a * GRID_W:(a + 1) * GRID_W, b * GRID_W:(b + 1) * GRID_W]
                    acc[dr] = part if dr not in acc else acc[dr] + part
    for dr in range(2 * KH - 1):
        dslab_ref[dr] = acc[dr]


def _block_geometry(g):
    start = jnp.clip(g * NA_RB - KH // 2, 0, GRID_ROWS - NA_KR)
    block_type = jnp.where(g == 0, 0, jnp.where(g == NA_BLOCKS - 1, 2, 1))
    q0 = pl.multiple_of(N_META + g * NA_QB, 16)
    k0 = pl.multiple_of(N_META + start * GRID_W, 16)
    return block_type, q0, k0


def _na_probs(q, kk, km, bias):
    s = _dot_nt(q, kk) * ATT_SCALE + bias
    sm = _dot_nt(q, km) * ATT_SCALE
    m = jnp.maximum(jnp.max(s, axis=-1, keepdims=True), jnp.max(sm, axis=-1, keepdims=True))
    p = jnp.exp(s - m)
    pm = jnp.exp(sm - m)
    inv = 1.0 / (jnp.sum(p, axis=-1, keepdims=True) + jnp.sum(pm, axis=-1, keepdims=True))
    return p * inv, pm * inv


def _meta_probs(qm, km):
    s = _dot_nt(qm, km) * ATT_SCALE
    p = jnp.exp(s - jnp.max(s, axis=-1, keepdims=True))
    return p / jnp.sum(p, axis=-1, keepdims=True)


def _qkv_specs():
    return [pl.BlockSpec((None, SEQ, HEAD_DIM), lambda h, which=which: (h + which * HEADS, 0, 0)) for which in range(3)]


def _na_fwd(qkv, bias):
    def body(q_ref, k_ref, v_ref, slab_ref, o_ref, b_ref):
        _bias_tiles(slab_ref, b_ref)
        km = k_ref[0:N_META, :].astype(BF16)
        vm = v_ref[0:N_META, :].astype(BF16)
        pmm = _meta_probs(q_ref[0:N_META, :].astype(BF16), km)
        o_ref[0:N_META, :] = _dot(pmm.astype(BF16), vm)

        def block(g, carry):
            block_type, q0, k0 = _block_geometry(g)
            qb = q_ref[pl.ds(q0, NA_QB), :].astype(BF16)
            kk = k_ref[pl.ds(k0, NA_KB), :].astype(BF16)
            vv = v_ref[pl.ds(k0, NA_KB), :].astype(BF16)
            p, pm = _na_probs(qb, kk, km, b_ref[block_type])
            o_ref[pl.ds(q0, NA_QB), :] = _dot(p.astype(BF16), vv) + _dot(pm.astype(BF16), vm)
            return carry

        lax.fori_loop(0, NA_BLOCKS, block, 0, unroll=2)

    head = pl.BlockSpec((None, SEQ, HEAD_DIM), lambda h: (h, 0, 0))
    return pl.pallas_call(
        body, grid=(HEADS,), in_specs=_qkv_specs() + [pl.BlockSpec((None, 2 * KH - 1, GRID_W, GRID_W), lambda h: (h, 0, 0, 0))],
        out_specs=head, out_shape=SDS((HEADS, SEQ, HEAD_DIM), F32), name="na_fwd",
        scratch_shapes=[pltpu.VMEM((NA_TYPES, NA_QB, NA_KB), F32)],
        compiler_params=_params(("parallel",)))(qkv, qkv, qkv, bias)


def _na_bwd(qkv, bias, do):
    def body(q_ref, k_ref, v_ref, slab_ref, do_ref, dq_ref, dk_ref, dv_ref, dslab_ref, b_ref, db_ref):
        _bias_tiles(slab_ref, b_ref)
        km = k_ref[0:N_META, :].astype(BF16)
        vm = v_ref[0:N_META, :].astype(BF16)
        dk_ref[...] = jnp.zeros_like(dk_ref)
        dv_ref[...] = jnp.zeros_like(dv_ref)
        db_ref[...] = jnp.zeros_like(db_ref)

        qm = q_ref[0:N_META, :].astype(BF16)
        dom = do_ref[0:N_META, :].astype(BF16)
        pmm = _meta_probs(qm, km)
        dpm = _dot_nt(dom, vm)
        dsm = (pmm * (dpm - jnp.sum(pmm * dpm, axis=-1, keepdims=True)) * ATT_SCALE).astype(BF16)
        dq_ref[0:N_META, :] = _dot(dsm, km)
        dkm0 = _dot_tn(dsm, qm)
        dvm0 = _dot_tn(pmm.astype(BF16), dom)

        def block(g, carry):
            dkm, dvm = carry
            block_type, q0, k0 = _block_geometry(g)
            qb = q_ref[pl.ds(q0, NA_QB), :].astype(BF16)
            kk = k_ref[pl.ds(k0, NA_KB), :].astype(BF16)
            vv = v_ref[pl.ds(k0, NA_KB), :].astype(BF16)
            dob = do_ref[pl.ds(q0, NA_QB), :].astype(BF16)
            p, pm = _na_probs(qb, kk, km, b_ref[block_type])
            dp = _dot_nt(dob, vv)
            dpm_ = _dot_nt(dob, vm)
            delta = jnp.sum(p * dp, axis=-1, keepdims=True) + jnp.sum(pm * dpm_, axis=-1, keepdims=True)
            ds = p * (dp - delta)
            dsm_ = pm * (dpm_ - delta)
            db_ref[block_type] += ds
            dsb = (ds * ATT_SCALE).astype(BF16)
            dsmb = (dsm_ * ATT_SCALE).astype(BF16)
            dq_ref[pl.ds(q0, NA_QB), :] = _dot(dsb, kk) + _dot(dsmb, km)
            dk_ref[pl.ds(k0, NA_KB), :] += _dot_tn(dsb, qb)
            dv_ref[pl.ds(k0, NA_KB), :] += _dot_tn(p.astype(BF16), dob)
            return dkm + _dot_tn(dsmb, qb), dvm + _dot_tn(pm.astype(BF16), dob)

        dkm, dvm = lax.fori_loop(0, NA_BLOCKS, block, (dkm0, dvm0), unroll=2)
        dk_ref[0:N_META, :] = dkm
        dv_ref[0:N_META, :] = dvm
        _bias_tiles_bwd(db_ref, dslab_ref)

    head = pl.BlockSpec((None, SEQ, HEAD_DIM), lambda h: (h, 0, 0))
    bspec = pl.BlockSpec((None, 2 * KH - 1, GRID_W, GRID_W), lambda h: (h, 0, 0, 0))
    return pl.pallas_call(
        body, grid=(HEADS,), in_specs=_qkv_specs() + [bspec, head], out_specs=[head, head, head, bspec],
        out_shape=[SDS((HEADS, SEQ, HEAD_DIM), F32)] * 3 + [SDS((HEADS, 2 * KH - 1, GRID_W, GRID_W), F32)],
        scratch_shapes=[pltpu.VMEM((NA_TYPES, NA_QB, NA_KB), F32), pltpu.VMEM((NA_TYPES, NA_QB, NA_KB), F32)],
        name="na_bwd", compiler_params=_params(("parallel",)))(qkv, qkv, qkv, bias, do)


def _cmul(ar, ai, br, bi):
    return ar * br - ai * bi, ar * bi + ai * br


def _cpow(ar, ai, n):
    rr, ri = None, None
    br, bi = ar, ai
    while n:
        if n & 1:
            rr, ri = (br, bi) if rr is None else _cmul(rr, ri, br, bi)
        n >>= 1
        if n:
            br, bi = _cmul(br, bi, br, bi)
    return rr, ri


def _s5_prep(lr, li, logdt, bre, bim):
    def body(lr_ref, li_ref, dt_ref, br_ref, bi_ref, lbr_ref, lbi_ref, bbr_ref, bbi_ref):
        lr_, li_ = lr_ref[...], li_ref[...]
        dt = jnp.exp(dt_ref[...])
        mag = jnp.exp(lr_ * dt)
        lbr = mag * jnp.cos(li_ * dt)
        lbi = mag * jnp.sin(li_ * dt)
        lbr_ref[...] = lbr
        lbi_ref[...] = lbi
        den = lr_ * lr_ + li_ * li_
        xr = lbr - 1.0
        cr = (xr * lr_ + lbi * li_) / den
        ci = (lbi * lr_ - xr * li_) / den
        br, bi = br_ref[...], bi_ref[...]
        bbr_ref[...] = cr[:, None, :] * br - ci[:, None, :] * bi
        bbi_ref[...] = cr[:, None, :] * bi + ci[:, None, :] * br

    n = 2 * S5_GROUPS
    return pl.pallas_call(
        body, out_shape=[SDS((n, S5_STATE), F32)] * 2 + [SDS((n, S5_GROUP, S5_STATE), F32)] * 2,
        name="s5_prep", compiler_params=_params())(lr, li, logdt, bre, bim)


def _s5_prep_bwd(lr, li, logdt, bre, bim, dar, dai, dbbr, dbbi):
    def body(lr_ref, li_ref, dt_ref, br_ref, bi_ref, dar_ref, dai_ref, dbr_ref, dbi_ref,
             glr_ref, gli_ref, gdt_ref, gbr_ref, gbi_ref):
        lr_, li_ = lr_ref[...], li_ref[...]
        dt = jnp.exp(dt_ref[...])
        mag = jnp.exp(lr_ * dt)
        lbr = mag * jnp.cos(li_ * dt)
        lbi = mag * jnp.sin(li_ * dt)
        den = lr_ * lr_ + li_ * li_
        xr = lbr - 1.0
        cr = (xr * lr_ + lbi * li_) / den
        ci = (lbi * lr_ - xr * li_) / den
        br, bi = br_ref[...], bi_ref[...]
        dbr, dbi = dbr_ref[...], dbi_ref[...]
        gbr_ref[...] = cr[:, None, :] * dbr + ci[:, None, :] * dbi
        gbi_ref[...] = cr[:, None, :] * dbi - ci[:, None, :] * dbr
        gcr = jnp.sum(dbr * br + dbi * bi, axis=1)
        gci = jnp.sum(dbi * br - dbr * bi, axis=1)
        ilr, ili = lr_ / den, li_ / den
        tr, ti = _cmul(gcr, gci, ilr, ili)
        glbr = dar_ref[...] + tr
        glbi = dai_ref[...] + ti
        dr_, di_ = _cmul(tr, ti, cr, -ci)
        gwr, gwi = _cmul(glbr, glbi, lbr, -lbi)
        glr_ref[...] = gwr * dt - dr_
        gli_ref[...] = gwi * dt - di_
        gdt_ref[...] = jnp.sum(gwr * lr_ + gwi * li_, axis=-1, keepdims=True) * dt

    n = 2 * S5_GROUPS
    return pl.pallas_call(
        body, out_shape=[SDS((n, S5_STATE), F32)] * 2 + [SDS((n, 1), F32)] + [SDS((n, S5_GROUP, S5_STATE), F32)] * 2,
        name="s5_prep_bwd", compiler_params=_params())(lr, li, logdt, bre, bim, dar, dai, dbbr, dbbi)


def _scan_local(xr_ref, xi_ref, ar8, ai8, reverse):
    def step(i, carry):
        sr, si = carry
        idx = (SCAN_T - 1 - i) if reverse else i
        rows = pl.ds(pl.multiple_of(idx * SCAN_BLOCKS, SCAN_BLOCKS), SCAN_BLOCKS)
        nr = ar8 * sr - ai8 * si + xr_ref[rows, :]
        ni = ar8 * si + ai8 * sr + xi_ref[rows, :]
        xr_ref[rows, :] = nr
        xi_ref[rows, :] = ni
        return nr, ni

    z = jnp.zeros(ar8.shape, F32)
    return lax.fori_loop(0, SCAN_T, step, (z, z))


def _scan_carries(er, ei, atr, ati, reverse):
    row = lax.broadcasted_iota(jnp.int32, er.shape, 0)
    cr = jnp.zeros((1, er.shape[1]), F32)
    ci = cr
    outr = jnp.zeros(er.shape, F32)
    outi = outr
    order = range(SCAN_BLOCKS - 1, -1, -1) if reverse else range(SCAN_BLOCKS)
    for b in order:
        outr = jnp.where(row == b, cr, outr)
        outi = jnp.where(row == b, ci, outi)
        nr, ni = _cmul(atr, ati, cr, ci)
        cr, ci = nr + er[b:b + 1, :], ni + ei[b:b + 1, :]
    return outr, outi


def _scan_fixup(xr_ref, xi_ref, cr8, ci8, ar8, ai8, reverse, pair=None):
    tile = lambda idx: pl.ds(pl.multiple_of(idx * SCAN_BLOCKS, SCAN_BLOCKS), SCAN_BLOCKS)

    def fix(idx, pr, pi):
        fr, fi = _cmul(pr, pi, cr8, ci8)
        nr, ni = xr_ref[tile(idx), :] + fr, xi_ref[tile(idx), :] + fi
        xr_ref[tile(idx), :] = nr
        xi_ref[tile(idx), :] = ni
        return nr, ni

    if pair is None:
        def step(i, carry):
            pr, pi = carry
            fix((SCAN_T - 1 - i) if reverse else i, pr, pi)
            return _cmul(pr, pi, ar8, ai8)

        lax.fori_loop(0, SCAN_T, step, (ar8, ai8), unroll=2)
        return None

    sr_ref, si_ref = pair
    earlier = -1 if reverse else 1

    def step(i, carry):
        pr, pi, accr, acci = carry
        idx = (SCAN_T - 1 - i) if reverse else i
        nr, ni = fix(idx, pr, pi)
        qr, qi = _cmul(nr, ni, sr_ref[tile(idx + earlier), :], -si_ref[tile(idx + earlier), :])
        pr, pi = _cmul(pr, pi, ar8, ai8)
        return pr, pi, accr + qr, acci + qi

    z = jnp.zeros(ar8.shape, F32)
    pr, pi, accr, acci = lax.fori_loop(0, SCAN_T - 1, step, (ar8, ai8, z, z))
    edge, src, shift, empty = (0, SCAN_T - 1, 1, 0) if reverse else (SCAN_T - 1, 0, SCAN_BLOCKS - 1, SCAN_BLOCKS - 1)
    nr, ni = fix(edge, pr, pi)
    row = lax.broadcasted_iota(jnp.int32, ar8.shape, 0)
    spr = jnp.where(row == empty, 0.0, pltpu.roll(sr_ref[tile(src), :], shift, 0))
    spi = jnp.where(row == empty, 0.0, pltpu.roll(si_ref[tile(src), :], shift, 0))
    qr, qi = _cmul(nr, ni, spr, -spi)
    return jnp.sum(accr + qr, axis=0, keepdims=True), jnp.sum(acci + qi, axis=0, keepdims=True)


def _scan(xr_ref, xi_ref, ar, ai, reverse, pair=None):
    n = ar.shape[1]
    ar8 = jnp.broadcast_to(ar, (SCAN_BLOCKS, n))
    ai8 = jnp.broadcast_to(ai, (SCAN_BLOCKS, n))
    er, ei = _scan_local(xr_ref, xi_ref, ar8, ai8, reverse)
    atr, ati = _cpow(ar, ai, SCAN_T)
    cr8, ci8 = _scan_carries(er, ei, atr, ati, reverse)
    return _scan_fixup(xr_ref, xi_ref, cr8, ci8, ar8, ai8, reverse, pair)


def _s5_specs():
    chan = pl.BlockSpec((SEQ, CH_W), lambda c, d: (0, c))
    chan2 = pl.BlockSpec((None, SEQ, CH_W), lambda c, d: (d, 0, c))
    state = pl.BlockSpec((None, SEQ, ST_W), lambda c, d: (d, 0, c))
    bmat = pl.BlockSpec((None, None, CH_W, ST_W), lambda c, d: (d, c, 0, 0))
    cmat = pl.BlockSpec((None, None, ST_W, CH_W), lambda c, d: (d, c, 0, 0))
    avec = pl.BlockSpec((None, None, 1, ST_W), lambda c, d: (d, c, 0, 0))
    return chan, chan2, state, bmat, cmat, avec


def _scan_by_direction(xr_ref, xi_ref, ar, ai, d, adjoint, pair=None, da_out=None):
    for direction in range(2):
        @pl.when(d == direction)
        def _(direction=direction):
            res = _scan(xr_ref, xi_ref, ar, ai, adjoint != (direction == 1), pair)
            if pair is not None:
                da_out[0][...], da_out[1][...] = res


def _s5_scan_fwd(u, bre, bim, are, aim, cre, cim):
    def body(u_ref, bre_ref, bim_ref, are_ref, aim_ref, cre_ref, cim_ref, sr_ref, si_ref, y_ref):
        ub = u_ref[...].astype(BF16)
        sr_ref[...] = _dot(ub, bre_ref[...])
        si_ref[...] = _dot(ub, bim_ref[...])
        _scan_by_direction(sr_ref, si_ref, are_ref[...], aim_ref[...], pl.program_id(1), adjoint=False)
        y_ref[...] = _dot(sr_ref[...].astype(BF16), cre_ref[...]) - _dot(si_ref[...].astype(BF16), cim_ref[...])

    chan, chan2, state, bmat, cmat, avec = _s5_specs()
    return pl.pallas_call(
        body, grid=(S5_CHUNKS, 2), in_specs=[chan, bmat, bmat, avec, avec, cmat, cmat], out_specs=[state, state, chan2],
        out_shape=[SDS((2, SEQ, S5_GROUPS * S5_STATE), F32)] * 2 + [SDS((2, SEQ, S5_WIDTH), F32)],
        name="s5_scan_fwd", compiler_params=_params(("parallel", "parallel")))(u, bre, bim, are, aim, cre, cim)


def _diag_out(out_ref, full):
    for g in range(8):
        out_ref[g] = full[g * S5_GROUP:(g + 1) * S5_GROUP, g * S5_STATE:(g + 1) * S5_STATE]


def _s5_scan_bwd(dy, du_skip, u, sr, si, bre, bim, are, aim, cre, cim):
    def body(dy_ref, dus_ref, u_ref, sr_ref, si_ref, bre_ref, bim_ref, are_ref, aim_ref, cre_ref, cim_ref,
             du_ref, dbr_ref, dbi_ref, dcr_ref, dci_ref, dar_ref, dai_ref, gr_ref, gi_ref):
        d = pl.program_id(1)
        dyb = dy_ref[...].astype(BF16)
        gr_ref[...] = _dot_nt(dyb, cre_ref[...])
        gi_ref[...] = -_dot_nt(dyb, cim_ref[...])
        _diag_out(dcr_ref, _dot_tn(dyb, sr_ref[...].astype(BF16)))
        _diag_out(dci_ref, -_dot_tn(dyb, si_ref[...].astype(BF16)))
        _scan_by_direction(gr_ref, gi_ref, are_ref[...], -aim_ref[...], d, adjoint=True, pair=(sr_ref, si_ref),
                           da_out=(dar_ref, dai_ref))

        @pl.when(d == 0)
        def _():
            du_ref[...] = dus_ref[...]

        grb = gr_ref[...].astype(BF16)
        gib = gi_ref[...].astype(BF16)
        du_ref[...] += _dot_nt(grb, bre_ref[...]) + _dot_nt(gib, bim_ref[...])
        ub = u_ref[...].astype(BF16)
        _diag_out(dbr_ref, _dot_tn(ub, grb))
        _diag_out(dbi_ref, _dot_tn(ub, gib))

    chan, _, state, bmat, cmat, avec = _s5_specs()
    diag = pl.BlockSpec((None, None, 8, S5_GROUP, S5_STATE), lambda c, d: (d, c, 0, 0, 0))
    return pl.pallas_call(
        body, grid=(S5_CHUNKS, 2), in_specs=[chan, chan, chan, state, state, bmat, bmat, avec, avec, cmat, cmat],
        out_specs=[chan, diag, diag, diag, diag, avec, avec],
        out_shape=[SDS((SEQ, S5_WIDTH), F32)] + [SDS((2, S5_CHUNKS, 8, S5_GROUP, S5_STATE), F32)] * 4
                  + [SDS((2, S5_CHUNKS, 1, ST_W), F32)] * 2,
        scratch_shapes=[pltpu.VMEM((SEQ, ST_W), F32), pltpu.VMEM((SEQ, ST_W), F32)],
        name="s5_scan_bwd", compiler_params=_params(("parallel", "arbitrary")))(dy, du_skip, u, sr, si, bre, bim, are, aim, cre, cim)


_GELU_K = math.sqrt(2.0 / math.pi)
_GELU_C = 0.044715


def _gelu(x):
    t = jnp.tanh(_GELU_K * (x + _GELU_C * x * x * x))
    return 0.5 * x * (1.0 + t), t


def _s5_glu_fwd(u, y2, dskip, wglu, bglu):
    def body(u_ref, y0_ref, y1_ref, d_ref, w_ref, b_ref, o_ref, yp_ref):
        ypre = u_ref[...] * d_ref[...] + y0_ref[...] + y1_ref[...]
        yp_ref[...] = ypre
        y, _ = _gelu(ypre)
        z = _dot(y.astype(BF16), w_ref[...]) + b_ref[...]
        o_ref[...] = y * jax.nn.sigmoid(z)

    row = _row_spec(S5_WIDTH)
    vec = _fix_spec((1, S5_WIDTH))
    dir0 = pl.BlockSpec((None, ROW_TILE, S5_WIDTH), lambda i: (0, i, 0))
    dir1 = pl.BlockSpec((None, ROW_TILE, S5_WIDTH), lambda i: (1, i, 0))
    return pl.pallas_call(
        body, grid=(N_ROW_TILES,), in_specs=[row, dir0, dir1, vec, _fix_spec((S5_WIDTH, S5_WIDTH)), vec],
        out_specs=[row, row], out_shape=[SDS((SEQ, S5_WIDTH), F32)] * 2, name="s5_glu_fwd",
        compiler_params=_params(("parallel",)))(u, y2, y2, dskip, wglu, bglu)


def _s5_glu_bwd(do, ypre, u, dskip, wglu, bglu):
    def body(do_ref, yp_ref, u_ref, d_ref, w_ref, b_ref, dyp_ref, du_ref, dw_ref, db_ref, dd_ref):
        i = pl.program_id(0)
        ypre = yp_ref[...]
        y, t = _gelu(ypre)
        yb = y.astype(BF16)
        sg = jax.nn.sigmoid(_dot(yb, w_ref[...]) + b_ref[...])
        dov = do_ref[...]
        dz = dov * y * sg * (1.0 - sg)
        dzb = dz.astype(BF16)
        dy = dov * sg + _dot_nt(dzb, w_ref[...])
        dgelu = 0.5 * (1.0 + t) + 0.5 * ypre * (1.0 - t * t) * _GELU_K * (1.0 + 3.0 * _GELU_C * ypre * ypre)
        dyp = dy * dgelu
        dyp_ref[...] = dyp
        uv = u_ref[...]
        du_ref[...] = dyp * d_ref[...]

        @pl.when(i == 0)
        def _():
            dw_ref[...] = jnp.zeros_like(dw_ref)
            db_ref[...] = jnp.zeros_like(db_ref)
            dd_ref[...] = jnp.zeros_like(dd_ref)

        dw_ref[...] += _dot_tn(yb, dzb)
        db_ref[...] += jnp.sum(dz, axis=0, keepdims=True)
        dd_ref[...] += jnp.sum(dyp * uv, axis=0, keepdims=True)

    row = _row_spec(S5_WIDTH)
    vec = _fix_spec((1, S5_WIDTH))
    mat = _fix_spec((S5_WIDTH, S5_WIDTH))
    return pl.pallas_call(
        body, grid=(N_ROW_TILES,), in_specs=[row, row, row, vec, mat, vec], out_specs=[row, row, mat, vec, vec],
        out_shape=[SDS((SEQ, S5_WIDTH), F32)] * 2 + [SDS((S5_WIDTH, S5_WIDTH), F32), SDS((1, S5_WIDTH), F32), SDS((1, S5_WIDTH), F32)],
        name="s5_glu_bwd", compiler_params=_params(("arbitrary",)))(do, ypre, u, dskip, wglu, bglu)


def _heads_side_by_side(o_ref):
    return jnp.concatenate([o_ref[h] for h in range(HEADS)], axis=-1)


def _mix_out_fwd(ona, os5, g_na, g_s5, wout):
    def body(a_ref, s_ref, ga_ref, gs_ref, w_ref, o_ref):
        av, sv = _heads_side_by_side(a_ref), s_ref[...]
        ca = (av * _rstd(av) * ga_ref[...]).astype(BF16)
        cs = (sv * _rstd(sv) * gs_ref[...]).astype(BF16)
        o_ref[...] = _dot(ca, w_ref[0:NA_WIDTH, :]) + _dot(cs, w_ref[NA_WIDTH:, :])

    row = _row_spec(NA_WIDTH)
    vec = _fix_spec((1, NA_WIDTH))
    heads = pl.BlockSpec((HEADS, ROW_TILE, HEAD_DIM), lambda i: (0, i, 0))
    return pl.pallas_call(
        body, grid=(N_ROW_TILES,), in_specs=[heads, row, vec, vec, _fix_spec((D_MODEL, D_MODEL))],
        out_specs=_row_spec(D_MODEL), out_shape=SDS((SEQ, D_MODEL), F32), name="mix_out_fwd",
        compiler_params=_params(("parallel",)))(ona, os5, g_na, g_s5, wout)


def _mix_out_bwd(dmix, ona, os5, g_na, g_s5, wout):
    def body(dm_ref, a_ref, s_ref, ga_ref, gs_ref, w_ref, da_ref, ds_ref, dw_ref, dga_ref, dgs_ref):
        i = pl.program_id(0)
        dm = dm_ref[...]
        av, sv = _heads_side_by_side(a_ref), s_ref[...]
        ra, rs = _rstd(av), _rstd(sv)
        ga, gs = ga_ref[...], gs_ref[...]
        ca = (av * ra * ga).astype(BF16)
        cs = (sv * rs * gs).astype(BF16)
        dca = _dot_nt(dm, w_ref[0:NA_WIDTH, :])
        dcs = _dot_nt(dm, w_ref[NA_WIDTH:, :])
        da, dga = _rms_bwd(av, ra, ga, dca)
        ds, dgs = _rms_bwd(sv, rs, gs, dcs)
        for h in range(HEADS):
            da_ref[h] = da[:, h * HEAD_DIM:(h + 1) * HEAD_DIM]
        ds_ref[...] = ds

        @pl.when(i == 0)
        def _():
            dw_ref[...] = jnp.zeros_like(dw_ref)
            dga_ref[...] = jnp.zeros_like(dga_ref)
            dgs_ref[...] = jnp.zeros_like(dgs_ref)

        dw_ref[0:NA_WIDTH, :] += _dot_tn(ca, dm)
        dw_ref[NA_WIDTH:, :] += _dot_tn(cs, dm)
        dga_ref[...] += jnp.sum(dga, axis=0, keepdims=True)
        dgs_ref[...] += jnp.sum(dgs, axis=0, keepdims=True)

    row = _row_spec(NA_WIDTH)
    vec = _fix_spec((1, NA_WIDTH))
    mat = _fix_spec((D_MODEL, D_MODEL))
    heads = pl.BlockSpec((HEADS, ROW_TILE, HEAD_DIM), lambda i: (0, i, 0))
    return pl.pallas_call(
        body, grid=(N_ROW_TILES,), in_specs=[_row_spec(D_MODEL), heads, row, vec, vec, mat],
        out_specs=[heads, row, mat, vec, vec],
        out_shape=[SDS((HEADS, SEQ, HEAD_DIM), F32), SDS((SEQ, NA_WIDTH), F32), SDS((D_MODEL, D_MODEL), F32),
                   SDS((1, NA_WIDTH), F32), SDS((1, NA_WIDTH), F32)],
        name="mix_out_bwd", compiler_params=_params(("arbitrary",)))(dmix, ona, os5, g_na, g_s5, wout)


def _me():
    x, y, c = lax.axis_index("x"), lax.axis_index("y"), lax.axis_index("c")
    return x, y, c, 4 * x + 2 * y + c


def _peer(k):
    x, y, c, _ = _me()
    px = 1 - x if (k >> 2) & 1 else x
    py = 1 - y if (k >> 1) & 1 else y
    pc = 1 - c if k & 1 else c
    return (px, py, pc), 4 * px + 2 * py + pc


ALL_PEERS = (1, 2, 3, 4, 5, 6, 7)
CHIP_PEERS = (2, 4, 6)
SIBLING = 1


def _slot8(pos):
    return 4 * pos[0] + 2 * pos[1] + pos[2]


def _slot4(pos):
    return 2 * pos[0] + pos[1]


_HBM = pl.BlockSpec(memory_space=pltpu.HBM)
_SEM = pl.BlockSpec(memory_space=pltpu.SEMAPHORE)
_EFFECT = pltpu.SideEffectType.DATAFLOW_SIDE_EFFECTING


def _exchange_start(arrays, lands, gather, name, peers=ALL_PEERS, slot=_slot8, own=True):
    n = len(arrays)

    def body(*refs):
        ins, lnd = refs[:n], refs[n:2 * n]
        send_sems, recv_sems = refs[2 * n], refs[2 * n + 1]
        token = refs[-1]
        me = slot(_me()[:3])
        for i, k in enumerate(peers):
            peer, _ = _peer(k)
            for a in range(n):
                src = ins[a] if gather else ins[a].at[slot(peer)]
                s = a * len(peers) + i
                pltpu.make_async_remote_copy(src_ref=src, dst_ref=lnd[a].at[me], send_sem=send_sems.at[s],
                                             recv_sem=recv_sems.at[s], device_id=peer, device_id_type=MESH).start()
        if own:
            for a in range(n):
                pltpu.make_async_copy(ins[a] if gather else ins[a].at[me], lnd[a].at[me], recv_sems.at[n * len(peers) + a]).start()
        token[...] = jnp.zeros_like(token)

    sems = pltpu.SemaphoreType.DMA((n * (len(peers) + int(own)),))
    out = pl.pallas_call(
        body, name=name, in_specs=[_HBM] * (2 * n),
        out_shape=(sems, sems) + tuple(pltpu.HBM(a.shape, a.dtype) for a in list(arrays) + list(lands)) + (SDS((8, 128), F32),),
        out_specs=(_SEM, _SEM) + (_HBM,) * (2 * n) + (pl.BlockSpec(memory_space=pltpu.VMEM),),
        input_output_aliases={i: 2 + i for i in range(2 * n)},
        compiler_params=pltpu.CompilerParams(has_side_effects=_EFFECT),
    )(*[pltpu.with_memory_space_constraint(a, pltpu.HBM) for a in list(arrays) + list(lands)])
    return out[0], out[1], list(out[2:2 + n]), list(out[2 + n:2 + 2 * n]), out[-1]


def _exchange_wait(send_sems, recv_sems, arrays, lands, after, gather, name, peers=ALL_PEERS, slot=_slot8, own=True):
    n = len(arrays)

    def body(*refs):
        ins, lnd = refs[:n], refs[n:2 * n]
        send_sems, recv_sems = refs[2 * n], refs[2 * n + 1]
        if own:
            me = slot(_me()[:3])
            for a in range(n):
                pltpu.make_async_copy(ins[a] if gather else ins[a].at[me], lnd[a].at[me], recv_sems.at[n * len(peers) + a]).wait()
        for i, k in enumerate(peers):
            peer, _ = _peer(k)
            for a in range(n):
                src = ins[a] if gather else ins[a].at[slot(peer)]
                s = a * len(peers) + i
                cp = pltpu.make_async_remote_copy(src_ref=src, dst_ref=lnd[a].at[slot(peer)], send_sem=send_sems.at[s],
                                                  recv_sem=recv_sems.at[s], device_id=peer, device_id_type=MESH)
                cp.wait_send()
                cp.wait_recv()

        refs[-1][...] = jnp.zeros_like(refs[-1])

    after = list(after) if isinstance(after, (list, tuple)) else [after]
    out = pl.pallas_call(
        body, name=name, in_specs=[_HBM] * (2 * n) + [_SEM, _SEM] + [pl.BlockSpec(memory_space=pl.ANY)] * len(after),
        out_shape=tuple(pltpu.HBM(a.shape, a.dtype) for a in list(arrays) + list(lands)) + (SDS((8, 128), F32),),
        out_specs=(_HBM,) * (2 * n) + (pl.BlockSpec(memory_space=pltpu.VMEM),), input_output_aliases={i: i for i in range(2 * n)},
        compiler_params=pltpu.CompilerParams(has_side_effects=_EFFECT),
    )(*arrays, *lands, send_sems, recv_sems, *after)
    return list(out[n:2 * n]), out[-1]


def _forward_sibling(lands, name):
    n = len(lands)

    def body(*refs):
        outs = refs[n:2 * n]
        send_sems, recv_sems = refs[2 * n:]
        x, y, c, _ = _me()
        sends = []
        for i, k in enumerate(CHIP_PEERS):
            peer, _ = _peer(k)
            for a in range(n):
                rows = outs[a].at[_slot8(peer)]
                cp = pltpu.make_async_remote_copy(src_ref=rows, dst_ref=rows, send_sem=send_sems.at[a, i], recv_sem=recv_sems.at[a, i],
                                                  device_id=(x, y, 1 - c), device_id_type=MESH)
                cp.start()
                sends.append(cp)
        for i, k in enumerate(CHIP_PEERS):
            (px, py, pc), _ = _peer(k)
            for a in range(n):
                rows = outs[a].at[_slot8((px, py, 1 - pc))]
                pltpu.make_async_remote_copy(src_ref=rows, dst_ref=rows, send_sem=send_sems.at[a, i], recv_sem=recv_sems.at[a, i],
                                             device_id=(x, y, 1 - c), device_id_type=MESH).wait_recv()
        for cp in sends:
            cp.wait_send()

    return pl.pallas_call(
        body, in_specs=[_HBM] * n, out_specs=[_HBM] * n, out_shape=[SDS(a.shape, a.dtype) for a in lands],
        input_output_aliases={i: i for i in range(n)},
        scratch_shapes=[pltpu.SemaphoreType.DMA((n, len(CHIP_PEERS))), pltpu.SemaphoreType.DMA((n, len(CHIP_PEERS)))],
        name=name)(*lands)


def _swap_sibling(arrays, name, after=()):
    n, n_after = len(arrays), len(after)
    chips = N_DEV // 2

    def body(*refs):
        ins, outs = refs[:n], refs[n + n_after:2 * n + n_after]
        send_sems, recv_sems = refs[2 * n + n_after:]
        x, y, c, _ = _me()
        sends = []
        for q in range(chips):
            for a in range(n):
                cp = pltpu.make_async_remote_copy(src_ref=ins[a].at[q, 1 - c], dst_ref=outs[a].at[q], send_sem=send_sems.at[a, q],
                                                  recv_sem=recv_sems.at[a, q], device_id=(x, y, 1 - c), device_id_type=MESH)
                cp.start()
                sends.append(cp)
        for cp in sends:
            cp.wait_recv()
        for cp in sends:
            cp.wait_send()

    return pl.pallas_call(
        body, in_specs=[_HBM] * n + [pl.BlockSpec(memory_space=pl.ANY)] * n_after, out_specs=[_HBM] * n,
        out_shape=[SDS((chips,) + a.shape[2:], a.dtype) for a in arrays],
        scratch_shapes=[pltpu.SemaphoreType.DMA((n, chips)), pltpu.SemaphoreType.DMA((n, chips))], name=name)(*arrays, *after)


def _sum_pairs(mine, theirs, name):
    n = len(mine)
    chips = mine[0].shape[0]
    c = lax.axis_index("c")

    def body(c_ref, *refs):
        for a in range(n):
            refs[2 * n + a][...] = (refs[a][...].astype(F32) + refs[n + a][...].astype(F32)).astype(refs[2 * n + a].dtype)

    def pair(a):
        return pl.BlockSpec((None, None) + a.shape[2:], lambda q, c_ref: (q, c_ref[0], 0, 0))

    def single(a):
        return pl.BlockSpec((None,) + a.shape[2:], lambda q, c_ref: (q, 0, 0))

    return pl.pallas_call(
        body, grid_spec=pltpu.PrefetchScalarGridSpec(
            num_scalar_prefetch=1, grid=(chips,), in_specs=[pair(a) for a in mine] + [single(a) for a in mine],
            out_specs=[single(a) for a in mine]),
        out_shape=[SDS((chips,) + a.shape[2:], a.dtype) for a in mine], name=name,
        compiler_params=_params(("parallel",)))(c.reshape(1).astype(jnp.int32), *mine, *theirs)


def _adamw_math(w, g, m, v):
    m = ADAM_B1 * m + (1.0 - ADAM_B1) * g
    v = ADAM_B2 * v + (1.0 - ADAM_B2) * (g * g)
    m_hat = m / (1.0 - ADAM_B1 ** ADAM_STEP)
    v_hat = v / (1.0 - ADAM_B2 ** ADAM_STEP)
    delta = -ADAM_LR * (m_hat / (jnp.sqrt(v_hat) + ADAM_EPS) + ADAM_WD * w)
    return delta, m, v


def _adamw(w, m, v, pieces, name):
    rows, cols = w.shape[-2:]
    lead = w.ndim - 2
    tile = rows
    for cand in (256, 176, 128, 64, 16):
        if rows > cand and rows % cand == 0:
            tile = cand
            break

    def body(w_ref, m_ref, v_ref, p_ref, g_ref, d_ref, mo_ref, vo_ref):
        g = _sum_pieces(p_ref)
        g_ref[...] = g
        d_ref[...], mo_ref[...], vo_ref[...] = _adamw_math(w_ref[...], g, m_ref[...], v_ref[...])

    blk = pl.BlockSpec((None,) * lead + (tile, cols), lambda i: (0,) * lead + (i, 0))
    return pl.pallas_call(
        body, grid=(rows // tile,), in_specs=[blk, blk, blk, pl.BlockSpec((pieces.shape[0], tile, cols), lambda i: (0, i, 0))],
        out_specs=[blk] * 4, out_shape=[SDS(w.shape, F32)] * 4, name=name,
        compiler_params=_params(("parallel",)))(w, m, v, pieces)


def _sum_pieces(p_ref):
    g = p_ref[0].astype(F32)
    for p in range(1, p_ref.shape[0]):
        g = g + p_ref[p].astype(F32)
    return g


def _adamw_s5_mat(w, m, v, g, name):
    _, ndir, groups, b, c = w.shape
    per_dir = groups // 8

    def body(w_ref, m_ref, v_ref, g_ref, d_ref, mo_ref, vo_ref):
        d_ref[...], mo_ref[...], vo_ref[...] = _adamw_math(w_ref[...], g_ref[...], m_ref[...], v_ref[...])

    blk = pl.BlockSpec((None, None, 8, b, c), lambda i: (0, i // per_dir, i % per_dir, 0, 0))
    return pl.pallas_call(
        body, grid=(ndir * per_dir,), in_specs=[blk] * 4, out_specs=[blk] * 3, out_shape=[SDS(w.shape, F32)] * 3, name=name,
        compiler_params=_params(("parallel",)))(w, m, v, g)


VEC_ROWS = ['ffn1_pre_g', 'ffn1_post_g', 'mix_pre_g', 'mix_post_g', 'ffn2_pre_g', 'ffn2_post_g', 'final_g',
            ('na_out_g', 's5_out_g'), ('s5_d', 's5_b_glu')]
VEC_NAMES = [n for row in VEC_ROWS for n in ((row,) if isinstance(row, str) else row)]
VEC_PACK_ROWS = 16
LOSS_ROW = len(VEC_ROWS)


def _pack_vectors(grads, loss8):
    def body(*refs):
        o_ref = refs[-1]
        o_ref[...] = jnp.zeros_like(o_ref)
        o_ref[LOSS_ROW:LOSS_ROW + 1, 0:128] = refs[-2][0:1, :]
        k = 0
        for i, row in enumerate(VEC_ROWS):
            if isinstance(row, str):
                o_ref[i:i + 1, :] = refs[k][...]
                k += 1
            else:
                o_ref[i:i + 1, 0:NA_WIDTH] = refs[k][...]
                o_ref[i:i + 1, NA_WIDTH:] = refs[k + 1][...]
                k += 2

    return pl.pallas_call(body, out_shape=SDS((VEC_PACK_ROWS, D_MODEL), F32), name="pack_vectors",
                          compiler_params=_params())(*[grads[n] for n in VEC_NAMES], loss8)


def _sum8(pieces, name):
    def body(p_ref, o_ref):
        o_ref[...] = _sum_pieces(p_ref)

    return pl.pallas_call(body, out_shape=SDS(pieces.shape[1:], F32), name=name, compiler_params=_params())(pieces)


def _adamw_small(packed8, vec_wmv, others):
    n_vec, n_oth = len(VEC_NAMES), len(others)

    def body(*refs):
        p_ref = refs[0]
        ins = refs[1:1 + 3 * n_vec + 4 * n_oth]
        outs = refs[1 + 3 * n_vec + 4 * n_oth:]
        gsum = _sum_pieces(p_ref)
        outs[-1][...] = gsum[LOSS_ROW:LOSS_ROW + 1, 0:128]
        k = 0
        for i, row in enumerate(VEC_ROWS):
            parts = [(row, gsum[i:i + 1, :])] if isinstance(row, str) else \
                [(row[0], gsum[i:i + 1, 0:NA_WIDTH]), (row[1], gsum[i:i + 1, NA_WIDTH:])]
            for _, g in parts:
                w_ref, m_ref, v_ref = ins[3 * k:3 * k + 3]
                outs[4 * k][...] = g
                outs[4 * k + 1][...], outs[4 * k + 2][...], outs[4 * k + 3][...] = _adamw_math(w_ref[...], g, m_ref[...], v_ref[...])
                k += 1
        for j in range(n_oth):
            w_ref, m_ref, v_ref, g_ref = ins[3 * n_vec + 4 * j:3 * n_vec + 4 * j + 4]
            g = _sum_pieces(g_ref)
            g = g[tuple(slice(0, s) for s in w_ref.shape[1:])].reshape(w_ref.shape)
            o = outs[4 * (n_vec + j):4 * (n_vec + j) + 4]
            o[0][...] = g
            o[1][...], o[2][...], o[3][...] = _adamw_math(w_ref[...], g, m_ref[...], v_ref[...])

    args, out_shape = [packed8], []
    for w, m, v in vec_wmv:
        args += [w, m, v]
        out_shape += [SDS(w.shape, F32)] * 4
    for w, m, v, g in others:
        args += [w, m, v, g]
        out_shape += [SDS(w.shape, F32)] * 4
    out_shape += [SDS((1, 128), F32)]
    return pl.pallas_call(body, out_shape=out_shape, name="adamw_small", compiler_params=_params())(*args)


def _perm_rows(x):
    return x.reshape(SCAN_BLOCKS, SCAN_T, x.shape[-1]).transpose(1, 0, 2).reshape(SEQ, x.shape[-1])


def _unperm_rows(x):
    return x.reshape(SCAN_T, SCAN_BLOCKS, x.shape[-1]).transpose(1, 0, 2).reshape(SEQ, x.shape[-1])


def _block_diag(x):
    eye = np.eye(8, dtype=bool)[None, None, :, None, :, None]
    full = jnp.where(eye, x[:, :, :, :, None, :], 0.0)
    return full.reshape(2, S5_CHUNKS, 8 * x.shape[3], 8 * x.shape[4])


STORED_SWAPPED = {"ffn1_w_gate": (1, 2), "ffn1_w_up": (1, 2), "ffn2_w_gate": (1, 2), "ffn2_w_up": (1, 2),
                  "s5_b_re": (3, 4), "s5_b_im": (3, 4)}


def _stored(name, x):
    return jnp.swapaxes(x, *STORED_SWAPPED[name]) if name in STORED_SWAPPED else x


def _dep(x, token):
    return x if token is None else x + token


def _local_step(x, target, get_w, small, emit):
    bias = _rpb_expand(small["na_rpb"][0])
    lr = small["s5_lam_re"].reshape(64, S5_STATE)
    li = small["s5_lam_im"].reshape(64, S5_STATE)
    logdt = small["s5_log_dt"].reshape(64, 1)
    b_t = [_stored(n, small[n]).reshape(64, S5_GROUP, S5_STATE) for n in ("s5_b_re", "s5_b_im")]
    lbr, lbi, bbr, bbi = _s5_prep(lr, li, logdt, b_t[0], b_t[1])
    are = lbr.reshape(2, S5_CHUNKS, 1, ST_W)
    aim = lbi.reshape(2, S5_CHUNKS, 1, ST_W)
    bre = _block_diag(bbr.reshape(2, S5_CHUNKS, 8, S5_GROUP, S5_STATE)).astype(BF16)
    bim = _block_diag(bbi.reshape(2, S5_CHUNKS, 8, S5_GROUP, S5_STATE)).astype(BF16)
    c_t = [small[n].reshape(2, S5_CHUNKS, 8, S5_GROUP, S5_STATE).transpose(0, 1, 2, 4, 3) for n in ("s5_c_re", "s5_c_im")]
    cre = _block_diag(c_t[0]).astype(BF16)
    cim = _block_diag(c_t[1]).astype(BF16)
    tgt = jnp.concatenate([jnp.zeros((N_META, D_MODEL), F32), target], axis=0)

    h0, a1 = _embed_prenorm(get_w("meta", None)["meta_tokens"], x, small["ffn1_pre_g"])
    wts = dict(get_w("ffn1", [bias, are, aim, bre, bim, cre, cim, tgt, a1]))
    gate1, up1, f1 = _ffn_fwd(a1, wts["ffn1_w_gate"], wts["ffn1_w_up"], wts["ffn1_w_down"], "ffn1_fwd",
                              after=wts.get("tokens", ()))
    h1, a2 = _post_pre(f1, h0, small["ffn1_post_g"], small["mix_pre_g"], 0.5, "post_pre1")
    wts.update(get_w("w_in", a2))
    qkv = _proj_heads(a2, wts["w_in"])
    u = _proj_u(a2, wts["w_in"])
    ona = _na_fwd(qkv, bias)
    u_p = _perm_rows(u)
    sr, si, y2 = _s5_scan_fwd(u_p, bre, bim, are, aim, cre, cim)
    wts.update(get_w("mix", y2))
    os5_p, ypre_p = _s5_glu_fwd(u_p, y2, small["s5_d"], wts["s5_w_glu"], small["s5_b_glu"])
    os5 = _unperm_rows(os5_p)

    mix = _mix_out_fwd(ona, os5, small["na_out_g"], small["s5_out_g"], wts["w_out"])
    h2, a3 = _post_pre(mix, h1, small["mix_post_g"], small["ffn2_pre_g"], 1.0, "post_pre2")
    wts.update(get_w("ffn2", a3))
    gate2, up2, f2 = _ffn_fwd(a3, wts["ffn2_w_gate"], wts["ffn2_w_up"], wts["ffn2_w_down"], "ffn2_fwd")
    loss8, dh3, df2, g_final, g_ffn2_post = _final_loss(f2, h2, small["ffn2_post_g"], small["final_g"], tgt)

    da3, dwg2, dwu2, dwd2 = _ffn_bwd(df2, a3, gate2, up2, wts["ffn2_w_gate"], wts["ffn2_w_up"], wts["ffn2_w_down"], "ffn2_bwd")
    tok = emit("ffn2", {"ffn2_w_gate": dwg2, "ffn2_w_up": dwu2, "ffn2_w_down": dwd2})
    dh2, dmix, g_ffn2_pre, g_mix_post = _bwd_pre_post(da3, h2, _dep(small["ffn2_pre_g"], tok), dh3, mix, small["mix_post_g"], 1.0,
                                                      "bwd_pre_post2")
    dona, dos5, dwout, g_na_out, g_s5_out = _mix_out_bwd(dmix, ona, os5, small["na_out_g"], small["s5_out_g"], wts["w_out"])

    dypre_p, du_skip_p, dwglu, g_b_glu, g_s5_d = _s5_glu_bwd(_perm_rows(dos5), ypre_p, u_p, small["s5_d"], wts["s5_w_glu"],
                                                             small["s5_b_glu"])
    tok = emit("mix", {"s5_w_glu": dwglu.reshape(N_DEV, S5_WIDTH // N_DEV, S5_WIDTH).astype(BF16),
                       "w_out": dwout.reshape(N_DEV, D_MODEL // N_DEV, D_MODEL).astype(BF16)})
    du_p, dbr, dbi, dcr, dci, dar, dai = _s5_scan_bwd(dypre_p, du_skip_p, u_p, sr, si, bre, bim, _dep(are, tok), aim, cre, cim)
    du = _unperm_rows(du_p)
    per_group = (2 * S5_GROUPS, S5_GROUP, S5_STATE)
    g_lr, g_li, g_dt, g_br, g_bi = _s5_prep_bwd(lr, li, logdt, b_t[0], b_t[1], dar.reshape(64, S5_STATE),
                                                dai.reshape(64, S5_STATE), dbr.reshape(per_group), dbi.reshape(per_group))
    g_c = [dcr.reshape(per_group), dci.reshape(per_group)]

    dq, dk, dv, dbias = _na_bwd(qkv, bias, dona)
    g_rpb = _rpb_reduce(dbias)
    dense = jnp.stack([g.reshape(2 * S5_GROUPS, S5_STATE * S5_GROUP) for g in (g_br, g_bi, *g_c)])
    tok = emit("small", {"dense": dense, "na_rpb": g_rpb,
                         "s5_lam_re": g_lr.reshape(2, S5_GROUPS, S5_STATE), "s5_lam_im": g_li.reshape(2, S5_GROUPS, S5_STATE),
                         "s5_log_dt": g_dt.reshape(2, S5_GROUPS)})
    da2, dwin = _proj_bwd(dq, dk, dv, du, a2, wts["w_in"])
    tok2 = emit("w_in", {"w_in": dwin})
    tok = tok if tok2 is None else tok + tok2
    dh1, df1, g_mix_pre, g_ffn1_post = _bwd_pre_post(da2, h1, _dep(small["mix_pre_g"], tok), dh2, f1, small["ffn1_post_g"], 0.5,
                                                     "bwd_pre_post1")
    da1, dwg1, dwu1, dwd1 = _ffn_bwd(df1, a1, gate1, up1, wts["ffn1_w_gate"], wts["ffn1_w_up"], wts["ffn1_w_down"], "ffn1_bwd")
    grad_x, grad_meta, g_ffn1_pre = _bwd_embed(da1, h0, small["ffn1_pre_g"], dh1)
    vec_g = {
        "ffn1_pre_g": g_ffn1_pre, "ffn1_post_g": g_ffn1_post, "mix_pre_g": g_mix_pre, "s5_d": g_s5_d, "s5_b_glu": g_b_glu,
        "na_out_g": g_na_out, "s5_out_g": g_s5_out, "mix_post_g": g_mix_post,
        "ffn2_pre_g": g_ffn2_pre, "ffn2_post_g": g_ffn2_post, "final_g": g_final,
    }
    emit("vec", {"packed": _pack_vectors(vec_g, loss8), "meta_tokens": grad_meta})
    emit("ffn1", {"ffn1_w_gate": dwg1, "ffn1_w_up": dwu1, "ffn1_w_down": dwd1})
    return grad_x


WEIGHT_NAMES = ['meta_tokens', 'ffn1_pre_g', 'ffn1_post_g', 'ffn1_w_gate', 'ffn1_w_up', 'ffn1_w_down', 'mix_pre_g', 'w_in',
                'na_rpb', 's5_lam_re', 's5_lam_im', 's5_log_dt', 's5_b_re', 's5_b_im', 's5_c_re', 's5_c_im', 's5_d',
                's5_w_glu', 's5_b_glu', 'na_out_g', 's5_out_g', 'w_out', 'mix_post_g', 'ffn2_pre_g', 'ffn2_post_g',
                'ffn2_w_gate', 'ffn2_w_up', 'ffn2_w_down', 'final_g']
BIG_NAMES = ['ffn1_w_gate', 'ffn1_w_up', 'ffn1_w_down', 'w_in', 's5_w_glu', 'w_out', 'ffn2_w_gate', 'ffn2_w_up', 'ffn2_w_down']
SMALL_NAMES = [n for n in WEIGHT_NAMES if n not in BIG_NAMES and n != 'meta_tokens']
WHOLE_NAMES = ['na_rpb', 's5_lam_re', 's5_lam_im', 's5_log_dt']
LEAD_NAMES = ['s5_b_re', 's5_b_im', 's5_c_re', 's5_c_im']


def kernel(x, meta_tokens, ffn1_pre_g, ffn1_post_g, ffn1_w_gate, ffn1_w_up, ffn1_w_down, mix_pre_g, w_in, na_rpb, s5_lam_re, s5_lam_im, s5_log_dt, s5_b_re, s5_b_im, s5_c_re, s5_c_im, s5_d, s5_w_glu, s5_b_glu, na_out_g, s5_out_g, w_out, mix_post_g, ffn2_pre_g, ffn2_post_g, ffn2_w_gate, ffn2_w_up, ffn2_w_down, final_g, loss_target, m_meta_tokens, m_ffn1_pre_g, m_ffn1_post_g, m_ffn1_w_gate, m_ffn1_w_up, m_ffn1_w_down, m_mix_pre_g, m_w_in, m_na_rpb, m_s5_lam_re, m_s5_lam_im, m_s5_log_dt, m_s5_b_re, m_s5_b_im, m_s5_c_re, m_s5_c_im, m_s5_d, m_s5_w_glu, m_s5_b_glu, m_na_out_g, m_s5_out_g, m_w_out, m_mix_post_g, m_ffn2_pre_g, m_ffn2_post_g, m_ffn2_w_gate, m_ffn2_w_up, m_ffn2_w_down, m_final_g, v_meta_tokens, v_ffn1_pre_g, v_ffn1_post_g, v_ffn1_w_gate, v_ffn1_w_up, v_ffn1_w_down, v_mix_pre_g, v_w_in, v_na_rpb, v_s5_lam_re, v_s5_lam_im, v_s5_log_dt, v_s5_b_re, v_s5_b_im, v_s5_c_re, v_s5_c_im, v_s5_d, v_s5_w_glu, v_s5_b_glu, v_na_out_g, v_s5_out_g, v_w_out, v_mix_post_g, v_ffn2_pre_g, v_ffn2_post_g, v_ffn2_w_gate, v_ffn2_w_up, v_ffn2_w_down, v_final_g):
    args = dict(locals())
    w = {n: args[n] for n in WEIGHT_NAMES}
    m = {n: args["m_" + n] for n in WEIGHT_NAMES}
    v = {n: args["v_" + n] for n in WEIGHT_NAMES}

    small = {n: w[n] for n in SMALL_NAMES}

    pending = {}

    def start(group, names, arrays, gather, peers=ALL_PEERS, slot=_slot8):
        n_slots = N_DEV if slot is _slot8 else N_DEV // 2
        lands = [lax.empty((n_slots,) + a.shape if gather else a.shape, a.dtype) for a in arrays]
        send_sems, recv_sems, arrays, lands, token = _exchange_start(arrays, lands, gather, "start_" + group, peers, slot)
        pending[group] = (names, send_sems, recv_sems, arrays, lands, gather, peers, slot)
        return token

    def finish(group, after):
        names, send_sems, recv_sems, arrays, lands, gather, peers, slot = pending.pop(group)
        lands, token = _exchange_wait(send_sems, recv_sems, arrays, lands, after, gather, "wait_" + group, peers, slot)
        return dict(zip(names, lands)), token

    first = ["ffn1_w_gate", "ffn1_w_up", "ffn1_w_down"]
    def shard(n, token=None):
        return _dep(_stored(n, w[n])[0], None if token is None else token[0, 0]).astype(BF16)

    ffn_names = ("ffn1_w_gate", "ffn1_w_up", "ffn1_w_down", "ffn2_w_gate", "ffn2_w_up", "ffn2_w_down")
    later_groups = (("w_in", ["w_in"]), ("mix", ["s5_w_glu", "w_out"]), ("ffn2", ["ffn2_w_gate", "ffn2_w_up", "ffn2_w_down"]))
    token0 = start("meta", ["meta_tokens"], [w["meta_tokens"]], True)
    token1 = start("ffn1", first, [shard(n, token0) for n in first], True, (SIBLING,) + CHIP_PEERS)
    meta_full = finish("meta", [token1])[0]["meta_tokens"].transpose(1, 0, 2).reshape(N_META, D_MODEL)
    later_shards = {n: shard(n, token1) for _, names in later_groups for n in names}
    for n in ("na_rpb", "s5_lam_re"):
        small[n] = _dep(small[n], token1[0, 0])

    def get_w(group, after):
        if group == "meta":
            return {"meta_tokens": meta_full}
        if group == "ffn1":
            after = list(after) + list(later_shards.values())
        got, token = finish(group, after)
        if group == "ffn1":
            got = dict(zip(got, _forward_sibling(list(got.values()), "forward_ffn1")))
            got["tokens"] = [start(g, names + ["order"], [later_shards[n] for n in names] + [token], True) for g, names in later_groups]
        if group == "mix":
            got = {"s5_w_glu": got["s5_w_glu"].reshape(S5_WIDTH, S5_WIDTH), "w_out": got["w_out"].reshape(D_MODEL, D_MODEL)}
        return {n: (a.reshape(D_FF, D_MODEL) if n in ffn_names else a) for n, a in got.items()}

    tokens = {}

    def emit(group, grads):
        grads = {n: (g.reshape(N_DEV, FF_SHARD, D_MODEL) if n in ffn_names else g) for n, g in grads.items()}
        if group == "ffn1":
            mine = [g.reshape((N_DEV // 2, 2) + g.shape[1:]) for g in grads.values()]
            theirs = _swap_sibling(mine, "swap_g_ffn1", after=[tokens["vec"]])
            sums = _sum_pairs(mine, theirs, "pair_sum_g_ffn1")
            tokens[group] = start("g_ffn1", list(grads), sums, False, CHIP_PEERS, _slot4)
        else:
            tokens[group] = start("g_" + group, list(grads), list(grads.values()), group in ("small", "vec"))
        return tokens[group][0, 0]

    grad_x = _local_step(x[0], loss_target[0], get_w, small, emit)
    res = {}

    def update_shard(n, pieces):
        outs = _adamw(_stored(n, w[n]), _stored(n, m[n]), _stored(n, v[n]), pieces, "adamw_" + n)
        res[n] = [_stored(n, o) for o in outs]

    late = [grad_x, tokens["ffn1"]]
    for group in ("g_ffn2", "g_mix", "g_w_in"):
        for n, pieces in finish(group, late)[0].items():
            update_shard(n, pieces)
    g8 = finish("g_small", late)[0]
    dense = _sum8(g8["dense"], "sum_dense")
    for i, n in enumerate(LEAD_NAMES):
        g = dense[i].reshape(_stored(n, w[n]).shape)
        upd = _adamw_s5_mat(_stored(n, w[n]), _stored(n, m[n]), _stored(n, v[n]), g, "adamw_" + n)
        res[n] = [_stored(n, o) for o in [g] + list(upd)]

    done = [res[n][1] for n in ("ffn2_w_gate", "ffn2_w_up", "ffn2_w_down", "w_in", "w_out", "s5_w_glu") + tuple(LEAD_NAMES)]
    got = finish("g_vec", done)[0]
    packed8, gmeta8 = got["packed"], got["meta_tokens"]
    for n, pieces in finish("g_ffn1", packed8)[0].items():
        update_shard(n, pieces)
    _, _, _, me = _me()
    update_shard("meta_tokens", lax.dynamic_slice_in_dim(gmeta8, me * (D_MODEL // N_DEV), D_MODEL // N_DEV, axis=2))

    outs = _adamw_small(packed8, [(w[n], m[n], v[n]) for n in VEC_NAMES], [(w[n], m[n], v[n], g8[n]) for n in WHOLE_NAMES])
    for i, n in enumerate(VEC_NAMES + WHOLE_NAMES):
        res[n] = list(outs[4 * i:4 * i + 4])

    out = [outs[-1][0, 0], grad_x[None]]
    for kind in range(4):
        out += [res[n][kind] for n in WEIGHT_NAMES]
    return tuple(out)
```

```python
import math

import numpy as np
import jax
import jax.numpy as jnp
from jax import lax
from jax.experimental import pallas as pl
from jax.experimental.pallas import tpu as pltpu

F32 = jnp.float32
BF16 = jnp.bfloat16
SDS = jax.ShapeDtypeStruct

D_MODEL = 1024
N_TOK = 2048
N_META = 16
SEQ = N_TOK + N_META
ROW_TILE = 688
N_ROW_TILES = SEQ // ROW_TILE
N_DEV = 8
D_FF = 2816
FF_SHARD = D_FF // N_DEV
FF_TILE = 256
IN_SHARD = 256
NA_WIDTH = 512
S5_WIDTH = 512
HEADS = 8
HEAD_DIM = 64
GRID_W = 64
GRID_ROWS = N_TOK // GRID_W
KH = 8
KW = 16
NA_RB = 4
NA_KR = KH + NA_RB - 1
NA_BLOCKS = GRID_ROWS // NA_RB
NA_QB = NA_RB * GRID_W
NA_KB = NA_KR * GRID_W
NA_TYPES = 3
S5_GROUPS = 32
S5_GROUP = 16
S5_STATE = 64
S5_CHUNKS = 4
CH_W = S5_WIDTH // S5_CHUNKS
ST_W = S5_GROUPS * S5_STATE // S5_CHUNKS
SCAN_BLOCKS = 8
SCAN_T = SEQ // SCAN_BLOCKS
RMS_EPS = 1e-6
NEG_INF = -1e30
ATT_SCALE = HEAD_DIM ** -0.5
ADAM_LR, ADAM_B1, ADAM_B2, ADAM_EPS, ADAM_WD, ADAM_STEP = 0.001, 0.9, 0.999, 1e-08, 0.01, 10
VMEM_LIMIT = 56 * 1024 * 1024
MESH = pl.DeviceIdType.MESH


def _params(sem=None):
    return pltpu.CompilerParams(dimension_semantics=sem, vmem_limit_bytes=VMEM_LIMIT)


def _dot(a, b):
    return jnp.dot(a, b, preferred_element_type=F32)


def _dot_nt(a, b):
    return lax.dot_general(a, b, (((1,), (1,)), ((), ())), preferred_element_type=F32)


def _dot_tn(a, b):
    return lax.dot_general(a, b, (((0,), (0,)), ((), ())), preferred_element_type=F32)


def _rstd(x):
    return lax.rsqrt(jnp.mean(x * x, axis=-1, keepdims=True) + RMS_EPS)


def _rms_bwd(x, r, g, dy):
    dyg = dy * g
    xr = x * r
    dx = r * (dyg - xr * jnp.mean(dyg * xr, axis=-1, keepdims=True))
    return dx, dy * xr


def _rows(i, size=ROW_TILE):
    return pl.ds(pl.multiple_of(i * size, 16), size)


def _row_spec(width):
    return pl.BlockSpec((ROW_TILE, width), lambda i: (i, 0))


def _fix_spec(shape):
    return pl.BlockSpec(shape, lambda i: (0,) * len(shape))


def _split3(x):
    hi = x.astype(BF16)
    r1 = x - hi.astype(F32)
    mid = r1.astype(BF16)
    lo = (r1 - mid.astype(F32)).astype(BF16)
    return hi, mid, lo


def _embed_prenorm(meta, x, g):
    def body(m_ref, x_ref, g_ref, h_ref, a_ref):
        h_ref[0:N_META, :] = m_ref[...]
        h_ref[N_META:, :] = x_ref[...]
        for i in range(N_ROW_TILES):
            rows = slice(i * ROW_TILE, (i + 1) * ROW_TILE)
            hv = h_ref[rows, :]
            a_ref[rows, :] = (hv * _rstd(hv) * g_ref[...]).astype(BF16)

    return pl.pallas_call(
        body, out_shape=[SDS((SEQ, D_MODEL), F32), SDS((SEQ, D_MODEL), BF16)], name="embed_prenorm",
        compiler_params=_params())(meta, x, g)


def _post_pre(f, hres, g_post, g_next, scale, name):
    def body(f_ref, h_ref, gp_ref, gn_ref, ho_ref, a_ref):
        fv = f_ref[...]
        h = h_ref[...] + scale * (fv * _rstd(fv) * gp_ref[...])
        ho_ref[...] = h
        a_ref[...] = (h * _rstd(h) * gn_ref[...]).astype(BF16)

    return pl.pallas_call(
        body, grid=(N_ROW_TILES,),
        in_specs=[_row_spec(D_MODEL), _row_spec(D_MODEL), _fix_spec((1, D_MODEL)), _fix_spec((1, D_MODEL))],
        out_specs=[_row_spec(D_MODEL), _row_spec(D_MODEL)],
        out_shape=[SDS((SEQ, D_MODEL), F32), SDS((SEQ, D_MODEL), BF16)], name=name,
        compiler_params=_params(("parallel",)))(f, hres, g_post, g_next)


def _final_loss(f2, h2, g_post, g_final, target):
    def body(f_ref, h_ref, gp_ref, gf_ref, t_ref, loss_ref, dh_ref, df_ref, dgf_ref, dgp_ref):
        i = pl.program_id(0)
        fv = f_ref[...]
        r1 = _rstd(fv)
        gp = gp_ref[...]
        h3 = h_ref[...] + 0.5 * (fv * r1 * gp)
        r2 = _rstd(h3)
        gf = gf_ref[...]
        y = h3 * r2 * gf
        row = lax.broadcasted_iota(jnp.int32, (ROW_TILE, 1), 0) + i * ROW_TILE
        err = jnp.where(row >= N_META, y - t_ref[...], 0.0)
        part = 0.5 * jnp.sum(jnp.mean(err * err, axis=-1, keepdims=True))
        dy = err * (1.0 / D_MODEL)
        dh3, dgf = _rms_bwd(h3, r2, gf, dy)
        dh_ref[...] = dh3
        df, dgp = _rms_bwd(fv, r1, gp, 0.5 * dh3)
        df_ref[...] = df.astype(BF16)

        @pl.when(i == 0)
        def _():
            loss_ref[...] = jnp.zeros_like(loss_ref)
            dgf_ref[...] = jnp.zeros_like(dgf_ref)
            dgp_ref[...] = jnp.zeros_like(dgp_ref)

        loss_ref[...] += part
        dgf_ref[...] += jnp.sum(dgf, axis=0, keepdims=True)
        dgp_ref[...] += jnp.sum(dgp, axis=0, keepdims=True)

    gain = _fix_spec((1, D_MODEL))
    return pl.pallas_call(
        body, grid=(N_ROW_TILES,),
        in_specs=[_row_spec(D_MODEL), _row_spec(D_MODEL), gain, gain, _row_spec(D_MODEL)],
        out_specs=[_fix_spec((8, 128)), _row_spec(D_MODEL), _row_spec(D_MODEL), gain, gain],
        out_shape=[SDS((8, 128), F32), SDS((SEQ, D_MODEL), F32), SDS((SEQ, D_MODEL), BF16),
                   SDS((1, D_MODEL), F32), SDS((1, D_MODEL), F32)],
        name="final_loss", compiler_params=_params(("arbitrary",)))(f2, h2, g_post, g_final, target)


def _bwd_pre_post(da, h, g_pre, dh_res, fprev, g_post, scale, name):
    def body(da_ref, h_ref, gpre_ref, dhr_ref, f_ref, gpost_ref, dh_ref, df_ref, dgpre_ref, dgpost_ref):
        i = pl.program_id(0)
        hv = h_ref[...]
        dxa, dgpre = _rms_bwd(hv, _rstd(hv), gpre_ref[...], da_ref[...])
        dh = dhr_ref[...] + dxa
        dh_ref[...] = dh
        fv = f_ref[...]
        df, dgpost = _rms_bwd(fv, _rstd(fv), gpost_ref[...], scale * dh)
        df_ref[...] = df.astype(BF16)

        @pl.when(i == 0)
        def _():
            dgpre_ref[...] = jnp.zeros_like(dgpre_ref)
            dgpost_ref[...] = jnp.zeros_like(dgpost_ref)

        dgpre_ref[...] += jnp.sum(dgpre, axis=0, keepdims=True)
        dgpost_ref[...] += jnp.sum(dgpost, axis=0, keepdims=True)

    gain = _fix_spec((1, D_MODEL))
    row = _row_spec(D_MODEL)
    return pl.pallas_call(
        body, grid=(N_ROW_TILES,), in_specs=[row, row, gain, row, row, gain],
        out_specs=[row, row, gain, gain],
        out_shape=[SDS((SEQ, D_MODEL), F32), SDS((SEQ, D_MODEL), BF16), SDS((1, D_MODEL), F32), SDS((1, D_MODEL), F32)],
        name=name, compiler_params=_params(("arbitrary",)))(da, h, g_pre, dh_res, fprev, g_post)


def _bwd_embed(da, h, g_pre, dh_res):
    def body(da_ref, h_ref, gpre_ref, dhr_ref, gx_ref, gm_ref, dgpre_ref):
        total = jnp.zeros((1, D_MODEL), F32)
        for i in range(N_ROW_TILES):
            rows = slice(i * ROW_TILE, (i + 1) * ROW_TILE)
            hv = h_ref[rows, :]
            dxa, dgpre = _rms_bwd(hv, _rstd(hv), gpre_ref[...], da_ref[rows, :])
            dh = dhr_ref[rows, :] + dxa
            total = total + jnp.sum(dgpre, axis=0, keepdims=True)
            if i == 0:
                gm_ref[...] = dh[0:N_META, :]
                gx_ref[0:ROW_TILE - N_META, :] = dh[N_META:, :]
            else:
                gx_ref[i * ROW_TILE - N_META:(i + 1) * ROW_TILE - N_META, :] = dh
        dgpre_ref[...] = total

    return pl.pallas_call(
        body, out_shape=[SDS((N_TOK, D_MODEL), F32), SDS((N_META, D_MODEL), F32), SDS((1, D_MODEL), F32)],
        name="bwd_embed", compiler_params=_params())(da, h, g_pre, dh_res)


def _ffn_fwd(a, wg, wu, wd, name, after=()):
    def body(a_ref, wg_ref, wu_ref, wd_ref, *rest):
        gate_ref, up_ref, f_ref = rest[len(after):]
        j = pl.program_id(0)

        def tile(i, carry):
            rows = _rows(i)
            at = a_ref[rows, :]
            gate = _dot_nt(at, wg_ref[...])
            up = _dot_nt(at, wu_ref[...])
            gate_ref[rows, :] = gate.astype(BF16)
            up_ref[rows, :] = up.astype(BF16)
            act = (gate * jax.nn.sigmoid(gate) * up).astype(BF16)
            contrib = _dot(act, wd_ref[...])

            @pl.when(j == 0)
            def _():
                f_ref[rows, :] = contrib

            @pl.when(j != 0)
            def _():
                f_ref[rows, :] += contrib

            return carry

        lax.fori_loop(0, N_ROW_TILES, tile, 0)

    wtile = pl.BlockSpec((FF_TILE, D_MODEL), lambda j: (j, 0))
    hid = pl.BlockSpec((SEQ, FF_TILE), lambda j: (0, j))
    full = pl.BlockSpec((SEQ, D_MODEL), lambda j: (0, 0))
    return pl.pallas_call(
        body, grid=(D_FF // FF_TILE,), in_specs=[full, wtile, wtile, wtile] + [pl.BlockSpec(memory_space=pl.ANY)] * len(after),
        out_specs=[hid, hid, full],
        out_shape=[SDS((SEQ, D_FF), BF16), SDS((SEQ, D_FF), BF16), SDS((SEQ, D_MODEL), F32)],
        name=name, compiler_params=_params(("arbitrary",)))(a, wg, wu, wd, *after)


def _ffn_bwd(df, a, gate, up, wg, wu, wd, name):
    def body(df_ref, a_ref, gate_ref, up_ref, wg_ref, wu_ref, wd_ref, da_ref, dwg_ref, dwu_ref, dwd_ref,
             acc_g, acc_u, acc_d):
        j = pl.program_id(0)

        def tile(i, carry):
            rows = _rows(i)
            dft = df_ref[rows, :]
            at = a_ref[rows, :]
            gate = gate_ref[rows, :].astype(F32)
            up = up_ref[rows, :].astype(F32)
            dact = _dot_nt(dft, wd_ref[...])
            sig = jax.nn.sigmoid(gate)
            silu = gate * sig
            dgate = (dact * up * (sig * (1.0 + gate * (1.0 - sig)))).astype(BF16)
            dup = (dact * silu).astype(BF16)
            act = (silu * up).astype(BF16)
            dwd = _dot_tn(act, dft)
            dwg = _dot_tn(dgate, at)
            dwu = _dot_tn(dup, at)
            dat = _dot(dgate, wg_ref[...]) + _dot(dup, wu_ref[...])

            @pl.when(i == 0)
            def _():
                acc_d[...] = dwd
                acc_g[...] = dwg
                acc_u[...] = dwu

            @pl.when(i != 0)
            def _():
                acc_d[...] += dwd
                acc_g[...] += dwg
                acc_u[...] += dwu

            @pl.when(j == 0)
            def _():
                da_ref[rows, :] = dat

            @pl.when(j != 0)
            def _():
                da_ref[rows, :] += dat

            return carry

        lax.fori_loop(0, N_ROW_TILES, tile, 0)
        dwg_ref[...] = acc_g[...].astype(BF16)
        dwu_ref[...] = acc_u[...].astype(BF16)
        dwd_ref[...] = acc_d[...].astype(BF16)

    wtile = pl.BlockSpec((FF_TILE, D_MODEL), lambda j: (j, 0))
    hid = pl.BlockSpec((SEQ, FF_TILE), lambda j: (0, j))
    full = pl.BlockSpec((SEQ, D_MODEL), lambda j: (0, 0))
    return pl.pallas_call(
        body, grid=(D_FF // FF_TILE,), in_specs=[full, full, hid, hid, wtile, wtile, wtile],
        out_specs=[full, wtile, wtile, wtile],
        out_shape=[SDS((SEQ, D_MODEL), F32)] + [SDS((D_FF, D_MODEL), BF16)] * 3,
        scratch_shapes=[pltpu.VMEM((FF_TILE, D_MODEL), F32)] * 3,
        name=name, compiler_params=_params(("arbitrary",)))(df, a, gate, up, wg, wu, wd)


HEADS_PER_BLOCK = IN_SHARD // HEAD_DIM
QKV_BLOCKS = 3 * NA_WIDTH // IN_SHARD


def _proj_heads(a, w):
    def body(a_ref, w_ref, o_ref):
        def tile(i, carry):
            rows = _rows(i)
            res = _dot(a_ref[rows, :], w_ref[...])
            for sub in range(HEADS_PER_BLOCK):
                o_ref[sub, rows, :] = res[:, sub * HEAD_DIM:(sub + 1) * HEAD_DIM]
            return carry

        lax.fori_loop(0, N_ROW_TILES, tile, 0)

    return pl.pallas_call(
        body, grid=(QKV_BLOCKS,),
        in_specs=[pl.BlockSpec((SEQ, D_MODEL), lambda j: (0, 0)), pl.BlockSpec((None, D_MODEL, IN_SHARD), lambda j: (j, 0, 0))],
        out_specs=pl.BlockSpec((HEADS_PER_BLOCK, SEQ, HEAD_DIM), lambda j: (j, 0, 0)),
        out_shape=SDS((3 * HEADS, SEQ, HEAD_DIM), F32), name="proj_heads",
        compiler_params=_params(("parallel",)))(a, w)


def _proj_u(a, w):
    def body(a_ref, w_ref, o_ref):
        def tile(i, carry):
            rows = _rows(i)
            o_ref[rows, :] = _dot(a_ref[rows, :], w_ref[...])
            return carry

        lax.fori_loop(0, N_ROW_TILES, tile, 0)

    return pl.pallas_call(
        body, grid=(N_DEV - QKV_BLOCKS,),
        in_specs=[pl.BlockSpec((SEQ, D_MODEL), lambda j: (0, 0)),
                  pl.BlockSpec((None, D_MODEL, IN_SHARD), lambda j: (j + QKV_BLOCKS, 0, 0))],
        out_specs=pl.BlockSpec((SEQ, IN_SHARD), lambda j: (0, j)),
        out_shape=SDS((SEQ, S5_WIDTH), F32), name="proj_u",
        compiler_params=_params(("parallel",)))(a, w)


def _proj_bwd(dq, dk, dv, du, a, w):
    def body(dq_ref, dk_ref, dv_ref, du_ref, a_ref, w_ref, da_ref, dw_ref, acc, dp_ref):
        j = pl.program_id(0)

        for which, src in enumerate((dq_ref, dk_ref, dv_ref)):
            @pl.when((j >= 2 * which) & (j < 2 * which + 2))
            def _(src=src):
                dp_ref[...] = jnp.concatenate([src[sub] for sub in range(HEADS_PER_BLOCK)], axis=-1).astype(BF16)

        @pl.when(j >= QKV_BLOCKS)
        def _():
            dp_ref[...] = du_ref[...].astype(BF16)

        def tile(i, carry):
            rows = _rows(i)
            dpt = dp_ref[rows, :]
            dw = _dot_tn(a_ref[rows, :], dpt)
            dat = _dot_nt(dpt, w_ref[...])

            @pl.when(i == 0)
            def _():
                acc[...] = dw

            @pl.when(i != 0)
            def _():
                acc[...] += dw

            @pl.when(j == 0)
            def _():
                da_ref[rows, :] = dat

            @pl.when(j != 0)
            def _():
                da_ref[rows, :] += dat

            return carry

        lax.fori_loop(0, N_ROW_TILES, tile, 0)
        dw_ref[...] = acc[...].astype(BF16)

    full = pl.BlockSpec((SEQ, D_MODEL), lambda j: (0, 0))
    wspec = pl.BlockSpec((None, D_MODEL, IN_SHARD), lambda j: (j, 0, 0))

    def heads(which):
        return pl.BlockSpec((HEADS_PER_BLOCK, SEQ, HEAD_DIM), lambda j: (jnp.clip(j - 2 * which, 0, 1), 0, 0))

    return pl.pallas_call(
        body, grid=(N_DEV,),
        in_specs=[heads(0), heads(1), heads(2),
                  pl.BlockSpec((SEQ, IN_SHARD), lambda j: (0, jnp.clip(j - QKV_BLOCKS, 0, 1))), full, wspec],
        out_specs=[full, wspec],
        out_shape=[SDS((SEQ, D_MODEL), F32), SDS((N_DEV, D_MODEL, IN_SHARD), BF16)],
        scratch_shapes=[pltpu.VMEM((D_MODEL, IN_SHARD), F32), pltpu.VMEM((SEQ, IN_SHARD), BF16)],
        name="proj_bwd", compiler_params=_params(("arbitrary",)))(dq, dk, dv, du, a, w)


def _na_consts():
    c = np.arange(GRID_W)
    col_start = np.clip(c - KW // 2, 0, GRID_W - KW)
    col_in = (c[None, :] >= col_start[:, None]) & (c[None, :] < col_start[:, None] + KW)
    dc = np.clip(c[None, :] - c[:, None] + KW - 1, 0, 2 * KW - 2)
    onehot = np.zeros((128, GRID_W * GRID_W), np.float32)
    qq, kk = np.meshgrid(c, c, indexing="ij")
    onehot[dc[col_in], (qq * GRID_W + kk)[col_in]] = 1.0
    negmask = np.where(col_in, 0.0, NEG_INF).astype(np.float32).reshape(1, -1)
    return onehot, negmask


def _na_pair(block_type, a, b):
    if block_type == 0:
        return b - a + KH - 1 if b < KH else None
    if block_type == 1:
        return b - a + KH // 2 - 1 if a <= b < a + KH else None
    return b - a if b >= NA_KR - KH else None


def _rpb_expand(rpb):
    onehot, negmask = _na_consts()
    rows = HEADS * (2 * KH - 1)
    rpb_pad = jnp.pad(rpb.reshape(rows, 2 * KW - 1), ((0, 128 - rows), (0, 128 - (2 * KW - 1))))

    def body(r_ref, oh_ref, m_ref, t_ref):
        hi, mid, lo = _split3(r_ref[...])
        oh = oh_ref[...]
        t_ref[...] = _dot(hi, oh) + _dot(mid, oh) + _dot(lo, oh) + m_ref[...]

    table = pl.pallas_call(body, out_shape=SDS((128, GRID_W * GRID_W), F32), name="rpb_expand",
                           compiler_params=_params())(rpb_pad, jnp.asarray(onehot, BF16), jnp.asarray(negmask))
    return table[:rows].reshape(HEADS, 2 * KH - 1, GRID_W, GRID_W)


def _rpb_reduce(dslabs):
    onehot, _ = _na_consts()
    rows = HEADS * (2 * KH - 1)

    def body(x_ref, oht_ref, o_ref):
        hi, mid, lo = _split3(x_ref[...])
        oht = oht_ref[...]
        o_ref[...] = _dot(hi, oht) + _dot(mid, oht) + _dot(lo, oht)

    out = pl.pallas_call(body, out_shape=SDS((rows, 128), F32), name="rpb_reduce", compiler_params=_params())(
        dslabs.reshape(rows, GRID_W * GRID_W), jnp.asarray(onehot.T, BF16))
    return out.reshape(HEADS, 2 * KH - 1, 128)


def _bias_tiles(slab_ref, tile_ref):
    tile_ref[...] = jnp.full(tile_ref.shape, NEG_INF, F32)
    for t in range(NA_TYPES):
        for a in range(NA_RB):
            for b in range(NA_KR):
                dr = _na_pair(t, a, b)
                if dr is not None:
                    tile_ref[t, a * GRID_W:(a + 1) * GRID_W, b * GRID_W:(b + 1) * GRID_W] = slab_ref[dr]


def _bias_tiles_bwd(dtile_ref, dslab_ref):
    acc = {}
    for t in range(NA_TYPES):
        for a in range(NA_RB):
            for b in range(NA_KR):
                dr = _na_pair(t, a, b)
                if dr is not None:
                    part = dtile_ref[t, a * GRID_W:(a + 1) * GRID_W, b * GRID_W:(b + 1) * GRID_W]
                    acc[dr] = part if dr not in acc else acc[dr] + part
    for dr in range(2 * KH - 1):
        dslab_ref[dr] = acc[dr]


def _block_geometry(g):
    start = jnp.clip(g * NA_RB - KH // 2, 0, GRID_ROWS - NA_KR)
    block_type = jnp.where(g == 0, 0, jnp.where(g == NA_BLOCKS - 1, 2, 1))
    q0 = pl.multiple_of(N_META + g * NA_QB, 16)
    k0 = pl.multiple_of(N_META + start * GRID_W, 16)
    return block_type, q0, k0


def _na_probs(q, kk, km, bias):
    s = _dot_nt(q, kk) * ATT_SCALE + bias
    sm = _dot_nt(q, km) * ATT_SCALE
    m = jnp.maximum(jnp.max(s, axis=-1, keepdims=True), jnp.max(sm, axis=-1, keepdims=True))
    p = jnp.exp(s - m)
    pm = jnp.exp(sm - m)
    inv = 1.0 / (jnp.sum(p, axis=-1, keepdims=True) + jnp.sum(pm, axis=-1, keepdims=True))
    return p * inv, pm * inv


def _meta_probs(qm, km):
    s = _dot_nt(qm, km) * ATT_SCALE
    p = jnp.exp(s - jnp.max(s, axis=-1, keepdims=True))
    return p / jnp.sum(p, axis=-1, keepdims=True)


def _qkv_specs():
    return [pl.BlockSpec((None, SEQ, HEAD_DIM), lambda h, which=which: (h + which * HEADS, 0, 0)) for which in range(3)]


def _na_fwd(qkv, bias):
    def body(q_ref, k_ref, v_ref, slab_ref, o_ref, b_ref):
        _bias_tiles(slab_ref, b_ref)
        km = k_ref[0:N_META, :].astype(BF16)
        vm = v_ref[0:N_META, :].astype(BF16)
        pmm = _meta_probs(q_ref[0:N_META, :].astype(BF16), km)
        o_ref[0:N_META, :] = _dot(pmm.astype(BF16), vm)

        def block(g, carry):
            block_type, q0, k0 = _block_geometry(g)
            qb = q_ref[pl.ds(q0, NA_QB), :].astype(BF16)
            kk = k_ref[pl.ds(k0, NA_KB), :].astype(BF16)
            vv = v_ref[pl.ds(k0, NA_KB), :].astype(BF16)
            p, pm = _na_probs(qb, kk, km, b_ref[block_type])
            o_ref[pl.ds(q0, NA_QB), :] = _dot(p.astype(BF16), vv) + _dot(pm.astype(BF16), vm)
            return carry

        lax.fori_loop(0, NA_BLOCKS, block, 0, unroll=2)

    head = pl.BlockSpec((None, SEQ, HEAD_DIM), lambda h: (h, 0, 0))
    return pl.pallas_call(
        body, grid=(HEADS,), in_specs=_qkv_specs() + [pl.BlockSpec((None, 2 * KH - 1, GRID_W, GRID_W), lambda h: (h, 0, 0, 0))],
        out_specs=head, out_shape=SDS((HEADS, SEQ, HEAD_DIM), F32), name="na_fwd",
        scratch_shapes=[pltpu.VMEM((NA_TYPES, NA_QB, NA_KB), F32)],
        compiler_params=_params(("parallel",)))(qkv, qkv, qkv, bias)


def _na_bwd(qkv, bias, do):
    def body(q_ref, k_ref, v_ref, slab_ref, do_ref, dq_ref, dk_ref, dv_ref, dslab_ref, b_ref, db_ref):
        _bias_tiles(slab_ref, b_ref)
        km = k_ref[0:N_META, :].astype(BF16)
        vm = v_ref[0:N_META, :].astype(BF16)
        dk_ref[...] = jnp.zeros_like(dk_ref)
        dv_ref[...] = jnp.zeros_like(dv_ref)
        db_ref[...] = jnp.zeros_like(db_ref)

        qm = q_ref[0:N_META, :].astype(BF16)
        dom = do_ref[0:N_META, :].astype(BF16)
        pmm = _meta_probs(qm, km)
        dpm = _dot_nt(dom, vm)
        dsm = (pmm * (dpm - jnp.sum(pmm * dpm, axis=-1, keepdims=True)) * ATT_SCALE).astype(BF16)
        dq_ref[0:N_META, :] = _dot(dsm, km)
        dkm0 = _dot_tn(dsm, qm)
        dvm0 = _dot_tn(pmm.astype(BF16), dom)

        def block(g, carry):
            dkm, dvm = carry
            block_type, q0, k0 = _block_geometry(g)
            qb = q_ref[pl.ds(q0, NA_QB), :].astype(BF16)
            kk = k_ref[pl.ds(k0, NA_KB), :].astype(BF16)
            vv = v_ref[pl.ds(k0, NA_KB), :].astype(BF16)
            dob = do_ref[pl.ds(q0, NA_QB), :].astype(BF16)
            p, pm = _na_probs(qb, kk, km, b_ref[block_type])
            dp = _dot_nt(dob, vv)
            dpm_ = _dot_nt(dob, vm)
            delta = jnp.sum(p * dp, axis=-1, keepdims=True) + jnp.sum(pm * dpm_, axis=-1, keepdims=True)
            ds = p * (dp - delta)
            dsm_ = pm * (dpm_ - delta)
            db_ref[block_type] += ds
            dsb = (ds * ATT_SCALE).astype(BF16)
            dsmb = (dsm_ * ATT_SCALE).astype(BF16)
            dq_ref[pl.ds(q0, NA_QB), :] = _dot(dsb, kk) + _dot(dsmb, km)
            dk_ref[pl.ds(k0, NA_KB), :] += _dot_tn(dsb, qb)
            dv_ref[pl.ds(k0, NA_KB), :] += _dot_tn(p.astype(BF16), dob)
            return dkm + _dot_tn(dsmb, qb), dvm + _dot_tn(pm.astype(BF16), dob)

        dkm, dvm = lax.fori_loop(0, NA_BLOCKS // 2, lambda t, carry: block(2 * t + 1, block(2 * t, carry)), (dkm0, dvm0))
        dk_ref[0:N_META, :] = dkm
        dv_ref[0:N_META, :] = dvm
        _bias_tiles_bwd(db_ref, dslab_ref)

    head = pl.BlockSpec((None, SEQ, HEAD_DIM), lambda h: (h, 0, 0))
    bspec = pl.BlockSpec((None, 2 * KH - 1, GRID_W, GRID_W), lambda h: (h, 0, 0, 0))
    return pl.pallas_call(
        body, grid=(HEADS,), in_specs=_qkv_specs() + [bspec, head], out_specs=[head, head, head, bspec],
        out_shape=[SDS((HEADS, SEQ, HEAD_DIM), F32)] * 3 + [SDS((HEADS, 2 * KH - 1, GRID_W, GRID_W), F32)],
        scratch_shapes=[pltpu.VMEM((NA_TYPES, NA_QB, NA_KB), F32), pltpu.VMEM((NA_TYPES, NA_QB, NA_KB), F32)],
        name="na_bwd", compiler_params=_params(("parallel",)))(qkv, qkv, qkv, bias, do)


def _cmul(ar, ai, br, bi):
    return ar * br - ai * bi, ar * bi + ai * br


def _cpow(ar, ai, n):
    rr, ri = None, None
    br, bi = ar, ai
    while n:
        if n & 1:
            rr, ri = (br, bi) if rr is None else _cmul(rr, ri, br, bi)
        n >>= 1
        if n:
            br, bi = _cmul(br, bi, br, bi)
    return rr, ri


def _s5_prep(lr, li, logdt, bre, bim):
    def body(lr_ref, li_ref, dt_ref, br_ref, bi_ref, lbr_ref, lbi_ref, bbr_ref, bbi_ref):
        lr_, li_ = lr_ref[...], li_ref[...]
        dt = jnp.exp(dt_ref[...])
        mag = jnp.exp(lr_ * dt)
        lbr = mag * jnp.cos(li_ * dt)
        lbi = mag * jnp.sin(li_ * dt)
        lbr_ref[...] = lbr
        lbi_ref[...] = lbi
        den = lr_ * lr_ + li_ * li_
        xr = lbr - 1.0
        cr = (xr * lr_ + lbi * li_) / den
        ci = (lbi * lr_ - xr * li_) / den
        br, bi = br_ref[...], bi_ref[...]
        bbr_ref[...] = cr[:, None, :] * br - ci[:, None, :] * bi
        bbi_ref[...] = cr[:, None, :] * bi + ci[:, None, :] * br

    n = 2 * S5_GROUPS
    return pl.pallas_call(
        body, out_shape=[SDS((n, S5_STATE), F32)] * 2 + [SDS((n, S5_GROUP, S5_STATE), F32)] * 2,
        name="s5_prep", compiler_params=_params())(lr, li, logdt, bre, bim)


def _s5_prep_bwd(lr, li, logdt, bre, bim, dar, dai, dbbr, dbbi):
    def body(lr_ref, li_ref, dt_ref, br_ref, bi_ref, dar_ref, dai_ref, dbr_ref, dbi_ref,
             glr_ref, gli_ref, gdt_ref, gbr_ref, gbi_ref):
        lr_, li_ = lr_ref[...], li_ref[...]
        dt = jnp.exp(dt_ref[...])
        mag = jnp.exp(lr_ * dt)
        lbr = mag * jnp.cos(li_ * dt)
        lbi = mag * jnp.sin(li_ * dt)
        den = lr_ * lr_ + li_ * li_
        xr = lbr - 1.0
        cr = (xr * lr_ + lbi * li_) / den
        ci = (lbi * lr_ - xr * li_) / den
        br, bi = br_ref[...], bi_ref[...]
        dbr, dbi = dbr_ref[...], dbi_ref[...]
        gbr_ref[...] = cr[:, None, :] * dbr + ci[:, None, :] * dbi
        gbi_ref[...] = cr[:, None, :] * dbi - ci[:, None, :] * dbr
        gcr = jnp.sum(dbr * br + dbi * bi, axis=1)
        gci = jnp.sum(dbi * br - dbr * bi, axis=1)
        ilr, ili = lr_ / den, li_ / den
        tr, ti = _cmul(gcr, gci, ilr, ili)
        glbr = dar_ref[...] + tr
        glbi = dai_ref[...] + ti
        dr_, di_ = _cmul(tr, ti, cr, -ci)
        gwr, gwi = _cmul(glbr, glbi, lbr, -lbi)
        glr_ref[...] = gwr * dt - dr_
        gli_ref[...] = gwi * dt - di_
        gdt_ref[...] = jnp.sum(gwr * lr_ + gwi * li_, axis=-1, keepdims=True) * dt

    n = 2 * S5_GROUPS
    return pl.pallas_call(
        body, out_shape=[SDS((n, S5_STATE), F32)] * 2 + [SDS((n, 1), F32)] + [SDS((n, S5_GROUP, S5_STATE), F32)] * 2,
        name="s5_prep_bwd", compiler_params=_params())(lr, li, logdt, bre, bim, dar, dai, dbbr, dbbi)


def _scan_local(xr_ref, xi_ref, ar8, ai8, reverse):
    def step(i, carry):
        sr, si = carry
        idx = (SCAN_T - 1 - i) if reverse else i
        rows = pl.ds(pl.multiple_of(idx * SCAN_BLOCKS, SCAN_BLOCKS), SCAN_BLOCKS)
        nr = ar8 * sr - ai8 * si + xr_ref[rows, :]
        ni = ar8 * si + ai8 * sr + xi_ref[rows, :]
        xr_ref[rows, :] = nr
        xi_ref[rows, :] = ni
        return nr, ni

    z = jnp.zeros(ar8.shape, F32)
    return lax.fori_loop(0, SCAN_T, step, (z, z))


def _scan_carries(er, ei, atr, ati, reverse):
    row = lax.broadcasted_iota(jnp.int32, er.shape, 0)
    cr = jnp.zeros((1, er.shape[1]), F32)
    ci = cr
    outr = jnp.zeros(er.shape, F32)
    outi = outr
    order = range(SCAN_BLOCKS - 1, -1, -1) if reverse else range(SCAN_BLOCKS)
    for b in order:
        outr = jnp.where(row == b, cr, outr)
        outi = jnp.where(row == b, ci, outi)
        nr, ni = _cmul(atr, ati, cr, ci)
        cr, ci = nr + er[b:b + 1, :], ni + ei[b:b + 1, :]
    return outr, outi


def _scan_fixup(xr_ref, xi_ref, cr8, ci8, ar8, ai8, reverse, pair=None):
    tile = lambda idx: pl.ds(pl.multiple_of(idx * SCAN_BLOCKS, SCAN_BLOCKS), SCAN_BLOCKS)

    def fix(idx, pr, pi):
        fr, fi = _cmul(pr, pi, cr8, ci8)
        nr, ni = xr_ref[tile(idx), :] + fr, xi_ref[tile(idx), :] + fi
        xr_ref[tile(idx), :] = nr
        xi_ref[tile(idx), :] = ni
        return nr, ni

    if pair is None:
        def step(i, carry):
            pr, pi = carry
            fix((SCAN_T - 1 - i) if reverse else i, pr, pi)
            return _cmul(pr, pi, ar8, ai8)

        lax.fori_loop(0, SCAN_T, step, (ar8, ai8), unroll=2)
        return None

    sr_ref, si_ref = pair
    earlier = -1 if reverse else 1

    def step(i, carry):
        pr, pi, accr, acci = carry
        idx = (SCAN_T - 1 - i) if reverse else i
        nr, ni = fix(idx, pr, pi)
        qr, qi = _cmul(nr, ni, sr_ref[tile(idx + earlier), :], -si_ref[tile(idx + earlier), :])
        pr, pi = _cmul(pr, pi, ar8, ai8)
        return pr, pi, accr + qr, acci + qi

    z = jnp.zeros(ar8.shape, F32)
    pr, pi, accr, acci = lax.fori_loop(0, SCAN_T - 1, step, (ar8, ai8, z, z))
    edge, src, shift, empty = (0, SCAN_T - 1, 1, 0) if reverse else (SCAN_T - 1, 0, SCAN_BLOCKS - 1, SCAN_BLOCKS - 1)
    nr, ni = fix(edge, pr, pi)
    row = lax.broadcasted_iota(jnp.int32, ar8.shape, 0)
    spr = jnp.where(row == empty, 0.0, pltpu.roll(sr_ref[tile(src), :], shift, 0))
    spi = jnp.where(row == empty, 0.0, pltpu.roll(si_ref[tile(src), :], shift, 0))
    qr, qi = _cmul(nr, ni, spr, -spi)
    return jnp.sum(accr + qr, axis=0, keepdims=True), jnp.sum(acci + qi, axis=0, keepdims=True)


def _scan(xr_ref, xi_ref, ar, ai, reverse, pair=None):
    n = ar.shape[1]
    ar8 = jnp.broadcast_to(ar, (SCAN_BLOCKS, n))
    ai8 = jnp.broadcast_to(ai, (SCAN_BLOCKS, n))
    er, ei = _scan_local(xr_ref, xi_ref, ar8, ai8, reverse)
    atr, ati = _cpow(ar, ai, SCAN_T)
    cr8, ci8 = _scan_carries(er, ei, atr, ati, reverse)
    return _scan_fixup(xr_ref, xi_ref, cr8, ci8, ar8, ai8, reverse, pair)


def _s5_specs():
    chan = pl.BlockSpec((SEQ, CH_W), lambda c, d: (0, c))
    chan2 = pl.BlockSpec((None, SEQ, CH_W), lambda c, d: (d, 0, c))
    state = pl.BlockSpec((None, SEQ, ST_W), lambda c, d: (d, 0, c))
    bmat = pl.BlockSpec((None, None, CH_W, ST_W), lambda c, d: (d, c, 0, 0))
    cmat = pl.BlockSpec((None, None, ST_W, CH_W), lambda c, d: (d, c, 0, 0))
    avec = pl.BlockSpec((None, None, 1, ST_W), lambda c, d: (d, c, 0, 0))
    return chan, chan2, state, bmat, cmat, avec


def _scan_by_direction(xr_ref, xi_ref, ar, ai, d, adjoint, pair=None, da_out=None):
    for direction in range(2):
        @pl.when(d == direction)
        def _(direction=direction):
            res = _scan(xr_ref, xi_ref, ar, ai, adjoint != (direction == 1), pair)
            if pair is not None:
                da_out[0][...], da_out[1][...] = res


def _s5_scan_fwd(u, bre, bim, are, aim, cre, cim):
    def body(u_ref, bre_ref, bim_ref, are_ref, aim_ref, cre_ref, cim_ref, sr_ref, si_ref, y_ref):
        ub = u_ref[...].astype(BF16)
        sr_ref[...] = _dot(ub, bre_ref[...])
        si_ref[...] = _dot(ub, bim_ref[...])
        _scan_by_direction(sr_ref, si_ref, are_ref[...], aim_ref[...], pl.program_id(1), adjoint=False)
        y_ref[...] = _dot(sr_ref[...].astype(BF16), cre_ref[...]) - _dot(si_ref[...].astype(BF16), cim_ref[...])

    chan, chan2, state, bmat, cmat, avec = _s5_specs()
    return pl.pallas_call(
        body, grid=(S5_CHUNKS, 2), in_specs=[chan, bmat, bmat, avec, avec, cmat, cmat], out_specs=[state, state, chan2],
        out_shape=[SDS((2, SEQ, S5_GROUPS * S5_STATE), F32)] * 2 + [SDS((2, SEQ, S5_WIDTH), F32)],
        name="s5_scan_fwd", compiler_params=_params(("parallel", "parallel")))(u, bre, bim, are, aim, cre, cim)


def _diag_out(out_ref, full):
    for g in range(8):
        out_ref[g] = full[g * S5_GROUP:(g + 1) * S5_GROUP, g * S5_STATE:(g + 1) * S5_STATE]


def _s5_scan_bwd(dy, du_skip, u, sr, si, bre, bim, are, aim, cre, cim):
    def body(dy_ref, dus_ref, u_ref, sr_ref, si_ref, bre_ref, bim_ref, are_ref, aim_ref, cre_ref, cim_ref,
             du_ref, dbr_ref, dbi_ref, dcr_ref, dci_ref, dar_ref, dai_ref, gr_ref, gi_ref):
        d = pl.program_id(1)
        dyb = dy_ref[...].astype(BF16)
        gr_ref[...] = _dot_nt(dyb, cre_ref[...])
        gi_ref[...] = -_dot_nt(dyb, cim_ref[...])
        _diag_out(dcr_ref, _dot_tn(dyb, sr_ref[...].astype(BF16)))
        _diag_out(dci_ref, -_dot_tn(dyb, si_ref[...].astype(BF16)))
        _scan_by_direction(gr_ref, gi_ref, are_ref[...], -aim_ref[...], d, adjoint=True, pair=(sr_ref, si_ref),
                           da_out=(dar_ref, dai_ref))

        @pl.when(d == 0)
        def _():
            du_ref[...] = dus_ref[...]

        grb = gr_ref[...].astype(BF16)
        gib = gi_ref[...].astype(BF16)
        du_ref[...] += _dot_nt(grb, bre_ref[...]) + _dot_nt(gib, bim_ref[...])
        ub = u_ref[...].astype(BF16)
        _diag_out(dbr_ref, _dot_tn(ub, grb))
        _diag_out(dbi_ref, _dot_tn(ub, gib))

    chan, _, state, bmat, cmat, avec = _s5_specs()
    diag = pl.BlockSpec((None, None, 8, S5_GROUP, S5_STATE), lambda c, d: (d, c, 0, 0, 0))
    return pl.pallas_call(
        body, grid=(S5_CHUNKS, 2), in_specs=[chan, chan, chan, state, state, bmat, bmat, avec, avec, cmat, cmat],
        out_specs=[chan, diag, diag, diag, diag, avec, avec],
        out_shape=[SDS((SEQ, S5_WIDTH), F32)] + [SDS((2, S5_CHUNKS, 8, S5_GROUP, S5_STATE), F32)] * 4
                  + [SDS((2, S5_CHUNKS, 1, ST_W), F32)] * 2,
        scratch_shapes=[pltpu.VMEM((SEQ, ST_W), F32), pltpu.VMEM((SEQ, ST_W), F32)],
        name="s5_scan_bwd", compiler_params=_params(("parallel", "arbitrary")))(dy, du_skip, u, sr, si, bre, bim, are, aim, cre, cim)


_GELU_K = math.sqrt(2.0 / math.pi)
_GELU_C = 0.044715


def _gelu(x):
    t = jnp.tanh(_GELU_K * (x + _GELU_C * x * x * x))
    return 0.5 * x * (1.0 + t), t


def _s5_glu_fwd(u, y2, dskip, wglu, bglu):
    def body(u_ref, y0_ref, y1_ref, d_ref, w_ref, b_ref, o_ref, yp_ref):
        ypre = u_ref[...] * d_ref[...] + y0_ref[...] + y1_ref[...]
        yp_ref[...] = ypre
        y, _ = _gelu(ypre)
        z = _dot(y.astype(BF16), w_ref[...]) + b_ref[...]
        o_ref[...] = y * jax.nn.sigmoid(z)

    row = _row_spec(S5_WIDTH)
    vec = _fix_spec((1, S5_WIDTH))
    dir0 = pl.BlockSpec((None, ROW_TILE, S5_WIDTH), lambda i: (0, i, 0))
    dir1 = pl.BlockSpec((None, ROW_TILE, S5_WIDTH), lambda i: (1, i, 0))
    return pl.pallas_call(
        body, grid=(N_ROW_TILES,), in_specs=[row, dir0, dir1, vec, _fix_spec((S5_WIDTH, S5_WIDTH)), vec],
        out_specs=[row, row], out_shape=[SDS((SEQ, S5_WIDTH), F32)] * 2, name="s5_glu_fwd",
        compiler_params=_params(("parallel",)))(u, y2, y2, dskip, wglu, bglu)


def _s5_glu_bwd(do, ypre, u, dskip, wglu, bglu):
    def body(do_ref, yp_ref, u_ref, d_ref, w_ref, b_ref, dyp_ref, du_ref, dw_ref, db_ref, dd_ref):
        i = pl.program_id(0)
        ypre = yp_ref[...]
        y, t = _gelu(ypre)
        yb = y.astype(BF16)
        sg = jax.nn.sigmoid(_dot(yb, w_ref[...]) + b_ref[...])
        dov = do_ref[...]
        dz = dov * y * sg * (1.0 - sg)
        dzb = dz.astype(BF16)
        dy = dov * sg + _dot_nt(dzb, w_ref[...])
        dgelu = 0.5 * (1.0 + t) + 0.5 * ypre * (1.0 - t * t) * _GELU_K * (1.0 + 3.0 * _GELU_C * ypre * ypre)
        dyp = dy * dgelu
        dyp_ref[...] = dyp
        uv = u_ref[...]
        du_ref[...] = dyp * d_ref[...]

        @pl.when(i == 0)
        def _():
            dw_ref[...] = jnp.zeros_like(dw_ref)
            db_ref[...] = jnp.zeros_like(db_ref)
            dd_ref[...] = jnp.zeros_like(dd_ref)

        dw_ref[...] += _dot_tn(yb, dzb)
        db_ref[...] += jnp.sum(dz, axis=0, keepdims=True)
        dd_ref[...] += jnp.sum(dyp * uv, axis=0, keepdims=True)

    row = _row_spec(S5_WIDTH)
    vec = _fix_spec((1, S5_WIDTH))
    mat = _fix_spec((S5_WIDTH, S5_WIDTH))
    return pl.pallas_call(
        body, grid=(N_ROW_TILES,), in_specs=[row, row, row, vec, mat, vec], out_specs=[row, row, mat, vec, vec],
        out_shape=[SDS((SEQ, S5_WIDTH), F32)] * 2 + [SDS((S5_WIDTH, S5_WIDTH), F32), SDS((1, S5_WIDTH), F32), SDS((1, S5_WIDTH), F32)],
        name="s5_glu_bwd", compiler_params=_params(("arbitrary",)))(do, ypre, u, dskip, wglu, bglu)


def _heads_side_by_side(o_ref):
    return jnp.concatenate([o_ref[h] for h in range(HEADS)], axis=-1)


def _mix_out_fwd(ona, os5, g_na, g_s5, wout):
    def body(a_ref, s_ref, ga_ref, gs_ref, w_ref, o_ref):
        av, sv = _heads_side_by_side(a_ref), s_ref[...]
        ca = (av * _rstd(av) * ga_ref[...]).astype(BF16)
        cs = (sv * _rstd(sv) * gs_ref[...]).astype(BF16)
        o_ref[...] = _dot(ca, w_ref[0:NA_WIDTH, :]) + _dot(cs, w_ref[NA_WIDTH:, :])

    row = _row_spec(NA_WIDTH)
    vec = _fix_spec((1, NA_WIDTH))
    heads = pl.BlockSpec((HEADS, ROW_TILE, HEAD_DIM), lambda i: (0, i, 0))
    return pl.pallas_call(
        body, grid=(N_ROW_TILES,), in_specs=[heads, row, vec, vec, _fix_spec((D_MODEL, D_MODEL))],
        out_specs=_row_spec(D_MODEL), out_shape=SDS((SEQ, D_MODEL), F32), name="mix_out_fwd",
        compiler_params=_params(("parallel",)))(ona, os5, g_na, g_s5, wout)


def _mix_out_bwd(dmix, ona, os5, g_na, g_s5, wout):
    def body(dm_ref, a_ref, s_ref, ga_ref, gs_ref, w_ref, da_ref, ds_ref, dw_ref, dga_ref, dgs_ref):
        i = pl.program_id(0)
        dm = dm_ref[...]
        av, sv = _heads_side_by_side(a_ref), s_ref[...]
        ra, rs = _rstd(av), _rstd(sv)
        ga, gs = ga_ref[...], gs_ref[...]
        ca = (av * ra * ga).astype(BF16)
        cs = (sv * rs * gs).astype(BF16)
        dca = _dot_nt(dm, w_ref[0:NA_WIDTH, :])
        dcs = _dot_nt(dm, w_ref[NA_WIDTH:, :])
        da, dga = _rms_bwd(av, ra, ga, dca)
        ds, dgs = _rms_bwd(sv, rs, gs, dcs)
        for h in range(HEADS):
            da_ref[h] = da[:, h * HEAD_DIM:(h + 1) * HEAD_DIM]
        ds_ref[...] = ds

        @pl.when(i == 0)
        def _():
            dw_ref[...] = jnp.zeros_like(dw_ref)
            dga_ref[...] = jnp.zeros_like(dga_ref)
            dgs_ref[...] = jnp.zeros_like(dgs_ref)

        dw_ref[0:NA_WIDTH, :] += _dot_tn(ca, dm)
        dw_ref[NA_WIDTH:, :] += _dot_tn(cs, dm)
        dga_ref[...] += jnp.sum(dga, axis=0, keepdims=True)
        dgs_ref[...] += jnp.sum(dgs, axis=0, keepdims=True)

    row = _row_spec(NA_WIDTH)
    vec = _fix_spec((1, NA_WIDTH))
    mat = _fix_spec((D_MODEL, D_MODEL))
    heads = pl.BlockSpec((HEADS, ROW_TILE, HEAD_DIM), lambda i: (0, i, 0))
    return pl.pallas_call(
        body, grid=(N_ROW_TILES,), in_specs=[_row_spec(D_MODEL), heads, row, vec, vec, mat],
        out_specs=[heads, row, mat, vec, vec],
        out_shape=[SDS((HEADS, SEQ, HEAD_DIM), F32), SDS((SEQ, NA_WIDTH), F32), SDS((D_MODEL, D_MODEL), F32),
                   SDS((1, NA_WIDTH), F32), SDS((1, NA_WIDTH), F32)],
        name="mix_out_bwd", compiler_params=_params(("arbitrary",)))(dmix, ona, os5, g_na, g_s5, wout)


def _me():
    x, y, c = lax.axis_index("x"), lax.axis_index("y"), lax.axis_index("c")
    return x, y, c, 4 * x + 2 * y + c


def _peer(k):
    x, y, c, _ = _me()
    px = 1 - x if (k >> 2) & 1 else x
    py = 1 - y if (k >> 1) & 1 else y
    pc = 1 - c if k & 1 else c
    return (px, py, pc), 4 * px + 2 * py + pc


ALL_PEERS = (1, 2, 3, 4, 5, 6, 7)
CHIP_PEERS = (2, 4, 6)
SIBLING = 1


def _slot8(pos):
    return 4 * pos[0] + 2 * pos[1] + pos[2]


def _slot4(pos):
    return 2 * pos[0] + pos[1]


_HBM = pl.BlockSpec(memory_space=pltpu.HBM)
_SEM = pl.BlockSpec(memory_space=pltpu.SEMAPHORE)
_EFFECT = pltpu.SideEffectType.DATAFLOW_SIDE_EFFECTING


def _exchange_start(arrays, lands, gather, name, peers=ALL_PEERS, slot=_slot8, own=True):
    n = len(arrays)

    def body(*refs):
        ins, lnd = refs[:n], refs[n:2 * n]
        send_sems, recv_sems = refs[2 * n], refs[2 * n + 1]
        token = refs[-1]
        me = slot(_me()[:3])
        for i, k in enumerate(peers):
            peer, _ = _peer(k)
            for a in range(n):
                src = ins[a] if gather else ins[a].at[slot(peer)]
                s = a * len(peers) + i
                pltpu.make_async_remote_copy(src_ref=src, dst_ref=lnd[a].at[me], send_sem=send_sems.at[s],
                                             recv_sem=recv_sems.at[s], device_id=peer, device_id_type=MESH).start()
        if own:
            for a in range(n):
                pltpu.make_async_copy(ins[a] if gather else ins[a].at[me], lnd[a].at[me], recv_sems.at[n * len(peers) + a]).start()
        token[...] = jnp.zeros_like(token)

    sems = pltpu.SemaphoreType.DMA((n * (len(peers) + int(own)),))
    out = pl.pallas_call(
        body, name=name, in_specs=[_HBM] * (2 * n),
        out_shape=(sems, sems) + tuple(pltpu.HBM(a.shape, a.dtype) for a in list(arrays) + list(lands)) + (SDS((8, 128), F32),),
        out_specs=(_SEM, _SEM) + (_HBM,) * (2 * n) + (pl.BlockSpec(memory_space=pltpu.VMEM),),
        input_output_aliases={i: 2 + i for i in range(2 * n)},
        compiler_params=pltpu.CompilerParams(has_side_effects=_EFFECT),
    )(*[pltpu.with_memory_space_constraint(a, pltpu.HBM) for a in list(arrays) + list(lands)])
    return out[0], out[1], list(out[2:2 + n]), list(out[2 + n:2 + 2 * n]), out[-1]


def _exchange_wait(send_sems, recv_sems, arrays, lands, after, gather, name, peers=ALL_PEERS, slot=_slot8, own=True):
    n = len(arrays)

    def body(*refs):
        ins, lnd = refs[:n], refs[n:2 * n]
        send_sems, recv_sems = refs[2 * n], refs[2 * n + 1]
        if own:
            me = slot(_me()[:3])
            for a in range(n):
                pltpu.make_async_copy(ins[a] if gather else ins[a].at[me], lnd[a].at[me], recv_sems.at[n * len(peers) + a]).wait()
        for i, k in enumerate(peers):
            peer, _ = _peer(k)
            for a in range(n):
                src = ins[a] if gather else ins[a].at[slot(peer)]
                s = a * len(peers) + i
                cp = pltpu.make_async_remote_copy(src_ref=src, dst_ref=lnd[a].at[slot(peer)], send_sem=send_sems.at[s],
                                                  recv_sem=recv_sems.at[s], device_id=peer, device_id_type=MESH)
                cp.wait_send()
                cp.wait_recv()

        refs[-1][...] = jnp.zeros_like(refs[-1])

    after = list(after) if isinstance(after, (list, tuple)) else [after]
    out = pl.pallas_call(
        body, name=name, in_specs=[_HBM] * (2 * n) + [_SEM, _SEM] + [pl.BlockSpec(memory_space=pl.ANY)] * len(after),
        out_shape=tuple(pltpu.HBM(a.shape, a.dtype) for a in list(arrays) + list(lands)) + (SDS((8, 128), F32),),
        out_specs=(_HBM,) * (2 * n) + (pl.BlockSpec(memory_space=pltpu.VMEM),), input_output_aliases={i: i for i in range(2 * n)},
        compiler_params=pltpu.CompilerParams(has_side_effects=_EFFECT),
    )(*arrays, *lands, send_sems, recv_sems, *after)
    return list(out[n:2 * n]), out[-1]


def _forward_sibling(lands, name):
    n = len(lands)

    def body(*refs):
        outs = refs[n:2 * n]
        send_sems, recv_sems = refs[2 * n:]
        x, y, c, _ = _me()
        sends = []
        for i, k in enumerate(CHIP_PEERS):
            peer, _ = _peer(k)
            for a in range(n):
                rows = outs[a].at[_slot8(peer)]
                cp = pltpu.make_async_remote_copy(src_ref=rows, dst_ref=rows, send_sem=send_sems.at[a, i], recv_sem=recv_sems.at[a, i],
                                                  device_id=(x, y, 1 - c), device_id_type=MESH)
                cp.start()
                sends.append(cp)
        for i, k in enumerate(CHIP_PEERS):
            (px, py, pc), _ = _peer(k)
            for a in range(n):
                rows = outs[a].at[_slot8((px, py, 1 - pc))]
                pltpu.make_async_remote_copy(src_ref=rows, dst_ref=rows, send_sem=send_sems.at[a, i], recv_sem=recv_sems.at[a, i],
                                             device_id=(x, y, 1 - c), device_id_type=MESH).wait_recv()
        for cp in sends:
            cp.wait_send()

    return pl.pallas_call(
        body, in_specs=[_HBM] * n, out_specs=[_HBM] * n, out_shape=[SDS(a.shape, a.dtype) for a in lands],
        input_output_aliases={i: i for i in range(n)},
        scratch_shapes=[pltpu.SemaphoreType.DMA((n, len(CHIP_PEERS))), pltpu.SemaphoreType.DMA((n, len(CHIP_PEERS)))],
        name=name)(*lands)


def _swap_sibling(arrays, name, after=()):
    n, n_after = len(arrays), len(after)
    chips = N_DEV // 2

    def body(*refs):
        ins, outs = refs[:n], refs[n + n_after:2 * n + n_after]
        send_sems, recv_sems = refs[2 * n + n_after:]
        x, y, c, _ = _me()
        sends = []
        for q in range(chips):
            for a in range(n):
                cp = pltpu.make_async_remote_copy(src_ref=ins[a].at[q, 1 - c], dst_ref=outs[a].at[q], send_sem=send_sems.at[a, q],
                                                  recv_sem=recv_sems.at[a, q], device_id=(x, y, 1 - c), device_id_type=MESH)
                cp.start()
                sends.append(cp)
        for cp in sends:
            cp.wait_recv()
        for cp in sends:
            cp.wait_send()

    return pl.pallas_call(
        body, in_specs=[_HBM] * n + [pl.BlockSpec(memory_space=pl.ANY)] * n_after, out_specs=[_HBM] * n,
        out_shape=[SDS((chips,) + a.shape[2:], a.dtype) for a in arrays],
        scratch_shapes=[pltpu.SemaphoreType.DMA((n, chips)), pltpu.SemaphoreType.DMA((n, chips))], name=name)(*arrays, *after)


def _sum_pairs(mine, theirs, name):
    n = len(mine)
    chips = mine[0].shape[0]
    c = lax.axis_index("c")

    def body(c_ref, *refs):
        for a in range(n):
            refs[2 * n + a][...] = (refs[a][...].astype(F32) + refs[n + a][...].astype(F32)).astype(refs[2 * n + a].dtype)

    def pair(a):
        return pl.BlockSpec((None, None) + a.shape[2:], lambda q, c_ref: (q, c_ref[0], 0, 0))

    def single(a):
        return pl.BlockSpec((None,) + a.shape[2:], lambda q, c_ref: (q, 0, 0))

    return pl.pallas_call(
        body, grid_spec=pltpu.PrefetchScalarGridSpec(
            num_scalar_prefetch=1, grid=(chips,), in_specs=[pair(a) for a in mine] + [single(a) for a in mine],
            out_specs=[single(a) for a in mine]),
        out_shape=[SDS((chips,) + a.shape[2:], a.dtype) for a in mine], name=name,
        compiler_params=_params(("parallel",)))(c.reshape(1).astype(jnp.int32), *mine, *theirs)


def _adamw_math(w, g, m, v):
    m = ADAM_B1 * m + (1.0 - ADAM_B1) * g
    v = ADAM_B2 * v + (1.0 - ADAM_B2) * (g * g)
    m_hat = m / (1.0 - ADAM_B1 ** ADAM_STEP)
    v_hat = v / (1.0 - ADAM_B2 ** ADAM_STEP)
    delta = -ADAM_LR * (m_hat / (jnp.sqrt(v_hat) + ADAM_EPS) + ADAM_WD * w)
    return delta, m, v


def _adamw(w, m, v, pieces, name):
    rows, cols = w.shape[-2:]
    lead = w.ndim - 2
    tile = rows
    for cand in (256, 176, 128, 64, 16):
        if rows > cand and rows % cand == 0:
            tile = cand
            break

    def body(w_ref, m_ref, v_ref, p_ref, g_ref, d_ref, mo_ref, vo_ref):
        g = _sum_pieces(p_ref)
        g_ref[...] = g
        d_ref[...], mo_ref[...], vo_ref[...] = _adamw_math(w_ref[...], g, m_ref[...], v_ref[...])

    blk = pl.BlockSpec((None,) * lead + (tile, cols), lambda i: (0,) * lead + (i, 0))
    return pl.pallas_call(
        body, grid=(rows // tile,), in_specs=[blk, blk, blk, pl.BlockSpec((pieces.shape[0], tile, cols), lambda i: (0, i, 0))],
        out_specs=[blk] * 4, out_shape=[SDS(w.shape, F32)] * 4, name=name,
        compiler_params=_params(("parallel",)))(w, m, v, pieces)


def _sum_pieces(p_ref):
    g = p_ref[0].astype(F32)
    for p in range(1, p_ref.shape[0]):
        g = g + p_ref[p].astype(F32)
    return g


def _adamw_s5_mat(w, m, v, g, name):
    _, ndir, groups, b, c = w.shape
    per_dir = groups // 8

    def body(w_ref, m_ref, v_ref, g_ref, d_ref, mo_ref, vo_ref):
        d_ref[...], mo_ref[...], vo_ref[...] = _adamw_math(w_ref[...], g_ref[...], m_ref[...], v_ref[...])

    blk = pl.BlockSpec((None, None, 8, b, c), lambda i: (0, i // per_dir, i % per_dir, 0, 0))
    return pl.pallas_call(
        body, grid=(ndir * per_dir,), in_specs=[blk] * 4, out_specs=[blk] * 3, out_shape=[SDS(w.shape, F32)] * 3, name=name,
        compiler_params=_params(("parallel",)))(w, m, v, g)


VEC_ROWS = ['ffn1_pre_g', 'ffn1_post_g', 'mix_pre_g', 'mix_post_g', 'ffn2_pre_g', 'ffn2_post_g', 'final_g',
            ('na_out_g', 's5_out_g'), ('s5_d', 's5_b_glu')]
VEC_NAMES = [n for row in VEC_ROWS for n in ((row,) if isinstance(row, str) else row)]
VEC_PACK_ROWS = 16
LOSS_ROW = len(VEC_ROWS)


def _pack_vectors(grads, loss8):
    def body(*refs):
        o_ref = refs[-1]
        o_ref[...] = jnp.zeros_like(o_ref)
        o_ref[LOSS_ROW:LOSS_ROW + 1, 0:128] = refs[-2][0:1, :]
        k = 0
        for i, row in enumerate(VEC_ROWS):
            if isinstance(row, str):
                o_ref[i:i + 1, :] = refs[k][...]
                k += 1
            else:
                o_ref[i:i + 1, 0:NA_WIDTH] = refs[k][...]
                o_ref[i:i + 1, NA_WIDTH:] = refs[k + 1][...]
                k += 2

    return pl.pallas_call(body, out_shape=SDS((VEC_PACK_ROWS, D_MODEL), F32), name="pack_vectors",
                          compiler_params=_params())(*[grads[n] for n in VEC_NAMES], loss8)


def _sum8(pieces, name):
    def body(p_ref, o_ref):
        o_ref[...] = _sum_pieces(p_ref)

    return pl.pallas_call(body, out_shape=SDS(pieces.shape[1:], F32), name=name, compiler_params=_params())(pieces)


def _adamw_small(packed8, vec_wmv, others):
    n_vec, n_oth = len(VEC_NAMES), len(others)

    def body(*refs):
        p_ref = refs[0]
        ins = refs[1:1 + 3 * n_vec + 4 * n_oth]
        outs = refs[1 + 3 * n_vec + 4 * n_oth:]
        gsum = _sum_pieces(p_ref)
        outs[-1][...] = gsum[LOSS_ROW:LOSS_ROW + 1, 0:128]
        k = 0
        for i, row in enumerate(VEC_ROWS):
            parts = [(row, gsum[i:i + 1, :])] if isinstance(row, str) else \
                [(row[0], gsum[i:i + 1, 0:NA_WIDTH]), (row[1], gsum[i:i + 1, NA_WIDTH:])]
            for _, g in parts:
                w_ref, m_ref, v_ref = ins[3 * k:3 * k + 3]
                outs[4 * k][...] = g
                outs[4 * k + 1][...], outs[4 * k + 2][...], outs[4 * k + 3][...] = _adamw_math(w_ref[...], g, m_ref[...], v_ref[...])
                k += 1
        for j in range(n_oth):
            w_ref, m_ref, v_ref, g_ref = ins[3 * n_vec + 4 * j:3 * n_vec + 4 * j + 4]
            g = _sum_pieces(g_ref)
            g = g[tuple(slice(0, s) for s in w_ref.shape[1:])].reshape(w_ref.shape)
            o = outs[4 * (n_vec + j):4 * (n_vec + j) + 4]
            o[0][...] = g
            o[1][...], o[2][...], o[3][...] = _adamw_math(w_ref[...], g, m_ref[...], v_ref[...])

    args, out_shape = [packed8], []
    for w, m, v in vec_wmv:
        args += [w, m, v]
        out_shape += [SDS(w.shape, F32)] * 4
    for w, m, v, g in others:
        args += [w, m, v, g]
        out_shape += [SDS(w.shape, F32)] * 4
    out_shape += [SDS((1, 128), F32)]
    return pl.pallas_call(body, out_shape=out_shape, name="adamw_small", compiler_params=_params())(*args)


def _perm_rows(x):
    return x.reshape(SCAN_BLOCKS, SCAN_T, x.shape[-1]).transpose(1, 0, 2).reshape(SEQ, x.shape[-1])


def _unperm_rows(x):
    return x.reshape(SCAN_T, SCAN_BLOCKS, x.shape[-1]).transpose(1, 0, 2).reshape(SEQ, x.shape[-1])


def _block_diag(x):
    eye = np.eye(8, dtype=bool)[None, None, :, None, :, None]
    full = jnp.where(eye, x[:, :, :, :, None, :], 0.0)
    return full.reshape(2, S5_CHUNKS, 8 * x.shape[3], 8 * x.shape[4])


STORED_SWAPPED = {"ffn1_w_gate": (1, 2), "ffn1_w_up": (1, 2), "ffn2_w_gate": (1, 2), "ffn2_w_up": (1, 2),
                  "s5_b_re": (3, 4), "s5_b_im": (3, 4)}


def _stored(name, x):
    return jnp.swapaxes(x, *STORED_SWAPPED[name]) if name in STORED_SWAPPED else x


def _dep(x, token):
    return x if token is None else x + token


def _local_step(x, target, get_w, small, emit):
    bias = _rpb_expand(small["na_rpb"][0])
    lr = small["s5_lam_re"].reshape(64, S5_STATE)
    li = small["s5_lam_im"].reshape(64, S5_STATE)
    logdt = small["s5_log_dt"].reshape(64, 1)
    b_t = [_stored(n, small[n]).reshape(64, S5_GROUP, S5_STATE) for n in ("s5_b_re", "s5_b_im")]
    lbr, lbi, bbr, bbi = _s5_prep(lr, li, logdt, b_t[0], b_t[1])
    are = lbr.reshape(2, S5_CHUNKS, 1, ST_W)
    aim = lbi.reshape(2, S5_CHUNKS, 1, ST_W)
    bre = _block_diag(bbr.reshape(2, S5_CHUNKS, 8, S5_GROUP, S5_STATE)).astype(BF16)
    bim = _block_diag(bbi.reshape(2, S5_CHUNKS, 8, S5_GROUP, S5_STATE)).astype(BF16)
    c_t = [small[n].reshape(2, S5_CHUNKS, 8, S5_GROUP, S5_STATE).transpose(0, 1, 2, 4, 3) for n in ("s5_c_re", "s5_c_im")]
    cre = _block_diag(c_t[0]).astype(BF16)
    cim = _block_diag(c_t[1]).astype(BF16)
    tgt = jnp.concatenate([jnp.zeros((N_META, D_MODEL), F32), target], axis=0)

    h0, a1 = _embed_prenorm(get_w("meta", None)["meta_tokens"], x, small["ffn1_pre_g"])
    wts = dict(get_w("ffn1", [bias, are, aim, bre, bim, cre, cim, tgt, a1]))
    gate1, up1, f1 = _ffn_fwd(a1, wts["ffn1_w_gate"], wts["ffn1_w_up"], wts["ffn1_w_down"], "ffn1_fwd",
                              after=wts.get("tokens", ()))
    h1, a2 = _post_pre(f1, h0, small["ffn1_post_g"], small["mix_pre_g"], 0.5, "post_pre1")
    wts.update(get_w("w_in", a2))
    qkv = _proj_heads(a2, wts["w_in"])
    u = _proj_u(a2, wts["w_in"])
    ona = _na_fwd(qkv, bias)
    u_p = _perm_rows(u)
    sr, si, y2 = _s5_scan_fwd(u_p, bre, bim, are, aim, cre, cim)
    wts.update(get_w("mix", y2))
    os5_p, ypre_p = _s5_glu_fwd(u_p, y2, small["s5_d"], wts["s5_w_glu"], small["s5_b_glu"])
    os5 = _unperm_rows(os5_p)

    mix = _mix_out_fwd(ona, os5, small["na_out_g"], small["s5_out_g"], wts["w_out"])
    h2, a3 = _post_pre(mix, h1, small["mix_post_g"], small["ffn2_pre_g"], 1.0, "post_pre2")
    wts.update(get_w("ffn2", a3))
    gate2, up2, f2 = _ffn_fwd(a3, wts["ffn2_w_gate"], wts["ffn2_w_up"], wts["ffn2_w_down"], "ffn2_fwd")
    loss8, dh3, df2, g_final, g_ffn2_post = _final_loss(f2, h2, small["ffn2_post_g"], small["final_g"], tgt)

    da3, dwg2, dwu2, dwd2 = _ffn_bwd(df2, a3, gate2, up2, wts["ffn2_w_gate"], wts["ffn2_w_up"], wts["ffn2_w_down"], "ffn2_bwd")
    tok = emit("ffn2", {"ffn2_w_gate": dwg2, "ffn2_w_up": dwu2, "ffn2_w_down": dwd2})
    dh2, dmix, g_ffn2_pre, g_mix_post = _bwd_pre_post(da3, h2, _dep(small["ffn2_pre_g"], tok), dh3, mix, small["mix_post_g"], 1.0,
                                                      "bwd_pre_post2")
    dona, dos5, dwout, g_na_out, g_s5_out = _mix_out_bwd(dmix, ona, os5, small["na_out_g"], small["s5_out_g"], wts["w_out"])

    dypre_p, du_skip_p, dwglu, g_b_glu, g_s5_d = _s5_glu_bwd(_perm_rows(dos5), ypre_p, u_p, small["s5_d"], wts["s5_w_glu"],
                                                             small["s5_b_glu"])
    tok = emit("mix", {"s5_w_glu": dwglu.reshape(N_DEV, S5_WIDTH // N_DEV, S5_WIDTH).astype(BF16),
                       "w_out": dwout.reshape(N_DEV, D_MODEL // N_DEV, D_MODEL).astype(BF16)})
    du_p, dbr, dbi, dcr, dci, dar, dai = _s5_scan_bwd(dypre_p, du_skip_p, u_p, sr, si, bre, bim, _dep(are, tok), aim, cre, cim)
    du = _unperm_rows(du_p)
    per_group = (2 * S5_GROUPS, S5_GROUP, S5_STATE)
    g_lr, g_li, g_dt, g_br, g_bi = _s5_prep_bwd(lr, li, logdt, b_t[0], b_t[1], dar.reshape(64, S5_STATE),
                                                dai.reshape(64, S5_STATE), dbr.reshape(per_group), dbi.reshape(per_group))
    g_c = [dcr.reshape(per_group), dci.reshape(per_group)]

    dq, dk, dv, dbias = _na_bwd(qkv, bias, dona)
    g_rpb = _rpb_reduce(dbias)
    dense = jnp.stack([g.reshape(2 * S5_GROUPS, S5_STATE * S5_GROUP) for g in (g_br, g_bi, *g_c)])
    tok = emit("small", {"dense": dense, "na_rpb": g_rpb,
                         "s5_lam_re": g_lr.reshape(2, S5_GROUPS, S5_STATE), "s5_lam_im": g_li.reshape(2, S5_GROUPS, S5_STATE),
                         "s5_log_dt": g_dt.reshape(2, S5_GROUPS)})
    da2, dwin = _proj_bwd(dq, dk, dv, du, a2, wts["w_in"])
    tok2 = emit("w_in", {"w_in": dwin})
    tok = tok if tok2 is None else tok + tok2
    dh1, df1, g_mix_pre, g_ffn1_post = _bwd_pre_post(da2, h1, _dep(small["mix_pre_g"], tok), dh2, f1, small["ffn1_post_g"], 0.5,
                                                     "bwd_pre_post1")
    da1, dwg1, dwu1, dwd1 = _ffn_bwd(df1, a1, gate1, up1, wts["ffn1_w_gate"], wts["ffn1_w_up"], wts["ffn1_w_down"], "ffn1_bwd")
    grad_x, grad_meta, g_ffn1_pre = _bwd_embed(da1, h0, small["ffn1_pre_g"], dh1)
    vec_g = {
        "ffn1_pre_g": g_ffn1_pre, "ffn1_post_g": g_ffn1_post, "mix_pre_g": g_mix_pre, "s5_d": g_s5_d, "s5_b_glu": g_b_glu,
        "na_out_g": g_na_out, "s5_out_g": g_s5_out, "mix_post_g": g_mix_post,
        "ffn2_pre_g": g_ffn2_pre, "ffn2_post_g": g_ffn2_post, "final_g": g_final,
    }
    emit("vec", {"packed": _pack_vectors(vec_g, loss8), "meta_tokens": grad_meta})
    emit("ffn1", {"ffn1_w_gate": dwg1, "ffn1_w_up": dwu1, "ffn1_w_down": dwd1})
    return grad_x


WEIGHT_NAMES = ['meta_tokens', 'ffn1_pre_g', 'ffn1_post_g', 'ffn1_w_gate', 'ffn1_w_up', 'ffn1_w_down', 'mix_pre_g', 'w_in',
                'na_rpb', 's5_lam_re', 's5_lam_im', 's5_log_dt', 's5_b_re', 's5_b_im', 's5_c_re', 's5_c_im', 's5_d',
                's5_w_glu', 's5_b_glu', 'na_out_g', 's5_out_g', 'w_out', 'mix_post_g', 'ffn2_pre_g', 'ffn2_post_g',
                'ffn2_w_gate', 'ffn2_w_up', 'ffn2_w_down', 'final_g']
BIG_NAMES = ['ffn1_w_gate', 'ffn1_w_up', 'ffn1_w_down', 'w_in', 's5_w_glu', 'w_out', 'ffn2_w_gate', 'ffn2_w_up', 'ffn2_w_down']
SMALL_NAMES = [n for n in WEIGHT_NAMES if n not in BIG_NAMES and n != 'meta_tokens']
WHOLE_NAMES = ['na_rpb', 's5_lam_re', 's5_lam_im', 's5_log_dt']
LEAD_NAMES = ['s5_b_re', 's5_b_im', 's5_c_re', 's5_c_im']


def kernel(x, meta_tokens, ffn1_pre_g, ffn1_post_g, ffn1_w_gate, ffn1_w_up, ffn1_w_down, mix_pre_g, w_in, na_rpb, s5_lam_re, s5_lam_im, s5_log_dt, s5_b_re, s5_b_im, s5_c_re, s5_c_im, s5_d, s5_w_glu, s5_b_glu, na_out_g, s5_out_g, w_out, mix_post_g, ffn2_pre_g, ffn2_post_g, ffn2_w_gate, ffn2_w_up, ffn2_w_down, final_g, loss_target, m_meta_tokens, m_ffn1_pre_g, m_ffn1_post_g, m_ffn1_w_gate, m_ffn1_w_up, m_ffn1_w_down, m_mix_pre_g, m_w_in, m_na_rpb, m_s5_lam_re, m_s5_lam_im, m_s5_log_dt, m_s5_b_re, m_s5_b_im, m_s5_c_re, m_s5_c_im, m_s5_d, m_s5_w_glu, m_s5_b_glu, m_na_out_g, m_s5_out_g, m_w_out, m_mix_post_g, m_ffn2_pre_g, m_ffn2_post_g, m_ffn2_w_gate, m_ffn2_w_up, m_ffn2_w_down, m_final_g, v_meta_tokens, v_ffn1_pre_g, v_ffn1_post_g, v_ffn1_w_gate, v_ffn1_w_up, v_ffn1_w_down, v_mix_pre_g, v_w_in, v_na_rpb, v_s5_lam_re, v_s5_lam_im, v_s5_log_dt, v_s5_b_re, v_s5_b_im, v_s5_c_re, v_s5_c_im, v_s5_d, v_s5_w_glu, v_s5_b_glu, v_na_out_g, v_s5_out_g, v_w_out, v_mix_post_g, v_ffn2_pre_g, v_ffn2_post_g, v_ffn2_w_gate, v_ffn2_w_up, v_ffn2_w_down, v_final_g):
    args = dict(locals())
    w = {n: args[n] for n in WEIGHT_NAMES}
    m = {n: args["m_" + n] for n in WEIGHT_NAMES}
    v = {n: args["v_" + n] for n in WEIGHT_NAMES}

    small = {n: w[n] for n in SMALL_NAMES}

    pending = {}

    def start(group, names, arrays, gather, peers=ALL_PEERS, slot=_slot8):
        n_slots = N_DEV if slot is _slot8 else N_DEV // 2
        lands = [lax.empty((n_slots,) + a.shape if gather else a.shape, a.dtype) for a in arrays]
        send_sems, recv_sems, arrays, lands, token = _exchange_start(arrays, lands, gather, "start_" + group, peers, slot)
        pending[group] = (names, send_sems, recv_sems, arrays, lands, gather, peers, slot)
        return token

    def finish(group, after):
        names, send_sems, recv_sems, arrays, lands, gather, peers, slot = pending.pop(group)
        lands, token = _exchange_wait(send_sems, recv_sems, arrays, lands, after, gather, "wait_" + group, peers, slot)
        return dict(zip(names, lands)), token

    first = ["ffn1_w_gate", "ffn1_w_up", "ffn1_w_down"]
    def shard(n, token=None):
        return _dep(_stored(n, w[n])[0], None if token is None else token[0, 0]).astype(BF16)

    ffn_names = ("ffn1_w_gate", "ffn1_w_up", "ffn1_w_down", "ffn2_w_gate", "ffn2_w_up", "ffn2_w_down")
    later_groups = (("w_in", ["w_in"]), ("mix", ["s5_w_glu", "w_out"]), ("ffn2", ["ffn2_w_gate", "ffn2_w_up", "ffn2_w_down"]))
    token0 = start("meta", ["meta_tokens"], [w["meta_tokens"]], True)
    token1 = start("ffn1", first, [shard(n, token0) for n in first], True, (SIBLING,) + CHIP_PEERS)
    meta_full = finish("meta", [token1])[0]["meta_tokens"].transpose(1, 0, 2).reshape(N_META, D_MODEL)
    later_shards = {n: shard(n, token1) for _, names in later_groups for n in names}
    for n in ("na_rpb", "s5_lam_re"):
        small[n] = _dep(small[n], token1[0, 0])

    def get_w(group, after):
        if group == "meta":
            return {"meta_tokens": meta_full}
        if group == "ffn1":
            after = list(after) + list(later_shards.values())
        got, token = finish(group, after)
        if group == "ffn1":
            got = dict(zip(got, _forward_sibling(list(got.values()), "forward_ffn1")))
            got["tokens"] = [start(g, names + ["order"], [later_shards[n] for n in names] + [token], True) for g, names in later_groups]
        if group == "mix":
            got = {"s5_w_glu": got["s5_w_glu"].reshape(S5_WIDTH, S5_WIDTH), "w_out": got["w_out"].reshape(D_MODEL, D_MODEL)}
        return {n: (a.reshape(D_FF, D_MODEL) if n in ffn_names else a) for n, a in got.items()}

    tokens = {}

    def emit(group, grads):
        grads = {n: (g.reshape(N_DEV, FF_SHARD, D_MODEL) if n in ffn_names else g) for n, g in grads.items()}
        if group == "ffn1":
            mine = [g.reshape((N_DEV // 2, 2) + g.shape[1:]) for g in grads.values()]
            theirs = _swap_sibling(mine, "swap_g_ffn1", after=[tokens["vec"]])
            sums = _sum_pairs(mine, theirs, "pair_sum_g_ffn1")
            tokens[group] = start("g_ffn1", list(grads), sums, False, CHIP_PEERS, _slot4)
        else:
            tokens[group] = start("g_" + group, list(grads), list(grads.values()), group in ("small", "vec"))
        return tokens[group][0, 0]

    grad_x = _local_step(x[0], loss_target[0], get_w, small, emit)
    res = {}

    def update_shard(n, pieces):
        outs = _adamw(_stored(n, w[n]), _stored(n, m[n]), _stored(n, v[n]), pieces, "adamw_" + n)
        res[n] = [_stored(n, o) for o in outs]

    late = [grad_x, tokens["ffn1"]]
    for group in ("g_ffn2", "g_mix", "g_w_in"):
        for n, pieces in finish(group, late)[0].items():
            update_shard(n, pieces)
    g8 = finish("g_small", late)[0]
    dense = _sum8(g8["dense"], "sum_dense")
    for i, n in enumerate(LEAD_NAMES):
        g = dense[i].reshape(_stored(n, w[n]).shape)
        upd = _adamw_s5_mat(_stored(n, w[n]), _stored(n, m[n]), _stored(n, v[n]), g, "adamw_" + n)
        res[n] = [_stored(n, o) for o in [g] + list(upd)]

    done = [res[n][1] for n in ("ffn2_w_gate", "ffn2_w_up", "ffn2_w_down", "w_in", "w_out", "s5_w_glu") + tuple(LEAD_NAMES)]
    got = finish("g_vec", done)[0]
    packed8, gmeta8 = got["packed"], got["meta_tokens"]
    for n, pieces in finish("g_ffn1", packed8)[0].items():
        update_shard(n, pieces)
    _, _, _, me = _me()
    update_shard("meta_tokens", lax.dynamic_slice_in_dim(gmeta8, me * (D_MODEL // N_DEV), D_MODEL // N_DEV, axis=2))

    outs = _adamw_small(packed8, [(w[n], m[n], v[n]) for n in VEC_NAMES], [(w[n], m[n], v[n], g8[n]) for n in WHOLE_NAMES])
    for i, n in enumerate(VEC_NAMES + WHOLE_NAMES):
        res[n] = list(outs[4 * i:4 * i + 4])

    out = [outs[-1][0, 0], grad_x[None]]
    for kind in range(4):
        out += [res[n][kind] for n in WEIGHT_NAMES]
    return tuple(out)
```

```python
import math

import numpy as np
import jax
import jax.numpy as jnp
from jax import lax
from jax.experimental import pallas as pl
from jax.experimental.pallas import tpu as pltpu

F32 = jnp.float32
BF16 = jnp.bfloat16
SDS = jax.ShapeDtypeStruct

D_MODEL = 1024
N_TOK = 2048
N_META = 16
SEQ = N_TOK + N_META
ROW_TILE = 688
N_ROW_TILES = SEQ // ROW_TILE
N_DEV = 8
D_FF = 2816
FF_SHARD = D_FF // N_DEV
FF_TILE = 256
IN_SHARD = 256
NA_WIDTH = 512
S5_WIDTH = 512
HEADS = 8
HEAD_DIM = 64
GRID_W = 64
GRID_ROWS = N_TOK // GRID_W
KH = 8
KW = 16
NA_RB = 4
NA_KR = KH + NA_RB - 1
NA_BLOCKS = GRID_ROWS // NA_RB
NA_QB = NA_RB * GRID_W
NA_KB = NA_KR * GRID_W
NA_TYPES = 3
S5_GROUPS = 32
S5_GROUP = 16
S5_STATE = 64
S5_CHUNKS = 4
CH_W = S5_WIDTH // S5_CHUNKS
ST_W = S5_GROUPS * S5_STATE // S5_CHUNKS
SCAN_BLOCKS = 8
SCAN_T = SEQ // SCAN_BLOCKS
RMS_EPS = 1e-6
NEG_INF = -1e30
ATT_SCALE = HEAD_DIM ** -0.5
ADAM_LR, ADAM_B1, ADAM_B2, ADAM_EPS, ADAM_WD, ADAM_STEP = 0.001, 0.9, 0.999, 1e-08, 0.01, 10
VMEM_LIMIT = 56 * 1024 * 1024
MESH = pl.DeviceIdType.MESH


def _params(sem=None):
    return pltpu.CompilerParams(dimension_semantics=sem, vmem_limit_bytes=VMEM_LIMIT)


def _dot(a, b):
    return jnp.dot(a, b, preferred_element_type=F32)


def _dot_nt(a, b):
    return lax.dot_general(a, b, (((1,), (1,)), ((), ())), preferred_element_type=F32)


def _dot_tn(a, b):
    return lax.dot_general(a, b, (((0,), (0,)), ((), ())), preferred_element_type=F32)


def _rstd(x):
    return lax.rsqrt(jnp.mean(x * x, axis=-1, keepdims=True) + RMS_EPS)


def _rms_bwd(x, r, g, dy):
    dyg = dy * g
    xr = x * r
    dx = r * (dyg - xr * jnp.mean(dyg * xr, axis=-1, keepdims=True))
    return dx, dy * xr


def _rows(i, size=ROW_TILE):
    return pl.ds(pl.multiple_of(i * size, 16), size)


def _row_spec(width):
    return pl.BlockSpec((ROW_TILE, width), lambda i: (i, 0))


def _fix_spec(shape):
    return pl.BlockSpec(shape, lambda i: (0,) * len(shape))


def _split3(x):
    hi = x.astype(BF16)
    r1 = x - hi.astype(F32)
    mid = r1.astype(BF16)
    lo = (r1 - mid.astype(F32)).astype(BF16)
    return hi, mid, lo


def _embed_prenorm(meta, x, g):
    def body(m_ref, x_ref, g_ref, h_ref, a_ref):
        h_ref[0:N_META, :] = m_ref[...]
        h_ref[N_META:, :] = x_ref[...]
        for i in range(N_ROW_TILES):
            rows = slice(i * ROW_TILE, (i + 1) * ROW_TILE)
            hv = h_ref[rows, :]
            a_ref[rows, :] = (hv * _rstd(hv) * g_ref[...]).astype(BF16)

    return pl.pallas_call(
        body, out_shape=[SDS((SEQ, D_MODEL), F32), SDS((SEQ, D_MODEL), BF16)], name="embed_prenorm",
        compiler_params=_params())(meta, x, g)


def _post_pre(f, hres, g_post, g_next, scale, name):
    def body(f_ref, h_ref, gp_ref, gn_ref, ho_ref, a_ref):
        fv = f_ref[...]
        h = h_ref[...] + scale * (fv * _rstd(fv) * gp_ref[...])
        ho_ref[...] = h
        a_ref[...] = (h * _rstd(h) * gn_ref[...]).astype(BF16)

    return pl.pallas_call(
        body, grid=(N_ROW_TILES,),
        in_specs=[_row_spec(D_MODEL), _row_spec(D_MODEL), _fix_spec((1, D_MODEL)), _fix_spec((1, D_MODEL))],
        out_specs=[_row_spec(D_MODEL), _row_spec(D_MODEL)],
        out_shape=[SDS((SEQ, D_MODEL), F32), SDS((SEQ, D_MODEL), BF16)], name=name,
        compiler_params=_params(("parallel",)))(f, hres, g_post, g_next)


def _final_loss(f2, h2, g_post, g_final, target):
    def body(f_ref, h_ref, gp_ref, gf_ref, t_ref, loss_ref, dh_ref, df_ref, dgf_ref, dgp_ref):
        i = pl.program_id(0)
        fv = f_ref[...]
        r1 = _rstd(fv)
        gp = gp_ref[...]
        h3 = h_ref[...] + 0.5 * (fv * r1 * gp)
        r2 = _rstd(h3)
        gf = gf_ref[...]
        y = h3 * r2 * gf
        row = lax.broadcasted_iota(jnp.int32, (ROW_TILE, 1), 0) + i * ROW_TILE
        err = jnp.where(row >= N_META, y - t_ref[...], 0.0)
        part = 0.5 * jnp.sum(jnp.mean(err * err, axis=-1, keepdims=True))
        dy = err * (1.0 / D_MODEL)
        dh3, dgf = _rms_bwd(h3, r2, gf, dy)
        dh_ref[...] = dh3
        df, dgp = _rms_bwd(fv, r1, gp, 0.5 * dh3)
        df_ref[...] = df.astype(BF16)

        @pl.when(i == 0)
        def _():
            loss_ref[...] = jnp.zeros_like(loss_ref)
            dgf_ref[...] = jnp.zeros_like(dgf_ref)
            dgp_ref[...] = jnp.zeros_like(dgp_ref)

        loss_ref[...] += part
        dgf_ref[...] += jnp.sum(dgf, axis=0, keepdims=True)
        dgp_ref[...] += jnp.sum(dgp, axis=0, keepdims=True)

    gain = _fix_spec((1, D_MODEL))
    return pl.pallas_call(
        body, grid=(N_ROW_TILES,),
        in_specs=[_row_spec(D_MODEL), _row_spec(D_MODEL), gain, gain, _row_spec(D_MODEL)],
        out_specs=[_fix_spec((8, 128)), _row_spec(D_MODEL), _row_spec(D_MODEL), gain, gain],
        out_shape=[SDS((8, 128), F32), SDS((SEQ, D_MODEL), F32), SDS((SEQ, D_MODEL), BF16),
                   SDS((1, D_MODEL), F32), SDS((1, D_MODEL), F32)],
        name="final_loss", compiler_params=_params(("arbitrary",)))(f2, h2, g_post, g_final, target)


def _bwd_pre_post(da, h, g_pre, dh_res, fprev, g_post, scale, name):
    def body(da_ref, h_ref, gpre_ref, dhr_ref, f_ref, gpost_ref, dh_ref, df_ref, dgpre_ref, dgpost_ref):
        i = pl.program_id(0)
        hv = h_ref[...]
        dxa, dgpre = _rms_bwd(hv, _rstd(hv), gpre_ref[...], da_ref[...])
        dh = dhr_ref[...] + dxa
        dh_ref[...] = dh
        fv = f_ref[...]
        df, dgpost = _rms_bwd(fv, _rstd(fv), gpost_ref[...], scale * dh)
        df_ref[...] = df.astype(BF16)

        @pl.when(i == 0)
        def _():
            dgpre_ref[...] = jnp.zeros_like(dgpre_ref)
            dgpost_ref[...] = jnp.zeros_like(dgpost_ref)

        dgpre_ref[...] += jnp.sum(dgpre, axis=0, keepdims=True)
        dgpost_ref[...] += jnp.sum(dgpost, axis=0, keepdims=True)

    gain = _fix_spec((1, D_MODEL))
    row = _row_spec(D_MODEL)
    return pl.pallas_call(
        body, grid=(N_ROW_TILES,), in_specs=[row, row, gain, row, row, gain],
        out_specs=[row, row, gain, gain],
        out_shape=[SDS((SEQ, D_MODEL), F32), SDS((SEQ, D_MODEL), BF16), SDS((1, D_MODEL), F32), SDS((1, D_MODEL), F32)],
        name=name, compiler_params=_params(("arbitrary",)))(da, h, g_pre, dh_res, fprev, g_post)


def _bwd_embed(da, h, g_pre, dh_res):
    def body(da_ref, h_ref, gpre_ref, dhr_ref, gx_ref, gm_ref, dgpre_ref):
        total = jnp.zeros((1, D_MODEL), F32)
        for i in range(N_ROW_TILES):
            rows = slice(i * ROW_TILE, (i + 1) * ROW_TILE)
            hv = h_ref[rows, :]
            dxa, dgpre = _rms_bwd(hv, _rstd(hv), gpre_ref[...], da_ref[rows, :])
            dh = dhr_ref[rows, :] + dxa
            total = total + jnp.sum(dgpre, axis=0, keepdims=True)
            if i == 0:
                gm_ref[...] = dh[0:N_META, :]
                gx_ref[0:ROW_TILE - N_META, :] = dh[N_META:, :]
            else:
                gx_ref[i * ROW_TILE - N_META:(i + 1) * ROW_TILE - N_META, :] = dh
        dgpre_ref[...] = total

    return pl.pallas_call(
        body, out_shape=[SDS((N_TOK, D_MODEL), F32), SDS((N_META, D_MODEL), F32), SDS((1, D_MODEL), F32)],
        name="bwd_embed", compiler_params=_params())(da, h, g_pre, dh_res)


def _ffn_fwd(a, wg, wu, wd, name, after=()):
    def body(a_ref, wg_ref, wu_ref, wd_ref, *rest):
        gate_ref, up_ref, f_ref = rest[len(after):]
        j = pl.program_id(0)

        def tile(i, carry):
            rows = _rows(i)
            at = a_ref[rows, :]
            gate = _dot_nt(at, wg_ref[...])
            up = _dot_nt(at, wu_ref[...])
            gate_ref[rows, :] = gate.astype(BF16)
            up_ref[rows, :] = up.astype(BF16)
            act = (gate * jax.nn.sigmoid(gate) * up).astype(BF16)
            contrib = _dot(act, wd_ref[...])

            @pl.when(j == 0)
            def _():
                f_ref[rows, :] = contrib

            @pl.when(j != 0)
            def _():
                f_ref[rows, :] += contrib

            return carry

        lax.fori_loop(0, N_ROW_TILES, tile, 0)

    wtile = pl.BlockSpec((FF_TILE, D_MODEL), lambda j: (j, 0))
    hid = pl.BlockSpec((SEQ, FF_TILE), lambda j: (0, j))
    full = pl.BlockSpec((SEQ, D_MODEL), lambda j: (0, 0))
    return pl.pallas_call(
        body, grid=(D_FF // FF_TILE,), in_specs=[full, wtile, wtile, wtile] + [pl.BlockSpec(memory_space=pl.ANY)] * len(after),
        out_specs=[hid, hid, full],
        out_shape=[SDS((SEQ, D_FF), BF16), SDS((SEQ, D_FF), BF16), SDS((SEQ, D_MODEL), F32)],
        name=name, compiler_params=_params(("arbitrary",)))(a, wg, wu, wd, *after)


def _ffn_bwd(df, a, gate, up, wg, wu, wd, name):
    def body(df_ref, a_ref, gate_ref, up_ref, wg_ref, wu_ref, wd_ref, da_ref, dwg_ref, dwu_ref, dwd_ref,
             acc_g, acc_u, acc_d):
        j = pl.program_id(0)

        def tile(i, carry):
            rows = _rows(i)
            dft = df_ref[rows, :]
            at = a_ref[rows, :]
            gate = gate_ref[rows, :].astype(F32)
            up = up_ref[rows, :].astype(F32)
            dact = _dot_nt(dft, wd_ref[...])
            sig = jax.nn.sigmoid(gate)
            silu = gate * sig
            dgate = (dact * up * (sig * (1.0 + gate * (1.0 - sig)))).astype(BF16)
            dup = (dact * silu).astype(BF16)
            act = (silu * up).astype(BF16)
            dwd = _dot_tn(act, dft)
            dwg = _dot_tn(dgate, at)
            dwu = _dot_tn(dup, at)
            dat = _dot(dgate, wg_ref[...]) + _dot(dup, wu_ref[...])

            @pl.when(i == 0)
            def _():
                acc_d[...] = dwd
                acc_g[...] = dwg
                acc_u[...] = dwu

            @pl.when(i != 0)
            def _():
                acc_d[...] += dwd
                acc_g[...] += dwg
                acc_u[...] += dwu

            @pl.when(j == 0)
            def _():
                da_ref[rows, :] = dat

            @pl.when(j != 0)
            def _():
                da_ref[rows, :] += dat

            return carry

        lax.fori_loop(0, N_ROW_TILES, tile, 0)
        dwg_ref[...] = acc_g[...].astype(BF16)
        dwu_ref[...] = acc_u[...].astype(BF16)
        dwd_ref[...] = acc_d[...].astype(BF16)

    wtile = pl.BlockSpec((FF_TILE, D_MODEL), lambda j: (j, 0))
    hid = pl.BlockSpec((SEQ, FF_TILE), lambda j: (0, j))
    full = pl.BlockSpec((SEQ, D_MODEL), lambda j: (0, 0))
    return pl.pallas_call(
        body, grid=(D_FF // FF_TILE,), in_specs=[full, full, hid, hid, wtile, wtile, wtile],
        out_specs=[full, wtile, wtile, wtile],
        out_shape=[SDS((SEQ, D_MODEL), F32)] + [SDS((D_FF, D_MODEL), BF16)] * 3,
        scratch_shapes=[pltpu.VMEM((FF_TILE, D_MODEL), F32)] * 3,
        name=name, compiler_params=_params(("arbitrary",)))(df, a, gate, up, wg, wu, wd)


HEADS_PER_BLOCK = IN_SHARD // HEAD_DIM
QKV_BLOCKS = 3 * NA_WIDTH // IN_SHARD


def _proj_heads(a, w):
    def body(a_ref, w_ref, o_ref):
        def tile(i, carry):
            rows = _rows(i)
            res = _dot(a_ref[rows, :], w_ref[...])
            for sub in range(HEADS_PER_BLOCK):
                o_ref[sub, rows, :] = res[:, sub * HEAD_DIM:(sub + 1) * HEAD_DIM]
            return carry

        lax.fori_loop(0, N_ROW_TILES, tile, 0)

    return pl.pallas_call(
        body, grid=(QKV_BLOCKS,),
        in_specs=[pl.BlockSpec((SEQ, D_MODEL), lambda j: (0, 0)), pl.BlockSpec((None, D_MODEL, IN_SHARD), lambda j: (j, 0, 0))],
        out_specs=pl.BlockSpec((HEADS_PER_BLOCK, SEQ, HEAD_DIM), lambda j: (j, 0, 0)),
        out_shape=SDS((3 * HEADS, SEQ, HEAD_DIM), F32), name="proj_heads",
        compiler_params=_params(("parallel",)))(a, w)


def _proj_u(a, w):
    def body(a_ref, w_ref, o_ref):
        def tile(i, carry):
            rows = _rows(i)
            o_ref[rows, :] = _dot(a_ref[rows, :], w_ref[...])
            return carry

        lax.fori_loop(0, N_ROW_TILES, tile, 0)

    return pl.pallas_call(
        body, grid=(N_DEV - QKV_BLOCKS,),
        in_specs=[pl.BlockSpec((SEQ, D_MODEL), lambda j: (0, 0)),
                  pl.BlockSpec((None, D_MODEL, IN_SHARD), lambda j: (j + QKV_BLOCKS, 0, 0))],
        out_specs=pl.BlockSpec((SEQ, IN_SHARD), lambda j: (0, j)),
        out_shape=SDS((SEQ, S5_WIDTH), F32), name="proj_u",
        compiler_params=_params(("parallel",)))(a, w)


def _proj_bwd(dq, dk, dv, du, a, w):
    def body(dq_ref, dk_ref, dv_ref, du_ref, a_ref, w_ref, da_ref, dw_ref, acc, dp_ref):
        j = pl.program_id(0)

        for which, src in enumerate((dq_ref, dk_ref, dv_ref)):
            @pl.when((j >= 2 * which) & (j < 2 * which + 2))
            def _(src=src):
                dp_ref[...] = jnp.concatenate([src[sub] for sub in range(HEADS_PER_BLOCK)], axis=-1).astype(BF16)

        @pl.when(j >= QKV_BLOCKS)
        def _():
            dp_ref[...] = du_ref[...].astype(BF16)

        def tile(i, carry):
            rows = _rows(i)
            dpt = dp_ref[rows, :]
            dw = _dot_tn(a_ref[rows, :], dpt)
            dat = _dot_nt(dpt, w_ref[...])

            @pl.when(i == 0)
            def _():
                acc[...] = dw

            @pl.when(i != 0)
            def _():
                acc[...] += dw

            @pl.when(j == 0)
            def _():
                da_ref[rows, :] = dat

            @pl.when(j != 0)
            def _():
                da_ref[rows, :] += dat

            return carry

        lax.fori_loop(0, N_ROW_TILES, tile, 0)
        dw_ref[...] = acc[...].astype(BF16)

    full = pl.BlockSpec((SEQ, D_MODEL), lambda j: (0, 0))
    wspec = pl.BlockSpec((None, D_MODEL, IN_SHARD), lambda j: (j, 0, 0))

    def heads(which):
        return pl.BlockSpec((HEADS_PER_BLOCK, SEQ, HEAD_DIM), lambda j: (jnp.clip(j - 2 * which, 0, 1), 0, 0))

    return pl.pallas_call(
        body, grid=(N_DEV,),
        in_specs=[heads(0), heads(1), heads(2),
                  pl.BlockSpec((SEQ, IN_SHARD), lambda j: (0, jnp.clip(j - QKV_BLOCKS, 0, 1))), full, wspec],
        out_specs=[full, wspec],
        out_shape=[SDS((SEQ, D_MODEL), F32), SDS((N_DEV, D_MODEL, IN_SHARD), BF16)],
        scratch_shapes=[pltpu.VMEM((D_MODEL, IN_SHARD), F32), pltpu.VMEM((SEQ, IN_SHARD), BF16)],
        name="proj_bwd", compiler_params=_params(("arbitrary",)))(dq, dk, dv, du, a, w)


def _na_consts():
    c = np.arange(GRID_W)
    col_start = np.clip(c - KW // 2, 0, GRID_W - KW)
    col_in = (c[None, :] >= col_start[:, None]) & (c[None, :] < col_start[:, None] + KW)
    dc = np.clip(c[None, :] - c[:, None] + KW - 1, 0, 2 * KW - 2)
    onehot = np.zeros((128, GRID_W * GRID_W), np.float32)
    qq, kk = np.meshgrid(c, c, indexing="ij")
    onehot[dc[col_in], (qq * GRID_W + kk)[col_in]] = 1.0
    negmask = np.where(col_in, 0.0, NEG_INF).astype(np.float32).reshape(1, -1)
    return onehot, negmask


def _na_pair(block_type, a, b):
    if block_type == 0:
        return b - a + KH - 1 if b < KH else None
    if block_type == 1:
        return b - a + KH // 2 - 1 if a <= b < a + KH else None
    return b - a if b >= NA_KR - KH else None


def _rpb_expand(rpb):
    onehot, negmask = _na_consts()
    rows = HEADS * (2 * KH - 1)
    rpb_pad = jnp.pad(rpb.reshape(rows, 2 * KW - 1), ((0, 128 - rows), (0, 128 - (2 * KW - 1))))

    def body(r_ref, oh_ref, m_ref, t_ref):
        hi, mid, lo = _split3(r_ref[...])
        oh = oh_ref[...]
        t_ref[...] = _dot(hi, oh) + _dot(mid, oh) + _dot(lo, oh) + m_ref[...]

    table = pl.pallas_call(body, out_shape=SDS((128, GRID_W * GRID_W), F32), name="rpb_expand",
                           compiler_params=_params())(rpb_pad, jnp.asarray(onehot, BF16), jnp.asarray(negmask))
    return table[:rows].reshape(HEADS, 2 * KH - 1, GRID_W, GRID_W)


def _rpb_reduce(dslabs):
    onehot, _ = _na_consts()
    rows = HEADS * (2 * KH - 1)

    def body(x_ref, oht_ref, o_ref):
        hi, mid, lo = _split3(x_ref[...])
        oht = oht_ref[...]
        o_ref[...] = _dot(hi, oht) + _dot(mid, oht) + _dot(lo, oht)

    out = pl.pallas_call(body, out_shape=SDS((rows, 128), F32), name="rpb_reduce", compiler_params=_params())(
        dslabs.reshape(rows, GRID_W * GRID_W), jnp.asarray(onehot.T, BF16))
    return out.reshape(HEADS, 2 * KH - 1, 128)


def _bias_tiles(slab_ref, tile_ref):
    tile_ref[...] = jnp.full(tile_ref.shape, NEG_INF, F32)
    for t in range(NA_TYPES):
        for a in range(NA_RB):
            for b in range(NA_KR):
                dr = _na_pair(t, a, b)
                if dr is not None:
                    tile_ref[t, a * GRID_W:(a + 1) * GRID_W, b * GRID_W:(b + 1) * GRID_W] = slab_ref[dr]


def _bias_tiles_bwd(dtile_ref, dslab_ref):
    acc = {}
    for t in range(NA_TYPES):
        for a in range(NA_RB):
            for b in range(NA_KR):
                dr = _na_pair(t, a, b)
                if dr is not None:
                    part = dtile_ref[t, a * GRID_W:(a + 1) * GRID_W, b * GRID_W:(b + 1) * GRID_W]
                    acc[dr] = part if dr not in acc else acc[dr] + part
    for dr in range(2 * KH - 1):
        dslab_ref[dr] = acc[dr]


def _block_geometry(g):
    start = jnp.clip(g * NA_RB - KH // 2, 0, GRID_ROWS - NA_KR)
    block_type = jnp.where(g == 0, 0, jnp.where(g == NA_BLOCKS - 1, 2, 1))
    q0 = pl.multiple_of(N_META + g * NA_QB, 16)
    k0 = pl.multiple_of(N_META + start * GRID_W, 16)
    return block_type, q0, k0


def _scaled_q(q):
    return (q * ATT_SCALE).astype(BF16)


def _na_probs(qs, kk, km, bias):
    s = _dot_nt(qs, kk) + bias
    sm = _dot_nt(qs, km)
    m = jnp.maximum(jnp.max(s, axis=-1, keepdims=True), jnp.max(sm, axis=-1, keepdims=True))
    p = jnp.exp(s - m)
    pm = jnp.exp(sm - m)
    inv = 1.0 / (jnp.sum(p, axis=-1, keepdims=True) + jnp.sum(pm, axis=-1, keepdims=True))
    return p * inv, pm * inv


def _meta_probs(qm, km):
    s = _dot_nt(qm, km) * ATT_SCALE
    p = jnp.exp(s - jnp.max(s, axis=-1, keepdims=True))
    return p / jnp.sum(p, axis=-1, keepdims=True)


def _qkv_specs():
    return [pl.BlockSpec((None, SEQ, HEAD_DIM), lambda h, which=which: (h + which * HEADS, 0, 0)) for which in range(3)]


def _na_fwd(qkv, bias):
    def body(q_ref, k_ref, v_ref, slab_ref, o_ref, b_ref):
        _bias_tiles(slab_ref, b_ref)
        km = k_ref[0:N_META, :].astype(BF16)
        vm = v_ref[0:N_META, :].astype(BF16)
        pmm = _meta_probs(q_ref[0:N_META, :].astype(BF16), km)
        o_ref[0:N_META, :] = _dot(pmm.astype(BF16), vm)

        def block(g, carry):
            block_type, q0, k0 = _block_geometry(g)
            qs = _scaled_q(q_ref[pl.ds(q0, NA_QB), :])
            kk = k_ref[pl.ds(k0, NA_KB), :].astype(BF16)
            vv = v_ref[pl.ds(k0, NA_KB), :].astype(BF16)
            p, pm = _na_probs(qs, kk, km, b_ref[block_type])
            o_ref[pl.ds(q0, NA_QB), :] = _dot(p.astype(BF16), vv) + _dot(pm.astype(BF16), vm)
            return carry

        lax.fori_loop(0, NA_BLOCKS, block, 0, unroll=2)

    head = pl.BlockSpec((None, SEQ, HEAD_DIM), lambda h: (h, 0, 0))
    return pl.pallas_call(
        body, grid=(HEADS,), in_specs=_qkv_specs() + [pl.BlockSpec((None, 2 * KH - 1, GRID_W, GRID_W), lambda h: (h, 0, 0, 0))],
        out_specs=head, out_shape=SDS((HEADS, SEQ, HEAD_DIM), F32), name="na_fwd",
        scratch_shapes=[pltpu.VMEM((NA_TYPES, NA_QB, NA_KB), F32)],
        compiler_params=_params(("parallel",)))(qkv, qkv, qkv, bias)


def _na_bwd(qkv, bias, do):
    def body(q_ref, k_ref, v_ref, slab_ref, do_ref, dq_ref, dk_ref, dv_ref, dslab_ref, b_ref, db_ref):
        _bias_tiles(slab_ref, b_ref)
        km = k_ref[0:N_META, :].astype(BF16)
        vm = v_ref[0:N_META, :].astype(BF16)
        dk_ref[...] = jnp.zeros_like(dk_ref)
        dv_ref[...] = jnp.zeros_like(dv_ref)
        db_ref[...] = jnp.zeros_like(db_ref)

        qm = q_ref[0:N_META, :].astype(BF16)
        dom = do_ref[0:N_META, :].astype(BF16)
        pmm = _meta_probs(qm, km)
        dpm = _dot_nt(dom, vm)
        dsm = (pmm * (dpm - jnp.sum(pmm * dpm, axis=-1, keepdims=True)) * ATT_SCALE).astype(BF16)
        dq_ref[0:N_META, :] = _dot(dsm, km)
        dkm0 = _dot_tn(dsm, qm)
        dvm0 = _dot_tn(pmm.astype(BF16), dom)

        def block(g, carry):
            dkm, dvm = carry
            block_type, q0, k0 = _block_geometry(g)
            qs = _scaled_q(q_ref[pl.ds(q0, NA_QB), :])
            kk = k_ref[pl.ds(k0, NA_KB), :].astype(BF16)
            vv = v_ref[pl.ds(k0, NA_KB), :].astype(BF16)
            dob = do_ref[pl.ds(q0, NA_QB), :].astype(BF16)
            p, pm = _na_probs(qs, kk, km, b_ref[block_type])
            dp = _dot_nt(dob, vv)
            dpm_ = _dot_nt(dob, vm)
            delta = jnp.sum(p * dp, axis=-1, keepdims=True) + jnp.sum(pm * dpm_, axis=-1, keepdims=True)
            ds = p * (dp - delta)
            dsm_ = pm * (dpm_ - delta)
            db_ref[block_type] += ds
            dsb = ds.astype(BF16)
            dsmb = dsm_.astype(BF16)
            dq_ref[pl.ds(q0, NA_QB), :] = (_dot(dsb, kk) + _dot(dsmb, km)) * ATT_SCALE
            dk_ref[pl.ds(k0, NA_KB), :] += _dot_tn(dsb, qs)
            dv_ref[pl.ds(k0, NA_KB), :] += _dot_tn(p.astype(BF16), dob)
            return dkm + _dot_tn(dsmb, qs), dvm + _dot_tn(pm.astype(BF16), dob)

        dkm, dvm = lax.fori_loop(0, NA_BLOCKS // 2, lambda t, carry: block(2 * t + 1, block(2 * t, carry)), (dkm0, dvm0))
        dk_ref[0:N_META, :] = dkm
        dv_ref[0:N_META, :] = dvm
        _bias_tiles_bwd(db_ref, dslab_ref)

    head = pl.BlockSpec((None, SEQ, HEAD_DIM), lambda h: (h, 0, 0))
    bspec = pl.BlockSpec((None, 2 * KH - 1, GRID_W, GRID_W), lambda h: (h, 0, 0, 0))
    return pl.pallas_call(
        body, grid=(HEADS,), in_specs=_qkv_specs() + [bspec, head], out_specs=[head, head, head, bspec],
        out_shape=[SDS((HEADS, SEQ, HEAD_DIM), F32)] * 3 + [SDS((HEADS, 2 * KH - 1, GRID_W, GRID_W), F32)],
        scratch_shapes=[pltpu.VMEM((NA_TYPES, NA_QB, NA_KB), F32), pltpu.VMEM((NA_TYPES, NA_QB, NA_KB), F32)],
        name="na_bwd", compiler_params=_params(("parallel",)))(qkv, qkv, qkv, bias, do)


def _cmul(ar, ai, br, bi):
    return ar * br - ai * bi, ar * bi + ai * br


def _cpow(ar, ai, n):
    rr, ri = None, None
    br, bi = ar, ai
    while n:
        if n & 1:
            rr, ri = (br, bi) if rr is None else _cmul(rr, ri, br, bi)
        n >>= 1
        if n:
            br, bi = _cmul(br, bi, br, bi)
    return rr, ri


def _s5_prep(lr, li, logdt, bre, bim):
    def body(lr_ref, li_ref, dt_ref, br_ref, bi_ref, lbr_ref, lbi_ref, bbr_ref, bbi_ref):
        lr_, li_ = lr_ref[...], li_ref[...]
        dt = jnp.exp(dt_ref[...])
        mag = jnp.exp(lr_ * dt)
        lbr = mag * jnp.cos(li_ * dt)
        lbi = mag * jnp.sin(li_ * dt)
        lbr_ref[...] = lbr
        lbi_ref[...] = lbi
        den = lr_ * lr_ + li_ * li_
        xr = lbr - 1.0
        cr = (xr * lr_ + lbi * li_) / den
        ci = (lbi * lr_ - xr * li_) / den
        br, bi = br_ref[...], bi_ref[...]
        bbr_ref[...] = cr[:, None, :] * br - ci[:, None, :] * bi
        bbi_ref[...] = cr[:, None, :] * bi + ci[:, None, :] * br

    n = 2 * S5_GROUPS
    return pl.pallas_call(
        body, out_shape=[SDS((n, S5_STATE), F32)] * 2 + [SDS((n, S5_GROUP, S5_STATE), F32)] * 2,
        name="s5_prep", compiler_params=_params())(lr, li, logdt, bre, bim)


def _s5_prep_bwd(lr, li, logdt, bre, bim, dar, dai, dbbr, dbbi):
    def body(lr_ref, li_ref, dt_ref, br_ref, bi_ref, dar_ref, dai_ref, dbr_ref, dbi_ref,
             glr_ref, gli_ref, gdt_ref, gbr_ref, gbi_ref):
        lr_, li_ = lr_ref[...], li_ref[...]
        dt = jnp.exp(dt_ref[...])
        mag = jnp.exp(lr_ * dt)
        lbr = mag * jnp.cos(li_ * dt)
        lbi = mag * jnp.sin(li_ * dt)
        den = lr_ * lr_ + li_ * li_
        xr = lbr - 1.0
        cr = (xr * lr_ + lbi * li_) / den
        ci = (lbi * lr_ - xr * li_) / den
        br, bi = br_ref[...], bi_ref[...]
        dbr, dbi = dbr_ref[...], dbi_ref[...]
        gbr_ref[...] = cr[:, None, :] * dbr + ci[:, None, :] * dbi
        gbi_ref[...] = cr[:, None, :] * dbi - ci[:, None, :] * dbr
        gcr = jnp.sum(dbr * br + dbi * bi, axis=1)
        gci = jnp.sum(dbi * br - dbr * bi, axis=1)
        ilr, ili = lr_ / den, li_ / den
        tr, ti = _cmul(gcr, gci, ilr, ili)
        glbr = dar_ref[...] + tr
        glbi = dai_ref[...] + ti
        dr_, di_ = _cmul(tr, ti, cr, -ci)
        gwr, gwi = _cmul(glbr, glbi, lbr, -lbi)
        glr_ref[...] = gwr * dt - dr_
        gli_ref[...] = gwi * dt - di_
        gdt_ref[...] = jnp.sum(gwr * lr_ + gwi * li_, axis=-1, keepdims=True) * dt

    n = 2 * S5_GROUPS
    return pl.pallas_call(
        body, out_shape=[SDS((n, S5_STATE), F32)] * 2 + [SDS((n, 1), F32)] + [SDS((n, S5_GROUP, S5_STATE), F32)] * 2,
        name="s5_prep_bwd", compiler_params=_params())(lr, li, logdt, bre, bim, dar, dai, dbbr, dbbi)


def _scan_local(xr_ref, xi_ref, ar8, ai8, reverse):
    def step(i, carry):
        sr, si = carry
        idx = (SCAN_T - 1 - i) if reverse else i
        rows = pl.ds(pl.multiple_of(idx * SCAN_BLOCKS, SCAN_BLOCKS), SCAN_BLOCKS)
        nr = ar8 * sr - ai8 * si + xr_ref[rows, :]
        ni = ar8 * si + ai8 * sr + xi_ref[rows, :]
        xr_ref[rows, :] = nr
        xi_ref[rows, :] = ni
        return nr, ni

    z = jnp.zeros(ar8.shape, F32)
    return lax.fori_loop(0, SCAN_T, step, (z, z))


def _scan_carries(er, ei, atr, ati, reverse):
    row = lax.broadcasted_iota(jnp.int32, er.shape, 0)
    cr = jnp.zeros((1, er.shape[1]), F32)
    ci = cr
    outr = jnp.zeros(er.shape, F32)
    outi = outr
    order = range(SCAN_BLOCKS - 1, -1, -1) if reverse else range(SCAN_BLOCKS)
    for b in order:
        outr = jnp.where(row == b, cr, outr)
        outi = jnp.where(row == b, ci, outi)
        nr, ni = _cmul(atr, ati, cr, ci)
        cr, ci = nr + er[b:b + 1, :], ni + ei[b:b + 1, :]
    return outr, outi


def _scan_fixup(xr_ref, xi_ref, cr8, ci8, ar8, ai8, reverse, pair=None):
    tile = lambda idx: pl.ds(pl.multiple_of(idx * SCAN_BLOCKS, SCAN_BLOCKS), SCAN_BLOCKS)

    def fix(idx, pr, pi):
        fr, fi = _cmul(pr, pi, cr8, ci8)
        nr, ni = xr_ref[tile(idx), :] + fr, xi_ref[tile(idx), :] + fi
        xr_ref[tile(idx), :] = nr
        xi_ref[tile(idx), :] = ni
        return nr, ni

    if pair is None:
        def step(i, carry):
            pr, pi = carry
            fix((SCAN_T - 1 - i) if reverse else i, pr, pi)
            return _cmul(pr, pi, ar8, ai8)

        lax.fori_loop(0, SCAN_T, step, (ar8, ai8), unroll=2)
        return None

    sr_ref, si_ref = pair
    earlier = -1 if reverse else 1

    def step(i, carry):
        pr, pi, accr, acci = carry
        idx = (SCAN_T - 1 - i) if reverse else i
        nr, ni = fix(idx, pr, pi)
        qr, qi = _cmul(nr, ni, sr_ref[tile(idx + earlier), :], -si_ref[tile(idx + earlier), :])
        pr, pi = _cmul(pr, pi, ar8, ai8)
        return pr, pi, accr + qr, acci + qi

    z = jnp.zeros(ar8.shape, F32)
    pr, pi, accr, acci = lax.fori_loop(0, SCAN_T - 1, step, (ar8, ai8, z, z))
    edge, src, shift, empty = (0, SCAN_T - 1, 1, 0) if reverse else (SCAN_T - 1, 0, SCAN_BLOCKS - 1, SCAN_BLOCKS - 1)
    nr, ni = fix(edge, pr, pi)
    row = lax.broadcasted_iota(jnp.int32, ar8.shape, 0)
    spr = jnp.where(row == empty, 0.0, pltpu.roll(sr_ref[tile(src), :], shift, 0))
    spi = jnp.where(row == empty, 0.0, pltpu.roll(si_ref[tile(src), :], shift, 0))
    qr, qi = _cmul(nr, ni, spr, -spi)
    return jnp.sum(accr + qr, axis=0, keepdims=True), jnp.sum(acci + qi, axis=0, keepdims=True)


def _scan(xr_ref, xi_ref, ar, ai, reverse, pair=None):
    n = ar.shape[1]
    ar8 = jnp.broadcast_to(ar, (SCAN_BLOCKS, n))
    ai8 = jnp.broadcast_to(ai, (SCAN_BLOCKS, n))
    er, ei = _scan_local(xr_ref, xi_ref, ar8, ai8, reverse)
    atr, ati = _cpow(ar, ai, SCAN_T)
    cr8, ci8 = _scan_carries(er, ei, atr, ati, reverse)
    return _scan_fixup(xr_ref, xi_ref, cr8, ci8, ar8, ai8, reverse, pair)


def _s5_specs():
    chan = pl.BlockSpec((SEQ, CH_W), lambda c, d: (0, c))
    chan2 = pl.BlockSpec((None, SEQ, CH_W), lambda c, d: (d, 0, c))
    state = pl.BlockSpec((None, SEQ, ST_W), lambda c, d: (d, 0, c))
    bmat = pl.BlockSpec((None, None, CH_W, ST_W), lambda c, d: (d, c, 0, 0))
    cmat = pl.BlockSpec((None, None, ST_W, CH_W), lambda c, d: (d, c, 0, 0))
    avec = pl.BlockSpec((None, None, 1, ST_W), lambda c, d: (d, c, 0, 0))
    return chan, chan2, state, bmat, cmat, avec


def _scan_by_direction(xr_ref, xi_ref, ar, ai, d, adjoint, pair=None, da_out=None):
    for direction in range(2):
        @pl.when(d == direction)
        def _(direction=direction):
            res = _scan(xr_ref, xi_ref, ar, ai, adjoint != (direction == 1), pair)
            if pair is not None:
                da_out[0][...], da_out[1][...] = res


def _s5_scan_fwd(u, bre, bim, are, aim, cre, cim):
    def body(u_ref, bre_ref, bim_ref, are_ref, aim_ref, cre_ref, cim_ref, sr_ref, si_ref, y_ref):
        ub = u_ref[...].astype(BF16)
        sr_ref[...] = _dot(ub, bre_ref[...])
        si_ref[...] = _dot(ub, bim_ref[...])
        _scan_by_direction(sr_ref, si_ref, are_ref[...], aim_ref[...], pl.program_id(1), adjoint=False)
        y_ref[...] = _dot(sr_ref[...].astype(BF16), cre_ref[...]) - _dot(si_ref[...].astype(BF16), cim_ref[...])

    chan, chan2, state, bmat, cmat, avec = _s5_specs()
    return pl.pallas_call(
        body, grid=(S5_CHUNKS, 2), in_specs=[chan, bmat, bmat, avec, avec, cmat, cmat], out_specs=[state, state, chan2],
        out_shape=[SDS((2, SEQ, S5_GROUPS * S5_STATE), F32)] * 2 + [SDS((2, SEQ, S5_WIDTH), F32)],
        name="s5_scan_fwd", compiler_params=_params(("parallel", "parallel")))(u, bre, bim, are, aim, cre, cim)


def _diag_out(out_ref, full):
    for g in range(8):
        out_ref[g] = full[g * S5_GROUP:(g + 1) * S5_GROUP, g * S5_STATE:(g + 1) * S5_STATE]


def _s5_scan_bwd(dy, du_skip, u, sr, si, bre, bim, are, aim, cre, cim):
    def body(dy_ref, dus_ref, u_ref, sr_ref, si_ref, bre_ref, bim_ref, are_ref, aim_ref, cre_ref, cim_ref,
             du_ref, dbr_ref, dbi_ref, dcr_ref, dci_ref, dar_ref, dai_ref, gr_ref, gi_ref):
        d = pl.program_id(1)
        dyb = dy_ref[...].astype(BF16)
        gr_ref[...] = _dot_nt(dyb, cre_ref[...])
        gi_ref[...] = -_dot_nt(dyb, cim_ref[...])
        _diag_out(dcr_ref, _dot_tn(dyb, sr_ref[...].astype(BF16)))
        _diag_out(dci_ref, -_dot_tn(dyb, si_ref[...].astype(BF16)))
        _scan_by_direction(gr_ref, gi_ref, are_ref[...], -aim_ref[...], d, adjoint=True, pair=(sr_ref, si_ref),
                           da_out=(dar_ref, dai_ref))

        @pl.when(d == 0)
        def _():
            du_ref[...] = dus_ref[...]

        grb = gr_ref[...].astype(BF16)
        gib = gi_ref[...].astype(BF16)
        du_ref[...] += _dot_nt(grb, bre_ref[...]) + _dot_nt(gib, bim_ref[...])
        ub = u_ref[...].astype(BF16)
        _diag_out(dbr_ref, _dot_tn(ub, grb))
        _diag_out(dbi_ref, _dot_tn(ub, gib))

    chan, _, state, bmat, cmat, avec = _s5_specs()
    diag = pl.BlockSpec((None, None, 8, S5_GROUP, S5_STATE), lambda c, d: (d, c, 0, 0, 0))
    return pl.pallas_call(
        body, grid=(S5_CHUNKS, 2), in_specs=[chan, chan, chan, state, state, bmat, bmat, avec, avec, cmat, cmat],
        out_specs=[chan, diag, diag, diag, diag, avec, avec],
        out_shape=[SDS((SEQ, S5_WIDTH), F32)] + [SDS((2, S5_CHUNKS, 8, S5_GROUP, S5_STATE), F32)] * 4
                  + [SDS((2, S5_CHUNKS, 1, ST_W), F32)] * 2,
        scratch_shapes=[pltpu.VMEM((SEQ, ST_W), F32), pltpu.VMEM((SEQ, ST_W), F32)],
        name="s5_scan_bwd", compiler_params=_params(("parallel", "arbitrary")))(dy, du_skip, u, sr, si, bre, bim, are, aim, cre, cim)


_GELU_K = math.sqrt(2.0 / math.pi)
_GELU_C = 0.044715


def _gelu(x):
    t = jnp.tanh(_GELU_K * (x + _GELU_C * x * x * x))
    return 0.5 * x * (1.0 + t), t


def _s5_glu_fwd(u, y2, dskip, wglu, bglu):
    def body(u_ref, y0_ref, y1_ref, d_ref, w_ref, b_ref, o_ref, yp_ref):
        ypre = u_ref[...] * d_ref[...] + y0_ref[...] + y1_ref[...]
        yp_ref[...] = ypre
        y, _ = _gelu(ypre)
        z = _dot(y.astype(BF16), w_ref[...]) + b_ref[...]
        o_ref[...] = y * jax.nn.sigmoid(z)

    row = _row_spec(S5_WIDTH)
    vec = _fix_spec((1, S5_WIDTH))
    dir0 = pl.BlockSpec((None, ROW_TILE, S5_WIDTH), lambda i: (0, i, 0))
    dir1 = pl.BlockSpec((None, ROW_TILE, S5_WIDTH), lambda i: (1, i, 0))
    return pl.pallas_call(
        body, grid=(N_ROW_TILES,), in_specs=[row, dir0, dir1, vec, _fix_spec((S5_WIDTH, S5_WIDTH)), vec],
        out_specs=[row, row], out_shape=[SDS((SEQ, S5_WIDTH), F32)] * 2, name="s5_glu_fwd",
        compiler_params=_params(("parallel",)))(u, y2, y2, dskip, wglu, bglu)


def _s5_glu_bwd(do, ypre, u, dskip, wglu, bglu):
    def body(do_ref, yp_ref, u_ref, d_ref, w_ref, b_ref, dyp_ref, du_ref, dw_ref, db_ref, dd_ref):
        i = pl.program_id(0)
        ypre = yp_ref[...]
        y, t = _gelu(ypre)
        yb = y.astype(BF16)
        sg = jax.nn.sigmoid(_dot(yb, w_ref[...]) + b_ref[...])
        dov = do_ref[...]
        dz = dov * y * sg * (1.0 - sg)
        dzb = dz.astype(BF16)
        dy = dov * sg + _dot_nt(dzb, w_ref[...])
        dgelu = 0.5 * (1.0 + t) + 0.5 * ypre * (1.0 - t * t) * _GELU_K * (1.0 + 3.0 * _GELU_C * ypre * ypre)
        dyp = dy * dgelu
        dyp_ref[...] = dyp
        uv = u_ref[...]
        du_ref[...] = dyp * d_ref[...]

        @pl.when(i == 0)
        def _():
            dw_ref[...] = jnp.zeros_like(dw_ref)
            db_ref[...] = jnp.zeros_like(db_ref)
            dd_ref[...] = jnp.zeros_like(dd_ref)

        dw_ref[...] += _dot_tn(yb, dzb)
        db_ref[...] += jnp.sum(dz, axis=0, keepdims=True)
        dd_ref[...] += jnp.sum(dyp * uv, axis=0, keepdims=True)

    row = _row_spec(S5_WIDTH)
    vec = _fix_spec((1, S5_WIDTH))
    mat = _fix_spec((S5_WIDTH, S5_WIDTH))
    return pl.pallas_call(
        body, grid=(N_ROW_TILES,), in_specs=[row, row, row, vec, mat, vec], out_specs=[row, row, mat, vec, vec],
        out_shape=[SDS((SEQ, S5_WIDTH), F32)] * 2 + [SDS((S5_WIDTH, S5_WIDTH), F32), SDS((1, S5_WIDTH), F32), SDS((1, S5_WIDTH), F32)],
        name="s5_glu_bwd", compiler_params=_params(("arbitrary",)))(do, ypre, u, dskip, wglu, bglu)


def _heads_side_by_side(o_ref):
    return jnp.concatenate([o_ref[h] for h in range(HEADS)], axis=-1)


def _mix_out_fwd(ona, os5, g_na, g_s5, wout):
    def body(a_ref, s_ref, ga_ref, gs_ref, w_ref, o_ref):
        av, sv = _heads_side_by_side(a_ref), s_ref[...]
        ca = (av * _rstd(av) * ga_ref[...]).astype(BF16)
        cs = (sv * _rstd(sv) * gs_ref[...]).astype(BF16)
        o_ref[...] = _dot(ca, w_ref[0:NA_WIDTH, :]) + _dot(cs, w_ref[NA_WIDTH:, :])

    row = _row_spec(NA_WIDTH)
    vec = _fix_spec((1, NA_WIDTH))
    heads = pl.BlockSpec((HEADS, ROW_TILE, HEAD_DIM), lambda i: (0, i, 0))
    return pl.pallas_call(
        body, grid=(N_ROW_TILES,), in_specs=[heads, row, vec, vec, _fix_spec((D_MODEL, D_MODEL))],
        out_specs=_row_spec(D_MODEL), out_shape=SDS((SEQ, D_MODEL), F32), name="mix_out_fwd",
        compiler_params=_params(("parallel",)))(ona, os5, g_na, g_s5, wout)


def _mix_out_bwd(dmix, ona, os5, g_na, g_s5, wout):
    def body(dm_ref, a_ref, s_ref, ga_ref, gs_ref, w_ref, da_ref, ds_ref, dw_ref, dga_ref, dgs_ref):
        i = pl.program_id(0)
        dm = dm_ref[...]
        av, sv = _heads_side_by_side(a_ref), s_ref[...]
        ra, rs = _rstd(av), _rstd(sv)
        ga, gs = ga_ref[...], gs_ref[...]
        ca = (av * ra * ga).astype(BF16)
        cs = (sv * rs * gs).astype(BF16)
        dca = _dot_nt(dm, w_ref[0:NA_WIDTH, :])
        dcs = _dot_nt(dm, w_ref[NA_WIDTH:, :])
        da, dga = _rms_bwd(av, ra, ga, dca)
        ds, dgs = _rms_bwd(sv, rs, gs, dcs)
        for h in range(HEADS):
            da_ref[h] = da[:, h * HEAD_DIM:(h + 1) * HEAD_DIM]
        ds_ref[...] = ds

        @pl.when(i == 0)
        def _():
            dw_ref[...] = jnp.zeros_like(dw_ref)
            dga_ref[...] = jnp.zeros_like(dga_ref)
            dgs_ref[...] = jnp.zeros_like(dgs_ref)

        dw_ref[0:NA_WIDTH, :] += _dot_tn(ca, dm)
        dw_ref[NA_WIDTH:, :] += _dot_tn(cs, dm)
        dga_ref[...] += jnp.sum(dga, axis=0, keepdims=True)
        dgs_ref[...] += jnp.sum(dgs, axis=0, keepdims=True)

    row = _row_spec(NA_WIDTH)
    vec = _fix_spec((1, NA_WIDTH))
    mat = _fix_spec((D_MODEL, D_MODEL))
    heads = pl.BlockSpec((HEADS, ROW_TILE, HEAD_DIM), lambda i: (0, i, 0))
    return pl.pallas_call(
        body, grid=(N_ROW_TILES,), in_specs=[_row_spec(D_MODEL), heads, row, vec, vec, mat],
        out_specs=[heads, row, mat, vec, vec],
        out_shape=[SDS((HEADS, SEQ, HEAD_DIM), F32), SDS((SEQ, NA_WIDTH), F32), SDS((D_MODEL, D_MODEL), F32),
                   SDS((1, NA_WIDTH), F32), SDS((1, NA_WIDTH), F32)],
        name="mix_out_bwd", compiler_params=_params(("arbitrary",)))(dmix, ona, os5, g_na, g_s5, wout)


def _me():
    x, y, c = lax.axis_index("x"), lax.axis_index("y"), lax.axis_index("c")
    return x, y, c, 4 * x + 2 * y + c


def _peer(k):
    x, y, c, _ = _me()
    px = 1 - x if (k >> 2) & 1 else x
    py = 1 - y if (k >> 1) & 1 else y
    pc = 1 - c if k & 1 else c
    return (px, py, pc), 4 * px + 2 * py + pc


ALL_PEERS = (1, 2, 3, 4, 5, 6, 7)
CHIP_PEERS = (2, 4, 6)
SIBLING = 1


def _slot8(pos):
    return 4 * pos[0] + 2 * pos[1] + pos[2]


def _slot4(pos):
    return 2 * pos[0] + pos[1]


_HBM = pl.BlockSpec(memory_space=pltpu.HBM)
_SEM = pl.BlockSpec(memory_space=pltpu.SEMAPHORE)
_EFFECT = pltpu.SideEffectType.DATAFLOW_SIDE_EFFECTING


def _exchange_start(arrays, lands, gather, name, peers=ALL_PEERS, slot=_slot8, own=True):
    n = len(arrays)

    def body(*refs):
        ins, lnd = refs[:n], refs[n:2 * n]
        send_sems, recv_sems = refs[2 * n], refs[2 * n + 1]
        token = refs[-1]
        me = slot(_me()[:3])
        for i, k in enumerate(peers):
            peer, _ = _peer(k)
            for a in range(n):
                src = ins[a] if gather else ins[a].at[slot(peer)]
                s = a * len(peers) + i
                pltpu.make_async_remote_copy(src_ref=src, dst_ref=lnd[a].at[me], send_sem=send_sems.at[s],
                                             recv_sem=recv_sems.at[s], device_id=peer, device_id_type=MESH).start()
        if own:
            for a in range(n):
                pltpu.make_async_copy(ins[a] if gather else ins[a].at[me], lnd[a].at[me], recv_sems.at[n * len(peers) + a]).start()
        token[...] = jnp.zeros_like(token)

    sems = pltpu.SemaphoreType.DMA((n * (len(peers) + int(own)),))
    out = pl.pallas_call(
        body, name=name, in_specs=[_HBM] * (2 * n),
        out_shape=(sems, sems) + tuple(pltpu.HBM(a.shape, a.dtype) for a in list(arrays) + list(lands)) + (SDS((8, 128), F32),),
        out_specs=(_SEM, _SEM) + (_HBM,) * (2 * n) + (pl.BlockSpec(memory_space=pltpu.VMEM),),
        input_output_aliases={i: 2 + i for i in range(2 * n)},
        compiler_params=pltpu.CompilerParams(has_side_effects=_EFFECT),
    )(*[pltpu.with_memory_space_constraint(a, pltpu.HBM) for a in list(arrays) + list(lands)])
    return out[0], out[1], list(out[2:2 + n]), list(out[2 + n:2 + 2 * n]), out[-1]


def _exchange_wait(send_sems, recv_sems, arrays, lands, after, gather, name, peers=ALL_PEERS, slot=_slot8, own=True):
    n = len(arrays)

    def body(*refs):
        ins, lnd = refs[:n], refs[n:2 * n]
        send_sems, recv_sems = refs[2 * n], refs[2 * n + 1]
        if own:
            me = slot(_me()[:3])
            for a in range(n):
                pltpu.make_async_copy(ins[a] if gather else ins[a].at[me], lnd[a].at[me], recv_sems.at[n * len(peers) + a]).wait()
        for i, k in enumerate(peers):
            peer, _ = _peer(k)
            for a in range(n):
                src = ins[a] if gather else ins[a].at[slot(peer)]
                s = a * len(peers) + i
                cp = pltpu.make_async_remote_copy(src_ref=src, dst_ref=lnd[a].at[slot(peer)], send_sem=send_sems.at[s],
                                                  recv_sem=recv_sems.at[s], device_id=peer, device_id_type=MESH)
                cp.wait_send()
                cp.wait_recv()

        refs[-1][...] = jnp.zeros_like(refs[-1])

    after = list(after) if isinstance(after, (list, tuple)) else [after]
    out = pl.pallas_call(
        body, name=name, in_specs=[_HBM] * (2 * n) + [_SEM, _SEM] + [pl.BlockSpec(memory_space=pl.ANY)] * len(after),
        out_shape=tuple(pltpu.HBM(a.shape, a.dtype) for a in list(arrays) + list(lands)) + (SDS((8, 128), F32),),
        out_specs=(_HBM,) * (2 * n) + (pl.BlockSpec(memory_space=pltpu.VMEM),), input_output_aliases={i: i for i in range(2 * n)},
        compiler_params=pltpu.CompilerParams(has_side_effects=_EFFECT),
    )(*arrays, *lands, send_sems, recv_sems, *after)
    return list(out[n:2 * n]), out[-1]


def _forward_sibling(lands, name):
    n = len(lands)

    def body(*refs):
        outs = refs[n:2 * n]
        send_sems, recv_sems = refs[2 * n:]
        x, y, c, _ = _me()
        sends = []
        for i, k in enumerate(CHIP_PEERS):
            peer, _ = _peer(k)
            for a in range(n):
                rows = outs[a].at[_slot8(peer)]
                cp = pltpu.make_async_remote_copy(src_ref=rows, dst_ref=rows, send_sem=send_sems.at[a, i], recv_sem=recv_sems.at[a, i],
                                                  device_id=(x, y, 1 - c), device_id_type=MESH)
                cp.start()
                sends.append(cp)
        for i, k in enumerate(CHIP_PEERS):
            (px, py, pc), _ = _peer(k)
            for a in range(n):
                rows = outs[a].at[_slot8((px, py, 1 - pc))]
                pltpu.make_async_remote_copy(src_ref=rows, dst_ref=rows, send_sem=send_sems.at[a, i], recv_sem=recv_sems.at[a, i],
                                             device_id=(x, y, 1 - c), device_id_type=MESH).wait_recv()
        for cp in sends:
            cp.wait_send()

    return pl.pallas_call(
        body, in_specs=[_HBM] * n, out_specs=[_HBM] * n, out_shape=[SDS(a.shape, a.dtype) for a in lands],
        input_output_aliases={i: i for i in range(n)},
        scratch_shapes=[pltpu.SemaphoreType.DMA((n, len(CHIP_PEERS))), pltpu.SemaphoreType.DMA((n, len(CHIP_PEERS)))],
        name=name)(*lands)


def _swap_sibling(arrays, name, after=()):
    n, n_after = len(arrays), len(after)
    chips = N_DEV // 2

    def body(*refs):
        ins, outs = refs[:n], refs[n + n_after:2 * n + n_after]
        send_sems, recv_sems = refs[2 * n + n_after:]
        x, y, c, _ = _me()
        sends = []
        for q in range(chips):
            for a in range(n):
                cp = pltpu.make_async_remote_copy(src_ref=ins[a].at[q, 1 - c], dst_ref=outs[a].at[q], send_sem=send_sems.at[a, q],
                                                  recv_sem=recv_sems.at[a, q], device_id=(x, y, 1 - c), device_id_type=MESH)
                cp.start()
                sends.append(cp)
        for cp in sends:
            cp.wait_recv()
        for cp in sends:
            cp.wait_send()

    return pl.pallas_call(
        body, in_specs=[_HBM] * n + [pl.BlockSpec(memory_space=pl.ANY)] * n_after, out_specs=[_HBM] * n,
        out_shape=[SDS((chips,) + a.shape[2:], a.dtype) for a in arrays],
        scratch_shapes=[pltpu.SemaphoreType.DMA((n, chips)), pltpu.SemaphoreType.DMA((n, chips))], name=name)(*arrays, *after)


def _sum_pairs(mine, theirs, name):
    n = len(mine)
    chips = mine[0].shape[0]
    c = lax.axis_index("c")

    def body(c_ref, *refs):
        for a in range(n):
            refs[2 * n + a][...] = (refs[a][...].astype(F32) + refs[n + a][...].astype(F32)).astype(refs[2 * n + a].dtype)

    def pair(a):
        return pl.BlockSpec((None, None) + a.shape[2:], lambda q, c_ref: (q, c_ref[0], 0, 0))

    def single(a):
        return pl.BlockSpec((None,) + a.shape[2:], lambda q, c_ref: (q, 0, 0))

    return pl.pallas_call(
        body, grid_spec=pltpu.PrefetchScalarGridSpec(
            num_scalar_prefetch=1, grid=(chips,), in_specs=[pair(a) for a in mine] + [single(a) for a in mine],
            out_specs=[single(a) for a in mine]),
        out_shape=[SDS((chips,) + a.shape[2:], a.dtype) for a in mine], name=name,
        compiler_params=_params(("parallel",)))(c.reshape(1).astype(jnp.int32), *mine, *theirs)


def _adamw_math(w, g, m, v):
    m = ADAM_B1 * m + (1.0 - ADAM_B1) * g
    v = ADAM_B2 * v + (1.0 - ADAM_B2) * (g * g)
    m_hat = m / (1.0 - ADAM_B1 ** ADAM_STEP)
    v_hat = v / (1.0 - ADAM_B2 ** ADAM_STEP)
    delta = -ADAM_LR * (m_hat / (jnp.sqrt(v_hat) + ADAM_EPS) + ADAM_WD * w)
    return delta, m, v


def _adamw(w, m, v, pieces, name):
    rows, cols = w.shape[-2:]
    lead = w.ndim - 2
    tile = rows
    for cand in (256, 176, 128, 64, 16):
        if rows > cand and rows % cand == 0:
            tile = cand
            break

    def body(w_ref, m_ref, v_ref, p_ref, g_ref, d_ref, mo_ref, vo_ref):
        g = _sum_pieces(p_ref)
        g_ref[...] = g
        d_ref[...], mo_ref[...], vo_ref[...] = _adamw_math(w_ref[...], g, m_ref[...], v_ref[...])

    blk = pl.BlockSpec((None,) * lead + (tile, cols), lambda i: (0,) * lead + (i, 0))
    return pl.pallas_call(
        body, grid=(rows // tile,), in_specs=[blk, blk, blk, pl.BlockSpec((pieces.shape[0], tile, cols), lambda i: (0, i, 0))],
        out_specs=[blk] * 4, out_shape=[SDS(w.shape, F32)] * 4, name=name,
        compiler_params=_params(("parallel",)))(w, m, v, pieces)


def _sum_pieces(p_ref):
    g = p_ref[0].astype(F32)
    for p in range(1, p_ref.shape[0]):
        g = g + p_ref[p].astype(F32)
    return g


def _adamw_s5_mat(w, m, v, g, name):
    _, ndir, groups, b, c = w.shape
    per_dir = groups // 8

    def body(w_ref, m_ref, v_ref, g_ref, d_ref, mo_ref, vo_ref):
        d_ref[...], mo_ref[...], vo_ref[...] = _adamw_math(w_ref[...], g_ref[...], m_ref[...], v_ref[...])

    blk = pl.BlockSpec((None, None, 8, b, c), lambda i: (0, i // per_dir, i % per_dir, 0, 0))
    return pl.pallas_call(
        body, grid=(ndir * per_dir,), in_specs=[blk] * 4, out_specs=[blk] * 3, out_shape=[SDS(w.shape, F32)] * 3, name=name,
        compiler_params=_params(("parallel",)))(w, m, v, g)


VEC_ROWS = ['ffn1_pre_g', 'ffn1_post_g', 'mix_pre_g', 'mix_post_g', 'ffn2_pre_g', 'ffn2_post_g', 'final_g',
            ('na_out_g', 's5_out_g'), ('s5_d', 's5_b_glu')]
VEC_NAMES = [n for row in VEC_ROWS for n in ((row,) if isinstance(row, str) else row)]
VEC_PACK_ROWS = 16
LOSS_ROW = len(VEC_ROWS)


def _pack_vectors(grads, loss8):
    def body(*refs):
        o_ref = refs[-1]
        o_ref[...] = jnp.zeros_like(o_ref)
        o_ref[LOSS_ROW:LOSS_ROW + 1, 0:128] = refs[-2][0:1, :]
        k = 0
        for i, row in enumerate(VEC_ROWS):
            if isinstance(row, str):
                o_ref[i:i + 1, :] = refs[k][...]
                k += 1
            else:
                o_ref[i:i + 1, 0:NA_WIDTH] = refs[k][...]
                o_ref[i:i + 1, NA_WIDTH:] = refs[k + 1][...]
                k += 2

    return pl.pallas_call(body, out_shape=SDS((VEC_PACK_ROWS, D_MODEL), F32), name="pack_vectors",
                          compiler_params=_params())(*[grads[n] for n in VEC_NAMES], loss8)


def _sum8(pieces, name):
    def body(p_ref, o_ref):
        o_ref[...] = _sum_pieces(p_ref)

    return pl.pallas_call(body, out_shape=SDS(pieces.shape[1:], F32), name=name, compiler_params=_params())(pieces)


def _adamw_small(packed8, vec_wmv, others):
    n_vec, n_oth = len(VEC_NAMES), len(others)

    def body(*refs):
        p_ref = refs[0]
        ins = refs[1:1 + 3 * n_vec + 4 * n_oth]
        outs = refs[1 + 3 * n_vec + 4 * n_oth:]
        gsum = _sum_pieces(p_ref)
        outs[-1][...] = gsum[LOSS_ROW:LOSS_ROW + 1, 0:128]
        k = 0
        for i, row in enumerate(VEC_ROWS):
            parts = [(row, gsum[i:i + 1, :])] if isinstance(row, str) else \
                [(row[0], gsum[i:i + 1, 0:NA_WIDTH]), (row[1], gsum[i:i + 1, NA_WIDTH:])]
            for _, g in parts:
                w_ref, m_ref, v_ref = ins[3 * k:3 * k + 3]
                outs[4 * k][...] = g
                outs[4 * k + 1][...], outs[4 * k + 2][...], outs[4 * k + 3][...] = _adamw_math(w_ref[...], g, m_ref[...], v_ref[...])
                k += 1
        for j in range(n_oth):
            w_ref, m_ref, v_ref, g_ref = ins[3 * n_vec + 4 * j:3 * n_vec + 4 * j + 4]
            g = _sum_pieces(g_ref)
            g = g[tuple(slice(0, s) for s in w_ref.shape[1:])].reshape(w_ref.shape)
            o = outs[4 * (n_vec + j):4 * (n_vec + j) + 4]
            o[0][...] = g
            o[1][...], o[2][...], o[3][...] = _adamw_math(w_ref[...], g, m_ref[...], v_ref[...])

    args, out_shape = [packed8], []
    for w, m, v in vec_wmv:
        args += [w, m, v]
        out_shape += [SDS(w.shape, F32)] * 4
    for w, m, v, g in others:
        args += [w, m, v, g]
        out_shape += [SDS(w.shape, F32)] * 4
    out_shape += [SDS((1, 128), F32)]
    return pl.pallas_call(body, out_shape=out_shape, name="adamw_small", compiler_params=_params())(*args)


def _perm_rows(x):
    return x.reshape(SCAN_BLOCKS, SCAN_T, x.shape[-1]).transpose(1, 0, 2).reshape(SEQ, x.shape[-1])


def _unperm_rows(x):
    return x.reshape(SCAN_T, SCAN_BLOCKS, x.shape[-1]).transpose(1, 0, 2).reshape(SEQ, x.shape[-1])


def _block_diag(x):
    eye = np.eye(8, dtype=bool)[None, None, :, None, :, None]
    full = jnp.where(eye, x[:, :, :, :, None, :], 0.0)
    return full.reshape(2, S5_CHUNKS, 8 * x.shape[3], 8 * x.shape[4])


STORED_SWAPPED = {"ffn1_w_gate": (1, 2), "ffn1_w_up": (1, 2), "ffn2_w_gate": (1, 2), "ffn2_w_up": (1, 2),
                  "s5_b_re": (3, 4), "s5_b_im": (3, 4)}


def _stored(name, x):
    return jnp.swapaxes(x, *STORED_SWAPPED[name]) if name in STORED_SWAPPED else x


def _dep(x, token):
    return x if token is None else x + token


def _local_step(x, target, get_w, small, emit):
    bias = _rpb_expand(small["na_rpb"][0])
    lr = small["s5_lam_re"].reshape(64, S5_STATE)
    li = small["s5_lam_im"].reshape(64, S5_STATE)
    logdt = small["s5_log_dt"].reshape(64, 1)
    b_t = [_stored(n, small[n]).reshape(64, S5_GROUP, S5_STATE) for n in ("s5_b_re", "s5_b_im")]
    lbr, lbi, bbr, bbi = _s5_prep(lr, li, logdt, b_t[0], b_t[1])
    are = lbr.reshape(2, S5_CHUNKS, 1, ST_W)
    aim = lbi.reshape(2, S5_CHUNKS, 1, ST_W)
    bre = _block_diag(bbr.reshape(2, S5_CHUNKS, 8, S5_GROUP, S5_STATE)).astype(BF16)
    bim = _block_diag(bbi.reshape(2, S5_CHUNKS, 8, S5_GROUP, S5_STATE)).astype(BF16)
    c_t = [small[n].reshape(2, S5_CHUNKS, 8, S5_GROUP, S5_STATE).transpose(0, 1, 2, 4, 3) for n in ("s5_c_re", "s5_c_im")]
    cre = _block_diag(c_t[0]).astype(BF16)
    cim = _block_diag(c_t[1]).astype(BF16)
    tgt = jnp.concatenate([jnp.zeros((N_META, D_MODEL), F32), target], axis=0)

    h0, a1 = _embed_prenorm(get_w("meta", None)["meta_tokens"], x, small["ffn1_pre_g"])
    wts = dict(get_w("ffn1", [bias, are, aim, bre, bim, cre, cim, tgt, a1]))
    gate1, up1, f1 = _ffn_fwd(a1, wts["ffn1_w_gate"], wts["ffn1_w_up"], wts["ffn1_w_down"], "ffn1_fwd",
                              after=wts.get("tokens", ()))
    h1, a2 = _post_pre(f1, h0, small["ffn1_post_g"], small["mix_pre_g"], 0.5, "post_pre1")
    wts.update(get_w("w_in", a2))
    qkv = _proj_heads(a2, wts["w_in"])
    u = _proj_u(a2, wts["w_in"])
    ona = _na_fwd(qkv, bias)
    u_p = _perm_rows(u)
    sr, si, y2 = _s5_scan_fwd(u_p, bre, bim, are, aim, cre, cim)
    wts.update(get_w("mix", y2))
    os5_p, ypre_p = _s5_glu_fwd(u_p, y2, small["s5_d"], wts["s5_w_glu"], small["s5_b_glu"])
    os5 = _unperm_rows(os5_p)

    mix = _mix_out_fwd(ona, os5, small["na_out_g"], small["s5_out_g"], wts["w_out"])
    h2, a3 = _post_pre(mix, h1, small["mix_post_g"], small["ffn2_pre_g"], 1.0, "post_pre2")
    wts.update(get_w("ffn2", a3))
    gate2, up2, f2 = _ffn_fwd(a3, wts["ffn2_w_gate"], wts["ffn2_w_up"], wts["ffn2_w_down"], "ffn2_fwd")
    loss8, dh3, df2, g_final, g_ffn2_post = _final_loss(f2, h2, small["ffn2_post_g"], small["final_g"], tgt)

    da3, dwg2, dwu2, dwd2 = _ffn_bwd(df2, a3, gate2, up2, wts["ffn2_w_gate"], wts["ffn2_w_up"], wts["ffn2_w_down"], "ffn2_bwd")
    tok = emit("ffn2", {"ffn2_w_gate": dwg2, "ffn2_w_up": dwu2, "ffn2_w_down": dwd2})
    dh2, dmix, g_ffn2_pre, g_mix_post = _bwd_pre_post(da3, h2, _dep(small["ffn2_pre_g"], tok), dh3, mix, small["mix_post_g"], 1.0,
                                                      "bwd_pre_post2")
    dona, dos5, dwout, g_na_out, g_s5_out = _mix_out_bwd(dmix, ona, os5, small["na_out_g"], small["s5_out_g"], wts["w_out"])

    dypre_p, du_skip_p, dwglu, g_b_glu, g_s5_d = _s5_glu_bwd(_perm_rows(dos5), ypre_p, u_p, small["s5_d"], wts["s5_w_glu"],
                                                             small["s5_b_glu"])
    tok = emit("mix", {"s5_w_glu": dwglu.reshape(N_DEV, S5_WIDTH // N_DEV, S5_WIDTH).astype(BF16),
                       "w_out": dwout.reshape(N_DEV, D_MODEL // N_DEV, D_MODEL).astype(BF16)})
    du_p, dbr, dbi, dcr, dci, dar, dai = _s5_scan_bwd(dypre_p, du_skip_p, u_p, sr, si, bre, bim, _dep(are, tok), aim, cre, cim)
    du = _unperm_rows(du_p)
    per_group = (2 * S5_GROUPS, S5_GROUP, S5_STATE)
    g_lr, g_li, g_dt, g_br, g_bi = _s5_prep_bwd(lr, li, logdt, b_t[0], b_t[1], dar.reshape(64, S5_STATE),
                                                dai.reshape(64, S5_STATE), dbr.reshape(per_group), dbi.reshape(per_group))
    g_c = [dcr.reshape(per_group), dci.reshape(per_group)]

    dq, dk, dv, dbias = _na_bwd(qkv, bias, dona)
    g_rpb = _rpb_reduce(dbias)
    dense = jnp.stack([g.reshape(2 * S5_GROUPS, S5_STATE * S5_GROUP) for g in (g_br, g_bi, *g_c)])
    tok = emit("small", {"dense": dense, "na_rpb": g_rpb,
                         "s5_lam_re": g_lr.reshape(2, S5_GROUPS, S5_STATE), "s5_lam_im": g_li.reshape(2, S5_GROUPS, S5_STATE),
                         "s5_log_dt": g_dt.reshape(2, S5_GROUPS)})
    da2, dwin = _proj_bwd(dq, dk, dv, du, a2, wts["w_in"])
    tok2 = emit("w_in", {"w_in": dwin})
    tok = tok if tok2 is None else tok + tok2
    dh1, df1, g_mix_pre, g_ffn1_post = _bwd_pre_post(da2, h1, _dep(small["mix_pre_g"], tok), dh2, f1, small["ffn1_post_g"], 0.5,
                                                     "bwd_pre_post1")
    da1, dwg1, dwu1, dwd1 = _ffn_bwd(df1, a1, gate1, up1, wts["ffn1_w_gate"], wts["ffn1_w_up"], wts["ffn1_w_down"], "ffn1_bwd")
    grad_x, grad_meta, g_ffn1_pre = _bwd_embed(da1, h0, small["ffn1_pre_g"], dh1)
    vec_g = {
        "ffn1_pre_g": g_ffn1_pre, "ffn1_post_g": g_ffn1_post, "mix_pre_g": g_mix_pre, "s5_d": g_s5_d, "s5_b_glu": g_b_glu,
        "na_out_g": g_na_out, "s5_out_g": g_s5_out, "mix_post_g": g_mix_post,
        "ffn2_pre_g": g_ffn2_pre, "ffn2_post_g": g_ffn2_post, "final_g": g_final,
    }
    emit("vec", {"packed": _pack_vectors(vec_g, loss8), "meta_tokens": grad_meta})
    emit("ffn1", {"ffn1_w_gate": dwg1, "ffn1_w_up": dwu1, "ffn1_w_down": dwd1})
    return grad_x


WEIGHT_NAMES = ['meta_tokens', 'ffn1_pre_g', 'ffn1_post_g', 'ffn1_w_gate', 'ffn1_w_up', 'ffn1_w_down', 'mix_pre_g', 'w_in',
                'na_rpb', 's5_lam_re', 's5_lam_im', 's5_log_dt', 's5_b_re', 's5_b_im', 's5_c_re', 's5_c_im', 's5_d',
                's5_w_glu', 's5_b_glu', 'na_out_g', 's5_out_g', 'w_out', 'mix_post_g', 'ffn2_pre_g', 'ffn2_post_g',
                'ffn2_w_gate', 'ffn2_w_up', 'ffn2_w_down', 'final_g']
BIG_NAMES = ['ffn1_w_gate', 'ffn1_w_up', 'ffn1_w_down', 'w_in', 's5_w_glu', 'w_out', 'ffn2_w_gate', 'ffn2_w_up', 'ffn2_w_down']
SMALL_NAMES = [n for n in WEIGHT_NAMES if n not in BIG_NAMES and n != 'meta_tokens']
WHOLE_NAMES = ['na_rpb', 's5_lam_re', 's5_lam_im', 's5_log_dt']
LEAD_NAMES = ['s5_b_re', 's5_b_im', 's5_c_re', 's5_c_im']


def kernel(x, meta_tokens, ffn1_pre_g, ffn1_post_g, ffn1_w_gate, ffn1_w_up, ffn1_w_down, mix_pre_g, w_in, na_rpb, s5_lam_re, s5_lam_im, s5_log_dt, s5_b_re, s5_b_im, s5_c_re, s5_c_im, s5_d, s5_w_glu, s5_b_glu, na_out_g, s5_out_g, w_out, mix_post_g, ffn2_pre_g, ffn2_post_g, ffn2_w_gate, ffn2_w_up, ffn2_w_down, final_g, loss_target, m_meta_tokens, m_ffn1_pre_g, m_ffn1_post_g, m_ffn1_w_gate, m_ffn1_w_up, m_ffn1_w_down, m_mix_pre_g, m_w_in, m_na_rpb, m_s5_lam_re, m_s5_lam_im, m_s5_log_dt, m_s5_b_re, m_s5_b_im, m_s5_c_re, m_s5_c_im, m_s5_d, m_s5_w_glu, m_s5_b_glu, m_na_out_g, m_s5_out_g, m_w_out, m_mix_post_g, m_ffn2_pre_g, m_ffn2_post_g, m_ffn2_w_gate, m_ffn2_w_up, m_ffn2_w_down, m_final_g, v_meta_tokens, v_ffn1_pre_g, v_ffn1_post_g, v_ffn1_w_gate, v_ffn1_w_up, v_ffn1_w_down, v_mix_pre_g, v_w_in, v_na_rpb, v_s5_lam_re, v_s5_lam_im, v_s5_log_dt, v_s5_b_re, v_s5_b_im, v_s5_c_re, v_s5_c_im, v_s5_d, v_s5_w_glu, v_s5_b_glu, v_na_out_g, v_s5_out_g, v_w_out, v_mix_post_g, v_ffn2_pre_g, v_ffn2_post_g, v_ffn2_w_gate, v_ffn2_w_up, v_ffn2_w_down, v_final_g):
    args = dict(locals())
    w = {n: args[n] for n in WEIGHT_NAMES}
    m = {n: args["m_" + n] for n in WEIGHT_NAMES}
    v = {n: args["v_" + n] for n in WEIGHT_NAMES}

    small = {n: w[n] for n in SMALL_NAMES}

    pending = {}

    def start(group, names, arrays, gather, peers=ALL_PEERS, slot=_slot8):
        n_slots = N_DEV if slot is _slot8 else N_DEV // 2
        lands = [lax.empty((n_slots,) + a.shape if gather else a.shape, a.dtype) for a in arrays]
        send_sems, recv_sems, arrays, lands, token = _exchange_start(arrays, lands, gather, "start_" + group, peers, slot)
        pending[group] = (names, send_sems, recv_sems, arrays, lands, gather, peers, slot)
        return token

    def finish(group, after):
        names, send_sems, recv_sems, arrays, lands, gather, peers, slot = pending.pop(group)
        lands, token = _exchange_wait(send_sems, recv_sems, arrays, lands, after, gather, "wait_" + group, peers, slot)
        return dict(zip(names, lands)), token

    first = ["ffn1_w_gate", "ffn1_w_up", "ffn1_w_down"]
    def shard(n, token=None):
        return _dep(_stored(n, w[n])[0], None if token is None else token[0, 0]).astype(BF16)

    ffn_names = ("ffn1_w_gate", "ffn1_w_up", "ffn1_w_down", "ffn2_w_gate", "ffn2_w_up", "ffn2_w_down")
    later_groups = (("w_in", ["w_in"]), ("mix", ["s5_w_glu", "w_out"]), ("ffn2", ["ffn2_w_gate", "ffn2_w_up", "ffn2_w_down"]))
    token0 = start("meta", ["meta_tokens"], [w["meta_tokens"]], True)
    token1 = start("ffn1", first, [shard(n, token0) for n in first], True, (SIBLING,) + CHIP_PEERS)
    meta_full = finish("meta", [token1])[0]["meta_tokens"].transpose(1, 0, 2).reshape(N_META, D_MODEL)
    later_shards = {n: shard(n, token1) for _, names in later_groups for n in names}
    for n in ("na_rpb", "s5_lam_re"):
        small[n] = _dep(small[n], token1[0, 0])

    def get_w(group, after):
        if group == "meta":
            return {"meta_tokens": meta_full}
        if group == "ffn1":
            after = list(after) + list(later_shards.values())
        got, token = finish(group, after)
        if group == "ffn1":
            got = dict(zip(got, _forward_sibling(list(got.values()), "forward_ffn1")))
            got["tokens"] = [start(g, names + ["order"], [later_shards[n] for n in names] + [token], True) for g, names in later_groups]
        if group == "mix":
            got = {"s5_w_glu": got["s5_w_glu"].reshape(S5_WIDTH, S5_WIDTH), "w_out": got["w_out"].reshape(D_MODEL, D_MODEL)}
        return {n: (a.reshape(D_FF, D_MODEL) if n in ffn_names else a) for n, a in got.items()}

    tokens = {}

    def emit(group, grads):
        grads = {n: (g.reshape(N_DEV, FF_SHARD, D_MODEL) if n in ffn_names else g) for n, g in grads.items()}
        if group == "ffn1":
            mine = [g.reshape((N_DEV // 2, 2) + g.shape[1:]) for g in grads.values()]
            theirs = _swap_sibling(mine, "swap_g_ffn1", after=[tokens["vec"]])
            sums = _sum_pairs(mine, theirs, "pair_sum_g_ffn1")
            tokens[group] = start("g_ffn1", list(grads), sums, False, CHIP_PEERS, _slot4)
        else:
            tokens[group] = start("g_" + group, list(grads), list(grads.values()), group in ("small", "vec"))
        return tokens[group][0, 0]

    grad_x = _local_step(x[0], loss_target[0], get_w, small, emit)
    res = {}

    def update_shard(n, pieces):
        outs = _adamw(_stored(n, w[n]), _stored(n, m[n]), _stored(n, v[n]), pieces, "adamw_" + n)
        res[n] = [_stored(n, o) for o in outs]

    late = [grad_x, tokens["ffn1"]]
    for group in ("g_ffn2", "g_mix", "g_w_in"):
        for n, pieces in finish(group, late)[0].items():
            update_shard(n, pieces)
    g8 = finish("g_small", late)[0]
    dense = _sum8(g8["dense"], "sum_dense")
    for i, n in enumerate(LEAD_NAMES):
        g = dense[i].reshape(_stored(n, w[n]).shape)
        upd = _adamw_s5_mat(_stored(n, w[n]), _stored(n, m[n]), _stored(n, v[n]), g, "adamw_" + n)
        res[n] = [_stored(n, o) for o in [g] + list(upd)]

    done = [res[n][1] for n in ("ffn2_w_gate", "ffn2_w_up", "ffn2_w_down", "w_in", "w_out", "s5_w_glu") + tuple(LEAD_NAMES)]
    got = finish("g_vec", done)[0]
    packed8, gmeta8 = got["packed"], got["meta_tokens"]
    for n, pieces in finish("g_ffn1", packed8)[0].items():
        update_shard(n, pieces)
    _, _, _, me = _me()
    update_shard("meta_tokens", lax.dynamic_slice_in_dim(gmeta8, me * (D_MODEL // N_DEV), D_MODEL // N_DEV, axis=2))

    outs = _adamw_small(packed8, [(w[n], m[n], v[n]) for n in VEC_NAMES], [(w[n], m[n], v[n], g8[n]) for n in WHOLE_NAMES])
    for i, n in enumerate(VEC_NAMES + WHOLE_NAMES):
        res[n] = list(outs[4 * i:4 * i + 4])

    out = [outs[-1][0, 0], grad_x[None]]
    for kind in range(4):
        out += [res[n][kind] for n in WEIGHT_NAMES]
    return tuple(out)
```

```python
import math

import numpy as np
import jax
import jax.numpy as jnp
from jax import lax
from jax.experimental import pallas as pl
from jax.experimental.pallas import tpu as pltpu

F32 = jnp.float32
BF16 = jnp.bfloat16
SDS = jax.ShapeDtypeStruct

D_MODEL = 1024
N_TOK = 2048
N_META = 16
SEQ = N_TOK + N_META
ROW_TILE = 688
N_ROW_TILES = SEQ // ROW_TILE
N_DEV = 8
D_FF = 2816
FF_SHARD = D_FF // N_DEV
FF_TILE = 256
IN_SHARD = 256
NA_WIDTH = 512
S5_WIDTH = 512
HEADS = 8
HEAD_DIM = 64
GRID_W = 64
GRID_ROWS = N_TOK // GRID_W
KH = 8
KW = 16
NA_RB = 4
NA_KR = KH + NA_RB - 1
NA_BLOCKS = GRID_ROWS // NA_RB
NA_QB = NA_RB * GRID_W
NA_KB = NA_KR * GRID_W
NA_TYPES = 3
S5_GROUPS = 32
S5_GROUP = 16
S5_STATE = 64
S5_CHUNKS = 4
CH_W = S5_WIDTH // S5_CHUNKS
ST_W = S5_GROUPS * S5_STATE // S5_CHUNKS
SCAN_BLOCKS = 8
SCAN_T = SEQ // SCAN_BLOCKS
RMS_EPS = 1e-6
NEG_INF = -1e30
ATT_SCALE = HEAD_DIM ** -0.5
ADAM_LR, ADAM_B1, ADAM_B2, ADAM_EPS, ADAM_WD, ADAM_STEP = 0.001, 0.9, 0.999, 1e-08, 0.01, 10
VMEM_LIMIT = 56 * 1024 * 1024
MESH = pl.DeviceIdType.MESH


def _params(sem=None):
    return pltpu.CompilerParams(dimension_semantics=sem, vmem_limit_bytes=VMEM_LIMIT)


def _dot(a, b):
    return jnp.dot(a, b, preferred_element_type=F32)


def _dot_nt(a, b):
    return lax.dot_general(a, b, (((1,), (1,)), ((), ())), preferred_element_type=F32)


def _dot_tn(a, b):
    return lax.dot_general(a, b, (((0,), (0,)), ((), ())), preferred_element_type=F32)


def _rstd(x):
    return lax.rsqrt(jnp.mean(x * x, axis=-1, keepdims=True) + RMS_EPS)


def _rms_bwd(x, r, g, dy):
    dyg = dy * g
    xr = x * r
    dx = r * (dyg - xr * jnp.mean(dyg * xr, axis=-1, keepdims=True))
    return dx, dy * xr


def _rows(i, size=ROW_TILE):
    return pl.ds(pl.multiple_of(i * size, 16), size)


def _row_spec(width):
    return pl.BlockSpec((ROW_TILE, width), lambda i: (i, 0))


def _fix_spec(shape):
    return pl.BlockSpec(shape, lambda i: (0,) * len(shape))


def _split3(x):
    hi = x.astype(BF16)
    r1 = x - hi.astype(F32)
    mid = r1.astype(BF16)
    lo = (r1 - mid.astype(F32)).astype(BF16)
    return hi, mid, lo


def _embed_prenorm(meta, x, g):
    def body(m_ref, x_ref, g_ref, h_ref, a_ref):
        h_ref[0:N_META, :] = m_ref[...]
        h_ref[N_META:, :] = x_ref[...]
        for i in range(N_ROW_TILES):
            rows = slice(i * ROW_TILE, (i + 1) * ROW_TILE)
            hv = h_ref[rows, :]
            a_ref[rows, :] = (hv * _rstd(hv) * g_ref[...]).astype(BF16)

    return pl.pallas_call(
        body, out_shape=[SDS((SEQ, D_MODEL), F32), SDS((SEQ, D_MODEL), BF16)], name="embed_prenorm",
        compiler_params=_params())(meta, x, g)


def _post_pre(f, hres, g_post, g_next, scale, name):
    def body(f_ref, h_ref, gp_ref, gn_ref, ho_ref, a_ref):
        fv = f_ref[...]
        h = h_ref[...] + scale * (fv * _rstd(fv) * gp_ref[...])
        ho_ref[...] = h
        a_ref[...] = (h * _rstd(h) * gn_ref[...]).astype(BF16)

    return pl.pallas_call(
        body, grid=(N_ROW_TILES,),
        in_specs=[_row_spec(D_MODEL), _row_spec(D_MODEL), _fix_spec((1, D_MODEL)), _fix_spec((1, D_MODEL))],
        out_specs=[_row_spec(D_MODEL), _row_spec(D_MODEL)],
        out_shape=[SDS((SEQ, D_MODEL), F32), SDS((SEQ, D_MODEL), BF16)], name=name,
        compiler_params=_params(("parallel",)))(f, hres, g_post, g_next)


def _final_loss(f2, h2, g_post, g_final, target):
    def body(f_ref, h_ref, gp_ref, gf_ref, t_ref, loss_ref, dh_ref, df_ref, dgf_ref, dgp_ref):
        i = pl.program_id(0)
        fv = f_ref[...]
        r1 = _rstd(fv)
        gp = gp_ref[...]
        h3 = h_ref[...] + 0.5 * (fv * r1 * gp)
        r2 = _rstd(h3)
        gf = gf_ref[...]
        y = h3 * r2 * gf
        row = lax.broadcasted_iota(jnp.int32, (ROW_TILE, 1), 0) + i * ROW_TILE
        err = jnp.where(row >= N_META, y - t_ref[...], 0.0)
        part = 0.5 * jnp.sum(jnp.mean(err * err, axis=-1, keepdims=True))
        dy = err * (1.0 / D_MODEL)
        dh3, dgf = _rms_bwd(h3, r2, gf, dy)
        dh_ref[...] = dh3
        df, dgp = _rms_bwd(fv, r1, gp, 0.5 * dh3)
        df_ref[...] = df.astype(BF16)

        @pl.when(i == 0)
        def _():
            loss_ref[...] = jnp.zeros_like(loss_ref)
            dgf_ref[...] = jnp.zeros_like(dgf_ref)
            dgp_ref[...] = jnp.zeros_like(dgp_ref)

        loss_ref[...] += part
        dgf_ref[...] += jnp.sum(dgf, axis=0, keepdims=True)
        dgp_ref[...] += jnp.sum(dgp, axis=0, keepdims=True)

    gain = _fix_spec((1, D_MODEL))
    return pl.pallas_call(
        body, grid=(N_ROW_TILES,),
        in_specs=[_row_spec(D_MODEL), _row_spec(D_MODEL), gain, gain, _row_spec(D_MODEL)],
        out_specs=[_fix_spec((8, 128)), _row_spec(D_MODEL), _row_spec(D_MODEL), gain, gain],
        out_shape=[SDS((8, 128), F32), SDS((SEQ, D_MODEL), F32), SDS((SEQ, D_MODEL), BF16),
                   SDS((1, D_MODEL), F32), SDS((1, D_MODEL), F32)],
        name="final_loss", compiler_params=_params(("arbitrary",)))(f2, h2, g_post, g_final, target)


def _bwd_pre_post(da, h, g_pre, dh_res, fprev, g_post, scale, name):
    def body(da_ref, h_ref, gpre_ref, dhr_ref, f_ref, gpost_ref, dh_ref, df_ref, dgpre_ref, dgpost_ref):
        i = pl.program_id(0)
        hv = h_ref[...]
        dxa, dgpre = _rms_bwd(hv, _rstd(hv), gpre_ref[...], da_ref[...])
        dh = dhr_ref[...] + dxa
        dh_ref[...] = dh
        fv = f_ref[...]
        df, dgpost = _rms_bwd(fv, _rstd(fv), gpost_ref[...], scale * dh)
        df_ref[...] = df.astype(BF16)

        @pl.when(i == 0)
        def _():
            dgpre_ref[...] = jnp.zeros_like(dgpre_ref)
            dgpost_ref[...] = jnp.zeros_like(dgpost_ref)

        dgpre_ref[...] += jnp.sum(dgpre, axis=0, keepdims=True)
        dgpost_ref[...] += jnp.sum(dgpost, axis=0, keepdims=True)

    gain = _fix_spec((1, D_MODEL))
    row = _row_spec(D_MODEL)
    return pl.pallas_call(
        body, grid=(N_ROW_TILES,), in_specs=[row, row, gain, row, row, gain],
        out_specs=[row, row, gain, gain],
        out_shape=[SDS((SEQ, D_MODEL), F32), SDS((SEQ, D_MODEL), BF16), SDS((1, D_MODEL), F32), SDS((1, D_MODEL), F32)],
        name=name, compiler_params=_params(("arbitrary",)))(da, h, g_pre, dh_res, fprev, g_post)


def _bwd_embed(da, h, g_pre, dh_res):
    def body(da_ref, h_ref, gpre_ref, dhr_ref, gx_ref, gm_ref, dgpre_ref):
        total = jnp.zeros((1, D_MODEL), F32)
        for i in range(N_ROW_TILES):
            rows = slice(i * ROW_TILE, (i + 1) * ROW_TILE)
            hv = h_ref[rows, :]
            dxa, dgpre = _rms_bwd(hv, _rstd(hv), gpre_ref[...], da_ref[rows, :])
            dh = dhr_ref[rows, :] + dxa
            total = total + jnp.sum(dgpre, axis=0, keepdims=True)
            if i == 0:
                gm_ref[...] = dh[0:N_META, :]
                gx_ref[0:ROW_TILE - N_META, :] = dh[N_META:, :]
            else:
                gx_ref[i * ROW_TILE - N_META:(i + 1) * ROW_TILE - N_META, :] = dh
        dgpre_ref[...] = total

    return pl.pallas_call(
        body, out_shape=[SDS((N_TOK, D_MODEL), F32), SDS((N_META, D_MODEL), F32), SDS((1, D_MODEL), F32)],
        name="bwd_embed", compiler_params=_params())(da, h, g_pre, dh_res)


def _ffn_fwd(a, wg, wu, wd, name, after=()):
    def body(a_ref, wg_ref, wu_ref, wd_ref, *rest):
        gate_ref, up_ref, f_ref = rest[len(after):]
        j = pl.program_id(0)

        def tile(i, carry):
            rows = _rows(i)
            at = a_ref[rows, :]
            gate = _dot_nt(at, wg_ref[...])
            up = _dot_nt(at, wu_ref[...])
            gate_ref[rows, :] = gate.astype(BF16)
            up_ref[rows, :] = up.astype(BF16)
            act = (gate * jax.nn.sigmoid(gate) * up).astype(BF16)
            contrib = _dot(act, wd_ref[...])

            @pl.when(j == 0)
            def _():
                f_ref[rows, :] = contrib

            @pl.when(j != 0)
            def _():
                f_ref[rows, :] += contrib

            return carry

        lax.fori_loop(0, N_ROW_TILES, tile, 0)

    wtile = pl.BlockSpec((FF_TILE, D_MODEL), lambda j: (j, 0))
    hid = pl.BlockSpec((SEQ, FF_TILE), lambda j: (0, j))
    full = pl.BlockSpec((SEQ, D_MODEL), lambda j: (0, 0))
    return pl.pallas_call(
        body, grid=(D_FF // FF_TILE,), in_specs=[full, wtile, wtile, wtile] + [pl.BlockSpec(memory_space=pl.ANY)] * len(after),
        out_specs=[hid, hid, full],
        out_shape=[SDS((SEQ, D_FF), BF16), SDS((SEQ, D_FF), BF16), SDS((SEQ, D_MODEL), F32)],
        name=name, compiler_params=_params(("arbitrary",)))(a, wg, wu, wd, *after)


def _ffn_bwd(df, a, gate, up, wg, wu, wd, name):
    def body(df_ref, a_ref, gate_ref, up_ref, wg_ref, wu_ref, wd_ref, da_ref, dwg_ref, dwu_ref, dwd_ref,
             acc_g, acc_u, acc_d):
        j = pl.program_id(0)

        def tile(i, carry):
            rows = _rows(i)
            dft = df_ref[rows, :]
            at = a_ref[rows, :]
            gate = gate_ref[rows, :].astype(F32)
            up = up_ref[rows, :].astype(F32)
            dact = _dot_nt(dft, wd_ref[...])
            sig = jax.nn.sigmoid(gate)
            silu = gate * sig
            dgate = (dact * up * (sig * (1.0 + gate * (1.0 - sig)))).astype(BF16)
            dup = (dact * silu).astype(BF16)
            act = (silu * up).astype(BF16)
            dwd = _dot_tn(act, dft)
            dwg = _dot_tn(dgate, at)
            dwu = _dot_tn(dup, at)
            dat = _dot(dgate, wg_ref[...]) + _dot(dup, wu_ref[...])

            @pl.when(i == 0)
            def _():
                acc_d[...] = dwd
                acc_g[...] = dwg
                acc_u[...] = dwu

            @pl.when(i != 0)
            def _():
                acc_d[...] += dwd
                acc_g[...] += dwg
                acc_u[...] += dwu

            @pl.when(j == 0)
            def _():
                da_ref[rows, :] = dat

            @pl.when(j != 0)
            def _():
                da_ref[rows, :] += dat

            return carry

        lax.fori_loop(0, N_ROW_TILES, tile, 0)
        dwg_ref[...] = acc_g[...].astype(BF16)
        dwu_ref[...] = acc_u[...].astype(BF16)
        dwd_ref[...] = acc_d[...].astype(BF16)

    wtile = pl.BlockSpec((FF_TILE, D_MODEL), lambda j: (j, 0))
    hid = pl.BlockSpec((SEQ, FF_TILE), lambda j: (0, j))
    full = pl.BlockSpec((SEQ, D_MODEL), lambda j: (0, 0))
    return pl.pallas_call(
        body, grid=(D_FF // FF_TILE,), in_specs=[full, full, hid, hid, wtile, wtile, wtile],
        out_specs=[full, wtile, wtile, wtile],
        out_shape=[SDS((SEQ, D_MODEL), F32)] + [SDS((D_FF, D_MODEL), BF16)] * 3,
        scratch_shapes=[pltpu.VMEM((FF_TILE, D_MODEL), F32)] * 3,
        name=name, compiler_params=_params(("arbitrary",)))(df, a, gate, up, wg, wu, wd)


HEADS_PER_BLOCK = IN_SHARD // HEAD_DIM
QKV_BLOCKS = 3 * NA_WIDTH // IN_SHARD


def _proj_heads(a, w):
    def body(a_ref, w_ref, o_ref):
        def tile(i, carry):
            rows = _rows(i)
            res = _dot(a_ref[rows, :], w_ref[...])
            for sub in range(HEADS_PER_BLOCK):
                o_ref[sub, rows, :] = res[:, sub * HEAD_DIM:(sub + 1) * HEAD_DIM]
            return carry

        lax.fori_loop(0, N_ROW_TILES, tile, 0)

    return pl.pallas_call(
        body, grid=(QKV_BLOCKS,),
        in_specs=[pl.BlockSpec((SEQ, D_MODEL), lambda j: (0, 0)), pl.BlockSpec((None, D_MODEL, IN_SHARD), lambda j: (j, 0, 0))],
        out_specs=pl.BlockSpec((HEADS_PER_BLOCK, SEQ, HEAD_DIM), lambda j: (j, 0, 0)),
        out_shape=SDS((3 * HEADS, SEQ, HEAD_DIM), F32), name="proj_heads",
        compiler_params=_params(("parallel",)))(a, w)


def _proj_u(a, w):
    def body(a_ref, w_ref, o_ref):
        def tile(i, carry):
            rows = _rows(i)
            o_ref[rows, :] = _dot(a_ref[rows, :], w_ref[...])
            return carry

        lax.fori_loop(0, N_ROW_TILES, tile, 0)

    return pl.pallas_call(
        body, grid=(N_DEV - QKV_BLOCKS,),
        in_specs=[pl.BlockSpec((SEQ, D_MODEL), lambda j: (0, 0)),
                  pl.BlockSpec((None, D_MODEL, IN_SHARD), lambda j: (j + QKV_BLOCKS, 0, 0))],
        out_specs=pl.BlockSpec((SEQ, IN_SHARD), lambda j: (0, j)),
        out_shape=SDS((SEQ, S5_WIDTH), F32), name="proj_u",
        compiler_params=_params(("parallel",)))(a, w)


def _proj_bwd(dq, dk, dv, du, a, w):
    def body(dq_ref, dk_ref, dv_ref, du_ref, a_ref, w_ref, da_ref, dw_ref, acc, dp_ref):
        j = pl.program_id(0)

        for which, src in enumerate((dq_ref, dk_ref, dv_ref)):
            @pl.when((j >= 2 * which) & (j < 2 * which + 2))
            def _(src=src):
                dp_ref[...] = jnp.concatenate([src[sub] for sub in range(HEADS_PER_BLOCK)], axis=-1).astype(BF16)

        @pl.when(j >= QKV_BLOCKS)
        def _():
            dp_ref[...] = du_ref[...].astype(BF16)

        def tile(i, carry):
            rows = _rows(i)
            dpt = dp_ref[rows, :]
            dw = _dot_tn(a_ref[rows, :], dpt)
            dat = _dot_nt(dpt, w_ref[...])

            @pl.when(i == 0)
            def _():
                acc[...] = dw

            @pl.when(i != 0)
            def _():
                acc[...] += dw

            @pl.when(j == 0)
            def _():
                da_ref[rows, :] = dat

            @pl.when(j != 0)
            def _():
                da_ref[rows, :] += dat

            return carry

        lax.fori_loop(0, N_ROW_TILES, tile, 0)
        dw_ref[...] = acc[...].astype(BF16)

    full = pl.BlockSpec((SEQ, D_MODEL), lambda j: (0, 0))
    wspec = pl.BlockSpec((None, D_MODEL, IN_SHARD), lambda j: (j, 0, 0))

    def heads(which):
        return pl.BlockSpec((HEADS_PER_BLOCK, SEQ, HEAD_DIM), lambda j: (jnp.clip(j - 2 * which, 0, 1), 0, 0))

    return pl.pallas_call(
        body, grid=(N_DEV,),
        in_specs=[heads(0), heads(1), heads(2),
                  pl.BlockSpec((SEQ, IN_SHARD), lambda j: (0, jnp.clip(j - QKV_BLOCKS, 0, 1))), full, wspec],
        out_specs=[full, wspec],
        out_shape=[SDS((SEQ, D_MODEL), F32), SDS((N_DEV, D_MODEL, IN_SHARD), BF16)],
        scratch_shapes=[pltpu.VMEM((D_MODEL, IN_SHARD), F32), pltpu.VMEM((SEQ, IN_SHARD), BF16)],
        name="proj_bwd", compiler_params=_params(("arbitrary",)))(dq, dk, dv, du, a, w)


def _na_consts():
    c = np.arange(GRID_W)
    col_start = np.clip(c - KW // 2, 0, GRID_W - KW)
    col_in = (c[None, :] >= col_start[:, None]) & (c[None, :] < col_start[:, None] + KW)
    dc = np.clip(c[None, :] - c[:, None] + KW - 1, 0, 2 * KW - 2)
    onehot = np.zeros((128, GRID_W * GRID_W), np.float32)
    qq, kk = np.meshgrid(c, c, indexing="ij")
    onehot[dc[col_in], (qq * GRID_W + kk)[col_in]] = 1.0
    negmask = np.where(col_in, 0.0, NEG_INF).astype(np.float32).reshape(1, -1)
    return onehot, negmask


def _na_pair(block_type, a, b):
    if block_type == 0:
        return b - a + KH - 1 if b < KH else None
    if block_type == 1:
        return b - a + KH // 2 - 1 if a <= b < a + KH else None
    return b - a if b >= NA_KR - KH else None


def _rpb_expand(rpb):
    onehot, negmask = _na_consts()
    rows = HEADS * (2 * KH - 1)
    rpb_pad = jnp.pad(rpb.reshape(rows, 2 * KW - 1), ((0, 128 - rows), (0, 128 - (2 * KW - 1))))

    def body(r_ref, oh_ref, m_ref, t_ref):
        hi, mid, lo = _split3(r_ref[...])
        oh = oh_ref[...]
        t_ref[...] = _dot(hi, oh) + _dot(mid, oh) + _dot(lo, oh) + m_ref[...]

    table = pl.pallas_call(body, out_shape=SDS((128, GRID_W * GRID_W), F32), name="rpb_expand",
                           compiler_params=_params())(rpb_pad, jnp.asarray(onehot, BF16), jnp.asarray(negmask))
    return table[:rows].reshape(HEADS, 2 * KH - 1, GRID_W, GRID_W)


def _rpb_reduce(dslabs):
    onehot, _ = _na_consts()
    rows = HEADS * (2 * KH - 1)

    def body(x_ref, oht_ref, o_ref):
        hi, mid, lo = _split3(x_ref[...])
        oht = oht_ref[...]
        o_ref[...] = _dot(hi, oht) + _dot(mid, oht) + _dot(lo, oht)

    out = pl.pallas_call(body, out_shape=SDS((rows, 128), F32), name="rpb_reduce", compiler_params=_params())(
        dslabs.reshape(rows, GRID_W * GRID_W), jnp.asarray(onehot.T, BF16))
    return out.reshape(HEADS, 2 * KH - 1, 128)


def _bias_tiles(slab_ref, tile_ref):
    tile_ref[...] = jnp.full(tile_ref.shape, NEG_INF, F32)
    for t in range(NA_TYPES):
        for a in range(NA_RB):
            for b in range(NA_KR):
                dr = _na_pair(t, a, b)
                if dr is not None:
                    tile_ref[t, a * GRID_W:(a + 1) * GRID_W, b * GRID_W:(b + 1) * GRID_W] = slab_ref[dr]


def _bias_tiles_bwd(dtile_ref, dslab_ref):
    acc = {}
    for t in range(NA_TYPES):
        for a in range(NA_RB):
            for b in range(NA_KR):
                dr = _na_pair(t, a, b)
                if dr is not None:
                    part = dtile_ref[t, a * GRID_W:(a + 1) * GRID_W, b * GRID_W:(b + 1) * GRID_W]
                    acc[dr] = part if dr not in acc else acc[dr] + part
    for dr in range(2 * KH - 1):
        dslab_ref[dr] = acc[dr]


def _block_geometry(g):
    if isinstance(g, int):
        start = min(max(g * NA_RB - KH // 2, 0), GRID_ROWS - NA_KR)
        return (0 if g == 0 else 2 if g == NA_BLOCKS - 1 else 1), N_META + g * NA_QB, N_META + start * GRID_W
    start = jnp.clip(g * NA_RB - KH // 2, 0, GRID_ROWS - NA_KR)
    block_type = jnp.where(g == 0, 0, jnp.where(g == NA_BLOCKS - 1, 2, 1))
    q0 = pl.multiple_of(N_META + g * NA_QB, 16)
    k0 = pl.multiple_of(N_META + start * GRID_W, 16)
    return block_type, q0, k0


def _scaled_q(q):
    return (q * ATT_SCALE).astype(BF16)


def _na_probs(qs, kk, km, bias):
    s = _dot_nt(qs, kk) + bias
    sm = _dot_nt(qs, km)
    m = jnp.maximum(jnp.max(s, axis=-1, keepdims=True), jnp.max(sm, axis=-1, keepdims=True))
    p = jnp.exp(s - m)
    pm = jnp.exp(sm - m)
    inv = 1.0 / (jnp.sum(p, axis=-1, keepdims=True) + jnp.sum(pm, axis=-1, keepdims=True))
    return p * inv, pm * inv


def _meta_probs(qm, km):
    s = _dot_nt(qm, km) * ATT_SCALE
    p = jnp.exp(s - jnp.max(s, axis=-1, keepdims=True))
    return p / jnp.sum(p, axis=-1, keepdims=True)


def _qkv_specs():
    return [pl.BlockSpec((None, SEQ, HEAD_DIM), lambda h, which=which: (h + which * HEADS, 0, 0)) for which in range(3)]


def _na_fwd(qkv, bias):
    def body(q_ref, k_ref, v_ref, slab_ref, o_ref, b_ref):
        _bias_tiles(slab_ref, b_ref)
        km = k_ref[0:N_META, :].astype(BF16)
        vm = v_ref[0:N_META, :].astype(BF16)
        pmm = _meta_probs(q_ref[0:N_META, :].astype(BF16), km)
        o_ref[0:N_META, :] = _dot(pmm.astype(BF16), vm)

        def block(g, carry):
            block_type, q0, k0 = _block_geometry(g)
            qs = _scaled_q(q_ref[pl.ds(q0, NA_QB), :])
            kk = k_ref[pl.ds(k0, NA_KB), :].astype(BF16)
            vv = v_ref[pl.ds(k0, NA_KB), :].astype(BF16)
            p, pm = _na_probs(qs, kk, km, b_ref[block_type])
            o_ref[pl.ds(q0, NA_QB), :] = _dot(p.astype(BF16), vv) + _dot(pm.astype(BF16), vm)
            return carry

        for g in range(NA_BLOCKS):
            block(g, 0)

    head = pl.BlockSpec((None, SEQ, HEAD_DIM), lambda h: (h, 0, 0))
    return pl.pallas_call(
        body, grid=(HEADS,), in_specs=_qkv_specs() + [pl.BlockSpec((None, 2 * KH - 1, GRID_W, GRID_W), lambda h: (h, 0, 0, 0))],
        out_specs=head, out_shape=SDS((HEADS, SEQ, HEAD_DIM), F32), name="na_fwd",
        scratch_shapes=[pltpu.VMEM((NA_TYPES, NA_QB, NA_KB), F32)],
        compiler_params=_params(("parallel",)))(qkv, qkv, qkv, bias)


def _na_bwd(qkv, bias, do):
    def body(q_ref, k_ref, v_ref, slab_ref, do_ref, dq_ref, dk_ref, dv_ref, dslab_ref, b_ref, db_ref):
        _bias_tiles(slab_ref, b_ref)
        km = k_ref[0:N_META, :].astype(BF16)
        vm = v_ref[0:N_META, :].astype(BF16)
        dk_ref[...] = jnp.zeros_like(dk_ref)
        dv_ref[...] = jnp.zeros_like(dv_ref)
        db_ref[...] = jnp.zeros_like(db_ref)

        qm = q_ref[0:N_META, :].astype(BF16)
        dom = do_ref[0:N_META, :].astype(BF16)
        pmm = _meta_probs(qm, km)
        dpm = _dot_nt(dom, vm)
        dsm = (pmm * (dpm - jnp.sum(pmm * dpm, axis=-1, keepdims=True)) * ATT_SCALE).astype(BF16)
        dq_ref[0:N_META, :] = _dot(dsm, km)
        dkm0 = _dot_tn(dsm, qm)
        dvm0 = _dot_tn(pmm.astype(BF16), dom)

        def block(g, carry):
            dkm, dvm = carry
            block_type, q0, k0 = _block_geometry(g)
            qs = _scaled_q(q_ref[pl.ds(q0, NA_QB), :])
            kk = k_ref[pl.ds(k0, NA_KB), :].astype(BF16)
            vv = v_ref[pl.ds(k0, NA_KB), :].astype(BF16)
            dob = do_ref[pl.ds(q0, NA_QB), :].astype(BF16)
            p, pm = _na_probs(qs, kk, km, b_ref[block_type])
            dp = _dot_nt(dob, vv)
            dpm_ = _dot_nt(dob, vm)
            delta = jnp.sum(p * dp, axis=-1, keepdims=True) + jnp.sum(pm * dpm_, axis=-1, keepdims=True)
            ds = p * (dp - delta)
            dsm_ = pm * (dpm_ - delta)
            db_ref[block_type] += ds
            dsb = ds.astype(BF16)
            dsmb = dsm_.astype(BF16)
            dq_ref[pl.ds(q0, NA_QB), :] = (_dot(dsb, kk) + _dot(dsmb, km)) * ATT_SCALE
            dk_ref[pl.ds(k0, NA_KB), :] += _dot_tn(dsb, qs)
            dv_ref[pl.ds(k0, NA_KB), :] += _dot_tn(p.astype(BF16), dob)
            return dkm + _dot_tn(dsmb, qs), dvm + _dot_tn(pm.astype(BF16), dob)

        dkm, dvm = dkm0, dvm0
        for g in range(NA_BLOCKS):
            dkm, dvm = block(g, (dkm, dvm))
        dk_ref[0:N_META, :] = dkm
        dv_ref[0:N_META, :] = dvm
        _bias_tiles_bwd(db_ref, dslab_ref)

    head = pl.BlockSpec((None, SEQ, HEAD_DIM), lambda h: (h, 0, 0))
    bspec = pl.BlockSpec((None, 2 * KH - 1, GRID_W, GRID_W), lambda h: (h, 0, 0, 0))
    return pl.pallas_call(
        body, grid=(HEADS,), in_specs=_qkv_specs() + [bspec, head], out_specs=[head, head, head, bspec],
        out_shape=[SDS((HEADS, SEQ, HEAD_DIM), F32)] * 3 + [SDS((HEADS, 2 * KH - 1, GRID_W, GRID_W), F32)],
        scratch_shapes=[pltpu.VMEM((NA_TYPES, NA_QB, NA_KB), F32), pltpu.VMEM((NA_TYPES, NA_QB, NA_KB), F32)],
        name="na_bwd", compiler_params=_params(("parallel",)))(qkv, qkv, qkv, bias, do)


def _cmul(ar, ai, br, bi):
    return ar * br - ai * bi, ar * bi + ai * br


def _cpow(ar, ai, n):
    rr, ri = None, None
    br, bi = ar, ai
    while n:
        if n & 1:
            rr, ri = (br, bi) if rr is None else _cmul(rr, ri, br, bi)
        n >>= 1
        if n:
            br, bi = _cmul(br, bi, br, bi)
    return rr, ri


def _s5_prep(lr, li, logdt, bre, bim):
    def body(lr_ref, li_ref, dt_ref, br_ref, bi_ref, lbr_ref, lbi_ref, bbr_ref, bbi_ref):
        lr_, li_ = lr_ref[...], li_ref[...]
        dt = jnp.exp(dt_ref[...])
        mag = jnp.exp(lr_ * dt)
        lbr = mag * jnp.cos(li_ * dt)
        lbi = mag * jnp.sin(li_ * dt)
        lbr_ref[...] = lbr
        lbi_ref[...] = lbi
        den = lr_ * lr_ + li_ * li_
        xr = lbr - 1.0
        cr = (xr * lr_ + lbi * li_) / den
        ci = (lbi * lr_ - xr * li_) / den
        br, bi = br_ref[...], bi_ref[...]
        bbr_ref[...] = cr[:, None, :] * br - ci[:, None, :] * bi
        bbi_ref[...] = cr[:, None, :] * bi + ci[:, None, :] * br

    n = 2 * S5_GROUPS
    return pl.pallas_call(
        body, out_shape=[SDS((n, S5_STATE), F32)] * 2 + [SDS((n, S5_GROUP, S5_STATE), F32)] * 2,
        name="s5_prep", compiler_params=_params())(lr, li, logdt, bre, bim)


def _s5_prep_bwd(lr, li, logdt, bre, bim, dar, dai, dbbr, dbbi):
    def body(lr_ref, li_ref, dt_ref, br_ref, bi_ref, dar_ref, dai_ref, dbr_ref, dbi_ref,
             glr_ref, gli_ref, gdt_ref, gbr_ref, gbi_ref):
        lr_, li_ = lr_ref[...], li_ref[...]
        dt = jnp.exp(dt_ref[...])
        mag = jnp.exp(lr_ * dt)
        lbr = mag * jnp.cos(li_ * dt)
        lbi = mag * jnp.sin(li_ * dt)
        den = lr_ * lr_ + li_ * li_
        xr = lbr - 1.0
        cr = (xr * lr_ + lbi * li_) / den
        ci = (lbi * lr_ - xr * li_) / den
        br, bi = br_ref[...], bi_ref[...]
        dbr, dbi = dbr_ref[...], dbi_ref[...]
        gbr_ref[...] = cr[:, None, :] * dbr + ci[:, None, :] * dbi
        gbi_ref[...] = cr[:, None, :] * dbi - ci[:, None, :] * dbr
        gcr = jnp.sum(dbr * br + dbi * bi, axis=1)
        gci = jnp.sum(dbi * br - dbr * bi, axis=1)
        ilr, ili = lr_ / den, li_ / den
        tr, ti = _cmul(gcr, gci, ilr, ili)
        glbr = dar_ref[...] + tr
        glbi = dai_ref[...] + ti
        dr_, di_ = _cmul(tr, ti, cr, -ci)
        gwr, gwi = _cmul(glbr, glbi, lbr, -lbi)
        glr_ref[...] = gwr * dt - dr_
        gli_ref[...] = gwi * dt - di_
        gdt_ref[...] = jnp.sum(gwr * lr_ + gwi * li_, axis=-1, keepdims=True) * dt

    n = 2 * S5_GROUPS
    return pl.pallas_call(
        body, out_shape=[SDS((n, S5_STATE), F32)] * 2 + [SDS((n, 1), F32)] + [SDS((n, S5_GROUP, S5_STATE), F32)] * 2,
        name="s5_prep_bwd", compiler_params=_params())(lr, li, logdt, bre, bim, dar, dai, dbbr, dbbi)


def _scan_local(xr_ref, xi_ref, ar8, ai8, reverse):
    def step(i, carry):
        sr, si = carry
        idx = (SCAN_T - 1 - i) if reverse else i
        rows = pl.ds(pl.multiple_of(idx * SCAN_BLOCKS, SCAN_BLOCKS), SCAN_BLOCKS)
        nr = ar8 * sr - ai8 * si + xr_ref[rows, :]
        ni = ar8 * si + ai8 * sr + xi_ref[rows, :]
        xr_ref[rows, :] = nr
        xi_ref[rows, :] = ni
        return nr, ni

    z = jnp.zeros(ar8.shape, F32)
    return lax.fori_loop(0, SCAN_T, step, (z, z))


def _scan_carries(er, ei, atr, ati, reverse):
    row = lax.broadcasted_iota(jnp.int32, er.shape, 0)
    cr = jnp.zeros((1, er.shape[1]), F32)
    ci = cr
    outr = jnp.zeros(er.shape, F32)
    outi = outr
    order = range(SCAN_BLOCKS - 1, -1, -1) if reverse else range(SCAN_BLOCKS)
    for b in order:
        outr = jnp.where(row == b, cr, outr)
        outi = jnp.where(row == b, ci, outi)
        nr, ni = _cmul(atr, ati, cr, ci)
        cr, ci = nr + er[b:b + 1, :], ni + ei[b:b + 1, :]
    return outr, outi


def _scan_fixup(xr_ref, xi_ref, cr8, ci8, ar8, ai8, reverse, pair=None):
    tile = lambda idx: pl.ds(pl.multiple_of(idx * SCAN_BLOCKS, SCAN_BLOCKS), SCAN_BLOCKS)

    def fix(idx, pr, pi):
        fr, fi = _cmul(pr, pi, cr8, ci8)
        nr, ni = xr_ref[tile(idx), :] + fr, xi_ref[tile(idx), :] + fi
        xr_ref[tile(idx), :] = nr
        xi_ref[tile(idx), :] = ni
        return nr, ni

    if pair is None:
        def step(i, carry):
            pr, pi = carry
            fix((SCAN_T - 1 - i) if reverse else i, pr, pi)
            return _cmul(pr, pi, ar8, ai8)

        lax.fori_loop(0, SCAN_T, step, (ar8, ai8), unroll=2)
        return None

    sr_ref, si_ref = pair
    earlier = -1 if reverse else 1

    def step(i, carry):
        pr, pi, accr, acci = carry
        idx = (SCAN_T - 1 - i) if reverse else i
        nr, ni = fix(idx, pr, pi)
        qr, qi = _cmul(nr, ni, sr_ref[tile(idx + earlier), :], -si_ref[tile(idx + earlier), :])
        pr, pi = _cmul(pr, pi, ar8, ai8)
        return pr, pi, accr + qr, acci + qi

    z = jnp.zeros(ar8.shape, F32)
    pr, pi, accr, acci = lax.fori_loop(0, SCAN_T - 1, step, (ar8, ai8, z, z))
    edge, src, shift, empty = (0, SCAN_T - 1, 1, 0) if reverse else (SCAN_T - 1, 0, SCAN_BLOCKS - 1, SCAN_BLOCKS - 1)
    nr, ni = fix(edge, pr, pi)
    row = lax.broadcasted_iota(jnp.int32, ar8.shape, 0)
    spr = jnp.where(row == empty, 0.0, pltpu.roll(sr_ref[tile(src), :], shift, 0))
    spi = jnp.where(row == empty, 0.0, pltpu.roll(si_ref[tile(src), :], shift, 0))
    qr, qi = _cmul(nr, ni, spr, -spi)
    return jnp.sum(accr + qr, axis=0, keepdims=True), jnp.sum(acci + qi, axis=0, keepdims=True)


def _scan(xr_ref, xi_ref, ar, ai, reverse, pair=None):
    n = ar.shape[1]
    ar8 = jnp.broadcast_to(ar, (SCAN_BLOCKS, n))
    ai8 = jnp.broadcast_to(ai, (SCAN_BLOCKS, n))
    er, ei = _scan_local(xr_ref, xi_ref, ar8, ai8, reverse)
    atr, ati = _cpow(ar, ai, SCAN_T)
    cr8, ci8 = _scan_carries(er, ei, atr, ati, reverse)
    return _scan_fixup(xr_ref, xi_ref, cr8, ci8, ar8, ai8, reverse, pair)


def _s5_specs():
    chan = pl.BlockSpec((SEQ, CH_W), lambda c, d: (0, c))
    chan2 = pl.BlockSpec((None, SEQ, CH_W), lambda c, d: (d, 0, c))
    state = pl.BlockSpec((None, SEQ, ST_W), lambda c, d: (d, 0, c))
    bmat = pl.BlockSpec((None, None, CH_W, ST_W), lambda c, d: (d, c, 0, 0))
    cmat = pl.BlockSpec((None, None, ST_W, CH_W), lambda c, d: (d, c, 0, 0))
    avec = pl.BlockSpec((None, None, 1, ST_W), lambda c, d: (d, c, 0, 0))
    return chan, chan2, state, bmat, cmat, avec


def _scan_by_direction(xr_ref, xi_ref, ar, ai, d, adjoint, pair=None, da_out=None):
    for direction in range(2):
        @pl.when(d == direction)
        def _(direction=direction):
            res = _scan(xr_ref, xi_ref, ar, ai, adjoint != (direction == 1), pair)
            if pair is not None:
                da_out[0][...], da_out[1][...] = res


def _s5_scan_fwd(u, bre, bim, are, aim, cre, cim):
    def body(u_ref, bre_ref, bim_ref, are_ref, aim_ref, cre_ref, cim_ref, sr_ref, si_ref, y_ref):
        ub = u_ref[...].astype(BF16)
        sr_ref[...] = _dot(ub, bre_ref[...])
        si_ref[...] = _dot(ub, bim_ref[...])
        _scan_by_direction(sr_ref, si_ref, are_ref[...], aim_ref[...], pl.program_id(1), adjoint=False)
        y_ref[...] = _dot(sr_ref[...].astype(BF16), cre_ref[...]) - _dot(si_ref[...].astype(BF16), cim_ref[...])

    chan, chan2, state, bmat, cmat, avec = _s5_specs()
    return pl.pallas_call(
        body, grid=(S5_CHUNKS, 2), in_specs=[chan, bmat, bmat, avec, avec, cmat, cmat], out_specs=[state, state, chan2],
        out_shape=[SDS((2, SEQ, S5_GROUPS * S5_STATE), F32)] * 2 + [SDS((2, SEQ, S5_WIDTH), F32)],
        name="s5_scan_fwd", compiler_params=_params(("parallel", "parallel")))(u, bre, bim, are, aim, cre, cim)


def _diag_out(out_ref, full):
    for g in range(8):
        out_ref[g] = full[g * S5_GROUP:(g + 1) * S5_GROUP, g * S5_STATE:(g + 1) * S5_STATE]


def _s5_scan_bwd(dy, du_skip, u, sr, si, bre, bim, are, aim, cre, cim):
    def body(dy_ref, dus_ref, u_ref, sr_ref, si_ref, bre_ref, bim_ref, are_ref, aim_ref, cre_ref, cim_ref,
             du_ref, dbr_ref, dbi_ref, dcr_ref, dci_ref, dar_ref, dai_ref, gr_ref, gi_ref):
        d = pl.program_id(1)
        dyb = dy_ref[...].astype(BF16)
        gr_ref[...] = _dot_nt(dyb, cre_ref[...])
        gi_ref[...] = -_dot_nt(dyb, cim_ref[...])
        _diag_out(dcr_ref, _dot_tn(dyb, sr_ref[...].astype(BF16)))
        _diag_out(dci_ref, -_dot_tn(dyb, si_ref[...].astype(BF16)))
        _scan_by_direction(gr_ref, gi_ref, are_ref[...], -aim_ref[...], d, adjoint=True, pair=(sr_ref, si_ref),
                           da_out=(dar_ref, dai_ref))

        @pl.when(d == 0)
        def _():
            du_ref[...] = dus_ref[...]

        grb = gr_ref[...].astype(BF16)
        gib = gi_ref[...].astype(BF16)
        du_ref[...] += _dot_nt(grb, bre_ref[...]) + _dot_nt(gib, bim_ref[...])
        ub = u_ref[...].astype(BF16)
        _diag_out(dbr_ref, _dot_tn(ub, grb))
        _diag_out(dbi_ref, _dot_tn(ub, gib))

    chan, _, state, bmat, cmat, avec = _s5_specs()
    diag = pl.BlockSpec((None, None, 8, S5_GROUP, S5_STATE), lambda c, d: (d, c, 0, 0, 0))
    return pl.pallas_call(
        body, grid=(S5_CHUNKS, 2), in_specs=[chan, chan, chan, state, state, bmat, bmat, avec, avec, cmat, cmat],
        out_specs=[chan, diag, diag, diag, diag, avec, avec],
        out_shape=[SDS((SEQ, S5_WIDTH), F32)] + [SDS((2, S5_CHUNKS, 8, S5_GROUP, S5_STATE), F32)] * 4
                  + [SDS((2, S5_CHUNKS, 1, ST_W), F32)] * 2,
        scratch_shapes=[pltpu.VMEM((SEQ, ST_W), F32), pltpu.VMEM((SEQ, ST_W), F32)],
        name="s5_scan_bwd", compiler_params=_params(("parallel", "arbitrary")))(dy, du_skip, u, sr, si, bre, bim, are, aim, cre, cim)


_GELU_K = math.sqrt(2.0 / math.pi)
_GELU_C = 0.044715


def _gelu(x):
    t = jnp.tanh(_GELU_K * (x + _GELU_C * x * x * x))
    return 0.5 * x * (1.0 + t), t


def _s5_glu_fwd(u, y2, dskip, wglu, bglu):
    def body(u_ref, y0_ref, y1_ref, d_ref, w_ref, b_ref, o_ref, yp_ref):
        ypre = u_ref[...] * d_ref[...] + y0_ref[...] + y1_ref[...]
        yp_ref[...] = ypre
        y, _ = _gelu(ypre)
        z = _dot(y.astype(BF16), w_ref[...]) + b_ref[...]
        o_ref[...] = y * jax.nn.sigmoid(z)

    row = _row_spec(S5_WIDTH)
    vec = _fix_spec((1, S5_WIDTH))
    dir0 = pl.BlockSpec((None, ROW_TILE, S5_WIDTH), lambda i: (0, i, 0))
    dir1 = pl.BlockSpec((None, ROW_TILE, S5_WIDTH), lambda i: (1, i, 0))
    return pl.pallas_call(
        body, grid=(N_ROW_TILES,), in_specs=[row, dir0, dir1, vec, _fix_spec((S5_WIDTH, S5_WIDTH)), vec],
        out_specs=[row, row], out_shape=[SDS((SEQ, S5_WIDTH), F32)] * 2, name="s5_glu_fwd",
        compiler_params=_params(("parallel",)))(u, y2, y2, dskip, wglu, bglu)


def _s5_glu_bwd(do, ypre, u, dskip, wglu, bglu):
    def body(do_ref, yp_ref, u_ref, d_ref, w_ref, b_ref, dyp_ref, du_ref, dw_ref, db_ref, dd_ref):
        i = pl.program_id(0)
        ypre = yp_ref[...]
        y, t = _gelu(ypre)
        yb = y.astype(BF16)
        sg = jax.nn.sigmoid(_dot(yb, w_ref[...]) + b_ref[...])
        dov = do_ref[...]
        dz = dov * y * sg * (1.0 - sg)
        dzb = dz.astype(BF16)
        dy = dov * sg + _dot_nt(dzb, w_ref[...])
        dgelu = 0.5 * (1.0 + t) + 0.5 * ypre * (1.0 - t * t) * _GELU_K * (1.0 + 3.0 * _GELU_C * ypre * ypre)
        dyp = dy * dgelu
        dyp_ref[...] = dyp
        uv = u_ref[...]
        du_ref[...] = dyp * d_ref[...]

        @pl.when(i == 0)
        def _():
            dw_ref[...] = jnp.zeros_like(dw_ref)
            db_ref[...] = jnp.zeros_like(db_ref)
            dd_ref[...] = jnp.zeros_like(dd_ref)

        dw_ref[...] += _dot_tn(yb, dzb)
        db_ref[...] += jnp.sum(dz, axis=0, keepdims=True)
        dd_ref[...] += jnp.sum(dyp * uv, axis=0, keepdims=True)

    row = _row_spec(S5_WIDTH)
    vec = _fix_spec((1, S5_WIDTH))
    mat = _fix_spec((S5_WIDTH, S5_WIDTH))
    return pl.pallas_call(
        body, grid=(N_ROW_TILES,), in_specs=[row, row, row, vec, mat, vec], out_specs=[row, row, mat, vec, vec],
        out_shape=[SDS((SEQ, S5_WIDTH), F32)] * 2 + [SDS((S5_WIDTH, S5_WIDTH), F32), SDS((1, S5_WIDTH), F32), SDS((1, S5_WIDTH), F32)],
        name="s5_glu_bwd", compiler_params=_params(("arbitrary",)))(do, ypre, u, dskip, wglu, bglu)


def _heads_side_by_side(o_ref):
    return jnp.concatenate([o_ref[h] for h in range(HEADS)], axis=-1)


def _mix_out_fwd(ona, os5, g_na, g_s5, wout):
    def body(a_ref, s_ref, ga_ref, gs_ref, w_ref, o_ref):
        av, sv = _heads_side_by_side(a_ref), s_ref[...]
        ca = (av * _rstd(av) * ga_ref[...]).astype(BF16)
        cs = (sv * _rstd(sv) * gs_ref[...]).astype(BF16)
        o_ref[...] = _dot(ca, w_ref[0:NA_WIDTH, :]) + _dot(cs, w_ref[NA_WIDTH:, :])

    row = _row_spec(NA_WIDTH)
    vec = _fix_spec((1, NA_WIDTH))
    heads = pl.BlockSpec((HEADS, ROW_TILE, HEAD_DIM), lambda i: (0, i, 0))
    return pl.pallas_call(
        body, grid=(N_ROW_TILES,), in_specs=[heads, row, vec, vec, _fix_spec((D_MODEL, D_MODEL))],
        out_specs=_row_spec(D_MODEL), out_shape=SDS((SEQ, D_MODEL), F32), name="mix_out_fwd",
        compiler_params=_params(("parallel",)))(ona, os5, g_na, g_s5, wout)


def _mix_out_bwd(dmix, ona, os5, g_na, g_s5, wout):
    def body(dm_ref, a_ref, s_ref, ga_ref, gs_ref, w_ref, da_ref, ds_ref, dw_ref, dga_ref, dgs_ref):
        i = pl.program_id(0)
        dm = dm_ref[...]
        av, sv = _heads_side_by_side(a_ref), s_ref[...]
        ra, rs = _rstd(av), _rstd(sv)
        ga, gs = ga_ref[...], gs_ref[...]
        ca = (av * ra * ga).astype(BF16)
        cs = (sv * rs * gs).astype(BF16)
        dca = _dot_nt(dm, w_ref[0:NA_WIDTH, :])
        dcs = _dot_nt(dm, w_ref[NA_WIDTH:, :])
        da, dga = _rms_bwd(av, ra, ga, dca)
        ds, dgs = _rms_bwd(sv, rs, gs, dcs)
        for h in range(HEADS):
            da_ref[h] = da[:, h * HEAD_DIM:(h + 1) * HEAD_DIM]
        ds_ref[...] = ds

        @pl.when(i == 0)
        def _():
            dw_ref[...] = jnp.zeros_like(dw_ref)
            dga_ref[...] = jnp.zeros_like(dga_ref)
            dgs_ref[...] = jnp.zeros_like(dgs_ref)

        dw_ref[0:NA_WIDTH, :] += _dot_tn(ca, dm)
        dw_ref[NA_WIDTH:, :] += _dot_tn(cs, dm)
        dga_ref[...] += jnp.sum(dga, axis=0, keepdims=True)
        dgs_ref[...] += jnp.sum(dgs, axis=0, keepdims=True)

    row = _row_spec(NA_WIDTH)
    vec = _fix_spec((1, NA_WIDTH))
    mat = _fix_spec((D_MODEL, D_MODEL))
    heads = pl.BlockSpec((HEADS, ROW_TILE, HEAD_DIM), lambda i: (0, i, 0))
    return pl.pallas_call(
        body, grid=(N_ROW_TILES,), in_specs=[_row_spec(D_MODEL), heads, row, vec, vec, mat],
        out_specs=[heads, row, mat, vec, vec],
        out_shape=[SDS((HEADS, SEQ, HEAD_DIM), F32), SDS((SEQ, NA_WIDTH), F32), SDS((D_MODEL, D_MODEL), F32),
                   SDS((1, NA_WIDTH), F32), SDS((1, NA_WIDTH), F32)],
        name="mix_out_bwd", compiler_params=_params(("arbitrary",)))(dmix, ona, os5, g_na, g_s5, wout)


def _me():
    x, y, c = lax.axis_index("x"), lax.axis_index("y"), lax.axis_index("c")
    return x, y, c, 4 * x + 2 * y + c


def _peer(k):
    x, y, c, _ = _me()
    px = 1 - x if (k >> 2) & 1 else x
    py = 1 - y if (k >> 1) & 1 else y
    pc = 1 - c if k & 1 else c
    return (px, py, pc), 4 * px + 2 * py + pc


ALL_PEERS = (1, 2, 3, 4, 5, 6, 7)
CHIP_PEERS = (2, 4, 6)
SIBLING = 1


def _slot8(pos):
    return 4 * pos[0] + 2 * pos[1] + pos[2]


def _slot4(pos):
    return 2 * pos[0] + pos[1]


_HBM = pl.BlockSpec(memory_space=pltpu.HBM)
_SEM = pl.BlockSpec(memory_space=pltpu.SEMAPHORE)
_EFFECT = pltpu.SideEffectType.DATAFLOW_SIDE_EFFECTING


def _exchange_start(arrays, lands, gather, name, peers=ALL_PEERS, slot=_slot8, own=True):
    n = len(arrays)

    def body(*refs):
        ins, lnd = refs[:n], refs[n:2 * n]
        send_sems, recv_sems = refs[2 * n], refs[2 * n + 1]
        token = refs[-1]
        me = slot(_me()[:3])
        for i, k in enumerate(peers):
            peer, _ = _peer(k)
            for a in range(n):
                src = ins[a] if gather else ins[a].at[slot(peer)]
                s = a * len(peers) + i
                pltpu.make_async_remote_copy(src_ref=src, dst_ref=lnd[a].at[me], send_sem=send_sems.at[s],
                                             recv_sem=recv_sems.at[s], device_id=peer, device_id_type=MESH).start()
        if own:
            for a in range(n):
                pltpu.make_async_copy(ins[a] if gather else ins[a].at[me], lnd[a].at[me], recv_sems.at[n * len(peers) + a]).start()
        token[...] = jnp.zeros_like(token)

    sems = pltpu.SemaphoreType.DMA((n * (len(peers) + int(own)),))
    out = pl.pallas_call(
        body, name=name, in_specs=[_HBM] * (2 * n),
        out_shape=(sems, sems) + tuple(pltpu.HBM(a.shape, a.dtype) for a in list(arrays) + list(lands)) + (SDS((8, 128), F32),),
        out_specs=(_SEM, _SEM) + (_HBM,) * (2 * n) + (pl.BlockSpec(memory_space=pltpu.VMEM),),
        input_output_aliases={i: 2 + i for i in range(2 * n)},
        compiler_params=pltpu.CompilerParams(has_side_effects=_EFFECT),
    )(*[pltpu.with_memory_space_constraint(a, pltpu.HBM) for a in list(arrays) + list(lands)])
    return out[0], out[1], list(out[2:2 + n]), list(out[2 + n:2 + 2 * n]), out[-1]


def _exchange_wait(send_sems, recv_sems, arrays, lands, after, gather, name, peers=ALL_PEERS, slot=_slot8, own=True):
    n = len(arrays)

    def body(*refs):
        ins, lnd = refs[:n], refs[n:2 * n]
        send_sems, recv_sems = refs[2 * n], refs[2 * n + 1]
        if own:
            me = slot(_me()[:3])
            for a in range(n):
                pltpu.make_async_copy(ins[a] if gather else ins[a].at[me], lnd[a].at[me], recv_sems.at[n * len(peers) + a]).wait()
        for i, k in enumerate(peers):
            peer, _ = _peer(k)
            for a in range(n):
                src = ins[a] if gather else ins[a].at[slot(peer)]
                s = a * len(peers) + i
                cp = pltpu.make_async_remote_copy(src_ref=src, dst_ref=lnd[a].at[slot(peer)], send_sem=send_sems.at[s],
                                                  recv_sem=recv_sems.at[s], device_id=peer, device_id_type=MESH)
                cp.wait_send()
                cp.wait_recv()

        refs[-1][...] = jnp.zeros_like(refs[-1])

    after = list(after) if isinstance(after, (list, tuple)) else [after]
    out = pl.pallas_call(
        body, name=name, in_specs=[_HBM] * (2 * n) + [_SEM, _SEM] + [pl.BlockSpec(memory_space=pl.ANY)] * len(after),
        out_shape=tuple(pltpu.HBM(a.shape, a.dtype) for a in list(arrays) + list(lands)) + (SDS((8, 128), F32),),
        out_specs=(_HBM,) * (2 * n) + (pl.BlockSpec(memory_space=pltpu.VMEM),), input_output_aliases={i: i for i in range(2 * n)},
        compiler_params=pltpu.CompilerParams(has_side_effects=_EFFECT),
    )(*arrays, *lands, send_sems, recv_sems, *after)
    return list(out[n:2 * n]), out[-1]


def _forward_sibling(lands, name):
    n = len(lands)

    def body(*refs):
        outs = refs[n:2 * n]
        send_sems, recv_sems = refs[2 * n:]
        x, y, c, _ = _me()
        sends = []
        for i, k in enumerate(CHIP_PEERS):
            peer, _ = _peer(k)
            for a in range(n):
                rows = outs[a].at[_slot8(peer)]
                cp = pltpu.make_async_remote_copy(src_ref=rows, dst_ref=rows, send_sem=send_sems.at[a, i], recv_sem=recv_sems.at[a, i],
                                                  device_id=(x, y, 1 - c), device_id_type=MESH)
                cp.start()
                sends.append(cp)
        for i, k in enumerate(CHIP_PEERS):
            (px, py, pc), _ = _peer(k)
            for a in range(n):
                rows = outs[a].at[_slot8((px, py, 1 - pc))]
                pltpu.make_async_remote_copy(src_ref=rows, dst_ref=rows, send_sem=send_sems.at[a, i], recv_sem=recv_sems.at[a, i],
                                             device_id=(x, y, 1 - c), device_id_type=MESH).wait_recv()
        for cp in sends:
            cp.wait_send()

    return pl.pallas_call(
        body, in_specs=[_HBM] * n, out_specs=[_HBM] * n, out_shape=[SDS(a.shape, a.dtype) for a in lands],
        input_output_aliases={i: i for i in range(n)},
        scratch_shapes=[pltpu.SemaphoreType.DMA((n, len(CHIP_PEERS))), pltpu.SemaphoreType.DMA((n, len(CHIP_PEERS)))],
        name=name)(*lands)


def _swap_sibling(arrays, name, after=()):
    n, n_after = len(arrays), len(after)
    chips = N_DEV // 2

    def body(*refs):
        ins, outs = refs[:n], refs[n + n_after:2 * n + n_after]
        send_sems, recv_sems = refs[2 * n + n_after:]
        x, y, c, _ = _me()
        sends = []
        for q in range(chips):
            for a in range(n):
                cp = pltpu.make_async_remote_copy(src_ref=ins[a].at[q, 1 - c], dst_ref=outs[a].at[q], send_sem=send_sems.at[a, q],
                                                  recv_sem=recv_sems.at[a, q], device_id=(x, y, 1 - c), device_id_type=MESH)
                cp.start()
                sends.append(cp)
        for cp in sends:
            cp.wait_recv()
        for cp in sends:
            cp.wait_send()

    return pl.pallas_call(
        body, in_specs=[_HBM] * n + [pl.BlockSpec(memory_space=pl.ANY)] * n_after, out_specs=[_HBM] * n,
        out_shape=[SDS((chips,) + a.shape[2:], a.dtype) for a in arrays],
        scratch_shapes=[pltpu.SemaphoreType.DMA((n, chips)), pltpu.SemaphoreType.DMA((n, chips))], name=name)(*arrays, *after)


def _sum_pairs(mine, theirs, name):
    n = len(mine)
    chips = mine[0].shape[0]
    c = lax.axis_index("c")

    def body(c_ref, *refs):
        for a in range(n):
            refs[2 * n + a][...] = (refs[a][...].astype(F32) + refs[n + a][...].astype(F32)).astype(refs[2 * n + a].dtype)

    def pair(a):
        return pl.BlockSpec((None, None) + a.shape[2:], lambda q, c_ref: (q, c_ref[0], 0, 0))

    def single(a):
        return pl.BlockSpec((None,) + a.shape[2:], lambda q, c_ref: (q, 0, 0))

    return pl.pallas_call(
        body, grid_spec=pltpu.PrefetchScalarGridSpec(
            num_scalar_prefetch=1, grid=(chips,), in_specs=[pair(a) for a in mine] + [single(a) for a in mine],
            out_specs=[single(a) for a in mine]),
        out_shape=[SDS((chips,) + a.shape[2:], a.dtype) for a in mine], name=name,
        compiler_params=_params(("parallel",)))(c.reshape(1).astype(jnp.int32), *mine, *theirs)


def _adamw_math(w, g, m, v):
    m = ADAM_B1 * m + (1.0 - ADAM_B1) * g
    v = ADAM_B2 * v + (1.0 - ADAM_B2) * (g * g)
    m_hat = m / (1.0 - ADAM_B1 ** ADAM_STEP)
    v_hat = v / (1.0 - ADAM_B2 ** ADAM_STEP)
    delta = -ADAM_LR * (m_hat / (jnp.sqrt(v_hat) + ADAM_EPS) + ADAM_WD * w)
    return delta, m, v


def _adamw(w, m, v, pieces, name):
    rows, cols = w.shape[-2:]
    lead = w.ndim - 2
    tile = rows
    for cand in (256, 176, 128, 64, 16):
        if rows > cand and rows % cand == 0:
            tile = cand
            break

    def body(w_ref, m_ref, v_ref, p_ref, g_ref, d_ref, mo_ref, vo_ref):
        g = _sum_pieces(p_ref)
        g_ref[...] = g
        d_ref[...], mo_ref[...], vo_ref[...] = _adamw_math(w_ref[...], g, m_ref[...], v_ref[...])

    blk = pl.BlockSpec((None,) * lead + (tile, cols), lambda i: (0,) * lead + (i, 0))
    return pl.pallas_call(
        body, grid=(rows // tile,), in_specs=[blk, blk, blk, pl.BlockSpec((pieces.shape[0], tile, cols), lambda i: (0, i, 0))],
        out_specs=[blk] * 4, out_shape=[SDS(w.shape, F32)] * 4, name=name,
        compiler_params=_params(("parallel",)))(w, m, v, pieces)


def _sum_pieces(p_ref):
    g = p_ref[0].astype(F32)
    for p in range(1, p_ref.shape[0]):
        g = g + p_ref[p].astype(F32)
    return g


def _adamw_s5_mat(w, m, v, g, name):
    _, ndir, groups, b, c = w.shape
    per_dir = groups // 8

    def body(w_ref, m_ref, v_ref, g_ref, d_ref, mo_ref, vo_ref):
        d_ref[...], mo_ref[...], vo_ref[...] = _adamw_math(w_ref[...], g_ref[...], m_ref[...], v_ref[...])

    blk = pl.BlockSpec((None, None, 8, b, c), lambda i: (0, i // per_dir, i % per_dir, 0, 0))
    return pl.pallas_call(
        body, grid=(ndir * per_dir,), in_specs=[blk] * 4, out_specs=[blk] * 3, out_shape=[SDS(w.shape, F32)] * 3, name=name,
        compiler_params=_params(("parallel",)))(w, m, v, g)


VEC_ROWS = ['ffn1_pre_g', 'ffn1_post_g', 'mix_pre_g', 'mix_post_g', 'ffn2_pre_g', 'ffn2_post_g', 'final_g',
            ('na_out_g', 's5_out_g'), ('s5_d', 's5_b_glu')]
VEC_NAMES = [n for row in VEC_ROWS for n in ((row,) if isinstance(row, str) else row)]
VEC_PACK_ROWS = 16
LOSS_ROW = len(VEC_ROWS)


def _pack_vectors(grads, loss8):
    def body(*refs):
        o_ref = refs[-1]
        o_ref[...] = jnp.zeros_like(o_ref)
        o_ref[LOSS_ROW:LOSS_ROW + 1, 0:128] = refs[-2][0:1, :]
        k = 0
        for i, row in enumerate(VEC_ROWS):
            if isinstance(row, str):
                o_ref[i:i + 1, :] = refs[k][...]
                k += 1
            else:
                o_ref[i:i + 1, 0:NA_WIDTH] = refs[k][...]
                o_ref[i:i + 1, NA_WIDTH:] = refs[k + 1][...]
                k += 2

    return pl.pallas_call(body, out_shape=SDS((VEC_PACK_ROWS, D_MODEL), F32), name="pack_vectors",
                          compiler_params=_params())(*[grads[n] for n in VEC_NAMES], loss8)


def _sum8(pieces, name):
    def body(p_ref, o_ref):
        o_ref[...] = _sum_pieces(p_ref)

    return pl.pallas_call(body, out_shape=SDS(pieces.shape[1:], F32), name=name, compiler_params=_params())(pieces)


def _adamw_small(packed8, vec_wmv, others):
    n_vec, n_oth = len(VEC_NAMES), len(others)

    def body(*refs):
        p_ref = refs[0]
        ins = refs[1:1 + 3 * n_vec + 4 * n_oth]
        outs = refs[1 + 3 * n_vec + 4 * n_oth:]
        gsum = _sum_pieces(p_ref)
        outs[-1][...] = gsum[LOSS_ROW:LOSS_ROW + 1, 0:128]
        k = 0
        for i, row in enumerate(VEC_ROWS):
            parts = [(row, gsum[i:i + 1, :])] if isinstance(row, str) else \
                [(row[0], gsum[i:i + 1, 0:NA_WIDTH]), (row[1], gsum[i:i + 1, NA_WIDTH:])]
            for _, g in parts:
                w_ref, m_ref, v_ref = ins[3 * k:3 * k + 3]
                outs[4 * k][...] = g
                outs[4 * k + 1][...], outs[4 * k + 2][...], outs[4 * k + 3][...] = _adamw_math(w_ref[...], g, m_ref[...], v_ref[...])
                k += 1
        for j in range(n_oth):
            w_ref, m_ref, v_ref, g_ref = ins[3 * n_vec + 4 * j:3 * n_vec + 4 * j + 4]
            g = _sum_pieces(g_ref)
            g = g[tuple(slice(0, s) for s in w_ref.shape[1:])].reshape(w_ref.shape)
            o = outs[4 * (n_vec + j):4 * (n_vec + j) + 4]
            o[0][...] = g
            o[1][...], o[2][...], o[3][...] = _adamw_math(w_ref[...], g, m_ref[...], v_ref[...])

    args, out_shape = [packed8], []
    for w, m, v in vec_wmv:
        args += [w, m, v]
        out_shape += [SDS(w.shape, F32)] * 4
    for w, m, v, g in others:
        args += [w, m, v, g]
        out_shape += [SDS(w.shape, F32)] * 4
    out_shape += [SDS((1, 128), F32)]
    return pl.pallas_call(body, out_shape=out_shape, name="adamw_small", compiler_params=_params())(*args)


def _perm_rows(x):
    return x.reshape(SCAN_BLOCKS, SCAN_T, x.shape[-1]).transpose(1, 0, 2).reshape(SEQ, x.shape[-1])


def _unperm_rows(x):
    return x.reshape(SCAN_T, SCAN_BLOCKS, x.shape[-1]).transpose(1, 0, 2).reshape(SEQ, x.shape[-1])


def _block_diag(x):
    eye = np.eye(8, dtype=bool)[None, None, :, None, :, None]
    full = jnp.where(eye, x[:, :, :, :, None, :], 0.0)
    return full.reshape(2, S5_CHUNKS, 8 * x.shape[3], 8 * x.shape[4])


STORED_SWAPPED = {"ffn1_w_gate": (1, 2), "ffn1_w_up": (1, 2), "ffn2_w_gate": (1, 2), "ffn2_w_up": (1, 2),
                  "s5_b_re": (3, 4), "s5_b_im": (3, 4)}


def _stored(name, x):
    return jnp.swapaxes(x, *STORED_SWAPPED[name]) if name in STORED_SWAPPED else x


def _dep(x, token):
    return x if token is None else x + token


def _local_step(x, target, get_w, small, emit):
    bias = _rpb_expand(small["na_rpb"][0])
    lr = small["s5_lam_re"].reshape(64, S5_STATE)
    li = small["s5_lam_im"].reshape(64, S5_STATE)
    logdt = small["s5_log_dt"].reshape(64, 1)
    b_t = [_stored(n, small[n]).reshape(64, S5_GROUP, S5_STATE) for n in ("s5_b_re", "s5_b_im")]
    lbr, lbi, bbr, bbi = _s5_prep(lr, li, logdt, b_t[0], b_t[1])
    are = lbr.reshape(2, S5_CHUNKS, 1, ST_W)
    aim = lbi.reshape(2, S5_CHUNKS, 1, ST_W)
    bre = _block_diag(bbr.reshape(2, S5_CHUNKS, 8, S5_GROUP, S5_STATE)).astype(BF16)
    bim = _block_diag(bbi.reshape(2, S5_CHUNKS, 8, S5_GROUP, S5_STATE)).astype(BF16)
    c_t = [small[n].reshape(2, S5_CHUNKS, 8, S5_GROUP, S5_STATE).transpose(0, 1, 2, 4, 3) for n in ("s5_c_re", "s5_c_im")]
    cre = _block_diag(c_t[0]).astype(BF16)
    cim = _block_diag(c_t[1]).astype(BF16)
    tgt = jnp.concatenate([jnp.zeros((N_META, D_MODEL), F32), target], axis=0)

    h0, a1 = _embed_prenorm(get_w("meta", None)["meta_tokens"], x, small["ffn1_pre_g"])
    wts = dict(get_w("ffn1", [bias, are, aim, bre, bim, cre, cim, tgt, a1]))
    gate1, up1, f1 = _ffn_fwd(a1, wts["ffn1_w_gate"], wts["ffn1_w_up"], wts["ffn1_w_down"], "ffn1_fwd",
                              after=wts.get("tokens", ()))
    h1, a2 = _post_pre(f1, h0, small["ffn1_post_g"], small["mix_pre_g"], 0.5, "post_pre1")
    wts.update(get_w("w_in", a2))
    qkv = _proj_heads(a2, wts["w_in"])
    u = _proj_u(a2, wts["w_in"])
    ona = _na_fwd(qkv, bias)
    u_p = _perm_rows(u)
    sr, si, y2 = _s5_scan_fwd(u_p, bre, bim, are, aim, cre, cim)
    wts.update(get_w("mix", y2))
    os5_p, ypre_p = _s5_glu_fwd(u_p, y2, small["s5_d"], wts["s5_w_glu"], small["s5_b_glu"])
    os5 = _unperm_rows(os5_p)

    mix = _mix_out_fwd(ona, os5, small["na_out_g"], small["s5_out_g"], wts["w_out"])
    h2, a3 = _post_pre(mix, h1, small["mix_post_g"], small["ffn2_pre_g"], 1.0, "post_pre2")
    wts.update(get_w("ffn2", a3))
    gate2, up2, f2 = _ffn_fwd(a3, wts["ffn2_w_gate"], wts["ffn2_w_up"], wts["ffn2_w_down"], "ffn2_fwd")
    loss8, dh3, df2, g_final, g_ffn2_post = _final_loss(f2, h2, small["ffn2_post_g"], small["final_g"], tgt)

    da3, dwg2, dwu2, dwd2 = _ffn_bwd(df2, a3, gate2, up2, wts["ffn2_w_gate"], wts["ffn2_w_up"], wts["ffn2_w_down"], "ffn2_bwd")
    tok = emit("ffn2", {"ffn2_w_gate": dwg2, "ffn2_w_up": dwu2, "ffn2_w_down": dwd2})
    dh2, dmix, g_ffn2_pre, g_mix_post = _bwd_pre_post(da3, h2, _dep(small["ffn2_pre_g"], tok), dh3, mix, small["mix_post_g"], 1.0,
                                                      "bwd_pre_post2")
    dona, dos5, dwout, g_na_out, g_s5_out = _mix_out_bwd(dmix, ona, os5, small["na_out_g"], small["s5_out_g"], wts["w_out"])

    dypre_p, du_skip_p, dwglu, g_b_glu, g_s5_d = _s5_glu_bwd(_perm_rows(dos5), ypre_p, u_p, small["s5_d"], wts["s5_w_glu"],
                                                             small["s5_b_glu"])
    tok = emit("mix", {"s5_w_glu": dwglu.reshape(N_DEV, S5_WIDTH // N_DEV, S5_WIDTH).astype(BF16),
                       "w_out": dwout.reshape(N_DEV, D_MODEL // N_DEV, D_MODEL).astype(BF16)})
    du_p, dbr, dbi, dcr, dci, dar, dai = _s5_scan_bwd(dypre_p, du_skip_p, u_p, sr, si, bre, bim, _dep(are, tok), aim, cre, cim)
    du = _unperm_rows(du_p)
    per_group = (2 * S5_GROUPS, S5_GROUP, S5_STATE)
    g_lr, g_li, g_dt, g_br, g_bi = _s5_prep_bwd(lr, li, logdt, b_t[0], b_t[1], dar.reshape(64, S5_STATE),
                                                dai.reshape(64, S5_STATE), dbr.reshape(per_group), dbi.reshape(per_group))
    g_c = [dcr.reshape(per_group), dci.reshape(per_group)]

    dq, dk, dv, dbias = _na_bwd(qkv, bias, dona)
    g_rpb = _rpb_reduce(dbias)
    dense = jnp.stack([g.reshape(2 * S5_GROUPS, S5_STATE * S5_GROUP) for g in (g_br, g_bi, *g_c)])
    tok = emit("small", {"dense": dense, "na_rpb": g_rpb,
                         "s5_lam_re": g_lr.reshape(2, S5_GROUPS, S5_STATE), "s5_lam_im": g_li.reshape(2, S5_GROUPS, S5_STATE),
                         "s5_log_dt": g_dt.reshape(2, S5_GROUPS)})
    da2, dwin = _proj_bwd(dq, dk, dv, du, a2, wts["w_in"])
    tok2 = emit("w_in", {"w_in": dwin})
    tok = tok if tok2 is None else tok + tok2
    dh1, df1, g_mix_pre, g_ffn1_post = _bwd_pre_post(da2, h1, _dep(small["mix_pre_g"], tok), dh2, f1, small["ffn1_post_g"], 0.5,
                                                     "bwd_pre_post1")
    da1, dwg1, dwu1, dwd1 = _ffn_bwd(df1, a1, gate1, up1, wts["ffn1_w_gate"], wts["ffn1_w_up"], wts["ffn1_w_down"], "ffn1_bwd")
    grad_x, grad_meta, g_ffn1_pre = _bwd_embed(da1, h0, small["ffn1_pre_g"], dh1)
    vec_g = {
        "ffn1_pre_g": g_ffn1_pre, "ffn1_post_g": g_ffn1_post, "mix_pre_g": g_mix_pre, "s5_d": g_s5_d, "s5_b_glu": g_b_glu,
        "na_out_g": g_na_out, "s5_out_g": g_s5_out, "mix_post_g": g_mix_post,
        "ffn2_pre_g": g_ffn2_pre, "ffn2_post_g": g_ffn2_post, "final_g": g_final,
    }
    emit("vec", {"packed": _pack_vectors(vec_g, loss8), "meta_tokens": grad_meta})
    emit("ffn1", {"ffn1_w_gate": dwg1, "ffn1_w_up": dwu1, "ffn1_w_down": dwd1})
    return grad_x


WEIGHT_NAMES = ['meta_tokens', 'ffn1_pre_g', 'ffn1_post_g', 'ffn1_w_gate', 'ffn1_w_up', 'ffn1_w_down', 'mix_pre_g', 'w_in',
                'na_rpb', 's5_lam_re', 's5_lam_im', 's5_log_dt', 's5_b_re', 's5_b_im', 's5_c_re', 's5_c_im', 's5_d',
                's5_w_glu', 's5_b_glu', 'na_out_g', 's5_out_g', 'w_out', 'mix_post_g', 'ffn2_pre_g', 'ffn2_post_g',
                'ffn2_w_gate', 'ffn2_w_up', 'ffn2_w_down', 'final_g']
BIG_NAMES = ['ffn1_w_gate', 'ffn1_w_up', 'ffn1_w_down', 'w_in', 's5_w_glu', 'w_out', 'ffn2_w_gate', 'ffn2_w_up', 'ffn2_w_down']
SMALL_NAMES = [n for n in WEIGHT_NAMES if n not in BIG_NAMES and n != 'meta_tokens']
WHOLE_NAMES = ['na_rpb', 's5_lam_re', 's5_lam_im', 's5_log_dt']
LEAD_NAMES = ['s5_b_re', 's5_b_im', 's5_c_re', 's5_c_im']


def kernel(x, meta_tokens, ffn1_pre_g, ffn1_post_g, ffn1_w_gate, ffn1_w_up, ffn1_w_down, mix_pre_g, w_in, na_rpb, s5_lam_re, s5_lam_im, s5_log_dt, s5_b_re, s5_b_im, s5_c_re, s5_c_im, s5_d, s5_w_glu, s5_b_glu, na_out_g, s5_out_g, w_out, mix_post_g, ffn2_pre_g, ffn2_post_g, ffn2_w_gate, ffn2_w_up, ffn2_w_down, final_g, loss_target, m_meta_tokens, m_ffn1_pre_g, m_ffn1_post_g, m_ffn1_w_gate, m_ffn1_w_up, m_ffn1_w_down, m_mix_pre_g, m_w_in, m_na_rpb, m_s5_lam_re, m_s5_lam_im, m_s5_log_dt, m_s5_b_re, m_s5_b_im, m_s5_c_re, m_s5_c_im, m_s5_d, m_s5_w_glu, m_s5_b_glu, m_na_out_g, m_s5_out_g, m_w_out, m_mix_post_g, m_ffn2_pre_g, m_ffn2_post_g, m_ffn2_w_gate, m_ffn2_w_up, m_ffn2_w_down, m_final_g, v_meta_tokens, v_ffn1_pre_g, v_ffn1_post_g, v_ffn1_w_gate, v_ffn1_w_up, v_ffn1_w_down, v_mix_pre_g, v_w_in, v_na_rpb, v_s5_lam_re, v_s5_lam_im, v_s5_log_dt, v_s5_b_re, v_s5_b_im, v_s5_c_re, v_s5_c_im, v_s5_d, v_s5_w_glu, v_s5_b_glu, v_na_out_g, v_s5_out_g, v_w_out, v_mix_post_g, v_ffn2_pre_g, v_ffn2_post_g, v_ffn2_w_gate, v_ffn2_w_up, v_ffn2_w_down, v_final_g):
    args = dict(locals())
    w = {n: args[n] for n in WEIGHT_NAMES}
    m = {n: args["m_" + n] for n in WEIGHT_NAMES}
    v = {n: args["v_" + n] for n in WEIGHT_NAMES}

    small = {n: w[n] for n in SMALL_NAMES}

    pending = {}

    def start(group, names, arrays, gather, peers=ALL_PEERS, slot=_slot8):
        n_slots = N_DEV if slot is _slot8 else N_DEV // 2
        lands = [lax.empty((n_slots,) + a.shape if gather else a.shape, a.dtype) for a in arrays]
        send_sems, recv_sems, arrays, lands, token = _exchange_start(arrays, lands, gather, "start_" + group, peers, slot)
        pending[group] = (names, send_sems, recv_sems, arrays, lands, gather, peers, slot)
        return token

    def finish(group, after):
        names, send_sems, recv_sems, arrays, lands, gather, peers, slot = pending.pop(group)
        lands, token = _exchange_wait(send_sems, recv_sems, arrays, lands, after, gather, "wait_" + group, peers, slot)
        return dict(zip(names, lands)), token

    first = ["ffn1_w_gate", "ffn1_w_up", "ffn1_w_down"]
    def shard(n, token=None):
        return _dep(_stored(n, w[n])[0], None if token is None else token[0, 0]).astype(BF16)

    ffn_names = ("ffn1_w_gate", "ffn1_w_up", "ffn1_w_down", "ffn2_w_gate", "ffn2_w_up", "ffn2_w_down")
    later_groups = (("w_in", ["w_in"]), ("mix", ["s5_w_glu", "w_out"]), ("ffn2", ["ffn2_w_gate", "ffn2_w_up", "ffn2_w_down"]))
    token0 = start("meta", ["meta_tokens"], [w["meta_tokens"]], True)
    token1 = start("ffn1", first, [shard(n, token0) for n in first], True, (SIBLING,) + CHIP_PEERS)
    meta_full = finish("meta", [token1])[0]["meta_tokens"].transpose(1, 0, 2).reshape(N_META, D_MODEL)
    later_shards = {n: shard(n, token1) for _, names in later_groups for n in names}
    for n in ("na_rpb", "s5_lam_re"):
        small[n] = _dep(small[n], token1[0, 0])

    def get_w(group, after):
        if group == "meta":
            return {"meta_tokens": meta_full}
        if group == "ffn1":
            after = list(after) + list(later_shards.values())
        got, token = finish(group, after)
        if group == "ffn1":
            got = dict(zip(got, _forward_sibling(list(got.values()), "forward_ffn1")))
            got["tokens"] = [start(g, names + ["order"], [later_shards[n] for n in names] + [token], True) for g, names in later_groups]
        if group == "mix":
            got = {"s5_w_glu": got["s5_w_glu"].reshape(S5_WIDTH, S5_WIDTH), "w_out": got["w_out"].reshape(D_MODEL, D_MODEL)}
        return {n: (a.reshape(D_FF, D_MODEL) if n in ffn_names else a) for n, a in got.items()}

    tokens = {}

    def emit(group, grads):
        grads = {n: (g.reshape(N_DEV, FF_SHARD, D_MODEL) if n in ffn_names else g) for n, g in grads.items()}
        if group == "ffn1":
            mine = [g.reshape((N_DEV // 2, 2) + g.shape[1:]) for g in grads.values()]
            theirs = _swap_sibling(mine, "swap_g_ffn1", after=[tokens["vec"]])
            sums = _sum_pairs(mine, theirs, "pair_sum_g_ffn1")
            tokens[group] = start("g_ffn1", list(grads), sums, False, CHIP_PEERS, _slot4)
        else:
            tokens[group] = start("g_" + group, list(grads), list(grads.values()), group in ("small", "vec"))
        return tokens[group][0, 0]

    grad_x = _local_step(x[0], loss_target[0], get_w, small, emit)
    res = {}

    def update_shard(n, pieces):
        outs = _adamw(_stored(n, w[n]), _stored(n, m[n]), _stored(n, v[n]), pieces, "adamw_" + n)
        res[n] = [_stored(n, o) for o in outs]

    late = [grad_x, tokens["ffn1"]]
    for group in ("g_ffn2", "g_mix", "g_w_in"):
        for n, pieces in finish(group, late)[0].items():
            update_shard(n, pieces)
    g8 = finish("g_small", late)[0]
    dense = _sum8(g8["dense"], "sum_dense")
    for i, n in enumerate(LEAD_NAMES):
        g = dense[i].reshape(_stored(n, w[n]).shape)
        upd = _adamw_s5_mat(_stored(n, w[n]), _stored(n, m[n]), _stored(n, v[n]), g, "adamw_" + n)
        res[n] = [_stored(n, o) for o in [g] + list(upd)]

    done = [res[n][1] for n in ("ffn2_w_gate", "ffn2_w_up", "ffn2_w_down", "w_in", "w_out", "s5_w_glu") + tuple(LEAD_NAMES)]
    got = finish("g_vec", done)[0]
    packed8, gmeta8 = got["packed"], got["meta_tokens"]
    for n, pieces in finish("g_ffn1", packed8)[0].items():
        update_shard(n, pieces)
    _, _, _, me = _me()
    update_shard("meta_tokens", lax.dynamic_slice_in_dim(gmeta8, me * (D_MODEL // N_DEV), D_MODEL // N_DEV, axis=2))

    outs = _adamw_small(packed8, [(w[n], m[n], v[n]) for n in VEC_NAMES], [(w[n], m[n], v[n], g8[n]) for n in WHOLE_NAMES])
    for i, n in enumerate(VEC_NAMES + WHOLE_NAMES):
        res[n] = list(outs[4 * i:4 * i + 4])

    out = [outs[-1][0, 0], grad_x[None]]
    for kind in range(4):
        out += [res[n][kind] for n in WEIGHT_NAMES]
    return tuple(out)
```

```python
import math

import numpy as np
import jax
import jax.numpy as jnp
from jax import lax
from jax.experimental import pallas as pl
from jax.experimental.pallas import tpu as pltpu

F32 = jnp.float32
BF16 = jnp.bfloat16
SDS = jax.ShapeDtypeStruct

D_MODEL = 1024
N_TOK = 2048
N_META = 16
SEQ = N_TOK + N_META
ROW_TILE = 688
N_ROW_TILES = SEQ // ROW_TILE
N_DEV = 8
D_FF = 2816
FF_SHARD = D_FF // N_DEV
FF_TILE = 256
IN_SHARD = 256
NA_WIDTH = 512
S5_WIDTH = 512
HEADS = 8
HEAD_DIM = 64
GRID_W = 64
GRID_ROWS = N_TOK // GRID_W
KH = 8
KW = 16
NA_RB = 4
NA_KR = KH + NA_RB - 1
NA_BLOCKS = GRID_ROWS // NA_RB
NA_QB = NA_RB * GRID_W
NA_KB = NA_KR * GRID_W
NA_TYPES = 3
S5_GROUPS = 32
S5_GROUP = 16
S5_STATE = 64
S5_CHUNKS = 4
CH_W = S5_WIDTH // S5_CHUNKS
ST_W = S5_GROUPS * S5_STATE // S5_CHUNKS
SCAN_BLOCKS = 8
SCAN_T = SEQ // SCAN_BLOCKS
RMS_EPS = 1e-6
NEG_INF = -1e30
ATT_SCALE = HEAD_DIM ** -0.5
ADAM_LR, ADAM_B1, ADAM_B2, ADAM_EPS, ADAM_WD, ADAM_STEP = 0.001, 0.9, 0.999, 1e-08, 0.01, 10
VMEM_LIMIT = 56 * 1024 * 1024
MESH = pl.DeviceIdType.MESH


def _params(sem=None):
    return pltpu.CompilerParams(dimension_semantics=sem, vmem_limit_bytes=VMEM_LIMIT)


def _dot(a, b):
    return jnp.dot(a, b, preferred_element_type=F32)


def _dot_nt(a, b):
    return lax.dot_general(a, b, (((1,), (1,)), ((), ())), preferred_element_type=F32)


def _dot_tn(a, b):
    return lax.dot_general(a, b, (((0,), (0,)), ((), ())), preferred_element_type=F32)


def _rstd(x):
    return lax.rsqrt(jnp.mean(x * x, axis=-1, keepdims=True) + RMS_EPS)


def _rms_bwd(x, r, g, dy):
    dyg = dy * g
    xr = x * r
    dx = r * (dyg - xr * jnp.mean(dyg * xr, axis=-1, keepdims=True))
    return dx, dy * xr


def _rows(i, size=ROW_TILE):
    return pl.ds(i * size if isinstance(i, int) else pl.multiple_of(i * size, 16), size)


def _row_spec(width):
    return pl.BlockSpec((ROW_TILE, width), lambda i: (i, 0))


def _fix_spec(shape):
    return pl.BlockSpec(shape, lambda i: (0,) * len(shape))


def _split3(x):
    hi = x.astype(BF16)
    r1 = x - hi.astype(F32)
    mid = r1.astype(BF16)
    lo = (r1 - mid.astype(F32)).astype(BF16)
    return hi, mid, lo


def _embed_prenorm(meta, x, g):
    def body(m_ref, x_ref, g_ref, h_ref, a_ref):
        h_ref[0:N_META, :] = m_ref[...]
        h_ref[N_META:, :] = x_ref[...]
        for i in range(N_ROW_TILES):
            rows = slice(i * ROW_TILE, (i + 1) * ROW_TILE)
            hv = h_ref[rows, :]
            a_ref[rows, :] = (hv * _rstd(hv) * g_ref[...]).astype(BF16)

    return pl.pallas_call(
        body, out_shape=[SDS((SEQ, D_MODEL), F32), SDS((SEQ, D_MODEL), BF16)], name="embed_prenorm",
        compiler_params=_params())(meta, x, g)


def _post_pre(f, hres, g_post, g_next, scale, name):
    def body(f_ref, h_ref, gp_ref, gn_ref, ho_ref, a_ref):
        fv = f_ref[...]
        h = h_ref[...] + scale * (fv * _rstd(fv) * gp_ref[...])
        ho_ref[...] = h
        a_ref[...] = (h * _rstd(h) * gn_ref[...]).astype(BF16)

    return pl.pallas_call(
        body, grid=(N_ROW_TILES,),
        in_specs=[_row_spec(D_MODEL), _row_spec(D_MODEL), _fix_spec((1, D_MODEL)), _fix_spec((1, D_MODEL))],
        out_specs=[_row_spec(D_MODEL), _row_spec(D_MODEL)],
        out_shape=[SDS((SEQ, D_MODEL), F32), SDS((SEQ, D_MODEL), BF16)], name=name,
        compiler_params=_params(("parallel",)))(f, hres, g_post, g_next)


def _final_loss(f2, h2, g_post, g_final, target):
    def body(f_ref, h_ref, gp_ref, gf_ref, t_ref, loss_ref, dh_ref, df_ref, dgf_ref, dgp_ref):
        i = pl.program_id(0)
        fv = f_ref[...]
        r1 = _rstd(fv)
        gp = gp_ref[...]
        h3 = h_ref[...] + 0.5 * (fv * r1 * gp)
        r2 = _rstd(h3)
        gf = gf_ref[...]
        y = h3 * r2 * gf
        row = lax.broadcasted_iota(jnp.int32, (ROW_TILE, 1), 0) + i * ROW_TILE
        err = jnp.where(row >= N_META, y - t_ref[...], 0.0)
        part = 0.5 * jnp.sum(jnp.mean(err * err, axis=-1, keepdims=True))
        dy = err * (1.0 / D_MODEL)
        dh3, dgf = _rms_bwd(h3, r2, gf, dy)
        dh_ref[...] = dh3
        df, dgp = _rms_bwd(fv, r1, gp, 0.5 * dh3)
        df_ref[...] = df.astype(BF16)

        @pl.when(i == 0)
        def _():
            loss_ref[...] = jnp.zeros_like(loss_ref)
            dgf_ref[...] = jnp.zeros_like(dgf_ref)
            dgp_ref[...] = jnp.zeros_like(dgp_ref)

        loss_ref[...] += part
        dgf_ref[...] += jnp.sum(dgf, axis=0, keepdims=True)
        dgp_ref[...] += jnp.sum(dgp, axis=0, keepdims=True)

    gain = _fix_spec((1, D_MODEL))
    return pl.pallas_call(
        body, grid=(N_ROW_TILES,),
        in_specs=[_row_spec(D_MODEL), _row_spec(D_MODEL), gain, gain, _row_spec(D_MODEL)],
        out_specs=[_fix_spec((8, 128)), _row_spec(D_MODEL), _row_spec(D_MODEL), gain, gain],
        out_shape=[SDS((8, 128), F32), SDS((SEQ, D_MODEL), F32), SDS((SEQ, D_MODEL), BF16),
                   SDS((1, D_MODEL), F32), SDS((1, D_MODEL), F32)],
        name="final_loss", compiler_params=_params(("arbitrary",)))(f2, h2, g_post, g_final, target)


def _bwd_pre_post(da, h, g_pre, dh_res, fprev, g_post, scale, name):
    def body(da_ref, h_ref, gpre_ref, dhr_ref, f_ref, gpost_ref, dh_ref, df_ref, dgpre_ref, dgpost_ref):
        i = pl.program_id(0)
        hv = h_ref[...]
        dxa, dgpre = _rms_bwd(hv, _rstd(hv), gpre_ref[...], da_ref[...])
        dh = dhr_ref[...] + dxa
        dh_ref[...] = dh
        fv = f_ref[...]
        df, dgpost = _rms_bwd(fv, _rstd(fv), gpost_ref[...], scale * dh)
        df_ref[...] = df.astype(BF16)

        @pl.when(i == 0)
        def _():
            dgpre_ref[...] = jnp.zeros_like(dgpre_ref)
            dgpost_ref[...] = jnp.zeros_like(dgpost_ref)

        dgpre_ref[...] += jnp.sum(dgpre, axis=0, keepdims=True)
        dgpost_ref[...] += jnp.sum(dgpost, axis=0, keepdims=True)

    gain = _fix_spec((1, D_MODEL))
    row = _row_spec(D_MODEL)
    return pl.pallas_call(
        body, grid=(N_ROW_TILES,), in_specs=[row, row, gain, row, row, gain],
        out_specs=[row, row, gain, gain],
        out_shape=[SDS((SEQ, D_MODEL), F32), SDS((SEQ, D_MODEL), BF16), SDS((1, D_MODEL), F32), SDS((1, D_MODEL), F32)],
        name=name, compiler_params=_params(("arbitrary",)))(da, h, g_pre, dh_res, fprev, g_post)


def _bwd_embed(da, h, g_pre, dh_res):
    def body(da_ref, h_ref, gpre_ref, dhr_ref, gx_ref, gm_ref, dgpre_ref):
        total = jnp.zeros((1, D_MODEL), F32)
        for i in range(N_ROW_TILES):
            rows = slice(i * ROW_TILE, (i + 1) * ROW_TILE)
            hv = h_ref[rows, :]
            dxa, dgpre = _rms_bwd(hv, _rstd(hv), gpre_ref[...], da_ref[rows, :])
            dh = dhr_ref[rows, :] + dxa
            total = total + jnp.sum(dgpre, axis=0, keepdims=True)
            if i == 0:
                gm_ref[...] = dh[0:N_META, :]
                gx_ref[0:ROW_TILE - N_META, :] = dh[N_META:, :]
            else:
                gx_ref[i * ROW_TILE - N_META:(i + 1) * ROW_TILE - N_META, :] = dh
        dgpre_ref[...] = total

    return pl.pallas_call(
        body, out_shape=[SDS((N_TOK, D_MODEL), F32), SDS((N_META, D_MODEL), F32), SDS((1, D_MODEL), F32)],
        name="bwd_embed", compiler_params=_params())(da, h, g_pre, dh_res)


def _ffn_fwd(a, wg, wu, wd, name, after=()):
    def body(a_ref, wg_ref, wu_ref, wd_ref, *rest):
        gate_ref, up_ref, f_ref = rest[len(after):]
        j = pl.program_id(0)

        def tile(i, carry):
            rows = _rows(i)
            at = a_ref[rows, :]
            gate = _dot_nt(at, wg_ref[...])
            up = _dot_nt(at, wu_ref[...])
            gate_ref[rows, :] = gate.astype(BF16)
            up_ref[rows, :] = up.astype(BF16)
            act = (gate * jax.nn.sigmoid(gate) * up).astype(BF16)
            contrib = _dot(act, wd_ref[...])

            @pl.when(j == 0)
            def _():
                f_ref[rows, :] = contrib

            @pl.when(j != 0)
            def _():
                f_ref[rows, :] += contrib

            return carry

        for i in range(N_ROW_TILES):
            tile(i, 0)

    wtile = pl.BlockSpec((FF_TILE, D_MODEL), lambda j: (j, 0))
    hid = pl.BlockSpec((SEQ, FF_TILE), lambda j: (0, j))
    full = pl.BlockSpec((SEQ, D_MODEL), lambda j: (0, 0))
    return pl.pallas_call(
        body, grid=(D_FF // FF_TILE,), in_specs=[full, wtile, wtile, wtile] + [pl.BlockSpec(memory_space=pl.ANY)] * len(after),
        out_specs=[hid, hid, full],
        out_shape=[SDS((SEQ, D_FF), BF16), SDS((SEQ, D_FF), BF16), SDS((SEQ, D_MODEL), F32)],
        name=name, compiler_params=_params(("arbitrary",)))(a, wg, wu, wd, *after)


def _ffn_bwd(df, a, gate, up, wg, wu, wd, name):
    def body(df_ref, a_ref, gate_ref, up_ref, wg_ref, wu_ref, wd_ref, da_ref, dwg_ref, dwu_ref, dwd_ref,
             acc_g, acc_u, acc_d):
        j = pl.program_id(0)

        def tile(i, carry):
            rows = _rows(i)
            dft = df_ref[rows, :]
            at = a_ref[rows, :]
            gate = gate_ref[rows, :].astype(F32)
            up = up_ref[rows, :].astype(F32)
            dact = _dot_nt(dft, wd_ref[...])
            sig = jax.nn.sigmoid(gate)
            silu = gate * sig
            dgate = (dact * up * (sig * (1.0 + gate * (1.0 - sig)))).astype(BF16)
            dup = (dact * silu).astype(BF16)
            act = (silu * up).astype(BF16)
            dwd = _dot_tn(act, dft)
            dwg = _dot_tn(dgate, at)
            dwu = _dot_tn(dup, at)
            dat = _dot(dgate, wg_ref[...]) + _dot(dup, wu_ref[...])

            @pl.when(i == 0)
            def _():
                acc_d[...] = dwd
                acc_g[...] = dwg
                acc_u[...] = dwu

            @pl.when(i != 0)
            def _():
                acc_d[...] += dwd
                acc_g[...] += dwg
                acc_u[...] += dwu

            @pl.when(j == 0)
            def _():
                da_ref[rows, :] = dat

            @pl.when(j != 0)
            def _():
                da_ref[rows, :] += dat

            return carry

        for i in range(N_ROW_TILES):
            tile(i, 0)
        dwg_ref[...] = acc_g[...].astype(BF16)
        dwu_ref[...] = acc_u[...].astype(BF16)
        dwd_ref[...] = acc_d[...].astype(BF16)

    wtile = pl.BlockSpec((FF_TILE, D_MODEL), lambda j: (j, 0))
    hid = pl.BlockSpec((SEQ, FF_TILE), lambda j: (0, j))
    full = pl.BlockSpec((SEQ, D_MODEL), lambda j: (0, 0))
    return pl.pallas_call(
        body, grid=(D_FF // FF_TILE,), in_specs=[full, full, hid, hid, wtile, wtile, wtile],
        out_specs=[full, wtile, wtile, wtile],
        out_shape=[SDS((SEQ, D_MODEL), F32)] + [SDS((D_FF, D_MODEL), BF16)] * 3,
        scratch_shapes=[pltpu.VMEM((FF_TILE, D_MODEL), F32)] * 3,
        name=name, compiler_params=_params(("arbitrary",)))(df, a, gate, up, wg, wu, wd)


HEADS_PER_BLOCK = IN_SHARD // HEAD_DIM
QKV_BLOCKS = 3 * NA_WIDTH // IN_SHARD


def _proj_heads(a, w):
    def body(a_ref, w_ref, o_ref):
        def tile(i, carry):
            rows = _rows(i)
            res = _dot(a_ref[rows, :], w_ref[...])
            for sub in range(HEADS_PER_BLOCK):
                o_ref[sub, rows, :] = res[:, sub * HEAD_DIM:(sub + 1) * HEAD_DIM]
            return carry

        for i in range(N_ROW_TILES):
            tile(i, 0)

    return pl.pallas_call(
        body, grid=(QKV_BLOCKS,),
        in_specs=[pl.BlockSpec((SEQ, D_MODEL), lambda j: (0, 0)), pl.BlockSpec((None, D_MODEL, IN_SHARD), lambda j: (j, 0, 0))],
        out_specs=pl.BlockSpec((HEADS_PER_BLOCK, SEQ, HEAD_DIM), lambda j: (j, 0, 0)),
        out_shape=SDS((3 * HEADS, SEQ, HEAD_DIM), F32), name="proj_heads",
        compiler_params=_params(("parallel",)))(a, w)


def _proj_u(a, w):
    def body(a_ref, w_ref, o_ref):
        def tile(i, carry):
            rows = _rows(i)
            o_ref[rows, :] = _dot(a_ref[rows, :], w_ref[...])
            return carry

        for i in range(N_ROW_TILES):
            tile(i, 0)

    return pl.pallas_call(
        body, grid=(N_DEV - QKV_BLOCKS,),
        in_specs=[pl.BlockSpec((SEQ, D_MODEL), lambda j: (0, 0)),
                  pl.BlockSpec((None, D_MODEL, IN_SHARD), lambda j: (j + QKV_BLOCKS, 0, 0))],
        out_specs=pl.BlockSpec((SEQ, IN_SHARD), lambda j: (0, j)),
        out_shape=SDS((SEQ, S5_WIDTH), F32), name="proj_u",
        compiler_params=_params(("parallel",)))(a, w)


def _proj_bwd(dq, dk, dv, du, a, w):
    def body(dq_ref, dk_ref, dv_ref, du_ref, a_ref, w_ref, da_ref, dw_ref, acc, dp_ref):
        j = pl.program_id(0)

        for which, src in enumerate((dq_ref, dk_ref, dv_ref)):
            @pl.when((j >= 2 * which) & (j < 2 * which + 2))
            def _(src=src):
                dp_ref[...] = jnp.concatenate([src[sub] for sub in range(HEADS_PER_BLOCK)], axis=-1).astype(BF16)

        @pl.when(j >= QKV_BLOCKS)
        def _():
            dp_ref[...] = du_ref[...].astype(BF16)

        def tile(i, carry):
            rows = _rows(i)
            dpt = dp_ref[rows, :]
            dw = _dot_tn(a_ref[rows, :], dpt)
            dat = _dot_nt(dpt, w_ref[...])

            @pl.when(i == 0)
            def _():
                acc[...] = dw

            @pl.when(i != 0)
            def _():
                acc[...] += dw

            @pl.when(j == 0)
            def _():
                da_ref[rows, :] = dat

            @pl.when(j != 0)
            def _():
                da_ref[rows, :] += dat

            return carry

        for i in range(N_ROW_TILES):
            tile(i, 0)
        dw_ref[...] = acc[...].astype(BF16)

    full = pl.BlockSpec((SEQ, D_MODEL), lambda j: (0, 0))
    wspec = pl.BlockSpec((None, D_MODEL, IN_SHARD), lambda j: (j, 0, 0))

    def heads(which):
        return pl.BlockSpec((HEADS_PER_BLOCK, SEQ, HEAD_DIM), lambda j: (jnp.clip(j - 2 * which, 0, 1), 0, 0))

    return pl.pallas_call(
        body, grid=(N_DEV,),
        in_specs=[heads(0), heads(1), heads(2),
                  pl.BlockSpec((SEQ, IN_SHARD), lambda j: (0, jnp.clip(j - QKV_BLOCKS, 0, 1))), full, wspec],
        out_specs=[full, wspec],
        out_shape=[SDS((SEQ, D_MODEL), F32), SDS((N_DEV, D_MODEL, IN_SHARD), BF16)],
        scratch_shapes=[pltpu.VMEM((D_MODEL, IN_SHARD), F32), pltpu.VMEM((SEQ, IN_SHARD), BF16)],
        name="proj_bwd", compiler_params=_params(("arbitrary",)))(dq, dk, dv, du, a, w)


def _na_consts():
    c = np.arange(GRID_W)
    col_start = np.clip(c - KW // 2, 0, GRID_W - KW)
    col_in = (c[None, :] >= col_start[:, None]) & (c[None, :] < col_start[:, None] + KW)
    dc = np.clip(c[None, :] - c[:, None] + KW - 1, 0, 2 * KW - 2)
    onehot = np.zeros((128, GRID_W * GRID_W), np.float32)
    qq, kk = np.meshgrid(c, c, indexing="ij")
    onehot[dc[col_in], (qq * GRID_W + kk)[col_in]] = 1.0
    negmask = np.where(col_in, 0.0, NEG_INF).astype(np.float32).reshape(1, -1)
    return onehot, negmask


def _na_pair(block_type, a, b):
    if block_type == 0:
        return b - a + KH - 1 if b < KH else None
    if block_type == 1:
        return b - a + KH // 2 - 1 if a <= b < a + KH else None
    return b - a if b >= NA_KR - KH else None


def _rpb_expand(rpb):
    onehot, negmask = _na_consts()
    rows = HEADS * (2 * KH - 1)
    rpb_pad = jnp.pad(rpb.reshape(rows, 2 * KW - 1), ((0, 128 - rows), (0, 128 - (2 * KW - 1))))

    def body(r_ref, oh_ref, m_ref, t_ref):
        hi, mid, lo = _split3(r_ref[...])
        oh = oh_ref[...]
        t_ref[...] = _dot(hi, oh) + _dot(mid, oh) + _dot(lo, oh) + m_ref[...]

    table = pl.pallas_call(body, out_shape=SDS((128, GRID_W * GRID_W), F32), name="rpb_expand",
                           compiler_params=_params())(rpb_pad, jnp.asarray(onehot, BF16), jnp.asarray(negmask))
    return table[:rows].reshape(HEADS, 2 * KH - 1, GRID_W, GRID_W)


def _rpb_reduce(dslabs):
    onehot, _ = _na_consts()
    rows = HEADS * (2 * KH - 1)

    def body(x_ref, oht_ref, o_ref):
        hi, mid, lo = _split3(x_ref[...])
        oht = oht_ref[...]
        o_ref[...] = _dot(hi, oht) + _dot(mid, oht) + _dot(lo, oht)

    out = pl.pallas_call(body, out_shape=SDS((rows, 128), F32), name="rpb_reduce", compiler_params=_params())(
        dslabs.reshape(rows, GRID_W * GRID_W), jnp.asarray(onehot.T, BF16))
    return out.reshape(HEADS, 2 * KH - 1, 128)


def _bias_tiles(slab_ref, tile_ref):
    tile_ref[...] = jnp.full(tile_ref.shape, NEG_INF, F32)
    for t in range(NA_TYPES):
        for a in range(NA_RB):
            for b in range(NA_KR):
                dr = _na_pair(t, a, b)
                if dr is not None:
                    tile_ref[t, a * GRID_W:(a + 1) * GRID_W, b * GRID_W:(b + 1) * GRID_W] = slab_ref[dr]


def _bias_tiles_bwd(dtile_ref, dslab_ref):
    acc = {}
    for t in range(NA_TYPES):
        for a in range(NA_RB):
            for b in range(NA_KR):
                dr = _na_pair(t, a, b)
                if dr is not None:
                    part = dtile_ref[t, a * GRID_W:(a + 1) * GRID_W, b * GRID_W:(b + 1) * GRID_W]
                    acc[dr] = part if dr not in acc else acc[dr] + part
    for dr in range(2 * KH - 1):
        dslab_ref[dr] = acc[dr]


def _block_geometry(g):
    if isinstance(g, int):
        start = min(max(g * NA_RB - KH // 2, 0), GRID_ROWS - NA_KR)
        return (0 if g == 0 else 2 if g == NA_BLOCKS - 1 else 1), N_META + g * NA_QB, N_META + start * GRID_W
    start = jnp.clip(g * NA_RB - KH // 2, 0, GRID_ROWS - NA_KR)
    block_type = jnp.where(g == 0, 0, jnp.where(g == NA_BLOCKS - 1, 2, 1))
    q0 = pl.multiple_of(N_META + g * NA_QB, 16)
    k0 = pl.multiple_of(N_META + start * GRID_W, 16)
    return block_type, q0, k0


def _scaled_q(q):
    return (q * ATT_SCALE).astype(BF16)


def _na_probs(qs, kk, km, bias):
    s = _dot_nt(qs, kk) + bias
    sm = _dot_nt(qs, km)
    m = jnp.maximum(jnp.max(s, axis=-1, keepdims=True), jnp.max(sm, axis=-1, keepdims=True))
    p = jnp.exp(s - m)
    pm = jnp.exp(sm - m)
    inv = 1.0 / (jnp.sum(p, axis=-1, keepdims=True) + jnp.sum(pm, axis=-1, keepdims=True))
    return p * inv, pm * inv


def _meta_probs(qm, km):
    s = _dot_nt(qm, km) * ATT_SCALE
    p = jnp.exp(s - jnp.max(s, axis=-1, keepdims=True))
    return p / jnp.sum(p, axis=-1, keepdims=True)


def _qkv_specs():
    return [pl.BlockSpec((None, SEQ, HEAD_DIM), lambda h, which=which: (h + which * HEADS, 0, 0)) for which in range(3)]


def _na_fwd(qkv, bias):
    def body(q_ref, k_ref, v_ref, slab_ref, o_ref, b_ref):
        _bias_tiles(slab_ref, b_ref)
        km = k_ref[0:N_META, :].astype(BF16)
        vm = v_ref[0:N_META, :].astype(BF16)
        pmm = _meta_probs(q_ref[0:N_META, :].astype(BF16), km)
        o_ref[0:N_META, :] = _dot(pmm.astype(BF16), vm)

        def block(g, carry):
            block_type, q0, k0 = _block_geometry(g)
            qs = _scaled_q(q_ref[pl.ds(q0, NA_QB), :])
            kk = k_ref[pl.ds(k0, NA_KB), :].astype(BF16)
            vv = v_ref[pl.ds(k0, NA_KB), :].astype(BF16)
            p, pm = _na_probs(qs, kk, km, b_ref[block_type])
            o_ref[pl.ds(q0, NA_QB), :] = _dot(p.astype(BF16), vv) + _dot(pm.astype(BF16), vm)
            return carry

        for g in range(NA_BLOCKS):
            block(g, 0)

    head = pl.BlockSpec((None, SEQ, HEAD_DIM), lambda h: (h, 0, 0))
    return pl.pallas_call(
        body, grid=(HEADS,), in_specs=_qkv_specs() + [pl.BlockSpec((None, 2 * KH - 1, GRID_W, GRID_W), lambda h: (h, 0, 0, 0))],
        out_specs=head, out_shape=SDS((HEADS, SEQ, HEAD_DIM), F32), name="na_fwd",
        scratch_shapes=[pltpu.VMEM((NA_TYPES, NA_QB, NA_KB), F32)],
        compiler_params=_params(("parallel",)))(qkv, qkv, qkv, bias)


def _na_bwd(qkv, bias, do):
    def body(q_ref, k_ref, v_ref, slab_ref, do_ref, dq_ref, dk_ref, dv_ref, dslab_ref, b_ref, db_ref):
        _bias_tiles(slab_ref, b_ref)
        km = k_ref[0:N_META, :].astype(BF16)
        vm = v_ref[0:N_META, :].astype(BF16)
        dk_ref[...] = jnp.zeros_like(dk_ref)
        dv_ref[...] = jnp.zeros_like(dv_ref)
        db_ref[...] = jnp.zeros_like(db_ref)

        qm = q_ref[0:N_META, :].astype(BF16)
        dom = do_ref[0:N_META, :].astype(BF16)
        pmm = _meta_probs(qm, km)
        dpm = _dot_nt(dom, vm)
        dsm = (pmm * (dpm - jnp.sum(pmm * dpm, axis=-1, keepdims=True)) * ATT_SCALE).astype(BF16)
        dq_ref[0:N_META, :] = _dot(dsm, km)
        dkm0 = _dot_tn(dsm, qm)
        dvm0 = _dot_tn(pmm.astype(BF16), dom)

        def block(g, carry):
            dkm, dvm = carry
            block_type, q0, k0 = _block_geometry(g)
            qs = _scaled_q(q_ref[pl.ds(q0, NA_QB), :])
            kk = k_ref[pl.ds(k0, NA_KB), :].astype(BF16)
            vv = v_ref[pl.ds(k0, NA_KB), :].astype(BF16)
            dob = do_ref[pl.ds(q0, NA_QB), :].astype(BF16)
            p, pm = _na_probs(qs, kk, km, b_ref[block_type])
            dp = _dot_nt(dob, vv)
            dpm_ = _dot_nt(dob, vm)
            delta = jnp.sum(p * dp, axis=-1, keepdims=True) + jnp.sum(pm * dpm_, axis=-1, keepdims=True)
            ds = p * (dp - delta)
            dsm_ = pm * (dpm_ - delta)
            db_ref[block_type] += ds
            dsb = ds.astype(BF16)
            dsmb = dsm_.astype(BF16)
            dq_ref[pl.ds(q0, NA_QB), :] = (_dot(dsb, kk) + _dot(dsmb, km)) * ATT_SCALE
            dk_ref[pl.ds(k0, NA_KB), :] += _dot_tn(dsb, qs)
            dv_ref[pl.ds(k0, NA_KB), :] += _dot_tn(p.astype(BF16), dob)
            return dkm + _dot_tn(dsmb, qs), dvm + _dot_tn(pm.astype(BF16), dob)

        dkm, dvm = dkm0, dvm0
        for g in range(NA_BLOCKS):
            dkm, dvm = block(g, (dkm, dvm))
        dk_ref[0:N_META, :] = dkm
        dv_ref[0:N_META, :] = dvm
        _bias_tiles_bwd(db_ref, dslab_ref)

    head = pl.BlockSpec((None, SEQ, HEAD_DIM), lambda h: (h, 0, 0))
    bspec = pl.BlockSpec((None, 2 * KH - 1, GRID_W, GRID_W), lambda h: (h, 0, 0, 0))
    return pl.pallas_call(
        body, grid=(HEADS,), in_specs=_qkv_specs() + [bspec, head], out_specs=[head, head, head, bspec],
        out_shape=[SDS((HEADS, SEQ, HEAD_DIM), F32)] * 3 + [SDS((HEADS, 2 * KH - 1, GRID_W, GRID_W), F32)],
        scratch_shapes=[pltpu.VMEM((NA_TYPES, NA_QB, NA_KB), F32), pltpu.VMEM((NA_TYPES, NA_QB, NA_KB), F32)],
        name="na_bwd", compiler_params=_params(("parallel",)))(qkv, qkv, qkv, bias, do)


def _cmul(ar, ai, br, bi):
    return ar * br - ai * bi, ar * bi + ai * br


def _cpow(ar, ai, n):
    rr, ri = None, None
    br, bi = ar, ai
    while n:
        if n & 1:
            rr, ri = (br, bi) if rr is None else _cmul(rr, ri, br, bi)
        n >>= 1
        if n:
            br, bi = _cmul(br, bi, br, bi)
    return rr, ri


def _s5_prep(lr, li, logdt, bre, bim):
    def body(lr_ref, li_ref, dt_ref, br_ref, bi_ref, lbr_ref, lbi_ref, bbr_ref, bbi_ref):
        lr_, li_ = lr_ref[...], li_ref[...]
        dt = jnp.exp(dt_ref[...])
        mag = jnp.exp(lr_ * dt)
        lbr = mag * jnp.cos(li_ * dt)
        lbi = mag * jnp.sin(li_ * dt)
        lbr_ref[...] = lbr
        lbi_ref[...] = lbi
        den = lr_ * lr_ + li_ * li_
        xr = lbr - 1.0
        cr = (xr * lr_ + lbi * li_) / den
        ci = (lbi * lr_ - xr * li_) / den
        br, bi = br_ref[...], bi_ref[...]
        bbr_ref[...] = cr[:, None, :] * br - ci[:, None, :] * bi
        bbi_ref[...] = cr[:, None, :] * bi + ci[:, None, :] * br

    n = 2 * S5_GROUPS
    return pl.pallas_call(
        body, out_shape=[SDS((n, S5_STATE), F32)] * 2 + [SDS((n, S5_GROUP, S5_STATE), F32)] * 2,
        name="s5_prep", compiler_params=_params())(lr, li, logdt, bre, bim)


def _s5_prep_bwd(lr, li, logdt, bre, bim, dar, dai, dbbr, dbbi):
    def body(lr_ref, li_ref, dt_ref, br_ref, bi_ref, dar_ref, dai_ref, dbr_ref, dbi_ref,
             glr_ref, gli_ref, gdt_ref, gbr_ref, gbi_ref):
        lr_, li_ = lr_ref[...], li_ref[...]
        dt = jnp.exp(dt_ref[...])
        mag = jnp.exp(lr_ * dt)
        lbr = mag * jnp.cos(li_ * dt)
        lbi = mag * jnp.sin(li_ * dt)
        den = lr_ * lr_ + li_ * li_
        xr = lbr - 1.0
        cr = (xr * lr_ + lbi * li_) / den
        ci = (lbi * lr_ - xr * li_) / den
        br, bi = br_ref[...], bi_ref[...]
        dbr, dbi = dbr_ref[...], dbi_ref[...]
        gbr_ref[...] = cr[:, None, :] * dbr + ci[:, None, :] * dbi
        gbi_ref[...] = cr[:, None, :] * dbi - ci[:, None, :] * dbr
        gcr = jnp.sum(dbr * br + dbi * bi, axis=1)
        gci = jnp.sum(dbi * br - dbr * bi, axis=1)
        ilr, ili = lr_ / den, li_ / den
        tr, ti = _cmul(gcr, gci, ilr, ili)
        glbr = dar_ref[...] + tr
        glbi = dai_ref[...] + ti
        dr_, di_ = _cmul(tr, ti, cr, -ci)
        gwr, gwi = _cmul(glbr, glbi, lbr, -lbi)
        glr_ref[...] = gwr * dt - dr_
        gli_ref[...] = gwi * dt - di_
        gdt_ref[...] = jnp.sum(gwr * lr_ + gwi * li_, axis=-1, keepdims=True) * dt

    n = 2 * S5_GROUPS
    return pl.pallas_call(
        body, out_shape=[SDS((n, S5_STATE), F32)] * 2 + [SDS((n, 1), F32)] + [SDS((n, S5_GROUP, S5_STATE), F32)] * 2,
        name="s5_prep_bwd", compiler_params=_params())(lr, li, logdt, bre, bim, dar, dai, dbbr, dbbi)


def _scan_local(xr_ref, xi_ref, ar8, ai8, reverse):
    def step(i, carry):
        sr, si = carry
        idx = (SCAN_T - 1 - i) if reverse else i
        rows = pl.ds(pl.multiple_of(idx * SCAN_BLOCKS, SCAN_BLOCKS), SCAN_BLOCKS)
        nr = ar8 * sr - ai8 * si + xr_ref[rows, :]
        ni = ar8 * si + ai8 * sr + xi_ref[rows, :]
        xr_ref[rows, :] = nr
        xi_ref[rows, :] = ni
        return nr, ni

    z = jnp.zeros(ar8.shape, F32)
    return lax.fori_loop(0, SCAN_T, step, (z, z))


def _scan_carries(er, ei, atr, ati, reverse):
    row = lax.broadcasted_iota(jnp.int32, er.shape, 0)
    cr = jnp.zeros((1, er.shape[1]), F32)
    ci = cr
    outr = jnp.zeros(er.shape, F32)
    outi = outr
    order = range(SCAN_BLOCKS - 1, -1, -1) if reverse else range(SCAN_BLOCKS)
    for b in order:
        outr = jnp.where(row == b, cr, outr)
        outi = jnp.where(row == b, ci, outi)
        nr, ni = _cmul(atr, ati, cr, ci)
        cr, ci = nr + er[b:b + 1, :], ni + ei[b:b + 1, :]
    return outr, outi


def _scan_fixup(xr_ref, xi_ref, cr8, ci8, ar8, ai8, reverse, pair=None):
    tile = lambda idx: pl.ds(pl.multiple_of(idx * SCAN_BLOCKS, SCAN_BLOCKS), SCAN_BLOCKS)

    def fix(idx, pr, pi):
        fr, fi = _cmul(pr, pi, cr8, ci8)
        nr, ni = xr_ref[tile(idx), :] + fr, xi_ref[tile(idx), :] + fi
        xr_ref[tile(idx), :] = nr
        xi_ref[tile(idx), :] = ni
        return nr, ni

    if pair is None:
        def step(i, carry):
            pr, pi = carry
            fix((SCAN_T - 1 - i) if reverse else i, pr, pi)
            return _cmul(pr, pi, ar8, ai8)

        lax.fori_loop(0, SCAN_T, step, (ar8, ai8), unroll=2)
        return None

    sr_ref, si_ref = pair
    earlier = -1 if reverse else 1

    def step(i, carry):
        pr, pi, accr, acci = carry
        idx = (SCAN_T - 1 - i) if reverse else i
        nr, ni = fix(idx, pr, pi)
        qr, qi = _cmul(nr, ni, sr_ref[tile(idx + earlier), :], -si_ref[tile(idx + earlier), :])
        pr, pi = _cmul(pr, pi, ar8, ai8)
        return pr, pi, accr + qr, acci + qi

    z = jnp.zeros(ar8.shape, F32)
    pr, pi, accr, acci = lax.fori_loop(0, SCAN_T - 1, step, (ar8, ai8, z, z))
    edge, src, shift, empty = (0, SCAN_T - 1, 1, 0) if reverse else (SCAN_T - 1, 0, SCAN_BLOCKS - 1, SCAN_BLOCKS - 1)
    nr, ni = fix(edge, pr, pi)
    row = lax.broadcasted_iota(jnp.int32, ar8.shape, 0)
    spr = jnp.where(row == empty, 0.0, pltpu.roll(sr_ref[tile(src), :], shift, 0))
    spi = jnp.where(row == empty, 0.0, pltpu.roll(si_ref[tile(src), :], shift, 0))
    qr, qi = _cmul(nr, ni, spr, -spi)
    return jnp.sum(accr + qr, axis=0, keepdims=True), jnp.sum(acci + qi, axis=0, keepdims=True)


def _scan(xr_ref, xi_ref, ar, ai, reverse, pair=None):
    n = ar.shape[1]
    ar8 = jnp.broadcast_to(ar, (SCAN_BLOCKS, n))
    ai8 = jnp.broadcast_to(ai, (SCAN_BLOCKS, n))
    er, ei = _scan_local(xr_ref, xi_ref, ar8, ai8, reverse)
    atr, ati = _cpow(ar, ai, SCAN_T)
    cr8, ci8 = _scan_carries(er, ei, atr, ati, reverse)
    return _scan_fixup(xr_ref, xi_ref, cr8, ci8, ar8, ai8, reverse, pair)


def _s5_specs():
    chan = pl.BlockSpec((SEQ, CH_W), lambda c, d: (0, c))
    chan2 = pl.BlockSpec((None, SEQ, CH_W), lambda c, d: (d, 0, c))
    state = pl.BlockSpec((None, SEQ, ST_W), lambda c, d: (d, 0, c))
    bmat = pl.BlockSpec((None, None, CH_W, ST_W), lambda c, d: (d, c, 0, 0))
    cmat = pl.BlockSpec((None, None, ST_W, CH_W), lambda c, d: (d, c, 0, 0))
    avec = pl.BlockSpec((None, None, 1, ST_W), lambda c, d: (d, c, 0, 0))
    return chan, chan2, state, bmat, cmat, avec


def _scan_by_direction(xr_ref, xi_ref, ar, ai, d, adjoint, pair=None, da_out=None):
    for direction in range(2):
        @pl.when(d == direction)
        def _(direction=direction):
            res = _scan(xr_ref, xi_ref, ar, ai, adjoint != (direction == 1), pair)
            if pair is not None:
                da_out[0][...], da_out[1][...] = res


def _s5_scan_fwd(u, bre, bim, are, aim, cre, cim):
    def body(u_ref, bre_ref, bim_ref, are_ref, aim_ref, cre_ref, cim_ref, sr_ref, si_ref, y_ref):
        ub = u_ref[...].astype(BF16)
        sr_ref[...] = _dot(ub, bre_ref[...])
        si_ref[...] = _dot(ub, bim_ref[...])
        _scan_by_direction(sr_ref, si_ref, are_ref[...], aim_ref[...], pl.program_id(1), adjoint=False)
        y_ref[...] = _dot(sr_ref[...].astype(BF16), cre_ref[...]) - _dot(si_ref[...].astype(BF16), cim_ref[...])

    chan, chan2, state, bmat, cmat, avec = _s5_specs()
    return pl.pallas_call(
        body, grid=(S5_CHUNKS, 2), in_specs=[chan, bmat, bmat, avec, avec, cmat, cmat], out_specs=[state, state, chan2],
        out_shape=[SDS((2, SEQ, S5_GROUPS * S5_STATE), F32)] * 2 + [SDS((2, SEQ, S5_WIDTH), F32)],
        name="s5_scan_fwd", compiler_params=_params(("parallel", "parallel")))(u, bre, bim, are, aim, cre, cim)


def _diag_out(out_ref, full):
    for g in range(8):
        out_ref[g] = full[g * S5_GROUP:(g + 1) * S5_GROUP, g * S5_STATE:(g + 1) * S5_STATE]


def _s5_scan_bwd(dy, du_skip, u, sr, si, bre, bim, are, aim, cre, cim):
    def body(dy_ref, dus_ref, u_ref, sr_ref, si_ref, bre_ref, bim_ref, are_ref, aim_ref, cre_ref, cim_ref,
             du_ref, dbr_ref, dbi_ref, dcr_ref, dci_ref, dar_ref, dai_ref, gr_ref, gi_ref):
        d = pl.program_id(1)
        dyb = dy_ref[...].astype(BF16)
        gr_ref[...] = _dot_nt(dyb, cre_ref[...])
        gi_ref[...] = -_dot_nt(dyb, cim_ref[...])
        _diag_out(dcr_ref, _dot_tn(dyb, sr_ref[...].astype(BF16)))
        _diag_out(dci_ref, -_dot_tn(dyb, si_ref[...].astype(BF16)))
        _scan_by_direction(gr_ref, gi_ref, are_ref[...], -aim_ref[...], d, adjoint=True, pair=(sr_ref, si_ref),
                           da_out=(dar_ref, dai_ref))

        @pl.when(d == 0)
        def _():
            du_ref[...] = dus_ref[...]

        grb = gr_ref[...].astype(BF16)
        gib = gi_ref[...].astype(BF16)
        du_ref[...] += _dot_nt(grb, bre_ref[...]) + _dot_nt(gib, bim_ref[...])
        ub = u_ref[...].astype(BF16)
        _diag_out(dbr_ref, _dot_tn(ub, grb))
        _diag_out(dbi_ref, _dot_tn(ub, gib))

    chan, _, state, bmat, cmat, avec = _s5_specs()
    diag = pl.BlockSpec((None, None, 8, S5_GROUP, S5_STATE), lambda c, d: (d, c, 0, 0, 0))
    return pl.pallas_call(
        body, grid=(S5_CHUNKS, 2), in_specs=[chan, chan, chan, state, state, bmat, bmat, avec, avec, cmat, cmat],
        out_specs=[chan, diag, diag, diag, diag, avec, avec],
        out_shape=[SDS((SEQ, S5_WIDTH), F32)] + [SDS((2, S5_CHUNKS, 8, S5_GROUP, S5_STATE), F32)] * 4
                  + [SDS((2, S5_CHUNKS, 1, ST_W), F32)] * 2,
        scratch_shapes=[pltpu.VMEM((SEQ, ST_W), F32), pltpu.VMEM((SEQ, ST_W), F32)],
        name="s5_scan_bwd", compiler_params=_params(("parallel", "arbitrary")))(dy, du_skip, u, sr, si, bre, bim, are, aim, cre, cim)


_GELU_K = math.sqrt(2.0 / math.pi)
_GELU_C = 0.044715


def _gelu(x):
    t = jnp.tanh(_GELU_K * (x + _GELU_C * x * x * x))
    return 0.5 * x * (1.0 + t), t


def _s5_glu_fwd(u, y2, dskip, wglu, bglu):
    def body(u_ref, y0_ref, y1_ref, d_ref, w_ref, b_ref, o_ref, yp_ref):
        ypre = u_ref[...] * d_ref[...] + y0_ref[...] + y1_ref[...]
        yp_ref[...] = ypre
        y, _ = _gelu(ypre)
        z = _dot(y.astype(BF16), w_ref[...]) + b_ref[...]
        o_ref[...] = y * jax.nn.sigmoid(z)

    row = _row_spec(S5_WIDTH)
    vec = _fix_spec((1, S5_WIDTH))
    dir0 = pl.BlockSpec((None, ROW_TILE, S5_WIDTH), lambda i: (0, i, 0))
    dir1 = pl.BlockSpec((None, ROW_TILE, S5_WIDTH), lambda i: (1, i, 0))
    return pl.pallas_call(
        body, grid=(N_ROW_TILES,), in_specs=[row, dir0, dir1, vec, _fix_spec((S5_WIDTH, S5_WIDTH)), vec],
        out_specs=[row, row], out_shape=[SDS((SEQ, S5_WIDTH), F32)] * 2, name="s5_glu_fwd",
        compiler_params=_params(("parallel",)))(u, y2, y2, dskip, wglu, bglu)


def _s5_glu_bwd(do, ypre, u, dskip, wglu, bglu):
    def body(do_ref, yp_ref, u_ref, d_ref, w_ref, b_ref, dyp_ref, du_ref, dw_ref, db_ref, dd_ref):
        i = pl.program_id(0)
        ypre = yp_ref[...]
        y, t = _gelu(ypre)
        yb = y.astype(BF16)
        sg = jax.nn.sigmoid(_dot(yb, w_ref[...]) + b_ref[...])
        dov = do_ref[...]
        dz = dov * y * sg * (1.0 - sg)
        dzb = dz.astype(BF16)
        dy = dov * sg + _dot_nt(dzb, w_ref[...])
        dgelu = 0.5 * (1.0 + t) + 0.5 * ypre * (1.0 - t * t) * _GELU_K * (1.0 + 3.0 * _GELU_C * ypre * ypre)
        dyp = dy * dgelu
        dyp_ref[...] = dyp
        uv = u_ref[...]
        du_ref[...] = dyp * d_ref[...]

        @pl.when(i == 0)
        def _():
            dw_ref[...] = jnp.zeros_like(dw_ref)
            db_ref[...] = jnp.zeros_like(db_ref)
            dd_ref[...] = jnp.zeros_like(dd_ref)

        dw_ref[...] += _dot_tn(yb, dzb)
        db_ref[...] += jnp.sum(dz, axis=0, keepdims=True)
        dd_ref[...] += jnp.sum(dyp * uv, axis=0, keepdims=True)

    row = _row_spec(S5_WIDTH)
    vec = _fix_spec((1, S5_WIDTH))
    mat = _fix_spec((S5_WIDTH, S5_WIDTH))
    return pl.pallas_call(
        body, grid=(N_ROW_TILES,), in_specs=[row, row, row, vec, mat, vec], out_specs=[row, row, mat, vec, vec],
        out_shape=[SDS((SEQ, S5_WIDTH), F32)] * 2 + [SDS((S5_WIDTH, S5_WIDTH), F32), SDS((1, S5_WIDTH), F32), SDS((1, S5_WIDTH), F32)],
        name="s5_glu_bwd", compiler_params=_params(("arbitrary",)))(do, ypre, u, dskip, wglu, bglu)


def _heads_side_by_side(o_ref):
    return jnp.concatenate([o_ref[h] for h in range(HEADS)], axis=-1)


def _mix_out_fwd(ona, os5, g_na, g_s5, wout):
    def body(a_ref, s_ref, ga_ref, gs_ref, w_ref, o_ref):
        av, sv = _heads_side_by_side(a_ref), s_ref[...]
        ca = (av * _rstd(av) * ga_ref[...]).astype(BF16)
        cs = (sv * _rstd(sv) * gs_ref[...]).astype(BF16)
        o_ref[...] = _dot(ca, w_ref[0:NA_WIDTH, :]) + _dot(cs, w_ref[NA_WIDTH:, :])

    row = _row_spec(NA_WIDTH)
    vec = _fix_spec((1, NA_WIDTH))
    heads = pl.BlockSpec((HEADS, ROW_TILE, HEAD_DIM), lambda i: (0, i, 0))
    return pl.pallas_call(
        body, grid=(N_ROW_TILES,), in_specs=[heads, row, vec, vec, _fix_spec((D_MODEL, D_MODEL))],
        out_specs=_row_spec(D_MODEL), out_shape=SDS((SEQ, D_MODEL), F32), name="mix_out_fwd",
        compiler_params=_params(("parallel",)))(ona, os5, g_na, g_s5, wout)


def _mix_out_bwd(dmix, ona, os5, g_na, g_s5, wout):
    def body(dm_ref, a_ref, s_ref, ga_ref, gs_ref, w_ref, da_ref, ds_ref, dw_ref, dga_ref, dgs_ref):
        i = pl.program_id(0)
        dm = dm_ref[...]
        av, sv = _heads_side_by_side(a_ref), s_ref[...]
        ra, rs = _rstd(av), _rstd(sv)
        ga, gs = ga_ref[...], gs_ref[...]
        ca = (av * ra * ga).astype(BF16)
        cs = (sv * rs * gs).astype(BF16)
        dca = _dot_nt(dm, w_ref[0:NA_WIDTH, :])
        dcs = _dot_nt(dm, w_ref[NA_WIDTH:, :])
        da, dga = _rms_bwd(av, ra, ga, dca)
        ds, dgs = _rms_bwd(sv, rs, gs, dcs)
        for h in range(HEADS):
            da_ref[h] = da[:, h * HEAD_DIM:(h + 1) * HEAD_DIM]
        ds_ref[...] = ds

        @pl.when(i == 0)
        def _():
            dw_ref[...] = jnp.zeros_like(dw_ref)
            dga_ref[...] = jnp.zeros_like(dga_ref)
            dgs_ref[...] = jnp.zeros_like(dgs_ref)

        dw_ref[0:NA_WIDTH, :] += _dot_tn(ca, dm)
        dw_ref[NA_WIDTH:, :] += _dot_tn(cs, dm)
        dga_ref[...] += jnp.sum(dga, axis=0, keepdims=True)
        dgs_ref[...] += jnp.sum(dgs, axis=0, keepdims=True)

    row = _row_spec(NA_WIDTH)
    vec = _fix_spec((1, NA_WIDTH))
    mat = _fix_spec((D_MODEL, D_MODEL))
    heads = pl.BlockSpec((HEADS, ROW_TILE, HEAD_DIM), lambda i: (0, i, 0))
    return pl.pallas_call(
        body, grid=(N_ROW_TILES,), in_specs=[_row_spec(D_MODEL), heads, row, vec, vec, mat],
        out_specs=[heads, row, mat, vec, vec],
        out_shape=[SDS((HEADS, SEQ, HEAD_DIM), F32), SDS((SEQ, NA_WIDTH), F32), SDS((D_MODEL, D_MODEL), F32),
                   SDS((1, NA_WIDTH), F32), SDS((1, NA_WIDTH), F32)],
        name="mix_out_bwd", compiler_params=_params(("arbitrary",)))(dmix, ona, os5, g_na, g_s5, wout)


def _me():
    x, y, c = lax.axis_index("x"), lax.axis_index("y"), lax.axis_index("c")
    return x, y, c, 4 * x + 2 * y + c


def _peer(k):
    x, y, c, _ = _me()
    px = 1 - x if (k >> 2) & 1 else x
    py = 1 - y if (k >> 1) & 1 else y
    pc = 1 - c if k & 1 else c
    return (px, py, pc), 4 * px + 2 * py + pc


ALL_PEERS = (1, 2, 3, 4, 5, 6, 7)
CHIP_PEERS = (2, 4, 6)
SIBLING = 1


def _slot8(pos):
    return 4 * pos[0] + 2 * pos[1] + pos[2]


def _slot4(pos):
    return 2 * pos[0] + pos[1]


_HBM = pl.BlockSpec(memory_space=pltpu.HBM)
_SEM = pl.BlockSpec(memory_space=pltpu.SEMAPHORE)
_EFFECT = pltpu.SideEffectType.DATAFLOW_SIDE_EFFECTING


def _exchange_start(arrays, lands, gather, name, peers=ALL_PEERS, slot=_slot8, own=True):
    n = len(arrays)

    def body(*refs):
        ins, lnd = refs[:n], refs[n:2 * n]
        send_sems, recv_sems = refs[2 * n], refs[2 * n + 1]
        token = refs[-1]
        me = slot(_me()[:3])
        for i, k in enumerate(peers):
            peer, _ = _peer(k)
            for a in range(n):
                src = ins[a] if gather else ins[a].at[slot(peer)]
                s = a * len(peers) + i
                pltpu.make_async_remote_copy(src_ref=src, dst_ref=lnd[a].at[me], send_sem=send_sems.at[s],
                                             recv_sem=recv_sems.at[s], device_id=peer, device_id_type=MESH).start()
        if own:
            for a in range(n):
                pltpu.make_async_copy(ins[a] if gather else ins[a].at[me], lnd[a].at[me], recv_sems.at[n * len(peers) + a]).start()
        token[...] = jnp.zeros_like(token)

    sems = pltpu.SemaphoreType.DMA((n * (len(peers) + int(own)),))
    out = pl.pallas_call(
        body, name=name, in_specs=[_HBM] * (2 * n),
        out_shape=(sems, sems) + tuple(pltpu.HBM(a.shape, a.dtype) for a in list(arrays) + list(lands)) + (SDS((8, 128), F32),),
        out_specs=(_SEM, _SEM) + (_HBM,) * (2 * n) + (pl.BlockSpec(memory_space=pltpu.VMEM),),
        input_output_aliases={i: 2 + i for i in range(2 * n)},
        compiler_params=pltpu.CompilerParams(has_side_effects=_EFFECT),
    )(*[pltpu.with_memory_space_constraint(a, pltpu.HBM) for a in list(arrays) + list(lands)])
    return out[0], out[1], list(out[2:2 + n]), list(out[2 + n:2 + 2 * n]), out[-1]


def _exchange_wait(send_sems, recv_sems, arrays, lands, after, gather, name, peers=ALL_PEERS, slot=_slot8, own=True):
    n = len(arrays)

    def body(*refs):
        ins, lnd = refs[:n], refs[n:2 * n]
        send_sems, recv_sems = refs[2 * n], refs[2 * n + 1]
        if own:
            me = slot(_me()[:3])
            for a in range(n):
                pltpu.make_async_copy(ins[a] if gather else ins[a].at[me], lnd[a].at[me], recv_sems.at[n * len(peers) + a]).wait()
        for i, k in enumerate(peers):
            peer, _ = _peer(k)
            for a in range(n):
                src = ins[a] if gather else ins[a].at[slot(peer)]
                s = a * len(peers) + i
                cp = pltpu.make_async_remote_copy(src_ref=src, dst_ref=lnd[a].at[slot(peer)], send_sem=send_sems.at[s],
                                                  recv_sem=recv_sems.at[s], device_id=peer, device_id_type=MESH)
                cp.wait_send()
                cp.wait_recv()

        refs[-1][...] = jnp.zeros_like(refs[-1])

    after = list(after) if isinstance(after, (list, tuple)) else [after]
    out = pl.pallas_call(
        body, name=name, in_specs=[_HBM] * (2 * n) + [_SEM, _SEM] + [pl.BlockSpec(memory_space=pl.ANY)] * len(after),
        out_shape=tuple(pltpu.HBM(a.shape, a.dtype) for a in list(arrays) + list(lands)) + (SDS((8, 128), F32),),
        out_specs=(_HBM,) * (2 * n) + (pl.BlockSpec(memory_space=pltpu.VMEM),), input_output_aliases={i: i for i in range(2 * n)},
        compiler_params=pltpu.CompilerParams(has_side_effects=_EFFECT),
    )(*arrays, *lands, send_sems, recv_sems, *after)
    return list(out[n:2 * n]), out[-1]


def _forward_sibling(lands, name):
    n = len(lands)

    def body(*refs):
        outs = refs[n:2 * n]
        send_sems, recv_sems = refs[2 * n:]
        x, y, c, _ = _me()
        sends = []
        for i, k in enumerate(CHIP_PEERS):
            peer, _ = _peer(k)
            for a in range(n):
                rows = outs[a].at[_slot8(peer)]
                cp = pltpu.make_async_remote_copy(src_ref=rows, dst_ref=rows, send_sem=send_sems.at[a, i], recv_sem=recv_sems.at[a, i],
                                                  device_id=(x, y, 1 - c), device_id_type=MESH)
                cp.start()
                sends.append(cp)
        for i, k in enumerate(CHIP_PEERS):
            (px, py, pc), _ = _peer(k)
            for a in range(n):
                rows = outs[a].at[_slot8((px, py, 1 - pc))]
                pltpu.make_async_remote_copy(src_ref=rows, dst_ref=rows, send_sem=send_sems.at[a, i], recv_sem=recv_sems.at[a, i],
                                             device_id=(x, y, 1 - c), device_id_type=MESH).wait_recv()
        for cp in sends:
            cp.wait_send()

    return pl.pallas_call(
        body, in_specs=[_HBM] * n, out_specs=[_HBM] * n, out_shape=[SDS(a.shape, a.dtype) for a in lands],
        input_output_aliases={i: i for i in range(n)},
        scratch_shapes=[pltpu.SemaphoreType.DMA((n, len(CHIP_PEERS))), pltpu.SemaphoreType.DMA((n, len(CHIP_PEERS)))],
        name=name)(*lands)


def _swap_sibling(arrays, name, after=()):
    n, n_after = len(arrays), len(after)
    chips = N_DEV // 2

    def body(*refs):
        ins, outs = refs[:n], refs[n + n_after:2 * n + n_after]
        send_sems, recv_sems = refs[2 * n + n_after:]
        x, y, c, _ = _me()
        sends = []
        for q in range(chips):
            for a in range(n):
                cp = pltpu.make_async_remote_copy(src_ref=ins[a].at[q, 1 - c], dst_ref=outs[a].at[q], send_sem=send_sems.at[a, q],
                                                  recv_sem=recv_sems.at[a, q], device_id=(x, y, 1 - c), device_id_type=MESH)
                cp.start()
                sends.append(cp)
        for cp in sends:
            cp.wait_recv()
        for cp in sends:
            cp.wait_send()

    return pl.pallas_call(
        body, in_specs=[_HBM] * n + [pl.BlockSpec(memory_space=pl.ANY)] * n_after, out_specs=[_HBM] * n,
        out_shape=[SDS((chips,) + a.shape[2:], a.dtype) for a in arrays],
        scratch_shapes=[pltpu.SemaphoreType.DMA((n, chips)), pltpu.SemaphoreType.DMA((n, chips))], name=name)(*arrays, *after)


def _sum_pairs(mine, theirs, name):
    n = len(mine)
    chips = mine[0].shape[0]
    c = lax.axis_index("c")

    def body(c_ref, *refs):
        for a in range(n):
            refs[2 * n + a][...] = (refs[a][...].astype(F32) + refs[n + a][...].astype(F32)).astype(refs[2 * n + a].dtype)

    def pair(a):
        return pl.BlockSpec((None, None) + a.shape[2:], lambda q, c_ref: (q, c_ref[0], 0, 0))

    def single(a):
        return pl.BlockSpec((None,) + a.shape[2:], lambda q, c_ref: (q, 0, 0))

    return pl.pallas_call(
        body, grid_spec=pltpu.PrefetchScalarGridSpec(
            num_scalar_prefetch=1, grid=(chips,), in_specs=[pair(a) for a in mine] + [single(a) for a in mine],
            out_specs=[single(a) for a in mine]),
        out_shape=[SDS((chips,) + a.shape[2:], a.dtype) for a in mine], name=name,
        compiler_params=_params(("parallel",)))(c.reshape(1).astype(jnp.int32), *mine, *theirs)


def _adamw_math(w, g, m, v):
    m = ADAM_B1 * m + (1.0 - ADAM_B1) * g
    v = ADAM_B2 * v + (1.0 - ADAM_B2) * (g * g)
    m_hat = m / (1.0 - ADAM_B1 ** ADAM_STEP)
    v_hat = v / (1.0 - ADAM_B2 ** ADAM_STEP)
    delta = -ADAM_LR * (m_hat / (jnp.sqrt(v_hat) + ADAM_EPS) + ADAM_WD * w)
    return delta, m, v


def _adamw(w, m, v, pieces, name):
    rows, cols = w.shape[-2:]
    lead = w.ndim - 2
    tile = rows
    for cand in (256, 176, 128, 64, 16):
        if rows > cand and rows % cand == 0:
            tile = cand
            break

    def body(w_ref, m_ref, v_ref, p_ref, g_ref, d_ref, mo_ref, vo_ref):
        g = _sum_pieces(p_ref)
        g_ref[...] = g
        d_ref[...], mo_ref[...], vo_ref[...] = _adamw_math(w_ref[...], g, m_ref[...], v_ref[...])

    blk = pl.BlockSpec((None,) * lead + (tile, cols), lambda i: (0,) * lead + (i, 0))
    return pl.pallas_call(
        body, grid=(rows // tile,), in_specs=[blk, blk, blk, pl.BlockSpec((pieces.shape[0], tile, cols), lambda i: (0, i, 0))],
        out_specs=[blk] * 4, out_shape=[SDS(w.shape, F32)] * 4, name=name,
        compiler_params=_params(("parallel",)))(w, m, v, pieces)


def _sum_pieces(p_ref):
    g = p_ref[0].astype(F32)
    for p in range(1, p_ref.shape[0]):
        g = g + p_ref[p].astype(F32)
    return g


def _adamw_s5_mat(w, m, v, g, name):
    _, ndir, groups, b, c = w.shape
    per_dir = groups // 8

    def body(w_ref, m_ref, v_ref, g_ref, d_ref, mo_ref, vo_ref):
        d_ref[...], mo_ref[...], vo_ref[...] = _adamw_math(w_ref[...], g_ref[...], m_ref[...], v_ref[...])

    blk = pl.BlockSpec((None, None, 8, b, c), lambda i: (0, i // per_dir, i % per_dir, 0, 0))
    return pl.pallas_call(
        body, grid=(ndir * per_dir,), in_specs=[blk] * 4, out_specs=[blk] * 3, out_shape=[SDS(w.shape, F32)] * 3, name=name,
        compiler_params=_params(("parallel",)))(w, m, v, g)


VEC_ROWS = ['ffn1_pre_g', 'ffn1_post_g', 'mix_pre_g', 'mix_post_g', 'ffn2_pre_g', 'ffn2_post_g', 'final_g',
            ('na_out_g', 's5_out_g'), ('s5_d', 's5_b_glu')]
VEC_NAMES = [n for row in VEC_ROWS for n in ((row,) if isinstance(row, str) else row)]
VEC_PACK_ROWS = 16
LOSS_ROW = len(VEC_ROWS)


def _pack_vectors(grads, loss8):
    def body(*refs):
        o_ref = refs[-1]
        o_ref[...] = jnp.zeros_like(o_ref)
        o_ref[LOSS_ROW:LOSS_ROW + 1, 0:128] = refs[-2][0:1, :]
        k = 0
        for i, row in enumerate(VEC_ROWS):
            if isinstance(row, str):
                o_ref[i:i + 1, :] = refs[k][...]
                k += 1
            else:
                o_ref[i:i + 1, 0:NA_WIDTH] = refs[k][...]
                o_ref[i:i + 1, NA_WIDTH:] = refs[k + 1][...]
                k += 2

    return pl.pallas_call(body, out_shape=SDS((VEC_PACK_ROWS, D_MODEL), F32), name="pack_vectors",
                          compiler_params=_params())(*[grads[n] for n in VEC_NAMES], loss8)


def _sum8(pieces, name):
    def body(p_ref, o_ref):
        o_ref[...] = _sum_pieces(p_ref)

    return pl.pallas_call(body, out_shape=SDS(pieces.shape[1:], F32), name=name, compiler_params=_params())(pieces)


def _adamw_small(packed8, vec_wmv, others):
    n_vec, n_oth = len(VEC_NAMES), len(others)

    def body(*refs):
        p_ref = refs[0]
        ins = refs[1:1 + 3 * n_vec + 4 * n_oth]
        outs = refs[1 + 3 * n_vec + 4 * n_oth:]
        gsum = _sum_pieces(p_ref)
        outs[-1][...] = gsum[LOSS_ROW:LOSS_ROW + 1, 0:128]
        k = 0
        for i, row in enumerate(VEC_ROWS):
            parts = [(row, gsum[i:i + 1, :])] if isinstance(row, str) else \
                [(row[0], gsum[i:i + 1, 0:NA_WIDTH]), (row[1], gsum[i:i + 1, NA_WIDTH:])]
            for _, g in parts:
                w_ref, m_ref, v_ref = ins[3 * k:3 * k + 3]
                outs[4 * k][...] = g
                outs[4 * k + 1][...], outs[4 * k + 2][...], outs[4 * k + 3][...] = _adamw_math(w_ref[...], g, m_ref[...], v_ref[...])
                k += 1
        for j in range(n_oth):
            w_ref, m_ref, v_ref, g_ref = ins[3 * n_vec + 4 * j:3 * n_vec + 4 * j + 4]
            g = _sum_pieces(g_ref)
            g = g[tuple(slice(0, s) for s in w_ref.shape[1:])].reshape(w_ref.shape)
            o = outs[4 * (n_vec + j):4 * (n_vec + j) + 4]
            o[0][...] = g
            o[1][...], o[2][...], o[3][...] = _adamw_math(w_ref[...], g, m_ref[...], v_ref[...])

    args, out_shape = [packed8], []
    for w, m, v in vec_wmv:
        args += [w, m, v]
        out_shape += [SDS(w.shape, F32)] * 4
    for w, m, v, g in others:
        args += [w, m, v, g]
        out_shape += [SDS(w.shape, F32)] * 4
    out_shape += [SDS((1, 128), F32)]
    return pl.pallas_call(body, out_shape=out_shape, name="adamw_small", compiler_params=_params())(*args)


def _perm_rows(x):
    return x.reshape(SCAN_BLOCKS, SCAN_T, x.shape[-1]).transpose(1, 0, 2).reshape(SEQ, x.shape[-1])


def _unperm_rows(x):
    return x.reshape(SCAN_T, SCAN_BLOCKS, x.shape[-1]).transpose(1, 0, 2).reshape(SEQ, x.shape[-1])


def _block_diag(x):
    eye = np.eye(8, dtype=bool)[None, None, :, None, :, None]
    full = jnp.where(eye, x[:, :, :, :, None, :], 0.0)
    return full.reshape(2, S5_CHUNKS, 8 * x.shape[3], 8 * x.shape[4])


STORED_SWAPPED = {"ffn1_w_gate": (1, 2), "ffn1_w_up": (1, 2), "ffn2_w_gate": (1, 2), "ffn2_w_up": (1, 2),
                  "s5_b_re": (3, 4), "s5_b_im": (3, 4)}


def _stored(name, x):
    return jnp.swapaxes(x, *STORED_SWAPPED[name]) if name in STORED_SWAPPED else x


def _dep(x, token):
    return x if token is None else x + token


def _local_step(x, target, get_w, small, emit):
    bias = _rpb_expand(small["na_rpb"][0])
    lr = small["s5_lam_re"].reshape(64, S5_STATE)
    li = small["s5_lam_im"].reshape(64, S5_STATE)
    logdt = small["s5_log_dt"].reshape(64, 1)
    b_t = [_stored(n, small[n]).reshape(64, S5_GROUP, S5_STATE) for n in ("s5_b_re", "s5_b_im")]
    lbr, lbi, bbr, bbi = _s5_prep(lr, li, logdt, b_t[0], b_t[1])
    are = lbr.reshape(2, S5_CHUNKS, 1, ST_W)
    aim = lbi.reshape(2, S5_CHUNKS, 1, ST_W)
    bre = _block_diag(bbr.reshape(2, S5_CHUNKS, 8, S5_GROUP, S5_STATE)).astype(BF16)
    bim = _block_diag(bbi.reshape(2, S5_CHUNKS, 8, S5_GROUP, S5_STATE)).astype(BF16)
    c_t = [small[n].reshape(2, S5_CHUNKS, 8, S5_GROUP, S5_STATE).transpose(0, 1, 2, 4, 3) for n in ("s5_c_re", "s5_c_im")]
    cre = _block_diag(c_t[0]).astype(BF16)
    cim = _block_diag(c_t[1]).astype(BF16)
    tgt = jnp.concatenate([jnp.zeros((N_META, D_MODEL), F32), target], axis=0)

    h0, a1 = _embed_prenorm(get_w("meta", None)["meta_tokens"], x, small["ffn1_pre_g"])
    wts = dict(get_w("ffn1", [bias, are, aim, bre, bim, cre, cim, tgt, a1]))
    gate1, up1, f1 = _ffn_fwd(a1, wts["ffn1_w_gate"], wts["ffn1_w_up"], wts["ffn1_w_down"], "ffn1_fwd",
                              after=wts.get("tokens", ()))
    h1, a2 = _post_pre(f1, h0, small["ffn1_post_g"], small["mix_pre_g"], 0.5, "post_pre1")
    wts.update(get_w("w_in", a2))
    qkv = _proj_heads(a2, wts["w_in"])
    u = _proj_u(a2, wts["w_in"])
    ona = _na_fwd(qkv, bias)
    u_p = _perm_rows(u)
    sr, si, y2 = _s5_scan_fwd(u_p, bre, bim, are, aim, cre, cim)
    wts.update(get_w("mix", y2))
    os5_p, ypre_p = _s5_glu_fwd(u_p, y2, small["s5_d"], wts["s5_w_glu"], small["s5_b_glu"])
    os5 = _unperm_rows(os5_p)

    mix = _mix_out_fwd(ona, os5, small["na_out_g"], small["s5_out_g"], wts["w_out"])
    h2, a3 = _post_pre(mix, h1, small["mix_post_g"], small["ffn2_pre_g"], 1.0, "post_pre2")
    wts.update(get_w("ffn2", a3))
    gate2, up2, f2 = _ffn_fwd(a3, wts["ffn2_w_gate"], wts["ffn2_w_up"], wts["ffn2_w_down"], "ffn2_fwd")
    loss8, dh3, df2, g_final, g_ffn2_post = _final_loss(f2, h2, small["ffn2_post_g"], small["final_g"], tgt)

    da3, dwg2, dwu2, dwd2 = _ffn_bwd(df2, a3, gate2, up2, wts["ffn2_w_gate"], wts["ffn2_w_up"], wts["ffn2_w_down"], "ffn2_bwd")
    tok = emit("ffn2", {"ffn2_w_gate": dwg2, "ffn2_w_up": dwu2, "ffn2_w_down": dwd2})
    dh2, dmix, g_ffn2_pre, g_mix_post = _bwd_pre_post(da3, h2, _dep(small["ffn2_pre_g"], tok), dh3, mix, small["mix_post_g"], 1.0,
                                                      "bwd_pre_post2")
    dona, dos5, dwout, g_na_out, g_s5_out = _mix_out_bwd(dmix, ona, os5, small["na_out_g"], small["s5_out_g"], wts["w_out"])

    dypre_p, du_skip_p, dwglu, g_b_glu, g_s5_d = _s5_glu_bwd(_perm_rows(dos5), ypre_p, u_p, small["s5_d"], wts["s5_w_glu"],
                                                             small["s5_b_glu"])
    tok = emit("mix", {"s5_w_glu": dwglu.reshape(N_DEV, S5_WIDTH // N_DEV, S5_WIDTH).astype(BF16),
                       "w_out": dwout.reshape(N_DEV, D_MODEL // N_DEV, D_MODEL).astype(BF16)})
    du_p, dbr, dbi, dcr, dci, dar, dai = _s5_scan_bwd(dypre_p, du_skip_p, u_p, sr, si, bre, bim, _dep(are, tok), aim, cre, cim)
    du = _unperm_rows(du_p)
    per_group = (2 * S5_GROUPS, S5_GROUP, S5_STATE)
    g_lr, g_li, g_dt, g_br, g_bi = _s5_prep_bwd(lr, li, logdt, b_t[0], b_t[1], dar.reshape(64, S5_STATE),
                                                dai.reshape(64, S5_STATE), dbr.reshape(per_group), dbi.reshape(per_group))
    g_c = [dcr.reshape(per_group), dci.reshape(per_group)]

    dq, dk, dv, dbias = _na_bwd(qkv, bias, dona)
    g_rpb = _rpb_reduce(dbias)
    dense = jnp.stack([g.reshape(2 * S5_GROUPS, S5_STATE * S5_GROUP) for g in (g_br, g_bi, *g_c)])
    tok = emit("small", {"dense": dense, "na_rpb": g_rpb,
                         "s5_lam_re": g_lr.reshape(2, S5_GROUPS, S5_STATE), "s5_lam_im": g_li.reshape(2, S5_GROUPS, S5_STATE),
                         "s5_log_dt": g_dt.reshape(2, S5_GROUPS)})
    da2, dwin = _proj_bwd(dq, dk, dv, du, a2, wts["w_in"])
    tok2 = emit("w_in", {"w_in": dwin})
    tok = tok if tok2 is None else tok + tok2
    dh1, df1, g_mix_pre, g_ffn1_post = _bwd_pre_post(da2, h1, _dep(small["mix_pre_g"], tok), dh2, f1, small["ffn1_post_g"], 0.5,
                                                     "bwd_pre_post1")
    da1, dwg1, dwu1, dwd1 = _ffn_bwd(df1, a1, gate1, up1, wts["ffn1_w_gate"], wts["ffn1_w_up"], wts["ffn1_w_down"], "ffn1_bwd")
    grad_x, grad_meta, g_ffn1_pre = _bwd_embed(da1, h0, small["ffn1_pre_g"], dh1)
    vec_g = {
        "ffn1_pre_g": g_ffn1_pre, "ffn1_post_g": g_ffn1_post, "mix_pre_g": g_mix_pre, "s5_d": g_s5_d, "s5_b_glu": g_b_glu,
        "na_out_g": g_na_out, "s5_out_g": g_s5_out, "mix_post_g": g_mix_post,
        "ffn2_pre_g": g_ffn2_pre, "ffn2_post_g": g_ffn2_post, "final_g": g_final,
    }
    emit("vec", {"packed": _pack_vectors(vec_g, loss8), "meta_tokens": grad_meta})
    emit("ffn1", {"ffn1_w_gate": dwg1, "ffn1_w_up": dwu1, "ffn1_w_down": dwd1})
    return grad_x


WEIGHT_NAMES = ['meta_tokens', 'ffn1_pre_g', 'ffn1_post_g', 'ffn1_w_gate', 'ffn1_w_up', 'ffn1_w_down', 'mix_pre_g', 'w_in',
                'na_rpb', 's5_lam_re', 's5_lam_im', 's5_log_dt', 's5_b_re', 's5_b_im', 's5_c_re', 's5_c_im', 's5_d',
                's5_w_glu', 's5_b_glu', 'na_out_g', 's5_out_g', 'w_out', 'mix_post_g', 'ffn2_pre_g', 'ffn2_post_g',
                'ffn2_w_gate', 'ffn2_w_up', 'ffn2_w_down', 'final_g']
BIG_NAMES = ['ffn1_w_gate', 'ffn1_w_up', 'ffn1_w_down', 'w_in', 's5_w_glu', 'w_out', 'ffn2_w_gate', 'ffn2_w_up', 'ffn2_w_down']
SMALL_NAMES = [n for n in WEIGHT_NAMES if n not in BIG_NAMES and n != 'meta_tokens']
WHOLE_NAMES = ['na_rpb', 's5_lam_re', 's5_lam_im', 's5_log_dt']
LEAD_NAMES = ['s5_b_re', 's5_b_im', 's5_c_re', 's5_c_im']


def kernel(x, meta_tokens, ffn1_pre_g, ffn1_post_g, ffn1_w_gate, ffn1_w_up, ffn1_w_down, mix_pre_g, w_in, na_rpb, s5_lam_re, s5_lam_im, s5_log_dt, s5_b_re, s5_b_im, s5_c_re, s5_c_im, s5_d, s5_w_glu, s5_b_glu, na_out_g, s5_out_g, w_out, mix_post_g, ffn2_pre_g, ffn2_post_g, ffn2_w_gate, ffn2_w_up, ffn2_w_down, final_g, loss_target, m_meta_tokens, m_ffn1_pre_g, m_ffn1_post_g, m_ffn1_w_gate, m_ffn1_w_up, m_ffn1_w_down, m_mix_pre_g, m_w_in, m_na_rpb, m_s5_lam_re, m_s5_lam_im, m_s5_log_dt, m_s5_b_re, m_s5_b_im, m_s5_c_re, m_s5_c_im, m_s5_d, m_s5_w_glu, m_s5_b_glu, m_na_out_g, m_s5_out_g, m_w_out, m_mix_post_g, m_ffn2_pre_g, m_ffn2_post_g, m_ffn2_w_gate, m_ffn2_w_up, m_ffn2_w_down, m_final_g, v_meta_tokens, v_ffn1_pre_g, v_ffn1_post_g, v_ffn1_w_gate, v_ffn1_w_up, v_ffn1_w_down, v_mix_pre_g, v_w_in, v_na_rpb, v_s5_lam_re, v_s5_lam_im, v_s5_log_dt, v_s5_b_re, v_s5_b_im, v_s5_c_re, v_s5_c_im, v_s5_d, v_s5_w_glu, v_s5_b_glu, v_na_out_g, v_s5_out_g, v_w_out, v_mix_post_g, v_ffn2_pre_g, v_ffn2_post_g, v_ffn2_w_gate, v_ffn2_w_up, v_ffn2_w_down, v_final_g):
    args = dict(locals())
    w = {n: args[n] for n in WEIGHT_NAMES}
    m = {n: args["m_" + n] for n in WEIGHT_NAMES}
    v = {n: args["v_" + n] for n in WEIGHT_NAMES}

    small = {n: w[n] for n in SMALL_NAMES}

    pending = {}

    def start(group, names, arrays, gather, peers=ALL_PEERS, slot=_slot8):
        n_slots = N_DEV if slot is _slot8 else N_DEV // 2
        lands = [lax.empty((n_slots,) + a.shape if gather else a.shape, a.dtype) for a in arrays]
        send_sems, recv_sems, arrays, lands, token = _exchange_start(arrays, lands, gather, "start_" + group, peers, slot)
        pending[group] = (names, send_sems, recv_sems, arrays, lands, gather, peers, slot)
        return token

    def finish(group, after):
        names, send_sems, recv_sems, arrays, lands, gather, peers, slot = pending.pop(group)
        lands, token = _exchange_wait(send_sems, recv_sems, arrays, lands, after, gather, "wait_" + group, peers, slot)
        return dict(zip(names, lands)), token

    first = ["ffn1_w_gate", "ffn1_w_up", "ffn1_w_down"]
    def shard(n, token=None):
        return _dep(_stored(n, w[n])[0], None if token is None else token[0, 0]).astype(BF16)

    ffn_names = ("ffn1_w_gate", "ffn1_w_up", "ffn1_w_down", "ffn2_w_gate", "ffn2_w_up", "ffn2_w_down")
    later_groups = (("w_in", ["w_in"]), ("mix", ["s5_w_glu", "w_out"]), ("ffn2", ["ffn2_w_gate", "ffn2_w_up", "ffn2_w_down"]))
    token0 = start("meta", ["meta_tokens"], [w["meta_tokens"]], True)
    token1 = start("ffn1", first, [shard(n, token0) for n in first], True, (SIBLING,) + CHIP_PEERS)
    meta_full = finish("meta", [token1])[0]["meta_tokens"].transpose(1, 0, 2).reshape(N_META, D_MODEL)
    later_shards = {n: shard(n, token1) for _, names in later_groups for n in names}
    for n in ("na_rpb", "s5_lam_re"):
        small[n] = _dep(small[n], token1[0, 0])

    def get_w(group, after):
        if group == "meta":
            return {"meta_tokens": meta_full}
        if group == "ffn1":
            after = list(after) + list(later_shards.values())
        got, token = finish(group, after)
        if group == "ffn1":
            got = dict(zip(got, _forward_sibling(list(got.values()), "forward_ffn1")))
            got["tokens"] = [start(g, names + ["order"], [later_shards[n] for n in names] + [token], True) for g, names in later_groups]
        if group == "mix":
            got = {"s5_w_glu": got["s5_w_glu"].reshape(S5_WIDTH, S5_WIDTH), "w_out": got["w_out"].reshape(D_MODEL, D_MODEL)}
        return {n: (a.reshape(D_FF, D_MODEL) if n in ffn_names else a) for n, a in got.items()}

    tokens = {}

    def emit(group, grads):
        grads = {n: (g.reshape(N_DEV, FF_SHARD, D_MODEL) if n in ffn_names else g) for n, g in grads.items()}
        if group == "ffn1":
            mine = [g.reshape((N_DEV // 2, 2) + g.shape[1:]) for g in grads.values()]
            theirs = _swap_sibling(mine, "swap_g_ffn1", after=[tokens["vec"]])
            sums = _sum_pairs(mine, theirs, "pair_sum_g_ffn1")
            tokens[group] = start("g_ffn1", list(grads), sums, False, CHIP_PEERS, _slot4)
        else:
            tokens[group] = start("g_" + group, list(grads), list(grads.values()), group in ("small", "vec"))
        return tokens[group][0, 0]

    grad_x = _local_step(x[0], loss_target[0], get_w, small, emit)
    res = {}

    def update_shard(n, pieces):
        outs = _adamw(_stored(n, w[n]), _stored(n, m[n]), _stored(n, v[n]), pieces, "adamw_" + n)
        res[n] = [_stored(n, o) for o in outs]

    late = [grad_x, tokens["ffn1"]]
    for group in ("g_ffn2", "g_mix", "g_w_in"):
        for n, pieces in finish(group, late)[0].items():
            update_shard(n, pieces)
    g8 = finish("g_small", late)[0]
    dense = _sum8(g8["dense"], "sum_dense")
    for i, n in enumerate(LEAD_NAMES):
        g = dense[i].reshape(_stored(n, w[n]).shape)
        upd = _adamw_s5_mat(_stored(n, w[n]), _stored(n, m[n]), _stored(n, v[n]), g, "adamw_" + n)
        res[n] = [_stored(n, o) for o in [g] + list(upd)]

    done = [res[n][1] for n in ("ffn2_w_gate", "ffn2_w_up", "ffn2_w_down", "w_in", "w_out", "s5_w_glu") + tuple(LEAD_NAMES)]
    got = finish("g_vec", done)[0]
    packed8, gmeta8 = got["packed"], got["meta_tokens"]
    for n, pieces in finish("g_ffn1", packed8)[0].items():
        update_shard(n, pieces)
    _, _, _, me = _me()
    update_shard("meta_tokens", lax.dynamic_slice_in_dim(gmeta8, me * (D_MODEL // N_DEV), D_MODEL // N_DEV, axis=2))

    outs = _adamw_small(packed8, [(w[n], m[n], v[n]) for n in VEC_NAMES], [(w[n], m[n], v[n], g8[n]) for n in WHOLE_NAMES])
    for i, n in enumerate(VEC_NAMES + WHOLE_NAMES):
        res[n] = list(outs[4 * i:4 * i + 4])

    out = [outs[-1][0, 0], grad_x[None]]
    for kind in range(4):
        out += [res[n][kind] for n in WEIGHT_NAMES]
    return tuple(out)
```

```python
import math

import numpy as np
import jax
import jax.numpy as jnp
from jax import lax
from jax.experimental import pallas as pl
from jax.experimental.pallas import tpu as pltpu

F32 = jnp.float32
BF16 = jnp.bfloat16
SDS = jax.ShapeDtypeStruct

D_MODEL = 1024
N_TOK = 2048
N_META = 16
SEQ = N_TOK + N_META
ROW_TILE = 688
N_ROW_TILES = SEQ // ROW_TILE
N_DEV = 8
D_FF = 2816
FF_SHARD = D_FF // N_DEV
FF_TILE = 256
IN_SHARD = 256
NA_WIDTH = 512
S5_WIDTH = 512
HEADS = 8
HEAD_DIM = 64
GRID_W = 64
GRID_ROWS = N_TOK // GRID_W
KH = 8
KW = 16
NA_RB = 4
NA_KR = KH + NA_RB - 1
NA_BLOCKS = GRID_ROWS // NA_RB
NA_QB = NA_RB * GRID_W
NA_KB = NA_KR * GRID_W
NA_TYPES = 3
S5_GROUPS = 32
S5_GROUP = 16
S5_STATE = 64
S5_CHUNKS = 4
CH_W = S5_WIDTH // S5_CHUNKS
ST_W = S5_GROUPS * S5_STATE // S5_CHUNKS
SCAN_BLOCKS = 8
SCAN_T = SEQ // SCAN_BLOCKS
RMS_EPS = 1e-6
NEG_INF = -1e30
ATT_SCALE = HEAD_DIM ** -0.5
ADAM_LR, ADAM_B1, ADAM_B2, ADAM_EPS, ADAM_WD, ADAM_STEP = 0.001, 0.9, 0.999, 1e-08, 0.01, 10
VMEM_LIMIT = 56 * 1024 * 1024
MESH = pl.DeviceIdType.MESH


def _params(sem=None):
    return pltpu.CompilerParams(dimension_semantics=sem, vmem_limit_bytes=VMEM_LIMIT)


def _dot(a, b):
    return jnp.dot(a, b, preferred_element_type=F32)


def _dot_nt(a, b):
    return lax.dot_general(a, b, (((1,), (1,)), ((), ())), preferred_element_type=F32)


def _dot_tn(a, b):
    return lax.dot_general(a, b, (((0,), (0,)), ((), ())), preferred_element_type=F32)


def _rstd(x):
    return lax.rsqrt(jnp.mean(x * x, axis=-1, keepdims=True) + RMS_EPS)


def _rms_bwd(x, r, g, dy):
    dyg = dy * g
    xr = x * r
    dx = r * (dyg - xr * jnp.mean(dyg * xr, axis=-1, keepdims=True))
    return dx, dy * xr


def _rows(i, size=ROW_TILE):
    return pl.ds(i * size if isinstance(i, int) else pl.multiple_of(i * size, 16), size)


def _row_spec(width):
    return pl.BlockSpec((ROW_TILE, width), lambda i: (i, 0))


def _fix_spec(shape):
    return pl.BlockSpec(shape, lambda i: (0,) * len(shape))


def _split3(x):
    hi = x.astype(BF16)
    r1 = x - hi.astype(F32)
    mid = r1.astype(BF16)
    lo = (r1 - mid.astype(F32)).astype(BF16)
    return hi, mid, lo


def _embed_prenorm(meta, x, g):
    def body(m_ref, x_ref, g_ref, h_ref, a_ref):
        h_ref[0:N_META, :] = m_ref[...]
        h_ref[N_META:, :] = x_ref[...]
        for i in range(N_ROW_TILES):
            rows = slice(i * ROW_TILE, (i + 1) * ROW_TILE)
            hv = h_ref[rows, :]
            a_ref[rows, :] = (hv * _rstd(hv) * g_ref[...]).astype(BF16)

    return pl.pallas_call(
        body, out_shape=[SDS((SEQ, D_MODEL), F32), SDS((SEQ, D_MODEL), BF16)], name="embed_prenorm",
        compiler_params=_params())(meta, x, g)


def _post_pre(f, hres, g_post, g_next, scale, name):
    def body(f_ref, h_ref, gp_ref, gn_ref, ho_ref, a_ref):
        fv = f_ref[...]
        h = h_ref[...] + scale * (fv * _rstd(fv) * gp_ref[...])
        ho_ref[...] = h
        a_ref[...] = (h * _rstd(h) * gn_ref[...]).astype(BF16)

    return pl.pallas_call(
        body, grid=(N_ROW_TILES,),
        in_specs=[_row_spec(D_MODEL), _row_spec(D_MODEL), _fix_spec((1, D_MODEL)), _fix_spec((1, D_MODEL))],
        out_specs=[_row_spec(D_MODEL), _row_spec(D_MODEL)],
        out_shape=[SDS((SEQ, D_MODEL), F32), SDS((SEQ, D_MODEL), BF16)], name=name,
        compiler_params=_params(("parallel",)))(f, hres, g_post, g_next)


def _final_loss(f2, h2, g_post, g_final, target):
    def body(f_ref, h_ref, gp_ref, gf_ref, t_ref, loss_ref, dh_ref, df_ref, dgf_ref, dgp_ref):
        i = pl.program_id(0)
        fv = f_ref[...]
        r1 = _rstd(fv)
        gp = gp_ref[...]
        h3 = h_ref[...] + 0.5 * (fv * r1 * gp)
        r2 = _rstd(h3)
        gf = gf_ref[...]
        y = h3 * r2 * gf
        row = lax.broadcasted_iota(jnp.int32, (ROW_TILE, 1), 0) + i * ROW_TILE
        err = jnp.where(row >= N_META, y - t_ref[...], 0.0)
        part = 0.5 * jnp.sum(jnp.mean(err * err, axis=-1, keepdims=True))
        dy = err * (1.0 / D_MODEL)
        dh3, dgf = _rms_bwd(h3, r2, gf, dy)
        dh_ref[...] = dh3
        df, dgp = _rms_bwd(fv, r1, gp, 0.5 * dh3)
        df_ref[...] = df.astype(BF16)

        @pl.when(i == 0)
        def _():
            loss_ref[...] = jnp.zeros_like(loss_ref)
            dgf_ref[...] = jnp.zeros_like(dgf_ref)
            dgp_ref[...] = jnp.zeros_like(dgp_ref)

        loss_ref[...] += part
        dgf_ref[...] += jnp.sum(dgf, axis=0, keepdims=True)
        dgp_ref[...] += jnp.sum(dgp, axis=0, keepdims=True)

    gain = _fix_spec((1, D_MODEL))
    return pl.pallas_call(
        body, grid=(N_ROW_TILES,),
        in_specs=[_row_spec(D_MODEL), _row_spec(D_MODEL), gain, gain, _row_spec(D_MODEL)],
        out_specs=[_fix_spec((8, 128)), _row_spec(D_MODEL), _row_spec(D_MODEL), gain, gain],
        out_shape=[SDS((8, 128), F32), SDS((SEQ, D_MODEL), F32), SDS((SEQ, D_MODEL), BF16),
                   SDS((1, D_MODEL), F32), SDS((1, D_MODEL), F32)],
        name="final_loss", compiler_params=_params(("arbitrary",)))(f2, h2, g_post, g_final, target)


def _bwd_pre_post(da, h, g_pre, dh_res, fprev, g_post, scale, name):
    def body(da_ref, h_ref, gpre_ref, dhr_ref, f_ref, gpost_ref, dh_ref, df_ref, dgpre_ref, dgpost_ref):
        i = pl.program_id(0)
        hv = h_ref[...]
        dxa, dgpre = _rms_bwd(hv, _rstd(hv), gpre_ref[...], da_ref[...])
        dh = dhr_ref[...] + dxa
        dh_ref[...] = dh
        fv = f_ref[...]
        df, dgpost = _rms_bwd(fv, _rstd(fv), gpost_ref[...], scale * dh)
        df_ref[...] = df.astype(BF16)

        @pl.when(i == 0)
        def _():
            dgpre_ref[...] = jnp.zeros_like(dgpre_ref)
            dgpost_ref[...] = jnp.zeros_like(dgpost_ref)

        dgpre_ref[...] += jnp.sum(dgpre, axis=0, keepdims=True)
        dgpost_ref[...] += jnp.sum(dgpost, axis=0, keepdims=True)

    gain = _fix_spec((1, D_MODEL))
    row = _row_spec(D_MODEL)
    return pl.pallas_call(
        body, grid=(N_ROW_TILES,), in_specs=[row, row, gain, row, row, gain],
        out_specs=[row, row, gain, gain],
        out_shape=[SDS((SEQ, D_MODEL), F32), SDS((SEQ, D_MODEL), BF16), SDS((1, D_MODEL), F32), SDS((1, D_MODEL), F32)],
        name=name, compiler_params=_params(("arbitrary",)))(da, h, g_pre, dh_res, fprev, g_post)


def _bwd_embed(da, h, g_pre, dh_res):
    def body(da_ref, h_ref, gpre_ref, dhr_ref, gx_ref, gm_ref, dgpre_ref):
        total = jnp.zeros((1, D_MODEL), F32)
        for i in range(N_ROW_TILES):
            rows = slice(i * ROW_TILE, (i + 1) * ROW_TILE)
            hv = h_ref[rows, :]
            dxa, dgpre = _rms_bwd(hv, _rstd(hv), gpre_ref[...], da_ref[rows, :])
            dh = dhr_ref[rows, :] + dxa
            total = total + jnp.sum(dgpre, axis=0, keepdims=True)
            if i == 0:
                gm_ref[...] = dh[0:N_META, :]
                gx_ref[0:ROW_TILE - N_META, :] = dh[N_META:, :]
            else:
                gx_ref[i * ROW_TILE - N_META:(i + 1) * ROW_TILE - N_META, :] = dh
        dgpre_ref[...] = total

    return pl.pallas_call(
        body, out_shape=[SDS((N_TOK, D_MODEL), F32), SDS((N_META, D_MODEL), F32), SDS((1, D_MODEL), F32)],
        name="bwd_embed", compiler_params=_params())(da, h, g_pre, dh_res)


def _ffn_fwd(a, wg, wu, wd, name, after=()):
    def body(a_ref, wg_ref, wu_ref, wd_ref, *rest):
        gate_ref, up_ref, f_ref = rest[len(after):]
        j = pl.program_id(0)

        def tile(i, carry):
            rows = _rows(i)
            at = a_ref[rows, :]
            gate = _dot_nt(at, wg_ref[...])
            up = _dot_nt(at, wu_ref[...])
            gate_ref[rows, :] = gate.astype(BF16)
            up_ref[rows, :] = up.astype(BF16)
            act = (gate * jax.nn.sigmoid(gate) * up).astype(BF16)
            contrib = _dot(act, wd_ref[...])

            @pl.when(j == 0)
            def _():
                f_ref[rows, :] = contrib

            @pl.when(j != 0)
            def _():
                f_ref[rows, :] += contrib

            return carry

        for i in range(N_ROW_TILES):
            tile(i, 0)

    wtile = pl.BlockSpec((FF_TILE, D_MODEL), lambda j: (j, 0))
    hid = pl.BlockSpec((SEQ, FF_TILE), lambda j: (0, j))
    full = pl.BlockSpec((SEQ, D_MODEL), lambda j: (0, 0))
    return pl.pallas_call(
        body, grid=(D_FF // FF_TILE,), in_specs=[full, wtile, wtile, wtile] + [pl.BlockSpec(memory_space=pl.ANY)] * len(after),
        out_specs=[hid, hid, full],
        out_shape=[SDS((SEQ, D_FF), BF16), SDS((SEQ, D_FF), BF16), SDS((SEQ, D_MODEL), F32)],
        name=name, compiler_params=_params(("arbitrary",)))(a, wg, wu, wd, *after)


def _ffn_bwd(df, a, gate, up, wg, wu, wd, name):
    def body(df_ref, a_ref, gate_ref, up_ref, wg_ref, wu_ref, wd_ref, da_ref, dwg_ref, dwu_ref, dwd_ref,
             acc_g, acc_u, acc_d):
        j = pl.program_id(0)

        def tile(i, carry):
            rows = _rows(i)
            dft = df_ref[rows, :]
            at = a_ref[rows, :]
            gate = gate_ref[rows, :].astype(F32)
            up = up_ref[rows, :].astype(F32)
            dact = _dot_nt(dft, wd_ref[...])
            sig = jax.nn.sigmoid(gate)
            silu = gate * sig
            dgate = (dact * up * (sig * (1.0 + gate * (1.0 - sig)))).astype(BF16)
            dup = (dact * silu).astype(BF16)
            act = (silu * up).astype(BF16)
            dwd = _dot_tn(act, dft)
            dwg = _dot_tn(dgate, at)
            dwu = _dot_tn(dup, at)
            dat = _dot(dgate, wg_ref[...]) + _dot(dup, wu_ref[...])

            @pl.when(i == 0)
            def _():
                acc_d[...] = dwd
                acc_g[...] = dwg
                acc_u[...] = dwu

            @pl.when(i != 0)
            def _():
                acc_d[...] += dwd
                acc_g[...] += dwg
                acc_u[...] += dwu

            @pl.when(j == 0)
            def _():
                da_ref[rows, :] = dat

            @pl.when(j != 0)
            def _():
                da_ref[rows, :] += dat

            return carry

        for i in range(N_ROW_TILES):
            tile(i, 0)
        dwg_ref[...] = acc_g[...].astype(BF16)
        dwu_ref[...] = acc_u[...].astype(BF16)
        dwd_ref[...] = acc_d[...].astype(BF16)

    wtile = pl.BlockSpec((FF_TILE, D_MODEL), lambda j: (j, 0))
    hid = pl.BlockSpec((SEQ, FF_TILE), lambda j: (0, j))
    full = pl.BlockSpec((SEQ, D_MODEL), lambda j: (0, 0))
    return pl.pallas_call(
        body, grid=(D_FF // FF_TILE,), in_specs=[full, full, hid, hid, wtile, wtile, wtile],
        out_specs=[full, wtile, wtile, wtile],
        out_shape=[SDS((SEQ, D_MODEL), F32)] + [SDS((D_FF, D_MODEL), BF16)] * 3,
        scratch_shapes=[pltpu.VMEM((FF_TILE, D_MODEL), F32)] * 3,
        name=name, compiler_params=_params(("arbitrary",)))(df, a, gate, up, wg, wu, wd)


HEADS_PER_BLOCK = IN_SHARD // HEAD_DIM
QKV_BLOCKS = 3 * NA_WIDTH // IN_SHARD


def _proj_heads(a, w):
    def body(a_ref, w_ref, o_ref):
        def tile(i, carry):
            rows = _rows(i)
            res = _dot(a_ref[rows, :], w_ref[...])
            for sub in range(HEADS_PER_BLOCK):
                o_ref[sub, rows, :] = res[:, sub * HEAD_DIM:(sub + 1) * HEAD_DIM]
            return carry

        for i in range(N_ROW_TILES):
            tile(i, 0)

    return pl.pallas_call(
        body, grid=(QKV_BLOCKS,),
        in_specs=[pl.BlockSpec((SEQ, D_MODEL), lambda j: (0, 0)), pl.BlockSpec((None, D_MODEL, IN_SHARD), lambda j: (j, 0, 0))],
        out_specs=pl.BlockSpec((HEADS_PER_BLOCK, SEQ, HEAD_DIM), lambda j: (j, 0, 0)),
        out_shape=SDS((3 * HEADS, SEQ, HEAD_DIM), F32), name="proj_heads",
        compiler_params=_params(("parallel",)))(a, w)


def _proj_u(a, w):
    def body(a_ref, w_ref, o_ref):
        def tile(i, carry):
            rows = _rows(i)
            o_ref[rows, :] = _dot(a_ref[rows, :], w_ref[...])
            return carry

        for i in range(N_ROW_TILES):
            tile(i, 0)

    return pl.pallas_call(
        body, grid=(N_DEV - QKV_BLOCKS,),
        in_specs=[pl.BlockSpec((SEQ, D_MODEL), lambda j: (0, 0)),
                  pl.BlockSpec((None, D_MODEL, IN_SHARD), lambda j: (j + QKV_BLOCKS, 0, 0))],
        out_specs=pl.BlockSpec((SEQ, IN_SHARD), lambda j: (0, j)),
        out_shape=SDS((SEQ, S5_WIDTH), F32), name="proj_u",
        compiler_params=_params(("parallel",)))(a, w)


def _proj_bwd(dq, dk, dv, du, a, w):
    def body(dq_ref, dk_ref, dv_ref, du_ref, a_ref, w_ref, da_ref, dw_ref, acc, dp_ref):
        j = pl.program_id(0)

        for which, src in enumerate((dq_ref, dk_ref, dv_ref)):
            @pl.when((j >= 2 * which) & (j < 2 * which + 2))
            def _(src=src):
                dp_ref[...] = jnp.concatenate([src[sub] for sub in range(HEADS_PER_BLOCK)], axis=-1).astype(BF16)

        @pl.when(j >= QKV_BLOCKS)
        def _():
            dp_ref[...] = du_ref[...].astype(BF16)

        def tile(i, carry):
            rows = _rows(i)
            dpt = dp_ref[rows, :]
            dw = _dot_tn(a_ref[rows, :], dpt)
            dat = _dot_nt(dpt, w_ref[...])

            @pl.when(i == 0)
            def _():
                acc[...] = dw

            @pl.when(i != 0)
            def _():
                acc[...] += dw

            @pl.when(j == 0)
            def _():
                da_ref[rows, :] = dat

            @pl.when(j != 0)
            def _():
                da_ref[rows, :] += dat

            return carry

        for i in range(N_ROW_TILES):
            tile(i, 0)
        dw_ref[...] = acc[...].astype(BF16)

    full = pl.BlockSpec((SEQ, D_MODEL), lambda j: (0, 0))
    wspec = pl.BlockSpec((None, D_MODEL, IN_SHARD), lambda j: (j, 0, 0))

    def heads(which):
        return pl.BlockSpec((HEADS_PER_BLOCK, SEQ, HEAD_DIM), lambda j: (jnp.clip(j - 2 * which, 0, 1), 0, 0))

    return pl.pallas_call(
        body, grid=(N_DEV,),
        in_specs=[heads(0), heads(1), heads(2),
                  pl.BlockSpec((SEQ, IN_SHARD), lambda j: (0, jnp.clip(j - QKV_BLOCKS, 0, 1))), full, wspec],
        out_specs=[full, wspec],
        out_shape=[SDS((SEQ, D_MODEL), F32), SDS((N_DEV, D_MODEL, IN_SHARD), BF16)],
        scratch_shapes=[pltpu.VMEM((D_MODEL, IN_SHARD), F32), pltpu.VMEM((SEQ, IN_SHARD), BF16)],
        name="proj_bwd", compiler_params=_params(("arbitrary",)))(dq, dk, dv, du, a, w)


def _na_consts():
    c = np.arange(GRID_W)
    col_start = np.clip(c - KW // 2, 0, GRID_W - KW)
    col_in = (c[None, :] >= col_start[:, None]) & (c[None, :] < col_start[:, None] + KW)
    dc = np.clip(c[None, :] - c[:, None] + KW - 1, 0, 2 * KW - 2)
    onehot = np.zeros((128, GRID_W * GRID_W), np.float32)
    qq, kk = np.meshgrid(c, c, indexing="ij")
    onehot[dc[col_in], (qq * GRID_W + kk)[col_in]] = 1.0
    negmask = np.where(col_in, 0.0, NEG_INF).astype(np.float32).reshape(1, -1)
    return onehot, negmask


def _na_pair(block_type, a, b):
    if block_type == 0:
        return b - a + KH - 1 if b < KH else None
    if block_type == 1:
        return b - a + KH // 2 - 1 if a <= b < a + KH else None
    return b - a if b >= NA_KR - KH else None


def _rpb_expand(rpb):
    onehot, negmask = _na_consts()
    rows = HEADS * (2 * KH - 1)
    rpb_pad = jnp.pad(rpb.reshape(rows, 2 * KW - 1), ((0, 128 - rows), (0, 128 - (2 * KW - 1))))

    def body(r_ref, oh_ref, m_ref, t_ref):
        hi, mid, lo = _split3(r_ref[...])
        oh = oh_ref[...]
        t_ref[...] = _dot(hi, oh) + _dot(mid, oh) + _dot(lo, oh) + m_ref[...]

    table = pl.pallas_call(body, out_shape=SDS((128, GRID_W * GRID_W), F32), name="rpb_expand",
                           compiler_params=_params())(rpb_pad, jnp.asarray(onehot, BF16), jnp.asarray(negmask))
    return table[:rows].reshape(HEADS, 2 * KH - 1, GRID_W, GRID_W)


def _rpb_reduce(dslabs):
    onehot, _ = _na_consts()
    rows = HEADS * (2 * KH - 1)

    def body(x_ref, oht_ref, o_ref):
        hi, mid, lo = _split3(x_ref[...])
        oht = oht_ref[...]
        o_ref[...] = _dot(hi, oht) + _dot(mid, oht) + _dot(lo, oht)

    out = pl.pallas_call(body, out_shape=SDS((rows, 128), F32), name="rpb_reduce", compiler_params=_params())(
        dslabs.reshape(rows, GRID_W * GRID_W), jnp.asarray(onehot.T, BF16))
    return out.reshape(HEADS, 2 * KH - 1, 128)


def _bias_tiles(slab_ref, tile_ref):
    tile_ref[...] = jnp.full(tile_ref.shape, NEG_INF, F32)
    for t in range(NA_TYPES):
        for a in range(NA_RB):
            for b in range(NA_KR):
                dr = _na_pair(t, a, b)
                if dr is not None:
                    tile_ref[t, a * GRID_W:(a + 1) * GRID_W, b * GRID_W:(b + 1) * GRID_W] = slab_ref[dr]


def _bias_tiles_bwd(dtile_ref, dslab_ref):
    acc = {}
    for t in range(NA_TYPES):
        for a in range(NA_RB):
            for b in range(NA_KR):
                dr = _na_pair(t, a, b)
                if dr is not None:
                    part = dtile_ref[t, a * GRID_W:(a + 1) * GRID_W, b * GRID_W:(b + 1) * GRID_W]
                    acc[dr] = part if dr not in acc else acc[dr] + part
    for dr in range(2 * KH - 1):
        dslab_ref[dr] = acc[dr]


def _block_geometry(g):
    if isinstance(g, int):
        start = min(max(g * NA_RB - KH // 2, 0), GRID_ROWS - NA_KR)
        return (0 if g == 0 else 2 if g == NA_BLOCKS - 1 else 1), N_META + g * NA_QB, N_META + start * GRID_W
    start = jnp.clip(g * NA_RB - KH // 2, 0, GRID_ROWS - NA_KR)
    block_type = jnp.where(g == 0, 0, jnp.where(g == NA_BLOCKS - 1, 2, 1))
    q0 = pl.multiple_of(N_META + g * NA_QB, 16)
    k0 = pl.multiple_of(N_META + start * GRID_W, 16)
    return block_type, q0, k0


def _scaled_q(q):
    return (q * ATT_SCALE).astype(BF16)


def _na_probs(qs, kk, km, bias):
    s = _dot_nt(qs, kk) + bias
    sm = _dot_nt(qs, km)
    m = jnp.maximum(jnp.max(s, axis=-1, keepdims=True), jnp.max(sm, axis=-1, keepdims=True))
    p = jnp.exp(s - m)
    pm = jnp.exp(sm - m)
    inv = 1.0 / (jnp.sum(p, axis=-1, keepdims=True) + jnp.sum(pm, axis=-1, keepdims=True))
    return p * inv, pm * inv


def _meta_probs(qm, km):
    s = _dot_nt(qm, km) * ATT_SCALE
    p = jnp.exp(s - jnp.max(s, axis=-1, keepdims=True))
    return p / jnp.sum(p, axis=-1, keepdims=True)


def _qkv_specs():
    return [pl.BlockSpec((None, SEQ, HEAD_DIM), lambda h, which=which: (h + which * HEADS, 0, 0)) for which in range(3)]


def _na_fwd(qkv, bias):
    def body(q_ref, k_ref, v_ref, slab_ref, o_ref, b_ref):
        _bias_tiles(slab_ref, b_ref)
        km = k_ref[0:N_META, :].astype(BF16)
        vm = v_ref[0:N_META, :].astype(BF16)
        pmm = _meta_probs(q_ref[0:N_META, :].astype(BF16), km)
        o_ref[0:N_META, :] = _dot(pmm.astype(BF16), vm)

        def block(g, carry):
            block_type, q0, k0 = _block_geometry(g)
            qs = _scaled_q(q_ref[pl.ds(q0, NA_QB), :])
            kk = k_ref[pl.ds(k0, NA_KB), :].astype(BF16)
            vv = v_ref[pl.ds(k0, NA_KB), :].astype(BF16)
            p, pm = _na_probs(qs, kk, km, b_ref[block_type])
            o_ref[pl.ds(q0, NA_QB), :] = _dot(p.astype(BF16), vv) + _dot(pm.astype(BF16), vm)
            return carry

        for g in range(NA_BLOCKS):
            block(g, 0)

    head = pl.BlockSpec((None, SEQ, HEAD_DIM), lambda h: (h, 0, 0))
    return pl.pallas_call(
        body, grid=(HEADS,), in_specs=_qkv_specs() + [pl.BlockSpec((None, 2 * KH - 1, GRID_W, GRID_W), lambda h: (h, 0, 0, 0))],
        out_specs=head, out_shape=SDS((HEADS, SEQ, HEAD_DIM), F32), name="na_fwd",
        scratch_shapes=[pltpu.VMEM((NA_TYPES, NA_QB, NA_KB), F32)],
        compiler_params=_params(("parallel",)))(qkv, qkv, qkv, bias)


def _na_bwd(qkv, bias, do):
    def body(q_ref, k_ref, v_ref, slab_ref, do_ref, dq_ref, dk_ref, dv_ref, dslab_ref, b_ref, db_ref):
        _bias_tiles(slab_ref, b_ref)
        km = k_ref[0:N_META, :].astype(BF16)
        vm = v_ref[0:N_META, :].astype(BF16)
        dk_ref[...] = jnp.zeros_like(dk_ref)
        dv_ref[...] = jnp.zeros_like(dv_ref)
        db_ref[...] = jnp.zeros_like(db_ref)

        qm = q_ref[0:N_META, :].astype(BF16)
        dom = do_ref[0:N_META, :].astype(BF16)
        pmm = _meta_probs(qm, km)
        dpm = _dot_nt(dom, vm)
        dsm = (pmm * (dpm - jnp.sum(pmm * dpm, axis=-1, keepdims=True)) * ATT_SCALE).astype(BF16)
        dq_ref[0:N_META, :] = _dot(dsm, km)
        dkm0 = _dot_tn(dsm, qm)
        dvm0 = _dot_tn(pmm.astype(BF16), dom)

        def block(g, carry):
            dkm, dvm = carry
            block_type, q0, k0 = _block_geometry(g)
            qs = _scaled_q(q_ref[pl.ds(q0, NA_QB), :])
            kk = k_ref[pl.ds(k0, NA_KB), :].astype(BF16)
            vv = v_ref[pl.ds(k0, NA_KB), :].astype(BF16)
            dob = do_ref[pl.ds(q0, NA_QB), :].astype(BF16)
            p, pm = _na_probs(qs, kk, km, b_ref[block_type])
            dp = _dot_nt(dob, vv)
            dpm_ = _dot_nt(dob, vm)
            delta = jnp.sum(p * dp, axis=-1, keepdims=True) + jnp.sum(pm * dpm_, axis=-1, keepdims=True)
            ds = p * (dp - delta)
            dsm_ = pm * (dpm_ - delta)
            db_ref[block_type] += ds
            dsb = ds.astype(BF16)
            dsmb = dsm_.astype(BF16)
            dq_ref[pl.ds(q0, NA_QB), :] = (_dot(dsb, kk) + _dot(dsmb, km)) * ATT_SCALE
            dk_ref[pl.ds(k0, NA_KB), :] += _dot_tn(dsb, qs)
            dv_ref[pl.ds(k0, NA_KB), :] += _dot_tn(p.astype(BF16), dob)
            return dkm + _dot_tn(dsmb, qs), dvm + _dot_tn(pm.astype(BF16), dob)

        dkm, dvm = dkm0, dvm0
        for g in range(NA_BLOCKS):
            dkm, dvm = block(g, (dkm, dvm))
        dk_ref[0:N_META, :] = dkm
        dv_ref[0:N_META, :] = dvm
        _bias_tiles_bwd(db_ref, dslab_ref)

    head = pl.BlockSpec((None, SEQ, HEAD_DIM), lambda h: (h, 0, 0))
    bspec = pl.BlockSpec((None, 2 * KH - 1, GRID_W, GRID_W), lambda h: (h, 0, 0, 0))
    return pl.pallas_call(
        body, grid=(HEADS,), in_specs=_qkv_specs() + [bspec, head], out_specs=[head, head, head, bspec],
        out_shape=[SDS((HEADS, SEQ, HEAD_DIM), F32)] * 3 + [SDS((HEADS, 2 * KH - 1, GRID_W, GRID_W), F32)],
        scratch_shapes=[pltpu.VMEM((NA_TYPES, NA_QB, NA_KB), F32), pltpu.VMEM((NA_TYPES, NA_QB, NA_KB), F32)],
        name="na_bwd", compiler_params=_params(("parallel",)))(qkv, qkv, qkv, bias, do)


def _cmul(ar, ai, br, bi):
    return ar * br - ai * bi, ar * bi + ai * br


def _cpow(ar, ai, n):
    rr, ri = None, None
    br, bi = ar, ai
    while n:
        if n & 1:
            rr, ri = (br, bi) if rr is None else _cmul(rr, ri, br, bi)
        n >>= 1
        if n:
            br, bi = _cmul(br, bi, br, bi)
    return rr, ri


def _s5_prep(lr, li, logdt, bre, bim):
    def body(lr_ref, li_ref, dt_ref, br_ref, bi_ref, lbr_ref, lbi_ref, bbr_ref, bbi_ref):
        lr_, li_ = lr_ref[...], li_ref[...]
        dt = jnp.exp(dt_ref[...])
        mag = jnp.exp(lr_ * dt)
        lbr = mag * jnp.cos(li_ * dt)
        lbi = mag * jnp.sin(li_ * dt)
        lbr_ref[...] = lbr
        lbi_ref[...] = lbi
        den = lr_ * lr_ + li_ * li_
        xr = lbr - 1.0
        cr = (xr * lr_ + lbi * li_) / den
        ci = (lbi * lr_ - xr * li_) / den
        br, bi = br_ref[...], bi_ref[...]
        bbr_ref[...] = cr[:, None, :] * br - ci[:, None, :] * bi
        bbi_ref[...] = cr[:, None, :] * bi + ci[:, None, :] * br

    n = 2 * S5_GROUPS
    return pl.pallas_call(
        body, out_shape=[SDS((n, S5_STATE), F32)] * 2 + [SDS((n, S5_GROUP, S5_STATE), F32)] * 2,
        name="s5_prep", compiler_params=_params())(lr, li, logdt, bre, bim)


def _s5_prep_bwd(lr, li, logdt, bre, bim, dar, dai, dbbr, dbbi):
    def body(lr_ref, li_ref, dt_ref, br_ref, bi_ref, dar_ref, dai_ref, dbr_ref, dbi_ref,
             glr_ref, gli_ref, gdt_ref, gbr_ref, gbi_ref):
        lr_, li_ = lr_ref[...], li_ref[...]
        dt = jnp.exp(dt_ref[...])
        mag = jnp.exp(lr_ * dt)
        lbr = mag * jnp.cos(li_ * dt)
        lbi = mag * jnp.sin(li_ * dt)
        den = lr_ * lr_ + li_ * li_
        xr = lbr - 1.0
        cr = (xr * lr_ + lbi * li_) / den
        ci = (lbi * lr_ - xr * li_) / den
        br, bi = br_ref[...], bi_ref[...]
        dbr, dbi = dbr_ref[...], dbi_ref[...]
        gbr_ref[...] = cr[:, None, :] * dbr + ci[:, None, :] * dbi
        gbi_ref[...] = cr[:, None, :] * dbi - ci[:, None, :] * dbr
        gcr = jnp.sum(dbr * br + dbi * bi, axis=1)
        gci = jnp.sum(dbi * br - dbr * bi, axis=1)
        ilr, ili = lr_ / den, li_ / den
        tr, ti = _cmul(gcr, gci, ilr, ili)
        glbr = dar_ref[...] + tr
        glbi = dai_ref[...] + ti
        dr_, di_ = _cmul(tr, ti, cr, -ci)
        gwr, gwi = _cmul(glbr, glbi, lbr, -lbi)
        glr_ref[...] = gwr * dt - dr_
        gli_ref[...] = gwi * dt - di_
        gdt_ref[...] = jnp.sum(gwr * lr_ + gwi * li_, axis=-1, keepdims=True) * dt

    n = 2 * S5_GROUPS
    return pl.pallas_call(
        body, out_shape=[SDS((n, S5_STATE), F32)] * 2 + [SDS((n, 1), F32)] + [SDS((n, S5_GROUP, S5_STATE), F32)] * 2,
        name="s5_prep_bwd", compiler_params=_params())(lr, li, logdt, bre, bim, dar, dai, dbbr, dbbi)


def _scan_local(xr_ref, xi_ref, ar8, ai8, reverse):
    def step(i, carry):
        sr, si = carry
        idx = (SCAN_T - 1 - i) if reverse else i
        rows = pl.ds(pl.multiple_of(idx * SCAN_BLOCKS, SCAN_BLOCKS), SCAN_BLOCKS)
        nr = ar8 * sr - ai8 * si + xr_ref[rows, :]
        ni = ar8 * si + ai8 * sr + xi_ref[rows, :]
        xr_ref[rows, :] = nr
        xi_ref[rows, :] = ni
        return nr, ni

    z = jnp.zeros(ar8.shape, F32)
    return lax.fori_loop(0, SCAN_T, step, (z, z), unroll=6)


def _scan_carries(er, ei, atr, ati, reverse):
    row = lax.broadcasted_iota(jnp.int32, er.shape, 0)
    cr = jnp.zeros((1, er.shape[1]), F32)
    ci = cr
    outr = jnp.zeros(er.shape, F32)
    outi = outr
    order = range(SCAN_BLOCKS - 1, -1, -1) if reverse else range(SCAN_BLOCKS)
    for b in order:
        outr = jnp.where(row == b, cr, outr)
        outi = jnp.where(row == b, ci, outi)
        nr, ni = _cmul(atr, ati, cr, ci)
        cr, ci = nr + er[b:b + 1, :], ni + ei[b:b + 1, :]
    return outr, outi


def _scan_fixup(xr_ref, xi_ref, cr8, ci8, ar8, ai8, reverse, pair=None):
    tile = lambda idx: pl.ds(pl.multiple_of(idx * SCAN_BLOCKS, SCAN_BLOCKS), SCAN_BLOCKS)

    def fix(idx, pr, pi):
        fr, fi = _cmul(pr, pi, cr8, ci8)
        nr, ni = xr_ref[tile(idx), :] + fr, xi_ref[tile(idx), :] + fi
        xr_ref[tile(idx), :] = nr
        xi_ref[tile(idx), :] = ni
        return nr, ni

    if pair is None:
        def step(i, carry):
            pr, pi = carry
            fix((SCAN_T - 1 - i) if reverse else i, pr, pi)
            return _cmul(pr, pi, ar8, ai8)

        lax.fori_loop(0, SCAN_T, step, (ar8, ai8), unroll=6)
        return None

    sr_ref, si_ref = pair
    earlier = -1 if reverse else 1

    def step(i, carry):
        pr, pi, accr, acci = carry
        idx = (SCAN_T - 1 - i) if reverse else i
        nr, ni = fix(idx, pr, pi)
        qr, qi = _cmul(nr, ni, sr_ref[tile(idx + earlier), :], -si_ref[tile(idx + earlier), :])
        pr, pi = _cmul(pr, pi, ar8, ai8)
        return pr, pi, accr + qr, acci + qi

    z = jnp.zeros(ar8.shape, F32)
    pr, pi, accr, acci = lax.fori_loop(0, SCAN_T - 1, step, (ar8, ai8, z, z), unroll=4)
    edge, src, shift, empty = (0, SCAN_T - 1, 1, 0) if reverse else (SCAN_T - 1, 0, SCAN_BLOCKS - 1, SCAN_BLOCKS - 1)
    nr, ni = fix(edge, pr, pi)
    row = lax.broadcasted_iota(jnp.int32, ar8.shape, 0)
    spr = jnp.where(row == empty, 0.0, pltpu.roll(sr_ref[tile(src), :], shift, 0))
    spi = jnp.where(row == empty, 0.0, pltpu.roll(si_ref[tile(src), :], shift, 0))
    qr, qi = _cmul(nr, ni, spr, -spi)
    return jnp.sum(accr + qr, axis=0, keepdims=True), jnp.sum(acci + qi, axis=0, keepdims=True)


def _scan(xr_ref, xi_ref, ar, ai, reverse, pair=None):
    n = ar.shape[1]
    ar8 = jnp.broadcast_to(ar, (SCAN_BLOCKS, n))
    ai8 = jnp.broadcast_to(ai, (SCAN_BLOCKS, n))
    er, ei = _scan_local(xr_ref, xi_ref, ar8, ai8, reverse)
    atr, ati = _cpow(ar, ai, SCAN_T)
    cr8, ci8 = _scan_carries(er, ei, atr, ati, reverse)
    return _scan_fixup(xr_ref, xi_ref, cr8, ci8, ar8, ai8, reverse, pair)


def _s5_specs():
    chan = pl.BlockSpec((SEQ, CH_W), lambda c, d: (0, c))
    chan2 = pl.BlockSpec((None, SEQ, CH_W), lambda c, d: (d, 0, c))
    state = pl.BlockSpec((None, SEQ, ST_W), lambda c, d: (d, 0, c))
    bmat = pl.BlockSpec((None, None, CH_W, ST_W), lambda c, d: (d, c, 0, 0))
    cmat = pl.BlockSpec((None, None, ST_W, CH_W), lambda c, d: (d, c, 0, 0))
    avec = pl.BlockSpec((None, None, 1, ST_W), lambda c, d: (d, c, 0, 0))
    return chan, chan2, state, bmat, cmat, avec


def _scan_by_direction(xr_ref, xi_ref, ar, ai, d, adjoint, pair=None, da_out=None):
    for direction in range(2):
        @pl.when(d == direction)
        def _(direction=direction):
            res = _scan(xr_ref, xi_ref, ar, ai, adjoint != (direction == 1), pair)
            if pair is not None:
                da_out[0][...], da_out[1][...] = res


def _s5_scan_fwd(u, bre, bim, are, aim, cre, cim):
    def body(u_ref, bre_ref, bim_ref, are_ref, aim_ref, cre_ref, cim_ref, sr_ref, si_ref, y_ref):
        ub = u_ref[...].astype(BF16)
        sr_ref[...] = _dot(ub, bre_ref[...])
        si_ref[...] = _dot(ub, bim_ref[...])
        _scan_by_direction(sr_ref, si_ref, are_ref[...], aim_ref[...], pl.program_id(1), adjoint=False)
        y_ref[...] = _dot(sr_ref[...].astype(BF16), cre_ref[...]) - _dot(si_ref[...].astype(BF16), cim_ref[...])

    chan, chan2, state, bmat, cmat, avec = _s5_specs()
    return pl.pallas_call(
        body, grid=(S5_CHUNKS, 2), in_specs=[chan, bmat, bmat, avec, avec, cmat, cmat], out_specs=[state, state, chan2],
        out_shape=[SDS((2, SEQ, S5_GROUPS * S5_STATE), F32)] * 2 + [SDS((2, SEQ, S5_WIDTH), F32)],
        name="s5_scan_fwd", compiler_params=_params(("parallel", "parallel")))(u, bre, bim, are, aim, cre, cim)


def _diag_out(out_ref, full):
    for g in range(8):
        out_ref[g] = full[g * S5_GROUP:(g + 1) * S5_GROUP, g * S5_STATE:(g + 1) * S5_STATE]


def _s5_scan_bwd(dy, du_skip, u, sr, si, bre, bim, are, aim, cre, cim):
    def body(dy_ref, dus_ref, u_ref, sr_ref, si_ref, bre_ref, bim_ref, are_ref, aim_ref, cre_ref, cim_ref,
             du_ref, dbr_ref, dbi_ref, dcr_ref, dci_ref, dar_ref, dai_ref, gr_ref, gi_ref):
        d = pl.program_id(1)
        dyb = dy_ref[...].astype(BF16)
        gr_ref[...] = _dot_nt(dyb, cre_ref[...])
        gi_ref[...] = -_dot_nt(dyb, cim_ref[...])
        _diag_out(dcr_ref, _dot_tn(dyb, sr_ref[...].astype(BF16)))
        _diag_out(dci_ref, -_dot_tn(dyb, si_ref[...].astype(BF16)))
        _scan_by_direction(gr_ref, gi_ref, are_ref[...], -aim_ref[...], d, adjoint=True, pair=(sr_ref, si_ref),
                           da_out=(dar_ref, dai_ref))

        @pl.when(d == 0)
        def _():
            du_ref[...] = dus_ref[...]

        grb = gr_ref[...].astype(BF16)
        gib = gi_ref[...].astype(BF16)
        du_ref[...] += _dot_nt(grb, bre_ref[...]) + _dot_nt(gib, bim_ref[...])
        ub = u_ref[...].astype(BF16)
        _diag_out(dbr_ref, _dot_tn(ub, grb))
        _diag_out(dbi_ref, _dot_tn(ub, gib))

    chan, _, state, bmat, cmat, avec = _s5_specs()
    diag = pl.BlockSpec((None, None, 8, S5_GROUP, S5_STATE), lambda c, d: (d, c, 0, 0, 0))
    return pl.pallas_call(
        body, grid=(S5_CHUNKS, 2), in_specs=[chan, chan, chan, state, state, bmat, bmat, avec, avec, cmat, cmat],
        out_specs=[chan, diag, diag, diag, diag, avec, avec],
        out_shape=[SDS((SEQ, S5_WIDTH), F32)] + [SDS((2, S5_CHUNKS, 8, S5_GROUP, S5_STATE), F32)] * 4
                  + [SDS((2, S5_CHUNKS, 1, ST_W), F32)] * 2,
        scratch_shapes=[pltpu.VMEM((SEQ, ST_W), F32), pltpu.VMEM((SEQ, ST_W), F32)],
        name="s5_scan_bwd", compiler_params=_params(("parallel", "arbitrary")))(dy, du_skip, u, sr, si, bre, bim, are, aim, cre, cim)


_GELU_K = math.sqrt(2.0 / math.pi)
_GELU_C = 0.044715


def _gelu(x):
    t = jnp.tanh(_GELU_K * (x + _GELU_C * x * x * x))
    return 0.5 * x * (1.0 + t), t


def _s5_glu_fwd(u, y2, dskip, wglu, bglu):
    def body(u_ref, y0_ref, y1_ref, d_ref, w_ref, b_ref, o_ref, yp_ref):
        ypre = u_ref[...] * d_ref[...] + y0_ref[...] + y1_ref[...]
        yp_ref[...] = ypre
        y, _ = _gelu(ypre)
        z = _dot(y.astype(BF16), w_ref[...]) + b_ref[...]
        o_ref[...] = y * jax.nn.sigmoid(z)

    row = _row_spec(S5_WIDTH)
    vec = _fix_spec((1, S5_WIDTH))
    dir0 = pl.BlockSpec((None, ROW_TILE, S5_WIDTH), lambda i: (0, i, 0))
    dir1 = pl.BlockSpec((None, ROW_TILE, S5_WIDTH), lambda i: (1, i, 0))
    return pl.pallas_call(
        body, grid=(N_ROW_TILES,), in_specs=[row, dir0, dir1, vec, _fix_spec((S5_WIDTH, S5_WIDTH)), vec],
        out_specs=[row, row], out_shape=[SDS((SEQ, S5_WIDTH), F32)] * 2, name="s5_glu_fwd",
        compiler_params=_params(("parallel",)))(u, y2, y2, dskip, wglu, bglu)


def _s5_glu_bwd(do, ypre, u, dskip, wglu, bglu):
    def body(do_ref, yp_ref, u_ref, d_ref, w_ref, b_ref, dyp_ref, du_ref, dw_ref, db_ref, dd_ref):
        i = pl.program_id(0)
        ypre = yp_ref[...]
        y, t = _gelu(ypre)
        yb = y.astype(BF16)
        sg = jax.nn.sigmoid(_dot(yb, w_ref[...]) + b_ref[...])
        dov = do_ref[...]
        dz = dov * y * sg * (1.0 - sg)
        dzb = dz.astype(BF16)
        dy = dov * sg + _dot_nt(dzb, w_ref[...])
        dgelu = 0.5 * (1.0 + t) + 0.5 * ypre * (1.0 - t * t) * _GELU_K * (1.0 + 3.0 * _GELU_C * ypre * ypre)
        dyp = dy * dgelu
        dyp_ref[...] = dyp
        uv = u_ref[...]
        du_ref[...] = dyp * d_ref[...]

        @pl.when(i == 0)
        def _():
            dw_ref[...] = jnp.zeros_like(dw_ref)
            db_ref[...] = jnp.zeros_like(db_ref)
            dd_ref[...] = jnp.zeros_like(dd_ref)

        dw_ref[...] += _dot_tn(yb, dzb)
        db_ref[...] += jnp.sum(dz, axis=0, keepdims=True)
        dd_ref[...] += jnp.sum(dyp * uv, axis=0, keepdims=True)

    row = _row_spec(S5_WIDTH)
    vec = _fix_spec((1, S5_WIDTH))
    mat = _fix_spec((S5_WIDTH, S5_WIDTH))
    return pl.pallas_call(
        body, grid=(N_ROW_TILES,), in_specs=[row, row, row, vec, mat, vec], out_specs=[row, row, mat, vec, vec],
        out_shape=[SDS((SEQ, S5_WIDTH), F32)] * 2 + [SDS((S5_WIDTH, S5_WIDTH), F32), SDS((1, S5_WIDTH), F32), SDS((1, S5_WIDTH), F32)],
        name="s5_glu_bwd", compiler_params=_params(("arbitrary",)))(do, ypre, u, dskip, wglu, bglu)


def _heads_side_by_side(o_ref):
    return jnp.concatenate([o_ref[h] for h in range(HEADS)], axis=-1)


def _mix_out_fwd(ona, os5, g_na, g_s5, wout):
    def body(a_ref, s_ref, ga_ref, gs_ref, w_ref, o_ref):
        av, sv = _heads_side_by_side(a_ref), s_ref[...]
        ca = (av * _rstd(av) * ga_ref[...]).astype(BF16)
        cs = (sv * _rstd(sv) * gs_ref[...]).astype(BF16)
        o_ref[...] = _dot(ca, w_ref[0:NA_WIDTH, :]) + _dot(cs, w_ref[NA_WIDTH:, :])

    row = _row_spec(NA_WIDTH)
    vec = _fix_spec((1, NA_WIDTH))
    heads = pl.BlockSpec((HEADS, ROW_TILE, HEAD_DIM), lambda i: (0, i, 0))
    return pl.pallas_call(
        body, grid=(N_ROW_TILES,), in_specs=[heads, row, vec, vec, _fix_spec((D_MODEL, D_MODEL))],
        out_specs=_row_spec(D_MODEL), out_shape=SDS((SEQ, D_MODEL), F32), name="mix_out_fwd",
        compiler_params=_params(("parallel",)))(ona, os5, g_na, g_s5, wout)


def _mix_out_bwd(dmix, ona, os5, g_na, g_s5, wout):
    def body(dm_ref, a_ref, s_ref, ga_ref, gs_ref, w_ref, da_ref, ds_ref, dw_ref, dga_ref, dgs_ref):
        i = pl.program_id(0)
        dm = dm_ref[...]
        av, sv = _heads_side_by_side(a_ref), s_ref[...]
        ra, rs = _rstd(av), _rstd(sv)
        ga, gs = ga_ref[...], gs_ref[...]
        ca = (av * ra * ga).astype(BF16)
        cs = (sv * rs * gs).astype(BF16)
        dca = _dot_nt(dm, w_ref[0:NA_WIDTH, :])
        dcs = _dot_nt(dm, w_ref[NA_WIDTH:, :])
        da, dga = _rms_bwd(av, ra, ga, dca)
        ds, dgs = _rms_bwd(sv, rs, gs, dcs)
        for h in range(HEADS):
            da_ref[h] = da[:, h * HEAD_DIM:(h + 1) * HEAD_DIM]
        ds_ref[...] = ds

        @pl.when(i == 0)
        def _():
            dw_ref[...] = jnp.zeros_like(dw_ref)
            dga_ref[...] = jnp.zeros_like(dga_ref)
            dgs_ref[...] = jnp.zeros_like(dgs_ref)

        dw_ref[0:NA_WIDTH, :] += _dot_tn(ca, dm)
        dw_ref[NA_WIDTH:, :] += _dot_tn(cs, dm)
        dga_ref[...] += jnp.sum(dga, axis=0, keepdims=True)
        dgs_ref[...] += jnp.sum(dgs, axis=0, keepdims=True)

    row = _row_spec(NA_WIDTH)
    vec = _fix_spec((1, NA_WIDTH))
    mat = _fix_spec((D_MODEL, D_MODEL))
    heads = pl.BlockSpec((HEADS, ROW_TILE, HEAD_DIM), lambda i: (0, i, 0))
    return pl.pallas_call(
        body, grid=(N_ROW_TILES,), in_specs=[_row_spec(D_MODEL), heads, row, vec, vec, mat],
        out_specs=[heads, row, mat, vec, vec],
        out_shape=[SDS((HEADS, SEQ, HEAD_DIM), F32), SDS((SEQ, NA_WIDTH), F32), SDS((D_MODEL, D_MODEL), F32),
                   SDS((1, NA_WIDTH), F32), SDS((1, NA_WIDTH), F32)],
        name="mix_out_bwd", compiler_params=_params(("arbitrary",)))(dmix, ona, os5, g_na, g_s5, wout)


def _me():
    x, y, c = lax.axis_index("x"), lax.axis_index("y"), lax.axis_index("c")
    return x, y, c, 4 * x + 2 * y + c


def _peer(k):
    x, y, c, _ = _me()
    px = 1 - x if (k >> 2) & 1 else x
    py = 1 - y if (k >> 1) & 1 else y
    pc = 1 - c if k & 1 else c
    return (px, py, pc), 4 * px + 2 * py + pc


ALL_PEERS = (1, 2, 3, 4, 5, 6, 7)
CHIP_PEERS = (2, 4, 6)
SIBLING = 1


def _slot8(pos):
    return 4 * pos[0] + 2 * pos[1] + pos[2]


def _slot4(pos):
    return 2 * pos[0] + pos[1]


_HBM = pl.BlockSpec(memory_space=pltpu.HBM)
_SEM = pl.BlockSpec(memory_space=pltpu.SEMAPHORE)
_EFFECT = pltpu.SideEffectType.DATAFLOW_SIDE_EFFECTING


def _exchange_start(arrays, lands, gather, name, peers=ALL_PEERS, slot=_slot8, own=True):
    n = len(arrays)

    def body(*refs):
        ins, lnd = refs[:n], refs[n:2 * n]
        send_sems, recv_sems = refs[2 * n], refs[2 * n + 1]
        token = refs[-1]
        me = slot(_me()[:3])
        for i, k in enumerate(peers):
            peer, _ = _peer(k)
            for a in range(n):
                src = ins[a] if gather else ins[a].at[slot(peer)]
                s = a * len(peers) + i
                pltpu.make_async_remote_copy(src_ref=src, dst_ref=lnd[a].at[me], send_sem=send_sems.at[s],
                                             recv_sem=recv_sems.at[s], device_id=peer, device_id_type=MESH).start()
        if own:
            for a in range(n):
                pltpu.make_async_copy(ins[a] if gather else ins[a].at[me], lnd[a].at[me], recv_sems.at[n * len(peers) + a]).start()
        token[...] = jnp.zeros_like(token)

    sems = pltpu.SemaphoreType.DMA((n * (len(peers) + int(own)),))
    out = pl.pallas_call(
        body, name=name, in_specs=[_HBM] * (2 * n),
        out_shape=(sems, sems) + tuple(pltpu.HBM(a.shape, a.dtype) for a in list(arrays) + list(lands)) + (SDS((8, 128), F32),),
        out_specs=(_SEM, _SEM) + (_HBM,) * (2 * n) + (pl.BlockSpec(memory_space=pltpu.VMEM),),
        input_output_aliases={i: 2 + i for i in range(2 * n)},
        compiler_params=pltpu.CompilerParams(has_side_effects=_EFFECT),
    )(*[pltpu.with_memory_space_constraint(a, pltpu.HBM) for a in list(arrays) + list(lands)])
    return out[0], out[1], list(out[2:2 + n]), list(out[2 + n:2 + 2 * n]), out[-1]


def _exchange_wait(send_sems, recv_sems, arrays, lands, after, gather, name, peers=ALL_PEERS, slot=_slot8, own=True):
    n = len(arrays)

    def body(*refs):
        ins, lnd = refs[:n], refs[n:2 * n]
        send_sems, recv_sems = refs[2 * n], refs[2 * n + 1]
        if own:
            me = slot(_me()[:3])
            for a in range(n):
                pltpu.make_async_copy(ins[a] if gather else ins[a].at[me], lnd[a].at[me], recv_sems.at[n * len(peers) + a]).wait()
        for i, k in enumerate(peers):
            peer, _ = _peer(k)
            for a in range(n):
                src = ins[a] if gather else ins[a].at[slot(peer)]
                s = a * len(peers) + i
                cp = pltpu.make_async_remote_copy(src_ref=src, dst_ref=lnd[a].at[slot(peer)], send_sem=send_sems.at[s],
                                                  recv_sem=recv_sems.at[s], device_id=peer, device_id_type=MESH)
                cp.wait_send()
                cp.wait_recv()

        refs[-1][...] = jnp.zeros_like(refs[-1])

    after = list(after) if isinstance(after, (list, tuple)) else [after]
    out = pl.pallas_call(
        body, name=name, in_specs=[_HBM] * (2 * n) + [_SEM, _SEM] + [pl.BlockSpec(memory_space=pl.ANY)] * len(after),
        out_shape=tuple(pltpu.HBM(a.shape, a.dtype) for a in list(arrays) + list(lands)) + (SDS((8, 128), F32),),
        out_specs=(_HBM,) * (2 * n) + (pl.BlockSpec(memory_space=pltpu.VMEM),), input_output_aliases={i: i for i in range(2 * n)},
        compiler_params=pltpu.CompilerParams(has_side_effects=_EFFECT),
    )(*arrays, *lands, send_sems, recv_sems, *after)
    return list(out[n:2 * n]), out[-1]


def _forward_sibling(lands, name):
    n = len(lands)

    def body(*refs):
        outs = refs[n:2 * n]
        send_sems, recv_sems = refs[2 * n:]
        x, y, c, _ = _me()
        sends = []
        for i, k in enumerate(CHIP_PEERS):
            peer, _ = _peer(k)
            for a in range(n):
                rows = outs[a].at[_slot8(peer)]
                cp = pltpu.make_async_remote_copy(src_ref=rows, dst_ref=rows, send_sem=send_sems.at[a, i], recv_sem=recv_sems.at[a, i],
                                                  device_id=(x, y, 1 - c), device_id_type=MESH)
                cp.start()
                sends.append(cp)
        for i, k in enumerate(CHIP_PEERS):
            (px, py, pc), _ = _peer(k)
            for a in range(n):
                rows = outs[a].at[_slot8((px, py, 1 - pc))]
                pltpu.make_async_remote_copy(src_ref=rows, dst_ref=rows, send_sem=send_sems.at[a, i], recv_sem=recv_sems.at[a, i],
                                             device_id=(x, y, 1 - c), device_id_type=MESH).wait_recv()
        for cp in sends:
            cp.wait_send()

    return pl.pallas_call(
        body, in_specs=[_HBM] * n, out_specs=[_HBM] * n, out_shape=[SDS(a.shape, a.dtype) for a in lands],
        input_output_aliases={i: i for i in range(n)},
        scratch_shapes=[pltpu.SemaphoreType.DMA((n, len(CHIP_PEERS))), pltpu.SemaphoreType.DMA((n, len(CHIP_PEERS)))],
        name=name)(*lands)


def _swap_sibling(arrays, name, after=()):
    n, n_after = len(arrays), len(after)
    chips = N_DEV // 2

    def body(*refs):
        ins, outs = refs[:n], refs[n + n_after:2 * n + n_after]
        send_sems, recv_sems = refs[2 * n + n_after:]
        x, y, c, _ = _me()
        sends = []
        for q in range(chips):
            for a in range(n):
                cp = pltpu.make_async_remote_copy(src_ref=ins[a].at[q, 1 - c], dst_ref=outs[a].at[q], send_sem=send_sems.at[a, q],
                                                  recv_sem=recv_sems.at[a, q], device_id=(x, y, 1 - c), device_id_type=MESH)
                cp.start()
                sends.append(cp)
        for cp in sends:
            cp.wait_recv()
        for cp in sends:
            cp.wait_send()

    return pl.pallas_call(
        body, in_specs=[_HBM] * n + [pl.BlockSpec(memory_space=pl.ANY)] * n_after, out_specs=[_HBM] * n,
        out_shape=[SDS((chips,) + a.shape[2:], a.dtype) for a in arrays],
        scratch_shapes=[pltpu.SemaphoreType.DMA((n, chips)), pltpu.SemaphoreType.DMA((n, chips))], name=name)(*arrays, *after)


def _sum_pairs(mine, theirs, name):
    n = len(mine)
    chips = mine[0].shape[0]
    c = lax.axis_index("c")

    def body(c_ref, *refs):
        for a in range(n):
            refs[2 * n + a][...] = (refs[a][...].astype(F32) + refs[n + a][...].astype(F32)).astype(refs[2 * n + a].dtype)

    def pair(a):
        return pl.BlockSpec((None, None) + a.shape[2:], lambda q, c_ref: (q, c_ref[0], 0, 0))

    def single(a):
        return pl.BlockSpec((None,) + a.shape[2:], lambda q, c_ref: (q, 0, 0))

    return pl.pallas_call(
        body, grid_spec=pltpu.PrefetchScalarGridSpec(
            num_scalar_prefetch=1, grid=(chips,), in_specs=[pair(a) for a in mine] + [single(a) for a in mine],
            out_specs=[single(a) for a in mine]),
        out_shape=[SDS((chips,) + a.shape[2:], a.dtype) for a in mine], name=name,
        compiler_params=_params(("parallel",)))(c.reshape(1).astype(jnp.int32), *mine, *theirs)


def _adamw_math(w, g, m, v):
    m = ADAM_B1 * m + (1.0 - ADAM_B1) * g
    v = ADAM_B2 * v + (1.0 - ADAM_B2) * (g * g)
    m_hat = m / (1.0 - ADAM_B1 ** ADAM_STEP)
    v_hat = v / (1.0 - ADAM_B2 ** ADAM_STEP)
    delta = -ADAM_LR * (m_hat / (jnp.sqrt(v_hat) + ADAM_EPS) + ADAM_WD * w)
    return delta, m, v


def _adamw(w, m, v, pieces, name):
    rows, cols = w.shape[-2:]
    lead = w.ndim - 2
    tile = rows
    for cand in (256, 176, 128, 64, 16):
        if rows > cand and rows % cand == 0:
            tile = cand
            break

    def body(w_ref, m_ref, v_ref, p_ref, g_ref, d_ref, mo_ref, vo_ref):
        g = _sum_pieces(p_ref)
        g_ref[...] = g
        d_ref[...], mo_ref[...], vo_ref[...] = _adamw_math(w_ref[...], g, m_ref[...], v_ref[...])

    blk = pl.BlockSpec((None,) * lead + (tile, cols), lambda i: (0,) * lead + (i, 0))
    return pl.pallas_call(
        body, grid=(rows // tile,), in_specs=[blk, blk, blk, pl.BlockSpec((pieces.shape[0], tile, cols), lambda i: (0, i, 0))],
        out_specs=[blk] * 4, out_shape=[SDS(w.shape, F32)] * 4, name=name,
        compiler_params=_params(("parallel",)))(w, m, v, pieces)


def _sum_pieces(p_ref):
    g = p_ref[0].astype(F32)
    for p in range(1, p_ref.shape[0]):
        g = g + p_ref[p].astype(F32)
    return g


def _adamw_s5_mat(w, m, v, g, name):
    _, ndir, groups, b, c = w.shape
    per_dir = groups // 8

    def body(w_ref, m_ref, v_ref, g_ref, d_ref, mo_ref, vo_ref):
        d_ref[...], mo_ref[...], vo_ref[...] = _adamw_math(w_ref[...], g_ref[...], m_ref[...], v_ref[...])

    blk = pl.BlockSpec((None, None, 8, b, c), lambda i: (0, i // per_dir, i % per_dir, 0, 0))
    return pl.pallas_call(
        body, grid=(ndir * per_dir,), in_specs=[blk] * 4, out_specs=[blk] * 3, out_shape=[SDS(w.shape, F32)] * 3, name=name,
        compiler_params=_params(("parallel",)))(w, m, v, g)


VEC_ROWS = ['ffn1_pre_g', 'ffn1_post_g', 'mix_pre_g', 'mix_post_g', 'ffn2_pre_g', 'ffn2_post_g', 'final_g',
            ('na_out_g', 's5_out_g'), ('s5_d', 's5_b_glu')]
VEC_NAMES = [n for row in VEC_ROWS for n in ((row,) if isinstance(row, str) else row)]
VEC_PACK_ROWS = 16
LOSS_ROW = len(VEC_ROWS)


def _pack_vectors(grads, loss8):
    def body(*refs):
        o_ref = refs[-1]
        o_ref[...] = jnp.zeros_like(o_ref)
        o_ref[LOSS_ROW:LOSS_ROW + 1, 0:128] = refs[-2][0:1, :]
        k = 0
        for i, row in enumerate(VEC_ROWS):
            if isinstance(row, str):
                o_ref[i:i + 1, :] = refs[k][...]
                k += 1
            else:
                o_ref[i:i + 1, 0:NA_WIDTH] = refs[k][...]
                o_ref[i:i + 1, NA_WIDTH:] = refs[k + 1][...]
                k += 2

    return pl.pallas_call(body, out_shape=SDS((VEC_PACK_ROWS, D_MODEL), F32), name="pack_vectors",
                          compiler_params=_params())(*[grads[n] for n in VEC_NAMES], loss8)


def _sum8(pieces, name):
    def body(p_ref, o_ref):
        o_ref[...] = _sum_pieces(p_ref)

    return pl.pallas_call(body, out_shape=SDS(pieces.shape[1:], F32), name=name, compiler_params=_params())(pieces)


def _adamw_small(packed8, vec_wmv, others):
    n_vec, n_oth = len(VEC_NAMES), len(others)

    def body(*refs):
        p_ref = refs[0]
        ins = refs[1:1 + 3 * n_vec + 4 * n_oth]
        outs = refs[1 + 3 * n_vec + 4 * n_oth:]
        gsum = _sum_pieces(p_ref)
        outs[-1][...] = gsum[LOSS_ROW:LOSS_ROW + 1, 0:128]
        k = 0
        for i, row in enumerate(VEC_ROWS):
            parts = [(row, gsum[i:i + 1, :])] if isinstance(row, str) else \
                [(row[0], gsum[i:i + 1, 0:NA_WIDTH]), (row[1], gsum[i:i + 1, NA_WIDTH:])]
            for _, g in parts:
                w_ref, m_ref, v_ref = ins[3 * k:3 * k + 3]
                outs[4 * k][...] = g
                outs[4 * k + 1][...], outs[4 * k + 2][...], outs[4 * k + 3][...] = _adamw_math(w_ref[...], g, m_ref[...], v_ref[...])
                k += 1
        for j in range(n_oth):
            w_ref, m_ref, v_ref, g_ref = ins[3 * n_vec + 4 * j:3 * n_vec + 4 * j + 4]
            g = _sum_pieces(g_ref)
            g = g[tuple(slice(0, s) for s in w_ref.shape[1:])].reshape(w_ref.shape)
            o = outs[4 * (n_vec + j):4 * (n_vec + j) + 4]
            o[0][...] = g
            o[1][...], o[2][...], o[3][...] = _adamw_math(w_ref[...], g, m_ref[...], v_ref[...])

    args, out_shape = [packed8], []
    for w, m, v in vec_wmv:
        args += [w, m, v]
        out_shape += [SDS(w.shape, F32)] * 4
    for w, m, v, g in others:
        args += [w, m, v, g]
        out_shape += [SDS(w.shape, F32)] * 4
    out_shape += [SDS((1, 128), F32)]
    return pl.pallas_call(body, out_shape=out_shape, name="adamw_small", compiler_params=_params())(*args)


def _perm_rows(x):
    return x.reshape(SCAN_BLOCKS, SCAN_T, x.shape[-1]).transpose(1, 0, 2).reshape(SEQ, x.shape[-1])


def _unperm_rows(x):
    return x.reshape(SCAN_T, SCAN_BLOCKS, x.shape[-1]).transpose(1, 0, 2).reshape(SEQ, x.shape[-1])


def _block_diag(x):
    eye = np.eye(8, dtype=bool)[None, None, :, None, :, None]
    full = jnp.where(eye, x[:, :, :, :, None, :], 0.0)
    return full.reshape(2, S5_CHUNKS, 8 * x.shape[3], 8 * x.shape[4])


STORED_SWAPPED = {"ffn1_w_gate": (1, 2), "ffn1_w_up": (1, 2), "ffn2_w_gate": (1, 2), "ffn2_w_up": (1, 2),
                  "s5_b_re": (3, 4), "s5_b_im": (3, 4)}


def _stored(name, x):
    return jnp.swapaxes(x, *STORED_SWAPPED[name]) if name in STORED_SWAPPED else x


def _dep(x, token):
    return x if token is None else x + token


def _local_step(x, target, get_w, small, emit):
    bias = _rpb_expand(small["na_rpb"][0])
    lr = small["s5_lam_re"].reshape(64, S5_STATE)
    li = small["s5_lam_im"].reshape(64, S5_STATE)
    logdt = small["s5_log_dt"].reshape(64, 1)
    b_t = [_stored(n, small[n]).reshape(64, S5_GROUP, S5_STATE) for n in ("s5_b_re", "s5_b_im")]
    lbr, lbi, bbr, bbi = _s5_prep(lr, li, logdt, b_t[0], b_t[1])
    are = lbr.reshape(2, S5_CHUNKS, 1, ST_W)
    aim = lbi.reshape(2, S5_CHUNKS, 1, ST_W)
    bre = _block_diag(bbr.reshape(2, S5_CHUNKS, 8, S5_GROUP, S5_STATE)).astype(BF16)
    bim = _block_diag(bbi.reshape(2, S5_CHUNKS, 8, S5_GROUP, S5_STATE)).astype(BF16)
    c_t = [small[n].reshape(2, S5_CHUNKS, 8, S5_GROUP, S5_STATE).transpose(0, 1, 2, 4, 3) for n in ("s5_c_re", "s5_c_im")]
    cre = _block_diag(c_t[0]).astype(BF16)
    cim = _block_diag(c_t[1]).astype(BF16)
    tgt = jnp.concatenate([jnp.zeros((N_META, D_MODEL), F32), target], axis=0)

    h0, a1 = _embed_prenorm(get_w("meta", None)["meta_tokens"], x, small["ffn1_pre_g"])
    wts = dict(get_w("ffn1", [bias, are, aim, bre, bim, cre, cim, tgt, a1]))
    gate1, up1, f1 = _ffn_fwd(a1, wts["ffn1_w_gate"], wts["ffn1_w_up"], wts["ffn1_w_down"], "ffn1_fwd",
                              after=wts.get("tokens", ()))
    h1, a2 = _post_pre(f1, h0, small["ffn1_post_g"], small["mix_pre_g"], 0.5, "post_pre1")
    wts.update(get_w("w_in", a2))
    qkv = _proj_heads(a2, wts["w_in"])
    u = _proj_u(a2, wts["w_in"])
    ona = _na_fwd(qkv, bias)
    u_p = _perm_rows(u)
    sr, si, y2 = _s5_scan_fwd(u_p, bre, bim, are, aim, cre, cim)
    wts.update(get_w("mix", y2))
    os5_p, ypre_p = _s5_glu_fwd(u_p, y2, small["s5_d"], wts["s5_w_glu"], small["s5_b_glu"])
    os5 = _unperm_rows(os5_p)

    mix = _mix_out_fwd(ona, os5, small["na_out_g"], small["s5_out_g"], wts["w_out"])
    h2, a3 = _post_pre(mix, h1, small["mix_post_g"], small["ffn2_pre_g"], 1.0, "post_pre2")
    wts.update(get_w("ffn2", a3))
    gate2, up2, f2 = _ffn_fwd(a3, wts["ffn2_w_gate"], wts["ffn2_w_up"], wts["ffn2_w_down"], "ffn2_fwd")
    loss8, dh3, df2, g_final, g_ffn2_post = _final_loss(f2, h2, small["ffn2_post_g"], small["final_g"], tgt)

    da3, dwg2, dwu2, dwd2 = _ffn_bwd(df2, a3, gate2, up2, wts["ffn2_w_gate"], wts["ffn2_w_up"], wts["ffn2_w_down"], "ffn2_bwd")
    tok = emit("ffn2", {"ffn2_w_gate": dwg2, "ffn2_w_up": dwu2, "ffn2_w_down": dwd2})
    dh2, dmix, g_ffn2_pre, g_mix_post = _bwd_pre_post(da3, h2, _dep(small["ffn2_pre_g"], tok), dh3, mix, small["mix_post_g"], 1.0,
                                                      "bwd_pre_post2")
    dona, dos5, dwout, g_na_out, g_s5_out = _mix_out_bwd(dmix, ona, os5, small["na_out_g"], small["s5_out_g"], wts["w_out"])

    dypre_p, du_skip_p, dwglu, g_b_glu, g_s5_d = _s5_glu_bwd(_perm_rows(dos5), ypre_p, u_p, small["s5_d"], wts["s5_w_glu"],
                                                             small["s5_b_glu"])
    tok = emit("mix", {"s5_w_glu": dwglu.reshape(N_DEV, S5_WIDTH // N_DEV, S5_WIDTH).astype(BF16),
                       "w_out": dwout.reshape(N_DEV, D_MODEL // N_DEV, D_MODEL).astype(BF16)})
    du_p, dbr, dbi, dcr, dci, dar, dai = _s5_scan_bwd(dypre_p, du_skip_p, u_p, sr, si, bre, bim, _dep(are, tok), aim, cre, cim)
    du = _unperm_rows(du_p)
    per_group = (2 * S5_GROUPS, S5_GROUP, S5_STATE)
    g_lr, g_li, g_dt, g_br, g_bi = _s5_prep_bwd(lr, li, logdt, b_t[0], b_t[1], dar.reshape(64, S5_STATE),
                                                dai.reshape(64, S5_STATE), dbr.reshape(per_group), dbi.reshape(per_group))
    g_c = [dcr.reshape(per_group), dci.reshape(per_group)]

    dq, dk, dv, dbias = _na_bwd(qkv, bias, dona)
    g_rpb = _rpb_reduce(dbias)
    dense = jnp.stack([g.reshape(2 * S5_GROUPS, S5_STATE * S5_GROUP) for g in (g_br, g_bi, *g_c)])
    tok = emit("small", {"dense": dense, "na_rpb": g_rpb,
                         "s5_lam_re": g_lr.reshape(2, S5_GROUPS, S5_STATE), "s5_lam_im": g_li.reshape(2, S5_GROUPS, S5_STATE),
                         "s5_log_dt": g_dt.reshape(2, S5_GROUPS)})
    da2, dwin = _proj_bwd(dq, dk, dv, du, a2, wts["w_in"])
    tok2 = emit("w_in", {"w_in": dwin})
    tok = tok if tok2 is None else tok + tok2
    dh1, df1, g_mix_pre, g_ffn1_post = _bwd_pre_post(da2, h1, _dep(small["mix_pre_g"], tok), dh2, f1, small["ffn1_post_g"], 0.5,
                                                     "bwd_pre_post1")
    da1, dwg1, dwu1, dwd1 = _ffn_bwd(df1, a1, gate1, up1, wts["ffn1_w_gate"], wts["ffn1_w_up"], wts["ffn1_w_down"], "ffn1_bwd")
    grad_x, grad_meta, g_ffn1_pre = _bwd_embed(da1, h0, small["ffn1_pre_g"], dh1)
    vec_g = {
        "ffn1_pre_g": g_ffn1_pre, "ffn1_post_g": g_ffn1_post, "mix_pre_g": g_mix_pre, "s5_d": g_s5_d, "s5_b_glu": g_b_glu,
        "na_out_g": g_na_out, "s5_out_g": g_s5_out, "mix_post_g": g_mix_post,
        "ffn2_pre_g": g_ffn2_pre, "ffn2_post_g": g_ffn2_post, "final_g": g_final,
    }
    emit("vec", {"packed": _pack_vectors(vec_g, loss8), "meta_tokens": grad_meta})
    emit("ffn1", {"ffn1_w_gate": dwg1, "ffn1_w_up": dwu1, "ffn1_w_down": dwd1})
    return grad_x


WEIGHT_NAMES = ['meta_tokens', 'ffn1_pre_g', 'ffn1_post_g', 'ffn1_w_gate', 'ffn1_w_up', 'ffn1_w_down', 'mix_pre_g', 'w_in',
                'na_rpb', 's5_lam_re', 's5_lam_im', 's5_log_dt', 's5_b_re', 's5_b_im', 's5_c_re', 's5_c_im', 's5_d',
                's5_w_glu', 's5_b_glu', 'na_out_g', 's5_out_g', 'w_out', 'mix_post_g', 'ffn2_pre_g', 'ffn2_post_g',
                'ffn2_w_gate', 'ffn2_w_up', 'ffn2_w_down', 'final_g']
BIG_NAMES = ['ffn1_w_gate', 'ffn1_w_up', 'ffn1_w_down', 'w_in', 's5_w_glu', 'w_out', 'ffn2_w_gate', 'ffn2_w_up', 'ffn2_w_down']
SMALL_NAMES = [n for n in WEIGHT_NAMES if n not in BIG_NAMES and n != 'meta_tokens']
WHOLE_NAMES = ['na_rpb', 's5_lam_re', 's5_lam_im', 's5_log_dt']
LEAD_NAMES = ['s5_b_re', 's5_b_im', 's5_c_re', 's5_c_im']


def kernel(x, meta_tokens, ffn1_pre_g, ffn1_post_g, ffn1_w_gate, ffn1_w_up, ffn1_w_down, mix_pre_g, w_in, na_rpb, s5_lam_re, s5_lam_im, s5_log_dt, s5_b_re, s5_b_im, s5_c_re, s5_c_im, s5_d, s5_w_glu, s5_b_glu, na_out_g, s5_out_g, w_out, mix_post_g, ffn2_pre_g, ffn2_post_g, ffn2_w_gate, ffn2_w_up, ffn2_w_down, final_g, loss_target, m_meta_tokens, m_ffn1_pre_g, m_ffn1_post_g, m_ffn1_w_gate, m_ffn1_w_up, m_ffn1_w_down, m_mix_pre_g, m_w_in, m_na_rpb, m_s5_lam_re, m_s5_lam_im, m_s5_log_dt, m_s5_b_re, m_s5_b_im, m_s5_c_re, m_s5_c_im, m_s5_d, m_s5_w_glu, m_s5_b_glu, m_na_out_g, m_s5_out_g, m_w_out, m_mix_post_g, m_ffn2_pre_g, m_ffn2_post_g, m_ffn2_w_gate, m_ffn2_w_up, m_ffn2_w_down, m_final_g, v_meta_tokens, v_ffn1_pre_g, v_ffn1_post_g, v_ffn1_w_gate, v_ffn1_w_up, v_ffn1_w_down, v_mix_pre_g, v_w_in, v_na_rpb, v_s5_lam_re, v_s5_lam_im, v_s5_log_dt, v_s5_b_re, v_s5_b_im, v_s5_c_re, v_s5_c_im, v_s5_d, v_s5_w_glu, v_s5_b_glu, v_na_out_g, v_s5_out_g, v_w_out, v_mix_post_g, v_ffn2_pre_g, v_ffn2_post_g, v_ffn2_w_gate, v_ffn2_w_up, v_ffn2_w_down, v_final_g):
    args = dict(locals())
    w = {n: args[n] for n in WEIGHT_NAMES}
    m = {n: args["m_" + n] for n in WEIGHT_NAMES}
    v = {n: args["v_" + n] for n in WEIGHT_NAMES}

    small = {n: w[n] for n in SMALL_NAMES}

    pending = {}

    def start(group, names, arrays, gather, peers=ALL_PEERS, slot=_slot8):
        n_slots = N_DEV if slot is _slot8 else N_DEV // 2
        lands = [lax.empty((n_slots,) + a.shape if gather else a.shape, a.dtype) for a in arrays]
        send_sems, recv_sems, arrays, lands, token = _exchange_start(arrays, lands, gather, "start_" + group, peers, slot)
        pending[group] = (names, send_sems, recv_sems, arrays, lands, gather, peers, slot)
        return token

    def finish(group, after):
        names, send_sems, recv_sems, arrays, lands, gather, peers, slot = pending.pop(group)
        lands, token = _exchange_wait(send_sems, recv_sems, arrays, lands, after, gather, "wait_" + group, peers, slot)
        return dict(zip(names, lands)), token

    first = ["ffn1_w_gate", "ffn1_w_up", "ffn1_w_down"]
    def shard(n, token=None):
        return _dep(_stored(n, w[n])[0], None if token is None else token[0, 0]).astype(BF16)

    ffn_names = ("ffn1_w_gate", "ffn1_w_up", "ffn1_w_down", "ffn2_w_gate", "ffn2_w_up", "ffn2_w_down")
    later_groups = (("w_in", ["w_in"]), ("mix", ["s5_w_glu", "w_out"]), ("ffn2", ["ffn2_w_gate", "ffn2_w_up", "ffn2_w_down"]))
    token0 = start("meta", ["meta_tokens"], [w["meta_tokens"]], True)
    token1 = start("ffn1", first, [shard(n, token0) for n in first], True, (SIBLING,) + CHIP_PEERS)
    meta_full = finish("meta", [token1])[0]["meta_tokens"].transpose(1, 0, 2).reshape(N_META, D_MODEL)
    later_shards = {n: shard(n, token1) for _, names in later_groups for n in names}
    for n in ("na_rpb", "s5_lam_re"):
        small[n] = _dep(small[n], token1[0, 0])

    def get_w(group, after):
        if group == "meta":
            return {"meta_tokens": meta_full}
        if group == "ffn1":
            after = list(after) + list(later_shards.values())
        got, token = finish(group, after)
        if group == "ffn1":
            got = dict(zip(got, _forward_sibling(list(got.values()), "forward_ffn1")))
            got["tokens"] = [start(g, names + ["order"], [later_shards[n] for n in names] + [token], True) for g, names in later_groups]
        if group == "mix":
            got = {"s5_w_glu": got["s5_w_glu"].reshape(S5_WIDTH, S5_WIDTH), "w_out": got["w_out"].reshape(D_MODEL, D_MODEL)}
        return {n: (a.reshape(D_FF, D_MODEL) if n in ffn_names else a) for n, a in got.items()}

    tokens = {}

    def emit(group, grads):
        grads = {n: (g.reshape(N_DEV, FF_SHARD, D_MODEL) if n in ffn_names else g) for n, g in grads.items()}
        if group == "ffn1":
            mine = [g.reshape((N_DEV // 2, 2) + g.shape[1:]) for g in grads.values()]
            theirs = _swap_sibling(mine, "swap_g_ffn1", after=[tokens["vec"]])
            sums = _sum_pairs(mine, theirs, "pair_sum_g_ffn1")
            tokens[group] = start("g_ffn1", list(grads), sums, False, CHIP_PEERS, _slot4)
        else:
            tokens[group] = start("g_" + group, list(grads), list(grads.values()), group in ("small", "vec"))
        return tokens[group][0, 0]

    grad_x = _local_step(x[0], loss_target[0], get_w, small, emit)
    res = {}

    def update_shard(n, pieces):
        outs = _adamw(_stored(n, w[n]), _stored(n, m[n]), _stored(n, v[n]), pieces, "adamw_" + n)
        res[n] = [_stored(n, o) for o in outs]

    late = [grad_x, tokens["ffn1"]]
    for group in ("g_ffn2", "g_mix", "g_w_in"):
        for n, pieces in finish(group, late)[0].items():
            update_shard(n, pieces)
    g8 = finish("g_small", late)[0]
    dense = _sum8(g8["dense"], "sum_dense")
    for i, n in enumerate(LEAD_NAMES):
        g = dense[i].reshape(_stored(n, w[n]).shape)
        upd = _adamw_s5_mat(_stored(n, w[n]), _stored(n, m[n]), _stored(n, v[n]), g, "adamw_" + n)
        res[n] = [_stored(n, o) for o in [g] + list(upd)]

    done = [res[n][1] for n in ("ffn2_w_gate", "ffn2_w_up", "ffn2_w_down", "w_in", "w_out", "s5_w_glu") + tuple(LEAD_NAMES)]
    got = finish("g_vec", done)[0]
    packed8, gmeta8 = got["packed"], got["meta_tokens"]
    for n, pieces in finish("g_ffn1", packed8)[0].items():
        update_shard(n, pieces)
    _, _, _, me = _me()
    update_shard("meta_tokens", lax.dynamic_slice_in_dim(gmeta8, me * (D_MODEL // N_DEV), D_MODEL // N_DEV, axis=2))

    outs = _adamw_small(packed8, [(w[n], m[n], v[n]) for n in VEC_NAMES], [(w[n], m[n], v[n], g8[n]) for n in WHOLE_NAMES])
    for i, n in enumerate(VEC_NAMES + WHOLE_NAMES):
        res[n] = list(outs[4 * i:4 * i + 4])

    out = [outs[-1][0, 0], grad_x[None]]
    for kind in range(4):
        out += [res[n][kind] for n in WEIGHT_NAMES]
    return tuple(out)
```

```python
import math

import numpy as np
import jax
import jax.numpy as jnp
from jax import lax
from jax.experimental import pallas as pl
from jax.experimental.pallas import tpu as pltpu

F32 = jnp.float32
BF16 = jnp.bfloat16
SDS = jax.ShapeDtypeStruct

D_MODEL = 1024
N_TOK = 2048
N_META = 16
SEQ = N_TOK + N_META
ROW_TILE = 688
N_ROW_TILES = SEQ // ROW_TILE
N_DEV = 8
D_FF = 2816
FF_SHARD = D_FF // N_DEV
FF_TILE = 256
IN_SHARD = 256
NA_WIDTH = 512
S5_WIDTH = 512
HEADS = 8
HEAD_DIM = 64
GRID_W = 64
GRID_ROWS = N_TOK // GRID_W
KH = 8
KW = 16
NA_RB = 4
NA_KR = KH + NA_RB - 1
NA_BLOCKS = GRID_ROWS // NA_RB
NA_QB = NA_RB * GRID_W
NA_KB = NA_KR * GRID_W
NA_TYPES = 3
S5_GROUPS = 32
S5_GROUP = 16
S5_STATE = 64
S5_CHUNKS = 4
CH_W = S5_WIDTH // S5_CHUNKS
ST_W = S5_GROUPS * S5_STATE // S5_CHUNKS
SCAN_BLOCKS = 8
SCAN_T = SEQ // SCAN_BLOCKS
RMS_EPS = 1e-6
NEG_INF = -1e30
ATT_SCALE = HEAD_DIM ** -0.5
ADAM_LR, ADAM_B1, ADAM_B2, ADAM_EPS, ADAM_WD, ADAM_STEP = 0.001, 0.9, 0.999, 1e-08, 0.01, 10
VMEM_LIMIT = 56 * 1024 * 1024
MESH = pl.DeviceIdType.MESH


def _params(sem=None):
    return pltpu.CompilerParams(dimension_semantics=sem, vmem_limit_bytes=VMEM_LIMIT)


def _dot(a, b):
    return jnp.dot(a, b, preferred_element_type=F32)


def _dot_nt(a, b):
    return lax.dot_general(a, b, (((1,), (1,)), ((), ())), preferred_element_type=F32)


def _dot_tn(a, b):
    return lax.dot_general(a, b, (((0,), (0,)), ((), ())), preferred_element_type=F32)


def _rstd(x):
    return lax.rsqrt(jnp.mean(x * x, axis=-1, keepdims=True) + RMS_EPS)


def _rms_bwd(x, r, g, dy):
    dyg = dy * g
    xr = x * r
    dx = r * (dyg - xr * jnp.mean(dyg * xr, axis=-1, keepdims=True))
    return dx, dy * xr


def _rows(i, size=ROW_TILE):
    return pl.ds(i * size if isinstance(i, int) else pl.multiple_of(i * size, 16), size)


def _row_spec(width):
    return pl.BlockSpec((ROW_TILE, width), lambda i: (i, 0))


def _fix_spec(shape):
    return pl.BlockSpec(shape, lambda i: (0,) * len(shape))


def _split3(x):
    hi = x.astype(BF16)
    r1 = x - hi.astype(F32)
    mid = r1.astype(BF16)
    lo = (r1 - mid.astype(F32)).astype(BF16)
    return hi, mid, lo


def _embed_prenorm(meta, x, g):
    def body(m_ref, x_ref, g_ref, h_ref, a_ref):
        h_ref[0:N_META, :] = m_ref[...]
        h_ref[N_META:, :] = x_ref[...]
        for i in range(N_ROW_TILES):
            rows = slice(i * ROW_TILE, (i + 1) * ROW_TILE)
            hv = h_ref[rows, :]
            a_ref[rows, :] = (hv * _rstd(hv) * g_ref[...]).astype(BF16)

    return pl.pallas_call(
        body, out_shape=[SDS((SEQ, D_MODEL), F32), SDS((SEQ, D_MODEL), BF16)], name="embed_prenorm",
        compiler_params=_params())(meta, x, g)


def _post_pre(f, hres, g_post, g_next, scale, name):
    def body(f_ref, h_ref, gp_ref, gn_ref, ho_ref, a_ref):
        fv = f_ref[...]
        h = h_ref[...] + scale * (fv * _rstd(fv) * gp_ref[...])
        ho_ref[...] = h
        a_ref[...] = (h * _rstd(h) * gn_ref[...]).astype(BF16)

    return pl.pallas_call(
        body, grid=(N_ROW_TILES,),
        in_specs=[_row_spec(D_MODEL), _row_spec(D_MODEL), _fix_spec((1, D_MODEL)), _fix_spec((1, D_MODEL))],
        out_specs=[_row_spec(D_MODEL), _row_spec(D_MODEL)],
        out_shape=[SDS((SEQ, D_MODEL), F32), SDS((SEQ, D_MODEL), BF16)], name=name,
        compiler_params=_params(("parallel",)))(f, hres, g_post, g_next)


def _final_loss(f2, h2, g_post, g_final, target):
    def body(f_ref, h_ref, gp_ref, gf_ref, t_ref, loss_ref, dh_ref, df_ref, dgf_ref, dgp_ref):
        i = pl.program_id(0)
        fv = f_ref[...]
        r1 = _rstd(fv)
        gp = gp_ref[...]
        h3 = h_ref[...] + 0.5 * (fv * r1 * gp)
        r2 = _rstd(h3)
        gf = gf_ref[...]
        y = h3 * r2 * gf
        row = lax.broadcasted_iota(jnp.int32, (ROW_TILE, 1), 0) + i * ROW_TILE
        err = jnp.where(row >= N_META, y - t_ref[...], 0.0)
        part = 0.5 * jnp.sum(jnp.mean(err * err, axis=-1, keepdims=True))
        dy = err * (1.0 / D_MODEL)
        dh3, dgf = _rms_bwd(h3, r2, gf, dy)
        dh_ref[...] = dh3
        df, dgp = _rms_bwd(fv, r1, gp, 0.5 * dh3)
        df_ref[...] = df.astype(BF16)

        @pl.when(i == 0)
        def _():
            loss_ref[...] = jnp.zeros_like(loss_ref)
            dgf_ref[...] = jnp.zeros_like(dgf_ref)
            dgp_ref[...] = jnp.zeros_like(dgp_ref)

        loss_ref[...] += part
        dgf_ref[...] += jnp.sum(dgf, axis=0, keepdims=True)
        dgp_ref[...] += jnp.sum(dgp, axis=0, keepdims=True)

    gain = _fix_spec((1, D_MODEL))
    return pl.pallas_call(
        body, grid=(N_ROW_TILES,),
        in_specs=[_row_spec(D_MODEL), _row_spec(D_MODEL), gain, gain, _row_spec(D_MODEL)],
        out_specs=[_fix_spec((8, 128)), _row_spec(D_MODEL), _row_spec(D_MODEL), gain, gain],
        out_shape=[SDS((8, 128), F32), SDS((SEQ, D_MODEL), F32), SDS((SEQ, D_MODEL), BF16),
                   SDS((1, D_MODEL), F32), SDS((1, D_MODEL), F32)],
        name="final_loss", compiler_params=_params(("arbitrary",)))(f2, h2, g_post, g_final, target)


def _bwd_pre_post(da, h, g_pre, dh_res, fprev, g_post, scale, name):
    def body(da_ref, h_ref, gpre_ref, dhr_ref, f_ref, gpost_ref, dh_ref, df_ref, dgpre_ref, dgpost_ref):
        i = pl.program_id(0)
        hv = h_ref[...]
        dxa, dgpre = _rms_bwd(hv, _rstd(hv), gpre_ref[...], da_ref[...])
        dh = dhr_ref[...] + dxa
        dh_ref[...] = dh
        fv = f_ref[...]
        df, dgpost = _rms_bwd(fv, _rstd(fv), gpost_ref[...], scale * dh)
        df_ref[...] = df.astype(BF16)

        @pl.when(i == 0)
        def _():
            dgpre_ref[...] = jnp.zeros_like(dgpre_ref)
            dgpost_ref[...] = jnp.zeros_like(dgpost_ref)

        dgpre_ref[...] += jnp.sum(dgpre, axis=0, keepdims=True)
        dgpost_ref[...] += jnp.sum(dgpost, axis=0, keepdims=True)

    gain = _fix_spec((1, D_MODEL))
    row = _row_spec(D_MODEL)
    return pl.pallas_call(
        body, grid=(N_ROW_TILES,), in_specs=[row, row, gain, row, row, gain],
        out_specs=[row, row, gain, gain],
        out_shape=[SDS((SEQ, D_MODEL), F32), SDS((SEQ, D_MODEL), BF16), SDS((1, D_MODEL), F32), SDS((1, D_MODEL), F32)],
        name=name, compiler_params=_params(("arbitrary",)))(da, h, g_pre, dh_res, fprev, g_post)


def _bwd_embed(da, h, g_pre, dh_res):
    def body(da_ref, h_ref, gpre_ref, dhr_ref, gx_ref, gm_ref, dgpre_ref):
        total = jnp.zeros((1, D_MODEL), F32)
        for i in range(N_ROW_TILES):
            rows = slice(i * ROW_TILE, (i + 1) * ROW_TILE)
            hv = h_ref[rows, :]
            dxa, dgpre = _rms_bwd(hv, _rstd(hv), gpre_ref[...], da_ref[rows, :])
            dh = dhr_ref[rows, :] + dxa
            total = total + jnp.sum(dgpre, axis=0, keepdims=True)
            if i == 0:
                gm_ref[...] = dh[0:N_META, :]
                gx_ref[0:ROW_TILE - N_META, :] = dh[N_META:, :]
            else:
                gx_ref[i * ROW_TILE - N_META:(i + 1) * ROW_TILE - N_META, :] = dh
        dgpre_ref[...] = total

    return pl.pallas_call(
        body, out_shape=[SDS((N_TOK, D_MODEL), F32), SDS((N_META, D_MODEL), F32), SDS((1, D_MODEL), F32)],
        name="bwd_embed", compiler_params=_params())(da, h, g_pre, dh_res)


def _ffn_fwd(a, wg, wu, wd, name, after=()):
    def body(a_ref, wg_ref, wu_ref, wd_ref, *rest):
        gate_ref, up_ref, f_ref = rest[len(after):]
        j = pl.program_id(0)

        def tile(i, carry):
            rows = _rows(i)
            at = a_ref[rows, :]
            gate = _dot_nt(at, wg_ref[...])
            up = _dot_nt(at, wu_ref[...])
            gate_ref[rows, :] = gate.astype(BF16)
            up_ref[rows, :] = up.astype(BF16)
            act = (gate * jax.nn.sigmoid(gate) * up).astype(BF16)
            contrib = _dot(act, wd_ref[...])

            @pl.when(j == 0)
            def _():
                f_ref[rows, :] = contrib

            @pl.when(j != 0)
            def _():
                f_ref[rows, :] += contrib

            return carry

        for i in range(N_ROW_TILES):
            tile(i, 0)

    wtile = pl.BlockSpec((FF_TILE, D_MODEL), lambda j: (j, 0))
    hid = pl.BlockSpec((SEQ, FF_TILE), lambda j: (0, j))
    full = pl.BlockSpec((SEQ, D_MODEL), lambda j: (0, 0))
    return pl.pallas_call(
        body, grid=(D_FF // FF_TILE,), in_specs=[full, wtile, wtile, wtile] + [pl.BlockSpec(memory_space=pl.ANY)] * len(after),
        out_specs=[hid, hid, full],
        out_shape=[SDS((SEQ, D_FF), BF16), SDS((SEQ, D_FF), BF16), SDS((SEQ, D_MODEL), F32)],
        name=name, compiler_params=_params(("arbitrary",)))(a, wg, wu, wd, *after)


def _ffn_bwd(df, a, gate, up, wg, wu, wd, name):
    def body(df_ref, a_ref, gate_ref, up_ref, wg_ref, wu_ref, wd_ref, da_ref, dwg_ref, dwu_ref, dwd_ref):
        j = pl.program_id(0)

        def tile(i, carry):
            rows = _rows(i)
            dft = df_ref[rows, :]
            at = a_ref[rows, :]
            gate = gate_ref[rows, :].astype(F32)
            up = up_ref[rows, :].astype(F32)
            dact = _dot_nt(dft, wd_ref[...])
            sig = jax.nn.sigmoid(gate)
            silu = gate * sig
            dgate = (dact * up * (sig * (1.0 + gate * (1.0 - sig)))).astype(BF16)
            dup = (dact * silu).astype(BF16)
            act = (silu * up).astype(BF16)
            dwd = _dot_tn(act, dft)
            dwg = _dot_tn(dgate, at)
            dwu = _dot_tn(dup, at)
            dat = _dot(dgate, wg_ref[...]) + _dot(dup, wu_ref[...])

            @pl.when(j == 0)
            def _():
                da_ref[rows, :] = dat

            @pl.when(j != 0)
            def _():
                da_ref[rows, :] += dat

            return (dwg, dwu, dwd) if carry is None else tuple(c + d for c, d in zip(carry, (dwg, dwu, dwd)))

        sums = None
        for i in range(N_ROW_TILES):
            sums = tile(i, sums)
        dwg_ref[...] = sums[0].astype(BF16)
        dwu_ref[...] = sums[1].astype(BF16)
        dwd_ref[...] = sums[2].astype(BF16)

    wtile = pl.BlockSpec((FF_TILE, D_MODEL), lambda j: (j, 0))
    hid = pl.BlockSpec((SEQ, FF_TILE), lambda j: (0, j))
    full = pl.BlockSpec((SEQ, D_MODEL), lambda j: (0, 0))
    return pl.pallas_call(
        body, grid=(D_FF // FF_TILE,), in_specs=[full, full, hid, hid, wtile, wtile, wtile],
        out_specs=[full, wtile, wtile, wtile],
        out_shape=[SDS((SEQ, D_MODEL), F32)] + [SDS((D_FF, D_MODEL), BF16)] * 3,
        name=name, compiler_params=_params(("arbitrary",)))(df, a, gate, up, wg, wu, wd)


HEADS_PER_BLOCK = IN_SHARD // HEAD_DIM
QKV_BLOCKS = 3 * NA_WIDTH // IN_SHARD


def _proj_heads(a, w):
    def body(a_ref, w_ref, o_ref):
        def tile(i, carry):
            rows = _rows(i)
            res = _dot(a_ref[rows, :], w_ref[...])
            for sub in range(HEADS_PER_BLOCK):
                o_ref[sub, rows, :] = res[:, sub * HEAD_DIM:(sub + 1) * HEAD_DIM]
            return carry

        for i in range(N_ROW_TILES):
            tile(i, 0)

    return pl.pallas_call(
        body, grid=(QKV_BLOCKS,),
        in_specs=[pl.BlockSpec((SEQ, D_MODEL), lambda j: (0, 0)), pl.BlockSpec((None, D_MODEL, IN_SHARD), lambda j: (j, 0, 0))],
        out_specs=pl.BlockSpec((HEADS_PER_BLOCK, SEQ, HEAD_DIM), lambda j: (j, 0, 0)),
        out_shape=SDS((3 * HEADS, SEQ, HEAD_DIM), F32), name="proj_heads",
        compiler_params=_params(("parallel",)))(a, w)


def _proj_u(a, w):
    def body(a_ref, w_ref, o_ref):
        def tile(i, carry):
            rows = _rows(i)
            o_ref[rows, :] = _dot(a_ref[rows, :], w_ref[...])
            return carry

        for i in range(N_ROW_TILES):
            tile(i, 0)

    return pl.pallas_call(
        body, grid=(N_DEV - QKV_BLOCKS,),
        in_specs=[pl.BlockSpec((SEQ, D_MODEL), lambda j: (0, 0)),
                  pl.BlockSpec((None, D_MODEL, IN_SHARD), lambda j: (j + QKV_BLOCKS, 0, 0))],
        out_specs=pl.BlockSpec((SEQ, IN_SHARD), lambda j: (0, j)),
        out_shape=SDS((SEQ, S5_WIDTH), F32), name="proj_u",
        compiler_params=_params(("parallel",)))(a, w)


def _proj_bwd(dq, dk, dv, du, a, w):
    def body(dq_ref, dk_ref, dv_ref, du_ref, a_ref, w_ref, da_ref, dw_ref, acc, dp_ref):
        j = pl.program_id(0)

        for which, src in enumerate((dq_ref, dk_ref, dv_ref)):
            @pl.when((j >= 2 * which) & (j < 2 * which + 2))
            def _(src=src):
                dp_ref[...] = jnp.concatenate([src[sub] for sub in range(HEADS_PER_BLOCK)], axis=-1).astype(BF16)

        @pl.when(j >= QKV_BLOCKS)
        def _():
            dp_ref[...] = du_ref[...].astype(BF16)

        def tile(i, carry):
            rows = _rows(i)
            dpt = dp_ref[rows, :]
            dw = _dot_tn(a_ref[rows, :], dpt)
            dat = _dot_nt(dpt, w_ref[...])

            @pl.when(i == 0)
            def _():
                acc[...] = dw

            @pl.when(i != 0)
            def _():
                acc[...] += dw

            @pl.when(j == 0)
            def _():
                da_ref[rows, :] = dat

            @pl.when(j != 0)
            def _():
                da_ref[rows, :] += dat

            return carry

        for i in range(N_ROW_TILES):
            tile(i, 0)
        dw_ref[...] = acc[...].astype(BF16)

    full = pl.BlockSpec((SEQ, D_MODEL), lambda j: (0, 0))
    wspec = pl.BlockSpec((None, D_MODEL, IN_SHARD), lambda j: (j, 0, 0))

    def heads(which):
        return pl.BlockSpec((HEADS_PER_BLOCK, SEQ, HEAD_DIM), lambda j: (jnp.clip(j - 2 * which, 0, 1), 0, 0))

    return pl.pallas_call(
        body, grid=(N_DEV,),
        in_specs=[heads(0), heads(1), heads(2),
                  pl.BlockSpec((SEQ, IN_SHARD), lambda j: (0, jnp.clip(j - QKV_BLOCKS, 0, 1))), full, wspec],
        out_specs=[full, wspec],
        out_shape=[SDS((SEQ, D_MODEL), F32), SDS((N_DEV, D_MODEL, IN_SHARD), BF16)],
        scratch_shapes=[pltpu.VMEM((D_MODEL, IN_SHARD), F32), pltpu.VMEM((SEQ, IN_SHARD), BF16)],
        name="proj_bwd", compiler_params=_params(("arbitrary",)))(dq, dk, dv, du, a, w)


def _na_consts():
    c = np.arange(GRID_W)
    col_start = np.clip(c - KW // 2, 0, GRID_W - KW)
    col_in = (c[None, :] >= col_start[:, None]) & (c[None, :] < col_start[:, None] + KW)
    dc = np.clip(c[None, :] - c[:, None] + KW - 1, 0, 2 * KW - 2)
    onehot = np.zeros((128, GRID_W * GRID_W), np.float32)
    qq, kk = np.meshgrid(c, c, indexing="ij")
    onehot[dc[col_in], (qq * GRID_W + kk)[col_in]] = 1.0
    negmask = np.where(col_in, 0.0, NEG_INF).astype(np.float32).reshape(1, -1)
    return onehot, negmask


def _na_pair(block_type, a, b):
    if block_type == 0:
        return b - a + KH - 1 if b < KH else None
    if block_type == 1:
        return b - a + KH // 2 - 1 if a <= b < a + KH else None
    return b - a if b >= NA_KR - KH else None


def _rpb_expand(rpb):
    onehot, negmask = _na_consts()
    rows = HEADS * (2 * KH - 1)
    rpb_pad = jnp.pad(rpb.reshape(rows, 2 * KW - 1), ((0, 128 - rows), (0, 128 - (2 * KW - 1))))

    def body(r_ref, oh_ref, m_ref, t_ref):
        hi, mid, lo = _split3(r_ref[...])
        oh = oh_ref[...]
        t_ref[...] = _dot(hi, oh) + _dot(mid, oh) + _dot(lo, oh) + m_ref[...]

    table = pl.pallas_call(body, out_shape=SDS((128, GRID_W * GRID_W), F32), name="rpb_expand",
                           compiler_params=_params())(rpb_pad, jnp.asarray(onehot, BF16), jnp.asarray(negmask))
    return table[:rows].reshape(HEADS, 2 * KH - 1, GRID_W, GRID_W)


def _rpb_reduce(dslabs):
    onehot, _ = _na_consts()
    rows = HEADS * (2 * KH - 1)

    def body(x_ref, oht_ref, o_ref):
        hi, mid, lo = _split3(x_ref[...])
        oht = oht_ref[...]
        o_ref[...] = _dot(hi, oht) + _dot(mid, oht) + _dot(lo, oht)

    out = pl.pallas_call(body, out_shape=SDS((rows, 128), F32), name="rpb_reduce", compiler_params=_params())(
        dslabs.reshape(rows, GRID_W * GRID_W), jnp.asarray(onehot.T, BF16))
    return out.reshape(HEADS, 2 * KH - 1, 128)


def _bias_tiles(slab_ref, tile_ref):
    tile_ref[...] = jnp.full(tile_ref.shape, NEG_INF, F32)
    for t in range(NA_TYPES):
        for a in range(NA_RB):
            for b in range(NA_KR):
                dr = _na_pair(t, a, b)
                if dr is not None:
                    tile_ref[t, a * GRID_W:(a + 1) * GRID_W, b * GRID_W:(b + 1) * GRID_W] = slab_ref[dr]


def _bias_tiles_bwd(dtile_ref, dslab_ref):
    acc = {}
    for t in range(NA_TYPES):
        for a in range(NA_RB):
            for b in range(NA_KR):
                dr = _na_pair(t, a, b)
                if dr is not None:
                    part = dtile_ref[t, a * GRID_W:(a + 1) * GRID_W, b * GRID_W:(b + 1) * GRID_W]
                    acc[dr] = part if dr not in acc else acc[dr] + part
    for dr in range(2 * KH - 1):
        dslab_ref[dr] = acc[dr]


def _block_geometry(g):
    if isinstance(g, int):
        start = min(max(g * NA_RB - KH // 2, 0), GRID_ROWS - NA_KR)
        return (0 if g == 0 else 2 if g == NA_BLOCKS - 1 else 1), N_META + g * NA_QB, N_META + start * GRID_W
    start = jnp.clip(g * NA_RB - KH // 2, 0, GRID_ROWS - NA_KR)
    block_type = jnp.where(g == 0, 0, jnp.where(g == NA_BLOCKS - 1, 2, 1))
    q0 = pl.multiple_of(N_META + g * NA_QB, 16)
    k0 = pl.multiple_of(N_META + start * GRID_W, 16)
    return block_type, q0, k0


def _scaled_q(q):
    return (q * ATT_SCALE).astype(BF16)


def _na_probs(qs, kk, km, bias):
    s = _dot_nt(qs, kk) + bias
    sm = _dot_nt(qs, km)
    m = jnp.maximum(jnp.max(s, axis=-1, keepdims=True), jnp.max(sm, axis=-1, keepdims=True))
    p = jnp.exp(s - m)
    pm = jnp.exp(sm - m)
    inv = 1.0 / (jnp.sum(p, axis=-1, keepdims=True) + jnp.sum(pm, axis=-1, keepdims=True))
    return p * inv, pm * inv


def _meta_probs(qm, km):
    s = _dot_nt(qm, km) * ATT_SCALE
    p = jnp.exp(s - jnp.max(s, axis=-1, keepdims=True))
    return p / jnp.sum(p, axis=-1, keepdims=True)


def _qkv_specs():
    return [pl.BlockSpec((None, SEQ, HEAD_DIM), lambda h, which=which: (h + which * HEADS, 0, 0)) for which in range(3)]


def _na_fwd(qkv, bias):
    def body(q_ref, k_ref, v_ref, slab_ref, o_ref, b_ref):
        _bias_tiles(slab_ref, b_ref)
        km = k_ref[0:N_META, :].astype(BF16)
        vm = v_ref[0:N_META, :].astype(BF16)
        pmm = _meta_probs(q_ref[0:N_META, :].astype(BF16), km)
        o_ref[0:N_META, :] = _dot(pmm.astype(BF16), vm)

        def block(g, carry):
            block_type, q0, k0 = _block_geometry(g)
            qs = _scaled_q(q_ref[pl.ds(q0, NA_QB), :])
            kk = k_ref[pl.ds(k0, NA_KB), :].astype(BF16)
            vv = v_ref[pl.ds(k0, NA_KB), :].astype(BF16)
            p, pm = _na_probs(qs, kk, km, b_ref[block_type])
            o_ref[pl.ds(q0, NA_QB), :] = _dot(p.astype(BF16), vv) + _dot(pm.astype(BF16), vm)
            return carry

        for g in range(NA_BLOCKS):
            block(g, 0)

    head = pl.BlockSpec((None, SEQ, HEAD_DIM), lambda h: (h, 0, 0))
    return pl.pallas_call(
        body, grid=(HEADS,), in_specs=_qkv_specs() + [pl.BlockSpec((None, 2 * KH - 1, GRID_W, GRID_W), lambda h: (h, 0, 0, 0))],
        out_specs=head, out_shape=SDS((HEADS, SEQ, HEAD_DIM), F32), name="na_fwd",
        scratch_shapes=[pltpu.VMEM((NA_TYPES, NA_QB, NA_KB), F32)],
        compiler_params=_params(("parallel",)))(qkv, qkv, qkv, bias)


def _na_bwd(qkv, bias, do):
    def body(q_ref, k_ref, v_ref, slab_ref, do_ref, dq_ref, dk_ref, dv_ref, dslab_ref, b_ref, db_ref):
        _bias_tiles(slab_ref, b_ref)
        km = k_ref[0:N_META, :].astype(BF16)
        vm = v_ref[0:N_META, :].astype(BF16)
        dk_ref[...] = jnp.zeros_like(dk_ref)
        dv_ref[...] = jnp.zeros_like(dv_ref)
        db_ref[...] = jnp.zeros_like(db_ref)

        qm = q_ref[0:N_META, :].astype(BF16)
        dom = do_ref[0:N_META, :].astype(BF16)
        pmm = _meta_probs(qm, km)
        dpm = _dot_nt(dom, vm)
        dsm = (pmm * (dpm - jnp.sum(pmm * dpm, axis=-1, keepdims=True)) * ATT_SCALE).astype(BF16)
        dq_ref[0:N_META, :] = _dot(dsm, km)
        dkm0 = _dot_tn(dsm, qm)
        dvm0 = _dot_tn(pmm.astype(BF16), dom)

        def block(g, carry):
            dkm, dvm = carry
            block_type, q0, k0 = _block_geometry(g)
            qs = _scaled_q(q_ref[pl.ds(q0, NA_QB), :])
            kk = k_ref[pl.ds(k0, NA_KB), :].astype(BF16)
            vv = v_ref[pl.ds(k0, NA_KB), :].astype(BF16)
            dob = do_ref[pl.ds(q0, NA_QB), :].astype(BF16)
            p, pm = _na_probs(qs, kk, km, b_ref[block_type])
            dp = _dot_nt(dob, vv)
            dpm_ = _dot_nt(dob, vm)
            delta = jnp.sum(p * dp, axis=-1, keepdims=True) + jnp.sum(pm * dpm_, axis=-1, keepdims=True)
            ds = p * (dp - delta)
            dsm_ = pm * (dpm_ - delta)
            db_ref[block_type] += ds
            dsb = ds.astype(BF16)
            dsmb = dsm_.astype(BF16)
            dq_ref[pl.ds(q0, NA_QB), :] = (_dot(dsb, kk) + _dot(dsmb, km)) * ATT_SCALE
            dk_ref[pl.ds(k0, NA_KB), :] += _dot_tn(dsb, qs)
            dv_ref[pl.ds(k0, NA_KB), :] += _dot_tn(p.astype(BF16), dob)
            return dkm + _dot_tn(dsmb, qs), dvm + _dot_tn(pm.astype(BF16), dob)

        dkm, dvm = dkm0, dvm0
        for g in range(NA_BLOCKS):
            dkm, dvm = block(g, (dkm, dvm))
        dk_ref[0:N_META, :] = dkm
        dv_ref[0:N_META, :] = dvm
        _bias_tiles_bwd(db_ref, dslab_ref)

    head = pl.BlockSpec((None, SEQ, HEAD_DIM), lambda h: (h, 0, 0))
    bspec = pl.BlockSpec((None, 2 * KH - 1, GRID_W, GRID_W), lambda h: (h, 0, 0, 0))
    return pl.pallas_call(
        body, grid=(HEADS,), in_specs=_qkv_specs() + [bspec, head], out_specs=[head, head, head, bspec],
        out_shape=[SDS((HEADS, SEQ, HEAD_DIM), F32)] * 3 + [SDS((HEADS, 2 * KH - 1, GRID_W, GRID_W), F32)],
        scratch_shapes=[pltpu.VMEM((NA_TYPES, NA_QB, NA_KB), F32), pltpu.VMEM((NA_TYPES, NA_QB, NA_KB), F32)],
        name="na_bwd", compiler_params=_params(("parallel",)))(qkv, qkv, qkv, bias, do)


def _cmul(ar, ai, br, bi):
    return ar * br - ai * bi, ar * bi + ai * br


def _cpow(ar, ai, n):
    rr, ri = None, None
    br, bi = ar, ai
    while n:
        if n & 1:
            rr, ri = (br, bi) if rr is None else _cmul(rr, ri, br, bi)
        n >>= 1
        if n:
            br, bi = _cmul(br, bi, br, bi)
    return rr, ri


def _s5_prep(lr, li, logdt, bre, bim):
    def body(lr_ref, li_ref, dt_ref, br_ref, bi_ref, lbr_ref, lbi_ref, bbr_ref, bbi_ref):
        lr_, li_ = lr_ref[...], li_ref[...]
        dt = jnp.exp(dt_ref[...])
        mag = jnp.exp(lr_ * dt)
        lbr = mag * jnp.cos(li_ * dt)
        lbi = mag * jnp.sin(li_ * dt)
        lbr_ref[...] = lbr
        lbi_ref[...] = lbi
        den = lr_ * lr_ + li_ * li_
        xr = lbr - 1.0
        cr = (xr * lr_ + lbi * li_) / den
        ci = (lbi * lr_ - xr * li_) / den
        br, bi = br_ref[...], bi_ref[...]
        bbr_ref[...] = cr[:, None, :] * br - ci[:, None, :] * bi
        bbi_ref[...] = cr[:, None, :] * bi + ci[:, None, :] * br

    n = 2 * S5_GROUPS
    return pl.pallas_call(
        body, out_shape=[SDS((n, S5_STATE), F32)] * 2 + [SDS((n, S5_GROUP, S5_STATE), F32)] * 2,
        name="s5_prep", compiler_params=_params())(lr, li, logdt, bre, bim)


def _s5_prep_bwd(lr, li, logdt, bre, bim, dar, dai, dbbr, dbbi):
    def body(lr_ref, li_ref, dt_ref, br_ref, bi_ref, dar_ref, dai_ref, dbr_ref, dbi_ref,
             glr_ref, gli_ref, gdt_ref, gbr_ref, gbi_ref):
        lr_, li_ = lr_ref[...], li_ref[...]
        dt = jnp.exp(dt_ref[...])
        mag = jnp.exp(lr_ * dt)
        lbr = mag * jnp.cos(li_ * dt)
        lbi = mag * jnp.sin(li_ * dt)
        den = lr_ * lr_ + li_ * li_
        xr = lbr - 1.0
        cr = (xr * lr_ + lbi * li_) / den
        ci = (lbi * lr_ - xr * li_) / den
        br, bi = br_ref[...], bi_ref[...]
        dbr, dbi = dbr_ref[...], dbi_ref[...]
        gbr_ref[...] = cr[:, None, :] * dbr + ci[:, None, :] * dbi
        gbi_ref[...] = cr[:, None, :] * dbi - ci[:, None, :] * dbr
        gcr = jnp.sum(dbr * br + dbi * bi, axis=1)
        gci = jnp.sum(dbi * br - dbr * bi, axis=1)
        ilr, ili = lr_ / den, li_ / den
        tr, ti = _cmul(gcr, gci, ilr, ili)
        glbr = dar_ref[...] + tr
        glbi = dai_ref[...] + ti
        dr_, di_ = _cmul(tr, ti, cr, -ci)
        gwr, gwi = _cmul(glbr, glbi, lbr, -lbi)
        glr_ref[...] = gwr * dt - dr_
        gli_ref[...] = gwi * dt - di_
        gdt_ref[...] = jnp.sum(gwr * lr_ + gwi * li_, axis=-1, keepdims=True) * dt

    n = 2 * S5_GROUPS
    return pl.pallas_call(
        body, out_shape=[SDS((n, S5_STATE), F32)] * 2 + [SDS((n, 1), F32)] + [SDS((n, S5_GROUP, S5_STATE), F32)] * 2,
        name="s5_prep_bwd", compiler_params=_params())(lr, li, logdt, bre, bim, dar, dai, dbbr, dbbi)


def _scan_local(xr_ref, xi_ref, ar8, ai8, reverse):
    def step(i, carry):
        sr, si = carry
        idx = (SCAN_T - 1 - i) if reverse else i
        rows = pl.ds(pl.multiple_of(idx * SCAN_BLOCKS, SCAN_BLOCKS), SCAN_BLOCKS)
        nr = ar8 * sr - ai8 * si + xr_ref[rows, :]
        ni = ar8 * si + ai8 * sr + xi_ref[rows, :]
        xr_ref[rows, :] = nr
        xi_ref[rows, :] = ni
        return nr, ni

    z = jnp.zeros(ar8.shape, F32)
    return lax.fori_loop(0, SCAN_T, step, (z, z))


def _scan_carries(er, ei, atr, ati, reverse):
    row = lax.broadcasted_iota(jnp.int32, er.shape, 0)
    cr = jnp.zeros((1, er.shape[1]), F32)
    ci = cr
    outr = jnp.zeros(er.shape, F32)
    outi = outr
    order = range(SCAN_BLOCKS - 1, -1, -1) if reverse else range(SCAN_BLOCKS)
    for b in order:
        outr = jnp.where(row == b, cr, outr)
        outi = jnp.where(row == b, ci, outi)
        nr, ni = _cmul(atr, ati, cr, ci)
        cr, ci = nr + er[b:b + 1, :], ni + ei[b:b + 1, :]
    return outr, outi


def _scan_fixup(xr_ref, xi_ref, cr8, ci8, ar8, ai8, reverse, pair=None):
    tile = lambda idx: pl.ds(pl.multiple_of(idx * SCAN_BLOCKS, SCAN_BLOCKS), SCAN_BLOCKS)

    def fix(idx, pr, pi):
        fr, fi = _cmul(pr, pi, cr8, ci8)
        nr, ni = xr_ref[tile(idx), :] + fr, xi_ref[tile(idx), :] + fi
        xr_ref[tile(idx), :] = nr
        xi_ref[tile(idx), :] = ni
        return nr, ni

    if pair is None:
        def step(i, carry):
            pr, pi = carry
            fix((SCAN_T - 1 - i) if reverse else i, pr, pi)
            return _cmul(pr, pi, ar8, ai8)

        lax.fori_loop(0, SCAN_T, step, (ar8, ai8), unroll=2)
        return None

    sr_ref, si_ref = pair
    earlier = -1 if reverse else 1

    def step(i, carry):
        pr, pi, accr, acci = carry
        idx = (SCAN_T - 1 - i) if reverse else i
        nr, ni = fix(idx, pr, pi)
        qr, qi = _cmul(nr, ni, sr_ref[tile(idx + earlier), :], -si_ref[tile(idx + earlier), :])
        pr, pi = _cmul(pr, pi, ar8, ai8)
        return pr, pi, accr + qr, acci + qi

    z = jnp.zeros(ar8.shape, F32)
    pr, pi, accr, acci = lax.fori_loop(0, SCAN_T - 1, step, (ar8, ai8, z, z))
    edge, src, shift, empty = (0, SCAN_T - 1, 1, 0) if reverse else (SCAN_T - 1, 0, SCAN_BLOCKS - 1, SCAN_BLOCKS - 1)
    nr, ni = fix(edge, pr, pi)
    row = lax.broadcasted_iota(jnp.int32, ar8.shape, 0)
    spr = jnp.where(row == empty, 0.0, pltpu.roll(sr_ref[tile(src), :], shift, 0))
    spi = jnp.where(row == empty, 0.0, pltpu.roll(si_ref[tile(src), :], shift, 0))
    qr, qi = _cmul(nr, ni, spr, -spi)
    return jnp.sum(accr + qr, axis=0, keepdims=True), jnp.sum(acci + qi, axis=0, keepdims=True)


def _scan(xr_ref, xi_ref, ar, ai, reverse, pair=None):
    n = ar.shape[1]
    ar8 = jnp.broadcast_to(ar, (SCAN_BLOCKS, n))
    ai8 = jnp.broadcast_to(ai, (SCAN_BLOCKS, n))
    er, ei = _scan_local(xr_ref, xi_ref, ar8, ai8, reverse)
    atr, ati = _cpow(ar, ai, SCAN_T)
    cr8, ci8 = _scan_carries(er, ei, atr, ati, reverse)
    return _scan_fixup(xr_ref, xi_ref, cr8, ci8, ar8, ai8, reverse, pair)


def _s5_specs():
    chan = pl.BlockSpec((SEQ, CH_W), lambda c, d: (0, c))
    chan2 = pl.BlockSpec((None, SEQ, CH_W), lambda c, d: (d, 0, c))
    state = pl.BlockSpec((None, SEQ, ST_W), lambda c, d: (d, 0, c))
    bmat = pl.BlockSpec((None, None, CH_W, ST_W), lambda c, d: (d, c, 0, 0))
    cmat = pl.BlockSpec((None, None, ST_W, CH_W), lambda c, d: (d, c, 0, 0))
    avec = pl.BlockSpec((None, None, 1, ST_W), lambda c, d: (d, c, 0, 0))
    return chan, chan2, state, bmat, cmat, avec


def _scan_by_direction(xr_ref, xi_ref, ar, ai, d, adjoint, pair=None, da_out=None):
    for direction in range(2):
        @pl.when(d == direction)
        def _(direction=direction):
            res = _scan(xr_ref, xi_ref, ar, ai, adjoint != (direction == 1), pair)
            if pair is not None:
                da_out[0][...], da_out[1][...] = res


def _s5_scan_fwd(u, bre, bim, are, aim, cre, cim):
    def body(u_ref, bre_ref, bim_ref, are_ref, aim_ref, cre_ref, cim_ref, sr_ref, si_ref, y_ref):
        ub = u_ref[...].astype(BF16)
        sr_ref[...] = _dot(ub, bre_ref[...])
        si_ref[...] = _dot(ub, bim_ref[...])
        _scan_by_direction(sr_ref, si_ref, are_ref[...], aim_ref[...], pl.program_id(1), adjoint=False)
        y_ref[...] = _dot(sr_ref[...].astype(BF16), cre_ref[...]) - _dot(si_ref[...].astype(BF16), cim_ref[...])

    chan, chan2, state, bmat, cmat, avec = _s5_specs()
    return pl.pallas_call(
        body, grid=(S5_CHUNKS, 2), in_specs=[chan, bmat, bmat, avec, avec, cmat, cmat], out_specs=[state, state, chan2],
        out_shape=[SDS((2, SEQ, S5_GROUPS * S5_STATE), F32)] * 2 + [SDS((2, SEQ, S5_WIDTH), F32)],
        name="s5_scan_fwd", compiler_params=_params(("parallel", "parallel")))(u, bre, bim, are, aim, cre, cim)


def _diag_out(out_ref, full):
    for g in range(8):
        out_ref[g] = full[g * S5_GROUP:(g + 1) * S5_GROUP, g * S5_STATE:(g + 1) * S5_STATE]


def _s5_scan_bwd(dy, du_skip, u, sr, si, bre, bim, are, aim, cre, cim):
    def body(dy_ref, dus_ref, u_ref, sr_ref, si_ref, bre_ref, bim_ref, are_ref, aim_ref, cre_ref, cim_ref,
             du_ref, dbr_ref, dbi_ref, dcr_ref, dci_ref, dar_ref, dai_ref, gr_ref, gi_ref):
        d = pl.program_id(1)
        dyb = dy_ref[...].astype(BF16)
        gr_ref[...] = _dot_nt(dyb, cre_ref[...])
        gi_ref[...] = -_dot_nt(dyb, cim_ref[...])
        _diag_out(dcr_ref, _dot_tn(dyb, sr_ref[...].astype(BF16)))
        _diag_out(dci_ref, -_dot_tn(dyb, si_ref[...].astype(BF16)))
        _scan_by_direction(gr_ref, gi_ref, are_ref[...], -aim_ref[...], d, adjoint=True, pair=(sr_ref, si_ref),
                           da_out=(dar_ref, dai_ref))

        @pl.when(d == 0)
        def _():
            du_ref[...] = dus_ref[...]

        grb = gr_ref[...].astype(BF16)
        gib = gi_ref[...].astype(BF16)
        du_ref[...] += _dot_nt(grb, bre_ref[...]) + _dot_nt(gib, bim_ref[...])
        ub = u_ref[...].astype(BF16)
        _diag_out(dbr_ref, _dot_tn(ub, grb))
        _diag_out(dbi_ref, _dot_tn(ub, gib))

    chan, _, state, bmat, cmat, avec = _s5_specs()
    diag = pl.BlockSpec((None, None, 8, S5_GROUP, S5_STATE), lambda c, d: (d, c, 0, 0, 0))
    return pl.pallas_call(
        body, grid=(S5_CHUNKS, 2), in_specs=[chan, chan, chan, state, state, bmat, bmat, avec, avec, cmat, cmat],
        out_specs=[chan, diag, diag, diag, diag, avec, avec],
        out_shape=[SDS((SEQ, S5_WIDTH), F32)] + [SDS((2, S5_CHUNKS, 8, S5_GROUP, S5_STATE), F32)] * 4
                  + [SDS((2, S5_CHUNKS, 1, ST_W), F32)] * 2,
        scratch_shapes=[pltpu.VMEM((SEQ, ST_W), F32), pltpu.VMEM((SEQ, ST_W), F32)],
        name="s5_scan_bwd", compiler_params=_params(("parallel", "arbitrary")))(dy, du_skip, u, sr, si, bre, bim, are, aim, cre, cim)


_GELU_K = math.sqrt(2.0 / math.pi)
_GELU_C = 0.044715


def _gelu(x):
    t = jnp.tanh(_GELU_K * (x + _GELU_C * x * x * x))
    return 0.5 * x * (1.0 + t), t


def _s5_glu_fwd(u, y2, dskip, wglu, bglu):
    def body(u_ref, y0_ref, y1_ref, d_ref, w_ref, b_ref, o_ref, yp_ref):
        ypre = u_ref[...] * d_ref[...] + y0_ref[...] + y1_ref[...]
        yp_ref[...] = ypre
        y, _ = _gelu(ypre)
        z = _dot(y.astype(BF16), w_ref[...]) + b_ref[...]
        o_ref[...] = y * jax.nn.sigmoid(z)

    row = _row_spec(S5_WIDTH)
    vec = _fix_spec((1, S5_WIDTH))
    dir0 = pl.BlockSpec((None, ROW_TILE, S5_WIDTH), lambda i: (0, i, 0))
    dir1 = pl.BlockSpec((None, ROW_TILE, S5_WIDTH), lambda i: (1, i, 0))
    return pl.pallas_call(
        body, grid=(N_ROW_TILES,), in_specs=[row, dir0, dir1, vec, _fix_spec((S5_WIDTH, S5_WIDTH)), vec],
        out_specs=[row, row], out_shape=[SDS((SEQ, S5_WIDTH), F32)] * 2, name="s5_glu_fwd",
        compiler_params=_params(("parallel",)))(u, y2, y2, dskip, wglu, bglu)


def _s5_glu_bwd(do, ypre, u, dskip, wglu, bglu):
    def body(do_ref, yp_ref, u_ref, d_ref, w_ref, b_ref, dyp_ref, du_ref, dw_ref, db_ref, dd_ref):
        i = pl.program_id(0)
        ypre = yp_ref[...]
        y, t = _gelu(ypre)
        yb = y.astype(BF16)
        sg = jax.nn.sigmoid(_dot(yb, w_ref[...]) + b_ref[...])
        dov = do_ref[...]
        dz = dov * y * sg * (1.0 - sg)
        dzb = dz.astype(BF16)
        dy = dov * sg + _dot_nt(dzb, w_ref[...])
        dgelu = 0.5 * (1.0 + t) + 0.5 * ypre * (1.0 - t * t) * _GELU_K * (1.0 + 3.0 * _GELU_C * ypre * ypre)
        dyp = dy * dgelu
        dyp_ref[...] = dyp
        uv = u_ref[...]
        du_ref[...] = dyp * d_ref[...]

        @pl.when(i == 0)
        def _():
            dw_ref[...] = jnp.zeros_like(dw_ref)
            db_ref[...] = jnp.zeros_like(db_ref)
            dd_ref[...] = jnp.zeros_like(dd_ref)

        dw_ref[...] += _dot_tn(yb, dzb)
        db_ref[...] += jnp.sum(dz, axis=0, keepdims=True)
        dd_ref[...] += jnp.sum(dyp * uv, axis=0, keepdims=True)

    row = _row_spec(S5_WIDTH)
    vec = _fix_spec((1, S5_WIDTH))
    mat = _fix_spec((S5_WIDTH, S5_WIDTH))
    return pl.pallas_call(
        body, grid=(N_ROW_TILES,), in_specs=[row, row, row, vec, mat, vec], out_specs=[row, row, mat, vec, vec],
        out_shape=[SDS((SEQ, S5_WIDTH), F32)] * 2 + [SDS((S5_WIDTH, S5_WIDTH), F32), SDS((1, S5_WIDTH), F32), SDS((1, S5_WIDTH), F32)],
        name="s5_glu_bwd", compiler_params=_params(("arbitrary",)))(do, ypre, u, dskip, wglu, bglu)


def _heads_side_by_side(o_ref):
    return jnp.concatenate([o_ref[h] for h in range(HEADS)], axis=-1)


def _mix_out_fwd(ona, os5, g_na, g_s5, wout):
    def body(a_ref, s_ref, ga_ref, gs_ref, w_ref, o_ref):
        av, sv = _heads_side_by_side(a_ref), s_ref[...]
        ca = (av * _rstd(av) * ga_ref[...]).astype(BF16)
        cs = (sv * _rstd(sv) * gs_ref[...]).astype(BF16)
        o_ref[...] = _dot(ca, w_ref[0:NA_WIDTH, :]) + _dot(cs, w_ref[NA_WIDTH:, :])

    row = _row_spec(NA_WIDTH)
    vec = _fix_spec((1, NA_WIDTH))
    heads = pl.BlockSpec((HEADS, ROW_TILE, HEAD_DIM), lambda i: (0, i, 0))
    return pl.pallas_call(
        body, grid=(N_ROW_TILES,), in_specs=[heads, row, vec, vec, _fix_spec((D_MODEL, D_MODEL))],
        out_specs=_row_spec(D_MODEL), out_shape=SDS((SEQ, D_MODEL), F32), name="mix_out_fwd",
        compiler_params=_params(("parallel",)))(ona, os5, g_na, g_s5, wout)


def _mix_out_bwd(dmix, ona, os5, g_na, g_s5, wout):
    def body(dm_ref, a_ref, s_ref, ga_ref, gs_ref, w_ref, da_ref, ds_ref, dw_ref, dga_ref, dgs_ref):
        i = pl.program_id(0)
        dm = dm_ref[...]
        av, sv = _heads_side_by_side(a_ref), s_ref[...]
        ra, rs = _rstd(av), _rstd(sv)
        ga, gs = ga_ref[...], gs_ref[...]
        ca = (av * ra * ga).astype(BF16)
        cs = (sv * rs * gs).astype(BF16)
        dca = _dot_nt(dm, w_ref[0:NA_WIDTH, :])
        dcs = _dot_nt(dm, w_ref[NA_WIDTH:, :])
        da, dga = _rms_bwd(av, ra, ga, dca)
        ds, dgs = _rms_bwd(sv, rs, gs, dcs)
        for h in range(HEADS):
            da_ref[h] = da[:, h * HEAD_DIM:(h + 1) * HEAD_DIM]
        ds_ref[...] = ds

        @pl.when(i == 0)
        def _():
            dw_ref[...] = jnp.zeros_like(dw_ref)
            dga_ref[...] = jnp.zeros_like(dga_ref)
            dgs_ref[...] = jnp.zeros_like(dgs_ref)

        dw_ref[0:NA_WIDTH, :] += _dot_tn(ca, dm)
        dw_ref[NA_WIDTH:, :] += _dot_tn(cs, dm)
        dga_ref[...] += jnp.sum(dga, axis=0, keepdims=True)
        dgs_ref[...] += jnp.sum(dgs, axis=0, keepdims=True)

    row = _row_spec(NA_WIDTH)
    vec = _fix_spec((1, NA_WIDTH))
    mat = _fix_spec((D_MODEL, D_MODEL))
    heads = pl.BlockSpec((HEADS, ROW_TILE, HEAD_DIM), lambda i: (0, i, 0))
    return pl.pallas_call(
        body, grid=(N_ROW_TILES,), in_specs=[_row_spec(D_MODEL), heads, row, vec, vec, mat],
        out_specs=[heads, row, mat, vec, vec],
        out_shape=[SDS((HEADS, SEQ, HEAD_DIM), F32), SDS((SEQ, NA_WIDTH), F32), SDS((D_MODEL, D_MODEL), F32),
                   SDS((1, NA_WIDTH), F32), SDS((1, NA_WIDTH), F32)],
        name="mix_out_bwd", compiler_params=_params(("arbitrary",)))(dmix, ona, os5, g_na, g_s5, wout)


def _me():
    x, y, c = lax.axis_index("x"), lax.axis_index("y"), lax.axis_index("c")
    return x, y, c, 4 * x + 2 * y + c


def _peer(k):
    x, y, c, _ = _me()
    px = 1 - x if (k >> 2) & 1 else x
    py = 1 - y if (k >> 1) & 1 else y
    pc = 1 - c if k & 1 else c
    return (px, py, pc), 4 * px + 2 * py + pc


ALL_PEERS = (1, 2, 3, 4, 5, 6, 7)
CHIP_PEERS = (2, 4, 6)
SIBLING = 1


def _slot8(pos):
    return 4 * pos[0] + 2 * pos[1] + pos[2]


def _slot4(pos):
    return 2 * pos[0] + pos[1]


_HBM = pl.BlockSpec(memory_space=pltpu.HBM)
_SEM = pl.BlockSpec(memory_space=pltpu.SEMAPHORE)
_EFFECT = pltpu.SideEffectType.DATAFLOW_SIDE_EFFECTING


def _exchange_start(arrays, lands, gather, name, peers=ALL_PEERS, slot=_slot8, own=True):
    n = len(arrays)

    def body(*refs):
        ins, lnd = refs[:n], refs[n:2 * n]
        send_sems, recv_sems = refs[2 * n], refs[2 * n + 1]
        token = refs[-1]
        me = slot(_me()[:3])
        for i, k in enumerate(peers):
            peer, _ = _peer(k)
            for a in range(n):
                src = ins[a] if gather else ins[a].at[slot(peer)]
                s = a * len(peers) + i
                pltpu.make_async_remote_copy(src_ref=src, dst_ref=lnd[a].at[me], send_sem=send_sems.at[s],
                                             recv_sem=recv_sems.at[s], device_id=peer, device_id_type=MESH).start()
        if own:
            for a in range(n):
                pltpu.make_async_copy(ins[a] if gather else ins[a].at[me], lnd[a].at[me], recv_sems.at[n * len(peers) + a]).start()
        token[...] = jnp.zeros_like(token)

    sems = pltpu.SemaphoreType.DMA((n * (len(peers) + int(own)),))
    out = pl.pallas_call(
        body, name=name, in_specs=[_HBM] * (2 * n),
        out_shape=(sems, sems) + tuple(pltpu.HBM(a.shape, a.dtype) for a in list(arrays) + list(lands)) + (SDS((8, 128), F32),),
        out_specs=(_SEM, _SEM) + (_HBM,) * (2 * n) + (pl.BlockSpec(memory_space=pltpu.VMEM),),
        input_output_aliases={i: 2 + i for i in range(2 * n)},
        compiler_params=pltpu.CompilerParams(has_side_effects=_EFFECT),
    )(*[pltpu.with_memory_space_constraint(a, pltpu.HBM) for a in list(arrays) + list(lands)])
    return out[0], out[1], list(out[2:2 + n]), list(out[2 + n:2 + 2 * n]), out[-1]


def _exchange_wait(send_sems, recv_sems, arrays, lands, after, gather, name, peers=ALL_PEERS, slot=_slot8, own=True):
    n = len(arrays)

    def body(*refs):
        ins, lnd = refs[:n], refs[n:2 * n]
        send_sems, recv_sems = refs[2 * n], refs[2 * n + 1]
        if own:
            me = slot(_me()[:3])
            for a in range(n):
                pltpu.make_async_copy(ins[a] if gather else ins[a].at[me], lnd[a].at[me], recv_sems.at[n * len(peers) + a]).wait()
        for i, k in enumerate(peers):
            peer, _ = _peer(k)
            for a in range(n):
                src = ins[a] if gather else ins[a].at[slot(peer)]
                s = a * len(peers) + i
                cp = pltpu.make_async_remote_copy(src_ref=src, dst_ref=lnd[a].at[slot(peer)], send_sem=send_sems.at[s],
                                                  recv_sem=recv_sems.at[s], device_id=peer, device_id_type=MESH)
                cp.wait_send()
                cp.wait_recv()

        refs[-1][...] = jnp.zeros_like(refs[-1])

    after = list(after) if isinstance(after, (list, tuple)) else [after]
    out = pl.pallas_call(
        body, name=name, in_specs=[_HBM] * (2 * n) + [_SEM, _SEM] + [pl.BlockSpec(memory_space=pl.ANY)] * len(after),
        out_shape=tuple(pltpu.HBM(a.shape, a.dtype) for a in list(arrays) + list(lands)) + (SDS((8, 128), F32),),
        out_specs=(_HBM,) * (2 * n) + (pl.BlockSpec(memory_space=pltpu.VMEM),), input_output_aliases={i: i for i in range(2 * n)},
        compiler_params=pltpu.CompilerParams(has_side_effects=_EFFECT),
    )(*arrays, *lands, send_sems, recv_sems, *after)
    return list(out[n:2 * n]), out[-1]


def _forward_sibling(lands, name):
    n = len(lands)

    def body(*refs):
        outs = refs[n:2 * n]
        send_sems, recv_sems = refs[2 * n:]
        x, y, c, _ = _me()
        sends = []
        for i, k in enumerate(CHIP_PEERS):
            peer, _ = _peer(k)
            for a in range(n):
                rows = outs[a].at[_slot8(peer)]
                cp = pltpu.make_async_remote_copy(src_ref=rows, dst_ref=rows, send_sem=send_sems.at[a, i], recv_sem=recv_sems.at[a, i],
                                                  device_id=(x, y, 1 - c), device_id_type=MESH)
                cp.start()
                sends.append(cp)
        for i, k in enumerate(CHIP_PEERS):
            (px, py, pc), _ = _peer(k)
            for a in range(n):
                rows = outs[a].at[_slot8((px, py, 1 - pc))]
                pltpu.make_async_remote_copy(src_ref=rows, dst_ref=rows, send_sem=send_sems.at[a, i], recv_sem=recv_sems.at[a, i],
                                             device_id=(x, y, 1 - c), device_id_type=MESH).wait_recv()
        for cp in sends:
            cp.wait_send()

    return pl.pallas_call(
        body, in_specs=[_HBM] * n, out_specs=[_HBM] * n, out_shape=[SDS(a.shape, a.dtype) for a in lands],
        input_output_aliases={i: i for i in range(n)},
        scratch_shapes=[pltpu.SemaphoreType.DMA((n, len(CHIP_PEERS))), pltpu.SemaphoreType.DMA((n, len(CHIP_PEERS)))],
        name=name)(*lands)


def _swap_sibling(arrays, name, after=()):
    n, n_after = len(arrays), len(after)
    chips = N_DEV // 2

    def body(*refs):
        ins, outs = refs[:n], refs[n + n_after:2 * n + n_after]
        send_sems, recv_sems = refs[2 * n + n_after:]
        x, y, c, _ = _me()
        sends = []
        for q in range(chips):
            for a in range(n):
                cp = pltpu.make_async_remote_copy(src_ref=ins[a].at[q, 1 - c], dst_ref=outs[a].at[q], send_sem=send_sems.at[a, q],
                                                  recv_sem=recv_sems.at[a, q], device_id=(x, y, 1 - c), device_id_type=MESH)
                cp.start()
                sends.append(cp)
        for cp in sends:
            cp.wait_recv()
        for cp in sends:
            cp.wait_send()

    return pl.pallas_call(
        body, in_specs=[_HBM] * n + [pl.BlockSpec(memory_space=pl.ANY)] * n_after, out_specs=[_HBM] * n,
        out_shape=[SDS((chips,) + a.shape[2:], a.dtype) for a in arrays],
        scratch_shapes=[pltpu.SemaphoreType.DMA((n, chips)), pltpu.SemaphoreType.DMA((n, chips))], name=name)(*arrays, *after)


def _sum_pairs(mine, theirs, name):
    n = len(mine)
    chips = mine[0].shape[0]
    c = lax.axis_index("c")

    def body(c_ref, *refs):
        for a in range(n):
            refs[2 * n + a][...] = (refs[a][...].astype(F32) + refs[n + a][...].astype(F32)).astype(refs[2 * n + a].dtype)

    def pair(a):
        return pl.BlockSpec((None, None) + a.shape[2:], lambda q, c_ref: (q, c_ref[0], 0, 0))

    def single(a):
        return pl.BlockSpec((None,) + a.shape[2:], lambda q, c_ref: (q, 0, 0))

    return pl.pallas_call(
        body, grid_spec=pltpu.PrefetchScalarGridSpec(
            num_scalar_prefetch=1, grid=(chips,), in_specs=[pair(a) for a in mine] + [single(a) for a in mine],
            out_specs=[single(a) for a in mine]),
        out_shape=[SDS((chips,) + a.shape[2:], a.dtype) for a in mine], name=name,
        compiler_params=_params(("parallel",)))(c.reshape(1).astype(jnp.int32), *mine, *theirs)


def _adamw_math(w, g, m, v):
    m = ADAM_B1 * m + (1.0 - ADAM_B1) * g
    v = ADAM_B2 * v + (1.0 - ADAM_B2) * (g * g)
    m_hat = m / (1.0 - ADAM_B1 ** ADAM_STEP)
    v_hat = v / (1.0 - ADAM_B2 ** ADAM_STEP)
    delta = -ADAM_LR * (m_hat / (jnp.sqrt(v_hat) + ADAM_EPS) + ADAM_WD * w)
    return delta, m, v


def _adamw(w, m, v, pieces, name):
    rows, cols = w.shape[-2:]
    lead = w.ndim - 2
    tile = rows
    for cand in (256, 176, 128, 64, 16):
        if rows > cand and rows % cand == 0:
            tile = cand
            break

    def body(w_ref, m_ref, v_ref, p_ref, g_ref, d_ref, mo_ref, vo_ref):
        g = _sum_pieces(p_ref)
        g_ref[...] = g
        d_ref[...], mo_ref[...], vo_ref[...] = _adamw_math(w_ref[...], g, m_ref[...], v_ref[...])

    blk = pl.BlockSpec((None,) * lead + (tile, cols), lambda i: (0,) * lead + (i, 0))
    return pl.pallas_call(
        body, grid=(rows // tile,), in_specs=[blk, blk, blk, pl.BlockSpec((pieces.shape[0], tile, cols), lambda i: (0, i, 0))],
        out_specs=[blk] * 4, out_shape=[SDS(w.shape, F32)] * 4, name=name,
        compiler_params=_params(("parallel",)))(w, m, v, pieces)


def _sum_pieces(p_ref):
    g = p_ref[0].astype(F32)
    for p in range(1, p_ref.shape[0]):
        g = g + p_ref[p].astype(F32)
    return g


def _adamw_s5_mat(w, m, v, g, name):
    _, ndir, groups, b, c = w.shape
    per_dir = groups // 8

    def body(w_ref, m_ref, v_ref, g_ref, d_ref, mo_ref, vo_ref):
        d_ref[...], mo_ref[...], vo_ref[...] = _adamw_math(w_ref[...], g_ref[...], m_ref[...], v_ref[...])

    blk = pl.BlockSpec((None, None, 8, b, c), lambda i: (0, i // per_dir, i % per_dir, 0, 0))
    return pl.pallas_call(
        body, grid=(ndir * per_dir,), in_specs=[blk] * 4, out_specs=[blk] * 3, out_shape=[SDS(w.shape, F32)] * 3, name=name,
        compiler_params=_params(("parallel",)))(w, m, v, g)


VEC_ROWS = ['ffn1_pre_g', 'ffn1_post_g', 'mix_pre_g', 'mix_post_g', 'ffn2_pre_g', 'ffn2_post_g', 'final_g',
            ('na_out_g', 's5_out_g'), ('s5_d', 's5_b_glu')]
VEC_NAMES = [n for row in VEC_ROWS for n in ((row,) if isinstance(row, str) else row)]
VEC_PACK_ROWS = 16
LOSS_ROW = len(VEC_ROWS)


def _pack_vectors(grads, loss8):
    def body(*refs):
        o_ref = refs[-1]
        o_ref[...] = jnp.zeros_like(o_ref)
        o_ref[LOSS_ROW:LOSS_ROW + 1, 0:128] = refs[-2][0:1, :]
        k = 0
        for i, row in enumerate(VEC_ROWS):
            if isinstance(row, str):
                o_ref[i:i + 1, :] = refs[k][...]
                k += 1
            else:
                o_ref[i:i + 1, 0:NA_WIDTH] = refs[k][...]
                o_ref[i:i + 1, NA_WIDTH:] = refs[k + 1][...]
                k += 2

    return pl.pallas_call(body, out_shape=SDS((VEC_PACK_ROWS, D_MODEL), F32), name="pack_vectors",
                          compiler_params=_params())(*[grads[n] for n in VEC_NAMES], loss8)


def _sum8(pieces, name):
    def body(p_ref, o_ref):
        o_ref[...] = _sum_pieces(p_ref)

    return pl.pallas_call(body, out_shape=SDS(pieces.shape[1:], F32), name=name, compiler_params=_params())(pieces)


def _adamw_small(packed8, vec_wmv, others):
    n_vec, n_oth = len(VEC_NAMES), len(others)

    def body(*refs):
        p_ref = refs[0]
        ins = refs[1:1 + 3 * n_vec + 4 * n_oth]
        outs = refs[1 + 3 * n_vec + 4 * n_oth:]
        gsum = _sum_pieces(p_ref)
        outs[-1][...] = gsum[LOSS_ROW:LOSS_ROW + 1, 0:128]
        k = 0
        for i, row in enumerate(VEC_ROWS):
            parts = [(row, gsum[i:i + 1, :])] if isinstance(row, str) else \
                [(row[0], gsum[i:i + 1, 0:NA_WIDTH]), (row[1], gsum[i:i + 1, NA_WIDTH:])]
            for _, g in parts:
                w_ref, m_ref, v_ref = ins[3 * k:3 * k + 3]
                outs[4 * k][...] = g
                outs[4 * k + 1][...], outs[4 * k + 2][...], outs[4 * k + 3][...] = _adamw_math(w_ref[...], g, m_ref[...], v_ref[...])
                k += 1
        for j in range(n_oth):
            w_ref, m_ref, v_ref, g_ref = ins[3 * n_vec + 4 * j:3 * n_vec + 4 * j + 4]
            g = _sum_pieces(g_ref)
            g = g[tuple(slice(0, s) for s in w_ref.shape[1:])].reshape(w_ref.shape)
            o = outs[4 * (n_vec + j):4 * (n_vec + j) + 4]
            o[0][...] = g
            o[1][...], o[2][...], o[3][...] = _adamw_math(w_ref[...], g, m_ref[...], v_ref[...])

    args, out_shape = [packed8], []
    for w, m, v in vec_wmv:
        args += [w, m, v]
        out_shape += [SDS(w.shape, F32)] * 4
    for w, m, v, g in others:
        args += [w, m, v, g]
        out_shape += [SDS(w.shape, F32)] * 4
    out_shape += [SDS((1, 128), F32)]
    return pl.pallas_call(body, out_shape=out_shape, name="adamw_small", compiler_params=_params())(*args)


def _perm_rows(x):
    return x.reshape(SCAN_BLOCKS, SCAN_T, x.shape[-1]).transpose(1, 0, 2).reshape(SEQ, x.shape[-1])


def _unperm_rows(x):
    return x.reshape(SCAN_T, SCAN_BLOCKS, x.shape[-1]).transpose(1, 0, 2).reshape(SEQ, x.shape[-1])


def _block_diag(x):
    eye = np.eye(8, dtype=bool)[None, None, :, None, :, None]
    full = jnp.where(eye, x[:, :, :, :, None, :], 0.0)
    return full.reshape(2, S5_CHUNKS, 8 * x.shape[3], 8 * x.shape[4])


STORED_SWAPPED = {"ffn1_w_gate": (1, 2), "ffn1_w_up": (1, 2), "ffn2_w_gate": (1, 2), "ffn2_w_up": (1, 2),
                  "s5_b_re": (3, 4), "s5_b_im": (3, 4)}


def _stored(name, x):
    return jnp.swapaxes(x, *STORED_SWAPPED[name]) if name in STORED_SWAPPED else x


def _dep(x, token):
    return x if token is None else x + token


def _local_step(x, target, get_w, small, emit):
    bias = _rpb_expand(small["na_rpb"][0])
    lr = small["s5_lam_re"].reshape(64, S5_STATE)
    li = small["s5_lam_im"].reshape(64, S5_STATE)
    logdt = small["s5_log_dt"].reshape(64, 1)
    b_t = [_stored(n, small[n]).reshape(64, S5_GROUP, S5_STATE) for n in ("s5_b_re", "s5_b_im")]
    lbr, lbi, bbr, bbi = _s5_prep(lr, li, logdt, b_t[0], b_t[1])
    are = lbr.reshape(2, S5_CHUNKS, 1, ST_W)
    aim = lbi.reshape(2, S5_CHUNKS, 1, ST_W)
    bre = _block_diag(bbr.reshape(2, S5_CHUNKS, 8, S5_GROUP, S5_STATE)).astype(BF16)
    bim = _block_diag(bbi.reshape(2, S5_CHUNKS, 8, S5_GROUP, S5_STATE)).astype(BF16)
    c_t = [small[n].reshape(2, S5_CHUNKS, 8, S5_GROUP, S5_STATE).transpose(0, 1, 2, 4, 3) for n in ("s5_c_re", "s5_c_im")]
    cre = _block_diag(c_t[0]).astype(BF16)
    cim = _block_diag(c_t[1]).astype(BF16)
    tgt = jnp.concatenate([jnp.zeros((N_META, D_MODEL), F32), target], axis=0)

    h0, a1 = _embed_prenorm(get_w("meta", None)["meta_tokens"], x, small["ffn1_pre_g"])
    wts = dict(get_w("ffn1", [bias, are, aim, bre, bim, cre, cim, tgt, a1]))
    gate1, up1, f1 = _ffn_fwd(a1, wts["ffn1_w_gate"], wts["ffn1_w_up"], wts["ffn1_w_down"], "ffn1_fwd",
                              after=wts.get("tokens", ()))
    h1, a2 = _post_pre(f1, h0, small["ffn1_post_g"], small["mix_pre_g"], 0.5, "post_pre1")
    wts.update(get_w("w_in", a2))
    qkv = _proj_heads(a2, wts["w_in"])
    u = _proj_u(a2, wts["w_in"])
    ona = _na_fwd(qkv, bias)
    u_p = _perm_rows(u)
    sr, si, y2 = _s5_scan_fwd(u_p, bre, bim, are, aim, cre, cim)
    wts.update(get_w("mix", y2))
    os5_p, ypre_p = _s5_glu_fwd(u_p, y2, small["s5_d"], wts["s5_w_glu"], small["s5_b_glu"])
    os5 = _unperm_rows(os5_p)

    mix = _mix_out_fwd(ona, os5, small["na_out_g"], small["s5_out_g"], wts["w_out"])
    h2, a3 = _post_pre(mix, h1, small["mix_post_g"], small["ffn2_pre_g"], 1.0, "post_pre2")
    wts.update(get_w("ffn2", a3))
    gate2, up2, f2 = _ffn_fwd(a3, wts["ffn2_w_gate"], wts["ffn2_w_up"], wts["ffn2_w_down"], "ffn2_fwd")
    loss8, dh3, df2, g_final, g_ffn2_post = _final_loss(f2, h2, small["ffn2_post_g"], small["final_g"], tgt)

    da3, dwg2, dwu2, dwd2 = _ffn_bwd(df2, a3, gate2, up2, wts["ffn2_w_gate"], wts["ffn2_w_up"], wts["ffn2_w_down"], "ffn2_bwd")
    tok = emit("ffn2", {"ffn2_w_gate": dwg2, "ffn2_w_up": dwu2, "ffn2_w_down": dwd2})
    dh2, dmix, g_ffn2_pre, g_mix_post = _bwd_pre_post(da3, h2, _dep(small["ffn2_pre_g"], tok), dh3, mix, small["mix_post_g"], 1.0,
                                                      "bwd_pre_post2")
    dona, dos5, dwout, g_na_out, g_s5_out = _mix_out_bwd(dmix, ona, os5, small["na_out_g"], small["s5_out_g"], wts["w_out"])

    dypre_p, du_skip_p, dwglu, g_b_glu, g_s5_d = _s5_glu_bwd(_perm_rows(dos5), ypre_p, u_p, small["s5_d"], wts["s5_w_glu"],
                                                             small["s5_b_glu"])
    tok = emit("mix", {"s5_w_glu": dwglu.reshape(N_DEV, S5_WIDTH // N_DEV, S5_WIDTH).astype(BF16),
                       "w_out": dwout.reshape(N_DEV, D_MODEL // N_DEV, D_MODEL).astype(BF16)})
    du_p, dbr, dbi, dcr, dci, dar, dai = _s5_scan_bwd(dypre_p, du_skip_p, u_p, sr, si, bre, bim, _dep(are, tok), aim, cre, cim)
    du = _unperm_rows(du_p)
    per_group = (2 * S5_GROUPS, S5_GROUP, S5_STATE)
    g_lr, g_li, g_dt, g_br, g_bi = _s5_prep_bwd(lr, li, logdt, b_t[0], b_t[1], dar.reshape(64, S5_STATE),
                                                dai.reshape(64, S5_STATE), dbr.reshape(per_group), dbi.reshape(per_group))
    g_c = [dcr.reshape(per_group), dci.reshape(per_group)]

    dq, dk, dv, dbias = _na_bwd(qkv, bias, dona)
    g_rpb = _rpb_reduce(dbias)
    dense = jnp.stack([g.reshape(2 * S5_GROUPS, S5_STATE * S5_GROUP) for g in (g_br, g_bi, *g_c)])
    tok = emit("small", {"dense": dense, "na_rpb": g_rpb,
                         "s5_lam_re": g_lr.reshape(2, S5_GROUPS, S5_STATE), "s5_lam_im": g_li.reshape(2, S5_GROUPS, S5_STATE),
                         "s5_log_dt": g_dt.reshape(2, S5_GROUPS)})
    da2, dwin = _proj_bwd(dq, dk, dv, du, a2, wts["w_in"])
    tok2 = emit("w_in", {"w_in": dwin})
    tok = tok if tok2 is None else tok + tok2
    dh1, df1, g_mix_pre, g_ffn1_post = _bwd_pre_post(da2, h1, _dep(small["mix_pre_g"], tok), dh2, f1, small["ffn1_post_g"], 0.5,
                                                     "bwd_pre_post1")
    da1, dwg1, dwu1, dwd1 = _ffn_bwd(df1, a1, gate1, up1, wts["ffn1_w_gate"], wts["ffn1_w_up"], wts["ffn1_w_down"], "ffn1_bwd")
    grad_x, grad_meta, g_ffn1_pre = _bwd_embed(da1, h0, small["ffn1_pre_g"], dh1)
    vec_g = {
        "ffn1_pre_g": g_ffn1_pre, "ffn1_post_g": g_ffn1_post, "mix_pre_g": g_mix_pre, "s5_d": g_s5_d, "s5_b_glu": g_b_glu,
        "na_out_g": g_na_out, "s5_out_g": g_s5_out, "mix_post_g": g_mix_post,
        "ffn2_pre_g": g_ffn2_pre, "ffn2_post_g": g_ffn2_post, "final_g": g_final,
    }
    emit("vec", {"packed": _pack_vectors(vec_g, loss8), "meta_tokens": grad_meta})
    emit("ffn1", {"ffn1_w_gate": dwg1, "ffn1_w_up": dwu1, "ffn1_w_down": dwd1})
    return grad_x


WEIGHT_NAMES = ['meta_tokens', 'ffn1_pre_g', 'ffn1_post_g', 'ffn1_w_gate', 'ffn1_w_up', 'ffn1_w_down', 'mix_pre_g', 'w_in',
                'na_rpb', 's5_lam_re', 's5_lam_im', 's5_log_dt', 's5_b_re', 's5_b_im', 's5_c_re', 's5_c_im', 's5_d',
                's5_w_glu', 's5_b_glu', 'na_out_g', 's5_out_g', 'w_out', 'mix_post_g', 'ffn2_pre_g', 'ffn2_post_g',
                'ffn2_w_gate', 'ffn2_w_up', 'ffn2_w_down', 'final_g']
BIG_NAMES = ['ffn1_w_gate', 'ffn1_w_up', 'ffn1_w_down', 'w_in', 's5_w_glu', 'w_out', 'ffn2_w_gate', 'ffn2_w_up', 'ffn2_w_down']
SMALL_NAMES = [n for n in WEIGHT_NAMES if n not in BIG_NAMES and n != 'meta_tokens']
WHOLE_NAMES = ['na_rpb', 's5_lam_re', 's5_lam_im', 's5_log_dt']
LEAD_NAMES = ['s5_b_re', 's5_b_im', 's5_c_re', 's5_c_im']


def kernel(x, meta_tokens, ffn1_pre_g, ffn1_post_g, ffn1_w_gate, ffn1_w_up, ffn1_w_down, mix_pre_g, w_in, na_rpb, s5_lam_re, s5_lam_im, s5_log_dt, s5_b_re, s5_b_im, s5_c_re, s5_c_im, s5_d, s5_w_glu, s5_b_glu, na_out_g, s5_out_g, w_out, mix_post_g, ffn2_pre_g, ffn2_post_g, ffn2_w_gate, ffn2_w_up, ffn2_w_down, final_g, loss_target, m_meta_tokens, m_ffn1_pre_g, m_ffn1_post_g, m_ffn1_w_gate, m_ffn1_w_up, m_ffn1_w_down, m_mix_pre_g, m_w_in, m_na_rpb, m_s5_lam_re, m_s5_lam_im, m_s5_log_dt, m_s5_b_re, m_s5_b_im, m_s5_c_re, m_s5_c_im, m_s5_d, m_s5_w_glu, m_s5_b_glu, m_na_out_g, m_s5_out_g, m_w_out, m_mix_post_g, m_ffn2_pre_g, m_ffn2_post_g, m_ffn2_w_gate, m_ffn2_w_up, m_ffn2_w_down, m_final_g, v_meta_tokens, v_ffn1_pre_g, v_ffn1_post_g, v_ffn1_w_gate, v_ffn1_w_up, v_ffn1_w_down, v_mix_pre_g, v_w_in, v_na_rpb, v_s5_lam_re, v_s5_lam_im, v_s5_log_dt, v_s5_b_re, v_s5_b_im, v_s5_c_re, v_s5_c_im, v_s5_d, v_s5_w_glu, v_s5_b_glu, v_na_out_g, v_s5_out_g, v_w_out, v_mix_post_g, v_ffn2_pre_g, v_ffn2_post_g, v_ffn2_w_gate, v_ffn2_w_up, v_ffn2_w_down, v_final_g):
    args = dict(locals())
    w = {n: args[n] for n in WEIGHT_NAMES}
    m = {n: args["m_" + n] for n in WEIGHT_NAMES}
    v = {n: args["v_" + n] for n in WEIGHT_NAMES}

    small = {n: w[n] for n in SMALL_NAMES}

    pending = {}

    def start(group, names, arrays, gather, peers=ALL_PEERS, slot=_slot8):
        n_slots = N_DEV if slot is _slot8 else N_DEV // 2
        lands = [lax.empty((n_slots,) + a.shape if gather else a.shape, a.dtype) for a in arrays]
        send_sems, recv_sems, arrays, lands, token = _exchange_start(arrays, lands, gather, "start_" + group, peers, slot)
        pending[group] = (names, send_sems, recv_sems, arrays, lands, gather, peers, slot)
        return token

    def finish(group, after):
        names, send_sems, recv_sems, arrays, lands, gather, peers, slot = pending.pop(group)
        lands, token = _exchange_wait(send_sems, recv_sems, arrays, lands, after, gather, "wait_" + group, peers, slot)
        return dict(zip(names, lands)), token

    first = ["ffn1_w_gate", "ffn1_w_up", "ffn1_w_down"]
    def shard(n, token=None):
        return _dep(_stored(n, w[n])[0], None if token is None else token[0, 0]).astype(BF16)

    ffn_names = ("ffn1_w_gate", "ffn1_w_up", "ffn1_w_down", "ffn2_w_gate", "ffn2_w_up", "ffn2_w_down")
    later_groups = (("w_in", ["w_in"]), ("mix", ["s5_w_glu", "w_out"]), ("ffn2", ["ffn2_w_gate", "ffn2_w_up", "ffn2_w_down"]))
    token0 = start("meta", ["meta_tokens"], [w["meta_tokens"]], True)
    token1 = start("ffn1", first, [shard(n, token0) for n in first], True, (SIBLING,) + CHIP_PEERS)
    meta_full = finish("meta", [token1])[0]["meta_tokens"].transpose(1, 0, 2).reshape(N_META, D_MODEL)
    later_shards = {n: shard(n, token1) for _, names in later_groups for n in names}
    for n in ("na_rpb", "s5_lam_re"):
        small[n] = _dep(small[n], token1[0, 0])

    def get_w(group, after):
        if group == "meta":
            return {"meta_tokens": meta_full}
        if group == "ffn1":
            after = list(after) + list(later_shards.values())
        got, token = finish(group, after)
        if group == "ffn1":
            got = dict(zip(got, _forward_sibling(list(got.values()), "forward_ffn1")))
            got["tokens"] = [start(g, names + ["order"], [later_shards[n] for n in names] + [token], True) for g, names in later_groups]
        if group == "mix":
            got = {"s5_w_glu": got["s5_w_glu"].reshape(S5_WIDTH, S5_WIDTH), "w_out": got["w_out"].reshape(D_MODEL, D_MODEL)}
        return {n: (a.reshape(D_FF, D_MODEL) if n in ffn_names else a) for n, a in got.items()}

    tokens = {}

    def emit(group, grads):
        grads = {n: (g.reshape(N_DEV, FF_SHARD, D_MODEL) if n in ffn_names else g) for n, g in grads.items()}
        if group == "ffn1":
            mine = [g.reshape((N_DEV // 2, 2) + g.shape[1:]) for g in grads.values()]
            theirs = _swap_sibling(mine, "swap_g_ffn1", after=[tokens["vec"]])
            sums = _sum_pairs(mine, theirs, "pair_sum_g_ffn1")
            tokens[group] = start("g_ffn1", list(grads), sums, False, CHIP_PEERS, _slot4)
        else:
            tokens[group] = start("g_" + group, list(grads), list(grads.values()), group in ("small", "vec"))
        return tokens[group][0, 0]

    grad_x = _local_step(x[0], loss_target[0], get_w, small, emit)
    res = {}

    def update_shard(n, pieces):
        outs = _adamw(_stored(n, w[n]), _stored(n, m[n]), _stored(n, v[n]), pieces, "adamw_" + n)
        res[n] = [_stored(n, o) for o in outs]

    late = [grad_x, tokens["ffn1"]]
    for group in ("g_ffn2", "g_mix", "g_w_in"):
        for n, pieces in finish(group, late)[0].items():
            update_shard(n, pieces)
    g8 = finish("g_small", late)[0]
    dense = _sum8(g8["dense"], "sum_dense")
    for i, n in enumerate(LEAD_NAMES):
        g = dense[i].reshape(_stored(n, w[n]).shape)
        upd = _adamw_s5_mat(_stored(n, w[n]), _stored(n, m[n]), _stored(n, v[n]), g, "adamw_" + n)
        res[n] = [_stored(n, o) for o in [g] + list(upd)]

    done = [res[n][1] for n in ("ffn2_w_gate", "ffn2_w_up", "ffn2_w_down", "w_in", "w_out", "s5_w_glu") + tuple(LEAD_NAMES)]
    got = finish("g_vec", done)[0]
    packed8, gmeta8 = got["packed"], got["meta_tokens"]
    for n, pieces in finish("g_ffn1", packed8)[0].items():
        update_shard(n, pieces)
    _, _, _, me = _me()
    update_shard("meta_tokens", lax.dynamic_slice_in_dim(gmeta8, me * (D_MODEL // N_DEV), D_MODEL // N_DEV, axis=2))

    outs = _adamw_small(packed8, [(w[n], m[n], v[n]) for n in VEC_NAMES], [(w[n], m[n], v[n], g8[n]) for n in WHOLE_NAMES])
    for i, n in enumerate(VEC_NAMES + WHOLE_NAMES):
        res[n] = list(outs[4 * i:4 * i + 4])

    out = [outs[-1][0, 0], grad_x[None]]
    for kind in range(4):
        out += [res[n][kind] for n in WEIGHT_NAMES]
    return tuple(out)
```
